```python
import jax, jax.numpy as jnp
from jax import lax
import numpy as np

D_MODEL = 1024
BATCH = 8
SEQ = 4096
DEPTH = 1

CHUNK = 64
LEFT_CHUNKS = 8
BAND = (LEFT_CHUNKS + 1) * CHUNK
ATT_HEADS = 8
ATT_HEAD_DIM = 64
ATT_WIDTH = ATT_HEADS * ATT_HEAD_DIM
MAX_REL = 128
LRU_WIDTH = D_MODEL
LRU_BLOCKS = 16
LRU_BLOCK = LRU_WIDTH // LRU_BLOCKS
CONV_WIDTH = 4
LRU_C = 8.0
D_FF = 2816
N_SUB = 3
EPS = 1e-6
PROJ_SIZES = (ATT_WIDTH, ATT_WIDTH, ATT_WIDTH, LRU_WIDTH, LRU_WIDTH, D_MODEL, D_MODEL)
PROJ_WIDTH = sum(PROJ_SIZES)

kernel_name = "hybrid_chunked_attn_rglru_macaron_block"


def rmsnorm(x, g):
    xf = x.astype(jnp.float32)
    y = xf * lax.rsqrt(jnp.mean(xf * xf, axis=-1, keepdims=True) + EPS)
    return (y * g.astype(jnp.float32)).astype(x.dtype)


def modulate(h, shift, scale):
    return h * (1 + scale[:, None, :]) + shift[:, None, :]


def swiglu(h, w_gu, w_down):
    g, u = jnp.split(h @ w_gu, 2, axis=-1)
    return (jax.nn.silu(g) * u) @ w_down


def chunked_attention(q, k, v, rel_bias):
    b, s, _ = q.shape
    nc = s // CHUNK
    q = q.reshape(b, nc, CHUNK, ATT_HEADS, ATT_HEAD_DIM)
    k = k.reshape(b, nc, CHUNK, ATT_HEADS, ATT_HEAD_DIM)
    v = v.reshape(b, nc, CHUNK, ATT_HEADS, ATT_HEAD_DIM)
    pad = ((0, 0), (LEFT_CHUNKS, 0), (0, 0), (0, 0), (0, 0))
    kp, vp = jnp.pad(k, pad), jnp.pad(v, pad)
    kb = jnp.concatenate([kp[:, j:j + nc] for j in range(LEFT_CHUNKS + 1)], axis=2)
    vb = jnp.concatenate([vp[:, j:j + nc] for j in range(LEFT_CHUNKS + 1)], axis=2)
    scores = jnp.einsum('bnqhd,bnkhd->bhnqk', q, kb).astype(jnp.float32) * (ATT_HEAD_DIM ** -0.5)
    qi = jnp.arange(CHUNK)[:, None]
    kj = jnp.arange(BAND)[None, :]
    rel = jnp.clip(qi - kj + LEFT_CHUNKS * CHUNK, -MAX_REL, MAX_REL) + MAX_REL
    bias = rel_bias.astype(jnp.float32)[:, rel]
    scores = scores + bias[None, :, None, :, :]
    valid = (jnp.arange(nc)[:, None] - LEFT_CHUNKS + jnp.arange(BAND)[None, :] // CHUNK) >= 0
    scores = jnp.where(valid[None, None, :, None, :], scores, jnp.finfo(jnp.float32).min)
    p = jax.nn.softmax(scores, axis=-1).astype(v.dtype)
    out = jnp.einsum('bhnqk,bnkhd->bnqhd', p, vb)
    return out.reshape(b, s, ATT_WIDTH)


def causal_depthwise_conv(x, w, bias):
    rhs = w[:, None, :]
    y = lax.conv_general_dilated(x, rhs, window_strides=(1,), padding=[(CONV_WIDTH - 1, 0)],
                                 dimension_numbers=('NWC', 'WIO', 'NWC'),
                                 feature_group_count=x.shape[-1])
    return y + bias


def block_diag_linear(x, w, bias):
    b, s, _ = x.shape
    xb = x.reshape(b, s, LRU_BLOCKS, LRU_BLOCK)
    return jnp.einsum('bsnk,nkj->bsnj', xb, w).reshape(b, s, LRU_WIDTH) + bias


def rg_lru(x, w_a, b_a, w_x, b_x, lam):
    r = jax.nn.sigmoid(block_diag_linear(x, w_a, b_a).astype(jnp.float32))
    i = jax.nn.sigmoid(block_diag_linear(x, w_x, b_x).astype(jnp.float32))
    log_a = -LRU_C * r * jax.nn.softplus(-lam.astype(jnp.float32))
    a = jnp.exp(log_a)
    mult = jnp.sqrt(-jnp.expm1(2.0 * log_a))
    u = mult * (i * x.astype(jnp.float32))

    def combine(left, right):
        a1, b1 = left
        a2, b2 = right
        return a1 * a2, a2 * b1 + b2

    _, h = lax.associative_scan(combine, (a, u), axis=1)
    return h.astype(x.dtype)


def mixer(h, w_in, rel_bias, conv_w, conv_b, lru_wa, lru_ba, lru_wx, lru_bx, lru_lambda,
          w_att_o, w_rec_o, w_out):
    proj = h @ w_in
    idx = [int(v) for v in np.cumsum(PROJ_SIZES)[:-1]]
    q, k, v, xr, yr, g_att, g_rec = jnp.split(proj, idx, axis=-1)
    att = chunked_attention(q, k, v, rel_bias) @ w_att_o
    xr = causal_depthwise_conv(xr, conv_w, conv_b)
    rec = (rg_lru(xr, lru_wa, lru_ba, lru_wx, lru_bx, lru_lambda) * jax.nn.gelu(yr)) @ w_rec_o
    merged = jax.nn.sigmoid(g_att) * att + jax.nn.sigmoid(g_rec) * rec
    return merged @ w_out


def sandwich(x, fn, g_pre, g_post, shift, scale, gate, res_w):
    h = modulate(rmsnorm(x, g_pre), shift, scale)
    y = rmsnorm(fn(h), g_post)
    return x + res_w * gate[:, None, :] * y


def _fwd_setup_inputs(seed: int = 0) -> dict:
    key = jax.random.key(seed)
    ks = jax.random.split(key, 24)
    L, D, F, W = DEPTH, D_MODEL, D_FF, LRU_WIDTH
    nrm = lambda k, shape, fan_in: jax.random.normal(k, shape, jnp.float32) * (fan_in ** -0.5)
    u = jax.random.uniform(ks[20], (L, W), jnp.float32, 0.9, 0.999)
    p = u ** (1.0 / LRU_C)
    lam = jnp.log(p) - jnp.log1p(-p)
    return {
        "x": jax.random.normal(ks[0], (BATCH, SEQ, D), jnp.float32),
        "c": jax.random.normal(ks[1], (BATCH, D), jnp.float32),
        "w_ada": nrm(ks[2], (L, D, N_SUB * 3 * D), D) * 0.5,
        "b_ada": 0.02 * jax.random.normal(ks[3], (L, N_SUB * 3 * D), jnp.float32),
        "norm_pre": 1.0 + 0.05 * jax.random.normal(ks[4], (L, N_SUB, D), jnp.float32),
        "norm_post": 1.0 + 0.05 * jax.random.normal(ks[5], (L, N_SUB, D), jnp.float32),
        "ffn1_w_gu": nrm(ks[6], (L, D, 2 * F), D),
        "ffn1_w_down": nrm(ks[7], (L, F, D), F),
        "w_in": nrm(ks[8], (L, D, PROJ_WIDTH), D),
        "rel_bias": 0.5 * jax.random.normal(ks[9], (L, ATT_HEADS, 2 * MAX_REL + 1), jnp.float32),
        "conv_w": nrm(ks[10], (L, CONV_WIDTH, W), CONV_WIDTH),
        "conv_b": 0.02 * jax.random.normal(ks[11], (L, W), jnp.float32),
        "lru_wa": nrm(ks[12], (L, LRU_BLOCKS, LRU_BLOCK, LRU_BLOCK), LRU_BLOCK),
        "lru_ba": 0.02 * jax.random.normal(ks[13], (L, W), jnp.float32),
        "lru_wx": nrm(ks[14], (L, LRU_BLOCKS, LRU_BLOCK, LRU_BLOCK), LRU_BLOCK),
        "lru_bx": 0.02 * jax.random.normal(ks[15], (L, W), jnp.float32),
        "lru_lambda": lam,
        "w_att_o": nrm(ks[16], (L, ATT_WIDTH, D), ATT_WIDTH),
        "w_rec_o": nrm(ks[17], (L, W, D), W),
        "w_out": nrm(ks[18], (L, D, D), D),
        "ffn2_w_gu": nrm(ks[19], (L, D, 2 * F), D),
        "ffn2_w_down": nrm(ks[21], (L, F, D), F),
    }


def _fwd_reference(x, c, w_ada, b_ada, norm_pre, norm_post, ffn1_w_gu, ffn1_w_down, w_in, rel_bias,
              conv_w, conv_b, lru_wa, lru_ba, lru_wx, lru_bx, lru_lambda, w_att_o, w_rec_o,
              w_out, ffn2_w_gu, ffn2_w_down):
    b = x.shape[0]
    c_act = jax.nn.silu(c)
    for l in range(DEPTH):
        mod = (c_act @ w_ada[l] + b_ada[l]).reshape(b, N_SUB, 3, D_MODEL)
        ffn1 = lambda h: swiglu(h, ffn1_w_gu[l], ffn1_w_down[l])
        mix = lambda h: mixer(h, w_in[l], rel_bias[l], conv_w[l], conv_b[l], lru_wa[l], lru_ba[l],
                              lru_wx[l], lru_bx[l], lru_lambda[l], w_att_o[l], w_rec_o[l], w_out[l])
        ffn2 = lambda h: swiglu(h, ffn2_w_gu[l], ffn2_w_down[l])
        x = sandwich(x, ffn1, norm_pre[l, 0], norm_post[l, 0], mod[:, 0, 0], mod[:, 0, 1], mod[:, 0, 2], 0.5)
        x = sandwich(x, mix, norm_pre[l, 1], norm_post[l, 1], mod[:, 1, 0], mod[:, 1, 1], mod[:, 1, 2], 1.0)
        x = sandwich(x, ffn2, norm_pre[l, 2], norm_post[l, 2], mod[:, 2, 0], mod[:, 2, 1], mod[:, 2, 2], 0.5)
    return x


import jax as _jax
import jax.numpy as _jnp

TWIN_FORMAT = 'train_step'
FWD_PARAMS = ['x', 'c', 'w_ada', 'b_ada', 'norm_pre', 'norm_post', 'ffn1_w_gu', 'ffn1_w_down', 'w_in', 'rel_bias', 'conv_w', 'conv_b', 'lru_wa', 'lru_ba', 'lru_wx', 'lru_bx', 'lru_lambda', 'w_att_o', 'w_rec_o', 'w_out', 'ffn2_w_gu', 'ffn2_w_down']
TWIN_WEIGHTS = ['w_ada', 'b_ada', 'norm_pre', 'norm_post', 'ffn1_w_gu', 'ffn1_w_down', 'w_in', 'rel_bias', 'conv_w', 'conv_b', 'lru_wa', 'lru_ba', 'lru_wx', 'lru_bx', 'lru_lambda', 'w_att_o', 'w_rec_o', 'w_out', 'ffn2_w_gu', 'ffn2_w_down']
TWIN_DIFF_INPUT = 'x'
TWIN_INPUTS = ['x', 'c', 'w_ada', 'b_ada', 'norm_pre', 'norm_post', 'ffn1_w_gu', 'ffn1_w_down', 'w_in', 'rel_bias', 'conv_w', 'conv_b', 'lru_wa', 'lru_ba', 'lru_wx', 'lru_bx', 'lru_lambda', 'w_att_o', 'w_rec_o', 'w_out', 'ffn2_w_gu', 'ffn2_w_down', 'loss_target', 'm_w_ada', 'm_b_ada', 'm_norm_pre', 'm_norm_post', 'm_ffn1_w_gu', 'm_ffn1_w_down', 'm_w_in', 'm_rel_bias', 'm_conv_w', 'm_conv_b', 'm_lru_wa', 'm_lru_ba', 'm_lru_wx', 'm_lru_bx', 'm_lru_lambda', 'm_w_att_o', 'm_w_rec_o', 'm_w_out', 'm_ffn2_w_gu', 'm_ffn2_w_down', 'v_w_ada', 'v_b_ada', 'v_norm_pre', 'v_norm_post', 'v_ffn1_w_gu', 'v_ffn1_w_down', 'v_w_in', 'v_rel_bias', 'v_conv_w', 'v_conv_b', 'v_lru_wa', 'v_lru_ba', 'v_lru_wx', 'v_lru_bx', 'v_lru_lambda', 'v_w_att_o', 'v_w_rec_o', 'v_w_out', 'v_ffn2_w_gu', 'v_ffn2_w_down']
TWIN_OUTPUTS = ['loss', 'grad_x', 'grad_w_ada', 'grad_b_ada', 'grad_norm_pre', 'grad_norm_post', 'grad_ffn1_w_gu', 'grad_ffn1_w_down', 'grad_w_in', 'grad_rel_bias', 'grad_conv_w', 'grad_conv_b', 'grad_lru_wa', 'grad_lru_ba', 'grad_lru_wx', 'grad_lru_bx', 'grad_lru_lambda', 'grad_w_att_o', 'grad_w_rec_o', 'grad_w_out', 'grad_ffn2_w_gu', 'grad_ffn2_w_down', 'delta_w_ada', 'delta_b_ada', 'delta_norm_pre', 'delta_norm_post', 'delta_ffn1_w_gu', 'delta_ffn1_w_down', 'delta_w_in', 'delta_rel_bias', 'delta_conv_w', 'delta_conv_b', 'delta_lru_wa', 'delta_lru_ba', 'delta_lru_wx', 'delta_lru_bx', 'delta_lru_lambda', 'delta_w_att_o', 'delta_w_rec_o', 'delta_w_out', 'delta_ffn2_w_gu', 'delta_ffn2_w_down', 'new_m_w_ada', 'new_m_b_ada', 'new_m_norm_pre', 'new_m_norm_post', 'new_m_ffn1_w_gu', 'new_m_ffn1_w_down', 'new_m_w_in', 'new_m_rel_bias', 'new_m_conv_w', 'new_m_conv_b', 'new_m_lru_wa', 'new_m_lru_ba', 'new_m_lru_wx', 'new_m_lru_bx', 'new_m_lru_lambda', 'new_m_w_att_o', 'new_m_w_rec_o', 'new_m_w_out', 'new_m_ffn2_w_gu', 'new_m_ffn2_w_down', 'new_v_w_ada', 'new_v_b_ada', 'new_v_norm_pre', 'new_v_norm_post', 'new_v_ffn1_w_gu', 'new_v_ffn1_w_down', 'new_v_w_in', 'new_v_rel_bias', 'new_v_conv_w', 'new_v_conv_b', 'new_v_lru_wa', 'new_v_lru_ba', 'new_v_lru_wx', 'new_v_lru_bx', 'new_v_lru_lambda', 'new_v_w_att_o', 'new_v_w_rec_o', 'new_v_w_out', 'new_v_ffn2_w_gu', 'new_v_ffn2_w_down']
TWIN_LEAF_KINDS = {'loss': 'loss', 'grad_x': 'grad_x', 'grad_w_ada': 'grad_w', 'grad_b_ada': 'grad_w', 'grad_norm_pre': 'grad_w', 'grad_norm_post': 'grad_w', 'grad_ffn1_w_gu': 'grad_w', 'grad_ffn1_w_down': 'grad_w', 'grad_w_in': 'grad_w', 'grad_rel_bias': 'grad_w', 'grad_conv_w': 'grad_w', 'grad_conv_b': 'grad_w', 'grad_lru_wa': 'grad_w', 'grad_lru_ba': 'grad_w', 'grad_lru_wx': 'grad_w', 'grad_lru_bx': 'grad_w', 'grad_lru_lambda': 'grad_w', 'grad_w_att_o': 'grad_w', 'grad_w_rec_o': 'grad_w', 'grad_w_out': 'grad_w', 'grad_ffn2_w_gu': 'grad_w', 'grad_ffn2_w_down': 'grad_w', 'delta_w_ada': 'delta_w', 'delta_b_ada': 'delta_w', 'delta_norm_pre': 'delta_w', 'delta_norm_post': 'delta_w', 'delta_ffn1_w_gu': 'delta_w', 'delta_ffn1_w_down': 'delta_w', 'delta_w_in': 'delta_w', 'delta_rel_bias': 'delta_w', 'delta_conv_w': 'delta_w', 'delta_conv_b': 'delta_w', 'delta_lru_wa': 'delta_w', 'delta_lru_ba': 'delta_w', 'delta_lru_wx': 'delta_w', 'delta_lru_bx': 'delta_w', 'delta_lru_lambda': 'delta_w', 'delta_w_att_o': 'delta_w', 'delta_w_rec_o': 'delta_w', 'delta_w_out': 'delta_w', 'delta_ffn2_w_gu': 'delta_w', 'delta_ffn2_w_down': 'delta_w', 'new_m_w_ada': 'new_m', 'new_m_b_ada': 'new_m', 'new_m_norm_pre': 'new_m', 'new_m_norm_post': 'new_m', 'new_m_ffn1_w_gu': 'new_m', 'new_m_ffn1_w_down': 'new_m', 'new_m_w_in': 'new_m', 'new_m_rel_bias': 'new_m', 'new_m_conv_w': 'new_m', 'new_m_conv_b': 'new_m', 'new_m_lru_wa': 'new_m', 'new_m_lru_ba': 'new_m', 'new_m_lru_wx': 'new_m', 'new_m_lru_bx': 'new_m', 'new_m_lru_lambda': 'new_m', 'new_m_w_att_o': 'new_m', 'new_m_w_rec_o': 'new_m', 'new_m_w_out': 'new_m', 'new_m_ffn2_w_gu': 'new_m', 'new_m_ffn2_w_down': 'new_m', 'new_v_w_ada': 'new_v', 'new_v_b_ada': 'new_v', 'new_v_norm_pre': 'new_v', 'new_v_norm_post': 'new_v', 'new_v_ffn1_w_gu': 'new_v', 'new_v_ffn1_w_down': 'new_v', 'new_v_w_in': 'new_v', 'new_v_rel_bias': 'new_v', 'new_v_conv_w': 'new_v', 'new_v_conv_b': 'new_v', 'new_v_lru_wa': 'new_v', 'new_v_lru_ba': 'new_v', 'new_v_lru_wx': 'new_v', 'new_v_lru_bx': 'new_v', 'new_v_lru_lambda': 'new_v', 'new_v_w_att_o': 'new_v', 'new_v_w_rec_o': 'new_v', 'new_v_w_out': 'new_v', 'new_v_ffn2_w_gu': 'new_v', 'new_v_ffn2_w_down': 'new_v'}


def _forward(args):
    return _fwd_reference(*[args[k] for k in FWD_PARAMS])


def _output_shape():
    def fwd():
        inp = _fwd_setup_inputs(0)
        return _fwd_reference(*[inp[k] for k in FWD_PARAMS])
    out = _jax.eval_shape(fwd)
    return out.shape, out.dtype

N_MICROBATCH = 1
ADAM_LR = 0.001
ADAM_B1 = 0.9
ADAM_B2 = 0.999
ADAM_EPS = 1e-08
ADAM_WD = 0.01
ADAM_STEP = 10
PER_EXAMPLE_BATCH_AXIS = {'x': 0, 'c': 0, 'loss_target': 0}
SHARED_INPUTS = []
_WEIGHT_DTYPES = {'w_ada': _jnp.float32, 'b_ada': _jnp.float32, 'norm_pre': _jnp.float32, 'norm_post': _jnp.float32, 'ffn1_w_gu': _jnp.float32, 'ffn1_w_down': _jnp.float32, 'w_in': _jnp.float32, 'rel_bias': _jnp.float32, 'conv_w': _jnp.float32, 'conv_b': _jnp.float32, 'lru_wa': _jnp.float32, 'lru_ba': _jnp.float32, 'lru_wx': _jnp.float32, 'lru_bx': _jnp.float32, 'lru_lambda': _jnp.float32, 'w_att_o': _jnp.float32, 'w_rec_o': _jnp.float32, 'w_out': _jnp.float32, 'ffn2_w_gu': _jnp.float32, 'ffn2_w_down': _jnp.float32}
MOMENT_SCALE = {'w_ada': 8.093201e-01, 'b_ada': 1.698298e+00, 'norm_pre': 7.537816e-02, 'norm_post': 2.039426e+00, 'ffn1_w_gu': 3.102225e-02, 'ffn1_w_down': 5.491056e-02, 'w_in': 1.262197e-01, 'rel_bias': 1.264469e-02, 'conv_w': 2.689845e-01, 'conv_b': 8.815520e-01, 'lru_wa': 3.146057e-02, 'lru_ba': 4.559030e-02, 'lru_wx': 6.887177e-02, 'lru_bx': 9.619753e-02, 'lru_lambda': 1.167633e-01, 'w_att_o': 1.056309e-01, 'w_rec_o': 2.944747e-01, 'w_out': 3.152854e-01, 'ffn2_w_gu': 2.956076e-02, 'ffn2_w_down': 5.288898e-02}


def _to_microbatches(a, axis):
    t = _jnp.moveaxis(a, axis, 0)
    t = t.reshape((N_MICROBATCH, t.shape[0] // N_MICROBATCH) + t.shape[1:])
    return _jnp.moveaxis(t, 1, axis + 1)


def setup_inputs(seed: int = 0) -> dict:
    inp = _fwd_setup_inputs(seed)
    key = _jax.random.fold_in(_jax.random.key(seed), 7919)
    shape, _ = _output_shape()
    out = dict(inp)
    out["loss_target"] = _jax.random.normal(_jax.random.fold_in(key, 0), shape, _jnp.float32)
    for i, name in enumerate(TWIN_WEIGHTS):
        w = inp[name].astype(_jnp.float32)
        if MOMENT_SCALE is None:
            s = _jnp.sqrt(_jnp.mean(_jnp.square(w)) + 1e-30)
        else:
            s = MOMENT_SCALE[name]
        km, kv = _jax.random.split(_jax.random.fold_in(key, i + 1))
        out[name] = w
        out["m_" + name] = s * _jax.random.normal(km, w.shape, _jnp.float32)
        out["v_" + name] = (s * s) * _jax.random.uniform(kv, w.shape, _jnp.float32, 0.5, 1.5)
    if N_MICROBATCH > 1:
        for name, axis in PER_EXAMPLE_BATCH_AXIS.items():
            out[name] = _to_microbatches(out[name], axis)
    return {'x': out['x'], 'c': out['c'], 'w_ada': out['w_ada'], 'b_ada': out['b_ada'], 'norm_pre': out['norm_pre'], 'norm_post': out['norm_post'], 'ffn1_w_gu': out['ffn1_w_gu'], 'ffn1_w_down': out['ffn1_w_down'], 'w_in': out['w_in'], 'rel_bias': out['rel_bias'], 'conv_w': out['conv_w'], 'conv_b': out['conv_b'], 'lru_wa': out['lru_wa'], 'lru_ba': out['lru_ba'], 'lru_wx': out['lru_wx'], 'lru_bx': out['lru_bx'], 'lru_lambda': out['lru_lambda'], 'w_att_o': out['w_att_o'], 'w_rec_o': out['w_rec_o'], 'w_out': out['w_out'], 'ffn2_w_gu': out['ffn2_w_gu'], 'ffn2_w_down': out['ffn2_w_down'], 'loss_target': out['loss_target'], 'm_w_ada': out['m_w_ada'], 'm_b_ada': out['m_b_ada'], 'm_norm_pre': out['m_norm_pre'], 'm_norm_post': out['m_norm_post'], 'm_ffn1_w_gu': out['m_ffn1_w_gu'], 'm_ffn1_w_down': out['m_ffn1_w_down'], 'm_w_in': out['m_w_in'], 'm_rel_bias': out['m_rel_bias'], 'm_conv_w': out['m_conv_w'], 'm_conv_b': out['m_conv_b'], 'm_lru_wa': out['m_lru_wa'], 'm_lru_ba': out['m_lru_ba'], 'm_lru_wx': out['m_lru_wx'], 'm_lru_bx': out['m_lru_bx'], 'm_lru_lambda': out['m_lru_lambda'], 'm_w_att_o': out['m_w_att_o'], 'm_w_rec_o': out['m_w_rec_o'], 'm_w_out': out['m_w_out'], 'm_ffn2_w_gu': out['m_ffn2_w_gu'], 'm_ffn2_w_down': out['m_ffn2_w_down'], 'v_w_ada': out['v_w_ada'], 'v_b_ada': out['v_b_ada'], 'v_norm_pre': out['v_norm_pre'], 'v_norm_post': out['v_norm_post'], 'v_ffn1_w_gu': out['v_ffn1_w_gu'], 'v_ffn1_w_down': out['v_ffn1_w_down'], 'v_w_in': out['v_w_in'], 'v_rel_bias': out['v_rel_bias'], 'v_conv_w': out['v_conv_w'], 'v_conv_b': out['v_conv_b'], 'v_lru_wa': out['v_lru_wa'], 'v_lru_ba': out['v_lru_ba'], 'v_lru_wx': out['v_lru_wx'], 'v_lru_bx': out['v_lru_bx'], 'v_lru_lambda': out['v_lru_lambda'], 'v_w_att_o': out['v_w_att_o'], 'v_w_rec_o': out['v_w_rec_o'], 'v_w_out': out['v_w_out'], 'v_ffn2_w_gu': out['v_ffn2_w_gu'], 'v_ffn2_w_down': out['v_ffn2_w_down']}


def _loss(weights, diff, rest, loss_target):
    with _jax.named_scope("forward"):
        args = {**rest, TWIN_DIFF_INPUT: diff, **{k: w.astype(_WEIGHT_DTYPES[k]) for k, w in weights.items()}}
        y = _forward(args)
    with _jax.named_scope("loss_head"):
        err = _jnp.square(y.astype(_jnp.float32) - loss_target)
        return 0.5 * _jnp.sum(_jnp.mean(err, axis=-1)) if err.ndim else 0.5 * err


def _adamw(w, g, m, v):
    m = ADAM_B1 * m + (1.0 - ADAM_B1) * g
    v = ADAM_B2 * v + (1.0 - ADAM_B2) * _jnp.square(g)
    m_hat = m / (1.0 - ADAM_B1 ** ADAM_STEP)
    v_hat = v / (1.0 - ADAM_B2 ** ADAM_STEP)
    delta = -ADAM_LR * (m_hat / (_jnp.sqrt(v_hat) + ADAM_EPS) + ADAM_WD * w)
    return delta, m, v


def reference(x, c, w_ada, b_ada, norm_pre, norm_post, ffn1_w_gu, ffn1_w_down, w_in, rel_bias, conv_w, conv_b, lru_wa, lru_ba, lru_wx, lru_bx, lru_lambda, w_att_o, w_rec_o, w_out, ffn2_w_gu, ffn2_w_down, loss_target, m_w_ada, m_b_ada, m_norm_pre, m_norm_post, m_ffn1_w_gu, m_ffn1_w_down, m_w_in, m_rel_bias, m_conv_w, m_conv_b, m_lru_wa, m_lru_ba, m_lru_wx, m_lru_bx, m_lru_lambda, m_w_att_o, m_w_rec_o, m_w_out, m_ffn2_w_gu, m_ffn2_w_down, v_w_ada, v_b_ada, v_norm_pre, v_norm_post, v_ffn1_w_gu, v_ffn1_w_down, v_w_in, v_rel_bias, v_conv_w, v_conv_b, v_lru_wa, v_lru_ba, v_lru_wx, v_lru_bx, v_lru_lambda, v_w_att_o, v_w_rec_o, v_w_out, v_ffn2_w_gu, v_ffn2_w_down):
    given = dict(x=x, c=c, w_ada=w_ada, b_ada=b_ada, norm_pre=norm_pre, norm_post=norm_post, ffn1_w_gu=ffn1_w_gu, ffn1_w_down=ffn1_w_down, w_in=w_in, rel_bias=rel_bias, conv_w=conv_w, conv_b=conv_b, lru_wa=lru_wa, lru_ba=lru_ba, lru_wx=lru_wx, lru_bx=lru_bx, lru_lambda=lru_lambda, w_att_o=w_att_o, w_rec_o=w_rec_o, w_out=w_out, ffn2_w_gu=ffn2_w_gu, ffn2_w_down=ffn2_w_down, loss_target=loss_target, m_w_ada=m_w_ada, m_b_ada=m_b_ada, m_norm_pre=m_norm_pre, m_norm_post=m_norm_post, m_ffn1_w_gu=m_ffn1_w_gu, m_ffn1_w_down=m_ffn1_w_down, m_w_in=m_w_in, m_rel_bias=m_rel_bias, m_conv_w=m_conv_w, m_conv_b=m_conv_b, m_lru_wa=m_lru_wa, m_lru_ba=m_lru_ba, m_lru_wx=m_lru_wx, m_lru_bx=m_lru_bx, m_lru_lambda=m_lru_lambda, m_w_att_o=m_w_att_o, m_w_rec_o=m_w_rec_o, m_w_out=m_w_out, m_ffn2_w_gu=m_ffn2_w_gu, m_ffn2_w_down=m_ffn2_w_down, v_w_ada=v_w_ada, v_b_ada=v_b_ada, v_norm_pre=v_norm_pre, v_norm_post=v_norm_post, v_ffn1_w_gu=v_ffn1_w_gu, v_ffn1_w_down=v_ffn1_w_down, v_w_in=v_w_in, v_rel_bias=v_rel_bias, v_conv_w=v_conv_w, v_conv_b=v_conv_b, v_lru_wa=v_lru_wa, v_lru_ba=v_lru_ba, v_lru_wx=v_lru_wx, v_lru_bx=v_lru_bx, v_lru_lambda=v_lru_lambda, v_w_att_o=v_w_att_o, v_w_rec_o=v_w_rec_o, v_w_out=v_w_out, v_ffn2_w_gu=v_ffn2_w_gu, v_ffn2_w_down=v_ffn2_w_down)
    weights = {n: given[n] for n in TWIN_WEIGHTS}
    shared = {n: given[n] for n in SHARED_INPUTS}
    per_example = {n: given[n] for n in ['x', 'c']}
    grad_fn = _jax.value_and_grad(_loss, argnums=(0, 1))

    def one_microbatch(ex, loss_target):
        ex = dict(ex)
        diff = ex.pop(TWIN_DIFF_INPUT)
        return grad_fn(weights, diff, {**shared, **ex}, loss_target)

    if N_MICROBATCH == 1:
        loss, (grad_w, grad_x) = one_microbatch(per_example, given["loss_target"])
    else:
        def body(carry, xs):
            loss_sum, grad_sum = carry
            l_k, (gw_k, gx_k) = one_microbatch(xs[0], xs[1])
            with _jax.named_scope("update"):
                return (loss_sum + l_k, _jax.tree.map(_jnp.add, grad_sum, gw_k)), gx_k

        init = (_jnp.zeros((), _jnp.float32), _jax.tree.map(_jnp.zeros_like, weights))
        (loss, grad_w), grad_x = _jax.lax.scan(body, init, (per_example, given["loss_target"]))
    with _jax.named_scope("update"):
        delta_w, new_m, new_v = {}, {}, {}
        for n in TWIN_WEIGHTS:
            delta_w[n], new_m[n], new_v[n] = _adamw(weights[n], grad_w[n], given["m_" + n], given["v_" + n])
    return (loss, grad_x, *[grad_w[n] for n in TWIN_WEIGHTS], *[delta_w[n] for n in TWIN_WEIGHTS],
            *[new_m[n] for n in TWIN_WEIGHTS], *[new_v[n] for n in TWIN_WEIGHTS])
```

```python
import functools

import numpy as np
import jax
import jax.numpy as jnp
from jax import lax
from jax.experimental import pallas as pl
from jax.experimental.pallas import tpu as pltpu

F32 = jnp.float32
BF16 = jnp.bfloat16

D = 1024
FF = 2816
PW = 5632
HP = 128
CHUNK = 64
WIN = 640
TQ = 512
EPS = 1e-6
NEG = -1e30
LRU_C = 8.0
N_DEV = 8
VMEM_LIMIT = 50 * 1024 * 1024

ADAM_LR, ADAM_B1, ADAM_B2, ADAM_EPS, ADAM_WD, ADAM_STEP = 0.001, 0.9, 0.999, 1e-08, 0.01, 10

MESH = pl.DeviceIdType.MESH
ANY = pl.BlockSpec(memory_space=pl.ANY)


def _cp(*sem):
    return pltpu.CompilerParams(dimension_semantics=tuple(sem), vmem_limit_bytes=VMEM_LIMIT)


def _dot(a, b):
    return jnp.dot(a, b, preferred_element_type=F32)


def _dot_nt(a, b):
    return lax.dot_general(a, b, (((1,), (1,)), ((), ())), preferred_element_type=F32)


def _dot_tn(a, b):
    return lax.dot_general(a, b, (((0,), (0,)), ((), ())), preferred_element_type=F32)


def _mean(v):
    return jnp.mean(v, axis=-1, keepdims=True)


def _colsum(v):
    return jnp.sum(v, axis=0, keepdims=True)


def _sigmoid(v):
    return jax.nn.sigmoid(v)


def _expm1(v):
    small = v * (1.0 + v * 0.5 * (1.0 + v * (1.0 / 3.0) * (1.0 + v * 0.25 * (1.0 + v * 0.2 * (
        1.0 + v * (1.0 / 6.0) * (1.0 + v * (1.0 / 7.0)))))))
    return jnp.where(jnp.abs(v) < 0.25, small, jnp.exp(v) - 1.0)


_GK = 0.7978845608028654


def _gelu(v):
    t = jnp.tanh(_GK * (v + 0.044715 * v * v * v))
    return 0.5 * v * (1.0 + t)


def _gelu_grad(v):
    t = jnp.tanh(_GK * (v + 0.044715 * v * v * v))
    return 0.5 * (1.0 + t) + 0.5 * v * (1.0 - t * t) * _GK * (1.0 + 3.0 * 0.044715 * v * v)


def _pre_norm(xv, vec_ref):
    r = lax.rsqrt(_mean(xv * xv) + EPS)
    n = xv * r * vec_ref[0:1, :]
    return n * (1.0 + vec_ref[3:4, :]) + vec_ref[2:3, :]


def _pre_norm_bwd(dh, xv, dres, vec_ref, vacc_ref):
    r = lax.rsqrt(_mean(xv * xv) + EPS)
    xh = xv * r
    n = xh * vec_ref[0:1, :]
    vacc_ref[2:3, :] += _colsum(dh)
    vacc_ref[3:4, :] += _colsum(dh * n)
    dn = dh * (1.0 + vec_ref[3:4, :])
    vacc_ref[0:1, :] += _colsum(dn * xh)
    dxh = dn * vec_ref[0:1, :]
    return r * (dxh - xh * _mean(dxh * xh)) + dres


def _post_norm_bwd(dxo, fv, res, vec_ref, vacc_ref):
    rf = lax.rsqrt(_mean(fv * fv) + EPS)
    fh = fv * rf
    gp = vec_ref[1:2, :]
    vacc_ref[4:5, :] += _colsum(res * dxo * (fh * gp))
    dy = (res * vec_ref[4:5, :]) * dxo
    vacc_ref[1:2, :] += _colsum(dy * fh)
    dfn = dy * gp
    return rf * (dfn - fh * _mean(dfn * fh))


def ffn_fwd(x, vec, w_gu, w_dn, res, name, tm=512, tf=256):
    S = x.shape[0]
    tm = min(tm, S)
    nf = FF // tf

    def body(x_ref, vec_ref, wg_ref, wu_ref, wd_ref, xo_ref, h_ref, g_ref, u_ref, a_ref, f_ref, hs, acc):
        j = pl.program_id(1)

        @pl.when(j == 0)
        def _():
            h = _pre_norm(x_ref[...], vec_ref).astype(BF16)
            hs[...] = h
            h_ref[...] = h
            acc[...] = jnp.zeros_like(acc)

        h = hs[...]
        g = _dot(h, wg_ref[...])
        u = _dot(h, wu_ref[...])
        g_ref[...] = g
        u_ref[...] = u
        a = (g * _sigmoid(g) * u).astype(BF16)
        a_ref[...] = a
        acc[...] += _dot(a, wd_ref[...])

        @pl.when(j == nf - 1)
        def _():
            f = acc[...]
            f_ref[...] = f
            y = f * lax.rsqrt(_mean(f * f) + EPS) * vec_ref[1:2, :]
            xo_ref[...] = x_ref[...] + (res * vec_ref[4:5, :]) * y

    row = lambda i, j: (i, 0)
    return pl.pallas_call(
        body, name=name, grid=(S // tm, nf),
        in_specs=[pl.BlockSpec((tm, D), row), pl.BlockSpec((8, D), lambda i, j: (0, 0)),
                  pl.BlockSpec((D, tf), lambda i, j: (0, j)), pl.BlockSpec((D, tf), lambda i, j: (0, j + nf)),
                  pl.BlockSpec((tf, D), lambda i, j: (j, 0))],
        out_specs=[pl.BlockSpec((tm, D), row), pl.BlockSpec((tm, D), row),
                   pl.BlockSpec((tm, tf), lambda i, j: (i, j)), pl.BlockSpec((tm, tf), lambda i, j: (i, j)),
                   pl.BlockSpec((tm, tf), lambda i, j: (i, j)), pl.BlockSpec((tm, D), row)],
        out_shape=[jax.ShapeDtypeStruct((S, D), F32), jax.ShapeDtypeStruct((S, D), BF16),
                   jax.ShapeDtypeStruct((S, FF), F32), jax.ShapeDtypeStruct((S, FF), F32),
                   jax.ShapeDtypeStruct((S, FF), BF16), jax.ShapeDtypeStruct((S, D), F32)],
        scratch_shapes=[pltpu.VMEM((tm, D), BF16), pltpu.VMEM((tm, D), F32)],
        compiler_params=_cp("parallel", "arbitrary"),
    )(x, vec, w_gu, w_gu, w_dn)


def ffn_bwd(dxo, x, f, g, u, vec, w_gu, w_dn, res, name, tm=512, tf=256):
    S = x.shape[0]
    tm = min(tm, S)
    nf = FF // tf

    def body(dxo_ref, x_ref, f_ref, g_ref, u_ref, vec_ref, wg_ref, wu_ref, wd_ref,
             dx_ref, df_ref, dgu_ref, vacc_ref, dfs, acc):
        i, j = pl.program_id(0), pl.program_id(1)

        @pl.when((i == 0) & (j == 0))
        def _():
            vacc_ref[...] = jnp.zeros_like(vacc_ref)

        @pl.when(j == 0)
        def _():
            df = _post_norm_bwd(dxo_ref[...], f_ref[...], res, vec_ref, vacc_ref).astype(BF16)
            dfs[...] = df
            df_ref[...] = df
            acc[...] = jnp.zeros_like(acc)

        da = _dot_nt(dfs[...], wd_ref[...])
        gv, uv = g_ref[...], u_ref[...]
        sg = _sigmoid(gv)
        dg = (da * uv * (sg * (1.0 + gv * (1.0 - sg)))).astype(BF16)
        du = (da * (gv * sg)).astype(BF16)
        dgu_ref[0] = dg
        dgu_ref[1] = du
        acc[...] += _dot_nt(dg, wg_ref[...]) + _dot_nt(du, wu_ref[...])

        @pl.when(j == nf - 1)
        def _():
            dx_ref[...] = _pre_norm_bwd(acc[...], x_ref[...], dxo_ref[...], vec_ref, vacc_ref)

    row = lambda i, j: (i, 0)
    return pl.pallas_call(
        body, name=name, grid=(S // tm, nf),
        in_specs=[pl.BlockSpec((tm, D), row), pl.BlockSpec((tm, D), row), pl.BlockSpec((tm, D), row),
                  pl.BlockSpec((tm, tf), lambda i, j: (i, j)), pl.BlockSpec((tm, tf), lambda i, j: (i, j)),
                  pl.BlockSpec((8, D), lambda i, j: (0, 0)),
                  pl.BlockSpec((D, tf), lambda i, j: (0, j)), pl.BlockSpec((D, tf), lambda i, j: (0, j + nf)),
                  pl.BlockSpec((tf, D), lambda i, j: (j, 0))],
        out_specs=[pl.BlockSpec((tm, D), row), pl.BlockSpec((tm, D), row),
                   pl.BlockSpec((2, tm, tf), lambda i, j: (0, i, j)),
                   pl.BlockSpec((8, D), lambda i, j: (0, 0))],
        out_shape=[jax.ShapeDtypeStruct((S, D), F32), jax.ShapeDtypeStruct((S, D), BF16),
                   jax.ShapeDtypeStruct((2, S, FF), BF16), jax.ShapeDtypeStruct((8, D), F32)],
        scratch_shapes=[pltpu.VMEM((tm, D), BF16), pltpu.VMEM((tm, D), F32)],
        compiler_params=_cp("arbitrary", "arbitrary"),
    )(dxo, x, f, g, u, vec, w_gu, w_gu, w_dn)


def mm_tn(a, b, name, tm, tn, tk, out_dtype=BF16, prev=None, col_off=0, n_total=None):
    S, M = a.shape
    if b.ndim == 3:
        G, _, Nf = b.shape
    else:
        G, Nf = 1, b.shape[1]
    N = G * Nf
    n_total = N if n_total is None else n_total
    tk = min(tk, S)
    nbf = Nf // tn
    nk = S // tk
    ob = col_off // tn

    def body(*refs):
        a_ref, b_ref = refs[0], refs[1]
        o_ref, acc = refs[-2], refs[-1]
        k = pl.program_id(2)

        @pl.when(k == 0)
        def _():
            acc[...] = jnp.zeros_like(acc)

        acc[...] += _dot_tn(a_ref[...], b_ref[...])

        @pl.when(k == nk - 1)
        def _():
            o_ref[...] = acc[...].astype(out_dtype)

    if b.ndim == 3:
        b_spec = pl.BlockSpec((None, tk, tn), lambda i, j, k: (j // nbf, k, j % nbf))
    else:
        b_spec = pl.BlockSpec((tk, tn), lambda i, j, k: (k, j))
    in_specs = [pl.BlockSpec((tk, tm), lambda i, j, k: (k, i)), b_spec]
    args = [a, b]
    aliases = {}
    if prev is not None:
        in_specs.append(ANY)
        args.append(prev)
        aliases = {2: 0}
    return pl.pallas_call(
        body, name=name, grid=(M // tm, N // tn, nk),
        in_specs=in_specs,
        out_specs=pl.BlockSpec((tm, tn), lambda i, j, k: (i, j + ob)),
        out_shape=jax.ShapeDtypeStruct((M, n_total), out_dtype),
        scratch_shapes=[pltpu.VMEM((tm, tn), F32)],
        input_output_aliases=aliases,
        compiler_params=_cp("parallel", "parallel", "arbitrary"),
    )(*args)


def proj_fwd(x, vec, w_in, name, tm=512, tn=512):
    S = x.shape[0]
    tm = min(tm, S)
    nq = 1536 // tn

    def body(x_ref, vec_ref, w_ref, h_ref, qkv_ref, rest_ref, hs):
        j = pl.program_id(1)

        @pl.when(j == 0)
        def _():
            h = _pre_norm(x_ref[...], vec_ref).astype(BF16)
            hs[...] = h
            h_ref[...] = h

        r = _dot(hs[...], w_ref[...])

        @pl.when(j < nq)
        def _():
            qkv_ref[...] = r.astype(BF16)

        @pl.when(j >= nq)
        def _():
            rest_ref[...] = r

    row = lambda i, j: (i, 0)
    return pl.pallas_call(
        body, name=name, grid=(S // tm, PW // tn),
        in_specs=[pl.BlockSpec((tm, D), row), pl.BlockSpec((8, D), lambda i, j: (0, 0)),
                  pl.BlockSpec((D, tn), lambda i, j: (0, j))],
        out_specs=[pl.BlockSpec((tm, D), row),
                   pl.BlockSpec((tm, tn), lambda i, j: (i, jnp.minimum(j, nq - 1))),
                   pl.BlockSpec((tm, tn), lambda i, j: (i, jnp.maximum(j - nq, 0)))],
        out_shape=[jax.ShapeDtypeStruct((S, D), BF16), jax.ShapeDtypeStruct((S, 1536), BF16),
                   jax.ShapeDtypeStruct((S, 4096), F32)],
        scratch_shapes=[pltpu.VMEM((tm, D), BF16)],
        compiler_params=_cp("parallel", "arbitrary"),
    )(x, vec, w_in)


def proj_bwd(dq, dkv, dxr, d3, w_in, x, dxo, vec, name, tm=512, tk=512):
    S = x.shape[0]
    tm = min(tm, S)
    nk = PW // tk

    def body(dq_ref, dkv_ref, dxr_ref, d3_ref, w_ref, x_ref, dxo_ref, vec_ref, dx_ref, vacc_ref, acc):
        i, j = pl.program_id(0), pl.program_id(1)

        @pl.when((i == 0) & (j == 0))
        def _():
            vacc_ref[...] = jnp.zeros_like(vacc_ref)

        @pl.when(j == 0)
        def _():
            acc[...] = _dot_nt(dq_ref[...], w_ref[...])

        @pl.when((j >= 1) & (j < 3))
        def _():
            acc[...] += _dot_nt(dkv_ref[...], w_ref[...])

        @pl.when((j >= 3) & (j < 5))
        def _():
            acc[...] += _dot_nt(dxr_ref[...], w_ref[...])

        @pl.when(j >= 5)
        def _():
            acc[...] += _dot_nt(d3_ref[...], w_ref[...])

        @pl.when(j == nk - 1)
        def _():
            dx_ref[...] = _pre_norm_bwd(acc[...], x_ref[...], dxo_ref[...], vec_ref, vacc_ref)

    row = lambda i, j: (i, 0)
    return pl.pallas_call(
        body, name=name, grid=(S // tm, nk),
        in_specs=[pl.BlockSpec((None, tm, tk), lambda i, j: (0, i, 0)),
                  pl.BlockSpec((None, tm, tk), lambda i, j: (jnp.clip(j - 1, 0, 1), i, 0)),
                  pl.BlockSpec((tm, tk), lambda i, j: (i, jnp.clip(j - 3, 0, 1))),
                  pl.BlockSpec((None, tm, tk), lambda i, j: (jnp.clip(j - 5, 0, 5) // 2, i, jnp.clip(j - 5, 0, 5) % 2)),
                  pl.BlockSpec((D, tk), lambda i, j: (0, j)),
                  pl.BlockSpec((tm, D), row), pl.BlockSpec((tm, D), row),
                  pl.BlockSpec((8, D), lambda i, j: (0, 0))],
        out_specs=[pl.BlockSpec((tm, D), row), pl.BlockSpec((8, D), lambda i, j: (0, 0))],
        out_shape=[jax.ShapeDtypeStruct((S, D), F32), jax.ShapeDtypeStruct((8, D), F32)],
        scratch_shapes=[pltpu.VMEM((tm, D), F32)],
        compiler_params=_cp("arbitrary", "arbitrary"),
    )(dq, dkv, dxr, d3, w_in, x, dxo, vec)


def _attn_probs(qm, ka, bias_h, i, grp):
    s = _dot_nt(qm, ka) + bias_h
    col = lax.broadcasted_iota(jnp.int32, s.shape, 1)
    first_key = jnp.where(i == 0, 512 - 128 * grp, 0)
    s = jnp.where(col >= first_key, s, NEG)
    e = jnp.exp(s - jnp.max(s, axis=-1, keepdims=True))
    return e / jnp.sum(e, axis=-1, keepdims=True)


def attn_fwd(qkv, bias, name):
    S = qkv.shape[0]
    nb = S // TQ

    def body(q_ref, kp_ref, kc_ref, vp_ref, vc_ref, b_ref, o_ref, kw, vw):
        i = pl.program_id(1)
        kw[0:TQ, :] = kp_ref[...]
        kw[TQ:2 * TQ, :] = kc_ref[...]
        vw[0:TQ, :] = vp_ref[...]
        vw[TQ:2 * TQ, :] = vc_ref[...]
        lane = lax.broadcasted_iota(jnp.int32, (1, HP), 1)
        zero = jnp.zeros((), BF16)

        def group(a, carry):
            r0 = pl.multiple_of(a * 128, 128)
            qa = q_ref[pl.ds(r0, 128), :] * jnp.asarray(0.125, BF16)
            ka = kw[pl.ds(r0, WIN), :]
            va = vw[pl.ds(r0, WIN), :]
            o = jnp.zeros((128, HP), F32)
            for hh in range(2):
                msk = (lane < 64) if hh == 0 else (lane >= 64)
                p = _attn_probs(jnp.where(msk, qa, zero), ka, b_ref[hh], i, a)
                o += _dot(p.astype(BF16), jnp.where(msk, va, zero))
            o_ref[pl.ds(r0, 128), :] = o.astype(BF16)
            return carry

        lax.fori_loop(0, TQ // 128, group, 0)

    prev = lambda h, i: (jnp.maximum(i - 1, 0), 0)
    return pl.pallas_call(
        body, name=name, grid=(4, nb),
        in_specs=[pl.BlockSpec((TQ, HP), lambda h, i: (i, h)),
                  pl.BlockSpec((TQ, HP), lambda h, i: (jnp.maximum(i - 1, 0), 4 + h)),
                  pl.BlockSpec((TQ, HP), lambda h, i: (i, 4 + h)),
                  pl.BlockSpec((TQ, HP), lambda h, i: (jnp.maximum(i - 1, 0), 8 + h)),
                  pl.BlockSpec((TQ, HP), lambda h, i: (i, 8 + h)),
                  pl.BlockSpec((2, 128, WIN), lambda h, i: (h, 0, 0))],
        out_specs=pl.BlockSpec((TQ, HP), lambda h, i: (i, h)),
        out_shape=jax.ShapeDtypeStruct((S, 512), BF16),
        scratch_shapes=[pltpu.VMEM((2 * TQ, HP), BF16), pltpu.VMEM((2 * TQ, HP), BF16)],
        compiler_params=_cp("parallel", "arbitrary"),
    )(qkv, qkv, qkv, qkv, qkv, bias)


def attn_bwd(qkv, do, bias, name):
    S = qkv.shape[0]
    nb = S // TQ

    def body(q_ref, kp_ref, kc_ref, vp_ref, vc_ref, do_ref, b_ref, dqkv_ref, db_ref, dkv_ref, kw, vw, ak, av):
        i = pl.program_id(1)

        @pl.when(i == 0)
        def _():
            db_ref[...] = jnp.zeros_like(db_ref)
            ak[...] = jnp.zeros_like(ak)
            av[...] = jnp.zeros_like(av)

        @pl.when(i > 0)
        def _():
            ak[0:TQ, :] = ak[TQ:2 * TQ, :]
            av[0:TQ, :] = av[TQ:2 * TQ, :]
            ak[TQ:2 * TQ, :] = jnp.zeros((TQ, HP), F32)
            av[TQ:2 * TQ, :] = jnp.zeros((TQ, HP), F32)

        @pl.when(i < nb)
        def _():
            kw[0:TQ, :] = kp_ref[...]
            kw[TQ:2 * TQ, :] = kc_ref[...]
            vw[0:TQ, :] = vp_ref[...]
            vw[TQ:2 * TQ, :] = vc_ref[...]
            lane = lax.broadcasted_iota(jnp.int32, (1, HP), 1)
            zero = jnp.zeros((), BF16)

            def group(a, carry):
                r0 = pl.multiple_of(a * 128, 128)
                qa = q_ref[pl.ds(r0, 128), :] * jnp.asarray(0.125, BF16)
                doa = do_ref[pl.ds(r0, 128), :]
                ka = kw[pl.ds(r0, WIN), :]
                va = vw[pl.ds(r0, WIN), :]
                dq = jnp.zeros((128, HP), F32)
                for hh in range(2):
                    msk = (lane < 64) if hh == 0 else (lane >= 64)
                    qm = jnp.where(msk, qa, zero)
                    dom = jnp.where(msk, doa, zero)
                    p = _attn_probs(qm, ka, b_ref[hh], i, a)
                    dp = _dot_nt(dom, va)
                    ds = p * (dp - jnp.sum(p * dp, axis=-1, keepdims=True))
                    db_ref[hh] += ds
                    dsb = ds.astype(BF16)
                    dq += jnp.where(msk, _dot(dsb, ka), 0.0)
                    ak[pl.ds(r0, WIN), :] += _dot_tn(dsb, qm)
                    av[pl.ds(r0, WIN), :] += _dot_tn(p.astype(BF16), dom)
                dqkv_ref[0, pl.ds(r0, 128), :] = (dq * 0.125).astype(BF16)
                return carry

            lax.fori_loop(0, TQ // 128, group, 0)

        @pl.when(i > 0)
        def _():
            dkv_ref[0] = ak[0:TQ, :].astype(BF16)
            dkv_ref[1] = av[0:TQ, :].astype(BF16)

    cur = lambda i: jnp.minimum(i, nb - 1)
    prv = lambda i: jnp.clip(i - 1, 0, nb - 1)
    dq, db, dkv = pl.pallas_call(
        body, name=name, grid=(4, nb + 1),
        in_specs=[pl.BlockSpec((TQ, HP), lambda h, i: (cur(i), h)),
                  pl.BlockSpec((TQ, HP), lambda h, i: (prv(i), 4 + h)),
                  pl.BlockSpec((TQ, HP), lambda h, i: (cur(i), 4 + h)),
                  pl.BlockSpec((TQ, HP), lambda h, i: (prv(i), 8 + h)),
                  pl.BlockSpec((TQ, HP), lambda h, i: (cur(i), 8 + h)),
                  pl.BlockSpec((TQ, HP), lambda h, i: (cur(i), h)),
                  pl.BlockSpec((2, 128, WIN), lambda h, i: (h, 0, 0))],
        out_specs=[pl.BlockSpec((1, TQ, HP), lambda h, i: (0, cur(i), h)),
                   pl.BlockSpec((2, 128, WIN), lambda h, i: (h, 0, 0)),
                   pl.BlockSpec((2, TQ, HP), lambda h, i: (0, prv(i), h))],
        out_shape=[jax.ShapeDtypeStruct((1, S, 512), BF16), jax.ShapeDtypeStruct((8, 128, WIN), F32),
                   jax.ShapeDtypeStruct((2, S, 512), BF16)],
        scratch_shapes=[pltpu.VMEM((2 * TQ, HP), BF16), pltpu.VMEM((2 * TQ, HP), BF16),
                        pltpu.VMEM((2 * TQ, HP), F32), pltpu.VMEM((2 * TQ, HP), F32)],
        compiler_params=_cp("parallel", "arbitrary"),
    )(qkv, qkv, qkv, qkv, qkv, do, bias)
    return dq, db, dkv


def bias_grad(db, name):
    def body(db_ref, o_ref):
        r = lax.broadcasted_iota(jnp.int32, (128, 128), 0)
        c = lax.broadcasted_iota(jnp.int32, (128, 128), 1)
        flip = (r + c == 127).astype(BF16)
        lane = lax.broadcasted_iota(jnp.int32, (16, 384), 1)
        src = lax.broadcasted_iota(jnp.int32, (128, 384), 0)
        dst = lax.broadcasted_iota(jnp.int32, (128, 384), 1)

        def split_dot(v, m):
            hi = v.astype(BF16)
            r1 = v - hi.astype(F32)
            mid = r1.astype(BF16)
            lo = (r1 - mid.astype(F32)).astype(BF16)
            return _dot(hi, m) + _dot(mid, m) + _dot(lo, m)

        def diag_sums(w):
            y = pltpu.roll(split_dot(w, flip), 0, 1, stride=1, stride_axis=0)
            return jnp.broadcast_to(_colsum(y), (16, 128))

        w4 = db_ref[0, :, 512:640]
        w3 = db_ref[0, :, 384:512]
        far = jnp.sum(db_ref[0, :, 0:384]) + jnp.sum(jnp.where(r >= c, w3, 0.0))
        lo4 = diag_sums(jnp.where(r >= c, w4, 0.0))
        up4 = diag_sums(jnp.where(r < c, w4, 0.0))
        up3 = diag_sums(jnp.where(r < c, w3, 0.0))
        p_lo4 = (dst == 128 + (src + 1) % 128).astype(BF16)
        p_up4 = ((dst == src + 1) & (src < 127)).astype(BF16)
        p_up3 = ((dst == src + 129) & (src < 127)).astype(BF16)
        out = split_dot(lo4, p_lo4) + split_dot(up4, p_up4) + split_dot(up3, p_up3)
        o_ref[0] = out + jnp.where(lane == 256, far, 0.0)

    return pl.pallas_call(
        body, name=name, grid=(8,),
        in_specs=[pl.BlockSpec((1, 128, WIN), lambda h: (h, 0, 0))],
        out_specs=pl.BlockSpec((1, 16, 384), lambda h: (h, 0, 0)),
        out_shape=jax.ShapeDtypeStruct((8, 16, 384), F32),
        compiler_params=_cp("parallel"),
    )(db)[:, 0, :]


LT = 256
LC = 512


def _lru_gates(xs, pv_ref, wa_ref, wx_ref, tl):
    xc = (pv_ref[4:5, :] + pv_ref[3:4, :] * xs[pl.ds(8, tl), :] + pv_ref[2:3, :] * xs[pl.ds(7, tl), :]
          + pv_ref[1:2, :] * xs[pl.ds(6, tl), :] + pv_ref[0:1, :] * xs[pl.ds(5, tl), :])
    xcb = xc.astype(BF16)
    pa = jnp.concatenate([_dot(xcb[:, 0:256], wa_ref[0]), _dot(xcb[:, 256:512], wa_ref[1])], axis=1)
    px = jnp.concatenate([_dot(xcb[:, 0:256], wx_ref[0]), _dot(xcb[:, 256:512], wx_ref[1])], axis=1)
    r = _sigmoid(pa + pv_ref[5:6, :])
    ig = _sigmoid(px + pv_ref[6:7, :])
    z = -pv_ref[7:8, :]
    sp = jnp.maximum(z, 0.0) + jnp.log1p(jnp.exp(-jnp.abs(z)))
    log_a = (-LRU_C * r) * sp
    a = jnp.exp(log_a)
    mult = jnp.sqrt(-_expm1(2.0 * log_a))
    return xc, xcb, r, ig, sp, a, mult


def lru_fwd(rest, pvec, wa, wx, name):
    S = rest.shape[0]
    tl = min(LT, S)
    nt = S // tl

    def body(xr_ref, halo_ref, yr_ref, pv_ref, wa_ref, wx_ref, h_ref, hg_ref, xs, a_s, u_s, h_s, carry):
        ti = pl.program_id(1)

        @pl.when(ti == 0)
        def _():
            carry[...] = jnp.zeros_like(carry)

        xs[0:8, :] = jnp.where(ti > 0, halo_ref[...], 0.0)
        xs[pl.ds(8, tl), :] = xr_ref[...]
        xc, _, _, ig, _, a, mult = _lru_gates(xs, pv_ref, wa_ref, wx_ref, tl)
        a_s[...] = a
        u_s[...] = mult * (ig * xc)
        row = lax.broadcasted_iota(jnp.int32, (8, LC), 0)

        def blk(bi, c):
            o = pl.multiple_of(bi * 8, 8)
            av = a_s[pl.ds(o, 8), :]
            bv = u_s[pl.ds(o, 8), :]
            for d in (1, 2, 4):
                a_sh = pltpu.roll(av, d, 0)
                b_sh = pltpu.roll(bv, d, 0)
                m = row >= d
                bv = jnp.where(m, av * b_sh + bv, bv)
                av = jnp.where(m, av * a_sh, av)
            hv = bv + av * c
            h_s[pl.ds(o, 8), :] = hv
            return hv[7:8, :]

        carry[...] = lax.fori_loop(0, tl // 8, blk, carry[...])
        h = h_s[...]
        h_ref[...] = h
        hg_ref[...] = (h * _gelu(yr_ref[...])).astype(BF16)

    hb = tl // 8
    return pl.pallas_call(
        body, name=name, grid=(2, nt),
        in_specs=[pl.BlockSpec((tl, LC), lambda c, t: (t, c)),
                  pl.BlockSpec((8, LC), lambda c, t: (jnp.maximum(t * hb - 1, 0), c)),
                  pl.BlockSpec((tl, LC), lambda c, t: (t, 2 + c)),
                  pl.BlockSpec((8, LC), lambda c, t: (0, c)),
                  pl.BlockSpec((2, 256, 256), lambda c, t: (c, 0, 0)),
                  pl.BlockSpec((2, 256, 256), lambda c, t: (c, 0, 0))],
        out_specs=[pl.BlockSpec((tl, LC), lambda c, t: (t, c)), pl.BlockSpec((tl, LC), lambda c, t: (t, c))],
        out_shape=[jax.ShapeDtypeStruct((S, D), F32), jax.ShapeDtypeStruct((S, D), BF16)],
        scratch_shapes=[pltpu.VMEM((tl + 8, LC), F32), pltpu.VMEM((tl, LC), F32), pltpu.VMEM((tl, LC), F32),
                        pltpu.VMEM((tl, LC), F32), pltpu.VMEM((1, LC), F32)],
        compiler_params=_cp("parallel", "arbitrary"),
    )(rest, rest, rest, pvec, wa, wx)


def lru_bwd(dh, h, rest, pvec, wa, wx, name):
    S = rest.shape[0]
    tl = min(LT, S)
    nt = S // tl

    def body(dh_ref, h_ref, hhalo_ref, xr_ref, xhalo_ref, pv_ref, wa_ref, wx_ref,
             dxr_ref, vacc_ref, dwa_ref, dwx_ref,
             xs, hs, a_s, ash_s, b_s, lam_s, dxe, anext, lnext, dxnext):
        ti = pl.program_id(1)
        tr = nt - 1 - ti

        @pl.when(ti == 0)
        def _():
            anext[...] = jnp.zeros_like(anext)
            lnext[...] = jnp.zeros_like(lnext)
            dxnext[...] = jnp.zeros_like(dxnext)
            vacc_ref[...] = jnp.zeros_like(vacc_ref)
            dwa_ref[...] = jnp.zeros_like(dwa_ref)
            dwx_ref[...] = jnp.zeros_like(dwx_ref)

        xs[0:8, :] = jnp.where(tr > 0, xhalo_ref[...], 0.0)
        xs[pl.ds(8, tl), :] = xr_ref[...]
        xc, xcb, r, ig, sp, a, mult = _lru_gates(xs, pv_ref, wa_ref, wx_ref, tl)

        a_s[pl.ds(0, tl), :] = a
        a_s[pl.ds(tl, 8), :] = jnp.broadcast_to(anext[...], (8, LC))
        ash_s[...] = a_s[pl.ds(1, tl), :]
        b_s[...] = dh_ref[...]
        row = lax.broadcasted_iota(jnp.int32, (8, LC), 0)

        def blk(k, c):
            o = pl.multiple_of((tl // 8 - 1 - k) * 8, 8)
            av = ash_s[pl.ds(o, 8), :]
            bv = b_s[pl.ds(o, 8), :]
            for d in (1, 2, 4):
                a_sh = pltpu.roll(av, 8 - d, 0)
                b_sh = pltpu.roll(bv, 8 - d, 0)
                m = row < 8 - d
                bv = jnp.where(m, bv + av * b_sh, bv)
                av = jnp.where(m, av * a_sh, av)
            lv = bv + av * c
            lam_s[pl.ds(o, 8), :] = lv
            return lv[0:1, :]

        lnext[...] = lax.fori_loop(0, tl // 8, blk, lnext[...])
        anext[...] = a[0:1, :]
        lam = lam_s[...]

        hs[0:8, :] = jnp.where(tr > 0, hhalo_ref[...], 0.0)
        hs[pl.ds(8, tl), :] = h_ref[...]
        d_a = lam * hs[pl.ds(7, tl), :]
        d_mult = lam * (ig * xc)
        d_ig = lam * mult * xc
        dxc = lam * mult * ig
        d_log_a = d_a * a - d_mult * (a * a) / mult
        d_r = d_log_a * (-LRU_C * sp)
        vacc_ref[7:8, :] += _colsum(d_log_a * (-LRU_C * r)) * (-_sigmoid(-pv_ref[7:8, :]))
        d_pa = d_r * r * (1.0 - r)
        d_px = d_ig * ig * (1.0 - ig)
        vacc_ref[5:6, :] += _colsum(d_pa)
        vacc_ref[6:7, :] += _colsum(d_px)
        dpa = d_pa.astype(BF16)
        dpx = d_px.astype(BF16)
        back = []
        for g in range(2):
            sl = slice(256 * g, 256 * g + 256)
            dwa_ref[g] += _dot_tn(xcb[:, sl], dpa[:, sl])
            dwx_ref[g] += _dot_tn(xcb[:, sl], dpx[:, sl])
            back.append(_dot_nt(dpa[:, sl], wa_ref[g]) + _dot_nt(dpx[:, sl], wx_ref[g]))
        dxc = dxc + jnp.concatenate(back, axis=1)
        vacc_ref[4:5, :] += _colsum(dxc)
        for k in range(4):
            vacc_ref[k:k + 1, :] += _colsum(dxc * xs[pl.ds(5 + k, tl), :])
        dxe[pl.ds(0, tl), :] = dxc
        dxe[pl.ds(tl, 8), :] = dxnext[...]
        dxr = (pv_ref[3:4, :] * dxc + pv_ref[2:3, :] * dxe[pl.ds(1, tl), :]
               + pv_ref[1:2, :] * dxe[pl.ds(2, tl), :] + pv_ref[0:1, :] * dxe[pl.ds(3, tl), :])
        dxr_ref[...] = dxr.astype(BF16)
        dxnext[...] = dxc[0:8, :]

    hb = tl // 8
    rev = lambda t: nt - 1 - t
    halo = lambda t: jnp.maximum(rev(t) * hb - 1, 0)
    big = lambda: pltpu.VMEM((tl + 8, LC), F32)
    til = lambda: pltpu.VMEM((tl, LC), F32)
    return pl.pallas_call(
        body, name=name, grid=(2, nt),
        in_specs=[pl.BlockSpec((tl, LC), lambda c, t: (rev(t), c)),
                  pl.BlockSpec((tl, LC), lambda c, t: (rev(t), c)),
                  pl.BlockSpec((8, LC), lambda c, t: (halo(t), c)),
                  pl.BlockSpec((tl, LC), lambda c, t: (rev(t), c)),
                  pl.BlockSpec((8, LC), lambda c, t: (halo(t), c)),
                  pl.BlockSpec((8, LC), lambda c, t: (0, c)),
                  pl.BlockSpec((2, 256, 256), lambda c, t: (c, 0, 0)),
                  pl.BlockSpec((2, 256, 256), lambda c, t: (c, 0, 0))],
        out_specs=[pl.BlockSpec((tl, LC), lambda c, t: (rev(t), c)),
                   pl.BlockSpec((8, LC), lambda c, t: (0, c)),
                   pl.BlockSpec((2, 256, 256), lambda c, t: (c, 0, 0)),
                   pl.BlockSpec((2, 256, 256), lambda c, t: (c, 0, 0))],
        out_shape=[jax.ShapeDtypeStruct((S, D), BF16), jax.ShapeDtypeStruct((8, D), F32),
                   jax.ShapeDtypeStruct((4, 256, 256), F32), jax.ShapeDtypeStruct((4, 256, 256), F32)],
        scratch_shapes=[big(), big(), big(), til(), til(), til(), big(),
                        pltpu.VMEM((1, LC), F32), pltpu.VMEM((1, LC), F32), pltpu.VMEM((8, LC), F32)],
        compiler_params=_cp("parallel", "arbitrary"),
    )(dh, h, h, rest, rest, pvec, wa, wx)


def mix_out_fwd(x, ao, hg, rest, vec, w_att_o, w_rec_o, w_out, name, tm=256):
    S = x.shape[0]
    tm = min(tm, S)

    def body(x_ref, ao_ref, hg_ref, ga_ref, gr_ref, vec_ref, wa_ref, wr_ref, wo_ref,
             xo_ref, att_ref, rec_ref, mg_ref, f_ref):
        att = _dot(ao_ref[...], wa_ref[...])
        rec = _dot(hg_ref[...], wr_ref[...])
        att_ref[...] = att
        rec_ref[...] = rec
        mg = (_sigmoid(ga_ref[...]) * att + _sigmoid(gr_ref[...]) * rec).astype(BF16)
        mg_ref[...] = mg
        f = _dot(mg, wo_ref[...])
        f_ref[...] = f
        y = f * lax.rsqrt(_mean(f * f) + EPS) * vec_ref[1:2, :]
        xo_ref[...] = x_ref[...] + (1.0 * vec_ref[4:5, :]) * y

    row = lambda i: (i, 0)
    full = lambda r: pl.BlockSpec((r, D), lambda i: (0, 0))
    return pl.pallas_call(
        body, name=name, grid=(S // tm,),
        in_specs=[pl.BlockSpec((tm, D), row), pl.BlockSpec((tm, 512), row), pl.BlockSpec((tm, D), row),
                  pl.BlockSpec((tm, D), lambda i: (i, 2)), pl.BlockSpec((tm, D), lambda i: (i, 3)),
                  full(8), full(512), full(D), full(D)],
        out_specs=[pl.BlockSpec((tm, D), row)] * 5,
        out_shape=[jax.ShapeDtypeStruct((S, D), F32), jax.ShapeDtypeStruct((S, D), F32),
                   jax.ShapeDtypeStruct((S, D), F32), jax.ShapeDtypeStruct((S, D), BF16),
                   jax.ShapeDtypeStruct((S, D), F32)],
        compiler_params=_cp("parallel"),
    )(x, ao, hg, rest, rest, vec, w_att_o, w_rec_o, w_out)


def mix_out_bwd(dxo, f, att, rec, rest, h, vec, w_att_o, w_rec_o, w_out, name, tm=256):
    S = dxo.shape[0]
    tm = min(tm, S)

    def body(dxo_ref, f_ref, att_ref, rec_ref, yr_ref, ga_ref, gr_ref, h_ref, vec_ref, wa_ref, wr_ref, wo_ref,
             df_ref, da_ref, dr_ref, dao_ref, dh_ref, d3_ref, vacc_ref):
        @pl.when(pl.program_id(0) == 0)
        def _():
            vacc_ref[...] = jnp.zeros_like(vacc_ref)

        df = _post_norm_bwd(dxo_ref[...], f_ref[...], 1.0, vec_ref, vacc_ref).astype(BF16)
        df_ref[...] = df
        dm = _dot_nt(df, wo_ref[...])
        sa = _sigmoid(ga_ref[...])
        sr = _sigmoid(gr_ref[...])
        d_att = (dm * sa).astype(BF16)
        d_rec = (dm * sr).astype(BF16)
        da_ref[...] = d_att
        dr_ref[...] = d_rec
        d3_ref[1] = (dm * att_ref[...] * (sa * (1.0 - sa))).astype(BF16)
        d3_ref[2] = (dm * rec_ref[...] * (sr * (1.0 - sr))).astype(BF16)
        dao_ref[...] = _dot_nt(d_att, wa_ref[...]).astype(BF16)
        d_hg = _dot_nt(d_rec, wr_ref[...])
        yr = yr_ref[...]
        dh_ref[...] = d_hg * _gelu(yr)
        d3_ref[0] = (d_hg * h_ref[...] * _gelu_grad(yr)).astype(BF16)

    row = lambda i: (i, 0)
    full = lambda r: pl.BlockSpec((r, D), lambda i: (0, 0))
    return pl.pallas_call(
        body, name=name, grid=(S // tm,),
        in_specs=[pl.BlockSpec((tm, D), row)] * 4
        + [pl.BlockSpec((tm, D), lambda i: (i, 1)), pl.BlockSpec((tm, D), lambda i: (i, 2)),
           pl.BlockSpec((tm, D), lambda i: (i, 3)), pl.BlockSpec((tm, D), row),
           full(8), full(512), full(D), full(D)],
        out_specs=[pl.BlockSpec((tm, D), row)] * 3
        + [pl.BlockSpec((tm, 512), row), pl.BlockSpec((tm, D), row),
           pl.BlockSpec((3, tm, D), lambda i: (0, i, 0)), pl.BlockSpec((8, D), lambda i: (0, 0))],
        out_shape=[jax.ShapeDtypeStruct((S, D), BF16)] * 3
        + [jax.ShapeDtypeStruct((S, 512), BF16), jax.ShapeDtypeStruct((S, D), F32),
           jax.ShapeDtypeStruct((3, S, D), BF16), jax.ShapeDtypeStruct((8, D), F32)],
        compiler_params=_cp("arbitrary"),
    )(dxo, f, att, rec, rest, rest, rest, h, vec, w_att_o, w_rec_o, w_out)


def loss_grad(y, tgt, name, tm=512):
    S = y.shape[0]
    tm = min(tm, S)
    nt = S // tm

    def body(y_ref, t_ref, dy_ref, l_ref, acc):
        i = pl.program_id(0)

        @pl.when(i == 0)
        def _():
            acc[...] = jnp.zeros_like(acc)

        d = y_ref[...] - t_ref[...]
        dy_ref[...] = d * (1.0 / D)
        acc[...] += _colsum(d * d)

        @pl.when(i == nt - 1)
        def _():
            l_ref[...] = jnp.broadcast_to(0.5 * jnp.sum(acc[...]) * (1.0 / D), (8, 128))

    return pl.pallas_call(
        body, name=name, grid=(nt,),
        in_specs=[pl.BlockSpec((tm, D), lambda i: (i, 0))] * 2,
        out_specs=[pl.BlockSpec((tm, D), lambda i: (i, 0)), pl.BlockSpec((8, 128), lambda i: (0, 0))],
        out_shape=[jax.ShapeDtypeStruct((S, D), F32), jax.ShapeDtypeStruct((8, 128), F32)],
        scratch_shapes=[pltpu.VMEM((1, D), F32)],
        compiler_params=_cp("arbitrary"),
    )(y, tgt)


def ada_fwd(c_all, w_ada, b_ada, name, tn=768):
    n = w_ada.shape[1]

    def body(c_ref, w_ref, b_ref, o_ref):
        cv = c_ref[...]
        ca = (cv * _sigmoid(cv)).astype(BF16)
        o_ref[...] = _dot(ca, w_ref[...].astype(BF16)) + b_ref[...]

    return pl.pallas_call(
        body, name=name, grid=(n // tn,),
        in_specs=[pl.BlockSpec((8, D), lambda j: (0, 0)), pl.BlockSpec((D, tn), lambda j: (0, j)),
                  pl.BlockSpec((1, tn), lambda j: (0, j))],
        out_specs=pl.BlockSpec((8, tn), lambda j: (0, j)),
        out_shape=jax.ShapeDtypeStruct((8, n), F32),
        compiler_params=_cp("parallel"),
    )(c_all, w_ada, b_ada)


def ada_bwd(c_all_t, dmod, name, tn=768):
    n = dmod.shape[1]

    def body(c_ref, d_ref, o_ref):
        cv = c_ref[...]
        ca = (cv * _sigmoid(cv)).astype(BF16)
        o_ref[...] = _dot(ca, d_ref[...].astype(BF16))

    return pl.pallas_call(
        body, name=name, grid=(n // tn,),
        in_specs=[pl.BlockSpec((D, 128), lambda j: (0, 0)), pl.BlockSpec((128, tn), lambda j: (0, j))],
        out_specs=pl.BlockSpec((D, tn), lambda j: (0, j)),
        out_shape=jax.ShapeDtypeStruct((D, n), F32),
        compiler_params=_cp("parallel"),
    )(c_all_t, dmod)


def _row_tile(rows, cols, itemsize=4, budget=1536 * 1024):
    best = None
    for t in range(8, rows + 1, 8):
        if rows % t == 0 and t * cols * itemsize <= budget:
            best = t
    return rows if best is None else best


def sum_lead(parts, name, out_dtype=F32):
    n, R, C = parts.shape
    tr = _row_tile(R, C * n)

    def body(p_ref, o_ref):
        acc = p_ref[0].astype(F32)
        for k in range(1, n):
            acc = acc + p_ref[k].astype(F32)
        o_ref[...] = acc.astype(out_dtype)

    return pl.pallas_call(
        body, name=name, grid=(R // tr,),
        in_specs=[pl.BlockSpec((n, tr, C), lambda i: (0, i, 0))],
        out_specs=pl.BlockSpec((tr, C), lambda i: (i, 0)),
        out_shape=jax.ShapeDtypeStruct((R, C), out_dtype),
        compiler_params=_cp("parallel"),
    )(parts)


def add2(a, b, name, out_dtype=BF16):
    R, C = a.shape
    tr = _row_tile(R, C * 2)

    def body(a_ref, b_ref, o_ref):
        o_ref[...] = (a_ref[...].astype(F32) + b_ref[...].astype(F32)).astype(out_dtype)

    return pl.pallas_call(
        body, name=name, grid=(R // tr,),
        in_specs=[pl.BlockSpec((tr, C), lambda i: (i, 0))] * 2,
        out_specs=pl.BlockSpec((tr, C), lambda i: (i, 0)),
        out_shape=jax.ShapeDtypeStruct((R, C), out_dtype),
        compiler_params=_cp("parallel"),
    )(a, b)


def adamw(w, g, m, v, name):
    R, C = w.shape
    tr = _row_tile(R, C * 7, budget=8 * 1024 * 1024)

    def body(w_ref, g_ref, m_ref, v_ref, d_ref, mo_ref, vo_ref):
        gv = g_ref[...]
        mn = ADAM_B1 * m_ref[...] + (1.0 - ADAM_B1) * gv
        vn = ADAM_B2 * v_ref[...] + (1.0 - ADAM_B2) * (gv * gv)
        m_hat = mn / (1.0 - ADAM_B1 ** ADAM_STEP)
        v_hat = vn / (1.0 - ADAM_B2 ** ADAM_STEP)
        d_ref[...] = -ADAM_LR * (m_hat / (jnp.sqrt(v_hat) + ADAM_EPS) + ADAM_WD * w_ref[...])
        mo_ref[...] = mn
        vo_ref[...] = vn

    spec = pl.BlockSpec((tr, C), lambda i: (i, 0))
    return pl.pallas_call(
        body, name=name, grid=(R // tr,),
        in_specs=[spec] * 4, out_specs=[spec] * 3,
        out_shape=[jax.ShapeDtypeStruct((R, C), F32)] * 3,
        compiler_params=_cp("parallel"),
    )(w, g, m, v)


def _mesh_pos():
    return lax.axis_index("x"), lax.axis_index("y"), lax.axis_index("c")


def _other_chips(mx, my):
    return [(1 - mx, my), (mx, 1 - my), (1 - mx, 1 - my)]


def ag_small(x, name):
    R = x.shape[0]

    def body(x_ref, out_ref, send_sems, recv_sems, local_sem):
        mx, my, mc = _mesh_pos()
        me, sibling = (mx, my, mc), (mx, my, 1 - mc)
        chips = _other_chips(mx, my)

        def slot(px, py, pc):
            return out_ref.at[4 * px + 2 * py + pc]

        def copy(k, block, to, src=None):
            return pltpu.make_async_remote_copy(
                src_ref=slot(*block) if src is None else src, dst_ref=slot(*block),
                send_sem=send_sems.at[k], recv_sem=recv_sems.at[k], device_id=to, device_id_type=MESH)

        mine = pltpu.make_async_copy(x_ref, slot(*me), local_sem)
        mine.start()
        first = [copy(0, me, sibling, src=x_ref)]
        first += [copy(1 + j, me, (*chip, mc), src=x_ref) for j, chip in enumerate(chips)]
        for cp in first:
            cp.start()
        passed = [copy(4 + j, (*chip, mc), sibling) for j, chip in enumerate(chips)]
        for j, chip in enumerate(chips):
            copy(1 + j, (*chip, mc), me).wait_recv()
            passed[j].start()
        copy(0, sibling, me).wait_recv()
        for j, chip in enumerate(chips):
            copy(4 + j, (*chip, 1 - mc), me).wait_recv()
        for cp in first + passed:
            cp.wait_send()
        mine.wait()

    return pl.pallas_call(
        body, name=name,
        out_shape=jax.ShapeDtypeStruct((N_DEV, R, 128), F32),
        in_specs=[pl.BlockSpec(memory_space=pltpu.VMEM)],
        out_specs=pl.BlockSpec(memory_space=pltpu.VMEM),
        scratch_shapes=[pltpu.SemaphoreType.DMA((7,)), pltpu.SemaphoreType.DMA((7,)), pltpu.SemaphoreType.DMA],
        compiler_params=pltpu.CompilerParams(vmem_limit_bytes=VMEM_LIMIT),
    )(x)


BIG = (("ffn1_w_gu", "col", D, PW), ("ffn1_w_down", "row", FF, D), ("w_in", "col", D, PW),
       ("w_att_o", "col", 512, D), ("w_rec_o", "row", D, D), ("w_out", "row", D, D),
       ("ffn2_w_gu", "col", D, PW), ("ffn2_w_down", "row", FF, D))
NBIG = len(BIG)


def _shard_shape(kind, R, C):
    return (R, C // 4) if kind == "col" else (R // 4, C)


def _piece_shape(kind, R, C):
    r, c = _shard_shape(kind, R, C)
    return (r // 2, c)


def _piece(ref, kind, R, C, q, half):
    sr, sc = _shard_shape(kind, R, C)
    r0, c0 = (0, q * sc) if kind == "col" else (q * sr, 0)
    if half is None:
        return ref.at[pl.ds(r0, sr), pl.ds(c0, sc)]
    return ref.at[pl.ds(pl.multiple_of(r0 + half * (sr // 2), 16), sr // 2), pl.ds(c0, sc)]


def _half(ref, half):
    rows = ref.shape[0] // 2
    return ref.at[pl.ds(pl.multiple_of(half * rows, 16), rows), :]


def ag_weights(shards, name):
    def body(*refs):
        sh, full = refs[:NBIG], refs[NBIG:2 * NBIG]
        send_sems, recv_sems, local_sems = refs[2 * NBIG:]
        mx, my, mc = _mesh_pos()
        p = 2 * mx + my
        sibling = (mx, my, 1 - mc)
        chips = _other_chips(mx, my)

        def remote(w, k, src, dst, to):
            return pltpu.make_async_remote_copy(src_ref=src, dst_ref=dst, send_sem=send_sems.at[w, k],
                                                recv_sem=recv_sems.at[w, k], device_id=to, device_id_type=MESH)

        for q in range(4):
            @pl.when(p == q)
            def _(q=q):
                for w, (_, kind, R, C) in enumerate(BIG):
                    pltpu.make_async_copy(sh[w], _piece(full[w], kind, R, C, q, None), local_sems.at[w]).start()
                    for j, chip in enumerate(chips):
                        remote(w, j, _half(sh[w], mc), _piece(full[w], kind, R, C, q, mc), (*chip, mc)).start()

        for j, chip in enumerate(chips):
            qj = 2 * chip[0] + chip[1]
            for q in range(4):
                @pl.when(qj == q)
                def _(q=q, j=j):
                    for w, (_, kind, R, C) in enumerate(BIG):
                        got = _piece(full[w], kind, R, C, q, mc)
                        remote(w, j, got, got, sibling).wait_recv()
                        remote(w, 3 + j, got, got, sibling).start()

        for w, (_, kind, R, C) in enumerate(BIG):
            any_piece = _piece(full[w], kind, R, C, 0, 0)
            for j in range(3):
                remote(w, 3 + j, any_piece, any_piece, sibling).wait_recv()
            for k in range(6):
                remote(w, k, any_piece, any_piece, sibling).wait_send()
            pltpu.make_async_copy(sh[w], _piece(full[w], kind, R, C, 0, None), local_sems.at[w]).wait()

    return pl.pallas_call(
        body, name=name,
        out_shape=[jax.ShapeDtypeStruct((R, C), BF16) for (_, _, R, C) in BIG],
        in_specs=[ANY] * NBIG, out_specs=[ANY] * NBIG,
        scratch_shapes=[pltpu.SemaphoreType.DMA((NBIG, 6)), pltpu.SemaphoreType.DMA((NBIG, 6)),
                        pltpu.SemaphoreType.DMA((NBIG,))],
    )(*shards)


def rs_pair(grads, name):
    def body(*refs):
        g, got, own = refs[:NBIG], refs[NBIG:2 * NBIG], refs[2 * NBIG:3 * NBIG]
        send_sems, recv_sems, local_sems = refs[3 * NBIG:]
        mx, my, mc = _mesh_pos()
        sibling = (mx, my, 1 - mc)
        cps = []
        for w, (_, kind, R, C) in enumerate(BIG):
            for q in range(4):
                cps.append(pltpu.make_async_remote_copy(
                    src_ref=_piece(g[w], kind, R, C, q, 1 - mc), dst_ref=got[w].at[q],
                    send_sem=send_sems.at[w, q], recv_sem=recv_sems.at[w, q], device_id=sibling, device_id_type=MESH))
                cps.append(pltpu.make_async_copy(_piece(g[w], kind, R, C, q, mc), own[w].at[q], local_sems.at[w, q]))
        for cp in cps:
            cp.start()
        for cp in cps:
            cp.wait()

    pieces = [jax.ShapeDtypeStruct((4,) + _piece_shape(kind, R, C), BF16) for (_, kind, R, C) in BIG]
    outs = pl.pallas_call(
        body, name=name, out_shape=pieces + pieces,
        in_specs=[ANY] * NBIG, out_specs=[ANY] * (2 * NBIG),
        scratch_shapes=[pltpu.SemaphoreType.DMA((NBIG, 4)), pltpu.SemaphoreType.DMA((NBIG, 4)),
                        pltpu.SemaphoreType.DMA((NBIG, 4))],
    )(*grads)
    return outs[:NBIG], outs[NBIG:]


def rs_chips(pair_sums, name):
    def body(*refs):
        ps, rb = refs[:NBIG], refs[NBIG:2 * NBIG]
        send_sems, recv_sems, local_sems = refs[2 * NBIG:]
        mx, my, mc = _mesh_pos()
        p = 2 * mx + my
        cps = []
        for w in range(NBIG):
            cps.append(pltpu.make_async_copy(ps[w].at[p], rb[w].at[p], local_sems.at[w]))
            for j, chip in enumerate(_other_chips(mx, my)):
                cps.append(pltpu.make_async_remote_copy(
                    src_ref=ps[w].at[2 * chip[0] + chip[1]], dst_ref=rb[w].at[p],
                    send_sem=send_sems.at[w, j], recv_sem=recv_sems.at[w, j],
                    device_id=(*chip, mc), device_id_type=MESH))
        for cp in cps:
            cp.start()
        for cp in cps:
            cp.wait()

    return pl.pallas_call(
        body, name=name,
        out_shape=[jax.ShapeDtypeStruct(a.shape, a.dtype) for a in pair_sums],
        in_specs=[ANY] * NBIG, out_specs=[ANY] * NBIG,
        scratch_shapes=[pltpu.SemaphoreType.DMA((NBIG, 3)), pltpu.SemaphoreType.DMA((NBIG, 3)),
                        pltpu.SemaphoreType.DMA((NBIG,))],
    )(*pair_sums)


def rs_share(halves, name):
    def body(*refs):
        hv, fin = refs[:NBIG], refs[NBIG:2 * NBIG]
        send_sems, recv_sems, local_sems = refs[2 * NBIG:]
        mx, my, mc = _mesh_pos()
        cps = []
        for w in range(NBIG):
            cps.append(pltpu.make_async_copy(hv[w], fin[w].at[mc], local_sems.at[w]))
            cps.append(pltpu.make_async_remote_copy(
                src_ref=hv[w], dst_ref=fin[w].at[mc], send_sem=send_sems.at[w], recv_sem=recv_sems.at[w],
                device_id=(mx, my, 1 - mc), device_id_type=MESH))
        for cp in cps:
            cp.start()
        for cp in cps:
            cp.wait()

    return pl.pallas_call(
        body, name=name,
        out_shape=[jax.ShapeDtypeStruct((2,) + a.shape, a.dtype) for a in halves],
        in_specs=[ANY] * NBIG, out_specs=[ANY] * NBIG,
        scratch_shapes=[pltpu.SemaphoreType.DMA((NBIG,)), pltpu.SemaphoreType.DMA((NBIG,)),
                        pltpu.SemaphoreType.DMA((NBIG,))],
    )(*halves)


def _pack(parts, rows):
    flat = []
    for a in parts:
        a = jnp.ravel(a).astype(F32)
        flat.append(jnp.pad(a, (0, (-a.shape[0]) % 128)))
    v = jnp.concatenate(flat)
    return jnp.pad(v, (0, rows * 128 - v.shape[0])).reshape(rows, 128)


def _unpack(block, shapes):
    lead = block.shape[:-2]
    v = block.reshape(lead + (-1,))
    out, off = [], 0
    for shp in shapes:
        n = int(np.prod(shp))
        out.append(v[..., off:off + n].reshape(lead + tuple(shp)))
        off += n + (-n) % 128
    return out


def _block_diag4(w):
    w4 = w.reshape(4, 4, 64, 64)
    eye = jnp.eye(4, dtype=w.dtype)
    return (w4[:, :, :, None, :] * eye[None, :, None, :, None]).reshape(4, 256, 256)


def _diag_blocks(bd):
    b5 = bd.reshape(4, 4, 64, 4, 64)
    return jnp.stack([b5[:, i, :, i, :] for i in range(4)], axis=1).reshape(16, 64, 64)


def _bias_window(rel_bias):
    m = np.arange(767)
    tv = rel_bias[:, np.clip(639 - m, -128, 128) + 128]
    win = jnp.stack([tv[:, 127 - r:127 - r + WIN] for r in range(128)], axis=1)
    qh = np.arange(128)[:, None] // CHUNK
    kc = np.arange(WIN)[None, :] // CHUNK
    valid = (kc >= qh) & (kc <= qh + 8)
    return jnp.where(jnp.asarray(valid)[None], win, NEG)


SMALL = ("b_ada", "norm_pre", "norm_post", "rel_bias", "conv_w", "conv_b", "lru_wa", "lru_ba", "lru_wx",
         "lru_bx", "lru_lambda")
WEIGHTS = ("w_ada", "b_ada", "norm_pre", "norm_post", "ffn1_w_gu", "ffn1_w_down", "w_in", "rel_bias", "conv_w",
           "conv_b", "lru_wa", "lru_ba", "lru_wx", "lru_bx", "lru_lambda", "w_att_o", "w_rec_o", "w_out",
           "ffn2_w_gu", "ffn2_w_down")


def kernel(x, c, w_ada, b_ada, norm_pre, norm_post, ffn1_w_gu, ffn1_w_down, w_in, rel_bias, conv_w, conv_b, lru_wa, lru_ba, lru_wx, lru_bx, lru_lambda, w_att_o, w_rec_o, w_out, ffn2_w_gu, ffn2_w_down, loss_target, m_w_ada, m_b_ada, m_norm_pre, m_norm_post, m_ffn1_w_gu, m_ffn1_w_down, m_w_in, m_rel_bias, m_conv_w, m_conv_b, m_lru_wa, m_lru_ba, m_lru_wx, m_lru_bx, m_lru_lambda, m_w_att_o, m_w_rec_o, m_w_out, m_ffn2_w_gu, m_ffn2_w_down, v_w_ada, v_b_ada, v_norm_pre, v_norm_post, v_ffn1_w_gu, v_ffn1_w_down, v_w_in, v_rel_bias, v_conv_w, v_conv_b, v_lru_wa, v_lru_ba, v_lru_wx, v_lru_bx, v_lru_lambda, v_w_att_o, v_w_rec_o, v_w_out, v_ffn2_w_gu, v_ffn2_w_down):
    W = dict(w_ada=w_ada, b_ada=b_ada, norm_pre=norm_pre, norm_post=norm_post, ffn1_w_gu=ffn1_w_gu,
             ffn1_w_down=ffn1_w_down, w_in=w_in, rel_bias=rel_bias, conv_w=conv_w, conv_b=conv_b, lru_wa=lru_wa,
             lru_ba=lru_ba, lru_wx=lru_wx, lru_bx=lru_bx, lru_lambda=lru_lambda, w_att_o=w_att_o, w_rec_o=w_rec_o,
             w_out=w_out, ffn2_w_gu=ffn2_w_gu, ffn2_w_down=ffn2_w_down)
    M = dict(w_ada=m_w_ada, b_ada=m_b_ada, norm_pre=m_norm_pre, norm_post=m_norm_post, ffn1_w_gu=m_ffn1_w_gu,
             ffn1_w_down=m_ffn1_w_down, w_in=m_w_in, rel_bias=m_rel_bias, conv_w=m_conv_w, conv_b=m_conv_b,
             lru_wa=m_lru_wa, lru_ba=m_lru_ba, lru_wx=m_lru_wx, lru_bx=m_lru_bx, lru_lambda=m_lru_lambda,
             w_att_o=m_w_att_o, w_rec_o=m_w_rec_o, w_out=m_w_out, ffn2_w_gu=m_ffn2_w_gu, ffn2_w_down=m_ffn2_w_down)
    V = dict(w_ada=v_w_ada, b_ada=v_b_ada, norm_pre=v_norm_pre, norm_post=v_norm_post, ffn1_w_gu=v_ffn1_w_gu,
             ffn1_w_down=v_ffn1_w_down, w_in=v_w_in, rel_bias=v_rel_bias, conv_w=v_conv_w, conv_b=v_conv_b,
             lru_wa=v_lru_wa, lru_ba=v_lru_ba, lru_wx=v_lru_wx, lru_bx=v_lru_bx, lru_lambda=v_lru_lambda,
             w_att_o=v_w_att_o, w_rec_o=v_w_rec_o, w_out=v_w_out, ffn2_w_gu=v_ffn2_w_gu, ffn2_w_down=v_ffn2_w_down)
    mx, my, mc = _mesh_pos()
    p = 2 * mx + my
    e = 4 * mx + 2 * my + mc
    xs = x[0]

    g1 = ag_small(_pack([c, norm_pre, norm_post, conv_w], 32), "ag_small_params")
    c_all, npre4, npost4, cw4 = _unpack(g1, [(D,), (3, 256), (3, 256), (4, 256)])
    chipwise = lambda a: jnp.moveaxis(a[0::2], 0, 1).reshape(a.shape[1], D)
    npre, npost, conv_full = chipwise(npre4), chipwise(npost4), chipwise(cw4)

    b_cols = lax.dynamic_slice(b_ada, (0, p * 2304), (1, 2304))
    mod_cols = ada_fwd(c_all, w_ada[0], b_cols, "ada_fwd")
    g2 = ag_small(mod_cols.reshape(144, 128), "ag_mod")
    mod_all = jnp.moveaxis(g2[0::2].reshape(4, 8, 2304), 0, 1).reshape(8, 9 * D)
    mod = lax.dynamic_index_in_dim(mod_all, e, 0, keepdims=False).reshape(3, 3, D)
    zeros3 = jnp.zeros((3, D), F32)
    vecs = [jnp.concatenate([npre[k:k + 1], npost[k:k + 1], mod[k], zeros3], axis=0) for k in range(3)]

    full = ag_weights([W[n][0].astype(BF16) for (n, _, _, _) in BIG], "ag_weights")
    f1_gu, f1_dn, win, wao, wro, wout, f2_gu, f2_dn = full
    wa_bd = _block_diag4(lru_wa[0]).astype(BF16)
    wx_bd = _block_diag4(lru_wx[0]).astype(BF16)
    pvec = jnp.concatenate([conv_full, conv_b, lru_ba, lru_bx, lru_lambda], axis=0)
    bias = _bias_window(rel_bias[0])

    x1, h1, g1_, u1, a1, f1 = ffn_fwd(xs, vecs[0], f1_gu, f1_dn, 0.5, "ffn1_fwd")
    h2, qkv, rest = proj_fwd(x1, vecs[1], win, "proj_fwd")
    ao = attn_fwd(qkv, bias, "attn_fwd")
    hl, hg = lru_fwd(rest, pvec, wa_bd, wx_bd, "lru_fwd")
    x2, att, rec, mg, f2 = mix_out_fwd(x1, ao, hg, rest, vecs[1], wao, wro, wout, "mix_out_fwd")
    x3, h3, g3_, u3, a3, f3 = ffn_fwd(x2, vecs[2], f2_gu, f2_dn, 0.5, "ffn2_fwd")
    dy, lvec = loss_grad(x3, loss_target[0], "loss_grad")
    loss = lax.psum(lvec[0, 0], ("x", "y", "c"))

    G = {}
    dx2, df3, dgu3, va2 = ffn_bwd(dy, x2, f3, g3_, u3, vecs[2], f2_gu, f2_dn, 0.5, "ffn2_bwd")
    G["ffn2_w_gu"] = mm_tn(h3, dgu3, "dw_ffn2_gu", 512, 1408, 512)
    G["ffn2_w_down"] = mm_tn(a3, df3, "dw_ffn2_down", 1408, D, 512)
    df2, d_att, d_rec, dao, dhl, d3, va_out = mix_out_bwd(dx2, f2, att, rec, rest, hl, vecs[1], wao, wro, wout,
                                                          "mix_out_bwd")
    G["w_out"] = mm_tn(mg, df2, "dw_out", D, D, 512)
    G["w_att_o"] = mm_tn(ao, d_att, "dw_att_o", 512, D, 512)
    G["w_rec_o"] = mm_tn(hg, d_rec, "dw_rec_o", D, D, 512)
    dq, db, dkv = attn_bwd(qkv, dao, bias, "attn_bwd")
    dxr, v_lru, dwa_bd, dwx_bd = lru_bwd(dhl, hl, rest, pvec, wa_bd, wx_bd, "lru_bwd")
    dx1, va_in = proj_bwd(dq, dkv, dxr, d3, win, x1, dx2, vecs[1], "proj_bwd")
    gin = mm_tn(h2, dq, "dw_in_q", D, 512, 512, n_total=PW)
    gin = mm_tn(h2, dkv, "dw_in_kv", D, 512, 512, prev=gin, col_off=512, n_total=PW)
    gin = mm_tn(h2, dxr, "dw_in_xr", D, 512, 512, prev=gin, col_off=1536, n_total=PW)
    G["w_in"] = mm_tn(h2, d3, "dw_in_gates", D, 512, 512, prev=gin, col_off=2560, n_total=PW)
    dx0, df1, dgu1, va0 = ffn_bwd(dx1, xs, f1, g1_, u1, vecs[0], f1_gu, f1_dn, 0.5, "ffn1_bwd")
    G["ffn1_w_gu"] = mm_tn(h1, dgu1, "dw_ffn1_gu", 512, 1408, 512)
    G["ffn1_w_down"] = mm_tn(a1, df1, "dw_ffn1_down", 1408, D, 512)

    va1 = va_out + va_in
    vas = (va0, va1, va2)
    dmod = jnp.stack([v[2:5] for v in vas])
    part = {"b_ada": dmod, "norm_pre": jnp.stack([v[0] for v in vas]), "norm_post": jnp.stack([v[1] for v in vas]),
            "rel_bias": bias_grad(db, "bias_grad")[:, :257], "conv_w": v_lru[0:4], "conv_b": v_lru[4],
            "lru_wa": _diag_blocks(dwa_bd), "lru_ba": v_lru[5], "lru_wx": _diag_blocks(dwx_bd), "lru_bx": v_lru[6],
            "lru_lambda": v_lru[7]}
    full_shapes = {"b_ada": (9 * D,), "norm_pre": (3, D), "norm_post": (3, D), "rel_bias": (8, 257),
                   "conv_w": (4, D), "conv_b": (D,), "lru_wa": (16, 64, 64), "lru_ba": (D,),
                   "lru_wx": (16, 64, 64), "lru_bx": (D,), "lru_lambda": (D,)}
    g3 = ag_small(_pack([part[n] for n in SMALL], 1232), "ag_small_grads")
    red = dict(zip(SMALL, _unpack(sum_lead(g3, "sum_small_grads"), [full_shapes[n] for n in SMALL])))
    cols = lambda a: lax.dynamic_slice(a, (0, p * 256), (a.shape[0], 256))
    grads = {"b_ada": red["b_ada"][None], "norm_pre": cols(red["norm_pre"])[None],
             "norm_post": cols(red["norm_post"])[None], "rel_bias": red["rel_bias"][None],
             "conv_w": cols(red["conv_w"])[None], "conv_b": red["conv_b"][None], "lru_wa": red["lru_wa"][None],
             "lru_ba": red["lru_ba"][None], "lru_wx": red["lru_wx"][None], "lru_bx": red["lru_bx"][None],
             "lru_lambda": red["lru_lambda"][None]}

    dmod_all = g3[:, :72].reshape(8, 9 * D)
    dmod_cols = jnp.pad(lax.dynamic_slice(dmod_all, (0, p * 2304), (8, 2304)), ((0, 120), (0, 0)))
    c_all_t = jnp.pad(c_all.T, ((0, 0), (0, 120)))
    grads["w_ada"] = ada_bwd(c_all_t, dmod_cols, "ada_bwd")[None]

    from_sib, own = rs_pair([G[n] for (n, _, _, _) in BIG], "rs_pair")
    flat2 = lambda a: a.reshape(-1, a.shape[-1])
    pair = [add2(flat2(a), flat2(b), "rs_pair_sum_" + n).reshape(a.shape)
            for a, b, (n, _, _, _) in zip(own, from_sib, BIG)]
    by_chip = rs_chips(pair, "rs_chips")
    halves = [sum_lead(a, "rs_chip_sum_" + n) for a, (n, _, _, _) in zip(by_chip, BIG)]
    fin = rs_share(halves, "rs_share")
    for a, (n, kind, R, C) in zip(fin, BIG):
        grads[n] = a.reshape((1,) + _shard_shape(kind, R, C))

    delta, new_m, new_v = {}, {}, {}
    for n in ("w_ada",) + tuple(b[0] for b in BIG):
        shp = W[n].shape
        d_, m_, v_ = adamw(W[n][0], grads[n][0], M[n][0], V[n][0], "adamw_" + n)
        delta[n], new_m[n], new_v[n] = d_.reshape(shp), m_.reshape(shp), v_.reshape(shp)
    packed = [_pack([src[n] for n in SMALL], 1168) for src in (W, grads, M, V)]
    outs = adamw(*packed, "adamw_small")
    for dst, blk in zip((delta, new_m, new_v), outs):
        for n, a in zip(SMALL, _unpack(blk, [W[n].shape for n in SMALL])):
            dst[n] = a

    return (loss, dx0[None], *[grads[n] for n in WEIGHTS], *[delta[n] for n in WEIGHTS],
            *[new_m[n] for n in WEIGHTS], *[new_v[n] for n in WEIGHTS])
```

```python
import functools

import numpy as np
import jax
import jax.numpy as jnp
from jax import lax
from jax.experimental import pallas as pl
from jax.experimental.pallas import tpu as pltpu

F32 = jnp.float32
BF16 = jnp.bfloat16

D = 1024
FF = 2816
PW = 5632
HP = 128
CHUNK = 64
WIN = 640
TQ = 512
EPS = 1e-6
NEG = -1e30
LRU_C = 8.0
N_DEV = 8
VMEM_LIMIT = 50 * 1024 * 1024

ADAM_LR, ADAM_B1, ADAM_B2, ADAM_EPS, ADAM_WD, ADAM_STEP = 0.001, 0.9, 0.999, 1e-08, 0.01, 10

MESH = pl.DeviceIdType.MESH
ANY = pl.BlockSpec(memory_space=pl.ANY)


def _cp(*sem):
    return pltpu.CompilerParams(dimension_semantics=tuple(sem), vmem_limit_bytes=VMEM_LIMIT)


def _dot(a, b):
    return jnp.dot(a, b, preferred_element_type=F32)


def _dot_nt(a, b):
    return lax.dot_general(a, b, (((1,), (1,)), ((), ())), preferred_element_type=F32)


def _dot_tn(a, b):
    return lax.dot_general(a, b, (((0,), (0,)), ((), ())), preferred_element_type=F32)


def _mean(v):
    return jnp.mean(v, axis=-1, keepdims=True)


def _colsum(v):
    return jnp.sum(v, axis=0, keepdims=True)


def _sigmoid(v):
    return jax.nn.sigmoid(v)


def _expm1(v):
    small = v * (1.0 + v * 0.5 * (1.0 + v * (1.0 / 3.0) * (1.0 + v * 0.25 * (1.0 + v * 0.2 * (
        1.0 + v * (1.0 / 6.0) * (1.0 + v * (1.0 / 7.0)))))))
    return jnp.where(jnp.abs(v) < 0.25, small, jnp.exp(v) - 1.0)


_GK = 0.7978845608028654


def _gelu(v):
    t = jnp.tanh(_GK * (v + 0.044715 * v * v * v))
    return 0.5 * v * (1.0 + t)


def _gelu_grad(v):
    t = jnp.tanh(_GK * (v + 0.044715 * v * v * v))
    return 0.5 * (1.0 + t) + 0.5 * v * (1.0 - t * t) * _GK * (1.0 + 3.0 * 0.044715 * v * v)


def _pre_norm(xv, vec_ref):
    r = lax.rsqrt(_mean(xv * xv) + EPS)
    n = xv * r * vec_ref[0:1, :]
    return n * (1.0 + vec_ref[3:4, :]) + vec_ref[2:3, :]


def _pre_norm_bwd(dh, xv, dres, vec_ref, vacc_ref):
    r = lax.rsqrt(_mean(xv * xv) + EPS)
    xh = xv * r
    n = xh * vec_ref[0:1, :]
    vacc_ref[2:3, :] += _colsum(dh)
    vacc_ref[3:4, :] += _colsum(dh * n)
    dn = dh * (1.0 + vec_ref[3:4, :])
    vacc_ref[0:1, :] += _colsum(dn * xh)
    dxh = dn * vec_ref[0:1, :]
    return r * (dxh - xh * _mean(dxh * xh)) + dres


def _post_norm_bwd(dxo, fv, res, vec_ref, vacc_ref):
    rf = lax.rsqrt(_mean(fv * fv) + EPS)
    fh = fv * rf
    gp = vec_ref[1:2, :]
    vacc_ref[4:5, :] += _colsum(res * dxo * (fh * gp))
    dy = (res * vec_ref[4:5, :]) * dxo
    vacc_ref[1:2, :] += _colsum(dy * fh)
    dfn = dy * gp
    return rf * (dfn - fh * _mean(dfn * fh))


def ffn_fwd(x, vec, w_gu, w_dn, res, name, tm=512, tf=256):
    S = x.shape[0]
    tm = min(tm, S)
    nf = FF // tf

    def body(x_ref, vec_ref, wg_ref, wu_ref, wd_ref, xo_ref, h_ref, g_ref, u_ref, a_ref, f_ref, hs, acc):
        j = pl.program_id(1)

        @pl.when(j == 0)
        def _():
            h = _pre_norm(x_ref[...], vec_ref).astype(BF16)
            hs[...] = h
            h_ref[...] = h
            acc[...] = jnp.zeros_like(acc)

        h = hs[...]
        g = _dot(h, wg_ref[...])
        u = _dot(h, wu_ref[...])
        g_ref[...] = g
        u_ref[...] = u
        a = (g * _sigmoid(g) * u).astype(BF16)
        a_ref[...] = a
        acc[...] += _dot(a, wd_ref[...])

        @pl.when(j == nf - 1)
        def _():
            f = acc[...]
            f_ref[...] = f
            y = f * lax.rsqrt(_mean(f * f) + EPS) * vec_ref[1:2, :]
            xo_ref[...] = x_ref[...] + (res * vec_ref[4:5, :]) * y

    row = lambda i, j: (i, 0)
    return pl.pallas_call(
        body, name=name, grid=(S // tm, nf),
        in_specs=[pl.BlockSpec((tm, D), row), pl.BlockSpec((8, D), lambda i, j: (0, 0)),
                  pl.BlockSpec((D, tf), lambda i, j: (0, j)), pl.BlockSpec((D, tf), lambda i, j: (0, j + nf)),
                  pl.BlockSpec((tf, D), lambda i, j: (j, 0))],
        out_specs=[pl.BlockSpec((tm, D), row), pl.BlockSpec((tm, D), row),
                   pl.BlockSpec((tm, tf), lambda i, j: (i, j)), pl.BlockSpec((tm, tf), lambda i, j: (i, j)),
                   pl.BlockSpec((tm, tf), lambda i, j: (i, j)), pl.BlockSpec((tm, D), row)],
        out_shape=[jax.ShapeDtypeStruct((S, D), F32), jax.ShapeDtypeStruct((S, D), BF16),
                   jax.ShapeDtypeStruct((S, FF), F32), jax.ShapeDtypeStruct((S, FF), F32),
                   jax.ShapeDtypeStruct((S, FF), BF16), jax.ShapeDtypeStruct((S, D), F32)],
        scratch_shapes=[pltpu.VMEM((tm, D), BF16), pltpu.VMEM((tm, D), F32)],
        compiler_params=_cp("parallel", "arbitrary"),
    )(x, vec, w_gu, w_gu, w_dn)


def ffn_bwd(dxo, x, f, g, u, vec, w_gu, w_dn, res, name, tm=512, tf=256):
    S = x.shape[0]
    tm = min(tm, S)
    nf = FF // tf

    def body(dxo_ref, x_ref, f_ref, g_ref, u_ref, vec_ref, wg_ref, wu_ref, wd_ref,
             dx_ref, df_ref, dgu_ref, vacc_ref, dfs, acc):
        i, j = pl.program_id(0), pl.program_id(1)

        @pl.when((i == 0) & (j == 0))
        def _():
            vacc_ref[...] = jnp.zeros_like(vacc_ref)

        @pl.when(j == 0)
        def _():
            df = _post_norm_bwd(dxo_ref[...], f_ref[...], res, vec_ref, vacc_ref).astype(BF16)
            dfs[...] = df
            df_ref[...] = df
            acc[...] = jnp.zeros_like(acc)

        da = _dot_nt(dfs[...], wd_ref[...])
        gv, uv = g_ref[...], u_ref[...]
        sg = _sigmoid(gv)
        dg = (da * uv * (sg * (1.0 + gv * (1.0 - sg)))).astype(BF16)
        du = (da * (gv * sg)).astype(BF16)
        dgu_ref[0] = dg
        dgu_ref[1] = du
        acc[...] += _dot_nt(dg, wg_ref[...]) + _dot_nt(du, wu_ref[...])

        @pl.when(j == nf - 1)
        def _():
            dx_ref[...] = _pre_norm_bwd(acc[...], x_ref[...], dxo_ref[...], vec_ref, vacc_ref)

    row = lambda i, j: (i, 0)
    return pl.pallas_call(
        body, name=name, grid=(S // tm, nf),
        in_specs=[pl.BlockSpec((tm, D), row), pl.BlockSpec((tm, D), row), pl.BlockSpec((tm, D), row),
                  pl.BlockSpec((tm, tf), lambda i, j: (i, j)), pl.BlockSpec((tm, tf), lambda i, j: (i, j)),
                  pl.BlockSpec((8, D), lambda i, j: (0, 0)),
                  pl.BlockSpec((D, tf), lambda i, j: (0, j)), pl.BlockSpec((D, tf), lambda i, j: (0, j + nf)),
                  pl.BlockSpec((tf, D), lambda i, j: (j, 0))],
        out_specs=[pl.BlockSpec((tm, D), row), pl.BlockSpec((tm, D), row),
                   pl.BlockSpec((2, tm, tf), lambda i, j: (0, i, j)),
                   pl.BlockSpec((8, D), lambda i, j: (0, 0))],
        out_shape=[jax.ShapeDtypeStruct((S, D), F32), jax.ShapeDtypeStruct((S, D), BF16),
                   jax.ShapeDtypeStruct((2, S, FF), BF16), jax.ShapeDtypeStruct((8, D), F32)],
        scratch_shapes=[pltpu.VMEM((tm, D), BF16), pltpu.VMEM((tm, D), F32)],
        compiler_params=_cp("arbitrary", "arbitrary"),
    )(dxo, x, f, g, u, vec, w_gu, w_gu, w_dn)


def mm_tn(a, b, name, tm, tn, tk, out_dtype=BF16, prev=None, col_off=0, n_total=None):
    S, M = a.shape
    if b.ndim == 3:
        G, _, Nf = b.shape
    else:
        G, Nf = 1, b.shape[1]
    N = G * Nf
    n_total = N if n_total is None else n_total
    tk = min(tk, S)
    nbf = Nf // tn
    nk = S // tk
    ob = col_off // tn

    def body(*refs):
        a_ref, b_ref = refs[0], refs[1]
        o_ref, acc = refs[-2], refs[-1]
        k = pl.program_id(2)

        @pl.when(k == 0)
        def _():
            acc[...] = jnp.zeros_like(acc)

        acc[...] += _dot_tn(a_ref[...], b_ref[...])

        @pl.when(k == nk - 1)
        def _():
            o_ref[...] = acc[...].astype(out_dtype)

    if b.ndim == 3:
        b_spec = pl.BlockSpec((None, tk, tn), lambda i, j, k: (j // nbf, k, j % nbf))
    else:
        b_spec = pl.BlockSpec((tk, tn), lambda i, j, k: (k, j))
    in_specs = [pl.BlockSpec((tk, tm), lambda i, j, k: (k, i)), b_spec]
    args = [a, b]
    aliases = {}
    if prev is not None:
        in_specs.append(ANY)
        args.append(prev)
        aliases = {2: 0}
    return pl.pallas_call(
        body, name=name, grid=(M // tm, N // tn, nk),
        in_specs=in_specs,
        out_specs=pl.BlockSpec((tm, tn), lambda i, j, k: (i, j + ob)),
        out_shape=jax.ShapeDtypeStruct((M, n_total), out_dtype),
        scratch_shapes=[pltpu.VMEM((tm, tn), F32)],
        input_output_aliases=aliases,
        compiler_params=_cp("parallel", "parallel", "arbitrary"),
    )(*args)


def proj_fwd(x, vec, w_in, name, tm=512, tn=512):
    S = x.shape[0]
    tm = min(tm, S)
    nq = 1536 // tn

    def body(x_ref, vec_ref, w_ref, h_ref, qkv_ref, rest_ref, hs):
        j = pl.program_id(1)

        @pl.when(j == 0)
        def _():
            h = _pre_norm(x_ref[...], vec_ref).astype(BF16)
            hs[...] = h
            h_ref[...] = h

        r = _dot(hs[...], w_ref[...])

        @pl.when(j < nq)
        def _():
            qkv_ref[...] = r.astype(BF16)

        @pl.when(j >= nq)
        def _():
            rest_ref[...] = r

    row = lambda i, j: (i, 0)
    return pl.pallas_call(
        body, name=name, grid=(S // tm, PW // tn),
        in_specs=[pl.BlockSpec((tm, D), row), pl.BlockSpec((8, D), lambda i, j: (0, 0)),
                  pl.BlockSpec((D, tn), lambda i, j: (0, j))],
        out_specs=[pl.BlockSpec((tm, D), row),
                   pl.BlockSpec((tm, tn), lambda i, j: (i, jnp.minimum(j, nq - 1))),
                   pl.BlockSpec((tm, tn), lambda i, j: (i, jnp.maximum(j - nq, 0)))],
        out_shape=[jax.ShapeDtypeStruct((S, D), BF16), jax.ShapeDtypeStruct((S, 1536), BF16),
                   jax.ShapeDtypeStruct((S, 4096), F32)],
        scratch_shapes=[pltpu.VMEM((tm, D), BF16)],
        compiler_params=_cp("parallel", "arbitrary"),
    )(x, vec, w_in)


def proj_bwd(dq, dkv, dxr, d3, w_in, x, dxo, vec, name, tm=512, tk=512):
    S = x.shape[0]
    tm = min(tm, S)
    nk = PW // tk

    def body(dq_ref, dkv_ref, dxr_ref, d3_ref, w_ref, x_ref, dxo_ref, vec_ref, dx_ref, vacc_ref, acc):
        i, j = pl.program_id(0), pl.program_id(1)

        @pl.when((i == 0) & (j == 0))
        def _():
            vacc_ref[...] = jnp.zeros_like(vacc_ref)

        @pl.when(j == 0)
        def _():
            acc[...] = _dot_nt(dq_ref[...], w_ref[...])

        @pl.when((j >= 1) & (j < 3))
        def _():
            acc[...] += _dot_nt(dkv_ref[...], w_ref[...])

        @pl.when((j >= 3) & (j < 5))
        def _():
            acc[...] += _dot_nt(dxr_ref[...], w_ref[...])

        @pl.when(j >= 5)
        def _():
            acc[...] += _dot_nt(d3_ref[...], w_ref[...])

        @pl.when(j == nk - 1)
        def _():
            dx_ref[...] = _pre_norm_bwd(acc[...], x_ref[...], dxo_ref[...], vec_ref, vacc_ref)

    row = lambda i, j: (i, 0)
    return pl.pallas_call(
        body, name=name, grid=(S // tm, nk),
        in_specs=[pl.BlockSpec((None, tm, tk), lambda i, j: (0, i, 0)),
                  pl.BlockSpec((None, tm, tk), lambda i, j: (jnp.clip(j - 1, 0, 1), i, 0)),
                  pl.BlockSpec((tm, tk), lambda i, j: (i, jnp.clip(j - 3, 0, 1))),
                  pl.BlockSpec((None, tm, tk), lambda i, j: (jnp.clip(j - 5, 0, 5) // 2, i, jnp.clip(j - 5, 0, 5) % 2)),
                  pl.BlockSpec((D, tk), lambda i, j: (0, j)),
                  pl.BlockSpec((tm, D), row), pl.BlockSpec((tm, D), row),
                  pl.BlockSpec((8, D), lambda i, j: (0, 0))],
        out_specs=[pl.BlockSpec((tm, D), row), pl.BlockSpec((8, D), lambda i, j: (0, 0))],
        out_shape=[jax.ShapeDtypeStruct((S, D), F32), jax.ShapeDtypeStruct((8, D), F32)],
        scratch_shapes=[pltpu.VMEM((tm, D), F32)],
        compiler_params=_cp("arbitrary", "arbitrary"),
    )(dq, dkv, dxr, d3, w_in, x, dxo, vec)


def _attn_probs(qm, ka, bias_h, i, grp):
    s = _dot_nt(qm, ka) + bias_h
    col = lax.broadcasted_iota(jnp.int32, s.shape, 1)
    first_key = jnp.where(i == 0, 512 - 128 * grp, 0)
    s = jnp.where(col >= first_key, s, NEG)
    e = jnp.exp(s - jnp.max(s, axis=-1, keepdims=True))
    return e / jnp.sum(e, axis=-1, keepdims=True)


def attn_fwd(qkv, bias, name):
    S = qkv.shape[0]
    nb = S // TQ

    def body(q_ref, kp_ref, kc_ref, vp_ref, vc_ref, b_ref, o_ref, kw, vw):
        i = pl.program_id(1)
        kw[0:TQ, :] = kp_ref[...]
        kw[TQ:2 * TQ, :] = kc_ref[...]
        vw[0:TQ, :] = vp_ref[...]
        vw[TQ:2 * TQ, :] = vc_ref[...]
        lane = lax.broadcasted_iota(jnp.int32, (1, HP), 1)
        zero = jnp.zeros((), BF16)

        def group(a, carry):
            r0 = pl.multiple_of(a * 128, 128)
            qa = q_ref[pl.ds(r0, 128), :] * jnp.asarray(0.125, BF16)
            ka = kw[pl.ds(r0, WIN), :]
            va = vw[pl.ds(r0, WIN), :]
            o = jnp.zeros((128, HP), F32)
            for hh in range(2):
                msk = (lane < 64) if hh == 0 else (lane >= 64)
                p = _attn_probs(jnp.where(msk, qa, zero), ka, b_ref[hh], i, a)
                o += _dot(p.astype(BF16), jnp.where(msk, va, zero))
            o_ref[pl.ds(r0, 128), :] = o.astype(BF16)
            return carry

        lax.fori_loop(0, TQ // 128, group, 0)

    prev = lambda h, i: (jnp.maximum(i - 1, 0), 0)
    return pl.pallas_call(
        body, name=name, grid=(4, nb),
        in_specs=[pl.BlockSpec((TQ, HP), lambda h, i: (i, h)),
                  pl.BlockSpec((TQ, HP), lambda h, i: (jnp.maximum(i - 1, 0), 4 + h)),
                  pl.BlockSpec((TQ, HP), lambda h, i: (i, 4 + h)),
                  pl.BlockSpec((TQ, HP), lambda h, i: (jnp.maximum(i - 1, 0), 8 + h)),
                  pl.BlockSpec((TQ, HP), lambda h, i: (i, 8 + h)),
                  pl.BlockSpec((2, 128, WIN), lambda h, i: (h, 0, 0))],
        out_specs=pl.BlockSpec((TQ, HP), lambda h, i: (i, h)),
        out_shape=jax.ShapeDtypeStruct((S, 512), BF16),
        scratch_shapes=[pltpu.VMEM((2 * TQ, HP), BF16), pltpu.VMEM((2 * TQ, HP), BF16)],
        compiler_params=_cp("parallel", "arbitrary"),
    )(qkv, qkv, qkv, qkv, qkv, bias)


def attn_bwd(qkv, do, bias, name):
    S = qkv.shape[0]
    nb = S // TQ

    def body(q_ref, kp_ref, kc_ref, vp_ref, vc_ref, do_ref, b_ref, dqkv_ref, db_ref, dkv_ref, kw, vw, ak, av):
        i = pl.program_id(1)

        @pl.when(i == 0)
        def _():
            db_ref[...] = jnp.zeros_like(db_ref)
            ak[...] = jnp.zeros_like(ak)
            av[...] = jnp.zeros_like(av)

        @pl.when(i > 0)
        def _():
            ak[0:TQ, :] = ak[TQ:2 * TQ, :]
            av[0:TQ, :] = av[TQ:2 * TQ, :]
            ak[TQ:2 * TQ, :] = jnp.zeros((TQ, HP), F32)
            av[TQ:2 * TQ, :] = jnp.zeros((TQ, HP), F32)

        @pl.when(i < nb)
        def _():
            kw[0:TQ, :] = kp_ref[...]
            kw[TQ:2 * TQ, :] = kc_ref[...]
            vw[0:TQ, :] = vp_ref[...]
            vw[TQ:2 * TQ, :] = vc_ref[...]
            lane = lax.broadcasted_iota(jnp.int32, (1, HP), 1)
            zero = jnp.zeros((), BF16)

            def group(a, carry):
                r0 = pl.multiple_of(a * 128, 128)
                qa = q_ref[pl.ds(r0, 128), :] * jnp.asarray(0.125, BF16)
                doa = do_ref[pl.ds(r0, 128), :]
                ka = kw[pl.ds(r0, WIN), :]
                va = vw[pl.ds(r0, WIN), :]
                dq = jnp.zeros((128, HP), F32)
                for hh in range(2):
                    msk = (lane < 64) if hh == 0 else (lane >= 64)
                    qm = jnp.where(msk, qa, zero)
                    dom = jnp.where(msk, doa, zero)
                    p = _attn_probs(qm, ka, b_ref[hh], i, a)
                    dp = _dot_nt(dom, va)
                    ds = p * (dp - jnp.sum(p * dp, axis=-1, keepdims=True))
                    db_ref[hh] += ds
                    dsb = ds.astype(BF16)
                    dq += jnp.where(msk, _dot(dsb, ka), 0.0)
                    ak[pl.ds(r0, WIN), :] += _dot_tn(dsb, qm)
                    av[pl.ds(r0, WIN), :] += _dot_tn(p.astype(BF16), dom)
                dqkv_ref[0, pl.ds(r0, 128), :] = (dq * 0.125).astype(BF16)
                return carry

            lax.fori_loop(0, TQ // 128, group, 0)

        @pl.when(i > 0)
        def _():
            dkv_ref[0] = ak[0:TQ, :].astype(BF16)
            dkv_ref[1] = av[0:TQ, :].astype(BF16)

    cur = lambda i: jnp.minimum(i, nb - 1)
    prv = lambda i: jnp.clip(i - 1, 0, nb - 1)
    dq, db, dkv = pl.pallas_call(
        body, name=name, grid=(4, nb + 1),
        in_specs=[pl.BlockSpec((TQ, HP), lambda h, i: (cur(i), h)),
                  pl.BlockSpec((TQ, HP), lambda h, i: (prv(i), 4 + h)),
                  pl.BlockSpec((TQ, HP), lambda h, i: (cur(i), 4 + h)),
                  pl.BlockSpec((TQ, HP), lambda h, i: (prv(i), 8 + h)),
                  pl.BlockSpec((TQ, HP), lambda h, i: (cur(i), 8 + h)),
                  pl.BlockSpec((TQ, HP), lambda h, i: (cur(i), h)),
                  pl.BlockSpec((2, 128, WIN), lambda h, i: (h, 0, 0))],
        out_specs=[pl.BlockSpec((1, TQ, HP), lambda h, i: (0, cur(i), h)),
                   pl.BlockSpec((2, 128, WIN), lambda h, i: (h, 0, 0)),
                   pl.BlockSpec((2, TQ, HP), lambda h, i: (0, prv(i), h))],
        out_shape=[jax.ShapeDtypeStruct((1, S, 512), BF16), jax.ShapeDtypeStruct((8, 128, WIN), F32),
                   jax.ShapeDtypeStruct((2, S, 512), BF16)],
        scratch_shapes=[pltpu.VMEM((2 * TQ, HP), BF16), pltpu.VMEM((2 * TQ, HP), BF16),
                        pltpu.VMEM((2 * TQ, HP), F32), pltpu.VMEM((2 * TQ, HP), F32)],
        compiler_params=_cp("parallel", "arbitrary"),
    )(qkv, qkv, qkv, qkv, qkv, do, bias)
    return dq, db, dkv


def bias_grad(db, name):
    def body(db_ref, o_ref):
        r = lax.broadcasted_iota(jnp.int32, (128, 128), 0)
        c = lax.broadcasted_iota(jnp.int32, (128, 128), 1)
        flip = (r + c == 127).astype(BF16)
        lane = lax.broadcasted_iota(jnp.int32, (16, 384), 1)
        src = lax.broadcasted_iota(jnp.int32, (128, 384), 0)
        dst = lax.broadcasted_iota(jnp.int32, (128, 384), 1)

        def split_dot(v, m):
            hi = v.astype(BF16)
            r1 = v - hi.astype(F32)
            mid = r1.astype(BF16)
            lo = (r1 - mid.astype(F32)).astype(BF16)
            return _dot(hi, m) + _dot(mid, m) + _dot(lo, m)

        def diag_sums(w):
            y = pltpu.roll(split_dot(w, flip), 0, 1, stride=1, stride_axis=0)
            return jnp.broadcast_to(_colsum(y), (16, 128))

        w4 = db_ref[0, :, 512:640]
        w3 = db_ref[0, :, 384:512]
        far = jnp.sum(db_ref[0, :, 0:384]) + jnp.sum(jnp.where(r >= c, w3, 0.0))
        lo4 = diag_sums(jnp.where(r >= c, w4, 0.0))
        up4 = diag_sums(jnp.where(r < c, w4, 0.0))
        up3 = diag_sums(jnp.where(r < c, w3, 0.0))
        p_lo4 = (dst == 128 + (src + 1) % 128).astype(BF16)
        p_up4 = ((dst == src + 1) & (src < 127)).astype(BF16)
        p_up3 = ((dst == src + 129) & (src < 127)).astype(BF16)
        out = split_dot(lo4, p_lo4) + split_dot(up4, p_up4) + split_dot(up3, p_up3)
        o_ref[0] = out + jnp.where(lane == 256, far, 0.0)

    return pl.pallas_call(
        body, name=name, grid=(8,),
        in_specs=[pl.BlockSpec((1, 128, WIN), lambda h: (h, 0, 0))],
        out_specs=pl.BlockSpec((1, 16, 384), lambda h: (h, 0, 0)),
        out_shape=jax.ShapeDtypeStruct((8, 16, 384), F32),
        compiler_params=_cp("parallel"),
    )(db)[:, 0, :]


LT = 256
LC = 512


def _lru_gates(xs, pv_ref, wa_ref, wx_ref, tl):
    xc = (pv_ref[4:5, :] + pv_ref[3:4, :] * xs[pl.ds(8, tl), :] + pv_ref[2:3, :] * xs[pl.ds(7, tl), :]
          + pv_ref[1:2, :] * xs[pl.ds(6, tl), :] + pv_ref[0:1, :] * xs[pl.ds(5, tl), :])
    xcb = xc.astype(BF16)
    pa = jnp.concatenate([_dot(xcb[:, 0:256], wa_ref[0]), _dot(xcb[:, 256:512], wa_ref[1])], axis=1)
    px = jnp.concatenate([_dot(xcb[:, 0:256], wx_ref[0]), _dot(xcb[:, 256:512], wx_ref[1])], axis=1)
    r = _sigmoid(pa + pv_ref[5:6, :])
    ig = _sigmoid(px + pv_ref[6:7, :])
    z = -pv_ref[7:8, :]
    sp = jnp.maximum(z, 0.0) + jnp.log1p(jnp.exp(-jnp.abs(z)))
    log_a = (-LRU_C * r) * sp
    a = jnp.exp(log_a)
    mult = jnp.sqrt(-_expm1(2.0 * log_a))
    return xc, xcb, r, ig, sp, a, mult


def lru_fwd(rest, pvec, wa, wx, name):
    S = rest.shape[0]
    tl = min(LT, S)
    nt = S // tl

    def body(xr_ref, halo_ref, yr_ref, pv_ref, wa_ref, wx_ref, h_ref, hg_ref, xs, a_s, u_s, h_s, carry):
        ti = pl.program_id(1)

        @pl.when(ti == 0)
        def _():
            carry[...] = jnp.zeros_like(carry)

        xs[0:8, :] = jnp.where(ti > 0, halo_ref[...], 0.0)
        xs[pl.ds(8, tl), :] = xr_ref[...]
        xc, _, _, ig, _, a, mult = _lru_gates(xs, pv_ref, wa_ref, wx_ref, tl)
        a_s[...] = a
        u_s[...] = mult * (ig * xc)
        row = lax.broadcasted_iota(jnp.int32, (8, LC), 0)

        def blk(bi, c):
            o = pl.multiple_of(bi * 8, 8)
            av = a_s[pl.ds(o, 8), :]
            bv = u_s[pl.ds(o, 8), :]
            for d in (1, 2, 4):
                a_sh = pltpu.roll(av, d, 0)
                b_sh = pltpu.roll(bv, d, 0)
                m = row >= d
                bv = jnp.where(m, av * b_sh + bv, bv)
                av = jnp.where(m, av * a_sh, av)
            hv = bv + av * c
            h_s[pl.ds(o, 8), :] = hv
            return hv[7:8, :]

        carry[...] = lax.fori_loop(0, tl // 8, blk, carry[...])
        h = h_s[...]
        h_ref[...] = h
        hg_ref[...] = (h * _gelu(yr_ref[...])).astype(BF16)

    hb = tl // 8
    return pl.pallas_call(
        body, name=name, grid=(2, nt),
        in_specs=[pl.BlockSpec((tl, LC), lambda c, t: (t, c)),
                  pl.BlockSpec((8, LC), lambda c, t: (jnp.maximum(t * hb - 1, 0), c)),
                  pl.BlockSpec((tl, LC), lambda c, t: (t, 2 + c)),
                  pl.BlockSpec((8, LC), lambda c, t: (0, c)),
                  pl.BlockSpec((2, 256, 256), lambda c, t: (c, 0, 0)),
                  pl.BlockSpec((2, 256, 256), lambda c, t: (c, 0, 0))],
        out_specs=[pl.BlockSpec((tl, LC), lambda c, t: (t, c)), pl.BlockSpec((tl, LC), lambda c, t: (t, c))],
        out_shape=[jax.ShapeDtypeStruct((S, D), F32), jax.ShapeDtypeStruct((S, D), BF16)],
        scratch_shapes=[pltpu.VMEM((tl + 8, LC), F32), pltpu.VMEM((tl, LC), F32), pltpu.VMEM((tl, LC), F32),
                        pltpu.VMEM((tl, LC), F32), pltpu.VMEM((1, LC), F32)],
        compiler_params=_cp("parallel", "arbitrary"),
    )(rest, rest, rest, pvec, wa, wx)


def lru_bwd(dh, h, rest, pvec, wa, wx, name):
    S = rest.shape[0]
    tl = min(LT, S)
    nt = S // tl

    def body(dh_ref, h_ref, hhalo_ref, xr_ref, xhalo_ref, pv_ref, wa_ref, wx_ref,
             dxr_ref, vacc_ref, dwa_ref, dwx_ref,
             xs, hs, a_s, ash_s, b_s, lam_s, dxe, anext, lnext, dxnext):
        ti = pl.program_id(1)
        tr = nt - 1 - ti

        @pl.when(ti == 0)
        def _():
            anext[...] = jnp.zeros_like(anext)
            lnext[...] = jnp.zeros_like(lnext)
            dxnext[...] = jnp.zeros_like(dxnext)
            vacc_ref[...] = jnp.zeros_like(vacc_ref)
            dwa_ref[...] = jnp.zeros_like(dwa_ref)
            dwx_ref[...] = jnp.zeros_like(dwx_ref)

        xs[0:8, :] = jnp.where(tr > 0, xhalo_ref[...], 0.0)
        xs[pl.ds(8, tl), :] = xr_ref[...]
        xc, xcb, r, ig, sp, a, mult = _lru_gates(xs, pv_ref, wa_ref, wx_ref, tl)

        a_s[pl.ds(0, tl), :] = a
        a_s[pl.ds(tl, 8), :] = jnp.broadcast_to(anext[...], (8, LC))
        ash_s[...] = a_s[pl.ds(1, tl), :]
        b_s[...] = dh_ref[...]
        row = lax.broadcasted_iota(jnp.int32, (8, LC), 0)

        def blk(k, c):
            o = pl.multiple_of((tl // 8 - 1 - k) * 8, 8)
            av = ash_s[pl.ds(o, 8), :]
            bv = b_s[pl.ds(o, 8), :]
            for d in (1, 2, 4):
                a_sh = pltpu.roll(av, 8 - d, 0)
                b_sh = pltpu.roll(bv, 8 - d, 0)
                m = row < 8 - d
                bv = jnp.where(m, bv + av * b_sh, bv)
                av = jnp.where(m, av * a_sh, av)
            lv = bv + av * c
            lam_s[pl.ds(o, 8), :] = lv
            return lv[0:1, :]

        lnext[...] = lax.fori_loop(0, tl // 8, blk, lnext[...])
        anext[...] = a[0:1, :]
        lam = lam_s[...]

        hs[0:8, :] = jnp.where(tr > 0, hhalo_ref[...], 0.0)
        hs[pl.ds(8, tl), :] = h_ref[...]
        d_a = lam * hs[pl.ds(7, tl), :]
        d_mult = lam * (ig * xc)
        d_ig = lam * mult * xc
        dxc = lam * mult * ig
        d_log_a = d_a * a - d_mult * (a * a) / mult
        d_r = d_log_a * (-LRU_C * sp)
        vacc_ref[7:8, :] += _colsum(d_log_a * (-LRU_C * r)) * (-_sigmoid(-pv_ref[7:8, :]))
        d_pa = d_r * r * (1.0 - r)
        d_px = d_ig * ig * (1.0 - ig)
        vacc_ref[5:6, :] += _colsum(d_pa)
        vacc_ref[6:7, :] += _colsum(d_px)
        dpa = d_pa.astype(BF16)
        dpx = d_px.astype(BF16)
        back = []
        for g in range(2):
            sl = slice(256 * g, 256 * g + 256)
            dwa_ref[g] += _dot_tn(xcb[:, sl], dpa[:, sl])
            dwx_ref[g] += _dot_tn(xcb[:, sl], dpx[:, sl])
            back.append(_dot_nt(dpa[:, sl], wa_ref[g]) + _dot_nt(dpx[:, sl], wx_ref[g]))
        dxc = dxc + jnp.concatenate(back, axis=1)
        vacc_ref[4:5, :] += _colsum(dxc)
        for k in range(4):
            vacc_ref[k:k + 1, :] += _colsum(dxc * xs[pl.ds(5 + k, tl), :])
        dxe[pl.ds(0, tl), :] = dxc
        dxe[pl.ds(tl, 8), :] = dxnext[...]
        dxr = (pv_ref[3:4, :] * dxc + pv_ref[2:3, :] * dxe[pl.ds(1, tl), :]
               + pv_ref[1:2, :] * dxe[pl.ds(2, tl), :] + pv_ref[0:1, :] * dxe[pl.ds(3, tl), :])
        dxr_ref[...] = dxr.astype(BF16)
        dxnext[...] = dxc[0:8, :]

    hb = tl // 8
    rev = lambda t: nt - 1 - t
    halo = lambda t: jnp.maximum(rev(t) * hb - 1, 0)
    big = lambda: pltpu.VMEM((tl + 8, LC), F32)
    til = lambda: pltpu.VMEM((tl, LC), F32)
    return pl.pallas_call(
        body, name=name, grid=(2, nt),
        in_specs=[pl.BlockSpec((tl, LC), lambda c, t: (rev(t), c)),
                  pl.BlockSpec((tl, LC), lambda c, t: (rev(t), c)),
                  pl.BlockSpec((8, LC), lambda c, t: (halo(t), c)),
                  pl.BlockSpec((tl, LC), lambda c, t: (rev(t), c)),
                  pl.BlockSpec((8, LC), lambda c, t: (halo(t), c)),
                  pl.BlockSpec((8, LC), lambda c, t: (0, c)),
                  pl.BlockSpec((2, 256, 256), lambda c, t: (c, 0, 0)),
                  pl.BlockSpec((2, 256, 256), lambda c, t: (c, 0, 0))],
        out_specs=[pl.BlockSpec((tl, LC), lambda c, t: (rev(t), c)),
                   pl.BlockSpec((8, LC), lambda c, t: (0, c)),
                   pl.BlockSpec((2, 256, 256), lambda c, t: (c, 0, 0)),
                   pl.BlockSpec((2, 256, 256), lambda c, t: (c, 0, 0))],
        out_shape=[jax.ShapeDtypeStruct((S, D), BF16), jax.ShapeDtypeStruct((8, D), F32),
                   jax.ShapeDtypeStruct((4, 256, 256), F32), jax.ShapeDtypeStruct((4, 256, 256), F32)],
        scratch_shapes=[big(), big(), big(), til(), til(), til(), big(),
                        pltpu.VMEM((1, LC), F32), pltpu.VMEM((1, LC), F32), pltpu.VMEM((8, LC), F32)],
        compiler_params=_cp("parallel", "arbitrary"),
    )(dh, h, h, rest, rest, pvec, wa, wx)


def mix_out_fwd(x, ao, hg, rest, vec, w_att_o, w_rec_o, w_out, name, tm=256):
    S = x.shape[0]
    tm = min(tm, S)

    def body(x_ref, ao_ref, hg_ref, ga_ref, gr_ref, vec_ref, wa_ref, wr_ref, wo_ref,
             xo_ref, att_ref, rec_ref, mg_ref, f_ref):
        att = _dot(ao_ref[...], wa_ref[...])
        rec = _dot(hg_ref[...], wr_ref[...])
        att_ref[...] = att
        rec_ref[...] = rec
        mg = (_sigmoid(ga_ref[...]) * att + _sigmoid(gr_ref[...]) * rec).astype(BF16)
        mg_ref[...] = mg
        f = _dot(mg, wo_ref[...])
        f_ref[...] = f
        y = f * lax.rsqrt(_mean(f * f) + EPS) * vec_ref[1:2, :]
        xo_ref[...] = x_ref[...] + (1.0 * vec_ref[4:5, :]) * y

    row = lambda i: (i, 0)
    full = lambda r: pl.BlockSpec((r, D), lambda i: (0, 0))
    return pl.pallas_call(
        body, name=name, grid=(S // tm,),
        in_specs=[pl.BlockSpec((tm, D), row), pl.BlockSpec((tm, 512), row), pl.BlockSpec((tm, D), row),
                  pl.BlockSpec((tm, D), lambda i: (i, 2)), pl.BlockSpec((tm, D), lambda i: (i, 3)),
                  full(8), full(512), full(D), full(D)],
        out_specs=[pl.BlockSpec((tm, D), row)] * 5,
        out_shape=[jax.ShapeDtypeStruct((S, D), F32), jax.ShapeDtypeStruct((S, D), F32),
                   jax.ShapeDtypeStruct((S, D), F32), jax.ShapeDtypeStruct((S, D), BF16),
                   jax.ShapeDtypeStruct((S, D), F32)],
        compiler_params=_cp("parallel"),
    )(x, ao, hg, rest, rest, vec, w_att_o, w_rec_o, w_out)


def mix_out_bwd(dxo, f, att, rec, rest, h, vec, w_att_o, w_rec_o, w_out, name, tm=256):
    S = dxo.shape[0]
    tm = min(tm, S)

    def body(dxo_ref, f_ref, att_ref, rec_ref, yr_ref, ga_ref, gr_ref, h_ref, vec_ref, wa_ref, wr_ref, wo_ref,
             df_ref, da_ref, dr_ref, dao_ref, dh_ref, d3_ref, vacc_ref):
        @pl.when(pl.program_id(0) == 0)
        def _():
            vacc_ref[...] = jnp.zeros_like(vacc_ref)

        df = _post_norm_bwd(dxo_ref[...], f_ref[...], 1.0, vec_ref, vacc_ref).astype(BF16)
        df_ref[...] = df
        dm = _dot_nt(df, wo_ref[...])
        sa = _sigmoid(ga_ref[...])
        sr = _sigmoid(gr_ref[...])
        d_att = (dm * sa).astype(BF16)
        d_rec = (dm * sr).astype(BF16)
        da_ref[...] = d_att
        dr_ref[...] = d_rec
        d3_ref[1] = (dm * att_ref[...] * (sa * (1.0 - sa))).astype(BF16)
        d3_ref[2] = (dm * rec_ref[...] * (sr * (1.0 - sr))).astype(BF16)
        dao_ref[...] = _dot_nt(d_att, wa_ref[...]).astype(BF16)
        d_hg = _dot_nt(d_rec, wr_ref[...])
        yr = yr_ref[...]
        dh_ref[...] = d_hg * _gelu(yr)
        d3_ref[0] = (d_hg * h_ref[...] * _gelu_grad(yr)).astype(BF16)

    row = lambda i: (i, 0)
    full = lambda r: pl.BlockSpec((r, D), lambda i: (0, 0))
    return pl.pallas_call(
        body, name=name, grid=(S // tm,),
        in_specs=[pl.BlockSpec((tm, D), row)] * 4
        + [pl.BlockSpec((tm, D), lambda i: (i, 1)), pl.BlockSpec((tm, D), lambda i: (i, 2)),
           pl.BlockSpec((tm, D), lambda i: (i, 3)), pl.BlockSpec((tm, D), row),
           full(8), full(512), full(D), full(D)],
        out_specs=[pl.BlockSpec((tm, D), row)] * 3
        + [pl.BlockSpec((tm, 512), row), pl.BlockSpec((tm, D), row),
           pl.BlockSpec((3, tm, D), lambda i: (0, i, 0)), pl.BlockSpec((8, D), lambda i: (0, 0))],
        out_shape=[jax.ShapeDtypeStruct((S, D), BF16)] * 3
        + [jax.ShapeDtypeStruct((S, 512), BF16), jax.ShapeDtypeStruct((S, D), F32),
           jax.ShapeDtypeStruct((3, S, D), BF16), jax.ShapeDtypeStruct((8, D), F32)],
        compiler_params=_cp("arbitrary"),
    )(dxo, f, att, rec, rest, rest, rest, h, vec, w_att_o, w_rec_o, w_out)


def loss_grad(y, tgt, name, tm=512):
    S = y.shape[0]
    tm = min(tm, S)
    nt = S // tm

    def body(y_ref, t_ref, dy_ref, l_ref, acc):
        i = pl.program_id(0)

        @pl.when(i == 0)
        def _():
            acc[...] = jnp.zeros_like(acc)

        d = y_ref[...] - t_ref[...]
        dy_ref[...] = d * (1.0 / D)
        acc[...] += _colsum(d * d)

        @pl.when(i == nt - 1)
        def _():
            l_ref[...] = jnp.broadcast_to(0.5 * jnp.sum(acc[...]) * (1.0 / D), (8, 128))

    return pl.pallas_call(
        body, name=name, grid=(nt,),
        in_specs=[pl.BlockSpec((tm, D), lambda i: (i, 0))] * 2,
        out_specs=[pl.BlockSpec((tm, D), lambda i: (i, 0)), pl.BlockSpec((8, 128), lambda i: (0, 0))],
        out_shape=[jax.ShapeDtypeStruct((S, D), F32), jax.ShapeDtypeStruct((8, 128), F32)],
        scratch_shapes=[pltpu.VMEM((1, D), F32)],
        compiler_params=_cp("arbitrary"),
    )(y, tgt)


def ada_fwd(c_all, w_ada, b_ada, name, tn=768):
    n = w_ada.shape[1]

    def body(c_ref, w_ref, b_ref, o_ref):
        cv = c_ref[...]
        ca = (cv * _sigmoid(cv)).astype(BF16)
        o_ref[...] = _dot(ca, w_ref[...].astype(BF16)) + b_ref[...]

    return pl.pallas_call(
        body, name=name, grid=(n // tn,),
        in_specs=[pl.BlockSpec((8, D), lambda j: (0, 0)), pl.BlockSpec((D, tn), lambda j: (0, j)),
                  pl.BlockSpec((1, tn), lambda j: (0, j))],
        out_specs=pl.BlockSpec((8, tn), lambda j: (0, j)),
        out_shape=jax.ShapeDtypeStruct((8, n), F32),
        compiler_params=_cp("parallel"),
    )(c_all, w_ada, b_ada)


def ada_bwd(c_all_t, dmod, name, tn=768):
    n = dmod.shape[1]

    def body(c_ref, d_ref, o_ref):
        cv = c_ref[...]
        ca = (cv * _sigmoid(cv)).astype(BF16)
        o_ref[...] = _dot(ca, d_ref[...].astype(BF16))

    return pl.pallas_call(
        body, name=name, grid=(n // tn,),
        in_specs=[pl.BlockSpec((D, 128), lambda j: (0, 0)), pl.BlockSpec((128, tn), lambda j: (0, j))],
        out_specs=pl.BlockSpec((D, tn), lambda j: (0, j)),
        out_shape=jax.ShapeDtypeStruct((D, n), F32),
        compiler_params=_cp("parallel"),
    )(c_all_t, dmod)


def _row_tile(rows, cols, itemsize=4, budget=1536 * 1024):
    best = None
    for t in range(8, rows + 1, 8):
        if rows % t == 0 and t * cols * itemsize <= budget:
            best = t
    return rows if best is None else best


def sum_lead(parts, name, out_dtype=F32):
    n, R, C = parts.shape
    tr = _row_tile(R, C * n)

    def body(p_ref, o_ref):
        acc = p_ref[0].astype(F32)
        for k in range(1, n):
            acc = acc + p_ref[k].astype(F32)
        o_ref[...] = acc.astype(out_dtype)

    return pl.pallas_call(
        body, name=name, grid=(R // tr,),
        in_specs=[pl.BlockSpec((n, tr, C), lambda i: (0, i, 0))],
        out_specs=pl.BlockSpec((tr, C), lambda i: (i, 0)),
        out_shape=jax.ShapeDtypeStruct((R, C), out_dtype),
        compiler_params=_cp("parallel"),
    )(parts)


def adamw(w, g, m, v, name):
    R, C = w.shape
    tr = _row_tile(R, C * 7, budget=8 * 1024 * 1024)

    def body(w_ref, g_ref, m_ref, v_ref, d_ref, mo_ref, vo_ref):
        gv = g_ref[...]
        mn = ADAM_B1 * m_ref[...] + (1.0 - ADAM_B1) * gv
        vn = ADAM_B2 * v_ref[...] + (1.0 - ADAM_B2) * (gv * gv)
        m_hat = mn / (1.0 - ADAM_B1 ** ADAM_STEP)
        v_hat = vn / (1.0 - ADAM_B2 ** ADAM_STEP)
        d_ref[...] = -ADAM_LR * (m_hat / (jnp.sqrt(v_hat) + ADAM_EPS) + ADAM_WD * w_ref[...])
        mo_ref[...] = mn
        vo_ref[...] = vn

    spec = pl.BlockSpec((tr, C), lambda i: (i, 0))
    return pl.pallas_call(
        body, name=name, grid=(R // tr,),
        in_specs=[spec] * 4, out_specs=[spec] * 3,
        out_shape=[jax.ShapeDtypeStruct((R, C), F32)] * 3,
        compiler_params=_cp("parallel"),
    )(w, g, m, v)


def _mesh_pos():
    return lax.axis_index("x"), lax.axis_index("y"), lax.axis_index("c")


def _other_chips(mx, my):
    return [(1 - mx, my), (mx, 1 - my), (1 - mx, 1 - my)]


def ag_small(x, name):
    R = x.shape[0]

    def body(x_ref, out_ref, send_sems, recv_sems, local_sem):
        mx, my, mc = _mesh_pos()
        me, sibling = (mx, my, mc), (mx, my, 1 - mc)
        chips = _other_chips(mx, my)

        def slot(px, py, pc):
            return out_ref.at[4 * px + 2 * py + pc]

        def copy(k, block, to, src=None):
            return pltpu.make_async_remote_copy(
                src_ref=slot(*block) if src is None else src, dst_ref=slot(*block),
                send_sem=send_sems.at[k], recv_sem=recv_sems.at[k], device_id=to, device_id_type=MESH)

        mine = pltpu.make_async_copy(x_ref, slot(*me), local_sem)
        mine.start()
        first = [copy(0, me, sibling, src=x_ref)]
        first += [copy(1 + j, me, (*chip, mc), src=x_ref) for j, chip in enumerate(chips)]
        for cp in first:
            cp.start()
        passed = [copy(4 + j, (*chip, mc), sibling) for j, chip in enumerate(chips)]
        for j, chip in enumerate(chips):
            copy(1 + j, (*chip, mc), me).wait_recv()
            passed[j].start()
        copy(0, sibling, me).wait_recv()
        for j, chip in enumerate(chips):
            copy(4 + j, (*chip, 1 - mc), me).wait_recv()
        for cp in first + passed:
            cp.wait_send()
        mine.wait()

    return pl.pallas_call(
        body, name=name,
        out_shape=jax.ShapeDtypeStruct((N_DEV, R, 128), F32),
        in_specs=[pl.BlockSpec(memory_space=pltpu.VMEM)],
        out_specs=pl.BlockSpec(memory_space=pltpu.VMEM),
        scratch_shapes=[pltpu.SemaphoreType.DMA((7,)), pltpu.SemaphoreType.DMA((7,)), pltpu.SemaphoreType.DMA],
        compiler_params=pltpu.CompilerParams(vmem_limit_bytes=VMEM_LIMIT),
    )(x)


BIG = (("ffn1_w_gu", "col", D, PW), ("ffn1_w_down", "row", FF, D), ("w_in", "col", D, PW),
       ("w_att_o", "col", 512, D), ("w_rec_o", "row", D, D), ("w_out", "row", D, D),
       ("ffn2_w_gu", "col", D, PW), ("ffn2_w_down", "row", FF, D))
NBIG = len(BIG)


def _shard_shape(kind, R, C):
    return (R, C // 4) if kind == "col" else (R // 4, C)


def _piece(ref, kind, R, C, q, half):
    sr, sc = _shard_shape(kind, R, C)
    r0, c0 = (0, q * sc) if kind == "col" else (q * sr, 0)
    if half is None:
        return ref.at[pl.ds(r0, sr), pl.ds(c0, sc)]
    return ref.at[pl.ds(pl.multiple_of(r0 + half * (sr // 2), 16), sr // 2), pl.ds(c0, sc)]


def _half(ref, half):
    rows = ref.shape[0] // 2
    return ref.at[pl.ds(pl.multiple_of(half * rows, 16), rows), :]


def ag_weights(shards, name):
    def body(*refs):
        sh, full = refs[:NBIG], refs[NBIG:2 * NBIG]
        send_sems, recv_sems, local_sems = refs[2 * NBIG:]
        mx, my, mc = _mesh_pos()
        p = 2 * mx + my
        sibling = (mx, my, 1 - mc)
        chips = _other_chips(mx, my)

        def remote(w, k, src, dst, to):
            return pltpu.make_async_remote_copy(src_ref=src, dst_ref=dst, send_sem=send_sems.at[w, k],
                                                recv_sem=recv_sems.at[w, k], device_id=to, device_id_type=MESH)

        for q in range(4):
            @pl.when(p == q)
            def _(q=q):
                for w, (_, kind, R, C) in enumerate(BIG):
                    pltpu.make_async_copy(sh[w], _piece(full[w], kind, R, C, q, None), local_sems.at[w]).start()
                    for j, chip in enumerate(chips):
                        remote(w, j, _half(sh[w], mc), _piece(full[w], kind, R, C, q, mc), (*chip, mc)).start()

        for j, chip in enumerate(chips):
            qj = 2 * chip[0] + chip[1]
            for q in range(4):
                @pl.when(qj == q)
                def _(q=q, j=j):
                    for w, (_, kind, R, C) in enumerate(BIG):
                        got = _piece(full[w], kind, R, C, q, mc)
                        remote(w, j, got, got, sibling).wait_recv()
                        remote(w, 3 + j, got, got, sibling).start()

        for w, (_, kind, R, C) in enumerate(BIG):
            any_piece = _piece(full[w], kind, R, C, 0, 0)
            for j in range(3):
                remote(w, 3 + j, any_piece, any_piece, sibling).wait_recv()
            for k in range(6):
                remote(w, k, any_piece, any_piece, sibling).wait_send()
            pltpu.make_async_copy(sh[w], _piece(full[w], kind, R, C, 0, None), local_sems.at[w]).wait()

    return pl.pallas_call(
        body, name=name,
        out_shape=[jax.ShapeDtypeStruct((R, C), BF16) for (_, _, R, C) in BIG],
        in_specs=[ANY] * NBIG, out_specs=[ANY] * NBIG,
        scratch_shapes=[pltpu.SemaphoreType.DMA((NBIG, 6)), pltpu.SemaphoreType.DMA((NBIG, 6)),
                        pltpu.SemaphoreType.DMA((NBIG,))],
    )(*shards)


def _half_shape(kind, R, C):
    return (R // 2, C) if kind == "col" else (R, C // 2)


def _piece_shape(kind, R, C):
    return (R // 2, C // 4) if kind == "col" else (R // 4, C // 2)


def pair_push(g, kind, c_arr, name):
    R, C = g.shape
    hr, hc = _half_shape(kind, R, C)
    tr = _row_tile(hr, hc, itemsize=2, budget=1024 * 1024)
    nt = hr // tr

    def body(c_ref, g_ref, out_ref, stage, ssem, rsem):
        i = pl.program_id(0)
        slot = i % 2
        mx, my, mc = _mesh_pos()

        def push(s, t):
            return pltpu.make_async_remote_copy(
                src_ref=stage.at[s], dst_ref=out_ref.at[pl.ds(pl.multiple_of(t * tr, 16), tr)],
                send_sem=ssem.at[s], recv_sem=rsem, device_id=(mx, my, 1 - mc), device_id_type=MESH)

        @pl.when(i >= 2)
        def _():
            push(slot, 0).wait_send()

        stage[slot] = g_ref[...]
        push(slot, i).start()

        @pl.when(i == nt - 1)
        def _():
            push(slot, 0).wait_send()
            if nt >= 2:
                push(1 - slot, 0).wait_send()
            pltpu.make_async_remote_copy(src_ref=out_ref, dst_ref=out_ref, send_sem=ssem.at[0], recv_sem=rsem,
                                         device_id=(mx, my, 1 - mc), device_id_type=MESH).wait_recv()

    if kind == "col":
        g_spec = pl.BlockSpec((tr, hc), lambda i, c: ((1 - c[0]) * nt + i, 0))
    else:
        g_spec = pl.BlockSpec((tr, hc), lambda i, c: (i, 1 - c[0]))
    return pl.pallas_call(
        body, name=name,
        grid_spec=pltpu.PrefetchScalarGridSpec(
            num_scalar_prefetch=1, grid=(nt,), in_specs=[g_spec], out_specs=ANY,
            scratch_shapes=[pltpu.VMEM((2, tr, hc), BF16), pltpu.SemaphoreType.DMA((2,)), pltpu.SemaphoreType.DMA]),
        out_shape=jax.ShapeDtypeStruct((hr, hc), BF16),
        compiler_params=_cp("arbitrary"),
    )(c_arr, g)


def pair_add(g, got, kind, c_arr, name):
    R, C = g.shape
    pr, pc = _piece_shape(kind, R, C)
    tr = _row_tile(pr, pc, itemsize=2, budget=1024 * 1024)
    nt = pr // tr

    def body(c_ref, g_ref, got_ref, o_ref):
        o_ref[...] = (g_ref[...].astype(F32) + got_ref[...].astype(F32)).astype(BF16)

    if kind == "col":
        g_spec = pl.BlockSpec((tr, pc), lambda q, i, c: (c[0] * nt + i, q))
        got_spec = pl.BlockSpec((tr, pc), lambda q, i, c: (i, q))
    else:
        g_spec = pl.BlockSpec((tr, pc), lambda q, i, c: (q * nt + i, c[0]))
        got_spec = pl.BlockSpec((tr, pc), lambda q, i, c: (q * nt + i, 0))
    return pl.pallas_call(
        body, name=name,
        grid_spec=pltpu.PrefetchScalarGridSpec(
            num_scalar_prefetch=1, grid=(4, nt), in_specs=[g_spec, got_spec],
            out_specs=pl.BlockSpec((None, tr, pc), lambda q, i, c: (q, i, 0))),
        out_shape=jax.ShapeDtypeStruct((4, pr, pc), BF16),
        compiler_params=_cp("parallel", "parallel"),
    )(c_arr, g, got)


def rs_chips(pair_sums, name):
    def body(*refs):
        ps, rb = refs[:NBIG], refs[NBIG:2 * NBIG]
        send_sems, recv_sems, local_sems = refs[2 * NBIG:]
        mx, my, mc = _mesh_pos()
        p = 2 * mx + my
        cps = []
        for w in range(NBIG):
            cps.append(pltpu.make_async_copy(ps[w].at[p], rb[w].at[p], local_sems.at[w]))
            for j, chip in enumerate(_other_chips(mx, my)):
                cps.append(pltpu.make_async_remote_copy(
                    src_ref=ps[w].at[2 * chip[0] + chip[1]], dst_ref=rb[w].at[p],
                    send_sem=send_sems.at[w, j], recv_sem=recv_sems.at[w, j],
                    device_id=(*chip, mc), device_id_type=MESH))
        for cp in cps:
            cp.start()
        for cp in cps:
            cp.wait()

    return pl.pallas_call(
        body, name=name,
        out_shape=[jax.ShapeDtypeStruct(a.shape, a.dtype) for a in pair_sums],
        in_specs=[ANY] * NBIG, out_specs=[ANY] * NBIG,
        scratch_shapes=[pltpu.SemaphoreType.DMA((NBIG, 3)), pltpu.SemaphoreType.DMA((NBIG, 3)),
                        pltpu.SemaphoreType.DMA((NBIG,))],
    )(*pair_sums)


def sum_share(parts, kind, R, C, name):
    _, pr, pc = parts.shape
    sr, sc = _shard_shape(kind, R, C)
    tr = _row_tile(pr, pc * 4, budget=4 * 1024 * 1024)
    nt = pr // tr

    def body(p_ref, fin_ref, stage, lsem, ssem, rsem):
        i = pl.program_id(0)
        slot = i % 2
        mx, my, mc = _mesh_pos()

        def region(h, t):
            r0 = pl.multiple_of(t * tr, 8)
            if kind == "col":
                return fin_ref.at[pl.ds(pl.multiple_of(h * pr + r0, 8), tr)]
            return fin_ref.at[pl.ds(r0, tr), pl.ds(h * pc, pc)]

        def copies(s, h, t):
            return (pltpu.make_async_copy(stage.at[s], region(h, t), lsem.at[s]),
                    pltpu.make_async_remote_copy(src_ref=stage.at[s], dst_ref=region(h, t), send_sem=ssem.at[s],
                                                 recv_sem=rsem, device_id=(mx, my, 1 - mc), device_id_type=MESH))

        def wait_sent(s):
            loc, rem = copies(s, 0, 0)
            loc.wait()
            rem.wait_send()

        @pl.when(i >= 2)
        def _():
            wait_sent(slot)

        acc = p_ref[0].astype(F32)
        for k in range(1, 4):
            acc = acc + p_ref[k].astype(F32)
        stage[slot] = acc
        if kind == "col":
            for cp in copies(slot, mc, i):
                cp.start()
        else:
            for h in range(2):
                @pl.when(mc == h)
                def _(h=h):
                    for cp in copies(slot, h, i):
                        cp.start()

        @pl.when(i == nt - 1)
        def _():
            wait_sent(slot)
            if nt >= 2:
                wait_sent(1 - slot)
            half = fin_ref.at[pl.ds(0, pr), pl.ds(0, pc)]
            pltpu.make_async_remote_copy(src_ref=half, dst_ref=half, send_sem=ssem.at[0], recv_sem=rsem,
                                         device_id=(mx, my, 1 - mc), device_id_type=MESH).wait_recv()

    return pl.pallas_call(
        body, name=name, grid=(nt,),
        in_specs=[pl.BlockSpec((4, tr, pc), lambda i: (0, i, 0))],
        out_specs=ANY,
        out_shape=jax.ShapeDtypeStruct((sr, sc), F32),
        scratch_shapes=[pltpu.VMEM((2, tr, pc), F32), pltpu.SemaphoreType.DMA((2,)), pltpu.SemaphoreType.DMA((2,)),
                        pltpu.SemaphoreType.DMA],
        compiler_params=_cp("arbitrary"),
    )(parts)


def _pack(parts, rows):
    flat = []
    for a in parts:
        a = jnp.ravel(a).astype(F32)
        flat.append(jnp.pad(a, (0, (-a.shape[0]) % 128)))
    v = jnp.concatenate(flat)
    return jnp.pad(v, (0, rows * 128 - v.shape[0])).reshape(rows, 128)


def _unpack(block, shapes):
    lead = block.shape[:-2]
    v = block.reshape(lead + (-1,))
    out, off = [], 0
    for shp in shapes:
        n = int(np.prod(shp))
        out.append(v[..., off:off + n].reshape(lead + tuple(shp)))
        off += n + (-n) % 128
    return out


def _block_diag4(w):
    w4 = w.reshape(4, 4, 64, 64)
    eye = jnp.eye(4, dtype=w.dtype)
    return (w4[:, :, :, None, :] * eye[None, :, None, :, None]).reshape(4, 256, 256)


def _diag_blocks(bd):
    b5 = bd.reshape(4, 4, 64, 4, 64)
    return jnp.stack([b5[:, i, :, i, :] for i in range(4)], axis=1).reshape(16, 64, 64)


def _bias_window(rel_bias):
    m = np.arange(767)
    tv = rel_bias[:, np.clip(639 - m, -128, 128) + 128]
    win = jnp.stack([tv[:, 127 - r:127 - r + WIN] for r in range(128)], axis=1)
    qh = np.arange(128)[:, None] // CHUNK
    kc = np.arange(WIN)[None, :] // CHUNK
    valid = (kc >= qh) & (kc <= qh + 8)
    return jnp.where(jnp.asarray(valid)[None], win, NEG)


SMALL = ("b_ada", "norm_pre", "norm_post", "rel_bias", "conv_w", "conv_b", "lru_wa", "lru_ba", "lru_wx",
         "lru_bx", "lru_lambda")
WEIGHTS = ("w_ada", "b_ada", "norm_pre", "norm_post", "ffn1_w_gu", "ffn1_w_down", "w_in", "rel_bias", "conv_w",
           "conv_b", "lru_wa", "lru_ba", "lru_wx", "lru_bx", "lru_lambda", "w_att_o", "w_rec_o", "w_out",
           "ffn2_w_gu", "ffn2_w_down")


def kernel(x, c, w_ada, b_ada, norm_pre, norm_post, ffn1_w_gu, ffn1_w_down, w_in, rel_bias, conv_w, conv_b, lru_wa, lru_ba, lru_wx, lru_bx, lru_lambda, w_att_o, w_rec_o, w_out, ffn2_w_gu, ffn2_w_down, loss_target, m_w_ada, m_b_ada, m_norm_pre, m_norm_post, m_ffn1_w_gu, m_ffn1_w_down, m_w_in, m_rel_bias, m_conv_w, m_conv_b, m_lru_wa, m_lru_ba, m_lru_wx, m_lru_bx, m_lru_lambda, m_w_att_o, m_w_rec_o, m_w_out, m_ffn2_w_gu, m_ffn2_w_down, v_w_ada, v_b_ada, v_norm_pre, v_norm_post, v_ffn1_w_gu, v_ffn1_w_down, v_w_in, v_rel_bias, v_conv_w, v_conv_b, v_lru_wa, v_lru_ba, v_lru_wx, v_lru_bx, v_lru_lambda, v_w_att_o, v_w_rec_o, v_w_out, v_ffn2_w_gu, v_ffn2_w_down):
    W = dict(w_ada=w_ada, b_ada=b_ada, norm_pre=norm_pre, norm_post=norm_post, ffn1_w_gu=ffn1_w_gu,
             ffn1_w_down=ffn1_w_down, w_in=w_in, rel_bias=rel_bias, conv_w=conv_w, conv_b=conv_b, lru_wa=lru_wa,
             lru_ba=lru_ba, lru_wx=lru_wx, lru_bx=lru_bx, lru_lambda=lru_lambda, w_att_o=w_att_o, w_rec_o=w_rec_o,
             w_out=w_out, ffn2_w_gu=ffn2_w_gu, ffn2_w_down=ffn2_w_down)
    M = dict(w_ada=m_w_ada, b_ada=m_b_ada, norm_pre=m_norm_pre, norm_post=m_norm_post, ffn1_w_gu=m_ffn1_w_gu,
             ffn1_w_down=m_ffn1_w_down, w_in=m_w_in, rel_bias=m_rel_bias, conv_w=m_conv_w, conv_b=m_conv_b,
             lru_wa=m_lru_wa, lru_ba=m_lru_ba, lru_wx=m_lru_wx, lru_bx=m_lru_bx, lru_lambda=m_lru_lambda,
             w_att_o=m_w_att_o, w_rec_o=m_w_rec_o, w_out=m_w_out, ffn2_w_gu=m_ffn2_w_gu, ffn2_w_down=m_ffn2_w_down)
    V = dict(w_ada=v_w_ada, b_ada=v_b_ada, norm_pre=v_norm_pre, norm_post=v_norm_post, ffn1_w_gu=v_ffn1_w_gu,
             ffn1_w_down=v_ffn1_w_down, w_in=v_w_in, rel_bias=v_rel_bias, conv_w=v_conv_w, conv_b=v_conv_b,
             lru_wa=v_lru_wa, lru_ba=v_lru_ba, lru_wx=v_lru_wx, lru_bx=v_lru_bx, lru_lambda=v_lru_lambda,
             w_att_o=v_w_att_o, w_rec_o=v_w_rec_o, w_out=v_w_out, ffn2_w_gu=v_ffn2_w_gu, ffn2_w_down=v_ffn2_w_down)
    mx, my, mc = _mesh_pos()
    p = 2 * mx + my
    e = 4 * mx + 2 * my + mc
    xs = x[0]

    g1 = ag_small(_pack([c, norm_pre, norm_post, conv_w], 32), "ag_small_params")
    c_all, npre4, npost4, cw4 = _unpack(g1, [(D,), (3, 256), (3, 256), (4, 256)])
    chipwise = lambda a: jnp.moveaxis(a[0::2], 0, 1).reshape(a.shape[1], D)
    npre, npost, conv_full = chipwise(npre4), chipwise(npost4), chipwise(cw4)

    b_cols = lax.dynamic_slice(b_ada, (0, p * 2304), (1, 2304))
    mod_cols = ada_fwd(c_all, w_ada[0], b_cols, "ada_fwd")
    g2 = ag_small(mod_cols.reshape(144, 128), "ag_mod")
    mod_all = jnp.moveaxis(g2[0::2].reshape(4, 8, 2304), 0, 1).reshape(8, 9 * D)
    mod = lax.dynamic_index_in_dim(mod_all, e, 0, keepdims=False).reshape(3, 3, D)
    zeros3 = jnp.zeros((3, D), F32)
    vecs = [jnp.concatenate([npre[k:k + 1], npost[k:k + 1], mod[k], zeros3], axis=0) for k in range(3)]

    full = ag_weights([W[n][0].astype(BF16) for (n, _, _, _) in BIG], "ag_weights")
    f1_gu, f1_dn, win, wao, wro, wout, f2_gu, f2_dn = full
    wa_bd = _block_diag4(lru_wa[0]).astype(BF16)
    wx_bd = _block_diag4(lru_wx[0]).astype(BF16)
    pvec = jnp.concatenate([conv_full, conv_b, lru_ba, lru_bx, lru_lambda], axis=0)
    bias = _bias_window(rel_bias[0])

    x1, h1, g1_, u1, a1, f1 = ffn_fwd(xs, vecs[0], f1_gu, f1_dn, 0.5, "ffn1_fwd")
    h2, qkv, rest = proj_fwd(x1, vecs[1], win, "proj_fwd")
    ao = attn_fwd(qkv, bias, "attn_fwd")
    hl, hg = lru_fwd(rest, pvec, wa_bd, wx_bd, "lru_fwd")
    x2, att, rec, mg, f2 = mix_out_fwd(x1, ao, hg, rest, vecs[1], wao, wro, wout, "mix_out_fwd")
    x3, h3, g3_, u3, a3, f3 = ffn_fwd(x2, vecs[2], f2_gu, f2_dn, 0.5, "ffn2_fwd")
    dy, lvec = loss_grad(x3, loss_target[0], "loss_grad")
    loss = lax.psum(lvec[0, 0], ("x", "y", "c"))

    G = {}
    dx2, df3, dgu3, va2 = ffn_bwd(dy, x2, f3, g3_, u3, vecs[2], f2_gu, f2_dn, 0.5, "ffn2_bwd")
    G["ffn2_w_gu"] = mm_tn(h3, dgu3, "dw_ffn2_gu", 512, 1408, 512)
    G["ffn2_w_down"] = mm_tn(a3, df3, "dw_ffn2_down", 1408, D, 512)
    df2, d_att, d_rec, dao, dhl, d3, va_out = mix_out_bwd(dx2, f2, att, rec, rest, hl, vecs[1], wao, wro, wout,
                                                          "mix_out_bwd")
    G["w_out"] = mm_tn(mg, df2, "dw_out", D, D, 512)
    G["w_att_o"] = mm_tn(ao, d_att, "dw_att_o", 512, D, 512)
    G["w_rec_o"] = mm_tn(hg, d_rec, "dw_rec_o", D, D, 512)
    dq, db, dkv = attn_bwd(qkv, dao, bias, "attn_bwd")
    dxr, v_lru, dwa_bd, dwx_bd = lru_bwd(dhl, hl, rest, pvec, wa_bd, wx_bd, "lru_bwd")
    dx1, va_in = proj_bwd(dq, dkv, dxr, d3, win, x1, dx2, vecs[1], "proj_bwd")
    gin = mm_tn(h2, dq, "dw_in_q", D, 512, 512, n_total=PW)
    gin = mm_tn(h2, dkv, "dw_in_kv", D, 512, 512, prev=gin, col_off=512, n_total=PW)
    gin = mm_tn(h2, dxr, "dw_in_xr", D, 512, 512, prev=gin, col_off=1536, n_total=PW)
    G["w_in"] = mm_tn(h2, d3, "dw_in_gates", D, 512, 512, prev=gin, col_off=2560, n_total=PW)
    dx0, df1, dgu1, va0 = ffn_bwd(dx1, xs, f1, g1_, u1, vecs[0], f1_gu, f1_dn, 0.5, "ffn1_bwd")
    G["ffn1_w_gu"] = mm_tn(h1, dgu1, "dw_ffn1_gu", 512, 1408, 512)
    G["ffn1_w_down"] = mm_tn(a1, df1, "dw_ffn1_down", 1408, D, 512)

    va1 = va_out + va_in
    vas = (va0, va1, va2)
    dmod = jnp.stack([v[2:5] for v in vas])
    part = {"b_ada": dmod, "norm_pre": jnp.stack([v[0] for v in vas]), "norm_post": jnp.stack([v[1] for v in vas]),
            "rel_bias": bias_grad(db, "bias_grad")[:, :257], "conv_w": v_lru[0:4], "conv_b": v_lru[4],
            "lru_wa": _diag_blocks(dwa_bd), "lru_ba": v_lru[5], "lru_wx": _diag_blocks(dwx_bd), "lru_bx": v_lru[6],
            "lru_lambda": v_lru[7]}
    full_shapes = {"b_ada": (9 * D,), "norm_pre": (3, D), "norm_post": (3, D), "rel_bias": (8, 257),
                   "conv_w": (4, D), "conv_b": (D,), "lru_wa": (16, 64, 64), "lru_ba": (D,),
                   "lru_wx": (16, 64, 64), "lru_bx": (D,), "lru_lambda": (D,)}
    g3 = ag_small(_pack([part[n] for n in SMALL], 1232), "ag_small_grads")
    red = dict(zip(SMALL, _unpack(sum_lead(g3, "sum_small_grads"), [full_shapes[n] for n in SMALL])))
    cols = lambda a: lax.dynamic_slice(a, (0, p * 256), (a.shape[0], 256))
    grads = {"b_ada": red["b_ada"][None], "norm_pre": cols(red["norm_pre"])[None],
             "norm_post": cols(red["norm_post"])[None], "rel_bias": red["rel_bias"][None],
             "conv_w": cols(red["conv_w"])[None], "conv_b": red["conv_b"][None], "lru_wa": red["lru_wa"][None],
             "lru_ba": red["lru_ba"][None], "lru_wx": red["lru_wx"][None], "lru_bx": red["lru_bx"][None],
             "lru_lambda": red["lru_lambda"][None]}

    dmod_all = g3[:, :72].reshape(8, 9 * D)
    dmod_cols = jnp.pad(lax.dynamic_slice(dmod_all, (0, p * 2304), (8, 2304)), ((0, 120), (0, 0)))
    c_all_t = jnp.pad(c_all.T, ((0, 0), (0, 120)))
    grads["w_ada"] = ada_bwd(c_all_t, dmod_cols, "ada_bwd")[None]

    c_arr = jnp.reshape(mc, (1,)).astype(jnp.int32)
    pair = []
    for n, kind, _, _ in BIG:
        got = pair_push(G[n], kind, c_arr, "rs_push_" + n)
        pair.append(pair_add(G[n], got, kind, c_arr, "rs_pair_sum_" + n))
    by_chip = rs_chips(pair, "rs_chips")
    for a, (n, kind, R, C) in zip(by_chip, BIG):
        grads[n] = sum_share(a, kind, R, C, "rs_sum_share_" + n)[None]

    delta, new_m, new_v = {}, {}, {}
    for n in ("w_ada",) + tuple(b[0] for b in BIG):
        shp = W[n].shape
        d_, m_, v_ = adamw(W[n][0], grads[n][0], M[n][0], V[n][0], "adamw_" + n)
        delta[n], new_m[n], new_v[n] = d_.reshape(shp), m_.reshape(shp), v_.reshape(shp)
    packed = [_pack([src[n] for n in SMALL], 1168) for src in (W, grads, M, V)]
    outs = adamw(*packed, "adamw_small")
    for dst, blk in zip((delta, new_m, new_v), outs):
        for n, a in zip(SMALL, _unpack(blk, [W[n].shape for n in SMALL])):
            dst[n] = a

    return (loss, dx0[None], *[grads[n] for n in WEIGHTS], *[delta[n] for n in WEIGHTS],
            *[new_m[n] for n in WEIGHTS], *[new_v[n] for n in WEIGHTS])
```

```python
import functools

import numpy as np
import jax
import jax.numpy as jnp
from jax import lax
from jax.experimental import pallas as pl
from jax.experimental.pallas import tpu as pltpu

F32 = jnp.float32
BF16 = jnp.bfloat16

D = 1024
FF = 2816
PW = 5632
HP = 128
CHUNK = 64
WIN = 640
TQ = 512
EPS = 1e-6
NEG = -1e30
LRU_C = 8.0
N_DEV = 8
VMEM_LIMIT = 50 * 1024 * 1024

ADAM_LR, ADAM_B1, ADAM_B2, ADAM_EPS, ADAM_WD, ADAM_STEP = 0.001, 0.9, 0.999, 1e-08, 0.01, 10

MESH = pl.DeviceIdType.MESH
ANY = pl.BlockSpec(memory_space=pl.ANY)


def _cp(*sem):
    return pltpu.CompilerParams(dimension_semantics=tuple(sem), vmem_limit_bytes=VMEM_LIMIT)


def _dot(a, b):
    return jnp.dot(a, b, preferred_element_type=F32)


def _dot_nt(a, b):
    return lax.dot_general(a, b, (((1,), (1,)), ((), ())), preferred_element_type=F32)


def _dot_tn(a, b):
    return lax.dot_general(a, b, (((0,), (0,)), ((), ())), preferred_element_type=F32)


def _mean(v):
    return jnp.mean(v, axis=-1, keepdims=True)


def _colsum(v):
    return jnp.sum(v, axis=0, keepdims=True)


def _sigmoid(v):
    return jax.nn.sigmoid(v)


def _expm1(v):
    small = v * (1.0 + v * 0.5 * (1.0 + v * (1.0 / 3.0) * (1.0 + v * 0.25 * (1.0 + v * 0.2 * (
        1.0 + v * (1.0 / 6.0) * (1.0 + v * (1.0 / 7.0)))))))
    return jnp.where(jnp.abs(v) < 0.25, small, jnp.exp(v) - 1.0)


_GK = 0.7978845608028654


def _gelu(v):
    t = jnp.tanh(_GK * (v + 0.044715 * v * v * v))
    return 0.5 * v * (1.0 + t)


def _gelu_grad(v):
    t = jnp.tanh(_GK * (v + 0.044715 * v * v * v))
    return 0.5 * (1.0 + t) + 0.5 * v * (1.0 - t * t) * _GK * (1.0 + 3.0 * 0.044715 * v * v)


def _pre_norm(xv, vec_ref):
    r = lax.rsqrt(_mean(xv * xv) + EPS)
    n = xv * r * vec_ref[0:1, :]
    return n * (1.0 + vec_ref[3:4, :]) + vec_ref[2:3, :]


def _pre_norm_bwd(dh, xv, dres, vec_ref, vacc_ref):
    r = lax.rsqrt(_mean(xv * xv) + EPS)
    xh = xv * r
    n = xh * vec_ref[0:1, :]
    vacc_ref[2:3, :] += _colsum(dh)
    vacc_ref[3:4, :] += _colsum(dh * n)
    dn = dh * (1.0 + vec_ref[3:4, :])
    vacc_ref[0:1, :] += _colsum(dn * xh)
    dxh = dn * vec_ref[0:1, :]
    return r * (dxh - xh * _mean(dxh * xh)) + dres


def _post_norm_bwd(dxo, fv, res, vec_ref, vacc_ref):
    rf = lax.rsqrt(_mean(fv * fv) + EPS)
    fh = fv * rf
    gp = vec_ref[1:2, :]
    vacc_ref[4:5, :] += _colsum(res * dxo * (fh * gp))
    dy = (res * vec_ref[4:5, :]) * dxo
    vacc_ref[1:2, :] += _colsum(dy * fh)
    dfn = dy * gp
    return rf * (dfn - fh * _mean(dfn * fh))


def ffn_fwd(x, vec, w_gu, w_dn, res, name, tm=512, tf=256):
    S = x.shape[0]
    tm = min(tm, S)
    nf = FF // tf

    def body(x_ref, vec_ref, wg_ref, wu_ref, wd_ref, xo_ref, h_ref, g_ref, u_ref, a_ref, f_ref, hs, acc):
        j = pl.program_id(1)

        @pl.when(j == 0)
        def _():
            h = _pre_norm(x_ref[...], vec_ref).astype(BF16)
            hs[...] = h
            h_ref[...] = h
            acc[...] = jnp.zeros_like(acc)

        h = hs[...]
        g = _dot(h, wg_ref[...])
        u = _dot(h, wu_ref[...])
        g_ref[...] = g
        u_ref[...] = u
        a = (g * _sigmoid(g) * u).astype(BF16)
        a_ref[...] = a
        acc[...] += _dot(a, wd_ref[...])

        @pl.when(j == nf - 1)
        def _():
            f = acc[...]
            f_ref[...] = f
            y = f * lax.rsqrt(_mean(f * f) + EPS) * vec_ref[1:2, :]
            xo_ref[...] = x_ref[...] + (res * vec_ref[4:5, :]) * y

    row = lambda i, j: (i, 0)
    return pl.pallas_call(
        body, name=name, grid=(S // tm, nf),
        in_specs=[pl.BlockSpec((tm, D), row), pl.BlockSpec((8, D), lambda i, j: (0, 0)),
                  pl.BlockSpec((D, tf), lambda i, j: (0, j)), pl.BlockSpec((D, tf), lambda i, j: (0, j + nf)),
                  pl.BlockSpec((tf, D), lambda i, j: (j, 0))],
        out_specs=[pl.BlockSpec((tm, D), row), pl.BlockSpec((tm, D), row),
                   pl.BlockSpec((tm, tf), lambda i, j: (i, j)), pl.BlockSpec((tm, tf), lambda i, j: (i, j)),
                   pl.BlockSpec((tm, tf), lambda i, j: (i, j)), pl.BlockSpec((tm, D), row)],
        out_shape=[jax.ShapeDtypeStruct((S, D), F32), jax.ShapeDtypeStruct((S, D), BF16),
                   jax.ShapeDtypeStruct((S, FF), F32), jax.ShapeDtypeStruct((S, FF), F32),
                   jax.ShapeDtypeStruct((S, FF), BF16), jax.ShapeDtypeStruct((S, D), F32)],
        scratch_shapes=[pltpu.VMEM((tm, D), BF16), pltpu.VMEM((tm, D), F32)],
        compiler_params=_cp("parallel", "arbitrary"),
    )(x, vec, w_gu, w_gu, w_dn)


def ffn_bwd(dxo, x, f, g, u, vec, w_gu, w_dn, res, name, tm=512, tf=256):
    S = x.shape[0]
    tm = min(tm, S)
    nf = FF // tf

    def body(dxo_ref, x_ref, f_ref, g_ref, u_ref, vec_ref, wg_ref, wu_ref, wd_ref,
             dx_ref, df_ref, dgu_ref, vacc_ref, dfs, acc):
        i, j = pl.program_id(0), pl.program_id(1)

        @pl.when((i == 0) & (j == 0))
        def _():
            vacc_ref[...] = jnp.zeros_like(vacc_ref)

        @pl.when(j == 0)
        def _():
            df = _post_norm_bwd(dxo_ref[...], f_ref[...], res, vec_ref, vacc_ref).astype(BF16)
            dfs[...] = df
            df_ref[...] = df
            acc[...] = jnp.zeros_like(acc)

        da = _dot_nt(dfs[...], wd_ref[...])
        gv, uv = g_ref[...], u_ref[...]
        sg = _sigmoid(gv)
        dg = (da * uv * (sg * (1.0 + gv * (1.0 - sg)))).astype(BF16)
        du = (da * (gv * sg)).astype(BF16)
        dgu_ref[0] = dg
        dgu_ref[1] = du
        acc[...] += _dot_nt(dg, wg_ref[...]) + _dot_nt(du, wu_ref[...])

        @pl.when(j == nf - 1)
        def _():
            dx_ref[...] = _pre_norm_bwd(acc[...], x_ref[...], dxo_ref[...], vec_ref, vacc_ref)

    row = lambda i, j: (i, 0)
    return pl.pallas_call(
        body, name=name, grid=(S // tm, nf),
        in_specs=[pl.BlockSpec((tm, D), row), pl.BlockSpec((tm, D), row), pl.BlockSpec((tm, D), row),
                  pl.BlockSpec((tm, tf), lambda i, j: (i, j)), pl.BlockSpec((tm, tf), lambda i, j: (i, j)),
                  pl.BlockSpec((8, D), lambda i, j: (0, 0)),
                  pl.BlockSpec((D, tf), lambda i, j: (0, j)), pl.BlockSpec((D, tf), lambda i, j: (0, j + nf)),
                  pl.BlockSpec((tf, D), lambda i, j: (j, 0))],
        out_specs=[pl.BlockSpec((tm, D), row), pl.BlockSpec((tm, D), row),
                   pl.BlockSpec((2, tm, tf), lambda i, j: (0, i, j)),
                   pl.BlockSpec((8, D), lambda i, j: (0, 0))],
        out_shape=[jax.ShapeDtypeStruct((S, D), F32), jax.ShapeDtypeStruct((S, D), BF16),
                   jax.ShapeDtypeStruct((2, S, FF), BF16), jax.ShapeDtypeStruct((8, D), F32)],
        scratch_shapes=[pltpu.VMEM((tm, D), BF16), pltpu.VMEM((tm, D), F32)],
        compiler_params=_cp("arbitrary", "arbitrary"),
    )(dxo, x, f, g, u, vec, w_gu, w_gu, w_dn)


def mm_tn(a, b, name, tm, tn, tk, out_dtype=BF16, prev=None, col_off=0, n_total=None):
    S, M = a.shape
    if b.ndim == 3:
        G, _, Nf = b.shape
    else:
        G, Nf = 1, b.shape[1]
    N = G * Nf
    n_total = N if n_total is None else n_total
    tk = min(tk, S)
    nbf = Nf // tn
    nk = S // tk
    ob = col_off // tn

    def body(*refs):
        a_ref, b_ref = refs[0], refs[1]
        o_ref, acc = refs[-2], refs[-1]
        k = pl.program_id(2)

        @pl.when(k == 0)
        def _():
            acc[...] = jnp.zeros_like(acc)

        acc[...] += _dot_tn(a_ref[...], b_ref[...])

        @pl.when(k == nk - 1)
        def _():
            o_ref[...] = acc[...].astype(out_dtype)

    if b.ndim == 3:
        b_spec = pl.BlockSpec((None, tk, tn), lambda i, j, k: (j // nbf, k, j % nbf))
    else:
        b_spec = pl.BlockSpec((tk, tn), lambda i, j, k: (k, j))
    in_specs = [pl.BlockSpec((tk, tm), lambda i, j, k: (k, i)), b_spec]
    args = [a, b]
    aliases = {}
    if prev is not None:
        in_specs.append(ANY)
        args.append(prev)
        aliases = {2: 0}
    return pl.pallas_call(
        body, name=name, grid=(M // tm, N // tn, nk),
        in_specs=in_specs,
        out_specs=pl.BlockSpec((tm, tn), lambda i, j, k: (i, j + ob)),
        out_shape=jax.ShapeDtypeStruct((M, n_total), out_dtype),
        scratch_shapes=[pltpu.VMEM((tm, tn), F32)],
        input_output_aliases=aliases,
        compiler_params=_cp("parallel", "parallel", "arbitrary"),
    )(*args)


def proj_fwd(x, vec, w_in, name, tm=512, tn=512):
    S = x.shape[0]
    tm = min(tm, S)
    nq = 1536 // tn

    def body(x_ref, vec_ref, w_ref, h_ref, qkv_ref, rest_ref, hs):
        j = pl.program_id(1)

        @pl.when(j == 0)
        def _():
            h = _pre_norm(x_ref[...], vec_ref).astype(BF16)
            hs[...] = h
            h_ref[...] = h

        r = _dot(hs[...], w_ref[...])

        @pl.when(j < nq)
        def _():
            qkv_ref[...] = r.astype(BF16)

        @pl.when(j >= nq)
        def _():
            rest_ref[...] = r

    row = lambda i, j: (i, 0)
    return pl.pallas_call(
        body, name=name, grid=(S // tm, PW // tn),
        in_specs=[pl.BlockSpec((tm, D), row), pl.BlockSpec((8, D), lambda i, j: (0, 0)),
                  pl.BlockSpec((D, tn), lambda i, j: (0, j))],
        out_specs=[pl.BlockSpec((tm, D), row),
                   pl.BlockSpec((tm, tn), lambda i, j: (i, jnp.minimum(j, nq - 1))),
                   pl.BlockSpec((tm, tn), lambda i, j: (i, jnp.maximum(j - nq, 0)))],
        out_shape=[jax.ShapeDtypeStruct((S, D), BF16), jax.ShapeDtypeStruct((S, 1536), BF16),
                   jax.ShapeDtypeStruct((S, 4096), F32)],
        scratch_shapes=[pltpu.VMEM((tm, D), BF16)],
        compiler_params=_cp("parallel", "arbitrary"),
    )(x, vec, w_in)


def proj_bwd(dq, dkv, dxr, d3, w_in, x, dxo, vec, name, tm=512, tk=512):
    S = x.shape[0]
    tm = min(tm, S)
    nk = PW // tk

    def body(dq_ref, dkv_ref, dxr_ref, d3_ref, w_ref, x_ref, dxo_ref, vec_ref, dx_ref, vacc_ref, acc):
        i, j = pl.program_id(0), pl.program_id(1)

        @pl.when((i == 0) & (j == 0))
        def _():
            vacc_ref[...] = jnp.zeros_like(vacc_ref)

        @pl.when(j == 0)
        def _():
            acc[...] = _dot_nt(dq_ref[...], w_ref[...])

        @pl.when((j >= 1) & (j < 3))
        def _():
            acc[...] += _dot_nt(dkv_ref[...], w_ref[...])

        @pl.when((j >= 3) & (j < 5))
        def _():
            acc[...] += _dot_nt(dxr_ref[...], w_ref[...])

        @pl.when(j >= 5)
        def _():
            acc[...] += _dot_nt(d3_ref[...], w_ref[...])

        @pl.when(j == nk - 1)
        def _():
            dx_ref[...] = _pre_norm_bwd(acc[...], x_ref[...], dxo_ref[...], vec_ref, vacc_ref)

    row = lambda i, j: (i, 0)
    return pl.pallas_call(
        body, name=name, grid=(S // tm, nk),
        in_specs=[pl.BlockSpec((None, tm, tk), lambda i, j: (0, i, 0)),
                  pl.BlockSpec((None, tm, tk), lambda i, j: (jnp.clip(j - 1, 0, 1), i, 0)),
                  pl.BlockSpec((tm, tk), lambda i, j: (i, jnp.clip(j - 3, 0, 1))),
                  pl.BlockSpec((None, tm, tk), lambda i, j: (jnp.clip(j - 5, 0, 5) // 2, i, jnp.clip(j - 5, 0, 5) % 2)),
                  pl.BlockSpec((D, tk), lambda i, j: (0, j)),
                  pl.BlockSpec((tm, D), row), pl.BlockSpec((tm, D), row),
                  pl.BlockSpec((8, D), lambda i, j: (0, 0))],
        out_specs=[pl.BlockSpec((tm, D), row), pl.BlockSpec((8, D), lambda i, j: (0, 0))],
        out_shape=[jax.ShapeDtypeStruct((S, D), F32), jax.ShapeDtypeStruct((8, D), F32)],
        scratch_shapes=[pltpu.VMEM((tm, D), F32)],
        compiler_params=_cp("arbitrary", "arbitrary"),
    )(dq, dkv, dxr, d3, w_in, x, dxo, vec)


def _attn_probs(qm, ka, bias_h, i, grp):
    s = _dot_nt(qm, ka) + bias_h
    col = lax.broadcasted_iota(jnp.int32, s.shape, 1)
    first_key = jnp.where(i == 0, 512 - 128 * grp, 0)
    s = jnp.where(col >= first_key, s, NEG)
    e = jnp.exp(s - jnp.max(s, axis=-1, keepdims=True))
    return e / jnp.sum(e, axis=-1, keepdims=True)


def attn_fwd(qkv, bias, name):
    S = qkv.shape[0]
    nb = S // TQ

    def body(q_ref, kp_ref, kc_ref, vp_ref, vc_ref, b_ref, o_ref, kw, vw):
        i = pl.program_id(1)
        kw[0:TQ, :] = kp_ref[...]
        kw[TQ:2 * TQ, :] = kc_ref[...]
        vw[0:TQ, :] = vp_ref[...]
        vw[TQ:2 * TQ, :] = vc_ref[...]
        lane = lax.broadcasted_iota(jnp.int32, (1, HP), 1)
        zero = jnp.zeros((), BF16)

        def group(a, carry):
            r0 = pl.multiple_of(a * 128, 128)
            qa = q_ref[pl.ds(r0, 128), :] * jnp.asarray(0.125, BF16)
            ka = kw[pl.ds(r0, WIN), :]
            va = vw[pl.ds(r0, WIN), :]
            o = jnp.zeros((128, HP), F32)
            for hh in range(2):
                msk = (lane < 64) if hh == 0 else (lane >= 64)
                p = _attn_probs(jnp.where(msk, qa, zero), ka, b_ref[hh], i, a)
                o += _dot(p.astype(BF16), jnp.where(msk, va, zero))
            o_ref[pl.ds(r0, 128), :] = o.astype(BF16)
            return carry

        lax.fori_loop(0, TQ // 128, group, 0)

    prev = lambda h, i: (jnp.maximum(i - 1, 0), 0)
    return pl.pallas_call(
        body, name=name, grid=(4, nb),
        in_specs=[pl.BlockSpec((TQ, HP), lambda h, i: (i, h)),
                  pl.BlockSpec((TQ, HP), lambda h, i: (jnp.maximum(i - 1, 0), 4 + h)),
                  pl.BlockSpec((TQ, HP), lambda h, i: (i, 4 + h)),
                  pl.BlockSpec((TQ, HP), lambda h, i: (jnp.maximum(i - 1, 0), 8 + h)),
                  pl.BlockSpec((TQ, HP), lambda h, i: (i, 8 + h)),
                  pl.BlockSpec((2, 128, WIN), lambda h, i: (h, 0, 0))],
        out_specs=pl.BlockSpec((TQ, HP), lambda h, i: (i, h)),
        out_shape=jax.ShapeDtypeStruct((S, 512), BF16),
        scratch_shapes=[pltpu.VMEM((2 * TQ, HP), BF16), pltpu.VMEM((2 * TQ, HP), BF16)],
        compiler_params=_cp("parallel", "arbitrary"),
    )(qkv, qkv, qkv, qkv, qkv, bias)


def attn_bwd(qkv, do, bias, name):
    S = qkv.shape[0]
    nb = S // TQ

    def body(q_ref, kp_ref, kc_ref, vp_ref, vc_ref, do_ref, b_ref, dqkv_ref, db_ref, dkv_ref, kw, vw, ak, av):
        i = pl.program_id(1)

        @pl.when(i == 0)
        def _():
            db_ref[...] = jnp.zeros_like(db_ref)
            ak[...] = jnp.zeros_like(ak)
            av[...] = jnp.zeros_like(av)

        @pl.when(i > 0)
        def _():
            ak[0:TQ, :] = ak[TQ:2 * TQ, :]
            av[0:TQ, :] = av[TQ:2 * TQ, :]
            ak[TQ:2 * TQ, :] = jnp.zeros((TQ, HP), F32)
            av[TQ:2 * TQ, :] = jnp.zeros((TQ, HP), F32)

        @pl.when(i < nb)
        def _():
            kw[0:TQ, :] = kp_ref[...]
            kw[TQ:2 * TQ, :] = kc_ref[...]
            vw[0:TQ, :] = vp_ref[...]
            vw[TQ:2 * TQ, :] = vc_ref[...]
            lane = lax.broadcasted_iota(jnp.int32, (1, HP), 1)
            zero = jnp.zeros((), BF16)

            def group(a, carry):
                r0 = pl.multiple_of(a * 128, 128)
                qa = q_ref[pl.ds(r0, 128), :] * jnp.asarray(0.125, BF16)
                doa = do_ref[pl.ds(r0, 128), :]
                ka = kw[pl.ds(r0, WIN), :]
                va = vw[pl.ds(r0, WIN), :]
                dq = jnp.zeros((128, HP), F32)
                for hh in range(2):
                    msk = (lane < 64) if hh == 0 else (lane >= 64)
                    qm = jnp.where(msk, qa, zero)
                    dom = jnp.where(msk, doa, zero)
                    p = _attn_probs(qm, ka, b_ref[hh], i, a)
                    dp = _dot_nt(dom, va)
                    ds = p * (dp - jnp.sum(p * dp, axis=-1, keepdims=True))
                    db_ref[hh] += ds
                    dsb = ds.astype(BF16)
                    dq += jnp.where(msk, _dot(dsb, ka), 0.0)
                    ak[pl.ds(r0, WIN), :] += _dot_tn(dsb, qm)
                    av[pl.ds(r0, WIN), :] += _dot_tn(p.astype(BF16), dom)
                dqkv_ref[0, pl.ds(r0, 128), :] = (dq * 0.125).astype(BF16)
                return carry

            lax.fori_loop(0, TQ // 128, group, 0)

        @pl.when(i > 0)
        def _():
            dkv_ref[0] = ak[0:TQ, :].astype(BF16)
            dkv_ref[1] = av[0:TQ, :].astype(BF16)

    cur = lambda i: jnp.minimum(i, nb - 1)
    prv = lambda i: jnp.clip(i - 1, 0, nb - 1)
    dq, db, dkv = pl.pallas_call(
        body, name=name, grid=(4, nb + 1),
        in_specs=[pl.BlockSpec((TQ, HP), lambda h, i: (cur(i), h)),
                  pl.BlockSpec((TQ, HP), lambda h, i: (prv(i), 4 + h)),
                  pl.BlockSpec((TQ, HP), lambda h, i: (cur(i), 4 + h)),
                  pl.BlockSpec((TQ, HP), lambda h, i: (prv(i), 8 + h)),
                  pl.BlockSpec((TQ, HP), lambda h, i: (cur(i), 8 + h)),
                  pl.BlockSpec((TQ, HP), lambda h, i: (cur(i), h)),
                  pl.BlockSpec((2, 128, WIN), lambda h, i: (h, 0, 0))],
        out_specs=[pl.BlockSpec((1, TQ, HP), lambda h, i: (0, cur(i), h)),
                   pl.BlockSpec((2, 128, WIN), lambda h, i: (h, 0, 0)),
                   pl.BlockSpec((2, TQ, HP), lambda h, i: (0, prv(i), h))],
        out_shape=[jax.ShapeDtypeStruct((1, S, 512), BF16), jax.ShapeDtypeStruct((8, 128, WIN), F32),
                   jax.ShapeDtypeStruct((2, S, 512), BF16)],
        scratch_shapes=[pltpu.VMEM((2 * TQ, HP), BF16), pltpu.VMEM((2 * TQ, HP), BF16),
                        pltpu.VMEM((2 * TQ, HP), F32), pltpu.VMEM((2 * TQ, HP), F32)],
        compiler_params=_cp("parallel", "arbitrary"),
    )(qkv, qkv, qkv, qkv, qkv, do, bias)
    return dq, db, dkv


def bias_grad(db, name):
    def body(db_ref, o_ref):
        r = lax.broadcasted_iota(jnp.int32, (128, 128), 0)
        c = lax.broadcasted_iota(jnp.int32, (128, 128), 1)
        flip = (r + c == 127).astype(BF16)
        lane = lax.broadcasted_iota(jnp.int32, (16, 384), 1)
        src = lax.broadcasted_iota(jnp.int32, (128, 384), 0)
        dst = lax.broadcasted_iota(jnp.int32, (128, 384), 1)

        def split_dot(v, m):
            hi = v.astype(BF16)
            r1 = v - hi.astype(F32)
            mid = r1.astype(BF16)
            lo = (r1 - mid.astype(F32)).astype(BF16)
            return _dot(hi, m) + _dot(mid, m) + _dot(lo, m)

        def diag_sums(w):
            y = pltpu.roll(split_dot(w, flip), 0, 1, stride=1, stride_axis=0)
            return jnp.broadcast_to(_colsum(y), (16, 128))

        w4 = db_ref[0, :, 512:640]
        w3 = db_ref[0, :, 384:512]
        far = jnp.sum(db_ref[0, :, 0:384]) + jnp.sum(jnp.where(r >= c, w3, 0.0))
        lo4 = diag_sums(jnp.where(r >= c, w4, 0.0))
        up4 = diag_sums(jnp.where(r < c, w4, 0.0))
        up3 = diag_sums(jnp.where(r < c, w3, 0.0))
        p_lo4 = (dst == 128 + (src + 1) % 128).astype(BF16)
        p_up4 = ((dst == src + 1) & (src < 127)).astype(BF16)
        p_up3 = ((dst == src + 129) & (src < 127)).astype(BF16)
        out = split_dot(lo4, p_lo4) + split_dot(up4, p_up4) + split_dot(up3, p_up3)
        o_ref[0] = out + jnp.where(lane == 256, far, 0.0)

    return pl.pallas_call(
        body, name=name, grid=(8,),
        in_specs=[pl.BlockSpec((1, 128, WIN), lambda h: (h, 0, 0))],
        out_specs=pl.BlockSpec((1, 16, 384), lambda h: (h, 0, 0)),
        out_shape=jax.ShapeDtypeStruct((8, 16, 384), F32),
        compiler_params=_cp("parallel"),
    )(db)[:, 0, :]


LT = 256
LC = 512


def _lru_gates(xs, pv_ref, wa_ref, wx_ref, tl):
    xc = (pv_ref[4:5, :] + pv_ref[3:4, :] * xs[pl.ds(8, tl), :] + pv_ref[2:3, :] * xs[pl.ds(7, tl), :]
          + pv_ref[1:2, :] * xs[pl.ds(6, tl), :] + pv_ref[0:1, :] * xs[pl.ds(5, tl), :])
    xcb = xc.astype(BF16)
    pa = jnp.concatenate([_dot(xcb[:, 0:256], wa_ref[0]), _dot(xcb[:, 256:512], wa_ref[1])], axis=1)
    px = jnp.concatenate([_dot(xcb[:, 0:256], wx_ref[0]), _dot(xcb[:, 256:512], wx_ref[1])], axis=1)
    r = _sigmoid(pa + pv_ref[5:6, :])
    ig = _sigmoid(px + pv_ref[6:7, :])
    z = -pv_ref[7:8, :]
    sp = jnp.maximum(z, 0.0) + jnp.log1p(jnp.exp(-jnp.abs(z)))
    log_a = (-LRU_C * r) * sp
    a = jnp.exp(log_a)
    mult = jnp.sqrt(-_expm1(2.0 * log_a))
    return xc, xcb, r, ig, sp, a, mult


def lru_fwd(rest, pvec, wa, wx, name):
    S = rest.shape[0]
    tl = min(LT, S)
    nt = S // tl

    def body(xr_ref, halo_ref, yr_ref, pv_ref, wa_ref, wx_ref, h_ref, hg_ref, xs, a_s, u_s, h_s, carry):
        ti = pl.program_id(1)

        @pl.when(ti == 0)
        def _():
            carry[...] = jnp.zeros_like(carry)

        xs[0:8, :] = jnp.where(ti > 0, halo_ref[...], 0.0)
        xs[pl.ds(8, tl), :] = xr_ref[...]
        xc, _, _, ig, _, a, mult = _lru_gates(xs, pv_ref, wa_ref, wx_ref, tl)
        a_s[...] = a
        u_s[...] = mult * (ig * xc)
        row = lax.broadcasted_iota(jnp.int32, (8, LC), 0)

        def blk(bi, c):
            o = pl.multiple_of(bi * 8, 8)
            av = a_s[pl.ds(o, 8), :]
            bv = u_s[pl.ds(o, 8), :]
            for d in (1, 2, 4):
                a_sh = pltpu.roll(av, d, 0)
                b_sh = pltpu.roll(bv, d, 0)
                m = row >= d
                bv = jnp.where(m, av * b_sh + bv, bv)
                av = jnp.where(m, av * a_sh, av)
            hv = bv + av * c
            h_s[pl.ds(o, 8), :] = hv
            return hv[7:8, :]

        carry[...] = lax.fori_loop(0, tl // 8, blk, carry[...])
        h = h_s[...]
        h_ref[...] = h
        hg_ref[...] = (h * _gelu(yr_ref[...])).astype(BF16)

    hb = tl // 8
    return pl.pallas_call(
        body, name=name, grid=(2, nt),
        in_specs=[pl.BlockSpec((tl, LC), lambda c, t: (t, c)),
                  pl.BlockSpec((8, LC), lambda c, t: (jnp.maximum(t * hb - 1, 0), c)),
                  pl.BlockSpec((tl, LC), lambda c, t: (t, 2 + c)),
                  pl.BlockSpec((8, LC), lambda c, t: (0, c)),
                  pl.BlockSpec((2, 256, 256), lambda c, t: (c, 0, 0)),
                  pl.BlockSpec((2, 256, 256), lambda c, t: (c, 0, 0))],
        out_specs=[pl.BlockSpec((tl, LC), lambda c, t: (t, c)), pl.BlockSpec((tl, LC), lambda c, t: (t, c))],
        out_shape=[jax.ShapeDtypeStruct((S, D), F32), jax.ShapeDtypeStruct((S, D), BF16)],
        scratch_shapes=[pltpu.VMEM((tl + 8, LC), F32), pltpu.VMEM((tl, LC), F32), pltpu.VMEM((tl, LC), F32),
                        pltpu.VMEM((tl, LC), F32), pltpu.VMEM((1, LC), F32)],
        compiler_params=_cp("parallel", "arbitrary"),
    )(rest, rest, rest, pvec, wa, wx)


def lru_bwd(dh, h, rest, pvec, wa, wx, name):
    S = rest.shape[0]
    tl = min(LT, S)
    nt = S // tl

    def body(dh_ref, h_ref, hhalo_ref, xr_ref, xhalo_ref, pv_ref, wa_ref, wx_ref,
             dxr_ref, vacc_ref, dwa_ref, dwx_ref,
             xs, hs, a_s, ash_s, b_s, lam_s, dxe, anext, lnext, dxnext):
        ti = pl.program_id(1)
        tr = nt - 1 - ti

        @pl.when(ti == 0)
        def _():
            anext[...] = jnp.zeros_like(anext)
            lnext[...] = jnp.zeros_like(lnext)
            dxnext[...] = jnp.zeros_like(dxnext)
            vacc_ref[...] = jnp.zeros_like(vacc_ref)
            dwa_ref[...] = jnp.zeros_like(dwa_ref)
            dwx_ref[...] = jnp.zeros_like(dwx_ref)

        xs[0:8, :] = jnp.where(tr > 0, xhalo_ref[...], 0.0)
        xs[pl.ds(8, tl), :] = xr_ref[...]
        xc, xcb, r, ig, sp, a, mult = _lru_gates(xs, pv_ref, wa_ref, wx_ref, tl)

        a_s[pl.ds(0, tl), :] = a
        a_s[pl.ds(tl, 8), :] = jnp.broadcast_to(anext[...], (8, LC))
        ash_s[...] = a_s[pl.ds(1, tl), :]
        b_s[...] = dh_ref[...]
        row = lax.broadcasted_iota(jnp.int32, (8, LC), 0)

        def blk(k, c):
            o = pl.multiple_of((tl // 8 - 1 - k) * 8, 8)
            av = ash_s[pl.ds(o, 8), :]
            bv = b_s[pl.ds(o, 8), :]
            for d in (1, 2, 4):
                a_sh = pltpu.roll(av, 8 - d, 0)
                b_sh = pltpu.roll(bv, 8 - d, 0)
                m = row < 8 - d
                bv = jnp.where(m, bv + av * b_sh, bv)
                av = jnp.where(m, av * a_sh, av)
            lv = bv + av * c
            lam_s[pl.ds(o, 8), :] = lv
            return lv[0:1, :]

        lnext[...] = lax.fori_loop(0, tl // 8, blk, lnext[...])
        anext[...] = a[0:1, :]
        lam = lam_s[...]

        hs[0:8, :] = jnp.where(tr > 0, hhalo_ref[...], 0.0)
        hs[pl.ds(8, tl), :] = h_ref[...]
        d_a = lam * hs[pl.ds(7, tl), :]
        d_mult = lam * (ig * xc)
        d_ig = lam * mult * xc
        dxc = lam * mult * ig
        d_log_a = d_a * a - d_mult * (a * a) / mult
        d_r = d_log_a * (-LRU_C * sp)
        vacc_ref[7:8, :] += _colsum(d_log_a * (-LRU_C * r)) * (-_sigmoid(-pv_ref[7:8, :]))
        d_pa = d_r * r * (1.0 - r)
        d_px = d_ig * ig * (1.0 - ig)
        vacc_ref[5:6, :] += _colsum(d_pa)
        vacc_ref[6:7, :] += _colsum(d_px)
        dpa = d_pa.astype(BF16)
        dpx = d_px.astype(BF16)
        back = []
        for g in range(2):
            sl = slice(256 * g, 256 * g + 256)
            dwa_ref[g] += _dot_tn(xcb[:, sl], dpa[:, sl])
            dwx_ref[g] += _dot_tn(xcb[:, sl], dpx[:, sl])
            back.append(_dot_nt(dpa[:, sl], wa_ref[g]) + _dot_nt(dpx[:, sl], wx_ref[g]))
        dxc = dxc + jnp.concatenate(back, axis=1)
        vacc_ref[4:5, :] += _colsum(dxc)
        for k in range(4):
            vacc_ref[k:k + 1, :] += _colsum(dxc * xs[pl.ds(5 + k, tl), :])
        dxe[pl.ds(0, tl), :] = dxc
        dxe[pl.ds(tl, 8), :] = dxnext[...]
        dxr = (pv_ref[3:4, :] * dxc + pv_ref[2:3, :] * dxe[pl.ds(1, tl), :]
               + pv_ref[1:2, :] * dxe[pl.ds(2, tl), :] + pv_ref[0:1, :] * dxe[pl.ds(3, tl), :])
        dxr_ref[...] = dxr.astype(BF16)
        dxnext[...] = dxc[0:8, :]

    hb = tl // 8
    rev = lambda t: nt - 1 - t
    halo = lambda t: jnp.maximum(rev(t) * hb - 1, 0)
    big = lambda: pltpu.VMEM((tl + 8, LC), F32)
    til = lambda: pltpu.VMEM((tl, LC), F32)
    return pl.pallas_call(
        body, name=name, grid=(2, nt),
        in_specs=[pl.BlockSpec((tl, LC), lambda c, t: (rev(t), c)),
                  pl.BlockSpec((tl, LC), lambda c, t: (rev(t), c)),
                  pl.BlockSpec((8, LC), lambda c, t: (halo(t), c)),
                  pl.BlockSpec((tl, LC), lambda c, t: (rev(t), c)),
                  pl.BlockSpec((8, LC), lambda c, t: (halo(t), c)),
                  pl.BlockSpec((8, LC), lambda c, t: (0, c)),
                  pl.BlockSpec((2, 256, 256), lambda c, t: (c, 0, 0)),
                  pl.BlockSpec((2, 256, 256), lambda c, t: (c, 0, 0))],
        out_specs=[pl.BlockSpec((tl, LC), lambda c, t: (rev(t), c)),
                   pl.BlockSpec((8, LC), lambda c, t: (0, c)),
                   pl.BlockSpec((2, 256, 256), lambda c, t: (c, 0, 0)),
                   pl.BlockSpec((2, 256, 256), lambda c, t: (c, 0, 0))],
        out_shape=[jax.ShapeDtypeStruct((S, D), BF16), jax.ShapeDtypeStruct((8, D), F32),
                   jax.ShapeDtypeStruct((4, 256, 256), F32), jax.ShapeDtypeStruct((4, 256, 256), F32)],
        scratch_shapes=[big(), big(), big(), til(), til(), til(), big(),
                        pltpu.VMEM((1, LC), F32), pltpu.VMEM((1, LC), F32), pltpu.VMEM((8, LC), F32)],
        compiler_params=_cp("parallel", "arbitrary"),
    )(dh, h, h, rest, rest, pvec, wa, wx)


def mix_out_fwd(x, ao, hg, rest, vec, w_att_o, w_rec_o, w_out, name, tm=256):
    S = x.shape[0]
    tm = min(tm, S)

    def body(x_ref, ao_ref, hg_ref, ga_ref, gr_ref, vec_ref, wa_ref, wr_ref, wo_ref,
             xo_ref, att_ref, rec_ref, mg_ref, f_ref):
        att = _dot(ao_ref[...], wa_ref[...])
        rec = _dot(hg_ref[...], wr_ref[...])
        att_ref[...] = att
        rec_ref[...] = rec
        mg = (_sigmoid(ga_ref[...]) * att + _sigmoid(gr_ref[...]) * rec).astype(BF16)
        mg_ref[...] = mg
        f = _dot(mg, wo_ref[...])
        f_ref[...] = f
        y = f * lax.rsqrt(_mean(f * f) + EPS) * vec_ref[1:2, :]
        xo_ref[...] = x_ref[...] + (1.0 * vec_ref[4:5, :]) * y

    row = lambda i: (i, 0)
    full = lambda r: pl.BlockSpec((r, D), lambda i: (0, 0))
    return pl.pallas_call(
        body, name=name, grid=(S // tm,),
        in_specs=[pl.BlockSpec((tm, D), row), pl.BlockSpec((tm, 512), row), pl.BlockSpec((tm, D), row),
                  pl.BlockSpec((tm, D), lambda i: (i, 2)), pl.BlockSpec((tm, D), lambda i: (i, 3)),
                  full(8), full(512), full(D), full(D)],
        out_specs=[pl.BlockSpec((tm, D), row)] * 5,
        out_shape=[jax.ShapeDtypeStruct((S, D), F32), jax.ShapeDtypeStruct((S, D), F32),
                   jax.ShapeDtypeStruct((S, D), F32), jax.ShapeDtypeStruct((S, D), BF16),
                   jax.ShapeDtypeStruct((S, D), F32)],
        compiler_params=_cp("parallel"),
    )(x, ao, hg, rest, rest, vec, w_att_o, w_rec_o, w_out)


def mix_out_bwd(dxo, f, att, rec, rest, h, vec, w_att_o, w_rec_o, w_out, name, tm=256):
    S = dxo.shape[0]
    tm = min(tm, S)

    def body(dxo_ref, f_ref, att_ref, rec_ref, yr_ref, ga_ref, gr_ref, h_ref, vec_ref, wa_ref, wr_ref, wo_ref,
             df_ref, da_ref, dr_ref, dao_ref, dh_ref, d3_ref, vacc_ref):
        @pl.when(pl.program_id(0) == 0)
        def _():
            vacc_ref[...] = jnp.zeros_like(vacc_ref)

        df = _post_norm_bwd(dxo_ref[...], f_ref[...], 1.0, vec_ref, vacc_ref).astype(BF16)
        df_ref[...] = df
        dm = _dot_nt(df, wo_ref[...])
        sa = _sigmoid(ga_ref[...])
        sr = _sigmoid(gr_ref[...])
        d_att = (dm * sa).astype(BF16)
        d_rec = (dm * sr).astype(BF16)
        da_ref[...] = d_att
        dr_ref[...] = d_rec
        d3_ref[1] = (dm * att_ref[...] * (sa * (1.0 - sa))).astype(BF16)
        d3_ref[2] = (dm * rec_ref[...] * (sr * (1.0 - sr))).astype(BF16)
        dao_ref[...] = _dot_nt(d_att, wa_ref[...]).astype(BF16)
        d_hg = _dot_nt(d_rec, wr_ref[...])
        yr = yr_ref[...]
        dh_ref[...] = d_hg * _gelu(yr)
        d3_ref[0] = (d_hg * h_ref[...] * _gelu_grad(yr)).astype(BF16)

    row = lambda i: (i, 0)
    full = lambda r: pl.BlockSpec((r, D), lambda i: (0, 0))
    return pl.pallas_call(
        body, name=name, grid=(S // tm,),
        in_specs=[pl.BlockSpec((tm, D), row)] * 4
        + [pl.BlockSpec((tm, D), lambda i: (i, 1)), pl.BlockSpec((tm, D), lambda i: (i, 2)),
           pl.BlockSpec((tm, D), lambda i: (i, 3)), pl.BlockSpec((tm, D), row),
           full(8), full(512), full(D), full(D)],
        out_specs=[pl.BlockSpec((tm, D), row)] * 3
        + [pl.BlockSpec((tm, 512), row), pl.BlockSpec((tm, D), row),
           pl.BlockSpec((3, tm, D), lambda i: (0, i, 0)), pl.BlockSpec((8, D), lambda i: (0, 0))],
        out_shape=[jax.ShapeDtypeStruct((S, D), BF16)] * 3
        + [jax.ShapeDtypeStruct((S, 512), BF16), jax.ShapeDtypeStruct((S, D), F32),
           jax.ShapeDtypeStruct((3, S, D), BF16), jax.ShapeDtypeStruct((8, D), F32)],
        compiler_params=_cp("arbitrary"),
    )(dxo, f, att, rec, rest, rest, rest, h, vec, w_att_o, w_rec_o, w_out)


def loss_grad(y, tgt, name, tm=512):
    S = y.shape[0]
    tm = min(tm, S)
    nt = S // tm

    def body(y_ref, t_ref, dy_ref, l_ref, acc):
        i = pl.program_id(0)

        @pl.when(i == 0)
        def _():
            acc[...] = jnp.zeros_like(acc)

        d = y_ref[...] - t_ref[...]
        dy_ref[...] = d * (1.0 / D)
        acc[...] += _colsum(d * d)

        @pl.when(i == nt - 1)
        def _():
            l_ref[...] = jnp.broadcast_to(0.5 * jnp.sum(acc[...]) * (1.0 / D), (8, 128))

    return pl.pallas_call(
        body, name=name, grid=(nt,),
        in_specs=[pl.BlockSpec((tm, D), lambda i: (i, 0))] * 2,
        out_specs=[pl.BlockSpec((tm, D), lambda i: (i, 0)), pl.BlockSpec((8, 128), lambda i: (0, 0))],
        out_shape=[jax.ShapeDtypeStruct((S, D), F32), jax.ShapeDtypeStruct((8, 128), F32)],
        scratch_shapes=[pltpu.VMEM((1, D), F32)],
        compiler_params=_cp("arbitrary"),
    )(y, tgt)


def ada_fwd(c_all, w_ada, b_ada, name, tn=768):
    n = w_ada.shape[1]

    def body(c_ref, w_ref, b_ref, o_ref):
        cv = c_ref[...]
        ca = (cv * _sigmoid(cv)).astype(BF16)
        o_ref[...] = _dot(ca, w_ref[...].astype(BF16)) + b_ref[...]

    return pl.pallas_call(
        body, name=name, grid=(n // tn,),
        in_specs=[pl.BlockSpec((8, D), lambda j: (0, 0)), pl.BlockSpec((D, tn), lambda j: (0, j)),
                  pl.BlockSpec((1, tn), lambda j: (0, j))],
        out_specs=pl.BlockSpec((8, tn), lambda j: (0, j)),
        out_shape=jax.ShapeDtypeStruct((8, n), F32),
        compiler_params=_cp("parallel"),
    )(c_all, w_ada, b_ada)


def ada_bwd(c_all_t, dmod, name, tn=768):
    n = dmod.shape[1]

    def body(c_ref, d_ref, o_ref):
        cv = c_ref[...]
        ca = (cv * _sigmoid(cv)).astype(BF16)
        o_ref[...] = _dot(ca, d_ref[...].astype(BF16))

    return pl.pallas_call(
        body, name=name, grid=(n // tn,),
        in_specs=[pl.BlockSpec((D, 128), lambda j: (0, 0)), pl.BlockSpec((128, tn), lambda j: (0, j))],
        out_specs=pl.BlockSpec((D, tn), lambda j: (0, j)),
        out_shape=jax.ShapeDtypeStruct((D, n), F32),
        compiler_params=_cp("parallel"),
    )(c_all_t, dmod)


def _row_tile(rows, cols, itemsize=4, budget=1536 * 1024):
    best = None
    for t in range(8, rows + 1, 8):
        if rows % t == 0 and t * cols * itemsize <= budget:
            best = t
    return rows if best is None else best


def sum_lead(parts, name, out_dtype=F32):
    n, R, C = parts.shape
    tr = _row_tile(R, C * n)

    def body(p_ref, o_ref):
        acc = p_ref[0].astype(F32)
        for k in range(1, n):
            acc = acc + p_ref[k].astype(F32)
        o_ref[...] = acc.astype(out_dtype)

    return pl.pallas_call(
        body, name=name, grid=(R // tr,),
        in_specs=[pl.BlockSpec((n, tr, C), lambda i: (0, i, 0))],
        out_specs=pl.BlockSpec((tr, C), lambda i: (i, 0)),
        out_shape=jax.ShapeDtypeStruct((R, C), out_dtype),
        compiler_params=_cp("parallel"),
    )(parts)


def adamw(w, g, m, v, name):
    R, C = w.shape
    tr = _row_tile(R, C * 7, budget=8 * 1024 * 1024)

    def body(w_ref, g_ref, m_ref, v_ref, d_ref, mo_ref, vo_ref):
        gv = g_ref[...]
        mn = ADAM_B1 * m_ref[...] + (1.0 - ADAM_B1) * gv
        vn = ADAM_B2 * v_ref[...] + (1.0 - ADAM_B2) * (gv * gv)
        m_hat = mn / (1.0 - ADAM_B1 ** ADAM_STEP)
        v_hat = vn / (1.0 - ADAM_B2 ** ADAM_STEP)
        d_ref[...] = -ADAM_LR * (m_hat / (jnp.sqrt(v_hat) + ADAM_EPS) + ADAM_WD * w_ref[...])
        mo_ref[...] = mn
        vo_ref[...] = vn

    spec = pl.BlockSpec((tr, C), lambda i: (i, 0))
    return pl.pallas_call(
        body, name=name, grid=(R // tr,),
        in_specs=[spec] * 4, out_specs=[spec] * 3,
        out_shape=[jax.ShapeDtypeStruct((R, C), F32)] * 3,
        compiler_params=_cp("parallel"),
    )(w, g, m, v)


def _mesh_pos():
    return lax.axis_index("x"), lax.axis_index("y"), lax.axis_index("c")


def _other_chips(mx, my):
    return [(1 - mx, my), (mx, 1 - my), (1 - mx, 1 - my)]


def ag_small(x, name):
    R = x.shape[0]

    def body(x_ref, out_ref, send_sems, recv_sems, local_sem):
        mx, my, mc = _mesh_pos()
        me, sibling = (mx, my, mc), (mx, my, 1 - mc)
        chips = _other_chips(mx, my)

        def slot(px, py, pc):
            return out_ref.at[4 * px + 2 * py + pc]

        def copy(k, block, to, src=None):
            return pltpu.make_async_remote_copy(
                src_ref=slot(*block) if src is None else src, dst_ref=slot(*block),
                send_sem=send_sems.at[k], recv_sem=recv_sems.at[k], device_id=to, device_id_type=MESH)

        mine = pltpu.make_async_copy(x_ref, slot(*me), local_sem)
        mine.start()
        first = [copy(0, me, sibling, src=x_ref)]
        first += [copy(1 + j, me, (*chip, mc), src=x_ref) for j, chip in enumerate(chips)]
        for cp in first:
            cp.start()
        passed = [copy(4 + j, (*chip, mc), sibling) for j, chip in enumerate(chips)]
        for j, chip in enumerate(chips):
            copy(1 + j, (*chip, mc), me).wait_recv()
            passed[j].start()
        copy(0, sibling, me).wait_recv()
        for j, chip in enumerate(chips):
            copy(4 + j, (*chip, 1 - mc), me).wait_recv()
        for cp in first + passed:
            cp.wait_send()
        mine.wait()

    return pl.pallas_call(
        body, name=name,
        out_shape=jax.ShapeDtypeStruct((N_DEV, R, 128), F32),
        in_specs=[pl.BlockSpec(memory_space=pltpu.VMEM)],
        out_specs=pl.BlockSpec(memory_space=pltpu.VMEM),
        scratch_shapes=[pltpu.SemaphoreType.DMA((7,)), pltpu.SemaphoreType.DMA((7,)), pltpu.SemaphoreType.DMA],
        compiler_params=pltpu.CompilerParams(vmem_limit_bytes=VMEM_LIMIT),
    )(x)


BIG = (("ffn1_w_gu", "col", D, PW), ("ffn1_w_down", "row", FF, D), ("w_in", "col", D, PW),
       ("w_att_o", "col", 512, D), ("w_rec_o", "row", D, D), ("w_out", "row", D, D),
       ("ffn2_w_gu", "col", D, PW), ("ffn2_w_down", "row", FF, D))
NBIG = len(BIG)


def _shard_shape(kind, R, C):
    return (R, C // 4) if kind == "col" else (R // 4, C)


def _region(ref, kind, R, C, q, half, t, tr):
    sr, sc = _shard_shape(kind, R, C)
    if kind == "col":
        return ref.at[pl.ds(pl.multiple_of(half * (R // 2) + t * tr, 16), tr), pl.ds(q * sc, sc)]
    return ref.at[pl.ds(pl.multiple_of(q * sr + t * tr, 16), tr), pl.ds(half * (C // 2), C // 2)]


def ag_push(w, kind, R, C, c_arr, name):
    sr, sc = _shard_shape(kind, R, C)
    hr, hc = (sr // 2, sc) if kind == "col" else (sr, sc // 2)
    tr = _row_tile(hr, hc, itemsize=2, budget=512 * 1024)
    nt = hr // tr

    def body(c_ref, mine_ref, other_ref, full_ref, stage, lsem, ssem, rsem):
        i = pl.program_id(0)
        par = i % 2
        mx, my, mc = _mesh_pos()
        p = 2 * mx + my
        chips = _other_chips(mx, my)

        def copies(s, q, h, t):
            mine = _region(full_ref, kind, R, C, q, h, t, tr)
            other = _region(full_ref, kind, R, C, q, 1 - h, t, tr)
            out = [pltpu.make_async_remote_copy(src_ref=stage.at[s, 0], dst_ref=mine, send_sem=ssem.at[s, k],
                                                recv_sem=rsem.at[k], device_id=(*chips[k], mc), device_id_type=MESH)
                   for k in range(3)]
            out.append(pltpu.make_async_copy(stage.at[s, 0], mine, lsem.at[s, 0]))
            out.append(pltpu.make_async_copy(stage.at[s, 1], other, lsem.at[s, 1]))
            return out

        def wait_sent(s):
            cps = copies(s, 0, 0, 0)
            for cp in cps[:3]:
                cp.wait_send()
            for cp in cps[3:]:
                cp.wait()

        @pl.when(i >= 2)
        def _():
            wait_sent(par)

        stage[par, 0] = mine_ref[...].astype(BF16)
        stage[par, 1] = other_ref[...].astype(BF16)
        if kind == "col":
            for q in range(4):
                @pl.when(p == q)
                def _(q=q):
                    for cp in copies(par, q, mc, i):
                        cp.start()
        else:
            for h in range(2):
                @pl.when(mc == h)
                def _(h=h):
                    for cp in copies(par, p, h, i):
                        cp.start()

        @pl.when(i == nt - 1)
        def _():
            for s in range(min(nt, 2)):
                wait_sent(s)
            for k in range(3):
                half = full_ref.at[pl.ds(0, hr), pl.ds(0, hc)]
                pltpu.make_async_remote_copy(src_ref=half, dst_ref=half, send_sem=ssem.at[0, k], recv_sem=rsem.at[k],
                                             device_id=(*chips[k], mc), device_id_type=MESH).wait_recv()

    if kind == "col":
        mine_spec = pl.BlockSpec((tr, hc), lambda i, c: (c[0] * nt + i, 0))
        other_spec = pl.BlockSpec((tr, hc), lambda i, c: ((1 - c[0]) * nt + i, 0))
    else:
        mine_spec = pl.BlockSpec((tr, hc), lambda i, c: (i, c[0]))
        other_spec = pl.BlockSpec((tr, hc), lambda i, c: (i, 1 - c[0]))
    return pl.pallas_call(
        body, name=name,
        grid_spec=pltpu.PrefetchScalarGridSpec(
            num_scalar_prefetch=1, grid=(nt,), in_specs=[mine_spec, other_spec], out_specs=ANY,
            scratch_shapes=[pltpu.VMEM((2, 2, tr, hc), BF16), pltpu.SemaphoreType.DMA((2, 2)),
                            pltpu.SemaphoreType.DMA((2, 3)), pltpu.SemaphoreType.DMA((3,))]),
        out_shape=jax.ShapeDtypeStruct((R, C), BF16),
        compiler_params=_cp("arbitrary"),
    )(c_arr, w, w)


def ag_forward(full, kind, R, C, name):
    sr, sc = _shard_shape(kind, R, C)
    hr, hc = (sr // 2, sc) if kind == "col" else (sr, sc // 2)
    tr = _row_tile(hr, hc, itemsize=2, budget=512 * 1024)
    nt = hr // tr

    def body(src_ref, full_ref, stage, lsem, ssem, rsem):
        k, i = pl.program_id(0), pl.program_id(1)
        step = k * nt + i
        par = step % 2
        mx, my, mc = _mesh_pos()
        q_k = _partner_chip(k, 2 * mx + my)

        def push(s, dst):
            return pltpu.make_async_remote_copy(src_ref=stage.at[s], dst_ref=dst, send_sem=ssem.at[s], recv_sem=rsem,
                                                device_id=(mx, my, 1 - mc), device_id_type=MESH)

        @pl.when(step >= 2)
        def _():
            push(par, _region(full_ref, kind, R, C, 0, 0, 0, tr)).wait_send()

        def move(q, h):
            load = pltpu.make_async_copy(_region(src_ref, kind, R, C, q, h, i, tr), stage.at[par], lsem)
            load.start()
            load.wait()
            push(par, _region(full_ref, kind, R, C, q, h, i, tr)).start()

        if kind == "col":
            for q in range(4):
                @pl.when(q_k == q)
                def _(q=q):
                    move(q, mc)
        else:
            for h in range(2):
                @pl.when(mc == h)
                def _(h=h):
                    move(q_k, h)

        @pl.when(step == 3 * nt - 1)
        def _():
            for s in range(2):
                push(s, _region(full_ref, kind, R, C, 0, 0, 0, tr)).wait_send()
            three = full_ref.at[pl.ds(0, hr), pl.ds(0, 3 * hc)] if kind == "col" else full_ref.at[pl.ds(0, 3 * hr), pl.ds(0, hc)]
            pltpu.make_async_remote_copy(src_ref=three, dst_ref=three, send_sem=ssem.at[0], recv_sem=rsem,
                                         device_id=(mx, my, 1 - mc), device_id_type=MESH).wait_recv()

    return pl.pallas_call(
        body, name=name, grid=(3, nt),
        in_specs=[ANY], out_specs=ANY,
        out_shape=jax.ShapeDtypeStruct((R, C), BF16),
        scratch_shapes=[pltpu.VMEM((2, tr, hc), BF16), pltpu.SemaphoreType.DMA, pltpu.SemaphoreType.DMA((2,)),
                        pltpu.SemaphoreType.DMA],
        input_output_aliases={0: 0},
        compiler_params=_cp("arbitrary", "arbitrary"),
    )(full)


def _half_shape(kind, R, C):
    return (R // 2, C) if kind == "col" else (R, C // 2)


def _piece_shape(kind, R, C):
    return (R // 2, C // 4) if kind == "col" else (R // 4, C // 2)


def pair_push(g, kind, c_arr, name):
    R, C = g.shape
    hr, hc = _half_shape(kind, R, C)
    tr = _row_tile(hr, hc, itemsize=2, budget=1024 * 1024)
    nt = hr // tr

    def body(c_ref, g_ref, out_ref, stage, ssem, rsem):
        i = pl.program_id(0)
        slot = i % 2
        mx, my, mc = _mesh_pos()

        def push(s, t):
            return pltpu.make_async_remote_copy(
                src_ref=stage.at[s], dst_ref=out_ref.at[pl.ds(pl.multiple_of(t * tr, 16), tr)],
                send_sem=ssem.at[s], recv_sem=rsem, device_id=(mx, my, 1 - mc), device_id_type=MESH)

        @pl.when(i >= 2)
        def _():
            push(slot, 0).wait_send()

        stage[slot] = g_ref[...]
        push(slot, i).start()

        @pl.when(i == nt - 1)
        def _():
            push(slot, 0).wait_send()
            if nt >= 2:
                push(1 - slot, 0).wait_send()
            pltpu.make_async_remote_copy(src_ref=out_ref, dst_ref=out_ref, send_sem=ssem.at[0], recv_sem=rsem,
                                         device_id=(mx, my, 1 - mc), device_id_type=MESH).wait_recv()

    if kind == "col":
        g_spec = pl.BlockSpec((tr, hc), lambda i, c: ((1 - c[0]) * nt + i, 0))
    else:
        g_spec = pl.BlockSpec((tr, hc), lambda i, c: (i, 1 - c[0]))
    return pl.pallas_call(
        body, name=name,
        grid_spec=pltpu.PrefetchScalarGridSpec(
            num_scalar_prefetch=1, grid=(nt,), in_specs=[g_spec], out_specs=ANY,
            scratch_shapes=[pltpu.VMEM((2, tr, hc), BF16), pltpu.SemaphoreType.DMA((2,)), pltpu.SemaphoreType.DMA]),
        out_shape=jax.ShapeDtypeStruct((hr, hc), BF16),
        compiler_params=_cp("arbitrary"),
    )(c_arr, g)


def _partner_chip(k, p):
    return p ^ jnp.where(k == 0, 2, jnp.where(k == 1, 1, jnp.where(k == 2, 3, 0)))


def pair_add_scatter(g, got, kind, cp_arr, name):
    R, C = g.shape
    pr, pc = _piece_shape(kind, R, C)
    tr = _row_tile(pr, pc, itemsize=2, budget=768 * 1024)
    nt = pr // tr

    def body(cp_ref, g_ref, got_ref, rb_ref, stage, lsem, ssem, rsem):
        i, kk = pl.program_id(0), pl.program_id(1)
        par = i % 2
        mx, my, mc = _mesh_pos()
        p = 2 * mx + my
        chips = _other_chips(mx, my)

        def dst(t):
            return rb_ref.at[p, pl.ds(pl.multiple_of(t * tr, 16), tr)]

        def push(k, s, t):
            return pltpu.make_async_remote_copy(src_ref=stage.at[s, k], dst_ref=dst(t), send_sem=ssem.at[s, k],
                                                recv_sem=rsem.at[k], device_id=(*chips[k], mc), device_id_type=MESH)

        def keep(s, t):
            return pltpu.make_async_copy(stage.at[s, 3], dst(t), lsem.at[s])

        tile = (g_ref[...].astype(F32) + got_ref[...].astype(F32)).astype(BF16)
        for k in range(4):
            @pl.when(kk == k)
            def _(k=k):
                cp = push(k, par, i) if k < 3 else keep(par, i)

                @pl.when(i >= 2)
                def _():
                    cp.wait_send() if k < 3 else cp.wait()

                stage[par, k] = tile
                cp.start()

        @pl.when((i == nt - 1) & (kk == 3))
        def _():
            for s in range(min(nt, 2)):
                for k in range(3):
                    push(k, s, 0).wait_send()
                keep(s, 0).wait()
            for k in range(3):
                whole = rb_ref.at[0]
                pltpu.make_async_remote_copy(src_ref=whole, dst_ref=whole, send_sem=ssem.at[0, k], recv_sem=rsem.at[k],
                                             device_id=(*chips[k], mc), device_id_type=MESH).wait_recv()

    if kind == "col":
        g_spec = pl.BlockSpec((tr, pc), lambda i, k, cp: (cp[0] * nt + i, _partner_chip(k, cp[1])))
        got_spec = pl.BlockSpec((tr, pc), lambda i, k, cp: (i, _partner_chip(k, cp[1])))
    else:
        g_spec = pl.BlockSpec((tr, pc), lambda i, k, cp: (_partner_chip(k, cp[1]) * nt + i, cp[0]))
        got_spec = pl.BlockSpec((tr, pc), lambda i, k, cp: (_partner_chip(k, cp[1]) * nt + i, 0))
    return pl.pallas_call(
        body, name=name,
        grid_spec=pltpu.PrefetchScalarGridSpec(
            num_scalar_prefetch=1, grid=(nt, 4), in_specs=[g_spec, got_spec], out_specs=ANY,
            scratch_shapes=[pltpu.VMEM((2, 4, tr, pc), BF16), pltpu.SemaphoreType.DMA((2,)),
                            pltpu.SemaphoreType.DMA((2, 3)), pltpu.SemaphoreType.DMA((3,))]),
        out_shape=jax.ShapeDtypeStruct((4, pr, pc), BF16),
        compiler_params=_cp("arbitrary", "arbitrary"),
    )(cp_arr, g, got)


def sum_share(parts, kind, R, C, name):
    _, pr, pc = parts.shape
    sr, sc = _shard_shape(kind, R, C)
    tr = _row_tile(pr, pc * 4, budget=4 * 1024 * 1024)
    nt = pr // tr

    def body(p_ref, fin_ref, stage, lsem, ssem, rsem):
        i = pl.program_id(0)
        slot = i % 2
        mx, my, mc = _mesh_pos()

        def region(h, t):
            r0 = pl.multiple_of(t * tr, 8)
            if kind == "col":
                return fin_ref.at[pl.ds(pl.multiple_of(h * pr + r0, 8), tr)]
            return fin_ref.at[pl.ds(r0, tr), pl.ds(h * pc, pc)]

        def copies(s, h, t):
            return (pltpu.make_async_copy(stage.at[s], region(h, t), lsem.at[s]),
                    pltpu.make_async_remote_copy(src_ref=stage.at[s], dst_ref=region(h, t), send_sem=ssem.at[s],
                                                 recv_sem=rsem, device_id=(mx, my, 1 - mc), device_id_type=MESH))

        def wait_sent(s):
            loc, rem = copies(s, 0, 0)
            loc.wait()
            rem.wait_send()

        @pl.when(i >= 2)
        def _():
            wait_sent(slot)

        acc = p_ref[0].astype(F32)
        for k in range(1, 4):
            acc = acc + p_ref[k].astype(F32)
        stage[slot] = acc
        if kind == "col":
            for cp in copies(slot, mc, i):
                cp.start()
        else:
            for h in range(2):
                @pl.when(mc == h)
                def _(h=h):
                    for cp in copies(slot, h, i):
                        cp.start()

        @pl.when(i == nt - 1)
        def _():
            wait_sent(slot)
            if nt >= 2:
                wait_sent(1 - slot)
            half = fin_ref.at[pl.ds(0, pr), pl.ds(0, pc)]
            pltpu.make_async_remote_copy(src_ref=half, dst_ref=half, send_sem=ssem.at[0], recv_sem=rsem,
                                         device_id=(mx, my, 1 - mc), device_id_type=MESH).wait_recv()

    return pl.pallas_call(
        body, name=name, grid=(nt,),
        in_specs=[pl.BlockSpec((4, tr, pc), lambda i: (0, i, 0))],
        out_specs=ANY,
        out_shape=jax.ShapeDtypeStruct((sr, sc), F32),
        scratch_shapes=[pltpu.VMEM((2, tr, pc), F32), pltpu.SemaphoreType.DMA((2,)), pltpu.SemaphoreType.DMA((2,)),
                        pltpu.SemaphoreType.DMA],
        compiler_params=_cp("arbitrary"),
    )(parts)


def _pack(parts, rows):
    flat = []
    for a in parts:
        a = jnp.ravel(a).astype(F32)
        flat.append(jnp.pad(a, (0, (-a.shape[0]) % 128)))
    v = jnp.concatenate(flat)
    return jnp.pad(v, (0, rows * 128 - v.shape[0])).reshape(rows, 128)


def _unpack(block, shapes):
    lead = block.shape[:-2]
    v = block.reshape(lead + (-1,))
    out, off = [], 0
    for shp in shapes:
        n = int(np.prod(shp))
        out.append(v[..., off:off + n].reshape(lead + tuple(shp)))
        off += n + (-n) % 128
    return out


def _block_diag4(w):
    w4 = w.reshape(4, 4, 64, 64)
    eye = jnp.eye(4, dtype=w.dtype)
    return (w4[:, :, :, None, :] * eye[None, :, None, :, None]).reshape(4, 256, 256)


def _diag_blocks(bd):
    b5 = bd.reshape(4, 4, 64, 4, 64)
    return jnp.stack([b5[:, i, :, i, :] for i in range(4)], axis=1).reshape(16, 64, 64)


def _bias_window(rel_bias):
    m = np.arange(767)
    tv = rel_bias[:, np.clip(639 - m, -128, 128) + 128]
    win = jnp.stack([tv[:, 127 - r:127 - r + WIN] for r in range(128)], axis=1)
    qh = np.arange(128)[:, None] // CHUNK
    kc = np.arange(WIN)[None, :] // CHUNK
    valid = (kc >= qh) & (kc <= qh + 8)
    return jnp.where(jnp.asarray(valid)[None], win, NEG)


SMALL = ("b_ada", "norm_pre", "norm_post", "rel_bias", "conv_w", "conv_b", "lru_wa", "lru_ba", "lru_wx",
         "lru_bx", "lru_lambda")
WEIGHTS = ("w_ada", "b_ada", "norm_pre", "norm_post", "ffn1_w_gu", "ffn1_w_down", "w_in", "rel_bias", "conv_w",
           "conv_b", "lru_wa", "lru_ba", "lru_wx", "lru_bx", "lru_lambda", "w_att_o", "w_rec_o", "w_out",
           "ffn2_w_gu", "ffn2_w_down")


def kernel(x, c, w_ada, b_ada, norm_pre, norm_post, ffn1_w_gu, ffn1_w_down, w_in, rel_bias, conv_w, conv_b, lru_wa, lru_ba, lru_wx, lru_bx, lru_lambda, w_att_o, w_rec_o, w_out, ffn2_w_gu, ffn2_w_down, loss_target, m_w_ada, m_b_ada, m_norm_pre, m_norm_post, m_ffn1_w_gu, m_ffn1_w_down, m_w_in, m_rel_bias, m_conv_w, m_conv_b, m_lru_wa, m_lru_ba, m_lru_wx, m_lru_bx, m_lru_lambda, m_w_att_o, m_w_rec_o, m_w_out, m_ffn2_w_gu, m_ffn2_w_down, v_w_ada, v_b_ada, v_norm_pre, v_norm_post, v_ffn1_w_gu, v_ffn1_w_down, v_w_in, v_rel_bias, v_conv_w, v_conv_b, v_lru_wa, v_lru_ba, v_lru_wx, v_lru_bx, v_lru_lambda, v_w_att_o, v_w_rec_o, v_w_out, v_ffn2_w_gu, v_ffn2_w_down):
    W = dict(w_ada=w_ada, b_ada=b_ada, norm_pre=norm_pre, norm_post=norm_post, ffn1_w_gu=ffn1_w_gu,
             ffn1_w_down=ffn1_w_down, w_in=w_in, rel_bias=rel_bias, conv_w=conv_w, conv_b=conv_b, lru_wa=lru_wa,
             lru_ba=lru_ba, lru_wx=lru_wx, lru_bx=lru_bx, lru_lambda=lru_lambda, w_att_o=w_att_o, w_rec_o=w_rec_o,
             w_out=w_out, ffn2_w_gu=ffn2_w_gu, ffn2_w_down=ffn2_w_down)
    M = dict(w_ada=m_w_ada, b_ada=m_b_ada, norm_pre=m_norm_pre, norm_post=m_norm_post, ffn1_w_gu=m_ffn1_w_gu,
             ffn1_w_down=m_ffn1_w_down, w_in=m_w_in, rel_bias=m_rel_bias, conv_w=m_conv_w, conv_b=m_conv_b,
             lru_wa=m_lru_wa, lru_ba=m_lru_ba, lru_wx=m_lru_wx, lru_bx=m_lru_bx, lru_lambda=m_lru_lambda,
             w_att_o=m_w_att_o, w_rec_o=m_w_rec_o, w_out=m_w_out, ffn2_w_gu=m_ffn2_w_gu, ffn2_w_down=m_ffn2_w_down)
    V = dict(w_ada=v_w_ada, b_ada=v_b_ada, norm_pre=v_norm_pre, norm_post=v_norm_post, ffn1_w_gu=v_ffn1_w_gu,
             ffn1_w_down=v_ffn1_w_down, w_in=v_w_in, rel_bias=v_rel_bias, conv_w=v_conv_w, conv_b=v_conv_b,
             lru_wa=v_lru_wa, lru_ba=v_lru_ba, lru_wx=v_lru_wx, lru_bx=v_lru_bx, lru_lambda=v_lru_lambda,
             w_att_o=v_w_att_o, w_rec_o=v_w_rec_o, w_out=v_w_out, ffn2_w_gu=v_ffn2_w_gu, ffn2_w_down=v_ffn2_w_down)
    mx, my, mc = _mesh_pos()
    p = 2 * mx + my
    e = 4 * mx + 2 * my + mc
    xs = x[0]

    g1 = ag_small(_pack([c, norm_pre, norm_post, conv_w], 32), "ag_small_params")
    c_all, npre4, npost4, cw4 = _unpack(g1, [(D,), (3, 256), (3, 256), (4, 256)])
    chipwise = lambda a: jnp.moveaxis(a[0::2], 0, 1).reshape(a.shape[1], D)
    npre, npost, conv_full = chipwise(npre4), chipwise(npost4), chipwise(cw4)

    b_cols = lax.dynamic_slice(b_ada, (0, p * 2304), (1, 2304))
    mod_cols = ada_fwd(c_all, w_ada[0], b_cols, "ada_fwd")
    g2 = ag_small(mod_cols.reshape(144, 128), "ag_mod")
    mod_all = jnp.moveaxis(g2[0::2].reshape(4, 8, 2304), 0, 1).reshape(8, 9 * D)
    mod = lax.dynamic_index_in_dim(mod_all, e, 0, keepdims=False).reshape(3, 3, D)
    zeros3 = jnp.zeros((3, D), F32)
    vecs = [jnp.concatenate([npre[k:k + 1], npost[k:k + 1], mod[k], zeros3], axis=0) for k in range(3)]

    c_arr = jnp.reshape(mc, (1,)).astype(jnp.int32)
    cp_arr = jnp.stack([mc, p]).astype(jnp.int32)
    full = [ag_forward(ag_push(W[n][0], kind, R, C, c_arr, "ag_push_" + n), kind, R, C, "ag_forward_" + n)
            for (n, kind, R, C) in BIG]
    f1_gu, f1_dn, win, wao, wro, wout, f2_gu, f2_dn = full
    wa_bd = _block_diag4(lru_wa[0]).astype(BF16)
    wx_bd = _block_diag4(lru_wx[0]).astype(BF16)
    pvec = jnp.concatenate([conv_full, conv_b, lru_ba, lru_bx, lru_lambda], axis=0)
    bias = _bias_window(rel_bias[0])

    x1, h1, g1_, u1, a1, f1 = ffn_fwd(xs, vecs[0], f1_gu, f1_dn, 0.5, "ffn1_fwd")
    h2, qkv, rest = proj_fwd(x1, vecs[1], win, "proj_fwd")
    ao = attn_fwd(qkv, bias, "attn_fwd")
    hl, hg = lru_fwd(rest, pvec, wa_bd, wx_bd, "lru_fwd")
    x2, att, rec, mg, f2 = mix_out_fwd(x1, ao, hg, rest, vecs[1], wao, wro, wout, "mix_out_fwd")
    x3, h3, g3_, u3, a3, f3 = ffn_fwd(x2, vecs[2], f2_gu, f2_dn, 0.5, "ffn2_fwd")
    dy, lvec = loss_grad(x3, loss_target[0], "loss_grad")
    loss = lax.psum(lvec[0, 0], ("x", "y", "c"))

    G = {}
    dx2, df3, dgu3, va2 = ffn_bwd(dy, x2, f3, g3_, u3, vecs[2], f2_gu, f2_dn, 0.5, "ffn2_bwd")
    G["ffn2_w_gu"] = mm_tn(h3, dgu3, "dw_ffn2_gu", 512, 1408, 512)
    G["ffn2_w_down"] = mm_tn(a3, df3, "dw_ffn2_down", 1408, D, 512)
    df2, d_att, d_rec, dao, dhl, d3, va_out = mix_out_bwd(dx2, f2, att, rec, rest, hl, vecs[1], wao, wro, wout,
                                                          "mix_out_bwd")
    G["w_out"] = mm_tn(mg, df2, "dw_out", D, D, 512)
    G["w_att_o"] = mm_tn(ao, d_att, "dw_att_o", 512, D, 512)
    G["w_rec_o"] = mm_tn(hg, d_rec, "dw_rec_o", D, D, 512)
    dq, db, dkv = attn_bwd(qkv, dao, bias, "attn_bwd")
    dxr, v_lru, dwa_bd, dwx_bd = lru_bwd(dhl, hl, rest, pvec, wa_bd, wx_bd, "lru_bwd")
    dx1, va_in = proj_bwd(dq, dkv, dxr, d3, win, x1, dx2, vecs[1], "proj_bwd")
    gin = mm_tn(h2, dq, "dw_in_q", D, 512, 512, n_total=PW)
    gin = mm_tn(h2, dkv, "dw_in_kv", D, 512, 512, prev=gin, col_off=512, n_total=PW)
    gin = mm_tn(h2, dxr, "dw_in_xr", D, 512, 512, prev=gin, col_off=1536, n_total=PW)
    G["w_in"] = mm_tn(h2, d3, "dw_in_gates", D, 512, 512, prev=gin, col_off=2560, n_total=PW)
    dx0, df1, dgu1, va0 = ffn_bwd(dx1, xs, f1, g1_, u1, vecs[0], f1_gu, f1_dn, 0.5, "ffn1_bwd")
    G["ffn1_w_gu"] = mm_tn(h1, dgu1, "dw_ffn1_gu", 512, 1408, 512)
    G["ffn1_w_down"] = mm_tn(a1, df1, "dw_ffn1_down", 1408, D, 512)

    va1 = va_out + va_in
    vas = (va0, va1, va2)
    dmod = jnp.stack([v[2:5] for v in vas])
    part = {"b_ada": dmod, "norm_pre": jnp.stack([v[0] for v in vas]), "norm_post": jnp.stack([v[1] for v in vas]),
            "rel_bias": bias_grad(db, "bias_grad")[:, :257], "conv_w": v_lru[0:4], "conv_b": v_lru[4],
            "lru_wa": _diag_blocks(dwa_bd), "lru_ba": v_lru[5], "lru_wx": _diag_blocks(dwx_bd), "lru_bx": v_lru[6],
            "lru_lambda": v_lru[7]}
    full_shapes = {"b_ada": (9 * D,), "norm_pre": (3, D), "norm_post": (3, D), "rel_bias": (8, 257),
                   "conv_w": (4, D), "conv_b": (D,), "lru_wa": (16, 64, 64), "lru_ba": (D,),
                   "lru_wx": (16, 64, 64), "lru_bx": (D,), "lru_lambda": (D,)}
    g3 = ag_small(_pack([part[n] for n in SMALL], 1232), "ag_small_grads")
    red = dict(zip(SMALL, _unpack(sum_lead(g3, "sum_small_grads"), [full_shapes[n] for n in SMALL])))
    cols = lambda a: lax.dynamic_slice(a, (0, p * 256), (a.shape[0], 256))
    grads = {"b_ada": red["b_ada"][None], "norm_pre": cols(red["norm_pre"])[None],
             "norm_post": cols(red["norm_post"])[None], "rel_bias": red["rel_bias"][None],
             "conv_w": cols(red["conv_w"])[None], "conv_b": red["conv_b"][None], "lru_wa": red["lru_wa"][None],
             "lru_ba": red["lru_ba"][None], "lru_wx": red["lru_wx"][None], "lru_bx": red["lru_bx"][None],
             "lru_lambda": red["lru_lambda"][None]}

    dmod_all = g3[:, :72].reshape(8, 9 * D)
    dmod_cols = jnp.pad(lax.dynamic_slice(dmod_all, (0, p * 2304), (8, 2304)), ((0, 120), (0, 0)))
    c_all_t = jnp.pad(c_all.T, ((0, 0), (0, 120)))
    grads["w_ada"] = ada_bwd(c_all_t, dmod_cols, "ada_bwd")[None]

    for n, kind, R, C in BIG:
        got = pair_push(G[n], kind, c_arr, "rs_push_" + n)
        by_chip = pair_add_scatter(G[n], got, kind, cp_arr, "rs_scatter_" + n)
        grads[n] = sum_share(by_chip, kind, R, C, "rs_sum_share_" + n)[None]

    delta, new_m, new_v = {}, {}, {}
    for n in ("w_ada",) + tuple(b[0] for b in BIG):
        shp = W[n].shape
        d_, m_, v_ = adamw(W[n][0], grads[n][0], M[n][0], V[n][0], "adamw_" + n)
        delta[n], new_m[n], new_v[n] = d_.reshape(shp), m_.reshape(shp), v_.reshape(shp)
    packed = [_pack([src[n] for n in SMALL], 1168) for src in (W, grads, M, V)]
    outs = adamw(*packed, "adamw_small")
    for dst, blk in zip((delta, new_m, new_v), outs):
        for n, a in zip(SMALL, _unpack(blk, [W[n].shape for n in SMALL])):
            dst[n] = a

    return (loss, dx0[None], *[grads[n] for n in WEIGHTS], *[delta[n] for n in WEIGHTS],
            *[new_m[n] for n in WEIGHTS], *[new_v[n] for n in WEIGHTS])
```

```python
import functools

import numpy as np
import jax
import jax.numpy as jnp
from jax import lax
from jax.experimental import pallas as pl
from jax.experimental.pallas import tpu as pltpu

F32 = jnp.float32
BF16 = jnp.bfloat16

D = 1024
FF = 2816
PW = 5632
HP = 128
CHUNK = 64
WIN = 640
TQ = 512
EPS = 1e-6
NEG = -1e30
LRU_C = 8.0
N_DEV = 8
VMEM_LIMIT = 50 * 1024 * 1024

ADAM_LR, ADAM_B1, ADAM_B2, ADAM_EPS, ADAM_WD, ADAM_STEP = 0.001, 0.9, 0.999, 1e-08, 0.01, 10

MESH = pl.DeviceIdType.MESH
ANY = pl.BlockSpec(memory_space=pl.ANY)


def _cp(*sem):
    return pltpu.CompilerParams(dimension_semantics=tuple(sem), vmem_limit_bytes=VMEM_LIMIT)


def _dot(a, b):
    return jnp.dot(a, b, preferred_element_type=F32)


def _dot_nt(a, b):
    return lax.dot_general(a, b, (((1,), (1,)), ((), ())), preferred_element_type=F32)


def _dot_tn(a, b):
    return lax.dot_general(a, b, (((0,), (0,)), ((), ())), preferred_element_type=F32)


def _mean(v):
    return jnp.mean(v, axis=-1, keepdims=True)


def _colsum(v):
    return jnp.sum(v, axis=0, keepdims=True)


def _sigmoid(v):
    return jax.nn.sigmoid(v)


def _expm1(v):
    small = v * (1.0 + v * 0.5 * (1.0 + v * (1.0 / 3.0) * (1.0 + v * 0.25 * (1.0 + v * 0.2 * (
        1.0 + v * (1.0 / 6.0) * (1.0 + v * (1.0 / 7.0)))))))
    return jnp.where(jnp.abs(v) < 0.25, small, jnp.exp(v) - 1.0)


_GK = 0.7978845608028654


def _gelu(v):
    t = jnp.tanh(_GK * (v + 0.044715 * v * v * v))
    return 0.5 * v * (1.0 + t)


def _gelu_grad(v):
    t = jnp.tanh(_GK * (v + 0.044715 * v * v * v))
    return 0.5 * (1.0 + t) + 0.5 * v * (1.0 - t * t) * _GK * (1.0 + 3.0 * 0.044715 * v * v)


def _pre_norm(xv, vec_ref):
    r = lax.rsqrt(_mean(xv * xv) + EPS)
    n = xv * r * vec_ref[0:1, :]
    return n * (1.0 + vec_ref[3:4, :]) + vec_ref[2:3, :]


def _pre_norm_bwd(dh, xv, dres, vec_ref, vacc_ref):
    r = lax.rsqrt(_mean(xv * xv) + EPS)
    xh = xv * r
    n = xh * vec_ref[0:1, :]
    vacc_ref[2:3, :] += _colsum(dh)
    vacc_ref[3:4, :] += _colsum(dh * n)
    dn = dh * (1.0 + vec_ref[3:4, :])
    vacc_ref[0:1, :] += _colsum(dn * xh)
    dxh = dn * vec_ref[0:1, :]
    return r * (dxh - xh * _mean(dxh * xh)) + dres


def _post_norm_bwd(dxo, fv, res, vec_ref, vacc_ref):
    rf = lax.rsqrt(_mean(fv * fv) + EPS)
    fh = fv * rf
    gp = vec_ref[1:2, :]
    vacc_ref[4:5, :] += _colsum(res * dxo * (fh * gp))
    dy = (res * vec_ref[4:5, :]) * dxo
    vacc_ref[1:2, :] += _colsum(dy * fh)
    dfn = dy * gp
    return rf * (dfn - fh * _mean(dfn * fh))


def ffn_fwd(x, vec, w_gu, w_dn, res, name, tm=1024, tf=256):
    S = x.shape[0]
    tm = min(tm, S)
    nf = FF // tf

    def body(x_ref, vec_ref, wg_ref, wu_ref, wd_ref, xo_ref, h_ref, g_ref, u_ref, a_ref, f_ref, hs, acc):
        j = pl.program_id(1)

        @pl.when(j == 0)
        def _():
            h = _pre_norm(x_ref[...], vec_ref).astype(BF16)
            hs[...] = h
            h_ref[...] = h
            acc[...] = jnp.zeros_like(acc)

        h = hs[...]
        g = _dot(h, wg_ref[...])
        u = _dot(h, wu_ref[...])
        g_ref[...] = g.astype(BF16)
        u_ref[...] = u.astype(BF16)
        a = (g * _sigmoid(g) * u).astype(BF16)
        a_ref[...] = a
        acc[...] += _dot(a, wd_ref[...])

        @pl.when(j == nf - 1)
        def _():
            f = acc[...]
            f_ref[...] = f
            y = f * lax.rsqrt(_mean(f * f) + EPS) * vec_ref[1:2, :]
            xo_ref[...] = x_ref[...] + (res * vec_ref[4:5, :]) * y

    row = lambda i, j: (i, 0)
    return pl.pallas_call(
        body, name=name, grid=(S // tm, nf),
        in_specs=[pl.BlockSpec((tm, D), row), pl.BlockSpec((8, D), lambda i, j: (0, 0)),
                  pl.BlockSpec((D, tf), lambda i, j: (0, j)), pl.BlockSpec((D, tf), lambda i, j: (0, j + nf)),
                  pl.BlockSpec((tf, D), lambda i, j: (j, 0))],
        out_specs=[pl.BlockSpec((tm, D), row), pl.BlockSpec((tm, D), row),
                   pl.BlockSpec((tm, tf), lambda i, j: (i, j)), pl.BlockSpec((tm, tf), lambda i, j: (i, j)),
                   pl.BlockSpec((tm, tf), lambda i, j: (i, j)), pl.BlockSpec((tm, D), row)],
        out_shape=[jax.ShapeDtypeStruct((S, D), F32), jax.ShapeDtypeStruct((S, D), BF16),
                   jax.ShapeDtypeStruct((S, FF), BF16), jax.ShapeDtypeStruct((S, FF), BF16),
                   jax.ShapeDtypeStruct((S, FF), BF16), jax.ShapeDtypeStruct((S, D), F32)],
        scratch_shapes=[pltpu.VMEM((tm, D), BF16), pltpu.VMEM((tm, D), F32)],
        compiler_params=_cp("parallel", "arbitrary"),
    )(x, vec, w_gu, w_gu, w_dn)


def ffn_bwd(dxo, x, f, g, u, vec, w_gu, w_dn, res, name, tm=512, tf=256):
    S = x.shape[0]
    tm = min(tm, S)
    nf = FF // tf

    def body(dxo_ref, x_ref, f_ref, g_ref, u_ref, vec_ref, wg_ref, wu_ref, wd_ref,
             dx_ref, df_ref, dgu_ref, vacc_ref, dfs, acc):
        i, j = pl.program_id(0), pl.program_id(1)

        @pl.when((i == 0) & (j == 0))
        def _():
            vacc_ref[...] = jnp.zeros_like(vacc_ref)

        @pl.when(j == 0)
        def _():
            df = _post_norm_bwd(dxo_ref[...], f_ref[...], res, vec_ref, vacc_ref).astype(BF16)
            dfs[...] = df
            df_ref[...] = df
            acc[...] = jnp.zeros_like(acc)

        da = _dot_nt(dfs[...], wd_ref[...])
        gv, uv = g_ref[...].astype(F32), u_ref[...].astype(F32)
        sg = _sigmoid(gv)
        dg = (da * uv * (sg * (1.0 + gv * (1.0 - sg)))).astype(BF16)
        du = (da * (gv * sg)).astype(BF16)
        dgu_ref[0] = dg
        dgu_ref[1] = du
        acc[...] += _dot_nt(dg, wg_ref[...]) + _dot_nt(du, wu_ref[...])

        @pl.when(j == nf - 1)
        def _():
            dx_ref[...] = _pre_norm_bwd(acc[...], x_ref[...], dxo_ref[...], vec_ref, vacc_ref)

    row = lambda i, j: (i, 0)
    return pl.pallas_call(
        body, name=name, grid=(S // tm, nf),
        in_specs=[pl.BlockSpec((tm, D), row), pl.BlockSpec((tm, D), row), pl.BlockSpec((tm, D), row),
                  pl.BlockSpec((tm, tf), lambda i, j: (i, j)), pl.BlockSpec((tm, tf), lambda i, j: (i, j)),
                  pl.BlockSpec((8, D), lambda i, j: (0, 0)),
                  pl.BlockSpec((D, tf), lambda i, j: (0, j)), pl.BlockSpec((D, tf), lambda i, j: (0, j + nf)),
                  pl.BlockSpec((tf, D), lambda i, j: (j, 0))],
        out_specs=[pl.BlockSpec((tm, D), row), pl.BlockSpec((tm, D), row),
                   pl.BlockSpec((2, tm, tf), lambda i, j: (0, i, j)),
                   pl.BlockSpec((8, D), lambda i, j: (0, 0))],
        out_shape=[jax.ShapeDtypeStruct((S, D), F32), jax.ShapeDtypeStruct((S, D), BF16),
                   jax.ShapeDtypeStruct((2, S, FF), BF16), jax.ShapeDtypeStruct((8, D), F32)],
        scratch_shapes=[pltpu.VMEM((tm, D), BF16), pltpu.VMEM((tm, D), F32)],
        compiler_params=_cp("arbitrary", "arbitrary"),
    )(dxo, x, f, g, u, vec, w_gu, w_gu, w_dn)


def mm_tn(a, b, name, tm, tn, tk, out_dtype=BF16, prev=None, col_off=0, n_total=None):
    S, M = a.shape
    if b.ndim == 3:
        G, _, Nf = b.shape
    else:
        G, Nf = 1, b.shape[1]
    N = G * Nf
    n_total = N if n_total is None else n_total
    tk = min(tk, S)
    nbf = Nf // tn
    nk = S // tk
    ob = col_off // tn

    def body(*refs):
        a_ref, b_ref = refs[0], refs[1]
        o_ref, acc = refs[-2], refs[-1]
        k = pl.program_id(2)

        @pl.when(k == 0)
        def _():
            acc[...] = jnp.zeros_like(acc)

        acc[...] += _dot_tn(a_ref[...], b_ref[...])

        @pl.when(k == nk - 1)
        def _():
            o_ref[...] = acc[...].astype(out_dtype)

    if b.ndim == 3:
        b_spec = pl.BlockSpec((None, tk, tn), lambda i, j, k: (j // nbf, k, j % nbf))
    else:
        b_spec = pl.BlockSpec((tk, tn), lambda i, j, k: (k, j))
    in_specs = [pl.BlockSpec((tk, tm), lambda i, j, k: (k, i)), b_spec]
    args = [a, b]
    aliases = {}
    if prev is not None:
        in_specs.append(ANY)
        args.append(prev)
        aliases = {2: 0}
    return pl.pallas_call(
        body, name=name, grid=(M // tm, N // tn, nk),
        in_specs=in_specs,
        out_specs=pl.BlockSpec((tm, tn), lambda i, j, k: (i, j + ob)),
        out_shape=jax.ShapeDtypeStruct((M, n_total), out_dtype),
        scratch_shapes=[pltpu.VMEM((tm, tn), F32)],
        input_output_aliases=aliases,
        compiler_params=_cp("parallel", "parallel", "arbitrary"),
    )(*args)


def proj_fwd(x, vec, w_in, name, tm=1024, tn=512):
    S = x.shape[0]
    tm = min(tm, S)
    nq = 1536 // tn

    def body(x_ref, vec_ref, w_ref, h_ref, qkv_ref, rest_ref, hs):
        j = pl.program_id(1)

        @pl.when(j == 0)
        def _():
            h = _pre_norm(x_ref[...], vec_ref).astype(BF16)
            hs[...] = h
            h_ref[...] = h

        r = _dot(hs[...], w_ref[...])

        @pl.when(j < nq)
        def _():
            qkv_ref[...] = r.astype(BF16)

        @pl.when(j >= nq)
        def _():
            rest_ref[...] = r

    row = lambda i, j: (i, 0)
    return pl.pallas_call(
        body, name=name, grid=(S // tm, PW // tn),
        in_specs=[pl.BlockSpec((tm, D), row), pl.BlockSpec((8, D), lambda i, j: (0, 0)),
                  pl.BlockSpec((D, tn), lambda i, j: (0, j))],
        out_specs=[pl.BlockSpec((tm, D), row),
                   pl.BlockSpec((tm, tn), lambda i, j: (i, jnp.minimum(j, nq - 1))),
                   pl.BlockSpec((tm, tn), lambda i, j: (i, jnp.maximum(j - nq, 0)))],
        out_shape=[jax.ShapeDtypeStruct((S, D), BF16), jax.ShapeDtypeStruct((S, 1536), BF16),
                   jax.ShapeDtypeStruct((S, 4096), F32)],
        scratch_shapes=[pltpu.VMEM((tm, D), BF16)],
        compiler_params=_cp("parallel", "arbitrary"),
    )(x, vec, w_in)


def proj_bwd(dq, dkv, dxr, d3, w_in, x, dxo, vec, name, tm=1024, tk=512):
    S = x.shape[0]
    tm = min(tm, S)
    nk = PW // tk

    def body(dq_ref, dkv_ref, dxr_ref, d3_ref, w_ref, x_ref, dxo_ref, vec_ref, dx_ref, vacc_ref, acc):
        i, j = pl.program_id(0), pl.program_id(1)

        @pl.when((i == 0) & (j == 0))
        def _():
            vacc_ref[...] = jnp.zeros_like(vacc_ref)

        @pl.when(j == 0)
        def _():
            acc[...] = _dot_nt(dq_ref[...], w_ref[...])

        @pl.when((j >= 1) & (j < 3))
        def _():
            acc[...] += _dot_nt(dkv_ref[...], w_ref[...])

        @pl.when((j >= 3) & (j < 5))
        def _():
            acc[...] += _dot_nt(dxr_ref[...], w_ref[...])

        @pl.when(j >= 5)
        def _():
            acc[...] += _dot_nt(d3_ref[...], w_ref[...])

        @pl.when(j == nk - 1)
        def _():
            dx_ref[...] = _pre_norm_bwd(acc[...], x_ref[...], dxo_ref[...], vec_ref, vacc_ref)

    row = lambda i, j: (i, 0)
    return pl.pallas_call(
        body, name=name, grid=(S // tm, nk),
        in_specs=[pl.BlockSpec((None, tm, tk), lambda i, j: (0, i, 0)),
                  pl.BlockSpec((None, tm, tk), lambda i, j: (jnp.clip(j - 1, 0, 1), i, 0)),
                  pl.BlockSpec((tm, tk), lambda i, j: (i, jnp.clip(j - 3, 0, 1))),
                  pl.BlockSpec((None, tm, tk), lambda i, j: (jnp.clip(j - 5, 0, 5) // 2, i, jnp.clip(j - 5, 0, 5) % 2)),
                  pl.BlockSpec((D, tk), lambda i, j: (0, j)),
                  pl.BlockSpec((tm, D), row), pl.BlockSpec((tm, D), row),
                  pl.BlockSpec((8, D), lambda i, j: (0, 0))],
        out_specs=[pl.BlockSpec((tm, D), row), pl.BlockSpec((8, D), lambda i, j: (0, 0))],
        out_shape=[jax.ShapeDtypeStruct((S, D), F32), jax.ShapeDtypeStruct((8, D), F32)],
        scratch_shapes=[pltpu.VMEM((tm, D), F32)],
        compiler_params=_cp("arbitrary", "arbitrary"),
    )(dq, dkv, dxr, d3, w_in, x, dxo, vec)


def _attn_probs(qm, ka, bias_h, i, grp):
    s = _dot_nt(qm, ka) + bias_h
    col = lax.broadcasted_iota(jnp.int32, s.shape, 1)
    first_key = jnp.where(i == 0, 512 - 128 * grp, 0)
    s = jnp.where(col >= first_key, s, NEG)
    e = jnp.exp(s - jnp.max(s, axis=-1, keepdims=True))
    return e / jnp.sum(e, axis=-1, keepdims=True)


def attn_fwd(qkv, bias, name):
    S = qkv.shape[0]
    nb = S // TQ

    def body(q_ref, kp_ref, kc_ref, vp_ref, vc_ref, b_ref, o_ref, kw, vw):
        i = pl.program_id(1)
        kw[0:TQ, :] = kp_ref[...]
        kw[TQ:2 * TQ, :] = kc_ref[...]
        vw[0:TQ, :] = vp_ref[...]
        vw[TQ:2 * TQ, :] = vc_ref[...]
        lane = lax.broadcasted_iota(jnp.int32, (1, HP), 1)
        zero = jnp.zeros((), BF16)

        def group(a, carry):
            r0 = pl.multiple_of(a * 128, 128)
            qa = q_ref[pl.ds(r0, 128), :] * jnp.asarray(0.125, BF16)
            ka = kw[pl.ds(r0, WIN), :]
            va = vw[pl.ds(r0, WIN), :]
            o = jnp.zeros((128, HP), F32)
            for hh in range(2):
                msk = (lane < 64) if hh == 0 else (lane >= 64)
                p = _attn_probs(jnp.where(msk, qa, zero), ka, b_ref[hh], i, a)
                o += _dot(p.astype(BF16), jnp.where(msk, va, zero))
            o_ref[pl.ds(r0, 128), :] = o.astype(BF16)
            return carry

        lax.fori_loop(0, TQ // 128, group, 0, unroll=True)

    prev = lambda h, i: (jnp.maximum(i - 1, 0), 0)
    return pl.pallas_call(
        body, name=name, grid=(4, nb),
        in_specs=[pl.BlockSpec((TQ, HP), lambda h, i: (i, h)),
                  pl.BlockSpec((TQ, HP), lambda h, i: (jnp.maximum(i - 1, 0), 4 + h)),
                  pl.BlockSpec((TQ, HP), lambda h, i: (i, 4 + h)),
                  pl.BlockSpec((TQ, HP), lambda h, i: (jnp.maximum(i - 1, 0), 8 + h)),
                  pl.BlockSpec((TQ, HP), lambda h, i: (i, 8 + h)),
                  pl.BlockSpec((2, 128, WIN), lambda h, i: (h, 0, 0))],
        out_specs=pl.BlockSpec((TQ, HP), lambda h, i: (i, h)),
        out_shape=jax.ShapeDtypeStruct((S, 512), BF16),
        scratch_shapes=[pltpu.VMEM((2 * TQ, HP), BF16), pltpu.VMEM((2 * TQ, HP), BF16)],
        compiler_params=_cp("parallel", "arbitrary"),
    )(qkv, qkv, qkv, qkv, qkv, bias)


def attn_bwd(qkv, do, bias, name):
    S = qkv.shape[0]
    nb = S // TQ

    def body(q_ref, kp_ref, kc_ref, vp_ref, vc_ref, do_ref, b_ref, dqkv_ref, db_ref, dkv_ref, kw, vw, ak, av):
        i = pl.program_id(1)

        @pl.when(i == 0)
        def _():
            db_ref[...] = jnp.zeros_like(db_ref)
            ak[...] = jnp.zeros_like(ak)
            av[...] = jnp.zeros_like(av)

        @pl.when(i > 0)
        def _():
            ak[0:TQ, :] = ak[TQ:2 * TQ, :]
            av[0:TQ, :] = av[TQ:2 * TQ, :]
            ak[TQ:2 * TQ, :] = jnp.zeros((TQ, HP), F32)
            av[TQ:2 * TQ, :] = jnp.zeros((TQ, HP), F32)

        @pl.when(i < nb)
        def _():
            kw[0:TQ, :] = kp_ref[...]
            kw[TQ:2 * TQ, :] = kc_ref[...]
            vw[0:TQ, :] = vp_ref[...]
            vw[TQ:2 * TQ, :] = vc_ref[...]
            lane = lax.broadcasted_iota(jnp.int32, (1, HP), 1)
            zero = jnp.zeros((), BF16)

            def group(a, carry):
                r0 = pl.multiple_of(a * 128, 128)
                qa = q_ref[pl.ds(r0, 128), :] * jnp.asarray(0.125, BF16)
                doa = do_ref[pl.ds(r0, 128), :]
                ka = kw[pl.ds(r0, WIN), :]
                va = vw[pl.ds(r0, WIN), :]
                dq = jnp.zeros((128, HP), F32)
                for hh in range(2):
                    msk = (lane < 64) if hh == 0 else (lane >= 64)
                    qm = jnp.where(msk, qa, zero)
                    dom = jnp.where(msk, doa, zero)
                    p = _attn_probs(qm, ka, b_ref[hh], i, a)
                    dp = _dot_nt(dom, va)
                    ds = p * (dp - jnp.sum(p * dp, axis=-1, keepdims=True))
                    db_ref[hh] += ds
                    dsb = ds.astype(BF16)
                    dq += jnp.where(msk, _dot(dsb, ka), 0.0)
                    ak[pl.ds(r0, WIN), :] += _dot_tn(dsb, qm)
                    av[pl.ds(r0, WIN), :] += _dot_tn(p.astype(BF16), dom)
                dqkv_ref[0, pl.ds(r0, 128), :] = (dq * 0.125).astype(BF16)
                return carry

            lax.fori_loop(0, TQ // 128, group, 0, unroll=True)

        @pl.when(i > 0)
        def _():
            dkv_ref[0] = ak[0:TQ, :].astype(BF16)
            dkv_ref[1] = av[0:TQ, :].astype(BF16)

    cur = lambda i: jnp.minimum(i, nb - 1)
    prv = lambda i: jnp.clip(i - 1, 0, nb - 1)
    dq, db, dkv = pl.pallas_call(
        body, name=name, grid=(4, nb + 1),
        in_specs=[pl.BlockSpec((TQ, HP), lambda h, i: (cur(i), h)),
                  pl.BlockSpec((TQ, HP), lambda h, i: (prv(i), 4 + h)),
                  pl.BlockSpec((TQ, HP), lambda h, i: (cur(i), 4 + h)),
                  pl.BlockSpec((TQ, HP), lambda h, i: (prv(i), 8 + h)),
                  pl.BlockSpec((TQ, HP), lambda h, i: (cur(i), 8 + h)),
                  pl.BlockSpec((TQ, HP), lambda h, i: (cur(i), h)),
                  pl.BlockSpec((2, 128, WIN), lambda h, i: (h, 0, 0))],
        out_specs=[pl.BlockSpec((1, TQ, HP), lambda h, i: (0, cur(i), h)),
                   pl.BlockSpec((2, 128, WIN), lambda h, i: (h, 0, 0)),
                   pl.BlockSpec((2, TQ, HP), lambda h, i: (0, prv(i), h))],
        out_shape=[jax.ShapeDtypeStruct((1, S, 512), BF16), jax.ShapeDtypeStruct((8, 128, WIN), F32),
                   jax.ShapeDtypeStruct((2, S, 512), BF16)],
        scratch_shapes=[pltpu.VMEM((2 * TQ, HP), BF16), pltpu.VMEM((2 * TQ, HP), BF16),
                        pltpu.VMEM((2 * TQ, HP), F32), pltpu.VMEM((2 * TQ, HP), F32)],
        compiler_params=_cp("parallel", "arbitrary"),
    )(qkv, qkv, qkv, qkv, qkv, do, bias)
    return dq, db, dkv


def bias_grad(db, name):
    def body(db_ref, o_ref):
        r = lax.broadcasted_iota(jnp.int32, (128, 128), 0)
        c = lax.broadcasted_iota(jnp.int32, (128, 128), 1)
        flip = (r + c == 127).astype(BF16)
        lane = lax.broadcasted_iota(jnp.int32, (16, 384), 1)
        src = lax.broadcasted_iota(jnp.int32, (128, 384), 0)
        dst = lax.broadcasted_iota(jnp.int32, (128, 384), 1)

        def split_dot(v, m):
            hi = v.astype(BF16)
            r1 = v - hi.astype(F32)
            mid = r1.astype(BF16)
            lo = (r1 - mid.astype(F32)).astype(BF16)
            return _dot(hi, m) + _dot(mid, m) + _dot(lo, m)

        def diag_sums(w):
            y = pltpu.roll(split_dot(w, flip), 0, 1, stride=1, stride_axis=0)
            return jnp.broadcast_to(_colsum(y), (16, 128))

        w4 = db_ref[0, :, 512:640]
        w3 = db_ref[0, :, 384:512]
        far = jnp.sum(db_ref[0, :, 0:384]) + jnp.sum(jnp.where(r >= c, w3, 0.0))
        lo4 = diag_sums(jnp.where(r >= c, w4, 0.0))
        up4 = diag_sums(jnp.where(r < c, w4, 0.0))
        up3 = diag_sums(jnp.where(r < c, w3, 0.0))
        p_lo4 = (dst == 128 + (src + 1) % 128).astype(BF16)
        p_up4 = ((dst == src + 1) & (src < 127)).astype(BF16)
        p_up3 = ((dst == src + 129) & (src < 127)).astype(BF16)
        out = split_dot(lo4, p_lo4) + split_dot(up4, p_up4) + split_dot(up3, p_up3)
        o_ref[0] = out + jnp.where(lane == 256, far, 0.0)

    return pl.pallas_call(
        body, name=name, grid=(8,),
        in_specs=[pl.BlockSpec((1, 128, WIN), lambda h: (h, 0, 0))],
        out_specs=pl.BlockSpec((1, 16, 384), lambda h: (h, 0, 0)),
        out_shape=jax.ShapeDtypeStruct((8, 16, 384), F32),
        compiler_params=_cp("parallel"),
    )(db)[:, 0, :]


LT = 256
LC = 512


def _lru_gates(xs, pv_ref, wa_ref, wx_ref, tl):
    xc = (pv_ref[4:5, :] + pv_ref[3:4, :] * xs[pl.ds(8, tl), :] + pv_ref[2:3, :] * xs[pl.ds(7, tl), :]
          + pv_ref[1:2, :] * xs[pl.ds(6, tl), :] + pv_ref[0:1, :] * xs[pl.ds(5, tl), :])
    xcb = xc.astype(BF16)
    pa = jnp.concatenate([_dot(xcb[:, 0:256], wa_ref[0]), _dot(xcb[:, 256:512], wa_ref[1])], axis=1)
    px = jnp.concatenate([_dot(xcb[:, 0:256], wx_ref[0]), _dot(xcb[:, 256:512], wx_ref[1])], axis=1)
    r = _sigmoid(pa + pv_ref[5:6, :])
    ig = _sigmoid(px + pv_ref[6:7, :])
    z = -pv_ref[7:8, :]
    sp = jnp.maximum(z, 0.0) + jnp.log1p(jnp.exp(-jnp.abs(z)))
    log_a = (-LRU_C * r) * sp
    a = jnp.exp(log_a)
    mult = jnp.sqrt(-_expm1(2.0 * log_a))
    return xc, xcb, r, ig, sp, a, mult


def lru_fwd(rest, pvec, wa, wx, name):
    S = rest.shape[0]
    tl = min(LT, S)
    nt = S // tl

    def body(xr_ref, halo_ref, yr_ref, pv_ref, wa_ref, wx_ref, h_ref, hg_ref, xs, a_s, u_s, h_s, carry):
        ti = pl.program_id(1)

        @pl.when(ti == 0)
        def _():
            carry[...] = jnp.zeros_like(carry)

        xs[0:8, :] = jnp.where(ti > 0, halo_ref[...], 0.0)
        xs[pl.ds(8, tl), :] = xr_ref[...]
        xc, _, _, ig, _, a, mult = _lru_gates(xs, pv_ref, wa_ref, wx_ref, tl)
        a_s[...] = a
        u_s[...] = mult * (ig * xc)
        row = lax.broadcasted_iota(jnp.int32, (8, LC), 0)

        def blk(bi, c):
            o = pl.multiple_of(bi * 8, 8)
            av = a_s[pl.ds(o, 8), :]
            bv = u_s[pl.ds(o, 8), :]
            for d in (1, 2, 4):
                a_sh = pltpu.roll(av, d, 0)
                b_sh = pltpu.roll(bv, d, 0)
                m = row >= d
                bv = jnp.where(m, av * b_sh + bv, bv)
                av = jnp.where(m, av * a_sh, av)
            hv = bv + av * c
            h_s[pl.ds(o, 8), :] = hv
            return hv[7:8, :]

        carry[...] = lax.fori_loop(0, tl // 8, blk, carry[...])
        h = h_s[...]
        h_ref[...] = h
        hg_ref[...] = (h * _gelu(yr_ref[...])).astype(BF16)

    hb = tl // 8
    return pl.pallas_call(
        body, name=name, grid=(2, nt),
        in_specs=[pl.BlockSpec((tl, LC), lambda c, t: (t, c)),
                  pl.BlockSpec((8, LC), lambda c, t: (jnp.maximum(t * hb - 1, 0), c)),
                  pl.BlockSpec((tl, LC), lambda c, t: (t, 2 + c)),
                  pl.BlockSpec((8, LC), lambda c, t: (0, c)),
                  pl.BlockSpec((2, 256, 256), lambda c, t: (c, 0, 0)),
                  pl.BlockSpec((2, 256, 256), lambda c, t: (c, 0, 0))],
        out_specs=[pl.BlockSpec((tl, LC), lambda c, t: (t, c)), pl.BlockSpec((tl, LC), lambda c, t: (t, c))],
        out_shape=[jax.ShapeDtypeStruct((S, D), F32), jax.ShapeDtypeStruct((S, D), BF16)],
        scratch_shapes=[pltpu.VMEM((tl + 8, LC), F32), pltpu.VMEM((tl, LC), F32), pltpu.VMEM((tl, LC), F32),
                        pltpu.VMEM((tl, LC), F32), pltpu.VMEM((1, LC), F32)],
        compiler_params=_cp("parallel", "arbitrary"),
    )(rest, rest, rest, pvec, wa, wx)


def lru_bwd(dh, h, rest, pvec, wa, wx, name):
    S = rest.shape[0]
    tl = min(LT, S)
    nt = S // tl

    def body(dh_ref, h_ref, hhalo_ref, xr_ref, xhalo_ref, pv_ref, wa_ref, wx_ref,
             dxr_ref, vacc_ref, dwa_ref, dwx_ref,
             xs, hs, a_s, ash_s, b_s, lam_s, dxe, anext, lnext, dxnext):
        ti = pl.program_id(1)
        tr = nt - 1 - ti

        @pl.when(ti == 0)
        def _():
            anext[...] = jnp.zeros_like(anext)
            lnext[...] = jnp.zeros_like(lnext)
            dxnext[...] = jnp.zeros_like(dxnext)
            vacc_ref[...] = jnp.zeros_like(vacc_ref)
            dwa_ref[...] = jnp.zeros_like(dwa_ref)
            dwx_ref[...] = jnp.zeros_like(dwx_ref)

        xs[0:8, :] = jnp.where(tr > 0, xhalo_ref[...], 0.0)
        xs[pl.ds(8, tl), :] = xr_ref[...]
        xc, xcb, r, ig, sp, a, mult = _lru_gates(xs, pv_ref, wa_ref, wx_ref, tl)

        a_s[pl.ds(0, tl), :] = a
        a_s[pl.ds(tl, 8), :] = jnp.broadcast_to(anext[...], (8, LC))
        ash_s[...] = a_s[pl.ds(1, tl), :]
        b_s[...] = dh_ref[...]
        row = lax.broadcasted_iota(jnp.int32, (8, LC), 0)

        def blk(k, c):
            o = pl.multiple_of((tl // 8 - 1 - k) * 8, 8)
            av = ash_s[pl.ds(o, 8), :]
            bv = b_s[pl.ds(o, 8), :]
            for d in (1, 2, 4):
                a_sh = pltpu.roll(av, 8 - d, 0)
                b_sh = pltpu.roll(bv, 8 - d, 0)
                m = row < 8 - d
                bv = jnp.where(m, bv + av * b_sh, bv)
                av = jnp.where(m, av * a_sh, av)
            lv = bv + av * c
            lam_s[pl.ds(o, 8), :] = lv
            return lv[0:1, :]

        lnext[...] = lax.fori_loop(0, tl // 8, blk, lnext[...])
        anext[...] = a[0:1, :]
        lam = lam_s[...]

        hs[0:8, :] = jnp.where(tr > 0, hhalo_ref[...], 0.0)
        hs[pl.ds(8, tl), :] = h_ref[...]
        d_a = lam * hs[pl.ds(7, tl), :]
        d_mult = lam * (ig * xc)
        d_ig = lam * mult * xc
        dxc = lam * mult * ig
        d_log_a = d_a * a - d_mult * (a * a) / mult
        d_r = d_log_a * (-LRU_C * sp)
        vacc_ref[7:8, :] += _colsum(d_log_a * (-LRU_C * r)) * (-_sigmoid(-pv_ref[7:8, :]))
        d_pa = d_r * r * (1.0 - r)
        d_px = d_ig * ig * (1.0 - ig)
        vacc_ref[5:6, :] += _colsum(d_pa)
        vacc_ref[6:7, :] += _colsum(d_px)
        dpa = d_pa.astype(BF16)
        dpx = d_px.astype(BF16)
        back = []
        for g in range(2):
            sl = slice(256 * g, 256 * g + 256)
            dwa_ref[g] += _dot_tn(xcb[:, sl], dpa[:, sl])
            dwx_ref[g] += _dot_tn(xcb[:, sl], dpx[:, sl])
            back.append(_dot_nt(dpa[:, sl], wa_ref[g]) + _dot_nt(dpx[:, sl], wx_ref[g]))
        dxc = dxc + jnp.concatenate(back, axis=1)
        vacc_ref[4:5, :] += _colsum(dxc)
        for k in range(4):
            vacc_ref[k:k + 1, :] += _colsum(dxc * xs[pl.ds(5 + k, tl), :])
        dxe[pl.ds(0, tl), :] = dxc
        dxe[pl.ds(tl, 8), :] = dxnext[...]
        dxr = (pv_ref[3:4, :] * dxc + pv_ref[2:3, :] * dxe[pl.ds(1, tl), :]
               + pv_ref[1:2, :] * dxe[pl.ds(2, tl), :] + pv_ref[0:1, :] * dxe[pl.ds(3, tl), :])
        dxr_ref[...] = dxr.astype(BF16)
        dxnext[...] = dxc[0:8, :]

    hb = tl // 8
    rev = lambda t: nt - 1 - t
    halo = lambda t: jnp.maximum(rev(t) * hb - 1, 0)
    big = lambda: pltpu.VMEM((tl + 8, LC), F32)
    til = lambda: pltpu.VMEM((tl, LC), F32)
    return pl.pallas_call(
        body, name=name, grid=(2, nt),
        in_specs=[pl.BlockSpec((tl, LC), lambda c, t: (rev(t), c)),
                  pl.BlockSpec((tl, LC), lambda c, t: (rev(t), c)),
                  pl.BlockSpec((8, LC), lambda c, t: (halo(t), c)),
                  pl.BlockSpec((tl, LC), lambda c, t: (rev(t), c)),
                  pl.BlockSpec((8, LC), lambda c, t: (halo(t), c)),
                  pl.BlockSpec((8, LC), lambda c, t: (0, c)),
                  pl.BlockSpec((2, 256, 256), lambda c, t: (c, 0, 0)),
                  pl.BlockSpec((2, 256, 256), lambda c, t: (c, 0, 0))],
        out_specs=[pl.BlockSpec((tl, LC), lambda c, t: (rev(t), c)),
                   pl.BlockSpec((8, LC), lambda c, t: (0, c)),
                   pl.BlockSpec((2, 256, 256), lambda c, t: (c, 0, 0)),
                   pl.BlockSpec((2, 256, 256), lambda c, t: (c, 0, 0))],
        out_shape=[jax.ShapeDtypeStruct((S, D), BF16), jax.ShapeDtypeStruct((8, D), F32),
                   jax.ShapeDtypeStruct((4, 256, 256), F32), jax.ShapeDtypeStruct((4, 256, 256), F32)],
        scratch_shapes=[big(), big(), big(), til(), til(), til(), big(),
                        pltpu.VMEM((1, LC), F32), pltpu.VMEM((1, LC), F32), pltpu.VMEM((8, LC), F32)],
        compiler_params=_cp("parallel", "arbitrary"),
    )(dh, h, h, rest, rest, pvec, wa, wx)


def mix_out_fwd(x, ao, hg, rest, vec, w_att_o, w_rec_o, w_out, name, tm=256):
    S = x.shape[0]
    tm = min(tm, S)

    def body(x_ref, ao_ref, hg_ref, ga_ref, gr_ref, vec_ref, wa_ref, wr_ref, wo_ref,
             xo_ref, att_ref, rec_ref, mg_ref, f_ref):
        att = _dot(ao_ref[...], wa_ref[...])
        rec = _dot(hg_ref[...], wr_ref[...])
        att_ref[...] = att
        rec_ref[...] = rec
        mg = (_sigmoid(ga_ref[...]) * att + _sigmoid(gr_ref[...]) * rec).astype(BF16)
        mg_ref[...] = mg
        f = _dot(mg, wo_ref[...])
        f_ref[...] = f
        y = f * lax.rsqrt(_mean(f * f) + EPS) * vec_ref[1:2, :]
        xo_ref[...] = x_ref[...] + (1.0 * vec_ref[4:5, :]) * y

    row = lambda i: (i, 0)
    full = lambda r: pl.BlockSpec((r, D), lambda i: (0, 0))
    return pl.pallas_call(
        body, name=name, grid=(S // tm,),
        in_specs=[pl.BlockSpec((tm, D), row), pl.BlockSpec((tm, 512), row), pl.BlockSpec((tm, D), row),
                  pl.BlockSpec((tm, D), lambda i: (i, 2)), pl.BlockSpec((tm, D), lambda i: (i, 3)),
                  full(8), full(512), full(D), full(D)],
        out_specs=[pl.BlockSpec((tm, D), row)] * 5,
        out_shape=[jax.ShapeDtypeStruct((S, D), F32), jax.ShapeDtypeStruct((S, D), F32),
                   jax.ShapeDtypeStruct((S, D), F32), jax.ShapeDtypeStruct((S, D), BF16),
                   jax.ShapeDtypeStruct((S, D), F32)],
        compiler_params=_cp("parallel"),
    )(x, ao, hg, rest, rest, vec, w_att_o, w_rec_o, w_out)


def mix_out_bwd(dxo, f, att, rec, rest, h, vec, w_att_o, w_rec_o, w_out, name, tm=256):
    S = dxo.shape[0]
    tm = min(tm, S)

    def body(dxo_ref, f_ref, att_ref, rec_ref, yr_ref, ga_ref, gr_ref, h_ref, vec_ref, wa_ref, wr_ref, wo_ref,
             df_ref, da_ref, dr_ref, dao_ref, dh_ref, d3_ref, vacc_ref):
        @pl.when(pl.program_id(0) == 0)
        def _():
            vacc_ref[...] = jnp.zeros_like(vacc_ref)

        df = _post_norm_bwd(dxo_ref[...], f_ref[...], 1.0, vec_ref, vacc_ref).astype(BF16)
        df_ref[...] = df
        dm = _dot_nt(df, wo_ref[...])
        sa = _sigmoid(ga_ref[...])
        sr = _sigmoid(gr_ref[...])
        d_att = (dm * sa).astype(BF16)
        d_rec = (dm * sr).astype(BF16)
        da_ref[...] = d_att
        dr_ref[...] = d_rec
        d3_ref[1] = (dm * att_ref[...] * (sa * (1.0 - sa))).astype(BF16)
        d3_ref[2] = (dm * rec_ref[...] * (sr * (1.0 - sr))).astype(BF16)
        dao_ref[...] = _dot_nt(d_att, wa_ref[...]).astype(BF16)
        d_hg = _dot_nt(d_rec, wr_ref[...])
        yr = yr_ref[...]
        dh_ref[...] = d_hg * _gelu(yr)
        d3_ref[0] = (d_hg * h_ref[...] * _gelu_grad(yr)).astype(BF16)

    row = lambda i: (i, 0)
    full = lambda r: pl.BlockSpec((r, D), lambda i: (0, 0))
    return pl.pallas_call(
        body, name=name, grid=(S // tm,),
        in_specs=[pl.BlockSpec((tm, D), row)] * 4
        + [pl.BlockSpec((tm, D), lambda i: (i, 1)), pl.BlockSpec((tm, D), lambda i: (i, 2)),
           pl.BlockSpec((tm, D), lambda i: (i, 3)), pl.BlockSpec((tm, D), row),
           full(8), full(512), full(D), full(D)],
        out_specs=[pl.BlockSpec((tm, D), row)] * 3
        + [pl.BlockSpec((tm, 512), row), pl.BlockSpec((tm, D), row),
           pl.BlockSpec((3, tm, D), lambda i: (0, i, 0)), pl.BlockSpec((8, D), lambda i: (0, 0))],
        out_shape=[jax.ShapeDtypeStruct((S, D), BF16)] * 3
        + [jax.ShapeDtypeStruct((S, 512), BF16), jax.ShapeDtypeStruct((S, D), F32),
           jax.ShapeDtypeStruct((3, S, D), BF16), jax.ShapeDtypeStruct((8, D), F32)],
        compiler_params=_cp("arbitrary"),
    )(dxo, f, att, rec, rest, rest, rest, h, vec, w_att_o, w_rec_o, w_out)


def loss_grad(y, tgt, name, tm=512):
    S = y.shape[0]
    tm = min(tm, S)
    nt = S // tm

    def body(y_ref, t_ref, dy_ref, l_ref, acc):
        i = pl.program_id(0)

        @pl.when(i == 0)
        def _():
            acc[...] = jnp.zeros_like(acc)

        d = y_ref[...] - t_ref[...]
        dy_ref[...] = d * (1.0 / D)
        acc[...] += _colsum(d * d)

        @pl.when(i == nt - 1)
        def _():
            l_ref[...] = jnp.broadcast_to(0.5 * jnp.sum(acc[...]) * (1.0 / D), (8, 128))

    return pl.pallas_call(
        body, name=name, grid=(nt,),
        in_specs=[pl.BlockSpec((tm, D), lambda i: (i, 0))] * 2,
        out_specs=[pl.BlockSpec((tm, D), lambda i: (i, 0)), pl.BlockSpec((8, 128), lambda i: (0, 0))],
        out_shape=[jax.ShapeDtypeStruct((S, D), F32), jax.ShapeDtypeStruct((8, 128), F32)],
        scratch_shapes=[pltpu.VMEM((1, D), F32)],
        compiler_params=_cp("arbitrary"),
    )(y, tgt)


def ada_fwd(c_all, w_ada, b_ada, name, tn=768):
    n = w_ada.shape[1]

    def body(c_ref, w_ref, b_ref, o_ref):
        cv = c_ref[...]
        ca = (cv * _sigmoid(cv)).astype(BF16)
        o_ref[...] = _dot(ca, w_ref[...].astype(BF16)) + b_ref[...]

    return pl.pallas_call(
        body, name=name, grid=(n // tn,),
        in_specs=[pl.BlockSpec((8, D), lambda j: (0, 0)), pl.BlockSpec((D, tn), lambda j: (0, j)),
                  pl.BlockSpec((1, tn), lambda j: (0, j))],
        out_specs=pl.BlockSpec((8, tn), lambda j: (0, j)),
        out_shape=jax.ShapeDtypeStruct((8, n), F32),
        compiler_params=_cp("parallel"),
    )(c_all, w_ada, b_ada)


def ada_bwd(c_all_t, dmod, name, tn=768):
    n = dmod.shape[1]

    def body(c_ref, d_ref, o_ref):
        cv = c_ref[...]
        ca = (cv * _sigmoid(cv)).astype(BF16)
        o_ref[...] = _dot(ca, d_ref[...].astype(BF16))

    return pl.pallas_call(
        body, name=name, grid=(n // tn,),
        in_specs=[pl.BlockSpec((D, 128), lambda j: (0, 0)), pl.BlockSpec((128, tn), lambda j: (0, j))],
        out_specs=pl.BlockSpec((D, tn), lambda j: (0, j)),
        out_shape=jax.ShapeDtypeStruct((D, n), F32),
        compiler_params=_cp("parallel"),
    )(c_all_t, dmod)


def _row_tile(rows, cols, itemsize=4, budget=1536 * 1024):
    best = None
    for t in range(8, rows + 1, 8):
        if rows % t == 0 and t * cols * itemsize <= budget:
            best = t
    return rows if best is None else best


def sum_lead(parts, name, out_dtype=F32):
    n, R, C = parts.shape
    tr = _row_tile(R, C * n)

    def body(p_ref, o_ref):
        acc = p_ref[0].astype(F32)
        for k in range(1, n):
            acc = acc + p_ref[k].astype(F32)
        o_ref[...] = acc.astype(out_dtype)

    return pl.pallas_call(
        body, name=name, grid=(R // tr,),
        in_specs=[pl.BlockSpec((n, tr, C), lambda i: (0, i, 0))],
        out_specs=pl.BlockSpec((tr, C), lambda i: (i, 0)),
        out_shape=jax.ShapeDtypeStruct((R, C), out_dtype),
        compiler_params=_cp("parallel"),
    )(parts)


def adamw(w, g, m, v, name):
    R, C = w.shape
    tr = _row_tile(R, C * 7, budget=8 * 1024 * 1024)

    def body(w_ref, g_ref, m_ref, v_ref, d_ref, mo_ref, vo_ref):
        gv = g_ref[...]
        mn = ADAM_B1 * m_ref[...] + (1.0 - ADAM_B1) * gv
        vn = ADAM_B2 * v_ref[...] + (1.0 - ADAM_B2) * (gv * gv)
        m_hat = mn / (1.0 - ADAM_B1 ** ADAM_STEP)
        v_hat = vn / (1.0 - ADAM_B2 ** ADAM_STEP)
        d_ref[...] = -ADAM_LR * (m_hat / (jnp.sqrt(v_hat) + ADAM_EPS) + ADAM_WD * w_ref[...])
        mo_ref[...] = mn
        vo_ref[...] = vn

    spec = pl.BlockSpec((tr, C), lambda i: (i, 0))
    return pl.pallas_call(
        body, name=name, grid=(R // tr,),
        in_specs=[spec] * 4, out_specs=[spec] * 3,
        out_shape=[jax.ShapeDtypeStruct((R, C), F32)] * 3,
        compiler_params=_cp("parallel"),
    )(w, g, m, v)


def _mesh_pos():
    return lax.axis_index("x"), lax.axis_index("y"), lax.axis_index("c")


def _other_chips(mx, my):
    return [(1 - mx, my), (mx, 1 - my), (1 - mx, 1 - my)]


def ag_small(x, name):
    R = x.shape[0]

    def body(x_ref, out_ref, send_sems, recv_sems, local_sem):
        mx, my, mc = _mesh_pos()
        me, sibling = (mx, my, mc), (mx, my, 1 - mc)
        chips = _other_chips(mx, my)

        def slot(px, py, pc):
            return out_ref.at[4 * px + 2 * py + pc]

        def copy(k, block, to, src=None):
            return pltpu.make_async_remote_copy(
                src_ref=slot(*block) if src is None else src, dst_ref=slot(*block),
                send_sem=send_sems.at[k], recv_sem=recv_sems.at[k], device_id=to, device_id_type=MESH)

        mine = pltpu.make_async_copy(x_ref, slot(*me), local_sem)
        mine.start()
        first = [copy(0, me, sibling, src=x_ref)]
        first += [copy(1 + j, me, (*chip, mc), src=x_ref) for j, chip in enumerate(chips)]
        for cp in first:
            cp.start()
        passed = [copy(4 + j, (*chip, mc), sibling) for j, chip in enumerate(chips)]
        for j, chip in enumerate(chips):
            copy(1 + j, (*chip, mc), me).wait_recv()
            passed[j].start()
        copy(0, sibling, me).wait_recv()
        for j, chip in enumerate(chips):
            copy(4 + j, (*chip, 1 - mc), me).wait_recv()
        for cp in first + passed:
            cp.wait_send()
        mine.wait()

    return pl.pallas_call(
        body, name=name,
        out_shape=jax.ShapeDtypeStruct((N_DEV, R, 128), F32),
        in_specs=[pl.BlockSpec(memory_space=pltpu.VMEM)],
        out_specs=pl.BlockSpec(memory_space=pltpu.VMEM),
        scratch_shapes=[pltpu.SemaphoreType.DMA((7,)), pltpu.SemaphoreType.DMA((7,)), pltpu.SemaphoreType.DMA],
        compiler_params=pltpu.CompilerParams(vmem_limit_bytes=VMEM_LIMIT),
    )(x)


BIG = (("ffn1_w_gu", "col", D, PW), ("ffn1_w_down", "row", FF, D), ("w_in", "col", D, PW),
       ("w_att_o", "col", 512, D), ("w_rec_o", "row", D, D), ("w_out", "row", D, D),
       ("ffn2_w_gu", "col", D, PW), ("ffn2_w_down", "row", FF, D))
NBIG = len(BIG)


def _shard_shape(kind, R, C):
    return (R, C // 4) if kind == "col" else (R // 4, C)


def _region(ref, kind, R, C, q, half, t, tr):
    sr, sc = _shard_shape(kind, R, C)
    if kind == "col":
        return ref.at[pl.ds(pl.multiple_of(half * (R // 2) + t * tr, 16), tr), pl.ds(q * sc, sc)]
    return ref.at[pl.ds(pl.multiple_of(q * sr + t * tr, 16), tr), pl.ds(half * (C // 2), C // 2)]


def ag_push(w, kind, R, C, c_arr, name):
    sr, sc = _shard_shape(kind, R, C)
    hr, hc = (sr // 2, sc) if kind == "col" else (sr, sc // 2)
    tr = _row_tile(hr, hc, itemsize=2, budget=512 * 1024)
    nt = hr // tr

    def body(c_ref, mine_ref, other_ref, full_ref, stage, lsem, ssem, rsem):
        i = pl.program_id(0)
        par = i % 2
        mx, my, mc = _mesh_pos()
        p = 2 * mx + my
        chips = _other_chips(mx, my)

        def copies(s, q, h, t):
            mine = _region(full_ref, kind, R, C, q, h, t, tr)
            other = _region(full_ref, kind, R, C, q, 1 - h, t, tr)
            out = [pltpu.make_async_remote_copy(src_ref=stage.at[s, 0], dst_ref=mine, send_sem=ssem.at[s, k],
                                                recv_sem=rsem.at[k], device_id=(*chips[k], mc), device_id_type=MESH)
                   for k in range(3)]
            out.append(pltpu.make_async_copy(stage.at[s, 0], mine, lsem.at[s, 0]))
            out.append(pltpu.make_async_copy(stage.at[s, 1], other, lsem.at[s, 1]))
            return out

        def wait_sent(s):
            cps = copies(s, 0, 0, 0)
            for cp in cps[:3]:
                cp.wait_send()
            for cp in cps[3:]:
                cp.wait()

        @pl.when(i >= 2)
        def _():
            wait_sent(par)

        stage[par, 0] = mine_ref[...].astype(BF16)
        stage[par, 1] = other_ref[...].astype(BF16)
        if kind == "col":
            for q in range(4):
                @pl.when(p == q)
                def _(q=q):
                    for cp in copies(par, q, mc, i):
                        cp.start()
        else:
            for h in range(2):
                @pl.when(mc == h)
                def _(h=h):
                    for cp in copies(par, p, h, i):
                        cp.start()

        @pl.when(i == nt - 1)
        def _():
            for s in range(min(nt, 2)):
                wait_sent(s)
            for k in range(3):
                half = full_ref.at[pl.ds(0, hr), pl.ds(0, hc)]
                pltpu.make_async_remote_copy(src_ref=half, dst_ref=half, send_sem=ssem.at[0, k], recv_sem=rsem.at[k],
                                             device_id=(*chips[k], mc), device_id_type=MESH).wait_recv()

    if kind == "col":
        mine_spec = pl.BlockSpec((tr, hc), lambda i, c: (c[0] * nt + i, 0))
        other_spec = pl.BlockSpec((tr, hc), lambda i, c: ((1 - c[0]) * nt + i, 0))
    else:
        mine_spec = pl.BlockSpec((tr, hc), lambda i, c: (i, c[0]))
        other_spec = pl.BlockSpec((tr, hc), lambda i, c: (i, 1 - c[0]))
    return pl.pallas_call(
        body, name=name,
        grid_spec=pltpu.PrefetchScalarGridSpec(
            num_scalar_prefetch=1, grid=(nt,), in_specs=[mine_spec, other_spec], out_specs=ANY,
            scratch_shapes=[pltpu.VMEM((2, 2, tr, hc), BF16), pltpu.SemaphoreType.DMA((2, 2)),
                            pltpu.SemaphoreType.DMA((2, 3)), pltpu.SemaphoreType.DMA((3,))]),
        out_shape=jax.ShapeDtypeStruct((R, C), BF16),
        compiler_params=_cp("arbitrary"),
    )(c_arr, w, w)


def ag_forward(full, kind, R, C, name):
    sr, sc = _shard_shape(kind, R, C)
    hr, hc = (sr // 2, sc) if kind == "col" else (sr, sc // 2)
    tr = _row_tile(hr, hc, itemsize=2, budget=512 * 1024)
    nt = hr // tr

    def body(src_ref, full_ref, stage, lsem, ssem, rsem):
        k, i = pl.program_id(0), pl.program_id(1)
        step = k * nt + i
        par = step % 2
        mx, my, mc = _mesh_pos()
        q_k = _partner_chip(k, 2 * mx + my)

        def push(s, dst):
            return pltpu.make_async_remote_copy(src_ref=stage.at[s], dst_ref=dst, send_sem=ssem.at[s], recv_sem=rsem,
                                                device_id=(mx, my, 1 - mc), device_id_type=MESH)

        @pl.when(step >= 2)
        def _():
            push(par, _region(full_ref, kind, R, C, 0, 0, 0, tr)).wait_send()

        def move(q, h):
            load = pltpu.make_async_copy(_region(src_ref, kind, R, C, q, h, i, tr), stage.at[par], lsem)
            load.start()
            load.wait()
            push(par, _region(full_ref, kind, R, C, q, h, i, tr)).start()

        if kind == "col":
            for q in range(4):
                @pl.when(q_k == q)
                def _(q=q):
                    move(q, mc)
        else:
            for h in range(2):
                @pl.when(mc == h)
                def _(h=h):
                    move(q_k, h)

        @pl.when(step == 3 * nt - 1)
        def _():
            for s in range(2):
                push(s, _region(full_ref, kind, R, C, 0, 0, 0, tr)).wait_send()
            three = full_ref.at[pl.ds(0, hr), pl.ds(0, 3 * hc)] if kind == "col" else full_ref.at[pl.ds(0, 3 * hr), pl.ds(0, hc)]
            pltpu.make_async_remote_copy(src_ref=three, dst_ref=three, send_sem=ssem.at[0], recv_sem=rsem,
                                         device_id=(mx, my, 1 - mc), device_id_type=MESH).wait_recv()

    return pl.pallas_call(
        body, name=name, grid=(3, nt),
        in_specs=[ANY], out_specs=ANY,
        out_shape=jax.ShapeDtypeStruct((R, C), BF16),
        scratch_shapes=[pltpu.VMEM((2, tr, hc), BF16), pltpu.SemaphoreType.DMA, pltpu.SemaphoreType.DMA((2,)),
                        pltpu.SemaphoreType.DMA],
        input_output_aliases={0: 0},
        compiler_params=_cp("arbitrary", "arbitrary"),
    )(full)


def _half_shape(kind, R, C):
    return (R // 2, C) if kind == "col" else (R, C // 2)


def _piece_shape(kind, R, C):
    return (R // 2, C // 4) if kind == "col" else (R // 4, C // 2)


def pair_push(g, kind, c_arr, name):
    R, C = g.shape
    hr, hc = _half_shape(kind, R, C)
    tr = _row_tile(hr, hc, itemsize=2, budget=1024 * 1024)
    nt = hr // tr

    def body(c_ref, g_ref, out_ref, stage, ssem, rsem):
        i = pl.program_id(0)
        slot = i % 2
        mx, my, mc = _mesh_pos()

        def push(s, t):
            return pltpu.make_async_remote_copy(
                src_ref=stage.at[s], dst_ref=out_ref.at[pl.ds(pl.multiple_of(t * tr, 16), tr)],
                send_sem=ssem.at[s], recv_sem=rsem, device_id=(mx, my, 1 - mc), device_id_type=MESH)

        @pl.when(i >= 2)
        def _():
            push(slot, 0).wait_send()

        stage[slot] = g_ref[...]
        push(slot, i).start()

        @pl.when(i == nt - 1)
        def _():
            push(slot, 0).wait_send()
            if nt >= 2:
                push(1 - slot, 0).wait_send()
            pltpu.make_async_remote_copy(src_ref=out_ref, dst_ref=out_ref, send_sem=ssem.at[0], recv_sem=rsem,
                                         device_id=(mx, my, 1 - mc), device_id_type=MESH).wait_recv()

    if kind == "col":
        g_spec = pl.BlockSpec((tr, hc), lambda i, c: ((1 - c[0]) * nt + i, 0))
    else:
        g_spec = pl.BlockSpec((tr, hc), lambda i, c: (i, 1 - c[0]))
    return pl.pallas_call(
        body, name=name,
        grid_spec=pltpu.PrefetchScalarGridSpec(
            num_scalar_prefetch=1, grid=(nt,), in_specs=[g_spec], out_specs=ANY,
            scratch_shapes=[pltpu.VMEM((2, tr, hc), BF16), pltpu.SemaphoreType.DMA((2,)), pltpu.SemaphoreType.DMA]),
        out_shape=jax.ShapeDtypeStruct((hr, hc), BF16),
        compiler_params=_cp("arbitrary"),
    )(c_arr, g)


def _partner_chip(k, p):
    return p ^ jnp.where(k == 0, 2, jnp.where(k == 1, 1, jnp.where(k == 2, 3, 0)))


def pair_add_scatter(g, got, kind, cp_arr, name):
    R, C = g.shape
    pr, pc = _piece_shape(kind, R, C)
    tr = _row_tile(pr, pc, itemsize=2, budget=768 * 1024)
    nt = pr // tr

    def body(cp_ref, g_ref, got_ref, rb_ref, stage, lsem, ssem, rsem):
        i, kk = pl.program_id(0), pl.program_id(1)
        par = i % 2
        mx, my, mc = _mesh_pos()
        p = 2 * mx + my
        chips = _other_chips(mx, my)

        def dst(t):
            return rb_ref.at[p, pl.ds(pl.multiple_of(t * tr, 16), tr)]

        def push(k, s, t):
            return pltpu.make_async_remote_copy(src_ref=stage.at[s, k], dst_ref=dst(t), send_sem=ssem.at[s, k],
                                                recv_sem=rsem.at[k], device_id=(*chips[k], mc), device_id_type=MESH)

        def keep(s, t):
            return pltpu.make_async_copy(stage.at[s, 3], dst(t), lsem.at[s])

        tile = (g_ref[...].astype(F32) + got_ref[...].astype(F32)).astype(BF16)
        for k in range(4):
            @pl.when(kk == k)
            def _(k=k):
                cp = push(k, par, i) if k < 3 else keep(par, i)

                @pl.when(i >= 2)
                def _():
                    cp.wait_send() if k < 3 else cp.wait()

                stage[par, k] = tile
                cp.start()

        @pl.when((i == nt - 1) & (kk == 3))
        def _():
            for s in range(min(nt, 2)):
                for k in range(3):
                    push(k, s, 0).wait_send()
                keep(s, 0).wait()
            for k in range(3):
                whole = rb_ref.at[0]
                pltpu.make_async_remote_copy(src_ref=whole, dst_ref=whole, send_sem=ssem.at[0, k], recv_sem=rsem.at[k],
                                             device_id=(*chips[k], mc), device_id_type=MESH).wait_recv()

    if kind == "col":
        g_spec = pl.BlockSpec((tr, pc), lambda i, k, cp: (cp[0] * nt + i, _partner_chip(k, cp[1])))
        got_spec = pl.BlockSpec((tr, pc), lambda i, k, cp: (i, _partner_chip(k, cp[1])))
    else:
        g_spec = pl.BlockSpec((tr, pc), lambda i, k, cp: (_partner_chip(k, cp[1]) * nt + i, cp[0]))
        got_spec = pl.BlockSpec((tr, pc), lambda i, k, cp: (_partner_chip(k, cp[1]) * nt + i, 0))
    return pl.pallas_call(
        body, name=name,
        grid_spec=pltpu.PrefetchScalarGridSpec(
            num_scalar_prefetch=1, grid=(nt, 4), in_specs=[g_spec, got_spec], out_specs=ANY,
            scratch_shapes=[pltpu.VMEM((2, 4, tr, pc), BF16), pltpu.SemaphoreType.DMA((2,)),
                            pltpu.SemaphoreType.DMA((2, 3)), pltpu.SemaphoreType.DMA((3,))]),
        out_shape=jax.ShapeDtypeStruct((4, pr, pc), BF16),
        compiler_params=_cp("arbitrary", "arbitrary"),
    )(cp_arr, g, got)


def sum_share(parts, kind, R, C, name):
    _, pr, pc = parts.shape
    sr, sc = _shard_shape(kind, R, C)
    tr = _row_tile(pr, pc * 4, budget=4 * 1024 * 1024)
    nt = pr // tr

    def body(p_ref, fin_ref, stage, lsem, ssem, rsem):
        i = pl.program_id(0)
        slot = i % 2
        mx, my, mc = _mesh_pos()

        def region(h, t):
            r0 = pl.multiple_of(t * tr, 8)
            if kind == "col":
                return fin_ref.at[pl.ds(pl.multiple_of(h * pr + r0, 8), tr)]
            return fin_ref.at[pl.ds(r0, tr), pl.ds(h * pc, pc)]

        def copies(s, h, t):
            return (pltpu.make_async_copy(stage.at[s], region(h, t), lsem.at[s]),
                    pltpu.make_async_remote_copy(src_ref=stage.at[s], dst_ref=region(h, t), send_sem=ssem.at[s],
                                                 recv_sem=rsem, device_id=(mx, my, 1 - mc), device_id_type=MESH))

        def wait_sent(s):
            loc, rem = copies(s, 0, 0)
            loc.wait()
            rem.wait_send()

        @pl.when(i >= 2)
        def _():
            wait_sent(slot)

        acc = p_ref[0].astype(F32)
        for k in range(1, 4):
            acc = acc + p_ref[k].astype(F32)
        stage[slot] = acc
        if kind == "col":
            for cp in copies(slot, mc, i):
                cp.start()
        else:
            for h in range(2):
                @pl.when(mc == h)
                def _(h=h):
                    for cp in copies(slot, h, i):
                        cp.start()

        @pl.when(i == nt - 1)
        def _():
            wait_sent(slot)
            if nt >= 2:
                wait_sent(1 - slot)
            half = fin_ref.at[pl.ds(0, pr), pl.ds(0, pc)]
            pltpu.make_async_remote_copy(src_ref=half, dst_ref=half, send_sem=ssem.at[0], recv_sem=rsem,
                                         device_id=(mx, my, 1 - mc), device_id_type=MESH).wait_recv()

    return pl.pallas_call(
        body, name=name, grid=(nt,),
        in_specs=[pl.BlockSpec((4, tr, pc), lambda i: (0, i, 0))],
        out_specs=ANY,
        out_shape=jax.ShapeDtypeStruct((sr, sc), F32),
        scratch_shapes=[pltpu.VMEM((2, tr, pc), F32), pltpu.SemaphoreType.DMA((2,)), pltpu.SemaphoreType.DMA((2,)),
                        pltpu.SemaphoreType.DMA],
        compiler_params=_cp("arbitrary"),
    )(parts)


def _pack(parts, rows):
    flat = []
    for a in parts:
        a = jnp.ravel(a).astype(F32)
        flat.append(jnp.pad(a, (0, (-a.shape[0]) % 128)))
    v = jnp.concatenate(flat)
    return jnp.pad(v, (0, rows * 128 - v.shape[0])).reshape(rows, 128)


def _unpack(block, shapes):
    lead = block.shape[:-2]
    v = block.reshape(lead + (-1,))
    out, off = [], 0
    for shp in shapes:
        n = int(np.prod(shp))
        out.append(v[..., off:off + n].reshape(lead + tuple(shp)))
        off += n + (-n) % 128
    return out


def _block_diag4(w):
    w4 = w.reshape(4, 4, 64, 64)
    eye = jnp.eye(4, dtype=w.dtype)
    return (w4[:, :, :, None, :] * eye[None, :, None, :, None]).reshape(4, 256, 256)


def _diag_blocks(bd):
    b5 = bd.reshape(4, 4, 64, 4, 64)
    return jnp.stack([b5[:, i, :, i, :] for i in range(4)], axis=1).reshape(16, 64, 64)


def _bias_window(rel_bias):
    m = np.arange(767)
    tv = rel_bias[:, np.clip(639 - m, -128, 128) + 128]
    win = jnp.stack([tv[:, 127 - r:127 - r + WIN] for r in range(128)], axis=1)
    qh = np.arange(128)[:, None] // CHUNK
    kc = np.arange(WIN)[None, :] // CHUNK
    valid = (kc >= qh) & (kc <= qh + 8)
    return jnp.where(jnp.asarray(valid)[None], win, NEG)


SMALL = ("b_ada", "norm_pre", "norm_post", "rel_bias", "conv_w", "conv_b", "lru_wa", "lru_ba", "lru_wx",
         "lru_bx", "lru_lambda")
WEIGHTS = ("w_ada", "b_ada", "norm_pre", "norm_post", "ffn1_w_gu", "ffn1_w_down", "w_in", "rel_bias", "conv_w",
           "conv_b", "lru_wa", "lru_ba", "lru_wx", "lru_bx", "lru_lambda", "w_att_o", "w_rec_o", "w_out",
           "ffn2_w_gu", "ffn2_w_down")


def kernel(x, c, w_ada, b_ada, norm_pre, norm_post, ffn1_w_gu, ffn1_w_down, w_in, rel_bias, conv_w, conv_b, lru_wa, lru_ba, lru_wx, lru_bx, lru_lambda, w_att_o, w_rec_o, w_out, ffn2_w_gu, ffn2_w_down, loss_target, m_w_ada, m_b_ada, m_norm_pre, m_norm_post, m_ffn1_w_gu, m_ffn1_w_down, m_w_in, m_rel_bias, m_conv_w, m_conv_b, m_lru_wa, m_lru_ba, m_lru_wx, m_lru_bx, m_lru_lambda, m_w_att_o, m_w_rec_o, m_w_out, m_ffn2_w_gu, m_ffn2_w_down, v_w_ada, v_b_ada, v_norm_pre, v_norm_post, v_ffn1_w_gu, v_ffn1_w_down, v_w_in, v_rel_bias, v_conv_w, v_conv_b, v_lru_wa, v_lru_ba, v_lru_wx, v_lru_bx, v_lru_lambda, v_w_att_o, v_w_rec_o, v_w_out, v_ffn2_w_gu, v_ffn2_w_down):
    W = dict(w_ada=w_ada, b_ada=b_ada, norm_pre=norm_pre, norm_post=norm_post, ffn1_w_gu=ffn1_w_gu,
             ffn1_w_down=ffn1_w_down, w_in=w_in, rel_bias=rel_bias, conv_w=conv_w, conv_b=conv_b, lru_wa=lru_wa,
             lru_ba=lru_ba, lru_wx=lru_wx, lru_bx=lru_bx, lru_lambda=lru_lambda, w_att_o=w_att_o, w_rec_o=w_rec_o,
             w_out=w_out, ffn2_w_gu=ffn2_w_gu, ffn2_w_down=ffn2_w_down)
    M = dict(w_ada=m_w_ada, b_ada=m_b_ada, norm_pre=m_norm_pre, norm_post=m_norm_post, ffn1_w_gu=m_ffn1_w_gu,
             ffn1_w_down=m_ffn1_w_down, w_in=m_w_in, rel_bias=m_rel_bias, conv_w=m_conv_w, conv_b=m_conv_b,
             lru_wa=m_lru_wa, lru_ba=m_lru_ba, lru_wx=m_lru_wx, lru_bx=m_lru_bx, lru_lambda=m_lru_lambda,
             w_att_o=m_w_att_o, w_rec_o=m_w_rec_o, w_out=m_w_out, ffn2_w_gu=m_ffn2_w_gu, ffn2_w_down=m_ffn2_w_down)
    V = dict(w_ada=v_w_ada, b_ada=v_b_ada, norm_pre=v_norm_pre, norm_post=v_norm_post, ffn1_w_gu=v_ffn1_w_gu,
             ffn1_w_down=v_ffn1_w_down, w_in=v_w_in, rel_bias=v_rel_bias, conv_w=v_conv_w, conv_b=v_conv_b,
             lru_wa=v_lru_wa, lru_ba=v_lru_ba, lru_wx=v_lru_wx, lru_bx=v_lru_bx, lru_lambda=v_lru_lambda,
             w_att_o=v_w_att_o, w_rec_o=v_w_rec_o, w_out=v_w_out, ffn2_w_gu=v_ffn2_w_gu, ffn2_w_down=v_ffn2_w_down)
    mx, my, mc = _mesh_pos()
    p = 2 * mx + my
    e = 4 * mx + 2 * my + mc
    xs = x[0]

    g1 = ag_small(_pack([c, norm_pre, norm_post, conv_w], 32), "ag_small_params")
    c_all, npre4, npost4, cw4 = _unpack(g1, [(D,), (3, 256), (3, 256), (4, 256)])
    chipwise = lambda a: jnp.moveaxis(a[0::2], 0, 1).reshape(a.shape[1], D)
    npre, npost, conv_full = chipwise(npre4), chipwise(npost4), chipwise(cw4)

    b_cols = lax.dynamic_slice(b_ada, (0, p * 2304), (1, 2304))
    mod_cols = ada_fwd(c_all, w_ada[0], b_cols, "ada_fwd")
    g2 = ag_small(mod_cols.reshape(144, 128), "ag_mod")
    mod_all = jnp.moveaxis(g2[0::2].reshape(4, 8, 2304), 0, 1).reshape(8, 9 * D)
    mod = lax.dynamic_index_in_dim(mod_all, e, 0, keepdims=False).reshape(3, 3, D)
    zeros3 = jnp.zeros((3, D), F32)
    vecs = [jnp.concatenate([npre[k:k + 1], npost[k:k + 1], mod[k], zeros3], axis=0) for k in range(3)]

    c_arr = jnp.reshape(mc, (1,)).astype(jnp.int32)
    cp_arr = jnp.stack([mc, p]).astype(jnp.int32)
    full = [ag_forward(ag_push(W[n][0], kind, R, C, c_arr, "ag_push_" + n), kind, R, C, "ag_forward_" + n)
            for (n, kind, R, C) in BIG]
    f1_gu, f1_dn, win, wao, wro, wout, f2_gu, f2_dn = full
    wa_bd = _block_diag4(lru_wa[0]).astype(BF16)
    wx_bd = _block_diag4(lru_wx[0]).astype(BF16)
    pvec = jnp.concatenate([conv_full, conv_b, lru_ba, lru_bx, lru_lambda], axis=0)
    bias = _bias_window(rel_bias[0])

    x1, h1, g1_, u1, a1, f1 = ffn_fwd(xs, vecs[0], f1_gu, f1_dn, 0.5, "ffn1_fwd")
    h2, qkv, rest = proj_fwd(x1, vecs[1], win, "proj_fwd")
    ao = attn_fwd(qkv, bias, "attn_fwd")
    hl, hg = lru_fwd(rest, pvec, wa_bd, wx_bd, "lru_fwd")
    x2, att, rec, mg, f2 = mix_out_fwd(x1, ao, hg, rest, vecs[1], wao, wro, wout, "mix_out_fwd")
    x3, h3, g3_, u3, a3, f3 = ffn_fwd(x2, vecs[2], f2_gu, f2_dn, 0.5, "ffn2_fwd")
    dy, lvec = loss_grad(x3, loss_target[0], "loss_grad")
    loss = lax.psum(lvec[0, 0], ("x", "y", "c"))

    G = {}
    dx2, df3, dgu3, va2 = ffn_bwd(dy, x2, f3, g3_, u3, vecs[2], f2_gu, f2_dn, 0.5, "ffn2_bwd")
    G["ffn2_w_gu"] = mm_tn(h3, dgu3, "dw_ffn2_gu", D, 1408, 1024)
    G["ffn2_w_down"] = mm_tn(a3, df3, "dw_ffn2_down", 1408, D, 1024)
    df2, d_att, d_rec, dao, dhl, d3, va_out = mix_out_bwd(dx2, f2, att, rec, rest, hl, vecs[1], wao, wro, wout,
                                                          "mix_out_bwd")
    G["w_out"] = mm_tn(mg, df2, "dw_out", D, D, 1024)
    G["w_att_o"] = mm_tn(ao, d_att, "dw_att_o", 512, D, 1024)
    G["w_rec_o"] = mm_tn(hg, d_rec, "dw_rec_o", D, D, 1024)
    dq, db, dkv = attn_bwd(qkv, dao, bias, "attn_bwd")
    dxr, v_lru, dwa_bd, dwx_bd = lru_bwd(dhl, hl, rest, pvec, wa_bd, wx_bd, "lru_bwd")
    dx1, va_in = proj_bwd(dq, dkv, dxr, d3, win, x1, dx2, vecs[1], "proj_bwd")
    gin = mm_tn(h2, dq, "dw_in_q", D, 512, 1024, n_total=PW)
    gin = mm_tn(h2, dkv, "dw_in_kv", D, 512, 1024, prev=gin, col_off=512, n_total=PW)
    gin = mm_tn(h2, dxr, "dw_in_xr", D, 512, 1024, prev=gin, col_off=1536, n_total=PW)
    G["w_in"] = mm_tn(h2, d3, "dw_in_gates", D, 512, 1024, prev=gin, col_off=2560, n_total=PW)
    dx0, df1, dgu1, va0 = ffn_bwd(dx1, xs, f1, g1_, u1, vecs[0], f1_gu, f1_dn, 0.5, "ffn1_bwd")
    G["ffn1_w_gu"] = mm_tn(h1, dgu1, "dw_ffn1_gu", D, 1408, 1024)
    G["ffn1_w_down"] = mm_tn(a1, df1, "dw_ffn1_down", 1408, D, 1024)

    va1 = va_out + va_in
    vas = (va0, va1, va2)
    dmod = jnp.stack([v[2:5] for v in vas])
    part = {"b_ada": dmod, "norm_pre": jnp.stack([v[0] for v in vas]), "norm_post": jnp.stack([v[1] for v in vas]),
            "rel_bias": bias_grad(db, "bias_grad")[:, :257], "conv_w": v_lru[0:4], "conv_b": v_lru[4],
            "lru_wa": _diag_blocks(dwa_bd), "lru_ba": v_lru[5], "lru_wx": _diag_blocks(dwx_bd), "lru_bx": v_lru[6],
            "lru_lambda": v_lru[7]}
    full_shapes = {"b_ada": (9 * D,), "norm_pre": (3, D), "norm_post": (3, D), "rel_bias": (8, 257),
                   "conv_w": (4, D), "conv_b": (D,), "lru_wa": (16, 64, 64), "lru_ba": (D,),
                   "lru_wx": (16, 64, 64), "lru_bx": (D,), "lru_lambda": (D,)}
    g3 = ag_small(_pack([part[n] for n in SMALL], 1232), "ag_small_grads")
    red = dict(zip(SMALL, _unpack(sum_lead(g3, "sum_small_grads"), [full_shapes[n] for n in SMALL])))
    cols = lambda a: lax.dynamic_slice(a, (0, p * 256), (a.shape[0], 256))
    grads = {"b_ada": red["b_ada"][None], "norm_pre": cols(red["norm_pre"])[None],
             "norm_post": cols(red["norm_post"])[None], "rel_bias": red["rel_bias"][None],
             "conv_w": cols(red["conv_w"])[None], "conv_b": red["conv_b"][None], "lru_wa": red["lru_wa"][None],
             "lru_ba": red["lru_ba"][None], "lru_wx": red["lru_wx"][None], "lru_bx": red["lru_bx"][None],
             "lru_lambda": red["lru_lambda"][None]}

    dmod_all = g3[:, :72].reshape(8, 9 * D)
    dmod_cols = jnp.pad(lax.dynamic_slice(dmod_all, (0, p * 2304), (8, 2304)), ((0, 120), (0, 0)))
    c_all_t = jnp.pad(c_all.T, ((0, 0), (0, 120)))
    grads["w_ada"] = ada_bwd(c_all_t, dmod_cols, "ada_bwd")[None]

    for n, kind, R, C in BIG:
        got = pair_push(G[n], kind, c_arr, "rs_push_" + n)
        by_chip = pair_add_scatter(G[n], got, kind, cp_arr, "rs_scatter_" + n)
        grads[n] = sum_share(by_chip, kind, R, C, "rs_sum_share_" + n)[None]

    delta, new_m, new_v = {}, {}, {}
    for n in ("w_ada",) + tuple(b[0] for b in BIG):
        shp = W[n].shape
        d_, m_, v_ = adamw(W[n][0], grads[n][0], M[n][0], V[n][0], "adamw_" + n)
        delta[n], new_m[n], new_v[n] = d_.reshape(shp), m_.reshape(shp), v_.reshape(shp)
    packed = [_pack([src[n] for n in SMALL], 1168) for src in (W, grads, M, V)]
    outs = adamw(*packed, "adamw_small")
    for dst, blk in zip((delta, new_m, new_v), outs):
        for n, a in zip(SMALL, _unpack(blk, [W[n].shape for n in SMALL])):
            dst[n] = a

    return (loss, dx0[None], *[grads[n] for n in WEIGHTS], *[delta[n] for n in WEIGHTS],
            *[new_m[n] for n in WEIGHTS], *[new_v[n] for n in WEIGHTS])
```

```python
import functools

import numpy as np
import jax
import jax.numpy as jnp
from jax import lax
from jax.experimental import pallas as pl
from jax.experimental.pallas import tpu as pltpu

F32 = jnp.float32
BF16 = jnp.bfloat16

D = 1024
FF = 2816
PW = 5632
HP = 128
CHUNK = 64
WIN = 640
TQ = 512
EPS = 1e-6
NEG = -1e30
LRU_C = 8.0
N_DEV = 8
VMEM_LIMIT = 50 * 1024 * 1024

ADAM_LR, ADAM_B1, ADAM_B2, ADAM_EPS, ADAM_WD, ADAM_STEP = 0.001, 0.9, 0.999, 1e-08, 0.01, 10

MESH = pl.DeviceIdType.MESH
ANY = pl.BlockSpec(memory_space=pl.ANY)


def _cp(*sem):
    return pltpu.CompilerParams(dimension_semantics=tuple(sem), vmem_limit_bytes=VMEM_LIMIT)


def _dot(a, b):
    return jnp.dot(a, b, preferred_element_type=F32)


def _dot_nt(a, b):
    return lax.dot_general(a, b, (((1,), (1,)), ((), ())), preferred_element_type=F32)


def _dot_tn(a, b):
    return lax.dot_general(a, b, (((0,), (0,)), ((), ())), preferred_element_type=F32)


def _mean(v):
    return jnp.mean(v, axis=-1, keepdims=True)


def _colsum(v):
    return jnp.sum(v, axis=0, keepdims=True)


def _sigmoid(v):
    return jax.nn.sigmoid(v)


def _expm1(v):
    small = v * (1.0 + v * 0.5 * (1.0 + v * (1.0 / 3.0) * (1.0 + v * 0.25 * (1.0 + v * 0.2 * (
        1.0 + v * (1.0 / 6.0) * (1.0 + v * (1.0 / 7.0)))))))
    return jnp.where(jnp.abs(v) < 0.25, small, jnp.exp(v) - 1.0)


_GK = 0.7978845608028654


def _gelu(v):
    t = jnp.tanh(_GK * (v + 0.044715 * v * v * v))
    return 0.5 * v * (1.0 + t)


def _gelu_grad(v):
    t = jnp.tanh(_GK * (v + 0.044715 * v * v * v))
    return 0.5 * (1.0 + t) + 0.5 * v * (1.0 - t * t) * _GK * (1.0 + 3.0 * 0.044715 * v * v)


def _pre_norm(xv, vec_ref):
    r = lax.rsqrt(_mean(xv * xv) + EPS)
    n = xv * r * vec_ref[0:1, :]
    return n * (1.0 + vec_ref[3:4, :]) + vec_ref[2:3, :]


def _pre_norm_bwd(dh, xv, dres, vec_ref, vacc_ref):
    r = lax.rsqrt(_mean(xv * xv) + EPS)
    xh = xv * r
    n = xh * vec_ref[0:1, :]
    vacc_ref[2:3, :] += _colsum(dh)
    vacc_ref[3:4, :] += _colsum(dh * n)
    dn = dh * (1.0 + vec_ref[3:4, :])
    vacc_ref[0:1, :] += _colsum(dn * xh)
    dxh = dn * vec_ref[0:1, :]
    return r * (dxh - xh * _mean(dxh * xh)) + dres


def _post_norm_bwd(dxo, fv, res, vec_ref, vacc_ref):
    rf = lax.rsqrt(_mean(fv * fv) + EPS)
    fh = fv * rf
    gp = vec_ref[1:2, :]
    vacc_ref[4:5, :] += _colsum(res * dxo * (fh * gp))
    dy = (res * vec_ref[4:5, :]) * dxo
    vacc_ref[1:2, :] += _colsum(dy * fh)
    dfn = dy * gp
    return rf * (dfn - fh * _mean(dfn * fh))


def ffn_fwd(x, vec, w_gu, w_dn, res, name, tm=1024, tf=256):
    S = x.shape[0]
    tm = min(tm, S)
    nf = FF // tf

    def body(x_ref, vec_ref, wg_ref, wu_ref, wd_ref, xo_ref, h_ref, g_ref, u_ref, a_ref, f_ref, hs, acc):
        j = pl.program_id(1)

        @pl.when(j == 0)
        def _():
            h = _pre_norm(x_ref[...], vec_ref).astype(BF16)
            hs[...] = h
            h_ref[...] = h
            acc[...] = jnp.zeros_like(acc)

        h = hs[...]
        g = _dot(h, wg_ref[...])
        u = _dot(h, wu_ref[...])
        g_ref[...] = g.astype(BF16)
        u_ref[...] = u.astype(BF16)
        a = (g * _sigmoid(g) * u).astype(BF16)
        a_ref[...] = a
        acc[...] += _dot(a, wd_ref[...])

        @pl.when(j == nf - 1)
        def _():
            f = acc[...]
            f_ref[...] = f
            y = f * lax.rsqrt(_mean(f * f) + EPS) * vec_ref[1:2, :]
            xo_ref[...] = x_ref[...] + (res * vec_ref[4:5, :]) * y

    row = lambda i, j: (i, 0)
    return pl.pallas_call(
        body, name=name, grid=(S // tm, nf),
        in_specs=[pl.BlockSpec((tm, D), row), pl.BlockSpec((8, D), lambda i, j: (0, 0)),
                  pl.BlockSpec((D, tf), lambda i, j: (0, j)), pl.BlockSpec((D, tf), lambda i, j: (0, j + nf)),
                  pl.BlockSpec((tf, D), lambda i, j: (j, 0))],
        out_specs=[pl.BlockSpec((tm, D), row), pl.BlockSpec((tm, D), row),
                   pl.BlockSpec((tm, tf), lambda i, j: (i, j)), pl.BlockSpec((tm, tf), lambda i, j: (i, j)),
                   pl.BlockSpec((tm, tf), lambda i, j: (i, j)), pl.BlockSpec((tm, D), row)],
        out_shape=[jax.ShapeDtypeStruct((S, D), F32), jax.ShapeDtypeStruct((S, D), BF16),
                   jax.ShapeDtypeStruct((S, FF), BF16), jax.ShapeDtypeStruct((S, FF), BF16),
                   jax.ShapeDtypeStruct((S, FF), BF16), jax.ShapeDtypeStruct((S, D), F32)],
        scratch_shapes=[pltpu.VMEM((tm, D), BF16), pltpu.VMEM((tm, D), F32)],
        compiler_params=_cp("parallel", "arbitrary"),
    )(x, vec, w_gu, w_gu, w_dn)


def ffn_bwd(dxo, x, f, g, u, vec, w_gu, w_dn, res, name, tm=512, tf=256):
    S = x.shape[0]
    tm = min(tm, S)
    nf = FF // tf

    def body(dxo_ref, x_ref, f_ref, g_ref, u_ref, vec_ref, wg_ref, wu_ref, wd_ref,
             dx_ref, df_ref, dgu_ref, vacc_ref, dfs, acc):
        i, j = pl.program_id(0), pl.program_id(1)

        @pl.when((i == 0) & (j == 0))
        def _():
            vacc_ref[...] = jnp.zeros_like(vacc_ref)

        @pl.when(j == 0)
        def _():
            df = _post_norm_bwd(dxo_ref[...], f_ref[...], res, vec_ref, vacc_ref).astype(BF16)
            dfs[...] = df
            df_ref[...] = df
            acc[...] = jnp.zeros_like(acc)

        da = _dot_nt(dfs[...], wd_ref[...])
        gv, uv = g_ref[...].astype(F32), u_ref[...].astype(F32)
        sg = _sigmoid(gv)
        dg = (da * uv * (sg * (1.0 + gv * (1.0 - sg)))).astype(BF16)
        du = (da * (gv * sg)).astype(BF16)
        dgu_ref[0] = dg
        dgu_ref[1] = du
        acc[...] += _dot_nt(dg, wg_ref[...]) + _dot_nt(du, wu_ref[...])

        @pl.when(j == nf - 1)
        def _():
            dx_ref[...] = _pre_norm_bwd(acc[...], x_ref[...], dxo_ref[...], vec_ref, vacc_ref)

    row = lambda i, j: (i, 0)
    return pl.pallas_call(
        body, name=name, grid=(S // tm, nf),
        in_specs=[pl.BlockSpec((tm, D), row), pl.BlockSpec((tm, D), row), pl.BlockSpec((tm, D), row),
                  pl.BlockSpec((tm, tf), lambda i, j: (i, j)), pl.BlockSpec((tm, tf), lambda i, j: (i, j)),
                  pl.BlockSpec((8, D), lambda i, j: (0, 0)),
                  pl.BlockSpec((D, tf), lambda i, j: (0, j)), pl.BlockSpec((D, tf), lambda i, j: (0, j + nf)),
                  pl.BlockSpec((tf, D), lambda i, j: (j, 0))],
        out_specs=[pl.BlockSpec((tm, D), row), pl.BlockSpec((tm, D), row),
                   pl.BlockSpec((2, tm, tf), lambda i, j: (0, i, j)),
                   pl.BlockSpec((8, D), lambda i, j: (0, 0))],
        out_shape=[jax.ShapeDtypeStruct((S, D), F32), jax.ShapeDtypeStruct((S, D), BF16),
                   jax.ShapeDtypeStruct((2, S, FF), BF16), jax.ShapeDtypeStruct((8, D), F32)],
        scratch_shapes=[pltpu.VMEM((tm, D), BF16), pltpu.VMEM((tm, D), F32)],
        compiler_params=_cp("arbitrary", "arbitrary"),
    )(dxo, x, f, g, u, vec, w_gu, w_gu, w_dn)


def mm_tn(a, b, name, tm, tn, tk, out_dtype=BF16, prev=None, col_off=0, n_total=None):
    S, M = a.shape
    if b.ndim == 3:
        G, _, Nf = b.shape
    else:
        G, Nf = 1, b.shape[1]
    N = G * Nf
    n_total = N if n_total is None else n_total
    tk = min(tk, S)
    nbf = Nf // tn
    nk = S // tk
    ob = col_off // tn

    def body(*refs):
        a_ref, b_ref = refs[0], refs[1]
        o_ref, acc = refs[-2], refs[-1]
        k = pl.program_id(2)

        @pl.when(k == 0)
        def _():
            acc[...] = jnp.zeros_like(acc)

        acc[...] += _dot_tn(a_ref[...], b_ref[...])

        @pl.when(k == nk - 1)
        def _():
            o_ref[...] = acc[...].astype(out_dtype)

    if b.ndim == 3:
        b_spec = pl.BlockSpec((None, tk, tn), lambda i, j, k: (j // nbf, k, j % nbf))
    else:
        b_spec = pl.BlockSpec((tk, tn), lambda i, j, k: (k, j))
    in_specs = [pl.BlockSpec((tk, tm), lambda i, j, k: (k, i)), b_spec]
    args = [a, b]
    aliases = {}
    if prev is not None:
        in_specs.append(ANY)
        args.append(prev)
        aliases = {2: 0}
    return pl.pallas_call(
        body, name=name, grid=(M // tm, N // tn, nk),
        in_specs=in_specs,
        out_specs=pl.BlockSpec((tm, tn), lambda i, j, k: (i, j + ob)),
        out_shape=jax.ShapeDtypeStruct((M, n_total), out_dtype),
        scratch_shapes=[pltpu.VMEM((tm, tn), F32)],
        input_output_aliases=aliases,
        compiler_params=_cp("parallel", "parallel", "arbitrary"),
    )(*args)


def proj_fwd(x, vec, w_in, name, tm=1024, tn=512):
    S = x.shape[0]
    tm = min(tm, S)
    nq = 1536 // tn

    def body(x_ref, vec_ref, w_ref, h_ref, qkv_ref, rest_ref, hs):
        j = pl.program_id(1)

        @pl.when(j == 0)
        def _():
            h = _pre_norm(x_ref[...], vec_ref).astype(BF16)
            hs[...] = h
            h_ref[...] = h

        r = _dot(hs[...], w_ref[...])

        @pl.when(j < nq)
        def _():
            qkv_ref[...] = r.astype(BF16)

        @pl.when(j >= nq)
        def _():
            rest_ref[...] = r

    row = lambda i, j: (i, 0)
    return pl.pallas_call(
        body, name=name, grid=(S // tm, PW // tn),
        in_specs=[pl.BlockSpec((tm, D), row), pl.BlockSpec((8, D), lambda i, j: (0, 0)),
                  pl.BlockSpec((D, tn), lambda i, j: (0, j))],
        out_specs=[pl.BlockSpec((tm, D), row),
                   pl.BlockSpec((tm, tn), lambda i, j: (i, jnp.minimum(j, nq - 1))),
                   pl.BlockSpec((tm, tn), lambda i, j: (i, jnp.maximum(j - nq, 0)))],
        out_shape=[jax.ShapeDtypeStruct((S, D), BF16), jax.ShapeDtypeStruct((S, 1536), BF16),
                   jax.ShapeDtypeStruct((S, 4096), F32)],
        scratch_shapes=[pltpu.VMEM((tm, D), BF16)],
        compiler_params=_cp("parallel", "arbitrary"),
    )(x, vec, w_in)


def proj_bwd(dq, dkv, dxr, d3, w_in, x, dxo, vec, name, tm=1024, tk=512):
    S = x.shape[0]
    tm = min(tm, S)
    nk = PW // tk

    def body(dq_ref, dkv_ref, dxr_ref, d3_ref, w_ref, x_ref, dxo_ref, vec_ref, dx_ref, vacc_ref, acc):
        i, j = pl.program_id(0), pl.program_id(1)

        @pl.when((i == 0) & (j == 0))
        def _():
            vacc_ref[...] = jnp.zeros_like(vacc_ref)

        @pl.when(j == 0)
        def _():
            acc[...] = _dot_nt(dq_ref[...], w_ref[...])

        @pl.when((j >= 1) & (j < 3))
        def _():
            acc[...] += _dot_nt(dkv_ref[...], w_ref[...])

        @pl.when((j >= 3) & (j < 5))
        def _():
            acc[...] += _dot_nt(dxr_ref[...], w_ref[...])

        @pl.when(j >= 5)
        def _():
            acc[...] += _dot_nt(d3_ref[...], w_ref[...])

        @pl.when(j == nk - 1)
        def _():
            dx_ref[...] = _pre_norm_bwd(acc[...], x_ref[...], dxo_ref[...], vec_ref, vacc_ref)

    row = lambda i, j: (i, 0)
    return pl.pallas_call(
        body, name=name, grid=(S // tm, nk),
        in_specs=[pl.BlockSpec((None, tm, tk), lambda i, j: (0, i, 0)),
                  pl.BlockSpec((None, tm, tk), lambda i, j: (jnp.clip(j - 1, 0, 1), i, 0)),
                  pl.BlockSpec((tm, tk), lambda i, j: (i, jnp.clip(j - 3, 0, 1))),
                  pl.BlockSpec((None, tm, tk), lambda i, j: (jnp.clip(j - 5, 0, 5) // 2, i, jnp.clip(j - 5, 0, 5) % 2)),
                  pl.BlockSpec((D, tk), lambda i, j: (0, j)),
                  pl.BlockSpec((tm, D), row), pl.BlockSpec((tm, D), row),
                  pl.BlockSpec((8, D), lambda i, j: (0, 0))],
        out_specs=[pl.BlockSpec((tm, D), row), pl.BlockSpec((8, D), lambda i, j: (0, 0))],
        out_shape=[jax.ShapeDtypeStruct((S, D), F32), jax.ShapeDtypeStruct((8, D), F32)],
        scratch_shapes=[pltpu.VMEM((tm, D), F32)],
        compiler_params=_cp("arbitrary", "arbitrary"),
    )(dq, dkv, dxr, d3, w_in, x, dxo, vec)


def _attn_probs(qm, ka, bias_h, i, grp):
    s = _dot_nt(qm, ka) + bias_h
    col = lax.broadcasted_iota(jnp.int32, s.shape, 1)
    first_key = jnp.where(i == 0, 512 - 128 * grp, 0)
    s = jnp.where(col >= first_key, s, NEG)
    e = jnp.exp(s - jnp.max(s, axis=-1, keepdims=True))
    return e / jnp.sum(e, axis=-1, keepdims=True)


def attn_fwd(qkv, bias, name):
    S = qkv.shape[0]
    nb = S // TQ

    def body(q_ref, kp_ref, kc_ref, vp_ref, vc_ref, b_ref, o_ref, kw, vw):
        i = pl.program_id(1)
        kw[0:TQ, :] = kp_ref[...]
        kw[TQ:2 * TQ, :] = kc_ref[...]
        vw[0:TQ, :] = vp_ref[...]
        vw[TQ:2 * TQ, :] = vc_ref[...]
        lane = lax.broadcasted_iota(jnp.int32, (1, HP), 1)
        zero = jnp.zeros((), BF16)

        def group(a, carry):
            r0 = pl.multiple_of(a * 128, 128)
            qa = q_ref[pl.ds(r0, 128), :] * jnp.asarray(0.125, BF16)
            ka = kw[pl.ds(r0, WIN), :]
            va = vw[pl.ds(r0, WIN), :]
            o = jnp.zeros((128, HP), F32)
            for hh in range(2):
                msk = (lane < 64) if hh == 0 else (lane >= 64)
                p = _attn_probs(jnp.where(msk, qa, zero), ka, b_ref[hh], i, a)
                o += _dot(p.astype(BF16), jnp.where(msk, va, zero))
            o_ref[pl.ds(r0, 128), :] = o.astype(BF16)
            return carry

        lax.fori_loop(0, TQ // 128, group, 0, unroll=True)

    prev = lambda h, i: (jnp.maximum(i - 1, 0), 0)
    return pl.pallas_call(
        body, name=name, grid=(4, nb),
        in_specs=[pl.BlockSpec((TQ, HP), lambda h, i: (i, h)),
                  pl.BlockSpec((TQ, HP), lambda h, i: (jnp.maximum(i - 1, 0), 4 + h)),
                  pl.BlockSpec((TQ, HP), lambda h, i: (i, 4 + h)),
                  pl.BlockSpec((TQ, HP), lambda h, i: (jnp.maximum(i - 1, 0), 8 + h)),
                  pl.BlockSpec((TQ, HP), lambda h, i: (i, 8 + h)),
                  pl.BlockSpec((2, 128, WIN), lambda h, i: (h, 0, 0))],
        out_specs=pl.BlockSpec((TQ, HP), lambda h, i: (i, h)),
        out_shape=jax.ShapeDtypeStruct((S, 512), BF16),
        scratch_shapes=[pltpu.VMEM((2 * TQ, HP), BF16), pltpu.VMEM((2 * TQ, HP), BF16)],
        compiler_params=_cp("parallel", "arbitrary"),
    )(qkv, qkv, qkv, qkv, qkv, bias)


def attn_bwd(qkv, do, bias, name):
    S = qkv.shape[0]
    nb = S // TQ

    def body(q_ref, kp_ref, kc_ref, vp_ref, vc_ref, do_ref, b_ref, dqkv_ref, db_ref, dkv_ref, kw, vw, ak, av):
        i = pl.program_id(1)

        @pl.when(i == 0)
        def _():
            db_ref[...] = jnp.zeros_like(db_ref)
            ak[...] = jnp.zeros_like(ak)
            av[...] = jnp.zeros_like(av)

        @pl.when(i > 0)
        def _():
            ak[0:TQ, :] = ak[TQ:2 * TQ, :]
            av[0:TQ, :] = av[TQ:2 * TQ, :]
            ak[TQ:2 * TQ, :] = jnp.zeros((TQ, HP), F32)
            av[TQ:2 * TQ, :] = jnp.zeros((TQ, HP), F32)

        @pl.when(i < nb)
        def _():
            kw[0:TQ, :] = kp_ref[...]
            kw[TQ:2 * TQ, :] = kc_ref[...]
            vw[0:TQ, :] = vp_ref[...]
            vw[TQ:2 * TQ, :] = vc_ref[...]
            lane = lax.broadcasted_iota(jnp.int32, (1, HP), 1)
            zero = jnp.zeros((), BF16)

            def group(a, carry):
                r0 = pl.multiple_of(a * 128, 128)
                qa = q_ref[pl.ds(r0, 128), :] * jnp.asarray(0.125, BF16)
                doa = do_ref[pl.ds(r0, 128), :]
                ka = kw[pl.ds(r0, WIN), :]
                va = vw[pl.ds(r0, WIN), :]
                dq = jnp.zeros((128, HP), F32)
                for hh in range(2):
                    msk = (lane < 64) if hh == 0 else (lane >= 64)
                    qm = jnp.where(msk, qa, zero)
                    dom = jnp.where(msk, doa, zero)
                    p = _attn_probs(qm, ka, b_ref[hh], i, a)
                    dp = _dot_nt(dom, va)
                    ds = p * (dp - jnp.sum(p * dp, axis=-1, keepdims=True))
                    db_ref[hh] += ds
                    dsb = ds.astype(BF16)
                    dq += jnp.where(msk, _dot(dsb, ka), 0.0)
                    ak[pl.ds(r0, WIN), :] += _dot_tn(dsb, qm)
                    av[pl.ds(r0, WIN), :] += _dot_tn(p.astype(BF16), dom)
                dqkv_ref[0, pl.ds(r0, 128), :] = (dq * 0.125).astype(BF16)
                return carry

            lax.fori_loop(0, TQ // 128, group, 0, unroll=True)

        @pl.when(i > 0)
        def _():
            dkv_ref[0] = ak[0:TQ, :].astype(BF16)
            dkv_ref[1] = av[0:TQ, :].astype(BF16)

    cur = lambda i: jnp.minimum(i, nb - 1)
    prv = lambda i: jnp.clip(i - 1, 0, nb - 1)
    dq, db, dkv = pl.pallas_call(
        body, name=name, grid=(4, nb + 1),
        in_specs=[pl.BlockSpec((TQ, HP), lambda h, i: (cur(i), h)),
                  pl.BlockSpec((TQ, HP), lambda h, i: (prv(i), 4 + h)),
                  pl.BlockSpec((TQ, HP), lambda h, i: (cur(i), 4 + h)),
                  pl.BlockSpec((TQ, HP), lambda h, i: (prv(i), 8 + h)),
                  pl.BlockSpec((TQ, HP), lambda h, i: (cur(i), 8 + h)),
                  pl.BlockSpec((TQ, HP), lambda h, i: (cur(i), h)),
                  pl.BlockSpec((2, 128, WIN), lambda h, i: (h, 0, 0))],
        out_specs=[pl.BlockSpec((1, TQ, HP), lambda h, i: (0, cur(i), h)),
                   pl.BlockSpec((2, 128, WIN), lambda h, i: (h, 0, 0)),
                   pl.BlockSpec((2, TQ, HP), lambda h, i: (0, prv(i), h))],
        out_shape=[jax.ShapeDtypeStruct((1, S, 512), BF16), jax.ShapeDtypeStruct((8, 128, WIN), F32),
                   jax.ShapeDtypeStruct((2, S, 512), BF16)],
        scratch_shapes=[pltpu.VMEM((2 * TQ, HP), BF16), pltpu.VMEM((2 * TQ, HP), BF16),
                        pltpu.VMEM((2 * TQ, HP), F32), pltpu.VMEM((2 * TQ, HP), F32)],
        compiler_params=_cp("parallel", "arbitrary"),
    )(qkv, qkv, qkv, qkv, qkv, do, bias)
    return dq, db, dkv


def bias_grad(db, name):
    def body(db_ref, o_ref):
        r = lax.broadcasted_iota(jnp.int32, (128, 128), 0)
        c = lax.broadcasted_iota(jnp.int32, (128, 128), 1)
        flip = (r + c == 127).astype(BF16)
        lane = lax.broadcasted_iota(jnp.int32, (16, 384), 1)
        src = lax.broadcasted_iota(jnp.int32, (128, 384), 0)
        dst = lax.broadcasted_iota(jnp.int32, (128, 384), 1)

        def split_dot(v, m):
            hi = v.astype(BF16)
            r1 = v - hi.astype(F32)
            mid = r1.astype(BF16)
            lo = (r1 - mid.astype(F32)).astype(BF16)
            return _dot(hi, m) + _dot(mid, m) + _dot(lo, m)

        def diag_sums(w):
            y = pltpu.roll(split_dot(w, flip), 0, 1, stride=1, stride_axis=0)
            return jnp.broadcast_to(_colsum(y), (16, 128))

        w4 = db_ref[0, :, 512:640]
        w3 = db_ref[0, :, 384:512]
        far = jnp.sum(db_ref[0, :, 0:384]) + jnp.sum(jnp.where(r >= c, w3, 0.0))
        lo4 = diag_sums(jnp.where(r >= c, w4, 0.0))
        up4 = diag_sums(jnp.where(r < c, w4, 0.0))
        up3 = diag_sums(jnp.where(r < c, w3, 0.0))
        p_lo4 = (dst == 128 + (src + 1) % 128).astype(BF16)
        p_up4 = ((dst == src + 1) & (src < 127)).astype(BF16)
        p_up3 = ((dst == src + 129) & (src < 127)).astype(BF16)
        out = split_dot(lo4, p_lo4) + split_dot(up4, p_up4) + split_dot(up3, p_up3)
        o_ref[0] = out + jnp.where(lane == 256, far, 0.0)

    return pl.pallas_call(
        body, name=name, grid=(8,),
        in_specs=[pl.BlockSpec((1, 128, WIN), lambda h: (h, 0, 0))],
        out_specs=pl.BlockSpec((1, 16, 384), lambda h: (h, 0, 0)),
        out_shape=jax.ShapeDtypeStruct((8, 16, 384), F32),
        compiler_params=_cp("parallel"),
    )(db)[:, 0, :]


LT = 256
LC = 512


def _lru_gates(xs, pv_ref, wa_ref, wx_ref, tl):
    xc = (pv_ref[4:5, :] + pv_ref[3:4, :] * xs[pl.ds(8, tl), :] + pv_ref[2:3, :] * xs[pl.ds(7, tl), :]
          + pv_ref[1:2, :] * xs[pl.ds(6, tl), :] + pv_ref[0:1, :] * xs[pl.ds(5, tl), :])
    xcb = xc.astype(BF16)
    pa = jnp.concatenate([_dot(xcb[:, 0:256], wa_ref[0]), _dot(xcb[:, 256:512], wa_ref[1])], axis=1)
    px = jnp.concatenate([_dot(xcb[:, 0:256], wx_ref[0]), _dot(xcb[:, 256:512], wx_ref[1])], axis=1)
    r = _sigmoid(pa + pv_ref[5:6, :])
    ig = _sigmoid(px + pv_ref[6:7, :])
    z = -pv_ref[7:8, :]
    sp = jnp.maximum(z, 0.0) + jnp.log1p(jnp.exp(-jnp.abs(z)))
    log_a = (-LRU_C * r) * sp
    a = jnp.exp(log_a)
    mult = jnp.sqrt(-_expm1(2.0 * log_a))
    return xc, xcb, r, ig, sp, a, mult


def lru_fwd(rest, pvec, wa, wx, name):
    S = rest.shape[0]
    tl = min(LT, S)
    nt = S // tl

    def body(xr_ref, halo_ref, yr_ref, pv_ref, wa_ref, wx_ref, h_ref, hg_ref, xs, a_s, u_s, h_s, carry):
        ti = pl.program_id(1)

        @pl.when(ti == 0)
        def _():
            carry[...] = jnp.zeros_like(carry)

        xs[0:8, :] = jnp.where(ti > 0, halo_ref[...], 0.0)
        xs[pl.ds(8, tl), :] = xr_ref[...]
        xc, _, _, ig, _, a, mult = _lru_gates(xs, pv_ref, wa_ref, wx_ref, tl)
        a_s[...] = a
        u_s[...] = mult * (ig * xc)
        row = lax.broadcasted_iota(jnp.int32, (8, LC), 0)

        def blk(bi, c):
            o = pl.multiple_of(bi * 8, 8)
            av = a_s[pl.ds(o, 8), :]
            bv = u_s[pl.ds(o, 8), :]
            for d in (1, 2, 4):
                a_sh = pltpu.roll(av, d, 0)
                b_sh = pltpu.roll(bv, d, 0)
                m = row >= d
                bv = jnp.where(m, av * b_sh + bv, bv)
                av = jnp.where(m, av * a_sh, av)
            hv = bv + av * c
            h_s[pl.ds(o, 8), :] = hv
            return hv[7:8, :]

        carry[...] = lax.fori_loop(0, tl // 8, blk, carry[...])
        h = h_s[...]
        h_ref[...] = h
        hg_ref[...] = (h * _gelu(yr_ref[...])).astype(BF16)

    hb = tl // 8
    return pl.pallas_call(
        body, name=name, grid=(2, nt),
        in_specs=[pl.BlockSpec((tl, LC), lambda c, t: (t, c)),
                  pl.BlockSpec((8, LC), lambda c, t: (jnp.maximum(t * hb - 1, 0), c)),
                  pl.BlockSpec((tl, LC), lambda c, t: (t, 2 + c)),
                  pl.BlockSpec((8, LC), lambda c, t: (0, c)),
                  pl.BlockSpec((2, 256, 256), lambda c, t: (c, 0, 0)),
                  pl.BlockSpec((2, 256, 256), lambda c, t: (c, 0, 0))],
        out_specs=[pl.BlockSpec((tl, LC), lambda c, t: (t, c)), pl.BlockSpec((tl, LC), lambda c, t: (t, c))],
        out_shape=[jax.ShapeDtypeStruct((S, D), F32), jax.ShapeDtypeStruct((S, D), BF16)],
        scratch_shapes=[pltpu.VMEM((tl + 8, LC), F32), pltpu.VMEM((tl, LC), F32), pltpu.VMEM((tl, LC), F32),
                        pltpu.VMEM((tl, LC), F32), pltpu.VMEM((1, LC), F32)],
        compiler_params=_cp("parallel", "arbitrary"),
    )(rest, rest, rest, pvec, wa, wx)


def lru_bwd(dh, h, rest, pvec, wa, wx, name):
    S = rest.shape[0]
    tl = min(LT, S)
    nt = S // tl

    def body(dh_ref, h_ref, hhalo_ref, xr_ref, xhalo_ref, pv_ref, wa_ref, wx_ref,
             dxr_ref, vacc_ref, dwa_ref, dwx_ref,
             xs, hs, a_s, ash_s, b_s, lam_s, dxe, anext, lnext, dxnext):
        ti = pl.program_id(1)
        tr = nt - 1 - ti

        @pl.when(ti == 0)
        def _():
            anext[...] = jnp.zeros_like(anext)
            lnext[...] = jnp.zeros_like(lnext)
            dxnext[...] = jnp.zeros_like(dxnext)
            vacc_ref[...] = jnp.zeros_like(vacc_ref)
            dwa_ref[...] = jnp.zeros_like(dwa_ref)
            dwx_ref[...] = jnp.zeros_like(dwx_ref)

        xs[0:8, :] = jnp.where(tr > 0, xhalo_ref[...], 0.0)
        xs[pl.ds(8, tl), :] = xr_ref[...]
        xc, xcb, r, ig, sp, a, mult = _lru_gates(xs, pv_ref, wa_ref, wx_ref, tl)

        a_s[pl.ds(0, tl), :] = a
        a_s[pl.ds(tl, 8), :] = jnp.broadcast_to(anext[...], (8, LC))
        ash_s[...] = a_s[pl.ds(1, tl), :]
        b_s[...] = dh_ref[...]
        row = lax.broadcasted_iota(jnp.int32, (8, LC), 0)

        def blk(k, c):
            o = pl.multiple_of((tl // 8 - 1 - k) * 8, 8)
            av = ash_s[pl.ds(o, 8), :]
            bv = b_s[pl.ds(o, 8), :]
            for d in (1, 2, 4):
                a_sh = pltpu.roll(av, 8 - d, 0)
                b_sh = pltpu.roll(bv, 8 - d, 0)
                m = row < 8 - d
                bv = jnp.where(m, bv + av * b_sh, bv)
                av = jnp.where(m, av * a_sh, av)
            lv = bv + av * c
            lam_s[pl.ds(o, 8), :] = lv
            return lv[0:1, :]

        lnext[...] = lax.fori_loop(0, tl // 8, blk, lnext[...])
        anext[...] = a[0:1, :]
        lam = lam_s[...]

        hs[0:8, :] = jnp.where(tr > 0, hhalo_ref[...], 0.0)
        hs[pl.ds(8, tl), :] = h_ref[...]
        d_a = lam * hs[pl.ds(7, tl), :]
        d_mult = lam * (ig * xc)
        d_ig = lam * mult * xc
        dxc = lam * mult * ig
        d_log_a = d_a * a - d_mult * (a * a) / mult
        d_r = d_log_a * (-LRU_C * sp)
        vacc_ref[7:8, :] += _colsum(d_log_a * (-LRU_C * r)) * (-_sigmoid(-pv_ref[7:8, :]))
        d_pa = d_r * r * (1.0 - r)
        d_px = d_ig * ig * (1.0 - ig)
        vacc_ref[5:6, :] += _colsum(d_pa)
        vacc_ref[6:7, :] += _colsum(d_px)
        dpa = d_pa.astype(BF16)
        dpx = d_px.astype(BF16)
        back = []
        for g in range(2):
            sl = slice(256 * g, 256 * g + 256)
            dwa_ref[g] += _dot_tn(xcb[:, sl], dpa[:, sl])
            dwx_ref[g] += _dot_tn(xcb[:, sl], dpx[:, sl])
            back.append(_dot_nt(dpa[:, sl], wa_ref[g]) + _dot_nt(dpx[:, sl], wx_ref[g]))
        dxc = dxc + jnp.concatenate(back, axis=1)
        vacc_ref[4:5, :] += _colsum(dxc)
        for k in range(4):
            vacc_ref[k:k + 1, :] += _colsum(dxc * xs[pl.ds(5 + k, tl), :])
        dxe[pl.ds(0, tl), :] = dxc
        dxe[pl.ds(tl, 8), :] = dxnext[...]
        dxr = (pv_ref[3:4, :] * dxc + pv_ref[2:3, :] * dxe[pl.ds(1, tl), :]
               + pv_ref[1:2, :] * dxe[pl.ds(2, tl), :] + pv_ref[0:1, :] * dxe[pl.ds(3, tl), :])
        dxr_ref[...] = dxr.astype(BF16)
        dxnext[...] = dxc[0:8, :]

    hb = tl // 8
    rev = lambda t: nt - 1 - t
    halo = lambda t: jnp.maximum(rev(t) * hb - 1, 0)
    big = lambda: pltpu.VMEM((tl + 8, LC), F32)
    til = lambda: pltpu.VMEM((tl, LC), F32)
    return pl.pallas_call(
        body, name=name, grid=(2, nt),
        in_specs=[pl.BlockSpec((tl, LC), lambda c, t: (rev(t), c)),
                  pl.BlockSpec((tl, LC), lambda c, t: (rev(t), c)),
                  pl.BlockSpec((8, LC), lambda c, t: (halo(t), c)),
                  pl.BlockSpec((tl, LC), lambda c, t: (rev(t), c)),
                  pl.BlockSpec((8, LC), lambda c, t: (halo(t), c)),
                  pl.BlockSpec((8, LC), lambda c, t: (0, c)),
                  pl.BlockSpec((2, 256, 256), lambda c, t: (c, 0, 0)),
                  pl.BlockSpec((2, 256, 256), lambda c, t: (c, 0, 0))],
        out_specs=[pl.BlockSpec((tl, LC), lambda c, t: (rev(t), c)),
                   pl.BlockSpec((8, LC), lambda c, t: (0, c)),
                   pl.BlockSpec((2, 256, 256), lambda c, t: (c, 0, 0)),
                   pl.BlockSpec((2, 256, 256), lambda c, t: (c, 0, 0))],
        out_shape=[jax.ShapeDtypeStruct((S, D), BF16), jax.ShapeDtypeStruct((8, D), F32),
                   jax.ShapeDtypeStruct((4, 256, 256), F32), jax.ShapeDtypeStruct((4, 256, 256), F32)],
        scratch_shapes=[big(), big(), big(), til(), til(), til(), big(),
                        pltpu.VMEM((1, LC), F32), pltpu.VMEM((1, LC), F32), pltpu.VMEM((8, LC), F32)],
        compiler_params=_cp("parallel", "arbitrary"),
    )(dh, h, h, rest, rest, pvec, wa, wx)


def mix_out_fwd(x, ao, hg, rest, vec, w_att_o, w_rec_o, w_out, name, tm=256):
    S = x.shape[0]
    tm = min(tm, S)

    def body(x_ref, ao_ref, hg_ref, ga_ref, gr_ref, vec_ref, wa_ref, wr_ref, wo_ref,
             xo_ref, att_ref, rec_ref, mg_ref, f_ref):
        att = _dot(ao_ref[...], wa_ref[...])
        rec = _dot(hg_ref[...], wr_ref[...])
        att_ref[...] = att
        rec_ref[...] = rec
        mg = (_sigmoid(ga_ref[...]) * att + _sigmoid(gr_ref[...]) * rec).astype(BF16)
        mg_ref[...] = mg
        f = _dot(mg, wo_ref[...])
        f_ref[...] = f
        y = f * lax.rsqrt(_mean(f * f) + EPS) * vec_ref[1:2, :]
        xo_ref[...] = x_ref[...] + (1.0 * vec_ref[4:5, :]) * y

    row = lambda i: (i, 0)
    full = lambda r: pl.BlockSpec((r, D), lambda i: (0, 0))
    return pl.pallas_call(
        body, name=name, grid=(S // tm,),
        in_specs=[pl.BlockSpec((tm, D), row), pl.BlockSpec((tm, 512), row), pl.BlockSpec((tm, D), row),
                  pl.BlockSpec((tm, D), lambda i: (i, 2)), pl.BlockSpec((tm, D), lambda i: (i, 3)),
                  full(8), full(512), full(D), full(D)],
        out_specs=[pl.BlockSpec((tm, D), row)] * 5,
        out_shape=[jax.ShapeDtypeStruct((S, D), F32), jax.ShapeDtypeStruct((S, D), F32),
                   jax.ShapeDtypeStruct((S, D), F32), jax.ShapeDtypeStruct((S, D), BF16),
                   jax.ShapeDtypeStruct((S, D), F32)],
        compiler_params=_cp("parallel"),
    )(x, ao, hg, rest, rest, vec, w_att_o, w_rec_o, w_out)


def mix_out_bwd(dxo, f, att, rec, rest, h, vec, w_att_o, w_rec_o, w_out, name, tm=256):
    S = dxo.shape[0]
    tm = min(tm, S)

    def body(dxo_ref, f_ref, att_ref, rec_ref, yr_ref, ga_ref, gr_ref, h_ref, vec_ref, wa_ref, wr_ref, wo_ref,
             df_ref, da_ref, dr_ref, dao_ref, dh_ref, d3_ref, vacc_ref):
        @pl.when(pl.program_id(0) == 0)
        def _():
            vacc_ref[...] = jnp.zeros_like(vacc_ref)

        df = _post_norm_bwd(dxo_ref[...], f_ref[...], 1.0, vec_ref, vacc_ref).astype(BF16)
        df_ref[...] = df
        dm = _dot_nt(df, wo_ref[...])
        sa = _sigmoid(ga_ref[...])
        sr = _sigmoid(gr_ref[...])
        d_att = (dm * sa).astype(BF16)
        d_rec = (dm * sr).astype(BF16)
        da_ref[...] = d_att
        dr_ref[...] = d_rec
        d3_ref[1] = (dm * att_ref[...] * (sa * (1.0 - sa))).astype(BF16)
        d3_ref[2] = (dm * rec_ref[...] * (sr * (1.0 - sr))).astype(BF16)
        dao_ref[...] = _dot_nt(d_att, wa_ref[...]).astype(BF16)
        d_hg = _dot_nt(d_rec, wr_ref[...])
        yr = yr_ref[...]
        dh_ref[...] = d_hg * _gelu(yr)
        d3_ref[0] = (d_hg * h_ref[...] * _gelu_grad(yr)).astype(BF16)

    row = lambda i: (i, 0)
    full = lambda r: pl.BlockSpec((r, D), lambda i: (0, 0))
    return pl.pallas_call(
        body, name=name, grid=(S // tm,),
        in_specs=[pl.BlockSpec((tm, D), row)] * 4
        + [pl.BlockSpec((tm, D), lambda i: (i, 1)), pl.BlockSpec((tm, D), lambda i: (i, 2)),
           pl.BlockSpec((tm, D), lambda i: (i, 3)), pl.BlockSpec((tm, D), row),
           full(8), full(512), full(D), full(D)],
        out_specs=[pl.BlockSpec((tm, D), row)] * 3
        + [pl.BlockSpec((tm, 512), row), pl.BlockSpec((tm, D), row),
           pl.BlockSpec((3, tm, D), lambda i: (0, i, 0)), pl.BlockSpec((8, D), lambda i: (0, 0))],
        out_shape=[jax.ShapeDtypeStruct((S, D), BF16)] * 3
        + [jax.ShapeDtypeStruct((S, 512), BF16), jax.ShapeDtypeStruct((S, D), F32),
           jax.ShapeDtypeStruct((3, S, D), BF16), jax.ShapeDtypeStruct((8, D), F32)],
        compiler_params=_cp("arbitrary"),
    )(dxo, f, att, rec, rest, rest, rest, h, vec, w_att_o, w_rec_o, w_out)


def loss_grad(y, tgt, name, tm=512):
    S = y.shape[0]
    tm = min(tm, S)
    nt = S // tm

    def body(y_ref, t_ref, dy_ref, l_ref, acc):
        i = pl.program_id(0)

        @pl.when(i == 0)
        def _():
            acc[...] = jnp.zeros_like(acc)

        d = y_ref[...] - t_ref[...]
        dy_ref[...] = d * (1.0 / D)
        acc[...] += _colsum(d * d)

        @pl.when(i == nt - 1)
        def _():
            l_ref[...] = jnp.broadcast_to(0.5 * jnp.sum(acc[...]) * (1.0 / D), (8, 128))

    return pl.pallas_call(
        body, name=name, grid=(nt,),
        in_specs=[pl.BlockSpec((tm, D), lambda i: (i, 0))] * 2,
        out_specs=[pl.BlockSpec((tm, D), lambda i: (i, 0)), pl.BlockSpec((8, 128), lambda i: (0, 0))],
        out_shape=[jax.ShapeDtypeStruct((S, D), F32), jax.ShapeDtypeStruct((8, 128), F32)],
        scratch_shapes=[pltpu.VMEM((1, D), F32)],
        compiler_params=_cp("arbitrary"),
    )(y, tgt)


def ada_fwd(c_all, w_ada, b_ada, name, tn=768):
    n = w_ada.shape[1]

    def body(c_ref, w_ref, b_ref, o_ref):
        cv = c_ref[...]
        ca = (cv * _sigmoid(cv)).astype(BF16)
        o_ref[...] = _dot(ca, w_ref[...].astype(BF16)) + b_ref[...]

    return pl.pallas_call(
        body, name=name, grid=(n // tn,),
        in_specs=[pl.BlockSpec((8, D), lambda j: (0, 0)), pl.BlockSpec((D, tn), lambda j: (0, j)),
                  pl.BlockSpec((1, tn), lambda j: (0, j))],
        out_specs=pl.BlockSpec((8, tn), lambda j: (0, j)),
        out_shape=jax.ShapeDtypeStruct((8, n), F32),
        compiler_params=_cp("parallel"),
    )(c_all, w_ada, b_ada)


def ada_bwd(c_all_t, dmod, name, tn=768):
    n = dmod.shape[1]

    def body(c_ref, d_ref, o_ref):
        cv = c_ref[...]
        ca = (cv * _sigmoid(cv)).astype(BF16)
        o_ref[...] = _dot(ca, d_ref[...].astype(BF16))

    return pl.pallas_call(
        body, name=name, grid=(n // tn,),
        in_specs=[pl.BlockSpec((D, 128), lambda j: (0, 0)), pl.BlockSpec((128, tn), lambda j: (0, j))],
        out_specs=pl.BlockSpec((D, tn), lambda j: (0, j)),
        out_shape=jax.ShapeDtypeStruct((D, n), F32),
        compiler_params=_cp("parallel"),
    )(c_all_t, dmod)


def _row_tile(rows, cols, itemsize=4, budget=1536 * 1024):
    best = None
    for t in range(8, rows + 1, 8):
        if rows % t == 0 and t * cols * itemsize <= budget:
            best = t
    return rows if best is None else best


def sum_lead(parts, name, out_dtype=F32):
    n, R, C = parts.shape
    tr = _row_tile(R, C * n)

    def body(p_ref, o_ref):
        acc = p_ref[0].astype(F32)
        for k in range(1, n):
            acc = acc + p_ref[k].astype(F32)
        o_ref[...] = acc.astype(out_dtype)

    return pl.pallas_call(
        body, name=name, grid=(R // tr,),
        in_specs=[pl.BlockSpec((n, tr, C), lambda i: (0, i, 0))],
        out_specs=pl.BlockSpec((tr, C), lambda i: (i, 0)),
        out_shape=jax.ShapeDtypeStruct((R, C), out_dtype),
        compiler_params=_cp("parallel"),
    )(parts)


def adamw(w, g, m, v, name):
    R, C = w.shape
    tr = _row_tile(R, C * 7, budget=8 * 1024 * 1024)

    def body(w_ref, g_ref, m_ref, v_ref, d_ref, mo_ref, vo_ref):
        gv = g_ref[...]
        mn = ADAM_B1 * m_ref[...] + (1.0 - ADAM_B1) * gv
        vn = ADAM_B2 * v_ref[...] + (1.0 - ADAM_B2) * (gv * gv)
        m_hat = mn / (1.0 - ADAM_B1 ** ADAM_STEP)
        v_hat = vn / (1.0 - ADAM_B2 ** ADAM_STEP)
        d_ref[...] = -ADAM_LR * (m_hat / (jnp.sqrt(v_hat) + ADAM_EPS) + ADAM_WD * w_ref[...])
        mo_ref[...] = mn
        vo_ref[...] = vn

    spec = pl.BlockSpec((tr, C), lambda i: (i, 0))
    return pl.pallas_call(
        body, name=name, grid=(R // tr,),
        in_specs=[spec] * 4, out_specs=[spec] * 3,
        out_shape=[jax.ShapeDtypeStruct((R, C), F32)] * 3,
        compiler_params=_cp("parallel"),
    )(w, g, m, v)


def _mesh_pos():
    return lax.axis_index("x"), lax.axis_index("y"), lax.axis_index("c")


def _other_chips(mx, my):
    return [(1 - mx, my), (mx, 1 - my), (1 - mx, 1 - my)]


def ag_small(x, name):
    R = x.shape[0]

    def body(x_ref, out_ref, send_sems, recv_sems, local_sem):
        mx, my, mc = _mesh_pos()
        me, sibling = (mx, my, mc), (mx, my, 1 - mc)
        chips = _other_chips(mx, my)

        def slot(px, py, pc):
            return out_ref.at[4 * px + 2 * py + pc]

        def copy(k, block, to, src=None):
            return pltpu.make_async_remote_copy(
                src_ref=slot(*block) if src is None else src, dst_ref=slot(*block),
                send_sem=send_sems.at[k], recv_sem=recv_sems.at[k], device_id=to, device_id_type=MESH)

        mine = pltpu.make_async_copy(x_ref, slot(*me), local_sem)
        mine.start()
        first = [copy(0, me, sibling, src=x_ref)]
        first += [copy(1 + j, me, (*chip, mc), src=x_ref) for j, chip in enumerate(chips)]
        for cp in first:
            cp.start()
        passed = [copy(4 + j, (*chip, mc), sibling) for j, chip in enumerate(chips)]
        for j, chip in enumerate(chips):
            copy(1 + j, (*chip, mc), me).wait_recv()
            passed[j].start()
        copy(0, sibling, me).wait_recv()
        for j, chip in enumerate(chips):
            copy(4 + j, (*chip, 1 - mc), me).wait_recv()
        for cp in first + passed:
            cp.wait_send()
        mine.wait()

    return pl.pallas_call(
        body, name=name,
        out_shape=jax.ShapeDtypeStruct((N_DEV, R, 128), F32),
        in_specs=[pl.BlockSpec(memory_space=pltpu.VMEM)],
        out_specs=pl.BlockSpec(memory_space=pltpu.VMEM),
        scratch_shapes=[pltpu.SemaphoreType.DMA((7,)), pltpu.SemaphoreType.DMA((7,)), pltpu.SemaphoreType.DMA],
        compiler_params=pltpu.CompilerParams(vmem_limit_bytes=VMEM_LIMIT),
    )(x)


BIG = (("ffn1_w_gu", "col", D, PW), ("ffn1_w_down", "row", FF, D), ("w_in", "col", D, PW),
       ("w_att_o", "col", 512, D), ("w_rec_o", "row", D, D), ("w_out", "row", D, D),
       ("ffn2_w_gu", "col", D, PW), ("ffn2_w_down", "row", FF, D))
NBIG = len(BIG)


def _shard_shape(kind, R, C):
    return (R, C // 4) if kind == "col" else (R // 4, C)


def _region(ref, kind, R, C, q, half, t, tr):
    sr, sc = _shard_shape(kind, R, C)
    if kind == "col":
        return ref.at[pl.ds(pl.multiple_of(half * (R // 2) + t * tr, 16), tr), pl.ds(q * sc, sc)]
    return ref.at[pl.ds(pl.multiple_of(q * sr + t * tr, 16), tr), pl.ds(half * (C // 2), C // 2)]


def ag_push(w, kind, R, C, c_arr, name):
    sr, sc = _shard_shape(kind, R, C)
    hr, hc = (sr // 2, sc) if kind == "col" else (sr, sc // 2)
    tr = _row_tile(hr, hc, itemsize=2, budget=512 * 1024)
    nt = hr // tr

    def body(c_ref, mine_ref, other_ref, full_ref, stage, lsem, ssem, rsem):
        i = pl.program_id(0)
        par = i % 2
        mx, my, mc = _mesh_pos()
        p = 2 * mx + my
        chips = _other_chips(mx, my)

        def copies(s, q, h, t):
            mine = _region(full_ref, kind, R, C, q, h, t, tr)
            other = _region(full_ref, kind, R, C, q, 1 - h, t, tr)
            out = [pltpu.make_async_remote_copy(src_ref=stage.at[s, 0], dst_ref=mine, send_sem=ssem.at[s, k],
                                                recv_sem=rsem.at[k], device_id=(*chips[k], mc), device_id_type=MESH)
                   for k in range(3)]
            out.append(pltpu.make_async_copy(stage.at[s, 0], mine, lsem.at[s, 0]))
            out.append(pltpu.make_async_copy(stage.at[s, 1], other, lsem.at[s, 1]))
            return out

        def wait_sent(s):
            cps = copies(s, 0, 0, 0)
            for cp in cps[:3]:
                cp.wait_send()
            for cp in cps[3:]:
                cp.wait()

        @pl.when(i >= 2)
        def _():
            wait_sent(par)

        stage[par, 0] = mine_ref[...].astype(BF16)
        stage[par, 1] = other_ref[...].astype(BF16)
        if kind == "col":
            for q in range(4):
                @pl.when(p == q)
                def _(q=q):
                    for cp in copies(par, q, mc, i):
                        cp.start()
        else:
            for h in range(2):
                @pl.when(mc == h)
                def _(h=h):
                    for cp in copies(par, p, h, i):
                        cp.start()

        @pl.when(i == nt - 1)
        def _():
            for s in range(min(nt, 2)):
                wait_sent(s)
            for k in range(3):
                half = full_ref.at[pl.ds(0, hr), pl.ds(0, hc)]
                pltpu.make_async_remote_copy(src_ref=half, dst_ref=half, send_sem=ssem.at[0, k], recv_sem=rsem.at[k],
                                             device_id=(*chips[k], mc), device_id_type=MESH).wait_recv()

    if kind == "col":
        mine_spec = pl.BlockSpec((tr, hc), lambda i, c: (c[0] * nt + i, 0))
        other_spec = pl.BlockSpec((tr, hc), lambda i, c: ((1 - c[0]) * nt + i, 0))
    else:
        mine_spec = pl.BlockSpec((tr, hc), lambda i, c: (i, c[0]))
        other_spec = pl.BlockSpec((tr, hc), lambda i, c: (i, 1 - c[0]))
    return pl.pallas_call(
        body, name=name,
        grid_spec=pltpu.PrefetchScalarGridSpec(
            num_scalar_prefetch=1, grid=(nt,), in_specs=[mine_spec, other_spec], out_specs=ANY,
            scratch_shapes=[pltpu.VMEM((2, 2, tr, hc), BF16), pltpu.SemaphoreType.DMA((2, 2)),
                            pltpu.SemaphoreType.DMA((2, 3)), pltpu.SemaphoreType.DMA((3,))]),
        out_shape=jax.ShapeDtypeStruct((R, C), BF16),
        compiler_params=_cp("arbitrary"),
    )(c_arr, w, w)


def ag_local(w, kind, R, C, p_arr, name):
    sr, sc = _shard_shape(kind, R, C)
    tr = _row_tile(sr, sc, budget=2 * 1024 * 1024)
    nt = sr // tr

    def body(p_ref, w_ref, o_ref):
        o_ref[...] = w_ref[...].astype(BF16)

    if kind == "col":
        o_spec = pl.BlockSpec((tr, sc), lambda i, p: (i, p[0]))
    else:
        o_spec = pl.BlockSpec((tr, sc), lambda i, p: (p[0] * nt + i, 0))
    return pl.pallas_call(
        body, name=name,
        grid_spec=pltpu.PrefetchScalarGridSpec(
            num_scalar_prefetch=1, grid=(nt,), in_specs=[pl.BlockSpec((tr, sc), lambda i, p: (i, 0))],
            out_specs=o_spec),
        out_shape=jax.ShapeDtypeStruct((R, C), BF16),
        compiler_params=_cp("parallel"),
    )(p_arr, w)


HBM_SPEC = pl.BlockSpec(memory_space=pltpu.HBM)
SEM_SPEC = pl.BlockSpec(memory_space=pltpu.SEMAPHORE)


def _ag_copies(fulls, geoms, ssem, rsem, mx, my, mc, q, h):
    chips = _other_chips(mx, my)
    out = []
    for w, (kind, R, C) in enumerate(geoms):
        sr, sc = _shard_shape(kind, R, C)
        hr = sr // 2 if kind == "col" else sr
        reg = _region(fulls[w], kind, R, C, q, h, 0, hr)
        out += [pltpu.make_async_remote_copy(src_ref=reg, dst_ref=reg, send_sem=ssem.at[3 * w + k],
                                             recv_sem=rsem.at[3 * w + k], device_id=(*chips[k], mc),
                                             device_id_type=MESH) for k in range(3)]
    return out


def ag_start(fulls, geoms, after, name):
    n = len(fulls)

    def body(*refs):
        ssem, rsem = refs[n + 1:n + 3]
        outs, token = refs[n + 3:2 * n + 3], refs[2 * n + 3]
        mx, my, mc = _mesh_pos()
        p = 2 * mx + my
        col = [w for w, g in enumerate(geoms) if g[0] == "col"]
        row = [w for w, g in enumerate(geoms) if g[0] == "row"]
        for q in range(4):
            @pl.when(p == q)
            def _(q=q):
                cps = _ag_copies(outs, geoms, ssem, rsem, mx, my, mc, q, mc)
                for w in col:
                    for cp in cps[3 * w:3 * w + 3]:
                        cp.start()
        for h in range(2):
            @pl.when(mc == h)
            def _(h=h):
                cps = _ag_copies(outs, geoms, ssem, rsem, mx, my, mc, p, h)
                for w in row:
                    for cp in cps[3 * w:3 * w + 3]:
                        cp.start()
        token[...] = jnp.zeros_like(token)

    res = pl.pallas_call(
        body, name=name,
        out_shape=[pltpu.SemaphoreType.DMA((3 * n,)), pltpu.SemaphoreType.DMA((3 * n,))]
        + [pltpu.HBM(a.shape, a.dtype) for a in fulls] + [jax.ShapeDtypeStruct((8, 128), F32)],
        in_specs=[HBM_SPEC] * n + [ANY],
        out_specs=[SEM_SPEC, SEM_SPEC] + [HBM_SPEC] * n + [pl.BlockSpec(memory_space=pltpu.VMEM)],
        input_output_aliases={w: 2 + w for w in range(n)},
        compiler_params=pltpu.CompilerParams(has_side_effects=pltpu.SideEffectType.DATAFLOW_SIDE_EFFECTING),
    )(*[pltpu.with_memory_space_constraint(a, pltpu.HBM) for a in fulls], after)
    return res[0], res[1], list(res[2:2 + n]), res[2 + n]


def ag_wait(fulls, geoms, ssem, rsem, after, name):
    n = len(fulls)

    def body(*refs):
        ins, ssem_ref, rsem_ref = refs[:n], refs[n], refs[n + 1]
        mx, my, mc = _mesh_pos()
        for cp in _ag_copies(ins, geoms, ssem_ref, rsem_ref, mx, my, mc, 0, 0):
            cp.wait_send()
            cp.wait_recv()

    return list(pl.pallas_call(
        body, name=name,
        out_shape=[pltpu.HBM(a.shape, a.dtype) for a in fulls],
        in_specs=[HBM_SPEC] * n + [SEM_SPEC, SEM_SPEC, ANY],
        out_specs=[HBM_SPEC] * n,
        input_output_aliases={w: w for w in range(n)},
        compiler_params=pltpu.CompilerParams(has_side_effects=pltpu.SideEffectType.DATAFLOW_SIDE_EFFECTING),
    )(*fulls, ssem, rsem, after))


def ag_forward(full, kind, R, C, name):
    sr, sc = _shard_shape(kind, R, C)
    hr, hc = (sr // 2, sc) if kind == "col" else (sr, sc // 2)
    tr = _row_tile(hr, hc, itemsize=2, budget=512 * 1024)
    nt = hr // tr

    def body(src_ref, full_ref, stage, lsem, ssem, rsem):
        k, i = pl.program_id(0), pl.program_id(1)
        step = k * nt + i
        par = step % 2
        mx, my, mc = _mesh_pos()
        q_k = _partner_chip(k, 2 * mx + my)

        def push(s, dst):
            return pltpu.make_async_remote_copy(src_ref=stage.at[s], dst_ref=dst, send_sem=ssem.at[s], recv_sem=rsem,
                                                device_id=(mx, my, 1 - mc), device_id_type=MESH)

        @pl.when(step >= 2)
        def _():
            push(par, _region(full_ref, kind, R, C, 0, 0, 0, tr)).wait_send()

        def move(q, h):
            load = pltpu.make_async_copy(_region(src_ref, kind, R, C, q, h, i, tr), stage.at[par], lsem)
            load.start()
            load.wait()
            push(par, _region(full_ref, kind, R, C, q, h, i, tr)).start()

        if kind == "col":
            for q in range(4):
                @pl.when(q_k == q)
                def _(q=q):
                    move(q, mc)
        else:
            for h in range(2):
                @pl.when(mc == h)
                def _(h=h):
                    move(q_k, h)

        @pl.when(step == 3 * nt - 1)
        def _():
            for s in range(2):
                push(s, _region(full_ref, kind, R, C, 0, 0, 0, tr)).wait_send()
            three = full_ref.at[pl.ds(0, hr), pl.ds(0, 3 * hc)] if kind == "col" else full_ref.at[pl.ds(0, 3 * hr), pl.ds(0, hc)]
            pltpu.make_async_remote_copy(src_ref=three, dst_ref=three, send_sem=ssem.at[0], recv_sem=rsem,
                                         device_id=(mx, my, 1 - mc), device_id_type=MESH).wait_recv()

    return pl.pallas_call(
        body, name=name, grid=(3, nt),
        in_specs=[ANY], out_specs=ANY,
        out_shape=jax.ShapeDtypeStruct((R, C), BF16),
        scratch_shapes=[pltpu.VMEM((2, tr, hc), BF16), pltpu.SemaphoreType.DMA, pltpu.SemaphoreType.DMA((2,)),
                        pltpu.SemaphoreType.DMA],
        input_output_aliases={0: 0},
        compiler_params=_cp("arbitrary", "arbitrary"),
    )(full)


def _half_shape(kind, R, C):
    return (R // 2, C) if kind == "col" else (R, C // 2)


def _piece_shape(kind, R, C):
    return (R // 2, C // 4) if kind == "col" else (R // 4, C // 2)


def pair_push(g, kind, c_arr, name):
    R, C = g.shape
    hr, hc = _half_shape(kind, R, C)
    tr = _row_tile(hr, hc, itemsize=2, budget=1024 * 1024)
    nt = hr // tr

    def body(c_ref, g_ref, out_ref, stage, ssem, rsem):
        i = pl.program_id(0)
        slot = i % 2
        mx, my, mc = _mesh_pos()

        def push(s, t):
            return pltpu.make_async_remote_copy(
                src_ref=stage.at[s], dst_ref=out_ref.at[pl.ds(pl.multiple_of(t * tr, 16), tr)],
                send_sem=ssem.at[s], recv_sem=rsem, device_id=(mx, my, 1 - mc), device_id_type=MESH)

        @pl.when(i >= 2)
        def _():
            push(slot, 0).wait_send()

        stage[slot] = g_ref[...]
        push(slot, i).start()

        @pl.when(i == nt - 1)
        def _():
            push(slot, 0).wait_send()
            if nt >= 2:
                push(1 - slot, 0).wait_send()
            pltpu.make_async_remote_copy(src_ref=out_ref, dst_ref=out_ref, send_sem=ssem.at[0], recv_sem=rsem,
                                         device_id=(mx, my, 1 - mc), device_id_type=MESH).wait_recv()

    if kind == "col":
        g_spec = pl.BlockSpec((tr, hc), lambda i, c: ((1 - c[0]) * nt + i, 0))
    else:
        g_spec = pl.BlockSpec((tr, hc), lambda i, c: (i, 1 - c[0]))
    return pl.pallas_call(
        body, name=name,
        grid_spec=pltpu.PrefetchScalarGridSpec(
            num_scalar_prefetch=1, grid=(nt,), in_specs=[g_spec], out_specs=ANY,
            scratch_shapes=[pltpu.VMEM((2, tr, hc), BF16), pltpu.SemaphoreType.DMA((2,)), pltpu.SemaphoreType.DMA]),
        out_shape=jax.ShapeDtypeStruct((hr, hc), BF16),
        compiler_params=_cp("arbitrary"),
    )(c_arr, g)


def _partner_chip(k, p):
    return p ^ jnp.where(k == 0, 2, jnp.where(k == 1, 1, jnp.where(k == 2, 3, 0)))


def pair_add_scatter(g, got, kind, cp_arr, name):
    R, C = g.shape
    pr, pc = _piece_shape(kind, R, C)
    tr = _row_tile(pr, pc, itemsize=2, budget=768 * 1024)
    nt = pr // tr

    def body(cp_ref, g_ref, got_ref, rb_ref, stage, lsem, ssem, rsem):
        i, kk = pl.program_id(0), pl.program_id(1)
        par = i % 2
        mx, my, mc = _mesh_pos()
        p = 2 * mx + my
        chips = _other_chips(mx, my)

        def dst(t):
            return rb_ref.at[p, pl.ds(pl.multiple_of(t * tr, 16), tr)]

        def push(k, s, t):
            return pltpu.make_async_remote_copy(src_ref=stage.at[s, k], dst_ref=dst(t), send_sem=ssem.at[s, k],
                                                recv_sem=rsem.at[k], device_id=(*chips[k], mc), device_id_type=MESH)

        def keep(s, t):
            return pltpu.make_async_copy(stage.at[s, 3], dst(t), lsem.at[s])

        tile = (g_ref[...].astype(F32) + got_ref[...].astype(F32)).astype(BF16)
        for k in range(4):
            @pl.when(kk == k)
            def _(k=k):
                cp = push(k, par, i) if k < 3 else keep(par, i)

                @pl.when(i >= 2)
                def _():
                    cp.wait_send() if k < 3 else cp.wait()

                stage[par, k] = tile
                cp.start()

        @pl.when((i == nt - 1) & (kk == 3))
        def _():
            for s in range(min(nt, 2)):
                for k in range(3):
                    push(k, s, 0).wait_send()
                keep(s, 0).wait()
            for k in range(3):
                whole = rb_ref.at[0]
                pltpu.make_async_remote_copy(src_ref=whole, dst_ref=whole, send_sem=ssem.at[0, k], recv_sem=rsem.at[k],
                                             device_id=(*chips[k], mc), device_id_type=MESH).wait_recv()

    if kind == "col":
        g_spec = pl.BlockSpec((tr, pc), lambda i, k, cp: (cp[0] * nt + i, _partner_chip(k, cp[1])))
        got_spec = pl.BlockSpec((tr, pc), lambda i, k, cp: (i, _partner_chip(k, cp[1])))
    else:
        g_spec = pl.BlockSpec((tr, pc), lambda i, k, cp: (_partner_chip(k, cp[1]) * nt + i, cp[0]))
        got_spec = pl.BlockSpec((tr, pc), lambda i, k, cp: (_partner_chip(k, cp[1]) * nt + i, 0))
    return pl.pallas_call(
        body, name=name,
        grid_spec=pltpu.PrefetchScalarGridSpec(
            num_scalar_prefetch=1, grid=(nt, 4), in_specs=[g_spec, got_spec], out_specs=ANY,
            scratch_shapes=[pltpu.VMEM((2, 4, tr, pc), BF16), pltpu.SemaphoreType.DMA((2,)),
                            pltpu.SemaphoreType.DMA((2, 3)), pltpu.SemaphoreType.DMA((3,))]),
        out_shape=jax.ShapeDtypeStruct((4, pr, pc), BF16),
        compiler_params=_cp("arbitrary", "arbitrary"),
    )(cp_arr, g, got)


def sum_share(parts, kind, R, C, name):
    _, pr, pc = parts.shape
    sr, sc = _shard_shape(kind, R, C)
    tr = _row_tile(pr, pc * 4, budget=4 * 1024 * 1024)
    nt = pr // tr

    def body(p_ref, fin_ref, stage, lsem, ssem, rsem):
        i = pl.program_id(0)
        slot = i % 2
        mx, my, mc = _mesh_pos()

        def region(h, t):
            r0 = pl.multiple_of(t * tr, 8)
            if kind == "col":
                return fin_ref.at[pl.ds(pl.multiple_of(h * pr + r0, 8), tr)]
            return fin_ref.at[pl.ds(r0, tr), pl.ds(h * pc, pc)]

        def copies(s, h, t):
            return (pltpu.make_async_copy(stage.at[s], region(h, t), lsem.at[s]),
                    pltpu.make_async_remote_copy(src_ref=stage.at[s], dst_ref=region(h, t), send_sem=ssem.at[s],
                                                 recv_sem=rsem, device_id=(mx, my, 1 - mc), device_id_type=MESH))

        def wait_sent(s):
            loc, rem = copies(s, 0, 0)
            loc.wait()
            rem.wait_send()

        @pl.when(i >= 2)
        def _():
            wait_sent(slot)

        acc = p_ref[0].astype(F32)
        for k in range(1, 4):
            acc = acc + p_ref[k].astype(F32)
        stage[slot] = acc
        if kind == "col":
            for cp in copies(slot, mc, i):
                cp.start()
        else:
            for h in range(2):
                @pl.when(mc == h)
                def _(h=h):
                    for cp in copies(slot, h, i):
                        cp.start()

        @pl.when(i == nt - 1)
        def _():
            wait_sent(slot)
            if nt >= 2:
                wait_sent(1 - slot)
            half = fin_ref.at[pl.ds(0, pr), pl.ds(0, pc)]
            pltpu.make_async_remote_copy(src_ref=half, dst_ref=half, send_sem=ssem.at[0], recv_sem=rsem,
                                         device_id=(mx, my, 1 - mc), device_id_type=MESH).wait_recv()

    return pl.pallas_call(
        body, name=name, grid=(nt,),
        in_specs=[pl.BlockSpec((4, tr, pc), lambda i: (0, i, 0))],
        out_specs=ANY,
        out_shape=jax.ShapeDtypeStruct((sr, sc), F32),
        scratch_shapes=[pltpu.VMEM((2, tr, pc), F32), pltpu.SemaphoreType.DMA((2,)), pltpu.SemaphoreType.DMA((2,)),
                        pltpu.SemaphoreType.DMA],
        compiler_params=_cp("arbitrary"),
    )(parts)


def _pack(parts, rows):
    flat = []
    for a in parts:
        a = jnp.ravel(a).astype(F32)
        flat.append(jnp.pad(a, (0, (-a.shape[0]) % 128)))
    v = jnp.concatenate(flat)
    return jnp.pad(v, (0, rows * 128 - v.shape[0])).reshape(rows, 128)


def _unpack(block, shapes):
    lead = block.shape[:-2]
    v = block.reshape(lead + (-1,))
    out, off = [], 0
    for shp in shapes:
        n = int(np.prod(shp))
        out.append(v[..., off:off + n].reshape(lead + tuple(shp)))
        off += n + (-n) % 128
    return out


def _block_diag4(w):
    w4 = w.reshape(4, 4, 64, 64)
    eye = jnp.eye(4, dtype=w.dtype)
    return (w4[:, :, :, None, :] * eye[None, :, None, :, None]).reshape(4, 256, 256)


def _diag_blocks(bd):
    b5 = bd.reshape(4, 4, 64, 4, 64)
    return jnp.stack([b5[:, i, :, i, :] for i in range(4)], axis=1).reshape(16, 64, 64)


def _bias_window(rel_bias):
    m = np.arange(767)
    tv = rel_bias[:, np.clip(639 - m, -128, 128) + 128]
    win = jnp.stack([tv[:, 127 - r:127 - r + WIN] for r in range(128)], axis=1)
    qh = np.arange(128)[:, None] // CHUNK
    kc = np.arange(WIN)[None, :] // CHUNK
    valid = (kc >= qh) & (kc <= qh + 8)
    return jnp.where(jnp.asarray(valid)[None], win, NEG)


SMALL = ("b_ada", "norm_pre", "norm_post", "rel_bias", "conv_w", "conv_b", "lru_wa", "lru_ba", "lru_wx",
         "lru_bx", "lru_lambda")
WEIGHTS = ("w_ada", "b_ada", "norm_pre", "norm_post", "ffn1_w_gu", "ffn1_w_down", "w_in", "rel_bias", "conv_w",
           "conv_b", "lru_wa", "lru_ba", "lru_wx", "lru_bx", "lru_lambda", "w_att_o", "w_rec_o", "w_out",
           "ffn2_w_gu", "ffn2_w_down")


def kernel(x, c, w_ada, b_ada, norm_pre, norm_post, ffn1_w_gu, ffn1_w_down, w_in, rel_bias, conv_w, conv_b, lru_wa, lru_ba, lru_wx, lru_bx, lru_lambda, w_att_o, w_rec_o, w_out, ffn2_w_gu, ffn2_w_down, loss_target, m_w_ada, m_b_ada, m_norm_pre, m_norm_post, m_ffn1_w_gu, m_ffn1_w_down, m_w_in, m_rel_bias, m_conv_w, m_conv_b, m_lru_wa, m_lru_ba, m_lru_wx, m_lru_bx, m_lru_lambda, m_w_att_o, m_w_rec_o, m_w_out, m_ffn2_w_gu, m_ffn2_w_down, v_w_ada, v_b_ada, v_norm_pre, v_norm_post, v_ffn1_w_gu, v_ffn1_w_down, v_w_in, v_rel_bias, v_conv_w, v_conv_b, v_lru_wa, v_lru_ba, v_lru_wx, v_lru_bx, v_lru_lambda, v_w_att_o, v_w_rec_o, v_w_out, v_ffn2_w_gu, v_ffn2_w_down):
    W = dict(w_ada=w_ada, b_ada=b_ada, norm_pre=norm_pre, norm_post=norm_post, ffn1_w_gu=ffn1_w_gu,
             ffn1_w_down=ffn1_w_down, w_in=w_in, rel_bias=rel_bias, conv_w=conv_w, conv_b=conv_b, lru_wa=lru_wa,
             lru_ba=lru_ba, lru_wx=lru_wx, lru_bx=lru_bx, lru_lambda=lru_lambda, w_att_o=w_att_o, w_rec_o=w_rec_o,
             w_out=w_out, ffn2_w_gu=ffn2_w_gu, ffn2_w_down=ffn2_w_down)
    M = dict(w_ada=m_w_ada, b_ada=m_b_ada, norm_pre=m_norm_pre, norm_post=m_norm_post, ffn1_w_gu=m_ffn1_w_gu,
             ffn1_w_down=m_ffn1_w_down, w_in=m_w_in, rel_bias=m_rel_bias, conv_w=m_conv_w, conv_b=m_conv_b,
             lru_wa=m_lru_wa, lru_ba=m_lru_ba, lru_wx=m_lru_wx, lru_bx=m_lru_bx, lru_lambda=m_lru_lambda,
             w_att_o=m_w_att_o, w_rec_o=m_w_rec_o, w_out=m_w_out, ffn2_w_gu=m_ffn2_w_gu, ffn2_w_down=m_ffn2_w_down)
    V = dict(w_ada=v_w_ada, b_ada=v_b_ada, norm_pre=v_norm_pre, norm_post=v_norm_post, ffn1_w_gu=v_ffn1_w_gu,
             ffn1_w_down=v_ffn1_w_down, w_in=v_w_in, rel_bias=v_rel_bias, conv_w=v_conv_w, conv_b=v_conv_b,
             lru_wa=v_lru_wa, lru_ba=v_lru_ba, lru_wx=v_lru_wx, lru_bx=v_lru_bx, lru_lambda=v_lru_lambda,
             w_att_o=v_w_att_o, w_rec_o=v_w_rec_o, w_out=v_w_out, ffn2_w_gu=v_ffn2_w_gu, ffn2_w_down=v_ffn2_w_down)
    mx, my, mc = _mesh_pos()
    p = 2 * mx + my
    e = 4 * mx + 2 * my + mc
    xs = x[0]

    g1 = ag_small(_pack([c, norm_pre, norm_post, conv_w], 32), "ag_small_params")
    c_all, npre4, npost4, cw4 = _unpack(g1, [(D,), (3, 256), (3, 256), (4, 256)])
    chipwise = lambda a: jnp.moveaxis(a[0::2], 0, 1).reshape(a.shape[1], D)
    npre, npost, conv_full = chipwise(npre4), chipwise(npost4), chipwise(cw4)

    b_cols = lax.dynamic_slice(b_ada, (0, p * 2304), (1, 2304))
    mod_cols = ada_fwd(c_all, w_ada[0], b_cols, "ada_fwd")
    g2 = ag_small(mod_cols.reshape(144, 128), "ag_mod")
    mod_all = jnp.moveaxis(g2[0::2].reshape(4, 8, 2304), 0, 1).reshape(8, 9 * D)
    mod = lax.dynamic_index_in_dim(mod_all, e, 0, keepdims=False).reshape(3, 3, D)
    zeros3 = jnp.zeros((3, D), F32)
    vecs = [jnp.concatenate([npre[k:k + 1], npost[k:k + 1], mod[k], zeros3], axis=0) for k in range(3)]

    c_arr = jnp.reshape(mc, (1,)).astype(jnp.int32)
    cp_arr = jnp.stack([mc, p]).astype(jnp.int32)
    p_arr = jnp.reshape(p, (1,)).astype(jnp.int32)
    later = BIG[2:]
    geoms = [(kind, R, C) for (_, kind, R, C) in later]
    placed = [ag_local(W[n][0], kind, R, C, p_arr, "ag_local_" + n) for (n, kind, R, C) in later]
    f1_gu, f1_dn = [ag_forward(ag_push(W[n][0], kind, R, C, c_arr, "ag_push_" + n), kind, R, C, "ag_forward_" + n)
                    for (n, kind, R, C) in BIG[:2]]
    mix_s, mix_r, mix_fly, tok1 = ag_start(placed[:4], geoms[:4], f1_dn, "ag_start_mixer")
    ffn_s, ffn_r, ffn_fly, tok2 = ag_start(placed[4:], geoms[4:], tok1, "ag_start_ffn2")
    wa_bd = _block_diag4(lru_wa[0]).astype(BF16)
    wx_bd = _block_diag4(lru_wx[0]).astype(BF16)
    pvec = jnp.concatenate([conv_full, conv_b, lru_ba, lru_bx, lru_lambda], axis=0)
    bias = _bias_window(rel_bias[0])

    def arrived(fly, gm, names, ssem, rsem, after, tag):
        done = ag_wait(fly, gm, ssem, rsem, after, "ag_wait_" + tag)
        return [ag_forward(a, kind, R, C, "ag_forward_" + n) for a, (kind, R, C), n in zip(done, gm, names)]

    x1, h1, g1_, u1, a1, f1 = ffn_fwd(xs, vecs[0] + tok2[0:1, 0:1], f1_gu, f1_dn, 0.5, "ffn1_fwd")
    win, wao, wro, wout = arrived(mix_fly, geoms[:4], [b[0] for b in later[:4]], mix_s, mix_r, x1, "mixer")
    h2, qkv, rest = proj_fwd(x1, vecs[1], win, "proj_fwd")
    ao = attn_fwd(qkv, bias, "attn_fwd")
    hl, hg = lru_fwd(rest, pvec, wa_bd, wx_bd, "lru_fwd")
    x2, att, rec, mg, f2 = mix_out_fwd(x1, ao, hg, rest, vecs[1], wao, wro, wout, "mix_out_fwd")
    f2_gu, f2_dn = arrived(ffn_fly, geoms[4:], [b[0] for b in later[4:]], ffn_s, ffn_r, x2, "ffn2")
    x3, h3, g3_, u3, a3, f3 = ffn_fwd(x2, vecs[2], f2_gu, f2_dn, 0.5, "ffn2_fwd")
    dy, lvec = loss_grad(x3, loss_target[0], "loss_grad")
    loss = lax.psum(lvec[0, 0], ("x", "y", "c"))

    G = {}
    dx2, df3, dgu3, va2 = ffn_bwd(dy, x2, f3, g3_, u3, vecs[2], f2_gu, f2_dn, 0.5, "ffn2_bwd")
    G["ffn2_w_gu"] = mm_tn(h3, dgu3, "dw_ffn2_gu", D, 1408, 1024)
    G["ffn2_w_down"] = mm_tn(a3, df3, "dw_ffn2_down", 1408, D, 1024)
    df2, d_att, d_rec, dao, dhl, d3, va_out = mix_out_bwd(dx2, f2, att, rec, rest, hl, vecs[1], wao, wro, wout,
                                                          "mix_out_bwd")
    G["w_out"] = mm_tn(mg, df2, "dw_out", D, D, 1024)
    G["w_att_o"] = mm_tn(ao, d_att, "dw_att_o", 512, D, 1024)
    G["w_rec_o"] = mm_tn(hg, d_rec, "dw_rec_o", D, D, 1024)
    dq, db, dkv = attn_bwd(qkv, dao, bias, "attn_bwd")
    dxr, v_lru, dwa_bd, dwx_bd = lru_bwd(dhl, hl, rest, pvec, wa_bd, wx_bd, "lru_bwd")
    dx1, va_in = proj_bwd(dq, dkv, dxr, d3, win, x1, dx2, vecs[1], "proj_bwd")
    gin = mm_tn(h2, dq, "dw_in_q", D, 512, 1024, n_total=PW)
    gin = mm_tn(h2, dkv, "dw_in_kv", D, 512, 1024, prev=gin, col_off=512, n_total=PW)
    gin = mm_tn(h2, dxr, "dw_in_xr", D, 512, 1024, prev=gin, col_off=1536, n_total=PW)
    G["w_in"] = mm_tn(h2, d3, "dw_in_gates", D, 512, 1024, prev=gin, col_off=2560, n_total=PW)
    dx0, df1, dgu1, va0 = ffn_bwd(dx1, xs, f1, g1_, u1, vecs[0], f1_gu, f1_dn, 0.5, "ffn1_bwd")
    G["ffn1_w_gu"] = mm_tn(h1, dgu1, "dw_ffn1_gu", D, 1408, 1024)
    G["ffn1_w_down"] = mm_tn(a1, df1, "dw_ffn1_down", 1408, D, 1024)

    va1 = va_out + va_in
    vas = (va0, va1, va2)
    dmod = jnp.stack([v[2:5] for v in vas])
    part = {"b_ada": dmod, "norm_pre": jnp.stack([v[0] for v in vas]), "norm_post": jnp.stack([v[1] for v in vas]),
            "rel_bias": bias_grad(db, "bias_grad")[:, :257], "conv_w": v_lru[0:4], "conv_b": v_lru[4],
            "lru_wa": _diag_blocks(dwa_bd), "lru_ba": v_lru[5], "lru_wx": _diag_blocks(dwx_bd), "lru_bx": v_lru[6],
            "lru_lambda": v_lru[7]}
    full_shapes = {"b_ada": (9 * D,), "norm_pre": (3, D), "norm_post": (3, D), "rel_bias": (8, 257),
                   "conv_w": (4, D), "conv_b": (D,), "lru_wa": (16, 64, 64), "lru_ba": (D,),
                   "lru_wx": (16, 64, 64), "lru_bx": (D,), "lru_lambda": (D,)}
    g3 = ag_small(_pack([part[n] for n in SMALL], 1232), "ag_small_grads")
    red = dict(zip(SMALL, _unpack(sum_lead(g3, "sum_small_grads"), [full_shapes[n] for n in SMALL])))
    cols = lambda a: lax.dynamic_slice(a, (0, p * 256), (a.shape[0], 256))
    grads = {"b_ada": red["b_ada"][None], "norm_pre": cols(red["norm_pre"])[None],
             "norm_post": cols(red["norm_post"])[None], "rel_bias": red["rel_bias"][None],
             "conv_w": cols(red["conv_w"])[None], "conv_b": red["conv_b"][None], "lru_wa": red["lru_wa"][None],
             "lru_ba": red["lru_ba"][None], "lru_wx": red["lru_wx"][None], "lru_bx": red["lru_bx"][None],
             "lru_lambda": red["lru_lambda"][None]}

    dmod_all = g3[:, :72].reshape(8, 9 * D)
    dmod_cols = jnp.pad(lax.dynamic_slice(dmod_all, (0, p * 2304), (8, 2304)), ((0, 120), (0, 0)))
    c_all_t = jnp.pad(c_all.T, ((0, 0), (0, 120)))
    grads["w_ada"] = ada_bwd(c_all_t, dmod_cols, "ada_bwd")[None]

    for n, kind, R, C in BIG:
        got = pair_push(G[n], kind, c_arr, "rs_push_" + n)
        by_chip = pair_add_scatter(G[n], got, kind, cp_arr, "rs_scatter_" + n)
        grads[n] = sum_share(by_chip, kind, R, C, "rs_sum_share_" + n)[None]

    delta, new_m, new_v = {}, {}, {}
    for n in ("w_ada",) + tuple(b[0] for b in BIG):
        shp = W[n].shape
        d_, m_, v_ = adamw(W[n][0], grads[n][0], M[n][0], V[n][0], "adamw_" + n)
        delta[n], new_m[n], new_v[n] = d_.reshape(shp), m_.reshape(shp), v_.reshape(shp)
    packed = [_pack([src[n] for n in SMALL], 1168) for src in (W, grads, M, V)]
    outs = adamw(*packed, "adamw_small")
    for dst, blk in zip((delta, new_m, new_v), outs):
        for n, a in zip(SMALL, _unpack(blk, [W[n].shape for n in SMALL])):
            dst[n] = a

    return (loss, dx0[None], *[grads[n] for n in WEIGHTS], *[delta[n] for n in WEIGHTS],
            *[new_m[n] for n in WEIGHTS], *[new_v[n] for n in WEIGHTS])
```

```python
import functools

import numpy as np
import jax
import jax.numpy as jnp
from jax import lax
from jax.experimental import pallas as pl
from jax.experimental.pallas import tpu as pltpu

F32 = jnp.float32
BF16 = jnp.bfloat16

D = 1024
FF = 2816
PW = 5632
HP = 128
CHUNK = 64
WIN = 640
TQ = 512
EPS = 1e-6
NEG = -1e30
LRU_C = 8.0
N_DEV = 8
VMEM_LIMIT = 50 * 1024 * 1024

ADAM_LR, ADAM_B1, ADAM_B2, ADAM_EPS, ADAM_WD, ADAM_STEP = 0.001, 0.9, 0.999, 1e-08, 0.01, 10

MESH = pl.DeviceIdType.MESH
ANY = pl.BlockSpec(memory_space=pl.ANY)


def _cp(*sem):
    return pltpu.CompilerParams(dimension_semantics=tuple(sem), vmem_limit_bytes=VMEM_LIMIT)


def _dot(a, b):
    return jnp.dot(a, b, preferred_element_type=F32)


def _dot_nt(a, b):
    return lax.dot_general(a, b, (((1,), (1,)), ((), ())), preferred_element_type=F32)


def _dot_tn(a, b):
    return lax.dot_general(a, b, (((0,), (0,)), ((), ())), preferred_element_type=F32)


def _mean(v):
    return jnp.mean(v, axis=-1, keepdims=True)


def _colsum(v):
    return jnp.sum(v, axis=0, keepdims=True)


def _sigmoid(v):
    return jax.nn.sigmoid(v)


def _expm1(v):
    small = v * (1.0 + v * 0.5 * (1.0 + v * (1.0 / 3.0) * (1.0 + v * 0.25 * (1.0 + v * 0.2 * (
        1.0 + v * (1.0 / 6.0) * (1.0 + v * (1.0 / 7.0)))))))
    return jnp.where(jnp.abs(v) < 0.25, small, jnp.exp(v) - 1.0)


_GK = 0.7978845608028654


def _gelu(v):
    t = jnp.tanh(_GK * (v + 0.044715 * v * v * v))
    return 0.5 * v * (1.0 + t)


def _gelu_grad(v):
    t = jnp.tanh(_GK * (v + 0.044715 * v * v * v))
    return 0.5 * (1.0 + t) + 0.5 * v * (1.0 - t * t) * _GK * (1.0 + 3.0 * 0.044715 * v * v)


def _pre_norm(xv, vec_ref):
    r = lax.rsqrt(_mean(xv * xv) + EPS)
    n = xv * r * vec_ref[0:1, :]
    return n * (1.0 + vec_ref[3:4, :]) + vec_ref[2:3, :]


def _pre_norm_bwd(dh, xv, dres, vec_ref, vacc_ref):
    r = lax.rsqrt(_mean(xv * xv) + EPS)
    xh = xv * r
    n = xh * vec_ref[0:1, :]
    vacc_ref[2:3, :] += _colsum(dh)
    vacc_ref[3:4, :] += _colsum(dh * n)
    dn = dh * (1.0 + vec_ref[3:4, :])
    vacc_ref[0:1, :] += _colsum(dn * xh)
    dxh = dn * vec_ref[0:1, :]
    return r * (dxh - xh * _mean(dxh * xh)) + dres


def _post_norm_bwd(dxo, fv, res, vec_ref, vacc_ref):
    rf = lax.rsqrt(_mean(fv * fv) + EPS)
    fh = fv * rf
    gp = vec_ref[1:2, :]
    vacc_ref[4:5, :] += _colsum(res * dxo * (fh * gp))
    dy = (res * vec_ref[4:5, :]) * dxo
    vacc_ref[1:2, :] += _colsum(dy * fh)
    dfn = dy * gp
    return rf * (dfn - fh * _mean(dfn * fh))


def ffn_fwd(x, vec, w_gu, w_dn, res, name, tm=1024, tf=256):
    S = x.shape[0]
    tm = min(tm, S)
    nf = FF // tf

    def body(x_ref, vec_ref, wg_ref, wu_ref, wd_ref, xo_ref, h_ref, g_ref, u_ref, a_ref, f_ref, hs, acc):
        j = pl.program_id(1)

        @pl.when(j == 0)
        def _():
            h = _pre_norm(x_ref[...], vec_ref).astype(BF16)
            hs[...] = h
            h_ref[...] = h
            acc[...] = jnp.zeros_like(acc)

        h = hs[...]
        g = _dot(h, wg_ref[...])
        u = _dot(h, wu_ref[...])
        g_ref[...] = g.astype(BF16)
        u_ref[...] = u.astype(BF16)
        a = (g * _sigmoid(g) * u).astype(BF16)
        a_ref[...] = a
        acc[...] += _dot(a, wd_ref[...])

        @pl.when(j == nf - 1)
        def _():
            f = acc[...]
            f_ref[...] = f
            y = f * lax.rsqrt(_mean(f * f) + EPS) * vec_ref[1:2, :]
            xo_ref[...] = x_ref[...] + (res * vec_ref[4:5, :]) * y

    row = lambda i, j: (i, 0)
    return pl.pallas_call(
        body, name=name, grid=(S // tm, nf),
        in_specs=[pl.BlockSpec((tm, D), row), pl.BlockSpec((8, D), lambda i, j: (0, 0)),
                  pl.BlockSpec((D, tf), lambda i, j: (0, j)), pl.BlockSpec((D, tf), lambda i, j: (0, j + nf)),
                  pl.BlockSpec((tf, D), lambda i, j: (j, 0))],
        out_specs=[pl.BlockSpec((tm, D), row), pl.BlockSpec((tm, D), row),
                   pl.BlockSpec((tm, tf), lambda i, j: (i, j)), pl.BlockSpec((tm, tf), lambda i, j: (i, j)),
                   pl.BlockSpec((tm, tf), lambda i, j: (i, j)), pl.BlockSpec((tm, D), row)],
        out_shape=[jax.ShapeDtypeStruct((S, D), F32), jax.ShapeDtypeStruct((S, D), BF16),
                   jax.ShapeDtypeStruct((S, FF), BF16), jax.ShapeDtypeStruct((S, FF), BF16),
                   jax.ShapeDtypeStruct((S, FF), BF16), jax.ShapeDtypeStruct((S, D), F32)],
        scratch_shapes=[pltpu.VMEM((tm, D), BF16), pltpu.VMEM((tm, D), F32)],
        compiler_params=_cp("parallel", "arbitrary"),
    )(x, vec, w_gu, w_gu, w_dn)


def ffn_bwd(dxo, x, f, g, u, vec, w_gu, w_dn, res, name, tm=512, tf=256):
    S = x.shape[0]
    tm = min(tm, S)
    nf = FF // tf

    def body(dxo_ref, x_ref, f_ref, g_ref, u_ref, vec_ref, wg_ref, wu_ref, wd_ref,
             dx_ref, df_ref, dgu_ref, vacc_ref, dfs, acc):
        i, j = pl.program_id(0), pl.program_id(1)

        @pl.when((i == 0) & (j == 0))
        def _():
            vacc_ref[...] = jnp.zeros_like(vacc_ref)

        @pl.when(j == 0)
        def _():
            df = _post_norm_bwd(dxo_ref[...], f_ref[...], res, vec_ref, vacc_ref).astype(BF16)
            dfs[...] = df
            df_ref[...] = df
            acc[...] = jnp.zeros_like(acc)

        da = _dot_nt(dfs[...], wd_ref[...])
        gv, uv = g_ref[...].astype(F32), u_ref[...].astype(F32)
        sg = _sigmoid(gv)
        dg = (da * uv * (sg * (1.0 + gv * (1.0 - sg)))).astype(BF16)
        du = (da * (gv * sg)).astype(BF16)
        dgu_ref[0] = dg
        dgu_ref[1] = du
        acc[...] += _dot_nt(dg, wg_ref[...]) + _dot_nt(du, wu_ref[...])

        @pl.when(j == nf - 1)
        def _():
            dx_ref[...] = _pre_norm_bwd(acc[...], x_ref[...], dxo_ref[...], vec_ref, vacc_ref)

    row = lambda i, j: (i, 0)
    return pl.pallas_call(
        body, name=name, grid=(S // tm, nf),
        in_specs=[pl.BlockSpec((tm, D), row), pl.BlockSpec((tm, D), row), pl.BlockSpec((tm, D), row),
                  pl.BlockSpec((tm, tf), lambda i, j: (i, j)), pl.BlockSpec((tm, tf), lambda i, j: (i, j)),
                  pl.BlockSpec((8, D), lambda i, j: (0, 0)),
                  pl.BlockSpec((D, tf), lambda i, j: (0, j)), pl.BlockSpec((D, tf), lambda i, j: (0, j + nf)),
                  pl.BlockSpec((tf, D), lambda i, j: (j, 0))],
        out_specs=[pl.BlockSpec((tm, D), row), pl.BlockSpec((tm, D), row),
                   pl.BlockSpec((2, tm, tf), lambda i, j: (0, i, j)),
                   pl.BlockSpec((8, D), lambda i, j: (0, 0))],
        out_shape=[jax.ShapeDtypeStruct((S, D), F32), jax.ShapeDtypeStruct((S, D), BF16),
                   jax.ShapeDtypeStruct((2, S, FF), BF16), jax.ShapeDtypeStruct((8, D), F32)],
        scratch_shapes=[pltpu.VMEM((tm, D), BF16), pltpu.VMEM((tm, D), F32)],
        compiler_params=_cp("arbitrary", "arbitrary"),
    )(dxo, x, f, g, u, vec, w_gu, w_gu, w_dn)


def mm_tn(a, b, name, tm, tn, tk, out_dtype=BF16, prev=None, col_off=0, n_total=None):
    S, M = a.shape
    if b.ndim == 3:
        G, _, Nf = b.shape
    else:
        G, Nf = 1, b.shape[1]
    N = G * Nf
    n_total = N if n_total is None else n_total
    tk = min(tk, S)
    nbf = Nf // tn
    nk = S // tk
    ob = col_off // tn

    def body(*refs):
        a_ref, b_ref = refs[0], refs[1]
        o_ref, acc = refs[-2], refs[-1]
        k = pl.program_id(2)

        @pl.when(k == 0)
        def _():
            acc[...] = jnp.zeros_like(acc)

        acc[...] += _dot_tn(a_ref[...], b_ref[...])

        @pl.when(k == nk - 1)
        def _():
            o_ref[...] = acc[...].astype(out_dtype)

    if b.ndim == 3:
        b_spec = pl.BlockSpec((None, tk, tn), lambda i, j, k: (j // nbf, k, j % nbf))
    else:
        b_spec = pl.BlockSpec((tk, tn), lambda i, j, k: (k, j))
    in_specs = [pl.BlockSpec((tk, tm), lambda i, j, k: (k, i)), b_spec]
    args = [a, b]
    aliases = {}
    if prev is not None:
        in_specs.append(ANY)
        args.append(prev)
        aliases = {2: 0}
    return pl.pallas_call(
        body, name=name, grid=(M // tm, N // tn, nk),
        in_specs=in_specs,
        out_specs=pl.BlockSpec((tm, tn), lambda i, j, k: (i, j + ob)),
        out_shape=jax.ShapeDtypeStruct((M, n_total), out_dtype),
        scratch_shapes=[pltpu.VMEM((tm, tn), F32)],
        input_output_aliases=aliases,
        compiler_params=_cp("parallel", "parallel", "arbitrary"),
    )(*args)


def proj_fwd(x, vec, w_in, name, tm=1024, tn=512):
    S = x.shape[0]
    tm = min(tm, S)
    nq = 1536 // tn

    def body(x_ref, vec_ref, w_ref, h_ref, qkv_ref, rest_ref, hs):
        j = pl.program_id(1)

        @pl.when(j == 0)
        def _():
            h = _pre_norm(x_ref[...], vec_ref).astype(BF16)
            hs[...] = h
            h_ref[...] = h

        r = _dot(hs[...], w_ref[...])

        @pl.when(j < nq)
        def _():
            qkv_ref[...] = r.astype(BF16)

        @pl.when(j >= nq)
        def _():
            rest_ref[...] = r

    row = lambda i, j: (i, 0)
    return pl.pallas_call(
        body, name=name, grid=(S // tm, PW // tn),
        in_specs=[pl.BlockSpec((tm, D), row), pl.BlockSpec((8, D), lambda i, j: (0, 0)),
                  pl.BlockSpec((D, tn), lambda i, j: (0, j))],
        out_specs=[pl.BlockSpec((tm, D), row),
                   pl.BlockSpec((tm, tn), lambda i, j: (i, jnp.minimum(j, nq - 1))),
                   pl.BlockSpec((tm, tn), lambda i, j: (i, jnp.maximum(j - nq, 0)))],
        out_shape=[jax.ShapeDtypeStruct((S, D), BF16), jax.ShapeDtypeStruct((S, 1536), BF16),
                   jax.ShapeDtypeStruct((S, 4096), F32)],
        scratch_shapes=[pltpu.VMEM((tm, D), BF16)],
        compiler_params=_cp("parallel", "arbitrary"),
    )(x, vec, w_in)


def proj_bwd(dq, dkv, dxr, d3, w_in, x, dxo, vec, name, tm=1024, tk=512):
    S = x.shape[0]
    tm = min(tm, S)
    nk = PW // tk

    def body(dq_ref, dkv_ref, dxr_ref, d3_ref, w_ref, x_ref, dxo_ref, vec_ref, dx_ref, vacc_ref, acc):
        i, j = pl.program_id(0), pl.program_id(1)

        @pl.when((i == 0) & (j == 0))
        def _():
            vacc_ref[...] = jnp.zeros_like(vacc_ref)

        @pl.when(j == 0)
        def _():
            acc[...] = _dot_nt(dq_ref[...], w_ref[...])

        @pl.when((j >= 1) & (j < 3))
        def _():
            acc[...] += _dot_nt(dkv_ref[...], w_ref[...])

        @pl.when((j >= 3) & (j < 5))
        def _():
            acc[...] += _dot_nt(dxr_ref[...], w_ref[...])

        @pl.when(j >= 5)
        def _():
            acc[...] += _dot_nt(d3_ref[...], w_ref[...])

        @pl.when(j == nk - 1)
        def _():
            dx_ref[...] = _pre_norm_bwd(acc[...], x_ref[...], dxo_ref[...], vec_ref, vacc_ref)

    row = lambda i, j: (i, 0)
    return pl.pallas_call(
        body, name=name, grid=(S // tm, nk),
        in_specs=[pl.BlockSpec((None, tm, tk), lambda i, j: (0, i, 0)),
                  pl.BlockSpec((None, tm, tk), lambda i, j: (jnp.clip(j - 1, 0, 1), i, 0)),
                  pl.BlockSpec((tm, tk), lambda i, j: (i, jnp.clip(j - 3, 0, 1))),
                  pl.BlockSpec((None, tm, tk), lambda i, j: (jnp.clip(j - 5, 0, 5) // 2, i, jnp.clip(j - 5, 0, 5) % 2)),
                  pl.BlockSpec((D, tk), lambda i, j: (0, j)),
                  pl.BlockSpec((tm, D), row), pl.BlockSpec((tm, D), row),
                  pl.BlockSpec((8, D), lambda i, j: (0, 0))],
        out_specs=[pl.BlockSpec((tm, D), row), pl.BlockSpec((8, D), lambda i, j: (0, 0))],
        out_shape=[jax.ShapeDtypeStruct((S, D), F32), jax.ShapeDtypeStruct((8, D), F32)],
        scratch_shapes=[pltpu.VMEM((tm, D), F32)],
        compiler_params=_cp("arbitrary", "arbitrary"),
    )(dq, dkv, dxr, d3, w_in, x, dxo, vec)


def _attn_probs(qm, ka, bias_h, i, grp):
    s = _dot_nt(qm, ka) + bias_h
    col = lax.broadcasted_iota(jnp.int32, s.shape, 1)
    first_key = jnp.where(i == 0, 512 - 128 * grp, 0)
    s = jnp.where(col >= first_key, s, NEG)
    e = jnp.exp(s - jnp.max(s, axis=-1, keepdims=True))
    return e / jnp.sum(e, axis=-1, keepdims=True)


def attn_fwd(qkv, bias, name):
    S = qkv.shape[0]
    nb = S // TQ

    def body(q_ref, kp_ref, kc_ref, vp_ref, vc_ref, b_ref, o_ref, kw, vw):
        i = pl.program_id(1)
        kw[0:TQ, :] = kp_ref[...]
        kw[TQ:2 * TQ, :] = kc_ref[...]
        vw[0:TQ, :] = vp_ref[...]
        vw[TQ:2 * TQ, :] = vc_ref[...]
        lane = lax.broadcasted_iota(jnp.int32, (1, HP), 1)
        zero = jnp.zeros((), BF16)

        def group(a, carry):
            r0 = pl.multiple_of(a * 128, 128)
            qa = q_ref[pl.ds(r0, 128), :] * jnp.asarray(0.125, BF16)
            ka = kw[pl.ds(r0, WIN), :]
            va = vw[pl.ds(r0, WIN), :]
            o = jnp.zeros((128, HP), F32)
            for hh in range(2):
                msk = (lane < 64) if hh == 0 else (lane >= 64)
                p = _attn_probs(jnp.where(msk, qa, zero), ka, b_ref[hh], i, a)
                o += _dot(p.astype(BF16), jnp.where(msk, va, zero))
            o_ref[pl.ds(r0, 128), :] = o.astype(BF16)
            return carry

        lax.fori_loop(0, TQ // 128, group, 0, unroll=True)

    prev = lambda h, i: (jnp.maximum(i - 1, 0), 0)
    return pl.pallas_call(
        body, name=name, grid=(4, nb),
        in_specs=[pl.BlockSpec((TQ, HP), lambda h, i: (i, h)),
                  pl.BlockSpec((TQ, HP), lambda h, i: (jnp.maximum(i - 1, 0), 4 + h)),
                  pl.BlockSpec((TQ, HP), lambda h, i: (i, 4 + h)),
                  pl.BlockSpec((TQ, HP), lambda h, i: (jnp.maximum(i - 1, 0), 8 + h)),
                  pl.BlockSpec((TQ, HP), lambda h, i: (i, 8 + h)),
                  pl.BlockSpec((2, 128, WIN), lambda h, i: (h, 0, 0))],
        out_specs=pl.BlockSpec((TQ, HP), lambda h, i: (i, h)),
        out_shape=jax.ShapeDtypeStruct((S, 512), BF16),
        scratch_shapes=[pltpu.VMEM((2 * TQ, HP), BF16), pltpu.VMEM((2 * TQ, HP), BF16)],
        compiler_params=_cp("parallel", "arbitrary"),
    )(qkv, qkv, qkv, qkv, qkv, bias)


def attn_bwd(qkv, do, bias, name):
    S = qkv.shape[0]
    nb = S // TQ

    def body(q_ref, kp_ref, kc_ref, vp_ref, vc_ref, do_ref, b_ref, dqkv_ref, db_ref, dkv_ref, kw, vw, ak, av):
        i = pl.program_id(1)

        @pl.when(i == 0)
        def _():
            db_ref[...] = jnp.zeros_like(db_ref)
            ak[...] = jnp.zeros_like(ak)
            av[...] = jnp.zeros_like(av)

        @pl.when(i > 0)
        def _():
            ak[0:TQ, :] = ak[TQ:2 * TQ, :]
            av[0:TQ, :] = av[TQ:2 * TQ, :]
            ak[TQ:2 * TQ, :] = jnp.zeros((TQ, HP), F32)
            av[TQ:2 * TQ, :] = jnp.zeros((TQ, HP), F32)

        @pl.when(i < nb)
        def _():
            kw[0:TQ, :] = kp_ref[...]
            kw[TQ:2 * TQ, :] = kc_ref[...]
            vw[0:TQ, :] = vp_ref[...]
            vw[TQ:2 * TQ, :] = vc_ref[...]
            lane = lax.broadcasted_iota(jnp.int32, (1, HP), 1)
            zero = jnp.zeros((), BF16)

            def group(a, carry):
                r0 = pl.multiple_of(a * 128, 128)
                qa = q_ref[pl.ds(r0, 128), :] * jnp.asarray(0.125, BF16)
                doa = do_ref[pl.ds(r0, 128), :]
                ka = kw[pl.ds(r0, WIN), :]
                va = vw[pl.ds(r0, WIN), :]
                dq = jnp.zeros((128, HP), F32)
                for hh in range(2):
                    msk = (lane < 64) if hh == 0 else (lane >= 64)
                    qm = jnp.where(msk, qa, zero)
                    dom = jnp.where(msk, doa, zero)
                    p = _attn_probs(qm, ka, b_ref[hh], i, a)
                    dp = _dot_nt(dom, va)
                    ds = p * (dp - jnp.sum(p * dp, axis=-1, keepdims=True))
                    db_ref[hh] += ds
                    dsb = ds.astype(BF16)
                    dq += jnp.where(msk, _dot(dsb, ka), 0.0)
                    ak[pl.ds(r0, WIN), :] += _dot_tn(dsb, qm)
                    av[pl.ds(r0, WIN), :] += _dot_tn(p.astype(BF16), dom)
                dqkv_ref[0, pl.ds(r0, 128), :] = (dq * 0.125).astype(BF16)
                return carry

            lax.fori_loop(0, TQ // 128, group, 0, unroll=True)

        @pl.when(i > 0)
        def _():
            dkv_ref[0] = ak[0:TQ, :].astype(BF16)
            dkv_ref[1] = av[0:TQ, :].astype(BF16)

    cur = lambda i: jnp.minimum(i, nb - 1)
    prv = lambda i: jnp.clip(i - 1, 0, nb - 1)
    dq, db, dkv = pl.pallas_call(
        body, name=name, grid=(4, nb + 1),
        in_specs=[pl.BlockSpec((TQ, HP), lambda h, i: (cur(i), h)),
                  pl.BlockSpec((TQ, HP), lambda h, i: (prv(i), 4 + h)),
                  pl.BlockSpec((TQ, HP), lambda h, i: (cur(i), 4 + h)),
                  pl.BlockSpec((TQ, HP), lambda h, i: (prv(i), 8 + h)),
                  pl.BlockSpec((TQ, HP), lambda h, i: (cur(i), 8 + h)),
                  pl.BlockSpec((TQ, HP), lambda h, i: (cur(i), h)),
                  pl.BlockSpec((2, 128, WIN), lambda h, i: (h, 0, 0))],
        out_specs=[pl.BlockSpec((1, TQ, HP), lambda h, i: (0, cur(i), h)),
                   pl.BlockSpec((2, 128, WIN), lambda h, i: (h, 0, 0)),
                   pl.BlockSpec((2, TQ, HP), lambda h, i: (0, prv(i), h))],
        out_shape=[jax.ShapeDtypeStruct((1, S, 512), BF16), jax.ShapeDtypeStruct((8, 128, WIN), F32),
                   jax.ShapeDtypeStruct((2, S, 512), BF16)],
        scratch_shapes=[pltpu.VMEM((2 * TQ, HP), BF16), pltpu.VMEM((2 * TQ, HP), BF16),
                        pltpu.VMEM((2 * TQ, HP), F32), pltpu.VMEM((2 * TQ, HP), F32)],
        compiler_params=_cp("parallel", "arbitrary"),
    )(qkv, qkv, qkv, qkv, qkv, do, bias)
    return dq, db, dkv


def bias_grad(db, name):
    def body(db_ref, o_ref):
        r = lax.broadcasted_iota(jnp.int32, (128, 128), 0)
        c = lax.broadcasted_iota(jnp.int32, (128, 128), 1)
        flip = (r + c == 127).astype(BF16)
        lane = lax.broadcasted_iota(jnp.int32, (16, 384), 1)
        src = lax.broadcasted_iota(jnp.int32, (128, 384), 0)
        dst = lax.broadcasted_iota(jnp.int32, (128, 384), 1)

        def split_dot(v, m):
            hi = v.astype(BF16)
            r1 = v - hi.astype(F32)
            mid = r1.astype(BF16)
            lo = (r1 - mid.astype(F32)).astype(BF16)
            return _dot(hi, m) + _dot(mid, m) + _dot(lo, m)

        def diag_sums(w):
            y = pltpu.roll(split_dot(w, flip), 0, 1, stride=1, stride_axis=0)
            return jnp.broadcast_to(_colsum(y), (16, 128))

        w4 = db_ref[0, :, 512:640]
        w3 = db_ref[0, :, 384:512]
        far = jnp.sum(db_ref[0, :, 0:384]) + jnp.sum(jnp.where(r >= c, w3, 0.0))
        lo4 = diag_sums(jnp.where(r >= c, w4, 0.0))
        up4 = diag_sums(jnp.where(r < c, w4, 0.0))
        up3 = diag_sums(jnp.where(r < c, w3, 0.0))
        p_lo4 = (dst == 128 + (src + 1) % 128).astype(BF16)
        p_up4 = ((dst == src + 1) & (src < 127)).astype(BF16)
        p_up3 = ((dst == src + 129) & (src < 127)).astype(BF16)
        out = split_dot(lo4, p_lo4) + split_dot(up4, p_up4) + split_dot(up3, p_up3)
        o_ref[0] = out + jnp.where(lane == 256, far, 0.0)

    return pl.pallas_call(
        body, name=name, grid=(8,),
        in_specs=[pl.BlockSpec((1, 128, WIN), lambda h: (h, 0, 0))],
        out_specs=pl.BlockSpec((1, 16, 384), lambda h: (h, 0, 0)),
        out_shape=jax.ShapeDtypeStruct((8, 16, 384), F32),
        compiler_params=_cp("parallel"),
    )(db)[:, 0, :]


LT = 256
LC = 512


def _lru_gates(xs, pv_ref, wa_ref, wx_ref, tl):
    xc = (pv_ref[4:5, :] + pv_ref[3:4, :] * xs[pl.ds(8, tl), :] + pv_ref[2:3, :] * xs[pl.ds(7, tl), :]
          + pv_ref[1:2, :] * xs[pl.ds(6, tl), :] + pv_ref[0:1, :] * xs[pl.ds(5, tl), :])
    xcb = xc.astype(BF16)
    pa = jnp.concatenate([_dot(xcb[:, 0:256], wa_ref[0]), _dot(xcb[:, 256:512], wa_ref[1])], axis=1)
    px = jnp.concatenate([_dot(xcb[:, 0:256], wx_ref[0]), _dot(xcb[:, 256:512], wx_ref[1])], axis=1)
    r = _sigmoid(pa + pv_ref[5:6, :])
    ig = _sigmoid(px + pv_ref[6:7, :])
    z = -pv_ref[7:8, :]
    sp = jnp.maximum(z, 0.0) + jnp.log1p(jnp.exp(-jnp.abs(z)))
    log_a = (-LRU_C * r) * sp
    a = jnp.exp(log_a)
    mult = jnp.sqrt(-_expm1(2.0 * log_a))
    return xc, xcb, r, ig, sp, a, mult


def lru_fwd(rest, pvec, wa, wx, name):
    S = rest.shape[0]
    tl = min(LT, S)
    nt = S // tl

    def body(xr_ref, halo_ref, yr_ref, pv_ref, wa_ref, wx_ref, h_ref, hg_ref, xs, a_s, u_s, h_s, carry):
        ti = pl.program_id(1)

        @pl.when(ti == 0)
        def _():
            carry[...] = jnp.zeros_like(carry)

        xs[0:8, :] = jnp.where(ti > 0, halo_ref[...], 0.0)
        xs[pl.ds(8, tl), :] = xr_ref[...]
        xc, _, _, ig, _, a, mult = _lru_gates(xs, pv_ref, wa_ref, wx_ref, tl)
        a_s[...] = a
        u_s[...] = mult * (ig * xc)
        row = lax.broadcasted_iota(jnp.int32, (8, LC), 0)

        def blk(bi, c):
            o = pl.multiple_of(bi * 8, 8)
            av = a_s[pl.ds(o, 8), :]
            bv = u_s[pl.ds(o, 8), :]
            for d in (1, 2, 4):
                a_sh = pltpu.roll(av, d, 0)
                b_sh = pltpu.roll(bv, d, 0)
                m = row >= d
                bv = jnp.where(m, av * b_sh + bv, bv)
                av = jnp.where(m, av * a_sh, av)
            hv = bv + av * c
            h_s[pl.ds(o, 8), :] = hv
            return hv[7:8, :]

        carry[...] = lax.fori_loop(0, tl // 8, blk, carry[...])
        h = h_s[...]
        h_ref[...] = h
        hg_ref[...] = (h * _gelu(yr_ref[...])).astype(BF16)

    hb = tl // 8
    return pl.pallas_call(
        body, name=name, grid=(2, nt),
        in_specs=[pl.BlockSpec((tl, LC), lambda c, t: (t, c)),
                  pl.BlockSpec((8, LC), lambda c, t: (jnp.maximum(t * hb - 1, 0), c)),
                  pl.BlockSpec((tl, LC), lambda c, t: (t, 2 + c)),
                  pl.BlockSpec((8, LC), lambda c, t: (0, c)),
                  pl.BlockSpec((2, 256, 256), lambda c, t: (c, 0, 0)),
                  pl.BlockSpec((2, 256, 256), lambda c, t: (c, 0, 0))],
        out_specs=[pl.BlockSpec((tl, LC), lambda c, t: (t, c)), pl.BlockSpec((tl, LC), lambda c, t: (t, c))],
        out_shape=[jax.ShapeDtypeStruct((S, D), F32), jax.ShapeDtypeStruct((S, D), BF16)],
        scratch_shapes=[pltpu.VMEM((tl + 8, LC), F32), pltpu.VMEM((tl, LC), F32), pltpu.VMEM((tl, LC), F32),
                        pltpu.VMEM((tl, LC), F32), pltpu.VMEM((1, LC), F32)],
        compiler_params=_cp("parallel", "arbitrary"),
    )(rest, rest, rest, pvec, wa, wx)


def lru_bwd(dh, h, rest, pvec, wa, wx, name):
    S = rest.shape[0]
    tl = min(LT, S)
    nt = S // tl

    def body(dh_ref, h_ref, hhalo_ref, xr_ref, xhalo_ref, pv_ref, wa_ref, wx_ref,
             dxr_ref, vacc_ref, dwa_ref, dwx_ref,
             xs, hs, a_s, ash_s, b_s, lam_s, dxe, anext, lnext, dxnext):
        ti = pl.program_id(1)
        tr = nt - 1 - ti

        @pl.when(ti == 0)
        def _():
            anext[...] = jnp.zeros_like(anext)
            lnext[...] = jnp.zeros_like(lnext)
            dxnext[...] = jnp.zeros_like(dxnext)
            vacc_ref[...] = jnp.zeros_like(vacc_ref)
            dwa_ref[...] = jnp.zeros_like(dwa_ref)
            dwx_ref[...] = jnp.zeros_like(dwx_ref)

        xs[0:8, :] = jnp.where(tr > 0, xhalo_ref[...], 0.0)
        xs[pl.ds(8, tl), :] = xr_ref[...]
        xc, xcb, r, ig, sp, a, mult = _lru_gates(xs, pv_ref, wa_ref, wx_ref, tl)

        a_s[pl.ds(0, tl), :] = a
        a_s[pl.ds(tl, 8), :] = jnp.broadcast_to(anext[...], (8, LC))
        ash_s[...] = a_s[pl.ds(1, tl), :]
        b_s[...] = dh_ref[...]
        row = lax.broadcasted_iota(jnp.int32, (8, LC), 0)

        def blk(k, c):
            o = pl.multiple_of((tl // 8 - 1 - k) * 8, 8)
            av = ash_s[pl.ds(o, 8), :]
            bv = b_s[pl.ds(o, 8), :]
            for d in (1, 2, 4):
                a_sh = pltpu.roll(av, 8 - d, 0)
                b_sh = pltpu.roll(bv, 8 - d, 0)
                m = row < 8 - d
                bv = jnp.where(m, bv + av * b_sh, bv)
                av = jnp.where(m, av * a_sh, av)
            lv = bv + av * c
            lam_s[pl.ds(o, 8), :] = lv
            return lv[0:1, :]

        lnext[...] = lax.fori_loop(0, tl // 8, blk, lnext[...])
        anext[...] = a[0:1, :]
        lam = lam_s[...]

        hs[0:8, :] = jnp.where(tr > 0, hhalo_ref[...], 0.0)
        hs[pl.ds(8, tl), :] = h_ref[...]
        d_a = lam * hs[pl.ds(7, tl), :]
        d_mult = lam * (ig * xc)
        d_ig = lam * mult * xc
        dxc = lam * mult * ig
        d_log_a = d_a * a - d_mult * (a * a) / mult
        d_r = d_log_a * (-LRU_C * sp)
        vacc_ref[7:8, :] += _colsum(d_log_a * (-LRU_C * r)) * (-_sigmoid(-pv_ref[7:8, :]))
        d_pa = d_r * r * (1.0 - r)
        d_px = d_ig * ig * (1.0 - ig)
        vacc_ref[5:6, :] += _colsum(d_pa)
        vacc_ref[6:7, :] += _colsum(d_px)
        dpa = d_pa.astype(BF16)
        dpx = d_px.astype(BF16)
        back = []
        for g in range(2):
            sl = slice(256 * g, 256 * g + 256)
            dwa_ref[g] += _dot_tn(xcb[:, sl], dpa[:, sl])
            dwx_ref[g] += _dot_tn(xcb[:, sl], dpx[:, sl])
            back.append(_dot_nt(dpa[:, sl], wa_ref[g]) + _dot_nt(dpx[:, sl], wx_ref[g]))
        dxc = dxc + jnp.concatenate(back, axis=1)
        vacc_ref[4:5, :] += _colsum(dxc)
        for k in range(4):
            vacc_ref[k:k + 1, :] += _colsum(dxc * xs[pl.ds(5 + k, tl), :])
        dxe[pl.ds(0, tl), :] = dxc
        dxe[pl.ds(tl, 8), :] = dxnext[...]
        dxr = (pv_ref[3:4, :] * dxc + pv_ref[2:3, :] * dxe[pl.ds(1, tl), :]
               + pv_ref[1:2, :] * dxe[pl.ds(2, tl), :] + pv_ref[0:1, :] * dxe[pl.ds(3, tl), :])
        dxr_ref[...] = dxr.astype(BF16)
        dxnext[...] = dxc[0:8, :]

    hb = tl // 8
    rev = lambda t: nt - 1 - t
    halo = lambda t: jnp.maximum(rev(t) * hb - 1, 0)
    big = lambda: pltpu.VMEM((tl + 8, LC), F32)
    til = lambda: pltpu.VMEM((tl, LC), F32)
    return pl.pallas_call(
        body, name=name, grid=(2, nt),
        in_specs=[pl.BlockSpec((tl, LC), lambda c, t: (rev(t), c)),
                  pl.BlockSpec((tl, LC), lambda c, t: (rev(t), c)),
                  pl.BlockSpec((8, LC), lambda c, t: (halo(t), c)),
                  pl.BlockSpec((tl, LC), lambda c, t: (rev(t), c)),
                  pl.BlockSpec((8, LC), lambda c, t: (halo(t), c)),
                  pl.BlockSpec((8, LC), lambda c, t: (0, c)),
                  pl.BlockSpec((2, 256, 256), lambda c, t: (c, 0, 0)),
                  pl.BlockSpec((2, 256, 256), lambda c, t: (c, 0, 0))],
        out_specs=[pl.BlockSpec((tl, LC), lambda c, t: (rev(t), c)),
                   pl.BlockSpec((8, LC), lambda c, t: (0, c)),
                   pl.BlockSpec((2, 256, 256), lambda c, t: (c, 0, 0)),
                   pl.BlockSpec((2, 256, 256), lambda c, t: (c, 0, 0))],
        out_shape=[jax.ShapeDtypeStruct((S, D), BF16), jax.ShapeDtypeStruct((8, D), F32),
                   jax.ShapeDtypeStruct((4, 256, 256), F32), jax.ShapeDtypeStruct((4, 256, 256), F32)],
        scratch_shapes=[big(), big(), big(), til(), til(), til(), big(),
                        pltpu.VMEM((1, LC), F32), pltpu.VMEM((1, LC), F32), pltpu.VMEM((8, LC), F32)],
        compiler_params=_cp("parallel", "arbitrary"),
    )(dh, h, h, rest, rest, pvec, wa, wx)


def mix_out_fwd(x, ao, hg, rest, vec, w_att_o, w_rec_o, w_out, name, tm=256):
    S = x.shape[0]
    tm = min(tm, S)

    def body(x_ref, ao_ref, hg_ref, ga_ref, gr_ref, vec_ref, wa_ref, wr_ref, wo_ref,
             xo_ref, att_ref, rec_ref, mg_ref, f_ref):
        att = _dot(ao_ref[...], wa_ref[...])
        rec = _dot(hg_ref[...], wr_ref[...])
        att_ref[...] = att
        rec_ref[...] = rec
        mg = (_sigmoid(ga_ref[...]) * att + _sigmoid(gr_ref[...]) * rec).astype(BF16)
        mg_ref[...] = mg
        f = _dot(mg, wo_ref[...])
        f_ref[...] = f
        y = f * lax.rsqrt(_mean(f * f) + EPS) * vec_ref[1:2, :]
        xo_ref[...] = x_ref[...] + (1.0 * vec_ref[4:5, :]) * y

    row = lambda i: (i, 0)
    full = lambda r: pl.BlockSpec((r, D), lambda i: (0, 0))
    return pl.pallas_call(
        body, name=name, grid=(S // tm,),
        in_specs=[pl.BlockSpec((tm, D), row), pl.BlockSpec((tm, 512), row), pl.BlockSpec((tm, D), row),
                  pl.BlockSpec((tm, D), lambda i: (i, 2)), pl.BlockSpec((tm, D), lambda i: (i, 3)),
                  full(8), full(512), full(D), full(D)],
        out_specs=[pl.BlockSpec((tm, D), row)] * 5,
        out_shape=[jax.ShapeDtypeStruct((S, D), F32), jax.ShapeDtypeStruct((S, D), F32),
                   jax.ShapeDtypeStruct((S, D), F32), jax.ShapeDtypeStruct((S, D), BF16),
                   jax.ShapeDtypeStruct((S, D), F32)],
        compiler_params=_cp("parallel"),
    )(x, ao, hg, rest, rest, vec, w_att_o, w_rec_o, w_out)


def mix_out_bwd(dxo, f, att, rec, rest, h, vec, w_att_o, w_rec_o, w_out, name, tm=256):
    S = dxo.shape[0]
    tm = min(tm, S)

    def body(dxo_ref, f_ref, att_ref, rec_ref, yr_ref, ga_ref, gr_ref, h_ref, vec_ref, wa_ref, wr_ref, wo_ref,
             df_ref, da_ref, dr_ref, dao_ref, dh_ref, d3_ref, vacc_ref):
        @pl.when(pl.program_id(0) == 0)
        def _():
            vacc_ref[...] = jnp.zeros_like(vacc_ref)

        df = _post_norm_bwd(dxo_ref[...], f_ref[...], 1.0, vec_ref, vacc_ref).astype(BF16)
        df_ref[...] = df
        dm = _dot_nt(df, wo_ref[...])
        sa = _sigmoid(ga_ref[...])
        sr = _sigmoid(gr_ref[...])
        d_att = (dm * sa).astype(BF16)
        d_rec = (dm * sr).astype(BF16)
        da_ref[...] = d_att
        dr_ref[...] = d_rec
        d3_ref[1] = (dm * att_ref[...] * (sa * (1.0 - sa))).astype(BF16)
        d3_ref[2] = (dm * rec_ref[...] * (sr * (1.0 - sr))).astype(BF16)
        dao_ref[...] = _dot_nt(d_att, wa_ref[...]).astype(BF16)
        d_hg = _dot_nt(d_rec, wr_ref[...])
        yr = yr_ref[...]
        dh_ref[...] = d_hg * _gelu(yr)
        d3_ref[0] = (d_hg * h_ref[...] * _gelu_grad(yr)).astype(BF16)

    row = lambda i: (i, 0)
    full = lambda r: pl.BlockSpec((r, D), lambda i: (0, 0))
    return pl.pallas_call(
        body, name=name, grid=(S // tm,),
        in_specs=[pl.BlockSpec((tm, D), row)] * 4
        + [pl.BlockSpec((tm, D), lambda i: (i, 1)), pl.BlockSpec((tm, D), lambda i: (i, 2)),
           pl.BlockSpec((tm, D), lambda i: (i, 3)), pl.BlockSpec((tm, D), row),
           full(8), full(512), full(D), full(D)],
        out_specs=[pl.BlockSpec((tm, D), row)] * 3
        + [pl.BlockSpec((tm, 512), row), pl.BlockSpec((tm, D), row),
           pl.BlockSpec((3, tm, D), lambda i: (0, i, 0)), pl.BlockSpec((8, D), lambda i: (0, 0))],
        out_shape=[jax.ShapeDtypeStruct((S, D), BF16)] * 3
        + [jax.ShapeDtypeStruct((S, 512), BF16), jax.ShapeDtypeStruct((S, D), F32),
           jax.ShapeDtypeStruct((3, S, D), BF16), jax.ShapeDtypeStruct((8, D), F32)],
        compiler_params=_cp("arbitrary"),
    )(dxo, f, att, rec, rest, rest, rest, h, vec, w_att_o, w_rec_o, w_out)


def loss_grad(y, tgt, name, tm=512):
    S = y.shape[0]
    tm = min(tm, S)
    nt = S // tm

    def body(y_ref, t_ref, dy_ref, l_ref, acc):
        i = pl.program_id(0)

        @pl.when(i == 0)
        def _():
            acc[...] = jnp.zeros_like(acc)

        d = y_ref[...] - t_ref[...]
        dy_ref[...] = d * (1.0 / D)
        acc[...] += _colsum(d * d)

        @pl.when(i == nt - 1)
        def _():
            l_ref[...] = jnp.broadcast_to(0.5 * jnp.sum(acc[...]) * (1.0 / D), (8, 128))

    return pl.pallas_call(
        body, name=name, grid=(nt,),
        in_specs=[pl.BlockSpec((tm, D), lambda i: (i, 0))] * 2,
        out_specs=[pl.BlockSpec((tm, D), lambda i: (i, 0)), pl.BlockSpec((8, 128), lambda i: (0, 0))],
        out_shape=[jax.ShapeDtypeStruct((S, D), F32), jax.ShapeDtypeStruct((8, 128), F32)],
        scratch_shapes=[pltpu.VMEM((1, D), F32)],
        compiler_params=_cp("arbitrary"),
    )(y, tgt)


def ada_fwd(c_all, w_ada, b_ada, name, tn=768):
    n = w_ada.shape[1]

    def body(c_ref, w_ref, b_ref, o_ref):
        cv = c_ref[...]
        ca = (cv * _sigmoid(cv)).astype(BF16)
        o_ref[...] = _dot(ca, w_ref[...].astype(BF16)) + b_ref[...]

    return pl.pallas_call(
        body, name=name, grid=(n // tn,),
        in_specs=[pl.BlockSpec((8, D), lambda j: (0, 0)), pl.BlockSpec((D, tn), lambda j: (0, j)),
                  pl.BlockSpec((1, tn), lambda j: (0, j))],
        out_specs=pl.BlockSpec((8, tn), lambda j: (0, j)),
        out_shape=jax.ShapeDtypeStruct((8, n), F32),
        compiler_params=_cp("parallel"),
    )(c_all, w_ada, b_ada)


def ada_bwd(c_all_t, dmod, name, tn=768):
    n = dmod.shape[1]

    def body(c_ref, d_ref, o_ref):
        cv = c_ref[...]
        ca = (cv * _sigmoid(cv)).astype(BF16)
        o_ref[...] = _dot(ca, d_ref[...].astype(BF16))

    return pl.pallas_call(
        body, name=name, grid=(n // tn,),
        in_specs=[pl.BlockSpec((D, 128), lambda j: (0, 0)), pl.BlockSpec((128, tn), lambda j: (0, j))],
        out_specs=pl.BlockSpec((D, tn), lambda j: (0, j)),
        out_shape=jax.ShapeDtypeStruct((D, n), F32),
        compiler_params=_cp("parallel"),
    )(c_all_t, dmod)


def _row_tile(rows, cols, itemsize=4, budget=1536 * 1024):
    best = None
    for t in range(8, rows + 1, 8):
        if rows % t == 0 and t * cols * itemsize <= budget:
            best = t
    return rows if best is None else best


def sum_lead(parts, name, out_dtype=F32):
    n, R, C = parts.shape
    tr = _row_tile(R, C * n)

    def body(p_ref, o_ref):
        acc = p_ref[0].astype(F32)
        for k in range(1, n):
            acc = acc + p_ref[k].astype(F32)
        o_ref[...] = acc.astype(out_dtype)

    return pl.pallas_call(
        body, name=name, grid=(R // tr,),
        in_specs=[pl.BlockSpec((n, tr, C), lambda i: (0, i, 0))],
        out_specs=pl.BlockSpec((tr, C), lambda i: (i, 0)),
        out_shape=jax.ShapeDtypeStruct((R, C), out_dtype),
        compiler_params=_cp("parallel"),
    )(parts)


def adamw(w, g, m, v, name):
    R, C = w.shape
    tr = _row_tile(R, C * 7, budget=8 * 1024 * 1024)

    def body(w_ref, g_ref, m_ref, v_ref, d_ref, mo_ref, vo_ref):
        gv = g_ref[...]
        mn = ADAM_B1 * m_ref[...] + (1.0 - ADAM_B1) * gv
        vn = ADAM_B2 * v_ref[...] + (1.0 - ADAM_B2) * (gv * gv)
        m_hat = mn / (1.0 - ADAM_B1 ** ADAM_STEP)
        v_hat = vn / (1.0 - ADAM_B2 ** ADAM_STEP)
        d_ref[...] = -ADAM_LR * (m_hat / (jnp.sqrt(v_hat) + ADAM_EPS) + ADAM_WD * w_ref[...])
        mo_ref[...] = mn
        vo_ref[...] = vn

    spec = pl.BlockSpec((tr, C), lambda i: (i, 0))
    return pl.pallas_call(
        body, name=name, grid=(R // tr,),
        in_specs=[spec] * 4, out_specs=[spec] * 3,
        out_shape=[jax.ShapeDtypeStruct((R, C), F32)] * 3,
        compiler_params=_cp("parallel"),
    )(w, g, m, v)


def _mesh_pos():
    return lax.axis_index("x"), lax.axis_index("y"), lax.axis_index("c")


def _other_chips(mx, my):
    return [(1 - mx, my), (mx, 1 - my), (1 - mx, 1 - my)]


def ag_small(x, name):
    R = x.shape[0]

    def body(x_ref, out_ref, send_sems, recv_sems, local_sem):
        mx, my, mc = _mesh_pos()
        me, sibling = (mx, my, mc), (mx, my, 1 - mc)
        chips = _other_chips(mx, my)

        def slot(px, py, pc):
            return out_ref.at[4 * px + 2 * py + pc]

        def copy(k, block, to, src=None):
            return pltpu.make_async_remote_copy(
                src_ref=slot(*block) if src is None else src, dst_ref=slot(*block),
                send_sem=send_sems.at[k], recv_sem=recv_sems.at[k], device_id=to, device_id_type=MESH)

        mine = pltpu.make_async_copy(x_ref, slot(*me), local_sem)
        mine.start()
        first = [copy(0, me, sibling, src=x_ref)]
        first += [copy(1 + j, me, (*chip, mc), src=x_ref) for j, chip in enumerate(chips)]
        for cp in first:
            cp.start()
        passed = [copy(4 + j, (*chip, mc), sibling) for j, chip in enumerate(chips)]
        for j, chip in enumerate(chips):
            copy(1 + j, (*chip, mc), me).wait_recv()
            passed[j].start()
        copy(0, sibling, me).wait_recv()
        for j, chip in enumerate(chips):
            copy(4 + j, (*chip, 1 - mc), me).wait_recv()
        for cp in first + passed:
            cp.wait_send()
        mine.wait()

    return pl.pallas_call(
        body, name=name,
        out_shape=jax.ShapeDtypeStruct((N_DEV, R, 128), F32),
        in_specs=[pl.BlockSpec(memory_space=pltpu.VMEM)],
        out_specs=pl.BlockSpec(memory_space=pltpu.VMEM),
        scratch_shapes=[pltpu.SemaphoreType.DMA((7,)), pltpu.SemaphoreType.DMA((7,)), pltpu.SemaphoreType.DMA],
        compiler_params=pltpu.CompilerParams(vmem_limit_bytes=VMEM_LIMIT),
    )(x)


BIG = (("ffn1_w_gu", "col", D, PW), ("ffn1_w_down", "row", FF, D), ("w_in", "col", D, PW),
       ("w_att_o", "col", 512, D), ("w_rec_o", "row", D, D), ("w_out", "row", D, D),
       ("ffn2_w_gu", "col", D, PW), ("ffn2_w_down", "row", FF, D))
NBIG = len(BIG)


def _shard_shape(kind, R, C):
    return (R, C // 4) if kind == "col" else (R // 4, C)


def _region(ref, kind, R, C, q, half, t, tr):
    sr, sc = _shard_shape(kind, R, C)
    if kind == "col":
        return ref.at[pl.ds(pl.multiple_of(half * (R // 2) + t * tr, 16), tr), pl.ds(q * sc, sc)]
    return ref.at[pl.ds(pl.multiple_of(q * sr + t * tr, 16), tr), pl.ds(half * (C // 2), C // 2)]


def ag_push(w, kind, R, C, c_arr, name):
    sr, sc = _shard_shape(kind, R, C)
    hr, hc = (sr // 2, sc) if kind == "col" else (sr, sc // 2)
    tr = _row_tile(hr, hc, itemsize=2, budget=512 * 1024)
    nt = hr // tr

    def body(c_ref, mine_ref, other_ref, full_ref, stage, lsem, ssem, rsem):
        i = pl.program_id(0)
        par = i % 2
        mx, my, mc = _mesh_pos()
        p = 2 * mx + my
        chips = _other_chips(mx, my)

        def copies(s, q, h, t):
            mine = _region(full_ref, kind, R, C, q, h, t, tr)
            other = _region(full_ref, kind, R, C, q, 1 - h, t, tr)
            out = [pltpu.make_async_remote_copy(src_ref=stage.at[s, 0], dst_ref=mine, send_sem=ssem.at[s, k],
                                                recv_sem=rsem.at[k], device_id=(*chips[k], mc), device_id_type=MESH)
                   for k in range(3)]
            out.append(pltpu.make_async_copy(stage.at[s, 0], mine, lsem.at[s, 0]))
            out.append(pltpu.make_async_copy(stage.at[s, 1], other, lsem.at[s, 1]))
            return out

        def wait_sent(s):
            cps = copies(s, 0, 0, 0)
            for cp in cps[:3]:
                cp.wait_send()
            for cp in cps[3:]:
                cp.wait()

        @pl.when(i >= 2)
        def _():
            wait_sent(par)

        stage[par, 0] = mine_ref[...].astype(BF16)
        stage[par, 1] = other_ref[...].astype(BF16)
        if kind == "col":
            for q in range(4):
                @pl.when(p == q)
                def _(q=q):
                    for cp in copies(par, q, mc, i):
                        cp.start()
        else:
            for h in range(2):
                @pl.when(mc == h)
                def _(h=h):
                    for cp in copies(par, p, h, i):
                        cp.start()

        @pl.when(i == nt - 1)
        def _():
            for s in range(min(nt, 2)):
                wait_sent(s)
            for k in range(3):
                half = full_ref.at[pl.ds(0, hr), pl.ds(0, hc)]
                pltpu.make_async_remote_copy(src_ref=half, dst_ref=half, send_sem=ssem.at[0, k], recv_sem=rsem.at[k],
                                             device_id=(*chips[k], mc), device_id_type=MESH).wait_recv()

    if kind == "col":
        mine_spec = pl.BlockSpec((tr, hc), lambda i, c: (c[0] * nt + i, 0))
        other_spec = pl.BlockSpec((tr, hc), lambda i, c: ((1 - c[0]) * nt + i, 0))
    else:
        mine_spec = pl.BlockSpec((tr, hc), lambda i, c: (i, c[0]))
        other_spec = pl.BlockSpec((tr, hc), lambda i, c: (i, 1 - c[0]))
    return pl.pallas_call(
        body, name=name,
        grid_spec=pltpu.PrefetchScalarGridSpec(
            num_scalar_prefetch=1, grid=(nt,), in_specs=[mine_spec, other_spec], out_specs=ANY,
            scratch_shapes=[pltpu.VMEM((2, 2, tr, hc), BF16), pltpu.SemaphoreType.DMA((2, 2)),
                            pltpu.SemaphoreType.DMA((2, 3)), pltpu.SemaphoreType.DMA((3,))]),
        out_shape=jax.ShapeDtypeStruct((R, C), BF16),
        compiler_params=_cp("arbitrary"),
    )(c_arr, w, w)


def ag_local(w, kind, R, C, p_arr, name):
    sr, sc = _shard_shape(kind, R, C)
    tr = _row_tile(sr, sc, budget=2 * 1024 * 1024)
    nt = sr // tr

    def body(p_ref, w_ref, o_ref):
        o_ref[...] = w_ref[...].astype(BF16)

    if kind == "col":
        o_spec = pl.BlockSpec((tr, sc), lambda i, p: (i, p[0]))
    else:
        o_spec = pl.BlockSpec((tr, sc), lambda i, p: (p[0] * nt + i, 0))
    return pl.pallas_call(
        body, name=name,
        grid_spec=pltpu.PrefetchScalarGridSpec(
            num_scalar_prefetch=1, grid=(nt,), in_specs=[pl.BlockSpec((tr, sc), lambda i, p: (i, 0))],
            out_specs=o_spec),
        out_shape=jax.ShapeDtypeStruct((R, C), BF16),
        compiler_params=_cp("parallel"),
    )(p_arr, w)


HBM_SPEC = pl.BlockSpec(memory_space=pltpu.HBM)
SEM_SPEC = pl.BlockSpec(memory_space=pltpu.SEMAPHORE)


def _ag_copies(fulls, geoms, ssem, rsem, mx, my, mc, q, h):
    chips = _other_chips(mx, my)
    out = []
    for w, (kind, R, C) in enumerate(geoms):
        sr, sc = _shard_shape(kind, R, C)
        hr = sr // 2 if kind == "col" else sr
        reg = _region(fulls[w], kind, R, C, q, h, 0, hr)
        out += [pltpu.make_async_remote_copy(src_ref=reg, dst_ref=reg, send_sem=ssem.at[3 * w + k],
                                             recv_sem=rsem.at[3 * w + k], device_id=(*chips[k], mc),
                                             device_id_type=MESH) for k in range(3)]
    return out


def ag_start(fulls, geoms, after, name):
    n = len(fulls)
    after = list(after)
    m = len(after)

    def body(*refs):
        ssem, rsem = refs[n + m:n + m + 2]
        outs, token = refs[n + m + 2:2 * n + m + 2], refs[2 * n + m + 2]
        mx, my, mc = _mesh_pos()
        p = 2 * mx + my
        col = [w for w, g in enumerate(geoms) if g[0] == "col"]
        row = [w for w, g in enumerate(geoms) if g[0] == "row"]
        for q in range(4):
            @pl.when(p == q)
            def _(q=q):
                cps = _ag_copies(outs, geoms, ssem, rsem, mx, my, mc, q, mc)
                for w in col:
                    for cp in cps[3 * w:3 * w + 3]:
                        cp.start()
        for h in range(2):
            @pl.when(mc == h)
            def _(h=h):
                cps = _ag_copies(outs, geoms, ssem, rsem, mx, my, mc, p, h)
                for w in row:
                    for cp in cps[3 * w:3 * w + 3]:
                        cp.start()
        token[...] = jnp.zeros_like(token)

    res = pl.pallas_call(
        body, name=name,
        out_shape=[pltpu.SemaphoreType.DMA((3 * n,)), pltpu.SemaphoreType.DMA((3 * n,))]
        + [pltpu.HBM(a.shape, a.dtype) for a in fulls] + [jax.ShapeDtypeStruct((8, 128), F32)],
        in_specs=[HBM_SPEC] * n + [ANY] * m,
        out_specs=[SEM_SPEC, SEM_SPEC] + [HBM_SPEC] * n + [pl.BlockSpec(memory_space=pltpu.VMEM)],
        input_output_aliases={w: 2 + w for w in range(n)},
        compiler_params=pltpu.CompilerParams(has_side_effects=pltpu.SideEffectType.DATAFLOW_SIDE_EFFECTING),
    )(*[pltpu.with_memory_space_constraint(a, pltpu.HBM) for a in fulls], *after)
    return res[0], res[1], list(res[2:2 + n]), res[2 + n]


def ag_wait(fulls, geoms, ssem, rsem, after, name):
    n = len(fulls)

    def body(*refs):
        ins, ssem_ref, rsem_ref = refs[:n], refs[n], refs[n + 1]
        mx, my, mc = _mesh_pos()
        for cp in _ag_copies(ins, geoms, ssem_ref, rsem_ref, mx, my, mc, 0, 0):
            cp.wait_send()
            cp.wait_recv()

    return list(pl.pallas_call(
        body, name=name,
        out_shape=[pltpu.HBM(a.shape, a.dtype) for a in fulls],
        in_specs=[HBM_SPEC] * n + [SEM_SPEC, SEM_SPEC, ANY],
        out_specs=[HBM_SPEC] * n,
        input_output_aliases={w: w for w in range(n)},
        compiler_params=pltpu.CompilerParams(has_side_effects=pltpu.SideEffectType.DATAFLOW_SIDE_EFFECTING),
    )(*fulls, ssem, rsem, after))


def ag_forward(full, kind, R, C, name):
    sr, sc = _shard_shape(kind, R, C)
    hr, hc = (sr // 2, sc) if kind == "col" else (sr, sc // 2)
    tr = _row_tile(hr, hc, itemsize=2, budget=512 * 1024)
    nt = hr // tr

    def body(src_ref, full_ref, stage, lsem, ssem, rsem):
        k, i = pl.program_id(0), pl.program_id(1)
        step = k * nt + i
        par = step % 2
        mx, my, mc = _mesh_pos()
        q_k = _partner_chip(k, 2 * mx + my)

        def push(s, dst):
            return pltpu.make_async_remote_copy(src_ref=stage.at[s], dst_ref=dst, send_sem=ssem.at[s], recv_sem=rsem,
                                                device_id=(mx, my, 1 - mc), device_id_type=MESH)

        @pl.when(step >= 2)
        def _():
            push(par, _region(full_ref, kind, R, C, 0, 0, 0, tr)).wait_send()

        def move(q, h):
            load = pltpu.make_async_copy(_region(src_ref, kind, R, C, q, h, i, tr), stage.at[par], lsem)
            load.start()
            load.wait()
            push(par, _region(full_ref, kind, R, C, q, h, i, tr)).start()

        if kind == "col":
            for q in range(4):
                @pl.when(q_k == q)
                def _(q=q):
                    move(q, mc)
        else:
            for h in range(2):
                @pl.when(mc == h)
                def _(h=h):
                    move(q_k, h)

        @pl.when(step == 3 * nt - 1)
        def _():
            for s in range(2):
                push(s, _region(full_ref, kind, R, C, 0, 0, 0, tr)).wait_send()
            three = full_ref.at[pl.ds(0, hr), pl.ds(0, 3 * hc)] if kind == "col" else full_ref.at[pl.ds(0, 3 * hr), pl.ds(0, hc)]
            pltpu.make_async_remote_copy(src_ref=three, dst_ref=three, send_sem=ssem.at[0], recv_sem=rsem,
                                         device_id=(mx, my, 1 - mc), device_id_type=MESH).wait_recv()

    return pl.pallas_call(
        body, name=name, grid=(3, nt),
        in_specs=[ANY], out_specs=ANY,
        out_shape=jax.ShapeDtypeStruct((R, C), BF16),
        scratch_shapes=[pltpu.VMEM((2, tr, hc), BF16), pltpu.SemaphoreType.DMA, pltpu.SemaphoreType.DMA((2,)),
                        pltpu.SemaphoreType.DMA],
        input_output_aliases={0: 0},
        compiler_params=_cp("arbitrary", "arbitrary"),
    )(full)


def _half_shape(kind, R, C):
    return (R // 2, C) if kind == "col" else (R, C // 2)


def _piece_shape(kind, R, C):
    return (R // 2, C // 4) if kind == "col" else (R // 4, C // 2)


def pair_push(g, kind, c_arr, name):
    R, C = g.shape
    hr, hc = _half_shape(kind, R, C)
    tr = _row_tile(hr, hc, itemsize=2, budget=1024 * 1024)
    nt = hr // tr

    def body(c_ref, g_ref, out_ref, stage, ssem, rsem):
        i = pl.program_id(0)
        slot = i % 2
        mx, my, mc = _mesh_pos()

        def push(s, t):
            return pltpu.make_async_remote_copy(
                src_ref=stage.at[s], dst_ref=out_ref.at[pl.ds(pl.multiple_of(t * tr, 16), tr)],
                send_sem=ssem.at[s], recv_sem=rsem, device_id=(mx, my, 1 - mc), device_id_type=MESH)

        @pl.when(i >= 2)
        def _():
            push(slot, 0).wait_send()

        stage[slot] = g_ref[...]
        push(slot, i).start()

        @pl.when(i == nt - 1)
        def _():
            push(slot, 0).wait_send()
            if nt >= 2:
                push(1 - slot, 0).wait_send()
            pltpu.make_async_remote_copy(src_ref=out_ref, dst_ref=out_ref, send_sem=ssem.at[0], recv_sem=rsem,
                                         device_id=(mx, my, 1 - mc), device_id_type=MESH).wait_recv()

    if kind == "col":
        g_spec = pl.BlockSpec((tr, hc), lambda i, c: ((1 - c[0]) * nt + i, 0))
    else:
        g_spec = pl.BlockSpec((tr, hc), lambda i, c: (i, 1 - c[0]))
    return pl.pallas_call(
        body, name=name,
        grid_spec=pltpu.PrefetchScalarGridSpec(
            num_scalar_prefetch=1, grid=(nt,), in_specs=[g_spec], out_specs=ANY,
            scratch_shapes=[pltpu.VMEM((2, tr, hc), BF16), pltpu.SemaphoreType.DMA((2,)), pltpu.SemaphoreType.DMA]),
        out_shape=jax.ShapeDtypeStruct((hr, hc), BF16),
        compiler_params=_cp("arbitrary"),
    )(c_arr, g)


def _partner_chip(k, p):
    return p ^ jnp.where(k == 0, 2, jnp.where(k == 1, 1, jnp.where(k == 2, 3, 0)))


def pair_add(g, got, kind, cp_arr, name):
    R, C = g.shape
    pr, pc = _piece_shape(kind, R, C)
    tr = _row_tile(pr, pc, itemsize=2, budget=1024 * 1024)
    nt = pr // tr

    def body(cp_ref, g_ref, got_ref, ps_ref, rb_ref):
        tile = (g_ref[...].astype(F32) + got_ref[...].astype(F32)).astype(BF16)
        ps_ref[...] = tile

        @pl.when(pl.program_id(1) == cp_ref[1])
        def _():
            rb_ref[...] = tile

    if kind == "col":
        g_spec = pl.BlockSpec((tr, pc), lambda i, q, cp: (cp[0] * nt + i, q))
        got_spec = pl.BlockSpec((tr, pc), lambda i, q, cp: (i, q))
    else:
        g_spec = pl.BlockSpec((tr, pc), lambda i, q, cp: (q * nt + i, cp[0]))
        got_spec = pl.BlockSpec((tr, pc), lambda i, q, cp: (q * nt + i, 0))
    return pl.pallas_call(
        body, name=name,
        grid_spec=pltpu.PrefetchScalarGridSpec(
            num_scalar_prefetch=1, grid=(nt, 4), in_specs=[g_spec, got_spec],
            out_specs=[pl.BlockSpec((None, tr, pc), lambda i, q, cp: (q, i, 0)),
                       pl.BlockSpec((None, tr, pc), lambda i, q, cp: (cp[1], i, 0))]),
        out_shape=[jax.ShapeDtypeStruct((4, pr, pc), BF16)] * 2,
        compiler_params=_cp("arbitrary", "arbitrary"),
    )(cp_arr, g, got)


def _rs_copies(ps, rb, ssem, rsem, mx, my, mc):
    p = 2 * mx + my
    out = []
    for w in range(len(ps)):
        for k, chip in enumerate(_other_chips(mx, my)):
            out.append(pltpu.make_async_remote_copy(
                src_ref=ps[w].at[2 * chip[0] + chip[1]], dst_ref=rb[w].at[p], send_sem=ssem.at[3 * w + k],
                recv_sem=rsem.at[3 * w + k], device_id=(*chip, mc), device_id_type=MESH))
    return out


def rs_start(ps, rb, after, name):
    n = len(ps)
    after = list(after)
    m = len(after)

    def body(*refs):
        ssem, rsem = refs[2 * n + m:2 * n + m + 2]
        ps_o = refs[2 * n + m + 2:3 * n + m + 2]
        rb_o = refs[3 * n + m + 2:4 * n + m + 2]
        token = refs[4 * n + m + 2]
        for cp in _rs_copies(ps_o, rb_o, ssem, rsem, *_mesh_pos()):
            cp.start()
        token[...] = jnp.zeros_like(token)

    both = list(ps) + list(rb)
    res = pl.pallas_call(
        body, name=name,
        out_shape=[pltpu.SemaphoreType.DMA((3 * n,)), pltpu.SemaphoreType.DMA((3 * n,))]
        + [pltpu.HBM(a.shape, a.dtype) for a in both] + [jax.ShapeDtypeStruct((8, 128), F32)],
        in_specs=[HBM_SPEC] * (2 * n) + [ANY] * m,
        out_specs=[SEM_SPEC, SEM_SPEC] + [HBM_SPEC] * (2 * n) + [pl.BlockSpec(memory_space=pltpu.VMEM)],
        input_output_aliases={w: 2 + w for w in range(2 * n)},
        compiler_params=pltpu.CompilerParams(has_side_effects=pltpu.SideEffectType.DATAFLOW_SIDE_EFFECTING),
    )(*[pltpu.with_memory_space_constraint(a, pltpu.HBM) for a in both], *after)
    return res[0], res[1], list(res[2:2 + n]), list(res[2 + n:2 + 2 * n]), res[2 + 2 * n]


def rs_wait(ps, rb, ssem, rsem, after, name):
    n = len(ps)
    after = list(after)
    m = len(after)

    def body(*refs):
        ps_i, rb_i = refs[:n], refs[n:2 * n]
        ssem_ref, rsem_ref = refs[2 * n], refs[2 * n + 1]
        for cp in _rs_copies(ps_i, rb_i, ssem_ref, rsem_ref, *_mesh_pos()):
            cp.wait_send()
            cp.wait_recv()

    both = list(ps) + list(rb)
    res = pl.pallas_call(
        body, name=name,
        out_shape=[pltpu.HBM(a.shape, a.dtype) for a in both],
        in_specs=[HBM_SPEC] * (2 * n) + [SEM_SPEC, SEM_SPEC] + [ANY] * m,
        out_specs=[HBM_SPEC] * (2 * n),
        input_output_aliases={w: w for w in range(2 * n)},
        compiler_params=pltpu.CompilerParams(has_side_effects=pltpu.SideEffectType.DATAFLOW_SIDE_EFFECTING),
    )(*both, ssem, rsem, *after)
    return list(res[n:])


def sum_share(parts, kind, R, C, name):
    _, pr, pc = parts.shape
    sr, sc = _shard_shape(kind, R, C)
    tr = _row_tile(pr, pc * 4, budget=4 * 1024 * 1024)
    nt = pr // tr

    def body(p_ref, fin_ref, stage, lsem, ssem, rsem):
        i = pl.program_id(0)
        slot = i % 2
        mx, my, mc = _mesh_pos()

        def region(h, t):
            r0 = pl.multiple_of(t * tr, 8)
            if kind == "col":
                return fin_ref.at[pl.ds(pl.multiple_of(h * pr + r0, 8), tr)]
            return fin_ref.at[pl.ds(r0, tr), pl.ds(h * pc, pc)]

        def copies(s, h, t):
            return (pltpu.make_async_copy(stage.at[s], region(h, t), lsem.at[s]),
                    pltpu.make_async_remote_copy(src_ref=stage.at[s], dst_ref=region(h, t), send_sem=ssem.at[s],
                                                 recv_sem=rsem, device_id=(mx, my, 1 - mc), device_id_type=MESH))

        def wait_sent(s):
            loc, rem = copies(s, 0, 0)
            loc.wait()
            rem.wait_send()

        @pl.when(i >= 2)
        def _():
            wait_sent(slot)

        acc = p_ref[0].astype(F32)
        for k in range(1, 4):
            acc = acc + p_ref[k].astype(F32)
        stage[slot] = acc
        if kind == "col":
            for cp in copies(slot, mc, i):
                cp.start()
        else:
            for h in range(2):
                @pl.when(mc == h)
                def _(h=h):
                    for cp in copies(slot, h, i):
                        cp.start()

        @pl.when(i == nt - 1)
        def _():
            wait_sent(slot)
            if nt >= 2:
                wait_sent(1 - slot)
            half = fin_ref.at[pl.ds(0, pr), pl.ds(0, pc)]
            pltpu.make_async_remote_copy(src_ref=half, dst_ref=half, send_sem=ssem.at[0], recv_sem=rsem,
                                         device_id=(mx, my, 1 - mc), device_id_type=MESH).wait_recv()

    return pl.pallas_call(
        body, name=name, grid=(nt,),
        in_specs=[pl.BlockSpec((4, tr, pc), lambda i: (0, i, 0))],
        out_specs=ANY,
        out_shape=jax.ShapeDtypeStruct((sr, sc), F32),
        scratch_shapes=[pltpu.VMEM((2, tr, pc), F32), pltpu.SemaphoreType.DMA((2,)), pltpu.SemaphoreType.DMA((2,)),
                        pltpu.SemaphoreType.DMA],
        compiler_params=_cp("arbitrary"),
    )(parts)


def _pack(parts, rows):
    flat = []
    for a in parts:
        a = jnp.ravel(a).astype(F32)
        flat.append(jnp.pad(a, (0, (-a.shape[0]) % 128)))
    v = jnp.concatenate(flat)
    return jnp.pad(v, (0, rows * 128 - v.shape[0])).reshape(rows, 128)


def _unpack(block, shapes):
    lead = block.shape[:-2]
    v = block.reshape(lead + (-1,))
    out, off = [], 0
    for shp in shapes:
        n = int(np.prod(shp))
        out.append(v[..., off:off + n].reshape(lead + tuple(shp)))
        off += n + (-n) % 128
    return out


def _block_diag4(w):
    w4 = w.reshape(4, 4, 64, 64)
    eye = jnp.eye(4, dtype=w.dtype)
    return (w4[:, :, :, None, :] * eye[None, :, None, :, None]).reshape(4, 256, 256)


def _diag_blocks(bd):
    b5 = bd.reshape(4, 4, 64, 4, 64)
    return jnp.stack([b5[:, i, :, i, :] for i in range(4)], axis=1).reshape(16, 64, 64)


def _bias_window(rel_bias):
    m = np.arange(767)
    tv = rel_bias[:, np.clip(639 - m, -128, 128) + 128]
    win = jnp.stack([tv[:, 127 - r:127 - r + WIN] for r in range(128)], axis=1)
    qh = np.arange(128)[:, None] // CHUNK
    kc = np.arange(WIN)[None, :] // CHUNK
    valid = (kc >= qh) & (kc <= qh + 8)
    return jnp.where(jnp.asarray(valid)[None], win, NEG)


SMALL = ("b_ada", "norm_pre", "norm_post", "rel_bias", "conv_w", "conv_b", "lru_wa", "lru_ba", "lru_wx",
         "lru_bx", "lru_lambda")
WEIGHTS = ("w_ada", "b_ada", "norm_pre", "norm_post", "ffn1_w_gu", "ffn1_w_down", "w_in", "rel_bias", "conv_w",
           "conv_b", "lru_wa", "lru_ba", "lru_wx", "lru_bx", "lru_lambda", "w_att_o", "w_rec_o", "w_out",
           "ffn2_w_gu", "ffn2_w_down")


def kernel(x, c, w_ada, b_ada, norm_pre, norm_post, ffn1_w_gu, ffn1_w_down, w_in, rel_bias, conv_w, conv_b, lru_wa, lru_ba, lru_wx, lru_bx, lru_lambda, w_att_o, w_rec_o, w_out, ffn2_w_gu, ffn2_w_down, loss_target, m_w_ada, m_b_ada, m_norm_pre, m_norm_post, m_ffn1_w_gu, m_ffn1_w_down, m_w_in, m_rel_bias, m_conv_w, m_conv_b, m_lru_wa, m_lru_ba, m_lru_wx, m_lru_bx, m_lru_lambda, m_w_att_o, m_w_rec_o, m_w_out, m_ffn2_w_gu, m_ffn2_w_down, v_w_ada, v_b_ada, v_norm_pre, v_norm_post, v_ffn1_w_gu, v_ffn1_w_down, v_w_in, v_rel_bias, v_conv_w, v_conv_b, v_lru_wa, v_lru_ba, v_lru_wx, v_lru_bx, v_lru_lambda, v_w_att_o, v_w_rec_o, v_w_out, v_ffn2_w_gu, v_ffn2_w_down):
    W = dict(w_ada=w_ada, b_ada=b_ada, norm_pre=norm_pre, norm_post=norm_post, ffn1_w_gu=ffn1_w_gu,
             ffn1_w_down=ffn1_w_down, w_in=w_in, rel_bias=rel_bias, conv_w=conv_w, conv_b=conv_b, lru_wa=lru_wa,
             lru_ba=lru_ba, lru_wx=lru_wx, lru_bx=lru_bx, lru_lambda=lru_lambda, w_att_o=w_att_o, w_rec_o=w_rec_o,
             w_out=w_out, ffn2_w_gu=ffn2_w_gu, ffn2_w_down=ffn2_w_down)
    M = dict(w_ada=m_w_ada, b_ada=m_b_ada, norm_pre=m_norm_pre, norm_post=m_norm_post, ffn1_w_gu=m_ffn1_w_gu,
             ffn1_w_down=m_ffn1_w_down, w_in=m_w_in, rel_bias=m_rel_bias, conv_w=m_conv_w, conv_b=m_conv_b,
             lru_wa=m_lru_wa, lru_ba=m_lru_ba, lru_wx=m_lru_wx, lru_bx=m_lru_bx, lru_lambda=m_lru_lambda,
             w_att_o=m_w_att_o, w_rec_o=m_w_rec_o, w_out=m_w_out, ffn2_w_gu=m_ffn2_w_gu, ffn2_w_down=m_ffn2_w_down)
    V = dict(w_ada=v_w_ada, b_ada=v_b_ada, norm_pre=v_norm_pre, norm_post=v_norm_post, ffn1_w_gu=v_ffn1_w_gu,
             ffn1_w_down=v_ffn1_w_down, w_in=v_w_in, rel_bias=v_rel_bias, conv_w=v_conv_w, conv_b=v_conv_b,
             lru_wa=v_lru_wa, lru_ba=v_lru_ba, lru_wx=v_lru_wx, lru_bx=v_lru_bx, lru_lambda=v_lru_lambda,
             w_att_o=v_w_att_o, w_rec_o=v_w_rec_o, w_out=v_w_out, ffn2_w_gu=v_ffn2_w_gu, ffn2_w_down=v_ffn2_w_down)
    mx, my, mc = _mesh_pos()
    p = 2 * mx + my
    e = 4 * mx + 2 * my + mc
    xs = x[0]

    g1 = ag_small(_pack([c, norm_pre, norm_post, conv_w], 32), "ag_small_params")
    c_all, npre4, npost4, cw4 = _unpack(g1, [(D,), (3, 256), (3, 256), (4, 256)])
    chipwise = lambda a: jnp.moveaxis(a[0::2], 0, 1).reshape(a.shape[1], D)
    npre, npost, conv_full = chipwise(npre4), chipwise(npost4), chipwise(cw4)

    b_cols = lax.dynamic_slice(b_ada, (0, p * 2304), (1, 2304))
    mod_cols = ada_fwd(c_all, w_ada[0], b_cols, "ada_fwd")
    g2 = ag_small(mod_cols.reshape(144, 128), "ag_mod")
    mod_all = jnp.moveaxis(g2[0::2].reshape(4, 8, 2304), 0, 1).reshape(8, 9 * D)
    mod = lax.dynamic_index_in_dim(mod_all, e, 0, keepdims=False).reshape(3, 3, D)
    zeros3 = jnp.zeros((3, D), F32)
    vecs = [jnp.concatenate([npre[k:k + 1], npost[k:k + 1], mod[k], zeros3], axis=0) for k in range(3)]

    c_arr = jnp.reshape(mc, (1,)).astype(jnp.int32)
    cp_arr = jnp.stack([mc, p]).astype(jnp.int32)
    p_arr = jnp.reshape(p, (1,)).astype(jnp.int32)
    later = BIG[2:]
    geoms = [(kind, R, C) for (_, kind, R, C) in later]
    placed = [ag_local(W[n][0], kind, R, C, p_arr, "ag_local_" + n) for (n, kind, R, C) in later]
    f1_gu, f1_dn = [ag_forward(ag_push(W[n][0], kind, R, C, c_arr, "ag_push_" + n), kind, R, C, "ag_forward_" + n)
                    for (n, kind, R, C) in BIG[:2]]
    mix_s, mix_r, mix_fly, tok1 = ag_start(placed[:4], geoms[:4], [f1_gu, f1_dn], "ag_start_mixer")
    ffn_s, ffn_r, ffn_fly, tok2 = ag_start(placed[4:], geoms[4:], [tok1], "ag_start_ffn2")
    wa_bd = _block_diag4(lru_wa[0]).astype(BF16)
    wx_bd = _block_diag4(lru_wx[0]).astype(BF16)
    pvec = jnp.concatenate([conv_full, conv_b, lru_ba, lru_bx, lru_lambda], axis=0)
    bias = _bias_window(rel_bias[0])

    def arrived(fly, gm, names, ssem, rsem, after, tag):
        done = ag_wait(fly, gm, ssem, rsem, after, "ag_wait_" + tag)
        return [ag_forward(a, kind, R, C, "ag_forward_" + n) for a, (kind, R, C), n in zip(done, gm, names)]

    x1, h1, g1_, u1, a1, f1 = ffn_fwd(xs, vecs[0] + tok2[0:1, 0:1], f1_gu, f1_dn, 0.5, "ffn1_fwd")
    win, wao, wro, wout = arrived(mix_fly, geoms[:4], [b[0] for b in later[:4]], mix_s, mix_r, x1, "mixer")
    h2, qkv, rest = proj_fwd(x1, vecs[1], win, "proj_fwd")
    ao = attn_fwd(qkv, bias, "attn_fwd")
    hl, hg = lru_fwd(rest, pvec, wa_bd, wx_bd, "lru_fwd")
    x2, att, rec, mg, f2 = mix_out_fwd(x1, ao, hg, rest, vecs[1], wao, wro, wout, "mix_out_fwd")
    f2_gu, f2_dn = arrived(ffn_fly, geoms[4:], [b[0] for b in later[4:]], ffn_s, ffn_r, x2, "ffn2")
    x3, h3, g3_, u3, a3, f3 = ffn_fwd(x2, vecs[2], f2_gu, f2_dn, 0.5, "ffn2_fwd")
    dy, lvec = loss_grad(x3, loss_target[0], "loss_grad")
    loss = lax.psum(lvec[0, 0], ("x", "y", "c"))

    G, grads = {}, {}
    geo = {n: (kind, R, C) for (n, kind, R, C) in BIG}

    def reduce_begin(names, tag):
        ps, rb = [], []
        for n in names:
            got = pair_push(G[n], geo[n][0], c_arr, "rs_push_" + n)
            a, b = pair_add(G[n], got, geo[n][0], cp_arr, "rs_pair_sum_" + n)
            ps.append(a)
            rb.append(b)
        return rs_start(ps, rb, [], "rs_start_" + tag)

    def reduce_end(names, flight, after, tag):
        ssem, rsem, ps, rb, _ = flight
        for a, n in zip(rs_wait(ps, rb, ssem, rsem, after, "rs_wait_" + tag), names):
            grads[n] = sum_share(a, *geo[n], "rs_sum_share_" + n)[None]

    dx2, df3, dgu3, va2 = ffn_bwd(dy, x2, f3, g3_, u3, vecs[2], f2_gu, f2_dn, 0.5, "ffn2_bwd")
    G["ffn2_w_gu"] = mm_tn(h3, dgu3, "dw_ffn2_gu", D, 1408, 1024)
    G["ffn2_w_down"] = mm_tn(a3, df3, "dw_ffn2_down", 1408, D, 1024)
    fly_ffn2 = reduce_begin(("ffn2_w_gu", "ffn2_w_down"), "ffn2")
    vec1 = vecs[1] + fly_ffn2[4][0:1, 0:1]
    df2, d_att, d_rec, dao, dhl, d3, va_out = mix_out_bwd(dx2, f2, att, rec, rest, hl, vec1, wao, wro, wout,
                                                          "mix_out_bwd")
    G["w_out"] = mm_tn(mg, df2, "dw_out", D, D, 1024)
    G["w_att_o"] = mm_tn(ao, d_att, "dw_att_o", 512, D, 1024)
    G["w_rec_o"] = mm_tn(hg, d_rec, "dw_rec_o", D, D, 1024)
    dq, db, dkv = attn_bwd(qkv, dao, bias, "attn_bwd")
    dxr, v_lru, dwa_bd, dwx_bd = lru_bwd(dhl, hl, rest, pvec, wa_bd, wx_bd, "lru_bwd")
    dx1, va_in = proj_bwd(dq, dkv, dxr, d3, win, x1, dx2, vecs[1], "proj_bwd")
    gin = mm_tn(h2, dq, "dw_in_q", D, 512, 1024, n_total=PW)
    gin = mm_tn(h2, dkv, "dw_in_kv", D, 512, 1024, prev=gin, col_off=512, n_total=PW)
    gin = mm_tn(h2, dxr, "dw_in_xr", D, 512, 1024, prev=gin, col_off=1536, n_total=PW)
    G["w_in"] = mm_tn(h2, d3, "dw_in_gates", D, 512, 1024, prev=gin, col_off=2560, n_total=PW)
    fly_mix = reduce_begin(("w_in", "w_att_o", "w_rec_o", "w_out"), "mixer")
    vec0 = vecs[0] + fly_mix[4][0:1, 0:1]
    dx0, df1, dgu1, va0 = ffn_bwd(dx1, xs, f1, g1_, u1, vec0, f1_gu, f1_dn, 0.5, "ffn1_bwd")
    G["ffn1_w_gu"] = mm_tn(h1, dgu1, "dw_ffn1_gu", D, 1408, 1024)
    G["ffn1_w_down"] = mm_tn(a1, df1, "dw_ffn1_down", 1408, D, 1024)
    fly_ffn1 = reduce_begin(("ffn1_w_gu", "ffn1_w_down"), "ffn1")
    reduce_end(("ffn2_w_gu", "ffn2_w_down"), fly_ffn2, [fly_ffn1[4]], "ffn2")
    reduce_end(("w_in", "w_att_o", "w_rec_o", "w_out"), fly_mix, [fly_ffn1[4], grads["ffn2_w_down"]], "mixer")

    va1 = va_out + va_in
    vas = (va0, va1, va2)
    dmod = jnp.stack([v[2:5] for v in vas])
    part = {"b_ada": dmod, "norm_pre": jnp.stack([v[0] for v in vas]), "norm_post": jnp.stack([v[1] for v in vas]),
            "rel_bias": bias_grad(db, "bias_grad")[:, :257], "conv_w": v_lru[0:4], "conv_b": v_lru[4],
            "lru_wa": _diag_blocks(dwa_bd), "lru_ba": v_lru[5], "lru_wx": _diag_blocks(dwx_bd), "lru_bx": v_lru[6],
            "lru_lambda": v_lru[7]}
    full_shapes = {"b_ada": (9 * D,), "norm_pre": (3, D), "norm_post": (3, D), "rel_bias": (8, 257),
                   "conv_w": (4, D), "conv_b": (D,), "lru_wa": (16, 64, 64), "lru_ba": (D,),
                   "lru_wx": (16, 64, 64), "lru_bx": (D,), "lru_lambda": (D,)}
    g3 = ag_small(_pack([part[n] for n in SMALL], 1232), "ag_small_grads")
    red = dict(zip(SMALL, _unpack(sum_lead(g3, "sum_small_grads"), [full_shapes[n] for n in SMALL])))
    cols = lambda a: lax.dynamic_slice(a, (0, p * 256), (a.shape[0], 256))
    grads.update({"b_ada": red["b_ada"][None], "norm_pre": cols(red["norm_pre"])[None],
                  "norm_post": cols(red["norm_post"])[None], "rel_bias": red["rel_bias"][None],
                  "conv_w": cols(red["conv_w"])[None], "conv_b": red["conv_b"][None], "lru_wa": red["lru_wa"][None],
                  "lru_ba": red["lru_ba"][None], "lru_wx": red["lru_wx"][None], "lru_bx": red["lru_bx"][None],
                  "lru_lambda": red["lru_lambda"][None]})

    dmod_all = g3[:, :72].reshape(8, 9 * D)
    dmod_cols = jnp.pad(lax.dynamic_slice(dmod_all, (0, p * 2304), (8, 2304)), ((0, 120), (0, 0)))
    c_all_t = jnp.pad(c_all.T, ((0, 0), (0, 120)))
    grads["w_ada"] = ada_bwd(c_all_t, dmod_cols, "ada_bwd")[None]

    delta, new_m, new_v = {}, {}, {}

    def update(n):
        shp = W[n].shape
        d_, m_, v_ = adamw(W[n][0], grads[n][0], M[n][0], V[n][0], "adamw_" + n)
        delta[n], new_m[n], new_v[n] = d_.reshape(shp), m_.reshape(shp), v_.reshape(shp)

    for n in ("w_ada", "ffn2_w_gu", "ffn2_w_down", "w_in", "w_att_o", "w_rec_o", "w_out"):
        update(n)
    packed = [_pack([src[n] for n in SMALL], 1168) for src in (W, grads, M, V)]
    outs = adamw(*packed, "adamw_small")
    for dst, blk in zip((delta, new_m, new_v), outs):
        for n, a in zip(SMALL, _unpack(blk, [W[n].shape for n in SMALL])):
            dst[n] = a
    reduce_end(("ffn1_w_gu", "ffn1_w_down"), fly_ffn1,
               [outs[0], delta["w_ada"], delta["ffn2_w_gu"], delta["ffn2_w_down"], delta["w_in"], delta["w_out"]], "ffn1")
    for n in ("ffn1_w_gu", "ffn1_w_down"):
        update(n)

    return (loss, dx0[None], *[grads[n] for n in WEIGHTS], *[delta[n] for n in WEIGHTS],
            *[new_m[n] for n in WEIGHTS], *[new_v[n] for n in WEIGHTS])
```

```python
import functools

import numpy as np
import jax
import jax.numpy as jnp
from jax import lax
from jax.experimental import pallas as pl
from jax.experimental.pallas import tpu as pltpu

F32 = jnp.float32
BF16 = jnp.bfloat16

D = 1024
FF = 2816
PW = 5632
HP = 128
CHUNK = 64
WIN = 640
TQ = 512
EPS = 1e-6
NEG = -1e30
LRU_C = 8.0
N_DEV = 8
VMEM_LIMIT = 50 * 1024 * 1024

ADAM_LR, ADAM_B1, ADAM_B2, ADAM_EPS, ADAM_WD, ADAM_STEP = 0.001, 0.9, 0.999, 1e-08, 0.01, 10

MESH = pl.DeviceIdType.MESH
ANY = pl.BlockSpec(memory_space=pl.ANY)


def _cp(*sem):
    return pltpu.CompilerParams(dimension_semantics=tuple(sem), vmem_limit_bytes=VMEM_LIMIT)


def _dot(a, b):
    return jnp.dot(a, b, preferred_element_type=F32)


def _dot_nt(a, b):
    return lax.dot_general(a, b, (((1,), (1,)), ((), ())), preferred_element_type=F32)


def _dot_tn(a, b):
    return lax.dot_general(a, b, (((0,), (0,)), ((), ())), preferred_element_type=F32)


def _mean(v):
    return jnp.mean(v, axis=-1, keepdims=True)


def _colsum(v):
    return jnp.sum(v, axis=0, keepdims=True)


def _sigmoid(v):
    return jax.nn.sigmoid(v)


def _expm1(v):
    small = v * (1.0 + v * 0.5 * (1.0 + v * (1.0 / 3.0) * (1.0 + v * 0.25 * (1.0 + v * 0.2 * (
        1.0 + v * (1.0 / 6.0) * (1.0 + v * (1.0 / 7.0)))))))
    return jnp.where(jnp.abs(v) < 0.25, small, jnp.exp(v) - 1.0)


_GK = 0.7978845608028654


def _gelu(v):
    t = jnp.tanh(_GK * (v + 0.044715 * v * v * v))
    return 0.5 * v * (1.0 + t)


def _gelu_grad(v):
    t = jnp.tanh(_GK * (v + 0.044715 * v * v * v))
    return 0.5 * (1.0 + t) + 0.5 * v * (1.0 - t * t) * _GK * (1.0 + 3.0 * 0.044715 * v * v)


def _pre_norm(xv, vec_ref):
    r = lax.rsqrt(_mean(xv * xv) + EPS)
    n = xv * r * vec_ref[0:1, :]
    return n * (1.0 + vec_ref[3:4, :]) + vec_ref[2:3, :]


def _pre_norm_bwd(dh, xv, dres, vec_ref, vacc_ref):
    r = lax.rsqrt(_mean(xv * xv) + EPS)
    xh = xv * r
    n = xh * vec_ref[0:1, :]
    vacc_ref[2:3, :] += _colsum(dh)
    vacc_ref[3:4, :] += _colsum(dh * n)
    dn = dh * (1.0 + vec_ref[3:4, :])
    vacc_ref[0:1, :] += _colsum(dn * xh)
    dxh = dn * vec_ref[0:1, :]
    return r * (dxh - xh * _mean(dxh * xh)) + dres


def _post_norm_bwd(dxo, fv, res, vec_ref, vacc_ref):
    rf = lax.rsqrt(_mean(fv * fv) + EPS)
    fh = fv * rf
    gp = vec_ref[1:2, :]
    vacc_ref[4:5, :] += _colsum(res * dxo * (fh * gp))
    dy = (res * vec_ref[4:5, :]) * dxo
    vacc_ref[1:2, :] += _colsum(dy * fh)
    dfn = dy * gp
    return rf * (dfn - fh * _mean(dfn * fh))


def ffn_fwd(x, vec, w_gu, w_dn, res, name, tm=1024, tf=256):
    S = x.shape[0]
    tm = min(tm, S)
    nf = FF // tf

    def body(x_ref, vec_ref, wg_ref, wu_ref, wd_ref, xo_ref, h_ref, g_ref, u_ref, a_ref, f_ref, hs, acc):
        j = pl.program_id(1)

        @pl.when(j == 0)
        def _():
            h = _pre_norm(x_ref[...], vec_ref).astype(BF16)
            hs[...] = h
            h_ref[...] = h
            acc[...] = jnp.zeros_like(acc)

        h = hs[...]
        g = _dot(h, wg_ref[...])
        u = _dot(h, wu_ref[...])
        g_ref[...] = g.astype(BF16)
        u_ref[...] = u.astype(BF16)
        a = (g * _sigmoid(g) * u).astype(BF16)
        a_ref[...] = a
        acc[...] += _dot(a, wd_ref[...])

        @pl.when(j == nf - 1)
        def _():
            f = acc[...]
            f_ref[...] = f
            y = f * lax.rsqrt(_mean(f * f) + EPS) * vec_ref[1:2, :]
            xo_ref[...] = x_ref[...] + (res * vec_ref[4:5, :]) * y

    row = lambda i, j: (i, 0)
    return pl.pallas_call(
        body, name=name, grid=(S // tm, nf),
        in_specs=[pl.BlockSpec((tm, D), row), pl.BlockSpec((8, D), lambda i, j: (0, 0)),
                  pl.BlockSpec((D, tf), lambda i, j: (0, j)), pl.BlockSpec((D, tf), lambda i, j: (0, j + nf)),
                  pl.BlockSpec((tf, D), lambda i, j: (j, 0))],
        out_specs=[pl.BlockSpec((tm, D), row), pl.BlockSpec((tm, D), row),
                   pl.BlockSpec((tm, tf), lambda i, j: (i, j)), pl.BlockSpec((tm, tf), lambda i, j: (i, j)),
                   pl.BlockSpec((tm, tf), lambda i, j: (i, j)), pl.BlockSpec((tm, D), row)],
        out_shape=[jax.ShapeDtypeStruct((S, D), F32), jax.ShapeDtypeStruct((S, D), BF16),
                   jax.ShapeDtypeStruct((S, FF), BF16), jax.ShapeDtypeStruct((S, FF), BF16),
                   jax.ShapeDtypeStruct((S, FF), BF16), jax.ShapeDtypeStruct((S, D), F32)],
        scratch_shapes=[pltpu.VMEM((tm, D), BF16), pltpu.VMEM((tm, D), F32)],
        compiler_params=_cp("parallel", "arbitrary"),
    )(x, vec, w_gu, w_gu, w_dn)


def ffn_bwd(dxo, x, f, g, u, vec, w_gu, w_dn, res, name, tm=1024, tf=256):
    S = x.shape[0]
    tm = min(tm, S)
    nf = FF // tf

    def body(dxo_ref, x_ref, f_ref, g_ref, u_ref, vec_ref, wg_ref, wu_ref, wd_ref,
             dx_ref, df_ref, dgu_ref, vacc_ref, dfs, acc):
        i, j = pl.program_id(0), pl.program_id(1)

        @pl.when((i == 0) & (j == 0))
        def _():
            vacc_ref[...] = jnp.zeros_like(vacc_ref)

        @pl.when(j == 0)
        def _():
            df = _post_norm_bwd(dxo_ref[...], f_ref[...], res, vec_ref, vacc_ref).astype(BF16)
            dfs[...] = df
            df_ref[...] = df
            acc[...] = jnp.zeros_like(acc)

        da = _dot_nt(dfs[...], wd_ref[...])
        gv, uv = g_ref[...].astype(F32), u_ref[...].astype(F32)
        sg = _sigmoid(gv)
        dg = (da * uv * (sg * (1.0 + gv * (1.0 - sg)))).astype(BF16)
        du = (da * (gv * sg)).astype(BF16)
        dgu_ref[0] = dg
        dgu_ref[1] = du
        acc[...] += _dot_nt(dg, wg_ref[...]) + _dot_nt(du, wu_ref[...])

        @pl.when(j == nf - 1)
        def _():
            dx_ref[...] = _pre_norm_bwd(acc[...], x_ref[...], dxo_ref[...], vec_ref, vacc_ref)

    row = lambda i, j: (i, 0)
    return pl.pallas_call(
        body, name=name, grid=(S // tm, nf),
        in_specs=[pl.BlockSpec((tm, D), row, pipeline_mode=pl.Buffered(1))] * 3
        + [pl.BlockSpec((tm, tf), lambda i, j: (i, j)), pl.BlockSpec((tm, tf), lambda i, j: (i, j)),
                  pl.BlockSpec((8, D), lambda i, j: (0, 0)),
                  pl.BlockSpec((D, tf), lambda i, j: (0, j)), pl.BlockSpec((D, tf), lambda i, j: (0, j + nf)),
                  pl.BlockSpec((tf, D), lambda i, j: (j, 0))],
        out_specs=[pl.BlockSpec((tm, D), row), pl.BlockSpec((tm, D), row),
                   pl.BlockSpec((2, tm, tf), lambda i, j: (0, i, j)),
                   pl.BlockSpec((8, D), lambda i, j: (0, 0))],
        out_shape=[jax.ShapeDtypeStruct((S, D), F32), jax.ShapeDtypeStruct((S, D), BF16),
                   jax.ShapeDtypeStruct((2, S, FF), BF16), jax.ShapeDtypeStruct((8, D), F32)],
        scratch_shapes=[pltpu.VMEM((tm, D), BF16), pltpu.VMEM((tm, D), F32)],
        compiler_params=_cp("arbitrary", "arbitrary"),
    )(dxo, x, f, g, u, vec, w_gu, w_gu, w_dn)


def mm_tn(a, b, name, tm, tn, tk, out_dtype=BF16, prev=None, col_off=0, n_total=None):
    S, M = a.shape
    if b.ndim == 3:
        G, _, Nf = b.shape
    else:
        G, Nf = 1, b.shape[1]
    N = G * Nf
    n_total = N if n_total is None else n_total
    tk = min(tk, S)
    nbf = Nf // tn
    nk = S // tk
    ob = col_off // tn

    def body(*refs):
        a_ref, b_ref = refs[0], refs[1]
        o_ref, acc = refs[-2], refs[-1]
        k = pl.program_id(2)

        @pl.when(k == 0)
        def _():
            acc[...] = jnp.zeros_like(acc)

        acc[...] += _dot_tn(a_ref[...], b_ref[...])

        @pl.when(k == nk - 1)
        def _():
            o_ref[...] = acc[...].astype(out_dtype)

    if b.ndim == 3:
        b_spec = pl.BlockSpec((None, tk, tn), lambda i, j, k: (j // nbf, k, j % nbf))
    else:
        b_spec = pl.BlockSpec((tk, tn), lambda i, j, k: (k, j))
    in_specs = [pl.BlockSpec((tk, tm), lambda i, j, k: (k, i)), b_spec]
    args = [a, b]
    aliases = {}
    if prev is not None:
        in_specs.append(ANY)
        args.append(prev)
        aliases = {2: 0}
    return pl.pallas_call(
        body, name=name, grid=(M // tm, N // tn, nk),
        in_specs=in_specs,
        out_specs=pl.BlockSpec((tm, tn), lambda i, j, k: (i, j + ob)),
        out_shape=jax.ShapeDtypeStruct((M, n_total), out_dtype),
        scratch_shapes=[pltpu.VMEM((tm, tn), F32)],
        input_output_aliases=aliases,
        compiler_params=_cp("parallel", "parallel", "arbitrary"),
    )(*args)


def proj_fwd(x, vec, w_in, name, tm=1024, tn=512):
    S = x.shape[0]
    tm = min(tm, S)
    nq = 1536 // tn

    def body(x_ref, vec_ref, w_ref, h_ref, qkv_ref, rest_ref, hs):
        j = pl.program_id(1)

        @pl.when(j == 0)
        def _():
            h = _pre_norm(x_ref[...], vec_ref).astype(BF16)
            hs[...] = h
            h_ref[...] = h

        r = _dot(hs[...], w_ref[...])

        @pl.when(j < nq)
        def _():
            qkv_ref[...] = r.astype(BF16)

        @pl.when(j >= nq)
        def _():
            rest_ref[...] = r

    row = lambda i, j: (i, 0)
    return pl.pallas_call(
        body, name=name, grid=(S // tm, PW // tn),
        in_specs=[pl.BlockSpec((tm, D), row), pl.BlockSpec((8, D), lambda i, j: (0, 0)),
                  pl.BlockSpec((D, tn), lambda i, j: (0, j))],
        out_specs=[pl.BlockSpec((tm, D), row),
                   pl.BlockSpec((tm, tn), lambda i, j: (i, jnp.minimum(j, nq - 1))),
                   pl.BlockSpec((tm, tn), lambda i, j: (i, jnp.maximum(j - nq, 0)))],
        out_shape=[jax.ShapeDtypeStruct((S, D), BF16), jax.ShapeDtypeStruct((S, 1536), BF16),
                   jax.ShapeDtypeStruct((S, 4096), F32)],
        scratch_shapes=[pltpu.VMEM((tm, D), BF16)],
        compiler_params=_cp("parallel", "arbitrary"),
    )(x, vec, w_in)


def proj_bwd(dq, dkv, dxr, d3, w_in, x, dxo, vec, name, tm=1024, tk=512):
    S = x.shape[0]
    tm = min(tm, S)
    nk = PW // tk

    def body(dq_ref, dkv_ref, dxr_ref, d3_ref, w_ref, x_ref, dxo_ref, vec_ref, dx_ref, vacc_ref, acc):
        i, j = pl.program_id(0), pl.program_id(1)

        @pl.when((i == 0) & (j == 0))
        def _():
            vacc_ref[...] = jnp.zeros_like(vacc_ref)

        @pl.when(j == 0)
        def _():
            acc[...] = _dot_nt(dq_ref[...], w_ref[...])

        @pl.when((j >= 1) & (j < 3))
        def _():
            acc[...] += _dot_nt(dkv_ref[...], w_ref[...])

        @pl.when((j >= 3) & (j < 5))
        def _():
            acc[...] += _dot_nt(dxr_ref[...], w_ref[...])

        @pl.when(j >= 5)
        def _():
            acc[...] += _dot_nt(d3_ref[...], w_ref[...])

        @pl.when(j == nk - 1)
        def _():
            dx_ref[...] = _pre_norm_bwd(acc[...], x_ref[...], dxo_ref[...], vec_ref, vacc_ref)

    row = lambda i, j: (i, 0)
    return pl.pallas_call(
        body, name=name, grid=(S // tm, nk),
        in_specs=[pl.BlockSpec((None, tm, tk), lambda i, j: (0, i, 0)),
                  pl.BlockSpec((None, tm, tk), lambda i, j: (jnp.clip(j - 1, 0, 1), i, 0)),
                  pl.BlockSpec((tm, tk), lambda i, j: (i, jnp.clip(j - 3, 0, 1))),
                  pl.BlockSpec((None, tm, tk), lambda i, j: (jnp.clip(j - 5, 0, 5) // 2, i, jnp.clip(j - 5, 0, 5) % 2)),
                  pl.BlockSpec((D, tk), lambda i, j: (0, j)),
                  pl.BlockSpec((tm, D), row), pl.BlockSpec((tm, D), row),
                  pl.BlockSpec((8, D), lambda i, j: (0, 0))],
        out_specs=[pl.BlockSpec((tm, D), row), pl.BlockSpec((8, D), lambda i, j: (0, 0))],
        out_shape=[jax.ShapeDtypeStruct((S, D), F32), jax.ShapeDtypeStruct((8, D), F32)],
        scratch_shapes=[pltpu.VMEM((tm, D), F32)],
        compiler_params=_cp("arbitrary", "arbitrary"),
    )(dq, dkv, dxr, d3, w_in, x, dxo, vec)


def _two_heads(v, lane):
    zero = jnp.zeros((), v.dtype)
    return jnp.concatenate([jnp.where(lane < 64, v, zero), jnp.where(lane >= 64, v, zero)], axis=0)


def _attn_probs(qm, ka, bias_h, i, grp):
    s = _dot_nt(qm, ka) + bias_h
    col = lax.broadcasted_iota(jnp.int32, s.shape, 1)
    first_key = jnp.where(i == 0, 512 - 128 * grp, 0)
    s = jnp.where(col >= first_key, s, NEG)
    e = jnp.exp(s - jnp.max(s, axis=-1, keepdims=True))
    return e / jnp.sum(e, axis=-1, keepdims=True)


def attn_fwd(qkv, bias, name):
    S = qkv.shape[0]
    nb = S // TQ

    def body(q_ref, kp_ref, kc_ref, vp_ref, vc_ref, b_ref, o_ref, kw, vw):
        i = pl.program_id(1)
        kw[0:TQ, :] = kp_ref[...]
        kw[TQ:2 * TQ, :] = kc_ref[...]
        vw[0:TQ, :] = vp_ref[...]
        vw[TQ:2 * TQ, :] = vc_ref[...]
        lane = lax.broadcasted_iota(jnp.int32, (1, HP), 1)

        def group(a, carry):
            r0 = pl.multiple_of(a * 128, 128)
            qa = q_ref[pl.ds(r0, 128), :] * jnp.asarray(0.125, BF16)
            ka = kw[pl.ds(r0, WIN), :]
            va = vw[pl.ds(r0, WIN), :]
            p = _attn_probs(_two_heads(qa, lane), ka, b_ref[...], i, a)
            o2 = _dot(p.astype(BF16), va)
            o_ref[pl.ds(r0, 128), :] = jnp.where(lane < 64, o2[0:128], o2[128:256]).astype(BF16)
            return carry

        lax.fori_loop(0, TQ // 128, group, 0, unroll=True)

    prev = lambda h, i: (jnp.maximum(i - 1, 0), 0)
    return pl.pallas_call(
        body, name=name, grid=(4, nb),
        in_specs=[pl.BlockSpec((TQ, HP), lambda h, i: (i, h)),
                  pl.BlockSpec((TQ, HP), lambda h, i: (jnp.maximum(i - 1, 0), 4 + h)),
                  pl.BlockSpec((TQ, HP), lambda h, i: (i, 4 + h)),
                  pl.BlockSpec((TQ, HP), lambda h, i: (jnp.maximum(i - 1, 0), 8 + h)),
                  pl.BlockSpec((TQ, HP), lambda h, i: (i, 8 + h)),
                  pl.BlockSpec((None, 256, WIN), lambda h, i: (h, 0, 0))],
        out_specs=pl.BlockSpec((TQ, HP), lambda h, i: (i, h)),
        out_shape=jax.ShapeDtypeStruct((S, 512), BF16),
        scratch_shapes=[pltpu.VMEM((2 * TQ, HP), BF16), pltpu.VMEM((2 * TQ, HP), BF16)],
        compiler_params=_cp("parallel", "arbitrary"),
    )(qkv, qkv, qkv, qkv, qkv, bias)


def attn_bwd(qkv, do, bias, name):
    S = qkv.shape[0]
    nb = S // TQ

    def body(q_ref, kp_ref, kc_ref, vp_ref, vc_ref, do_ref, b_ref, dqkv_ref, db_ref, dkv_ref, kw, vw, ak, av):
        i = pl.program_id(1)

        @pl.when(i == 0)
        def _():
            db_ref[...] = jnp.zeros_like(db_ref)
            ak[...] = jnp.zeros_like(ak)
            av[...] = jnp.zeros_like(av)

        @pl.when(i > 0)
        def _():
            ak[0:TQ, :] = ak[TQ:2 * TQ, :]
            av[0:TQ, :] = av[TQ:2 * TQ, :]
            ak[TQ:2 * TQ, :] = jnp.zeros((TQ, HP), F32)
            av[TQ:2 * TQ, :] = jnp.zeros((TQ, HP), F32)

        @pl.when(i < nb)
        def _():
            kw[0:TQ, :] = kp_ref[...]
            kw[TQ:2 * TQ, :] = kc_ref[...]
            vw[0:TQ, :] = vp_ref[...]
            vw[TQ:2 * TQ, :] = vc_ref[...]
            lane = lax.broadcasted_iota(jnp.int32, (1, HP), 1)

            def group(a, carry):
                r0 = pl.multiple_of(a * 128, 128)
                q2 = _two_heads(q_ref[pl.ds(r0, 128), :] * jnp.asarray(0.125, BF16), lane)
                do2 = _two_heads(do_ref[pl.ds(r0, 128), :], lane)
                ka = kw[pl.ds(r0, WIN), :]
                va = vw[pl.ds(r0, WIN), :]
                p = _attn_probs(q2, ka, b_ref[...], i, a)
                dp = _dot_nt(do2, va)
                ds = p * (dp - jnp.sum(p * dp, axis=-1, keepdims=True))
                db_ref[...] += ds
                dsb = ds.astype(BF16)
                dq2 = _dot(dsb, ka)
                ak[pl.ds(r0, WIN), :] += _dot_tn(dsb, q2)
                av[pl.ds(r0, WIN), :] += _dot_tn(p.astype(BF16), do2)
                dq = jnp.where(lane < 64, dq2[0:128], dq2[128:256])
                dqkv_ref[0, pl.ds(r0, 128), :] = (dq * 0.125).astype(BF16)
                return carry

            lax.fori_loop(0, TQ // 128, group, 0, unroll=True)

        @pl.when(i > 0)
        def _():
            dkv_ref[0] = ak[0:TQ, :].astype(BF16)
            dkv_ref[1] = av[0:TQ, :].astype(BF16)

    cur = lambda i: jnp.minimum(i, nb - 1)
    prv = lambda i: jnp.clip(i - 1, 0, nb - 1)
    dq, db, dkv = pl.pallas_call(
        body, name=name, grid=(4, nb + 1),
        in_specs=[pl.BlockSpec((TQ, HP), lambda h, i: (cur(i), h)),
                  pl.BlockSpec((TQ, HP), lambda h, i: (prv(i), 4 + h)),
                  pl.BlockSpec((TQ, HP), lambda h, i: (cur(i), 4 + h)),
                  pl.BlockSpec((TQ, HP), lambda h, i: (prv(i), 8 + h)),
                  pl.BlockSpec((TQ, HP), lambda h, i: (cur(i), 8 + h)),
                  pl.BlockSpec((TQ, HP), lambda h, i: (cur(i), h)),
                  pl.BlockSpec((None, 256, WIN), lambda h, i: (h, 0, 0))],
        out_specs=[pl.BlockSpec((1, TQ, HP), lambda h, i: (0, cur(i), h)),
                   pl.BlockSpec((None, 256, WIN), lambda h, i: (h, 0, 0)),
                   pl.BlockSpec((2, TQ, HP), lambda h, i: (0, prv(i), h))],
        out_shape=[jax.ShapeDtypeStruct((1, S, 512), BF16), jax.ShapeDtypeStruct((4, 256, WIN), F32),
                   jax.ShapeDtypeStruct((2, S, 512), BF16)],
        scratch_shapes=[pltpu.VMEM((2 * TQ, HP), BF16), pltpu.VMEM((2 * TQ, HP), BF16),
                        pltpu.VMEM((2 * TQ, HP), F32), pltpu.VMEM((2 * TQ, HP), F32)],
        compiler_params=_cp("parallel", "arbitrary"),
    )(qkv, qkv, qkv, qkv, qkv, do, bias)
    return dq, db, dkv


def bias_grad(db, name):
    def body(db_ref, o_ref):
        r = lax.broadcasted_iota(jnp.int32, (128, 128), 0)
        c = lax.broadcasted_iota(jnp.int32, (128, 128), 1)
        flip = (r + c == 127).astype(BF16)
        lane = lax.broadcasted_iota(jnp.int32, (16, 384), 1)
        src = lax.broadcasted_iota(jnp.int32, (128, 384), 0)
        dst = lax.broadcasted_iota(jnp.int32, (128, 384), 1)

        def split_dot(v, m):
            hi = v.astype(BF16)
            r1 = v - hi.astype(F32)
            mid = r1.astype(BF16)
            lo = (r1 - mid.astype(F32)).astype(BF16)
            return _dot(hi, m) + _dot(mid, m) + _dot(lo, m)

        def diag_sums(w):
            y = pltpu.roll(split_dot(w, flip), 0, 1, stride=1, stride_axis=0)
            return jnp.broadcast_to(_colsum(y), (16, 128))

        w4 = db_ref[0, :, 512:640]
        w3 = db_ref[0, :, 384:512]
        far = jnp.sum(db_ref[0, :, 0:384]) + jnp.sum(jnp.where(r >= c, w3, 0.0))
        lo4 = diag_sums(jnp.where(r >= c, w4, 0.0))
        up4 = diag_sums(jnp.where(r < c, w4, 0.0))
        up3 = diag_sums(jnp.where(r < c, w3, 0.0))
        p_lo4 = (dst == 128 + (src + 1) % 128).astype(BF16)
        p_up4 = ((dst == src + 1) & (src < 127)).astype(BF16)
        p_up3 = ((dst == src + 129) & (src < 127)).astype(BF16)
        out = split_dot(lo4, p_lo4) + split_dot(up4, p_up4) + split_dot(up3, p_up3)
        o_ref[0] = out + jnp.where(lane == 256, far, 0.0)

    return pl.pallas_call(
        body, name=name, grid=(8,),
        in_specs=[pl.BlockSpec((1, 128, WIN), lambda h: (h, 0, 0))],
        out_specs=pl.BlockSpec((1, 16, 384), lambda h: (h, 0, 0)),
        out_shape=jax.ShapeDtypeStruct((8, 16, 384), F32),
        compiler_params=_cp("parallel"),
    )(db)[:, 0, :]


LT = 256
LC = 512


def _lru_gates(xs, pv_ref, wa_ref, wx_ref, tl):
    xc = (pv_ref[4:5, :] + pv_ref[3:4, :] * xs[pl.ds(8, tl), :] + pv_ref[2:3, :] * xs[pl.ds(7, tl), :]
          + pv_ref[1:2, :] * xs[pl.ds(6, tl), :] + pv_ref[0:1, :] * xs[pl.ds(5, tl), :])
    xcb = xc.astype(BF16)
    pa = jnp.concatenate([_dot(xcb[:, 0:256], wa_ref[0]), _dot(xcb[:, 256:512], wa_ref[1])], axis=1)
    px = jnp.concatenate([_dot(xcb[:, 0:256], wx_ref[0]), _dot(xcb[:, 256:512], wx_ref[1])], axis=1)
    r = _sigmoid(pa + pv_ref[5:6, :])
    ig = _sigmoid(px + pv_ref[6:7, :])
    z = -pv_ref[7:8, :]
    sp = jnp.maximum(z, 0.0) + jnp.log1p(jnp.exp(-jnp.abs(z)))
    log_a = (-LRU_C * r) * sp
    a = jnp.exp(log_a)
    mult = jnp.sqrt(-_expm1(2.0 * log_a))
    return xc, xcb, r, ig, sp, a, mult


def lru_fwd(rest, pvec, wa, wx, name):
    S = rest.shape[0]
    tl = min(LT, S)
    nt = S // tl

    def body(xr_ref, halo_ref, yr_ref, pv_ref, wa_ref, wx_ref, h_ref, hg_ref, xs, a_s, u_s, h_s, carry):
        ti = pl.program_id(1)

        @pl.when(ti == 0)
        def _():
            carry[...] = jnp.zeros_like(carry)

        xs[0:8, :] = jnp.where(ti > 0, halo_ref[...], 0.0)
        xs[pl.ds(8, tl), :] = xr_ref[...]
        xc, _, _, ig, _, a, mult = _lru_gates(xs, pv_ref, wa_ref, wx_ref, tl)
        a_s[...] = a
        u_s[...] = mult * (ig * xc)
        row = lax.broadcasted_iota(jnp.int32, (8, LC), 0)

        def blk(bi, c):
            o = pl.multiple_of(bi * 8, 8)
            av = a_s[pl.ds(o, 8), :]
            bv = u_s[pl.ds(o, 8), :]
            for d in (1, 2, 4):
                a_sh = pltpu.roll(av, d, 0)
                b_sh = pltpu.roll(bv, d, 0)
                m = row >= d
                bv = jnp.where(m, av * b_sh + bv, bv)
                av = jnp.where(m, av * a_sh, av)
            hv = bv + av * c
            h_s[pl.ds(o, 8), :] = hv
            return hv[7:8, :]

        carry[...] = lax.fori_loop(0, tl // 8, blk, carry[...])
        h = h_s[...]
        h_ref[...] = h
        hg_ref[...] = (h * _gelu(yr_ref[...])).astype(BF16)

    hb = tl // 8
    return pl.pallas_call(
        body, name=name, grid=(2, nt),
        in_specs=[pl.BlockSpec((tl, LC), lambda c, t: (t, c)),
                  pl.BlockSpec((8, LC), lambda c, t: (jnp.maximum(t * hb - 1, 0), c)),
                  pl.BlockSpec((tl, LC), lambda c, t: (t, 2 + c)),
                  pl.BlockSpec((8, LC), lambda c, t: (0, c)),
                  pl.BlockSpec((2, 256, 256), lambda c, t: (c, 0, 0)),
                  pl.BlockSpec((2, 256, 256), lambda c, t: (c, 0, 0))],
        out_specs=[pl.BlockSpec((tl, LC), lambda c, t: (t, c)), pl.BlockSpec((tl, LC), lambda c, t: (t, c))],
        out_shape=[jax.ShapeDtypeStruct((S, D), F32), jax.ShapeDtypeStruct((S, D), BF16)],
        scratch_shapes=[pltpu.VMEM((tl + 8, LC), F32), pltpu.VMEM((tl, LC), F32), pltpu.VMEM((tl, LC), F32),
                        pltpu.VMEM((tl, LC), F32), pltpu.VMEM((1, LC), F32)],
        compiler_params=_cp("parallel", "arbitrary"),
    )(rest, rest, rest, pvec, wa, wx)


def lru_bwd(dh, h, rest, pvec, wa, wx, name):
    S = rest.shape[0]
    tl = min(LT, S)
    nt = S // tl

    def body(dh_ref, h_ref, hhalo_ref, xr_ref, xhalo_ref, pv_ref, wa_ref, wx_ref,
             dxr_ref, vacc_ref, dwa_ref, dwx_ref,
             xs, hs, a_s, ash_s, b_s, lam_s, dxe, anext, lnext, dxnext):
        ti = pl.program_id(1)
        tr = nt - 1 - ti

        @pl.when(ti == 0)
        def _():
            anext[...] = jnp.zeros_like(anext)
            lnext[...] = jnp.zeros_like(lnext)
            dxnext[...] = jnp.zeros_like(dxnext)
            vacc_ref[...] = jnp.zeros_like(vacc_ref)
            dwa_ref[...] = jnp.zeros_like(dwa_ref)
            dwx_ref[...] = jnp.zeros_like(dwx_ref)

        xs[0:8, :] = jnp.where(tr > 0, xhalo_ref[...], 0.0)
        xs[pl.ds(8, tl), :] = xr_ref[...]
        xc, xcb, r, ig, sp, a, mult = _lru_gates(xs, pv_ref, wa_ref, wx_ref, tl)

        a_s[pl.ds(0, tl), :] = a
        a_s[pl.ds(tl, 8), :] = jnp.broadcast_to(anext[...], (8, LC))
        ash_s[...] = a_s[pl.ds(1, tl), :]
        b_s[...] = dh_ref[...]
        row = lax.broadcasted_iota(jnp.int32, (8, LC), 0)

        def blk(k, c):
            o = pl.multiple_of((tl // 8 - 1 - k) * 8, 8)
            av = ash_s[pl.ds(o, 8), :]
            bv = b_s[pl.ds(o, 8), :]
            for d in (1, 2, 4):
                a_sh = pltpu.roll(av, 8 - d, 0)
                b_sh = pltpu.roll(bv, 8 - d, 0)
                m = row < 8 - d
                bv = jnp.where(m, bv + av * b_sh, bv)
                av = jnp.where(m, av * a_sh, av)
            lv = bv + av * c
            lam_s[pl.ds(o, 8), :] = lv
            return lv[0:1, :]

        lnext[...] = lax.fori_loop(0, tl // 8, blk, lnext[...])
        anext[...] = a[0:1, :]
        lam = lam_s[...]

        hs[0:8, :] = jnp.where(tr > 0, hhalo_ref[...], 0.0)
        hs[pl.ds(8, tl), :] = h_ref[...]
        d_a = lam * hs[pl.ds(7, tl), :]
        d_mult = lam * (ig * xc)
        d_ig = lam * mult * xc
        dxc = lam * mult * ig
        d_log_a = d_a * a - d_mult * (a * a) / mult
        d_r = d_log_a * (-LRU_C * sp)
        vacc_ref[7:8, :] += _colsum(d_log_a * (-LRU_C * r)) * (-_sigmoid(-pv_ref[7:8, :]))
        d_pa = d_r * r * (1.0 - r)
        d_px = d_ig * ig * (1.0 - ig)
        vacc_ref[5:6, :] += _colsum(d_pa)
        vacc_ref[6:7, :] += _colsum(d_px)
        dpa = d_pa.astype(BF16)
        dpx = d_px.astype(BF16)
        back = []
        for g in range(2):
            sl = slice(256 * g, 256 * g + 256)
            dwa_ref[g] += _dot_tn(xcb[:, sl], dpa[:, sl])
            dwx_ref[g] += _dot_tn(xcb[:, sl], dpx[:, sl])
            back.append(_dot_nt(dpa[:, sl], wa_ref[g]) + _dot_nt(dpx[:, sl], wx_ref[g]))
        dxc = dxc + jnp.concatenate(back, axis=1)
        vacc_ref[4:5, :] += _colsum(dxc)
        for k in range(4):
            vacc_ref[k:k + 1, :] += _colsum(dxc * xs[pl.ds(5 + k, tl), :])
        dxe[pl.ds(0, tl), :] = dxc
        dxe[pl.ds(tl, 8), :] = dxnext[...]
        dxr = (pv_ref[3:4, :] * dxc + pv_ref[2:3, :] * dxe[pl.ds(1, tl), :]
               + pv_ref[1:2, :] * dxe[pl.ds(2, tl), :] + pv_ref[0:1, :] * dxe[pl.ds(3, tl), :])
        dxr_ref[...] = dxr.astype(BF16)
        dxnext[...] = dxc[0:8, :]

    hb = tl // 8
    rev = lambda t: nt - 1 - t
    halo = lambda t: jnp.maximum(rev(t) * hb - 1, 0)
    big = lambda: pltpu.VMEM((tl + 8, LC), F32)
    til = lambda: pltpu.VMEM((tl, LC), F32)
    return pl.pallas_call(
        body, name=name, grid=(2, nt),
        in_specs=[pl.BlockSpec((tl, LC), lambda c, t: (rev(t), c)),
                  pl.BlockSpec((tl, LC), lambda c, t: (rev(t), c)),
                  pl.BlockSpec((8, LC), lambda c, t: (halo(t), c)),
                  pl.BlockSpec((tl, LC), lambda c, t: (rev(t), c)),
                  pl.BlockSpec((8, LC), lambda c, t: (halo(t), c)),
                  pl.BlockSpec((8, LC), lambda c, t: (0, c)),
                  pl.BlockSpec((2, 256, 256), lambda c, t: (c, 0, 0)),
                  pl.BlockSpec((2, 256, 256), lambda c, t: (c, 0, 0))],
        out_specs=[pl.BlockSpec((tl, LC), lambda c, t: (rev(t), c)),
                   pl.BlockSpec((8, LC), lambda c, t: (0, c)),
                   pl.BlockSpec((2, 256, 256), lambda c, t: (c, 0, 0)),
                   pl.BlockSpec((2, 256, 256), lambda c, t: (c, 0, 0))],
        out_shape=[jax.ShapeDtypeStruct((S, D), BF16), jax.ShapeDtypeStruct((8, D), F32),
                   jax.ShapeDtypeStruct((4, 256, 256), F32), jax.ShapeDtypeStruct((4, 256, 256), F32)],
        scratch_shapes=[big(), big(), big(), til(), til(), til(), big(),
                        pltpu.VMEM((1, LC), F32), pltpu.VMEM((1, LC), F32), pltpu.VMEM((8, LC), F32)],
        compiler_params=_cp("parallel", "arbitrary"),
    )(dh, h, h, rest, rest, pvec, wa, wx)


def mix_out_fwd(x, ao, hg, rest, vec, w_att_o, w_rec_o, w_out, name, tm=256):
    S = x.shape[0]
    tm = min(tm, S)

    def body(x_ref, ao_ref, hg_ref, ga_ref, gr_ref, vec_ref, wa_ref, wr_ref, wo_ref,
             xo_ref, att_ref, rec_ref, mg_ref, f_ref):
        att = _dot(ao_ref[...], wa_ref[...])
        rec = _dot(hg_ref[...], wr_ref[...])
        att_ref[...] = att
        rec_ref[...] = rec
        mg = (_sigmoid(ga_ref[...]) * att + _sigmoid(gr_ref[...]) * rec).astype(BF16)
        mg_ref[...] = mg
        f = _dot(mg, wo_ref[...])
        f_ref[...] = f
        y = f * lax.rsqrt(_mean(f * f) + EPS) * vec_ref[1:2, :]
        xo_ref[...] = x_ref[...] + (1.0 * vec_ref[4:5, :]) * y

    row = lambda i: (i, 0)
    full = lambda r: pl.BlockSpec((r, D), lambda i: (0, 0))
    return pl.pallas_call(
        body, name=name, grid=(S // tm,),
        in_specs=[pl.BlockSpec((tm, D), row), pl.BlockSpec((tm, 512), row), pl.BlockSpec((tm, D), row),
                  pl.BlockSpec((tm, D), lambda i: (i, 2)), pl.BlockSpec((tm, D), lambda i: (i, 3)),
                  full(8), full(512), full(D), full(D)],
        out_specs=[pl.BlockSpec((tm, D), row)] * 5,
        out_shape=[jax.ShapeDtypeStruct((S, D), F32), jax.ShapeDtypeStruct((S, D), F32),
                   jax.ShapeDtypeStruct((S, D), F32), jax.ShapeDtypeStruct((S, D), BF16),
                   jax.ShapeDtypeStruct((S, D), F32)],
        compiler_params=_cp("parallel"),
    )(x, ao, hg, rest, rest, vec, w_att_o, w_rec_o, w_out)


def mix_out_bwd(dxo, f, att, rec, rest, h, vec, w_att_o, w_rec_o, w_out, name, tm=256):
    S = dxo.shape[0]
    tm = min(tm, S)

    def body(dxo_ref, f_ref, att_ref, rec_ref, yr_ref, ga_ref, gr_ref, h_ref, vec_ref, wa_ref, wr_ref, wo_ref,
             df_ref, da_ref, dr_ref, dao_ref, dh_ref, d3_ref, vacc_ref):
        @pl.when(pl.program_id(0) == 0)
        def _():
            vacc_ref[...] = jnp.zeros_like(vacc_ref)

        df = _post_norm_bwd(dxo_ref[...], f_ref[...], 1.0, vec_ref, vacc_ref).astype(BF16)
        df_ref[...] = df
        dm = _dot_nt(df, wo_ref[...])
        sa = _sigmoid(ga_ref[...])
        sr = _sigmoid(gr_ref[...])
        d_att = (dm * sa).astype(BF16)
        d_rec = (dm * sr).astype(BF16)
        da_ref[...] = d_att
        dr_ref[...] = d_rec
        d3_ref[1] = (dm * att_ref[...] * (sa * (1.0 - sa))).astype(BF16)
        d3_ref[2] = (dm * rec_ref[...] * (sr * (1.0 - sr))).astype(BF16)
        dao_ref[...] = _dot_nt(d_att, wa_ref[...]).astype(BF16)
        d_hg = _dot_nt(d_rec, wr_ref[...])
        yr = yr_ref[...]
        dh_ref[...] = d_hg * _gelu(yr)
        d3_ref[0] = (d_hg * h_ref[...] * _gelu_grad(yr)).astype(BF16)

    row = lambda i: (i, 0)
    full = lambda r: pl.BlockSpec((r, D), lambda i: (0, 0))
    return pl.pallas_call(
        body, name=name, grid=(S // tm,),
        in_specs=[pl.BlockSpec((tm, D), row)] * 4
        + [pl.BlockSpec((tm, D), lambda i: (i, 1)), pl.BlockSpec((tm, D), lambda i: (i, 2)),
           pl.BlockSpec((tm, D), lambda i: (i, 3)), pl.BlockSpec((tm, D), row),
           full(8), full(512), full(D), full(D)],
        out_specs=[pl.BlockSpec((tm, D), row)] * 3
        + [pl.BlockSpec((tm, 512), row), pl.BlockSpec((tm, D), row),
           pl.BlockSpec((3, tm, D), lambda i: (0, i, 0)), pl.BlockSpec((8, D), lambda i: (0, 0))],
        out_shape=[jax.ShapeDtypeStruct((S, D), BF16)] * 3
        + [jax.ShapeDtypeStruct((S, 512), BF16), jax.ShapeDtypeStruct((S, D), F32),
           jax.ShapeDtypeStruct((3, S, D), BF16), jax.ShapeDtypeStruct((8, D), F32)],
        compiler_params=_cp("arbitrary"),
    )(dxo, f, att, rec, rest, rest, rest, h, vec, w_att_o, w_rec_o, w_out)


def loss_grad(y, tgt, name, tm=512):
    S = y.shape[0]
    tm = min(tm, S)
    nt = S // tm

    def body(y_ref, t_ref, dy_ref, l_ref, acc):
        i = pl.program_id(0)

        @pl.when(i == 0)
        def _():
            acc[...] = jnp.zeros_like(acc)

        d = y_ref[...] - t_ref[...]
        dy_ref[...] = d * (1.0 / D)
        acc[...] += _colsum(d * d)

        @pl.when(i == nt - 1)
        def _():
            l_ref[...] = jnp.broadcast_to(0.5 * jnp.sum(acc[...]) * (1.0 / D), (8, 128))

    return pl.pallas_call(
        body, name=name, grid=(nt,),
        in_specs=[pl.BlockSpec((tm, D), lambda i: (i, 0))] * 2,
        out_specs=[pl.BlockSpec((tm, D), lambda i: (i, 0)), pl.BlockSpec((8, 128), lambda i: (0, 0))],
        out_shape=[jax.ShapeDtypeStruct((S, D), F32), jax.ShapeDtypeStruct((8, 128), F32)],
        scratch_shapes=[pltpu.VMEM((1, D), F32)],
        compiler_params=_cp("arbitrary"),
    )(y, tgt)


def ada_fwd(c_all, w_ada, b_ada, name, tn=768):
    n = w_ada.shape[1]

    def body(c_ref, w_ref, b_ref, o_ref):
        cv = c_ref[...]
        ca = (cv * _sigmoid(cv)).astype(BF16)
        o_ref[...] = _dot(ca, w_ref[...].astype(BF16)) + b_ref[...]

    return pl.pallas_call(
        body, name=name, grid=(n // tn,),
        in_specs=[pl.BlockSpec((8, D), lambda j: (0, 0)), pl.BlockSpec((D, tn), lambda j: (0, j)),
                  pl.BlockSpec((1, tn), lambda j: (0, j))],
        out_specs=pl.BlockSpec((8, tn), lambda j: (0, j)),
        out_shape=jax.ShapeDtypeStruct((8, n), F32),
        compiler_params=_cp("parallel"),
    )(c_all, w_ada, b_ada)


def ada_bwd(c_all_t, dmod, name, tn=768):
    n = dmod.shape[1]

    def body(c_ref, d_ref, o_ref):
        cv = c_ref[...]
        ca = (cv * _sigmoid(cv)).astype(BF16)
        o_ref[...] = _dot(ca, d_ref[...].astype(BF16))

    return pl.pallas_call(
        body, name=name, grid=(n // tn,),
        in_specs=[pl.BlockSpec((D, 128), lambda j: (0, 0)), pl.BlockSpec((128, tn), lambda j: (0, j))],
        out_specs=pl.BlockSpec((D, tn), lambda j: (0, j)),
        out_shape=jax.ShapeDtypeStruct((D, n), F32),
        compiler_params=_cp("parallel"),
    )(c_all_t, dmod)


def _row_tile(rows, cols, itemsize=4, budget=1536 * 1024):
    best = None
    for t in range(8, rows + 1, 8):
        if rows % t == 0 and t * cols * itemsize <= budget:
            best = t
    return rows if best is None else best


def sum_lead(parts, name, out_dtype=F32):
    n, R, C = parts.shape
    tr = _row_tile(R, C * n)

    def body(p_ref, o_ref):
        acc = p_ref[0].astype(F32)
        for k in range(1, n):
            acc = acc + p_ref[k].astype(F32)
        o_ref[...] = acc.astype(out_dtype)

    return pl.pallas_call(
        body, name=name, grid=(R // tr,),
        in_specs=[pl.BlockSpec((n, tr, C), lambda i: (0, i, 0))],
        out_specs=pl.BlockSpec((tr, C), lambda i: (i, 0)),
        out_shape=jax.ShapeDtypeStruct((R, C), out_dtype),
        compiler_params=_cp("parallel"),
    )(parts)


def adamw(w, g, m, v, name):
    R, C = w.shape
    tr = _row_tile(R, C * 7, budget=8 * 1024 * 1024)

    def body(w_ref, g_ref, m_ref, v_ref, d_ref, mo_ref, vo_ref):
        gv = g_ref[...]
        mn = ADAM_B1 * m_ref[...] + (1.0 - ADAM_B1) * gv
        vn = ADAM_B2 * v_ref[...] + (1.0 - ADAM_B2) * (gv * gv)
        m_hat = mn / (1.0 - ADAM_B1 ** ADAM_STEP)
        v_hat = vn / (1.0 - ADAM_B2 ** ADAM_STEP)
        d_ref[...] = -ADAM_LR * (m_hat / (jnp.sqrt(v_hat) + ADAM_EPS) + ADAM_WD * w_ref[...])
        mo_ref[...] = mn
        vo_ref[...] = vn

    spec = pl.BlockSpec((tr, C), lambda i: (i, 0))
    return pl.pallas_call(
        body, name=name, grid=(R // tr,),
        in_specs=[spec] * 4, out_specs=[spec] * 3,
        out_shape=[jax.ShapeDtypeStruct((R, C), F32)] * 3,
        compiler_params=_cp("parallel"),
    )(w, g, m, v)


def _mesh_pos():
    return lax.axis_index("x"), lax.axis_index("y"), lax.axis_index("c")


def _other_chips(mx, my):
    return [(1 - mx, my), (mx, 1 - my), (1 - mx, 1 - my)]


def ag_small(x, name):
    R = x.shape[0]

    def body(x_ref, out_ref, send_sems, recv_sems, local_sem):
        mx, my, mc = _mesh_pos()
        me, sibling = (mx, my, mc), (mx, my, 1 - mc)
        chips = _other_chips(mx, my)

        def slot(px, py, pc):
            return out_ref.at[4 * px + 2 * py + pc]

        def copy(k, block, to, src=None):
            return pltpu.make_async_remote_copy(
                src_ref=slot(*block) if src is None else src, dst_ref=slot(*block),
                send_sem=send_sems.at[k], recv_sem=recv_sems.at[k], device_id=to, device_id_type=MESH)

        mine = pltpu.make_async_copy(x_ref, slot(*me), local_sem)
        mine.start()
        first = [copy(0, me, sibling, src=x_ref)]
        first += [copy(1 + j, me, (*chip, mc), src=x_ref) for j, chip in enumerate(chips)]
        for cp in first:
            cp.start()
        passed = [copy(4 + j, (*chip, mc), sibling) for j, chip in enumerate(chips)]
        for j, chip in enumerate(chips):
            copy(1 + j, (*chip, mc), me).wait_recv()
            passed[j].start()
        copy(0, sibling, me).wait_recv()
        for j, chip in enumerate(chips):
            copy(4 + j, (*chip, 1 - mc), me).wait_recv()
        for cp in first + passed:
            cp.wait_send()
        mine.wait()

    return pl.pallas_call(
        body, name=name,
        out_shape=jax.ShapeDtypeStruct((N_DEV, R, 128), F32),
        in_specs=[pl.BlockSpec(memory_space=pltpu.VMEM)],
        out_specs=pl.BlockSpec(memory_space=pltpu.VMEM),
        scratch_shapes=[pltpu.SemaphoreType.DMA((7,)), pltpu.SemaphoreType.DMA((7,)), pltpu.SemaphoreType.DMA],
        compiler_params=pltpu.CompilerParams(vmem_limit_bytes=VMEM_LIMIT),
    )(x)


BIG = (("ffn1_w_gu", "col", D, PW), ("ffn1_w_down", "row", FF, D), ("w_in", "col", D, PW),
       ("w_att_o", "col", 512, D), ("w_rec_o", "row", D, D), ("w_out", "row", D, D),
       ("ffn2_w_gu", "col", D, PW), ("ffn2_w_down", "row", FF, D))
NBIG = len(BIG)


def _shard_shape(kind, R, C):
    return (R, C // 4) if kind == "col" else (R // 4, C)


def _region(ref, kind, R, C, q, half, t, tr):
    sr, sc = _shard_shape(kind, R, C)
    if kind == "col":
        return ref.at[pl.ds(pl.multiple_of(half * (R // 2) + t * tr, 16), tr), pl.ds(q * sc, sc)]
    return ref.at[pl.ds(pl.multiple_of(q * sr + t * tr, 16), tr), pl.ds(half * (C // 2), C // 2)]


def ag_push(w, kind, R, C, c_arr, name):
    sr, sc = _shard_shape(kind, R, C)
    hr, hc = (sr // 2, sc) if kind == "col" else (sr, sc // 2)
    tr = _row_tile(hr, hc, itemsize=2, budget=512 * 1024)
    nt = hr // tr

    def body(c_ref, mine_ref, other_ref, full_ref, stage, lsem, ssem, rsem):
        i = pl.program_id(0)
        par = i % 2
        mx, my, mc = _mesh_pos()
        p = 2 * mx + my
        chips = _other_chips(mx, my)

        def copies(s, q, h, t):
            mine = _region(full_ref, kind, R, C, q, h, t, tr)
            other = _region(full_ref, kind, R, C, q, 1 - h, t, tr)
            out = [pltpu.make_async_remote_copy(src_ref=stage.at[s, 0], dst_ref=mine, send_sem=ssem.at[s, k],
                                                recv_sem=rsem.at[k], device_id=(*chips[k], mc), device_id_type=MESH)
                   for k in range(3)]
            out.append(pltpu.make_async_copy(stage.at[s, 0], mine, lsem.at[s, 0]))
            out.append(pltpu.make_async_copy(stage.at[s, 1], other, lsem.at[s, 1]))
            return out

        def wait_sent(s):
            cps = copies(s, 0, 0, 0)
            for cp in cps[:3]:
                cp.wait_send()
            for cp in cps[3:]:
                cp.wait()

        @pl.when(i >= 2)
        def _():
            wait_sent(par)

        stage[par, 0] = mine_ref[...].astype(BF16)
        stage[par, 1] = other_ref[...].astype(BF16)
        if kind == "col":
            for q in range(4):
                @pl.when(p == q)
                def _(q=q):
                    for cp in copies(par, q, mc, i):
                        cp.start()
        else:
            for h in range(2):
                @pl.when(mc == h)
                def _(h=h):
                    for cp in copies(par, p, h, i):
                        cp.start()

        @pl.when(i == nt - 1)
        def _():
            for s in range(min(nt, 2)):
                wait_sent(s)
            for k in range(3):
                half = full_ref.at[pl.ds(0, hr), pl.ds(0, hc)]
                pltpu.make_async_remote_copy(src_ref=half, dst_ref=half, send_sem=ssem.at[0, k], recv_sem=rsem.at[k],
                                             device_id=(*chips[k], mc), device_id_type=MESH).wait_recv()

    if kind == "col":
        mine_spec = pl.BlockSpec((tr, hc), lambda i, c: (c[0] * nt + i, 0))
        other_spec = pl.BlockSpec((tr, hc), lambda i, c: ((1 - c[0]) * nt + i, 0))
    else:
        mine_spec = pl.BlockSpec((tr, hc), lambda i, c: (i, c[0]))
        other_spec = pl.BlockSpec((tr, hc), lambda i, c: (i, 1 - c[0]))
    return pl.pallas_call(
        body, name=name,
        grid_spec=pltpu.PrefetchScalarGridSpec(
            num_scalar_prefetch=1, grid=(nt,), in_specs=[mine_spec, other_spec], out_specs=ANY,
            scratch_shapes=[pltpu.VMEM((2, 2, tr, hc), BF16), pltpu.SemaphoreType.DMA((2, 2)),
                            pltpu.SemaphoreType.DMA((2, 3)), pltpu.SemaphoreType.DMA((3,))]),
        out_shape=jax.ShapeDtypeStruct((R, C), BF16),
        compiler_params=_cp("arbitrary"),
    )(c_arr, w, w)


def ag_local(w, kind, R, C, p_arr, name):
    sr, sc = _shard_shape(kind, R, C)
    tr = _row_tile(sr, sc, budget=2 * 1024 * 1024)
    nt = sr // tr

    def body(p_ref, w_ref, o_ref):
        o_ref[...] = w_ref[...].astype(BF16)

    if kind == "col":
        o_spec = pl.BlockSpec((tr, sc), lambda i, p: (i, p[0]))
    else:
        o_spec = pl.BlockSpec((tr, sc), lambda i, p: (p[0] * nt + i, 0))
    return pl.pallas_call(
        body, name=name,
        grid_spec=pltpu.PrefetchScalarGridSpec(
            num_scalar_prefetch=1, grid=(nt,), in_specs=[pl.BlockSpec((tr, sc), lambda i, p: (i, 0))],
            out_specs=o_spec),
        out_shape=jax.ShapeDtypeStruct((R, C), BF16),
        compiler_params=_cp("parallel"),
    )(p_arr, w)


HBM_SPEC = pl.BlockSpec(memory_space=pltpu.HBM)
SEM_SPEC = pl.BlockSpec(memory_space=pltpu.SEMAPHORE)


def _ag_copies(fulls, geoms, ssem, rsem, mx, my, mc, q, h):
    chips = _other_chips(mx, my)
    out = []
    for w, (kind, R, C) in enumerate(geoms):
        sr, sc = _shard_shape(kind, R, C)
        hr = sr // 2 if kind == "col" else sr
        reg = _region(fulls[w], kind, R, C, q, h, 0, hr)
        out += [pltpu.make_async_remote_copy(src_ref=reg, dst_ref=reg, send_sem=ssem.at[3 * w + k],
                                             recv_sem=rsem.at[3 * w + k], device_id=(*chips[k], mc),
                                             device_id_type=MESH) for k in range(3)]
    return out


def ag_start(fulls, geoms, after, name):
    n = len(fulls)
    after = list(after)
    m = len(after)

    def body(*refs):
        ssem, rsem = refs[n + m:n + m + 2]
        outs, token = refs[n + m + 2:2 * n + m + 2], refs[2 * n + m + 2]
        mx, my, mc = _mesh_pos()
        p = 2 * mx + my
        col = [w for w, g in enumerate(geoms) if g[0] == "col"]
        row = [w for w, g in enumerate(geoms) if g[0] == "row"]
        for q in range(4):
            @pl.when(p == q)
            def _(q=q):
                cps = _ag_copies(outs, geoms, ssem, rsem, mx, my, mc, q, mc)
                for w in col:
                    for cp in cps[3 * w:3 * w + 3]:
                        cp.start()
        for h in range(2):
            @pl.when(mc == h)
            def _(h=h):
                cps = _ag_copies(outs, geoms, ssem, rsem, mx, my, mc, p, h)
                for w in row:
                    for cp in cps[3 * w:3 * w + 3]:
                        cp.start()
        token[...] = jnp.zeros_like(token)

    res = pl.pallas_call(
        body, name=name,
        out_shape=[pltpu.SemaphoreType.DMA((3 * n,)), pltpu.SemaphoreType.DMA((3 * n,))]
        + [pltpu.HBM(a.shape, a.dtype) for a in fulls] + [jax.ShapeDtypeStruct((8, 128), F32)],
        in_specs=[HBM_SPEC] * n + [ANY] * m,
        out_specs=[SEM_SPEC, SEM_SPEC] + [HBM_SPEC] * n + [pl.BlockSpec(memory_space=pltpu.VMEM)],
        input_output_aliases={w: 2 + w for w in range(n)},
        compiler_params=pltpu.CompilerParams(has_side_effects=pltpu.SideEffectType.DATAFLOW_SIDE_EFFECTING),
    )(*[pltpu.with_memory_space_constraint(a, pltpu.HBM) for a in fulls], *after)
    return res[0], res[1], list(res[2:2 + n]), res[2 + n]


def ag_wait(fulls, geoms, ssem, rsem, after, name):
    n = len(fulls)

    def body(*refs):
        ins, ssem_ref, rsem_ref = refs[:n], refs[n], refs[n + 1]
        mx, my, mc = _mesh_pos()
        for cp in _ag_copies(ins, geoms, ssem_ref, rsem_ref, mx, my, mc, 0, 0):
            cp.wait_send()
            cp.wait_recv()

    return list(pl.pallas_call(
        body, name=name,
        out_shape=[pltpu.HBM(a.shape, a.dtype) for a in fulls],
        in_specs=[HBM_SPEC] * n + [SEM_SPEC, SEM_SPEC, ANY],
        out_specs=[HBM_SPEC] * n,
        input_output_aliases={w: w for w in range(n)},
        compiler_params=pltpu.CompilerParams(has_side_effects=pltpu.SideEffectType.DATAFLOW_SIDE_EFFECTING),
    )(*fulls, ssem, rsem, after))


def ag_forward(full, kind, R, C, name):
    sr, sc = _shard_shape(kind, R, C)
    hr, hc = (sr // 2, sc) if kind == "col" else (sr, sc // 2)
    tr = _row_tile(hr, hc, itemsize=2, budget=512 * 1024)
    nt = hr // tr

    def body(src_ref, full_ref, stage, lsem, ssem, rsem):
        k, i = pl.program_id(0), pl.program_id(1)
        step = k * nt + i
        par = step % 2
        mx, my, mc = _mesh_pos()
        q_k = _partner_chip(k, 2 * mx + my)

        def push(s, dst):
            return pltpu.make_async_remote_copy(src_ref=stage.at[s], dst_ref=dst, send_sem=ssem.at[s], recv_sem=rsem,
                                                device_id=(mx, my, 1 - mc), device_id_type=MESH)

        @pl.when(step >= 2)
        def _():
            push(par, _region(full_ref, kind, R, C, 0, 0, 0, tr)).wait_send()

        def move(q, h):
            load = pltpu.make_async_copy(_region(src_ref, kind, R, C, q, h, i, tr), stage.at[par], lsem)
            load.start()
            load.wait()
            push(par, _region(full_ref, kind, R, C, q, h, i, tr)).start()

        if kind == "col":
            for q in range(4):
                @pl.when(q_k == q)
                def _(q=q):
                    move(q, mc)
        else:
            for h in range(2):
                @pl.when(mc == h)
                def _(h=h):
                    move(q_k, h)

        @pl.when(step == 3 * nt - 1)
        def _():
            for s in range(2):
                push(s, _region(full_ref, kind, R, C, 0, 0, 0, tr)).wait_send()
            three = full_ref.at[pl.ds(0, hr), pl.ds(0, 3 * hc)] if kind == "col" else full_ref.at[pl.ds(0, 3 * hr), pl.ds(0, hc)]
            pltpu.make_async_remote_copy(src_ref=three, dst_ref=three, send_sem=ssem.at[0], recv_sem=rsem,
                                         device_id=(mx, my, 1 - mc), device_id_type=MESH).wait_recv()

    return pl.pallas_call(
        body, name=name, grid=(3, nt),
        in_specs=[ANY], out_specs=ANY,
        out_shape=jax.ShapeDtypeStruct((R, C), BF16),
        scratch_shapes=[pltpu.VMEM((2, tr, hc), BF16), pltpu.SemaphoreType.DMA, pltpu.SemaphoreType.DMA((2,)),
                        pltpu.SemaphoreType.DMA],
        input_output_aliases={0: 0},
        compiler_params=_cp("arbitrary", "arbitrary"),
    )(full)


def _half_shape(kind, R, C):
    return (R // 2, C) if kind == "col" else (R, C // 2)


def _piece_shape(kind, R, C):
    return (R // 2, C // 4) if kind == "col" else (R // 4, C // 2)


def pair_push(g, kind, c_arr, name):
    R, C = g.shape
    hr, hc = _half_shape(kind, R, C)
    tr = _row_tile(hr, hc, itemsize=2, budget=1024 * 1024)
    nt = hr // tr

    def body(c_ref, g_ref, out_ref, stage, ssem, rsem):
        i = pl.program_id(0)
        slot = i % 2
        mx, my, mc = _mesh_pos()

        def push(s, t):
            return pltpu.make_async_remote_copy(
                src_ref=stage.at[s], dst_ref=out_ref.at[pl.ds(pl.multiple_of(t * tr, 16), tr)],
                send_sem=ssem.at[s], recv_sem=rsem, device_id=(mx, my, 1 - mc), device_id_type=MESH)

        @pl.when(i >= 2)
        def _():
            push(slot, 0).wait_send()

        stage[slot] = g_ref[...]
        push(slot, i).start()

        @pl.when(i == nt - 1)
        def _():
            push(slot, 0).wait_send()
            if nt >= 2:
                push(1 - slot, 0).wait_send()
            pltpu.make_async_remote_copy(src_ref=out_ref, dst_ref=out_ref, send_sem=ssem.at[0], recv_sem=rsem,
                                         device_id=(mx, my, 1 - mc), device_id_type=MESH).wait_recv()

    if kind == "col":
        g_spec = pl.BlockSpec((tr, hc), lambda i, c: ((1 - c[0]) * nt + i, 0))
    else:
        g_spec = pl.BlockSpec((tr, hc), lambda i, c: (i, 1 - c[0]))
    return pl.pallas_call(
        body, name=name,
        grid_spec=pltpu.PrefetchScalarGridSpec(
            num_scalar_prefetch=1, grid=(nt,), in_specs=[g_spec], out_specs=ANY,
            scratch_shapes=[pltpu.VMEM((2, tr, hc), BF16), pltpu.SemaphoreType.DMA((2,)), pltpu.SemaphoreType.DMA]),
        out_shape=jax.ShapeDtypeStruct((hr, hc), BF16),
        compiler_params=_cp("arbitrary"),
    )(c_arr, g)


def _partner_chip(k, p):
    return p ^ jnp.where(k == 0, 2, jnp.where(k == 1, 1, jnp.where(k == 2, 3, 0)))


def pair_add(g, got, kind, cp_arr, name):
    R, C = g.shape
    pr, pc = _piece_shape(kind, R, C)
    tr = _row_tile(pr, pc, itemsize=2, budget=1024 * 1024)
    nt = pr // tr

    def body(cp_ref, g_ref, got_ref, ps_ref, rb_ref):
        tile = (g_ref[...].astype(F32) + got_ref[...].astype(F32)).astype(BF16)
        ps_ref[...] = tile

        @pl.when(pl.program_id(1) == cp_ref[1])
        def _():
            rb_ref[...] = tile

    if kind == "col":
        g_spec = pl.BlockSpec((tr, pc), lambda i, q, cp: (cp[0] * nt + i, q))
        got_spec = pl.BlockSpec((tr, pc), lambda i, q, cp: (i, q))
    else:
        g_spec = pl.BlockSpec((tr, pc), lambda i, q, cp: (q * nt + i, cp[0]))
        got_spec = pl.BlockSpec((tr, pc), lambda i, q, cp: (q * nt + i, 0))
    return pl.pallas_call(
        body, name=name,
        grid_spec=pltpu.PrefetchScalarGridSpec(
            num_scalar_prefetch=1, grid=(nt, 4), in_specs=[g_spec, got_spec],
            out_specs=[pl.BlockSpec((None, tr, pc), lambda i, q, cp: (q, i, 0)),
                       pl.BlockSpec((None, tr, pc), lambda i, q, cp: (cp[1], i, 0))]),
        out_shape=[jax.ShapeDtypeStruct((4, pr, pc), BF16)] * 2,
        compiler_params=_cp("arbitrary", "arbitrary"),
    )(cp_arr, g, got)


def _rs_copies(ps, rb, ssem, rsem, mx, my, mc):
    p = 2 * mx + my
    out = []
    for w in range(len(ps)):
        for k, chip in enumerate(_other_chips(mx, my)):
            out.append(pltpu.make_async_remote_copy(
                src_ref=ps[w].at[2 * chip[0] + chip[1]], dst_ref=rb[w].at[p], send_sem=ssem.at[3 * w + k],
                recv_sem=rsem.at[3 * w + k], device_id=(*chip, mc), device_id_type=MESH))
    return out


def rs_start(ps, rb, after, name):
    n = len(ps)
    after = list(after)
    m = len(after)

    def body(*refs):
        ssem, rsem = refs[2 * n + m:2 * n + m + 2]
        ps_o = refs[2 * n + m + 2:3 * n + m + 2]
        rb_o = refs[3 * n + m + 2:4 * n + m + 2]
        token = refs[4 * n + m + 2]
        for cp in _rs_copies(ps_o, rb_o, ssem, rsem, *_mesh_pos()):
            cp.start()
        token[...] = jnp.zeros_like(token)

    both = list(ps) + list(rb)
    res = pl.pallas_call(
        body, name=name,
        out_shape=[pltpu.SemaphoreType.DMA((3 * n,)), pltpu.SemaphoreType.DMA((3 * n,))]
        + [pltpu.HBM(a.shape, a.dtype) for a in both] + [jax.ShapeDtypeStruct((8, 128), F32)],
        in_specs=[HBM_SPEC] * (2 * n) + [ANY] * m,
        out_specs=[SEM_SPEC, SEM_SPEC] + [HBM_SPEC] * (2 * n) + [pl.BlockSpec(memory_space=pltpu.VMEM)],
        input_output_aliases={w: 2 + w for w in range(2 * n)},
        compiler_params=pltpu.CompilerParams(has_side_effects=pltpu.SideEffectType.DATAFLOW_SIDE_EFFECTING),
    )(*[pltpu.with_memory_space_constraint(a, pltpu.HBM) for a in both], *after)
    return res[0], res[1], list(res[2:2 + n]), list(res[2 + n:2 + 2 * n]), res[2 + 2 * n]


def rs_wait(ps, rb, ssem, rsem, after, name):
    n = len(ps)
    after = list(after)
    m = len(after)

    def body(*refs):
        ps_i, rb_i = refs[:n], refs[n:2 * n]
        ssem_ref, rsem_ref = refs[2 * n], refs[2 * n + 1]
        for cp in _rs_copies(ps_i, rb_i, ssem_ref, rsem_ref, *_mesh_pos()):
            cp.wait_send()
            cp.wait_recv()

    both = list(ps) + list(rb)
    res = pl.pallas_call(
        body, name=name,
        out_shape=[pltpu.HBM(a.shape, a.dtype) for a in both],
        in_specs=[HBM_SPEC] * (2 * n) + [SEM_SPEC, SEM_SPEC] + [ANY] * m,
        out_specs=[HBM_SPEC] * (2 * n),
        input_output_aliases={w: w for w in range(2 * n)},
        compiler_params=pltpu.CompilerParams(has_side_effects=pltpu.SideEffectType.DATAFLOW_SIDE_EFFECTING),
    )(*both, ssem, rsem, *after)
    return list(res[n:])


def sum_share(parts, kind, R, C, name):
    _, pr, pc = parts.shape
    sr, sc = _shard_shape(kind, R, C)
    tr = _row_tile(pr, pc * 4, budget=4 * 1024 * 1024)
    nt = pr // tr

    def body(p_ref, fin_ref, stage, lsem, ssem, rsem):
        i = pl.program_id(0)
        slot = i % 2
        mx, my, mc = _mesh_pos()

        def region(h, t):
            r0 = pl.multiple_of(t * tr, 8)
            if kind == "col":
                return fin_ref.at[pl.ds(pl.multiple_of(h * pr + r0, 8), tr)]
            return fin_ref.at[pl.ds(r0, tr), pl.ds(h * pc, pc)]

        def copies(s, h, t):
            return (pltpu.make_async_copy(stage.at[s], region(h, t), lsem.at[s]),
                    pltpu.make_async_remote_copy(src_ref=stage.at[s], dst_ref=region(h, t), send_sem=ssem.at[s],
                                                 recv_sem=rsem, device_id=(mx, my, 1 - mc), device_id_type=MESH))

        def wait_sent(s):
            loc, rem = copies(s, 0, 0)
            loc.wait()
            rem.wait_send()

        @pl.when(i >= 2)
        def _():
            wait_sent(slot)

        acc = p_ref[0].astype(F32)
        for k in range(1, 4):
            acc = acc + p_ref[k].astype(F32)
        stage[slot] = acc
        if kind == "col":
            for cp in copies(slot, mc, i):
                cp.start()
        else:
            for h in range(2):
                @pl.when(mc == h)
                def _(h=h):
                    for cp in copies(slot, h, i):
                        cp.start()

        @pl.when(i == nt - 1)
        def _():
            wait_sent(slot)
            if nt >= 2:
                wait_sent(1 - slot)
            half = fin_ref.at[pl.ds(0, pr), pl.ds(0, pc)]
            pltpu.make_async_remote_copy(src_ref=half, dst_ref=half, send_sem=ssem.at[0], recv_sem=rsem,
                                         device_id=(mx, my, 1 - mc), device_id_type=MESH).wait_recv()

    return pl.pallas_call(
        body, name=name, grid=(nt,),
        in_specs=[pl.BlockSpec((4, tr, pc), lambda i: (0, i, 0))],
        out_specs=ANY,
        out_shape=jax.ShapeDtypeStruct((sr, sc), F32),
        scratch_shapes=[pltpu.VMEM((2, tr, pc), F32), pltpu.SemaphoreType.DMA((2,)), pltpu.SemaphoreType.DMA((2,)),
                        pltpu.SemaphoreType.DMA],
        compiler_params=_cp("arbitrary"),
    )(parts)


def _pack(parts, rows):
    flat = []
    for a in parts:
        a = jnp.ravel(a).astype(F32)
        flat.append(jnp.pad(a, (0, (-a.shape[0]) % 128)))
    v = jnp.concatenate(flat)
    return jnp.pad(v, (0, rows * 128 - v.shape[0])).reshape(rows, 128)


def _unpack(block, shapes):
    lead = block.shape[:-2]
    v = block.reshape(lead + (-1,))
    out, off = [], 0
    for shp in shapes:
        n = int(np.prod(shp))
        out.append(v[..., off:off + n].reshape(lead + tuple(shp)))
        off += n + (-n) % 128
    return out


def _block_diag4(w):
    w4 = w.reshape(4, 4, 64, 64)
    eye = jnp.eye(4, dtype=w.dtype)
    return (w4[:, :, :, None, :] * eye[None, :, None, :, None]).reshape(4, 256, 256)


def _diag_blocks(bd):
    b5 = bd.reshape(4, 4, 64, 4, 64)
    return jnp.stack([b5[:, i, :, i, :] for i in range(4)], axis=1).reshape(16, 64, 64)


def _bias_window(rel_bias):
    m = np.arange(767)
    tv = rel_bias[:, np.clip(639 - m, -128, 128) + 128]
    win = jnp.stack([tv[:, 127 - r:127 - r + WIN] for r in range(128)], axis=1)
    qh = np.arange(128)[:, None] // CHUNK
    kc = np.arange(WIN)[None, :] // CHUNK
    valid = (kc >= qh) & (kc <= qh + 8)
    return jnp.where(jnp.asarray(valid)[None], win, NEG)


SMALL = ("b_ada", "norm_pre", "norm_post", "rel_bias", "conv_w", "conv_b", "lru_wa", "lru_ba", "lru_wx",
         "lru_bx", "lru_lambda")
WEIGHTS = ("w_ada", "b_ada", "norm_pre", "norm_post", "ffn1_w_gu", "ffn1_w_down", "w_in", "rel_bias", "conv_w",
           "conv_b", "lru_wa", "lru_ba", "lru_wx", "lru_bx", "lru_lambda", "w_att_o", "w_rec_o", "w_out",
           "ffn2_w_gu", "ffn2_w_down")


def kernel(x, c, w_ada, b_ada, norm_pre, norm_post, ffn1_w_gu, ffn1_w_down, w_in, rel_bias, conv_w, conv_b, lru_wa, lru_ba, lru_wx, lru_bx, lru_lambda, w_att_o, w_rec_o, w_out, ffn2_w_gu, ffn2_w_down, loss_target, m_w_ada, m_b_ada, m_norm_pre, m_norm_post, m_ffn1_w_gu, m_ffn1_w_down, m_w_in, m_rel_bias, m_conv_w, m_conv_b, m_lru_wa, m_lru_ba, m_lru_wx, m_lru_bx, m_lru_lambda, m_w_att_o, m_w_rec_o, m_w_out, m_ffn2_w_gu, m_ffn2_w_down, v_w_ada, v_b_ada, v_norm_pre, v_norm_post, v_ffn1_w_gu, v_ffn1_w_down, v_w_in, v_rel_bias, v_conv_w, v_conv_b, v_lru_wa, v_lru_ba, v_lru_wx, v_lru_bx, v_lru_lambda, v_w_att_o, v_w_rec_o, v_w_out, v_ffn2_w_gu, v_ffn2_w_down):
    W = dict(w_ada=w_ada, b_ada=b_ada, norm_pre=norm_pre, norm_post=norm_post, ffn1_w_gu=ffn1_w_gu,
             ffn1_w_down=ffn1_w_down, w_in=w_in, rel_bias=rel_bias, conv_w=conv_w, conv_b=conv_b, lru_wa=lru_wa,
             lru_ba=lru_ba, lru_wx=lru_wx, lru_bx=lru_bx, lru_lambda=lru_lambda, w_att_o=w_att_o, w_rec_o=w_rec_o,
             w_out=w_out, ffn2_w_gu=ffn2_w_gu, ffn2_w_down=ffn2_w_down)
    M = dict(w_ada=m_w_ada, b_ada=m_b_ada, norm_pre=m_norm_pre, norm_post=m_norm_post, ffn1_w_gu=m_ffn1_w_gu,
             ffn1_w_down=m_ffn1_w_down, w_in=m_w_in, rel_bias=m_rel_bias, conv_w=m_conv_w, conv_b=m_conv_b,
             lru_wa=m_lru_wa, lru_ba=m_lru_ba, lru_wx=m_lru_wx, lru_bx=m_lru_bx, lru_lambda=m_lru_lambda,
             w_att_o=m_w_att_o, w_rec_o=m_w_rec_o, w_out=m_w_out, ffn2_w_gu=m_ffn2_w_gu, ffn2_w_down=m_ffn2_w_down)
    V = dict(w_ada=v_w_ada, b_ada=v_b_ada, norm_pre=v_norm_pre, norm_post=v_norm_post, ffn1_w_gu=v_ffn1_w_gu,
             ffn1_w_down=v_ffn1_w_down, w_in=v_w_in, rel_bias=v_rel_bias, conv_w=v_conv_w, conv_b=v_conv_b,
             lru_wa=v_lru_wa, lru_ba=v_lru_ba, lru_wx=v_lru_wx, lru_bx=v_lru_bx, lru_lambda=v_lru_lambda,
             w_att_o=v_w_att_o, w_rec_o=v_w_rec_o, w_out=v_w_out, ffn2_w_gu=v_ffn2_w_gu, ffn2_w_down=v_ffn2_w_down)
    mx, my, mc = _mesh_pos()
    p = 2 * mx + my
    e = 4 * mx + 2 * my + mc
    xs = x[0]

    g1 = ag_small(_pack([c, norm_pre, norm_post, conv_w], 32), "ag_small_params")
    c_all, npre4, npost4, cw4 = _unpack(g1, [(D,), (3, 256), (3, 256), (4, 256)])
    chipwise = lambda a: jnp.moveaxis(a[0::2], 0, 1).reshape(a.shape[1], D)
    npre, npost, conv_full = chipwise(npre4), chipwise(npost4), chipwise(cw4)

    b_cols = lax.dynamic_slice(b_ada, (0, p * 2304), (1, 2304))
    mod_cols = ada_fwd(c_all, w_ada[0], b_cols, "ada_fwd")
    g2 = ag_small(mod_cols.reshape(144, 128), "ag_mod")
    mod_all = jnp.moveaxis(g2[0::2].reshape(4, 8, 2304), 0, 1).reshape(8, 9 * D)
    mod = lax.dynamic_index_in_dim(mod_all, e, 0, keepdims=False).reshape(3, 3, D)
    zeros3 = jnp.zeros((3, D), F32)
    vecs = [jnp.concatenate([npre[k:k + 1], npost[k:k + 1], mod[k], zeros3], axis=0) for k in range(3)]

    c_arr = jnp.reshape(mc, (1,)).astype(jnp.int32)
    cp_arr = jnp.stack([mc, p]).astype(jnp.int32)
    p_arr = jnp.reshape(p, (1,)).astype(jnp.int32)
    later = BIG[2:]
    geoms = [(kind, R, C) for (_, kind, R, C) in later]
    placed = [ag_local(W[n][0], kind, R, C, p_arr, "ag_local_" + n) for (n, kind, R, C) in later]
    f1_gu, f1_dn = [ag_forward(ag_push(W[n][0], kind, R, C, c_arr, "ag_push_" + n), kind, R, C, "ag_forward_" + n)
                    for (n, kind, R, C) in BIG[:2]]
    mix_s, mix_r, mix_fly, tok1 = ag_start(placed[:4], geoms[:4], [f1_gu, f1_dn], "ag_start_mixer")
    ffn_s, ffn_r, ffn_fly, tok2 = ag_start(placed[4:], geoms[4:], [tok1], "ag_start_ffn2")
    wa_bd = _block_diag4(lru_wa[0]).astype(BF16)
    wx_bd = _block_diag4(lru_wx[0]).astype(BF16)
    pvec = jnp.concatenate([conv_full, conv_b, lru_ba, lru_bx, lru_lambda], axis=0)
    bias = _bias_window(rel_bias[0]).reshape(4, 256, WIN)

    def arrived(fly, gm, names, ssem, rsem, after, tag):
        done = ag_wait(fly, gm, ssem, rsem, after, "ag_wait_" + tag)
        return [ag_forward(a, kind, R, C, "ag_forward_" + n) for a, (kind, R, C), n in zip(done, gm, names)]

    x1, h1, g1_, u1, a1, f1 = ffn_fwd(xs, vecs[0] + tok2[0:1, 0:1], f1_gu, f1_dn, 0.5, "ffn1_fwd")
    win, wao, wro, wout = arrived(mix_fly, geoms[:4], [b[0] for b in later[:4]], mix_s, mix_r, x1, "mixer")
    h2, qkv, rest = proj_fwd(x1, vecs[1], win, "proj_fwd")
    ao = attn_fwd(qkv, bias, "attn_fwd")
    hl, hg = lru_fwd(rest, pvec, wa_bd, wx_bd, "lru_fwd")
    x2, att, rec, mg, f2 = mix_out_fwd(x1, ao, hg, rest, vecs[1], wao, wro, wout, "mix_out_fwd")
    f2_gu, f2_dn = arrived(ffn_fly, geoms[4:], [b[0] for b in later[4:]], ffn_s, ffn_r, x2, "ffn2")
    x3, h3, g3_, u3, a3, f3 = ffn_fwd(x2, vecs[2], f2_gu, f2_dn, 0.5, "ffn2_fwd")
    dy, lvec = loss_grad(x3, loss_target[0], "loss_grad")
    loss = lax.psum(lvec[0, 0], ("x", "y", "c"))

    G, grads = {}, {}
    geo = {n: (kind, R, C) for (n, kind, R, C) in BIG}

    def reduce_begin(names, tag):
        ps, rb = [], []
        for n in names:
            got = pair_push(G[n], geo[n][0], c_arr, "rs_push_" + n)
            a, b = pair_add(G[n], got, geo[n][0], cp_arr, "rs_pair_sum_" + n)
            ps.append(a)
            rb.append(b)
        return rs_start(ps, rb, [], "rs_start_" + tag)

    def reduce_end(names, flight, after, tag):
        ssem, rsem, ps, rb, _ = flight
        for a, n in zip(rs_wait(ps, rb, ssem, rsem, after, "rs_wait_" + tag), names):
            grads[n] = sum_share(a, *geo[n], "rs_sum_share_" + n)[None]

    dx2, df3, dgu3, va2 = ffn_bwd(dy, x2, f3, g3_, u3, vecs[2], f2_gu, f2_dn, 0.5, "ffn2_bwd")
    G["ffn2_w_gu"] = mm_tn(h3, dgu3, "dw_ffn2_gu", D, 1408, 1024)
    G["ffn2_w_down"] = mm_tn(a3, df3, "dw_ffn2_down", 1408, D, 1024)
    fly_ffn2 = reduce_begin(("ffn2_w_gu", "ffn2_w_down"), "ffn2")
    vec1 = vecs[1] + fly_ffn2[4][0:1, 0:1]
    df2, d_att, d_rec, dao, dhl, d3, va_out = mix_out_bwd(dx2, f2, att, rec, rest, hl, vec1, wao, wro, wout,
                                                          "mix_out_bwd")
    G["w_out"] = mm_tn(mg, df2, "dw_out", D, D, 1024)
    G["w_att_o"] = mm_tn(ao, d_att, "dw_att_o", 512, D, 1024)
    G["w_rec_o"] = mm_tn(hg, d_rec, "dw_rec_o", D, D, 1024)
    dq, db, dkv = attn_bwd(qkv, dao, bias, "attn_bwd")
    dxr, v_lru, dwa_bd, dwx_bd = lru_bwd(dhl, hl, rest, pvec, wa_bd, wx_bd, "lru_bwd")
    dx1, va_in = proj_bwd(dq, dkv, dxr, d3, win, x1, dx2, vecs[1], "proj_bwd")
    gin = mm_tn(h2, dq, "dw_in_q", D, 512, 1024, n_total=PW)
    gin = mm_tn(h2, dkv, "dw_in_kv", D, 512, 1024, prev=gin, col_off=512, n_total=PW)
    gin = mm_tn(h2, dxr, "dw_in_xr", D, 512, 1024, prev=gin, col_off=1536, n_total=PW)
    G["w_in"] = mm_tn(h2, d3, "dw_in_gates", D, 512, 1024, prev=gin, col_off=2560, n_total=PW)
    fly_mix = reduce_begin(("w_in", "w_att_o", "w_rec_o", "w_out"), "mixer")
    vec0 = vecs[0] + fly_mix[4][0:1, 0:1]
    dx0, df1, dgu1, va0 = ffn_bwd(dx1, xs, f1, g1_, u1, vec0, f1_gu, f1_dn, 0.5, "ffn1_bwd")
    G["ffn1_w_gu"] = mm_tn(h1, dgu1, "dw_ffn1_gu", D, 1408, 1024)
    G["ffn1_w_down"] = mm_tn(a1, df1, "dw_ffn1_down", 1408, D, 1024)
    fly_ffn1 = reduce_begin(("ffn1_w_gu", "ffn1_w_down"), "ffn1")
    reduce_end(("ffn2_w_gu", "ffn2_w_down"), fly_ffn2, [fly_ffn1[4]], "ffn2")
    reduce_end(("w_in", "w_att_o", "w_rec_o", "w_out"), fly_mix, [fly_ffn1[4], grads["ffn2_w_down"]], "mixer")

    va1 = va_out + va_in
    vas = (va0, va1, va2)
    dmod = jnp.stack([v[2:5] for v in vas])
    part = {"b_ada": dmod, "norm_pre": jnp.stack([v[0] for v in vas]), "norm_post": jnp.stack([v[1] for v in vas]),
            "rel_bias": bias_grad(db.reshape(8, 128, WIN), "bias_grad")[:, :257], "conv_w": v_lru[0:4], "conv_b": v_lru[4],
            "lru_wa": _diag_blocks(dwa_bd), "lru_ba": v_lru[5], "lru_wx": _diag_blocks(dwx_bd), "lru_bx": v_lru[6],
            "lru_lambda": v_lru[7]}
    full_shapes = {"b_ada": (9 * D,), "norm_pre": (3, D), "norm_post": (3, D), "rel_bias": (8, 257),
                   "conv_w": (4, D), "conv_b": (D,), "lru_wa": (16, 64, 64), "lru_ba": (D,),
                   "lru_wx": (16, 64, 64), "lru_bx": (D,), "lru_lambda": (D,)}
    g3 = ag_small(_pack([part[n] for n in SMALL], 1232), "ag_small_grads")
    red = dict(zip(SMALL, _unpack(sum_lead(g3, "sum_small_grads"), [full_shapes[n] for n in SMALL])))
    cols = lambda a: lax.dynamic_slice(a, (0, p * 256), (a.shape[0], 256))
    grads.update({"b_ada": red["b_ada"][None], "norm_pre": cols(red["norm_pre"])[None],
                  "norm_post": cols(red["norm_post"])[None], "rel_bias": red["rel_bias"][None],
                  "conv_w": cols(red["conv_w"])[None], "conv_b": red["conv_b"][None], "lru_wa": red["lru_wa"][None],
                  "lru_ba": red["lru_ba"][None], "lru_wx": red["lru_wx"][None], "lru_bx": red["lru_bx"][None],
                  "lru_lambda": red["lru_lambda"][None]})

    dmod_all = g3[:, :72].reshape(8, 9 * D)
    dmod_cols = jnp.pad(lax.dynamic_slice(dmod_all, (0, p * 2304), (8, 2304)), ((0, 120), (0, 0)))
    c_all_t = jnp.pad(c_all.T, ((0, 0), (0, 120)))
    grads["w_ada"] = ada_bwd(c_all_t, dmod_cols, "ada_bwd")[None]

    delta, new_m, new_v = {}, {}, {}

    def update(n):
        shp = W[n].shape
        d_, m_, v_ = adamw(W[n][0], grads[n][0], M[n][0], V[n][0], "adamw_" + n)
        delta[n], new_m[n], new_v[n] = d_.reshape(shp), m_.reshape(shp), v_.reshape(shp)

    for n in ("w_ada", "ffn2_w_gu", "ffn2_w_down", "w_in", "w_att_o", "w_rec_o", "w_out"):
        update(n)
    packed = [_pack([src[n] for n in SMALL], 1168) for src in (W, grads, M, V)]
    outs = adamw(*packed, "adamw_small")
    for dst, blk in zip((delta, new_m, new_v), outs):
        for n, a in zip(SMALL, _unpack(blk, [W[n].shape for n in SMALL])):
            dst[n] = a
    reduce_end(("ffn1_w_gu", "ffn1_w_down"), fly_ffn1,
               [outs[0], delta["w_ada"], delta["ffn2_w_gu"], delta["ffn2_w_down"], delta["w_in"], delta["w_out"]], "ffn1")
    for n in ("ffn1_w_gu", "ffn1_w_down"):
        update(n)

    return (loss, dx0[None], *[grads[n] for n in WEIGHTS], *[delta[n] for n in WEIGHTS],
            *[new_m[n] for n in WEIGHTS], *[new_v[n] for n in WEIGHTS])
```

```python
import functools

import numpy as np
import jax
import jax.numpy as jnp
from jax import lax
from jax.experimental import pallas as pl
from jax.experimental.pallas import tpu as pltpu

F32 = jnp.float32
BF16 = jnp.bfloat16

D = 1024
FF = 2816
PW = 5632
HP = 128
CHUNK = 64
WIN = 640
TQ = 512
EPS = 1e-6
NEG = -1e30
LRU_C = 8.0
N_DEV = 8
VMEM_LIMIT = 50 * 1024 * 1024

ADAM_LR, ADAM_B1, ADAM_B2, ADAM_EPS, ADAM_WD, ADAM_STEP = 0.001, 0.9, 0.999, 1e-08, 0.01, 10

MESH = pl.DeviceIdType.MESH
ANY = pl.BlockSpec(memory_space=pl.ANY)


def _cp(*sem):
    return pltpu.CompilerParams(dimension_semantics=tuple(sem), vmem_limit_bytes=VMEM_LIMIT)


def _dot(a, b):
    return jnp.dot(a, b, preferred_element_type=F32)


def _dot_nt(a, b):
    return lax.dot_general(a, b, (((1,), (1,)), ((), ())), preferred_element_type=F32)


def _dot_tn(a, b):
    return lax.dot_general(a, b, (((0,), (0,)), ((), ())), preferred_element_type=F32)


def _mean(v):
    return jnp.mean(v, axis=-1, keepdims=True)


def _colsum(v):
    return jnp.sum(v, axis=0, keepdims=True)


def _sigmoid(v):
    return jax.nn.sigmoid(v)


def _expm1(v):
    small = v * (1.0 + v * 0.5 * (1.0 + v * (1.0 / 3.0) * (1.0 + v * 0.25 * (1.0 + v * 0.2 * (
        1.0 + v * (1.0 / 6.0) * (1.0 + v * (1.0 / 7.0)))))))
    return jnp.where(jnp.abs(v) < 0.25, small, jnp.exp(v) - 1.0)


_GK = 0.7978845608028654


def _gelu(v):
    t = jnp.tanh(_GK * (v + 0.044715 * v * v * v))
    return 0.5 * v * (1.0 + t)


def _gelu_grad(v):
    t = jnp.tanh(_GK * (v + 0.044715 * v * v * v))
    return 0.5 * (1.0 + t) + 0.5 * v * (1.0 - t * t) * _GK * (1.0 + 3.0 * 0.044715 * v * v)


def _pre_norm(xv, vec_ref):
    r = lax.rsqrt(_mean(xv * xv) + EPS)
    n = xv * r * vec_ref[0:1, :]
    return n * (1.0 + vec_ref[3:4, :]) + vec_ref[2:3, :]


def _pre_norm_bwd(dh, xv, dres, vec_ref, vacc_ref):
    r = lax.rsqrt(_mean(xv * xv) + EPS)
    xh = xv * r
    n = xh * vec_ref[0:1, :]
    vacc_ref[2:3, :] += _colsum(dh)
    vacc_ref[3:4, :] += _colsum(dh * n)
    dn = dh * (1.0 + vec_ref[3:4, :])
    vacc_ref[0:1, :] += _colsum(dn * xh)
    dxh = dn * vec_ref[0:1, :]
    return r * (dxh - xh * _mean(dxh * xh)) + dres


def _post_norm_bwd(dxo, fv, res, vec_ref, vacc_ref):
    rf = lax.rsqrt(_mean(fv * fv) + EPS)
    fh = fv * rf
    gp = vec_ref[1:2, :]
    vacc_ref[4:5, :] += _colsum(res * dxo * (fh * gp))
    dy = (res * vec_ref[4:5, :]) * dxo
    vacc_ref[1:2, :] += _colsum(dy * fh)
    dfn = dy * gp
    return rf * (dfn - fh * _mean(dfn * fh))


def ffn_fwd(x, vec, w_gu, w_dn, res, name, tm=1024, tf=256):
    S = x.shape[0]
    tm = min(tm, S)
    nf = FF // tf

    def body(x_ref, vec_ref, wg_ref, wu_ref, wd_ref, xo_ref, h_ref, g_ref, u_ref, a_ref, f_ref, hs, acc):
        j = pl.program_id(1)

        @pl.when(j == 0)
        def _():
            h = _pre_norm(x_ref[...], vec_ref).astype(BF16)
            hs[...] = h
            h_ref[...] = h
            acc[...] = jnp.zeros_like(acc)

        h = hs[...]
        g = _dot(h, wg_ref[...])
        u = _dot(h, wu_ref[...])
        g_ref[...] = g.astype(BF16)
        u_ref[...] = u.astype(BF16)
        a = (g * _sigmoid(g) * u).astype(BF16)
        a_ref[...] = a
        acc[...] += _dot(a, wd_ref[...])

        @pl.when(j == nf - 1)
        def _():
            f = acc[...]
            f_ref[...] = f
            y = f * lax.rsqrt(_mean(f * f) + EPS) * vec_ref[1:2, :]
            xo_ref[...] = x_ref[...] + (res * vec_ref[4:5, :]) * y

    row = lambda i, j: (i, 0)
    return pl.pallas_call(
        body, name=name, grid=(S // tm, nf),
        in_specs=[pl.BlockSpec((tm, D), row), pl.BlockSpec((8, D), lambda i, j: (0, 0)),
                  pl.BlockSpec((D, tf), lambda i, j: (0, j)), pl.BlockSpec((D, tf), lambda i, j: (0, j + nf)),
                  pl.BlockSpec((tf, D), lambda i, j: (j, 0))],
        out_specs=[pl.BlockSpec((tm, D), row), pl.BlockSpec((tm, D), row),
                   pl.BlockSpec((tm, tf), lambda i, j: (i, j)), pl.BlockSpec((tm, tf), lambda i, j: (i, j)),
                   pl.BlockSpec((tm, tf), lambda i, j: (i, j)), pl.BlockSpec((tm, D), row)],
        out_shape=[jax.ShapeDtypeStruct((S, D), F32), jax.ShapeDtypeStruct((S, D), BF16),
                   jax.ShapeDtypeStruct((S, FF), BF16), jax.ShapeDtypeStruct((S, FF), BF16),
                   jax.ShapeDtypeStruct((S, FF), BF16), jax.ShapeDtypeStruct((S, D), F32)],
        scratch_shapes=[pltpu.VMEM((tm, D), BF16), pltpu.VMEM((tm, D), F32)],
        compiler_params=_cp("parallel", "arbitrary"),
    )(x, vec, w_gu, w_gu, w_dn)


def ffn_bwd(dxo, x, f, g, u, vec, w_gu, w_dn, res, name, tm=1024, tf=256):
    S = x.shape[0]
    tm = min(tm, S)
    nf = FF // tf

    def body(dxo_ref, x_ref, f_ref, g_ref, u_ref, vec_ref, wg_ref, wu_ref, wd_ref,
             dx_ref, df_ref, dgu_ref, vacc_ref, dfs, acc):
        i, j = pl.program_id(0), pl.program_id(1)

        @pl.when((i == 0) & (j == 0))
        def _():
            vacc_ref[...] = jnp.zeros_like(vacc_ref)

        @pl.when(j == 0)
        def _():
            df = _post_norm_bwd(dxo_ref[...], f_ref[...], res, vec_ref, vacc_ref).astype(BF16)
            dfs[...] = df
            df_ref[...] = df
            acc[...] = jnp.zeros_like(acc)

        da = _dot_nt(dfs[...], wd_ref[...])
        gv, uv = g_ref[...].astype(F32), u_ref[...].astype(F32)
        sg = _sigmoid(gv)
        dg = (da * uv * (sg * (1.0 + gv * (1.0 - sg)))).astype(BF16)
        du = (da * (gv * sg)).astype(BF16)
        dgu_ref[0] = dg
        dgu_ref[1] = du
        acc[...] += _dot_nt(dg, wg_ref[...]) + _dot_nt(du, wu_ref[...])

        @pl.when(j == nf - 1)
        def _():
            dx_ref[...] = _pre_norm_bwd(acc[...], x_ref[...], dxo_ref[...], vec_ref, vacc_ref)

    row = lambda i, j: (i, 0)
    return pl.pallas_call(
        body, name=name, grid=(S // tm, nf),
        in_specs=[pl.BlockSpec((tm, D), row, pipeline_mode=pl.Buffered(1))] * 3
        + [pl.BlockSpec((tm, tf), lambda i, j: (i, j)), pl.BlockSpec((tm, tf), lambda i, j: (i, j)),
                  pl.BlockSpec((8, D), lambda i, j: (0, 0)),
                  pl.BlockSpec((D, tf), lambda i, j: (0, j)), pl.BlockSpec((D, tf), lambda i, j: (0, j + nf)),
                  pl.BlockSpec((tf, D), lambda i, j: (j, 0))],
        out_specs=[pl.BlockSpec((tm, D), row), pl.BlockSpec((tm, D), row),
                   pl.BlockSpec((2, tm, tf), lambda i, j: (0, i, j)),
                   pl.BlockSpec((8, D), lambda i, j: (0, 0))],
        out_shape=[jax.ShapeDtypeStruct((S, D), F32), jax.ShapeDtypeStruct((S, D), BF16),
                   jax.ShapeDtypeStruct((2, S, FF), BF16), jax.ShapeDtypeStruct((8, D), F32)],
        scratch_shapes=[pltpu.VMEM((tm, D), BF16), pltpu.VMEM((tm, D), F32)],
        compiler_params=_cp("arbitrary", "arbitrary"),
    )(dxo, x, f, g, u, vec, w_gu, w_gu, w_dn)


def mm_tn(a, b, name, tm, tn, tk, out_dtype=BF16, prev=None, col_off=0, n_total=None):
    S, M = a.shape
    if b.ndim == 3:
        G, _, Nf = b.shape
    else:
        G, Nf = 1, b.shape[1]
    N = G * Nf
    n_total = N if n_total is None else n_total
    tk = min(tk, S)
    nbf = Nf // tn
    nk = S // tk
    ob = col_off // tn

    def body(*refs):
        a_ref, b_ref = refs[0], refs[1]
        o_ref, acc = refs[-2], refs[-1]
        k = pl.program_id(2)

        @pl.when(k == 0)
        def _():
            acc[...] = jnp.zeros_like(acc)

        acc[...] += _dot_tn(a_ref[...], b_ref[...])

        @pl.when(k == nk - 1)
        def _():
            o_ref[...] = acc[...].astype(out_dtype)

    if b.ndim == 3:
        b_spec = pl.BlockSpec((None, tk, tn), lambda i, j, k: (j // nbf, k, j % nbf))
    else:
        b_spec = pl.BlockSpec((tk, tn), lambda i, j, k: (k, j))
    in_specs = [pl.BlockSpec((tk, tm), lambda i, j, k: (k, i)), b_spec]
    args = [a, b]
    aliases = {}
    if prev is not None:
        in_specs.append(ANY)
        args.append(prev)
        aliases = {2: 0}
    return pl.pallas_call(
        body, name=name, grid=(M // tm, N // tn, nk),
        in_specs=in_specs,
        out_specs=pl.BlockSpec((tm, tn), lambda i, j, k: (i, j + ob)),
        out_shape=jax.ShapeDtypeStruct((M, n_total), out_dtype),
        scratch_shapes=[pltpu.VMEM((tm, tn), F32)],
        input_output_aliases=aliases,
        compiler_params=_cp("parallel", "parallel", "arbitrary"),
    )(*args)


def proj_fwd(x, vec, w_in, name, tm=1024, tn=512):
    S = x.shape[0]
    tm = min(tm, S)
    nq = 1536 // tn

    def body(x_ref, vec_ref, w_ref, h_ref, qkv_ref, rest_ref, hs):
        j = pl.program_id(1)

        @pl.when(j == 0)
        def _():
            h = _pre_norm(x_ref[...], vec_ref).astype(BF16)
            hs[...] = h
            h_ref[...] = h

        r = _dot(hs[...], w_ref[...])

        @pl.when(j < nq)
        def _():
            qkv_ref[...] = r.astype(BF16)

        @pl.when(j >= nq)
        def _():
            rest_ref[...] = r

    row = lambda i, j: (i, 0)
    return pl.pallas_call(
        body, name=name, grid=(S // tm, PW // tn),
        in_specs=[pl.BlockSpec((tm, D), row), pl.BlockSpec((8, D), lambda i, j: (0, 0)),
                  pl.BlockSpec((D, tn), lambda i, j: (0, j))],
        out_specs=[pl.BlockSpec((tm, D), row),
                   pl.BlockSpec((tm, tn), lambda i, j: (i, jnp.minimum(j, nq - 1))),
                   pl.BlockSpec((tm, tn), lambda i, j: (i, jnp.maximum(j - nq, 0)))],
        out_shape=[jax.ShapeDtypeStruct((S, D), BF16), jax.ShapeDtypeStruct((S, 1536), BF16),
                   jax.ShapeDtypeStruct((S, 4096), F32)],
        scratch_shapes=[pltpu.VMEM((tm, D), BF16)],
        compiler_params=_cp("parallel", "arbitrary"),
    )(x, vec, w_in)


def proj_bwd(dq, dkv, dxr, d3, w_in, x, dxo, vec, name, tm=1024, tk=512):
    S = x.shape[0]
    tm = min(tm, S)
    nk = PW // tk

    def body(dq_ref, dkv_ref, dxr_ref, d3_ref, w_ref, x_ref, dxo_ref, vec_ref, dx_ref, vacc_ref, acc):
        i, j = pl.program_id(0), pl.program_id(1)

        @pl.when((i == 0) & (j == 0))
        def _():
            vacc_ref[...] = jnp.zeros_like(vacc_ref)

        @pl.when(j == 0)
        def _():
            acc[...] = _dot_nt(dq_ref[...], w_ref[...])

        @pl.when((j >= 1) & (j < 3))
        def _():
            acc[...] += _dot_nt(dkv_ref[...], w_ref[...])

        @pl.when((j >= 3) & (j < 5))
        def _():
            acc[...] += _dot_nt(dxr_ref[...], w_ref[...])

        @pl.when(j >= 5)
        def _():
            acc[...] += _dot_nt(d3_ref[...], w_ref[...])

        @pl.when(j == nk - 1)
        def _():
            dx_ref[...] = _pre_norm_bwd(acc[...], x_ref[...], dxo_ref[...], vec_ref, vacc_ref)

    row = lambda i, j: (i, 0)
    return pl.pallas_call(
        body, name=name, grid=(S // tm, nk),
        in_specs=[pl.BlockSpec((None, tm, tk), lambda i, j: (0, i, 0)),
                  pl.BlockSpec((None, tm, tk), lambda i, j: (jnp.clip(j - 1, 0, 1), i, 0)),
                  pl.BlockSpec((tm, tk), lambda i, j: (i, jnp.clip(j - 3, 0, 1))),
                  pl.BlockSpec((None, tm, tk), lambda i, j: (jnp.clip(j - 5, 0, 5) // 2, i, jnp.clip(j - 5, 0, 5) % 2)),
                  pl.BlockSpec((D, tk), lambda i, j: (0, j)),
                  pl.BlockSpec((tm, D), row), pl.BlockSpec((tm, D), row),
                  pl.BlockSpec((8, D), lambda i, j: (0, 0))],
        out_specs=[pl.BlockSpec((tm, D), row), pl.BlockSpec((8, D), lambda i, j: (0, 0))],
        out_shape=[jax.ShapeDtypeStruct((S, D), F32), jax.ShapeDtypeStruct((8, D), F32)],
        scratch_shapes=[pltpu.VMEM((tm, D), F32)],
        compiler_params=_cp("arbitrary", "arbitrary"),
    )(dq, dkv, dxr, d3, w_in, x, dxo, vec)


def _two_heads(v, lane):
    zero = jnp.zeros((), v.dtype)
    return jnp.concatenate([jnp.where(lane < 64, v, zero), jnp.where(lane >= 64, v, zero)], axis=0)


def _attn_probs(qm, ka, bias_h, i, grp):
    s = _dot_nt(qm, ka) + bias_h
    col = lax.broadcasted_iota(jnp.int32, s.shape, 1)
    first_key = jnp.where(i == 0, 512 - 128 * grp, 0)
    s = jnp.where(col >= first_key, s, NEG)
    e = jnp.exp(s - jnp.max(s, axis=-1, keepdims=True))
    return e / jnp.sum(e, axis=-1, keepdims=True)


def attn_fwd(qkv, bias, name):
    S = qkv.shape[0]
    nb = S // TQ

    def body(q_ref, kp_ref, kc_ref, vp_ref, vc_ref, b_ref, o_ref, kw, vw):
        i = pl.program_id(1)
        kw[0:TQ, :] = kp_ref[...]
        kw[TQ:2 * TQ, :] = kc_ref[...]
        vw[0:TQ, :] = vp_ref[...]
        vw[TQ:2 * TQ, :] = vc_ref[...]
        lane = lax.broadcasted_iota(jnp.int32, (1, HP), 1)

        def group(a, carry):
            r0 = pl.multiple_of(a * 128, 128)
            qa = q_ref[pl.ds(r0, 128), :] * jnp.asarray(0.125, BF16)
            ka = kw[pl.ds(r0, WIN), :]
            va = vw[pl.ds(r0, WIN), :]
            p = _attn_probs(_two_heads(qa, lane), ka, b_ref[...], i, a)
            o2 = _dot(p.astype(BF16), va)
            o_ref[pl.ds(r0, 128), :] = jnp.where(lane < 64, o2[0:128], o2[128:256]).astype(BF16)
            return carry

        lax.fori_loop(0, TQ // 128, group, 0, unroll=True)

    prev = lambda h, i: (jnp.maximum(i - 1, 0), 0)
    return pl.pallas_call(
        body, name=name, grid=(4, nb),
        in_specs=[pl.BlockSpec((TQ, HP), lambda h, i: (i, h)),
                  pl.BlockSpec((TQ, HP), lambda h, i: (jnp.maximum(i - 1, 0), 4 + h)),
                  pl.BlockSpec((TQ, HP), lambda h, i: (i, 4 + h)),
                  pl.BlockSpec((TQ, HP), lambda h, i: (jnp.maximum(i - 1, 0), 8 + h)),
                  pl.BlockSpec((TQ, HP), lambda h, i: (i, 8 + h)),
                  pl.BlockSpec((None, 256, WIN), lambda h, i: (h, 0, 0))],
        out_specs=pl.BlockSpec((TQ, HP), lambda h, i: (i, h)),
        out_shape=jax.ShapeDtypeStruct((S, 512), BF16),
        scratch_shapes=[pltpu.VMEM((2 * TQ, HP), BF16), pltpu.VMEM((2 * TQ, HP), BF16)],
        compiler_params=_cp("parallel", "arbitrary"),
    )(qkv, qkv, qkv, qkv, qkv, bias)


def attn_bwd(qkv, do, bias, name):
    S = qkv.shape[0]
    nb = S // TQ

    def body(q_ref, kp_ref, kc_ref, vp_ref, vc_ref, do_ref, b_ref, dqkv_ref, db_ref, dkv_ref, kw, vw, ak, av):
        i = pl.program_id(1)

        @pl.when(i == 0)
        def _():
            db_ref[...] = jnp.zeros_like(db_ref)
            ak[...] = jnp.zeros_like(ak)
            av[...] = jnp.zeros_like(av)

        @pl.when(i > 0)
        def _():
            ak[0:TQ, :] = ak[TQ:2 * TQ, :]
            av[0:TQ, :] = av[TQ:2 * TQ, :]
            ak[TQ:2 * TQ, :] = jnp.zeros((TQ, HP), F32)
            av[TQ:2 * TQ, :] = jnp.zeros((TQ, HP), F32)

        @pl.when(i < nb)
        def _():
            kw[0:TQ, :] = kp_ref[...]
            kw[TQ:2 * TQ, :] = kc_ref[...]
            vw[0:TQ, :] = vp_ref[...]
            vw[TQ:2 * TQ, :] = vc_ref[...]
            lane = lax.broadcasted_iota(jnp.int32, (1, HP), 1)

            def group(a, carry):
                r0 = pl.multiple_of(a * 128, 128)
                q2 = _two_heads(q_ref[pl.ds(r0, 128), :] * jnp.asarray(0.125, BF16), lane)
                do2 = _two_heads(do_ref[pl.ds(r0, 128), :], lane)
                ka = kw[pl.ds(r0, WIN), :]
                va = vw[pl.ds(r0, WIN), :]
                p = _attn_probs(q2, ka, b_ref[...], i, a)
                dp = _dot_nt(do2, va)
                ds = p * (dp - jnp.sum(p * dp, axis=-1, keepdims=True))
                db_ref[...] += ds
                dsb = ds.astype(BF16)
                dq2 = _dot(dsb, ka)
                ak[pl.ds(r0, WIN), :] += _dot_tn(dsb, q2)
                av[pl.ds(r0, WIN), :] += _dot_tn(p.astype(BF16), do2)
                dq = jnp.where(lane < 64, dq2[0:128], dq2[128:256])
                dqkv_ref[0, pl.ds(r0, 128), :] = (dq * 0.125).astype(BF16)
                return carry

            lax.fori_loop(0, TQ // 128, group, 0, unroll=True)

        @pl.when(i > 0)
        def _():
            dkv_ref[0] = ak[0:TQ, :].astype(BF16)
            dkv_ref[1] = av[0:TQ, :].astype(BF16)

    cur = lambda i: jnp.minimum(i, nb - 1)
    prv = lambda i: jnp.clip(i - 1, 0, nb - 1)
    dq, db, dkv = pl.pallas_call(
        body, name=name, grid=(4, nb + 1),
        in_specs=[pl.BlockSpec((TQ, HP), lambda h, i: (cur(i), h)),
                  pl.BlockSpec((TQ, HP), lambda h, i: (prv(i), 4 + h)),
                  pl.BlockSpec((TQ, HP), lambda h, i: (cur(i), 4 + h)),
                  pl.BlockSpec((TQ, HP), lambda h, i: (prv(i), 8 + h)),
                  pl.BlockSpec((TQ, HP), lambda h, i: (cur(i), 8 + h)),
                  pl.BlockSpec((TQ, HP), lambda h, i: (cur(i), h)),
                  pl.BlockSpec((None, 256, WIN), lambda h, i: (h, 0, 0))],
        out_specs=[pl.BlockSpec((1, TQ, HP), lambda h, i: (0, cur(i), h)),
                   pl.BlockSpec((None, 256, WIN), lambda h, i: (h, 0, 0)),
                   pl.BlockSpec((2, TQ, HP), lambda h, i: (0, prv(i), h))],
        out_shape=[jax.ShapeDtypeStruct((1, S, 512), BF16), jax.ShapeDtypeStruct((4, 256, WIN), F32),
                   jax.ShapeDtypeStruct((2, S, 512), BF16)],
        scratch_shapes=[pltpu.VMEM((2 * TQ, HP), BF16), pltpu.VMEM((2 * TQ, HP), BF16),
                        pltpu.VMEM((2 * TQ, HP), F32), pltpu.VMEM((2 * TQ, HP), F32)],
        compiler_params=_cp("parallel", "arbitrary"),
    )(qkv, qkv, qkv, qkv, qkv, do, bias)
    return dq, db, dkv


def bias_grad(db, name):
    def body(db_ref, o_ref):
        r = lax.broadcasted_iota(jnp.int32, (128, 128), 0)
        c = lax.broadcasted_iota(jnp.int32, (128, 128), 1)
        flip = (r + c == 127).astype(BF16)
        lane = lax.broadcasted_iota(jnp.int32, (16, 384), 1)
        src = lax.broadcasted_iota(jnp.int32, (128, 384), 0)
        dst = lax.broadcasted_iota(jnp.int32, (128, 384), 1)

        def split_dot(v, m):
            hi = v.astype(BF16)
            r1 = v - hi.astype(F32)
            mid = r1.astype(BF16)
            lo = (r1 - mid.astype(F32)).astype(BF16)
            return _dot(hi, m) + _dot(mid, m) + _dot(lo, m)

        def diag_sums(w):
            y = pltpu.roll(split_dot(w, flip), 0, 1, stride=1, stride_axis=0)
            return jnp.broadcast_to(_colsum(y), (16, 128))

        w4 = db_ref[0, :, 512:640]
        w3 = db_ref[0, :, 384:512]
        far = jnp.sum(db_ref[0, :, 0:384]) + jnp.sum(jnp.where(r >= c, w3, 0.0))
        lo4 = diag_sums(jnp.where(r >= c, w4, 0.0))
        up4 = diag_sums(jnp.where(r < c, w4, 0.0))
        up3 = diag_sums(jnp.where(r < c, w3, 0.0))
        p_lo4 = (dst == 128 + (src + 1) % 128).astype(BF16)
        p_up4 = ((dst == src + 1) & (src < 127)).astype(BF16)
        p_up3 = ((dst == src + 129) & (src < 127)).astype(BF16)
        out = split_dot(lo4, p_lo4) + split_dot(up4, p_up4) + split_dot(up3, p_up3)
        o_ref[0] = out + jnp.where(lane == 256, far, 0.0)

    return pl.pallas_call(
        body, name=name, grid=(8,),
        in_specs=[pl.BlockSpec((1, 128, WIN), lambda h: (h, 0, 0))],
        out_specs=pl.BlockSpec((1, 16, 384), lambda h: (h, 0, 0)),
        out_shape=jax.ShapeDtypeStruct((8, 16, 384), F32),
        compiler_params=_cp("parallel"),
    )(db)[:, 0, :]


LT = 256
LC = 512


def _lru_gates(xs, pv_ref, wa_ref, wx_ref, tl):
    xc = (pv_ref[4:5, :] + pv_ref[3:4, :] * xs[pl.ds(8, tl), :] + pv_ref[2:3, :] * xs[pl.ds(7, tl), :]
          + pv_ref[1:2, :] * xs[pl.ds(6, tl), :] + pv_ref[0:1, :] * xs[pl.ds(5, tl), :])
    xcb = xc.astype(BF16)
    pa = jnp.concatenate([_dot(xcb[:, 0:256], wa_ref[0]), _dot(xcb[:, 256:512], wa_ref[1])], axis=1)
    px = jnp.concatenate([_dot(xcb[:, 0:256], wx_ref[0]), _dot(xcb[:, 256:512], wx_ref[1])], axis=1)
    r = _sigmoid(pa + pv_ref[5:6, :])
    ig = _sigmoid(px + pv_ref[6:7, :])
    z = -pv_ref[7:8, :]
    sp = jnp.maximum(z, 0.0) + jnp.log1p(jnp.exp(-jnp.abs(z)))
    log_a = (-LRU_C * r) * sp
    a = jnp.exp(log_a)
    mult = jnp.sqrt(-_expm1(2.0 * log_a))
    return xc, xcb, r, ig, sp, a, mult


def lru_fwd(rest, pvec, wa, wx, name):
    S = rest.shape[0]
    tl = min(LT, S)
    nt = S // tl

    def body(xr_ref, halo_ref, yr_ref, pv_ref, wa_ref, wx_ref, h_ref, hg_ref, xs, a_s, u_s, h_s, carry):
        ti = pl.program_id(1)

        @pl.when(ti == 0)
        def _():
            carry[...] = jnp.zeros_like(carry)

        xs[0:8, :] = jnp.where(ti > 0, halo_ref[...], 0.0)
        xs[pl.ds(8, tl), :] = xr_ref[...]
        xc, _, _, ig, _, a, mult = _lru_gates(xs, pv_ref, wa_ref, wx_ref, tl)
        a_s[...] = a
        u_s[...] = mult * (ig * xc)
        row = lax.broadcasted_iota(jnp.int32, (8, LC), 0)

        def blk(bi, c):
            o = pl.multiple_of(bi * 8, 8)
            av = a_s[pl.ds(o, 8), :]
            bv = u_s[pl.ds(o, 8), :]
            for d in (1, 2, 4):
                a_sh = pltpu.roll(av, d, 0)
                b_sh = pltpu.roll(bv, d, 0)
                m = row >= d
                bv = jnp.where(m, av * b_sh + bv, bv)
                av = jnp.where(m, av * a_sh, av)
            hv = bv + av * c
            h_s[pl.ds(o, 8), :] = hv
            return hv[7:8, :]

        carry[...] = lax.fori_loop(0, tl // 8, blk, carry[...])
        h = h_s[...]
        h_ref[...] = h
        hg_ref[...] = (h * _gelu(yr_ref[...])).astype(BF16)

    hb = tl // 8
    return pl.pallas_call(
        body, name=name, grid=(2, nt),
        in_specs=[pl.BlockSpec((tl, LC), lambda c, t: (t, c)),
                  pl.BlockSpec((8, LC), lambda c, t: (jnp.maximum(t * hb - 1, 0), c)),
                  pl.BlockSpec((tl, LC), lambda c, t: (t, 2 + c)),
                  pl.BlockSpec((8, LC), lambda c, t: (0, c)),
                  pl.BlockSpec((2, 256, 256), lambda c, t: (c, 0, 0)),
                  pl.BlockSpec((2, 256, 256), lambda c, t: (c, 0, 0))],
        out_specs=[pl.BlockSpec((tl, LC), lambda c, t: (t, c)), pl.BlockSpec((tl, LC), lambda c, t: (t, c))],
        out_shape=[jax.ShapeDtypeStruct((S, D), F32), jax.ShapeDtypeStruct((S, D), BF16)],
        scratch_shapes=[pltpu.VMEM((tl + 8, LC), F32), pltpu.VMEM((tl, LC), F32), pltpu.VMEM((tl, LC), F32),
                        pltpu.VMEM((tl, LC), F32), pltpu.VMEM((1, LC), F32)],
        compiler_params=_cp("parallel", "arbitrary"),
    )(rest, rest, rest, pvec, wa, wx)


def lru_bwd(dh, h, rest, pvec, wa, wx, name):
    S = rest.shape[0]
    tl = min(LT, S)
    nt = S // tl

    def body(dh_ref, h_ref, hhalo_ref, xr_ref, xhalo_ref, pv_ref, wa_ref, wx_ref,
             dxr_ref, vacc_ref, dwa_ref, dwx_ref,
             xs, hs, a_s, ash_s, b_s, lam_s, dxe, anext, lnext, dxnext):
        ti = pl.program_id(1)
        tr = nt - 1 - ti

        @pl.when(ti == 0)
        def _():
            anext[...] = jnp.zeros_like(anext)
            lnext[...] = jnp.zeros_like(lnext)
            dxnext[...] = jnp.zeros_like(dxnext)
            vacc_ref[...] = jnp.zeros_like(vacc_ref)
            dwa_ref[...] = jnp.zeros_like(dwa_ref)
            dwx_ref[...] = jnp.zeros_like(dwx_ref)

        xs[0:8, :] = jnp.where(tr > 0, xhalo_ref[...], 0.0)
        xs[pl.ds(8, tl), :] = xr_ref[...]
        xc, xcb, r, ig, sp, a, mult = _lru_gates(xs, pv_ref, wa_ref, wx_ref, tl)

        a_s[pl.ds(0, tl), :] = a
        a_s[pl.ds(tl, 8), :] = jnp.broadcast_to(anext[...], (8, LC))
        ash_s[...] = a_s[pl.ds(1, tl), :]
        b_s[...] = dh_ref[...]
        row = lax.broadcasted_iota(jnp.int32, (8, LC), 0)

        def blk(k, c):
            o = pl.multiple_of((tl // 8 - 1 - k) * 8, 8)
            av = ash_s[pl.ds(o, 8), :]
            bv = b_s[pl.ds(o, 8), :]
            for d in (1, 2, 4):
                a_sh = pltpu.roll(av, 8 - d, 0)
                b_sh = pltpu.roll(bv, 8 - d, 0)
                m = row < 8 - d
                bv = jnp.where(m, bv + av * b_sh, bv)
                av = jnp.where(m, av * a_sh, av)
            lv = bv + av * c
            lam_s[pl.ds(o, 8), :] = lv
            return lv[0:1, :]

        lnext[...] = lax.fori_loop(0, tl // 8, blk, lnext[...])
        anext[...] = a[0:1, :]
        lam = lam_s[...]

        hs[0:8, :] = jnp.where(tr > 0, hhalo_ref[...], 0.0)
        hs[pl.ds(8, tl), :] = h_ref[...]
        d_a = lam * hs[pl.ds(7, tl), :]
        d_mult = lam * (ig * xc)
        d_ig = lam * mult * xc
        dxc = lam * mult * ig
        d_log_a = d_a * a - d_mult * (a * a) / mult
        d_r = d_log_a * (-LRU_C * sp)
        vacc_ref[7:8, :] += _colsum(d_log_a * (-LRU_C * r)) * (-_sigmoid(-pv_ref[7:8, :]))
        d_pa = d_r * r * (1.0 - r)
        d_px = d_ig * ig * (1.0 - ig)
        vacc_ref[5:6, :] += _colsum(d_pa)
        vacc_ref[6:7, :] += _colsum(d_px)
        dpa = d_pa.astype(BF16)
        dpx = d_px.astype(BF16)
        back = []
        for g in range(2):
            sl = slice(256 * g, 256 * g + 256)
            dwa_ref[g] += _dot_tn(xcb[:, sl], dpa[:, sl])
            dwx_ref[g] += _dot_tn(xcb[:, sl], dpx[:, sl])
            back.append(_dot_nt(dpa[:, sl], wa_ref[g]) + _dot_nt(dpx[:, sl], wx_ref[g]))
        dxc = dxc + jnp.concatenate(back, axis=1)
        vacc_ref[4:5, :] += _colsum(dxc)
        for k in range(4):
            vacc_ref[k:k + 1, :] += _colsum(dxc * xs[pl.ds(5 + k, tl), :])
        dxe[pl.ds(0, tl), :] = dxc
        dxe[pl.ds(tl, 8), :] = dxnext[...]
        dxr = (pv_ref[3:4, :] * dxc + pv_ref[2:3, :] * dxe[pl.ds(1, tl), :]
               + pv_ref[1:2, :] * dxe[pl.ds(2, tl), :] + pv_ref[0:1, :] * dxe[pl.ds(3, tl), :])
        dxr_ref[...] = dxr.astype(BF16)
        dxnext[...] = dxc[0:8, :]

    hb = tl // 8
    rev = lambda t: nt - 1 - t
    halo = lambda t: jnp.maximum(rev(t) * hb - 1, 0)
    big = lambda: pltpu.VMEM((tl + 8, LC), F32)
    til = lambda: pltpu.VMEM((tl, LC), F32)
    return pl.pallas_call(
        body, name=name, grid=(2, nt),
        in_specs=[pl.BlockSpec((tl, LC), lambda c, t: (rev(t), c)),
                  pl.BlockSpec((tl, LC), lambda c, t: (rev(t), c)),
                  pl.BlockSpec((8, LC), lambda c, t: (halo(t), c)),
                  pl.BlockSpec((tl, LC), lambda c, t: (rev(t), c)),
                  pl.BlockSpec((8, LC), lambda c, t: (halo(t), c)),
                  pl.BlockSpec((8, LC), lambda c, t: (0, c)),
                  pl.BlockSpec((2, 256, 256), lambda c, t: (c, 0, 0)),
                  pl.BlockSpec((2, 256, 256), lambda c, t: (c, 0, 0))],
        out_specs=[pl.BlockSpec((tl, LC), lambda c, t: (rev(t), c)),
                   pl.BlockSpec((8, LC), lambda c, t: (0, c)),
                   pl.BlockSpec((2, 256, 256), lambda c, t: (c, 0, 0)),
                   pl.BlockSpec((2, 256, 256), lambda c, t: (c, 0, 0))],
        out_shape=[jax.ShapeDtypeStruct((S, D), BF16), jax.ShapeDtypeStruct((8, D), F32),
                   jax.ShapeDtypeStruct((4, 256, 256), F32), jax.ShapeDtypeStruct((4, 256, 256), F32)],
        scratch_shapes=[big(), big(), big(), til(), til(), til(), big(),
                        pltpu.VMEM((1, LC), F32), pltpu.VMEM((1, LC), F32), pltpu.VMEM((8, LC), F32)],
        compiler_params=_cp("parallel", "arbitrary"),
    )(dh, h, h, rest, rest, pvec, wa, wx)


def mix_out_fwd(x, ao, hg, rest, vec, w_att_o, w_rec_o, w_out, name, tm=256):
    S = x.shape[0]
    tm = min(tm, S)

    def body(x_ref, ao_ref, hg_ref, ga_ref, gr_ref, vec_ref, wa_ref, wr_ref, wo_ref,
             xo_ref, att_ref, rec_ref, mg_ref, f_ref):
        att = _dot(ao_ref[...], wa_ref[...])
        rec = _dot(hg_ref[...], wr_ref[...])
        att_ref[...] = att
        rec_ref[...] = rec
        mg = (_sigmoid(ga_ref[...]) * att + _sigmoid(gr_ref[...]) * rec).astype(BF16)
        mg_ref[...] = mg
        f = _dot(mg, wo_ref[...])
        f_ref[...] = f
        y = f * lax.rsqrt(_mean(f * f) + EPS) * vec_ref[1:2, :]
        xo_ref[...] = x_ref[...] + (1.0 * vec_ref[4:5, :]) * y

    row = lambda i: (i, 0)
    full = lambda r: pl.BlockSpec((r, D), lambda i: (0, 0))
    return pl.pallas_call(
        body, name=name, grid=(S // tm,),
        in_specs=[pl.BlockSpec((tm, D), row), pl.BlockSpec((tm, 512), row), pl.BlockSpec((tm, D), row),
                  pl.BlockSpec((tm, D), lambda i: (i, 2)), pl.BlockSpec((tm, D), lambda i: (i, 3)),
                  full(8), full(512), full(D), full(D)],
        out_specs=[pl.BlockSpec((tm, D), row)] * 5,
        out_shape=[jax.ShapeDtypeStruct((S, D), F32), jax.ShapeDtypeStruct((S, D), F32),
                   jax.ShapeDtypeStruct((S, D), F32), jax.ShapeDtypeStruct((S, D), BF16),
                   jax.ShapeDtypeStruct((S, D), F32)],
        compiler_params=_cp("parallel"),
    )(x, ao, hg, rest, rest, vec, w_att_o, w_rec_o, w_out)


def mix_out_bwd(dxo, f, att, rec, rest, h, vec, w_att_o, w_rec_o, w_out, name, tm=256):
    S = dxo.shape[0]
    tm = min(tm, S)

    def body(dxo_ref, f_ref, att_ref, rec_ref, yr_ref, ga_ref, gr_ref, h_ref, vec_ref, wa_ref, wr_ref, wo_ref,
             df_ref, da_ref, dr_ref, dao_ref, dh_ref, d3_ref, vacc_ref):
        @pl.when(pl.program_id(0) == 0)
        def _():
            vacc_ref[...] = jnp.zeros_like(vacc_ref)

        df = _post_norm_bwd(dxo_ref[...], f_ref[...], 1.0, vec_ref, vacc_ref).astype(BF16)
        df_ref[...] = df
        dm = _dot_nt(df, wo_ref[...])
        sa = _sigmoid(ga_ref[...])
        sr = _sigmoid(gr_ref[...])
        d_att = (dm * sa).astype(BF16)
        d_rec = (dm * sr).astype(BF16)
        da_ref[...] = d_att
        dr_ref[...] = d_rec
        d3_ref[1] = (dm * att_ref[...] * (sa * (1.0 - sa))).astype(BF16)
        d3_ref[2] = (dm * rec_ref[...] * (sr * (1.0 - sr))).astype(BF16)
        dao_ref[...] = _dot_nt(d_att, wa_ref[...]).astype(BF16)
        d_hg = _dot_nt(d_rec, wr_ref[...])
        yr = yr_ref[...]
        dh_ref[...] = d_hg * _gelu(yr)
        d3_ref[0] = (d_hg * h_ref[...] * _gelu_grad(yr)).astype(BF16)

    row = lambda i: (i, 0)
    full = lambda r: pl.BlockSpec((r, D), lambda i: (0, 0))
    return pl.pallas_call(
        body, name=name, grid=(S // tm,),
        in_specs=[pl.BlockSpec((tm, D), row)] * 4
        + [pl.BlockSpec((tm, D), lambda i: (i, 1)), pl.BlockSpec((tm, D), lambda i: (i, 2)),
           pl.BlockSpec((tm, D), lambda i: (i, 3)), pl.BlockSpec((tm, D), row),
           full(8), full(512), full(D), full(D)],
        out_specs=[pl.BlockSpec((tm, D), row)] * 3
        + [pl.BlockSpec((tm, 512), row), pl.BlockSpec((tm, D), row),
           pl.BlockSpec((3, tm, D), lambda i: (0, i, 0)), pl.BlockSpec((8, D), lambda i: (0, 0))],
        out_shape=[jax.ShapeDtypeStruct((S, D), BF16)] * 3
        + [jax.ShapeDtypeStruct((S, 512), BF16), jax.ShapeDtypeStruct((S, D), F32),
           jax.ShapeDtypeStruct((3, S, D), BF16), jax.ShapeDtypeStruct((8, D), F32)],
        compiler_params=_cp("arbitrary"),
    )(dxo, f, att, rec, rest, rest, rest, h, vec, w_att_o, w_rec_o, w_out)


def loss_grad(y, tgt, name, tm=512):
    S = y.shape[0]
    tm = min(tm, S)
    nt = S // tm

    def body(y_ref, t_ref, dy_ref, l_ref, acc):
        i = pl.program_id(0)

        @pl.when(i == 0)
        def _():
            acc[...] = jnp.zeros_like(acc)

        d = y_ref[...] - t_ref[...]
        dy_ref[...] = d * (1.0 / D)
        acc[...] += _colsum(d * d)

        @pl.when(i == nt - 1)
        def _():
            l_ref[...] = jnp.broadcast_to(0.5 * jnp.sum(acc[...]) * (1.0 / D), (8, 128))

    return pl.pallas_call(
        body, name=name, grid=(nt,),
        in_specs=[pl.BlockSpec((tm, D), lambda i: (i, 0))] * 2,
        out_specs=[pl.BlockSpec((tm, D), lambda i: (i, 0)), pl.BlockSpec((8, 128), lambda i: (0, 0))],
        out_shape=[jax.ShapeDtypeStruct((S, D), F32), jax.ShapeDtypeStruct((8, 128), F32)],
        scratch_shapes=[pltpu.VMEM((1, D), F32)],
        compiler_params=_cp("arbitrary"),
    )(y, tgt)


def ada_fwd(c_all, w_ada, b_ada, name, tn=768):
    n = w_ada.shape[1]

    def body(c_ref, w_ref, b_ref, o_ref):
        cv = c_ref[...]
        ca = (cv * _sigmoid(cv)).astype(BF16)
        o_ref[...] = _dot(ca, w_ref[...].astype(BF16)) + b_ref[...]

    return pl.pallas_call(
        body, name=name, grid=(n // tn,),
        in_specs=[pl.BlockSpec((8, D), lambda j: (0, 0)), pl.BlockSpec((D, tn), lambda j: (0, j)),
                  pl.BlockSpec((1, tn), lambda j: (0, j))],
        out_specs=pl.BlockSpec((8, tn), lambda j: (0, j)),
        out_shape=jax.ShapeDtypeStruct((8, n), F32),
        compiler_params=_cp("parallel"),
    )(c_all, w_ada, b_ada)


def ada_bwd(c_all_t, dmod, name, tn=768):
    n = dmod.shape[1]

    def body(c_ref, d_ref, o_ref):
        cv = c_ref[...]
        ca = (cv * _sigmoid(cv)).astype(BF16)
        o_ref[...] = _dot(ca, d_ref[...].astype(BF16))

    return pl.pallas_call(
        body, name=name, grid=(n // tn,),
        in_specs=[pl.BlockSpec((D, 128), lambda j: (0, 0)), pl.BlockSpec((128, tn), lambda j: (0, j))],
        out_specs=pl.BlockSpec((D, tn), lambda j: (0, j)),
        out_shape=jax.ShapeDtypeStruct((D, n), F32),
        compiler_params=_cp("parallel"),
    )(c_all_t, dmod)


def _row_tile(rows, cols, itemsize=4, budget=1536 * 1024):
    best = None
    for t in range(8, rows + 1, 8):
        if rows % t == 0 and t * cols * itemsize <= budget:
            best = t
    return rows if best is None else best


def sum_lead(parts, name, out_dtype=F32):
    n, R, C = parts.shape
    tr = _row_tile(R, C * n)

    def body(p_ref, o_ref):
        acc = p_ref[0].astype(F32)
        for k in range(1, n):
            acc = acc + p_ref[k].astype(F32)
        o_ref[...] = acc.astype(out_dtype)

    return pl.pallas_call(
        body, name=name, grid=(R // tr,),
        in_specs=[pl.BlockSpec((n, tr, C), lambda i: (0, i, 0))],
        out_specs=pl.BlockSpec((tr, C), lambda i: (i, 0)),
        out_shape=jax.ShapeDtypeStruct((R, C), out_dtype),
        compiler_params=_cp("parallel"),
    )(parts)


def adamw(w, g, m, v, name):
    R, C = w.shape
    tr = _row_tile(R, C * 7, budget=8 * 1024 * 1024)

    def body(w_ref, g_ref, m_ref, v_ref, d_ref, mo_ref, vo_ref):
        gv = g_ref[...]
        mn = ADAM_B1 * m_ref[...] + (1.0 - ADAM_B1) * gv
        vn = ADAM_B2 * v_ref[...] + (1.0 - ADAM_B2) * (gv * gv)
        m_hat = mn / (1.0 - ADAM_B1 ** ADAM_STEP)
        v_hat = vn / (1.0 - ADAM_B2 ** ADAM_STEP)
        d_ref[...] = -ADAM_LR * (m_hat / (jnp.sqrt(v_hat) + ADAM_EPS) + ADAM_WD * w_ref[...])
        mo_ref[...] = mn
        vo_ref[...] = vn

    spec = pl.BlockSpec((tr, C), lambda i: (i, 0))
    return pl.pallas_call(
        body, name=name, grid=(R // tr,),
        in_specs=[spec] * 4, out_specs=[spec] * 3,
        out_shape=[jax.ShapeDtypeStruct((R, C), F32)] * 3,
        compiler_params=_cp("parallel"),
    )(w, g, m, v)


def _mesh_pos():
    return lax.axis_index("x"), lax.axis_index("y"), lax.axis_index("c")


def _other_chips(mx, my):
    return [(1 - mx, my), (mx, 1 - my), (1 - mx, 1 - my)]


def ag_small(x, name):
    R = x.shape[0]

    def body(x_ref, out_ref, send_sems, recv_sems, local_sem):
        mx, my, mc = _mesh_pos()
        me, sibling = (mx, my, mc), (mx, my, 1 - mc)
        chips = _other_chips(mx, my)

        def slot(px, py, pc):
            return out_ref.at[4 * px + 2 * py + pc]

        def copy(k, block, to, src=None):
            return pltpu.make_async_remote_copy(
                src_ref=slot(*block) if src is None else src, dst_ref=slot(*block),
                send_sem=send_sems.at[k], recv_sem=recv_sems.at[k], device_id=to, device_id_type=MESH)

        mine = pltpu.make_async_copy(x_ref, slot(*me), local_sem)
        mine.start()
        first = [copy(0, me, sibling, src=x_ref)]
        first += [copy(1 + j, me, (*chip, mc), src=x_ref) for j, chip in enumerate(chips)]
        for cp in first:
            cp.start()
        passed = [copy(4 + j, (*chip, mc), sibling) for j, chip in enumerate(chips)]
        for j, chip in enumerate(chips):
            copy(1 + j, (*chip, mc), me).wait_recv()
            passed[j].start()
        copy(0, sibling, me).wait_recv()
        for j, chip in enumerate(chips):
            copy(4 + j, (*chip, 1 - mc), me).wait_recv()
        for cp in first + passed:
            cp.wait_send()
        mine.wait()

    return pl.pallas_call(
        body, name=name,
        out_shape=jax.ShapeDtypeStruct((N_DEV, R, 128), F32),
        in_specs=[pl.BlockSpec(memory_space=pltpu.VMEM)],
        out_specs=pl.BlockSpec(memory_space=pltpu.VMEM),
        scratch_shapes=[pltpu.SemaphoreType.DMA((7,)), pltpu.SemaphoreType.DMA((7,)), pltpu.SemaphoreType.DMA],
        compiler_params=pltpu.CompilerParams(vmem_limit_bytes=VMEM_LIMIT),
    )(x)


BIG = (("ffn1_w_gu", "col", D, PW), ("ffn1_w_down", "row", FF, D), ("w_in", "col", D, PW),
       ("w_att_o", "col", 512, D), ("w_rec_o", "row", D, D), ("w_out", "row", D, D),
       ("ffn2_w_gu", "col", D, PW), ("ffn2_w_down", "row", FF, D))
NBIG = len(BIG)


def _shard_shape(kind, R, C):
    return (R, C // 4) if kind == "col" else (R // 4, C)


def _region(ref, kind, R, C, q, half, t, tr):
    sr, sc = _shard_shape(kind, R, C)
    if kind == "col":
        return ref.at[pl.ds(pl.multiple_of(half * (R // 2) + t * tr, 16), tr), pl.ds(q * sc, sc)]
    return ref.at[pl.ds(pl.multiple_of(q * sr + t * tr, 16), tr), pl.ds(half * (C // 2), C // 2)]


def ag_local(w, kind, R, C, p_arr, name):
    sr, sc = _shard_shape(kind, R, C)
    tr = _row_tile(sr, sc, budget=2 * 1024 * 1024)
    nt = sr // tr

    def body(p_ref, w_ref, o_ref):
        o_ref[...] = w_ref[...].astype(BF16)

    if kind == "col":
        o_spec = pl.BlockSpec((tr, sc), lambda i, p: (i, p[0]))
    else:
        o_spec = pl.BlockSpec((tr, sc), lambda i, p: (p[0] * nt + i, 0))
    return pl.pallas_call(
        body, name=name,
        grid_spec=pltpu.PrefetchScalarGridSpec(
            num_scalar_prefetch=1, grid=(nt,), in_specs=[pl.BlockSpec((tr, sc), lambda i, p: (i, 0))],
            out_specs=o_spec),
        out_shape=jax.ShapeDtypeStruct((R, C), BF16),
        compiler_params=_cp("parallel"),
    )(p_arr, w)


HBM_SPEC = pl.BlockSpec(memory_space=pltpu.HBM)
SEM_SPEC = pl.BlockSpec(memory_space=pltpu.SEMAPHORE)


def _ag_copies(fulls, geoms, ssem, rsem, mx, my, mc, q, h):
    chips = _other_chips(mx, my)
    out = []
    for w, (kind, R, C) in enumerate(geoms):
        sr, sc = _shard_shape(kind, R, C)
        hr = sr // 2 if kind == "col" else sr
        reg = _region(fulls[w], kind, R, C, q, h, 0, hr)
        out += [pltpu.make_async_remote_copy(src_ref=reg, dst_ref=reg, send_sem=ssem.at[3 * w + k],
                                             recv_sem=rsem.at[3 * w + k], device_id=(*chips[k], mc),
                                             device_id_type=MESH) for k in range(3)]
    return out


def ag_start(fulls, geoms, after, name):
    n = len(fulls)
    after = list(after)
    m = len(after)

    def body(*refs):
        ssem, rsem = refs[n + m:n + m + 2]
        outs, token = refs[n + m + 2:2 * n + m + 2], refs[2 * n + m + 2]
        mx, my, mc = _mesh_pos()
        p = 2 * mx + my
        col = [w for w, g in enumerate(geoms) if g[0] == "col"]
        row = [w for w, g in enumerate(geoms) if g[0] == "row"]
        for q in range(4):
            @pl.when(p == q)
            def _(q=q):
                cps = _ag_copies(outs, geoms, ssem, rsem, mx, my, mc, q, mc)
                for w in col:
                    for cp in cps[3 * w:3 * w + 3]:
                        cp.start()
        for h in range(2):
            @pl.when(mc == h)
            def _(h=h):
                cps = _ag_copies(outs, geoms, ssem, rsem, mx, my, mc, p, h)
                for w in row:
                    for cp in cps[3 * w:3 * w + 3]:
                        cp.start()
        token[...] = jnp.zeros_like(token)

    res = pl.pallas_call(
        body, name=name,
        out_shape=[pltpu.SemaphoreType.DMA((3 * n,)), pltpu.SemaphoreType.DMA((3 * n,))]
        + [pltpu.HBM(a.shape, a.dtype) for a in fulls] + [jax.ShapeDtypeStruct((8, 128), F32)],
        in_specs=[HBM_SPEC] * n + [ANY] * m,
        out_specs=[SEM_SPEC, SEM_SPEC] + [HBM_SPEC] * n + [pl.BlockSpec(memory_space=pltpu.VMEM)],
        input_output_aliases={w: 2 + w for w in range(n)},
        compiler_params=pltpu.CompilerParams(has_side_effects=pltpu.SideEffectType.DATAFLOW_SIDE_EFFECTING),
    )(*[pltpu.with_memory_space_constraint(a, pltpu.HBM) for a in fulls], *after)
    return res[0], res[1], list(res[2:2 + n]), res[2 + n]


def ag_wait(fulls, geoms, ssem, rsem, after, name):
    n = len(fulls)

    def body(*refs):
        ins, ssem_ref, rsem_ref = refs[:n], refs[n], refs[n + 1]
        mx, my, mc = _mesh_pos()
        for cp in _ag_copies(ins, geoms, ssem_ref, rsem_ref, mx, my, mc, 0, 0):
            cp.wait_send()
            cp.wait_recv()

    return list(pl.pallas_call(
        body, name=name,
        out_shape=[pltpu.HBM(a.shape, a.dtype) for a in fulls],
        in_specs=[HBM_SPEC] * n + [SEM_SPEC, SEM_SPEC, ANY],
        out_specs=[HBM_SPEC] * n,
        input_output_aliases={w: w for w in range(n)},
        compiler_params=pltpu.CompilerParams(has_side_effects=pltpu.SideEffectType.DATAFLOW_SIDE_EFFECTING),
    )(*fulls, ssem, rsem, after))


def ag_forward(full, kind, R, C, name):
    sr, sc = _shard_shape(kind, R, C)
    hr, hc = (sr // 2, sc) if kind == "col" else (sr, sc // 2)
    tr = _row_tile(hr, hc, itemsize=2, budget=512 * 1024)
    nt = hr // tr

    total = 3 * nt

    def body(src_ref, full_ref, stage, lsem, ssem, rsem):
        step = pl.program_id(0) * nt + pl.program_id(1)
        par = step % 2
        mx, my, mc = _mesh_pos()

        def load(s, q, h, t):
            return pltpu.make_async_copy(_region(src_ref, kind, R, C, q, h, t, tr), stage.at[s], lsem.at[s])

        def push(s, q, h, t):
            return pltpu.make_async_remote_copy(src_ref=stage.at[s], dst_ref=_region(full_ref, kind, R, C, q, h, t, tr),
                                                send_sem=ssem.at[s], recv_sem=rsem, device_id=(mx, my, 1 - mc),
                                                device_id_type=MESH)

        def for_tile(stp, fn):
            q_k = _partner_chip(stp // nt, 2 * mx + my)
            if kind == "col":
                for q in range(4):
                    @pl.when(q_k == q)
                    def _(q=q):
                        fn(q, mc, stp % nt)
            else:
                for h in range(2):
                    @pl.when(mc == h)
                    def _(h=h):
                        fn(q_k, h, stp % nt)

        @pl.when(step == 0)
        def _():
            for_tile(step, lambda q, h, t: load(0, q, h, t).start())

        load(par, 0, 0, 0).wait()
        for_tile(step, lambda q, h, t: push(par, q, h, t).start())

        @pl.when(step + 1 < total)
        def _():
            @pl.when(step >= 1)
            def _():
                push(1 - par, 0, 0, 0).wait_send()
            for_tile(step + 1, lambda q, h, t: load(1 - par, q, h, t).start())

        @pl.when(step == total - 1)
        def _():
            push(par, 0, 0, 0).wait_send()
            push(1 - par, 0, 0, 0).wait_send()
            three = full_ref.at[pl.ds(0, hr), pl.ds(0, 3 * hc)] if kind == "col" else full_ref.at[pl.ds(0, 3 * hr), pl.ds(0, hc)]
            pltpu.make_async_remote_copy(src_ref=three, dst_ref=three, send_sem=ssem.at[0], recv_sem=rsem,
                                         device_id=(mx, my, 1 - mc), device_id_type=MESH).wait_recv()

    return pl.pallas_call(
        body, name=name, grid=(3, nt),
        in_specs=[ANY], out_specs=ANY,
        out_shape=jax.ShapeDtypeStruct((R, C), BF16),
        scratch_shapes=[pltpu.VMEM((2, tr, hc), BF16), pltpu.SemaphoreType.DMA((2,)), pltpu.SemaphoreType.DMA((2,)),
                        pltpu.SemaphoreType.DMA],
        input_output_aliases={0: 0},
        compiler_params=_cp("arbitrary", "arbitrary"),
    )(full)


def _half_shape(kind, R, C):
    return (R // 2, C) if kind == "col" else (R, C // 2)


def _piece_shape(kind, R, C):
    return (R // 2, C // 4) if kind == "col" else (R // 4, C // 2)


def pair_push(g, kind, c_arr, name):
    R, C = g.shape
    hr, hc = _half_shape(kind, R, C)
    tr = _row_tile(hr, hc, itemsize=2, budget=1024 * 1024)
    nt = hr // tr

    def body(c_ref, g_ref, out_ref, stage, ssem, rsem):
        i = pl.program_id(0)
        slot = i % 2
        mx, my, mc = _mesh_pos()

        def push(s, t):
            return pltpu.make_async_remote_copy(
                src_ref=stage.at[s], dst_ref=out_ref.at[pl.ds(pl.multiple_of(t * tr, 16), tr)],
                send_sem=ssem.at[s], recv_sem=rsem, device_id=(mx, my, 1 - mc), device_id_type=MESH)

        @pl.when(i >= 2)
        def _():
            push(slot, 0).wait_send()

        stage[slot] = g_ref[...]
        push(slot, i).start()

        @pl.when(i == nt - 1)
        def _():
            push(slot, 0).wait_send()
            if nt >= 2:
                push(1 - slot, 0).wait_send()
            pltpu.make_async_remote_copy(src_ref=out_ref, dst_ref=out_ref, send_sem=ssem.at[0], recv_sem=rsem,
                                         device_id=(mx, my, 1 - mc), device_id_type=MESH).wait_recv()

    if kind == "col":
        g_spec = pl.BlockSpec((tr, hc), lambda i, c: ((1 - c[0]) * nt + i, 0))
    else:
        g_spec = pl.BlockSpec((tr, hc), lambda i, c: (i, 1 - c[0]))
    return pl.pallas_call(
        body, name=name,
        grid_spec=pltpu.PrefetchScalarGridSpec(
            num_scalar_prefetch=1, grid=(nt,), in_specs=[g_spec], out_specs=ANY,
            scratch_shapes=[pltpu.VMEM((2, tr, hc), BF16), pltpu.SemaphoreType.DMA((2,)), pltpu.SemaphoreType.DMA]),
        out_shape=jax.ShapeDtypeStruct((hr, hc), BF16),
        compiler_params=_cp("arbitrary"),
    )(c_arr, g)


def _partner_chip(k, p):
    return p ^ jnp.where(k == 0, 2, jnp.where(k == 1, 1, jnp.where(k == 2, 3, 0)))


def pair_add(g, got, kind, cp_arr, name):
    R, C = g.shape
    pr, pc = _piece_shape(kind, R, C)
    tr = _row_tile(pr, pc, itemsize=2, budget=1024 * 1024)
    nt = pr // tr

    def body(cp_ref, g_ref, got_ref, ps_ref, rb_ref):
        tile = (g_ref[...].astype(F32) + got_ref[...].astype(F32)).astype(BF16)
        ps_ref[...] = tile

        @pl.when(pl.program_id(1) == cp_ref[1])
        def _():
            rb_ref[...] = tile

    if kind == "col":
        g_spec = pl.BlockSpec((tr, pc), lambda i, q, cp: (cp[0] * nt + i, q))
        got_spec = pl.BlockSpec((tr, pc), lambda i, q, cp: (i, q))
    else:
        g_spec = pl.BlockSpec((tr, pc), lambda i, q, cp: (q * nt + i, cp[0]))
        got_spec = pl.BlockSpec((tr, pc), lambda i, q, cp: (q * nt + i, 0))
    return pl.pallas_call(
        body, name=name,
        grid_spec=pltpu.PrefetchScalarGridSpec(
            num_scalar_prefetch=1, grid=(nt, 4), in_specs=[g_spec, got_spec],
            out_specs=[pl.BlockSpec((None, tr, pc), lambda i, q, cp: (q, i, 0)),
                       pl.BlockSpec((None, tr, pc), lambda i, q, cp: (cp[1], i, 0))]),
        out_shape=[jax.ShapeDtypeStruct((4, pr, pc), BF16)] * 2,
        compiler_params=_cp("arbitrary", "arbitrary"),
    )(cp_arr, g, got)


def _rs_copies(ps, rb, ssem, rsem, mx, my, mc):
    p = 2 * mx + my
    out = []
    for w in range(len(ps)):
        for k, chip in enumerate(_other_chips(mx, my)):
            out.append(pltpu.make_async_remote_copy(
                src_ref=ps[w].at[2 * chip[0] + chip[1]], dst_ref=rb[w].at[p], send_sem=ssem.at[3 * w + k],
                recv_sem=rsem.at[3 * w + k], device_id=(*chip, mc), device_id_type=MESH))
    return out


def rs_start(ps, rb, after, name):
    n = len(ps)
    after = list(after)
    m = len(after)

    def body(*refs):
        ssem, rsem = refs[2 * n + m:2 * n + m + 2]
        ps_o = refs[2 * n + m + 2:3 * n + m + 2]
        rb_o = refs[3 * n + m + 2:4 * n + m + 2]
        token = refs[4 * n + m + 2]
        for cp in _rs_copies(ps_o, rb_o, ssem, rsem, *_mesh_pos()):
            cp.start()
        token[...] = jnp.zeros_like(token)

    both = list(ps) + list(rb)
    res = pl.pallas_call(
        body, name=name,
        out_shape=[pltpu.SemaphoreType.DMA((3 * n,)), pltpu.SemaphoreType.DMA((3 * n,))]
        + [pltpu.HBM(a.shape, a.dtype) for a in both] + [jax.ShapeDtypeStruct((8, 128), F32)],
        in_specs=[HBM_SPEC] * (2 * n) + [ANY] * m,
        out_specs=[SEM_SPEC, SEM_SPEC] + [HBM_SPEC] * (2 * n) + [pl.BlockSpec(memory_space=pltpu.VMEM)],
        input_output_aliases={w: 2 + w for w in range(2 * n)},
        compiler_params=pltpu.CompilerParams(has_side_effects=pltpu.SideEffectType.DATAFLOW_SIDE_EFFECTING),
    )(*[pltpu.with_memory_space_constraint(a, pltpu.HBM) for a in both], *after)
    return res[0], res[1], list(res[2:2 + n]), list(res[2 + n:2 + 2 * n]), res[2 + 2 * n]


def rs_wait(ps, rb, ssem, rsem, after, name):
    n = len(ps)
    after = list(after)
    m = len(after)

    def body(*refs):
        ps_i, rb_i = refs[:n], refs[n:2 * n]
        ssem_ref, rsem_ref = refs[2 * n], refs[2 * n + 1]
        for cp in _rs_copies(ps_i, rb_i, ssem_ref, rsem_ref, *_mesh_pos()):
            cp.wait_send()
            cp.wait_recv()

    both = list(ps) + list(rb)
    res = pl.pallas_call(
        body, name=name,
        out_shape=[pltpu.HBM(a.shape, a.dtype) for a in both],
        in_specs=[HBM_SPEC] * (2 * n) + [SEM_SPEC, SEM_SPEC] + [ANY] * m,
        out_specs=[HBM_SPEC] * (2 * n),
        input_output_aliases={w: w for w in range(2 * n)},
        compiler_params=pltpu.CompilerParams(has_side_effects=pltpu.SideEffectType.DATAFLOW_SIDE_EFFECTING),
    )(*both, ssem, rsem, *after)
    return list(res[n:])


def sum_share(parts, kind, R, C, name):
    _, pr, pc = parts.shape
    sr, sc = _shard_shape(kind, R, C)
    tr = _row_tile(pr, pc * 4, budget=4 * 1024 * 1024)
    nt = pr // tr

    def body(p_ref, fin_ref, stage, lsem, ssem, rsem):
        i = pl.program_id(0)
        slot = i % 2
        mx, my, mc = _mesh_pos()

        def region(h, t):
            r0 = pl.multiple_of(t * tr, 8)
            if kind == "col":
                return fin_ref.at[pl.ds(pl.multiple_of(h * pr + r0, 8), tr)]
            return fin_ref.at[pl.ds(r0, tr), pl.ds(h * pc, pc)]

        def copies(s, h, t):
            return (pltpu.make_async_copy(stage.at[s], region(h, t), lsem.at[s]),
                    pltpu.make_async_remote_copy(src_ref=stage.at[s], dst_ref=region(h, t), send_sem=ssem.at[s],
                                                 recv_sem=rsem, device_id=(mx, my, 1 - mc), device_id_type=MESH))

        def wait_sent(s):
            loc, rem = copies(s, 0, 0)
            loc.wait()
            rem.wait_send()

        @pl.when(i >= 2)
        def _():
            wait_sent(slot)

        acc = p_ref[0].astype(F32)
        for k in range(1, 4):
            acc = acc + p_ref[k].astype(F32)
        stage[slot] = acc
        if kind == "col":
            for cp in copies(slot, mc, i):
                cp.start()
        else:
            for h in range(2):
                @pl.when(mc == h)
                def _(h=h):
                    for cp in copies(slot, h, i):
                        cp.start()

        @pl.when(i == nt - 1)
        def _():
            wait_sent(slot)
            if nt >= 2:
                wait_sent(1 - slot)
            half = fin_ref.at[pl.ds(0, pr), pl.ds(0, pc)]
            pltpu.make_async_remote_copy(src_ref=half, dst_ref=half, send_sem=ssem.at[0], recv_sem=rsem,
                                         device_id=(mx, my, 1 - mc), device_id_type=MESH).wait_recv()

    return pl.pallas_call(
        body, name=name, grid=(nt,),
        in_specs=[pl.BlockSpec((4, tr, pc), lambda i: (0, i, 0))],
        out_specs=ANY,
        out_shape=jax.ShapeDtypeStruct((sr, sc), F32),
        scratch_shapes=[pltpu.VMEM((2, tr, pc), F32), pltpu.SemaphoreType.DMA((2,)), pltpu.SemaphoreType.DMA((2,)),
                        pltpu.SemaphoreType.DMA],
        compiler_params=_cp("arbitrary"),
    )(parts)


def _pack(parts, rows):
    flat = []
    for a in parts:
        a = jnp.ravel(a).astype(F32)
        flat.append(jnp.pad(a, (0, (-a.shape[0]) % 128)))
    v = jnp.concatenate(flat)
    return jnp.pad(v, (0, rows * 128 - v.shape[0])).reshape(rows, 128)


def _unpack(block, shapes):
    lead = block.shape[:-2]
    v = block.reshape(lead + (-1,))
    out, off = [], 0
    for shp in shapes:
        n = int(np.prod(shp))
        out.append(v[..., off:off + n].reshape(lead + tuple(shp)))
        off += n + (-n) % 128
    return out


def _block_diag4(w):
    w4 = w.reshape(4, 4, 64, 64)
    eye = jnp.eye(4, dtype=w.dtype)
    return (w4[:, :, :, None, :] * eye[None, :, None, :, None]).reshape(4, 256, 256)


def _diag_blocks(bd):
    b5 = bd.reshape(4, 4, 64, 4, 64)
    return jnp.stack([b5[:, i, :, i, :] for i in range(4)], axis=1).reshape(16, 64, 64)


def _bias_window(rel_bias):
    m = (np.arange(768) + 127) % 768 - 127
    w = rel_bias[:, np.clip(512 - m, -128, 128) + 128]
    win = jnp.tile(w, (1, 128))[:, :128 * 767].reshape(8, 128, 767)[:, :, :WIN]
    qh = np.arange(128)[:, None] // CHUNK
    kc = np.arange(WIN)[None, :] // CHUNK
    valid = (kc >= qh) & (kc <= qh + 8)
    return jnp.where(jnp.asarray(valid)[None], win, NEG)


SMALL = ("b_ada", "norm_pre", "norm_post", "rel_bias", "conv_w", "conv_b", "lru_wa", "lru_ba", "lru_wx",
         "lru_bx", "lru_lambda")
WEIGHTS = ("w_ada", "b_ada", "norm_pre", "norm_post", "ffn1_w_gu", "ffn1_w_down", "w_in", "rel_bias", "conv_w",
           "conv_b", "lru_wa", "lru_ba", "lru_wx", "lru_bx", "lru_lambda", "w_att_o", "w_rec_o", "w_out",
           "ffn2_w_gu", "ffn2_w_down")


def kernel(x, c, w_ada, b_ada, norm_pre, norm_post, ffn1_w_gu, ffn1_w_down, w_in, rel_bias, conv_w, conv_b, lru_wa, lru_ba, lru_wx, lru_bx, lru_lambda, w_att_o, w_rec_o, w_out, ffn2_w_gu, ffn2_w_down, loss_target, m_w_ada, m_b_ada, m_norm_pre, m_norm_post, m_ffn1_w_gu, m_ffn1_w_down, m_w_in, m_rel_bias, m_conv_w, m_conv_b, m_lru_wa, m_lru_ba, m_lru_wx, m_lru_bx, m_lru_lambda, m_w_att_o, m_w_rec_o, m_w_out, m_ffn2_w_gu, m_ffn2_w_down, v_w_ada, v_b_ada, v_norm_pre, v_norm_post, v_ffn1_w_gu, v_ffn1_w_down, v_w_in, v_rel_bias, v_conv_w, v_conv_b, v_lru_wa, v_lru_ba, v_lru_wx, v_lru_bx, v_lru_lambda, v_w_att_o, v_w_rec_o, v_w_out, v_ffn2_w_gu, v_ffn2_w_down):
    W = dict(w_ada=w_ada, b_ada=b_ada, norm_pre=norm_pre, norm_post=norm_post, ffn1_w_gu=ffn1_w_gu,
             ffn1_w_down=ffn1_w_down, w_in=w_in, rel_bias=rel_bias, conv_w=conv_w, conv_b=conv_b, lru_wa=lru_wa,
             lru_ba=lru_ba, lru_wx=lru_wx, lru_bx=lru_bx, lru_lambda=lru_lambda, w_att_o=w_att_o, w_rec_o=w_rec_o,
             w_out=w_out, ffn2_w_gu=ffn2_w_gu, ffn2_w_down=ffn2_w_down)
    M = dict(w_ada=m_w_ada, b_ada=m_b_ada, norm_pre=m_norm_pre, norm_post=m_norm_post, ffn1_w_gu=m_ffn1_w_gu,
             ffn1_w_down=m_ffn1_w_down, w_in=m_w_in, rel_bias=m_rel_bias, conv_w=m_conv_w, conv_b=m_conv_b,
             lru_wa=m_lru_wa, lru_ba=m_lru_ba, lru_wx=m_lru_wx, lru_bx=m_lru_bx, lru_lambda=m_lru_lambda,
             w_att_o=m_w_att_o, w_rec_o=m_w_rec_o, w_out=m_w_out, ffn2_w_gu=m_ffn2_w_gu, ffn2_w_down=m_ffn2_w_down)
    V = dict(w_ada=v_w_ada, b_ada=v_b_ada, norm_pre=v_norm_pre, norm_post=v_norm_post, ffn1_w_gu=v_ffn1_w_gu,
             ffn1_w_down=v_ffn1_w_down, w_in=v_w_in, rel_bias=v_rel_bias, conv_w=v_conv_w, conv_b=v_conv_b,
             lru_wa=v_lru_wa, lru_ba=v_lru_ba, lru_wx=v_lru_wx, lru_bx=v_lru_bx, lru_lambda=v_lru_lambda,
             w_att_o=v_w_att_o, w_rec_o=v_w_rec_o, w_out=v_w_out, ffn2_w_gu=v_ffn2_w_gu, ffn2_w_down=v_ffn2_w_down)
    mx, my, mc = _mesh_pos()
    p = 2 * mx + my
    e = 4 * mx + 2 * my + mc
    xs = x[0]

    c_arr = jnp.reshape(mc, (1,)).astype(jnp.int32)
    cp_arr = jnp.stack([mc, p]).astype(jnp.int32)
    p_arr = jnp.reshape(p, (1,)).astype(jnp.int32)
    geoms = [(kind, R, C) for (_, kind, R, C) in BIG]
    names = [b[0] for b in BIG]
    placed = [ag_local(W[n][0], kind, R, C, p_arr, "ag_local_" + n) for (n, kind, R, C) in BIG]
    f1_s, f1_r, f1_fly, tok0 = ag_start(placed[:2], geoms[:2], [], "ag_start_ffn1")

    def arrived(fly, lo, hi, ssem, rsem, after, tag):
        done = ag_wait(fly, geoms[lo:hi], ssem, rsem, after, "ag_wait_" + tag)
        return [ag_forward(a, kind, R, C, "ag_forward_" + n)
                for a, (kind, R, C), n in zip(done, geoms[lo:hi], names[lo:hi])]

    g1 = ag_small(_pack([c, norm_pre, norm_post, conv_w], 32) + tok0[0:1, 0:1], "ag_small_params")
    c_all, npre4, npost4, cw4 = _unpack(g1, [(D,), (3, 256), (3, 256), (4, 256)])
    chipwise = lambda a: jnp.moveaxis(a[0::2], 0, 1).reshape(a.shape[1], D)
    npre, npost, conv_full = chipwise(npre4), chipwise(npost4), chipwise(cw4)

    b_cols = lax.dynamic_slice(b_ada, (0, p * 2304), (1, 2304))
    mod_cols = ada_fwd(c_all, w_ada[0], b_cols, "ada_fwd")
    g2 = ag_small(mod_cols.reshape(144, 128), "ag_mod")
    mod_all = jnp.moveaxis(g2[0::2].reshape(4, 8, 2304), 0, 1).reshape(8, 9 * D)
    mod = lax.dynamic_index_in_dim(mod_all, e, 0, keepdims=False).reshape(3, 3, D)
    zeros3 = jnp.zeros((3, D), F32)
    vecs = [jnp.concatenate([npre[k:k + 1], npost[k:k + 1], mod[k], zeros3], axis=0) for k in range(3)]

    f1_gu, f1_dn = arrived(f1_fly, 0, 2, f1_s, f1_r, vecs[2], "ffn1")
    mix_s, mix_r, mix_fly, tok1 = ag_start(placed[2:6], geoms[2:6], [f1_gu, f1_dn], "ag_start_mixer")
    ffn_s, ffn_r, ffn_fly, tok2 = ag_start(placed[6:], geoms[6:], [tok1], "ag_start_ffn2")
    wa_bd = _block_diag4(lru_wa[0]).astype(BF16)
    wx_bd = _block_diag4(lru_wx[0]).astype(BF16)
    pvec = jnp.concatenate([conv_full, conv_b, lru_ba, lru_bx, lru_lambda], axis=0)
    bias = _bias_window(rel_bias[0]).reshape(4, 256, WIN)

    x1, h1, g1_, u1, a1, f1 = ffn_fwd(xs, vecs[0] + tok2[0:1, 0:1], f1_gu, f1_dn, 0.5, "ffn1_fwd")
    win, wao, wro, wout = arrived(mix_fly, 2, 6, mix_s, mix_r, x1, "mixer")
    h2, qkv, rest = proj_fwd(x1, vecs[1], win, "proj_fwd")
    ao = attn_fwd(qkv, bias, "attn_fwd")
    hl, hg = lru_fwd(rest, pvec, wa_bd, wx_bd, "lru_fwd")
    x2, att, rec, mg, f2 = mix_out_fwd(x1, ao, hg, rest, vecs[1], wao, wro, wout, "mix_out_fwd")
    f2_gu, f2_dn = arrived(ffn_fly, 6, 8, ffn_s, ffn_r, x2, "ffn2")
    x3, h3, g3_, u3, a3, f3 = ffn_fwd(x2, vecs[2], f2_gu, f2_dn, 0.5, "ffn2_fwd")
    dy, lvec = loss_grad(x3, loss_target[0], "loss_grad")
    loss = lax.psum(lvec[0, 0], ("x", "y", "c"))

    G, grads = {}, {}
    geo = {n: (kind, R, C) for (n, kind, R, C) in BIG}

    def reduce_begin(names, tag):
        ps, rb = [], []
        for n in names:
            got = pair_push(G[n], geo[n][0], c_arr, "rs_push_" + n)
            a, b = pair_add(G[n], got, geo[n][0], cp_arr, "rs_pair_sum_" + n)
            ps.append(a)
            rb.append(b)
        return rs_start(ps, rb, [], "rs_start_" + tag)

    def reduce_end(names, flight, after, tag):
        ssem, rsem, ps, rb, _ = flight
        for a, n in zip(rs_wait(ps, rb, ssem, rsem, after, "rs_wait_" + tag), names):
            grads[n] = sum_share(a, *geo[n], "rs_sum_share_" + n)[None]

    dx2, df3, dgu3, va2 = ffn_bwd(dy, x2, f3, g3_, u3, vecs[2], f2_gu, f2_dn, 0.5, "ffn2_bwd")
    G["ffn2_w_gu"] = mm_tn(h3, dgu3, "dw_ffn2_gu", D, 1408, 1024)
    G["ffn2_w_down"] = mm_tn(a3, df3, "dw_ffn2_down", 1408, D, 1024)
    fly_ffn2 = reduce_begin(("ffn2_w_gu", "ffn2_w_down"), "ffn2")
    vec1 = vecs[1] + fly_ffn2[4][0:1, 0:1]
    df2, d_att, d_rec, dao, dhl, d3, va_out = mix_out_bwd(dx2, f2, att, rec, rest, hl, vec1, wao, wro, wout,
                                                          "mix_out_bwd")
    G["w_out"] = mm_tn(mg, df2, "dw_out", D, D, 1024)
    G["w_att_o"] = mm_tn(ao, d_att, "dw_att_o", 512, D, 1024)
    G["w_rec_o"] = mm_tn(hg, d_rec, "dw_rec_o", D, D, 1024)
    dq, db, dkv = attn_bwd(qkv, dao, bias, "attn_bwd")
    dxr, v_lru, dwa_bd, dwx_bd = lru_bwd(dhl, hl, rest, pvec, wa_bd, wx_bd, "lru_bwd")
    dx1, va_in = proj_bwd(dq, dkv, dxr, d3, win, x1, dx2, vecs[1], "proj_bwd")
    gin = mm_tn(h2, dq, "dw_in_q", D, 512, 1024, n_total=PW)
    gin = mm_tn(h2, dkv, "dw_in_kv", D, 512, 1024, prev=gin, col_off=512, n_total=PW)
    gin = mm_tn(h2, dxr, "dw_in_xr", D, 512, 1024, prev=gin, col_off=1536, n_total=PW)
    G["w_in"] = mm_tn(h2, d3, "dw_in_gates", D, 512, 1024, prev=gin, col_off=2560, n_total=PW)
    fly_mix = reduce_begin(("w_in", "w_att_o", "w_rec_o", "w_out"), "mixer")
    vec0 = vecs[0] + fly_mix[4][0:1, 0:1]
    dx0, df1, dgu1, va0 = ffn_bwd(dx1, xs, f1, g1_, u1, vec0, f1_gu, f1_dn, 0.5, "ffn1_bwd")
    G["ffn1_w_gu"] = mm_tn(h1, dgu1, "dw_ffn1_gu", D, 1408, 1024)
    G["ffn1_w_down"] = mm_tn(a1, df1, "dw_ffn1_down", 1408, D, 1024)
    fly_ffn1 = reduce_begin(("ffn1_w_gu", "ffn1_w_down"), "ffn1")
    reduce_end(("ffn2_w_gu", "ffn2_w_down"), fly_ffn2, [fly_ffn1[4]], "ffn2")
    reduce_end(("w_in", "w_att_o", "w_rec_o", "w_out"), fly_mix, [fly_ffn1[4], grads["ffn2_w_down"]], "mixer")

    va1 = va_out + va_in
    vas = (va0, va1, va2)
    dmod = jnp.stack([v[2:5] for v in vas])
    part = {"b_ada": dmod, "norm_pre": jnp.stack([v[0] for v in vas]), "norm_post": jnp.stack([v[1] for v in vas]),
            "rel_bias": bias_grad(db.reshape(8, 128, WIN), "bias_grad")[:, :257], "conv_w": v_lru[0:4], "conv_b": v_lru[4],
            "lru_wa": _diag_blocks(dwa_bd), "lru_ba": v_lru[5], "lru_wx": _diag_blocks(dwx_bd), "lru_bx": v_lru[6],
            "lru_lambda": v_lru[7]}
    full_shapes = {"b_ada": (9 * D,), "norm_pre": (3, D), "norm_post": (3, D), "rel_bias": (8, 257),
                   "conv_w": (4, D), "conv_b": (D,), "lru_wa": (16, 64, 64), "lru_ba": (D,),
                   "lru_wx": (16, 64, 64), "lru_bx": (D,), "lru_lambda": (D,)}
    g3 = ag_small(_pack([part[n] for n in SMALL], 1232), "ag_small_grads")
    red = dict(zip(SMALL, _unpack(sum_lead(g3, "sum_small_grads"), [full_shapes[n] for n in SMALL])))
    cols = lambda a: lax.dynamic_slice(a, (0, p * 256), (a.shape[0], 256))
    grads.update({"b_ada": red["b_ada"][None], "norm_pre": cols(red["norm_pre"])[None],
                  "norm_post": cols(red["norm_post"])[None], "rel_bias": red["rel_bias"][None],
                  "conv_w": cols(red["conv_w"])[None], "conv_b": red["conv_b"][None], "lru_wa": red["lru_wa"][None],
                  "lru_ba": red["lru_ba"][None], "lru_wx": red["lru_wx"][None], "lru_bx": red["lru_bx"][None],
                  "lru_lambda": red["lru_lambda"][None]})

    dmod_all = g3[:, :72].reshape(8, 9 * D)
    dmod_cols = jnp.pad(lax.dynamic_slice(dmod_all, (0, p * 2304), (8, 2304)), ((0, 120), (0, 0)))
    c_all_t = jnp.pad(c_all.T, ((0, 0), (0, 120)))
    grads["w_ada"] = ada_bwd(c_all_t, dmod_cols, "ada_bwd")[None]

    delta, new_m, new_v = {}, {}, {}

    def update(n):
        shp = W[n].shape
        d_, m_, v_ = adamw(W[n][0], grads[n][0], M[n][0], V[n][0], "adamw_" + n)
        delta[n], new_m[n], new_v[n] = d_.reshape(shp), m_.reshape(shp), v_.reshape(shp)

    for n in ("w_ada", "ffn2_w_gu", "ffn2_w_down", "w_in", "w_att_o", "w_rec_o", "w_out"):
        update(n)
    packed = [_pack([src[n] for n in SMALL], 1168) for src in (W, grads, M, V)]
    outs = adamw(*packed, "adamw_small")
    for dst, blk in zip((delta, new_m, new_v), outs):
        for n, a in zip(SMALL, _unpack(blk, [W[n].shape for n in SMALL])):
            dst[n] = a
    reduce_end(("ffn1_w_gu", "ffn1_w_down"), fly_ffn1,
               [outs[0], delta["w_ada"], delta["ffn2_w_gu"], delta["ffn2_w_down"], delta["w_in"], delta["w_out"]], "ffn1")
    for n in ("ffn1_w_gu", "ffn1_w_down"):
        update(n)

    return (loss, dx0[None], *[grads[n] for n in WEIGHTS], *[delta[n] for n in WEIGHTS],
            *[new_m[n] for n in WEIGHTS], *[new_v[n] for n in WEIGHTS])
```

```python
import functools

import numpy as np
import jax
import jax.numpy as jnp
from jax import lax
from jax.experimental import pallas as pl
from jax.experimental.pallas import tpu as pltpu

F32 = jnp.float32
BF16 = jnp.bfloat16

D = 1024
FF = 2816
PW = 5632
HP = 128
CHUNK = 64
WIN = 640
TQ = 512
EPS = 1e-6
NEG = -1e30
LRU_C = 8.0
N_DEV = 8
VMEM_LIMIT = 50 * 1024 * 1024

ADAM_LR, ADAM_B1, ADAM_B2, ADAM_EPS, ADAM_WD, ADAM_STEP = 0.001, 0.9, 0.999, 1e-08, 0.01, 10

MESH = pl.DeviceIdType.MESH
ANY = pl.BlockSpec(memory_space=pl.ANY)


def _cp(*sem):
    return pltpu.CompilerParams(dimension_semantics=tuple(sem), vmem_limit_bytes=VMEM_LIMIT)


def _dot(a, b):
    return jnp.dot(a, b, preferred_element_type=F32)


def _dot_nt(a, b):
    return lax.dot_general(a, b, (((1,), (1,)), ((), ())), preferred_element_type=F32)


def _dot_tn(a, b):
    return lax.dot_general(a, b, (((0,), (0,)), ((), ())), preferred_element_type=F32)


def _mean(v):
    return jnp.mean(v, axis=-1, keepdims=True)


def _colsum(v):
    return jnp.sum(v, axis=0, keepdims=True)


def _sigmoid(v):
    return jax.nn.sigmoid(v)


def _expm1(v):
    small = v * (1.0 + v * 0.5 * (1.0 + v * (1.0 / 3.0) * (1.0 + v * 0.25 * (1.0 + v * 0.2 * (
        1.0 + v * (1.0 / 6.0) * (1.0 + v * (1.0 / 7.0)))))))
    return jnp.where(jnp.abs(v) < 0.25, small, jnp.exp(v) - 1.0)


_GK = 0.7978845608028654


def _gelu(v):
    t = jnp.tanh(_GK * (v + 0.044715 * v * v * v))
    return 0.5 * v * (1.0 + t)


def _gelu_grad(v):
    t = jnp.tanh(_GK * (v + 0.044715 * v * v * v))
    return 0.5 * (1.0 + t) + 0.5 * v * (1.0 - t * t) * _GK * (1.0 + 3.0 * 0.044715 * v * v)


def _pre_norm(xv, vec_ref):
    r = lax.rsqrt(_mean(xv * xv) + EPS)
    n = xv * r * vec_ref[0:1, :]
    return n * (1.0 + vec_ref[3:4, :]) + vec_ref[2:3, :]


def _pre_norm_bwd(dh, xv, dres, vec_ref, vacc_ref):
    r = lax.rsqrt(_mean(xv * xv) + EPS)
    xh = xv * r
    n = xh * vec_ref[0:1, :]
    vacc_ref[2:3, :] += _colsum(dh)
    vacc_ref[3:4, :] += _colsum(dh * n)
    dn = dh * (1.0 + vec_ref[3:4, :])
    vacc_ref[0:1, :] += _colsum(dn * xh)
    dxh = dn * vec_ref[0:1, :]
    return r * (dxh - xh * _mean(dxh * xh)) + dres


def _post_norm_bwd(dxo, fv, res, vec_ref, vacc_ref):
    rf = lax.rsqrt(_mean(fv * fv) + EPS)
    fh = fv * rf
    gp = vec_ref[1:2, :]
    vacc_ref[4:5, :] += _colsum(res * dxo * (fh * gp))
    dy = (res * vec_ref[4:5, :]) * dxo
    vacc_ref[1:2, :] += _colsum(dy * fh)
    dfn = dy * gp
    return rf * (dfn - fh * _mean(dfn * fh))


def ffn_fwd(x, vec, w_gu, w_dn, res, name, tm=1024, tf=256):
    S = x.shape[0]
    tm = min(tm, S)
    nf = FF // tf

    def body(x_ref, vec_ref, wg_ref, wu_ref, wd_ref, xo_ref, h_ref, g_ref, u_ref, a_ref, f_ref, hs, acc):
        j = pl.program_id(1)

        @pl.when(j == 0)
        def _():
            h = _pre_norm(x_ref[...], vec_ref).astype(BF16)
            hs[...] = h
            h_ref[...] = h
            acc[...] = jnp.zeros_like(acc)

        h = hs[...]
        g = _dot(h, wg_ref[...])
        u = _dot(h, wu_ref[...])
        g_ref[...] = g.astype(BF16)
        u_ref[...] = u.astype(BF16)
        a = (g * _sigmoid(g) * u).astype(BF16)
        a_ref[...] = a
        acc[...] += _dot(a, wd_ref[...])

        @pl.when(j == nf - 1)
        def _():
            f = acc[...]
            f_ref[...] = f
            y = f * lax.rsqrt(_mean(f * f) + EPS) * vec_ref[1:2, :]
            xo_ref[...] = x_ref[...] + (res * vec_ref[4:5, :]) * y

    row = lambda i, j: (i, 0)
    return pl.pallas_call(
        body, name=name, grid=(S // tm, nf),
        in_specs=[pl.BlockSpec((tm, D), row), pl.BlockSpec((8, D), lambda i, j: (0, 0)),
                  pl.BlockSpec((D, tf), lambda i, j: (0, j)), pl.BlockSpec((D, tf), lambda i, j: (0, j + nf)),
                  pl.BlockSpec((tf, D), lambda i, j: (j, 0))],
        out_specs=[pl.BlockSpec((tm, D), row), pl.BlockSpec((tm, D), row),
                   pl.BlockSpec((tm, tf), lambda i, j: (i, j)), pl.BlockSpec((tm, tf), lambda i, j: (i, j)),
                   pl.BlockSpec((tm, tf), lambda i, j: (i, j)), pl.BlockSpec((tm, D), row)],
        out_shape=[jax.ShapeDtypeStruct((S, D), F32), jax.ShapeDtypeStruct((S, D), BF16),
                   jax.ShapeDtypeStruct((S, FF), BF16), jax.ShapeDtypeStruct((S, FF), BF16),
                   jax.ShapeDtypeStruct((S, FF), BF16), jax.ShapeDtypeStruct((S, D), F32)],
        scratch_shapes=[pltpu.VMEM((tm, D), BF16), pltpu.VMEM((tm, D), F32)],
        compiler_params=_cp("parallel", "arbitrary"),
    )(x, vec, w_gu, w_gu, w_dn)


def ffn_bwd(dxo, x, f, g, u, vec, w_gu, w_u, w_dn, res, name, tm=512, tf=512):
    S = x.shape[0]
    tm = min(tm, S)
    nf = -(-FF // tf)
    tail = FF - tf * (nf - 1)

    def body(dxo_ref, x_ref, f_ref, g_ref, u_ref, vec_ref, wg_ref, wu_ref, wd_ref,
             dx_ref, df_ref, dgu_ref, vacc_ref, dfs, acc):
        i, j = pl.program_id(0), pl.program_id(1)

        @pl.when((i == 0) & (j == 0))
        def _():
            vacc_ref[...] = jnp.zeros_like(vacc_ref)

        @pl.when(j == 0)
        def _():
            df = _post_norm_bwd(dxo_ref[...], f_ref[...], res, vec_ref, vacc_ref).astype(BF16)
            dfs[...] = df
            df_ref[...] = df
            acc[...] = jnp.zeros_like(acc)

        def chunk(w):
            da = _dot_nt(dfs[...], wd_ref[0:w, :])
            gv, uv = g_ref[:, 0:w].astype(F32), u_ref[:, 0:w].astype(F32)
            sg = _sigmoid(gv)
            dg = (da * uv * (sg * (1.0 + gv * (1.0 - sg)))).astype(BF16)
            du = (da * (gv * sg)).astype(BF16)
            dgu_ref[0, :, 0:w] = dg
            dgu_ref[1, :, 0:w] = du
            acc[...] += _dot_nt(dg, wg_ref[:, 0:w]) + _dot_nt(du, wu_ref[:, 0:w])

        @pl.when(j < nf - 1)
        def _():
            chunk(tf)

        @pl.when(j == nf - 1)
        def _():
            chunk(tail)
            dx_ref[...] = _pre_norm_bwd(acc[...], x_ref[...], dxo_ref[...], vec_ref, vacc_ref)

    row = lambda i, j: (i, 0)
    col = lambda i, j: (i, j)
    return pl.pallas_call(
        body, name=name, grid=(S // tm, nf),
        in_specs=[pl.BlockSpec((tm, D), row), pl.BlockSpec((tm, D), row), pl.BlockSpec((tm, D), row),
                  pl.BlockSpec((tm, tf), col), pl.BlockSpec((tm, tf), col),
                  pl.BlockSpec((8, D), lambda i, j: (0, 0)),
                  pl.BlockSpec((D, tf), lambda i, j: (0, j)), pl.BlockSpec((D, tf), lambda i, j: (0, j)),
                  pl.BlockSpec((tf, D), lambda i, j: (j, 0))],
        out_specs=[pl.BlockSpec((tm, D), row), pl.BlockSpec((tm, D), row),
                   pl.BlockSpec((2, tm, tf), lambda i, j: (0, i, j)),
                   pl.BlockSpec((8, D), lambda i, j: (0, 0))],
        out_shape=[jax.ShapeDtypeStruct((S, D), F32), jax.ShapeDtypeStruct((S, D), BF16),
                   jax.ShapeDtypeStruct((2, S, FF), BF16), jax.ShapeDtypeStruct((8, D), F32)],
        scratch_shapes=[pltpu.VMEM((tm, D), BF16), pltpu.VMEM((tm, D), F32)],
        compiler_params=_cp("arbitrary", "arbitrary"),
    )(dxo, x, f, g, u, vec, w_gu, w_u, w_dn)


def mm_tn(a, b, name, tm, tn, tk, out_dtype=BF16, prev=None, col_off=0, n_total=None):
    S, M = a.shape
    if b.ndim == 3:
        G, _, Nf = b.shape
    else:
        G, Nf = 1, b.shape[1]
    N = G * Nf
    n_total = N if n_total is None else n_total
    tk = min(tk, S)
    nbf = Nf // tn
    nk = S // tk
    ob = col_off // tn

    def body(*refs):
        a_ref, b_ref = refs[0], refs[1]
        o_ref, acc = refs[-2], refs[-1]
        k = pl.program_id(2)

        @pl.when(k == 0)
        def _():
            acc[...] = jnp.zeros_like(acc)

        acc[...] += _dot_tn(a_ref[...], b_ref[...])

        @pl.when(k == nk - 1)
        def _():
            o_ref[...] = acc[...].astype(out_dtype)

    if b.ndim == 3:
        b_spec = pl.BlockSpec((None, tk, tn), lambda i, j, k: (j // nbf, k, j % nbf))
    else:
        b_spec = pl.BlockSpec((tk, tn), lambda i, j, k: (k, j))
    in_specs = [pl.BlockSpec((tk, tm), lambda i, j, k: (k, i)), b_spec]
    args = [a, b]
    aliases = {}
    if prev is not None:
        in_specs.append(ANY)
        args.append(prev)
        aliases = {2: 0}
    return pl.pallas_call(
        body, name=name, grid=(M // tm, N // tn, nk),
        in_specs=in_specs,
        out_specs=pl.BlockSpec((tm, tn), lambda i, j, k: (i, j + ob)),
        out_shape=jax.ShapeDtypeStruct((M, n_total), out_dtype),
        scratch_shapes=[pltpu.VMEM((tm, tn), F32)],
        input_output_aliases=aliases,
        compiler_params=_cp("parallel", "parallel", "arbitrary"),
    )(*args)


def proj_fwd(x, vec, w_in, name, tm=1024, tn=512):
    S = x.shape[0]
    tm = min(tm, S)
    nq = 1536 // tn

    def body(x_ref, vec_ref, w_ref, h_ref, qkv_ref, rest_ref, hs):
        j = pl.program_id(1)

        @pl.when(j == 0)
        def _():
            h = _pre_norm(x_ref[...], vec_ref).astype(BF16)
            hs[...] = h
            h_ref[...] = h

        r = _dot(hs[...], w_ref[...])

        @pl.when(j < nq)
        def _():
            qkv_ref[...] = r.astype(BF16)

        @pl.when(j >= nq)
        def _():
            rest_ref[...] = r

    row = lambda i, j: (i, 0)
    return pl.pallas_call(
        body, name=name, grid=(S // tm, PW // tn),
        in_specs=[pl.BlockSpec((tm, D), row), pl.BlockSpec((8, D), lambda i, j: (0, 0)),
                  pl.BlockSpec((D, tn), lambda i, j: (0, j))],
        out_specs=[pl.BlockSpec((tm, D), row),
                   pl.BlockSpec((tm, tn), lambda i, j: (i, jnp.minimum(j, nq - 1))),
                   pl.BlockSpec((tm, tn), lambda i, j: (i, jnp.maximum(j - nq, 0)))],
        out_shape=[jax.ShapeDtypeStruct((S, D), BF16), jax.ShapeDtypeStruct((S, 1536), BF16),
                   jax.ShapeDtypeStruct((S, 4096), F32)],
        scratch_shapes=[pltpu.VMEM((tm, D), BF16)],
        compiler_params=_cp("parallel", "arbitrary"),
    )(x, vec, w_in)


def proj_bwd(dq, dkv, dxr, d3, w_in, x, dxo, vec, name, tm=1024, tk=512):
    S = x.shape[0]
    tm = min(tm, S)
    nk = PW // tk

    def body(dq_ref, dkv_ref, dxr_ref, d3_ref, w_ref, x_ref, dxo_ref, vec_ref, dx_ref, vacc_ref, acc):
        i, j = pl.program_id(0), pl.program_id(1)

        @pl.when((i == 0) & (j == 0))
        def _():
            vacc_ref[...] = jnp.zeros_like(vacc_ref)

        @pl.when(j == 0)
        def _():
            acc[...] = _dot_nt(dq_ref[...], w_ref[...])

        @pl.when((j >= 1) & (j < 3))
        def _():
            acc[...] += _dot_nt(dkv_ref[...], w_ref[...])

        @pl.when((j >= 3) & (j < 5))
        def _():
            acc[...] += _dot_nt(dxr_ref[...], w_ref[...])

        @pl.when(j >= 5)
        def _():
            acc[...] += _dot_nt(d3_ref[...], w_ref[...])

        @pl.when(j == nk - 1)
        def _():
            dx_ref[...] = _pre_norm_bwd(acc[...], x_ref[...], dxo_ref[...], vec_ref, vacc_ref)

    row = lambda i, j: (i, 0)
    return pl.pallas_call(
        body, name=name, grid=(S // tm, nk),
        in_specs=[pl.BlockSpec((None, tm, tk), lambda i, j: (0, i, 0)),
                  pl.BlockSpec((None, tm, tk), lambda i, j: (jnp.clip(j - 1, 0, 1), i, 0)),
                  pl.BlockSpec((tm, tk), lambda i, j: (i, jnp.clip(j - 3, 0, 1))),
                  pl.BlockSpec((None, tm, tk), lambda i, j: (jnp.clip(j - 5, 0, 5) // 2, i, jnp.clip(j - 5, 0, 5) % 2)),
                  pl.BlockSpec((D, tk), lambda i, j: (0, j)),
                  pl.BlockSpec((tm, D), row), pl.BlockSpec((tm, D), row),
                  pl.BlockSpec((8, D), lambda i, j: (0, 0))],
        out_specs=[pl.BlockSpec((tm, D), row), pl.BlockSpec((8, D), lambda i, j: (0, 0))],
        out_shape=[jax.ShapeDtypeStruct((S, D), F32), jax.ShapeDtypeStruct((8, D), F32)],
        scratch_shapes=[pltpu.VMEM((tm, D), F32)],
        compiler_params=_cp("arbitrary", "arbitrary"),
    )(dq, dkv, dxr, d3, w_in, x, dxo, vec)


def _two_heads(v, lane):
    zero = jnp.zeros((), v.dtype)
    return jnp.concatenate([jnp.where(lane < 64, v, zero), jnp.where(lane >= 64, v, zero)], axis=0)


def _attn_probs(qm, ka, bias_h, i, grp):
    s = _dot_nt(qm, ka) + bias_h
    col = lax.broadcasted_iota(jnp.int32, s.shape, 1)
    first_key = jnp.where(i == 0, 512 - 128 * grp, 0)
    s = jnp.where(col >= first_key, s, NEG)
    e = jnp.exp(s - jnp.max(s, axis=-1, keepdims=True))
    return e / jnp.sum(e, axis=-1, keepdims=True)


def attn_fwd(qkv, bias, name):
    S = qkv.shape[0]
    nb = S // TQ

    def body(q_ref, kp_ref, kc_ref, vp_ref, vc_ref, b_ref, o_ref, kw, vw):
        i = pl.program_id(1)
        kw[0:TQ, :] = kp_ref[...]
        kw[TQ:2 * TQ, :] = kc_ref[...]
        vw[0:TQ, :] = vp_ref[...]
        vw[TQ:2 * TQ, :] = vc_ref[...]
        lane = lax.broadcasted_iota(jnp.int32, (1, HP), 1)

        def group(a, carry):
            r0 = pl.multiple_of(a * 128, 128)
            qa = q_ref[pl.ds(r0, 128), :] * jnp.asarray(0.125, BF16)
            ka = kw[pl.ds(r0, WIN), :]
            va = vw[pl.ds(r0, WIN), :]
            p = _attn_probs(_two_heads(qa, lane), ka, b_ref[...], i, a)
            o2 = _dot(p.astype(BF16), va)
            o_ref[pl.ds(r0, 128), :] = jnp.where(lane < 64, o2[0:128], o2[128:256]).astype(BF16)
            return carry

        lax.fori_loop(0, TQ // 128, group, 0, unroll=True)

    prev = lambda h, i: (jnp.maximum(i - 1, 0), 0)
    return pl.pallas_call(
        body, name=name, grid=(4, nb),
        in_specs=[pl.BlockSpec((TQ, HP), lambda h, i: (i, h)),
                  pl.BlockSpec((TQ, HP), lambda h, i: (jnp.maximum(i - 1, 0), 4 + h)),
                  pl.BlockSpec((TQ, HP), lambda h, i: (i, 4 + h)),
                  pl.BlockSpec((TQ, HP), lambda h, i: (jnp.maximum(i - 1, 0), 8 + h)),
                  pl.BlockSpec((TQ, HP), lambda h, i: (i, 8 + h)),
                  pl.BlockSpec((None, 256, WIN), lambda h, i: (h, 0, 0))],
        out_specs=pl.BlockSpec((TQ, HP), lambda h, i: (i, h)),
        out_shape=jax.ShapeDtypeStruct((S, 512), BF16),
        scratch_shapes=[pltpu.VMEM((2 * TQ, HP), BF16), pltpu.VMEM((2 * TQ, HP), BF16)],
        compiler_params=_cp("parallel", "arbitrary"),
    )(qkv, qkv, qkv, qkv, qkv, bias)


def attn_bwd(qkv, do, bias, name):
    S = qkv.shape[0]
    nb = S // TQ

    def body(q_ref, kp_ref, kc_ref, vp_ref, vc_ref, do_ref, b_ref, dqkv_ref, db_ref, dkv_ref, kw, vw, ak, av):
        i = pl.program_id(1)

        @pl.when(i == 0)
        def _():
            db_ref[...] = jnp.zeros_like(db_ref)
            ak[...] = jnp.zeros_like(ak)
            av[...] = jnp.zeros_like(av)

        @pl.when(i > 0)
        def _():
            ak[0:TQ, :] = ak[TQ:2 * TQ, :]
            av[0:TQ, :] = av[TQ:2 * TQ, :]
            ak[TQ:2 * TQ, :] = jnp.zeros((TQ, HP), F32)
            av[TQ:2 * TQ, :] = jnp.zeros((TQ, HP), F32)

        @pl.when(i < nb)
        def _():
            kw[0:TQ, :] = kp_ref[...]
            kw[TQ:2 * TQ, :] = kc_ref[...]
            vw[0:TQ, :] = vp_ref[...]
            vw[TQ:2 * TQ, :] = vc_ref[...]
            lane = lax.broadcasted_iota(jnp.int32, (1, HP), 1)

            def group(a, carry):
                r0 = pl.multiple_of(a * 128, 128)
                q2 = _two_heads(q_ref[pl.ds(r0, 128), :] * jnp.asarray(0.125, BF16), lane)
                do2 = _two_heads(do_ref[pl.ds(r0, 128), :], lane)
                ka = kw[pl.ds(r0, WIN), :]
                va = vw[pl.ds(r0, WIN), :]
                p = _attn_probs(q2, ka, b_ref[...], i, a)
                dp = _dot_nt(do2, va)
                ds = p * (dp - jnp.sum(p * dp, axis=-1, keepdims=True))
                db_ref[...] += ds
                dsb = ds.astype(BF16)
                dq2 = _dot(dsb, ka)
                ak[pl.ds(r0, WIN), :] += _dot_tn(dsb, q2)
                av[pl.ds(r0, WIN), :] += _dot_tn(p.astype(BF16), do2)
                dq = jnp.where(lane < 64, dq2[0:128], dq2[128:256])
                dqkv_ref[0, pl.ds(r0, 128), :] = (dq * 0.125).astype(BF16)
                return carry

            lax.fori_loop(0, TQ // 128, group, 0, unroll=True)

        @pl.when(i > 0)
        def _():
            dkv_ref[0] = ak[0:TQ, :].astype(BF16)
            dkv_ref[1] = av[0:TQ, :].astype(BF16)

    cur = lambda i: jnp.minimum(i, nb - 1)
    prv = lambda i: jnp.clip(i - 1, 0, nb - 1)
    dq, db, dkv = pl.pallas_call(
        body, name=name, grid=(4, nb + 1),
        in_specs=[pl.BlockSpec((TQ, HP), lambda h, i: (cur(i), h)),
                  pl.BlockSpec((TQ, HP), lambda h, i: (prv(i), 4 + h)),
                  pl.BlockSpec((TQ, HP), lambda h, i: (cur(i), 4 + h)),
                  pl.BlockSpec((TQ, HP), lambda h, i: (prv(i), 8 + h)),
                  pl.BlockSpec((TQ, HP), lambda h, i: (cur(i), 8 + h)),
                  pl.BlockSpec((TQ, HP), lambda h, i: (cur(i), h)),
                  pl.BlockSpec((None, 256, WIN), lambda h, i: (h, 0, 0))],
        out_specs=[pl.BlockSpec((1, TQ, HP), lambda h, i: (0, cur(i), h)),
                   pl.BlockSpec((None, 256, WIN), lambda h, i: (h, 0, 0)),
                   pl.BlockSpec((2, TQ, HP), lambda h, i: (0, prv(i), h))],
        out_shape=[jax.ShapeDtypeStruct((1, S, 512), BF16), jax.ShapeDtypeStruct((4, 256, WIN), F32),
                   jax.ShapeDtypeStruct((2, S, 512), BF16)],
        scratch_shapes=[pltpu.VMEM((2 * TQ, HP), BF16), pltpu.VMEM((2 * TQ, HP), BF16),
                        pltpu.VMEM((2 * TQ, HP), F32), pltpu.VMEM((2 * TQ, HP), F32)],
        compiler_params=_cp("parallel", "arbitrary"),
    )(qkv, qkv, qkv, qkv, qkv, do, bias)
    return dq, db, dkv


def bias_grad(db, name):
    def body(db_ref, o_ref):
        r = lax.broadcasted_iota(jnp.int32, (128, 128), 0)
        c = lax.broadcasted_iota(jnp.int32, (128, 128), 1)
        flip = (r + c == 127).astype(BF16)
        lane = lax.broadcasted_iota(jnp.int32, (16, 384), 1)
        src = lax.broadcasted_iota(jnp.int32, (128, 384), 0)
        dst = lax.broadcasted_iota(jnp.int32, (128, 384), 1)

        def split_dot(v, m):
            hi = v.astype(BF16)
            r1 = v - hi.astype(F32)
            mid = r1.astype(BF16)
            lo = (r1 - mid.astype(F32)).astype(BF16)
            return _dot(hi, m) + _dot(mid, m) + _dot(lo, m)

        def diag_sums(w):
            y = pltpu.roll(split_dot(w, flip), 0, 1, stride=1, stride_axis=0)
            return jnp.broadcast_to(_colsum(y), (16, 128))

        w4 = db_ref[0, :, 512:640]
        w3 = db_ref[0, :, 384:512]
        far = jnp.sum(db_ref[0, :, 0:384]) + jnp.sum(jnp.where(r >= c, w3, 0.0))
        lo4 = diag_sums(jnp.where(r >= c, w4, 0.0))
        up4 = diag_sums(jnp.where(r < c, w4, 0.0))
        up3 = diag_sums(jnp.where(r < c, w3, 0.0))
        p_lo4 = (dst == 128 + (src + 1) % 128).astype(BF16)
        p_up4 = ((dst == src + 1) & (src < 127)).astype(BF16)
        p_up3 = ((dst == src + 129) & (src < 127)).astype(BF16)
        out = split_dot(lo4, p_lo4) + split_dot(up4, p_up4) + split_dot(up3, p_up3)
        o_ref[0] = out + jnp.where(lane == 256, far, 0.0)

    return pl.pallas_call(
        body, name=name, grid=(8,),
        in_specs=[pl.BlockSpec((1, 128, WIN), lambda h: (h, 0, 0))],
        out_specs=pl.BlockSpec((1, 16, 384), lambda h: (h, 0, 0)),
        out_shape=jax.ShapeDtypeStruct((8, 16, 384), F32),
        compiler_params=_cp("parallel"),
    )(db)[:, 0, :]


LT = 256
LC = 512


def _lru_gates(xs, pv_ref, wa_ref, wx_ref, tl):
    xc = (pv_ref[4:5, :] + pv_ref[3:4, :] * xs[pl.ds(8, tl), :] + pv_ref[2:3, :] * xs[pl.ds(7, tl), :]
          + pv_ref[1:2, :] * xs[pl.ds(6, tl), :] + pv_ref[0:1, :] * xs[pl.ds(5, tl), :])
    xcb = xc.astype(BF16)
    pa = jnp.concatenate([_dot(xcb[:, 0:256], wa_ref[0]), _dot(xcb[:, 256:512], wa_ref[1])], axis=1)
    px = jnp.concatenate([_dot(xcb[:, 0:256], wx_ref[0]), _dot(xcb[:, 256:512], wx_ref[1])], axis=1)
    r = _sigmoid(pa + pv_ref[5:6, :])
    ig = _sigmoid(px + pv_ref[6:7, :])
    z = -pv_ref[7:8, :]
    sp = jnp.maximum(z, 0.0) + jnp.log1p(jnp.exp(-jnp.abs(z)))
    log_a = (-LRU_C * r) * sp
    a = jnp.exp(log_a)
    mult = jnp.sqrt(-_expm1(2.0 * log_a))
    return xc, xcb, r, ig, sp, a, mult


def lru_fwd(rest, pvec, wa, wx, name):
    S = rest.shape[0]
    tl = min(LT, S)
    nt = S // tl

    def body(xr_ref, halo_ref, yr_ref, pv_ref, wa_ref, wx_ref, h_ref, hg_ref, xs, a_s, u_s, h_s, carry):
        ti = pl.program_id(1)

        @pl.when(ti == 0)
        def _():
            carry[...] = jnp.zeros_like(carry)

        xs[0:8, :] = jnp.where(ti > 0, halo_ref[...], 0.0)
        xs[pl.ds(8, tl), :] = xr_ref[...]
        xc, _, _, ig, _, a, mult = _lru_gates(xs, pv_ref, wa_ref, wx_ref, tl)
        a_s[...] = a
        u_s[...] = mult * (ig * xc)
        row = lax.broadcasted_iota(jnp.int32, (8, LC), 0)

        def blk(bi, c):
            o = pl.multiple_of(bi * 8, 8)
            av = a_s[pl.ds(o, 8), :]
            bv = u_s[pl.ds(o, 8), :]
            for d in (1, 2, 4):
                a_sh = pltpu.roll(av, d, 0)
                b_sh = pltpu.roll(bv, d, 0)
                m = row >= d
                bv = jnp.where(m, av * b_sh + bv, bv)
                av = jnp.where(m, av * a_sh, av)
            hv = bv + av * c
            h_s[pl.ds(o, 8), :] = hv
            return hv[7:8, :]

        carry[...] = lax.fori_loop(0, tl // 8, blk, carry[...])
        h = h_s[...]
        h_ref[...] = h
        hg_ref[...] = (h * _gelu(yr_ref[...])).astype(BF16)

    hb = tl // 8
    return pl.pallas_call(
        body, name=name, grid=(2, nt),
        in_specs=[pl.BlockSpec((tl, LC), lambda c, t: (t, c)),
                  pl.BlockSpec((8, LC), lambda c, t: (jnp.maximum(t * hb - 1, 0), c)),
                  pl.BlockSpec((tl, LC), lambda c, t: (t, 2 + c)),
                  pl.BlockSpec((8, LC), lambda c, t: (0, c)),
                  pl.BlockSpec((2, 256, 256), lambda c, t: (c, 0, 0)),
                  pl.BlockSpec((2, 256, 256), lambda c, t: (c, 0, 0))],
        out_specs=[pl.BlockSpec((tl, LC), lambda c, t: (t, c)), pl.BlockSpec((tl, LC), lambda c, t: (t, c))],
        out_shape=[jax.ShapeDtypeStruct((S, D), F32), jax.ShapeDtypeStruct((S, D), BF16)],
        scratch_shapes=[pltpu.VMEM((tl + 8, LC), F32), pltpu.VMEM((tl, LC), F32), pltpu.VMEM((tl, LC), F32),
                        pltpu.VMEM((tl, LC), F32), pltpu.VMEM((1, LC), F32)],
        compiler_params=_cp("parallel", "arbitrary"),
    )(rest, rest, rest, pvec, wa, wx)


def lru_bwd(dh, h, rest, pvec, wa, wx, name):
    S = rest.shape[0]
    tl = min(LT, S)
    nt = S // tl

    def body(dh_ref, h_ref, hhalo_ref, xr_ref, xhalo_ref, pv_ref, wa_ref, wx_ref,
             dxr_ref, vacc_ref, dwa_ref, dwx_ref,
             xs, hs, a_s, ash_s, b_s, lam_s, dxe, anext, lnext, dxnext):
        ti = pl.program_id(1)
        tr = nt - 1 - ti

        @pl.when(ti == 0)
        def _():
            anext[...] = jnp.zeros_like(anext)
            lnext[...] = jnp.zeros_like(lnext)
            dxnext[...] = jnp.zeros_like(dxnext)
            vacc_ref[...] = jnp.zeros_like(vacc_ref)
            dwa_ref[...] = jnp.zeros_like(dwa_ref)
            dwx_ref[...] = jnp.zeros_like(dwx_ref)

        xs[0:8, :] = jnp.where(tr > 0, xhalo_ref[...], 0.0)
        xs[pl.ds(8, tl), :] = xr_ref[...]
        xc, xcb, r, ig, sp, a, mult = _lru_gates(xs, pv_ref, wa_ref, wx_ref, tl)

        a_s[pl.ds(0, tl), :] = a
        a_s[pl.ds(tl, 8), :] = jnp.broadcast_to(anext[...], (8, LC))
        ash_s[...] = a_s[pl.ds(1, tl), :]
        b_s[...] = dh_ref[...]
        row = lax.broadcasted_iota(jnp.int32, (8, LC), 0)

        def blk(k, c):
            o = pl.multiple_of((tl // 8 - 1 - k) * 8, 8)
            av = ash_s[pl.ds(o, 8), :]
            bv = b_s[pl.ds(o, 8), :]
            for d in (1, 2, 4):
                a_sh = pltpu.roll(av, 8 - d, 0)
                b_sh = pltpu.roll(bv, 8 - d, 0)
                m = row < 8 - d
                bv = jnp.where(m, bv + av * b_sh, bv)
                av = jnp.where(m, av * a_sh, av)
            lv = bv + av * c
            lam_s[pl.ds(o, 8), :] = lv
            return lv[0:1, :]

        lnext[...] = lax.fori_loop(0, tl // 8, blk, lnext[...])
        anext[...] = a[0:1, :]
        lam = lam_s[...]

        hs[0:8, :] = jnp.where(tr > 0, hhalo_ref[...], 0.0)
        hs[pl.ds(8, tl), :] = h_ref[...]
        d_a = lam * hs[pl.ds(7, tl), :]
        d_mult = lam * (ig * xc)
        d_ig = lam * mult * xc
        dxc = lam * mult * ig
        d_log_a = d_a * a - d_mult * (a * a) / mult
        d_r = d_log_a * (-LRU_C * sp)
        vacc_ref[7:8, :] += _colsum(d_log_a * (-LRU_C * r)) * (-_sigmoid(-pv_ref[7:8, :]))
        d_pa = d_r * r * (1.0 - r)
        d_px = d_ig * ig * (1.0 - ig)
        vacc_ref[5:6, :] += _colsum(d_pa)
        vacc_ref[6:7, :] += _colsum(d_px)
        dpa = d_pa.astype(BF16)
        dpx = d_px.astype(BF16)
        back = []
        for g in range(2):
            sl = slice(256 * g, 256 * g + 256)
            dwa_ref[g] += _dot_tn(xcb[:, sl], dpa[:, sl])
            dwx_ref[g] += _dot_tn(xcb[:, sl], dpx[:, sl])
            back.append(_dot_nt(dpa[:, sl], wa_ref[g]) + _dot_nt(dpx[:, sl], wx_ref[g]))
        dxc = dxc + jnp.concatenate(back, axis=1)
        vacc_ref[4:5, :] += _colsum(dxc)
        for k in range(4):
            vacc_ref[k:k + 1, :] += _colsum(dxc * xs[pl.ds(5 + k, tl), :])
        dxe[pl.ds(0, tl), :] = dxc
        dxe[pl.ds(tl, 8), :] = dxnext[...]
        dxr = (pv_ref[3:4, :] * dxc + pv_ref[2:3, :] * dxe[pl.ds(1, tl), :]
               + pv_ref[1:2, :] * dxe[pl.ds(2, tl), :] + pv_ref[0:1, :] * dxe[pl.ds(3, tl), :])
        dxr_ref[...] = dxr.astype(BF16)
        dxnext[...] = dxc[0:8, :]

    hb = tl // 8
    rev = lambda t: nt - 1 - t
    halo = lambda t: jnp.maximum(rev(t) * hb - 1, 0)
    big = lambda: pltpu.VMEM((tl + 8, LC), F32)
    til = lambda: pltpu.VMEM((tl, LC), F32)
    return pl.pallas_call(
        body, name=name, grid=(2, nt),
        in_specs=[pl.BlockSpec((tl, LC), lambda c, t: (rev(t), c)),
                  pl.BlockSpec((tl, LC), lambda c, t: (rev(t), c)),
                  pl.BlockSpec((8, LC), lambda c, t: (halo(t), c)),
                  pl.BlockSpec((tl, LC), lambda c, t: (rev(t), c)),
                  pl.BlockSpec((8, LC), lambda c, t: (halo(t), c)),
                  pl.BlockSpec((8, LC), lambda c, t: (0, c)),
                  pl.BlockSpec((2, 256, 256), lambda c, t: (c, 0, 0)),
                  pl.BlockSpec((2, 256, 256), lambda c, t: (c, 0, 0))],
        out_specs=[pl.BlockSpec((tl, LC), lambda c, t: (rev(t), c)),
                   pl.BlockSpec((8, LC), lambda c, t: (0, c)),
                   pl.BlockSpec((2, 256, 256), lambda c, t: (c, 0, 0)),
                   pl.BlockSpec((2, 256, 256), lambda c, t: (c, 0, 0))],
        out_shape=[jax.ShapeDtypeStruct((S, D), BF16), jax.ShapeDtypeStruct((8, D), F32),
                   jax.ShapeDtypeStruct((4, 256, 256), F32), jax.ShapeDtypeStruct((4, 256, 256), F32)],
        scratch_shapes=[big(), big(), big(), til(), til(), til(), big(),
                        pltpu.VMEM((1, LC), F32), pltpu.VMEM((1, LC), F32), pltpu.VMEM((8, LC), F32)],
        compiler_params=_cp("parallel", "arbitrary"),
    )(dh, h, h, rest, rest, pvec, wa, wx)


def mix_out_fwd(x, ao, hg, rest, vec, w_att_o, w_rec_o, w_out, name, tm=256):
    S = x.shape[0]
    tm = min(tm, S)

    def body(x_ref, ao_ref, hg_ref, ga_ref, gr_ref, vec_ref, wa_ref, wr_ref, wo_ref,
             xo_ref, att_ref, rec_ref, mg_ref, f_ref):
        att = _dot(ao_ref[...], wa_ref[...])
        rec = _dot(hg_ref[...], wr_ref[...])
        att_ref[...] = att
        rec_ref[...] = rec
        mg = (_sigmoid(ga_ref[...]) * att + _sigmoid(gr_ref[...]) * rec).astype(BF16)
        mg_ref[...] = mg
        f = _dot(mg, wo_ref[...])
        f_ref[...] = f
        y = f * lax.rsqrt(_mean(f * f) + EPS) * vec_ref[1:2, :]
        xo_ref[...] = x_ref[...] + (1.0 * vec_ref[4:5, :]) * y

    row = lambda i: (i, 0)
    full = lambda r: pl.BlockSpec((r, D), lambda i: (0, 0))
    return pl.pallas_call(
        body, name=name, grid=(S // tm,),
        in_specs=[pl.BlockSpec((tm, D), row), pl.BlockSpec((tm, 512), row), pl.BlockSpec((tm, D), row),
                  pl.BlockSpec((tm, D), lambda i: (i, 2)), pl.BlockSpec((tm, D), lambda i: (i, 3)),
                  full(8), full(512), full(D), full(D)],
        out_specs=[pl.BlockSpec((tm, D), row)] * 5,
        out_shape=[jax.ShapeDtypeStruct((S, D), F32), jax.ShapeDtypeStruct((S, D), F32),
                   jax.ShapeDtypeStruct((S, D), F32), jax.ShapeDtypeStruct((S, D), BF16),
                   jax.ShapeDtypeStruct((S, D), F32)],
        compiler_params=_cp("parallel"),
    )(x, ao, hg, rest, rest, vec, w_att_o, w_rec_o, w_out)


def mix_out_bwd(dxo, f, att, rec, rest, h, vec, w_att_o, w_rec_o, w_out, name, tm=256):
    S = dxo.shape[0]
    tm = min(tm, S)

    def body(dxo_ref, f_ref, att_ref, rec_ref, yr_ref, ga_ref, gr_ref, h_ref, vec_ref, wa_ref, wr_ref, wo_ref,
             df_ref, da_ref, dr_ref, dao_ref, dh_ref, d3_ref, vacc_ref):
        @pl.when(pl.program_id(0) == 0)
        def _():
            vacc_ref[...] = jnp.zeros_like(vacc_ref)

        df = _post_norm_bwd(dxo_ref[...], f_ref[...], 1.0, vec_ref, vacc_ref).astype(BF16)
        df_ref[...] = df
        dm = _dot_nt(df, wo_ref[...])
        sa = _sigmoid(ga_ref[...])
        sr = _sigmoid(gr_ref[...])
        d_att = (dm * sa).astype(BF16)
        d_rec = (dm * sr).astype(BF16)
        da_ref[...] = d_att
        dr_ref[...] = d_rec
        d3_ref[1] = (dm * att_ref[...] * (sa * (1.0 - sa))).astype(BF16)
        d3_ref[2] = (dm * rec_ref[...] * (sr * (1.0 - sr))).astype(BF16)
        dao_ref[...] = _dot_nt(d_att, wa_ref[...]).astype(BF16)
        d_hg = _dot_nt(d_rec, wr_ref[...])
        yr = yr_ref[...]
        dh_ref[...] = d_hg * _gelu(yr)
        d3_ref[0] = (d_hg * h_ref[...] * _gelu_grad(yr)).astype(BF16)

    row = lambda i: (i, 0)
    full = lambda r: pl.BlockSpec((r, D), lambda i: (0, 0))
    return pl.pallas_call(
        body, name=name, grid=(S // tm,),
        in_specs=[pl.BlockSpec((tm, D), row)] * 4
        + [pl.BlockSpec((tm, D), lambda i: (i, 1)), pl.BlockSpec((tm, D), lambda i: (i, 2)),
           pl.BlockSpec((tm, D), lambda i: (i, 3)), pl.BlockSpec((tm, D), row),
           full(8), full(512), full(D), full(D)],
        out_specs=[pl.BlockSpec((tm, D), row)] * 3
        + [pl.BlockSpec((tm, 512), row), pl.BlockSpec((tm, D), row),
           pl.BlockSpec((3, tm, D), lambda i: (0, i, 0)), pl.BlockSpec((8, D), lambda i: (0, 0))],
        out_shape=[jax.ShapeDtypeStruct((S, D), BF16)] * 3
        + [jax.ShapeDtypeStruct((S, 512), BF16), jax.ShapeDtypeStruct((S, D), F32),
           jax.ShapeDtypeStruct((3, S, D), BF16), jax.ShapeDtypeStruct((8, D), F32)],
        compiler_params=_cp("arbitrary"),
    )(dxo, f, att, rec, rest, rest, rest, h, vec, w_att_o, w_rec_o, w_out)


def loss_grad(y, tgt, name, tm=512):
    S = y.shape[0]
    tm = min(tm, S)
    nt = S // tm

    def body(y_ref, t_ref, dy_ref, l_ref, acc):
        i = pl.program_id(0)

        @pl.when(i == 0)
        def _():
            acc[...] = jnp.zeros_like(acc)

        d = y_ref[...] - t_ref[...]
        dy_ref[...] = d * (1.0 / D)
        acc[...] += _colsum(d * d)

        @pl.when(i == nt - 1)
        def _():
            l_ref[...] = jnp.broadcast_to(0.5 * jnp.sum(acc[...]) * (1.0 / D), (8, 128))

    return pl.pallas_call(
        body, name=name, grid=(nt,),
        in_specs=[pl.BlockSpec((tm, D), lambda i: (i, 0))] * 2,
        out_specs=[pl.BlockSpec((tm, D), lambda i: (i, 0)), pl.BlockSpec((8, 128), lambda i: (0, 0))],
        out_shape=[jax.ShapeDtypeStruct((S, D), F32), jax.ShapeDtypeStruct((8, 128), F32)],
        scratch_shapes=[pltpu.VMEM((1, D), F32)],
        compiler_params=_cp("arbitrary"),
    )(y, tgt)


def ada_fwd(c_all, w_ada, b_ada, name, tn=768):
    n = w_ada.shape[1]

    def body(c_ref, w_ref, b_ref, o_ref):
        cv = c_ref[...]
        ca = (cv * _sigmoid(cv)).astype(BF16)
        o_ref[...] = _dot(ca, w_ref[...].astype(BF16)) + b_ref[...]

    return pl.pallas_call(
        body, name=name, grid=(n // tn,),
        in_specs=[pl.BlockSpec((8, D), lambda j: (0, 0)), pl.BlockSpec((D, tn), lambda j: (0, j)),
                  pl.BlockSpec((1, tn), lambda j: (0, j))],
        out_specs=pl.BlockSpec((8, tn), lambda j: (0, j)),
        out_shape=jax.ShapeDtypeStruct((8, n), F32),
        compiler_params=_cp("parallel"),
    )(c_all, w_ada, b_ada)


def ada_bwd(c_all_t, dmod, name, tn=768):
    n = dmod.shape[1]

    def body(c_ref, d_ref, o_ref):
        cv = c_ref[...]
        ca = (cv * _sigmoid(cv)).astype(BF16)
        o_ref[...] = _dot(ca, d_ref[...].astype(BF16))

    return pl.pallas_call(
        body, name=name, grid=(n // tn,),
        in_specs=[pl.BlockSpec((D, 128), lambda j: (0, 0)), pl.BlockSpec((128, tn), lambda j: (0, j))],
        out_specs=pl.BlockSpec((D, tn), lambda j: (0, j)),
        out_shape=jax.ShapeDtypeStruct((D, n), F32),
        compiler_params=_cp("parallel"),
    )(c_all_t, dmod)


def _row_tile(rows, cols, itemsize=4, budget=1536 * 1024):
    best = None
    for t in range(8, rows + 1, 8):
        if rows % t == 0 and t * cols * itemsize <= budget:
            best = t
    return rows if best is None else best


def sum_lead(parts, name, out_dtype=F32):
    n, R, C = parts.shape
    tr = _row_tile(R, C * n)

    def body(p_ref, o_ref):
        acc = p_ref[0].astype(F32)
        for k in range(1, n):
            acc = acc + p_ref[k].astype(F32)
        o_ref[...] = acc.astype(out_dtype)

    return pl.pallas_call(
        body, name=name, grid=(R // tr,),
        in_specs=[pl.BlockSpec((n, tr, C), lambda i: (0, i, 0))],
        out_specs=pl.BlockSpec((tr, C), lambda i: (i, 0)),
        out_shape=jax.ShapeDtypeStruct((R, C), out_dtype),
        compiler_params=_cp("parallel"),
    )(parts)


def adamw(w, g, m, v, name):
    R, C = w.shape
    tr = _row_tile(R, C * 7, budget=8 * 1024 * 1024)

    def body(w_ref, g_ref, m_ref, v_ref, d_ref, mo_ref, vo_ref):
        gv = g_ref[...]
        mn = ADAM_B1 * m_ref[...] + (1.0 - ADAM_B1) * gv
        vn = ADAM_B2 * v_ref[...] + (1.0 - ADAM_B2) * (gv * gv)
        m_hat = mn / (1.0 - ADAM_B1 ** ADAM_STEP)
        v_hat = vn / (1.0 - ADAM_B2 ** ADAM_STEP)
        d_ref[...] = -ADAM_LR * (m_hat / (jnp.sqrt(v_hat) + ADAM_EPS) + ADAM_WD * w_ref[...])
        mo_ref[...] = mn
        vo_ref[...] = vn

    spec = pl.BlockSpec((tr, C), lambda i: (i, 0))
    return pl.pallas_call(
        body, name=name, grid=(R // tr,),
        in_specs=[spec] * 4, out_specs=[spec] * 3,
        out_shape=[jax.ShapeDtypeStruct((R, C), F32)] * 3,
        compiler_params=_cp("parallel"),
    )(w, g, m, v)


def _mesh_pos():
    return lax.axis_index("x"), lax.axis_index("y"), lax.axis_index("c")


def _other_chips(mx, my):
    return [(1 - mx, my), (mx, 1 - my), (1 - mx, 1 - my)]


def ag_small(x, name):
    R = x.shape[0]

    def body(x_ref, out_ref, send_sems, recv_sems, local_sem):
        mx, my, mc = _mesh_pos()
        me, sibling = (mx, my, mc), (mx, my, 1 - mc)
        chips = _other_chips(mx, my)

        def slot(px, py, pc):
            return out_ref.at[4 * px + 2 * py + pc]

        def copy(k, block, to, src=None):
            return pltpu.make_async_remote_copy(
                src_ref=slot(*block) if src is None else src, dst_ref=slot(*block),
                send_sem=send_sems.at[k], recv_sem=recv_sems.at[k], device_id=to, device_id_type=MESH)

        mine = pltpu.make_async_copy(x_ref, slot(*me), local_sem)
        mine.start()
        first = [copy(0, me, sibling, src=x_ref)]
        first += [copy(1 + j, me, (*chip, mc), src=x_ref) for j, chip in enumerate(chips)]
        for cp in first:
            cp.start()
        passed = [copy(4 + j, (*chip, mc), sibling) for j, chip in enumerate(chips)]
        for j, chip in enumerate(chips):
            copy(1 + j, (*chip, mc), me).wait_recv()
            passed[j].start()
        copy(0, sibling, me).wait_recv()
        for j, chip in enumerate(chips):
            copy(4 + j, (*chip, 1 - mc), me).wait_recv()
        for cp in first + passed:
            cp.wait_send()
        mine.wait()

    return pl.pallas_call(
        body, name=name,
        out_shape=jax.ShapeDtypeStruct((N_DEV, R, 128), F32),
        in_specs=[pl.BlockSpec(memory_space=pltpu.VMEM)],
        out_specs=pl.BlockSpec(memory_space=pltpu.VMEM),
        scratch_shapes=[pltpu.SemaphoreType.DMA((7,)), pltpu.SemaphoreType.DMA((7,)), pltpu.SemaphoreType.DMA],
        compiler_params=pltpu.CompilerParams(vmem_limit_bytes=VMEM_LIMIT),
    )(x)


BIG = (("ffn1_w_gu", "col", D, PW), ("ffn1_w_down", "row", FF, D), ("w_in", "col", D, PW),
       ("w_att_o", "col", 512, D), ("w_rec_o", "row", D, D), ("w_out", "row", D, D),
       ("ffn2_w_gu", "col", D, PW), ("ffn2_w_down", "row", FF, D))
NBIG = len(BIG)


def _shard_shape(kind, R, C):
    return (R, C // 4) if kind == "col" else (R // 4, C)


def _region(ref, kind, R, C, q, half, t, tr):
    sr, sc = _shard_shape(kind, R, C)
    if kind == "col":
        return ref.at[pl.ds(pl.multiple_of(half * (R // 2) + t * tr, 16), tr), pl.ds(q * sc, sc)]
    return ref.at[pl.ds(pl.multiple_of(q * sr + t * tr, 16), tr), pl.ds(half * (C // 2), C // 2)]


def ag_local(w, kind, R, C, p_arr, name, after=()):
    sr, sc = _shard_shape(kind, R, C)
    tr = _row_tile(sr, sc, budget=2 * 1024 * 1024)
    nt = sr // tr
    after = list(after)

    def body(p_ref, w_ref, *rest):
        rest[-1][...] = w_ref[...].astype(BF16)

    if kind == "col":
        o_spec = pl.BlockSpec((tr, sc), lambda i, p: (i, p[0]))
    else:
        o_spec = pl.BlockSpec((tr, sc), lambda i, p: (p[0] * nt + i, 0))
    return pl.pallas_call(
        body, name=name,
        grid_spec=pltpu.PrefetchScalarGridSpec(
            num_scalar_prefetch=1, grid=(nt,),
            in_specs=[pl.BlockSpec((tr, sc), lambda i, p: (i, 0))] + [ANY] * len(after), out_specs=o_spec),
        out_shape=jax.ShapeDtypeStruct((R, C), BF16),
        compiler_params=_cp("parallel"),
    )(p_arr, w, *after)


HBM_SPEC = pl.BlockSpec(memory_space=pltpu.HBM)
SEM_SPEC = pl.BlockSpec(memory_space=pltpu.SEMAPHORE)


def _ag_copies(fulls, geoms, ssem, rsem, mx, my, mc, q, h):
    chips = _other_chips(mx, my)
    out = []
    for w, (kind, R, C) in enumerate(geoms):
        sr, sc = _shard_shape(kind, R, C)
        hr = sr // 2 if kind == "col" else sr
        reg = _region(fulls[w], kind, R, C, q, h, 0, hr)
        out += [pltpu.make_async_remote_copy(src_ref=reg, dst_ref=reg, send_sem=ssem.at[3 * w + k],
                                             recv_sem=rsem.at[3 * w + k], device_id=(*chips[k], mc),
                                             device_id_type=MESH) for k in range(3)]
    return out


def ag_start(fulls, geoms, after, name):
    n = len(fulls)
    after = list(after)
    m = len(after)

    def body(*refs):
        ssem, rsem = refs[n + m:n + m + 2]
        outs, token = refs[n + m + 2:2 * n + m + 2], refs[2 * n + m + 2]
        mx, my, mc = _mesh_pos()
        p = 2 * mx + my
        col = [w for w, g in enumerate(geoms) if g[0] == "col"]
        row = [w for w, g in enumerate(geoms) if g[0] == "row"]
        for q in range(4):
            @pl.when(p == q)
            def _(q=q):
                cps = _ag_copies(outs, geoms, ssem, rsem, mx, my, mc, q, mc)
                for w in col:
                    for cp in cps[3 * w:3 * w + 3]:
                        cp.start()
        for h in range(2):
            @pl.when(mc == h)
            def _(h=h):
                cps = _ag_copies(outs, geoms, ssem, rsem, mx, my, mc, p, h)
                for w in row:
                    for cp in cps[3 * w:3 * w + 3]:
                        cp.start()
        token[...] = jnp.zeros_like(token)

    res = pl.pallas_call(
        body, name=name,
        out_shape=[pltpu.SemaphoreType.DMA((3 * n,)), pltpu.SemaphoreType.DMA((3 * n,))]
        + [pltpu.HBM(a.shape, a.dtype) for a in fulls] + [jax.ShapeDtypeStruct((8, 128), F32)],
        in_specs=[HBM_SPEC] * n + [ANY] * m,
        out_specs=[SEM_SPEC, SEM_SPEC] + [HBM_SPEC] * n + [pl.BlockSpec(memory_space=pltpu.VMEM)],
        input_output_aliases={w: 2 + w for w in range(n)},
        compiler_params=pltpu.CompilerParams(has_side_effects=pltpu.SideEffectType.DATAFLOW_SIDE_EFFECTING),
    )(*[pltpu.with_memory_space_constraint(a, pltpu.HBM) for a in fulls], *after)
    return res[0], res[1], list(res[2:2 + n]), res[2 + n]


def ag_wait(fulls, geoms, ssem, rsem, after, name):
    n = len(fulls)

    def body(*refs):
        ins, ssem_ref, rsem_ref = refs[:n], refs[n], refs[n + 1]
        mx, my, mc = _mesh_pos()
        for cp in _ag_copies(ins, geoms, ssem_ref, rsem_ref, mx, my, mc, 0, 0):
            cp.wait_send()
            cp.wait_recv()

    return list(pl.pallas_call(
        body, name=name,
        out_shape=[pltpu.HBM(a.shape, a.dtype) for a in fulls],
        in_specs=[HBM_SPEC] * n + [SEM_SPEC, SEM_SPEC, ANY],
        out_specs=[HBM_SPEC] * n,
        input_output_aliases={w: w for w in range(n)},
        compiler_params=pltpu.CompilerParams(has_side_effects=pltpu.SideEffectType.DATAFLOW_SIDE_EFFECTING),
    )(*fulls, ssem, rsem, after))


def ag_forward(full, kind, R, C, name):
    sr, sc = _shard_shape(kind, R, C)
    hr, hc = (sr // 2, sc) if kind == "col" else (sr, sc // 2)
    tr = _row_tile(hr, hc, itemsize=2, budget=512 * 1024)
    nt = hr // tr

    total = 3 * nt

    def body(src_ref, full_ref, stage, lsem, ssem, rsem):
        step = pl.program_id(0) * nt + pl.program_id(1)
        par = step % 2
        mx, my, mc = _mesh_pos()

        def load(s, q, h, t):
            return pltpu.make_async_copy(_region(src_ref, kind, R, C, q, h, t, tr), stage.at[s], lsem.at[s])

        def push(s, q, h, t):
            return pltpu.make_async_remote_copy(src_ref=stage.at[s], dst_ref=_region(full_ref, kind, R, C, q, h, t, tr),
                                                send_sem=ssem.at[s], recv_sem=rsem, device_id=(mx, my, 1 - mc),
                                                device_id_type=MESH)

        def for_tile(stp, fn):
            q_k = _partner_chip(stp // nt, 2 * mx + my)
            if kind == "col":
                for q in range(4):
                    @pl.when(q_k == q)
                    def _(q=q):
                        fn(q, mc, stp % nt)
            else:
                for h in range(2):
                    @pl.when(mc == h)
                    def _(h=h):
                        fn(q_k, h, stp % nt)

        @pl.when(step == 0)
        def _():
            for_tile(step, lambda q, h, t: load(0, q, h, t).start())

        load(par, 0, 0, 0).wait()
        for_tile(step, lambda q, h, t: push(par, q, h, t).start())

        @pl.when(step + 1 < total)
        def _():
            @pl.when(step >= 1)
            def _():
                push(1 - par, 0, 0, 0).wait_send()
            for_tile(step + 1, lambda q, h, t: load(1 - par, q, h, t).start())

        @pl.when(step == total - 1)
        def _():
            push(par, 0, 0, 0).wait_send()
            push(1 - par, 0, 0, 0).wait_send()
            three = full_ref.at[pl.ds(0, hr), pl.ds(0, 3 * hc)] if kind == "col" else full_ref.at[pl.ds(0, 3 * hr), pl.ds(0, hc)]
            pltpu.make_async_remote_copy(src_ref=three, dst_ref=three, send_sem=ssem.at[0], recv_sem=rsem,
                                         device_id=(mx, my, 1 - mc), device_id_type=MESH).wait_recv()

    return pl.pallas_call(
        body, name=name, grid=(3, nt),
        in_specs=[ANY], out_specs=ANY,
        out_shape=jax.ShapeDtypeStruct((R, C), BF16),
        scratch_shapes=[pltpu.VMEM((2, tr, hc), BF16), pltpu.SemaphoreType.DMA((2,)), pltpu.SemaphoreType.DMA((2,)),
                        pltpu.SemaphoreType.DMA],
        input_output_aliases={0: 0},
        compiler_params=_cp("arbitrary", "arbitrary"),
    )(full)


def _half_shape(kind, R, C):
    return (R // 2, C) if kind == "col" else (R, C // 2)


def _piece_shape(kind, R, C):
    return (R // 2, C // 4) if kind == "col" else (R // 4, C // 2)


def pair_push(g, kind, c_arr, name):
    R, C = g.shape
    hr, hc = _half_shape(kind, R, C)
    tr = _row_tile(hr, hc, itemsize=2, budget=1024 * 1024)
    nt = hr // tr

    def body(c_ref, g_ref, out_ref, stage, ssem, rsem):
        i = pl.program_id(0)
        slot = i % 2
        mx, my, mc = _mesh_pos()

        def push(s, t):
            return pltpu.make_async_remote_copy(
                src_ref=stage.at[s], dst_ref=out_ref.at[pl.ds(pl.multiple_of(t * tr, 16), tr)],
                send_sem=ssem.at[s], recv_sem=rsem, device_id=(mx, my, 1 - mc), device_id_type=MESH)

        @pl.when(i >= 2)
        def _():
            push(slot, 0).wait_send()

        stage[slot] = g_ref[...]
        push(slot, i).start()

        @pl.when(i == nt - 1)
        def _():
            push(slot, 0).wait_send()
            if nt >= 2:
                push(1 - slot, 0).wait_send()
            pltpu.make_async_remote_copy(src_ref=out_ref, dst_ref=out_ref, send_sem=ssem.at[0], recv_sem=rsem,
                                         device_id=(mx, my, 1 - mc), device_id_type=MESH).wait_recv()

    if kind == "col":
        g_spec = pl.BlockSpec((tr, hc), lambda i, c: ((1 - c[0]) * nt + i, 0))
    else:
        g_spec = pl.BlockSpec((tr, hc), lambda i, c: (i, 1 - c[0]))
    return pl.pallas_call(
        body, name=name,
        grid_spec=pltpu.PrefetchScalarGridSpec(
            num_scalar_prefetch=1, grid=(nt,), in_specs=[g_spec], out_specs=ANY,
            scratch_shapes=[pltpu.VMEM((2, tr, hc), BF16), pltpu.SemaphoreType.DMA((2,)), pltpu.SemaphoreType.DMA]),
        out_shape=jax.ShapeDtypeStruct((hr, hc), BF16),
        compiler_params=_cp("arbitrary"),
    )(c_arr, g)


def _partner_chip(k, p):
    return p ^ jnp.where(k == 0, 2, jnp.where(k == 1, 1, jnp.where(k == 2, 3, 0)))


def pair_add(g, got, kind, cp_arr, name):
    R, C = g.shape
    pr, pc = _piece_shape(kind, R, C)
    tr = _row_tile(pr, pc, itemsize=2, budget=1024 * 1024)
    nt = pr // tr

    def body(cp_ref, g_ref, got_ref, ps_ref, rb_ref):
        tile = (g_ref[...].astype(F32) + got_ref[...].astype(F32)).astype(BF16)
        ps_ref[...] = tile

        @pl.when(pl.program_id(1) == cp_ref[1])
        def _():
            rb_ref[...] = tile

    if kind == "col":
        g_spec = pl.BlockSpec((tr, pc), lambda i, q, cp: (cp[0] * nt + i, q))
        got_spec = pl.BlockSpec((tr, pc), lambda i, q, cp: (i, q))
    else:
        g_spec = pl.BlockSpec((tr, pc), lambda i, q, cp: (q * nt + i, cp[0]))
        got_spec = pl.BlockSpec((tr, pc), lambda i, q, cp: (q * nt + i, 0))
    return pl.pallas_call(
        body, name=name,
        grid_spec=pltpu.PrefetchScalarGridSpec(
            num_scalar_prefetch=1, grid=(nt, 4), in_specs=[g_spec, got_spec],
            out_specs=[pl.BlockSpec((None, tr, pc), lambda i, q, cp: (q, i, 0)),
                       pl.BlockSpec((None, tr, pc), lambda i, q, cp: (cp[1], i, 0))]),
        out_shape=[jax.ShapeDtypeStruct((4, pr, pc), BF16)] * 2,
        compiler_params=_cp("arbitrary", "arbitrary"),
    )(cp_arr, g, got)


def _rs_copies(ps, rb, ssem, rsem, mx, my, mc):
    p = 2 * mx + my
    out = []
    for w in range(len(ps)):
        for k, chip in enumerate(_other_chips(mx, my)):
            out.append(pltpu.make_async_remote_copy(
                src_ref=ps[w].at[2 * chip[0] + chip[1]], dst_ref=rb[w].at[p], send_sem=ssem.at[3 * w + k],
                recv_sem=rsem.at[3 * w + k], device_id=(*chip, mc), device_id_type=MESH))
    return out


def rs_start(ps, rb, after, name):
    n = len(ps)
    after = list(after)
    m = len(after)

    def body(*refs):
        ssem, rsem = refs[2 * n + m:2 * n + m + 2]
        ps_o = refs[2 * n + m + 2:3 * n + m + 2]
        rb_o = refs[3 * n + m + 2:4 * n + m + 2]
        token = refs[4 * n + m + 2]
        for cp in _rs_copies(ps_o, rb_o, ssem, rsem, *_mesh_pos()):
            cp.start()
        token[...] = jnp.zeros_like(token)

    both = list(ps) + list(rb)
    res = pl.pallas_call(
        body, name=name,
        out_shape=[pltpu.SemaphoreType.DMA((3 * n,)), pltpu.SemaphoreType.DMA((3 * n,))]
        + [pltpu.HBM(a.shape, a.dtype) for a in both] + [jax.ShapeDtypeStruct((8, 128), F32)],
        in_specs=[HBM_SPEC] * (2 * n) + [ANY] * m,
        out_specs=[SEM_SPEC, SEM_SPEC] + [HBM_SPEC] * (2 * n) + [pl.BlockSpec(memory_space=pltpu.VMEM)],
        input_output_aliases={w: 2 + w for w in range(2 * n)},
        compiler_params=pltpu.CompilerParams(has_side_effects=pltpu.SideEffectType.DATAFLOW_SIDE_EFFECTING),
    )(*[pltpu.with_memory_space_constraint(a, pltpu.HBM) for a in both], *after)
    return res[0], res[1], list(res[2:2 + n]), list(res[2 + n:2 + 2 * n]), res[2 + 2 * n]


def rs_wait(ps, rb, ssem, rsem, after, name):
    n = len(ps)
    after = list(after)
    m = len(after)

    def body(*refs):
        ps_i, rb_i = refs[:n], refs[n:2 * n]
        ssem_ref, rsem_ref = refs[2 * n], refs[2 * n + 1]
        for cp in _rs_copies(ps_i, rb_i, ssem_ref, rsem_ref, *_mesh_pos()):
            cp.wait_send()
            cp.wait_recv()

    both = list(ps) + list(rb)
    res = pl.pallas_call(
        body, name=name,
        out_shape=[pltpu.HBM(a.shape, a.dtype) for a in both],
        in_specs=[HBM_SPEC] * (2 * n) + [SEM_SPEC, SEM_SPEC] + [ANY] * m,
        out_specs=[HBM_SPEC] * (2 * n),
        input_output_aliases={w: w for w in range(2 * n)},
        compiler_params=pltpu.CompilerParams(has_side_effects=pltpu.SideEffectType.DATAFLOW_SIDE_EFFECTING),
    )(*both, ssem, rsem, *after)
    return list(res[n:])


def sum_share(parts, kind, R, C, name):
    _, pr, pc = parts.shape
    sr, sc = _shard_shape(kind, R, C)
    tr = _row_tile(pr, pc * 4, budget=4 * 1024 * 1024)
    nt = pr // tr

    def body(p_ref, fin_ref, stage, lsem, ssem, rsem):
        i = pl.program_id(0)
        slot = i % 2
        mx, my, mc = _mesh_pos()

        def region(h, t):
            r0 = pl.multiple_of(t * tr, 8)
            if kind == "col":
                return fin_ref.at[pl.ds(pl.multiple_of(h * pr + r0, 8), tr)]
            return fin_ref.at[pl.ds(r0, tr), pl.ds(h * pc, pc)]

        def copies(s, h, t):
            return (pltpu.make_async_copy(stage.at[s], region(h, t), lsem.at[s]),
                    pltpu.make_async_remote_copy(src_ref=stage.at[s], dst_ref=region(h, t), send_sem=ssem.at[s],
                                                 recv_sem=rsem, device_id=(mx, my, 1 - mc), device_id_type=MESH))

        def wait_sent(s):
            loc, rem = copies(s, 0, 0)
            loc.wait()
            rem.wait_send()

        @pl.when(i >= 2)
        def _():
            wait_sent(slot)

        acc = p_ref[0].astype(F32)
        for k in range(1, 4):
            acc = acc + p_ref[k].astype(F32)
        stage[slot] = acc
        if kind == "col":
            for cp in copies(slot, mc, i):
                cp.start()
        else:
            for h in range(2):
                @pl.when(mc == h)
                def _(h=h):
                    for cp in copies(slot, h, i):
                        cp.start()

        @pl.when(i == nt - 1)
        def _():
            wait_sent(slot)
            if nt >= 2:
                wait_sent(1 - slot)
            half = fin_ref.at[pl.ds(0, pr), pl.ds(0, pc)]
            pltpu.make_async_remote_copy(src_ref=half, dst_ref=half, send_sem=ssem.at[0], recv_sem=rsem,
                                         device_id=(mx, my, 1 - mc), device_id_type=MESH).wait_recv()

    return pl.pallas_call(
        body, name=name, grid=(nt,),
        in_specs=[pl.BlockSpec((4, tr, pc), lambda i: (0, i, 0))],
        out_specs=ANY,
        out_shape=jax.ShapeDtypeStruct((sr, sc), F32),
        scratch_shapes=[pltpu.VMEM((2, tr, pc), F32), pltpu.SemaphoreType.DMA((2,)), pltpu.SemaphoreType.DMA((2,)),
                        pltpu.SemaphoreType.DMA],
        compiler_params=_cp("arbitrary"),
    )(parts)


def _pack(parts, rows):
    flat = []
    for a in parts:
        a = jnp.ravel(a).astype(F32)
        flat.append(jnp.pad(a, (0, (-a.shape[0]) % 128)))
    v = jnp.concatenate(flat)
    return jnp.pad(v, (0, rows * 128 - v.shape[0])).reshape(rows, 128)


def _unpack(block, shapes):
    lead = block.shape[:-2]
    v = block.reshape(lead + (-1,))
    out, off = [], 0
    for shp in shapes:
        n = int(np.prod(shp))
        out.append(v[..., off:off + n].reshape(lead + tuple(shp)))
        off += n + (-n) % 128
    return out


def _block_diag4(w):
    w4 = w.reshape(4, 4, 64, 64)
    eye = jnp.eye(4, dtype=w.dtype)
    return (w4[:, :, :, None, :] * eye[None, :, None, :, None]).reshape(4, 256, 256)


def _diag_blocks(bd):
    b5 = bd.reshape(4, 4, 64, 4, 64)
    return jnp.stack([b5[:, i, :, i, :] for i in range(4)], axis=1).reshape(16, 64, 64)


def _bias_window(rel_bias):
    m = (np.arange(768) + 127) % 768 - 127
    w = rel_bias[:, np.clip(512 - m, -128, 128) + 128]
    win = jnp.tile(w, (1, 128))[:, :128 * 767].reshape(8, 128, 767)[:, :, :WIN]
    qh = np.arange(128)[:, None] // CHUNK
    kc = np.arange(WIN)[None, :] // CHUNK
    valid = (kc >= qh) & (kc <= qh + 8)
    return jnp.where(jnp.asarray(valid)[None], win, NEG)


SMALL = ("b_ada", "norm_pre", "norm_post", "rel_bias", "conv_w", "conv_b", "lru_wa", "lru_ba", "lru_wx",
         "lru_bx", "lru_lambda")
WEIGHTS = ("w_ada", "b_ada", "norm_pre", "norm_post", "ffn1_w_gu", "ffn1_w_down", "w_in", "rel_bias", "conv_w",
           "conv_b", "lru_wa", "lru_ba", "lru_wx", "lru_bx", "lru_lambda", "w_att_o", "w_rec_o", "w_out",
           "ffn2_w_gu", "ffn2_w_down")


def kernel(x, c, w_ada, b_ada, norm_pre, norm_post, ffn1_w_gu, ffn1_w_down, w_in, rel_bias, conv_w, conv_b, lru_wa, lru_ba, lru_wx, lru_bx, lru_lambda, w_att_o, w_rec_o, w_out, ffn2_w_gu, ffn2_w_down, loss_target, m_w_ada, m_b_ada, m_norm_pre, m_norm_post, m_ffn1_w_gu, m_ffn1_w_down, m_w_in, m_rel_bias, m_conv_w, m_conv_b, m_lru_wa, m_lru_ba, m_lru_wx, m_lru_bx, m_lru_lambda, m_w_att_o, m_w_rec_o, m_w_out, m_ffn2_w_gu, m_ffn2_w_down, v_w_ada, v_b_ada, v_norm_pre, v_norm_post, v_ffn1_w_gu, v_ffn1_w_down, v_w_in, v_rel_bias, v_conv_w, v_conv_b, v_lru_wa, v_lru_ba, v_lru_wx, v_lru_bx, v_lru_lambda, v_w_att_o, v_w_rec_o, v_w_out, v_ffn2_w_gu, v_ffn2_w_down):
    W = dict(w_ada=w_ada, b_ada=b_ada, norm_pre=norm_pre, norm_post=norm_post, ffn1_w_gu=ffn1_w_gu,
             ffn1_w_down=ffn1_w_down, w_in=w_in, rel_bias=rel_bias, conv_w=conv_w, conv_b=conv_b, lru_wa=lru_wa,
             lru_ba=lru_ba, lru_wx=lru_wx, lru_bx=lru_bx, lru_lambda=lru_lambda, w_att_o=w_att_o, w_rec_o=w_rec_o,
             w_out=w_out, ffn2_w_gu=ffn2_w_gu, ffn2_w_down=ffn2_w_down)
    M = dict(w_ada=m_w_ada, b_ada=m_b_ada, norm_pre=m_norm_pre, norm_post=m_norm_post, ffn1_w_gu=m_ffn1_w_gu,
             ffn1_w_down=m_ffn1_w_down, w_in=m_w_in, rel_bias=m_rel_bias, conv_w=m_conv_w, conv_b=m_conv_b,
             lru_wa=m_lru_wa, lru_ba=m_lru_ba, lru_wx=m_lru_wx, lru_bx=m_lru_bx, lru_lambda=m_lru_lambda,
             w_att_o=m_w_att_o, w_rec_o=m_w_rec_o, w_out=m_w_out, ffn2_w_gu=m_ffn2_w_gu, ffn2_w_down=m_ffn2_w_down)
    V = dict(w_ada=v_w_ada, b_ada=v_b_ada, norm_pre=v_norm_pre, norm_post=v_norm_post, ffn1_w_gu=v_ffn1_w_gu,
             ffn1_w_down=v_ffn1_w_down, w_in=v_w_in, rel_bias=v_rel_bias, conv_w=v_conv_w, conv_b=v_conv_b,
             lru_wa=v_lru_wa, lru_ba=v_lru_ba, lru_wx=v_lru_wx, lru_bx=v_lru_bx, lru_lambda=v_lru_lambda,
             w_att_o=v_w_att_o, w_rec_o=v_w_rec_o, w_out=v_w_out, ffn2_w_gu=v_ffn2_w_gu, ffn2_w_down=v_ffn2_w_down)
    mx, my, mc = _mesh_pos()
    p = 2 * mx + my
    e = 4 * mx + 2 * my + mc
    xs = x[0]

    c_arr = jnp.reshape(mc, (1,)).astype(jnp.int32)
    cp_arr = jnp.stack([mc, p]).astype(jnp.int32)
    p_arr = jnp.reshape(p, (1,)).astype(jnp.int32)
    geoms = [(kind, R, C) for (_, kind, R, C) in BIG]
    names = [b[0] for b in BIG]
    placed = [ag_local(W[n][0], kind, R, C, p_arr, "ag_local_" + n) for (n, kind, R, C) in BIG[:2]]

    def arrived(fly, lo, hi, ssem, rsem, after, tag):
        done = ag_wait(fly, geoms[lo:hi], ssem, rsem, after, "ag_wait_" + tag)
        return [ag_forward(a, kind, R, C, "ag_forward_" + n)
                for a, (kind, R, C), n in zip(done, geoms[lo:hi], names[lo:hi])]

    g1 = ag_small(_pack([c, norm_pre, norm_post, conv_w], 32), "ag_small_params")
    c_all, npre4, npost4, cw4 = _unpack(g1, [(D,), (3, 256), (3, 256), (4, 256)])
    chipwise = lambda a: jnp.moveaxis(a[0::2], 0, 1).reshape(a.shape[1], D)
    npre, npost, conv_full = chipwise(npre4), chipwise(npost4), chipwise(cw4)

    b_cols = lax.dynamic_slice(b_ada, (0, p * 2304), (1, 2304))
    mod_cols = ada_fwd(c_all, w_ada[0], b_cols, "ada_fwd")
    g2 = ag_small(mod_cols.reshape(144, 128), "ag_mod")
    mod_all = jnp.moveaxis(g2[0::2].reshape(4, 8, 2304), 0, 1).reshape(8, 9 * D)
    mod = lax.dynamic_index_in_dim(mod_all, e, 0, keepdims=False).reshape(3, 3, D)
    zeros3 = jnp.zeros((3, D), F32)
    vecs = [jnp.concatenate([npre[k:k + 1], npost[k:k + 1], mod[k], zeros3], axis=0) for k in range(3)]

    f1_s, f1_r, f1_fly, tok0 = ag_start(placed[:2], geoms[:2], [g2], "ag_start_ffn1")
    placed += [ag_local(W[n][0], kind, R, C, p_arr, "ag_local_" + n, after=[tok0]) for (n, kind, R, C) in BIG[2:]]
    f1_gu, f1_dn = arrived(f1_fly, 0, 2, f1_s, f1_r, placed[7], "ffn1")
    mix_s, mix_r, mix_fly, tok1 = ag_start(placed[2:6], geoms[2:6], [f1_gu, f1_dn], "ag_start_mixer")
    ffn_s, ffn_r, ffn_fly, tok2 = ag_start(placed[6:], geoms[6:], [tok1], "ag_start_ffn2")
    wa_bd = _block_diag4(lru_wa[0]).astype(BF16)
    wx_bd = _block_diag4(lru_wx[0]).astype(BF16)
    pvec = jnp.concatenate([conv_full, conv_b, lru_ba, lru_bx, lru_lambda], axis=0)
    bias = _bias_window(rel_bias[0]).reshape(4, 256, WIN)

    x1, h1, g1_, u1, a1, f1 = ffn_fwd(xs, vecs[0] + tok2[0:1, 0:1], f1_gu, f1_dn, 0.5, "ffn1_fwd")
    win, wao, wro, wout = arrived(mix_fly, 2, 6, mix_s, mix_r, x1, "mixer")
    h2, qkv, rest = proj_fwd(x1, vecs[1], win, "proj_fwd")
    ao = attn_fwd(qkv, bias, "attn_fwd")
    hl, hg = lru_fwd(rest, pvec, wa_bd, wx_bd, "lru_fwd")
    x2, att, rec, mg, f2 = mix_out_fwd(x1, ao, hg, rest, vecs[1], wao, wro, wout, "mix_out_fwd")
    f2_gu, f2_dn = arrived(ffn_fly, 6, 8, ffn_s, ffn_r, x2, "ffn2")
    x3, h3, g3_, u3, a3, f3 = ffn_fwd(x2, vecs[2], f2_gu, f2_dn, 0.5, "ffn2_fwd")
    dy, lvec = loss_grad(x3, loss_target[0], "loss_grad")
    loss = lax.psum(lvec[0, 0], ("x", "y", "c"))

    G, grads = {}, {}
    geo = {n: (kind, R, C) for (n, kind, R, C) in BIG}

    def reduce_begin(names, tag):
        ps, rb = [], []
        for n in names:
            got = pair_push(G[n], geo[n][0], c_arr, "rs_push_" + n)
            a, b = pair_add(G[n], got, geo[n][0], cp_arr, "rs_pair_sum_" + n)
            ps.append(a)
            rb.append(b)
        return rs_start(ps, rb, [], "rs_start_" + tag)

    def reduce_end(names, flight, after, tag):
        ssem, rsem, ps, rb, _ = flight
        for a, n in zip(rs_wait(ps, rb, ssem, rsem, after, "rs_wait_" + tag), names):
            grads[n] = sum_share(a, *geo[n], "rs_sum_share_" + n)[None]

    dx2, df3, dgu3, va2 = ffn_bwd(dy, x2, f3, g3_, u3, vecs[2], f2_gu, f2_gu[:, FF:], f2_dn, 0.5, "ffn2_bwd")
    G["ffn2_w_gu"] = mm_tn(h3, dgu3, "dw_ffn2_gu", D, 1408, 1024)
    G["ffn2_w_down"] = mm_tn(a3, df3, "dw_ffn2_down", 1408, D, 1024)
    fly_ffn2 = reduce_begin(("ffn2_w_gu", "ffn2_w_down"), "ffn2")
    vec1 = vecs[1] + fly_ffn2[4][0:1, 0:1]
    df2, d_att, d_rec, dao, dhl, d3, va_out = mix_out_bwd(dx2, f2, att, rec, rest, hl, vec1, wao, wro, wout,
                                                          "mix_out_bwd")
    G["w_out"] = mm_tn(mg, df2, "dw_out", D, D, 1024)
    G["w_att_o"] = mm_tn(ao, d_att, "dw_att_o", 512, D, 1024)
    G["w_rec_o"] = mm_tn(hg, d_rec, "dw_rec_o", D, D, 1024)
    dq, db, dkv = attn_bwd(qkv, dao, bias, "attn_bwd")
    dxr, v_lru, dwa_bd, dwx_bd = lru_bwd(dhl, hl, rest, pvec, wa_bd, wx_bd, "lru_bwd")
    dx1, va_in = proj_bwd(dq, dkv, dxr, d3, win, x1, dx2, vecs[1], "proj_bwd")
    gin = mm_tn(h2, dq, "dw_in_q", D, 512, 1024, n_total=PW)
    gin = mm_tn(h2, dkv, "dw_in_kv", D, 512, 1024, prev=gin, col_off=512, n_total=PW)
    gin = mm_tn(h2, dxr, "dw_in_xr", D, 512, 1024, prev=gin, col_off=1536, n_total=PW)
    G["w_in"] = mm_tn(h2, d3, "dw_in_gates", D, 512, 1024, prev=gin, col_off=2560, n_total=PW)
    fly_mix = reduce_begin(("w_in", "w_att_o", "w_rec_o", "w_out"), "mixer")
    vec0 = vecs[0] + fly_mix[4][0:1, 0:1]
    dx0, df1, dgu1, va0 = ffn_bwd(dx1, xs, f1, g1_, u1, vec0, f1_gu, f1_gu[:, FF:], f1_dn, 0.5, "ffn1_bwd")
    G["ffn1_w_gu"] = mm_tn(h1, dgu1, "dw_ffn1_gu", D, 1408, 1024)
    G["ffn1_w_down"] = mm_tn(a1, df1, "dw_ffn1_down", 1408, D, 1024)
    fly_ffn1 = reduce_begin(("ffn1_w_gu", "ffn1_w_down"), "ffn1")
    reduce_end(("ffn2_w_gu", "ffn2_w_down"), fly_ffn2, [fly_ffn1[4]], "ffn2")
    reduce_end(("w_in", "w_att_o", "w_rec_o", "w_out"), fly_mix, [fly_ffn1[4], grads["ffn2_w_down"]], "mixer")

    va1 = va_out + va_in
    vas = (va0, va1, va2)
    dmod = jnp.stack([v[2:5] for v in vas])
    part = {"b_ada": dmod, "norm_pre": jnp.stack([v[0] for v in vas]), "norm_post": jnp.stack([v[1] for v in vas]),
            "rel_bias": bias_grad(db.reshape(8, 128, WIN), "bias_grad")[:, :257], "conv_w": v_lru[0:4], "conv_b": v_lru[4],
            "lru_wa": _diag_blocks(dwa_bd), "lru_ba": v_lru[5], "lru_wx": _diag_blocks(dwx_bd), "lru_bx": v_lru[6],
            "lru_lambda": v_lru[7]}
    full_shapes = {"b_ada": (9 * D,), "norm_pre": (3, D), "norm_post": (3, D), "rel_bias": (8, 257),
                   "conv_w": (4, D), "conv_b": (D,), "lru_wa": (16, 64, 64), "lru_ba": (D,),
                   "lru_wx": (16, 64, 64), "lru_bx": (D,), "lru_lambda": (D,)}
    g3 = ag_small(_pack([part[n] for n in SMALL], 1232), "ag_small_grads")
    red = dict(zip(SMALL, _unpack(sum_lead(g3, "sum_small_grads"), [full_shapes[n] for n in SMALL])))
    cols = lambda a: lax.dynamic_slice(a, (0, p * 256), (a.shape[0], 256))
    grads.update({"b_ada": red["b_ada"][None], "norm_pre": cols(red["norm_pre"])[None],
                  "norm_post": cols(red["norm_post"])[None], "rel_bias": red["rel_bias"][None],
                  "conv_w": cols(red["conv_w"])[None], "conv_b": red["conv_b"][None], "lru_wa": red["lru_wa"][None],
                  "lru_ba": red["lru_ba"][None], "lru_wx": red["lru_wx"][None], "lru_bx": red["lru_bx"][None],
                  "lru_lambda": red["lru_lambda"][None]})

    dmod_all = g3[:, :72].reshape(8, 9 * D)
    dmod_cols = jnp.pad(lax.dynamic_slice(dmod_all, (0, p * 2304), (8, 2304)), ((0, 120), (0, 0)))
    c_all_t = jnp.pad(c_all.T, ((0, 0), (0, 120)))
    grads["w_ada"] = ada_bwd(c_all_t, dmod_cols, "ada_bwd")[None]

    delta, new_m, new_v = {}, {}, {}

    def update(n):
        shp = W[n].shape
        d_, m_, v_ = adamw(W[n][0], grads[n][0], M[n][0], V[n][0], "adamw_" + n)
        delta[n], new_m[n], new_v[n] = d_.reshape(shp), m_.reshape(shp), v_.reshape(shp)

    for n in ("w_ada", "ffn2_w_gu", "ffn2_w_down", "w_in", "w_att_o", "w_rec_o", "w_out"):
        update(n)
    packed = [_pack([src[n] for n in SMALL], 1168) for src in (W, grads, M, V)]
    outs = adamw(*packed, "adamw_small")
    for dst, blk in zip((delta, new_m, new_v), outs):
        for n, a in zip(SMALL, _unpack(blk, [W[n].shape for n in SMALL])):
            dst[n] = a
    reduce_end(("ffn1_w_gu", "ffn1_w_down"), fly_ffn1,
               [outs[0], delta["w_ada"], delta["ffn2_w_gu"], delta["ffn2_w_down"], delta["w_in"], delta["w_out"]], "ffn1")
    for n in ("ffn1_w_gu", "ffn1_w_down"):
        update(n)

    return (loss, dx0[None], *[grads[n] for n in WEIGHTS], *[delta[n] for n in WEIGHTS],
            *[new_m[n] for n in WEIGHTS], *[new_v[n] for n in WEIGHTS])
```

```python
import functools

import numpy as np
import jax
import jax.numpy as jnp
from jax import lax
from jax.experimental import pallas as pl
from jax.experimental.pallas import tpu as pltpu

F32 = jnp.float32
BF16 = jnp.bfloat16

D = 1024
FF = 2816
PW = 5632
HP = 128
CHUNK = 64
WIN = 640
TQ = 512
EPS = 1e-6
NEG = -1e30
LRU_C = 8.0
N_DEV = 8
VMEM_LIMIT = 50 * 1024 * 1024

ADAM_LR, ADAM_B1, ADAM_B2, ADAM_EPS, ADAM_WD, ADAM_STEP = 0.001, 0.9, 0.999, 1e-08, 0.01, 10

MESH = pl.DeviceIdType.MESH
ANY = pl.BlockSpec(memory_space=pl.ANY)


def _cp(*sem):
    return pltpu.CompilerParams(dimension_semantics=tuple(sem), vmem_limit_bytes=VMEM_LIMIT)


def _dot(a, b):
    return jnp.dot(a, b, preferred_element_type=F32)


def _dot_nt(a, b):
    return lax.dot_general(a, b, (((1,), (1,)), ((), ())), preferred_element_type=F32)


def _dot_tn(a, b):
    return lax.dot_general(a, b, (((0,), (0,)), ((), ())), preferred_element_type=F32)


def _mean(v):
    return jnp.mean(v, axis=-1, keepdims=True)


def _colsum(v):
    return jnp.sum(v, axis=0, keepdims=True)


def _sigmoid(v):
    return 0.5 * jnp.tanh(0.5 * v) + 0.5


_GK = 0.7978845608028654


def _gelu(v):
    t = jnp.tanh(_GK * (v + 0.044715 * v * v * v))
    return 0.5 * v * (1.0 + t)


def _pre_norm(xv, vec_ref):
    r = lax.rsqrt(_mean(xv * xv) + EPS)
    n = xv * r * vec_ref[0:1, :]
    return n * (1.0 + vec_ref[3:4, :]) + vec_ref[2:3, :]


def _pre_norm_bwd(dh, xv, dres, vec_ref, vacc_ref):
    r = lax.rsqrt(_mean(xv * xv) + EPS)
    xh = xv * r
    n = xh * vec_ref[0:1, :]
    vacc_ref[2:3, :] += _colsum(dh)
    vacc_ref[3:4, :] += _colsum(dh * n)
    dn = dh * (1.0 + vec_ref[3:4, :])
    vacc_ref[0:1, :] += _colsum(dn * xh)
    dxh = dn * vec_ref[0:1, :]
    return r * (dxh - xh * _mean(dxh * xh)) + dres


def _post_norm_bwd(dxo, fv, res, vec_ref, vacc_ref):
    rf = lax.rsqrt(_mean(fv * fv) + EPS)
    fh = fv * rf
    gp = vec_ref[1:2, :]
    vacc_ref[4:5, :] += _colsum(res * dxo * (fh * gp))
    dy = (res * vec_ref[4:5, :]) * dxo
    vacc_ref[1:2, :] += _colsum(dy * fh)
    dfn = dy * gp
    return rf * (dfn - fh * _mean(dfn * fh))


def ffn_fwd(x, vec, w_gu, w_dn, res, name, tm=1024, tf=256):
    S = x.shape[0]
    tm = min(tm, S)
    nf = FF // tf

    def body(x_ref, vec_ref, wg_ref, wu_ref, wd_ref, xo_ref, h_ref, g_ref, u_ref, a_ref, f_ref, hs, acc):
        j = pl.program_id(1)

        @pl.when(j == 0)
        def _():
            h = _pre_norm(x_ref[...], vec_ref).astype(BF16)
            hs[...] = h
            h_ref[...] = h
            acc[...] = jnp.zeros_like(acc)

        h = hs[...]
        g = _dot(h, wg_ref[...])
        u = _dot(h, wu_ref[...])
        g_ref[...] = g.astype(BF16)
        u_ref[...] = u.astype(BF16)
        a = (g * _sigmoid(g) * u).astype(BF16)
        a_ref[...] = a
        acc[...] += _dot(a, wd_ref[...])

        @pl.when(j == nf - 1)
        def _():
            f = acc[...]
            f_ref[...] = f
            y = f * lax.rsqrt(_mean(f * f) + EPS) * vec_ref[1:2, :]
            xo_ref[...] = x_ref[...] + (res * vec_ref[4:5, :]) * y

    row = lambda i, j: (i, 0)
    return pl.pallas_call(
        body, name=name, grid=(S // tm, nf),
        in_specs=[pl.BlockSpec((tm, D), row), pl.BlockSpec((8, D), lambda i, j: (0, 0)),
                  pl.BlockSpec((D, tf), lambda i, j: (0, j)), pl.BlockSpec((D, tf), lambda i, j: (0, j + nf)),
                  pl.BlockSpec((tf, D), lambda i, j: (j, 0))],
        out_specs=[pl.BlockSpec((tm, D), row), pl.BlockSpec((tm, D), row),
                   pl.BlockSpec((tm, tf), lambda i, j: (i, j)), pl.BlockSpec((tm, tf), lambda i, j: (i, j)),
                   pl.BlockSpec((tm, tf), lambda i, j: (i, j)), pl.BlockSpec((tm, D), row)],
        out_shape=[jax.ShapeDtypeStruct((S, D), F32), jax.ShapeDtypeStruct((S, D), BF16),
                   jax.ShapeDtypeStruct((S, FF), BF16), jax.ShapeDtypeStruct((S, FF), BF16),
                   jax.ShapeDtypeStruct((S, FF), BF16), jax.ShapeDtypeStruct((S, D), F32)],
        scratch_shapes=[pltpu.VMEM((tm, D), BF16), pltpu.VMEM((tm, D), F32)],
        compiler_params=_cp("parallel", "arbitrary"),
    )(x, vec, w_gu, w_gu, w_dn)


def ffn_bwd(dxo, x, f, g, u, vec, w_gu, w_u, w_dn, res, name, tm=512, tf=512):
    S = x.shape[0]
    tm = min(tm, S)
    nf = -(-FF // tf)
    tail = FF - tf * (nf - 1)

    def body(dxo_ref, x_ref, f_ref, g_ref, u_ref, vec_ref, wg_ref, wu_ref, wd_ref,
             dx_ref, df_ref, dgu_ref, vacc_ref, dfs, acc):
        i, j = pl.program_id(0), pl.program_id(1)

        @pl.when((i == 0) & (j == 0))
        def _():
            vacc_ref[...] = jnp.zeros_like(vacc_ref)

        @pl.when(j == 0)
        def _():
            df = _post_norm_bwd(dxo_ref[...], f_ref[...], res, vec_ref, vacc_ref).astype(BF16)
            dfs[...] = df
            df_ref[...] = df
            acc[...] = jnp.zeros_like(acc)

        def chunk(w):
            da = _dot_nt(dfs[...], wd_ref[0:w, :])
            gv, uv = g_ref[:, 0:w].astype(F32), u_ref[:, 0:w].astype(F32)
            sg = _sigmoid(gv)
            dg = (da * uv * (sg * (1.0 + gv * (1.0 - sg)))).astype(BF16)
            du = (da * (gv * sg)).astype(BF16)
            dgu_ref[0, :, 0:w] = dg
            dgu_ref[1, :, 0:w] = du
            acc[...] += _dot_nt(dg, wg_ref[:, 0:w]) + _dot_nt(du, wu_ref[:, 0:w])

        @pl.when(j < nf - 1)
        def _():
            chunk(tf)

        @pl.when(j == nf - 1)
        def _():
            chunk(tail)
            dx_ref[...] = _pre_norm_bwd(acc[...], x_ref[...], dxo_ref[...], vec_ref, vacc_ref)

    row = lambda i, j: (i, 0)
    col = lambda i, j: (i, j)
    return pl.pallas_call(
        body, name=name, grid=(S // tm, nf),
        in_specs=[pl.BlockSpec((tm, D), row), pl.BlockSpec((tm, D), row), pl.BlockSpec((tm, D), row),
                  pl.BlockSpec((tm, tf), col), pl.BlockSpec((tm, tf), col),
                  pl.BlockSpec((8, D), lambda i, j: (0, 0)),
                  pl.BlockSpec((D, tf), lambda i, j: (0, j)), pl.BlockSpec((D, tf), lambda i, j: (0, j)),
                  pl.BlockSpec((tf, D), lambda i, j: (j, 0))],
        out_specs=[pl.BlockSpec((tm, D), row), pl.BlockSpec((tm, D), row),
                   pl.BlockSpec((2, tm, tf), lambda i, j: (0, i, j)),
                   pl.BlockSpec((8, D), lambda i, j: (0, 0))],
        out_shape=[jax.ShapeDtypeStruct((S, D), F32), jax.ShapeDtypeStruct((S, D), BF16),
                   jax.ShapeDtypeStruct((2, S, FF), BF16), jax.ShapeDtypeStruct((8, D), F32)],
        scratch_shapes=[pltpu.VMEM((tm, D), BF16), pltpu.VMEM((tm, D), F32)],
        compiler_params=_cp("arbitrary", "arbitrary"),
    )(dxo, x, f, g, u, vec, w_gu, w_u, w_dn)


def mm_tn(a, b, name, tm, tn, tk, out_dtype=BF16, prev=None, col_off=0, n_total=None):
    S, M = a.shape
    if b.ndim == 3:
        G, _, Nf = b.shape
    else:
        G, Nf = 1, b.shape[1]
    N = G * Nf
    n_total = N if n_total is None else n_total
    tk = min(tk, S)
    nbf = Nf // tn
    nk = S // tk
    ob = col_off // tn

    def body(*refs):
        a_ref, b_ref = refs[0], refs[1]
        o_ref, acc = refs[-2], refs[-1]
        k = pl.program_id(2)

        @pl.when(k == 0)
        def _():
            acc[...] = jnp.zeros_like(acc)

        acc[...] += _dot_tn(a_ref[...], b_ref[...])

        @pl.when(k == nk - 1)
        def _():
            o_ref[...] = acc[...].astype(out_dtype)

    if b.ndim == 3:
        b_spec = pl.BlockSpec((None, tk, tn), lambda i, j, k: (j // nbf, k, j % nbf))
    else:
        b_spec = pl.BlockSpec((tk, tn), lambda i, j, k: (k, j))
    in_specs = [pl.BlockSpec((tk, tm), lambda i, j, k: (k, i)), b_spec]
    args = [a, b]
    aliases = {}
    if prev is not None:
        in_specs.append(ANY)
        args.append(prev)
        aliases = {2: 0}
    return pl.pallas_call(
        body, name=name, grid=(M // tm, N // tn, nk),
        in_specs=in_specs,
        out_specs=pl.BlockSpec((tm, tn), lambda i, j, k: (i, j + ob)),
        out_shape=jax.ShapeDtypeStruct((M, n_total), out_dtype),
        scratch_shapes=[pltpu.VMEM((tm, tn), F32)],
        input_output_aliases=aliases,
        compiler_params=_cp("parallel", "parallel", "arbitrary"),
    )(*args)


def proj_fwd(x, vec, w_in, name, tm=1024, tn=512):
    S = x.shape[0]
    tm = min(tm, S)
    nq = 1536 // tn

    def body(x_ref, vec_ref, w_ref, h_ref, qkv_ref, rest_ref, hs):
        j = pl.program_id(1)

        @pl.when(j == 0)
        def _():
            h = _pre_norm(x_ref[...], vec_ref).astype(BF16)
            hs[...] = h
            h_ref[...] = h

        r = _dot(hs[...], w_ref[...])

        @pl.when(j < nq)
        def _():
            qkv_ref[...] = r.astype(BF16)

        @pl.when(j >= nq)
        def _():
            rest_ref[...] = r

    row = lambda i, j: (i, 0)
    return pl.pallas_call(
        body, name=name, grid=(S // tm, PW // tn),
        in_specs=[pl.BlockSpec((tm, D), row), pl.BlockSpec((8, D), lambda i, j: (0, 0)),
                  pl.BlockSpec((D, tn), lambda i, j: (0, j))],
        out_specs=[pl.BlockSpec((tm, D), row),
                   pl.BlockSpec((tm, tn), lambda i, j: (i, jnp.minimum(j, nq - 1))),
                   pl.BlockSpec((tm, tn), lambda i, j: (i, jnp.maximum(j - nq, 0)))],
        out_shape=[jax.ShapeDtypeStruct((S, D), BF16), jax.ShapeDtypeStruct((S, 1536), BF16),
                   jax.ShapeDtypeStruct((S, 4096), F32)],
        scratch_shapes=[pltpu.VMEM((tm, D), BF16)],
        compiler_params=_cp("parallel", "arbitrary"),
    )(x, vec, w_in)


def proj_bwd(dq, dkv, dxr, d3, w_in, x, dxo, vec, name, tm=1024, tk=512):
    S = x.shape[0]
    tm = min(tm, S)
    nk = PW // tk

    def body(dq_ref, dkv_ref, dxr_ref, d3_ref, w_ref, x_ref, dxo_ref, vec_ref, dx_ref, vacc_ref, acc):
        i, j = pl.program_id(0), pl.program_id(1)

        @pl.when((i == 0) & (j == 0))
        def _():
            vacc_ref[...] = jnp.zeros_like(vacc_ref)

        @pl.when(j == 0)
        def _():
            acc[...] = _dot_nt(dq_ref[...], w_ref[...])

        @pl.when((j >= 1) & (j < 3))
        def _():
            acc[...] += _dot_nt(dkv_ref[...], w_ref[...])

        @pl.when((j >= 3) & (j < 5))
        def _():
            acc[...] += _dot_nt(dxr_ref[...], w_ref[...])

        @pl.when(j >= 5)
        def _():
            acc[...] += _dot_nt(d3_ref[...], w_ref[...])

        @pl.when(j == nk - 1)
        def _():
            dx_ref[...] = _pre_norm_bwd(acc[...], x_ref[...], dxo_ref[...], vec_ref, vacc_ref)

    row = lambda i, j: (i, 0)
    return pl.pallas_call(
        body, name=name, grid=(S // tm, nk),
        in_specs=[pl.BlockSpec((None, tm, tk), lambda i, j: (0, i, 0)),
                  pl.BlockSpec((None, tm, tk), lambda i, j: (jnp.clip(j - 1, 0, 1), i, 0)),
                  pl.BlockSpec((tm, tk), lambda i, j: (i, jnp.clip(j - 3, 0, 1))),
                  pl.BlockSpec((None, tm, tk), lambda i, j: (jnp.clip(j - 5, 0, 5) // 2, i, jnp.clip(j - 5, 0, 5) % 2)),
                  pl.BlockSpec((D, tk), lambda i, j: (0, j)),
                  pl.BlockSpec((tm, D), row), pl.BlockSpec((tm, D), row),
                  pl.BlockSpec((8, D), lambda i, j: (0, 0))],
        out_specs=[pl.BlockSpec((tm, D), row), pl.BlockSpec((8, D), lambda i, j: (0, 0))],
        out_shape=[jax.ShapeDtypeStruct((S, D), F32), jax.ShapeDtypeStruct((8, D), F32)],
        scratch_shapes=[pltpu.VMEM((tm, D), F32)],
        compiler_params=_cp("arbitrary", "arbitrary"),
    )(dq, dkv, dxr, d3, w_in, x, dxo, vec)


def _two_heads(v, lane):
    zero = jnp.zeros((), v.dtype)
    return jnp.concatenate([jnp.where(lane < 64, v, zero), jnp.where(lane >= 64, v, zero)], axis=0)


def _attn_probs(qm, ka, bias_h, i, grp):
    s = _dot_nt(qm, ka) + bias_h
    col = lax.broadcasted_iota(jnp.int32, s.shape, 1)
    first_key = jnp.where(i == 0, 512 - 128 * grp, 0)
    s = jnp.where(col >= first_key, s, NEG)
    e = jnp.exp(s - jnp.max(s, axis=-1, keepdims=True))
    return e * (1.0 / jnp.sum(e, axis=-1, keepdims=True))


def attn_fwd(qkv, bias, name):
    S = qkv.shape[0]
    nb = S // TQ

    def body(q_ref, kp_ref, kc_ref, vp_ref, vc_ref, b_ref, o_ref, kw, vw):
        i = pl.program_id(1)
        kw[0:TQ, :] = kp_ref[...]
        kw[TQ:2 * TQ, :] = kc_ref[...]
        vw[0:TQ, :] = vp_ref[...]
        vw[TQ:2 * TQ, :] = vc_ref[...]
        lane = lax.broadcasted_iota(jnp.int32, (1, HP), 1)

        def group(a, carry):
            r0 = pl.multiple_of(a * 128, 128)
            qa = q_ref[pl.ds(r0, 128), :] * jnp.asarray(0.125, BF16)
            ka = kw[pl.ds(r0, WIN), :]
            va = vw[pl.ds(r0, WIN), :]
            p = _attn_probs(_two_heads(qa, lane), ka, b_ref[...], i, a)
            o2 = _dot(p.astype(BF16), va)
            o_ref[pl.ds(r0, 128), :] = jnp.where(lane < 64, o2[0:128], o2[128:256]).astype(BF16)
            return carry

        lax.fori_loop(0, TQ // 128, group, 0, unroll=True)

    prev = lambda h, i: (jnp.maximum(i - 1, 0), 0)
    return pl.pallas_call(
        body, name=name, grid=(4, nb),
        in_specs=[pl.BlockSpec((TQ, HP), lambda h, i: (i, h)),
                  pl.BlockSpec((TQ, HP), lambda h, i: (jnp.maximum(i - 1, 0), 4 + h)),
                  pl.BlockSpec((TQ, HP), lambda h, i: (i, 4 + h)),
                  pl.BlockSpec((TQ, HP), lambda h, i: (jnp.maximum(i - 1, 0), 8 + h)),
                  pl.BlockSpec((TQ, HP), lambda h, i: (i, 8 + h)),
                  pl.BlockSpec((None, 256, WIN), lambda h, i: (h, 0, 0))],
        out_specs=pl.BlockSpec((TQ, HP), lambda h, i: (i, h)),
        out_shape=jax.ShapeDtypeStruct((S, 512), BF16),
        scratch_shapes=[pltpu.VMEM((2 * TQ, HP), BF16), pltpu.VMEM((2 * TQ, HP), BF16)],
        compiler_params=_cp("parallel", "arbitrary"),
    )(qkv, qkv, qkv, qkv, qkv, bias)


def attn_bwd(qkv, do, bias, name):
    S = qkv.shape[0]
    nb = S // TQ

    def body(q_ref, kp_ref, kc_ref, vp_ref, vc_ref, do_ref, b_ref, dqkv_ref, db_ref, dkv_ref, kw, vw, ak, av):
        i = pl.program_id(1)

        @pl.when(i == 0)
        def _():
            db_ref[...] = jnp.zeros_like(db_ref)
            ak[...] = jnp.zeros_like(ak)
            av[...] = jnp.zeros_like(av)

        @pl.when(i > 0)
        def _():
            ak[0:TQ, :] = ak[TQ:2 * TQ, :]
            av[0:TQ, :] = av[TQ:2 * TQ, :]
            ak[TQ:2 * TQ, :] = jnp.zeros((TQ, HP), F32)
            av[TQ:2 * TQ, :] = jnp.zeros((TQ, HP), F32)

        @pl.when(i < nb)
        def _():
            kw[0:TQ, :] = kp_ref[...]
            kw[TQ:2 * TQ, :] = kc_ref[...]
            vw[0:TQ, :] = vp_ref[...]
            vw[TQ:2 * TQ, :] = vc_ref[...]
            lane = lax.broadcasted_iota(jnp.int32, (1, HP), 1)

            def group(a, carry):
                r0 = pl.multiple_of(a * 128, 128)
                q2 = _two_heads(q_ref[pl.ds(r0, 128), :] * jnp.asarray(0.125, BF16), lane)
                do2 = _two_heads(do_ref[pl.ds(r0, 128), :], lane)
                ka = kw[pl.ds(r0, WIN), :]
                va = vw[pl.ds(r0, WIN), :]
                p = _attn_probs(q2, ka, b_ref[...], i, a)
                dp = _dot_nt(do2, va)
                ds = p * (dp - jnp.sum(p * dp, axis=-1, keepdims=True))
                db_ref[...] += ds
                dsb = ds.astype(BF16)
                dq2 = _dot(dsb, ka)
                ak[pl.ds(r0, WIN), :] += _dot_tn(dsb, q2)
                av[pl.ds(r0, WIN), :] += _dot_tn(p.astype(BF16), do2)
                dq = jnp.where(lane < 64, dq2[0:128], dq2[128:256])
                dqkv_ref[0, pl.ds(r0, 128), :] = (dq * 0.125).astype(BF16)
                return carry

            lax.fori_loop(0, TQ // 128, group, 0, unroll=True)

        @pl.when(i > 0)
        def _():
            dkv_ref[0] = ak[0:TQ, :].astype(BF16)
            dkv_ref[1] = av[0:TQ, :].astype(BF16)

    cur = lambda i: jnp.minimum(i, nb - 1)
    prv = lambda i: jnp.clip(i - 1, 0, nb - 1)
    dq, db, dkv = pl.pallas_call(
        body, name=name, grid=(4, nb + 1),
        in_specs=[pl.BlockSpec((TQ, HP), lambda h, i: (cur(i), h)),
                  pl.BlockSpec((TQ, HP), lambda h, i: (prv(i), 4 + h)),
                  pl.BlockSpec((TQ, HP), lambda h, i: (cur(i), 4 + h)),
                  pl.BlockSpec((TQ, HP), lambda h, i: (prv(i), 8 + h)),
                  pl.BlockSpec((TQ, HP), lambda h, i: (cur(i), 8 + h)),
                  pl.BlockSpec((TQ, HP), lambda h, i: (cur(i), h)),
                  pl.BlockSpec((None, 256, WIN), lambda h, i: (h, 0, 0))],
        out_specs=[pl.BlockSpec((1, TQ, HP), lambda h, i: (0, cur(i), h)),
                   pl.BlockSpec((None, 256, WIN), lambda h, i: (h, 0, 0)),
                   pl.BlockSpec((2, TQ, HP), lambda h, i: (0, prv(i), h))],
        out_shape=[jax.ShapeDtypeStruct((1, S, 512), BF16), jax.ShapeDtypeStruct((4, 256, WIN), F32),
                   jax.ShapeDtypeStruct((2, S, 512), BF16)],
        scratch_shapes=[pltpu.VMEM((2 * TQ, HP), BF16), pltpu.VMEM((2 * TQ, HP), BF16),
                        pltpu.VMEM((2 * TQ, HP), F32), pltpu.VMEM((2 * TQ, HP), F32)],
        compiler_params=_cp("parallel", "arbitrary"),
    )(qkv, qkv, qkv, qkv, qkv, do, bias)
    return dq, db, dkv


def bias_grad(db, name):
    def body(db_ref, o_ref):
        r = lax.broadcasted_iota(jnp.int32, (128, 128), 0)
        c = lax.broadcasted_iota(jnp.int32, (128, 128), 1)
        flip = (r + c == 127).astype(BF16)
        lane = lax.broadcasted_iota(jnp.int32, (16, 384), 1)
        src = lax.broadcasted_iota(jnp.int32, (128, 384), 0)
        dst = lax.broadcasted_iota(jnp.int32, (128, 384), 1)

        def split_dot(v, m):
            hi = v.astype(BF16)
            r1 = v - hi.astype(F32)
            mid = r1.astype(BF16)
            lo = (r1 - mid.astype(F32)).astype(BF16)
            return _dot(hi, m) + _dot(mid, m) + _dot(lo, m)

        def diag_sums(w):
            y = pltpu.roll(split_dot(w, flip), 0, 1, stride=1, stride_axis=0)
            return jnp.broadcast_to(_colsum(y), (16, 128))

        w4 = db_ref[0, :, 512:640]
        w3 = db_ref[0, :, 384:512]
        far = jnp.sum(db_ref[0, :, 0:384]) + jnp.sum(jnp.where(r >= c, w3, 0.0))
        lo4 = diag_sums(jnp.where(r >= c, w4, 0.0))
        up4 = diag_sums(jnp.where(r < c, w4, 0.0))
        up3 = diag_sums(jnp.where(r < c, w3, 0.0))
        p_lo4 = (dst == 128 + (src + 1) % 128).astype(BF16)
        p_up4 = ((dst == src + 1) & (src < 127)).astype(BF16)
        p_up3 = ((dst == src + 129) & (src < 127)).astype(BF16)
        out = split_dot(lo4, p_lo4) + split_dot(up4, p_up4) + split_dot(up3, p_up3)
        o_ref[0] = out + jnp.where(lane == 256, far, 0.0)

    return pl.pallas_call(
        body, name=name, grid=(8,),
        in_specs=[pl.BlockSpec((1, 128, WIN), lambda h: (h, 0, 0))],
        out_specs=pl.BlockSpec((1, 16, 384), lambda h: (h, 0, 0)),
        out_shape=jax.ShapeDtypeStruct((8, 16, 384), F32),
        compiler_params=_cp("parallel"),
    )(db)[:, 0, :]


LT = 256
LC = 512


def _lru_gates(xs, pv_ref, wa_ref, wx_ref, tl):
    xc = (pv_ref[4:5, :] + pv_ref[3:4, :] * xs[pl.ds(8, tl), :] + pv_ref[2:3, :] * xs[pl.ds(7, tl), :]
          + pv_ref[1:2, :] * xs[pl.ds(6, tl), :] + pv_ref[0:1, :] * xs[pl.ds(5, tl), :])
    xcb = xc.astype(BF16)
    pa = jnp.concatenate([_dot(xcb[:, 0:256], wa_ref[0]), _dot(xcb[:, 256:512], wa_ref[1])], axis=1)
    px = jnp.concatenate([_dot(xcb[:, 0:256], wx_ref[0]), _dot(xcb[:, 256:512], wx_ref[1])], axis=1)
    r = _sigmoid(pa + pv_ref[5:6, :])
    ig = _sigmoid(px + pv_ref[6:7, :])
    z = -pv_ref[7:8, :]
    sp = jnp.maximum(z, 0.0) + jnp.log1p(jnp.exp(-jnp.abs(z)))
    log_a = (-LRU_C * r) * sp
    a = jnp.exp(log_a)
    s = jnp.tanh(-log_a) * (1.0 + a * a)
    inv_mult = lax.rsqrt(s)
    mult = jnp.where(s > 0.0, s * inv_mult, 0.0)
    return xc, xcb, r, ig, sp, a, mult, inv_mult


def lru_fwd(rest, pvec, wa, wx, name):
    S = rest.shape[0]
    tl = min(LT, S)
    nt = S // tl

    def body(xr_ref, halo_ref, yr_ref, pv_ref, wa_ref, wx_ref, h_ref, hg_ref, xs, a_s, u_s, h_s, carry):
        ti = pl.program_id(1)

        @pl.when(ti == 0)
        def _():
            carry[...] = jnp.zeros_like(carry)

        xs[0:8, :] = jnp.where(ti > 0, halo_ref[...], 0.0)
        xs[pl.ds(8, tl), :] = xr_ref[...]
        xc, _, _, ig, _, a, mult, _ = _lru_gates(xs, pv_ref, wa_ref, wx_ref, tl)
        a_s[...] = a
        u_s[...] = mult * (ig * xc)
        row = lax.broadcasted_iota(jnp.int32, (8, LC), 0)

        def blk(bi, c):
            o = pl.multiple_of(bi * 8, 8)
            av = a_s[pl.ds(o, 8), :]
            bv = u_s[pl.ds(o, 8), :]
            for d in (1, 2, 4):
                a_sh = pltpu.roll(av, d, 0)
                b_sh = pltpu.roll(bv, d, 0)
                m = row >= d
                bv = jnp.where(m, av * b_sh + bv, bv)
                av = jnp.where(m, av * a_sh, av)
            hv = bv + av * c
            h_s[pl.ds(o, 8), :] = hv
            return hv[7:8, :]

        carry[...] = lax.fori_loop(0, tl // 8, blk, carry[...])
        h = h_s[...]
        h_ref[...] = h
        hg_ref[...] = (h * _gelu(yr_ref[...])).astype(BF16)

    hb = tl // 8
    return pl.pallas_call(
        body, name=name, grid=(2, nt),
        in_specs=[pl.BlockSpec((tl, LC), lambda c, t: (t, c)),
                  pl.BlockSpec((8, LC), lambda c, t: (jnp.maximum(t * hb - 1, 0), c)),
                  pl.BlockSpec((tl, LC), lambda c, t: (t, 2 + c)),
                  pl.BlockSpec((8, LC), lambda c, t: (0, c)),
                  pl.BlockSpec((2, 256, 256), lambda c, t: (c, 0, 0)),
                  pl.BlockSpec((2, 256, 256), lambda c, t: (c, 0, 0))],
        out_specs=[pl.BlockSpec((tl, LC), lambda c, t: (t, c)), pl.BlockSpec((tl, LC), lambda c, t: (t, c))],
        out_shape=[jax.ShapeDtypeStruct((S, D), F32), jax.ShapeDtypeStruct((S, D), BF16)],
        scratch_shapes=[pltpu.VMEM((tl + 8, LC), F32), pltpu.VMEM((tl, LC), F32), pltpu.VMEM((tl, LC), F32),
                        pltpu.VMEM((tl, LC), F32), pltpu.VMEM((1, LC), F32)],
        compiler_params=_cp("parallel", "arbitrary"),
    )(rest, rest, rest, pvec, wa, wx)


def lru_bwd(dh, h, rest, pvec, wa, wx, name):
    S = rest.shape[0]
    tl = min(LT, S)
    nt = S // tl

    def body(dh_ref, h_ref, hhalo_ref, xr_ref, xhalo_ref, pv_ref, wa_ref, wx_ref,
             dxr_ref, vacc_ref, dwa_ref, dwx_ref,
             xs, hs, a_s, ash_s, b_s, lam_s, dxe, anext, lnext, dxnext):
        ti = pl.program_id(1)
        tr = nt - 1 - ti

        @pl.when(ti == 0)
        def _():
            anext[...] = jnp.zeros_like(anext)
            lnext[...] = jnp.zeros_like(lnext)
            dxnext[...] = jnp.zeros_like(dxnext)
            vacc_ref[...] = jnp.zeros_like(vacc_ref)
            dwa_ref[...] = jnp.zeros_like(dwa_ref)
            dwx_ref[...] = jnp.zeros_like(dwx_ref)

        xs[0:8, :] = jnp.where(tr > 0, xhalo_ref[...], 0.0)
        xs[pl.ds(8, tl), :] = xr_ref[...]
        xc, xcb, r, ig, sp, a, mult, inv_mult = _lru_gates(xs, pv_ref, wa_ref, wx_ref, tl)

        a_s[pl.ds(0, tl), :] = a
        a_s[pl.ds(tl, 8), :] = jnp.broadcast_to(anext[...], (8, LC))
        ash_s[...] = a_s[pl.ds(1, tl), :]
        b_s[...] = dh_ref[...]
        row = lax.broadcasted_iota(jnp.int32, (8, LC), 0)

        def blk(k, c):
            o = pl.multiple_of((tl // 8 - 1 - k) * 8, 8)
            av = ash_s[pl.ds(o, 8), :]
            bv = b_s[pl.ds(o, 8), :]
            for d in (1, 2, 4):
                a_sh = pltpu.roll(av, 8 - d, 0)
                b_sh = pltpu.roll(bv, 8 - d, 0)
                m = row < 8 - d
                bv = jnp.where(m, bv + av * b_sh, bv)
                av = jnp.where(m, av * a_sh, av)
            lv = bv + av * c
            lam_s[pl.ds(o, 8), :] = lv
            return lv[0:1, :]

        lnext[...] = lax.fori_loop(0, tl // 8, blk, lnext[...])
        anext[...] = a[0:1, :]
        lam = lam_s[...]

        hs[0:8, :] = jnp.where(tr > 0, hhalo_ref[...], 0.0)
        hs[pl.ds(8, tl), :] = h_ref[...]
        d_a = lam * hs[pl.ds(7, tl), :]
        d_mult = lam * (ig * xc)
        d_ig = lam * mult * xc
        dxc = lam * mult * ig
        d_log_a = d_a * a - d_mult * (a * a) * inv_mult
        d_r = d_log_a * (-LRU_C * sp)
        vacc_ref[7:8, :] += _colsum(d_log_a * (-LRU_C * r)) * (-_sigmoid(-pv_ref[7:8, :]))
        d_pa = d_r * r * (1.0 - r)
        d_px = d_ig * ig * (1.0 - ig)
        vacc_ref[5:6, :] += _colsum(d_pa)
        vacc_ref[6:7, :] += _colsum(d_px)
        dpa = d_pa.astype(BF16)
        dpx = d_px.astype(BF16)
        back = []
        for g in range(2):
            sl = slice(256 * g, 256 * g + 256)
            dwa_ref[g] += _dot_tn(xcb[:, sl], dpa[:, sl])
            dwx_ref[g] += _dot_tn(xcb[:, sl], dpx[:, sl])
            back.append(_dot_nt(dpa[:, sl], wa_ref[g]) + _dot_nt(dpx[:, sl], wx_ref[g]))
        dxc = dxc + jnp.concatenate(back, axis=1)
        vacc_ref[4:5, :] += _colsum(dxc)
        for k in range(4):
            vacc_ref[k:k + 1, :] += _colsum(dxc * xs[pl.ds(5 + k, tl), :])
        dxe[pl.ds(0, tl), :] = dxc
        dxe[pl.ds(tl, 8), :] = dxnext[...]
        dxr = (pv_ref[3:4, :] * dxc + pv_ref[2:3, :] * dxe[pl.ds(1, tl), :]
               + pv_ref[1:2, :] * dxe[pl.ds(2, tl), :] + pv_ref[0:1, :] * dxe[pl.ds(3, tl), :])
        dxr_ref[...] = dxr.astype(BF16)
        dxnext[...] = dxc[0:8, :]

    hb = tl // 8
    rev = lambda t: nt - 1 - t
    halo = lambda t: jnp.maximum(rev(t) * hb - 1, 0)
    big = lambda: pltpu.VMEM((tl + 8, LC), F32)
    til = lambda: pltpu.VMEM((tl, LC), F32)
    return pl.pallas_call(
        body, name=name, grid=(2, nt),
        in_specs=[pl.BlockSpec((tl, LC), lambda c, t: (rev(t), c)),
                  pl.BlockSpec((tl, LC), lambda c, t: (rev(t), c)),
                  pl.BlockSpec((8, LC), lambda c, t: (halo(t), c)),
                  pl.BlockSpec((tl, LC), lambda c, t: (rev(t), c)),
                  pl.BlockSpec((8, LC), lambda c, t: (halo(t), c)),
                  pl.BlockSpec((8, LC), lambda c, t: (0, c)),
                  pl.BlockSpec((2, 256, 256), lambda c, t: (c, 0, 0)),
                  pl.BlockSpec((2, 256, 256), lambda c, t: (c, 0, 0))],
        out_specs=[pl.BlockSpec((tl, LC), lambda c, t: (rev(t), c)),
                   pl.BlockSpec((8, LC), lambda c, t: (0, c)),
                   pl.BlockSpec((2, 256, 256), lambda c, t: (c, 0, 0)),
                   pl.BlockSpec((2, 256, 256), lambda c, t: (c, 0, 0))],
        out_shape=[jax.ShapeDtypeStruct((S, D), BF16), jax.ShapeDtypeStruct((8, D), F32),
                   jax.ShapeDtypeStruct((4, 256, 256), F32), jax.ShapeDtypeStruct((4, 256, 256), F32)],
        scratch_shapes=[big(), big(), big(), til(), til(), til(), big(),
                        pltpu.VMEM((1, LC), F32), pltpu.VMEM((1, LC), F32), pltpu.VMEM((8, LC), F32)],
        compiler_params=_cp("parallel", "arbitrary"),
    )(dh, h, h, rest, rest, pvec, wa, wx)


def mix_out_fwd(x, ao, hg, rest, vec, w_att_o, w_rec_o, w_out, name, tm=256):
    S = x.shape[0]
    tm = min(tm, S)

    def body(x_ref, ao_ref, hg_ref, ga_ref, gr_ref, vec_ref, wa_ref, wr_ref, wo_ref,
             xo_ref, att_ref, rec_ref, mg_ref, f_ref):
        att = _dot(ao_ref[...], wa_ref[...])
        rec = _dot(hg_ref[...], wr_ref[...])
        att_ref[...] = att
        rec_ref[...] = rec
        mg = (_sigmoid(ga_ref[...]) * att + _sigmoid(gr_ref[...]) * rec).astype(BF16)
        mg_ref[...] = mg
        f = _dot(mg, wo_ref[...])
        f_ref[...] = f
        y = f * lax.rsqrt(_mean(f * f) + EPS) * vec_ref[1:2, :]
        xo_ref[...] = x_ref[...] + (1.0 * vec_ref[4:5, :]) * y

    row = lambda i: (i, 0)
    full = lambda r: pl.BlockSpec((r, D), lambda i: (0, 0))
    return pl.pallas_call(
        body, name=name, grid=(S // tm,),
        in_specs=[pl.BlockSpec((tm, D), row), pl.BlockSpec((tm, 512), row), pl.BlockSpec((tm, D), row),
                  pl.BlockSpec((tm, D), lambda i: (i, 2)), pl.BlockSpec((tm, D), lambda i: (i, 3)),
                  full(8), full(512), full(D), full(D)],
        out_specs=[pl.BlockSpec((tm, D), row)] * 5,
        out_shape=[jax.ShapeDtypeStruct((S, D), F32), jax.ShapeDtypeStruct((S, D), F32),
                   jax.ShapeDtypeStruct((S, D), F32), jax.ShapeDtypeStruct((S, D), BF16),
                   jax.ShapeDtypeStruct((S, D), F32)],
        compiler_params=_cp("parallel"),
    )(x, ao, hg, rest, rest, vec, w_att_o, w_rec_o, w_out)


def mix_out_bwd(dxo, f, att, rec, rest, h, vec, w_att_o, w_rec_o, w_out, name, tm=256):
    S = dxo.shape[0]
    tm = min(tm, S)

    def body(dxo_ref, f_ref, att_ref, rec_ref, yr_ref, ga_ref, gr_ref, h_ref, vec_ref, wa_ref, wr_ref, wo_ref,
             df_ref, da_ref, dr_ref, dao_ref, dh_ref, d3_ref, vacc_ref):
        @pl.when(pl.program_id(0) == 0)
        def _():
            vacc_ref[...] = jnp.zeros_like(vacc_ref)

        df = _post_norm_bwd(dxo_ref[...], f_ref[...], 1.0, vec_ref, vacc_ref).astype(BF16)
        df_ref[...] = df
        dm = _dot_nt(df, wo_ref[...])
        sa = _sigmoid(ga_ref[...])
        sr = _sigmoid(gr_ref[...])
        d_att = (dm * sa).astype(BF16)
        d_rec = (dm * sr).astype(BF16)
        da_ref[...] = d_att
        dr_ref[...] = d_rec
        d3_ref[1] = (dm * att_ref[...] * (sa * (1.0 - sa))).astype(BF16)
        d3_ref[2] = (dm * rec_ref[...] * (sr * (1.0 - sr))).astype(BF16)
        dao_ref[...] = _dot_nt(d_att, wa_ref[...]).astype(BF16)
        d_hg = _dot_nt(d_rec, wr_ref[...])
        yr = yr_ref[...]
        t = jnp.tanh(_GK * (yr + 0.044715 * yr * yr * yr))
        dh_ref[...] = d_hg * (0.5 * yr * (1.0 + t))
        gelu_grad = 0.5 * (1.0 + t) + 0.5 * yr * (1.0 - t * t) * _GK * (1.0 + 3.0 * 0.044715 * yr * yr)
        d3_ref[0] = (d_hg * h_ref[...] * gelu_grad).astype(BF16)

    row = lambda i: (i, 0)
    full = lambda r: pl.BlockSpec((r, D), lambda i: (0, 0))
    return pl.pallas_call(
        body, name=name, grid=(S // tm,),
        in_specs=[pl.BlockSpec((tm, D), row)] * 4
        + [pl.BlockSpec((tm, D), lambda i: (i, 1)), pl.BlockSpec((tm, D), lambda i: (i, 2)),
           pl.BlockSpec((tm, D), lambda i: (i, 3)), pl.BlockSpec((tm, D), row),
           full(8), full(512), full(D), full(D)],
        out_specs=[pl.BlockSpec((tm, D), row)] * 3
        + [pl.BlockSpec((tm, 512), row), pl.BlockSpec((tm, D), row),
           pl.BlockSpec((3, tm, D), lambda i: (0, i, 0)), pl.BlockSpec((8, D), lambda i: (0, 0))],
        out_shape=[jax.ShapeDtypeStruct((S, D), BF16)] * 3
        + [jax.ShapeDtypeStruct((S, 512), BF16), jax.ShapeDtypeStruct((S, D), F32),
           jax.ShapeDtypeStruct((3, S, D), BF16), jax.ShapeDtypeStruct((8, D), F32)],
        compiler_params=_cp("arbitrary"),
    )(dxo, f, att, rec, rest, rest, rest, h, vec, w_att_o, w_rec_o, w_out)


def loss_grad(y, tgt, name, tm=512):
    S = y.shape[0]
    tm = min(tm, S)
    nt = S // tm

    def body(y_ref, t_ref, dy_ref, l_ref, acc):
        i = pl.program_id(0)

        @pl.when(i == 0)
        def _():
            acc[...] = jnp.zeros_like(acc)

        d = y_ref[...] - t_ref[...]
        dy_ref[...] = d * (1.0 / D)
        acc[...] += _colsum(d * d)

        @pl.when(i == nt - 1)
        def _():
            l_ref[...] = jnp.broadcast_to(0.5 * jnp.sum(acc[...]) * (1.0 / D), (8, 128))

    return pl.pallas_call(
        body, name=name, grid=(nt,),
        in_specs=[pl.BlockSpec((tm, D), lambda i: (i, 0))] * 2,
        out_specs=[pl.BlockSpec((tm, D), lambda i: (i, 0)), pl.BlockSpec((8, 128), lambda i: (0, 0))],
        out_shape=[jax.ShapeDtypeStruct((S, D), F32), jax.ShapeDtypeStruct((8, 128), F32)],
        scratch_shapes=[pltpu.VMEM((1, D), F32)],
        compiler_params=_cp("arbitrary"),
    )(y, tgt)


def ada_fwd(c_all, w_ada, b_ada, name, tn=768):
    n = w_ada.shape[1]

    def body(c_ref, w_ref, b_ref, o_ref):
        cv = c_ref[...]
        ca = (cv * _sigmoid(cv)).astype(BF16)
        o_ref[...] = _dot(ca, w_ref[...].astype(BF16)) + b_ref[...]

    return pl.pallas_call(
        body, name=name, grid=(n // tn,),
        in_specs=[pl.BlockSpec((8, D), lambda j: (0, 0)), pl.BlockSpec((D, tn), lambda j: (0, j)),
                  pl.BlockSpec((1, tn), lambda j: (0, j))],
        out_specs=pl.BlockSpec((8, tn), lambda j: (0, j)),
        out_shape=jax.ShapeDtypeStruct((8, n), F32),
        compiler_params=_cp("parallel"),
    )(c_all, w_ada, b_ada)


def ada_bwd(c_all_t, dmod, name, tn=768):
    n = dmod.shape[1]

    def body(c_ref, d_ref, o_ref):
        cv = c_ref[...]
        ca = (cv * _sigmoid(cv)).astype(BF16)
        o_ref[...] = _dot(ca, d_ref[...].astype(BF16))

    return pl.pallas_call(
        body, name=name, grid=(n // tn,),
        in_specs=[pl.BlockSpec((D, 128), lambda j: (0, 0)), pl.BlockSpec((128, tn), lambda j: (0, j))],
        out_specs=pl.BlockSpec((D, tn), lambda j: (0, j)),
        out_shape=jax.ShapeDtypeStruct((D, n), F32),
        compiler_params=_cp("parallel"),
    )(c_all_t, dmod)


def _row_tile(rows, cols, itemsize=4, budget=1536 * 1024):
    best = None
    for t in range(8, rows + 1, 8):
        if rows % t == 0 and t * cols * itemsize <= budget:
            best = t
    return rows if best is None else best


def sum_lead(parts, name, out_dtype=F32):
    n, R, C = parts.shape
    tr = _row_tile(R, C * n)

    def body(p_ref, o_ref):
        acc = p_ref[0].astype(F32)
        for k in range(1, n):
            acc = acc + p_ref[k].astype(F32)
        o_ref[...] = acc.astype(out_dtype)

    return pl.pallas_call(
        body, name=name, grid=(R // tr,),
        in_specs=[pl.BlockSpec((n, tr, C), lambda i: (0, i, 0))],
        out_specs=pl.BlockSpec((tr, C), lambda i: (i, 0)),
        out_shape=jax.ShapeDtypeStruct((R, C), out_dtype),
        compiler_params=_cp("parallel"),
    )(parts)


def adamw(w, g, m, v, name):
    R, C = w.shape
    tr = _row_tile(R, C * 7, budget=8 * 1024 * 1024)

    def body(w_ref, g_ref, m_ref, v_ref, d_ref, mo_ref, vo_ref):
        gv = g_ref[...]
        mn = ADAM_B1 * m_ref[...] + (1.0 - ADAM_B1) * gv
        vn = ADAM_B2 * v_ref[...] + (1.0 - ADAM_B2) * (gv * gv)
        m_hat = mn / (1.0 - ADAM_B1 ** ADAM_STEP)
        v_hat = vn / (1.0 - ADAM_B2 ** ADAM_STEP)
        d_ref[...] = -ADAM_LR * (m_hat / (jnp.sqrt(v_hat) + ADAM_EPS) + ADAM_WD * w_ref[...])
        mo_ref[...] = mn
        vo_ref[...] = vn

    spec = pl.BlockSpec((tr, C), lambda i: (i, 0))
    return pl.pallas_call(
        body, name=name, grid=(R // tr,),
        in_specs=[spec] * 4, out_specs=[spec] * 3,
        out_shape=[jax.ShapeDtypeStruct((R, C), F32)] * 3,
        compiler_params=_cp("parallel"),
    )(w, g, m, v)


def _mesh_pos():
    return lax.axis_index("x"), lax.axis_index("y"), lax.axis_index("c")


def _other_chips(mx, my):
    return [(1 - mx, my), (mx, 1 - my), (1 - mx, 1 - my)]


def ag_small(x, name):
    R = x.shape[0]

    def body(x_ref, out_ref, send_sems, recv_sems, local_sem):
        mx, my, mc = _mesh_pos()
        me, sibling = (mx, my, mc), (mx, my, 1 - mc)
        chips = _other_chips(mx, my)

        def slot(px, py, pc):
            return out_ref.at[4 * px + 2 * py + pc]

        def copy(k, block, to, src=None):
            return pltpu.make_async_remote_copy(
                src_ref=slot(*block) if src is None else src, dst_ref=slot(*block),
                send_sem=send_sems.at[k], recv_sem=recv_sems.at[k], device_id=to, device_id_type=MESH)

        mine = pltpu.make_async_copy(x_ref, slot(*me), local_sem)
        mine.start()
        first = [copy(0, me, sibling, src=x_ref)]
        first += [copy(1 + j, me, (*chip, mc), src=x_ref) for j, chip in enumerate(chips)]
        for cp in first:
            cp.start()
        passed = [copy(4 + j, (*chip, mc), sibling) for j, chip in enumerate(chips)]
        for j, chip in enumerate(chips):
            copy(1 + j, (*chip, mc), me).wait_recv()
            passed[j].start()
        copy(0, sibling, me).wait_recv()
        for j, chip in enumerate(chips):
            copy(4 + j, (*chip, 1 - mc), me).wait_recv()
        for cp in first + passed:
            cp.wait_send()
        mine.wait()

    return pl.pallas_call(
        body, name=name,
        out_shape=jax.ShapeDtypeStruct((N_DEV, R, 128), F32),
        in_specs=[pl.BlockSpec(memory_space=pltpu.VMEM)],
        out_specs=pl.BlockSpec(memory_space=pltpu.VMEM),
        scratch_shapes=[pltpu.SemaphoreType.DMA((7,)), pltpu.SemaphoreType.DMA((7,)), pltpu.SemaphoreType.DMA],
        compiler_params=pltpu.CompilerParams(vmem_limit_bytes=VMEM_LIMIT),
    )(x)


BIG = (("ffn1_w_gu", "col", D, PW), ("ffn1_w_down", "row", FF, D), ("w_in", "col", D, PW),
       ("w_att_o", "col", 512, D), ("w_rec_o", "row", D, D), ("w_out", "row", D, D),
       ("ffn2_w_gu", "col", D, PW), ("ffn2_w_down", "row", FF, D))
NBIG = len(BIG)


def _shard_shape(kind, R, C):
    return (R, C // 4) if kind == "col" else (R // 4, C)


def _region(ref, kind, R, C, q, half, t, tr):
    sr, sc = _shard_shape(kind, R, C)
    if kind == "col":
        return ref.at[pl.ds(pl.multiple_of(half * (R // 2) + t * tr, 16), tr), pl.ds(q * sc, sc)]
    return ref.at[pl.ds(pl.multiple_of(q * sr + t * tr, 16), tr), pl.ds(half * (C // 2), C // 2)]


def ag_local(w, kind, R, C, p_arr, name, after=()):
    sr, sc = _shard_shape(kind, R, C)
    tr = _row_tile(sr, sc, budget=2 * 1024 * 1024)
    nt = sr // tr
    after = list(after)

    def body(p_ref, w_ref, *rest):
        rest[-1][...] = w_ref[...].astype(BF16)

    if kind == "col":
        o_spec = pl.BlockSpec((tr, sc), lambda i, p: (i, p[0]))
    else:
        o_spec = pl.BlockSpec((tr, sc), lambda i, p: (p[0] * nt + i, 0))
    return pl.pallas_call(
        body, name=name,
        grid_spec=pltpu.PrefetchScalarGridSpec(
            num_scalar_prefetch=1, grid=(nt,),
            in_specs=[pl.BlockSpec((tr, sc), lambda i, p: (i, 0))] + [ANY] * len(after), out_specs=o_spec),
        out_shape=jax.ShapeDtypeStruct((R, C), BF16),
        compiler_params=_cp("parallel"),
    )(p_arr, w, *after)


HBM_SPEC = pl.BlockSpec(memory_space=pltpu.HBM)
SEM_SPEC = pl.BlockSpec(memory_space=pltpu.SEMAPHORE)


def _ag_copies(fulls, geoms, ssem, rsem, mx, my, mc, q, h):
    chips = _other_chips(mx, my)
    out = []
    for w, (kind, R, C) in enumerate(geoms):
        sr, sc = _shard_shape(kind, R, C)
        hr = sr // 2 if kind == "col" else sr
        reg = _region(fulls[w], kind, R, C, q, h, 0, hr)
        out += [pltpu.make_async_remote_copy(src_ref=reg, dst_ref=reg, send_sem=ssem.at[3 * w + k],
                                             recv_sem=rsem.at[3 * w + k], device_id=(*chips[k], mc),
                                             device_id_type=MESH) for k in range(3)]
    return out


def ag_start(fulls, geoms, after, name):
    n = len(fulls)
    after = list(after)
    m = len(after)

    def body(*refs):
        ssem, rsem = refs[n + m:n + m + 2]
        outs, token = refs[n + m + 2:2 * n + m + 2], refs[2 * n + m + 2]
        mx, my, mc = _mesh_pos()
        p = 2 * mx + my
        col = [w for w, g in enumerate(geoms) if g[0] == "col"]
        row = [w for w, g in enumerate(geoms) if g[0] == "row"]
        for q in range(4):
            @pl.when(p == q)
            def _(q=q):
                cps = _ag_copies(outs, geoms, ssem, rsem, mx, my, mc, q, mc)
                for w in col:
                    for cp in cps[3 * w:3 * w + 3]:
                        cp.start()
        for h in range(2):
            @pl.when(mc == h)
            def _(h=h):
                cps = _ag_copies(outs, geoms, ssem, rsem, mx, my, mc, p, h)
                for w in row:
                    for cp in cps[3 * w:3 * w + 3]:
                        cp.start()
        token[...] = jnp.zeros_like(token)

    res = pl.pallas_call(
        body, name=name,
        out_shape=[pltpu.SemaphoreType.DMA((3 * n,)), pltpu.SemaphoreType.DMA((3 * n,))]
        + [pltpu.HBM(a.shape, a.dtype) for a in fulls] + [jax.ShapeDtypeStruct((8, 128), F32)],
        in_specs=[HBM_SPEC] * n + [ANY] * m,
        out_specs=[SEM_SPEC, SEM_SPEC] + [HBM_SPEC] * n + [pl.BlockSpec(memory_space=pltpu.VMEM)],
        input_output_aliases={w: 2 + w for w in range(n)},
        compiler_params=pltpu.CompilerParams(has_side_effects=pltpu.SideEffectType.DATAFLOW_SIDE_EFFECTING),
    )(*[pltpu.with_memory_space_constraint(a, pltpu.HBM) for a in fulls], *after)
    return res[0], res[1], list(res[2:2 + n]), res[2 + n]


def ag_wait(fulls, geoms, ssem, rsem, after, name):
    n = len(fulls)

    def body(*refs):
        ins, ssem_ref, rsem_ref = refs[:n], refs[n], refs[n + 1]
        mx, my, mc = _mesh_pos()
        for cp in _ag_copies(ins, geoms, ssem_ref, rsem_ref, mx, my, mc, 0, 0):
            cp.wait_send()
            cp.wait_recv()

    return list(pl.pallas_call(
        body, name=name,
        out_shape=[pltpu.HBM(a.shape, a.dtype) for a in fulls],
        in_specs=[HBM_SPEC] * n + [SEM_SPEC, SEM_SPEC, ANY],
        out_specs=[HBM_SPEC] * n,
        input_output_aliases={w: w for w in range(n)},
        compiler_params=pltpu.CompilerParams(has_side_effects=pltpu.SideEffectType.DATAFLOW_SIDE_EFFECTING),
    )(*fulls, ssem, rsem, after))


def ag_forward(full, kind, R, C, name):
    sr, sc = _shard_shape(kind, R, C)
    hr, hc = (sr // 2, sc) if kind == "col" else (sr, sc // 2)
    tr = _row_tile(hr, hc, itemsize=2, budget=512 * 1024)
    nt = hr // tr

    total = 3 * nt

    def body(src_ref, full_ref, stage, lsem, ssem, rsem):
        step = pl.program_id(0) * nt + pl.program_id(1)
        par = step % 2
        mx, my, mc = _mesh_pos()

        def load(s, q, h, t):
            return pltpu.make_async_copy(_region(src_ref, kind, R, C, q, h, t, tr), stage.at[s], lsem.at[s])

        def push(s, q, h, t):
            return pltpu.make_async_remote_copy(src_ref=stage.at[s], dst_ref=_region(full_ref, kind, R, C, q, h, t, tr),
                                                send_sem=ssem.at[s], recv_sem=rsem, device_id=(mx, my, 1 - mc),
                                                device_id_type=MESH)

        def for_tile(stp, fn):
            q_k = _partner_chip(stp // nt, 2 * mx + my)
            if kind == "col":
                for q in range(4):
                    @pl.when(q_k == q)
                    def _(q=q):
                        fn(q, mc, stp % nt)
            else:
                for h in range(2):
                    @pl.when(mc == h)
                    def _(h=h):
                        fn(q_k, h, stp % nt)

        @pl.when(step == 0)
        def _():
            for_tile(step, lambda q, h, t: load(0, q, h, t).start())

        load(par, 0, 0, 0).wait()
        for_tile(step, lambda q, h, t: push(par, q, h, t).start())

        @pl.when(step + 1 < total)
        def _():
            @pl.when(step >= 1)
            def _():
                push(1 - par, 0, 0, 0).wait_send()
            for_tile(step + 1, lambda q, h, t: load(1 - par, q, h, t).start())

        @pl.when(step == total - 1)
        def _():
            push(par, 0, 0, 0).wait_send()
            push(1 - par, 0, 0, 0).wait_send()
            three = full_ref.at[pl.ds(0, hr), pl.ds(0, 3 * hc)] if kind == "col" else full_ref.at[pl.ds(0, 3 * hr), pl.ds(0, hc)]
            pltpu.make_async_remote_copy(src_ref=three, dst_ref=three, send_sem=ssem.at[0], recv_sem=rsem,
                                         device_id=(mx, my, 1 - mc), device_id_type=MESH).wait_recv()

    return pl.pallas_call(
        body, name=name, grid=(3, nt),
        in_specs=[ANY], out_specs=ANY,
        out_shape=jax.ShapeDtypeStruct((R, C), BF16),
        scratch_shapes=[pltpu.VMEM((2, tr, hc), BF16), pltpu.SemaphoreType.DMA((2,)), pltpu.SemaphoreType.DMA((2,)),
                        pltpu.SemaphoreType.DMA],
        input_output_aliases={0: 0},
        compiler_params=_cp("arbitrary", "arbitrary"),
    )(full)


def _half_shape(kind, R, C):
    return (R // 2, C) if kind == "col" else (R, C // 2)


def _piece_shape(kind, R, C):
    return (R // 2, C // 4) if kind == "col" else (R // 4, C // 2)


def pair_push(g, kind, c_arr, name):
    R, C = g.shape
    hr, hc = _half_shape(kind, R, C)
    tr = _row_tile(hr, hc, itemsize=2, budget=1024 * 1024)
    nt = hr // tr

    def body(c_ref, g_ref, out_ref, stage, ssem, rsem):
        i = pl.program_id(0)
        slot = i % 2
        mx, my, mc = _mesh_pos()

        def push(s, t):
            return pltpu.make_async_remote_copy(
                src_ref=stage.at[s], dst_ref=out_ref.at[pl.ds(pl.multiple_of(t * tr, 16), tr)],
                send_sem=ssem.at[s], recv_sem=rsem, device_id=(mx, my, 1 - mc), device_id_type=MESH)

        @pl.when(i >= 2)
        def _():
            push(slot, 0).wait_send()

        stage[slot] = g_ref[...]
        push(slot, i).start()

        @pl.when(i == nt - 1)
        def _():
            push(slot, 0).wait_send()
            if nt >= 2:
                push(1 - slot, 0).wait_send()
            pltpu.make_async_remote_copy(src_ref=out_ref, dst_ref=out_ref, send_sem=ssem.at[0], recv_sem=rsem,
                                         device_id=(mx, my, 1 - mc), device_id_type=MESH).wait_recv()

    if kind == "col":
        g_spec = pl.BlockSpec((tr, hc), lambda i, c: ((1 - c[0]) * nt + i, 0))
    else:
        g_spec = pl.BlockSpec((tr, hc), lambda i, c: (i, 1 - c[0]))
    return pl.pallas_call(
        body, name=name,
        grid_spec=pltpu.PrefetchScalarGridSpec(
            num_scalar_prefetch=1, grid=(nt,), in_specs=[g_spec], out_specs=ANY,
            scratch_shapes=[pltpu.VMEM((2, tr, hc), BF16), pltpu.SemaphoreType.DMA((2,)), pltpu.SemaphoreType.DMA]),
        out_shape=jax.ShapeDtypeStruct((hr, hc), BF16),
        compiler_params=_cp("arbitrary"),
    )(c_arr, g)


def _partner_chip(k, p):
    return p ^ jnp.where(k == 0, 2, jnp.where(k == 1, 1, jnp.where(k == 2, 3, 0)))


def pair_add(g, got, kind, cp_arr, name):
    R, C = g.shape
    pr, pc = _piece_shape(kind, R, C)
    tr = _row_tile(pr, pc, itemsize=2, budget=1024 * 1024)
    nt = pr // tr

    def body(cp_ref, g_ref, got_ref, ps_ref, rb_ref):
        tile = (g_ref[...].astype(F32) + got_ref[...].astype(F32)).astype(BF16)
        ps_ref[...] = tile

        @pl.when(pl.program_id(1) == cp_ref[1])
        def _():
            rb_ref[...] = tile

    if kind == "col":
        g_spec = pl.BlockSpec((tr, pc), lambda i, q, cp: (cp[0] * nt + i, q))
        got_spec = pl.BlockSpec((tr, pc), lambda i, q, cp: (i, q))
    else:
        g_spec = pl.BlockSpec((tr, pc), lambda i, q, cp: (q * nt + i, cp[0]))
        got_spec = pl.BlockSpec((tr, pc), lambda i, q, cp: (q * nt + i, 0))
    return pl.pallas_call(
        body, name=name,
        grid_spec=pltpu.PrefetchScalarGridSpec(
            num_scalar_prefetch=1, grid=(nt, 4), in_specs=[g_spec, got_spec],
            out_specs=[pl.BlockSpec((None, tr, pc), lambda i, q, cp: (q, i, 0)),
                       pl.BlockSpec((None, tr, pc), lambda i, q, cp: (cp[1], i, 0))]),
        out_shape=[jax.ShapeDtypeStruct((4, pr, pc), BF16)] * 2,
        compiler_params=_cp("arbitrary", "arbitrary"),
    )(cp_arr, g, got)


def _rs_copies(ps, rb, ssem, rsem, mx, my, mc):
    p = 2 * mx + my
    out = []
    for w in range(len(ps)):
        for k, chip in enumerate(_other_chips(mx, my)):
            out.append(pltpu.make_async_remote_copy(
                src_ref=ps[w].at[2 * chip[0] + chip[1]], dst_ref=rb[w].at[p], send_sem=ssem.at[3 * w + k],
                recv_sem=rsem.at[3 * w + k], device_id=(*chip, mc), device_id_type=MESH))
    return out


def rs_start(ps, rb, after, name):
    n = len(ps)
    after = list(after)
    m = len(after)

    def body(*refs):
        ssem, rsem = refs[2 * n + m:2 * n + m + 2]
        ps_o = refs[2 * n + m + 2:3 * n + m + 2]
        rb_o = refs[3 * n + m + 2:4 * n + m + 2]
        token = refs[4 * n + m + 2]
        for cp in _rs_copies(ps_o, rb_o, ssem, rsem, *_mesh_pos()):
            cp.start()
        token[...] = jnp.zeros_like(token)

    both = list(ps) + list(rb)
    res = pl.pallas_call(
        body, name=name,
        out_shape=[pltpu.SemaphoreType.DMA((3 * n,)), pltpu.SemaphoreType.DMA((3 * n,))]
        + [pltpu.HBM(a.shape, a.dtype) for a in both] + [jax.ShapeDtypeStruct((8, 128), F32)],
        in_specs=[HBM_SPEC] * (2 * n) + [ANY] * m,
        out_specs=[SEM_SPEC, SEM_SPEC] + [HBM_SPEC] * (2 * n) + [pl.BlockSpec(memory_space=pltpu.VMEM)],
        input_output_aliases={w: 2 + w for w in range(2 * n)},
        compiler_params=pltpu.CompilerParams(has_side_effects=pltpu.SideEffectType.DATAFLOW_SIDE_EFFECTING),
    )(*[pltpu.with_memory_space_constraint(a, pltpu.HBM) for a in both], *after)
    return res[0], res[1], list(res[2:2 + n]), list(res[2 + n:2 + 2 * n]), res[2 + 2 * n]


def rs_wait(ps, rb, ssem, rsem, after, name):
    n = len(ps)
    after = list(after)
    m = len(after)

    def body(*refs):
        ps_i, rb_i = refs[:n], refs[n:2 * n]
        ssem_ref, rsem_ref = refs[2 * n], refs[2 * n + 1]
        for cp in _rs_copies(ps_i, rb_i, ssem_ref, rsem_ref, *_mesh_pos()):
            cp.wait_send()
            cp.wait_recv()

    both = list(ps) + list(rb)
    res = pl.pallas_call(
        body, name=name,
        out_shape=[pltpu.HBM(a.shape, a.dtype) for a in both],
        in_specs=[HBM_SPEC] * (2 * n) + [SEM_SPEC, SEM_SPEC] + [ANY] * m,
        out_specs=[HBM_SPEC] * (2 * n),
        input_output_aliases={w: w for w in range(2 * n)},
        compiler_params=pltpu.CompilerParams(has_side_effects=pltpu.SideEffectType.DATAFLOW_SIDE_EFFECTING),
    )(*both, ssem, rsem, *after)
    return list(res[n:])


def sum_share(parts, kind, R, C, name):
    _, pr, pc = parts.shape
    sr, sc = _shard_shape(kind, R, C)
    tr = _row_tile(pr, pc * 4, budget=4 * 1024 * 1024)
    nt = pr // tr

    def body(p_ref, fin_ref, stage, lsem, ssem, rsem):
        i = pl.program_id(0)
        slot = i % 2
        mx, my, mc = _mesh_pos()

        def region(h, t):
            r0 = pl.multiple_of(t * tr, 8)
            if kind == "col":
                return fin_ref.at[pl.ds(pl.multiple_of(h * pr + r0, 8), tr)]
            return fin_ref.at[pl.ds(r0, tr), pl.ds(h * pc, pc)]

        def copies(s, h, t):
            return (pltpu.make_async_copy(stage.at[s], region(h, t), lsem.at[s]),
                    pltpu.make_async_remote_copy(src_ref=stage.at[s], dst_ref=region(h, t), send_sem=ssem.at[s],
                                                 recv_sem=rsem, device_id=(mx, my, 1 - mc), device_id_type=MESH))

        def wait_sent(s):
            loc, rem = copies(s, 0, 0)
            loc.wait()
            rem.wait_send()

        @pl.when(i >= 2)
        def _():
            wait_sent(slot)

        acc = p_ref[0].astype(F32)
        for k in range(1, 4):
            acc = acc + p_ref[k].astype(F32)
        stage[slot] = acc
        if kind == "col":
            for cp in copies(slot, mc, i):
                cp.start()
        else:
            for h in range(2):
                @pl.when(mc == h)
                def _(h=h):
                    for cp in copies(slot, h, i):
                        cp.start()

        @pl.when(i == nt - 1)
        def _():
            wait_sent(slot)
            if nt >= 2:
                wait_sent(1 - slot)
            half = fin_ref.at[pl.ds(0, pr), pl.ds(0, pc)]
            pltpu.make_async_remote_copy(src_ref=half, dst_ref=half, send_sem=ssem.at[0], recv_sem=rsem,
                                         device_id=(mx, my, 1 - mc), device_id_type=MESH).wait_recv()

    return pl.pallas_call(
        body, name=name, grid=(nt,),
        in_specs=[pl.BlockSpec((4, tr, pc), lambda i: (0, i, 0))],
        out_specs=ANY,
        out_shape=jax.ShapeDtypeStruct((sr, sc), F32),
        scratch_shapes=[pltpu.VMEM((2, tr, pc), F32), pltpu.SemaphoreType.DMA((2,)), pltpu.SemaphoreType.DMA((2,)),
                        pltpu.SemaphoreType.DMA],
        compiler_params=_cp("arbitrary"),
    )(parts)


def _pack(parts, rows):
    flat = []
    for a in parts:
        a = jnp.ravel(a).astype(F32)
        flat.append(jnp.pad(a, (0, (-a.shape[0]) % 128)))
    v = jnp.concatenate(flat)
    return jnp.pad(v, (0, rows * 128 - v.shape[0])).reshape(rows, 128)


def _unpack(block, shapes):
    lead = block.shape[:-2]
    v = block.reshape(lead + (-1,))
    out, off = [], 0
    for shp in shapes:
        n = int(np.prod(shp))
        out.append(v[..., off:off + n].reshape(lead + tuple(shp)))
        off += n + (-n) % 128
    return out


def _block_diag4(w):
    w4 = w.reshape(4, 4, 64, 64)
    eye = jnp.eye(4, dtype=w.dtype)
    return (w4[:, :, :, None, :] * eye[None, :, None, :, None]).reshape(4, 256, 256)


def _diag_blocks(bd):
    b5 = bd.reshape(4, 4, 64, 4, 64)
    return jnp.stack([b5[:, i, :, i, :] for i in range(4)], axis=1).reshape(16, 64, 64)


def _bias_window(rel_bias):
    m = (np.arange(768) + 127) % 768 - 127
    w = rel_bias[:, np.clip(512 - m, -128, 128) + 128]
    win = jnp.tile(w, (1, 128))[:, :128 * 767].reshape(8, 128, 767)[:, :, :WIN]
    qh = np.arange(128)[:, None] // CHUNK
    kc = np.arange(WIN)[None, :] // CHUNK
    valid = (kc >= qh) & (kc <= qh + 8)
    return jnp.where(jnp.asarray(valid)[None], win, NEG)


SMALL = ("b_ada", "norm_pre", "norm_post", "rel_bias", "conv_w", "conv_b", "lru_wa", "lru_ba", "lru_wx",
         "lru_bx", "lru_lambda")
WEIGHTS = ("w_ada", "b_ada", "norm_pre", "norm_post", "ffn1_w_gu", "ffn1_w_down", "w_in", "rel_bias", "conv_w",
           "conv_b", "lru_wa", "lru_ba", "lru_wx", "lru_bx", "lru_lambda", "w_att_o", "w_rec_o", "w_out",
           "ffn2_w_gu", "ffn2_w_down")


def kernel(x, c, w_ada, b_ada, norm_pre, norm_post, ffn1_w_gu, ffn1_w_down, w_in, rel_bias, conv_w, conv_b, lru_wa, lru_ba, lru_wx, lru_bx, lru_lambda, w_att_o, w_rec_o, w_out, ffn2_w_gu, ffn2_w_down, loss_target, m_w_ada, m_b_ada, m_norm_pre, m_norm_post, m_ffn1_w_gu, m_ffn1_w_down, m_w_in, m_rel_bias, m_conv_w, m_conv_b, m_lru_wa, m_lru_ba, m_lru_wx, m_lru_bx, m_lru_lambda, m_w_att_o, m_w_rec_o, m_w_out, m_ffn2_w_gu, m_ffn2_w_down, v_w_ada, v_b_ada, v_norm_pre, v_norm_post, v_ffn1_w_gu, v_ffn1_w_down, v_w_in, v_rel_bias, v_conv_w, v_conv_b, v_lru_wa, v_lru_ba, v_lru_wx, v_lru_bx, v_lru_lambda, v_w_att_o, v_w_rec_o, v_w_out, v_ffn2_w_gu, v_ffn2_w_down):
    W = dict(w_ada=w_ada, b_ada=b_ada, norm_pre=norm_pre, norm_post=norm_post, ffn1_w_gu=ffn1_w_gu,
             ffn1_w_down=ffn1_w_down, w_in=w_in, rel_bias=rel_bias, conv_w=conv_w, conv_b=conv_b, lru_wa=lru_wa,
             lru_ba=lru_ba, lru_wx=lru_wx, lru_bx=lru_bx, lru_lambda=lru_lambda, w_att_o=w_att_o, w_rec_o=w_rec_o,
             w_out=w_out, ffn2_w_gu=ffn2_w_gu, ffn2_w_down=ffn2_w_down)
    M = dict(w_ada=m_w_ada, b_ada=m_b_ada, norm_pre=m_norm_pre, norm_post=m_norm_post, ffn1_w_gu=m_ffn1_w_gu,
             ffn1_w_down=m_ffn1_w_down, w_in=m_w_in, rel_bias=m_rel_bias, conv_w=m_conv_w, conv_b=m_conv_b,
             lru_wa=m_lru_wa, lru_ba=m_lru_ba, lru_wx=m_lru_wx, lru_bx=m_lru_bx, lru_lambda=m_lru_lambda,
             w_att_o=m_w_att_o, w_rec_o=m_w_rec_o, w_out=m_w_out, ffn2_w_gu=m_ffn2_w_gu, ffn2_w_down=m_ffn2_w_down)
    V = dict(w_ada=v_w_ada, b_ada=v_b_ada, norm_pre=v_norm_pre, norm_post=v_norm_post, ffn1_w_gu=v_ffn1_w_gu,
             ffn1_w_down=v_ffn1_w_down, w_in=v_w_in, rel_bias=v_rel_bias, conv_w=v_conv_w, conv_b=v_conv_b,
             lru_wa=v_lru_wa, lru_ba=v_lru_ba, lru_wx=v_lru_wx, lru_bx=v_lru_bx, lru_lambda=v_lru_lambda,
             w_att_o=v_w_att_o, w_rec_o=v_w_rec_o, w_out=v_w_out, ffn2_w_gu=v_ffn2_w_gu, ffn2_w_down=v_ffn2_w_down)
    mx, my, mc = _mesh_pos()
    p = 2 * mx + my
    e = 4 * mx + 2 * my + mc
    xs = x[0]

    c_arr = jnp.reshape(mc, (1,)).astype(jnp.int32)
    cp_arr = jnp.stack([mc, p]).astype(jnp.int32)
    p_arr = jnp.reshape(p, (1,)).astype(jnp.int32)
    geoms = [(kind, R, C) for (_, kind, R, C) in BIG]
    names = [b[0] for b in BIG]
    placed = [ag_local(W[n][0], kind, R, C, p_arr, "ag_local_" + n) for (n, kind, R, C) in BIG[:2]]

    def arrived(fly, lo, hi, ssem, rsem, after, tag):
        done = ag_wait(fly, geoms[lo:hi], ssem, rsem, after, "ag_wait_" + tag)
        return [ag_forward(a, kind, R, C, "ag_forward_" + n)
                for a, (kind, R, C), n in zip(done, geoms[lo:hi], names[lo:hi])]

    g1 = ag_small(_pack([c, norm_pre, norm_post, conv_w], 32), "ag_small_params")
    c_all, npre4, npost4, cw4 = _unpack(g1, [(D,), (3, 256), (3, 256), (4, 256)])
    chipwise = lambda a: jnp.moveaxis(a[0::2], 0, 1).reshape(a.shape[1], D)
    npre, npost, conv_full = chipwise(npre4), chipwise(npost4), chipwise(cw4)

    b_cols = lax.dynamic_slice(b_ada, (0, p * 2304), (1, 2304))
    mod_cols = ada_fwd(c_all, w_ada[0], b_cols, "ada_fwd")
    g2 = ag_small(mod_cols.reshape(144, 128), "ag_mod")
    mod_all = jnp.moveaxis(g2[0::2].reshape(4, 8, 2304), 0, 1).reshape(8, 9 * D)
    mod = lax.dynamic_index_in_dim(mod_all, e, 0, keepdims=False).reshape(3, 3, D)
    zeros3 = jnp.zeros((3, D), F32)
    vecs = [jnp.concatenate([npre[k:k + 1], npost[k:k + 1], mod[k], zeros3], axis=0) for k in range(3)]

    f1_s, f1_r, f1_fly, tok0 = ag_start(placed[:2], geoms[:2], [g2], "ag_start_ffn1")
    placed += [ag_local(W[n][0], kind, R, C, p_arr, "ag_local_" + n, after=[tok0]) for (n, kind, R, C) in BIG[2:]]
    f1_gu, f1_dn = arrived(f1_fly, 0, 2, f1_s, f1_r, placed[7], "ffn1")
    mix_s, mix_r, mix_fly, tok1 = ag_start(placed[2:6], geoms[2:6], [f1_gu, f1_dn], "ag_start_mixer")
    ffn_s, ffn_r, ffn_fly, tok2 = ag_start(placed[6:], geoms[6:], [tok1], "ag_start_ffn2")
    wa_bd = _block_diag4(lru_wa[0]).astype(BF16)
    wx_bd = _block_diag4(lru_wx[0]).astype(BF16)
    pvec = jnp.concatenate([conv_full, conv_b, lru_ba, lru_bx, lru_lambda], axis=0)
    bias = _bias_window(rel_bias[0]).reshape(4, 256, WIN)

    x1, h1, g1_, u1, a1, f1 = ffn_fwd(xs, vecs[0] + tok2[0:1, 0:1], f1_gu, f1_dn, 0.5, "ffn1_fwd")
    win, wao, wro, wout = arrived(mix_fly, 2, 6, mix_s, mix_r, x1, "mixer")
    h2, qkv, rest = proj_fwd(x1, vecs[1], win, "proj_fwd")
    ao = attn_fwd(qkv, bias, "attn_fwd")
    hl, hg = lru_fwd(rest, pvec, wa_bd, wx_bd, "lru_fwd")
    x2, att, rec, mg, f2 = mix_out_fwd(x1, ao, hg, rest, vecs[1], wao, wro, wout, "mix_out_fwd")
    f2_gu, f2_dn = arrived(ffn_fly, 6, 8, ffn_s, ffn_r, x2, "ffn2")
    x3, h3, g3_, u3, a3, f3 = ffn_fwd(x2, vecs[2], f2_gu, f2_dn, 0.5, "ffn2_fwd")
    dy, lvec = loss_grad(x3, loss_target[0], "loss_grad")
    loss = lax.psum(lvec[0, 0], ("x", "y", "c"))

    G, grads = {}, {}
    geo = {n: (kind, R, C) for (n, kind, R, C) in BIG}

    def reduce_begin(names, tag):
        ps, rb = [], []
        for n in names:
            got = pair_push(G[n], geo[n][0], c_arr, "rs_push_" + n)
            a, b = pair_add(G[n], got, geo[n][0], cp_arr, "rs_pair_sum_" + n)
            ps.append(a)
            rb.append(b)
        return rs_start(ps, rb, [], "rs_start_" + tag)

    def reduce_end(names, flight, after, tag):
        ssem, rsem, ps, rb, _ = flight
        for a, n in zip(rs_wait(ps, rb, ssem, rsem, after, "rs_wait_" + tag), names):
            grads[n] = sum_share(a, *geo[n], "rs_sum_share_" + n)[None]

    dx2, df3, dgu3, va2 = ffn_bwd(dy, x2, f3, g3_, u3, vecs[2], f2_gu, f2_gu[:, FF:], f2_dn, 0.5, "ffn2_bwd")
    G["ffn2_w_gu"] = mm_tn(h3, dgu3, "dw_ffn2_gu", D, 1408, 1024)
    G["ffn2_w_down"] = mm_tn(a3, df3, "dw_ffn2_down", 1408, D, 1024)
    fly_ffn2 = reduce_begin(("ffn2_w_gu", "ffn2_w_down"), "ffn2")
    vec1 = vecs[1] + fly_ffn2[4][0:1, 0:1]
    df2, d_att, d_rec, dao, dhl, d3, va_out = mix_out_bwd(dx2, f2, att, rec, rest, hl, vec1, wao, wro, wout,
                                                          "mix_out_bwd")
    G["w_out"] = mm_tn(mg, df2, "dw_out", D, D, 1024)
    G["w_att_o"] = mm_tn(ao, d_att, "dw_att_o", 512, D, 1024)
    G["w_rec_o"] = mm_tn(hg, d_rec, "dw_rec_o", D, D, 1024)
    dq, db, dkv = attn_bwd(qkv, dao, bias, "attn_bwd")
    dxr, v_lru, dwa_bd, dwx_bd = lru_bwd(dhl, hl, rest, pvec, wa_bd, wx_bd, "lru_bwd")
    dx1, va_in = proj_bwd(dq, dkv, dxr, d3, win, x1, dx2, vecs[1], "proj_bwd")
    gin = mm_tn(h2, dq, "dw_in_q", D, 512, 1024, n_total=PW)
    gin = mm_tn(h2, dkv, "dw_in_kv", D, 512, 1024, prev=gin, col_off=512, n_total=PW)
    gin = mm_tn(h2, dxr, "dw_in_xr", D, 512, 1024, prev=gin, col_off=1536, n_total=PW)
    G["w_in"] = mm_tn(h2, d3, "dw_in_gates", D, 512, 1024, prev=gin, col_off=2560, n_total=PW)
    fly_mix = reduce_begin(("w_in", "w_att_o", "w_rec_o", "w_out"), "mixer")
    vec0 = vecs[0] + fly_mix[4][0:1, 0:1]
    dx0, df1, dgu1, va0 = ffn_bwd(dx1, xs, f1, g1_, u1, vec0, f1_gu, f1_gu[:, FF:], f1_dn, 0.5, "ffn1_bwd")
    G["ffn1_w_gu"] = mm_tn(h1, dgu1, "dw_ffn1_gu", D, 1408, 1024)
    G["ffn1_w_down"] = mm_tn(a1, df1, "dw_ffn1_down", 1408, D, 1024)
    fly_ffn1 = reduce_begin(("ffn1_w_gu", "ffn1_w_down"), "ffn1")
    reduce_end(("ffn2_w_gu", "ffn2_w_down"), fly_ffn2, [fly_ffn1[4]], "ffn2")
    reduce_end(("w_in", "w_att_o", "w_rec_o", "w_out"), fly_mix, [fly_ffn1[4], grads["ffn2_w_down"]], "mixer")

    va1 = va_out + va_in
    vas = (va0, va1, va2)
    dmod = jnp.stack([v[2:5] for v in vas])
    part = {"b_ada": dmod, "norm_pre": jnp.stack([v[0] for v in vas]), "norm_post": jnp.stack([v[1] for v in vas]),
            "rel_bias": bias_grad(db.reshape(8, 128, WIN), "bias_grad")[:, :257], "conv_w": v_lru[0:4], "conv_b": v_lru[4],
            "lru_wa": _diag_blocks(dwa_bd), "lru_ba": v_lru[5], "lru_wx": _diag_blocks(dwx_bd), "lru_bx": v_lru[6],
            "lru_lambda": v_lru[7]}
    full_shapes = {"b_ada": (9 * D,), "norm_pre": (3, D), "norm_post": (3, D), "rel_bias": (8, 257),
                   "conv_w": (4, D), "conv_b": (D,), "lru_wa": (16, 64, 64), "lru_ba": (D,),
                   "lru_wx": (16, 64, 64), "lru_bx": (D,), "lru_lambda": (D,)}
    g3 = ag_small(_pack([part[n] for n in SMALL], 1232), "ag_small_grads")
    red = dict(zip(SMALL, _unpack(sum_lead(g3, "sum_small_grads"), [full_shapes[n] for n in SMALL])))
    cols = lambda a: lax.dynamic_slice(a, (0, p * 256), (a.shape[0], 256))
    grads.update({"b_ada": red["b_ada"][None], "norm_pre": cols(red["norm_pre"])[None],
                  "norm_post": cols(red["norm_post"])[None], "rel_bias": red["rel_bias"][None],
                  "conv_w": cols(red["conv_w"])[None], "conv_b": red["conv_b"][None], "lru_wa": red["lru_wa"][None],
                  "lru_ba": red["lru_ba"][None], "lru_wx": red["lru_wx"][None], "lru_bx": red["lru_bx"][None],
                  "lru_lambda": red["lru_lambda"][None]})

    dmod_all = g3[:, :72].reshape(8, 9 * D)
    dmod_cols = jnp.pad(lax.dynamic_slice(dmod_all, (0, p * 2304), (8, 2304)), ((0, 120), (0, 0)))
    c_all_t = jnp.pad(c_all.T, ((0, 0), (0, 120)))
    grads["w_ada"] = ada_bwd(c_all_t, dmod_cols, "ada_bwd")[None]

    delta, new_m, new_v = {}, {}, {}

    def update(n):
        shp = W[n].shape
        d_, m_, v_ = adamw(W[n][0], grads[n][0], M[n][0], V[n][0], "adamw_" + n)
        delta[n], new_m[n], new_v[n] = d_.reshape(shp), m_.reshape(shp), v_.reshape(shp)

    for n in ("w_ada", "ffn2_w_gu", "ffn2_w_down", "w_in", "w_att_o", "w_rec_o", "w_out"):
        update(n)
    packed = [_pack([src[n] for n in SMALL], 1168) for src in (W, grads, M, V)]
    outs = adamw(*packed, "adamw_small")
    for dst, blk in zip((delta, new_m, new_v), outs):
        for n, a in zip(SMALL, _unpack(blk, [W[n].shape for n in SMALL])):
            dst[n] = a
    reduce_end(("ffn1_w_gu", "ffn1_w_down"), fly_ffn1,
               [outs[0], delta["w_ada"], delta["ffn2_w_gu"], delta["ffn2_w_down"], delta["w_in"], delta["w_out"]], "ffn1")
    for n in ("ffn1_w_gu", "ffn1_w_down"):
        update(n)

    return (loss, dx0[None], *[grads[n] for n in WEIGHTS], *[delta[n] for n in WEIGHTS],
            *[new_m[n] for n in WEIGHTS], *[new_v[n] for n in WEIGHTS])
```

```python
import functools

import numpy as np
import jax
import jax.numpy as jnp
from jax import lax
from jax.experimental import pallas as pl
from jax.experimental.pallas import tpu as pltpu

F32 = jnp.float32
BF16 = jnp.bfloat16

D = 1024
FF = 2816
PW = 5632
HP = 128
CHUNK = 64
WIN = 640
TQ = 512
EPS = 1e-6
NEG = -1e30
LRU_C = 8.0
N_DEV = 8
VMEM_LIMIT = 50 * 1024 * 1024

ADAM_LR, ADAM_B1, ADAM_B2, ADAM_EPS, ADAM_WD, ADAM_STEP = 0.001, 0.9, 0.999, 1e-08, 0.01, 10

MESH = pl.DeviceIdType.MESH
ANY = pl.BlockSpec(memory_space=pl.ANY)


def _cp(*sem):
    return pltpu.CompilerParams(dimension_semantics=tuple(sem), vmem_limit_bytes=VMEM_LIMIT)


def _dot(a, b):
    return jnp.dot(a, b, preferred_element_type=F32)


def _dot_nt(a, b):
    return lax.dot_general(a, b, (((1,), (1,)), ((), ())), preferred_element_type=F32)


def _dot_tn(a, b):
    return lax.dot_general(a, b, (((0,), (0,)), ((), ())), preferred_element_type=F32)


def _mean(v):
    return jnp.mean(v, axis=-1, keepdims=True)


def _colsum(v):
    return jnp.sum(v, axis=0, keepdims=True)


def _sigmoid(v):
    return 0.5 * jnp.tanh(0.5 * v) + 0.5


_GK = 0.7978845608028654


def _gelu(v):
    t = jnp.tanh(_GK * (v + 0.044715 * v * v * v))
    return 0.5 * v * (1.0 + t)


def _pre_norm(xv, vec_ref):
    r = lax.rsqrt(_mean(xv * xv) + EPS)
    n = xv * r * vec_ref[0:1, :]
    return n * (1.0 + vec_ref[3:4, :]) + vec_ref[2:3, :]


def _pre_norm_bwd(dh, xv, dres, vec_ref, vacc_ref):
    r = lax.rsqrt(_mean(xv * xv) + EPS)
    xh = xv * r
    n = xh * vec_ref[0:1, :]
    vacc_ref[2:3, :] += _colsum(dh)
    vacc_ref[3:4, :] += _colsum(dh * n)
    dn = dh * (1.0 + vec_ref[3:4, :])
    vacc_ref[0:1, :] += _colsum(dn * xh)
    dxh = dn * vec_ref[0:1, :]
    return r * (dxh - xh * _mean(dxh * xh)) + dres


def _post_norm_bwd(dxo, fv, res, vec_ref, vacc_ref):
    rf = lax.rsqrt(_mean(fv * fv) + EPS)
    fh = fv * rf
    gp = vec_ref[1:2, :]
    vacc_ref[4:5, :] += _colsum(res * dxo * (fh * gp))
    dy = (res * vec_ref[4:5, :]) * dxo
    vacc_ref[1:2, :] += _colsum(dy * fh)
    dfn = dy * gp
    return rf * (dfn - fh * _mean(dfn * fh))


def ffn_fwd(x, vec, w_gu, w_dn, res, name, tm=1024, tf=256):
    S = x.shape[0]
    tm = min(tm, S)
    nf = FF // tf

    def body(x_ref, vec_ref, wg_ref, wu_ref, wd_ref, xo_ref, h_ref, g_ref, u_ref, a_ref, f_ref, hs, acc):
        j = pl.program_id(1)

        @pl.when(j == 0)
        def _():
            h = _pre_norm(x_ref[...], vec_ref).astype(BF16)
            hs[...] = h
            h_ref[...] = h
            acc[...] = jnp.zeros_like(acc)

        h = hs[...]
        g = _dot(h, wg_ref[...])
        u = _dot(h, wu_ref[...])
        g_ref[...] = g.astype(BF16)
        u_ref[...] = u.astype(BF16)
        a = (g * _sigmoid(g) * u).astype(BF16)
        a_ref[...] = a
        acc[...] += _dot(a, wd_ref[...])

        @pl.when(j == nf - 1)
        def _():
            f = acc[...]
            f_ref[...] = f
            y = f * lax.rsqrt(_mean(f * f) + EPS) * vec_ref[1:2, :]
            xo_ref[...] = x_ref[...] + (res * vec_ref[4:5, :]) * y

    row = lambda i, j: (i, 0)
    return pl.pallas_call(
        body, name=name, grid=(S // tm, nf),
        in_specs=[pl.BlockSpec((tm, D), row), pl.BlockSpec((8, D), lambda i, j: (0, 0)),
                  pl.BlockSpec((D, tf), lambda i, j: (0, j)), pl.BlockSpec((D, tf), lambda i, j: (0, j + nf)),
                  pl.BlockSpec((tf, D), lambda i, j: (j, 0))],
        out_specs=[pl.BlockSpec((tm, D), row), pl.BlockSpec((tm, D), row),
                   pl.BlockSpec((tm, tf), lambda i, j: (i, j)), pl.BlockSpec((tm, tf), lambda i, j: (i, j)),
                   pl.BlockSpec((tm, tf), lambda i, j: (i, j)), pl.BlockSpec((tm, D), row)],
        out_shape=[jax.ShapeDtypeStruct((S, D), F32), jax.ShapeDtypeStruct((S, D), BF16),
                   jax.ShapeDtypeStruct((S, FF), BF16), jax.ShapeDtypeStruct((S, FF), BF16),
                   jax.ShapeDtypeStruct((S, FF), BF16), jax.ShapeDtypeStruct((S, D), F32)],
        scratch_shapes=[pltpu.VMEM((tm, D), BF16), pltpu.VMEM((tm, D), F32)],
        compiler_params=_cp("parallel", "arbitrary"),
    )(x, vec, w_gu, w_gu, w_dn)


def ffn_bwd(dxo, x, f, g, u, vec, w_gu, w_u, w_dn, res, name, tm=512, tf=512):
    S = x.shape[0]
    tm = min(tm, S)
    nf = -(-FF // tf)
    tail = FF - tf * (nf - 1)

    def body(dxo_ref, x_ref, f_ref, g_ref, u_ref, vec_ref, wg_ref, wu_ref, wd_ref,
             dx_ref, df_ref, dgu_ref, vacc_ref, dfs, acc):
        i, j = pl.program_id(0), pl.program_id(1)

        @pl.when((i == 0) & (j == 0))
        def _():
            vacc_ref[...] = jnp.zeros_like(vacc_ref)

        @pl.when(j == 0)
        def _():
            df = _post_norm_bwd(dxo_ref[...], f_ref[...], res, vec_ref, vacc_ref).astype(BF16)
            dfs[...] = df
            df_ref[...] = df
            acc[...] = jnp.zeros_like(acc)

        def chunk(w):
            da = _dot_nt(dfs[...], wd_ref[0:w, :])
            gv, uv = g_ref[:, 0:w].astype(F32), u_ref[:, 0:w].astype(F32)
            sg = _sigmoid(gv)
            dg = (da * uv * (sg * (1.0 + gv * (1.0 - sg)))).astype(BF16)
            du = (da * (gv * sg)).astype(BF16)
            dgu_ref[0, :, 0:w] = dg
            dgu_ref[1, :, 0:w] = du
            acc[...] += _dot_nt(dg, wg_ref[:, 0:w]) + _dot_nt(du, wu_ref[:, 0:w])

        @pl.when(j < nf - 1)
        def _():
            chunk(tf)

        @pl.when(j == nf - 1)
        def _():
            chunk(tail)
            dx_ref[...] = _pre_norm_bwd(acc[...], x_ref[...], dxo_ref[...], vec_ref, vacc_ref)

    row = lambda i, j: (i, 0)
    col = lambda i, j: (i, j)
    return pl.pallas_call(
        body, name=name, grid=(S // tm, nf),
        in_specs=[pl.BlockSpec((tm, D), row), pl.BlockSpec((tm, D), row), pl.BlockSpec((tm, D), row),
                  pl.BlockSpec((tm, tf), col), pl.BlockSpec((tm, tf), col),
                  pl.BlockSpec((8, D), lambda i, j: (0, 0)),
                  pl.BlockSpec((D, tf), lambda i, j: (0, j)), pl.BlockSpec((D, tf), lambda i, j: (0, j)),
                  pl.BlockSpec((tf, D), lambda i, j: (j, 0))],
        out_specs=[pl.BlockSpec((tm, D), row), pl.BlockSpec((tm, D), row),
                   pl.BlockSpec((2, tm, tf), lambda i, j: (0, i, j)),
                   pl.BlockSpec((8, D), lambda i, j: (0, 0))],
        out_shape=[jax.ShapeDtypeStruct((S, D), F32), jax.ShapeDtypeStruct((S, D), BF16),
                   jax.ShapeDtypeStruct((2, S, FF), BF16), jax.ShapeDtypeStruct((8, D), F32)],
        scratch_shapes=[pltpu.VMEM((tm, D), BF16), pltpu.VMEM((tm, D), F32)],
        compiler_params=_cp("arbitrary", "arbitrary"),
    )(dxo, x, f, g, u, vec, w_gu, w_u, w_dn)


def mm_tn(a, b, name, tm, tn, tk, out_dtype=BF16, prev=None, col_off=0, n_total=None):
    S, M = a.shape
    if b.ndim == 3:
        G, _, Nf = b.shape
    else:
        G, Nf = 1, b.shape[1]
    N = G * Nf
    n_total = N if n_total is None else n_total
    tk = min(tk, S)
    nbf = Nf // tn
    nk = S // tk
    ob = col_off // tn

    def body(*refs):
        a_ref, b_ref = refs[0], refs[1]
        o_ref, acc = refs[-2], refs[-1]
        k = pl.program_id(2)

        @pl.when(k == 0)
        def _():
            acc[...] = jnp.zeros_like(acc)

        acc[...] += _dot_tn(a_ref[...], b_ref[...])

        @pl.when(k == nk - 1)
        def _():
            o_ref[...] = acc[...].astype(out_dtype)

    if b.ndim == 3:
        b_spec = pl.BlockSpec((None, tk, tn), lambda i, j, k: (j // nbf, k, j % nbf))
    else:
        b_spec = pl.BlockSpec((tk, tn), lambda i, j, k: (k, j))
    in_specs = [pl.BlockSpec((tk, tm), lambda i, j, k: (k, i)), b_spec]
    args = [a, b]
    aliases = {}
    if prev is not None:
        in_specs.append(ANY)
        args.append(prev)
        aliases = {2: 0}
    return pl.pallas_call(
        body, name=name, grid=(M // tm, N // tn, nk),
        in_specs=in_specs,
        out_specs=pl.BlockSpec((tm, tn), lambda i, j, k: (i, j + ob)),
        out_shape=jax.ShapeDtypeStruct((M, n_total), out_dtype),
        scratch_shapes=[pltpu.VMEM((tm, tn), F32)],
        input_output_aliases=aliases,
        compiler_params=_cp("parallel", "parallel", "arbitrary"),
    )(*args)


def proj_fwd(x, vec, w_in, name, tm=1024, tn=512):
    S = x.shape[0]
    tm = min(tm, S)
    nq = 1536 // tn

    def body(x_ref, vec_ref, w_ref, h_ref, qkv_ref, rest_ref, hs):
        j = pl.program_id(1)

        @pl.when(j == 0)
        def _():
            h = _pre_norm(x_ref[...], vec_ref).astype(BF16)
            hs[...] = h
            h_ref[...] = h

        r = _dot(hs[...], w_ref[...])

        @pl.when(j < nq)
        def _():
            qkv_ref[...] = r.astype(BF16)

        @pl.when(j >= nq)
        def _():
            rest_ref[...] = r

    row = lambda i, j: (i, 0)
    return pl.pallas_call(
        body, name=name, grid=(S // tm, PW // tn),
        in_specs=[pl.BlockSpec((tm, D), row), pl.BlockSpec((8, D), lambda i, j: (0, 0)),
                  pl.BlockSpec((D, tn), lambda i, j: (0, j))],
        out_specs=[pl.BlockSpec((tm, D), row),
                   pl.BlockSpec((tm, tn), lambda i, j: (i, jnp.minimum(j, nq - 1))),
                   pl.BlockSpec((tm, tn), lambda i, j: (i, jnp.maximum(j - nq, 0)))],
        out_shape=[jax.ShapeDtypeStruct((S, D), BF16), jax.ShapeDtypeStruct((S, 1536), BF16),
                   jax.ShapeDtypeStruct((S, 4096), F32)],
        scratch_shapes=[pltpu.VMEM((tm, D), BF16)],
        compiler_params=_cp("parallel", "arbitrary"),
    )(x, vec, w_in)


def proj_bwd(dq, dkv, dxr, d3, w_in, x, dxo, vec, name, tm=1024, tk=512):
    S = x.shape[0]
    tm = min(tm, S)
    nk = PW // tk

    def body(dq_ref, dkv_ref, dxr_ref, d3_ref, w_ref, x_ref, dxo_ref, vec_ref, dx_ref, vacc_ref, acc):
        i, j = pl.program_id(0), pl.program_id(1)

        @pl.when((i == 0) & (j == 0))
        def _():
            vacc_ref[...] = jnp.zeros_like(vacc_ref)

        @pl.when(j == 0)
        def _():
            acc[...] = _dot_nt(dq_ref[...], w_ref[...])

        @pl.when((j >= 1) & (j < 3))
        def _():
            acc[...] += _dot_nt(dkv_ref[...], w_ref[...])

        @pl.when((j >= 3) & (j < 5))
        def _():
            acc[...] += _dot_nt(dxr_ref[...], w_ref[...])

        @pl.when(j >= 5)
        def _():
            acc[...] += _dot_nt(d3_ref[...], w_ref[...])

        @pl.when(j == nk - 1)
        def _():
            dx_ref[...] = _pre_norm_bwd(acc[...], x_ref[...], dxo_ref[...], vec_ref, vacc_ref)

    row = lambda i, j: (i, 0)
    return pl.pallas_call(
        body, name=name, grid=(S // tm, nk),
        in_specs=[pl.BlockSpec((None, tm, tk), lambda i, j: (0, i, 0)),
                  pl.BlockSpec((None, tm, tk), lambda i, j: (jnp.clip(j - 1, 0, 1), i, 0)),
                  pl.BlockSpec((tm, tk), lambda i, j: (i, jnp.clip(j - 3, 0, 1))),
                  pl.BlockSpec((None, tm, tk), lambda i, j: (jnp.clip(j - 5, 0, 5) // 2, i, jnp.clip(j - 5, 0, 5) % 2)),
                  pl.BlockSpec((D, tk), lambda i, j: (0, j)),
                  pl.BlockSpec((tm, D), row), pl.BlockSpec((tm, D), row),
                  pl.BlockSpec((8, D), lambda i, j: (0, 0))],
        out_specs=[pl.BlockSpec((tm, D), row), pl.BlockSpec((8, D), lambda i, j: (0, 0))],
        out_shape=[jax.ShapeDtypeStruct((S, D), F32), jax.ShapeDtypeStruct((8, D), F32)],
        scratch_shapes=[pltpu.VMEM((tm, D), F32)],
        compiler_params=_cp("arbitrary", "arbitrary"),
    )(dq, dkv, dxr, d3, w_in, x, dxo, vec)


def _two_heads(v, lane):
    zero = jnp.zeros((), v.dtype)
    return jnp.concatenate([jnp.where(lane < 64, v, zero), jnp.where(lane >= 64, v, zero)], axis=0)


def _attn_probs(qm, ka, bias_h, i, grp):
    s = _dot_nt(qm, ka) + bias_h
    col = lax.broadcasted_iota(jnp.int32, s.shape, 1)
    first_key = jnp.where(i == 0, 512 - 128 * grp, 0)
    s = jnp.where(col >= first_key, s, NEG)
    e = jnp.exp(s - jnp.max(s, axis=-1, keepdims=True))
    return e * (1.0 / jnp.sum(e, axis=-1, keepdims=True))


def attn_fwd(qkv, bias, name):
    S = qkv.shape[0]
    nb = S // TQ

    def body(q_ref, kp_ref, kc_ref, vp_ref, vc_ref, b_ref, o_ref, kw, vw):
        i = pl.program_id(1)
        kw[0:TQ, :] = kp_ref[...]
        kw[TQ:2 * TQ, :] = kc_ref[...]
        vw[0:TQ, :] = vp_ref[...]
        vw[TQ:2 * TQ, :] = vc_ref[...]
        lane = lax.broadcasted_iota(jnp.int32, (1, HP), 1)

        def group(a, carry):
            r0 = pl.multiple_of(a * 128, 128)
            qa = q_ref[pl.ds(r0, 128), :] * jnp.asarray(0.125, BF16)
            ka = kw[pl.ds(r0, WIN), :]
            va = vw[pl.ds(r0, WIN), :]
            p = _attn_probs(_two_heads(qa, lane), ka, b_ref[...], i, a)
            o2 = _dot(p.astype(BF16), va)
            o_ref[pl.ds(r0, 128), :] = jnp.where(lane < 64, o2[0:128], o2[128:256]).astype(BF16)
            return carry

        lax.fori_loop(0, TQ // 128, group, 0, unroll=True)

    prev = lambda h, i: (jnp.maximum(i - 1, 0), 0)
    return pl.pallas_call(
        body, name=name, grid=(4, nb),
        in_specs=[pl.BlockSpec((TQ, HP), lambda h, i: (i, h)),
                  pl.BlockSpec((TQ, HP), lambda h, i: (jnp.maximum(i - 1, 0), 4 + h)),
                  pl.BlockSpec((TQ, HP), lambda h, i: (i, 4 + h)),
                  pl.BlockSpec((TQ, HP), lambda h, i: (jnp.maximum(i - 1, 0), 8 + h)),
                  pl.BlockSpec((TQ, HP), lambda h, i: (i, 8 + h)),
                  pl.BlockSpec((None, 256, WIN), lambda h, i: (h, 0, 0))],
        out_specs=pl.BlockSpec((TQ, HP), lambda h, i: (i, h)),
        out_shape=jax.ShapeDtypeStruct((S, 512), BF16),
        scratch_shapes=[pltpu.VMEM((2 * TQ, HP), BF16), pltpu.VMEM((2 * TQ, HP), BF16)],
        compiler_params=_cp("parallel", "arbitrary"),
    )(qkv, qkv, qkv, qkv, qkv, bias)


def attn_bwd(qkv, do, bias, name):
    S = qkv.shape[0]
    nb = S // TQ

    def body(q_ref, kp_ref, kc_ref, vp_ref, vc_ref, do_ref, b_ref, dqkv_ref, db_ref, dkv_ref, kw, vw, ak, av):
        i = pl.program_id(1)

        @pl.when(i == 0)
        def _():
            db_ref[...] = jnp.zeros_like(db_ref)
            ak[...] = jnp.zeros_like(ak)
            av[...] = jnp.zeros_like(av)

        @pl.when(i > 0)
        def _():
            ak[0:TQ, :] = ak[TQ:2 * TQ, :]
            av[0:TQ, :] = av[TQ:2 * TQ, :]
            ak[TQ:2 * TQ, :] = jnp.zeros((TQ, HP), F32)
            av[TQ:2 * TQ, :] = jnp.zeros((TQ, HP), F32)

        @pl.when(i < nb)
        def _():
            kw[0:TQ, :] = kp_ref[...]
            kw[TQ:2 * TQ, :] = kc_ref[...]
            vw[0:TQ, :] = vp_ref[...]
            vw[TQ:2 * TQ, :] = vc_ref[...]
            lane = lax.broadcasted_iota(jnp.int32, (1, HP), 1)

            def group(a, carry):
                r0 = pl.multiple_of(a * 128, 128)
                q2 = _two_heads(q_ref[pl.ds(r0, 128), :] * jnp.asarray(0.125, BF16), lane)
                do2 = _two_heads(do_ref[pl.ds(r0, 128), :], lane)
                ka = kw[pl.ds(r0, WIN), :]
                va = vw[pl.ds(r0, WIN), :]
                p = _attn_probs(q2, ka, b_ref[...], i, a)
                dp = _dot_nt(do2, va)
                ds = p * (dp - jnp.sum(p * dp, axis=-1, keepdims=True))
                db_ref[...] += ds
                dsb = ds.astype(BF16)
                dq2 = _dot(dsb, ka)
                ak[pl.ds(r0, WIN), :] += _dot_tn(dsb, q2)
                av[pl.ds(r0, WIN), :] += _dot_tn(p.astype(BF16), do2)
                dq = jnp.where(lane < 64, dq2[0:128], dq2[128:256])
                dqkv_ref[0, pl.ds(r0, 128), :] = (dq * 0.125).astype(BF16)
                return carry

            lax.fori_loop(0, TQ // 128, group, 0, unroll=True)

        @pl.when(i > 0)
        def _():
            dkv_ref[0] = ak[0:TQ, :].astype(BF16)
            dkv_ref[1] = av[0:TQ, :].astype(BF16)

    cur = lambda i: jnp.minimum(i, nb - 1)
    prv = lambda i: jnp.clip(i - 1, 0, nb - 1)
    dq, db, dkv = pl.pallas_call(
        body, name=name, grid=(4, nb + 1),
        in_specs=[pl.BlockSpec((TQ, HP), lambda h, i: (cur(i), h)),
                  pl.BlockSpec((TQ, HP), lambda h, i: (prv(i), 4 + h)),
                  pl.BlockSpec((TQ, HP), lambda h, i: (cur(i), 4 + h)),
                  pl.BlockSpec((TQ, HP), lambda h, i: (prv(i), 8 + h)),
                  pl.BlockSpec((TQ, HP), lambda h, i: (cur(i), 8 + h)),
                  pl.BlockSpec((TQ, HP), lambda h, i: (cur(i), h)),
                  pl.BlockSpec((None, 256, WIN), lambda h, i: (h, 0, 0))],
        out_specs=[pl.BlockSpec((1, TQ, HP), lambda h, i: (0, cur(i), h)),
                   pl.BlockSpec((None, 256, WIN), lambda h, i: (h, 0, 0)),
                   pl.BlockSpec((2, TQ, HP), lambda h, i: (0, prv(i), h))],
        out_shape=[jax.ShapeDtypeStruct((1, S, 512), BF16), jax.ShapeDtypeStruct((4, 256, WIN), F32),
                   jax.ShapeDtypeStruct((2, S, 512), BF16)],
        scratch_shapes=[pltpu.VMEM((2 * TQ, HP), BF16), pltpu.VMEM((2 * TQ, HP), BF16),
                        pltpu.VMEM((2 * TQ, HP), F32), pltpu.VMEM((2 * TQ, HP), F32)],
        compiler_params=_cp("parallel", "arbitrary"),
    )(qkv, qkv, qkv, qkv, qkv, do, bias)
    return dq, db, dkv


def bias_grad(db, name):
    def body(db_ref, o_ref):
        r = lax.broadcasted_iota(jnp.int32, (128, 128), 0)
        c = lax.broadcasted_iota(jnp.int32, (128, 128), 1)
        flip = (r + c == 127).astype(BF16)
        lane = lax.broadcasted_iota(jnp.int32, (16, 384), 1)
        src = lax.broadcasted_iota(jnp.int32, (128, 384), 0)
        dst = lax.broadcasted_iota(jnp.int32, (128, 384), 1)

        def split_dot(v, m):
            hi = v.astype(BF16)
            r1 = v - hi.astype(F32)
            mid = r1.astype(BF16)
            lo = (r1 - mid.astype(F32)).astype(BF16)
            return _dot(hi, m) + _dot(mid, m) + _dot(lo, m)

        def diag_sums(w):
            y = pltpu.roll(split_dot(w, flip), 0, 1, stride=1, stride_axis=0)
            return jnp.broadcast_to(_colsum(y), (16, 128))

        w4 = db_ref[0, :, 512:640]
        w3 = db_ref[0, :, 384:512]
        far = jnp.sum(db_ref[0, :, 0:384]) + jnp.sum(jnp.where(r >= c, w3, 0.0))
        lo4 = diag_sums(jnp.where(r >= c, w4, 0.0))
        up4 = diag_sums(jnp.where(r < c, w4, 0.0))
        up3 = diag_sums(jnp.where(r < c, w3, 0.0))
        p_lo4 = (dst == 128 + (src + 1) % 128).astype(BF16)
        p_up4 = ((dst == src + 1) & (src < 127)).astype(BF16)
        p_up3 = ((dst == src + 129) & (src < 127)).astype(BF16)
        out = split_dot(lo4, p_lo4) + split_dot(up4, p_up4) + split_dot(up3, p_up3)
        o_ref[0] = out + jnp.where(lane == 256, far, 0.0)

    return pl.pallas_call(
        body, name=name, grid=(8,),
        in_specs=[pl.BlockSpec((1, 128, WIN), lambda h: (h, 0, 0))],
        out_specs=pl.BlockSpec((1, 16, 384), lambda h: (h, 0, 0)),
        out_shape=jax.ShapeDtypeStruct((8, 16, 384), F32),
        compiler_params=_cp("parallel"),
    )(db)[:, 0, :]


LT = 256
LC = 512


def _lru_gates(xs, pv_ref, wa_ref, wx_ref, tl):
    xc = (pv_ref[4:5, :] + pv_ref[3:4, :] * xs[pl.ds(8, tl), :] + pv_ref[2:3, :] * xs[pl.ds(7, tl), :]
          + pv_ref[1:2, :] * xs[pl.ds(6, tl), :] + pv_ref[0:1, :] * xs[pl.ds(5, tl), :])
    xcb = xc.astype(BF16)
    pa = jnp.concatenate([_dot(xcb[:, 0:256], wa_ref[0]), _dot(xcb[:, 256:512], wa_ref[1])], axis=1)
    px = jnp.concatenate([_dot(xcb[:, 0:256], wx_ref[0]), _dot(xcb[:, 256:512], wx_ref[1])], axis=1)
    r = _sigmoid(pa + pv_ref[5:6, :])
    ig = _sigmoid(px + pv_ref[6:7, :])
    z = -pv_ref[7:8, :]
    sp = jnp.maximum(z, 0.0) + jnp.log1p(jnp.exp(-jnp.abs(z)))
    log_a = (-LRU_C * r) * sp
    a = jnp.exp(log_a)
    s = jnp.tanh(-log_a) * (1.0 + a * a)
    inv_mult = lax.rsqrt(s)
    mult = jnp.where(s > 0.0, s * inv_mult, 0.0)
    return xc, xcb, r, ig, sp, a, mult, inv_mult


def lru_fwd(rest, pvec, wa, wx, name):
    S = rest.shape[0]
    tl = min(LT, S)
    nt = S // tl

    def body(xr_ref, halo_ref, yr_ref, pv_ref, wa_ref, wx_ref, h_ref, hg_ref, xs, a_s, u_s, h_s, carry):
        ti = pl.program_id(1)

        @pl.when(ti == 0)
        def _():
            carry[...] = jnp.zeros_like(carry)

        xs[0:8, :] = jnp.where(ti > 0, halo_ref[...], 0.0)
        xs[pl.ds(8, tl), :] = xr_ref[...]
        xc, _, _, ig, _, a, mult, _ = _lru_gates(xs, pv_ref, wa_ref, wx_ref, tl)
        a_s[...] = a
        u_s[...] = mult * (ig * xc)
        row = lax.broadcasted_iota(jnp.int32, (8, LC), 0)

        def blk(bi, c):
            o = pl.multiple_of(bi * 8, 8)
            av = a_s[pl.ds(o, 8), :]
            bv = u_s[pl.ds(o, 8), :]
            for d in (1, 2, 4):
                a_sh = pltpu.roll(av, d, 0)
                b_sh = pltpu.roll(bv, d, 0)
                m = row >= d
                bv = jnp.where(m, av * b_sh + bv, bv)
                av = jnp.where(m, av * a_sh, av)
            hv = bv + av * c
            h_s[pl.ds(o, 8), :] = hv
            return hv[7:8, :]

        carry[...] = lax.fori_loop(0, tl // 8, blk, carry[...])
        h = h_s[...]
        h_ref[...] = h
        hg_ref[...] = (h * _gelu(yr_ref[...])).astype(BF16)

    hb = tl // 8
    return pl.pallas_call(
        body, name=name, grid=(2, nt),
        in_specs=[pl.BlockSpec((tl, LC), lambda c, t: (t, c)),
                  pl.BlockSpec((8, LC), lambda c, t: (jnp.maximum(t * hb - 1, 0), c)),
                  pl.BlockSpec((tl, LC), lambda c, t: (t, 2 + c)),
                  pl.BlockSpec((8, LC), lambda c, t: (0, c)),
                  pl.BlockSpec((2, 256, 256), lambda c, t: (c, 0, 0)),
                  pl.BlockSpec((2, 256, 256), lambda c, t: (c, 0, 0))],
        out_specs=[pl.BlockSpec((tl, LC), lambda c, t: (t, c)), pl.BlockSpec((tl, LC), lambda c, t: (t, c))],
        out_shape=[jax.ShapeDtypeStruct((S, D), F32), jax.ShapeDtypeStruct((S, D), BF16)],
        scratch_shapes=[pltpu.VMEM((tl + 8, LC), F32), pltpu.VMEM((tl, LC), F32), pltpu.VMEM((tl, LC), F32),
                        pltpu.VMEM((tl, LC), F32), pltpu.VMEM((1, LC), F32)],
        compiler_params=_cp("parallel", "arbitrary"),
    )(rest, rest, rest, pvec, wa, wx)


def lru_bwd(dh, h, rest, pvec, wa, wx, name):
    S = rest.shape[0]
    tl = min(LT, S)
    nt = S // tl

    def body(dh_ref, h_ref, hhalo_ref, xr_ref, xhalo_ref, pv_ref, wa_ref, wx_ref,
             dxr_ref, vacc_ref, dwa_ref, dwx_ref,
             xs, hs, a_s, ash_s, b_s, lam_s, dxe, anext, lnext, dxnext):
        ti = pl.program_id(1)
        tr = nt - 1 - ti

        @pl.when(ti == 0)
        def _():
            anext[...] = jnp.zeros_like(anext)
            lnext[...] = jnp.zeros_like(lnext)
            dxnext[...] = jnp.zeros_like(dxnext)
            vacc_ref[...] = jnp.zeros_like(vacc_ref)
            dwa_ref[...] = jnp.zeros_like(dwa_ref)
            dwx_ref[...] = jnp.zeros_like(dwx_ref)

        xs[0:8, :] = jnp.where(tr > 0, xhalo_ref[...], 0.0)
        xs[pl.ds(8, tl), :] = xr_ref[...]
        xc, xcb, r, ig, sp, a, mult, inv_mult = _lru_gates(xs, pv_ref, wa_ref, wx_ref, tl)

        a_s[pl.ds(0, tl), :] = a
        a_s[pl.ds(tl, 8), :] = jnp.broadcast_to(anext[...], (8, LC))
        ash_s[...] = a_s[pl.ds(1, tl), :]
        b_s[...] = dh_ref[...]
        row = lax.broadcasted_iota(jnp.int32, (8, LC), 0)

        def blk(k, c):
            o = pl.multiple_of((tl // 8 - 1 - k) * 8, 8)
            av = ash_s[pl.ds(o, 8), :]
            bv = b_s[pl.ds(o, 8), :]
            for d in (1, 2, 4):
                a_sh = pltpu.roll(av, 8 - d, 0)
                b_sh = pltpu.roll(bv, 8 - d, 0)
                m = row < 8 - d
                bv = jnp.where(m, bv + av * b_sh, bv)
                av = jnp.where(m, av * a_sh, av)
            lv = bv + av * c
            lam_s[pl.ds(o, 8), :] = lv
            return lv[0:1, :]

        lnext[...] = lax.fori_loop(0, tl // 8, blk, lnext[...])
        anext[...] = a[0:1, :]
        lam = lam_s[...]

        hs[0:8, :] = jnp.where(tr > 0, hhalo_ref[...], 0.0)
        hs[pl.ds(8, tl), :] = h_ref[...]
        d_a = lam * hs[pl.ds(7, tl), :]
        d_mult = lam * (ig * xc)
        d_ig = lam * mult * xc
        dxc = lam * mult * ig
        d_log_a = d_a * a - d_mult * (a * a) * inv_mult
        d_r = d_log_a * (-LRU_C * sp)
        vacc_ref[7:8, :] += _colsum(d_log_a * (-LRU_C * r)) * (-_sigmoid(-pv_ref[7:8, :]))
        d_pa = d_r * r * (1.0 - r)
        d_px = d_ig * ig * (1.0 - ig)
        vacc_ref[5:6, :] += _colsum(d_pa)
        vacc_ref[6:7, :] += _colsum(d_px)
        dpa = d_pa.astype(BF16)
        dpx = d_px.astype(BF16)
        back = []
        for g in range(2):
            sl = slice(256 * g, 256 * g + 256)
            dwa_ref[g] += _dot_tn(xcb[:, sl], dpa[:, sl])
            dwx_ref[g] += _dot_tn(xcb[:, sl], dpx[:, sl])
            back.append(_dot_nt(dpa[:, sl], wa_ref[g]) + _dot_nt(dpx[:, sl], wx_ref[g]))
        dxc = dxc + jnp.concatenate(back, axis=1)
        vacc_ref[4:5, :] += _colsum(dxc)
        for k in range(4):
            vacc_ref[k:k + 1, :] += _colsum(dxc * xs[pl.ds(5 + k, tl), :])
        dxe[pl.ds(0, tl), :] = dxc
        dxe[pl.ds(tl, 8), :] = dxnext[...]
        dxr = (pv_ref[3:4, :] * dxc + pv_ref[2:3, :] * dxe[pl.ds(1, tl), :]
               + pv_ref[1:2, :] * dxe[pl.ds(2, tl), :] + pv_ref[0:1, :] * dxe[pl.ds(3, tl), :])
        dxr_ref[...] = dxr.astype(BF16)
        dxnext[...] = dxc[0:8, :]

    hb = tl // 8
    rev = lambda t: nt - 1 - t
    halo = lambda t: jnp.maximum(rev(t) * hb - 1, 0)
    big = lambda: pltpu.VMEM((tl + 8, LC), F32)
    til = lambda: pltpu.VMEM((tl, LC), F32)
    return pl.pallas_call(
        body, name=name, grid=(2, nt),
        in_specs=[pl.BlockSpec((tl, LC), lambda c, t: (rev(t), c)),
                  pl.BlockSpec((tl, LC), lambda c, t: (rev(t), c)),
                  pl.BlockSpec((8, LC), lambda c, t: (halo(t), c)),
                  pl.BlockSpec((tl, LC), lambda c, t: (rev(t), c)),
                  pl.BlockSpec((8, LC), lambda c, t: (halo(t), c)),
                  pl.BlockSpec((8, LC), lambda c, t: (0, c)),
                  pl.BlockSpec((2, 256, 256), lambda c, t: (c, 0, 0)),
                  pl.BlockSpec((2, 256, 256), lambda c, t: (c, 0, 0))],
        out_specs=[pl.BlockSpec((tl, LC), lambda c, t: (rev(t), c)),
                   pl.BlockSpec((8, LC), lambda c, t: (0, c)),
                   pl.BlockSpec((2, 256, 256), lambda c, t: (c, 0, 0)),
                   pl.BlockSpec((2, 256, 256), lambda c, t: (c, 0, 0))],
        out_shape=[jax.ShapeDtypeStruct((S, D), BF16), jax.ShapeDtypeStruct((8, D), F32),
                   jax.ShapeDtypeStruct((4, 256, 256), F32), jax.ShapeDtypeStruct((4, 256, 256), F32)],
        scratch_shapes=[big(), big(), big(), til(), til(), til(), big(),
                        pltpu.VMEM((1, LC), F32), pltpu.VMEM((1, LC), F32), pltpu.VMEM((8, LC), F32)],
        compiler_params=_cp("parallel", "arbitrary"),
    )(dh, h, h, rest, rest, pvec, wa, wx)


def mix_out_fwd(x, ao, hg, rest, vec, w_att_o, w_rec_o, w_out, name, tm=256):
    S = x.shape[0]
    tm = min(tm, S)

    def body(x_ref, ao_ref, hg_ref, ga_ref, gr_ref, vec_ref, wa_ref, wr_ref, wo_ref,
             xo_ref, att_ref, rec_ref, mg_ref, f_ref):
        att = _dot(ao_ref[...], wa_ref[...])
        rec = _dot(hg_ref[...], wr_ref[...])
        att_ref[...] = att
        rec_ref[...] = rec
        mg = (_sigmoid(ga_ref[...]) * att + _sigmoid(gr_ref[...]) * rec).astype(BF16)
        mg_ref[...] = mg
        f = _dot(mg, wo_ref[...])
        f_ref[...] = f
        y = f * lax.rsqrt(_mean(f * f) + EPS) * vec_ref[1:2, :]
        xo_ref[...] = x_ref[...] + (1.0 * vec_ref[4:5, :]) * y

    row = lambda i: (i, 0)
    full = lambda r: pl.BlockSpec((r, D), lambda i: (0, 0))
    return pl.pallas_call(
        body, name=name, grid=(S // tm,),
        in_specs=[pl.BlockSpec((tm, D), row), pl.BlockSpec((tm, 512), row), pl.BlockSpec((tm, D), row),
                  pl.BlockSpec((tm, D), lambda i: (i, 2)), pl.BlockSpec((tm, D), lambda i: (i, 3)),
                  full(8), full(512), full(D), full(D)],
        out_specs=[pl.BlockSpec((tm, D), row)] * 5,
        out_shape=[jax.ShapeDtypeStruct((S, D), F32), jax.ShapeDtypeStruct((S, D), F32),
                   jax.ShapeDtypeStruct((S, D), F32), jax.ShapeDtypeStruct((S, D), BF16),
                   jax.ShapeDtypeStruct((S, D), F32)],
        compiler_params=_cp("parallel"),
    )(x, ao, hg, rest, rest, vec, w_att_o, w_rec_o, w_out)


def mix_out_bwd(dxo, f, att, rec, rest, h, vec, w_att_o, w_rec_o, w_out, name, tm=256):
    S = dxo.shape[0]
    tm = min(tm, S)

    def body(dxo_ref, f_ref, att_ref, rec_ref, yr_ref, ga_ref, gr_ref, h_ref, vec_ref, wa_ref, wr_ref, wo_ref,
             df_ref, da_ref, dr_ref, dao_ref, dh_ref, d3_ref, vacc_ref):
        @pl.when(pl.program_id(0) == 0)
        def _():
            vacc_ref[...] = jnp.zeros_like(vacc_ref)

        df = _post_norm_bwd(dxo_ref[...], f_ref[...], 1.0, vec_ref, vacc_ref).astype(BF16)
        df_ref[...] = df
        dm = _dot_nt(df, wo_ref[...])
        sa = _sigmoid(ga_ref[...])
        sr = _sigmoid(gr_ref[...])
        d_att = (dm * sa).astype(BF16)
        d_rec = (dm * sr).astype(BF16)
        da_ref[...] = d_att
        dr_ref[...] = d_rec
        d3_ref[1] = (dm * att_ref[...] * (sa * (1.0 - sa))).astype(BF16)
        d3_ref[2] = (dm * rec_ref[...] * (sr * (1.0 - sr))).astype(BF16)
        dao_ref[...] = _dot_nt(d_att, wa_ref[...]).astype(BF16)
        d_hg = _dot_nt(d_rec, wr_ref[...])
        yr = yr_ref[...]
        t = jnp.tanh(_GK * (yr + 0.044715 * yr * yr * yr))
        dh_ref[...] = d_hg * (0.5 * yr * (1.0 + t))
        gelu_grad = 0.5 * (1.0 + t) + 0.5 * yr * (1.0 - t * t) * _GK * (1.0 + 3.0 * 0.044715 * yr * yr)
        d3_ref[0] = (d_hg * h_ref[...] * gelu_grad).astype(BF16)

    row = lambda i: (i, 0)
    full = lambda r: pl.BlockSpec((r, D), lambda i: (0, 0))
    return pl.pallas_call(
        body, name=name, grid=(S // tm,),
        in_specs=[pl.BlockSpec((tm, D), row)] * 4
        + [pl.BlockSpec((tm, D), lambda i: (i, 1)), pl.BlockSpec((tm, D), lambda i: (i, 2)),
           pl.BlockSpec((tm, D), lambda i: (i, 3)), pl.BlockSpec((tm, D), row),
           full(8), full(512), full(D), full(D)],
        out_specs=[pl.BlockSpec((tm, D), row)] * 3
        + [pl.BlockSpec((tm, 512), row), pl.BlockSpec((tm, D), row),
           pl.BlockSpec((3, tm, D), lambda i: (0, i, 0)), pl.BlockSpec((8, D), lambda i: (0, 0))],
        out_shape=[jax.ShapeDtypeStruct((S, D), BF16)] * 3
        + [jax.ShapeDtypeStruct((S, 512), BF16), jax.ShapeDtypeStruct((S, D), F32),
           jax.ShapeDtypeStruct((3, S, D), BF16), jax.ShapeDtypeStruct((8, D), F32)],
        compiler_params=_cp("arbitrary"),
    )(dxo, f, att, rec, rest, rest, rest, h, vec, w_att_o, w_rec_o, w_out)


def loss_grad(y, tgt, name, tm=512):
    S = y.shape[0]
    tm = min(tm, S)
    nt = S // tm

    def body(y_ref, t_ref, dy_ref, l_ref, acc):
        i = pl.program_id(0)

        @pl.when(i == 0)
        def _():
            acc[...] = jnp.zeros_like(acc)

        d = y_ref[...] - t_ref[...]
        dy_ref[...] = d * (1.0 / D)
        acc[...] += _colsum(d * d)

        @pl.when(i == nt - 1)
        def _():
            l_ref[...] = jnp.broadcast_to(0.5 * jnp.sum(acc[...]) * (1.0 / D), (8, 128))

    return pl.pallas_call(
        body, name=name, grid=(nt,),
        in_specs=[pl.BlockSpec((tm, D), lambda i: (i, 0))] * 2,
        out_specs=[pl.BlockSpec((tm, D), lambda i: (i, 0)), pl.BlockSpec((8, 128), lambda i: (0, 0))],
        out_shape=[jax.ShapeDtypeStruct((S, D), F32), jax.ShapeDtypeStruct((8, 128), F32)],
        scratch_shapes=[pltpu.VMEM((1, D), F32)],
        compiler_params=_cp("arbitrary"),
    )(y, tgt)


def ada_fwd(c_all, w_ada, b_ada, name, tn=768):
    n = w_ada.shape[1]

    def body(c_ref, w_ref, b_ref, o_ref):
        cv = c_ref[...]
        ca = (cv * _sigmoid(cv)).astype(BF16)
        o_ref[...] = _dot(ca, w_ref[...].astype(BF16)) + b_ref[...]

    return pl.pallas_call(
        body, name=name, grid=(n // tn,),
        in_specs=[pl.BlockSpec((8, D), lambda j: (0, 0)), pl.BlockSpec((D, tn), lambda j: (0, j)),
                  pl.BlockSpec((1, tn), lambda j: (0, j))],
        out_specs=pl.BlockSpec((8, tn), lambda j: (0, j)),
        out_shape=jax.ShapeDtypeStruct((8, n), F32),
        compiler_params=_cp("parallel"),
    )(c_all, w_ada, b_ada)


def ada_bwd(c_all_t, dmod, name, tn=768):
    n = dmod.shape[1]

    def body(c_ref, d_ref, o_ref):
        cv = c_ref[...]
        ca = (cv * _sigmoid(cv)).astype(BF16)
        o_ref[...] = _dot(ca, d_ref[...].astype(BF16))

    return pl.pallas_call(
        body, name=name, grid=(n // tn,),
        in_specs=[pl.BlockSpec((D, 128), lambda j: (0, 0)), pl.BlockSpec((128, tn), lambda j: (0, j))],
        out_specs=pl.BlockSpec((D, tn), lambda j: (0, j)),
        out_shape=jax.ShapeDtypeStruct((D, n), F32),
        compiler_params=_cp("parallel"),
    )(c_all_t, dmod)


def _row_tile(rows, cols, itemsize=4, budget=1536 * 1024):
    best = None
    for t in range(8, rows + 1, 8):
        if rows % t == 0 and t * cols * itemsize <= budget:
            best = t
    return rows if best is None else best


def sum_lead(parts, name, out_dtype=F32):
    n, R, C = parts.shape
    tr = _row_tile(R, C * n)

    def body(p_ref, o_ref):
        acc = p_ref[0].astype(F32)
        for k in range(1, n):
            acc = acc + p_ref[k].astype(F32)
        o_ref[...] = acc.astype(out_dtype)

    return pl.pallas_call(
        body, name=name, grid=(R // tr,),
        in_specs=[pl.BlockSpec((n, tr, C), lambda i: (0, i, 0))],
        out_specs=pl.BlockSpec((tr, C), lambda i: (i, 0)),
        out_shape=jax.ShapeDtypeStruct((R, C), out_dtype),
        compiler_params=_cp("parallel"),
    )(parts)


def adamw(w, g, m, v, name):
    R, C = w.shape
    tr = _row_tile(R, C * 7, budget=8 * 1024 * 1024)

    def body(w_ref, g_ref, m_ref, v_ref, d_ref, mo_ref, vo_ref):
        gv = g_ref[...]
        mn = ADAM_B1 * m_ref[...] + (1.0 - ADAM_B1) * gv
        vn = ADAM_B2 * v_ref[...] + (1.0 - ADAM_B2) * (gv * gv)
        m_hat = mn / (1.0 - ADAM_B1 ** ADAM_STEP)
        v_hat = vn / (1.0 - ADAM_B2 ** ADAM_STEP)
        d_ref[...] = -ADAM_LR * (m_hat / (jnp.sqrt(v_hat) + ADAM_EPS) + ADAM_WD * w_ref[...])
        mo_ref[...] = mn
        vo_ref[...] = vn

    spec = pl.BlockSpec((tr, C), lambda i: (i, 0))
    return pl.pallas_call(
        body, name=name, grid=(R // tr,),
        in_specs=[spec] * 4, out_specs=[spec] * 3,
        out_shape=[jax.ShapeDtypeStruct((R, C), F32)] * 3,
        compiler_params=_cp("parallel"),
    )(w, g, m, v)


def _mesh_pos():
    return lax.axis_index("x"), lax.axis_index("y"), lax.axis_index("c")


def _other_chips(mx, my):
    return [(1 - mx, my), (mx, 1 - my), (1 - mx, 1 - my)]


def ag_small(x, name):
    R = x.shape[0]

    def body(x_ref, out_ref, send_sems, recv_sems, local_sem):
        mx, my, mc = _mesh_pos()
        me, sibling = (mx, my, mc), (mx, my, 1 - mc)
        chips = _other_chips(mx, my)

        def slot(px, py, pc):
            return out_ref.at[4 * px + 2 * py + pc]

        def copy(k, block, to, src=None):
            return pltpu.make_async_remote_copy(
                src_ref=slot(*block) if src is None else src, dst_ref=slot(*block),
                send_sem=send_sems.at[k], recv_sem=recv_sems.at[k], device_id=to, device_id_type=MESH)

        mine = pltpu.make_async_copy(x_ref, slot(*me), local_sem)
        mine.start()
        first = [copy(0, me, sibling, src=x_ref)]
        first += [copy(1 + j, me, (*chip, mc), src=x_ref) for j, chip in enumerate(chips)]
        for cp in first:
            cp.start()
        passed = [copy(4 + j, (*chip, mc), sibling) for j, chip in enumerate(chips)]
        for j, chip in enumerate(chips):
            copy(1 + j, (*chip, mc), me).wait_recv()
            passed[j].start()
        copy(0, sibling, me).wait_recv()
        for j, chip in enumerate(chips):
            copy(4 + j, (*chip, 1 - mc), me).wait_recv()
        for cp in first + passed:
            cp.wait_send()
        mine.wait()

    return pl.pallas_call(
        body, name=name,
        out_shape=jax.ShapeDtypeStruct((N_DEV, R, 128), F32),
        in_specs=[pl.BlockSpec(memory_space=pltpu.VMEM)],
        out_specs=pl.BlockSpec(memory_space=pltpu.VMEM),
        scratch_shapes=[pltpu.SemaphoreType.DMA((7,)), pltpu.SemaphoreType.DMA((7,)), pltpu.SemaphoreType.DMA],
        compiler_params=pltpu.CompilerParams(vmem_limit_bytes=VMEM_LIMIT),
    )(x)


BIG = (("ffn1_w_gu", "col", D, PW), ("ffn1_w_down", "row", FF, D), ("w_in", "col", D, PW),
       ("w_att_o", "col", 512, D), ("w_rec_o", "row", D, D), ("w_out", "row", D, D),
       ("ffn2_w_gu", "col", D, PW), ("ffn2_w_down", "row", FF, D))
NBIG = len(BIG)


def _shard_shape(kind, R, C):
    return (R, C // 4) if kind == "col" else (R // 4, C)


def _region(ref, kind, R, C, q, half, t, tr):
    sr, sc = _shard_shape(kind, R, C)
    if kind == "col":
        return ref.at[pl.ds(pl.multiple_of(half * (R // 2) + t * tr, 16), tr), pl.ds(q * sc, sc)]
    return ref.at[pl.ds(pl.multiple_of(q * sr + t * tr, 16), tr), pl.ds(half * (C // 2), C // 2)]


def ag_local(w, kind, R, C, p_arr, name, after=()):
    sr, sc = _shard_shape(kind, R, C)
    tr = _row_tile(sr, sc, budget=2 * 1024 * 1024)
    nt = sr // tr
    after = list(after)

    def body(p_ref, w_ref, *rest):
        rest[-1][...] = w_ref[...].astype(BF16)

    if kind == "col":
        o_spec = pl.BlockSpec((tr, sc), lambda i, p: (i, p[0]))
    else:
        o_spec = pl.BlockSpec((tr, sc), lambda i, p: (p[0] * nt + i, 0))
    return pl.pallas_call(
        body, name=name,
        grid_spec=pltpu.PrefetchScalarGridSpec(
            num_scalar_prefetch=1, grid=(nt,),
            in_specs=[pl.BlockSpec((tr, sc), lambda i, p: (i, 0))] + [ANY] * len(after), out_specs=o_spec),
        out_shape=jax.ShapeDtypeStruct((R, C), BF16),
        compiler_params=_cp("parallel"),
    )(p_arr, w, *after)


HBM_SPEC = pl.BlockSpec(memory_space=pltpu.HBM)
SEM_SPEC = pl.BlockSpec(memory_space=pltpu.SEMAPHORE)


def _ag_sems(geoms):
    return sum(6 if both else 3 for (_, _, _, both) in geoms)


def _ag_copies(fulls, geoms, ssem, rsem, mx, my, mc, q, h):
    chips = _other_chips(mx, my)
    out, base = [], 0
    for w, (kind, R, C, both) in enumerate(geoms):
        sr, sc = _shard_shape(kind, R, C)
        hr = sr // 2 if kind == "col" else sr
        reg = _region(fulls[w], kind, R, C, q, h, 0, hr)
        out.append([pltpu.make_async_remote_copy(
            src_ref=reg, dst_ref=reg, send_sem=ssem.at[base + 3 * t + k], recv_sem=rsem.at[base + 3 * t + k],
            device_id=(*chips[k], mc if t == 0 else 1 - mc), device_id_type=MESH)
            for t in range(2 if both else 1) for k in range(3)])
        base += 6 if both else 3
    return out


def ag_start(fulls, geoms, after, name):
    n = len(fulls)
    after = list(after)
    m = len(after)

    def body(*refs):
        ssem, rsem = refs[n + m:n + m + 2]
        outs, token = refs[n + m + 2:2 * n + m + 2], refs[2 * n + m + 2]
        mx, my, mc = _mesh_pos()
        p = 2 * mx + my
        col = [w for w, g in enumerate(geoms) if g[0] == "col"]
        row = [w for w, g in enumerate(geoms) if g[0] == "row"]
        for q in range(4):
            @pl.when(p == q)
            def _(q=q):
                cps = _ag_copies(outs, geoms, ssem, rsem, mx, my, mc, q, mc)
                for w in col:
                    for cp in cps[w]:
                        cp.start()
        for h in range(2):
            @pl.when(mc == h)
            def _(h=h):
                cps = _ag_copies(outs, geoms, ssem, rsem, mx, my, mc, p, h)
                for w in row:
                    for cp in cps[w]:
                        cp.start()
        token[...] = jnp.zeros_like(token)

    res = pl.pallas_call(
        body, name=name,
        out_shape=[pltpu.SemaphoreType.DMA((_ag_sems(geoms),)), pltpu.SemaphoreType.DMA((_ag_sems(geoms),))]
        + [pltpu.HBM(a.shape, a.dtype) for a in fulls] + [jax.ShapeDtypeStruct((8, 128), F32)],
        in_specs=[HBM_SPEC] * n + [ANY] * m,
        out_specs=[SEM_SPEC, SEM_SPEC] + [HBM_SPEC] * n + [pl.BlockSpec(memory_space=pltpu.VMEM)],
        input_output_aliases={w: 2 + w for w in range(n)},
        compiler_params=pltpu.CompilerParams(has_side_effects=pltpu.SideEffectType.DATAFLOW_SIDE_EFFECTING),
    )(*[pltpu.with_memory_space_constraint(a, pltpu.HBM) for a in fulls], *after)
    return res[0], res[1], list(res[2:2 + n]), res[2 + n]


def ag_wait(fulls, geoms, ssem, rsem, after, name):
    n = len(fulls)

    def body(*refs):
        ins, ssem_ref, rsem_ref = refs[:n], refs[n], refs[n + 1]
        mx, my, mc = _mesh_pos()
        for cps in _ag_copies(ins, geoms, ssem_ref, rsem_ref, mx, my, mc, 0, 0):
            for cp in cps:
                cp.wait_send()
                cp.wait_recv()

    return list(pl.pallas_call(
        body, name=name,
        out_shape=[pltpu.HBM(a.shape, a.dtype) for a in fulls],
        in_specs=[HBM_SPEC] * n + [SEM_SPEC, SEM_SPEC, ANY],
        out_specs=[HBM_SPEC] * n,
        input_output_aliases={w: w for w in range(n)},
        compiler_params=pltpu.CompilerParams(has_side_effects=pltpu.SideEffectType.DATAFLOW_SIDE_EFFECTING),
    )(*fulls, ssem, rsem, after))


def ag_forward(full, kind, R, C, name):
    sr, sc = _shard_shape(kind, R, C)
    hr, hc = (sr // 2, sc) if kind == "col" else (sr, sc // 2)
    tr = _row_tile(hr, hc, itemsize=2, budget=512 * 1024)
    nt = hr // tr

    total = 3 * nt

    def body(src_ref, full_ref, stage, lsem, ssem, rsem):
        step = pl.program_id(0) * nt + pl.program_id(1)
        par = step % 2
        mx, my, mc = _mesh_pos()

        def load(s, q, h, t):
            return pltpu.make_async_copy(_region(src_ref, kind, R, C, q, h, t, tr), stage.at[s], lsem.at[s])

        def push(s, q, h, t):
            return pltpu.make_async_remote_copy(src_ref=stage.at[s], dst_ref=_region(full_ref, kind, R, C, q, h, t, tr),
                                                send_sem=ssem.at[s], recv_sem=rsem, device_id=(mx, my, 1 - mc),
                                                device_id_type=MESH)

        def for_tile(stp, fn):
            q_k = _partner_chip(stp // nt, 2 * mx + my)
            if kind == "col":
                for q in range(4):
                    @pl.when(q_k == q)
                    def _(q=q):
                        fn(q, mc, stp % nt)
            else:
                for h in range(2):
                    @pl.when(mc == h)
                    def _(h=h):
                        fn(q_k, h, stp % nt)

        @pl.when(step == 0)
        def _():
            for_tile(step, lambda q, h, t: load(0, q, h, t).start())

        load(par, 0, 0, 0).wait()
        for_tile(step, lambda q, h, t: push(par, q, h, t).start())

        @pl.when(step + 1 < total)
        def _():
            @pl.when(step >= 1)
            def _():
                push(1 - par, 0, 0, 0).wait_send()
            for_tile(step + 1, lambda q, h, t: load(1 - par, q, h, t).start())

        @pl.when(step == total - 1)
        def _():
            push(par, 0, 0, 0).wait_send()
            push(1 - par, 0, 0, 0).wait_send()
            three = full_ref.at[pl.ds(0, hr), pl.ds(0, 3 * hc)] if kind == "col" else full_ref.at[pl.ds(0, 3 * hr), pl.ds(0, hc)]
            pltpu.make_async_remote_copy(src_ref=three, dst_ref=three, send_sem=ssem.at[0], recv_sem=rsem,
                                         device_id=(mx, my, 1 - mc), device_id_type=MESH).wait_recv()

    return pl.pallas_call(
        body, name=name, grid=(3, nt),
        in_specs=[ANY], out_specs=ANY,
        out_shape=jax.ShapeDtypeStruct((R, C), BF16),
        scratch_shapes=[pltpu.VMEM((2, tr, hc), BF16), pltpu.SemaphoreType.DMA((2,)), pltpu.SemaphoreType.DMA((2,)),
                        pltpu.SemaphoreType.DMA],
        input_output_aliases={0: 0},
        compiler_params=_cp("arbitrary", "arbitrary"),
    )(full)


def _half_shape(kind, R, C):
    return (R // 2, C) if kind == "col" else (R, C // 2)


def _piece_shape(kind, R, C):
    return (R // 2, C // 4) if kind == "col" else (R // 4, C // 2)


def pair_push(g, kind, c_arr, name):
    R, C = g.shape
    hr, hc = _half_shape(kind, R, C)
    tr = _row_tile(hr, hc, itemsize=2, budget=1024 * 1024)
    nt = hr // tr

    def body(c_ref, g_ref, out_ref, stage, ssem, rsem):
        i = pl.program_id(0)
        slot = i % 2
        mx, my, mc = _mesh_pos()

        def push(s, t):
            return pltpu.make_async_remote_copy(
                src_ref=stage.at[s], dst_ref=out_ref.at[pl.ds(pl.multiple_of(t * tr, 16), tr)],
                send_sem=ssem.at[s], recv_sem=rsem, device_id=(mx, my, 1 - mc), device_id_type=MESH)

        @pl.when(i >= 2)
        def _():
            push(slot, 0).wait_send()

        stage[slot] = g_ref[...]
        push(slot, i).start()

        @pl.when(i == nt - 1)
        def _():
            push(slot, 0).wait_send()
            if nt >= 2:
                push(1 - slot, 0).wait_send()
            pltpu.make_async_remote_copy(src_ref=out_ref, dst_ref=out_ref, send_sem=ssem.at[0], recv_sem=rsem,
                                         device_id=(mx, my, 1 - mc), device_id_type=MESH).wait_recv()

    if kind == "col":
        g_spec = pl.BlockSpec((tr, hc), lambda i, c: ((1 - c[0]) * nt + i, 0))
    else:
        g_spec = pl.BlockSpec((tr, hc), lambda i, c: (i, 1 - c[0]))
    return pl.pallas_call(
        body, name=name,
        grid_spec=pltpu.PrefetchScalarGridSpec(
            num_scalar_prefetch=1, grid=(nt,), in_specs=[g_spec], out_specs=ANY,
            scratch_shapes=[pltpu.VMEM((2, tr, hc), BF16), pltpu.SemaphoreType.DMA((2,)), pltpu.SemaphoreType.DMA]),
        out_shape=jax.ShapeDtypeStruct((hr, hc), BF16),
        compiler_params=_cp("arbitrary"),
    )(c_arr, g)


def _partner_chip(k, p):
    return p ^ jnp.where(k == 0, 2, jnp.where(k == 1, 1, jnp.where(k == 2, 3, 0)))


def pair_add(g, got, kind, cp_arr, name):
    R, C = g.shape
    pr, pc = _piece_shape(kind, R, C)
    tr = _row_tile(pr, pc, itemsize=2, budget=1024 * 1024)
    nt = pr // tr

    def body(cp_ref, g_ref, got_ref, ps_ref, rb_ref):
        tile = (g_ref[...].astype(F32) + got_ref[...].astype(F32)).astype(BF16)
        ps_ref[...] = tile

        @pl.when(pl.program_id(1) == cp_ref[1])
        def _():
            rb_ref[...] = tile

    if kind == "col":
        g_spec = pl.BlockSpec((tr, pc), lambda i, q, cp: (cp[0] * nt + i, q))
        got_spec = pl.BlockSpec((tr, pc), lambda i, q, cp: (i, q))
    else:
        g_spec = pl.BlockSpec((tr, pc), lambda i, q, cp: (q * nt + i, cp[0]))
        got_spec = pl.BlockSpec((tr, pc), lambda i, q, cp: (q * nt + i, 0))
    return pl.pallas_call(
        body, name=name,
        grid_spec=pltpu.PrefetchScalarGridSpec(
            num_scalar_prefetch=1, grid=(nt, 4), in_specs=[g_spec, got_spec],
            out_specs=[pl.BlockSpec((None, tr, pc), lambda i, q, cp: (q, i, 0)),
                       pl.BlockSpec((None, tr, pc), lambda i, q, cp: (cp[1], i, 0))]),
        out_shape=[jax.ShapeDtypeStruct((4, pr, pc), BF16)] * 2,
        compiler_params=_cp("arbitrary", "arbitrary"),
    )(cp_arr, g, got)


def _rs_copies(ps, rb, ssem, rsem, mx, my, mc):
    p = 2 * mx + my
    out = []
    for w in range(len(ps)):
        for k, chip in enumerate(_other_chips(mx, my)):
            out.append(pltpu.make_async_remote_copy(
                src_ref=ps[w].at[2 * chip[0] + chip[1]], dst_ref=rb[w].at[p], send_sem=ssem.at[3 * w + k],
                recv_sem=rsem.at[3 * w + k], device_id=(*chip, mc), device_id_type=MESH))
    return out


def rs_start(ps, rb, after, name):
    n = len(ps)
    after = list(after)
    m = len(after)

    def body(*refs):
        ssem, rsem = refs[2 * n + m:2 * n + m + 2]
        ps_o = refs[2 * n + m + 2:3 * n + m + 2]
        rb_o = refs[3 * n + m + 2:4 * n + m + 2]
        token = refs[4 * n + m + 2]
        for cp in _rs_copies(ps_o, rb_o, ssem, rsem, *_mesh_pos()):
            cp.start()
        token[...] = jnp.zeros_like(token)

    both = list(ps) + list(rb)
    res = pl.pallas_call(
        body, name=name,
        out_shape=[pltpu.SemaphoreType.DMA((3 * n,)), pltpu.SemaphoreType.DMA((3 * n,))]
        + [pltpu.HBM(a.shape, a.dtype) for a in both] + [jax.ShapeDtypeStruct((8, 128), F32)],
        in_specs=[HBM_SPEC] * (2 * n) + [ANY] * m,
        out_specs=[SEM_SPEC, SEM_SPEC] + [HBM_SPEC] * (2 * n) + [pl.BlockSpec(memory_space=pltpu.VMEM)],
        input_output_aliases={w: 2 + w for w in range(2 * n)},
        compiler_params=pltpu.CompilerParams(has_side_effects=pltpu.SideEffectType.DATAFLOW_SIDE_EFFECTING),
    )(*[pltpu.with_memory_space_constraint(a, pltpu.HBM) for a in both], *after)
    return res[0], res[1], list(res[2:2 + n]), list(res[2 + n:2 + 2 * n]), res[2 + 2 * n]


def rs_wait(ps, rb, ssem, rsem, after, name):
    n = len(ps)
    after = list(after)
    m = len(after)

    def body(*refs):
        ps_i, rb_i = refs[:n], refs[n:2 * n]
        ssem_ref, rsem_ref = refs[2 * n], refs[2 * n + 1]
        for cp in _rs_copies(ps_i, rb_i, ssem_ref, rsem_ref, *_mesh_pos()):
            cp.wait_send()
            cp.wait_recv()

    both = list(ps) + list(rb)
    res = pl.pallas_call(
        body, name=name,
        out_shape=[pltpu.HBM(a.shape, a.dtype) for a in both],
        in_specs=[HBM_SPEC] * (2 * n) + [SEM_SPEC, SEM_SPEC] + [ANY] * m,
        out_specs=[HBM_SPEC] * (2 * n),
        input_output_aliases={w: w for w in range(2 * n)},
        compiler_params=pltpu.CompilerParams(has_side_effects=pltpu.SideEffectType.DATAFLOW_SIDE_EFFECTING),
    )(*both, ssem, rsem, *after)
    return list(res[n:])


def sum_share(parts, kind, R, C, name):
    _, pr, pc = parts.shape
    sr, sc = _shard_shape(kind, R, C)
    tr = _row_tile(pr, pc * 4, budget=4 * 1024 * 1024)
    nt = pr // tr

    def body(p_ref, fin_ref, stage, lsem, ssem, rsem):
        i = pl.program_id(0)
        slot = i % 2
        mx, my, mc = _mesh_pos()

        def region(h, t):
            r0 = pl.multiple_of(t * tr, 8)
            if kind == "col":
                return fin_ref.at[pl.ds(pl.multiple_of(h * pr + r0, 8), tr)]
            return fin_ref.at[pl.ds(r0, tr), pl.ds(h * pc, pc)]

        def copies(s, h, t):
            return (pltpu.make_async_copy(stage.at[s], region(h, t), lsem.at[s]),
                    pltpu.make_async_remote_copy(src_ref=stage.at[s], dst_ref=region(h, t), send_sem=ssem.at[s],
                                                 recv_sem=rsem, device_id=(mx, my, 1 - mc), device_id_type=MESH))

        def wait_sent(s):
            loc, rem = copies(s, 0, 0)
            loc.wait()
            rem.wait_send()

        @pl.when(i >= 2)
        def _():
            wait_sent(slot)

        acc = p_ref[0].astype(F32)
        for k in range(1, 4):
            acc = acc + p_ref[k].astype(F32)
        stage[slot] = acc
        if kind == "col":
            for cp in copies(slot, mc, i):
                cp.start()
        else:
            for h in range(2):
                @pl.when(mc == h)
                def _(h=h):
                    for cp in copies(slot, h, i):
                        cp.start()

        @pl.when(i == nt - 1)
        def _():
            wait_sent(slot)
            if nt >= 2:
                wait_sent(1 - slot)
            half = fin_ref.at[pl.ds(0, pr), pl.ds(0, pc)]
            pltpu.make_async_remote_copy(src_ref=half, dst_ref=half, send_sem=ssem.at[0], recv_sem=rsem,
                                         device_id=(mx, my, 1 - mc), device_id_type=MESH).wait_recv()

    return pl.pallas_call(
        body, name=name, grid=(nt,),
        in_specs=[pl.BlockSpec((4, tr, pc), lambda i: (0, i, 0))],
        out_specs=ANY,
        out_shape=jax.ShapeDtypeStruct((sr, sc), F32),
        scratch_shapes=[pltpu.VMEM((2, tr, pc), F32), pltpu.SemaphoreType.DMA((2,)), pltpu.SemaphoreType.DMA((2,)),
                        pltpu.SemaphoreType.DMA],
        compiler_params=_cp("arbitrary"),
    )(parts)


def _pack(parts, rows):
    flat = []
    for a in parts:
        a = jnp.ravel(a).astype(F32)
        flat.append(jnp.pad(a, (0, (-a.shape[0]) % 128)))
    v = jnp.concatenate(flat)
    return jnp.pad(v, (0, rows * 128 - v.shape[0])).reshape(rows, 128)


def _unpack(block, shapes):
    lead = block.shape[:-2]
    v = block.reshape(lead + (-1,))
    out, off = [], 0
    for shp in shapes:
        n = int(np.prod(shp))
        out.append(v[..., off:off + n].reshape(lead + tuple(shp)))
        off += n + (-n) % 128
    return out


def _block_diag4(w):
    w4 = w.reshape(4, 4, 64, 64)
    eye = jnp.eye(4, dtype=w.dtype)
    return (w4[:, :, :, None, :] * eye[None, :, None, :, None]).reshape(4, 256, 256)


def _diag_blocks(bd):
    b5 = bd.reshape(4, 4, 64, 4, 64)
    return jnp.stack([b5[:, i, :, i, :] for i in range(4)], axis=1).reshape(16, 64, 64)


def _bias_window(rel_bias):
    m = (np.arange(768) + 127) % 768 - 127
    w = rel_bias[:, np.clip(512 - m, -128, 128) + 128]
    win = jnp.tile(w, (1, 128))[:, :128 * 767].reshape(8, 128, 767)[:, :, :WIN]
    qh = np.arange(128)[:, None] // CHUNK
    kc = np.arange(WIN)[None, :] // CHUNK
    valid = (kc >= qh) & (kc <= qh + 8)
    return jnp.where(jnp.asarray(valid)[None], win, NEG)


SMALL = ("b_ada", "norm_pre", "norm_post", "rel_bias", "conv_w", "conv_b", "lru_wa", "lru_ba", "lru_wx",
         "lru_bx", "lru_lambda")
WEIGHTS = ("w_ada", "b_ada", "norm_pre", "norm_post", "ffn1_w_gu", "ffn1_w_down", "w_in", "rel_bias", "conv_w",
           "conv_b", "lru_wa", "lru_ba", "lru_wx", "lru_bx", "lru_lambda", "w_att_o", "w_rec_o", "w_out",
           "ffn2_w_gu", "ffn2_w_down")


def kernel(x, c, w_ada, b_ada, norm_pre, norm_post, ffn1_w_gu, ffn1_w_down, w_in, rel_bias, conv_w, conv_b, lru_wa, lru_ba, lru_wx, lru_bx, lru_lambda, w_att_o, w_rec_o, w_out, ffn2_w_gu, ffn2_w_down, loss_target, m_w_ada, m_b_ada, m_norm_pre, m_norm_post, m_ffn1_w_gu, m_ffn1_w_down, m_w_in, m_rel_bias, m_conv_w, m_conv_b, m_lru_wa, m_lru_ba, m_lru_wx, m_lru_bx, m_lru_lambda, m_w_att_o, m_w_rec_o, m_w_out, m_ffn2_w_gu, m_ffn2_w_down, v_w_ada, v_b_ada, v_norm_pre, v_norm_post, v_ffn1_w_gu, v_ffn1_w_down, v_w_in, v_rel_bias, v_conv_w, v_conv_b, v_lru_wa, v_lru_ba, v_lru_wx, v_lru_bx, v_lru_lambda, v_w_att_o, v_w_rec_o, v_w_out, v_ffn2_w_gu, v_ffn2_w_down):
    W = dict(w_ada=w_ada, b_ada=b_ada, norm_pre=norm_pre, norm_post=norm_post, ffn1_w_gu=ffn1_w_gu,
             ffn1_w_down=ffn1_w_down, w_in=w_in, rel_bias=rel_bias, conv_w=conv_w, conv_b=conv_b, lru_wa=lru_wa,
             lru_ba=lru_ba, lru_wx=lru_wx, lru_bx=lru_bx, lru_lambda=lru_lambda, w_att_o=w_att_o, w_rec_o=w_rec_o,
             w_out=w_out, ffn2_w_gu=ffn2_w_gu, ffn2_w_down=ffn2_w_down)
    M = dict(w_ada=m_w_ada, b_ada=m_b_ada, norm_pre=m_norm_pre, norm_post=m_norm_post, ffn1_w_gu=m_ffn1_w_gu,
             ffn1_w_down=m_ffn1_w_down, w_in=m_w_in, rel_bias=m_rel_bias, conv_w=m_conv_w, conv_b=m_conv_b,
             lru_wa=m_lru_wa, lru_ba=m_lru_ba, lru_wx=m_lru_wx, lru_bx=m_lru_bx, lru_lambda=m_lru_lambda,
             w_att_o=m_w_att_o, w_rec_o=m_w_rec_o, w_out=m_w_out, ffn2_w_gu=m_ffn2_w_gu, ffn2_w_down=m_ffn2_w_down)
    V = dict(w_ada=v_w_ada, b_ada=v_b_ada, norm_pre=v_norm_pre, norm_post=v_norm_post, ffn1_w_gu=v_ffn1_w_gu,
             ffn1_w_down=v_ffn1_w_down, w_in=v_w_in, rel_bias=v_rel_bias, conv_w=v_conv_w, conv_b=v_conv_b,
             lru_wa=v_lru_wa, lru_ba=v_lru_ba, lru_wx=v_lru_wx, lru_bx=v_lru_bx, lru_lambda=v_lru_lambda,
             w_att_o=v_w_att_o, w_rec_o=v_w_rec_o, w_out=v_w_out, ffn2_w_gu=v_ffn2_w_gu, ffn2_w_down=v_ffn2_w_down)
    mx, my, mc = _mesh_pos()
    p = 2 * mx + my
    e = 4 * mx + 2 * my + mc
    xs = x[0]

    c_arr = jnp.reshape(mc, (1,)).astype(jnp.int32)
    cp_arr = jnp.stack([mc, p]).astype(jnp.int32)
    p_arr = jnp.reshape(p, (1,)).astype(jnp.int32)
    direct = ("w_att_o", "w_rec_o", "w_out", "ffn2_w_gu", "ffn2_w_down")
    geoms = [(kind, R, C, n in direct) for (n, kind, R, C) in BIG]
    names = [b[0] for b in BIG]
    placed = [ag_local(W[n][0], kind, R, C, p_arr, "ag_local_" + n) for (n, kind, R, C) in BIG[:2]]

    def arrived(fly, lo, hi, ssem, rsem, after, tag):
        done = ag_wait(fly, geoms[lo:hi], ssem, rsem, after, "ag_wait_" + tag)
        return [a if both else ag_forward(a, kind, R, C, "ag_forward_" + n)
                for a, (kind, R, C, both), n in zip(done, geoms[lo:hi], names[lo:hi])]

    g1 = ag_small(_pack([c, norm_pre, norm_post, conv_w], 32), "ag_small_params")
    c_all, npre4, npost4, cw4 = _unpack(g1, [(D,), (3, 256), (3, 256), (4, 256)])
    chipwise = lambda a: jnp.moveaxis(a[0::2], 0, 1).reshape(a.shape[1], D)
    npre, npost, conv_full = chipwise(npre4), chipwise(npost4), chipwise(cw4)

    b_cols = lax.dynamic_slice(b_ada, (0, p * 2304), (1, 2304))
    mod_cols = ada_fwd(c_all, w_ada[0], b_cols, "ada_fwd")
    g2 = ag_small(mod_cols.reshape(144, 128), "ag_mod")
    mod_all = jnp.moveaxis(g2[0::2].reshape(4, 8, 2304), 0, 1).reshape(8, 9 * D)
    mod = lax.dynamic_index_in_dim(mod_all, e, 0, keepdims=False).reshape(3, 3, D)
    zeros3 = jnp.zeros((3, D), F32)
    vecs = [jnp.concatenate([npre[k:k + 1], npost[k:k + 1], mod[k], zeros3], axis=0) for k in range(3)]

    f1_s, f1_r, f1_fly, tok0 = ag_start(placed[:2], geoms[:2], [g2], "ag_start_ffn1")
    placed += [ag_local(W[n][0], kind, R, C, p_arr, "ag_local_" + n, after=[tok0]) for (n, kind, R, C) in BIG[2:]]
    f1_gu, f1_dn = arrived(f1_fly, 0, 2, f1_s, f1_r, placed[7], "ffn1")
    mix_s, mix_r, mix_fly, tok1 = ag_start(placed[2:6], geoms[2:6], [f1_gu, f1_dn], "ag_start_mixer")
    ffn_s, ffn_r, ffn_fly, tok2 = ag_start(placed[6:], geoms[6:], [tok1], "ag_start_ffn2")
    wa_bd = _block_diag4(lru_wa[0]).astype(BF16)
    wx_bd = _block_diag4(lru_wx[0]).astype(BF16)
    pvec = jnp.concatenate([conv_full, conv_b, lru_ba, lru_bx, lru_lambda], axis=0)
    bias = _bias_window(rel_bias[0]).reshape(4, 256, WIN)

    x1, h1, g1_, u1, a1, f1 = ffn_fwd(xs, vecs[0] + tok2[0:1, 0:1], f1_gu, f1_dn, 0.5, "ffn1_fwd")
    win, wao, wro, wout = arrived(mix_fly, 2, 6, mix_s, mix_r, x1, "mixer")
    h2, qkv, rest = proj_fwd(x1, vecs[1], win, "proj_fwd")
    ao = attn_fwd(qkv, bias, "attn_fwd")
    hl, hg = lru_fwd(rest, pvec, wa_bd, wx_bd, "lru_fwd")
    x2, att, rec, mg, f2 = mix_out_fwd(x1, ao, hg, rest, vecs[1], wao, wro, wout, "mix_out_fwd")
    f2_gu, f2_dn = arrived(ffn_fly, 6, 8, ffn_s, ffn_r, x2, "ffn2")
    x3, h3, g3_, u3, a3, f3 = ffn_fwd(x2, vecs[2], f2_gu, f2_dn, 0.5, "ffn2_fwd")
    dy, lvec = loss_grad(x3, loss_target[0], "loss_grad")
    loss = lax.psum(lvec[0, 0], ("x", "y", "c"))

    G, grads = {}, {}
    geo = {n: (kind, R, C) for (n, kind, R, C) in BIG}

    def reduce_begin(names, tag):
        ps, rb = [], []
        for n in names:
            got = pair_push(G[n], geo[n][0], c_arr, "rs_push_" + n)
            a, b = pair_add(G[n], got, geo[n][0], cp_arr, "rs_pair_sum_" + n)
            ps.append(a)
            rb.append(b)
        return rs_start(ps, rb, [], "rs_start_" + tag)

    def reduce_end(names, flight, after, tag):
        ssem, rsem, ps, rb, _ = flight
        for a, n in zip(rs_wait(ps, rb, ssem, rsem, after, "rs_wait_" + tag), names):
            grads[n] = sum_share(a, *geo[n], "rs_sum_share_" + n)[None]

    dx2, df3, dgu3, va2 = ffn_bwd(dy, x2, f3, g3_, u3, vecs[2], f2_gu, f2_gu[:, FF:], f2_dn, 0.5, "ffn2_bwd")
    G["ffn2_w_gu"] = mm_tn(h3, dgu3, "dw_ffn2_gu", D, 1408, 1024)
    G["ffn2_w_down"] = mm_tn(a3, df3, "dw_ffn2_down", 1408, D, 1024)
    fly_ffn2 = reduce_begin(("ffn2_w_gu", "ffn2_w_down"), "ffn2")
    vec1 = vecs[1] + fly_ffn2[4][0:1, 0:1]
    df2, d_att, d_rec, dao, dhl, d3, va_out = mix_out_bwd(dx2, f2, att, rec, rest, hl, vec1, wao, wro, wout,
                                                          "mix_out_bwd")
    G["w_out"] = mm_tn(mg, df2, "dw_out", D, D, 1024)
    G["w_att_o"] = mm_tn(ao, d_att, "dw_att_o", 512, D, 1024)
    G["w_rec_o"] = mm_tn(hg, d_rec, "dw_rec_o", D, D, 1024)
    dq, db, dkv = attn_bwd(qkv, dao, bias, "attn_bwd")
    dxr, v_lru, dwa_bd, dwx_bd = lru_bwd(dhl, hl, rest, pvec, wa_bd, wx_bd, "lru_bwd")
    dx1, va_in = proj_bwd(dq, dkv, dxr, d3, win, x1, dx2, vecs[1], "proj_bwd")
    gin = mm_tn(h2, dq, "dw_in_q", D, 512, 1024, n_total=PW)
    gin = mm_tn(h2, dkv, "dw_in_kv", D, 512, 1024, prev=gin, col_off=512, n_total=PW)
    gin = mm_tn(h2, dxr, "dw_in_xr", D, 512, 1024, prev=gin, col_off=1536, n_total=PW)
    G["w_in"] = mm_tn(h2, d3, "dw_in_gates", D, 512, 1024, prev=gin, col_off=2560, n_total=PW)
    fly_mix = reduce_begin(("w_in", "w_att_o", "w_rec_o", "w_out"), "mixer")
    vec0 = vecs[0] + fly_mix[4][0:1, 0:1]
    dx0, df1, dgu1, va0 = ffn_bwd(dx1, xs, f1, g1_, u1, vec0, f1_gu, f1_gu[:, FF:], f1_dn, 0.5, "ffn1_bwd")
    G["ffn1_w_gu"] = mm_tn(h1, dgu1, "dw_ffn1_gu", D, 1408, 1024)
    G["ffn1_w_down"] = mm_tn(a1, df1, "dw_ffn1_down", 1408, D, 1024)
    fly_ffn1 = reduce_begin(("ffn1_w_gu", "ffn1_w_down"), "ffn1")
    reduce_end(("ffn2_w_gu", "ffn2_w_down"), fly_ffn2, [fly_ffn1[4]], "ffn2")
    reduce_end(("w_in", "w_att_o", "w_rec_o", "w_out"), fly_mix, [fly_ffn1[4], grads["ffn2_w_down"]], "mixer")

    va1 = va_out + va_in
    vas = (va0, va1, va2)
    dmod = jnp.stack([v[2:5] for v in vas])
    part = {"b_ada": dmod, "norm_pre": jnp.stack([v[0] for v in vas]), "norm_post": jnp.stack([v[1] for v in vas]),
            "rel_bias": bias_grad(db.reshape(8, 128, WIN), "bias_grad")[:, :257], "conv_w": v_lru[0:4], "conv_b": v_lru[4],
            "lru_wa": _diag_blocks(dwa_bd), "lru_ba": v_lru[5], "lru_wx": _diag_blocks(dwx_bd), "lru_bx": v_lru[6],
            "lru_lambda": v_lru[7]}
    full_shapes = {"b_ada": (9 * D,), "norm_pre": (3, D), "norm_post": (3, D), "rel_bias": (8, 257),
                   "conv_w": (4, D), "conv_b": (D,), "lru_wa": (16, 64, 64), "lru_ba": (D,),
                   "lru_wx": (16, 64, 64), "lru_bx": (D,), "lru_lambda": (D,)}
    g3 = ag_small(_pack([part[n] for n in SMALL], 1232), "ag_small_grads")
    red = dict(zip(SMALL, _unpack(sum_lead(g3, "sum_small_grads"), [full_shapes[n] for n in SMALL])))
    cols = lambda a: lax.dynamic_slice(a, (0, p * 256), (a.shape[0], 256))
    grads.update({"b_ada": red["b_ada"][None], "norm_pre": cols(red["norm_pre"])[None],
                  "norm_post": cols(red["norm_post"])[None], "rel_bias": red["rel_bias"][None],
                  "conv_w": cols(red["conv_w"])[None], "conv_b": red["conv_b"][None], "lru_wa": red["lru_wa"][None],
                  "lru_ba": red["lru_ba"][None], "lru_wx": red["lru_wx"][None], "lru_bx": red["lru_bx"][None],
                  "lru_lambda": red["lru_lambda"][None]})

    dmod_all = g3[:, :72].reshape(8, 9 * D)
    dmod_cols = jnp.pad(lax.dynamic_slice(dmod_all, (0, p * 2304), (8, 2304)), ((0, 120), (0, 0)))
    c_all_t = jnp.pad(c_all.T, ((0, 0), (0, 120)))
    grads["w_ada"] = ada_bwd(c_all_t, dmod_cols, "ada_bwd")[None]

    delta, new_m, new_v = {}, {}, {}

    def update(n):
        shp = W[n].shape
        d_, m_, v_ = adamw(W[n][0], grads[n][0], M[n][0], V[n][0], "adamw_" + n)
        delta[n], new_m[n], new_v[n] = d_.reshape(shp), m_.reshape(shp), v_.reshape(shp)

    for n in ("w_ada", "ffn2_w_gu", "ffn2_w_down", "w_in", "w_att_o", "w_rec_o", "w_out"):
        update(n)
    packed = [_pack([src[n] for n in SMALL], 1168) for src in (W, grads, M, V)]
    outs = adamw(*packed, "adamw_small")
    for dst, blk in zip((delta, new_m, new_v), outs):
        for n, a in zip(SMALL, _unpack(blk, [W[n].shape for n in SMALL])):
            dst[n] = a
    reduce_end(("ffn1_w_gu", "ffn1_w_down"), fly_ffn1,
               [outs[0], delta["w_ada"], delta["ffn2_w_gu"], delta["ffn2_w_down"], delta["w_in"], delta["w_out"]], "ffn1")
    for n in ("ffn1_w_gu", "ffn1_w_down"):
        update(n)

    return (loss, dx0[None], *[grads[n] for n in WEIGHTS], *[delta[n] for n in WEIGHTS],
            *[new_m[n] for n in WEIGHTS], *[new_v[n] for n in WEIGHTS])
```

```python
import functools

import numpy as np
import jax
import jax.numpy as jnp
from jax import lax
from jax.experimental import pallas as pl
from jax.experimental.pallas import tpu as pltpu

F32 = jnp.float32
BF16 = jnp.bfloat16

D = 1024
FF = 2816
PW = 5632
HP = 128
CHUNK = 64
WIN = 640
TQ = 512
EPS = 1e-6
NEG = -1e30
LRU_C = 8.0
N_DEV = 8
VMEM_LIMIT = 56 * 1024 * 1024

ADAM_LR, ADAM_B1, ADAM_B2, ADAM_EPS, ADAM_WD, ADAM_STEP = 0.001, 0.9, 0.999, 1e-08, 0.01, 10

MESH = pl.DeviceIdType.MESH
ANY = pl.BlockSpec(memory_space=pl.ANY)


def _cp(*sem):
    return pltpu.CompilerParams(dimension_semantics=tuple(sem), vmem_limit_bytes=VMEM_LIMIT)


def _dot(a, b):
    return jnp.dot(a, b, preferred_element_type=F32)


def _dot_nt(a, b):
    return lax.dot_general(a, b, (((1,), (1,)), ((), ())), preferred_element_type=F32)


def _dot_tn(a, b):
    return lax.dot_general(a, b, (((0,), (0,)), ((), ())), preferred_element_type=F32)


def _mean(v):
    return jnp.mean(v, axis=-1, keepdims=True)


def _colsum(v):
    return jnp.sum(v, axis=0, keepdims=True)


def _sigmoid(v):
    return 0.5 * jnp.tanh(0.5 * v) + 0.5


_GK = 0.7978845608028654


def _gelu(v):
    t = jnp.tanh(_GK * (v + 0.044715 * v * v * v))
    return 0.5 * v * (1.0 + t)


def _pre_norm(xv, vec_ref):
    r = lax.rsqrt(_mean(xv * xv) + EPS)
    n = xv * r * vec_ref[0:1, :]
    return n * (1.0 + vec_ref[3:4, :]) + vec_ref[2:3, :]


def _pre_norm_bwd(dh, xv, dres, vec_ref, vacc_ref):
    r = lax.rsqrt(_mean(xv * xv) + EPS)
    xh = xv * r
    n = xh * vec_ref[0:1, :]
    vacc_ref[2:3, :] += _colsum(dh)
    vacc_ref[3:4, :] += _colsum(dh * n)
    dn = dh * (1.0 + vec_ref[3:4, :])
    vacc_ref[0:1, :] += _colsum(dn * xh)
    dxh = dn * vec_ref[0:1, :]
    return r * (dxh - xh * _mean(dxh * xh)) + dres


def _post_norm_bwd(dxo, fv, res, vec_ref, vacc_ref):
    rf = lax.rsqrt(_mean(fv * fv) + EPS)
    fh = fv * rf
    gp = vec_ref[1:2, :]
    vacc_ref[4:5, :] += _colsum(res * dxo * (fh * gp))
    dy = (res * vec_ref[4:5, :]) * dxo
    vacc_ref[1:2, :] += _colsum(dy * fh)
    dfn = dy * gp
    return rf * (dfn - fh * _mean(dfn * fh))


def ffn_fwd(x, vec, w_gu, w_u, w_dn, res, name, tgt=None, tm=1024, tf=512):
    S = x.shape[0]
    tm = min(tm, S)
    nt = S // tm
    nf = -(-FF // tf)
    tail = FF - tf * (nf - 1)
    head = tgt is not None

    def body(*refs):
        x_ref, vec_ref, wg_ref, wu_ref, wd_ref = refs[:5]
        if head:
            t_ref, xo_ref, h_ref, g_ref, u_ref, a_ref, f_ref, l_ref, hs, acc, lacc = refs[5:]
        else:
            xo_ref, h_ref, g_ref, u_ref, a_ref, f_ref, hs, acc = refs[5:]
        i, j = pl.program_id(0), pl.program_id(1)

        @pl.when(j == 0)
        def _():
            h = _pre_norm(x_ref[...], vec_ref).astype(BF16)
            hs[...] = h
            h_ref[...] = h
            acc[...] = jnp.zeros_like(acc)

        def chunk(w):
            h = hs[...]
            g = _dot(h, wg_ref[:, 0:w])
            u = _dot(h, wu_ref[:, 0:w])
            g_ref[:, 0:w] = g.astype(BF16)
            u_ref[:, 0:w] = u.astype(BF16)
            a = (g * _sigmoid(g) * u).astype(BF16)
            a_ref[:, 0:w] = a
            acc[...] += _dot(a, wd_ref[0:w, :])

        @pl.when(j < nf - 1)
        def _():
            chunk(tf)

        @pl.when(j == nf - 1)
        def _():
            chunk(tail)
            f = acc[...]
            f_ref[...] = f
            y = f * lax.rsqrt(_mean(f * f) + EPS) * vec_ref[1:2, :]
            xo = x_ref[...] + (res * vec_ref[4:5, :]) * y
            if head:
                @pl.when(i == 0)
                def _():
                    lacc[...] = jnp.zeros_like(lacc)

                d = xo - t_ref[...]
                xo_ref[...] = d * (1.0 / D)
                lacc[...] += _colsum(d * d)

                @pl.when(i == nt - 1)
                def _():
                    l_ref[...] = jnp.broadcast_to(0.5 * jnp.sum(lacc[...]) * (1.0 / D), (8, 128))
            else:
                xo_ref[...] = xo

    row = lambda i, j: (i, 0)
    col = lambda i, j: (i, j)
    once = dict(pipeline_mode=pl.Buffered(1)) if head else {}
    in_specs = [pl.BlockSpec((tm, D), row, **once), pl.BlockSpec((8, D), lambda i, j: (0, 0)),
                pl.BlockSpec((D, tf), lambda i, j: (0, j)), pl.BlockSpec((D, tf), lambda i, j: (0, j)),
                pl.BlockSpec((tf, D), lambda i, j: (j, 0))]
    out_specs = [pl.BlockSpec((tm, D), row), pl.BlockSpec((tm, D), row), pl.BlockSpec((tm, tf), col),
                 pl.BlockSpec((tm, tf), col), pl.BlockSpec((tm, tf), col), pl.BlockSpec((tm, D), row)]
    out_shape = [jax.ShapeDtypeStruct((S, D), F32), jax.ShapeDtypeStruct((S, D), BF16),
                 jax.ShapeDtypeStruct((S, FF), BF16), jax.ShapeDtypeStruct((S, FF), BF16),
                 jax.ShapeDtypeStruct((S, FF), BF16), jax.ShapeDtypeStruct((S, D), F32)]
    scratch = [pltpu.VMEM((tm, D), BF16), pltpu.VMEM((tm, D), F32)]
    args = [x, vec, w_gu, w_u, w_dn]
    if head:
        in_specs.append(pl.BlockSpec((tm, D), row, **once))
        out_specs.append(pl.BlockSpec((8, 128), lambda i, j: (0, 0)))
        out_shape.append(jax.ShapeDtypeStruct((8, 128), F32))
        scratch.append(pltpu.VMEM((1, D), F32))
        args.append(tgt)
    return pl.pallas_call(
        body, name=name, grid=(nt, nf), in_specs=in_specs, out_specs=out_specs, out_shape=out_shape,
        scratch_shapes=scratch,
        compiler_params=_cp("arbitrary" if head else "parallel", "arbitrary"),
    )(*args)


def ffn_bwd(dxo, x, f, g, u, vec, w_gu, w_u, w_dn, res, name, tm=512, tf=512):
    S = x.shape[0]
    tm = min(tm, S)
    nf = -(-FF // tf)
    tail = FF - tf * (nf - 1)

    def body(dxo_ref, x_ref, f_ref, g_ref, u_ref, vec_ref, wg_ref, wu_ref, wd_ref,
             dx_ref, df_ref, dgu_ref, vacc_ref, dfs, acc):
        i, j = pl.program_id(0), pl.program_id(1)

        @pl.when((i == 0) & (j == 0))
        def _():
            vacc_ref[...] = jnp.zeros_like(vacc_ref)

        @pl.when(j == 0)
        def _():
            df = _post_norm_bwd(dxo_ref[...], f_ref[...], res, vec_ref, vacc_ref).astype(BF16)
            dfs[...] = df
            df_ref[...] = df
            acc[...] = jnp.zeros_like(acc)

        def chunk(w):
            da = _dot_nt(dfs[...], wd_ref[0:w, :])
            gv, uv = g_ref[:, 0:w].astype(F32), u_ref[:, 0:w].astype(F32)
            sg = _sigmoid(gv)
            dg = (da * uv * (sg * (1.0 + gv * (1.0 - sg)))).astype(BF16)
            du = (da * (gv * sg)).astype(BF16)
            dgu_ref[0, :, 0:w] = dg
            dgu_ref[1, :, 0:w] = du
            acc[...] += _dot_nt(dg, wg_ref[:, 0:w]) + _dot_nt(du, wu_ref[:, 0:w])

        @pl.when(j < nf - 1)
        def _():
            chunk(tf)

        @pl.when(j == nf - 1)
        def _():
            chunk(tail)
            dx_ref[...] = _pre_norm_bwd(acc[...], x_ref[...], dxo_ref[...], vec_ref, vacc_ref)

    row = lambda i, j: (i, 0)
    col = lambda i, j: (i, j)
    return pl.pallas_call(
        body, name=name, grid=(S // tm, nf),
        in_specs=[pl.BlockSpec((tm, D), row), pl.BlockSpec((tm, D), row), pl.BlockSpec((tm, D), row),
                  pl.BlockSpec((tm, tf), col), pl.BlockSpec((tm, tf), col),
                  pl.BlockSpec((8, D), lambda i, j: (0, 0)),
                  pl.BlockSpec((D, tf), lambda i, j: (0, j)), pl.BlockSpec((D, tf), lambda i, j: (0, j)),
                  pl.BlockSpec((tf, D), lambda i, j: (j, 0))],
        out_specs=[pl.BlockSpec((tm, D), row), pl.BlockSpec((tm, D), row),
                   pl.BlockSpec((2, tm, tf), lambda i, j: (0, i, j)),
                   pl.BlockSpec((8, D), lambda i, j: (0, 0))],
        out_shape=[jax.ShapeDtypeStruct((S, D), F32), jax.ShapeDtypeStruct((S, D), BF16),
                   jax.ShapeDtypeStruct((2, S, FF), BF16), jax.ShapeDtypeStruct((8, D), F32)],
        scratch_shapes=[pltpu.VMEM((tm, D), BF16), pltpu.VMEM((tm, D), F32)],
        compiler_params=_cp("arbitrary", "arbitrary"),
    )(dxo, x, f, g, u, vec, w_gu, w_u, w_dn)


def mm_tn(a, b, name, tm, tn, tk, out_dtype=BF16, prev=None, col_off=0, n_total=None):
    S, M = a.shape
    if b.ndim == 3:
        G, _, Nf = b.shape
    else:
        G, Nf = 1, b.shape[1]
    N = G * Nf
    n_total = N if n_total is None else n_total
    tk = min(tk, S)
    nbf = Nf // tn
    nk = S // tk
    ob = col_off // tn

    def body(*refs):
        a_ref, b_ref = refs[0], refs[1]
        o_ref, acc = refs[-2], refs[-1]
        k = pl.program_id(2)

        @pl.when(k == 0)
        def _():
            acc[...] = jnp.zeros_like(acc)

        acc[...] += _dot_tn(a_ref[...], b_ref[...])

        @pl.when(k == nk - 1)
        def _():
            o_ref[...] = acc[...].astype(out_dtype)

    if b.ndim == 3:
        b_spec = pl.BlockSpec((None, tk, tn), lambda i, j, k: (j // nbf, k, j % nbf))
    else:
        b_spec = pl.BlockSpec((tk, tn), lambda i, j, k: (k, j))
    in_specs = [pl.BlockSpec((tk, tm), lambda i, j, k: (k, i)), b_spec]
    args = [a, b]
    aliases = {}
    if prev is not None:
        in_specs.append(ANY)
        args.append(prev)
        aliases = {2: 0}
    return pl.pallas_call(
        body, name=name, grid=(M // tm, N // tn, nk),
        in_specs=in_specs,
        out_specs=pl.BlockSpec((tm, tn), lambda i, j, k: (i, j + ob)),
        out_shape=jax.ShapeDtypeStruct((M, n_total), out_dtype),
        scratch_shapes=[pltpu.VMEM((tm, tn), F32)],
        input_output_aliases=aliases,
        compiler_params=_cp("parallel", "parallel", "arbitrary"),
    )(*args)


def proj_fwd(x, vec, w_in, name, tm=1024, tn=512):
    S = x.shape[0]
    tm = min(tm, S)
    nq = 1536 // tn

    def body(x_ref, vec_ref, w_ref, h_ref, qkv_ref, rest_ref, hs):
        j = pl.program_id(1)

        @pl.when(j == 0)
        def _():
            h = _pre_norm(x_ref[...], vec_ref).astype(BF16)
            hs[...] = h
            h_ref[...] = h

        r = _dot(hs[...], w_ref[...])

        @pl.when(j < nq)
        def _():
            qkv_ref[...] = r.astype(BF16)

        @pl.when(j >= nq)
        def _():
            rest_ref[...] = r

    row = lambda i, j: (i, 0)
    return pl.pallas_call(
        body, name=name, grid=(S // tm, PW // tn),
        in_specs=[pl.BlockSpec((tm, D), row), pl.BlockSpec((8, D), lambda i, j: (0, 0)),
                  pl.BlockSpec((D, tn), lambda i, j: (0, j))],
        out_specs=[pl.BlockSpec((tm, D), row),
                   pl.BlockSpec((tm, tn), lambda i, j: (i, jnp.minimum(j, nq - 1))),
                   pl.BlockSpec((tm, tn), lambda i, j: (i, jnp.maximum(j - nq, 0)))],
        out_shape=[jax.ShapeDtypeStruct((S, D), BF16), jax.ShapeDtypeStruct((S, 1536), BF16),
                   jax.ShapeDtypeStruct((S, 4096), F32)],
        scratch_shapes=[pltpu.VMEM((tm, D), BF16)],
        compiler_params=_cp("parallel", "arbitrary"),
    )(x, vec, w_in)


def proj_bwd(dq, dkv, dxr, d3, w_in, x, dxo, vec, name, tm=1024, tk=512):
    S = x.shape[0]
    tm = min(tm, S)
    nk = PW // tk

    def body(dq_ref, dkv_ref, dxr_ref, d3_ref, w_ref, x_ref, dxo_ref, vec_ref, dx_ref, vacc_ref, acc):
        i, j = pl.program_id(0), pl.program_id(1)

        @pl.when((i == 0) & (j == 0))
        def _():
            vacc_ref[...] = jnp.zeros_like(vacc_ref)

        @pl.when(j == 0)
        def _():
            acc[...] = _dot_nt(dq_ref[...], w_ref[...])

        @pl.when((j >= 1) & (j < 3))
        def _():
            acc[...] += _dot_nt(dkv_ref[...], w_ref[...])

        @pl.when((j >= 3) & (j < 5))
        def _():
            acc[...] += _dot_nt(dxr_ref[...], w_ref[...])

        @pl.when(j >= 5)
        def _():
            acc[...] += _dot_nt(d3_ref[...], w_ref[...])

        @pl.when(j == nk - 1)
        def _():
            dx_ref[...] = _pre_norm_bwd(acc[...], x_ref[...], dxo_ref[...], vec_ref, vacc_ref)

    row = lambda i, j: (i, 0)
    return pl.pallas_call(
        body, name=name, grid=(S // tm, nk),
        in_specs=[pl.BlockSpec((None, tm, tk), lambda i, j: (0, i, 0)),
                  pl.BlockSpec((None, tm, tk), lambda i, j: (jnp.clip(j - 1, 0, 1), i, 0)),
                  pl.BlockSpec((tm, tk), lambda i, j: (i, jnp.clip(j - 3, 0, 1))),
                  pl.BlockSpec((None, tm, tk), lambda i, j: (jnp.clip(j - 5, 0, 5) // 2, i, jnp.clip(j - 5, 0, 5) % 2)),
                  pl.BlockSpec((D, tk), lambda i, j: (0, j)),
                  pl.BlockSpec((tm, D), row), pl.BlockSpec((tm, D), row),
                  pl.BlockSpec((8, D), lambda i, j: (0, 0))],
        out_specs=[pl.BlockSpec((tm, D), row), pl.BlockSpec((8, D), lambda i, j: (0, 0))],
        out_shape=[jax.ShapeDtypeStruct((S, D), F32), jax.ShapeDtypeStruct((8, D), F32)],
        scratch_shapes=[pltpu.VMEM((tm, D), F32)],
        compiler_params=_cp("arbitrary", "arbitrary"),
    )(dq, dkv, dxr, d3, w_in, x, dxo, vec)


def _two_heads(v, lane):
    zero = jnp.zeros((), v.dtype)
    return jnp.concatenate([jnp.where(lane < 64, v, zero), jnp.where(lane >= 64, v, zero)], axis=0)


def _attn_probs(qm, ka, bias_h, i, grp):
    s = _dot_nt(qm, ka) + bias_h
    col = lax.broadcasted_iota(jnp.int32, s.shape, 1)
    first_key = jnp.where(i == 0, 512 - 128 * grp, 0)
    s = jnp.where(col >= first_key, s, NEG)
    e = jnp.exp(s - jnp.max(s, axis=-1, keepdims=True))
    return e * (1.0 / jnp.sum(e, axis=-1, keepdims=True))


def attn_fwd(qkv, bias, name):
    S = qkv.shape[0]
    nb = S // TQ

    def body(q_ref, kp_ref, kc_ref, vp_ref, vc_ref, b_ref, o_ref, kw, vw):
        i = pl.program_id(1)
        kw[0:TQ, :] = kp_ref[...]
        kw[TQ:2 * TQ, :] = kc_ref[...]
        vw[0:TQ, :] = vp_ref[...]
        vw[TQ:2 * TQ, :] = vc_ref[...]
        lane = lax.broadcasted_iota(jnp.int32, (1, HP), 1)

        def group(a, carry):
            r0 = pl.multiple_of(a * 128, 128)
            qa = q_ref[pl.ds(r0, 128), :] * jnp.asarray(0.125, BF16)
            ka = kw[pl.ds(r0, WIN), :]
            va = vw[pl.ds(r0, WIN), :]
            p = _attn_probs(_two_heads(qa, lane), ka, b_ref[...], i, a)
            o2 = _dot(p.astype(BF16), va)
            o_ref[pl.ds(r0, 128), :] = jnp.where(lane < 64, o2[0:128], o2[128:256]).astype(BF16)
            return carry

        lax.fori_loop(0, TQ // 128, group, 0, unroll=True)

    prev = lambda h, i: (jnp.maximum(i - 1, 0), 0)
    return pl.pallas_call(
        body, name=name, grid=(4, nb),
        in_specs=[pl.BlockSpec((TQ, HP), lambda h, i: (i, h)),
                  pl.BlockSpec((TQ, HP), lambda h, i: (jnp.maximum(i - 1, 0), 4 + h)),
                  pl.BlockSpec((TQ, HP), lambda h, i: (i, 4 + h)),
                  pl.BlockSpec((TQ, HP), lambda h, i: (jnp.maximum(i - 1, 0), 8 + h)),
                  pl.BlockSpec((TQ, HP), lambda h, i: (i, 8 + h)),
                  pl.BlockSpec((None, 256, WIN), lambda h, i: (h, 0, 0))],
        out_specs=pl.BlockSpec((TQ, HP), lambda h, i: (i, h)),
        out_shape=jax.ShapeDtypeStruct((S, 512), BF16),
        scratch_shapes=[pltpu.VMEM((2 * TQ, HP), BF16), pltpu.VMEM((2 * TQ, HP), BF16)],
        compiler_params=_cp("parallel", "arbitrary"),
    )(qkv, qkv, qkv, qkv, qkv, bias)


def attn_bwd(qkv, do, bias, name):
    S = qkv.shape[0]
    nb = S // TQ

    def body(q_ref, kp_ref, kc_ref, vp_ref, vc_ref, do_ref, b_ref, dqkv_ref, db_ref, dkv_ref, kw, vw, ak, av):
        i = pl.program_id(1)

        @pl.when(i == 0)
        def _():
            db_ref[...] = jnp.zeros_like(db_ref)
            ak[...] = jnp.zeros_like(ak)
            av[...] = jnp.zeros_like(av)

        @pl.when(i > 0)
        def _():
            ak[0:TQ, :] = ak[TQ:2 * TQ, :]
            av[0:TQ, :] = av[TQ:2 * TQ, :]
            ak[TQ:2 * TQ, :] = jnp.zeros((TQ, HP), F32)
            av[TQ:2 * TQ, :] = jnp.zeros((TQ, HP), F32)

        @pl.when(i < nb)
        def _():
            kw[0:TQ, :] = kp_ref[...]
            kw[TQ:2 * TQ, :] = kc_ref[...]
            vw[0:TQ, :] = vp_ref[...]
            vw[TQ:2 * TQ, :] = vc_ref[...]
            lane = lax.broadcasted_iota(jnp.int32, (1, HP), 1)

            def group(a, carry):
                r0 = pl.multiple_of(a * 128, 128)
                q2 = _two_heads(q_ref[pl.ds(r0, 128), :] * jnp.asarray(0.125, BF16), lane)
                do2 = _two_heads(do_ref[pl.ds(r0, 128), :], lane)
                ka = kw[pl.ds(r0, WIN), :]
                va = vw[pl.ds(r0, WIN), :]
                p = _attn_probs(q2, ka, b_ref[...], i, a)
                dp = _dot_nt(do2, va)
                ds = p * (dp - jnp.sum(p * dp, axis=-1, keepdims=True))
                db_ref[...] += ds
                dsb = ds.astype(BF16)
                dq2 = _dot(dsb, ka)
                ak[pl.ds(r0, WIN), :] += _dot_tn(dsb, q2)
                av[pl.ds(r0, WIN), :] += _dot_tn(p.astype(BF16), do2)
                dq = jnp.where(lane < 64, dq2[0:128], dq2[128:256])
                dqkv_ref[0, pl.ds(r0, 128), :] = (dq * 0.125).astype(BF16)
                return carry

            lax.fori_loop(0, TQ // 128, group, 0, unroll=True)

        @pl.when(i > 0)
        def _():
            dkv_ref[0] = ak[0:TQ, :].astype(BF16)
            dkv_ref[1] = av[0:TQ, :].astype(BF16)

    cur = lambda i: jnp.minimum(i, nb - 1)
    prv = lambda i: jnp.clip(i - 1, 0, nb - 1)
    dq, db, dkv = pl.pallas_call(
        body, name=name, grid=(4, nb + 1),
        in_specs=[pl.BlockSpec((TQ, HP), lambda h, i: (cur(i), h)),
                  pl.BlockSpec((TQ, HP), lambda h, i: (prv(i), 4 + h)),
                  pl.BlockSpec((TQ, HP), lambda h, i: (cur(i), 4 + h)),
                  pl.BlockSpec((TQ, HP), lambda h, i: (prv(i), 8 + h)),
                  pl.BlockSpec((TQ, HP), lambda h, i: (cur(i), 8 + h)),
                  pl.BlockSpec((TQ, HP), lambda h, i: (cur(i), h)),
                  pl.BlockSpec((None, 256, WIN), lambda h, i: (h, 0, 0))],
        out_specs=[pl.BlockSpec((1, TQ, HP), lambda h, i: (0, cur(i), h)),
                   pl.BlockSpec((None, 256, WIN), lambda h, i: (h, 0, 0)),
                   pl.BlockSpec((2, TQ, HP), lambda h, i: (0, prv(i), h))],
        out_shape=[jax.ShapeDtypeStruct((1, S, 512), BF16), jax.ShapeDtypeStruct((4, 256, WIN), F32),
                   jax.ShapeDtypeStruct((2, S, 512), BF16)],
        scratch_shapes=[pltpu.VMEM((2 * TQ, HP), BF16), pltpu.VMEM((2 * TQ, HP), BF16),
                        pltpu.VMEM((2 * TQ, HP), F32), pltpu.VMEM((2 * TQ, HP), F32)],
        compiler_params=_cp("parallel", "arbitrary"),
    )(qkv, qkv, qkv, qkv, qkv, do, bias)
    return dq, db, dkv


def bias_grad(db, name):
    def body(db_ref, o_ref):
        r = lax.broadcasted_iota(jnp.int32, (128, 128), 0)
        c = lax.broadcasted_iota(jnp.int32, (128, 128), 1)
        flip = (r + c == 127).astype(BF16)
        lane = lax.broadcasted_iota(jnp.int32, (16, 384), 1)
        src = lax.broadcasted_iota(jnp.int32, (128, 384), 0)
        dst = lax.broadcasted_iota(jnp.int32, (128, 384), 1)

        def split_dot(v, m):
            hi = v.astype(BF16)
            r1 = v - hi.astype(F32)
            mid = r1.astype(BF16)
            lo = (r1 - mid.astype(F32)).astype(BF16)
            return _dot(hi, m) + _dot(mid, m) + _dot(lo, m)

        def diag_sums(w):
            y = pltpu.roll(split_dot(w, flip), 0, 1, stride=1, stride_axis=0)
            return jnp.broadcast_to(_colsum(y), (16, 128))

        w4 = db_ref[0, :, 512:640]
        w3 = db_ref[0, :, 384:512]
        far = jnp.sum(db_ref[0, :, 0:384]) + jnp.sum(jnp.where(r >= c, w3, 0.0))
        lo4 = diag_sums(jnp.where(r >= c, w4, 0.0))
        up4 = diag_sums(jnp.where(r < c, w4, 0.0))
        up3 = diag_sums(jnp.where(r < c, w3, 0.0))
        p_lo4 = (dst == 128 + (src + 1) % 128).astype(BF16)
        p_up4 = ((dst == src + 1) & (src < 127)).astype(BF16)
        p_up3 = ((dst == src + 129) & (src < 127)).astype(BF16)
        out = split_dot(lo4, p_lo4) + split_dot(up4, p_up4) + split_dot(up3, p_up3)
        o_ref[0] = out + jnp.where(lane == 256, far, 0.0)

    return pl.pallas_call(
        body, name=name, grid=(8,),
        in_specs=[pl.BlockSpec((1, 128, WIN), lambda h: (h, 0, 0))],
        out_specs=pl.BlockSpec((1, 16, 384), lambda h: (h, 0, 0)),
        out_shape=jax.ShapeDtypeStruct((8, 16, 384), F32),
        compiler_params=_cp("parallel"),
    )(db)[:, 0, :]


LT = 256
LC = 512


def _lru_gates(xs, pv_ref, wa_ref, wx_ref, tl):
    xc = (pv_ref[4:5, :] + pv_ref[3:4, :] * xs[pl.ds(8, tl), :] + pv_ref[2:3, :] * xs[pl.ds(7, tl), :]
          + pv_ref[1:2, :] * xs[pl.ds(6, tl), :] + pv_ref[0:1, :] * xs[pl.ds(5, tl), :])
    xcb = xc.astype(BF16)
    pa = jnp.concatenate([_dot(xcb[:, 0:256], wa_ref[0]), _dot(xcb[:, 256:512], wa_ref[1])], axis=1)
    px = jnp.concatenate([_dot(xcb[:, 0:256], wx_ref[0]), _dot(xcb[:, 256:512], wx_ref[1])], axis=1)
    r = _sigmoid(pa + pv_ref[5:6, :])
    ig = _sigmoid(px + pv_ref[6:7, :])
    z = -pv_ref[7:8, :]
    sp = jnp.maximum(z, 0.0) + jnp.log1p(jnp.exp(-jnp.abs(z)))
    log_a = (-LRU_C * r) * sp
    a = jnp.exp(log_a)
    s = jnp.tanh(-log_a) * (1.0 + a * a)
    inv_mult = lax.rsqrt(s)
    mult = jnp.where(s > 0.0, s * inv_mult, 0.0)
    return xc, xcb, r, ig, sp, a, mult, inv_mult


def lru_fwd(rest, pvec, wa, wx, name):
    S = rest.shape[0]
    tl = min(LT, S)
    nt = S // tl

    def body(xr_ref, halo_ref, yr_ref, pv_ref, wa_ref, wx_ref, h_ref, hg_ref, xs, a_s, u_s, h_s, carry):
        ti = pl.program_id(1)

        @pl.when(ti == 0)
        def _():
            carry[...] = jnp.zeros_like(carry)

        xs[0:8, :] = jnp.where(ti > 0, halo_ref[...], 0.0)
        xs[pl.ds(8, tl), :] = xr_ref[...]
        xc, _, _, ig, _, a, mult, _ = _lru_gates(xs, pv_ref, wa_ref, wx_ref, tl)
        a_s[...] = a
        u_s[...] = mult * (ig * xc)
        row = lax.broadcasted_iota(jnp.int32, (8, LC), 0)

        def blk(bi, c):
            o = pl.multiple_of(bi * 8, 8)
            av = a_s[pl.ds(o, 8), :]
            bv = u_s[pl.ds(o, 8), :]
            for d in (1, 2, 4):
                a_sh = pltpu.roll(av, d, 0)
                b_sh = pltpu.roll(bv, d, 0)
                m = row >= d
                bv = jnp.where(m, av * b_sh + bv, bv)
                av = jnp.where(m, av * a_sh, av)
            hv = bv + av * c
            h_s[pl.ds(o, 8), :] = hv
            return hv[7:8, :]

        carry[...] = lax.fori_loop(0, tl // 8, blk, carry[...])
        h = h_s[...]
        h_ref[...] = h
        hg_ref[...] = (h * _gelu(yr_ref[...])).astype(BF16)

    hb = tl // 8
    return pl.pallas_call(
        body, name=name, grid=(2, nt),
        in_specs=[pl.BlockSpec((tl, LC), lambda c, t: (t, c)),
                  pl.BlockSpec((8, LC), lambda c, t: (jnp.maximum(t * hb - 1, 0), c)),
                  pl.BlockSpec((tl, LC), lambda c, t: (t, 2 + c)),
                  pl.BlockSpec((8, LC), lambda c, t: (0, c)),
                  pl.BlockSpec((2, 256, 256), lambda c, t: (c, 0, 0)),
                  pl.BlockSpec((2, 256, 256), lambda c, t: (c, 0, 0))],
        out_specs=[pl.BlockSpec((tl, LC), lambda c, t: (t, c)), pl.BlockSpec((tl, LC), lambda c, t: (t, c))],
        out_shape=[jax.ShapeDtypeStruct((S, D), F32), jax.ShapeDtypeStruct((S, D), BF16)],
        scratch_shapes=[pltpu.VMEM((tl + 8, LC), F32), pltpu.VMEM((tl, LC), F32), pltpu.VMEM((tl, LC), F32),
                        pltpu.VMEM((tl, LC), F32), pltpu.VMEM((1, LC), F32)],
        compiler_params=_cp("parallel", "arbitrary"),
    )(rest, rest, rest, pvec, wa, wx)


def lru_bwd(dh, h, rest, pvec, wa, wx, name):
    S = rest.shape[0]
    tl = min(LT, S)
    nt = S // tl

    def body(dh_ref, h_ref, hhalo_ref, xr_ref, xhalo_ref, pv_ref, wa_ref, wx_ref,
             dxr_ref, vacc_ref, dwa_ref, dwx_ref,
             xs, hs, a_s, ash_s, b_s, lam_s, dxe, anext, lnext, dxnext):
        ti = pl.program_id(1)
        tr = nt - 1 - ti

        @pl.when(ti == 0)
        def _():
            anext[...] = jnp.zeros_like(anext)
            lnext[...] = jnp.zeros_like(lnext)
            dxnext[...] = jnp.zeros_like(dxnext)
            vacc_ref[...] = jnp.zeros_like(vacc_ref)
            dwa_ref[...] = jnp.zeros_like(dwa_ref)
            dwx_ref[...] = jnp.zeros_like(dwx_ref)

        xs[0:8, :] = jnp.where(tr > 0, xhalo_ref[...], 0.0)
        xs[pl.ds(8, tl), :] = xr_ref[...]
        xc, xcb, r, ig, sp, a, mult, inv_mult = _lru_gates(xs, pv_ref, wa_ref, wx_ref, tl)

        a_s[pl.ds(0, tl), :] = a
        a_s[pl.ds(tl, 8), :] = jnp.broadcast_to(anext[...], (8, LC))
        ash_s[...] = a_s[pl.ds(1, tl), :]
        b_s[...] = dh_ref[...]
        row = lax.broadcasted_iota(jnp.int32, (8, LC), 0)

        def blk(k, c):
            o = pl.multiple_of((tl // 8 - 1 - k) * 8, 8)
            av = ash_s[pl.ds(o, 8), :]
            bv = b_s[pl.ds(o, 8), :]
            for d in (1, 2, 4):
                a_sh = pltpu.roll(av, 8 - d, 0)
                b_sh = pltpu.roll(bv, 8 - d, 0)
                m = row < 8 - d
                bv = jnp.where(m, bv + av * b_sh, bv)
                av = jnp.where(m, av * a_sh, av)
            lv = bv + av * c
            lam_s[pl.ds(o, 8), :] = lv
            return lv[0:1, :]

        lnext[...] = lax.fori_loop(0, tl // 8, blk, lnext[...])
        anext[...] = a[0:1, :]
        lam = lam_s[...]

        hs[0:8, :] = jnp.where(tr > 0, hhalo_ref[...], 0.0)
        hs[pl.ds(8, tl), :] = h_ref[...]
        d_a = lam * hs[pl.ds(7, tl), :]
        d_mult = lam * (ig * xc)
        d_ig = lam * mult * xc
        dxc = lam * mult * ig
        d_log_a = d_a * a - d_mult * (a * a) * inv_mult
        d_r = d_log_a * (-LRU_C * sp)
        vacc_ref[7:8, :] += _colsum(d_log_a * (-LRU_C * r)) * (-_sigmoid(-pv_ref[7:8, :]))
        d_pa = d_r * r * (1.0 - r)
        d_px = d_ig * ig * (1.0 - ig)
        vacc_ref[5:6, :] += _colsum(d_pa)
        vacc_ref[6:7, :] += _colsum(d_px)
        dpa = d_pa.astype(BF16)
        dpx = d_px.astype(BF16)
        back = []
        for g in range(2):
            sl = slice(256 * g, 256 * g + 256)
            dwa_ref[g] += _dot_tn(xcb[:, sl], dpa[:, sl])
            dwx_ref[g] += _dot_tn(xcb[:, sl], dpx[:, sl])
            back.append(_dot_nt(dpa[:, sl], wa_ref[g]) + _dot_nt(dpx[:, sl], wx_ref[g]))
        dxc = dxc + jnp.concatenate(back, axis=1)
        vacc_ref[4:5, :] += _colsum(dxc)
        for k in range(4):
            vacc_ref[k:k + 1, :] += _colsum(dxc * xs[pl.ds(5 + k, tl), :])
        dxe[pl.ds(0, tl), :] = dxc
        dxe[pl.ds(tl, 8), :] = dxnext[...]
        dxr = (pv_ref[3:4, :] * dxc + pv_ref[2:3, :] * dxe[pl.ds(1, tl), :]
               + pv_ref[1:2, :] * dxe[pl.ds(2, tl), :] + pv_ref[0:1, :] * dxe[pl.ds(3, tl), :])
        dxr_ref[...] = dxr.astype(BF16)
        dxnext[...] = dxc[0:8, :]

    hb = tl // 8
    rev = lambda t: nt - 1 - t
    halo = lambda t: jnp.maximum(rev(t) * hb - 1, 0)
    big = lambda: pltpu.VMEM((tl + 8, LC), F32)
    til = lambda: pltpu.VMEM((tl, LC), F32)
    return pl.pallas_call(
        body, name=name, grid=(2, nt),
        in_specs=[pl.BlockSpec((tl, LC), lambda c, t: (rev(t), c)),
                  pl.BlockSpec((tl, LC), lambda c, t: (rev(t), c)),
                  pl.BlockSpec((8, LC), lambda c, t: (halo(t), c)),
                  pl.BlockSpec((tl, LC), lambda c, t: (rev(t), c)),
                  pl.BlockSpec((8, LC), lambda c, t: (halo(t), c)),
                  pl.BlockSpec((8, LC), lambda c, t: (0, c)),
                  pl.BlockSpec((2, 256, 256), lambda c, t: (c, 0, 0)),
                  pl.BlockSpec((2, 256, 256), lambda c, t: (c, 0, 0))],
        out_specs=[pl.BlockSpec((tl, LC), lambda c, t: (rev(t), c)),
                   pl.BlockSpec((8, LC), lambda c, t: (0, c)),
                   pl.BlockSpec((2, 256, 256), lambda c, t: (c, 0, 0)),
                   pl.BlockSpec((2, 256, 256), lambda c, t: (c, 0, 0))],
        out_shape=[jax.ShapeDtypeStruct((S, D), BF16), jax.ShapeDtypeStruct((8, D), F32),
                   jax.ShapeDtypeStruct((4, 256, 256), F32), jax.ShapeDtypeStruct((4, 256, 256), F32)],
        scratch_shapes=[big(), big(), big(), til(), til(), til(), big(),
                        pltpu.VMEM((1, LC), F32), pltpu.VMEM((1, LC), F32), pltpu.VMEM((8, LC), F32)],
        compiler_params=_cp("parallel", "arbitrary"),
    )(dh, h, h, rest, rest, pvec, wa, wx)


def mix_out_fwd(x, ao, hg, rest, vec, w_att_o, w_rec_o, w_out, name, tm=256):
    S = x.shape[0]
    tm = min(tm, S)

    def body(x_ref, ao_ref, hg_ref, ga_ref, gr_ref, vec_ref, wa_ref, wr_ref, wo_ref,
             xo_ref, att_ref, rec_ref, mg_ref, f_ref):
        att = _dot(ao_ref[...], wa_ref[...])
        rec = _dot(hg_ref[...], wr_ref[...])
        att_ref[...] = att
        rec_ref[...] = rec
        mg = (_sigmoid(ga_ref[...]) * att + _sigmoid(gr_ref[...]) * rec).astype(BF16)
        mg_ref[...] = mg
        f = _dot(mg, wo_ref[...])
        f_ref[...] = f
        y = f * lax.rsqrt(_mean(f * f) + EPS) * vec_ref[1:2, :]
        xo_ref[...] = x_ref[...] + (1.0 * vec_ref[4:5, :]) * y

    row = lambda i: (i, 0)
    full = lambda r: pl.BlockSpec((r, D), lambda i: (0, 0))
    return pl.pallas_call(
        body, name=name, grid=(S // tm,),
        in_specs=[pl.BlockSpec((tm, D), row), pl.BlockSpec((tm, 512), row), pl.BlockSpec((tm, D), row),
                  pl.BlockSpec((tm, D), lambda i: (i, 2)), pl.BlockSpec((tm, D), lambda i: (i, 3)),
                  full(8), full(512), full(D), full(D)],
        out_specs=[pl.BlockSpec((tm, D), row)] * 5,
        out_shape=[jax.ShapeDtypeStruct((S, D), F32), jax.ShapeDtypeStruct((S, D), F32),
                   jax.ShapeDtypeStruct((S, D), F32), jax.ShapeDtypeStruct((S, D), BF16),
                   jax.ShapeDtypeStruct((S, D), F32)],
        compiler_params=_cp("parallel"),
    )(x, ao, hg, rest, rest, vec, w_att_o, w_rec_o, w_out)


def mix_out_bwd(dxo, f, att, rec, rest, h, vec, w_att_o, w_rec_o, w_out, name, tm=256):
    S = dxo.shape[0]
    tm = min(tm, S)

    def body(dxo_ref, f_ref, att_ref, rec_ref, yr_ref, ga_ref, gr_ref, h_ref, vec_ref, wa_ref, wr_ref, wo_ref,
             df_ref, da_ref, dr_ref, dao_ref, dh_ref, d3_ref, vacc_ref):
        @pl.when(pl.program_id(0) == 0)
        def _():
            vacc_ref[...] = jnp.zeros_like(vacc_ref)

        df = _post_norm_bwd(dxo_ref[...], f_ref[...], 1.0, vec_ref, vacc_ref).astype(BF16)
        df_ref[...] = df
        dm = _dot_nt(df, wo_ref[...])
        sa = _sigmoid(ga_ref[...])
        sr = _sigmoid(gr_ref[...])
        d_att = (dm * sa).astype(BF16)
        d_rec = (dm * sr).astype(BF16)
        da_ref[...] = d_att
        dr_ref[...] = d_rec
        d3_ref[1] = (dm * att_ref[...] * (sa * (1.0 - sa))).astype(BF16)
        d3_ref[2] = (dm * rec_ref[...] * (sr * (1.0 - sr))).astype(BF16)
        dao_ref[...] = _dot_nt(d_att, wa_ref[...]).astype(BF16)
        d_hg = _dot_nt(d_rec, wr_ref[...])
        yr = yr_ref[...]
        t = jnp.tanh(_GK * (yr + 0.044715 * yr * yr * yr))
        dh_ref[...] = d_hg * (0.5 * yr * (1.0 + t))
        gelu_grad = 0.5 * (1.0 + t) + 0.5 * yr * (1.0 - t * t) * _GK * (1.0 + 3.0 * 0.044715 * yr * yr)
        d3_ref[0] = (d_hg * h_ref[...] * gelu_grad).astype(BF16)

    row = lambda i: (i, 0)
    full = lambda r: pl.BlockSpec((r, D), lambda i: (0, 0))
    return pl.pallas_call(
        body, name=name, grid=(S // tm,),
        in_specs=[pl.BlockSpec((tm, D), row)] * 4
        + [pl.BlockSpec((tm, D), lambda i: (i, 1)), pl.BlockSpec((tm, D), lambda i: (i, 2)),
           pl.BlockSpec((tm, D), lambda i: (i, 3)), pl.BlockSpec((tm, D), row),
           full(8), full(512), full(D), full(D)],
        out_specs=[pl.BlockSpec((tm, D), row)] * 3
        + [pl.BlockSpec((tm, 512), row), pl.BlockSpec((tm, D), row),
           pl.BlockSpec((3, tm, D), lambda i: (0, i, 0)), pl.BlockSpec((8, D), lambda i: (0, 0))],
        out_shape=[jax.ShapeDtypeStruct((S, D), BF16)] * 3
        + [jax.ShapeDtypeStruct((S, 512), BF16), jax.ShapeDtypeStruct((S, D), F32),
           jax.ShapeDtypeStruct((3, S, D), BF16), jax.ShapeDtypeStruct((8, D), F32)],
        compiler_params=_cp("arbitrary"),
    )(dxo, f, att, rec, rest, rest, rest, h, vec, w_att_o, w_rec_o, w_out)


def dw_in(h, dq, dkv, dxr, d3, name, tk=1024, tn=512):
    S = h.shape[0]
    tk = min(tk, S)
    nk = S // tk

    def body(h_ref, dq_ref, dkv_ref, dxr_ref, d3_ref, o_ref, acc):
        j, k = pl.program_id(0), pl.program_id(1)

        @pl.when(k == 0)
        def _():
            acc[...] = jnp.zeros_like(acc)

        @pl.when(j == 0)
        def _():
            acc[...] += _dot_tn(h_ref[...], dq_ref[...])

        @pl.when((j >= 1) & (j < 3))
        def _():
            acc[...] += _dot_tn(h_ref[...], dkv_ref[...])

        @pl.when((j >= 3) & (j < 5))
        def _():
            acc[...] += _dot_tn(h_ref[...], dxr_ref[...])

        @pl.when(j >= 5)
        def _():
            acc[...] += _dot_tn(h_ref[...], d3_ref[...])

        @pl.when(k == nk - 1)
        def _():
            o_ref[...] = acc[...].astype(BF16)

    use = lambda j, k, lo, hi: jnp.where((j >= lo) & (j < hi), k, 0)
    g3 = lambda j: jnp.clip(j - 5, 0, 5)
    return pl.pallas_call(
        body, name=name, grid=(PW // tn, nk),
        in_specs=[pl.BlockSpec((tk, D), lambda j, k: (k, 0)),
                  pl.BlockSpec((None, tk, tn), lambda j, k: (0, use(j, k, 0, 1), 0)),
                  pl.BlockSpec((None, tk, tn), lambda j, k: (jnp.clip(j - 1, 0, 1), use(j, k, 1, 3), 0)),
                  pl.BlockSpec((tk, tn), lambda j, k: (use(j, k, 3, 5), jnp.clip(j - 3, 0, 1))),
                  pl.BlockSpec((None, tk, tn), lambda j, k: (g3(j) // 2, use(j, k, 5, 11), g3(j) % 2))],
        out_specs=pl.BlockSpec((D, tn), lambda j, k: (0, j)),
        out_shape=jax.ShapeDtypeStruct((D, PW), BF16),
        scratch_shapes=[pltpu.VMEM((D, tn), F32)],
        compiler_params=_cp("parallel", "arbitrary"),
    )(h, dq, dkv, dxr, d3)


def ada_fwd(c_all, w_ada, b_ada, name, tn=768):
    n = w_ada.shape[1]

    def body(c_ref, w_ref, b_ref, o_ref):
        cv = c_ref[...]
        ca = (cv * _sigmoid(cv)).astype(BF16)
        o_ref[...] = _dot(ca, w_ref[...].astype(BF16)) + b_ref[...]

    return pl.pallas_call(
        body, name=name, grid=(n // tn,),
        in_specs=[pl.BlockSpec((8, D), lambda j: (0, 0)), pl.BlockSpec((D, tn), lambda j: (0, j)),
                  pl.BlockSpec((1, tn), lambda j: (0, j))],
        out_specs=pl.BlockSpec((8, tn), lambda j: (0, j)),
        out_shape=jax.ShapeDtypeStruct((8, n), F32),
        compiler_params=_cp("parallel"),
    )(c_all, w_ada, b_ada)


def ada_bwd(c_all_t, dmod, name, tn=768):
    n = dmod.shape[1]

    def body(c_ref, d_ref, o_ref):
        cv = c_ref[...]
        ca = (cv * _sigmoid(cv)).astype(BF16)
        o_ref[...] = _dot(ca, d_ref[...].astype(BF16))

    return pl.pallas_call(
        body, name=name, grid=(n // tn,),
        in_specs=[pl.BlockSpec((D, 128), lambda j: (0, 0)), pl.BlockSpec((128, tn), lambda j: (0, j))],
        out_specs=pl.BlockSpec((D, tn), lambda j: (0, j)),
        out_shape=jax.ShapeDtypeStruct((D, n), F32),
        compiler_params=_cp("parallel"),
    )(c_all_t, dmod)


def _row_tile(rows, cols, itemsize=4, budget=1536 * 1024):
    best = None
    for t in range(8, rows + 1, 8):
        if rows % t == 0 and t * cols * itemsize <= budget:
            best = t
    return rows if best is None else best


def sum_lead(parts, name, out_dtype=F32):
    n, R, C = parts.shape
    tr = _row_tile(R, C * n)

    def body(p_ref, o_ref):
        acc = p_ref[0].astype(F32)
        for k in range(1, n):
            acc = acc + p_ref[k].astype(F32)
        o_ref[...] = acc.astype(out_dtype)

    return pl.pallas_call(
        body, name=name, grid=(R // tr,),
        in_specs=[pl.BlockSpec((n, tr, C), lambda i: (0, i, 0))],
        out_specs=pl.BlockSpec((tr, C), lambda i: (i, 0)),
        out_shape=jax.ShapeDtypeStruct((R, C), out_dtype),
        compiler_params=_cp("parallel"),
    )(parts)


def adamw(w, g, m, v, name):
    R, C = w.shape
    tr = _row_tile(R, C * 7, budget=8 * 1024 * 1024)

    def body(w_ref, g_ref, m_ref, v_ref, d_ref, mo_ref, vo_ref):
        gv = g_ref[...]
        mn = ADAM_B1 * m_ref[...] + (1.0 - ADAM_B1) * gv
        vn = ADAM_B2 * v_ref[...] + (1.0 - ADAM_B2) * (gv * gv)
        m_hat = mn / (1.0 - ADAM_B1 ** ADAM_STEP)
        v_hat = vn / (1.0 - ADAM_B2 ** ADAM_STEP)
        d_ref[...] = -ADAM_LR * (m_hat / (jnp.sqrt(v_hat) + ADAM_EPS) + ADAM_WD * w_ref[...])
        mo_ref[...] = mn
        vo_ref[...] = vn

    spec = pl.BlockSpec((tr, C), lambda i: (i, 0))
    return pl.pallas_call(
        body, name=name, grid=(R // tr,),
        in_specs=[spec] * 4, out_specs=[spec] * 3,
        out_shape=[jax.ShapeDtypeStruct((R, C), F32)] * 3,
        compiler_params=_cp("parallel"),
    )(w, g, m, v)


def _mesh_pos():
    return lax.axis_index("x"), lax.axis_index("y"), lax.axis_index("c")


def _other_chips(mx, my):
    return [(1 - mx, my), (mx, 1 - my), (1 - mx, 1 - my)]


def ag_small(x, name):
    R = x.shape[0]

    def body(x_ref, out_ref, send_sems, recv_sems, local_sem):
        mx, my, mc = _mesh_pos()
        me, sibling = (mx, my, mc), (mx, my, 1 - mc)
        chips = _other_chips(mx, my)

        def slot(px, py, pc):
            return out_ref.at[4 * px + 2 * py + pc]

        def copy(k, block, to, src=None):
            return pltpu.make_async_remote_copy(
                src_ref=slot(*block) if src is None else src, dst_ref=slot(*block),
                send_sem=send_sems.at[k], recv_sem=recv_sems.at[k], device_id=to, device_id_type=MESH)

        mine = pltpu.make_async_copy(x_ref, slot(*me), local_sem)
        mine.start()
        first = [copy(0, me, sibling, src=x_ref)]
        first += [copy(1 + j, me, (*chip, mc), src=x_ref) for j, chip in enumerate(chips)]
        for cp in first:
            cp.start()
        passed = [copy(4 + j, (*chip, mc), sibling) for j, chip in enumerate(chips)]
        for j, chip in enumerate(chips):
            copy(1 + j, (*chip, mc), me).wait_recv()
            passed[j].start()
        copy(0, sibling, me).wait_recv()
        for j, chip in enumerate(chips):
            copy(4 + j, (*chip, 1 - mc), me).wait_recv()
        for cp in first + passed:
            cp.wait_send()
        mine.wait()

    return pl.pallas_call(
        body, name=name,
        out_shape=jax.ShapeDtypeStruct((N_DEV, R, 128), F32),
        in_specs=[pl.BlockSpec(memory_space=pltpu.VMEM)],
        out_specs=pl.BlockSpec(memory_space=pltpu.VMEM),
        scratch_shapes=[pltpu.SemaphoreType.DMA((7,)), pltpu.SemaphoreType.DMA((7,)), pltpu.SemaphoreType.DMA],
        compiler_params=pltpu.CompilerParams(vmem_limit_bytes=VMEM_LIMIT),
    )(x)


BIG = (("ffn1_w_gu", "col", D, PW), ("ffn1_w_down", "row", FF, D), ("w_in", "col", D, PW),
       ("w_att_o", "col", 512, D), ("w_rec_o", "row", D, D), ("w_out", "row", D, D),
       ("ffn2_w_gu", "col", D, PW), ("ffn2_w_down", "row", FF, D))
NBIG = len(BIG)


def _shard_shape(kind, R, C):
    return (R, C // 4) if kind == "col" else (R // 4, C)


def _region(ref, kind, R, C, q, half, t, tr):
    sr, sc = _shard_shape(kind, R, C)
    if kind == "col":
        return ref.at[pl.ds(pl.multiple_of(half * (R // 2) + t * tr, 16), tr), pl.ds(q * sc, sc)]
    return ref.at[pl.ds(pl.multiple_of(q * sr + t * tr, 16), tr), pl.ds(half * (C // 2), C // 2)]


def ag_local(w, kind, R, C, p_arr, name, after=()):
    sr, sc = _shard_shape(kind, R, C)
    tr = _row_tile(sr, sc, budget=2 * 1024 * 1024)
    nt = sr // tr
    after = list(after)

    def body(p_ref, w_ref, *rest):
        rest[-1][...] = w_ref[...].astype(BF16)

    if kind == "col":
        o_spec = pl.BlockSpec((tr, sc), lambda i, p: (i, p[0]))
    else:
        o_spec = pl.BlockSpec((tr, sc), lambda i, p: (p[0] * nt + i, 0))
    return pl.pallas_call(
        body, name=name,
        grid_spec=pltpu.PrefetchScalarGridSpec(
            num_scalar_prefetch=1, grid=(nt,),
            in_specs=[pl.BlockSpec((tr, sc), lambda i, p: (i, 0))] + [ANY] * len(after), out_specs=o_spec),
        out_shape=jax.ShapeDtypeStruct((R, C), BF16),
        compiler_params=_cp("parallel"),
    )(p_arr, w, *after)


HBM_SPEC = pl.BlockSpec(memory_space=pltpu.HBM)
SEM_SPEC = pl.BlockSpec(memory_space=pltpu.SEMAPHORE)


def _ag_sems(geoms):
    return sum(6 if both else 3 for (_, _, _, both) in geoms)


def _ag_copies(fulls, geoms, ssem, rsem, mx, my, mc, q, h):
    chips = _other_chips(mx, my)
    out, base = [], 0
    for w, (kind, R, C, both) in enumerate(geoms):
        sr, sc = _shard_shape(kind, R, C)
        hr = sr // 2 if kind == "col" else sr
        reg = _region(fulls[w], kind, R, C, q, h, 0, hr)
        out.append([pltpu.make_async_remote_copy(
            src_ref=reg, dst_ref=reg, send_sem=ssem.at[base + 3 * t + k], recv_sem=rsem.at[base + 3 * t + k],
            device_id=(*chips[k], mc if t == 0 else 1 - mc), device_id_type=MESH)
            for t in range(2 if both else 1) for k in range(3)])
        base += 6 if both else 3
    return out


def ag_start(fulls, geoms, after, name):
    n = len(fulls)
    after = list(after)
    m = len(after)

    def body(*refs):
        ssem, rsem = refs[n + m:n + m + 2]
        outs, token = refs[n + m + 2:2 * n + m + 2], refs[2 * n + m + 2]
        mx, my, mc = _mesh_pos()
        p = 2 * mx + my
        col = [w for w, g in enumerate(geoms) if g[0] == "col"]
        row = [w for w, g in enumerate(geoms) if g[0] == "row"]
        for q in range(4):
            @pl.when(p == q)
            def _(q=q):
                cps = _ag_copies(outs, geoms, ssem, rsem, mx, my, mc, q, mc)
                for w in col:
                    for cp in cps[w]:
                        cp.start()
        for h in range(2):
            @pl.when(mc == h)
            def _(h=h):
                cps = _ag_copies(outs, geoms, ssem, rsem, mx, my, mc, p, h)
                for w in row:
                    for cp in cps[w]:
                        cp.start()
        token[...] = jnp.zeros_like(token)

    res = pl.pallas_call(
        body, name=name,
        out_shape=[pltpu.SemaphoreType.DMA((_ag_sems(geoms),)), pltpu.SemaphoreType.DMA((_ag_sems(geoms),))]
        + [pltpu.HBM(a.shape, a.dtype) for a in fulls] + [jax.ShapeDtypeStruct((8, 128), F32)],
        in_specs=[HBM_SPEC] * n + [ANY] * m,
        out_specs=[SEM_SPEC, SEM_SPEC] + [HBM_SPEC] * n + [pl.BlockSpec(memory_space=pltpu.VMEM)],
        input_output_aliases={w: 2 + w for w in range(n)},
        compiler_params=pltpu.CompilerParams(has_side_effects=pltpu.SideEffectType.DATAFLOW_SIDE_EFFECTING),
    )(*[pltpu.with_memory_space_constraint(a, pltpu.HBM) for a in fulls], *after)
    return res[0], res[1], list(res[2:2 + n]), res[2 + n]


def ag_wait(fulls, geoms, ssem, rsem, after, name):
    n = len(fulls)

    def body(*refs):
        ins, ssem_ref, rsem_ref = refs[:n], refs[n], refs[n + 1]
        mx, my, mc = _mesh_pos()
        for cps in _ag_copies(ins, geoms, ssem_ref, rsem_ref, mx, my, mc, 0, 0):
            for cp in cps:
                cp.wait_send()
                cp.wait_recv()

    return list(pl.pallas_call(
        body, name=name,
        out_shape=[pltpu.HBM(a.shape, a.dtype) for a in fulls],
        in_specs=[HBM_SPEC] * n + [SEM_SPEC, SEM_SPEC, ANY],
        out_specs=[HBM_SPEC] * n,
        input_output_aliases={w: w for w in range(n)},
        compiler_params=pltpu.CompilerParams(has_side_effects=pltpu.SideEffectType.DATAFLOW_SIDE_EFFECTING),
    )(*fulls, ssem, rsem, after))


def ag_forward(full, kind, R, C, name):
    sr, sc = _shard_shape(kind, R, C)
    hr, hc = (sr // 2, sc) if kind == "col" else (sr, sc // 2)
    tr = _row_tile(hr, hc, itemsize=2, budget=512 * 1024)
    nt = hr // tr

    total = 3 * nt

    def body(src_ref, full_ref, stage, lsem, ssem, rsem):
        step = pl.program_id(0) * nt + pl.program_id(1)
        par = step % 2
        mx, my, mc = _mesh_pos()

        def load(s, q, h, t):
            return pltpu.make_async_copy(_region(src_ref, kind, R, C, q, h, t, tr), stage.at[s], lsem.at[s])

        def push(s, q, h, t):
            return pltpu.make_async_remote_copy(src_ref=stage.at[s], dst_ref=_region(full_ref, kind, R, C, q, h, t, tr),
                                                send_sem=ssem.at[s], recv_sem=rsem, device_id=(mx, my, 1 - mc),
                                                device_id_type=MESH)

        def for_tile(stp, fn):
            q_k = _partner_chip(stp // nt, 2 * mx + my)
            if kind == "col":
                for q in range(4):
                    @pl.when(q_k == q)
                    def _(q=q):
                        fn(q, mc, stp % nt)
            else:
                for h in range(2):
                    @pl.when(mc == h)
                    def _(h=h):
                        fn(q_k, h, stp % nt)

        @pl.when(step == 0)
        def _():
            for_tile(step, lambda q, h, t: load(0, q, h, t).start())

        load(par, 0, 0, 0).wait()
        for_tile(step, lambda q, h, t: push(par, q, h, t).start())

        @pl.when(step + 1 < total)
        def _():
            @pl.when(step >= 1)
            def _():
                push(1 - par, 0, 0, 0).wait_send()
            for_tile(step + 1, lambda q, h, t: load(1 - par, q, h, t).start())

        @pl.when(step == total - 1)
        def _():
            push(par, 0, 0, 0).wait_send()
            push(1 - par, 0, 0, 0).wait_send()
            three = full_ref.at[pl.ds(0, hr), pl.ds(0, 3 * hc)] if kind == "col" else full_ref.at[pl.ds(0, 3 * hr), pl.ds(0, hc)]
            pltpu.make_async_remote_copy(src_ref=three, dst_ref=three, send_sem=ssem.at[0], recv_sem=rsem,
                                         device_id=(mx, my, 1 - mc), device_id_type=MESH).wait_recv()

    return pl.pallas_call(
        body, name=name, grid=(3, nt),
        in_specs=[ANY], out_specs=ANY,
        out_shape=jax.ShapeDtypeStruct((R, C), BF16),
        scratch_shapes=[pltpu.VMEM((2, tr, hc), BF16), pltpu.SemaphoreType.DMA((2,)), pltpu.SemaphoreType.DMA((2,)),
                        pltpu.SemaphoreType.DMA],
        input_output_aliases={0: 0},
        compiler_params=_cp("arbitrary", "arbitrary"),
    )(full)


def _half_shape(kind, R, C):
    return (R // 2, C) if kind == "col" else (R, C // 2)


def _piece_shape(kind, R, C):
    return (R // 2, C // 4) if kind == "col" else (R // 4, C // 2)


def pair_push(g, kind, c_arr, name):
    R, C = g.shape
    hr, hc = _half_shape(kind, R, C)
    tr = _row_tile(hr, hc, itemsize=2, budget=1024 * 1024)
    nt = hr // tr

    def body(c_ref, g_ref, out_ref, stage, ssem, rsem):
        i = pl.program_id(0)
        slot = i % 2
        mx, my, mc = _mesh_pos()

        def push(s, t):
            return pltpu.make_async_remote_copy(
                src_ref=stage.at[s], dst_ref=out_ref.at[pl.ds(pl.multiple_of(t * tr, 16), tr)],
                send_sem=ssem.at[s], recv_sem=rsem, device_id=(mx, my, 1 - mc), device_id_type=MESH)

        @pl.when(i >= 2)
        def _():
            push(slot, 0).wait_send()

        stage[slot] = g_ref[...]
        push(slot, i).start()

        @pl.when(i == nt - 1)
        def _():
            push(slot, 0).wait_send()
            if nt >= 2:
                push(1 - slot, 0).wait_send()
            pltpu.make_async_remote_copy(src_ref=out_ref, dst_ref=out_ref, send_sem=ssem.at[0], recv_sem=rsem,
                                         device_id=(mx, my, 1 - mc), device_id_type=MESH).wait_recv()

    if kind == "col":
        g_spec = pl.BlockSpec((tr, hc), lambda i, c: ((1 - c[0]) * nt + i, 0))
    else:
        g_spec = pl.BlockSpec((tr, hc), lambda i, c: (i, 1 - c[0]))
    return pl.pallas_call(
        body, name=name,
        grid_spec=pltpu.PrefetchScalarGridSpec(
            num_scalar_prefetch=1, grid=(nt,), in_specs=[g_spec], out_specs=ANY,
            scratch_shapes=[pltpu.VMEM((2, tr, hc), BF16), pltpu.SemaphoreType.DMA((2,)), pltpu.SemaphoreType.DMA]),
        out_shape=jax.ShapeDtypeStruct((hr, hc), BF16),
        compiler_params=_cp("arbitrary"),
    )(c_arr, g)


def _partner_chip(k, p):
    return p ^ jnp.where(k == 0, 2, jnp.where(k == 1, 1, jnp.where(k == 2, 3, 0)))


def pair_add(g, got, kind, cp_arr, name):
    R, C = g.shape
    pr, pc = _piece_shape(kind, R, C)
    tr = _row_tile(pr, pc, itemsize=2, budget=1024 * 1024)
    nt = pr // tr

    def body(cp_ref, g_ref, got_ref, ps_ref, rb_ref):
        tile = (g_ref[...].astype(F32) + got_ref[...].astype(F32)).astype(BF16)
        ps_ref[...] = tile

        @pl.when(pl.program_id(1) == cp_ref[1])
        def _():
            rb_ref[...] = tile

    if kind == "col":
        g_spec = pl.BlockSpec((tr, pc), lambda i, q, cp: (cp[0] * nt + i, q))
        got_spec = pl.BlockSpec((tr, pc), lambda i, q, cp: (i, q))
    else:
        g_spec = pl.BlockSpec((tr, pc), lambda i, q, cp: (q * nt + i, cp[0]))
        got_spec = pl.BlockSpec((tr, pc), lambda i, q, cp: (q * nt + i, 0))
    return pl.pallas_call(
        body, name=name,
        grid_spec=pltpu.PrefetchScalarGridSpec(
            num_scalar_prefetch=1, grid=(nt, 4), in_specs=[g_spec, got_spec],
            out_specs=[pl.BlockSpec((None, tr, pc), lambda i, q, cp: (q, i, 0)),
                       pl.BlockSpec((None, tr, pc), lambda i, q, cp: (cp[1], i, 0))]),
        out_shape=[jax.ShapeDtypeStruct((4, pr, pc), BF16)] * 2,
        compiler_params=_cp("arbitrary", "arbitrary"),
    )(cp_arr, g, got)


def _rs_copies(ps, rb, ssem, rsem, mx, my, mc):
    p = 2 * mx + my
    out = []
    for w in range(len(ps)):
        for k, chip in enumerate(_other_chips(mx, my)):
            out.append(pltpu.make_async_remote_copy(
                src_ref=ps[w].at[2 * chip[0] + chip[1]], dst_ref=rb[w].at[p], send_sem=ssem.at[3 * w + k],
                recv_sem=rsem.at[3 * w + k], device_id=(*chip, mc), device_id_type=MESH))
    return out


def rs_start(ps, rb, after, name):
    n = len(ps)
    after = list(after)
    m = len(after)

    def body(*refs):
        ssem, rsem = refs[2 * n + m:2 * n + m + 2]
        ps_o = refs[2 * n + m + 2:3 * n + m + 2]
        rb_o = refs[3 * n + m + 2:4 * n + m + 2]
        token = refs[4 * n + m + 2]
        for cp in _rs_copies(ps_o, rb_o, ssem, rsem, *_mesh_pos()):
            cp.start()
        token[...] = jnp.zeros_like(token)

    both = list(ps) + list(rb)
    res = pl.pallas_call(
        body, name=name,
        out_shape=[pltpu.SemaphoreType.DMA((3 * n,)), pltpu.SemaphoreType.DMA((3 * n,))]
        + [pltpu.HBM(a.shape, a.dtype) for a in both] + [jax.ShapeDtypeStruct((8, 128), F32)],
        in_specs=[HBM_SPEC] * (2 * n) + [ANY] * m,
        out_specs=[SEM_SPEC, SEM_SPEC] + [HBM_SPEC] * (2 * n) + [pl.BlockSpec(memory_space=pltpu.VMEM)],
        input_output_aliases={w: 2 + w for w in range(2 * n)},
        compiler_params=pltpu.CompilerParams(has_side_effects=pltpu.SideEffectType.DATAFLOW_SIDE_EFFECTING),
    )(*[pltpu.with_memory_space_constraint(a, pltpu.HBM) for a in both], *after)
    return res[0], res[1], list(res[2:2 + n]), list(res[2 + n:2 + 2 * n]), res[2 + 2 * n]


def rs_wait(ps, rb, ssem, rsem, after, name):
    n = len(ps)
    after = list(after)
    m = len(after)

    def body(*refs):
        ps_i, rb_i = refs[:n], refs[n:2 * n]
        ssem_ref, rsem_ref = refs[2 * n], refs[2 * n + 1]
        for cp in _rs_copies(ps_i, rb_i, ssem_ref, rsem_ref, *_mesh_pos()):
            cp.wait_send()
            cp.wait_recv()

    both = list(ps) + list(rb)
    res = pl.pallas_call(
        body, name=name,
        out_shape=[pltpu.HBM(a.shape, a.dtype) for a in both],
        in_specs=[HBM_SPEC] * (2 * n) + [SEM_SPEC, SEM_SPEC] + [ANY] * m,
        out_specs=[HBM_SPEC] * (2 * n),
        input_output_aliases={w: w for w in range(2 * n)},
        compiler_params=pltpu.CompilerParams(has_side_effects=pltpu.SideEffectType.DATAFLOW_SIDE_EFFECTING),
    )(*both, ssem, rsem, *after)
    return list(res[n:])


def sum_share(parts, kind, R, C, name):
    _, pr, pc = parts.shape
    sr, sc = _shard_shape(kind, R, C)
    tr = _row_tile(pr, pc * 4, budget=4 * 1024 * 1024)
    nt = pr // tr

    def body(p_ref, fin_ref, stage, lsem, ssem, rsem):
        i = pl.program_id(0)
        slot = i % 2
        mx, my, mc = _mesh_pos()

        def region(h, t):
            r0 = pl.multiple_of(t * tr, 8)
            if kind == "col":
                return fin_ref.at[pl.ds(pl.multiple_of(h * pr + r0, 8), tr)]
            return fin_ref.at[pl.ds(r0, tr), pl.ds(h * pc, pc)]

        def copies(s, h, t):
            return (pltpu.make_async_copy(stage.at[s], region(h, t), lsem.at[s]),
                    pltpu.make_async_remote_copy(src_ref=stage.at[s], dst_ref=region(h, t), send_sem=ssem.at[s],
                                                 recv_sem=rsem, device_id=(mx, my, 1 - mc), device_id_type=MESH))

        def wait_sent(s):
            loc, rem = copies(s, 0, 0)
            loc.wait()
            rem.wait_send()

        @pl.when(i >= 2)
        def _():
            wait_sent(slot)

        acc = p_ref[0].astype(F32)
        for k in range(1, 4):
            acc = acc + p_ref[k].astype(F32)
        stage[slot] = acc
        if kind == "col":
            for cp in copies(slot, mc, i):
                cp.start()
        else:
            for h in range(2):
                @pl.when(mc == h)
                def _(h=h):
                    for cp in copies(slot, h, i):
                        cp.start()

        @pl.when(i == nt - 1)
        def _():
            wait_sent(slot)
            if nt >= 2:
                wait_sent(1 - slot)
            half = fin_ref.at[pl.ds(0, pr), pl.ds(0, pc)]
            pltpu.make_async_remote_copy(src_ref=half, dst_ref=half, send_sem=ssem.at[0], recv_sem=rsem,
                                         device_id=(mx, my, 1 - mc), device_id_type=MESH).wait_recv()

    return pl.pallas_call(
        body, name=name, grid=(nt,),
        in_specs=[pl.BlockSpec((4, tr, pc), lambda i: (0, i, 0))],
        out_specs=ANY,
        out_shape=jax.ShapeDtypeStruct((sr, sc), F32),
        scratch_shapes=[pltpu.VMEM((2, tr, pc), F32), pltpu.SemaphoreType.DMA((2,)), pltpu.SemaphoreType.DMA((2,)),
                        pltpu.SemaphoreType.DMA],
        compiler_params=_cp("arbitrary"),
    )(parts)


def _pack(parts, rows):
    flat = []
    for a in parts:
        a = jnp.ravel(a).astype(F32)
        flat.append(jnp.pad(a, (0, (-a.shape[0]) % 128)))
    v = jnp.concatenate(flat)
    return jnp.pad(v, (0, rows * 128 - v.shape[0])).reshape(rows, 128)


def _unpack(block, shapes):
    lead = block.shape[:-2]
    v = block.reshape(lead + (-1,))
    out, off = [], 0
    for shp in shapes:
        n = int(np.prod(shp))
        out.append(v[..., off:off + n].reshape(lead + tuple(shp)))
        off += n + (-n) % 128
    return out


def _block_diag4(w):
    w4 = w.reshape(4, 4, 64, 64)
    eye = jnp.eye(4, dtype=w.dtype)
    return (w4[:, :, :, None, :] * eye[None, :, None, :, None]).reshape(4, 256, 256)


def _diag_blocks(bd):
    b5 = bd.reshape(4, 4, 64, 4, 64)
    return jnp.stack([b5[:, i, :, i, :] for i in range(4)], axis=1).reshape(16, 64, 64)


def _bias_window(rel_bias):
    m = (np.arange(768) + 127) % 768 - 127
    w = rel_bias[:, np.clip(512 - m, -128, 128) + 128]
    win = jnp.tile(w, (1, 128))[:, :128 * 767].reshape(8, 128, 767)[:, :, :WIN]
    qh = np.arange(128)[:, None] // CHUNK
    kc = np.arange(WIN)[None, :] // CHUNK
    valid = (kc >= qh) & (kc <= qh + 8)
    return jnp.where(jnp.asarray(valid)[None], win, NEG)


SMALL = ("b_ada", "norm_pre", "norm_post", "rel_bias", "conv_w", "conv_b", "lru_wa", "lru_ba", "lru_wx",
         "lru_bx", "lru_lambda")
WEIGHTS = ("w_ada", "b_ada", "norm_pre", "norm_post", "ffn1_w_gu", "ffn1_w_down", "w_in", "rel_bias", "conv_w",
           "conv_b", "lru_wa", "lru_ba", "lru_wx", "lru_bx", "lru_lambda", "w_att_o", "w_rec_o", "w_out",
           "ffn2_w_gu", "ffn2_w_down")


def kernel(x, c, w_ada, b_ada, norm_pre, norm_post, ffn1_w_gu, ffn1_w_down, w_in, rel_bias, conv_w, conv_b, lru_wa, lru_ba, lru_wx, lru_bx, lru_lambda, w_att_o, w_rec_o, w_out, ffn2_w_gu, ffn2_w_down, loss_target, m_w_ada, m_b_ada, m_norm_pre, m_norm_post, m_ffn1_w_gu, m_ffn1_w_down, m_w_in, m_rel_bias, m_conv_w, m_conv_b, m_lru_wa, m_lru_ba, m_lru_wx, m_lru_bx, m_lru_lambda, m_w_att_o, m_w_rec_o, m_w_out, m_ffn2_w_gu, m_ffn2_w_down, v_w_ada, v_b_ada, v_norm_pre, v_norm_post, v_ffn1_w_gu, v_ffn1_w_down, v_w_in, v_rel_bias, v_conv_w, v_conv_b, v_lru_wa, v_lru_ba, v_lru_wx, v_lru_bx, v_lru_lambda, v_w_att_o, v_w_rec_o, v_w_out, v_ffn2_w_gu, v_ffn2_w_down):
    W = dict(w_ada=w_ada, b_ada=b_ada, norm_pre=norm_pre, norm_post=norm_post, ffn1_w_gu=ffn1_w_gu,
             ffn1_w_down=ffn1_w_down, w_in=w_in, rel_bias=rel_bias, conv_w=conv_w, conv_b=conv_b, lru_wa=lru_wa,
             lru_ba=lru_ba, lru_wx=lru_wx, lru_bx=lru_bx, lru_lambda=lru_lambda, w_att_o=w_att_o, w_rec_o=w_rec_o,
             w_out=w_out, ffn2_w_gu=ffn2_w_gu, ffn2_w_down=ffn2_w_down)
    M = dict(w_ada=m_w_ada, b_ada=m_b_ada, norm_pre=m_norm_pre, norm_post=m_norm_post, ffn1_w_gu=m_ffn1_w_gu,
             ffn1_w_down=m_ffn1_w_down, w_in=m_w_in, rel_bias=m_rel_bias, conv_w=m_conv_w, conv_b=m_conv_b,
             lru_wa=m_lru_wa, lru_ba=m_lru_ba, lru_wx=m_lru_wx, lru_bx=m_lru_bx, lru_lambda=m_lru_lambda,
             w_att_o=m_w_att_o, w_rec_o=m_w_rec_o, w_out=m_w_out, ffn2_w_gu=m_ffn2_w_gu, ffn2_w_down=m_ffn2_w_down)
    V = dict(w_ada=v_w_ada, b_ada=v_b_ada, norm_pre=v_norm_pre, norm_post=v_norm_post, ffn1_w_gu=v_ffn1_w_gu,
             ffn1_w_down=v_ffn1_w_down, w_in=v_w_in, rel_bias=v_rel_bias, conv_w=v_conv_w, conv_b=v_conv_b,
             lru_wa=v_lru_wa, lru_ba=v_lru_ba, lru_wx=v_lru_wx, lru_bx=v_lru_bx, lru_lambda=v_lru_lambda,
             w_att_o=v_w_att_o, w_rec_o=v_w_rec_o, w_out=v_w_out, ffn2_w_gu=v_ffn2_w_gu, ffn2_w_down=v_ffn2_w_down)
    mx, my, mc = _mesh_pos()
    p = 2 * mx + my
    e = 4 * mx + 2 * my + mc
    xs = x[0]

    c_arr = jnp.reshape(mc, (1,)).astype(jnp.int32)
    cp_arr = jnp.stack([mc, p]).astype(jnp.int32)
    p_arr = jnp.reshape(p, (1,)).astype(jnp.int32)
    direct = ("w_att_o", "w_rec_o", "w_out", "ffn2_w_gu", "ffn2_w_down")
    geoms = [(kind, R, C, n in direct) for (n, kind, R, C) in BIG]
    names = [b[0] for b in BIG]
    placed = [ag_local(W[n][0], kind, R, C, p_arr, "ag_local_" + n) for (n, kind, R, C) in BIG[:2]]

    def arrived(fly, lo, hi, ssem, rsem, after, tag):
        done = ag_wait(fly, geoms[lo:hi], ssem, rsem, after, "ag_wait_" + tag)
        return [a if both else ag_forward(a, kind, R, C, "ag_forward_" + n)
                for a, (kind, R, C, both), n in zip(done, geoms[lo:hi], names[lo:hi])]

    g1 = ag_small(_pack([c, norm_pre, norm_post, conv_w], 32), "ag_small_params")
    c_all, npre4, npost4, cw4 = _unpack(g1, [(D,), (3, 256), (3, 256), (4, 256)])
    chipwise = lambda a: jnp.moveaxis(a[0::2], 0, 1).reshape(a.shape[1], D)
    npre, npost, conv_full = chipwise(npre4), chipwise(npost4), chipwise(cw4)

    b_cols = lax.dynamic_slice(b_ada, (0, p * 2304), (1, 2304))
    mod_cols = ada_fwd(c_all, w_ada[0], b_cols, "ada_fwd")
    g2 = ag_small(mod_cols.reshape(144, 128), "ag_mod")
    mod_all = jnp.moveaxis(g2[0::2].reshape(4, 8, 2304), 0, 1).reshape(8, 9 * D)
    mod = lax.dynamic_index_in_dim(mod_all, e, 0, keepdims=False).reshape(3, 3, D)
    zeros3 = jnp.zeros((3, D), F32)
    vecs = [jnp.concatenate([npre[k:k + 1], npost[k:k + 1], mod[k], zeros3], axis=0) for k in range(3)]

    f1_s, f1_r, f1_fly, tok0 = ag_start(placed[:2], geoms[:2], [g2], "ag_start_ffn1")
    placed += [ag_local(W[n][0], kind, R, C, p_arr, "ag_local_" + n, after=[tok0]) for (n, kind, R, C) in BIG[2:]]
    f1_gu, f1_dn = arrived(f1_fly, 0, 2, f1_s, f1_r, placed[7], "ffn1")
    mix_s, mix_r, mix_fly, tok1 = ag_start(placed[2:6], geoms[2:6], [f1_gu, f1_dn], "ag_start_mixer")
    ffn_s, ffn_r, ffn_fly, tok2 = ag_start(placed[6:], geoms[6:], [tok1], "ag_start_ffn2")
    wa_bd = _block_diag4(lru_wa[0]).astype(BF16)
    wx_bd = _block_diag4(lru_wx[0]).astype(BF16)
    pvec = jnp.concatenate([conv_full, conv_b, lru_ba, lru_bx, lru_lambda], axis=0)
    bias = _bias_window(rel_bias[0]).reshape(4, 256, WIN)

    f1_u = f1_gu[:, FF:]
    x1, h1, g1_, u1, a1, f1 = ffn_fwd(xs, vecs[0] + tok2[0:1, 0:1], f1_gu, f1_u, f1_dn, 0.5, "ffn1_fwd")
    win, wao, wro, wout = arrived(mix_fly, 2, 6, mix_s, mix_r, x1, "mixer")
    h2, qkv, rest = proj_fwd(x1, vecs[1], win, "proj_fwd")
    ao = attn_fwd(qkv, bias, "attn_fwd")
    hl, hg = lru_fwd(rest, pvec, wa_bd, wx_bd, "lru_fwd")
    x2, att, rec, mg, f2 = mix_out_fwd(x1, ao, hg, rest, vecs[1], wao, wro, wout, "mix_out_fwd")
    f2_gu, f2_dn = arrived(ffn_fly, 6, 8, ffn_s, ffn_r, x2, "ffn2")
    f2_u = f2_gu[:, FF:]
    dy, h3, g3_, u3, a3, f3, lvec = ffn_fwd(x2, vecs[2], f2_gu, f2_u, f2_dn, 0.5, "ffn2_fwd", tgt=loss_target[0])
    loss = lax.psum(lvec[0, 0], ("x", "y", "c"))

    G, grads = {}, {}
    geo = {n: (kind, R, C) for (n, kind, R, C) in BIG}

    def reduce_begin(names, tag):
        ps, rb = [], []
        for n in names:
            got = pair_push(G[n], geo[n][0], c_arr, "rs_push_" + n)
            a, b = pair_add(G[n], got, geo[n][0], cp_arr, "rs_pair_sum_" + n)
            ps.append(a)
            rb.append(b)
        return rs_start(ps, rb, [], "rs_start_" + tag)

    def reduce_end(names, flight, after, tag):
        ssem, rsem, ps, rb, _ = flight
        for a, n in zip(rs_wait(ps, rb, ssem, rsem, after, "rs_wait_" + tag), names):
            grads[n] = sum_share(a, *geo[n], "rs_sum_share_" + n)[None]

    dx2, df3, dgu3, va2 = ffn_bwd(dy, x2, f3, g3_, u3, vecs[2], f2_gu, f2_u, f2_dn, 0.5, "ffn2_bwd")
    G["ffn2_w_gu"] = mm_tn(h3, dgu3, "dw_ffn2_gu", D, 1408, 1024)
    G["ffn2_w_down"] = mm_tn(a3, df3, "dw_ffn2_down", 1408, D, 1024)
    fly_ffn2 = reduce_begin(("ffn2_w_gu", "ffn2_w_down"), "ffn2")
    vec1 = vecs[1] + fly_ffn2[4][0:1, 0:1]
    df2, d_att, d_rec, dao, dhl, d3, va_out = mix_out_bwd(dx2, f2, att, rec, rest, hl, vec1, wao, wro, wout,
                                                          "mix_out_bwd")
    G["w_out"] = mm_tn(mg, df2, "dw_out", D, D, 1024)
    G["w_att_o"] = mm_tn(ao, d_att, "dw_att_o", 512, D, 1024)
    G["w_rec_o"] = mm_tn(hg, d_rec, "dw_rec_o", D, D, 1024)
    dq, db, dkv = attn_bwd(qkv, dao, bias, "attn_bwd")
    dxr, v_lru, dwa_bd, dwx_bd = lru_bwd(dhl, hl, rest, pvec, wa_bd, wx_bd, "lru_bwd")
    dx1, va_in = proj_bwd(dq, dkv, dxr, d3, win, x1, dx2, vecs[1], "proj_bwd")
    G["w_in"] = dw_in(h2, dq, dkv, dxr, d3, "dw_in")
    fly_mix = reduce_begin(("w_in", "w_att_o", "w_rec_o", "w_out"), "mixer")
    vec0 = vecs[0] + fly_mix[4][0:1, 0:1]
    dx0, df1, dgu1, va0 = ffn_bwd(dx1, xs, f1, g1_, u1, vec0, f1_gu, f1_u, f1_dn, 0.5, "ffn1_bwd")
    G["ffn1_w_gu"] = mm_tn(h1, dgu1, "dw_ffn1_gu", D, 1408, 1024)
    G["ffn1_w_down"] = mm_tn(a1, df1, "dw_ffn1_down", 1408, D, 1024)
    fly_ffn1 = reduce_begin(("ffn1_w_gu", "ffn1_w_down"), "ffn1")
    reduce_end(("ffn2_w_gu", "ffn2_w_down"), fly_ffn2, [fly_ffn1[4]], "ffn2")
    reduce_end(("w_in", "w_att_o", "w_rec_o", "w_out"), fly_mix, [fly_ffn1[4], grads["ffn2_w_down"]], "mixer")

    va1 = va_out + va_in
    vas = (va0, va1, va2)
    dmod = jnp.stack([v[2:5] for v in vas])
    part = {"b_ada": dmod, "norm_pre": jnp.stack([v[0] for v in vas]), "norm_post": jnp.stack([v[1] for v in vas]),
            "rel_bias": bias_grad(db.reshape(8, 128, WIN), "bias_grad")[:, :257], "conv_w": v_lru[0:4], "conv_b": v_lru[4],
            "lru_wa": _diag_blocks(dwa_bd), "lru_ba": v_lru[5], "lru_wx": _diag_blocks(dwx_bd), "lru_bx": v_lru[6],
            "lru_lambda": v_lru[7]}
    full_shapes = {"b_ada": (9 * D,), "norm_pre": (3, D), "norm_post": (3, D), "rel_bias": (8, 257),
                   "conv_w": (4, D), "conv_b": (D,), "lru_wa": (16, 64, 64), "lru_ba": (D,),
                   "lru_wx": (16, 64, 64), "lru_bx": (D,), "lru_lambda": (D,)}
    g3 = ag_small(_pack([part[n] for n in SMALL], 1232), "ag_small_grads")
    red = dict(zip(SMALL, _unpack(sum_lead(g3, "sum_small_grads"), [full_shapes[n] for n in SMALL])))
    cols = lambda a: lax.dynamic_slice(a, (0, p * 256), (a.shape[0], 256))
    grads.update({"b_ada": red["b_ada"][None], "norm_pre": cols(red["norm_pre"])[None],
                  "norm_post": cols(red["norm_post"])[None], "rel_bias": red["rel_bias"][None],
                  "conv_w": cols(red["conv_w"])[None], "conv_b": red["conv_b"][None], "lru_wa": red["lru_wa"][None],
                  "lru_ba": red["lru_ba"][None], "lru_wx": red["lru_wx"][None], "lru_bx": red["lru_bx"][None],
                  "lru_lambda": red["lru_lambda"][None]})

    dmod_all = g3[:, :72].reshape(8, 9 * D)
    dmod_cols = jnp.pad(lax.dynamic_slice(dmod_all, (0, p * 2304), (8, 2304)), ((0, 120), (0, 0)))
    c_all_t = jnp.pad(c_all.T, ((0, 0), (0, 120)))
    grads["w_ada"] = ada_bwd(c_all_t, dmod_cols, "ada_bwd")[None]

    delta, new_m, new_v = {}, {}, {}

    def update(n):
        shp = W[n].shape
        d_, m_, v_ = adamw(W[n][0], grads[n][0], M[n][0], V[n][0], "adamw_" + n)
        delta[n], new_m[n], new_v[n] = d_.reshape(shp), m_.reshape(shp), v_.reshape(shp)

    for n in ("w_ada", "ffn2_w_gu", "ffn2_w_down", "w_in", "w_att_o", "w_rec_o", "w_out"):
        update(n)
    packed = [_pack([src[n] for n in SMALL], 1168) for src in (W, grads, M, V)]
    outs = adamw(*packed, "adamw_small")
    for dst, blk in zip((delta, new_m, new_v), outs):
        for n, a in zip(SMALL, _unpack(blk, [W[n].shape for n in SMALL])):
            dst[n] = a
    reduce_end(("ffn1_w_gu", "ffn1_w_down"), fly_ffn1,
               [outs[0], delta["w_ada"], delta["ffn2_w_gu"], delta["ffn2_w_down"], delta["w_in"], delta["w_out"]], "ffn1")
    for n in ("ffn1_w_gu", "ffn1_w_down"):
        update(n)

    return (loss, dx0[None], *[grads[n] for n in WEIGHTS], *[delta[n] for n in WEIGHTS],
            *[new_m[n] for n in WEIGHTS], *[new_v[n] for n in WEIGHTS])
```

```python
import functools

import numpy as np
import jax
import jax.numpy as jnp
from jax import lax
from jax.experimental import pallas as pl
from jax.experimental.pallas import tpu as pltpu

F32 = jnp.float32
BF16 = jnp.bfloat16

D = 1024
FF = 2816
PW = 5632
HP = 128
CHUNK = 64
WIN = 640
TQ = 512
EPS = 1e-6
NEG = -1e30
LRU_C = 8.0
N_DEV = 8
VMEM_LIMIT = 56 * 1024 * 1024

ADAM_LR, ADAM_B1, ADAM_B2, ADAM_EPS, ADAM_WD, ADAM_STEP = 0.001, 0.9, 0.999, 1e-08, 0.01, 10

MESH = pl.DeviceIdType.MESH
ANY = pl.BlockSpec(memory_space=pl.ANY)


def _cp(*sem):
    return pltpu.CompilerParams(dimension_semantics=tuple(sem), vmem_limit_bytes=VMEM_LIMIT)


def _dot(a, b):
    return jnp.dot(a, b, preferred_element_type=F32)


def _dot_nt(a, b):
    return lax.dot_general(a, b, (((1,), (1,)), ((), ())), preferred_element_type=F32)


def _dot_tn(a, b):
    return lax.dot_general(a, b, (((0,), (0,)), ((), ())), preferred_element_type=F32)


def _mean(v):
    return jnp.mean(v, axis=-1, keepdims=True)


def _colsum(v):
    return jnp.sum(v, axis=0, keepdims=True)


def _sigmoid(v):
    return 0.5 * jnp.tanh(0.5 * v) + 0.5


_GK = 0.7978845608028654


def _gelu(v):
    t = jnp.tanh(_GK * (v + 0.044715 * v * v * v))
    return 0.5 * v * (1.0 + t)


def _pre_norm(xv, vec_ref):
    r = lax.rsqrt(_mean(xv * xv) + EPS)
    n = xv * r * vec_ref[0:1, :]
    return n * (1.0 + vec_ref[3:4, :]) + vec_ref[2:3, :]


def _pre_norm_bwd(dh, xv, dres, vec_ref, vacc_ref):
    r = lax.rsqrt(_mean(xv * xv) + EPS)
    xh = xv * r
    n = xh * vec_ref[0:1, :]
    vacc_ref[2:3, :] += _colsum(dh)
    vacc_ref[3:4, :] += _colsum(dh * n)
    dn = dh * (1.0 + vec_ref[3:4, :])
    vacc_ref[0:1, :] += _colsum(dn * xh)
    dxh = dn * vec_ref[0:1, :]
    return r * (dxh - xh * _mean(dxh * xh)) + dres


def _post_norm_bwd(dxo, fv, res, vec_ref, vacc_ref):
    rf = lax.rsqrt(_mean(fv * fv) + EPS)
    fh = fv * rf
    gp = vec_ref[1:2, :]
    vacc_ref[4:5, :] += _colsum(res * dxo * (fh * gp))
    dy = (res * vec_ref[4:5, :]) * dxo
    vacc_ref[1:2, :] += _colsum(dy * fh)
    dfn = dy * gp
    return rf * (dfn - fh * _mean(dfn * fh))


def ffn_fwd(x, vec, w_gu, w_u, w_dn, res, name, tgt=None, tm=1024, tf=512):
    S = x.shape[0]
    tm = min(tm, S)
    nt = S // tm
    nf = -(-FF // tf)
    tail = FF - tf * (nf - 1)
    head = tgt is not None

    def body(*refs):
        x_ref, vec_ref, wg_ref, wu_ref, wd_ref = refs[:5]
        if head:
            t_ref, xo_ref, h_ref, g_ref, u_ref, a_ref, f_ref, l_ref, hs, acc, lacc = refs[5:]
        else:
            xo_ref, h_ref, g_ref, u_ref, a_ref, f_ref, hs, acc = refs[5:]
        i, j = pl.program_id(0), pl.program_id(1)

        @pl.when(j == 0)
        def _():
            h = _pre_norm(x_ref[...], vec_ref).astype(BF16)
            hs[...] = h
            h_ref[...] = h
            acc[...] = jnp.zeros_like(acc)

        def chunk(w):
            h = hs[...]
            g = _dot(h, wg_ref[:, 0:w])
            u = _dot(h, wu_ref[:, 0:w])
            g_ref[:, 0:w] = g.astype(BF16)
            u_ref[:, 0:w] = u.astype(BF16)
            a = (g * _sigmoid(g) * u).astype(BF16)
            a_ref[:, 0:w] = a
            acc[...] += _dot(a, wd_ref[0:w, :])

        @pl.when(j < nf - 1)
        def _():
            chunk(tf)

        @pl.when(j == nf - 1)
        def _():
            chunk(tail)
            f = acc[...]
            f_ref[...] = f
            y = f * lax.rsqrt(_mean(f * f) + EPS) * vec_ref[1:2, :]
            xo = x_ref[...] + (res * vec_ref[4:5, :]) * y
            if head:
                @pl.when(i == 0)
                def _():
                    lacc[...] = jnp.zeros_like(lacc)

                d = xo - t_ref[...]
                xo_ref[...] = d * (1.0 / D)
                lacc[...] += _colsum(d * d)

                @pl.when(i == nt - 1)
                def _():
                    l_ref[...] = jnp.broadcast_to(0.5 * jnp.sum(lacc[...]) * (1.0 / D), (8, 128))
            else:
                xo_ref[...] = xo

    row = lambda i, j: (i, 0)
    col = lambda i, j: (i, j)
    once = dict(pipeline_mode=pl.Buffered(1)) if head else {}
    in_specs = [pl.BlockSpec((tm, D), row, **once), pl.BlockSpec((8, D), lambda i, j: (0, 0)),
                pl.BlockSpec((D, tf), lambda i, j: (0, j)), pl.BlockSpec((D, tf), lambda i, j: (0, j)),
                pl.BlockSpec((tf, D), lambda i, j: (j, 0))]
    out_specs = [pl.BlockSpec((tm, D), row), pl.BlockSpec((tm, D), row), pl.BlockSpec((tm, tf), col),
                 pl.BlockSpec((tm, tf), col), pl.BlockSpec((tm, tf), col), pl.BlockSpec((tm, D), row)]
    out_shape = [jax.ShapeDtypeStruct((S, D), F32), jax.ShapeDtypeStruct((S, D), BF16),
                 jax.ShapeDtypeStruct((S, FF), BF16), jax.ShapeDtypeStruct((S, FF), BF16),
                 jax.ShapeDtypeStruct((S, FF), BF16), jax.ShapeDtypeStruct((S, D), F32)]
    scratch = [pltpu.VMEM((tm, D), BF16), pltpu.VMEM((tm, D), F32)]
    args = [x, vec, w_gu, w_u, w_dn]
    if head:
        in_specs.append(pl.BlockSpec((tm, D), row, **once))
        out_specs.append(pl.BlockSpec((8, 128), lambda i, j: (0, 0)))
        out_shape.append(jax.ShapeDtypeStruct((8, 128), F32))
        scratch.append(pltpu.VMEM((1, D), F32))
        args.append(tgt)
    return pl.pallas_call(
        body, name=name, grid=(nt, nf), in_specs=in_specs, out_specs=out_specs, out_shape=out_shape,
        scratch_shapes=scratch,
        compiler_params=_cp("arbitrary" if head else "parallel", "arbitrary"),
    )(*args)


def ffn_bwd(dxo, x, f, g, u, vec, w_gu, w_u, w_dn, res, name, tm=1024, tf=512):
    S = x.shape[0]
    tm = min(tm, S)
    nf = -(-FF // tf)
    tail = FF - tf * (nf - 1)

    def body(dxo_ref, x_ref, f_ref, g_ref, u_ref, vec_ref, wg_ref, wu_ref, wd_ref,
             dx_ref, df_ref, dgu_ref, vacc_ref, dfs, acc):
        i, j = pl.program_id(0), pl.program_id(1)

        @pl.when((i == 0) & (j == 0))
        def _():
            vacc_ref[...] = jnp.zeros_like(vacc_ref)

        @pl.when(j == 0)
        def _():
            df = _post_norm_bwd(dxo_ref[...], f_ref[...], res, vec_ref, vacc_ref).astype(BF16)
            dfs[...] = df
            df_ref[...] = df
            acc[...] = jnp.zeros_like(acc)

        def chunk(w):
            da = _dot_nt(dfs[...], wd_ref[0:w, :])
            gv, uv = g_ref[:, 0:w].astype(F32), u_ref[:, 0:w].astype(F32)
            sg = _sigmoid(gv)
            dg = (da * uv * (sg * (1.0 + gv * (1.0 - sg)))).astype(BF16)
            du = (da * (gv * sg)).astype(BF16)
            dgu_ref[0, :, 0:w] = dg
            dgu_ref[1, :, 0:w] = du
            acc[...] += _dot_nt(dg, wg_ref[:, 0:w]) + _dot_nt(du, wu_ref[:, 0:w])

        @pl.when(j < nf - 1)
        def _():
            chunk(tf)

        @pl.when(j == nf - 1)
        def _():
            chunk(tail)
            dx_ref[...] = _pre_norm_bwd(acc[...], x_ref[...], dxo_ref[...], vec_ref, vacc_ref)

    row = lambda i, j: (i, 0)
    col = lambda i, j: (i, j)
    return pl.pallas_call(
        body, name=name, grid=(S // tm, nf),
        in_specs=[pl.BlockSpec((tm, D), row, pipeline_mode=pl.Buffered(1)) for _ in range(3)]
        + [pl.BlockSpec((tm, tf), col), pl.BlockSpec((tm, tf), col),
                  pl.BlockSpec((8, D), lambda i, j: (0, 0)),
                  pl.BlockSpec((D, tf), lambda i, j: (0, j)), pl.BlockSpec((D, tf), lambda i, j: (0, j)),
                  pl.BlockSpec((tf, D), lambda i, j: (j, 0))],
        out_specs=[pl.BlockSpec((tm, D), row), pl.BlockSpec((tm, D), row),
                   pl.BlockSpec((2, tm, tf), lambda i, j: (0, i, j)),
                   pl.BlockSpec((8, D), lambda i, j: (0, 0))],
        out_shape=[jax.ShapeDtypeStruct((S, D), F32), jax.ShapeDtypeStruct((S, D), BF16),
                   jax.ShapeDtypeStruct((2, S, FF), BF16), jax.ShapeDtypeStruct((8, D), F32)],
        scratch_shapes=[pltpu.VMEM((tm, D), BF16), pltpu.VMEM((tm, D), F32)],
        compiler_params=_cp("arbitrary", "arbitrary"),
    )(dxo, x, f, g, u, vec, w_gu, w_u, w_dn)


def mm_tn(a, b, name, tm, tn, tk, out_dtype=BF16, prev=None, col_off=0, n_total=None):
    S, M = a.shape
    if b.ndim == 3:
        G, _, Nf = b.shape
    else:
        G, Nf = 1, b.shape[1]
    N = G * Nf
    n_total = N if n_total is None else n_total
    tk = min(tk, S)
    nbf = Nf // tn
    nk = S // tk
    ob = col_off // tn

    def body(*refs):
        a_ref, b_ref = refs[0], refs[1]
        o_ref, acc = refs[-2], refs[-1]
        k = pl.program_id(2)

        @pl.when(k == 0)
        def _():
            acc[...] = jnp.zeros_like(acc)

        acc[...] += _dot_tn(a_ref[...], b_ref[...])

        @pl.when(k == nk - 1)
        def _():
            o_ref[...] = acc[...].astype(out_dtype)

    if b.ndim == 3:
        b_spec = pl.BlockSpec((None, tk, tn), lambda i, j, k: (j // nbf, k, j % nbf))
    else:
        b_spec = pl.BlockSpec((tk, tn), lambda i, j, k: (k, j))
    in_specs = [pl.BlockSpec((tk, tm), lambda i, j, k: (k, i)), b_spec]
    args = [a, b]
    aliases = {}
    if prev is not None:
        in_specs.append(ANY)
        args.append(prev)
        aliases = {2: 0}
    return pl.pallas_call(
        body, name=name, grid=(M // tm, N // tn, nk),
        in_specs=in_specs,
        out_specs=pl.BlockSpec((tm, tn), lambda i, j, k: (i, j + ob)),
        out_shape=jax.ShapeDtypeStruct((M, n_total), out_dtype),
        scratch_shapes=[pltpu.VMEM((tm, tn), F32)],
        input_output_aliases=aliases,
        compiler_params=_cp("parallel", "parallel", "arbitrary"),
    )(*args)


def proj_fwd(x, vec, w_in, name, tm=1024, tn=512):
    S = x.shape[0]
    tm = min(tm, S)
    nq = 1536 // tn

    def body(x_ref, vec_ref, w_ref, h_ref, qkv_ref, rest_ref, hs):
        j = pl.program_id(1)

        @pl.when(j == 0)
        def _():
            h = _pre_norm(x_ref[...], vec_ref).astype(BF16)
            hs[...] = h
            h_ref[...] = h

        r = _dot(hs[...], w_ref[...])

        @pl.when(j < nq)
        def _():
            qkv_ref[...] = r.astype(BF16)

        @pl.when(j >= nq)
        def _():
            rest_ref[...] = r

    row = lambda i, j: (i, 0)
    return pl.pallas_call(
        body, name=name, grid=(S // tm, PW // tn),
        in_specs=[pl.BlockSpec((tm, D), row), pl.BlockSpec((8, D), lambda i, j: (0, 0)),
                  pl.BlockSpec((D, tn), lambda i, j: (0, j))],
        out_specs=[pl.BlockSpec((tm, D), row),
                   pl.BlockSpec((tm, tn), lambda i, j: (i, jnp.minimum(j, nq - 1))),
                   pl.BlockSpec((tm, tn), lambda i, j: (i, jnp.maximum(j - nq, 0)))],
        out_shape=[jax.ShapeDtypeStruct((S, D), BF16), jax.ShapeDtypeStruct((S, 1536), BF16),
                   jax.ShapeDtypeStruct((S, 4096), F32)],
        scratch_shapes=[pltpu.VMEM((tm, D), BF16)],
        compiler_params=_cp("parallel", "arbitrary"),
    )(x, vec, w_in)


def proj_bwd(dq, dkv, dxr, d3, w_in, x, dxo, vec, name, tm=1024, tk=512):
    S = x.shape[0]
    tm = min(tm, S)
    nk = PW // tk

    def body(dq_ref, dkv_ref, dxr_ref, d3_ref, w_ref, x_ref, dxo_ref, vec_ref, dx_ref, vacc_ref, acc):
        i, j = pl.program_id(0), pl.program_id(1)

        @pl.when((i == 0) & (j == 0))
        def _():
            vacc_ref[...] = jnp.zeros_like(vacc_ref)

        @pl.when(j == 0)
        def _():
            acc[...] = _dot_nt(dq_ref[...], w_ref[...])

        @pl.when((j >= 1) & (j < 3))
        def _():
            acc[...] += _dot_nt(dkv_ref[...], w_ref[...])

        @pl.when((j >= 3) & (j < 5))
        def _():
            acc[...] += _dot_nt(dxr_ref[...], w_ref[...])

        @pl.when(j >= 5)
        def _():
            acc[...] += _dot_nt(d3_ref[...], w_ref[...])

        @pl.when(j == nk - 1)
        def _():
            dx_ref[...] = _pre_norm_bwd(acc[...], x_ref[...], dxo_ref[...], vec_ref, vacc_ref)

    row = lambda i, j: (i, 0)
    return pl.pallas_call(
        body, name=name, grid=(S // tm, nk),
        in_specs=[pl.BlockSpec((None, tm, tk), lambda i, j: (0, i, 0)),
                  pl.BlockSpec((None, tm, tk), lambda i, j: (jnp.clip(j - 1, 0, 1), i, 0)),
                  pl.BlockSpec((tm, tk), lambda i, j: (i, jnp.clip(j - 3, 0, 1))),
                  pl.BlockSpec((None, tm, tk), lambda i, j: (jnp.clip(j - 5, 0, 5) // 2, i, jnp.clip(j - 5, 0, 5) % 2)),
                  pl.BlockSpec((D, tk), lambda i, j: (0, j)),
                  pl.BlockSpec((tm, D), row), pl.BlockSpec((tm, D), row),
                  pl.BlockSpec((8, D), lambda i, j: (0, 0))],
        out_specs=[pl.BlockSpec((tm, D), row), pl.BlockSpec((8, D), lambda i, j: (0, 0))],
        out_shape=[jax.ShapeDtypeStruct((S, D), F32), jax.ShapeDtypeStruct((8, D), F32)],
        scratch_shapes=[pltpu.VMEM((tm, D), F32)],
        compiler_params=_cp("arbitrary", "arbitrary"),
    )(dq, dkv, dxr, d3, w_in, x, dxo, vec)


def _two_heads(v, lane):
    zero = jnp.zeros((), v.dtype)
    return jnp.concatenate([jnp.where(lane < 64, v, zero), jnp.where(lane >= 64, v, zero)], axis=0)


def _attn_probs(qm, ka, bias_h, i, grp):
    s = _dot_nt(qm, ka) + bias_h
    col = lax.broadcasted_iota(jnp.int32, s.shape, 1)
    first_key = jnp.where(i == 0, 512 - 128 * grp, 0)
    s = jnp.where(col >= first_key, s, NEG)
    e = jnp.exp(s - jnp.max(s, axis=-1, keepdims=True))
    return e * (1.0 / jnp.sum(e, axis=-1, keepdims=True))


def attn_fwd(qkv, bias, name):
    S = qkv.shape[0]
    nb = S // TQ

    def body(q_ref, kp_ref, kc_ref, vp_ref, vc_ref, b_ref, o_ref, kw, vw):
        i = pl.program_id(1)
        kw[0:TQ, :] = kp_ref[...]
        kw[TQ:2 * TQ, :] = kc_ref[...]
        vw[0:TQ, :] = vp_ref[...]
        vw[TQ:2 * TQ, :] = vc_ref[...]
        lane = lax.broadcasted_iota(jnp.int32, (1, HP), 1)

        def group(a, carry):
            r0 = pl.multiple_of(a * 128, 128)
            qa = q_ref[pl.ds(r0, 128), :] * jnp.asarray(0.125, BF16)
            ka = kw[pl.ds(r0, WIN), :]
            va = vw[pl.ds(r0, WIN), :]
            p = _attn_probs(_two_heads(qa, lane), ka, b_ref[...], i, a)
            o2 = _dot(p.astype(BF16), va)
            o_ref[pl.ds(r0, 128), :] = jnp.where(lane < 64, o2[0:128], o2[128:256]).astype(BF16)
            return carry

        lax.fori_loop(0, TQ // 128, group, 0, unroll=True)

    prev = lambda h, i: (jnp.maximum(i - 1, 0), 0)
    return pl.pallas_call(
        body, name=name, grid=(4, nb),
        in_specs=[pl.BlockSpec((TQ, HP), lambda h, i: (i, h)),
                  pl.BlockSpec((TQ, HP), lambda h, i: (jnp.maximum(i - 1, 0), 4 + h)),
                  pl.BlockSpec((TQ, HP), lambda h, i: (i, 4 + h)),
                  pl.BlockSpec((TQ, HP), lambda h, i: (jnp.maximum(i - 1, 0), 8 + h)),
                  pl.BlockSpec((TQ, HP), lambda h, i: (i, 8 + h)),
                  pl.BlockSpec((None, 256, WIN), lambda h, i: (h, 0, 0))],
        out_specs=pl.BlockSpec((TQ, HP), lambda h, i: (i, h)),
        out_shape=jax.ShapeDtypeStruct((S, 512), BF16),
        scratch_shapes=[pltpu.VMEM((2 * TQ, HP), BF16), pltpu.VMEM((2 * TQ, HP), BF16)],
        compiler_params=_cp("parallel", "arbitrary"),
    )(qkv, qkv, qkv, qkv, qkv, bias)


def attn_bwd(qkv, do, bias, name):
    S = qkv.shape[0]
    nb = S // TQ

    def body(q_ref, kp_ref, kc_ref, vp_ref, vc_ref, do_ref, b_ref, dqkv_ref, db_ref, dkv_ref, kw, vw, ak, av):
        i = pl.program_id(1)

        @pl.when(i == 0)
        def _():
            db_ref[...] = jnp.zeros_like(db_ref)
            ak[...] = jnp.zeros_like(ak)
            av[...] = jnp.zeros_like(av)

        @pl.when(i > 0)
        def _():
            ak[0:TQ, :] = ak[TQ:2 * TQ, :]
            av[0:TQ, :] = av[TQ:2 * TQ, :]
            ak[TQ:2 * TQ, :] = jnp.zeros((TQ, HP), F32)
            av[TQ:2 * TQ, :] = jnp.zeros((TQ, HP), F32)

        @pl.when(i < nb)
        def _():
            kw[0:TQ, :] = kp_ref[...]
            kw[TQ:2 * TQ, :] = kc_ref[...]
            vw[0:TQ, :] = vp_ref[...]
            vw[TQ:2 * TQ, :] = vc_ref[...]
            lane = lax.broadcasted_iota(jnp.int32, (1, HP), 1)

            def group(a, carry):
                r0 = pl.multiple_of(a * 128, 128)
                q2 = _two_heads(q_ref[pl.ds(r0, 128), :] * jnp.asarray(0.125, BF16), lane)
                do2 = _two_heads(do_ref[pl.ds(r0, 128), :], lane)
                ka = kw[pl.ds(r0, WIN), :]
                va = vw[pl.ds(r0, WIN), :]
                p = _attn_probs(q2, ka, b_ref[...], i, a)
                dp = _dot_nt(do2, va)
                ds = p * (dp - jnp.sum(p * dp, axis=-1, keepdims=True))
                db_ref[...] += ds
                dsb = ds.astype(BF16)
                dq2 = _dot(dsb, ka)
                ak[pl.ds(r0, WIN), :] += _dot_tn(dsb, q2)
                av[pl.ds(r0, WIN), :] += _dot_tn(p.astype(BF16), do2)
                dq = jnp.where(lane < 64, dq2[0:128], dq2[128:256])
                dqkv_ref[0, pl.ds(r0, 128), :] = (dq * 0.125).astype(BF16)
                return carry

            lax.fori_loop(0, TQ // 128, group, 0, unroll=True)

        @pl.when(i > 0)
        def _():
            dkv_ref[0] = ak[0:TQ, :].astype(BF16)
            dkv_ref[1] = av[0:TQ, :].astype(BF16)

    cur = lambda i: jnp.minimum(i, nb - 1)
    prv = lambda i: jnp.clip(i - 1, 0, nb - 1)
    dq, db, dkv = pl.pallas_call(
        body, name=name, grid=(4, nb + 1),
        in_specs=[pl.BlockSpec((TQ, HP), lambda h, i: (cur(i), h)),
                  pl.BlockSpec((TQ, HP), lambda h, i: (prv(i), 4 + h)),
                  pl.BlockSpec((TQ, HP), lambda h, i: (cur(i), 4 + h)),
                  pl.BlockSpec((TQ, HP), lambda h, i: (prv(i), 8 + h)),
                  pl.BlockSpec((TQ, HP), lambda h, i: (cur(i), 8 + h)),
                  pl.BlockSpec((TQ, HP), lambda h, i: (cur(i), h)),
                  pl.BlockSpec((None, 256, WIN), lambda h, i: (h, 0, 0))],
        out_specs=[pl.BlockSpec((1, TQ, HP), lambda h, i: (0, cur(i), h)),
                   pl.BlockSpec((None, 256, WIN), lambda h, i: (h, 0, 0)),
                   pl.BlockSpec((2, TQ, HP), lambda h, i: (0, prv(i), h))],
        out_shape=[jax.ShapeDtypeStruct((1, S, 512), BF16), jax.ShapeDtypeStruct((4, 256, WIN), F32),
                   jax.ShapeDtypeStruct((2, S, 512), BF16)],
        scratch_shapes=[pltpu.VMEM((2 * TQ, HP), BF16), pltpu.VMEM((2 * TQ, HP), BF16),
                        pltpu.VMEM((2 * TQ, HP), F32), pltpu.VMEM((2 * TQ, HP), F32)],
        compiler_params=_cp("parallel", "arbitrary"),
    )(qkv, qkv, qkv, qkv, qkv, do, bias)
    return dq, db, dkv


def bias_grad(db, name):
    def body(db_ref, o_ref):
        r = lax.broadcasted_iota(jnp.int32, (128, 128), 0)
        c = lax.broadcasted_iota(jnp.int32, (128, 128), 1)
        flip = (r + c == 127).astype(BF16)
        lane = lax.broadcasted_iota(jnp.int32, (16, 384), 1)
        src = lax.broadcasted_iota(jnp.int32, (128, 384), 0)
        dst = lax.broadcasted_iota(jnp.int32, (128, 384), 1)

        def split_dot(v, m):
            hi = v.astype(BF16)
            r1 = v - hi.astype(F32)
            mid = r1.astype(BF16)
            lo = (r1 - mid.astype(F32)).astype(BF16)
            return _dot(hi, m) + _dot(mid, m) + _dot(lo, m)

        def diag_sums(w):
            y = pltpu.roll(split_dot(w, flip), 0, 1, stride=1, stride_axis=0)
            return jnp.broadcast_to(_colsum(y), (16, 128))

        w4 = db_ref[0, :, 512:640]
        w3 = db_ref[0, :, 384:512]
        far = jnp.sum(db_ref[0, :, 0:384]) + jnp.sum(jnp.where(r >= c, w3, 0.0))
        lo4 = diag_sums(jnp.where(r >= c, w4, 0.0))
        up4 = diag_sums(jnp.where(r < c, w4, 0.0))
        up3 = diag_sums(jnp.where(r < c, w3, 0.0))
        p_lo4 = (dst == 128 + (src + 1) % 128).astype(BF16)
        p_up4 = ((dst == src + 1) & (src < 127)).astype(BF16)
        p_up3 = ((dst == src + 129) & (src < 127)).astype(BF16)
        out = split_dot(lo4, p_lo4) + split_dot(up4, p_up4) + split_dot(up3, p_up3)
        o_ref[0] = out + jnp.where(lane == 256, far, 0.0)

    return pl.pallas_call(
        body, name=name, grid=(8,),
        in_specs=[pl.BlockSpec((1, 128, WIN), lambda h: (h, 0, 0))],
        out_specs=pl.BlockSpec((1, 16, 384), lambda h: (h, 0, 0)),
        out_shape=jax.ShapeDtypeStruct((8, 16, 384), F32),
        compiler_params=_cp("parallel"),
    )(db)[:, 0, :]


LT = 256
LC = 512


def _lru_gates(xs, pv_ref, wa_ref, wx_ref, tl):
    xc = (pv_ref[4:5, :] + pv_ref[3:4, :] * xs[pl.ds(8, tl), :] + pv_ref[2:3, :] * xs[pl.ds(7, tl), :]
          + pv_ref[1:2, :] * xs[pl.ds(6, tl), :] + pv_ref[0:1, :] * xs[pl.ds(5, tl), :])
    xcb = xc.astype(BF16)
    pa = jnp.concatenate([_dot(xcb[:, 0:256], wa_ref[0]), _dot(xcb[:, 256:512], wa_ref[1])], axis=1)
    px = jnp.concatenate([_dot(xcb[:, 0:256], wx_ref[0]), _dot(xcb[:, 256:512], wx_ref[1])], axis=1)
    r = _sigmoid(pa + pv_ref[5:6, :])
    ig = _sigmoid(px + pv_ref[6:7, :])
    z = -pv_ref[7:8, :]
    sp = jnp.maximum(z, 0.0) + jnp.log1p(jnp.exp(-jnp.abs(z)))
    log_a = (-LRU_C * r) * sp
    a = jnp.exp(log_a)
    s = jnp.tanh(-log_a) * (1.0 + a * a)
    inv_mult = lax.rsqrt(s)
    mult = jnp.where(s > 0.0, s * inv_mult, 0.0)
    return xc, xcb, r, ig, sp, a, mult, inv_mult


def lru_fwd(rest, pvec, wa, wx, name):
    S = rest.shape[0]
    tl = min(LT, S)
    nt = S // tl

    def body(xr_ref, halo_ref, yr_ref, pv_ref, wa_ref, wx_ref, h_ref, hg_ref, xs, a_s, u_s, h_s, carry):
        ti = pl.program_id(1)

        @pl.when(ti == 0)
        def _():
            carry[...] = jnp.zeros_like(carry)

        xs[0:8, :] = jnp.where(ti > 0, halo_ref[...], 0.0)
        xs[pl.ds(8, tl), :] = xr_ref[...]
        xc, _, _, ig, _, a, mult, _ = _lru_gates(xs, pv_ref, wa_ref, wx_ref, tl)
        a_s[...] = a
        u_s[...] = mult * (ig * xc)
        row = lax.broadcasted_iota(jnp.int32, (8, LC), 0)

        def blk(bi, c):
            o = pl.multiple_of(bi * 8, 8)
            av = a_s[pl.ds(o, 8), :]
            bv = u_s[pl.ds(o, 8), :]
            for d in (1, 2, 4):
                a_sh = pltpu.roll(av, d, 0)
                b_sh = pltpu.roll(bv, d, 0)
                m = row >= d
                bv = jnp.where(m, av * b_sh + bv, bv)
                av = jnp.where(m, av * a_sh, av)
            hv = bv + av * c
            h_s[pl.ds(o, 8), :] = hv
            return hv[7:8, :]

        carry[...] = lax.fori_loop(0, tl // 8, blk, carry[...])
        h = h_s[...]
        h_ref[...] = h
        hg_ref[...] = (h * _gelu(yr_ref[...])).astype(BF16)

    hb = tl // 8
    return pl.pallas_call(
        body, name=name, grid=(2, nt),
        in_specs=[pl.BlockSpec((tl, LC), lambda c, t: (t, c)),
                  pl.BlockSpec((8, LC), lambda c, t: (jnp.maximum(t * hb - 1, 0), c)),
                  pl.BlockSpec((tl, LC), lambda c, t: (t, 2 + c)),
                  pl.BlockSpec((8, LC), lambda c, t: (0, c)),
                  pl.BlockSpec((2, 256, 256), lambda c, t: (c, 0, 0)),
                  pl.BlockSpec((2, 256, 256), lambda c, t: (c, 0, 0))],
        out_specs=[pl.BlockSpec((tl, LC), lambda c, t: (t, c)), pl.BlockSpec((tl, LC), lambda c, t: (t, c))],
        out_shape=[jax.ShapeDtypeStruct((S, D), F32), jax.ShapeDtypeStruct((S, D), BF16)],
        scratch_shapes=[pltpu.VMEM((tl + 8, LC), F32), pltpu.VMEM((tl, LC), F32), pltpu.VMEM((tl, LC), F32),
                        pltpu.VMEM((tl, LC), F32), pltpu.VMEM((1, LC), F32)],
        compiler_params=_cp("parallel", "arbitrary"),
    )(rest, rest, rest, pvec, wa, wx)


def lru_bwd(dh, h, rest, pvec, wa, wx, name):
    S = rest.shape[0]
    tl = min(LT, S)
    nt = S // tl

    def body(dh_ref, h_ref, hhalo_ref, xr_ref, xhalo_ref, pv_ref, wa_ref, wx_ref,
             dxr_ref, vacc_ref, dwa_ref, dwx_ref,
             xs, hs, a_s, ash_s, b_s, lam_s, dxe, anext, lnext, dxnext):
        ti = pl.program_id(1)
        tr = nt - 1 - ti

        @pl.when(ti == 0)
        def _():
            anext[...] = jnp.zeros_like(anext)
            lnext[...] = jnp.zeros_like(lnext)
            dxnext[...] = jnp.zeros_like(dxnext)
            vacc_ref[...] = jnp.zeros_like(vacc_ref)
            dwa_ref[...] = jnp.zeros_like(dwa_ref)
            dwx_ref[...] = jnp.zeros_like(dwx_ref)

        xs[0:8, :] = jnp.where(tr > 0, xhalo_ref[...], 0.0)
        xs[pl.ds(8, tl), :] = xr_ref[...]
        xc, xcb, r, ig, sp, a, mult, inv_mult = _lru_gates(xs, pv_ref, wa_ref, wx_ref, tl)

        a_s[pl.ds(0, tl), :] = a
        a_s[pl.ds(tl, 8), :] = jnp.broadcast_to(anext[...], (8, LC))
        ash_s[...] = a_s[pl.ds(1, tl), :]
        b_s[...] = dh_ref[...]
        row = lax.broadcasted_iota(jnp.int32, (8, LC), 0)

        def blk(k, c):
            o = pl.multiple_of((tl // 8 - 1 - k) * 8, 8)
            av = ash_s[pl.ds(o, 8), :]
            bv = b_s[pl.ds(o, 8), :]
            for d in (1, 2, 4):
                a_sh = pltpu.roll(av, 8 - d, 0)
                b_sh = pltpu.roll(bv, 8 - d, 0)
                m = row < 8 - d
                bv = jnp.where(m, bv + av * b_sh, bv)
                av = jnp.where(m, av * a_sh, av)
            lv = bv + av * c
            lam_s[pl.ds(o, 8), :] = lv
            return lv[0:1, :]

        lnext[...] = lax.fori_loop(0, tl // 8, blk, lnext[...])
        anext[...] = a[0:1, :]
        lam = lam_s[...]

        hs[0:8, :] = jnp.where(tr > 0, hhalo_ref[...], 0.0)
        hs[pl.ds(8, tl), :] = h_ref[...]
        d_a = lam * hs[pl.ds(7, tl), :]
        d_mult = lam * (ig * xc)
        d_ig = lam * mult * xc
        dxc = lam * mult * ig
        d_log_a = d_a * a - d_mult * (a * a) * inv_mult
        d_r = d_log_a * (-LRU_C * sp)
        vacc_ref[7:8, :] += _colsum(d_log_a * (-LRU_C * r)) * (-_sigmoid(-pv_ref[7:8, :]))
        d_pa = d_r * r * (1.0 - r)
        d_px = d_ig * ig * (1.0 - ig)
        vacc_ref[5:6, :] += _colsum(d_pa)
        vacc_ref[6:7, :] += _colsum(d_px)
        dpa = d_pa.astype(BF16)
        dpx = d_px.astype(BF16)
        back = []
        for g in range(2):
            sl = slice(256 * g, 256 * g + 256)
            dwa_ref[g] += _dot_tn(xcb[:, sl], dpa[:, sl])
            dwx_ref[g] += _dot_tn(xcb[:, sl], dpx[:, sl])
            back.append(_dot_nt(dpa[:, sl], wa_ref[g]) + _dot_nt(dpx[:, sl], wx_ref[g]))
        dxc = dxc + jnp.concatenate(back, axis=1)
        vacc_ref[4:5, :] += _colsum(dxc)
        for k in range(4):
            vacc_ref[k:k + 1, :] += _colsum(dxc * xs[pl.ds(5 + k, tl), :])
        dxe[pl.ds(0, tl), :] = dxc
        dxe[pl.ds(tl, 8), :] = dxnext[...]
        dxr = (pv_ref[3:4, :] * dxc + pv_ref[2:3, :] * dxe[pl.ds(1, tl), :]
               + pv_ref[1:2, :] * dxe[pl.ds(2, tl), :] + pv_ref[0:1, :] * dxe[pl.ds(3, tl), :])
        dxr_ref[...] = dxr.astype(BF16)
        dxnext[...] = dxc[0:8, :]

    hb = tl // 8
    rev = lambda t: nt - 1 - t
    halo = lambda t: jnp.maximum(rev(t) * hb - 1, 0)
    big = lambda: pltpu.VMEM((tl + 8, LC), F32)
    til = lambda: pltpu.VMEM((tl, LC), F32)
    return pl.pallas_call(
        body, name=name, grid=(2, nt),
        in_specs=[pl.BlockSpec((tl, LC), lambda c, t: (rev(t), c)),
                  pl.BlockSpec((tl, LC), lambda c, t: (rev(t), c)),
                  pl.BlockSpec((8, LC), lambda c, t: (halo(t), c)),
                  pl.BlockSpec((tl, LC), lambda c, t: (rev(t), c)),
                  pl.BlockSpec((8, LC), lambda c, t: (halo(t), c)),
                  pl.BlockSpec((8, LC), lambda c, t: (0, c)),
                  pl.BlockSpec((2, 256, 256), lambda c, t: (c, 0, 0)),
                  pl.BlockSpec((2, 256, 256), lambda c, t: (c, 0, 0))],
        out_specs=[pl.BlockSpec((tl, LC), lambda c, t: (rev(t), c)),
                   pl.BlockSpec((8, LC), lambda c, t: (0, c)),
                   pl.BlockSpec((2, 256, 256), lambda c, t: (c, 0, 0)),
                   pl.BlockSpec((2, 256, 256), lambda c, t: (c, 0, 0))],
        out_shape=[jax.ShapeDtypeStruct((S, D), BF16), jax.ShapeDtypeStruct((8, D), F32),
                   jax.ShapeDtypeStruct((4, 256, 256), F32), jax.ShapeDtypeStruct((4, 256, 256), F32)],
        scratch_shapes=[big(), big(), big(), til(), til(), til(), big(),
                        pltpu.VMEM((1, LC), F32), pltpu.VMEM((1, LC), F32), pltpu.VMEM((8, LC), F32)],
        compiler_params=_cp("parallel", "arbitrary"),
    )(dh, h, h, rest, rest, pvec, wa, wx)


def mix_out_fwd(x, ao, hg, rest, vec, w_att_o, w_rec_o, w_out, name, tm=256):
    S = x.shape[0]
    tm = min(tm, S)

    def body(x_ref, ao_ref, hg_ref, ga_ref, gr_ref, vec_ref, wa_ref, wr_ref, wo_ref,
             xo_ref, att_ref, rec_ref, mg_ref, f_ref):
        att = _dot(ao_ref[...], wa_ref[...])
        rec = _dot(hg_ref[...], wr_ref[...])
        att_ref[...] = att
        rec_ref[...] = rec
        mg = (_sigmoid(ga_ref[...]) * att + _sigmoid(gr_ref[...]) * rec).astype(BF16)
        mg_ref[...] = mg
        f = _dot(mg, wo_ref[...])
        f_ref[...] = f
        y = f * lax.rsqrt(_mean(f * f) + EPS) * vec_ref[1:2, :]
        xo_ref[...] = x_ref[...] + (1.0 * vec_ref[4:5, :]) * y

    row = lambda i: (i, 0)
    full = lambda r: pl.BlockSpec((r, D), lambda i: (0, 0))
    return pl.pallas_call(
        body, name=name, grid=(S // tm,),
        in_specs=[pl.BlockSpec((tm, D), row), pl.BlockSpec((tm, 512), row), pl.BlockSpec((tm, D), row),
                  pl.BlockSpec((tm, D), lambda i: (i, 2)), pl.BlockSpec((tm, D), lambda i: (i, 3)),
                  full(8), full(512), full(D), full(D)],
        out_specs=[pl.BlockSpec((tm, D), row)] * 5,
        out_shape=[jax.ShapeDtypeStruct((S, D), F32), jax.ShapeDtypeStruct((S, D), F32),
                   jax.ShapeDtypeStruct((S, D), F32), jax.ShapeDtypeStruct((S, D), BF16),
                   jax.ShapeDtypeStruct((S, D), F32)],
        compiler_params=_cp("parallel"),
    )(x, ao, hg, rest, rest, vec, w_att_o, w_rec_o, w_out)


def mix_out_bwd(dxo, f, att, rec, rest, h, vec, w_att_o, w_rec_o, w_out, name, tm=256):
    S = dxo.shape[0]
    tm = min(tm, S)

    def body(dxo_ref, f_ref, att_ref, rec_ref, yr_ref, ga_ref, gr_ref, h_ref, vec_ref, wa_ref, wr_ref, wo_ref,
             df_ref, da_ref, dr_ref, dao_ref, dh_ref, d3_ref, vacc_ref):
        @pl.when(pl.program_id(0) == 0)
        def _():
            vacc_ref[...] = jnp.zeros_like(vacc_ref)

        df = _post_norm_bwd(dxo_ref[...], f_ref[...], 1.0, vec_ref, vacc_ref).astype(BF16)
        df_ref[...] = df
        dm = _dot_nt(df, wo_ref[...])
        sa = _sigmoid(ga_ref[...])
        sr = _sigmoid(gr_ref[...])
        d_att = (dm * sa).astype(BF16)
        d_rec = (dm * sr).astype(BF16)
        da_ref[...] = d_att
        dr_ref[...] = d_rec
        d3_ref[1] = (dm * att_ref[...] * (sa * (1.0 - sa))).astype(BF16)
        d3_ref[2] = (dm * rec_ref[...] * (sr * (1.0 - sr))).astype(BF16)
        dao_ref[...] = _dot_nt(d_att, wa_ref[...]).astype(BF16)
        d_hg = _dot_nt(d_rec, wr_ref[...])
        yr = yr_ref[...]
        t = jnp.tanh(_GK * (yr + 0.044715 * yr * yr * yr))
        dh_ref[...] = d_hg * (0.5 * yr * (1.0 + t))
        gelu_grad = 0.5 * (1.0 + t) + 0.5 * yr * (1.0 - t * t) * _GK * (1.0 + 3.0 * 0.044715 * yr * yr)
        d3_ref[0] = (d_hg * h_ref[...] * gelu_grad).astype(BF16)

    row = lambda i: (i, 0)
    full = lambda r: pl.BlockSpec((r, D), lambda i: (0, 0))
    return pl.pallas_call(
        body, name=name, grid=(S // tm,),
        in_specs=[pl.BlockSpec((tm, D), row)] * 4
        + [pl.BlockSpec((tm, D), lambda i: (i, 1)), pl.BlockSpec((tm, D), lambda i: (i, 2)),
           pl.BlockSpec((tm, D), lambda i: (i, 3)), pl.BlockSpec((tm, D), row),
           full(8), full(512), full(D), full(D)],
        out_specs=[pl.BlockSpec((tm, D), row)] * 3
        + [pl.BlockSpec((tm, 512), row), pl.BlockSpec((tm, D), row),
           pl.BlockSpec((3, tm, D), lambda i: (0, i, 0)), pl.BlockSpec((8, D), lambda i: (0, 0))],
        out_shape=[jax.ShapeDtypeStruct((S, D), BF16)] * 3
        + [jax.ShapeDtypeStruct((S, 512), BF16), jax.ShapeDtypeStruct((S, D), F32),
           jax.ShapeDtypeStruct((3, S, D), BF16), jax.ShapeDtypeStruct((8, D), F32)],
        compiler_params=_cp("arbitrary"),
    )(dxo, f, att, rec, rest, rest, rest, h, vec, w_att_o, w_rec_o, w_out)


def dw_in(h, dq, dkv, dxr, d3, name, tk=1024, tn=512):
    S = h.shape[0]
    tk = min(tk, S)
    nk = S // tk

    def body(h_ref, dq_ref, dkv_ref, dxr_ref, d3_ref, o_ref, acc):
        j, k = pl.program_id(0), pl.program_id(1)

        @pl.when(k == 0)
        def _():
            acc[...] = jnp.zeros_like(acc)

        @pl.when(j == 0)
        def _():
            acc[...] += _dot_tn(h_ref[...], dq_ref[...])

        @pl.when((j >= 1) & (j < 3))
        def _():
            acc[...] += _dot_tn(h_ref[...], dkv_ref[...])

        @pl.when((j >= 3) & (j < 5))
        def _():
            acc[...] += _dot_tn(h_ref[...], dxr_ref[...])

        @pl.when(j >= 5)
        def _():
            acc[...] += _dot_tn(h_ref[...], d3_ref[...])

        @pl.when(k == nk - 1)
        def _():
            o_ref[...] = acc[...].astype(BF16)

    use = lambda j, k, lo, hi: jnp.where((j >= lo) & (j < hi), k, 0)
    g3 = lambda j: jnp.clip(j - 5, 0, 5)
    return pl.pallas_call(
        body, name=name, grid=(PW // tn, nk),
        in_specs=[pl.BlockSpec((tk, D), lambda j, k: (k, 0)),
                  pl.BlockSpec((None, tk, tn), lambda j, k: (0, use(j, k, 0, 1), 0)),
                  pl.BlockSpec((None, tk, tn), lambda j, k: (jnp.clip(j - 1, 0, 1), use(j, k, 1, 3), 0)),
                  pl.BlockSpec((tk, tn), lambda j, k: (use(j, k, 3, 5), jnp.clip(j - 3, 0, 1))),
                  pl.BlockSpec((None, tk, tn), lambda j, k: (g3(j) // 2, use(j, k, 5, 11), g3(j) % 2))],
        out_specs=pl.BlockSpec((D, tn), lambda j, k: (0, j)),
        out_shape=jax.ShapeDtypeStruct((D, PW), BF16),
        scratch_shapes=[pltpu.VMEM((D, tn), F32)],
        compiler_params=_cp("parallel", "arbitrary"),
    )(h, dq, dkv, dxr, d3)


def ada_fwd(c_all, w_ada, b_ada, name, tn=768):
    n = w_ada.shape[1]

    def body(c_ref, w_ref, b_ref, o_ref):
        cv = c_ref[...]
        ca = (cv * _sigmoid(cv)).astype(BF16)
        o_ref[...] = _dot(ca, w_ref[...].astype(BF16)) + b_ref[...]

    return pl.pallas_call(
        body, name=name, grid=(n // tn,),
        in_specs=[pl.BlockSpec((8, D), lambda j: (0, 0)), pl.BlockSpec((D, tn), lambda j: (0, j)),
                  pl.BlockSpec((1, tn), lambda j: (0, j))],
        out_specs=pl.BlockSpec((8, tn), lambda j: (0, j)),
        out_shape=jax.ShapeDtypeStruct((8, n), F32),
        compiler_params=_cp("parallel"),
    )(c_all, w_ada, b_ada)


def ada_bwd(c_all_t, dmod, name, tn=768):
    n = dmod.shape[1]

    def body(c_ref, d_ref, o_ref):
        cv = c_ref[...]
        ca = (cv * _sigmoid(cv)).astype(BF16)
        o_ref[...] = _dot(ca, d_ref[...].astype(BF16))

    return pl.pallas_call(
        body, name=name, grid=(n // tn,),
        in_specs=[pl.BlockSpec((D, 128), lambda j: (0, 0)), pl.BlockSpec((128, tn), lambda j: (0, j))],
        out_specs=pl.BlockSpec((D, tn), lambda j: (0, j)),
        out_shape=jax.ShapeDtypeStruct((D, n), F32),
        compiler_params=_cp("parallel"),
    )(c_all_t, dmod)


def _row_tile(rows, cols, itemsize=4, budget=1536 * 1024):
    best = None
    for t in range(8, rows + 1, 8):
        if rows % t == 0 and t * cols * itemsize <= budget:
            best = t
    return rows if best is None else best


def sum_lead(parts, name, out_dtype=F32):
    n, R, C = parts.shape
    tr = _row_tile(R, C * n)

    def body(p_ref, o_ref):
        acc = p_ref[0].astype(F32)
        for k in range(1, n):
            acc = acc + p_ref[k].astype(F32)
        o_ref[...] = acc.astype(out_dtype)

    return pl.pallas_call(
        body, name=name, grid=(R // tr,),
        in_specs=[pl.BlockSpec((n, tr, C), lambda i: (0, i, 0))],
        out_specs=pl.BlockSpec((tr, C), lambda i: (i, 0)),
        out_shape=jax.ShapeDtypeStruct((R, C), out_dtype),
        compiler_params=_cp("parallel"),
    )(parts)


def adamw(w, g, m, v, name):
    R, C = w.shape
    tr = _row_tile(R, C * 7, budget=8 * 1024 * 1024)

    def body(w_ref, g_ref, m_ref, v_ref, d_ref, mo_ref, vo_ref):
        gv = g_ref[...]
        mn = ADAM_B1 * m_ref[...] + (1.0 - ADAM_B1) * gv
        vn = ADAM_B2 * v_ref[...] + (1.0 - ADAM_B2) * (gv * gv)
        m_hat = mn / (1.0 - ADAM_B1 ** ADAM_STEP)
        v_hat = vn / (1.0 - ADAM_B2 ** ADAM_STEP)
        d_ref[...] = -ADAM_LR * (m_hat / (jnp.sqrt(v_hat) + ADAM_EPS) + ADAM_WD * w_ref[...])
        mo_ref[...] = mn
        vo_ref[...] = vn

    spec = pl.BlockSpec((tr, C), lambda i: (i, 0))
    return pl.pallas_call(
        body, name=name, grid=(R // tr,),
        in_specs=[spec] * 4, out_specs=[spec] * 3,
        out_shape=[jax.ShapeDtypeStruct((R, C), F32)] * 3,
        compiler_params=_cp("parallel"),
    )(w, g, m, v)


def _mesh_pos():
    return lax.axis_index("x"), lax.axis_index("y"), lax.axis_index("c")


def _other_chips(mx, my):
    return [(1 - mx, my), (mx, 1 - my), (1 - mx, 1 - my)]


def ag_small(x, name):
    R = x.shape[0]

    def body(x_ref, out_ref, send_sems, recv_sems, local_sem):
        mx, my, mc = _mesh_pos()
        me, sibling = (mx, my, mc), (mx, my, 1 - mc)
        chips = _other_chips(mx, my)

        def slot(px, py, pc):
            return out_ref.at[4 * px + 2 * py + pc]

        def copy(k, block, to, src=None):
            return pltpu.make_async_remote_copy(
                src_ref=slot(*block) if src is None else src, dst_ref=slot(*block),
                send_sem=send_sems.at[k], recv_sem=recv_sems.at[k], device_id=to, device_id_type=MESH)

        mine = pltpu.make_async_copy(x_ref, slot(*me), local_sem)
        mine.start()
        first = [copy(0, me, sibling, src=x_ref)]
        first += [copy(1 + j, me, (*chip, mc), src=x_ref) for j, chip in enumerate(chips)]
        for cp in first:
            cp.start()
        passed = [copy(4 + j, (*chip, mc), sibling) for j, chip in enumerate(chips)]
        for j, chip in enumerate(chips):
            copy(1 + j, (*chip, mc), me).wait_recv()
            passed[j].start()
        copy(0, sibling, me).wait_recv()
        for j, chip in enumerate(chips):
            copy(4 + j, (*chip, 1 - mc), me).wait_recv()
        for cp in first + passed:
            cp.wait_send()
        mine.wait()

    return pl.pallas_call(
        body, name=name,
        out_shape=jax.ShapeDtypeStruct((N_DEV, R, 128), F32),
        in_specs=[pl.BlockSpec(memory_space=pltpu.VMEM)],
        out_specs=pl.BlockSpec(memory_space=pltpu.VMEM),
        scratch_shapes=[pltpu.SemaphoreType.DMA((7,)), pltpu.SemaphoreType.DMA((7,)), pltpu.SemaphoreType.DMA],
        compiler_params=pltpu.CompilerParams(vmem_limit_bytes=VMEM_LIMIT),
    )(x)


BIG = (("ffn1_w_gu", "col", D, PW), ("ffn1_w_down", "row", FF, D), ("w_in", "col", D, PW),
       ("w_att_o", "col", 512, D), ("w_rec_o", "row", D, D), ("w_out", "row", D, D),
       ("ffn2_w_gu", "col", D, PW), ("ffn2_w_down", "row", FF, D))
NBIG = len(BIG)


def _shard_shape(kind, R, C):
    return (R, C // 4) if kind == "col" else (R // 4, C)


def _region(ref, kind, R, C, q, half, t, tr):
    sr, sc = _shard_shape(kind, R, C)
    if kind == "col":
        return ref.at[pl.ds(pl.multiple_of(half * (R // 2) + t * tr, 16), tr), pl.ds(q * sc, sc)]
    return ref.at[pl.ds(pl.multiple_of(q * sr + t * tr, 16), tr), pl.ds(half * (C // 2), C // 2)]


def ag_local(w, kind, R, C, p_arr, name, after=()):
    sr, sc = _shard_shape(kind, R, C)
    tr = _row_tile(sr, sc, budget=2 * 1024 * 1024)
    nt = sr // tr
    after = list(after)

    def body(p_ref, w_ref, *rest):
        rest[-1][...] = w_ref[...].astype(BF16)

    if kind == "col":
        o_spec = pl.BlockSpec((tr, sc), lambda i, p: (i, p[0]))
    else:
        o_spec = pl.BlockSpec((tr, sc), lambda i, p: (p[0] * nt + i, 0))
    return pl.pallas_call(
        body, name=name,
        grid_spec=pltpu.PrefetchScalarGridSpec(
            num_scalar_prefetch=1, grid=(nt,),
            in_specs=[pl.BlockSpec((tr, sc), lambda i, p: (i, 0))] + [ANY] * len(after), out_specs=o_spec),
        out_shape=jax.ShapeDtypeStruct((R, C), BF16),
        compiler_params=_cp("parallel"),
    )(p_arr, w, *after)


HBM_SPEC = pl.BlockSpec(memory_space=pltpu.HBM)
SEM_SPEC = pl.BlockSpec(memory_space=pltpu.SEMAPHORE)


def _ag_sems(geoms):
    return sum(6 if both else 3 for (_, _, _, both) in geoms)


def _ag_copies(fulls, geoms, ssem, rsem, mx, my, mc, q, h):
    chips = _other_chips(mx, my)
    out, base = [], 0
    for w, (kind, R, C, both) in enumerate(geoms):
        sr, sc = _shard_shape(kind, R, C)
        hr = sr // 2 if kind == "col" else sr
        reg = _region(fulls[w], kind, R, C, q, h, 0, hr)
        out.append([pltpu.make_async_remote_copy(
            src_ref=reg, dst_ref=reg, send_sem=ssem.at[base + 3 * t + k], recv_sem=rsem.at[base + 3 * t + k],
            device_id=(*chips[k], mc if t == 0 else 1 - mc), device_id_type=MESH)
            for t in range(2 if both else 1) for k in range(3)])
        base += 6 if both else 3
    return out


def ag_start(fulls, geoms, after, name):
    n = len(fulls)
    after = list(after)
    m = len(after)

    def body(*refs):
        ssem, rsem = refs[n + m:n + m + 2]
        outs, token = refs[n + m + 2:2 * n + m + 2], refs[2 * n + m + 2]
        mx, my, mc = _mesh_pos()
        p = 2 * mx + my
        col = [w for w, g in enumerate(geoms) if g[0] == "col"]
        row = [w for w, g in enumerate(geoms) if g[0] == "row"]
        for q in range(4):
            @pl.when(p == q)
            def _(q=q):
                cps = _ag_copies(outs, geoms, ssem, rsem, mx, my, mc, q, mc)
                for w in col:
                    for cp in cps[w]:
                        cp.start()
        for h in range(2):
            @pl.when(mc == h)
            def _(h=h):
                cps = _ag_copies(outs, geoms, ssem, rsem, mx, my, mc, p, h)
                for w in row:
                    for cp in cps[w]:
                        cp.start()
        token[...] = jnp.zeros_like(token)

    res = pl.pallas_call(
        body, name=name,
        out_shape=[pltpu.SemaphoreType.DMA((_ag_sems(geoms),)), pltpu.SemaphoreType.DMA((_ag_sems(geoms),))]
        + [pltpu.HBM(a.shape, a.dtype) for a in fulls] + [jax.ShapeDtypeStruct((8, 128), F32)],
        in_specs=[HBM_SPEC] * n + [ANY] * m,
        out_specs=[SEM_SPEC, SEM_SPEC] + [HBM_SPEC] * n + [pl.BlockSpec(memory_space=pltpu.VMEM)],
        input_output_aliases={w: 2 + w for w in range(n)},
        compiler_params=pltpu.CompilerParams(has_side_effects=pltpu.SideEffectType.DATAFLOW_SIDE_EFFECTING),
    )(*[pltpu.with_memory_space_constraint(a, pltpu.HBM) for a in fulls], *after)
    return res[0], res[1], list(res[2:2 + n]), res[2 + n]


def ag_wait(fulls, geoms, ssem, rsem, after, name):
    n = len(fulls)

    def body(*refs):
        ins, ssem_ref, rsem_ref = refs[:n], refs[n], refs[n + 1]
        mx, my, mc = _mesh_pos()
        for cps in _ag_copies(ins, geoms, ssem_ref, rsem_ref, mx, my, mc, 0, 0):
            for cp in cps:
                cp.wait_send()
                cp.wait_recv()

    return list(pl.pallas_call(
        body, name=name,
        out_shape=[pltpu.HBM(a.shape, a.dtype) for a in fulls],
        in_specs=[HBM_SPEC] * n + [SEM_SPEC, SEM_SPEC, ANY],
        out_specs=[HBM_SPEC] * n,
        input_output_aliases={w: w for w in range(n)},
        compiler_params=pltpu.CompilerParams(has_side_effects=pltpu.SideEffectType.DATAFLOW_SIDE_EFFECTING),
    )(*fulls, ssem, rsem, after))


def ag_forward(full, kind, R, C, name):
    sr, sc = _shard_shape(kind, R, C)
    hr, hc = (sr // 2, sc) if kind == "col" else (sr, sc // 2)
    tr = _row_tile(hr, hc, itemsize=2, budget=512 * 1024)
    nt = hr // tr

    total = 3 * nt

    def body(src_ref, full_ref, stage, lsem, ssem, rsem):
        step = pl.program_id(0) * nt + pl.program_id(1)
        par = step % 2
        mx, my, mc = _mesh_pos()

        def load(s, q, h, t):
            return pltpu.make_async_copy(_region(src_ref, kind, R, C, q, h, t, tr), stage.at[s], lsem.at[s])

        def push(s, q, h, t):
            return pltpu.make_async_remote_copy(src_ref=stage.at[s], dst_ref=_region(full_ref, kind, R, C, q, h, t, tr),
                                                send_sem=ssem.at[s], recv_sem=rsem, device_id=(mx, my, 1 - mc),
                                                device_id_type=MESH)

        def for_tile(stp, fn):
            q_k = _partner_chip(stp // nt, 2 * mx + my)
            if kind == "col":
                for q in range(4):
                    @pl.when(q_k == q)
                    def _(q=q):
                        fn(q, mc, stp % nt)
            else:
                for h in range(2):
                    @pl.when(mc == h)
                    def _(h=h):
                        fn(q_k, h, stp % nt)

        @pl.when(step == 0)
        def _():
            for_tile(step, lambda q, h, t: load(0, q, h, t).start())

        load(par, 0, 0, 0).wait()
        for_tile(step, lambda q, h, t: push(par, q, h, t).start())

        @pl.when(step + 1 < total)
        def _():
            @pl.when(step >= 1)
            def _():
                push(1 - par, 0, 0, 0).wait_send()
            for_tile(step + 1, lambda q, h, t: load(1 - par, q, h, t).start())

        @pl.when(step == total - 1)
        def _():
            push(par, 0, 0, 0).wait_send()
            push(1 - par, 0, 0, 0).wait_send()
            three = full_ref.at[pl.ds(0, hr), pl.ds(0, 3 * hc)] if kind == "col" else full_ref.at[pl.ds(0, 3 * hr), pl.ds(0, hc)]
            pltpu.make_async_remote_copy(src_ref=three, dst_ref=three, send_sem=ssem.at[0], recv_sem=rsem,
                                         device_id=(mx, my, 1 - mc), device_id_type=MESH).wait_recv()

    return pl.pallas_call(
        body, name=name, grid=(3, nt),
        in_specs=[ANY], out_specs=ANY,
        out_shape=jax.ShapeDtypeStruct((R, C), BF16),
        scratch_shapes=[pltpu.VMEM((2, tr, hc), BF16), pltpu.SemaphoreType.DMA((2,)), pltpu.SemaphoreType.DMA((2,)),
                        pltpu.SemaphoreType.DMA],
        input_output_aliases={0: 0},
        compiler_params=_cp("arbitrary", "arbitrary"),
    )(full)


def _half_shape(kind, R, C):
    return (R // 2, C) if kind == "col" else (R, C // 2)


def _piece_shape(kind, R, C):
    return (R // 2, C // 4) if kind == "col" else (R // 4, C // 2)


def pair_push(g, kind, c_arr, name):
    R, C = g.shape
    hr, hc = _half_shape(kind, R, C)
    tr = _row_tile(hr, hc, itemsize=2, budget=1024 * 1024)
    nt = hr // tr

    def body(c_ref, g_ref, out_ref, stage, ssem, rsem):
        i = pl.program_id(0)
        slot = i % 2
        mx, my, mc = _mesh_pos()

        def push(s, t):
            return pltpu.make_async_remote_copy(
                src_ref=stage.at[s], dst_ref=out_ref.at[pl.ds(pl.multiple_of(t * tr, 16), tr)],
                send_sem=ssem.at[s], recv_sem=rsem, device_id=(mx, my, 1 - mc), device_id_type=MESH)

        @pl.when(i >= 2)
        def _():
            push(slot, 0).wait_send()

        stage[slot] = g_ref[...]
        push(slot, i).start()

        @pl.when(i == nt - 1)
        def _():
            push(slot, 0).wait_send()
            if nt >= 2:
                push(1 - slot, 0).wait_send()
            pltpu.make_async_remote_copy(src_ref=out_ref, dst_ref=out_ref, send_sem=ssem.at[0], recv_sem=rsem,
                                         device_id=(mx, my, 1 - mc), device_id_type=MESH).wait_recv()

    if kind == "col":
        g_spec = pl.BlockSpec((tr, hc), lambda i, c: ((1 - c[0]) * nt + i, 0))
    else:
        g_spec = pl.BlockSpec((tr, hc), lambda i, c: (i, 1 - c[0]))
    return pl.pallas_call(
        body, name=name,
        grid_spec=pltpu.PrefetchScalarGridSpec(
            num_scalar_prefetch=1, grid=(nt,), in_specs=[g_spec], out_specs=ANY,
            scratch_shapes=[pltpu.VMEM((2, tr, hc), BF16), pltpu.SemaphoreType.DMA((2,)), pltpu.SemaphoreType.DMA]),
        out_shape=jax.ShapeDtypeStruct((hr, hc), BF16),
        compiler_params=_cp("arbitrary"),
    )(c_arr, g)


def _partner_chip(k, p):
    return p ^ jnp.where(k == 0, 2, jnp.where(k == 1, 1, jnp.where(k == 2, 3, 0)))


def pair_add(g, got, kind, cp_arr, name):
    R, C = g.shape
    pr, pc = _piece_shape(kind, R, C)
    tr = _row_tile(pr, pc, itemsize=2, budget=1024 * 1024)
    nt = pr // tr

    def body(cp_ref, g_ref, got_ref, ps_ref, rb_ref):
        tile = (g_ref[...].astype(F32) + got_ref[...].astype(F32)).astype(BF16)
        ps_ref[...] = tile

        @pl.when(pl.program_id(1) == cp_ref[1])
        def _():
            rb_ref[...] = tile

    if kind == "col":
        g_spec = pl.BlockSpec((tr, pc), lambda i, q, cp: (cp[0] * nt + i, q))
        got_spec = pl.BlockSpec((tr, pc), lambda i, q, cp: (i, q))
    else:
        g_spec = pl.BlockSpec((tr, pc), lambda i, q, cp: (q * nt + i, cp[0]))
        got_spec = pl.BlockSpec((tr, pc), lambda i, q, cp: (q * nt + i, 0))
    return pl.pallas_call(
        body, name=name,
        grid_spec=pltpu.PrefetchScalarGridSpec(
            num_scalar_prefetch=1, grid=(nt, 4), in_specs=[g_spec, got_spec],
            out_specs=[pl.BlockSpec((None, tr, pc), lambda i, q, cp: (q, i, 0)),
                       pl.BlockSpec((None, tr, pc), lambda i, q, cp: (cp[1], i, 0))]),
        out_shape=[jax.ShapeDtypeStruct((4, pr, pc), BF16)] * 2,
        compiler_params=_cp("arbitrary", "arbitrary"),
    )(cp_arr, g, got)


def _rs_copies(ps, rb, ssem, rsem, mx, my, mc):
    p = 2 * mx + my
    out = []
    for w in range(len(ps)):
        for k, chip in enumerate(_other_chips(mx, my)):
            out.append(pltpu.make_async_remote_copy(
                src_ref=ps[w].at[2 * chip[0] + chip[1]], dst_ref=rb[w].at[p], send_sem=ssem.at[3 * w + k],
                recv_sem=rsem.at[3 * w + k], device_id=(*chip, mc), device_id_type=MESH))
    return out


def rs_start(ps, rb, after, name):
    n = len(ps)
    after = list(after)
    m = len(after)

    def body(*refs):
        ssem, rsem = refs[2 * n + m:2 * n + m + 2]
        ps_o = refs[2 * n + m + 2:3 * n + m + 2]
        rb_o = refs[3 * n + m + 2:4 * n + m + 2]
        token = refs[4 * n + m + 2]
        for cp in _rs_copies(ps_o, rb_o, ssem, rsem, *_mesh_pos()):
            cp.start()
        token[...] = jnp.zeros_like(token)

    both = list(ps) + list(rb)
    res = pl.pallas_call(
        body, name=name,
        out_shape=[pltpu.SemaphoreType.DMA((3 * n,)), pltpu.SemaphoreType.DMA((3 * n,))]
        + [pltpu.HBM(a.shape, a.dtype) for a in both] + [jax.ShapeDtypeStruct((8, 128), F32)],
        in_specs=[HBM_SPEC] * (2 * n) + [ANY] * m,
        out_specs=[SEM_SPEC, SEM_SPEC] + [HBM_SPEC] * (2 * n) + [pl.BlockSpec(memory_space=pltpu.VMEM)],
        input_output_aliases={w: 2 + w for w in range(2 * n)},
        compiler_params=pltpu.CompilerParams(has_side_effects=pltpu.SideEffectType.DATAFLOW_SIDE_EFFECTING),
    )(*[pltpu.with_memory_space_constraint(a, pltpu.HBM) for a in both], *after)
    return res[0], res[1], list(res[2:2 + n]), list(res[2 + n:2 + 2 * n]), res[2 + 2 * n]


def rs_wait(ps, rb, ssem, rsem, after, name):
    n = len(ps)
    after = list(after)
    m = len(after)

    def body(*refs):
        ps_i, rb_i = refs[:n], refs[n:2 * n]
        ssem_ref, rsem_ref = refs[2 * n], refs[2 * n + 1]
        for cp in _rs_copies(ps_i, rb_i, ssem_ref, rsem_ref, *_mesh_pos()):
            cp.wait_send()
            cp.wait_recv()

    both = list(ps) + list(rb)
    res = pl.pallas_call(
        body, name=name,
        out_shape=[pltpu.HBM(a.shape, a.dtype) for a in both],
        in_specs=[HBM_SPEC] * (2 * n) + [SEM_SPEC, SEM_SPEC] + [ANY] * m,
        out_specs=[HBM_SPEC] * (2 * n),
        input_output_aliases={w: w for w in range(2 * n)},
        compiler_params=pltpu.CompilerParams(has_side_effects=pltpu.SideEffectType.DATAFLOW_SIDE_EFFECTING),
    )(*both, ssem, rsem, *after)
    return list(res[n:])


def sum_share(parts, kind, R, C, name):
    _, pr, pc = parts.shape
    sr, sc = _shard_shape(kind, R, C)
    tr = _row_tile(pr, pc * 4, budget=4 * 1024 * 1024)
    nt = pr // tr

    def body(p_ref, fin_ref, stage, lsem, ssem, rsem):
        i = pl.program_id(0)
        slot = i % 2
        mx, my, mc = _mesh_pos()

        def region(h, t):
            r0 = pl.multiple_of(t * tr, 8)
            if kind == "col":
                return fin_ref.at[pl.ds(pl.multiple_of(h * pr + r0, 8), tr)]
            return fin_ref.at[pl.ds(r0, tr), pl.ds(h * pc, pc)]

        def copies(s, h, t):
            return (pltpu.make_async_copy(stage.at[s], region(h, t), lsem.at[s]),
                    pltpu.make_async_remote_copy(src_ref=stage.at[s], dst_ref=region(h, t), send_sem=ssem.at[s],
                                                 recv_sem=rsem, device_id=(mx, my, 1 - mc), device_id_type=MESH))

        def wait_sent(s):
            loc, rem = copies(s, 0, 0)
            loc.wait()
            rem.wait_send()

        @pl.when(i >= 2)
        def _():
            wait_sent(slot)

        acc = p_ref[0].astype(F32)
        for k in range(1, 4):
            acc = acc + p_ref[k].astype(F32)
        stage[slot] = acc
        if kind == "col":
            for cp in copies(slot, mc, i):
                cp.start()
        else:
            for h in range(2):
                @pl.when(mc == h)
                def _(h=h):
                    for cp in copies(slot, h, i):
                        cp.start()

        @pl.when(i == nt - 1)
        def _():
            wait_sent(slot)
            if nt >= 2:
                wait_sent(1 - slot)
            half = fin_ref.at[pl.ds(0, pr), pl.ds(0, pc)]
            pltpu.make_async_remote_copy(src_ref=half, dst_ref=half, send_sem=ssem.at[0], recv_sem=rsem,
                                         device_id=(mx, my, 1 - mc), device_id_type=MESH).wait_recv()

    return pl.pallas_call(
        body, name=name, grid=(nt,),
        in_specs=[pl.BlockSpec((4, tr, pc), lambda i: (0, i, 0))],
        out_specs=ANY,
        out_shape=jax.ShapeDtypeStruct((sr, sc), F32),
        scratch_shapes=[pltpu.VMEM((2, tr, pc), F32), pltpu.SemaphoreType.DMA((2,)), pltpu.SemaphoreType.DMA((2,)),
                        pltpu.SemaphoreType.DMA],
        compiler_params=_cp("arbitrary"),
    )(parts)


def _pack(parts, rows):
    flat = []
    for a in parts:
        a = jnp.ravel(a).astype(F32)
        flat.append(jnp.pad(a, (0, (-a.shape[0]) % 128)))
    v = jnp.concatenate(flat)
    return jnp.pad(v, (0, rows * 128 - v.shape[0])).reshape(rows, 128)


def _unpack(block, shapes):
    lead = block.shape[:-2]
    v = block.reshape(lead + (-1,))
    out, off = [], 0
    for shp in shapes:
        n = int(np.prod(shp))
        out.append(v[..., off:off + n].reshape(lead + tuple(shp)))
        off += n + (-n) % 128
    return out


def _block_diag4(w):
    w4 = w.reshape(4, 4, 64, 64)
    eye = jnp.eye(4, dtype=w.dtype)
    return (w4[:, :, :, None, :] * eye[None, :, None, :, None]).reshape(4, 256, 256)


def _diag_blocks(bd):
    b5 = bd.reshape(4, 4, 64, 4, 64)
    return jnp.stack([b5[:, i, :, i, :] for i in range(4)], axis=1).reshape(16, 64, 64)


def _bias_window(rel_bias):
    m = (np.arange(768) + 127) % 768 - 127
    w = rel_bias[:, np.clip(512 - m, -128, 128) + 128]
    win = jnp.tile(w, (1, 128))[:, :128 * 767].reshape(8, 128, 767)[:, :, :WIN]
    qh = np.arange(128)[:, None] // CHUNK
    kc = np.arange(WIN)[None, :] // CHUNK
    valid = (kc >= qh) & (kc <= qh + 8)
    return jnp.where(jnp.asarray(valid)[None], win, NEG)


SMALL = ("b_ada", "norm_pre", "norm_post", "rel_bias", "conv_w", "conv_b", "lru_wa", "lru_ba", "lru_wx",
         "lru_bx", "lru_lambda")
WEIGHTS = ("w_ada", "b_ada", "norm_pre", "norm_post", "ffn1_w_gu", "ffn1_w_down", "w_in", "rel_bias", "conv_w",
           "conv_b", "lru_wa", "lru_ba", "lru_wx", "lru_bx", "lru_lambda", "w_att_o", "w_rec_o", "w_out",
           "ffn2_w_gu", "ffn2_w_down")


def kernel(x, c, w_ada, b_ada, norm_pre, norm_post, ffn1_w_gu, ffn1_w_down, w_in, rel_bias, conv_w, conv_b, lru_wa, lru_ba, lru_wx, lru_bx, lru_lambda, w_att_o, w_rec_o, w_out, ffn2_w_gu, ffn2_w_down, loss_target, m_w_ada, m_b_ada, m_norm_pre, m_norm_post, m_ffn1_w_gu, m_ffn1_w_down, m_w_in, m_rel_bias, m_conv_w, m_conv_b, m_lru_wa, m_lru_ba, m_lru_wx, m_lru_bx, m_lru_lambda, m_w_att_o, m_w_rec_o, m_w_out, m_ffn2_w_gu, m_ffn2_w_down, v_w_ada, v_b_ada, v_norm_pre, v_norm_post, v_ffn1_w_gu, v_ffn1_w_down, v_w_in, v_rel_bias, v_conv_w, v_conv_b, v_lru_wa, v_lru_ba, v_lru_wx, v_lru_bx, v_lru_lambda, v_w_att_o, v_w_rec_o, v_w_out, v_ffn2_w_gu, v_ffn2_w_down):
    W = dict(w_ada=w_ada, b_ada=b_ada, norm_pre=norm_pre, norm_post=norm_post, ffn1_w_gu=ffn1_w_gu,
             ffn1_w_down=ffn1_w_down, w_in=w_in, rel_bias=rel_bias, conv_w=conv_w, conv_b=conv_b, lru_wa=lru_wa,
             lru_ba=lru_ba, lru_wx=lru_wx, lru_bx=lru_bx, lru_lambda=lru_lambda, w_att_o=w_att_o, w_rec_o=w_rec_o,
             w_out=w_out, ffn2_w_gu=ffn2_w_gu, ffn2_w_down=ffn2_w_down)
    M = dict(w_ada=m_w_ada, b_ada=m_b_ada, norm_pre=m_norm_pre, norm_post=m_norm_post, ffn1_w_gu=m_ffn1_w_gu,
             ffn1_w_down=m_ffn1_w_down, w_in=m_w_in, rel_bias=m_rel_bias, conv_w=m_conv_w, conv_b=m_conv_b,
             lru_wa=m_lru_wa, lru_ba=m_lru_ba, lru_wx=m_lru_wx, lru_bx=m_lru_bx, lru_lambda=m_lru_lambda,
             w_att_o=m_w_att_o, w_rec_o=m_w_rec_o, w_out=m_w_out, ffn2_w_gu=m_ffn2_w_gu, ffn2_w_down=m_ffn2_w_down)
    V = dict(w_ada=v_w_ada, b_ada=v_b_ada, norm_pre=v_norm_pre, norm_post=v_norm_post, ffn1_w_gu=v_ffn1_w_gu,
             ffn1_w_down=v_ffn1_w_down, w_in=v_w_in, rel_bias=v_rel_bias, conv_w=v_conv_w, conv_b=v_conv_b,
             lru_wa=v_lru_wa, lru_ba=v_lru_ba, lru_wx=v_lru_wx, lru_bx=v_lru_bx, lru_lambda=v_lru_lambda,
             w_att_o=v_w_att_o, w_rec_o=v_w_rec_o, w_out=v_w_out, ffn2_w_gu=v_ffn2_w_gu, ffn2_w_down=v_ffn2_w_down)
    mx, my, mc = _mesh_pos()
    p = 2 * mx + my
    e = 4 * mx + 2 * my + mc
    xs = x[0]

    c_arr = jnp.reshape(mc, (1,)).astype(jnp.int32)
    cp_arr = jnp.stack([mc, p]).astype(jnp.int32)
    p_arr = jnp.reshape(p, (1,)).astype(jnp.int32)
    direct = ("w_att_o", "w_rec_o", "w_out", "ffn2_w_gu", "ffn2_w_down")
    geoms = [(kind, R, C, n in direct) for (n, kind, R, C) in BIG]
    names = [b[0] for b in BIG]
    placed = [ag_local(W[n][0], kind, R, C, p_arr, "ag_local_" + n) for (n, kind, R, C) in BIG[:2]]

    def arrived(fly, lo, hi, ssem, rsem, after, tag):
        done = ag_wait(fly, geoms[lo:hi], ssem, rsem, after, "ag_wait_" + tag)
        return [a if both else ag_forward(a, kind, R, C, "ag_forward_" + n)
                for a, (kind, R, C, both), n in zip(done, geoms[lo:hi], names[lo:hi])]

    g1 = ag_small(_pack([c, norm_pre, norm_post, conv_w], 32), "ag_small_params")
    c_all, npre4, npost4, cw4 = _unpack(g1, [(D,), (3, 256), (3, 256), (4, 256)])
    chipwise = lambda a: jnp.moveaxis(a[0::2], 0, 1).reshape(a.shape[1], D)
    npre, npost, conv_full = chipwise(npre4), chipwise(npost4), chipwise(cw4)

    b_cols = lax.dynamic_slice(b_ada, (0, p * 2304), (1, 2304))
    mod_cols = ada_fwd(c_all, w_ada[0], b_cols, "ada_fwd")
    g2 = ag_small(mod_cols.reshape(144, 128), "ag_mod")
    mod_all = jnp.moveaxis(g2[0::2].reshape(4, 8, 2304), 0, 1).reshape(8, 9 * D)
    mod = lax.dynamic_index_in_dim(mod_all, e, 0, keepdims=False).reshape(3, 3, D)
    zeros3 = jnp.zeros((3, D), F32)
    vecs = [jnp.concatenate([npre[k:k + 1], npost[k:k + 1], mod[k], zeros3], axis=0) for k in range(3)]

    f1_s, f1_r, f1_fly, tok0 = ag_start(placed[:2], geoms[:2], [g2], "ag_start_ffn1")
    placed += [ag_local(W[n][0], kind, R, C, p_arr, "ag_local_" + n, after=[tok0]) for (n, kind, R, C) in BIG[2:]]
    f1_gu, f1_dn = arrived(f1_fly, 0, 2, f1_s, f1_r, placed[7], "ffn1")
    mix_s, mix_r, mix_fly, tok1 = ag_start(placed[2:6], geoms[2:6], [f1_gu, f1_dn], "ag_start_mixer")
    ffn_s, ffn_r, ffn_fly, tok2 = ag_start(placed[6:], geoms[6:], [tok1], "ag_start_ffn2")
    wa_bd = _block_diag4(lru_wa[0]).astype(BF16)
    wx_bd = _block_diag4(lru_wx[0]).astype(BF16)
    pvec = jnp.concatenate([conv_full, conv_b, lru_ba, lru_bx, lru_lambda], axis=0)
    bias = _bias_window(rel_bias[0]).reshape(4, 256, WIN)

    f1_u = f1_gu[:, FF:]
    x1, h1, g1_, u1, a1, f1 = ffn_fwd(xs, vecs[0] + tok2[0:1, 0:1], f1_gu, f1_u, f1_dn, 0.5, "ffn1_fwd")
    win, wao, wro, wout = arrived(mix_fly, 2, 6, mix_s, mix_r, x1, "mixer")
    h2, qkv, rest = proj_fwd(x1, vecs[1], win, "proj_fwd")
    ao = attn_fwd(qkv, bias, "attn_fwd")
    hl, hg = lru_fwd(rest, pvec, wa_bd, wx_bd, "lru_fwd")
    x2, att, rec, mg, f2 = mix_out_fwd(x1, ao, hg, rest, vecs[1], wao, wro, wout, "mix_out_fwd")
    f2_gu, f2_dn = arrived(ffn_fly, 6, 8, ffn_s, ffn_r, x2, "ffn2")
    f2_u = f2_gu[:, FF:]
    dy, h3, g3_, u3, a3, f3, lvec = ffn_fwd(x2, vecs[2], f2_gu, f2_u, f2_dn, 0.5, "ffn2_fwd", tgt=loss_target[0])
    loss = lax.psum(lvec[0, 0], ("x", "y", "c"))

    G, grads = {}, {}
    geo = {n: (kind, R, C) for (n, kind, R, C) in BIG}

    def reduce_begin(names, tag):
        ps, rb = [], []
        for n in names:
            got = pair_push(G[n], geo[n][0], c_arr, "rs_push_" + n)
            a, b = pair_add(G[n], got, geo[n][0], cp_arr, "rs_pair_sum_" + n)
            ps.append(a)
            rb.append(b)
        return rs_start(ps, rb, [], "rs_start_" + tag)

    def reduce_end(names, flight, after, tag):
        ssem, rsem, ps, rb, _ = flight
        for a, n in zip(rs_wait(ps, rb, ssem, rsem, after, "rs_wait_" + tag), names):
            grads[n] = sum_share(a, *geo[n], "rs_sum_share_" + n)[None]

    dx2, df3, dgu3, va2 = ffn_bwd(dy, x2, f3, g3_, u3, vecs[2], f2_gu, f2_u, f2_dn, 0.5, "ffn2_bwd")
    G["ffn2_w_gu"] = mm_tn(h3, dgu3, "dw_ffn2_gu", D, 1408, 1024)
    G["ffn2_w_down"] = mm_tn(a3, df3, "dw_ffn2_down", 1408, D, 1024)
    fly_ffn2 = reduce_begin(("ffn2_w_gu", "ffn2_w_down"), "ffn2")
    vec1 = vecs[1] + fly_ffn2[4][0:1, 0:1]
    df2, d_att, d_rec, dao, dhl, d3, va_out = mix_out_bwd(dx2, f2, att, rec, rest, hl, vec1, wao, wro, wout,
                                                          "mix_out_bwd")
    G["w_out"] = mm_tn(mg, df2, "dw_out", D, D, 1024)
    G["w_att_o"] = mm_tn(ao, d_att, "dw_att_o", 512, D, 1024)
    G["w_rec_o"] = mm_tn(hg, d_rec, "dw_rec_o", D, D, 1024)
    dq, db, dkv = attn_bwd(qkv, dao, bias, "attn_bwd")
    dxr, v_lru, dwa_bd, dwx_bd = lru_bwd(dhl, hl, rest, pvec, wa_bd, wx_bd, "lru_bwd")
    dx1, va_in = proj_bwd(dq, dkv, dxr, d3, win, x1, dx2, vecs[1], "proj_bwd")
    G["w_in"] = dw_in(h2, dq, dkv, dxr, d3, "dw_in")
    fly_mix = reduce_begin(("w_in", "w_att_o", "w_rec_o", "w_out"), "mixer")
    vec0 = vecs[0] + fly_mix[4][0:1, 0:1]
    dx0, df1, dgu1, va0 = ffn_bwd(dx1, xs, f1, g1_, u1, vec0, f1_gu, f1_u, f1_dn, 0.5, "ffn1_bwd")
    G["ffn1_w_gu"] = mm_tn(h1, dgu1, "dw_ffn1_gu", D, 1408, 1024)
    G["ffn1_w_down"] = mm_tn(a1, df1, "dw_ffn1_down", 1408, D, 1024)
    fly_ffn1 = reduce_begin(("ffn1_w_gu", "ffn1_w_down"), "ffn1")
    reduce_end(("ffn2_w_gu", "ffn2_w_down"), fly_ffn2, [fly_ffn1[4]], "ffn2")
    reduce_end(("w_in", "w_att_o", "w_rec_o", "w_out"), fly_mix, [fly_ffn1[4], grads["ffn2_w_down"]], "mixer")

    va1 = va_out + va_in
    vas = (va0, va1, va2)
    dmod = jnp.stack([v[2:5] for v in vas])
    part = {"b_ada": dmod, "norm_pre": jnp.stack([v[0] for v in vas]), "norm_post": jnp.stack([v[1] for v in vas]),
            "rel_bias": bias_grad(db.reshape(8, 128, WIN), "bias_grad")[:, :257], "conv_w": v_lru[0:4], "conv_b": v_lru[4],
            "lru_wa": _diag_blocks(dwa_bd), "lru_ba": v_lru[5], "lru_wx": _diag_blocks(dwx_bd), "lru_bx": v_lru[6],
            "lru_lambda": v_lru[7]}
    full_shapes = {"b_ada": (9 * D,), "norm_pre": (3, D), "norm_post": (3, D), "rel_bias": (8, 257),
                   "conv_w": (4, D), "conv_b": (D,), "lru_wa": (16, 64, 64), "lru_ba": (D,),
                   "lru_wx": (16, 64, 64), "lru_bx": (D,), "lru_lambda": (D,)}
    g3 = ag_small(_pack([part[n] for n in SMALL], 1232), "ag_small_grads")
    red = dict(zip(SMALL, _unpack(sum_lead(g3, "sum_small_grads"), [full_shapes[n] for n in SMALL])))
    cols = lambda a: lax.dynamic_slice(a, (0, p * 256), (a.shape[0], 256))
    grads.update({"b_ada": red["b_ada"][None], "norm_pre": cols(red["norm_pre"])[None],
                  "norm_post": cols(red["norm_post"])[None], "rel_bias": red["rel_bias"][None],
                  "conv_w": cols(red["conv_w"])[None], "conv_b": red["conv_b"][None], "lru_wa": red["lru_wa"][None],
                  "lru_ba": red["lru_ba"][None], "lru_wx": red["lru_wx"][None], "lru_bx": red["lru_bx"][None],
                  "lru_lambda": red["lru_lambda"][None]})

    dmod_all = g3[:, :72].reshape(8, 9 * D)
    dmod_cols = jnp.pad(lax.dynamic_slice(dmod_all, (0, p * 2304), (8, 2304)), ((0, 120), (0, 0)))
    c_all_t = jnp.pad(c_all.T, ((0, 0), (0, 120)))
    grads["w_ada"] = ada_bwd(c_all_t, dmod_cols, "ada_bwd")[None]

    delta, new_m, new_v = {}, {}, {}

    def update(n):
        shp = W[n].shape
        d_, m_, v_ = adamw(W[n][0], grads[n][0], M[n][0], V[n][0], "adamw_" + n)
        delta[n], new_m[n], new_v[n] = d_.reshape(shp), m_.reshape(shp), v_.reshape(shp)

    for n in ("w_ada", "ffn2_w_gu", "ffn2_w_down", "w_in", "w_att_o", "w_rec_o", "w_out"):
        update(n)
    packed = [_pack([src[n] for n in SMALL], 1168) for src in (W, grads, M, V)]
    outs = adamw(*packed, "adamw_small")
    for dst, blk in zip((delta, new_m, new_v), outs):
        for n, a in zip(SMALL, _unpack(blk, [W[n].shape for n in SMALL])):
            dst[n] = a
    reduce_end(("ffn1_w_gu", "ffn1_w_down"), fly_ffn1,
               [outs[0], delta["w_ada"], delta["ffn2_w_gu"], delta["ffn2_w_down"], delta["w_in"], delta["w_out"]], "ffn1")
    for n in ("ffn1_w_gu", "ffn1_w_down"):
        update(n)

    return (loss, dx0[None], *[grads[n] for n in WEIGHTS], *[delta[n] for n in WEIGHTS],
            *[new_m[n] for n in WEIGHTS], *[new_v[n] for n in WEIGHTS])
```

```python
import functools

import numpy as np
import jax
import jax.numpy as jnp
from jax import lax
from jax.experimental import pallas as pl
from jax.experimental.pallas import tpu as pltpu

F32 = jnp.float32
BF16 = jnp.bfloat16

D = 1024
FF = 2816
PW = 5632
HP = 128
CHUNK = 64
WIN = 640
TQ = 512
EPS = 1e-6
NEG = -1e30
LRU_C = 8.0
N_DEV = 8
VMEM_LIMIT = 56 * 1024 * 1024

ADAM_LR, ADAM_B1, ADAM_B2, ADAM_EPS, ADAM_WD, ADAM_STEP = 0.001, 0.9, 0.999, 1e-08, 0.01, 10

MESH = pl.DeviceIdType.MESH
ANY = pl.BlockSpec(memory_space=pl.ANY)


def _cp(*sem):
    return pltpu.CompilerParams(dimension_semantics=tuple(sem), vmem_limit_bytes=VMEM_LIMIT)


def _dot(a, b):
    return jnp.dot(a, b, preferred_element_type=F32)


def _dot_nt(a, b):
    return lax.dot_general(a, b, (((1,), (1,)), ((), ())), preferred_element_type=F32)


def _dot_tn(a, b):
    return lax.dot_general(a, b, (((0,), (0,)), ((), ())), preferred_element_type=F32)


def _mean(v):
    return jnp.mean(v, axis=-1, keepdims=True)


def _colsum(v):
    return jnp.sum(v, axis=0, keepdims=True)


def _sigmoid(v):
    return 0.5 * jnp.tanh(0.5 * v) + 0.5


_GK = 0.7978845608028654


def _gelu(v):
    t = jnp.tanh(_GK * (v + 0.044715 * v * v * v))
    return 0.5 * v * (1.0 + t)


def _pre_norm(xv, vec_ref):
    r = lax.rsqrt(_mean(xv * xv) + EPS)
    n = xv * r * vec_ref[0:1, :]
    return n * (1.0 + vec_ref[3:4, :]) + vec_ref[2:3, :]


def _pre_norm_bwd(dh, xv, dres, vec_ref, vacc_ref):
    r = lax.rsqrt(_mean(xv * xv) + EPS)
    xh = xv * r
    n = xh * vec_ref[0:1, :]
    vacc_ref[2:3, :] += _colsum(dh)
    vacc_ref[3:4, :] += _colsum(dh * n)
    dn = dh * (1.0 + vec_ref[3:4, :])
    vacc_ref[0:1, :] += _colsum(dn * xh)
    dxh = dn * vec_ref[0:1, :]
    return r * (dxh - xh * _mean(dxh * xh)) + dres


def _post_norm_bwd(dxo, fv, res, vec_ref, vacc_ref):
    rf = lax.rsqrt(_mean(fv * fv) + EPS)
    fh = fv * rf
    gp = vec_ref[1:2, :]
    vacc_ref[4:5, :] += _colsum(res * dxo * (fh * gp))
    dy = (res * vec_ref[4:5, :]) * dxo
    vacc_ref[1:2, :] += _colsum(dy * fh)
    dfn = dy * gp
    return rf * (dfn - fh * _mean(dfn * fh))


def ffn_fwd(x, vec, w_gu, w_u, w_dn, res, name, tgt=None, tm=1024, tf=512):
    S = x.shape[0]
    tm = min(tm, S)
    nt = S // tm
    nf = -(-FF // tf)
    tail = FF - tf * (nf - 1)
    head = tgt is not None

    def body(*refs):
        x_ref, vec_ref, wg_ref, wu_ref, wd_ref = refs[:5]
        if head:
            t_ref, xo_ref, h_ref, g_ref, u_ref, a_ref, f_ref, l_ref, hs, acc, lacc = refs[5:]
        else:
            xo_ref, h_ref, g_ref, u_ref, a_ref, f_ref, hs, acc = refs[5:]
        i, j = pl.program_id(0), pl.program_id(1)

        @pl.when(j == 0)
        def _():
            h = _pre_norm(x_ref[...], vec_ref).astype(BF16)
            hs[...] = h
            h_ref[...] = h
            acc[...] = jnp.zeros_like(acc)

        def chunk(w):
            h = hs[...]
            g = _dot(h, wg_ref[:, 0:w])
            u = _dot(h, wu_ref[:, 0:w])
            g_ref[:, 0:w] = g.astype(BF16)
            u_ref[:, 0:w] = u.astype(BF16)
            a = (g * _sigmoid(g) * u).astype(BF16)
            a_ref[:, 0:w] = a
            acc[...] += _dot(a, wd_ref[0:w, :])

        @pl.when(j < nf - 1)
        def _():
            chunk(tf)

        @pl.when(j == nf - 1)
        def _():
            chunk(tail)
            f = acc[...]
            f_ref[...] = f
            y = f * lax.rsqrt(_mean(f * f) + EPS) * vec_ref[1:2, :]
            xo = x_ref[...] + (res * vec_ref[4:5, :]) * y
            if head:
                @pl.when(i == 0)
                def _():
                    lacc[...] = jnp.zeros_like(lacc)

                d = xo - t_ref[...]
                xo_ref[...] = d * (1.0 / D)
                lacc[...] += _colsum(d * d)

                @pl.when(i == nt - 1)
                def _():
                    l_ref[...] = jnp.broadcast_to(0.5 * jnp.sum(lacc[...]) * (1.0 / D), (8, 128))
            else:
                xo_ref[...] = xo

    row = lambda i, j: (i, 0)
    col = lambda i, j: (i, j)
    once = dict(pipeline_mode=pl.Buffered(1)) if head else {}
    in_specs = [pl.BlockSpec((tm, D), row, **once), pl.BlockSpec((8, D), lambda i, j: (0, 0)),
                pl.BlockSpec((D, tf), lambda i, j: (0, j)), pl.BlockSpec((D, tf), lambda i, j: (0, j)),
                pl.BlockSpec((tf, D), lambda i, j: (j, 0))]
    out_specs = [pl.BlockSpec((tm, D), row), pl.BlockSpec((tm, D), row), pl.BlockSpec((tm, tf), col),
                 pl.BlockSpec((tm, tf), col), pl.BlockSpec((tm, tf), col), pl.BlockSpec((tm, D), row)]
    out_shape = [jax.ShapeDtypeStruct((S, D), F32), jax.ShapeDtypeStruct((S, D), BF16),
                 jax.ShapeDtypeStruct((S, FF), BF16), jax.ShapeDtypeStruct((S, FF), BF16),
                 jax.ShapeDtypeStruct((S, FF), BF16), jax.ShapeDtypeStruct((S, D), F32)]
    scratch = [pltpu.VMEM((tm, D), BF16), pltpu.VMEM((tm, D), F32)]
    args = [x, vec, w_gu, w_u, w_dn]
    if head:
        in_specs.append(pl.BlockSpec((tm, D), row, **once))
        out_specs.append(pl.BlockSpec((8, 128), lambda i, j: (0, 0)))
        out_shape.append(jax.ShapeDtypeStruct((8, 128), F32))
        scratch.append(pltpu.VMEM((1, D), F32))
        args.append(tgt)
    return pl.pallas_call(
        body, name=name, grid=(nt, nf), in_specs=in_specs, out_specs=out_specs, out_shape=out_shape,
        scratch_shapes=scratch,
        compiler_params=_cp("arbitrary" if head else "parallel", "arbitrary"),
    )(*args)


def ffn_bwd(dxo, x, f, g, u, vec, w_gu, w_u, w_dn, res, name, tm=1024, tf=512):
    S = x.shape[0]
    tm = min(tm, S)
    nf = -(-FF // tf)
    tail = FF - tf * (nf - 1)

    def body(dxo_ref, x_ref, f_ref, g_ref, u_ref, vec_ref, wg_ref, wu_ref, wd_ref,
             dx_ref, df_ref, dgu_ref, vacc_ref, dfs, acc):
        i, j = pl.program_id(0), pl.program_id(1)

        @pl.when((i == 0) & (j == 0))
        def _():
            vacc_ref[...] = jnp.zeros_like(vacc_ref)

        @pl.when(j == 0)
        def _():
            df = _post_norm_bwd(dxo_ref[...], f_ref[...], res, vec_ref, vacc_ref).astype(BF16)
            dfs[...] = df
            df_ref[...] = df
            acc[...] = jnp.zeros_like(acc)

        def chunk(w):
            da = _dot_nt(dfs[...], wd_ref[0:w, :])
            gv, uv = g_ref[:, 0:w].astype(F32), u_ref[:, 0:w].astype(F32)
            sg = _sigmoid(gv)
            dg = (da * uv * (sg * (1.0 + gv * (1.0 - sg)))).astype(BF16)
            du = (da * (gv * sg)).astype(BF16)
            dgu_ref[0, :, 0:w] = dg
            dgu_ref[1, :, 0:w] = du
            acc[...] += _dot_nt(dg, wg_ref[:, 0:w]) + _dot_nt(du, wu_ref[:, 0:w])

        @pl.when(j < nf - 1)
        def _():
            chunk(tf)

        @pl.when(j == nf - 1)
        def _():
            chunk(tail)
            dx_ref[...] = _pre_norm_bwd(acc[...], x_ref[...], dxo_ref[...], vec_ref, vacc_ref)

    row = lambda i, j: (i, 0)
    col = lambda i, j: (i, j)
    return pl.pallas_call(
        body, name=name, grid=(S // tm, nf),
        in_specs=[pl.BlockSpec((tm, D), row, pipeline_mode=pl.Buffered(1)) for _ in range(3)]
        + [pl.BlockSpec((tm, tf), col), pl.BlockSpec((tm, tf), col),
                  pl.BlockSpec((8, D), lambda i, j: (0, 0)),
                  pl.BlockSpec((D, tf), lambda i, j: (0, j)), pl.BlockSpec((D, tf), lambda i, j: (0, j)),
                  pl.BlockSpec((tf, D), lambda i, j: (j, 0))],
        out_specs=[pl.BlockSpec((tm, D), row), pl.BlockSpec((tm, D), row),
                   pl.BlockSpec((2, tm, tf), lambda i, j: (0, i, j)),
                   pl.BlockSpec((8, D), lambda i, j: (0, 0))],
        out_shape=[jax.ShapeDtypeStruct((S, D), F32), jax.ShapeDtypeStruct((S, D), BF16),
                   jax.ShapeDtypeStruct((2, S, FF), BF16), jax.ShapeDtypeStruct((8, D), F32)],
        scratch_shapes=[pltpu.VMEM((tm, D), BF16), pltpu.VMEM((tm, D), F32)],
        compiler_params=_cp("arbitrary", "arbitrary"),
    )(dxo, x, f, g, u, vec, w_gu, w_u, w_dn)


def mm_tn(a, b, name, tm, tn, tk, out_dtype=BF16, prev=None, col_off=0, n_total=None):
    S, M = a.shape
    if b.ndim == 3:
        G, _, Nf = b.shape
    else:
        G, Nf = 1, b.shape[1]
    N = G * Nf
    n_total = N if n_total is None else n_total
    tk = min(tk, S)
    nbf = Nf // tn
    nk = S // tk
    ob = col_off // tn

    def body(*refs):
        a_ref, b_ref = refs[0], refs[1]
        o_ref, acc = refs[-2], refs[-1]
        k = pl.program_id(2)

        @pl.when(k == 0)
        def _():
            acc[...] = jnp.zeros_like(acc)

        acc[...] += _dot_tn(a_ref[...], b_ref[...])

        @pl.when(k == nk - 1)
        def _():
            o_ref[...] = acc[...].astype(out_dtype)

    if b.ndim == 3:
        b_spec = pl.BlockSpec((None, tk, tn), lambda i, j, k: (j // nbf, k, j % nbf))
    else:
        b_spec = pl.BlockSpec((tk, tn), lambda i, j, k: (k, j))
    in_specs = [pl.BlockSpec((tk, tm), lambda i, j, k: (k, i)), b_spec]
    args = [a, b]
    aliases = {}
    if prev is not None:
        in_specs.append(ANY)
        args.append(prev)
        aliases = {2: 0}
    return pl.pallas_call(
        body, name=name, grid=(M // tm, N // tn, nk),
        in_specs=in_specs,
        out_specs=pl.BlockSpec((tm, tn), lambda i, j, k: (i, j + ob)),
        out_shape=jax.ShapeDtypeStruct((M, n_total), out_dtype),
        scratch_shapes=[pltpu.VMEM((tm, tn), F32)],
        input_output_aliases=aliases,
        compiler_params=_cp("parallel", "parallel", "arbitrary"),
    )(*args)


def proj_fwd(x, vec, w_in, name, tm=1024, tn=512):
    S = x.shape[0]
    tm = min(tm, S)
    nq = 1536 // tn

    def body(x_ref, vec_ref, w_ref, h_ref, qkv_ref, rest_ref, hs):
        j = pl.program_id(1)

        @pl.when(j == 0)
        def _():
            h = _pre_norm(x_ref[...], vec_ref).astype(BF16)
            hs[...] = h
            h_ref[...] = h

        r = _dot(hs[...], w_ref[...])

        @pl.when(j < nq)
        def _():
            qkv_ref[...] = r.astype(BF16)

        @pl.when(j >= nq)
        def _():
            rest_ref[...] = r.astype(BF16)

    row = lambda i, j: (i, 0)
    return pl.pallas_call(
        body, name=name, grid=(S // tm, PW // tn),
        in_specs=[pl.BlockSpec((tm, D), row), pl.BlockSpec((8, D), lambda i, j: (0, 0)),
                  pl.BlockSpec((D, tn), lambda i, j: (0, j))],
        out_specs=[pl.BlockSpec((tm, D), row),
                   pl.BlockSpec((tm, tn), lambda i, j: (i, jnp.minimum(j, nq - 1))),
                   pl.BlockSpec((tm, tn), lambda i, j: (i, jnp.maximum(j - nq, 0)))],
        out_shape=[jax.ShapeDtypeStruct((S, D), BF16), jax.ShapeDtypeStruct((S, 1536), BF16),
                   jax.ShapeDtypeStruct((S, 4096), BF16)],
        scratch_shapes=[pltpu.VMEM((tm, D), BF16)],
        compiler_params=_cp("parallel", "arbitrary"),
    )(x, vec, w_in)


def proj_bwd(dq, dkv, dxr, d3, w_in, x, dxo, vec, name, tm=1024, tk=512):
    S = x.shape[0]
    tm = min(tm, S)
    nk = PW // tk

    def body(dq_ref, dkv_ref, dxr_ref, d3_ref, w_ref, x_ref, dxo_ref, vec_ref, dx_ref, vacc_ref, acc):
        i, j = pl.program_id(0), pl.program_id(1)

        @pl.when((i == 0) & (j == 0))
        def _():
            vacc_ref[...] = jnp.zeros_like(vacc_ref)

        @pl.when(j == 0)
        def _():
            acc[...] = _dot_nt(dq_ref[...], w_ref[...])

        @pl.when((j >= 1) & (j < 3))
        def _():
            acc[...] += _dot_nt(dkv_ref[...], w_ref[...])

        @pl.when((j >= 3) & (j < 5))
        def _():
            acc[...] += _dot_nt(dxr_ref[...], w_ref[...])

        @pl.when(j >= 5)
        def _():
            acc[...] += _dot_nt(d3_ref[...], w_ref[...])

        @pl.when(j == nk - 1)
        def _():
            dx_ref[...] = _pre_norm_bwd(acc[...], x_ref[...], dxo_ref[...], vec_ref, vacc_ref)

    row = lambda i, j: (i, 0)
    return pl.pallas_call(
        body, name=name, grid=(S // tm, nk),
        in_specs=[pl.BlockSpec((None, tm, tk), lambda i, j: (0, i, 0)),
                  pl.BlockSpec((None, tm, tk), lambda i, j: (jnp.clip(j - 1, 0, 1), i, 0)),
                  pl.BlockSpec((tm, tk), lambda i, j: (i, jnp.clip(j - 3, 0, 1))),
                  pl.BlockSpec((None, tm, tk), lambda i, j: (jnp.clip(j - 5, 0, 5) // 2, i, jnp.clip(j - 5, 0, 5) % 2)),
                  pl.BlockSpec((D, tk), lambda i, j: (0, j)),
                  pl.BlockSpec((tm, D), row), pl.BlockSpec((tm, D), row),
                  pl.BlockSpec((8, D), lambda i, j: (0, 0))],
        out_specs=[pl.BlockSpec((tm, D), row), pl.BlockSpec((8, D), lambda i, j: (0, 0))],
        out_shape=[jax.ShapeDtypeStruct((S, D), F32), jax.ShapeDtypeStruct((8, D), F32)],
        scratch_shapes=[pltpu.VMEM((tm, D), F32)],
        compiler_params=_cp("arbitrary", "arbitrary"),
    )(dq, dkv, dxr, d3, w_in, x, dxo, vec)


def _two_heads(v, lane):
    zero = jnp.zeros((), v.dtype)
    return jnp.concatenate([jnp.where(lane < 64, v, zero), jnp.where(lane >= 64, v, zero)], axis=0)


def _attn_probs(qm, ka, bias_h, i, grp):
    s = _dot_nt(qm, ka) + bias_h
    col = lax.broadcasted_iota(jnp.int32, s.shape, 1)
    first_key = jnp.where(i == 0, 512 - 128 * grp, 0)
    s = jnp.where(col >= first_key, s, NEG)
    e = jnp.exp(s - jnp.max(s, axis=-1, keepdims=True))
    return e * (1.0 / jnp.sum(e, axis=-1, keepdims=True))


def attn_fwd(qkv, bias, name):
    S = qkv.shape[0]
    nb = S // TQ

    def body(q_ref, kp_ref, kc_ref, vp_ref, vc_ref, b_ref, o_ref, kw, vw):
        i = pl.program_id(1)
        kw[0:TQ, :] = kp_ref[...]
        kw[TQ:2 * TQ, :] = kc_ref[...]
        vw[0:TQ, :] = vp_ref[...]
        vw[TQ:2 * TQ, :] = vc_ref[...]
        lane = lax.broadcasted_iota(jnp.int32, (1, HP), 1)

        def group(a, carry):
            r0 = pl.multiple_of(a * 128, 128)
            qa = q_ref[pl.ds(r0, 128), :] * jnp.asarray(0.125, BF16)
            ka = kw[pl.ds(r0, WIN), :]
            va = vw[pl.ds(r0, WIN), :]
            p = _attn_probs(_two_heads(qa, lane), ka, b_ref[...], i, a)
            o2 = _dot(p.astype(BF16), va)
            o_ref[pl.ds(r0, 128), :] = jnp.where(lane < 64, o2[0:128], o2[128:256]).astype(BF16)
            return carry

        lax.fori_loop(0, TQ // 128, group, 0, unroll=True)

    prev = lambda h, i: (jnp.maximum(i - 1, 0), 0)
    return pl.pallas_call(
        body, name=name, grid=(4, nb),
        in_specs=[pl.BlockSpec((TQ, HP), lambda h, i: (i, h)),
                  pl.BlockSpec((TQ, HP), lambda h, i: (jnp.maximum(i - 1, 0), 4 + h)),
                  pl.BlockSpec((TQ, HP), lambda h, i: (i, 4 + h)),
                  pl.BlockSpec((TQ, HP), lambda h, i: (jnp.maximum(i - 1, 0), 8 + h)),
                  pl.BlockSpec((TQ, HP), lambda h, i: (i, 8 + h)),
                  pl.BlockSpec((None, 256, WIN), lambda h, i: (h, 0, 0))],
        out_specs=pl.BlockSpec((TQ, HP), lambda h, i: (i, h)),
        out_shape=jax.ShapeDtypeStruct((S, 512), BF16),
        scratch_shapes=[pltpu.VMEM((2 * TQ, HP), BF16), pltpu.VMEM((2 * TQ, HP), BF16)],
        compiler_params=_cp("parallel", "arbitrary"),
    )(qkv, qkv, qkv, qkv, qkv, bias)


def attn_bwd(qkv, do, bias, name):
    S = qkv.shape[0]
    nb = S // TQ

    def body(q_ref, kp_ref, kc_ref, vp_ref, vc_ref, do_ref, b_ref, dqkv_ref, db_ref, dkv_ref, kw, vw, ak, av):
        i = pl.program_id(1)

        @pl.when(i == 0)
        def _():
            db_ref[...] = jnp.zeros_like(db_ref)
            ak[...] = jnp.zeros_like(ak)
            av[...] = jnp.zeros_like(av)

        @pl.when(i > 0)
        def _():
            ak[0:TQ, :] = ak[TQ:2 * TQ, :]
            av[0:TQ, :] = av[TQ:2 * TQ, :]
            ak[TQ:2 * TQ, :] = jnp.zeros((TQ, HP), F32)
            av[TQ:2 * TQ, :] = jnp.zeros((TQ, HP), F32)

        @pl.when(i < nb)
        def _():
            kw[0:TQ, :] = kp_ref[...]
            kw[TQ:2 * TQ, :] = kc_ref[...]
            vw[0:TQ, :] = vp_ref[...]
            vw[TQ:2 * TQ, :] = vc_ref[...]
            lane = lax.broadcasted_iota(jnp.int32, (1, HP), 1)

            def group(a, carry):
                r0 = pl.multiple_of(a * 128, 128)
                q2 = _two_heads(q_ref[pl.ds(r0, 128), :] * jnp.asarray(0.125, BF16), lane)
                do2 = _two_heads(do_ref[pl.ds(r0, 128), :], lane)
                ka = kw[pl.ds(r0, WIN), :]
                va = vw[pl.ds(r0, WIN), :]
                p = _attn_probs(q2, ka, b_ref[...], i, a)
                dp = _dot_nt(do2, va)
                ds = p * (dp - jnp.sum(p * dp, axis=-1, keepdims=True))
                db_ref[...] += ds
                dsb = ds.astype(BF16)
                dq2 = _dot(dsb, ka)
                ak[pl.ds(r0, WIN), :] += _dot_tn(dsb, q2)
                av[pl.ds(r0, WIN), :] += _dot_tn(p.astype(BF16), do2)
                dq = jnp.where(lane < 64, dq2[0:128], dq2[128:256])
                dqkv_ref[0, pl.ds(r0, 128), :] = (dq * 0.125).astype(BF16)
                return carry

            lax.fori_loop(0, TQ // 128, group, 0, unroll=True)

        @pl.when(i > 0)
        def _():
            dkv_ref[0] = ak[0:TQ, :].astype(BF16)
            dkv_ref[1] = av[0:TQ, :].astype(BF16)

    cur = lambda i: jnp.minimum(i, nb - 1)
    prv = lambda i: jnp.clip(i - 1, 0, nb - 1)
    dq, db, dkv = pl.pallas_call(
        body, name=name, grid=(4, nb + 1),
        in_specs=[pl.BlockSpec((TQ, HP), lambda h, i: (cur(i), h)),
                  pl.BlockSpec((TQ, HP), lambda h, i: (prv(i), 4 + h)),
                  pl.BlockSpec((TQ, HP), lambda h, i: (cur(i), 4 + h)),
                  pl.BlockSpec((TQ, HP), lambda h, i: (prv(i), 8 + h)),
                  pl.BlockSpec((TQ, HP), lambda h, i: (cur(i), 8 + h)),
                  pl.BlockSpec((TQ, HP), lambda h, i: (cur(i), h)),
                  pl.BlockSpec((None, 256, WIN), lambda h, i: (h, 0, 0))],
        out_specs=[pl.BlockSpec((1, TQ, HP), lambda h, i: (0, cur(i), h)),
                   pl.BlockSpec((None, 256, WIN), lambda h, i: (h, 0, 0)),
                   pl.BlockSpec((2, TQ, HP), lambda h, i: (0, prv(i), h))],
        out_shape=[jax.ShapeDtypeStruct((1, S, 512), BF16), jax.ShapeDtypeStruct((4, 256, WIN), F32),
                   jax.ShapeDtypeStruct((2, S, 512), BF16)],
        scratch_shapes=[pltpu.VMEM((2 * TQ, HP), BF16), pltpu.VMEM((2 * TQ, HP), BF16),
                        pltpu.VMEM((2 * TQ, HP), F32), pltpu.VMEM((2 * TQ, HP), F32)],
        compiler_params=_cp("parallel", "arbitrary"),
    )(qkv, qkv, qkv, qkv, qkv, do, bias)
    return dq, db, dkv


def bias_grad(db, name):
    def body(db_ref, o_ref):
        r = lax.broadcasted_iota(jnp.int32, (128, 128), 0)
        c = lax.broadcasted_iota(jnp.int32, (128, 128), 1)
        flip = (r + c == 127).astype(BF16)
        lane = lax.broadcasted_iota(jnp.int32, (16, 384), 1)
        src = lax.broadcasted_iota(jnp.int32, (128, 384), 0)
        dst = lax.broadcasted_iota(jnp.int32, (128, 384), 1)

        def split_dot(v, m):
            hi = v.astype(BF16)
            r1 = v - hi.astype(F32)
            mid = r1.astype(BF16)
            lo = (r1 - mid.astype(F32)).astype(BF16)
            return _dot(hi, m) + _dot(mid, m) + _dot(lo, m)

        def diag_sums(w):
            y = pltpu.roll(split_dot(w, flip), 0, 1, stride=1, stride_axis=0)
            return jnp.broadcast_to(_colsum(y), (16, 128))

        w4 = db_ref[0, :, 512:640]
        w3 = db_ref[0, :, 384:512]
        far = jnp.sum(db_ref[0, :, 0:384]) + jnp.sum(jnp.where(r >= c, w3, 0.0))
        lo4 = diag_sums(jnp.where(r >= c, w4, 0.0))
        up4 = diag_sums(jnp.where(r < c, w4, 0.0))
        up3 = diag_sums(jnp.where(r < c, w3, 0.0))
        p_lo4 = (dst == 128 + (src + 1) % 128).astype(BF16)
        p_up4 = ((dst == src + 1) & (src < 127)).astype(BF16)
        p_up3 = ((dst == src + 129) & (src < 127)).astype(BF16)
        out = split_dot(lo4, p_lo4) + split_dot(up4, p_up4) + split_dot(up3, p_up3)
        o_ref[0] = out + jnp.where(lane == 256, far, 0.0)

    return pl.pallas_call(
        body, name=name, grid=(8,),
        in_specs=[pl.BlockSpec((1, 128, WIN), lambda h: (h, 0, 0))],
        out_specs=pl.BlockSpec((1, 16, 384), lambda h: (h, 0, 0)),
        out_shape=jax.ShapeDtypeStruct((8, 16, 384), F32),
        compiler_params=_cp("parallel"),
    )(db)[:, 0, :]


LT = 256
LC = 512


def _lru_gates(xs, pv_ref, wa_ref, wx_ref, tl):
    xc = (pv_ref[4:5, :] + pv_ref[3:4, :] * xs[pl.ds(8, tl), :] + pv_ref[2:3, :] * xs[pl.ds(7, tl), :]
          + pv_ref[1:2, :] * xs[pl.ds(6, tl), :] + pv_ref[0:1, :] * xs[pl.ds(5, tl), :])
    xcb = xc.astype(BF16)
    pa = jnp.concatenate([_dot(xcb[:, 0:256], wa_ref[0]), _dot(xcb[:, 256:512], wa_ref[1])], axis=1)
    px = jnp.concatenate([_dot(xcb[:, 0:256], wx_ref[0]), _dot(xcb[:, 256:512], wx_ref[1])], axis=1)
    r = _sigmoid(pa + pv_ref[5:6, :])
    ig = _sigmoid(px + pv_ref[6:7, :])
    z = -pv_ref[7:8, :]
    sp = jnp.maximum(z, 0.0) + jnp.log1p(jnp.exp(-jnp.abs(z)))
    log_a = (-LRU_C * r) * sp
    a = jnp.exp(log_a)
    s = jnp.tanh(-log_a) * (1.0 + a * a)
    inv_mult = lax.rsqrt(s)
    mult = jnp.where(s > 0.0, s * inv_mult, 0.0)
    return xc, xcb, r, ig, sp, a, mult, inv_mult


def lru_fwd(rest, pvec, wa, wx, name):
    S = rest.shape[0]
    tl = min(LT, S)
    nt = S // tl

    def body(xr_ref, halo_ref, yr_ref, pv_ref, wa_ref, wx_ref, h_ref, hg_ref, xs, a_s, u_s, h_s, carry):
        ti = pl.program_id(1)

        @pl.when(ti == 0)
        def _():
            carry[...] = jnp.zeros_like(carry)

        xs[0:8, :] = jnp.where(ti > 0, halo_ref[8:16, :].astype(F32), 0.0)
        xs[pl.ds(8, tl), :] = xr_ref[...].astype(F32)
        xc, _, _, ig, _, a, mult, _ = _lru_gates(xs, pv_ref, wa_ref, wx_ref, tl)
        a_s[...] = a
        u_s[...] = mult * (ig * xc)
        row = lax.broadcasted_iota(jnp.int32, (8, LC), 0)

        def blk(bi, c):
            o = pl.multiple_of(bi * 8, 8)
            av = a_s[pl.ds(o, 8), :]
            bv = u_s[pl.ds(o, 8), :]
            for d in (1, 2, 4):
                a_sh = pltpu.roll(av, d, 0)
                b_sh = pltpu.roll(bv, d, 0)
                m = row >= d
                bv = jnp.where(m, av * b_sh + bv, bv)
                av = jnp.where(m, av * a_sh, av)
            hv = bv + av * c
            h_s[pl.ds(o, 8), :] = hv
            return hv[7:8, :]

        carry[...] = lax.fori_loop(0, tl // 8, blk, carry[...])
        h = h_s[...]
        h_ref[...] = h
        hg_ref[...] = (h * _gelu(yr_ref[...].astype(F32))).astype(BF16)

    hb = tl // 16
    return pl.pallas_call(
        body, name=name, grid=(2, nt),
        in_specs=[pl.BlockSpec((tl, LC), lambda c, t: (t, c)),
                  pl.BlockSpec((16, LC), lambda c, t: (jnp.maximum(t * hb - 1, 0), c)),
                  pl.BlockSpec((tl, LC), lambda c, t: (t, 2 + c)),
                  pl.BlockSpec((8, LC), lambda c, t: (0, c)),
                  pl.BlockSpec((2, 256, 256), lambda c, t: (c, 0, 0)),
                  pl.BlockSpec((2, 256, 256), lambda c, t: (c, 0, 0))],
        out_specs=[pl.BlockSpec((tl, LC), lambda c, t: (t, c)), pl.BlockSpec((tl, LC), lambda c, t: (t, c))],
        out_shape=[jax.ShapeDtypeStruct((S, D), F32), jax.ShapeDtypeStruct((S, D), BF16)],
        scratch_shapes=[pltpu.VMEM((tl + 8, LC), F32), pltpu.VMEM((tl, LC), F32), pltpu.VMEM((tl, LC), F32),
                        pltpu.VMEM((tl, LC), F32), pltpu.VMEM((1, LC), F32)],
        compiler_params=_cp("parallel", "arbitrary"),
    )(rest, rest, rest, pvec, wa, wx)


def lru_bwd(dh, h, rest, pvec, wa, wx, name):
    S = rest.shape[0]
    tl = min(LT, S)
    nt = S // tl

    def body(dh_ref, h_ref, hhalo_ref, xr_ref, xhalo_ref, pv_ref, wa_ref, wx_ref,
             dxr_ref, vacc_ref, dwa_ref, dwx_ref,
             xs, hs, a_s, ash_s, b_s, lam_s, dxe, anext, lnext, dxnext):
        ti = pl.program_id(1)
        tr = nt - 1 - ti

        @pl.when(ti == 0)
        def _():
            anext[...] = jnp.zeros_like(anext)
            lnext[...] = jnp.zeros_like(lnext)
            dxnext[...] = jnp.zeros_like(dxnext)
            vacc_ref[...] = jnp.zeros_like(vacc_ref)
            dwa_ref[...] = jnp.zeros_like(dwa_ref)
            dwx_ref[...] = jnp.zeros_like(dwx_ref)

        xs[0:8, :] = jnp.where(tr > 0, xhalo_ref[8:16, :].astype(F32), 0.0)
        xs[pl.ds(8, tl), :] = xr_ref[...].astype(F32)
        xc, xcb, r, ig, sp, a, mult, inv_mult = _lru_gates(xs, pv_ref, wa_ref, wx_ref, tl)

        a_s[pl.ds(0, tl), :] = a
        a_s[pl.ds(tl, 8), :] = jnp.broadcast_to(anext[...], (8, LC))
        ash_s[...] = a_s[pl.ds(1, tl), :]
        b_s[...] = dh_ref[...]
        row = lax.broadcasted_iota(jnp.int32, (8, LC), 0)

        def blk(k, c):
            o = pl.multiple_of((tl // 8 - 1 - k) * 8, 8)
            av = ash_s[pl.ds(o, 8), :]
            bv = b_s[pl.ds(o, 8), :]
            for d in (1, 2, 4):
                a_sh = pltpu.roll(av, 8 - d, 0)
                b_sh = pltpu.roll(bv, 8 - d, 0)
                m = row < 8 - d
                bv = jnp.where(m, bv + av * b_sh, bv)
                av = jnp.where(m, av * a_sh, av)
            lv = bv + av * c
            lam_s[pl.ds(o, 8), :] = lv
            return lv[0:1, :]

        lnext[...] = lax.fori_loop(0, tl // 8, blk, lnext[...])
        anext[...] = a[0:1, :]
        lam = lam_s[...]

        hs[0:8, :] = jnp.where(tr > 0, hhalo_ref[...], 0.0)
        hs[pl.ds(8, tl), :] = h_ref[...]
        d_a = lam * hs[pl.ds(7, tl), :]
        d_mult = lam * (ig * xc)
        d_ig = lam * mult * xc
        dxc = lam * mult * ig
        d_log_a = d_a * a - d_mult * (a * a) * inv_mult
        d_r = d_log_a * (-LRU_C * sp)
        vacc_ref[7:8, :] += _colsum(d_log_a * (-LRU_C * r)) * (-_sigmoid(-pv_ref[7:8, :]))
        d_pa = d_r * r * (1.0 - r)
        d_px = d_ig * ig * (1.0 - ig)
        vacc_ref[5:6, :] += _colsum(d_pa)
        vacc_ref[6:7, :] += _colsum(d_px)
        dpa = d_pa.astype(BF16)
        dpx = d_px.astype(BF16)
        back = []
        for g in range(2):
            sl = slice(256 * g, 256 * g + 256)
            dwa_ref[g] += _dot_tn(xcb[:, sl], dpa[:, sl])
            dwx_ref[g] += _dot_tn(xcb[:, sl], dpx[:, sl])
            back.append(_dot_nt(dpa[:, sl], wa_ref[g]) + _dot_nt(dpx[:, sl], wx_ref[g]))
        dxc = dxc + jnp.concatenate(back, axis=1)
        vacc_ref[4:5, :] += _colsum(dxc)
        for k in range(4):
            vacc_ref[k:k + 1, :] += _colsum(dxc * xs[pl.ds(5 + k, tl), :])
        dxe[pl.ds(0, tl), :] = dxc
        dxe[pl.ds(tl, 8), :] = dxnext[...]
        dxr = (pv_ref[3:4, :] * dxc + pv_ref[2:3, :] * dxe[pl.ds(1, tl), :]
               + pv_ref[1:2, :] * dxe[pl.ds(2, tl), :] + pv_ref[0:1, :] * dxe[pl.ds(3, tl), :])
        dxr_ref[...] = dxr.astype(BF16)
        dxnext[...] = dxc[0:8, :]

    hb = tl // 8
    rev = lambda t: nt - 1 - t
    halo = lambda t: jnp.maximum(rev(t) * hb - 1, 0)
    big = lambda: pltpu.VMEM((tl + 8, LC), F32)
    til = lambda: pltpu.VMEM((tl, LC), F32)
    return pl.pallas_call(
        body, name=name, grid=(2, nt),
        in_specs=[pl.BlockSpec((tl, LC), lambda c, t: (rev(t), c)),
                  pl.BlockSpec((tl, LC), lambda c, t: (rev(t), c)),
                  pl.BlockSpec((8, LC), lambda c, t: (halo(t), c)),
                  pl.BlockSpec((tl, LC), lambda c, t: (rev(t), c)),
                  pl.BlockSpec((16, LC), lambda c, t: (jnp.maximum(rev(t) * (tl // 16) - 1, 0), c)),
                  pl.BlockSpec((8, LC), lambda c, t: (0, c)),
                  pl.BlockSpec((2, 256, 256), lambda c, t: (c, 0, 0)),
                  pl.BlockSpec((2, 256, 256), lambda c, t: (c, 0, 0))],
        out_specs=[pl.BlockSpec((tl, LC), lambda c, t: (rev(t), c)),
                   pl.BlockSpec((8, LC), lambda c, t: (0, c)),
                   pl.BlockSpec((2, 256, 256), lambda c, t: (c, 0, 0)),
                   pl.BlockSpec((2, 256, 256), lambda c, t: (c, 0, 0))],
        out_shape=[jax.ShapeDtypeStruct((S, D), BF16), jax.ShapeDtypeStruct((8, D), F32),
                   jax.ShapeDtypeStruct((4, 256, 256), F32), jax.ShapeDtypeStruct((4, 256, 256), F32)],
        scratch_shapes=[big(), big(), big(), til(), til(), til(), big(),
                        pltpu.VMEM((1, LC), F32), pltpu.VMEM((1, LC), F32), pltpu.VMEM((8, LC), F32)],
        compiler_params=_cp("parallel", "arbitrary"),
    )(dh, h, h, rest, rest, pvec, wa, wx)


def mix_out_fwd(x, ao, hg, rest, vec, w_att_o, w_rec_o, w_out, name, tm=256):
    S = x.shape[0]
    tm = min(tm, S)

    def body(x_ref, ao_ref, hg_ref, ga_ref, gr_ref, vec_ref, wa_ref, wr_ref, wo_ref,
             xo_ref, att_ref, rec_ref, mg_ref, f_ref):
        att = _dot(ao_ref[...], wa_ref[...])
        rec = _dot(hg_ref[...], wr_ref[...])
        att_ref[...] = att.astype(BF16)
        rec_ref[...] = rec.astype(BF16)
        mg = (_sigmoid(ga_ref[...].astype(F32)) * att + _sigmoid(gr_ref[...].astype(F32)) * rec).astype(BF16)
        mg_ref[...] = mg
        f = _dot(mg, wo_ref[...])
        f_ref[...] = f
        y = f * lax.rsqrt(_mean(f * f) + EPS) * vec_ref[1:2, :]
        xo_ref[...] = x_ref[...] + (1.0 * vec_ref[4:5, :]) * y

    row = lambda i: (i, 0)
    full = lambda r: pl.BlockSpec((r, D), lambda i: (0, 0))
    return pl.pallas_call(
        body, name=name, grid=(S // tm,),
        in_specs=[pl.BlockSpec((tm, D), row), pl.BlockSpec((tm, 512), row), pl.BlockSpec((tm, D), row),
                  pl.BlockSpec((tm, D), lambda i: (i, 2)), pl.BlockSpec((tm, D), lambda i: (i, 3)),
                  full(8), full(512), full(D), full(D)],
        out_specs=[pl.BlockSpec((tm, D), row)] * 5,
        out_shape=[jax.ShapeDtypeStruct((S, D), F32), jax.ShapeDtypeStruct((S, D), BF16),
                   jax.ShapeDtypeStruct((S, D), BF16), jax.ShapeDtypeStruct((S, D), BF16),
                   jax.ShapeDtypeStruct((S, D), F32)],
        compiler_params=_cp("parallel"),
    )(x, ao, hg, rest, rest, vec, w_att_o, w_rec_o, w_out)


def mix_out_bwd(dxo, f, att, rec, rest, h, vec, w_att_o, w_rec_o, w_out, name, tm=256):
    S = dxo.shape[0]
    tm = min(tm, S)

    def body(dxo_ref, f_ref, att_ref, rec_ref, yr_ref, ga_ref, gr_ref, h_ref, vec_ref, wa_ref, wr_ref, wo_ref,
             df_ref, da_ref, dr_ref, dao_ref, dh_ref, d3_ref, vacc_ref):
        @pl.when(pl.program_id(0) == 0)
        def _():
            vacc_ref[...] = jnp.zeros_like(vacc_ref)

        df = _post_norm_bwd(dxo_ref[...], f_ref[...], 1.0, vec_ref, vacc_ref).astype(BF16)
        df_ref[...] = df
        dm = _dot_nt(df, wo_ref[...])
        sa = _sigmoid(ga_ref[...].astype(F32))
        sr = _sigmoid(gr_ref[...].astype(F32))
        d_att = (dm * sa).astype(BF16)
        d_rec = (dm * sr).astype(BF16)
        da_ref[...] = d_att
        dr_ref[...] = d_rec
        d3_ref[1] = (dm * att_ref[...].astype(F32) * (sa * (1.0 - sa))).astype(BF16)
        d3_ref[2] = (dm * rec_ref[...].astype(F32) * (sr * (1.0 - sr))).astype(BF16)
        dao_ref[...] = _dot_nt(d_att, wa_ref[...]).astype(BF16)
        d_hg = _dot_nt(d_rec, wr_ref[...])
        yr = yr_ref[...].astype(F32)
        t = jnp.tanh(_GK * (yr + 0.044715 * yr * yr * yr))
        dh_ref[...] = d_hg * (0.5 * yr * (1.0 + t))
        gelu_grad = 0.5 * (1.0 + t) + 0.5 * yr * (1.0 - t * t) * _GK * (1.0 + 3.0 * 0.044715 * yr * yr)
        d3_ref[0] = (d_hg * h_ref[...] * gelu_grad).astype(BF16)

    row = lambda i: (i, 0)
    full = lambda r: pl.BlockSpec((r, D), lambda i: (0, 0))
    return pl.pallas_call(
        body, name=name, grid=(S // tm,),
        in_specs=[pl.BlockSpec((tm, D), row)] * 4
        + [pl.BlockSpec((tm, D), lambda i: (i, 1)), pl.BlockSpec((tm, D), lambda i: (i, 2)),
           pl.BlockSpec((tm, D), lambda i: (i, 3)), pl.BlockSpec((tm, D), row),
           full(8), full(512), full(D), full(D)],
        out_specs=[pl.BlockSpec((tm, D), row)] * 3
        + [pl.BlockSpec((tm, 512), row), pl.BlockSpec((tm, D), row),
           pl.BlockSpec((3, tm, D), lambda i: (0, i, 0)), pl.BlockSpec((8, D), lambda i: (0, 0))],
        out_shape=[jax.ShapeDtypeStruct((S, D), BF16)] * 3
        + [jax.ShapeDtypeStruct((S, 512), BF16), jax.ShapeDtypeStruct((S, D), F32),
           jax.ShapeDtypeStruct((3, S, D), BF16), jax.ShapeDtypeStruct((8, D), F32)],
        compiler_params=_cp("arbitrary"),
    )(dxo, f, att, rec, rest, rest, rest, h, vec, w_att_o, w_rec_o, w_out)


def dw_in(h, dq, dkv, dxr, d3, name, tk=1024, tn=512):
    S = h.shape[0]
    tk = min(tk, S)
    nk = S // tk

    def body(h_ref, dq_ref, dkv_ref, dxr_ref, d3_ref, o_ref, acc):
        j, k = pl.program_id(0), pl.program_id(1)

        @pl.when(k == 0)
        def _():
            acc[...] = jnp.zeros_like(acc)

        @pl.when(j == 0)
        def _():
            acc[...] += _dot_tn(h_ref[...], dq_ref[...])

        @pl.when((j >= 1) & (j < 3))
        def _():
            acc[...] += _dot_tn(h_ref[...], dkv_ref[...])

        @pl.when((j >= 3) & (j < 5))
        def _():
            acc[...] += _dot_tn(h_ref[...], dxr_ref[...])

        @pl.when(j >= 5)
        def _():
            acc[...] += _dot_tn(h_ref[...], d3_ref[...])

        @pl.when(k == nk - 1)
        def _():
            o_ref[...] = acc[...].astype(BF16)

    use = lambda j, k, lo, hi: jnp.where((j >= lo) & (j < hi), k, 0)
    g3 = lambda j: jnp.clip(j - 5, 0, 5)
    return pl.pallas_call(
        body, name=name, grid=(PW // tn, nk),
        in_specs=[pl.BlockSpec((tk, D), lambda j, k: (k, 0)),
                  pl.BlockSpec((None, tk, tn), lambda j, k: (0, use(j, k, 0, 1), 0)),
                  pl.BlockSpec((None, tk, tn), lambda j, k: (jnp.clip(j - 1, 0, 1), use(j, k, 1, 3), 0)),
                  pl.BlockSpec((tk, tn), lambda j, k: (use(j, k, 3, 5), jnp.clip(j - 3, 0, 1))),
                  pl.BlockSpec((None, tk, tn), lambda j, k: (g3(j) // 2, use(j, k, 5, 11), g3(j) % 2))],
        out_specs=pl.BlockSpec((D, tn), lambda j, k: (0, j)),
        out_shape=jax.ShapeDtypeStruct((D, PW), BF16),
        scratch_shapes=[pltpu.VMEM((D, tn), F32)],
        compiler_params=_cp("parallel", "arbitrary"),
    )(h, dq, dkv, dxr, d3)


def ada_fwd(c_all, w_ada, b_ada, name, tn=768):
    n = w_ada.shape[1]

    def body(c_ref, w_ref, b_ref, o_ref):
        cv = c_ref[...]
        ca = (cv * _sigmoid(cv)).astype(BF16)
        o_ref[...] = _dot(ca, w_ref[...].astype(BF16)) + b_ref[...]

    return pl.pallas_call(
        body, name=name, grid=(n // tn,),
        in_specs=[pl.BlockSpec((8, D), lambda j: (0, 0)), pl.BlockSpec((D, tn), lambda j: (0, j)),
                  pl.BlockSpec((1, tn), lambda j: (0, j))],
        out_specs=pl.BlockSpec((8, tn), lambda j: (0, j)),
        out_shape=jax.ShapeDtypeStruct((8, n), F32),
        compiler_params=_cp("parallel"),
    )(c_all, w_ada, b_ada)


def ada_bwd(c_all_t, dmod, name, tn=768):
    n = dmod.shape[1]

    def body(c_ref, d_ref, o_ref):
        cv = c_ref[...]
        ca = (cv * _sigmoid(cv)).astype(BF16)
        o_ref[...] = _dot(ca, d_ref[...].astype(BF16))

    return pl.pallas_call(
        body, name=name, grid=(n // tn,),
        in_specs=[pl.BlockSpec((D, 128), lambda j: (0, 0)), pl.BlockSpec((128, tn), lambda j: (0, j))],
        out_specs=pl.BlockSpec((D, tn), lambda j: (0, j)),
        out_shape=jax.ShapeDtypeStruct((D, n), F32),
        compiler_params=_cp("parallel"),
    )(c_all_t, dmod)


def _row_tile(rows, cols, itemsize=4, budget=1536 * 1024):
    best = None
    for t in range(8, rows + 1, 8):
        if rows % t == 0 and t * cols * itemsize <= budget:
            best = t
    return rows if best is None else best


def sum_lead(parts, name, out_dtype=F32):
    n, R, C = parts.shape
    tr = _row_tile(R, C * n)

    def body(p_ref, o_ref):
        acc = p_ref[0].astype(F32)
        for k in range(1, n):
            acc = acc + p_ref[k].astype(F32)
        o_ref[...] = acc.astype(out_dtype)

    return pl.pallas_call(
        body, name=name, grid=(R // tr,),
        in_specs=[pl.BlockSpec((n, tr, C), lambda i: (0, i, 0))],
        out_specs=pl.BlockSpec((tr, C), lambda i: (i, 0)),
        out_shape=jax.ShapeDtypeStruct((R, C), out_dtype),
        compiler_params=_cp("parallel"),
    )(parts)


def adamw(w, g, m, v, name):
    R, C = w.shape
    tr = _row_tile(R, C * 7, budget=8 * 1024 * 1024)

    def body(w_ref, g_ref, m_ref, v_ref, d_ref, mo_ref, vo_ref):
        gv = g_ref[...]
        mn = ADAM_B1 * m_ref[...] + (1.0 - ADAM_B1) * gv
        vn = ADAM_B2 * v_ref[...] + (1.0 - ADAM_B2) * (gv * gv)
        m_hat = mn / (1.0 - ADAM_B1 ** ADAM_STEP)
        v_hat = vn / (1.0 - ADAM_B2 ** ADAM_STEP)
        d_ref[...] = -ADAM_LR * (m_hat / (jnp.sqrt(v_hat) + ADAM_EPS) + ADAM_WD * w_ref[...])
        mo_ref[...] = mn
        vo_ref[...] = vn

    spec = pl.BlockSpec((tr, C), lambda i: (i, 0))
    return pl.pallas_call(
        body, name=name, grid=(R // tr,),
        in_specs=[spec] * 4, out_specs=[spec] * 3,
        out_shape=[jax.ShapeDtypeStruct((R, C), F32)] * 3,
        compiler_params=_cp("parallel"),
    )(w, g, m, v)


def _mesh_pos():
    return lax.axis_index("x"), lax.axis_index("y"), lax.axis_index("c")


def _other_chips(mx, my):
    return [(1 - mx, my), (mx, 1 - my), (1 - mx, 1 - my)]


def ag_small(x, name):
    R = x.shape[0]

    def body(x_ref, out_ref, send_sems, recv_sems, local_sem):
        mx, my, mc = _mesh_pos()
        me, sibling = (mx, my, mc), (mx, my, 1 - mc)
        chips = _other_chips(mx, my)

        def slot(px, py, pc):
            return out_ref.at[4 * px + 2 * py + pc]

        def copy(k, block, to, src=None):
            return pltpu.make_async_remote_copy(
                src_ref=slot(*block) if src is None else src, dst_ref=slot(*block),
                send_sem=send_sems.at[k], recv_sem=recv_sems.at[k], device_id=to, device_id_type=MESH)

        mine = pltpu.make_async_copy(x_ref, slot(*me), local_sem)
        mine.start()
        first = [copy(0, me, sibling, src=x_ref)]
        first += [copy(1 + j, me, (*chip, mc), src=x_ref) for j, chip in enumerate(chips)]
        for cp in first:
            cp.start()
        passed = [copy(4 + j, (*chip, mc), sibling) for j, chip in enumerate(chips)]
        for j, chip in enumerate(chips):
            copy(1 + j, (*chip, mc), me).wait_recv()
            passed[j].start()
        copy(0, sibling, me).wait_recv()
        for j, chip in enumerate(chips):
            copy(4 + j, (*chip, 1 - mc), me).wait_recv()
        for cp in first + passed:
            cp.wait_send()
        mine.wait()

    return pl.pallas_call(
        body, name=name,
        out_shape=jax.ShapeDtypeStruct((N_DEV, R, 128), F32),
        in_specs=[pl.BlockSpec(memory_space=pltpu.VMEM)],
        out_specs=pl.BlockSpec(memory_space=pltpu.VMEM),
        scratch_shapes=[pltpu.SemaphoreType.DMA((7,)), pltpu.SemaphoreType.DMA((7,)), pltpu.SemaphoreType.DMA],
        compiler_params=pltpu.CompilerParams(vmem_limit_bytes=VMEM_LIMIT),
    )(x)


BIG = (("ffn1_w_gu", "col", D, PW), ("ffn1_w_down", "row", FF, D), ("w_in", "col", D, PW),
       ("w_att_o", "col", 512, D), ("w_rec_o", "row", D, D), ("w_out", "row", D, D),
       ("ffn2_w_gu", "col", D, PW), ("ffn2_w_down", "row", FF, D))
NBIG = len(BIG)


def _shard_shape(kind, R, C):
    return (R, C // 4) if kind == "col" else (R // 4, C)


def _region(ref, kind, R, C, q, half, t, tr):
    sr, sc = _shard_shape(kind, R, C)
    if kind == "col":
        return ref.at[pl.ds(pl.multiple_of(half * (R // 2) + t * tr, 16), tr), pl.ds(q * sc, sc)]
    return ref.at[pl.ds(pl.multiple_of(q * sr + t * tr, 16), tr), pl.ds(half * (C // 2), C // 2)]


def ag_local(w, kind, R, C, p_arr, name, after=()):
    sr, sc = _shard_shape(kind, R, C)
    tr = _row_tile(sr, sc, budget=2 * 1024 * 1024)
    nt = sr // tr
    after = list(after)

    def body(p_ref, w_ref, *rest):
        rest[-1][...] = w_ref[...].astype(BF16)

    if kind == "col":
        o_spec = pl.BlockSpec((tr, sc), lambda i, p: (i, p[0]))
    else:
        o_spec = pl.BlockSpec((tr, sc), lambda i, p: (p[0] * nt + i, 0))
    return pl.pallas_call(
        body, name=name,
        grid_spec=pltpu.PrefetchScalarGridSpec(
            num_scalar_prefetch=1, grid=(nt,),
            in_specs=[pl.BlockSpec((tr, sc), lambda i, p: (i, 0))] + [ANY] * len(after), out_specs=o_spec),
        out_shape=jax.ShapeDtypeStruct((R, C), BF16),
        compiler_params=_cp("parallel"),
    )(p_arr, w, *after)


HBM_SPEC = pl.BlockSpec(memory_space=pltpu.HBM)
SEM_SPEC = pl.BlockSpec(memory_space=pltpu.SEMAPHORE)


def _ag_sems(geoms):
    return sum(6 if both else 3 for (_, _, _, both) in geoms)


def _ag_copies(fulls, geoms, ssem, rsem, mx, my, mc, q, h):
    chips = _other_chips(mx, my)
    out, base = [], 0
    for w, (kind, R, C, both) in enumerate(geoms):
        sr, sc = _shard_shape(kind, R, C)
        hr = sr // 2 if kind == "col" else sr
        reg = _region(fulls[w], kind, R, C, q, h, 0, hr)
        out.append([pltpu.make_async_remote_copy(
            src_ref=reg, dst_ref=reg, send_sem=ssem.at[base + 3 * t + k], recv_sem=rsem.at[base + 3 * t + k],
            device_id=(*chips[k], mc if t == 0 else 1 - mc), device_id_type=MESH)
            for t in range(2 if both else 1) for k in range(3)])
        base += 6 if both else 3
    return out


def ag_start(fulls, geoms, after, name):
    n = len(fulls)
    after = list(after)
    m = len(after)

    def body(*refs):
        ssem, rsem = refs[n + m:n + m + 2]
        outs, token = refs[n + m + 2:2 * n + m + 2], refs[2 * n + m + 2]
        mx, my, mc = _mesh_pos()
        p = 2 * mx + my
        col = [w for w, g in enumerate(geoms) if g[0] == "col"]
        row = [w for w, g in enumerate(geoms) if g[0] == "row"]
        for q in range(4):
            @pl.when(p == q)
            def _(q=q):
                cps = _ag_copies(outs, geoms, ssem, rsem, mx, my, mc, q, mc)
                for w in col:
                    for cp in cps[w]:
                        cp.start()
        for h in range(2):
            @pl.when(mc == h)
            def _(h=h):
                cps = _ag_copies(outs, geoms, ssem, rsem, mx, my, mc, p, h)
                for w in row:
                    for cp in cps[w]:
                        cp.start()
        token[...] = jnp.zeros_like(token)

    res = pl.pallas_call(
        body, name=name,
        out_shape=[pltpu.SemaphoreType.DMA((_ag_sems(geoms),)), pltpu.SemaphoreType.DMA((_ag_sems(geoms),))]
        + [pltpu.HBM(a.shape, a.dtype) for a in fulls] + [jax.ShapeDtypeStruct((8, 128), F32)],
        in_specs=[HBM_SPEC] * n + [ANY] * m,
        out_specs=[SEM_SPEC, SEM_SPEC] + [HBM_SPEC] * n + [pl.BlockSpec(memory_space=pltpu.VMEM)],
        input_output_aliases={w: 2 + w for w in range(n)},
        compiler_params=pltpu.CompilerParams(has_side_effects=pltpu.SideEffectType.DATAFLOW_SIDE_EFFECTING),
    )(*[pltpu.with_memory_space_constraint(a, pltpu.HBM) for a in fulls], *after)
    return res[0], res[1], list(res[2:2 + n]), res[2 + n]


def ag_wait(fulls, geoms, ssem, rsem, after, name):
    n = len(fulls)

    def body(*refs):
        ins, ssem_ref, rsem_ref = refs[:n], refs[n], refs[n + 1]
        mx, my, mc = _mesh_pos()
        for cps in _ag_copies(ins, geoms, ssem_ref, rsem_ref, mx, my, mc, 0, 0):
            for cp in cps:
                cp.wait_send()
                cp.wait_recv()

    return list(pl.pallas_call(
        body, name=name,
        out_shape=[pltpu.HBM(a.shape, a.dtype) for a in fulls],
        in_specs=[HBM_SPEC] * n + [SEM_SPEC, SEM_SPEC, ANY],
        out_specs=[HBM_SPEC] * n,
        input_output_aliases={w: w for w in range(n)},
        compiler_params=pltpu.CompilerParams(has_side_effects=pltpu.SideEffectType.DATAFLOW_SIDE_EFFECTING),
    )(*fulls, ssem, rsem, after))


def ag_forward(full, kind, R, C, name):
    sr, sc = _shard_shape(kind, R, C)
    hr, hc = (sr // 2, sc) if kind == "col" else (sr, sc // 2)
    tr = _row_tile(hr, hc, itemsize=2, budget=512 * 1024)
    nt = hr // tr

    total = 3 * nt

    def body(src_ref, full_ref, stage, lsem, ssem, rsem):
        step = pl.program_id(0) * nt + pl.program_id(1)
        par = step % 2
        mx, my, mc = _mesh_pos()

        def load(s, q, h, t):
            return pltpu.make_async_copy(_region(src_ref, kind, R, C, q, h, t, tr), stage.at[s], lsem.at[s])

        def push(s, q, h, t):
            return pltpu.make_async_remote_copy(src_ref=stage.at[s], dst_ref=_region(full_ref, kind, R, C, q, h, t, tr),
                                                send_sem=ssem.at[s], recv_sem=rsem, device_id=(mx, my, 1 - mc),
                                                device_id_type=MESH)

        def for_tile(stp, fn):
            q_k = _partner_chip(stp // nt, 2 * mx + my)
            if kind == "col":
                for q in range(4):
                    @pl.when(q_k == q)
                    def _(q=q):
                        fn(q, mc, stp % nt)
            else:
                for h in range(2):
                    @pl.when(mc == h)
                    def _(h=h):
                        fn(q_k, h, stp % nt)

        @pl.when(step == 0)
        def _():
            for_tile(step, lambda q, h, t: load(0, q, h, t).start())

        load(par, 0, 0, 0).wait()
        for_tile(step, lambda q, h, t: push(par, q, h, t).start())

        @pl.when(step + 1 < total)
        def _():
            @pl.when(step >= 1)
            def _():
                push(1 - par, 0, 0, 0).wait_send()
            for_tile(step + 1, lambda q, h, t: load(1 - par, q, h, t).start())

        @pl.when(step == total - 1)
        def _():
            push(par, 0, 0, 0).wait_send()
            push(1 - par, 0, 0, 0).wait_send()
            three = full_ref.at[pl.ds(0, hr), pl.ds(0, 3 * hc)] if kind == "col" else full_ref.at[pl.ds(0, 3 * hr), pl.ds(0, hc)]
            pltpu.make_async_remote_copy(src_ref=three, dst_ref=three, send_sem=ssem.at[0], recv_sem=rsem,
                                         device_id=(mx, my, 1 - mc), device_id_type=MESH).wait_recv()

    return pl.pallas_call(
        body, name=name, grid=(3, nt),
        in_specs=[ANY], out_specs=ANY,
        out_shape=jax.ShapeDtypeStruct((R, C), BF16),
        scratch_shapes=[pltpu.VMEM((2, tr, hc), BF16), pltpu.SemaphoreType.DMA((2,)), pltpu.SemaphoreType.DMA((2,)),
                        pltpu.SemaphoreType.DMA],
        input_output_aliases={0: 0},
        compiler_params=_cp("arbitrary", "arbitrary"),
    )(full)


def _half_shape(kind, R, C):
    return (R // 2, C) if kind == "col" else (R, C // 2)


def _piece_shape(kind, R, C):
    return (R // 2, C // 4) if kind == "col" else (R // 4, C // 2)


def pair_push(g, kind, c_arr, name):
    R, C = g.shape
    hr, hc = _half_shape(kind, R, C)
    tr = _row_tile(hr, hc, itemsize=2, budget=1024 * 1024)
    nt = hr // tr

    def body(c_ref, g_ref, out_ref, stage, ssem, rsem):
        i = pl.program_id(0)
        slot = i % 2
        mx, my, mc = _mesh_pos()

        def push(s, t):
            return pltpu.make_async_remote_copy(
                src_ref=stage.at[s], dst_ref=out_ref.at[pl.ds(pl.multiple_of(t * tr, 16), tr)],
                send_sem=ssem.at[s], recv_sem=rsem, device_id=(mx, my, 1 - mc), device_id_type=MESH)

        @pl.when(i >= 2)
        def _():
            push(slot, 0).wait_send()

        stage[slot] = g_ref[...]
        push(slot, i).start()

        @pl.when(i == nt - 1)
        def _():
            push(slot, 0).wait_send()
            if nt >= 2:
                push(1 - slot, 0).wait_send()
            pltpu.make_async_remote_copy(src_ref=out_ref, dst_ref=out_ref, send_sem=ssem.at[0], recv_sem=rsem,
                                         device_id=(mx, my, 1 - mc), device_id_type=MESH).wait_recv()

    if kind == "col":
        g_spec = pl.BlockSpec((tr, hc), lambda i, c: ((1 - c[0]) * nt + i, 0))
    else:
        g_spec = pl.BlockSpec((tr, hc), lambda i, c: (i, 1 - c[0]))
    return pl.pallas_call(
        body, name=name,
        grid_spec=pltpu.PrefetchScalarGridSpec(
            num_scalar_prefetch=1, grid=(nt,), in_specs=[g_spec], out_specs=ANY,
            scratch_shapes=[pltpu.VMEM((2, tr, hc), BF16), pltpu.SemaphoreType.DMA((2,)), pltpu.SemaphoreType.DMA]),
        out_shape=jax.ShapeDtypeStruct((hr, hc), BF16),
        compiler_params=_cp("arbitrary"),
    )(c_arr, g)


def _partner_chip(k, p):
    return p ^ jnp.where(k == 0, 2, jnp.where(k == 1, 1, jnp.where(k == 2, 3, 0)))


def pair_add(g, got, kind, cp_arr, name):
    R, C = g.shape
    pr, pc = _piece_shape(kind, R, C)
    tr = _row_tile(pr, pc, itemsize=2, budget=1024 * 1024)
    nt = pr // tr

    def body(cp_ref, g_ref, got_ref, ps_ref, rb_ref):
        tile = (g_ref[...].astype(F32) + got_ref[...].astype(F32)).astype(BF16)
        ps_ref[...] = tile

        @pl.when(pl.program_id(1) == cp_ref[1])
        def _():
            rb_ref[...] = tile

    if kind == "col":
        g_spec = pl.BlockSpec((tr, pc), lambda i, q, cp: (cp[0] * nt + i, q))
        got_spec = pl.BlockSpec((tr, pc), lambda i, q, cp: (i, q))
    else:
        g_spec = pl.BlockSpec((tr, pc), lambda i, q, cp: (q * nt + i, cp[0]))
        got_spec = pl.BlockSpec((tr, pc), lambda i, q, cp: (q * nt + i, 0))
    return pl.pallas_call(
        body, name=name,
        grid_spec=pltpu.PrefetchScalarGridSpec(
            num_scalar_prefetch=1, grid=(nt, 4), in_specs=[g_spec, got_spec],
            out_specs=[pl.BlockSpec((None, tr, pc), lambda i, q, cp: (q, i, 0)),
                       pl.BlockSpec((None, tr, pc), lambda i, q, cp: (cp[1], i, 0))]),
        out_shape=[jax.ShapeDtypeStruct((4, pr, pc), BF16)] * 2,
        compiler_params=_cp("arbitrary", "arbitrary"),
    )(cp_arr, g, got)


def _rs_copies(ps, rb, ssem, rsem, mx, my, mc):
    p = 2 * mx + my
    out = []
    for w in range(len(ps)):
        for k, chip in enumerate(_other_chips(mx, my)):
            out.append(pltpu.make_async_remote_copy(
                src_ref=ps[w].at[2 * chip[0] + chip[1]], dst_ref=rb[w].at[p], send_sem=ssem.at[3 * w + k],
                recv_sem=rsem.at[3 * w + k], device_id=(*chip, mc), device_id_type=MESH))
    return out


def rs_start(ps, rb, after, name):
    n = len(ps)
    after = list(after)
    m = len(after)

    def body(*refs):
        ssem, rsem = refs[2 * n + m:2 * n + m + 2]
        ps_o = refs[2 * n + m + 2:3 * n + m + 2]
        rb_o = refs[3 * n + m + 2:4 * n + m + 2]
        token = refs[4 * n + m + 2]
        for cp in _rs_copies(ps_o, rb_o, ssem, rsem, *_mesh_pos()):
            cp.start()
        token[...] = jnp.zeros_like(token)

    both = list(ps) + list(rb)
    res = pl.pallas_call(
        body, name=name,
        out_shape=[pltpu.SemaphoreType.DMA((3 * n,)), pltpu.SemaphoreType.DMA((3 * n,))]
        + [pltpu.HBM(a.shape, a.dtype) for a in both] + [jax.ShapeDtypeStruct((8, 128), F32)],
        in_specs=[HBM_SPEC] * (2 * n) + [ANY] * m,
        out_specs=[SEM_SPEC, SEM_SPEC] + [HBM_SPEC] * (2 * n) + [pl.BlockSpec(memory_space=pltpu.VMEM)],
        input_output_aliases={w: 2 + w for w in range(2 * n)},
        compiler_params=pltpu.CompilerParams(has_side_effects=pltpu.SideEffectType.DATAFLOW_SIDE_EFFECTING),
    )(*[pltpu.with_memory_space_constraint(a, pltpu.HBM) for a in both], *after)
    return res[0], res[1], list(res[2:2 + n]), list(res[2 + n:2 + 2 * n]), res[2 + 2 * n]


def rs_wait(ps, rb, ssem, rsem, after, name):
    n = len(ps)
    after = list(after)
    m = len(after)

    def body(*refs):
        ps_i, rb_i = refs[:n], refs[n:2 * n]
        ssem_ref, rsem_ref = refs[2 * n], refs[2 * n + 1]
        for cp in _rs_copies(ps_i, rb_i, ssem_ref, rsem_ref, *_mesh_pos()):
            cp.wait_send()
            cp.wait_recv()

    both = list(ps) + list(rb)
    res = pl.pallas_call(
        body, name=name,
        out_shape=[pltpu.HBM(a.shape, a.dtype) for a in both],
        in_specs=[HBM_SPEC] * (2 * n) + [SEM_SPEC, SEM_SPEC] + [ANY] * m,
        out_specs=[HBM_SPEC] * (2 * n),
        input_output_aliases={w: w for w in range(2 * n)},
        compiler_params=pltpu.CompilerParams(has_side_effects=pltpu.SideEffectType.DATAFLOW_SIDE_EFFECTING),
    )(*both, ssem, rsem, *after)
    return list(res[n:])


def sum_share(parts, kind, R, C, name):
    _, pr, pc = parts.shape
    sr, sc = _shard_shape(kind, R, C)
    tr = _row_tile(pr, pc * 4, budget=4 * 1024 * 1024)
    nt = pr // tr

    def body(p_ref, fin_ref, stage, lsem, ssem, rsem):
        i = pl.program_id(0)
        slot = i % 2
        mx, my, mc = _mesh_pos()

        def region(h, t):
            r0 = pl.multiple_of(t * tr, 8)
            if kind == "col":
                return fin_ref.at[pl.ds(pl.multiple_of(h * pr + r0, 8), tr)]
            return fin_ref.at[pl.ds(r0, tr), pl.ds(h * pc, pc)]

        def copies(s, h, t):
            return (pltpu.make_async_copy(stage.at[s], region(h, t), lsem.at[s]),
                    pltpu.make_async_remote_copy(src_ref=stage.at[s], dst_ref=region(h, t), send_sem=ssem.at[s],
                                                 recv_sem=rsem, device_id=(mx, my, 1 - mc), device_id_type=MESH))

        def wait_sent(s):
            loc, rem = copies(s, 0, 0)
            loc.wait()
            rem.wait_send()

        @pl.when(i >= 2)
        def _():
            wait_sent(slot)

        acc = p_ref[0].astype(F32)
        for k in range(1, 4):
            acc = acc + p_ref[k].astype(F32)
        stage[slot] = acc
        if kind == "col":
            for cp in copies(slot, mc, i):
                cp.start()
        else:
            for h in range(2):
                @pl.when(mc == h)
                def _(h=h):
                    for cp in copies(slot, h, i):
                        cp.start()

        @pl.when(i == nt - 1)
        def _():
            wait_sent(slot)
            if nt >= 2:
                wait_sent(1 - slot)
            half = fin_ref.at[pl.ds(0, pr), pl.ds(0, pc)]
            pltpu.make_async_remote_copy(src_ref=half, dst_ref=half, send_sem=ssem.at[0], recv_sem=rsem,
                                         device_id=(mx, my, 1 - mc), device_id_type=MESH).wait_recv()

    return pl.pallas_call(
        body, name=name, grid=(nt,),
        in_specs=[pl.BlockSpec((4, tr, pc), lambda i: (0, i, 0))],
        out_specs=ANY,
        out_shape=jax.ShapeDtypeStruct((sr, sc), F32),
        scratch_shapes=[pltpu.VMEM((2, tr, pc), F32), pltpu.SemaphoreType.DMA((2,)), pltpu.SemaphoreType.DMA((2,)),
                        pltpu.SemaphoreType.DMA],
        compiler_params=_cp("arbitrary"),
    )(parts)


def _pack(parts, rows):
    flat = []
    for a in parts:
        a = jnp.ravel(a).astype(F32)
        flat.append(jnp.pad(a, (0, (-a.shape[0]) % 128)))
    v = jnp.concatenate(flat)
    return jnp.pad(v, (0, rows * 128 - v.shape[0])).reshape(rows, 128)


def _unpack(block, shapes):
    lead = block.shape[:-2]
    v = block.reshape(lead + (-1,))
    out, off = [], 0
    for shp in shapes:
        n = int(np.prod(shp))
        out.append(v[..., off:off + n].reshape(lead + tuple(shp)))
        off += n + (-n) % 128
    return out


def _block_diag4(w):
    w4 = w.reshape(4, 4, 64, 64)
    eye = jnp.eye(4, dtype=w.dtype)
    return (w4[:, :, :, None, :] * eye[None, :, None, :, None]).reshape(4, 256, 256)


def _diag_blocks(bd):
    b5 = bd.reshape(4, 4, 64, 4, 64)
    return jnp.stack([b5[:, i, :, i, :] for i in range(4)], axis=1).reshape(16, 64, 64)


def _bias_window(rel_bias):
    m = (np.arange(768) + 127) % 768 - 127
    w = rel_bias[:, np.clip(512 - m, -128, 128) + 128]
    win = jnp.tile(w, (1, 128))[:, :128 * 767].reshape(8, 128, 767)[:, :, :WIN]
    qh = np.arange(128)[:, None] // CHUNK
    kc = np.arange(WIN)[None, :] // CHUNK
    valid = (kc >= qh) & (kc <= qh + 8)
    return jnp.where(jnp.asarray(valid)[None], win, NEG)


SMALL = ("b_ada", "norm_pre", "norm_post", "rel_bias", "conv_w", "conv_b", "lru_wa", "lru_ba", "lru_wx",
         "lru_bx", "lru_lambda")
WEIGHTS = ("w_ada", "b_ada", "norm_pre", "norm_post", "ffn1_w_gu", "ffn1_w_down", "w_in", "rel_bias", "conv_w",
           "conv_b", "lru_wa", "lru_ba", "lru_wx", "lru_bx", "lru_lambda", "w_att_o", "w_rec_o", "w_out",
           "ffn2_w_gu", "ffn2_w_down")


def kernel(x, c, w_ada, b_ada, norm_pre, norm_post, ffn1_w_gu, ffn1_w_down, w_in, rel_bias, conv_w, conv_b, lru_wa, lru_ba, lru_wx, lru_bx, lru_lambda, w_att_o, w_rec_o, w_out, ffn2_w_gu, ffn2_w_down, loss_target, m_w_ada, m_b_ada, m_norm_pre, m_norm_post, m_ffn1_w_gu, m_ffn1_w_down, m_w_in, m_rel_bias, m_conv_w, m_conv_b, m_lru_wa, m_lru_ba, m_lru_wx, m_lru_bx, m_lru_lambda, m_w_att_o, m_w_rec_o, m_w_out, m_ffn2_w_gu, m_ffn2_w_down, v_w_ada, v_b_ada, v_norm_pre, v_norm_post, v_ffn1_w_gu, v_ffn1_w_down, v_w_in, v_rel_bias, v_conv_w, v_conv_b, v_lru_wa, v_lru_ba, v_lru_wx, v_lru_bx, v_lru_lambda, v_w_att_o, v_w_rec_o, v_w_out, v_ffn2_w_gu, v_ffn2_w_down):
    W = dict(w_ada=w_ada, b_ada=b_ada, norm_pre=norm_pre, norm_post=norm_post, ffn1_w_gu=ffn1_w_gu,
             ffn1_w_down=ffn1_w_down, w_in=w_in, rel_bias=rel_bias, conv_w=conv_w, conv_b=conv_b, lru_wa=lru_wa,
             lru_ba=lru_ba, lru_wx=lru_wx, lru_bx=lru_bx, lru_lambda=lru_lambda, w_att_o=w_att_o, w_rec_o=w_rec_o,
             w_out=w_out, ffn2_w_gu=ffn2_w_gu, ffn2_w_down=ffn2_w_down)
    M = dict(w_ada=m_w_ada, b_ada=m_b_ada, norm_pre=m_norm_pre, norm_post=m_norm_post, ffn1_w_gu=m_ffn1_w_gu,
             ffn1_w_down=m_ffn1_w_down, w_in=m_w_in, rel_bias=m_rel_bias, conv_w=m_conv_w, conv_b=m_conv_b,
             lru_wa=m_lru_wa, lru_ba=m_lru_ba, lru_wx=m_lru_wx, lru_bx=m_lru_bx, lru_lambda=m_lru_lambda,
             w_att_o=m_w_att_o, w_rec_o=m_w_rec_o, w_out=m_w_out, ffn2_w_gu=m_ffn2_w_gu, ffn2_w_down=m_ffn2_w_down)
    V = dict(w_ada=v_w_ada, b_ada=v_b_ada, norm_pre=v_norm_pre, norm_post=v_norm_post, ffn1_w_gu=v_ffn1_w_gu,
             ffn1_w_down=v_ffn1_w_down, w_in=v_w_in, rel_bias=v_rel_bias, conv_w=v_conv_w, conv_b=v_conv_b,
             lru_wa=v_lru_wa, lru_ba=v_lru_ba, lru_wx=v_lru_wx, lru_bx=v_lru_bx, lru_lambda=v_lru_lambda,
             w_att_o=v_w_att_o, w_rec_o=v_w_rec_o, w_out=v_w_out, ffn2_w_gu=v_ffn2_w_gu, ffn2_w_down=v_ffn2_w_down)
    mx, my, mc = _mesh_pos()
    p = 2 * mx + my
    e = 4 * mx + 2 * my + mc
    xs = x[0]

    c_arr = jnp.reshape(mc, (1,)).astype(jnp.int32)
    cp_arr = jnp.stack([mc, p]).astype(jnp.int32)
    p_arr = jnp.reshape(p, (1,)).astype(jnp.int32)
    direct = ("w_att_o", "w_rec_o", "w_out", "ffn2_w_gu", "ffn2_w_down")
    geoms = [(kind, R, C, n in direct) for (n, kind, R, C) in BIG]
    names = [b[0] for b in BIG]
    placed = [ag_local(W[n][0], kind, R, C, p_arr, "ag_local_" + n) for (n, kind, R, C) in BIG[:2]]

    def arrived(fly, lo, hi, ssem, rsem, after, tag):
        done = ag_wait(fly, geoms[lo:hi], ssem, rsem, after, "ag_wait_" + tag)
        return [a if both else ag_forward(a, kind, R, C, "ag_forward_" + n)
                for a, (kind, R, C, both), n in zip(done, geoms[lo:hi], names[lo:hi])]

    g1 = ag_small(_pack([c, norm_pre, norm_post, conv_w], 32), "ag_small_params")
    c_all, npre4, npost4, cw4 = _unpack(g1, [(D,), (3, 256), (3, 256), (4, 256)])
    chipwise = lambda a: jnp.moveaxis(a[0::2], 0, 1).reshape(a.shape[1], D)
    npre, npost, conv_full = chipwise(npre4), chipwise(npost4), chipwise(cw4)

    b_cols = lax.dynamic_slice(b_ada, (0, p * 2304), (1, 2304))
    mod_cols = ada_fwd(c_all, w_ada[0], b_cols, "ada_fwd")
    g2 = ag_small(mod_cols.reshape(144, 128), "ag_mod")
    mod_all = jnp.moveaxis(g2[0::2].reshape(4, 8, 2304), 0, 1).reshape(8, 9 * D)
    mod = lax.dynamic_index_in_dim(mod_all, e, 0, keepdims=False).reshape(3, 3, D)
    zeros3 = jnp.zeros((3, D), F32)
    vecs = [jnp.concatenate([npre[k:k + 1], npost[k:k + 1], mod[k], zeros3], axis=0) for k in range(3)]

    f1_s, f1_r, f1_fly, tok0 = ag_start(placed[:2], geoms[:2], [g2], "ag_start_ffn1")
    placed += [ag_local(W[n][0], kind, R, C, p_arr, "ag_local_" + n, after=[tok0]) for (n, kind, R, C) in BIG[2:]]
    f1_gu, f1_dn = arrived(f1_fly, 0, 2, f1_s, f1_r, placed[7], "ffn1")
    mix_s, mix_r, mix_fly, tok1 = ag_start(placed[2:6], geoms[2:6], [f1_gu, f1_dn], "ag_start_mixer")
    ffn_s, ffn_r, ffn_fly, tok2 = ag_start(placed[6:], geoms[6:], [tok1], "ag_start_ffn2")
    wa_bd = _block_diag4(lru_wa[0]).astype(BF16)
    wx_bd = _block_diag4(lru_wx[0]).astype(BF16)
    pvec = jnp.concatenate([conv_full, conv_b, lru_ba, lru_bx, lru_lambda], axis=0)
    bias = _bias_window(rel_bias[0]).reshape(4, 256, WIN)

    f1_u = f1_gu[:, FF:]
    x1, h1, g1_, u1, a1, f1 = ffn_fwd(xs, vecs[0] + tok2[0:1, 0:1], f1_gu, f1_u, f1_dn, 0.5, "ffn1_fwd")
    win, wao, wro, wout = arrived(mix_fly, 2, 6, mix_s, mix_r, x1, "mixer")
    h2, qkv, rest = proj_fwd(x1, vecs[1], win, "proj_fwd")
    ao = attn_fwd(qkv, bias, "attn_fwd")
    hl, hg = lru_fwd(rest, pvec, wa_bd, wx_bd, "lru_fwd")
    x2, att, rec, mg, f2 = mix_out_fwd(x1, ao, hg, rest, vecs[1], wao, wro, wout, "mix_out_fwd")
    f2_gu, f2_dn = arrived(ffn_fly, 6, 8, ffn_s, ffn_r, x2, "ffn2")
    f2_u = f2_gu[:, FF:]
    dy, h3, g3_, u3, a3, f3, lvec = ffn_fwd(x2, vecs[2], f2_gu, f2_u, f2_dn, 0.5, "ffn2_fwd", tgt=loss_target[0])
    loss = lax.psum(lvec[0, 0], ("x", "y", "c"))

    G, grads = {}, {}
    geo = {n: (kind, R, C) for (n, kind, R, C) in BIG}

    def reduce_begin(names, tag):
        ps, rb = [], []
        for n in names:
            got = pair_push(G[n], geo[n][0], c_arr, "rs_push_" + n)
            a, b = pair_add(G[n], got, geo[n][0], cp_arr, "rs_pair_sum_" + n)
            ps.append(a)
            rb.append(b)
        return rs_start(ps, rb, [], "rs_start_" + tag)

    def reduce_end(names, flight, after, tag):
        ssem, rsem, ps, rb, _ = flight
        for a, n in zip(rs_wait(ps, rb, ssem, rsem, after, "rs_wait_" + tag), names):
            grads[n] = sum_share(a, *geo[n], "rs_sum_share_" + n)[None]

    dx2, df3, dgu3, va2 = ffn_bwd(dy, x2, f3, g3_, u3, vecs[2], f2_gu, f2_u, f2_dn, 0.5, "ffn2_bwd")
    G["ffn2_w_gu"] = mm_tn(h3, dgu3, "dw_ffn2_gu", D, 1408, 1024)
    G["ffn2_w_down"] = mm_tn(a3, df3, "dw_ffn2_down", 1408, D, 1024)
    fly_ffn2 = reduce_begin(("ffn2_w_gu", "ffn2_w_down"), "ffn2")
    vec1 = vecs[1] + fly_ffn2[4][0:1, 0:1]
    df2, d_att, d_rec, dao, dhl, d3, va_out = mix_out_bwd(dx2, f2, att, rec, rest, hl, vec1, wao, wro, wout,
                                                          "mix_out_bwd")
    G["w_out"] = mm_tn(mg, df2, "dw_out", D, D, 1024)
    G["w_att_o"] = mm_tn(ao, d_att, "dw_att_o", 512, D, 1024)
    G["w_rec_o"] = mm_tn(hg, d_rec, "dw_rec_o", D, D, 1024)
    dq, db, dkv = attn_bwd(qkv, dao, bias, "attn_bwd")
    dxr, v_lru, dwa_bd, dwx_bd = lru_bwd(dhl, hl, rest, pvec, wa_bd, wx_bd, "lru_bwd")
    dx1, va_in = proj_bwd(dq, dkv, dxr, d3, win, x1, dx2, vecs[1], "proj_bwd")
    G["w_in"] = dw_in(h2, dq, dkv, dxr, d3, "dw_in")
    fly_mix = reduce_begin(("w_in", "w_att_o", "w_rec_o", "w_out"), "mixer")
    vec0 = vecs[0] + fly_mix[4][0:1, 0:1]
    dx0, df1, dgu1, va0 = ffn_bwd(dx1, xs, f1, g1_, u1, vec0, f1_gu, f1_u, f1_dn, 0.5, "ffn1_bwd")
    G["ffn1_w_gu"] = mm_tn(h1, dgu1, "dw_ffn1_gu", D, 1408, 1024)
    G["ffn1_w_down"] = mm_tn(a1, df1, "dw_ffn1_down", 1408, D, 1024)
    fly_ffn1 = reduce_begin(("ffn1_w_gu", "ffn1_w_down"), "ffn1")
    reduce_end(("ffn2_w_gu", "ffn2_w_down"), fly_ffn2, [fly_ffn1[4]], "ffn2")
    reduce_end(("w_in", "w_att_o", "w_rec_o", "w_out"), fly_mix, [fly_ffn1[4], grads["ffn2_w_down"]], "mixer")

    va1 = va_out + va_in
    vas = (va0, va1, va2)
    dmod = jnp.stack([v[2:5] for v in vas])
    part = {"b_ada": dmod, "norm_pre": jnp.stack([v[0] for v in vas]), "norm_post": jnp.stack([v[1] for v in vas]),
            "rel_bias": bias_grad(db.reshape(8, 128, WIN), "bias_grad")[:, :257], "conv_w": v_lru[0:4], "conv_b": v_lru[4],
            "lru_wa": _diag_blocks(dwa_bd), "lru_ba": v_lru[5], "lru_wx": _diag_blocks(dwx_bd), "lru_bx": v_lru[6],
            "lru_lambda": v_lru[7]}
    full_shapes = {"b_ada": (9 * D,), "norm_pre": (3, D), "norm_post": (3, D), "rel_bias": (8, 257),
                   "conv_w": (4, D), "conv_b": (D,), "lru_wa": (16, 64, 64), "lru_ba": (D,),
                   "lru_wx": (16, 64, 64), "lru_bx": (D,), "lru_lambda": (D,)}
    g3 = ag_small(_pack([part[n] for n in SMALL], 1232), "ag_small_grads")
    red = dict(zip(SMALL, _unpack(sum_lead(g3, "sum_small_grads"), [full_shapes[n] for n in SMALL])))
    cols = lambda a: lax.dynamic_slice(a, (0, p * 256), (a.shape[0], 256))
    grads.update({"b_ada": red["b_ada"][None], "norm_pre": cols(red["norm_pre"])[None],
                  "norm_post": cols(red["norm_post"])[None], "rel_bias": red["rel_bias"][None],
                  "conv_w": cols(red["conv_w"])[None], "conv_b": red["conv_b"][None], "lru_wa": red["lru_wa"][None],
                  "lru_ba": red["lru_ba"][None], "lru_wx": red["lru_wx"][None], "lru_bx": red["lru_bx"][None],
                  "lru_lambda": red["lru_lambda"][None]})

    dmod_all = g3[:, :72].reshape(8, 9 * D)
    dmod_cols = jnp.pad(lax.dynamic_slice(dmod_all, (0, p * 2304), (8, 2304)), ((0, 120), (0, 0)))
    c_all_t = jnp.pad(c_all.T, ((0, 0), (0, 120)))
    grads["w_ada"] = ada_bwd(c_all_t, dmod_cols, "ada_bwd")[None]

    delta, new_m, new_v = {}, {}, {}

    def update(n):
        shp = W[n].shape
        d_, m_, v_ = adamw(W[n][0], grads[n][0], M[n][0], V[n][0], "adamw_" + n)
        delta[n], new_m[n], new_v[n] = d_.reshape(shp), m_.reshape(shp), v_.reshape(shp)

    for n in ("w_ada", "ffn2_w_gu", "ffn2_w_down", "w_in", "w_att_o", "w_rec_o", "w_out"):
        update(n)
    packed = [_pack([src[n] for n in SMALL], 1168) for src in (W, grads, M, V)]
    outs = adamw(*packed, "adamw_small")
    for dst, blk in zip((delta, new_m, new_v), outs):
        for n, a in zip(SMALL, _unpack(blk, [W[n].shape for n in SMALL])):
            dst[n] = a
    reduce_end(("ffn1_w_gu", "ffn1_w_down"), fly_ffn1,
               [outs[0], delta["w_ada"], delta["ffn2_w_gu"], delta["ffn2_w_down"], delta["w_in"], delta["w_out"]], "ffn1")
    for n in ("ffn1_w_gu", "ffn1_w_down"):
        update(n)

    return (loss, dx0[None], *[grads[n] for n in WEIGHTS], *[delta[n] for n in WEIGHTS],
            *[new_m[n] for n in WEIGHTS], *[new_v[n] for n in WEIGHTS])
```

```python
import functools

import numpy as np
import jax
import jax.numpy as jnp
from jax import lax
from jax.experimental import pallas as pl
from jax.experimental.pallas import tpu as pltpu

F32 = jnp.float32
BF16 = jnp.bfloat16

D = 1024
FF = 2816
PW = 5632
HP = 128
CHUNK = 64
WIN = 640
TQ = 512
EPS = 1e-6
NEG = -1e30
LRU_C = 8.0
N_DEV = 8
VMEM_LIMIT = 56 * 1024 * 1024

ADAM_LR, ADAM_B1, ADAM_B2, ADAM_EPS, ADAM_WD, ADAM_STEP = 0.001, 0.9, 0.999, 1e-08, 0.01, 10

MESH = pl.DeviceIdType.MESH
ANY = pl.BlockSpec(memory_space=pl.ANY)


def _cp(*sem):
    return pltpu.CompilerParams(dimension_semantics=tuple(sem), vmem_limit_bytes=VMEM_LIMIT)


def _dot(a, b):
    return jnp.dot(a, b, preferred_element_type=F32)


def _dot_nt(a, b):
    return lax.dot_general(a, b, (((1,), (1,)), ((), ())), preferred_element_type=F32)


def _dot_tn(a, b):
    return lax.dot_general(a, b, (((0,), (0,)), ((), ())), preferred_element_type=F32)


def _mean(v):
    return jnp.mean(v, axis=-1, keepdims=True)


def _colsum(v):
    return jnp.sum(v, axis=0, keepdims=True)


def _sigmoid(v):
    return 0.5 * jnp.tanh(0.5 * v) + 0.5


_GK = 0.7978845608028654


def _gelu(v):
    t = jnp.tanh(_GK * (v + 0.044715 * v * v * v))
    return 0.5 * v * (1.0 + t)


def _pre_norm(xv, vec_ref):
    r = lax.rsqrt(_mean(xv * xv) + EPS)
    n = xv * r * vec_ref[0:1, :]
    return n * (1.0 + vec_ref[3:4, :]) + vec_ref[2:3, :]


def _pre_norm_bwd(dh, xv, dres, vec_ref, vacc_ref):
    r = lax.rsqrt(_mean(xv * xv) + EPS)
    xh = xv * r
    n = xh * vec_ref[0:1, :]
    vacc_ref[2:3, :] += _colsum(dh)
    vacc_ref[3:4, :] += _colsum(dh * n)
    dn = dh * (1.0 + vec_ref[3:4, :])
    vacc_ref[0:1, :] += _colsum(dn * xh)
    dxh = dn * vec_ref[0:1, :]
    return r * (dxh - xh * _mean(dxh * xh)) + dres


def _post_norm_bwd(dxo, fv, res, vec_ref, vacc_ref):
    rf = lax.rsqrt(_mean(fv * fv) + EPS)
    fh = fv * rf
    gp = vec_ref[1:2, :]
    vacc_ref[4:5, :] += _colsum(res * dxo * (fh * gp))
    dy = (res * vec_ref[4:5, :]) * dxo
    vacc_ref[1:2, :] += _colsum(dy * fh)
    dfn = dy * gp
    return rf * (dfn - fh * _mean(dfn * fh))


def ffn_fwd(x, vec, w_gu, w_u, w_dn, res, name, tgt=None, tm=1024, tf=512):
    S = x.shape[0]
    tm = min(tm, S)
    nt = S // tm
    nf = -(-FF // tf)
    tail = FF - tf * (nf - 1)
    head = tgt is not None

    def body(*refs):
        x_ref, vec_ref, wg_ref, wu_ref, wd_ref = refs[:5]
        if head:
            t_ref, xo_ref, h_ref, g_ref, u_ref, a_ref, f_ref, l_ref, hs, acc, lacc = refs[5:]
        else:
            xo_ref, h_ref, g_ref, u_ref, a_ref, f_ref, hs, acc = refs[5:]
        i, j = pl.program_id(0), pl.program_id(1)

        @pl.when(j == 0)
        def _():
            h = _pre_norm(x_ref[...], vec_ref).astype(BF16)
            hs[...] = h
            h_ref[...] = h
            acc[...] = jnp.zeros_like(acc)

        def chunk(w):
            h = hs[...]
            g = _dot(h, wg_ref[:, 0:w])
            u = _dot(h, wu_ref[:, 0:w])
            g_ref[:, 0:w] = g.astype(BF16)
            u_ref[:, 0:w] = u.astype(BF16)
            a = (g * _sigmoid(g) * u).astype(BF16)
            a_ref[:, 0:w] = a
            acc[...] += _dot(a, wd_ref[0:w, :])

        @pl.when(j < nf - 1)
        def _():
            chunk(tf)

        @pl.when(j == nf - 1)
        def _():
            chunk(tail)
            f = acc[...]
            f_ref[...] = f.astype(BF16)
            y = f * lax.rsqrt(_mean(f * f) + EPS) * vec_ref[1:2, :]
            xo = x_ref[...] + (res * vec_ref[4:5, :]) * y
            if head:
                @pl.when(i == 0)
                def _():
                    lacc[...] = jnp.zeros_like(lacc)

                d = xo - t_ref[...]
                xo_ref[...] = d * (1.0 / D)
                lacc[...] += _colsum(d * d)

                @pl.when(i == nt - 1)
                def _():
                    l_ref[...] = jnp.broadcast_to(0.5 * jnp.sum(lacc[...]) * (1.0 / D), (8, 128))
            else:
                xo_ref[...] = xo

    row = lambda i, j: (i, 0)
    col = lambda i, j: (i, j)
    once = dict(pipeline_mode=pl.Buffered(1)) if head else {}
    in_specs = [pl.BlockSpec((tm, D), row, **once), pl.BlockSpec((8, D), lambda i, j: (0, 0)),
                pl.BlockSpec((D, tf), lambda i, j: (0, j)), pl.BlockSpec((D, tf), lambda i, j: (0, j)),
                pl.BlockSpec((tf, D), lambda i, j: (j, 0))]
    out_specs = [pl.BlockSpec((tm, D), row), pl.BlockSpec((tm, D), row), pl.BlockSpec((tm, tf), col),
                 pl.BlockSpec((tm, tf), col), pl.BlockSpec((tm, tf), col), pl.BlockSpec((tm, D), row)]
    out_shape = [jax.ShapeDtypeStruct((S, D), F32), jax.ShapeDtypeStruct((S, D), BF16),
                 jax.ShapeDtypeStruct((S, FF), BF16), jax.ShapeDtypeStruct((S, FF), BF16),
                 jax.ShapeDtypeStruct((S, FF), BF16), jax.ShapeDtypeStruct((S, D), BF16)]
    scratch = [pltpu.VMEM((tm, D), BF16), pltpu.VMEM((tm, D), F32)]
    args = [x, vec, w_gu, w_u, w_dn]
    if head:
        in_specs.append(pl.BlockSpec((tm, D), row, **once))
        out_specs.append(pl.BlockSpec((8, 128), lambda i, j: (0, 0)))
        out_shape.append(jax.ShapeDtypeStruct((8, 128), F32))
        scratch.append(pltpu.VMEM((1, D), F32))
        args.append(tgt)
    return pl.pallas_call(
        body, name=name, grid=(nt, nf), in_specs=in_specs, out_specs=out_specs, out_shape=out_shape,
        scratch_shapes=scratch,
        compiler_params=_cp("arbitrary" if head else "parallel", "arbitrary"),
    )(*args)


def ffn_bwd(dxo, x, f, g, u, vec, w_gu, w_u, w_dn, res, name, tm=1024, tf=512):
    S = x.shape[0]
    tm = min(tm, S)
    nf = -(-FF // tf)
    tail = FF - tf * (nf - 1)

    def body(dxo_ref, x_ref, f_ref, g_ref, u_ref, vec_ref, wg_ref, wu_ref, wd_ref,
             dx_ref, df_ref, dgu_ref, vacc_ref, dfs, acc):
        i, j = pl.program_id(0), pl.program_id(1)

        @pl.when((i == 0) & (j == 0))
        def _():
            vacc_ref[...] = jnp.zeros_like(vacc_ref)

        @pl.when(j == 0)
        def _():
            df = _post_norm_bwd(dxo_ref[...], f_ref[...].astype(F32), res, vec_ref, vacc_ref).astype(BF16)
            dfs[...] = df
            df_ref[...] = df
            acc[...] = jnp.zeros_like(acc)

        def chunk(w):
            da = _dot_nt(dfs[...], wd_ref[0:w, :])
            gv, uv = g_ref[:, 0:w].astype(F32), u_ref[:, 0:w].astype(F32)
            sg = _sigmoid(gv)
            dg = (da * uv * (sg * (1.0 + gv * (1.0 - sg)))).astype(BF16)
            du = (da * (gv * sg)).astype(BF16)
            dgu_ref[0, :, 0:w] = dg
            dgu_ref[1, :, 0:w] = du
            acc[...] += _dot_nt(dg, wg_ref[:, 0:w]) + _dot_nt(du, wu_ref[:, 0:w])

        @pl.when(j < nf - 1)
        def _():
            chunk(tf)

        @pl.when(j == nf - 1)
        def _():
            chunk(tail)
            dx_ref[...] = _pre_norm_bwd(acc[...], x_ref[...], dxo_ref[...], vec_ref, vacc_ref)

    row = lambda i, j: (i, 0)
    col = lambda i, j: (i, j)
    return pl.pallas_call(
        body, name=name, grid=(S // tm, nf),
        in_specs=[pl.BlockSpec((tm, D), row, pipeline_mode=pl.Buffered(1)) for _ in range(3)]
        + [pl.BlockSpec((tm, tf), col), pl.BlockSpec((tm, tf), col),
                  pl.BlockSpec((8, D), lambda i, j: (0, 0)),
                  pl.BlockSpec((D, tf), lambda i, j: (0, j)), pl.BlockSpec((D, tf), lambda i, j: (0, j)),
                  pl.BlockSpec((tf, D), lambda i, j: (j, 0))],
        out_specs=[pl.BlockSpec((tm, D), row), pl.BlockSpec((tm, D), row),
                   pl.BlockSpec((2, tm, tf), lambda i, j: (0, i, j)),
                   pl.BlockSpec((8, D), lambda i, j: (0, 0))],
        out_shape=[jax.ShapeDtypeStruct((S, D), F32), jax.ShapeDtypeStruct((S, D), BF16),
                   jax.ShapeDtypeStruct((2, S, FF), BF16), jax.ShapeDtypeStruct((8, D), F32)],
        scratch_shapes=[pltpu.VMEM((tm, D), BF16), pltpu.VMEM((tm, D), F32)],
        compiler_params=_cp("arbitrary", "arbitrary"),
    )(dxo, x, f, g, u, vec, w_gu, w_u, w_dn)


def mm_tn(a, b, name, tm, tn, tk, out_dtype=BF16):
    S, M = a.shape
    if b.ndim == 3:
        G, _, Nf = b.shape
    else:
        G, Nf = 1, b.shape[1]
    N = G * Nf
    tk = min(tk, S)
    nbf = Nf // tn
    nk = S // tk

    def body(a_ref, b_ref, o_ref, acc):
        k, j = pl.program_id(1), pl.program_id(2)
        part = _dot_tn(a_ref[...], b_ref[...])

        @pl.when(k == 0)
        def _():
            acc[j] = part

        @pl.when(k > 0)
        def _():
            acc[j] += part

        @pl.when(k == nk - 1)
        def _():
            o_ref[...] = acc[j].astype(out_dtype)

    if b.ndim == 3:
        b_spec = pl.BlockSpec((None, tk, tn), lambda i, k, j: (j // nbf, k, j % nbf))
    else:
        b_spec = pl.BlockSpec((tk, tn), lambda i, k, j: (k, j))
    return pl.pallas_call(
        body, name=name, grid=(M // tm, nk, N // tn),
        in_specs=[pl.BlockSpec((tk, tm), lambda i, k, j: (k, i)), b_spec],
        out_specs=pl.BlockSpec((tm, tn), lambda i, k, j: (i, jnp.where(k == nk - 1, j, 0))),
        out_shape=jax.ShapeDtypeStruct((M, N), out_dtype),
        scratch_shapes=[pltpu.VMEM((N // tn, tm, tn), F32)],
        compiler_params=_cp("parallel", "arbitrary", "arbitrary"),
    )(a, b)


def proj_fwd(x, vec, w_in, name, tm=1024, tn=512):
    S = x.shape[0]
    tm = min(tm, S)
    nq = 1536 // tn

    def body(x_ref, vec_ref, w_ref, h_ref, qkv_ref, rest_ref, hs):
        j = pl.program_id(1)

        @pl.when(j == 0)
        def _():
            h = _pre_norm(x_ref[...], vec_ref).astype(BF16)
            hs[...] = h
            h_ref[...] = h

        r = _dot(hs[...], w_ref[...])

        @pl.when(j < nq)
        def _():
            qkv_ref[...] = r.astype(BF16)

        @pl.when(j >= nq)
        def _():
            rest_ref[...] = r.astype(BF16)

    row = lambda i, j: (i, 0)
    return pl.pallas_call(
        body, name=name, grid=(S // tm, PW // tn),
        in_specs=[pl.BlockSpec((tm, D), row), pl.BlockSpec((8, D), lambda i, j: (0, 0)),
                  pl.BlockSpec((D, tn), lambda i, j: (0, j))],
        out_specs=[pl.BlockSpec((tm, D), row),
                   pl.BlockSpec((tm, tn), lambda i, j: (i, jnp.minimum(j, nq - 1))),
                   pl.BlockSpec((tm, tn), lambda i, j: (i, jnp.maximum(j - nq, 0)))],
        out_shape=[jax.ShapeDtypeStruct((S, D), BF16), jax.ShapeDtypeStruct((S, 1536), BF16),
                   jax.ShapeDtypeStruct((S, 4096), BF16)],
        scratch_shapes=[pltpu.VMEM((tm, D), BF16)],
        compiler_params=_cp("parallel", "arbitrary"),
    )(x, vec, w_in)


def proj_bwd(dq, dkv, dxr, d3, w_in, x, dxo, vec, name, tm=1024, tk=512):
    S = x.shape[0]
    tm = min(tm, S)
    nk = PW // tk

    def body(dq_ref, dkv_ref, dxr_ref, d3_ref, w_ref, x_ref, dxo_ref, vec_ref, dx_ref, vacc_ref, acc):
        i, j = pl.program_id(0), pl.program_id(1)

        @pl.when((i == 0) & (j == 0))
        def _():
            vacc_ref[...] = jnp.zeros_like(vacc_ref)

        @pl.when(j == 0)
        def _():
            acc[...] = _dot_nt(dq_ref[...], w_ref[...])

        @pl.when((j >= 1) & (j < 3))
        def _():
            acc[...] += _dot_nt(dkv_ref[...], w_ref[...])

        @pl.when((j >= 3) & (j < 5))
        def _():
            acc[...] += _dot_nt(dxr_ref[...], w_ref[...])

        @pl.when(j >= 5)
        def _():
            acc[...] += _dot_nt(d3_ref[...], w_ref[...])

        @pl.when(j == nk - 1)
        def _():
            dx_ref[...] = _pre_norm_bwd(acc[...], x_ref[...], dxo_ref[...], vec_ref, vacc_ref)

    row = lambda i, j: (i, 0)
    return pl.pallas_call(
        body, name=name, grid=(S // tm, nk),
        in_specs=[pl.BlockSpec((None, tm, tk), lambda i, j: (0, i, 0)),
                  pl.BlockSpec((None, tm, tk), lambda i, j: (jnp.clip(j - 1, 0, 1), i, 0)),
                  pl.BlockSpec((tm, tk), lambda i, j: (i, jnp.clip(j - 3, 0, 1))),
                  pl.BlockSpec((None, tm, tk), lambda i, j: (jnp.clip(j - 5, 0, 5) // 2, i, jnp.clip(j - 5, 0, 5) % 2)),
                  pl.BlockSpec((D, tk), lambda i, j: (0, j)),
                  pl.BlockSpec((tm, D), row), pl.BlockSpec((tm, D), row),
                  pl.BlockSpec((8, D), lambda i, j: (0, 0))],
        out_specs=[pl.BlockSpec((tm, D), row), pl.BlockSpec((8, D), lambda i, j: (0, 0))],
        out_shape=[jax.ShapeDtypeStruct((S, D), F32), jax.ShapeDtypeStruct((8, D), F32)],
        scratch_shapes=[pltpu.VMEM((tm, D), F32)],
        compiler_params=_cp("arbitrary", "arbitrary"),
    )(dq, dkv, dxr, d3, w_in, x, dxo, vec)


def _two_heads(v, lane):
    zero = jnp.zeros((), v.dtype)
    return jnp.concatenate([jnp.where(lane < 64, v, zero), jnp.where(lane >= 64, v, zero)], axis=0)


def _attn_probs(qm, ka, bias_h, i, grp):
    s = _dot_nt(qm, ka) + bias_h
    col = lax.broadcasted_iota(jnp.int32, s.shape, 1)
    first_key = jnp.where(i == 0, 512 - 128 * grp, 0)
    s = jnp.where(col >= first_key, s, NEG)
    e = jnp.exp(s - jnp.max(s, axis=-1, keepdims=True))
    return e * (1.0 / jnp.sum(e, axis=-1, keepdims=True))


def attn_fwd(qkv, bias, name):
    S = qkv.shape[0]
    nb = S // TQ

    def body(q_ref, kp_ref, kc_ref, vp_ref, vc_ref, b_ref, o_ref, kw, vw):
        i = pl.program_id(1)
        kw[0:TQ, :] = kp_ref[...]
        kw[TQ:2 * TQ, :] = kc_ref[...]
        vw[0:TQ, :] = vp_ref[...]
        vw[TQ:2 * TQ, :] = vc_ref[...]
        lane = lax.broadcasted_iota(jnp.int32, (1, HP), 1)

        def group(a, carry):
            r0 = pl.multiple_of(a * 128, 128)
            qa = q_ref[pl.ds(r0, 128), :] * jnp.asarray(0.125, BF16)
            ka = kw[pl.ds(r0, WIN), :]
            va = vw[pl.ds(r0, WIN), :]
            p = _attn_probs(_two_heads(qa, lane), ka, b_ref[...], i, a)
            o2 = _dot(p.astype(BF16), va)
            o_ref[pl.ds(r0, 128), :] = jnp.where(lane < 64, o2[0:128], o2[128:256]).astype(BF16)
            return carry

        lax.fori_loop(0, TQ // 128, group, 0, unroll=True)

    prev = lambda h, i: (jnp.maximum(i - 1, 0), 0)
    return pl.pallas_call(
        body, name=name, grid=(4, nb),
        in_specs=[pl.BlockSpec((TQ, HP), lambda h, i: (i, h)),
                  pl.BlockSpec((TQ, HP), lambda h, i: (jnp.maximum(i - 1, 0), 4 + h)),
                  pl.BlockSpec((TQ, HP), lambda h, i: (i, 4 + h)),
                  pl.BlockSpec((TQ, HP), lambda h, i: (jnp.maximum(i - 1, 0), 8 + h)),
                  pl.BlockSpec((TQ, HP), lambda h, i: (i, 8 + h)),
                  pl.BlockSpec((None, 256, WIN), lambda h, i: (h, 0, 0))],
        out_specs=pl.BlockSpec((TQ, HP), lambda h, i: (i, h)),
        out_shape=jax.ShapeDtypeStruct((S, 512), BF16),
        scratch_shapes=[pltpu.VMEM((2 * TQ, HP), BF16), pltpu.VMEM((2 * TQ, HP), BF16)],
        compiler_params=_cp("parallel", "arbitrary"),
    )(qkv, qkv, qkv, qkv, qkv, bias)


def attn_bwd(qkv, do, bias, name):
    S = qkv.shape[0]
    nb = S // TQ

    def body(q_ref, kp_ref, kc_ref, vp_ref, vc_ref, do_ref, b_ref, dqkv_ref, db_ref, dkv_ref, kw, vw, ak, av):
        i = pl.program_id(1)

        @pl.when(i == 0)
        def _():
            db_ref[...] = jnp.zeros_like(db_ref)
            ak[...] = jnp.zeros_like(ak)
            av[...] = jnp.zeros_like(av)

        @pl.when(i > 0)
        def _():
            ak[0:TQ, :] = ak[TQ:2 * TQ, :]
            av[0:TQ, :] = av[TQ:2 * TQ, :]
            ak[TQ:2 * TQ, :] = jnp.zeros((TQ, HP), F32)
            av[TQ:2 * TQ, :] = jnp.zeros((TQ, HP), F32)

        @pl.when(i < nb)
        def _():
            kw[0:TQ, :] = kp_ref[...]
            kw[TQ:2 * TQ, :] = kc_ref[...]
            vw[0:TQ, :] = vp_ref[...]
            vw[TQ:2 * TQ, :] = vc_ref[...]
            lane = lax.broadcasted_iota(jnp.int32, (1, HP), 1)

            def group(a, carry):
                r0 = pl.multiple_of(a * 128, 128)
                q2 = _two_heads(q_ref[pl.ds(r0, 128), :] * jnp.asarray(0.125, BF16), lane)
                do2 = _two_heads(do_ref[pl.ds(r0, 128), :], lane)
                ka = kw[pl.ds(r0, WIN), :]
                va = vw[pl.ds(r0, WIN), :]
                p = _attn_probs(q2, ka, b_ref[...], i, a)
                dp = _dot_nt(do2, va)
                ds = p * (dp - jnp.sum(p * dp, axis=-1, keepdims=True))
                db_ref[...] += ds
                dsb = ds.astype(BF16)
                dq2 = _dot(dsb, ka)
                ak[pl.ds(r0, WIN), :] += _dot_tn(dsb, q2)
                av[pl.ds(r0, WIN), :] += _dot_tn(p.astype(BF16), do2)
                dq = jnp.where(lane < 64, dq2[0:128], dq2[128:256])
                dqkv_ref[0, pl.ds(r0, 128), :] = (dq * 0.125).astype(BF16)
                return carry

            lax.fori_loop(0, TQ // 128, group, 0, unroll=True)

        @pl.when(i > 0)
        def _():
            dkv_ref[0] = ak[0:TQ, :].astype(BF16)
            dkv_ref[1] = av[0:TQ, :].astype(BF16)

    cur = lambda i: jnp.minimum(i, nb - 1)
    prv = lambda i: jnp.clip(i - 1, 0, nb - 1)
    dq, db, dkv = pl.pallas_call(
        body, name=name, grid=(4, nb + 1),
        in_specs=[pl.BlockSpec((TQ, HP), lambda h, i: (cur(i), h)),
                  pl.BlockSpec((TQ, HP), lambda h, i: (prv(i), 4 + h)),
                  pl.BlockSpec((TQ, HP), lambda h, i: (cur(i), 4 + h)),
                  pl.BlockSpec((TQ, HP), lambda h, i: (prv(i), 8 + h)),
                  pl.BlockSpec((TQ, HP), lambda h, i: (cur(i), 8 + h)),
                  pl.BlockSpec((TQ, HP), lambda h, i: (cur(i), h)),
                  pl.BlockSpec((None, 256, WIN), lambda h, i: (h, 0, 0))],
        out_specs=[pl.BlockSpec((1, TQ, HP), lambda h, i: (0, cur(i), h)),
                   pl.BlockSpec((None, 256, WIN), lambda h, i: (h, 0, 0)),
                   pl.BlockSpec((2, TQ, HP), lambda h, i: (0, prv(i), h))],
        out_shape=[jax.ShapeDtypeStruct((1, S, 512), BF16), jax.ShapeDtypeStruct((4, 256, WIN), F32),
                   jax.ShapeDtypeStruct((2, S, 512), BF16)],
        scratch_shapes=[pltpu.VMEM((2 * TQ, HP), BF16), pltpu.VMEM((2 * TQ, HP), BF16),
                        pltpu.VMEM((2 * TQ, HP), F32), pltpu.VMEM((2 * TQ, HP), F32)],
        compiler_params=_cp("parallel", "arbitrary"),
    )(qkv, qkv, qkv, qkv, qkv, do, bias)
    return dq, db, dkv


def bias_grad(db, name):
    def body(db_ref, o_ref):
        r = lax.broadcasted_iota(jnp.int32, (128, 128), 0)
        c = lax.broadcasted_iota(jnp.int32, (128, 128), 1)
        flip = (r + c == 127).astype(BF16)
        lane = lax.broadcasted_iota(jnp.int32, (16, 384), 1)
        src = lax.broadcasted_iota(jnp.int32, (128, 384), 0)
        dst = lax.broadcasted_iota(jnp.int32, (128, 384), 1)

        def split_dot(v, m):
            hi = v.astype(BF16)
            r1 = v - hi.astype(F32)
            mid = r1.astype(BF16)
            lo = (r1 - mid.astype(F32)).astype(BF16)
            return _dot(hi, m) + _dot(mid, m) + _dot(lo, m)

        def diag_sums(w):
            y = pltpu.roll(split_dot(w, flip), 0, 1, stride=1, stride_axis=0)
            return jnp.broadcast_to(_colsum(y), (16, 128))

        w4 = db_ref[0, :, 512:640]
        w3 = db_ref[0, :, 384:512]
        far = jnp.sum(db_ref[0, :, 0:384]) + jnp.sum(jnp.where(r >= c, w3, 0.0))
        lo4 = diag_sums(jnp.where(r >= c, w4, 0.0))
        up4 = diag_sums(jnp.where(r < c, w4, 0.0))
        up3 = diag_sums(jnp.where(r < c, w3, 0.0))
        p_lo4 = (dst == 128 + (src + 1) % 128).astype(BF16)
        p_up4 = ((dst == src + 1) & (src < 127)).astype(BF16)
        p_up3 = ((dst == src + 129) & (src < 127)).astype(BF16)
        out = split_dot(lo4, p_lo4) + split_dot(up4, p_up4) + split_dot(up3, p_up3)
        o_ref[0] = out + jnp.where(lane == 256, far, 0.0)

    return pl.pallas_call(
        body, name=name, grid=(8,),
        in_specs=[pl.BlockSpec((1, 128, WIN), lambda h: (h, 0, 0))],
        out_specs=pl.BlockSpec((1, 16, 384), lambda h: (h, 0, 0)),
        out_shape=jax.ShapeDtypeStruct((8, 16, 384), F32),
        compiler_params=_cp("parallel"),
    )(db)[:, 0, :]


LT = 256
LC = 512


def _lru_gates(xs, pv_ref, wa_ref, wx_ref, tl):
    xc = (pv_ref[4:5, :] + pv_ref[3:4, :] * xs[pl.ds(8, tl), :] + pv_ref[2:3, :] * xs[pl.ds(7, tl), :]
          + pv_ref[1:2, :] * xs[pl.ds(6, tl), :] + pv_ref[0:1, :] * xs[pl.ds(5, tl), :])
    xcb = xc.astype(BF16)
    pa = jnp.concatenate([_dot(xcb[:, 0:256], wa_ref[0]), _dot(xcb[:, 256:512], wa_ref[1])], axis=1)
    px = jnp.concatenate([_dot(xcb[:, 0:256], wx_ref[0]), _dot(xcb[:, 256:512], wx_ref[1])], axis=1)
    r = _sigmoid(pa + pv_ref[5:6, :])
    ig = _sigmoid(px + pv_ref[6:7, :])
    z = -pv_ref[7:8, :]
    sp = jnp.maximum(z, 0.0) + jnp.log1p(jnp.exp(-jnp.abs(z)))
    log_a = (-LRU_C * r) * sp
    a = jnp.exp(log_a)
    s = jnp.tanh(-log_a) * (1.0 + a * a)
    inv_mult = lax.rsqrt(s)
    mult = jnp.where(s > 0.0, s * inv_mult, 0.0)
    return xc, xcb, r, ig, sp, a, mult, inv_mult


def lru_fwd(rest, pvec, wa, wx, name):
    S = rest.shape[0]
    tl = min(LT, S)
    nt = S // tl

    def body(xr_ref, halo_ref, yr_ref, pv_ref, wa_ref, wx_ref, h_ref, hg_ref, xs, a_s, u_s, h_s, carry):
        ti = pl.program_id(1)

        @pl.when(ti == 0)
        def _():
            carry[...] = jnp.zeros_like(carry)

        xs[0:8, :] = jnp.where(ti > 0, halo_ref[8:16, :].astype(F32), 0.0)
        xs[pl.ds(8, tl), :] = xr_ref[...].astype(F32)
        xc, _, _, ig, _, a, mult, _ = _lru_gates(xs, pv_ref, wa_ref, wx_ref, tl)
        a_s[...] = a
        u_s[...] = mult * (ig * xc)
        row = lax.broadcasted_iota(jnp.int32, (8, LC), 0)

        def blk(bi, c):
            o = pl.multiple_of(bi * 8, 8)
            av = a_s[pl.ds(o, 8), :]
            bv = u_s[pl.ds(o, 8), :]
            for d in (1, 2, 4):
                a_sh = pltpu.roll(av, d, 0)
                b_sh = pltpu.roll(bv, d, 0)
                m = row >= d
                bv = jnp.where(m, av * b_sh + bv, bv)
                av = jnp.where(m, av * a_sh, av)
            hv = bv + av * c
            h_s[pl.ds(o, 8), :] = hv
            return hv[7:8, :]

        carry[...] = lax.fori_loop(0, tl // 8, blk, carry[...])
        h = h_s[...]
        h_ref[...] = h
        hg_ref[...] = (h * _gelu(yr_ref[...].astype(F32))).astype(BF16)

    hb = tl // 16
    return pl.pallas_call(
        body, name=name, grid=(2, nt),
        in_specs=[pl.BlockSpec((tl, LC), lambda c, t: (t, c)),
                  pl.BlockSpec((16, LC), lambda c, t: (jnp.maximum(t * hb - 1, 0), c)),
                  pl.BlockSpec((tl, LC), lambda c, t: (t, 2 + c)),
                  pl.BlockSpec((8, LC), lambda c, t: (0, c)),
                  pl.BlockSpec((2, 256, 256), lambda c, t: (c, 0, 0)),
                  pl.BlockSpec((2, 256, 256), lambda c, t: (c, 0, 0))],
        out_specs=[pl.BlockSpec((tl, LC), lambda c, t: (t, c)), pl.BlockSpec((tl, LC), lambda c, t: (t, c))],
        out_shape=[jax.ShapeDtypeStruct((S, D), F32), jax.ShapeDtypeStruct((S, D), BF16)],
        scratch_shapes=[pltpu.VMEM((tl + 8, LC), F32), pltpu.VMEM((tl, LC), F32), pltpu.VMEM((tl, LC), F32),
                        pltpu.VMEM((tl, LC), F32), pltpu.VMEM((1, LC), F32)],
        compiler_params=_cp("parallel", "arbitrary"),
    )(rest, rest, rest, pvec, wa, wx)


def lru_bwd(dh, h, rest, pvec, wa, wx, name):
    S = rest.shape[0]
    tl = min(LT, S)
    nt = S // tl

    def body(dh_ref, h_ref, hhalo_ref, xr_ref, xhalo_ref, pv_ref, wa_ref, wx_ref,
             dxr_ref, vacc_ref, dwa_ref, dwx_ref,
             xs, hs, a_s, ash_s, b_s, lam_s, dxe, anext, lnext, dxnext):
        ti = pl.program_id(1)
        tr = nt - 1 - ti

        @pl.when(ti == 0)
        def _():
            anext[...] = jnp.zeros_like(anext)
            lnext[...] = jnp.zeros_like(lnext)
            dxnext[...] = jnp.zeros_like(dxnext)
            vacc_ref[...] = jnp.zeros_like(vacc_ref)
            dwa_ref[...] = jnp.zeros_like(dwa_ref)
            dwx_ref[...] = jnp.zeros_like(dwx_ref)

        xs[0:8, :] = jnp.where(tr > 0, xhalo_ref[8:16, :].astype(F32), 0.0)
        xs[pl.ds(8, tl), :] = xr_ref[...].astype(F32)
        xc, xcb, r, ig, sp, a, mult, inv_mult = _lru_gates(xs, pv_ref, wa_ref, wx_ref, tl)

        a_s[pl.ds(0, tl), :] = a
        a_s[pl.ds(tl, 8), :] = jnp.broadcast_to(anext[...], (8, LC))
        ash_s[...] = a_s[pl.ds(1, tl), :]
        b_s[...] = dh_ref[...]
        row = lax.broadcasted_iota(jnp.int32, (8, LC), 0)

        def blk(k, c):
            o = pl.multiple_of((tl // 8 - 1 - k) * 8, 8)
            av = ash_s[pl.ds(o, 8), :]
            bv = b_s[pl.ds(o, 8), :]
            for d in (1, 2, 4):
                a_sh = pltpu.roll(av, 8 - d, 0)
                b_sh = pltpu.roll(bv, 8 - d, 0)
                m = row < 8 - d
                bv = jnp.where(m, bv + av * b_sh, bv)
                av = jnp.where(m, av * a_sh, av)
            lv = bv + av * c
            lam_s[pl.ds(o, 8), :] = lv
            return lv[0:1, :]

        lnext[...] = lax.fori_loop(0, tl // 8, blk, lnext[...])
        anext[...] = a[0:1, :]
        lam = lam_s[...]

        hs[0:8, :] = jnp.where(tr > 0, hhalo_ref[...], 0.0)
        hs[pl.ds(8, tl), :] = h_ref[...]
        d_a = lam * hs[pl.ds(7, tl), :]
        d_mult = lam * (ig * xc)
        d_ig = lam * mult * xc
        dxc = lam * mult * ig
        d_log_a = d_a * a - d_mult * (a * a) * inv_mult
        d_r = d_log_a * (-LRU_C * sp)
        vacc_ref[7:8, :] += _colsum(d_log_a * (-LRU_C * r)) * (-_sigmoid(-pv_ref[7:8, :]))
        d_pa = d_r * r * (1.0 - r)
        d_px = d_ig * ig * (1.0 - ig)
        vacc_ref[5:6, :] += _colsum(d_pa)
        vacc_ref[6:7, :] += _colsum(d_px)
        dpa = d_pa.astype(BF16)
        dpx = d_px.astype(BF16)
        back = []
        for g in range(2):
            sl = slice(256 * g, 256 * g + 256)
            dwa_ref[g] += _dot_tn(xcb[:, sl], dpa[:, sl])
            dwx_ref[g] += _dot_tn(xcb[:, sl], dpx[:, sl])
            back.append(_dot_nt(dpa[:, sl], wa_ref[g]) + _dot_nt(dpx[:, sl], wx_ref[g]))
        dxc = dxc + jnp.concatenate(back, axis=1)
        vacc_ref[4:5, :] += _colsum(dxc)
        for k in range(4):
            vacc_ref[k:k + 1, :] += _colsum(dxc * xs[pl.ds(5 + k, tl), :])
        dxe[pl.ds(0, tl), :] = dxc
        dxe[pl.ds(tl, 8), :] = dxnext[...]
        dxr = (pv_ref[3:4, :] * dxc + pv_ref[2:3, :] * dxe[pl.ds(1, tl), :]
               + pv_ref[1:2, :] * dxe[pl.ds(2, tl), :] + pv_ref[0:1, :] * dxe[pl.ds(3, tl), :])
        dxr_ref[...] = dxr.astype(BF16)
        dxnext[...] = dxc[0:8, :]

    hb = tl // 8
    rev = lambda t: nt - 1 - t
    halo = lambda t: jnp.maximum(rev(t) * hb - 1, 0)
    big = lambda: pltpu.VMEM((tl + 8, LC), F32)
    til = lambda: pltpu.VMEM((tl, LC), F32)
    return pl.pallas_call(
        body, name=name, grid=(2, nt),
        in_specs=[pl.BlockSpec((tl, LC), lambda c, t: (rev(t), c)),
                  pl.BlockSpec((tl, LC), lambda c, t: (rev(t), c)),
                  pl.BlockSpec((8, LC), lambda c, t: (halo(t), c)),
                  pl.BlockSpec((tl, LC), lambda c, t: (rev(t), c)),
                  pl.BlockSpec((16, LC), lambda c, t: (jnp.maximum(rev(t) * (tl // 16) - 1, 0), c)),
                  pl.BlockSpec((8, LC), lambda c, t: (0, c)),
                  pl.BlockSpec((2, 256, 256), lambda c, t: (c, 0, 0)),
                  pl.BlockSpec((2, 256, 256), lambda c, t: (c, 0, 0))],
        out_specs=[pl.BlockSpec((tl, LC), lambda c, t: (rev(t), c)),
                   pl.BlockSpec((8, LC), lambda c, t: (0, c)),
                   pl.BlockSpec((2, 256, 256), lambda c, t: (c, 0, 0)),
                   pl.BlockSpec((2, 256, 256), lambda c, t: (c, 0, 0))],
        out_shape=[jax.ShapeDtypeStruct((S, D), BF16), jax.ShapeDtypeStruct((8, D), F32),
                   jax.ShapeDtypeStruct((4, 256, 256), F32), jax.ShapeDtypeStruct((4, 256, 256), F32)],
        scratch_shapes=[big(), big(), big(), til(), til(), til(), big(),
                        pltpu.VMEM((1, LC), F32), pltpu.VMEM((1, LC), F32), pltpu.VMEM((8, LC), F32)],
        compiler_params=_cp("parallel", "arbitrary"),
    )(dh, h, h, rest, rest, pvec, wa, wx)


def mix_out_fwd(x, ao, hg, rest, vec, w_att_o, w_rec_o, w_out, name, tm=256):
    S = x.shape[0]
    tm = min(tm, S)

    def body(x_ref, ao_ref, hg_ref, ga_ref, gr_ref, vec_ref, wa_ref, wr_ref, wo_ref,
             xo_ref, att_ref, rec_ref, mg_ref, f_ref):
        att = _dot(ao_ref[...], wa_ref[...])
        rec = _dot(hg_ref[...], wr_ref[...])
        att_ref[...] = att.astype(BF16)
        rec_ref[...] = rec.astype(BF16)
        mg = (_sigmoid(ga_ref[...].astype(F32)) * att + _sigmoid(gr_ref[...].astype(F32)) * rec).astype(BF16)
        mg_ref[...] = mg
        f = _dot(mg, wo_ref[...])
        f_ref[...] = f.astype(BF16)
        y = f * lax.rsqrt(_mean(f * f) + EPS) * vec_ref[1:2, :]
        xo_ref[...] = x_ref[...] + (1.0 * vec_ref[4:5, :]) * y

    row = lambda i: (i, 0)
    full = lambda r: pl.BlockSpec((r, D), lambda i: (0, 0))
    return pl.pallas_call(
        body, name=name, grid=(S // tm,),
        in_specs=[pl.BlockSpec((tm, D), row), pl.BlockSpec((tm, 512), row), pl.BlockSpec((tm, D), row),
                  pl.BlockSpec((tm, D), lambda i: (i, 2)), pl.BlockSpec((tm, D), lambda i: (i, 3)),
                  full(8), full(512), full(D), full(D)],
        out_specs=[pl.BlockSpec((tm, D), row)] * 5,
        out_shape=[jax.ShapeDtypeStruct((S, D), F32)] + [jax.ShapeDtypeStruct((S, D), BF16)] * 4,
        compiler_params=_cp("parallel"),
    )(x, ao, hg, rest, rest, vec, w_att_o, w_rec_o, w_out)


def mix_out_bwd(dxo, f, att, rec, rest, h, vec, w_att_o, w_rec_o, w_out, name, tm=256):
    S = dxo.shape[0]
    tm = min(tm, S)

    def body(dxo_ref, f_ref, att_ref, rec_ref, yr_ref, ga_ref, gr_ref, h_ref, vec_ref, wa_ref, wr_ref, wo_ref,
             df_ref, da_ref, dr_ref, dao_ref, dh_ref, d3_ref, vacc_ref):
        @pl.when(pl.program_id(0) == 0)
        def _():
            vacc_ref[...] = jnp.zeros_like(vacc_ref)

        df = _post_norm_bwd(dxo_ref[...], f_ref[...].astype(F32), 1.0, vec_ref, vacc_ref).astype(BF16)
        df_ref[...] = df
        dm = _dot_nt(df, wo_ref[...])
        sa = _sigmoid(ga_ref[...].astype(F32))
        sr = _sigmoid(gr_ref[...].astype(F32))
        d_att = (dm * sa).astype(BF16)
        d_rec = (dm * sr).astype(BF16)
        da_ref[...] = d_att
        dr_ref[...] = d_rec
        d3_ref[1] = (dm * att_ref[...].astype(F32) * (sa * (1.0 - sa))).astype(BF16)
        d3_ref[2] = (dm * rec_ref[...].astype(F32) * (sr * (1.0 - sr))).astype(BF16)
        dao_ref[...] = _dot_nt(d_att, wa_ref[...]).astype(BF16)
        d_hg = _dot_nt(d_rec, wr_ref[...])
        yr = yr_ref[...].astype(F32)
        t = jnp.tanh(_GK * (yr + 0.044715 * yr * yr * yr))
        dh_ref[...] = d_hg * (0.5 * yr * (1.0 + t))
        gelu_grad = 0.5 * (1.0 + t) + 0.5 * yr * (1.0 - t * t) * _GK * (1.0 + 3.0 * 0.044715 * yr * yr)
        d3_ref[0] = (d_hg * h_ref[...] * gelu_grad).astype(BF16)

    row = lambda i: (i, 0)
    full = lambda r: pl.BlockSpec((r, D), lambda i: (0, 0))
    return pl.pallas_call(
        body, name=name, grid=(S // tm,),
        in_specs=[pl.BlockSpec((tm, D), row)] * 4
        + [pl.BlockSpec((tm, D), lambda i: (i, 1)), pl.BlockSpec((tm, D), lambda i: (i, 2)),
           pl.BlockSpec((tm, D), lambda i: (i, 3)), pl.BlockSpec((tm, D), row),
           full(8), full(512), full(D), full(D)],
        out_specs=[pl.BlockSpec((tm, D), row)] * 3
        + [pl.BlockSpec((tm, 512), row), pl.BlockSpec((tm, D), row),
           pl.BlockSpec((3, tm, D), lambda i: (0, i, 0)), pl.BlockSpec((8, D), lambda i: (0, 0))],
        out_shape=[jax.ShapeDtypeStruct((S, D), BF16)] * 3
        + [jax.ShapeDtypeStruct((S, 512), BF16), jax.ShapeDtypeStruct((S, D), F32),
           jax.ShapeDtypeStruct((3, S, D), BF16), jax.ShapeDtypeStruct((8, D), F32)],
        compiler_params=_cp("arbitrary"),
    )(dxo, f, att, rec, rest, rest, rest, h, vec, w_att_o, w_rec_o, w_out)


def dw_in(h, dq, dkv, dxr, d3, name, tk=1024, tn=512):
    S = h.shape[0]
    tk = min(tk, S)
    nk = S // tk

    nj = PW // tn

    def body(h_ref, dq_ref, dkv_ref, dxr_ref, d3_ref, o_ref, acc):
        k, j = pl.program_id(0), pl.program_id(1)

        def add(b_ref):
            part = _dot_tn(h_ref[...], b_ref[...])

            @pl.when(k == 0)
            def _():
                acc[j] = part

            @pl.when(k > 0)
            def _():
                acc[j] += part

        @pl.when(j == 0)
        def _():
            add(dq_ref)

        @pl.when((j >= 1) & (j < 3))
        def _():
            add(dkv_ref)

        @pl.when((j >= 3) & (j < 5))
        def _():
            add(dxr_ref)

        @pl.when(j >= 5)
        def _():
            add(d3_ref)

        @pl.when(k == nk - 1)
        def _():
            o_ref[...] = acc[j].astype(BF16)

    g3 = lambda j: jnp.clip(j - 5, 0, 5)
    return pl.pallas_call(
        body, name=name, grid=(nk, nj),
        in_specs=[pl.BlockSpec((tk, D), lambda k, j: (k, 0)),
                  pl.BlockSpec((None, tk, tn), lambda k, j: (0, k, 0)),
                  pl.BlockSpec((None, tk, tn), lambda k, j: (jnp.clip(j - 1, 0, 1), k, 0)),
                  pl.BlockSpec((tk, tn), lambda k, j: (k, jnp.clip(j - 3, 0, 1))),
                  pl.BlockSpec((None, tk, tn), lambda k, j: (g3(j) // 2, k, g3(j) % 2))],
        out_specs=pl.BlockSpec((D, tn), lambda k, j: (0, jnp.where(k == nk - 1, j, 0))),
        out_shape=jax.ShapeDtypeStruct((D, PW), BF16),
        scratch_shapes=[pltpu.VMEM((nj, D, tn), F32)],
        compiler_params=_cp("arbitrary", "arbitrary"),
    )(h, dq, dkv, dxr, d3)


def ada_fwd(c_all, w_ada, b_ada, name, tn=768):
    n = w_ada.shape[1]

    def body(c_ref, w_ref, b_ref, o_ref):
        cv = c_ref[...]
        ca = (cv * _sigmoid(cv)).astype(BF16)
        o_ref[...] = _dot(ca, w_ref[...].astype(BF16)) + b_ref[...]

    return pl.pallas_call(
        body, name=name, grid=(n // tn,),
        in_specs=[pl.BlockSpec((8, D), lambda j: (0, 0)), pl.BlockSpec((D, tn), lambda j: (0, j)),
                  pl.BlockSpec((1, tn), lambda j: (0, j))],
        out_specs=pl.BlockSpec((8, tn), lambda j: (0, j)),
        out_shape=jax.ShapeDtypeStruct((8, n), F32),
        compiler_params=_cp("parallel"),
    )(c_all, w_ada, b_ada)


def ada_bwd(c_all_t, dmod, name, tn=768):
    n = dmod.shape[1]

    def body(c_ref, d_ref, o_ref):
        cv = c_ref[...]
        ca = (cv * _sigmoid(cv)).astype(BF16)
        o_ref[...] = _dot(ca, d_ref[...].astype(BF16))

    return pl.pallas_call(
        body, name=name, grid=(n // tn,),
        in_specs=[pl.BlockSpec((D, 128), lambda j: (0, 0)), pl.BlockSpec((128, tn), lambda j: (0, j))],
        out_specs=pl.BlockSpec((D, tn), lambda j: (0, j)),
        out_shape=jax.ShapeDtypeStruct((D, n), F32),
        compiler_params=_cp("parallel"),
    )(c_all_t, dmod)


def _row_tile(rows, cols, itemsize=4, budget=1536 * 1024):
    best = None
    for t in range(8, rows + 1, 8):
        if rows % t == 0 and t * cols * itemsize <= budget:
            best = t
    return rows if best is None else best


def sum_lead(parts, name, out_dtype=F32):
    n, R, C = parts.shape
    tr = _row_tile(R, C * n)

    def body(p_ref, o_ref):
        acc = p_ref[0].astype(F32)
        for k in range(1, n):
            acc = acc + p_ref[k].astype(F32)
        o_ref[...] = acc.astype(out_dtype)

    return pl.pallas_call(
        body, name=name, grid=(R // tr,),
        in_specs=[pl.BlockSpec((n, tr, C), lambda i: (0, i, 0))],
        out_specs=pl.BlockSpec((tr, C), lambda i: (i, 0)),
        out_shape=jax.ShapeDtypeStruct((R, C), out_dtype),
        compiler_params=_cp("parallel"),
    )(parts)


def adamw(w, g, m, v, name):
    R, C = w.shape
    tr = _row_tile(R, C * 7, budget=8 * 1024 * 1024)

    def body(w_ref, g_ref, m_ref, v_ref, d_ref, mo_ref, vo_ref):
        gv = g_ref[...]
        mn = ADAM_B1 * m_ref[...] + (1.0 - ADAM_B1) * gv
        vn = ADAM_B2 * v_ref[...] + (1.0 - ADAM_B2) * (gv * gv)
        m_hat = mn / (1.0 - ADAM_B1 ** ADAM_STEP)
        v_hat = vn / (1.0 - ADAM_B2 ** ADAM_STEP)
        d_ref[...] = -ADAM_LR * (m_hat / (jnp.sqrt(v_hat) + ADAM_EPS) + ADAM_WD * w_ref[...])
        mo_ref[...] = mn
        vo_ref[...] = vn

    spec = pl.BlockSpec((tr, C), lambda i: (i, 0))
    return pl.pallas_call(
        body, name=name, grid=(R // tr,),
        in_specs=[spec] * 4, out_specs=[spec] * 3,
        out_shape=[jax.ShapeDtypeStruct((R, C), F32)] * 3,
        compiler_params=_cp("parallel"),
    )(w, g, m, v)


def _mesh_pos():
    return lax.axis_index("x"), lax.axis_index("y"), lax.axis_index("c")


def _other_chips(mx, my):
    return [(1 - mx, my), (mx, 1 - my), (1 - mx, 1 - my)]


def ag_small(x, name):
    R = x.shape[0]

    def body(x_ref, out_ref, send_sems, recv_sems, local_sem):
        mx, my, mc = _mesh_pos()
        me, sibling = (mx, my, mc), (mx, my, 1 - mc)
        chips = _other_chips(mx, my)

        def slot(px, py, pc):
            return out_ref.at[4 * px + 2 * py + pc]

        def copy(k, block, to, src=None):
            return pltpu.make_async_remote_copy(
                src_ref=slot(*block) if src is None else src, dst_ref=slot(*block),
                send_sem=send_sems.at[k], recv_sem=recv_sems.at[k], device_id=to, device_id_type=MESH)

        mine = pltpu.make_async_copy(x_ref, slot(*me), local_sem)
        mine.start()
        first = [copy(0, me, sibling, src=x_ref)]
        first += [copy(1 + j, me, (*chip, mc), src=x_ref) for j, chip in enumerate(chips)]
        for cp in first:
            cp.start()
        passed = [copy(4 + j, (*chip, mc), sibling) for j, chip in enumerate(chips)]
        for j, chip in enumerate(chips):
            copy(1 + j, (*chip, mc), me).wait_recv()
            passed[j].start()
        copy(0, sibling, me).wait_recv()
        for j, chip in enumerate(chips):
            copy(4 + j, (*chip, 1 - mc), me).wait_recv()
        for cp in first + passed:
            cp.wait_send()
        mine.wait()

    return pl.pallas_call(
        body, name=name,
        out_shape=jax.ShapeDtypeStruct((N_DEV, R, 128), F32),
        in_specs=[pl.BlockSpec(memory_space=pltpu.VMEM)],
        out_specs=pl.BlockSpec(memory_space=pltpu.VMEM),
        scratch_shapes=[pltpu.SemaphoreType.DMA((7,)), pltpu.SemaphoreType.DMA((7,)), pltpu.SemaphoreType.DMA],
        compiler_params=pltpu.CompilerParams(vmem_limit_bytes=VMEM_LIMIT),
    )(x)


BIG = (("ffn1_w_gu", "col", D, PW), ("ffn1_w_down", "row", FF, D), ("w_in", "col", D, PW),
       ("w_att_o", "col", 512, D), ("w_rec_o", "row", D, D), ("w_out", "row", D, D),
       ("ffn2_w_gu", "col", D, PW), ("ffn2_w_down", "row", FF, D))
NBIG = len(BIG)


def _shard_shape(kind, R, C):
    return (R, C // 4) if kind == "col" else (R // 4, C)


def _region(ref, kind, R, C, q, half, t, tr):
    sr, sc = _shard_shape(kind, R, C)
    if kind == "col":
        return ref.at[pl.ds(pl.multiple_of(half * (R // 2) + t * tr, 16), tr), pl.ds(q * sc, sc)]
    return ref.at[pl.ds(pl.multiple_of(q * sr + t * tr, 16), tr), pl.ds(half * (C // 2), C // 2)]


def ag_local(w, kind, R, C, p_arr, name, after=()):
    sr, sc = _shard_shape(kind, R, C)
    tr = _row_tile(sr, sc, budget=2 * 1024 * 1024)
    nt = sr // tr
    after = list(after)

    def body(p_ref, w_ref, *rest):
        rest[-1][...] = w_ref[...].astype(BF16)

    if kind == "col":
        o_spec = pl.BlockSpec((tr, sc), lambda i, p: (i, p[0]))
    else:
        o_spec = pl.BlockSpec((tr, sc), lambda i, p: (p[0] * nt + i, 0))
    return pl.pallas_call(
        body, name=name,
        grid_spec=pltpu.PrefetchScalarGridSpec(
            num_scalar_prefetch=1, grid=(nt,),
            in_specs=[pl.BlockSpec((tr, sc), lambda i, p: (i, 0))] + [ANY] * len(after), out_specs=o_spec),
        out_shape=jax.ShapeDtypeStruct((R, C), BF16),
        compiler_params=_cp("parallel"),
    )(p_arr, w, *after)


HBM_SPEC = pl.BlockSpec(memory_space=pltpu.HBM)
SEM_SPEC = pl.BlockSpec(memory_space=pltpu.SEMAPHORE)


def _ag_sems(geoms):
    return sum(6 if both else 3 for (_, _, _, both) in geoms)


def _ag_copies(fulls, geoms, ssem, rsem, mx, my, mc, q, h):
    chips = _other_chips(mx, my)
    out, base = [], 0
    for w, (kind, R, C, both) in enumerate(geoms):
        sr, sc = _shard_shape(kind, R, C)
        hr = sr // 2 if kind == "col" else sr
        reg = _region(fulls[w], kind, R, C, q, h, 0, hr)
        out.append([pltpu.make_async_remote_copy(
            src_ref=reg, dst_ref=reg, send_sem=ssem.at[base + 3 * t + k], recv_sem=rsem.at[base + 3 * t + k],
            device_id=(*chips[k], mc if t == 0 else 1 - mc), device_id_type=MESH)
            for t in range(2 if both else 1) for k in range(3)])
        base += 6 if both else 3
    return out


def ag_start(fulls, geoms, after, name):
    n = len(fulls)
    after = list(after)
    m = len(after)

    def body(*refs):
        ssem, rsem = refs[n + m:n + m + 2]
        outs, token = refs[n + m + 2:2 * n + m + 2], refs[2 * n + m + 2]
        mx, my, mc = _mesh_pos()
        p = 2 * mx + my
        col = [w for w, g in enumerate(geoms) if g[0] == "col"]
        row = [w for w, g in enumerate(geoms) if g[0] == "row"]
        for q in range(4):
            @pl.when(p == q)
            def _(q=q):
                cps = _ag_copies(outs, geoms, ssem, rsem, mx, my, mc, q, mc)
                for w in col:
                    for cp in cps[w]:
                        cp.start()
        for h in range(2):
            @pl.when(mc == h)
            def _(h=h):
                cps = _ag_copies(outs, geoms, ssem, rsem, mx, my, mc, p, h)
                for w in row:
                    for cp in cps[w]:
                        cp.start()
        token[...] = jnp.zeros_like(token)

    res = pl.pallas_call(
        body, name=name,
        out_shape=[pltpu.SemaphoreType.DMA((_ag_sems(geoms),)), pltpu.SemaphoreType.DMA((_ag_sems(geoms),))]
        + [pltpu.HBM(a.shape, a.dtype) for a in fulls] + [jax.ShapeDtypeStruct((8, 128), F32)],
        in_specs=[HBM_SPEC] * n + [ANY] * m,
        out_specs=[SEM_SPEC, SEM_SPEC] + [HBM_SPEC] * n + [pl.BlockSpec(memory_space=pltpu.VMEM)],
        input_output_aliases={w: 2 + w for w in range(n)},
        compiler_params=pltpu.CompilerParams(has_side_effects=pltpu.SideEffectType.DATAFLOW_SIDE_EFFECTING),
    )(*[pltpu.with_memory_space_constraint(a, pltpu.HBM) for a in fulls], *after)
    return res[0], res[1], list(res[2:2 + n]), res[2 + n]


def ag_wait(fulls, geoms, ssem, rsem, after, name):
    n = len(fulls)

    def body(*refs):
        ins, ssem_ref, rsem_ref = refs[:n], refs[n], refs[n + 1]
        mx, my, mc = _mesh_pos()
        for cps in _ag_copies(ins, geoms, ssem_ref, rsem_ref, mx, my, mc, 0, 0):
            for cp in cps:
                cp.wait_send()
                cp.wait_recv()

    return list(pl.pallas_call(
        body, name=name,
        out_shape=[pltpu.HBM(a.shape, a.dtype) for a in fulls],
        in_specs=[HBM_SPEC] * n + [SEM_SPEC, SEM_SPEC, ANY],
        out_specs=[HBM_SPEC] * n,
        input_output_aliases={w: w for w in range(n)},
        compiler_params=pltpu.CompilerParams(has_side_effects=pltpu.SideEffectType.DATAFLOW_SIDE_EFFECTING),
    )(*fulls, ssem, rsem, after))


def ag_forward(full, kind, R, C, name):
    sr, sc = _shard_shape(kind, R, C)
    hr, hc = (sr // 2, sc) if kind == "col" else (sr, sc // 2)
    tr = _row_tile(hr, hc, itemsize=2, budget=512 * 1024)
    nt = hr // tr

    total = 3 * nt

    def body(src_ref, full_ref, stage, lsem, ssem, rsem):
        step = pl.program_id(0) * nt + pl.program_id(1)
        par = step % 2
        mx, my, mc = _mesh_pos()

        def load(s, q, h, t):
            return pltpu.make_async_copy(_region(src_ref, kind, R, C, q, h, t, tr), stage.at[s], lsem.at[s])

        def push(s, q, h, t):
            return pltpu.make_async_remote_copy(src_ref=stage.at[s], dst_ref=_region(full_ref, kind, R, C, q, h, t, tr),
                                                send_sem=ssem.at[s], recv_sem=rsem, device_id=(mx, my, 1 - mc),
                                                device_id_type=MESH)

        def for_tile(stp, fn):
            q_k = _partner_chip(stp // nt, 2 * mx + my)
            if kind == "col":
                for q in range(4):
                    @pl.when(q_k == q)
                    def _(q=q):
                        fn(q, mc, stp % nt)
            else:
                for h in range(2):
                    @pl.when(mc == h)
                    def _(h=h):
                        fn(q_k, h, stp % nt)

        @pl.when(step == 0)
        def _():
            for_tile(step, lambda q, h, t: load(0, q, h, t).start())

        load(par, 0, 0, 0).wait()
        for_tile(step, lambda q, h, t: push(par, q, h, t).start())

        @pl.when(step + 1 < total)
        def _():
            @pl.when(step >= 1)
            def _():
                push(1 - par, 0, 0, 0).wait_send()
            for_tile(step + 1, lambda q, h, t: load(1 - par, q, h, t).start())

        @pl.when(step == total - 1)
        def _():
            push(par, 0, 0, 0).wait_send()
            push(1 - par, 0, 0, 0).wait_send()
            three = full_ref.at[pl.ds(0, hr), pl.ds(0, 3 * hc)] if kind == "col" else full_ref.at[pl.ds(0, 3 * hr), pl.ds(0, hc)]
            pltpu.make_async_remote_copy(src_ref=three, dst_ref=three, send_sem=ssem.at[0], recv_sem=rsem,
                                         device_id=(mx, my, 1 - mc), device_id_type=MESH).wait_recv()

    return pl.pallas_call(
        body, name=name, grid=(3, nt),
        in_specs=[ANY], out_specs=ANY,
        out_shape=jax.ShapeDtypeStruct((R, C), BF16),
        scratch_shapes=[pltpu.VMEM((2, tr, hc), BF16), pltpu.SemaphoreType.DMA((2,)), pltpu.SemaphoreType.DMA((2,)),
                        pltpu.SemaphoreType.DMA],
        input_output_aliases={0: 0},
        compiler_params=_cp("arbitrary", "arbitrary"),
    )(full)


def _half_shape(kind, R, C):
    return (R // 2, C) if kind == "col" else (R, C // 2)


def _piece_shape(kind, R, C):
    return (R // 2, C // 4) if kind == "col" else (R // 4, C // 2)


def pair_push(g, kind, c_arr, name):
    R, C = g.shape
    hr, hc = _half_shape(kind, R, C)
    tr = _row_tile(hr, hc, itemsize=2, budget=1024 * 1024)
    nt = hr // tr

    def body(c_ref, g_ref, out_ref, stage, ssem, rsem):
        i = pl.program_id(0)
        slot = i % 2
        mx, my, mc = _mesh_pos()

        def push(s, t):
            return pltpu.make_async_remote_copy(
                src_ref=stage.at[s], dst_ref=out_ref.at[pl.ds(pl.multiple_of(t * tr, 16), tr)],
                send_sem=ssem.at[s], recv_sem=rsem, device_id=(mx, my, 1 - mc), device_id_type=MESH)

        @pl.when(i >= 2)
        def _():
            push(slot, 0).wait_send()

        stage[slot] = g_ref[...]
        push(slot, i).start()

        @pl.when(i == nt - 1)
        def _():
            push(slot, 0).wait_send()
            if nt >= 2:
                push(1 - slot, 0).wait_send()
            pltpu.make_async_remote_copy(src_ref=out_ref, dst_ref=out_ref, send_sem=ssem.at[0], recv_sem=rsem,
                                         device_id=(mx, my, 1 - mc), device_id_type=MESH).wait_recv()

    if kind == "col":
        g_spec = pl.BlockSpec((tr, hc), lambda i, c: ((1 - c[0]) * nt + i, 0))
    else:
        g_spec = pl.BlockSpec((tr, hc), lambda i, c: (i, 1 - c[0]))
    return pl.pallas_call(
        body, name=name,
        grid_spec=pltpu.PrefetchScalarGridSpec(
            num_scalar_prefetch=1, grid=(nt,), in_specs=[g_spec], out_specs=ANY,
            scratch_shapes=[pltpu.VMEM((2, tr, hc), BF16), pltpu.SemaphoreType.DMA((2,)), pltpu.SemaphoreType.DMA]),
        out_shape=jax.ShapeDtypeStruct((hr, hc), BF16),
        compiler_params=_cp("arbitrary"),
    )(c_arr, g)


def _partner_chip(k, p):
    return p ^ jnp.where(k == 0, 2, jnp.where(k == 1, 1, jnp.where(k == 2, 3, 0)))


def pair_add(g, got, kind, cp_arr, name):
    R, C = g.shape
    pr, pc = _piece_shape(kind, R, C)
    tr = _row_tile(pr, pc, itemsize=2, budget=1024 * 1024)
    nt = pr // tr

    def body(cp_ref, g_ref, got_ref, ps_ref, rb_ref):
        tile = (g_ref[...].astype(F32) + got_ref[...].astype(F32)).astype(BF16)
        ps_ref[...] = tile

        @pl.when(pl.program_id(1) == cp_ref[1])
        def _():
            rb_ref[...] = tile

    if kind == "col":
        g_spec = pl.BlockSpec((tr, pc), lambda i, q, cp: (cp[0] * nt + i, q))
        got_spec = pl.BlockSpec((tr, pc), lambda i, q, cp: (i, q))
    else:
        g_spec = pl.BlockSpec((tr, pc), lambda i, q, cp: (q * nt + i, cp[0]))
        got_spec = pl.BlockSpec((tr, pc), lambda i, q, cp: (q * nt + i, 0))
    return pl.pallas_call(
        body, name=name,
        grid_spec=pltpu.PrefetchScalarGridSpec(
            num_scalar_prefetch=1, grid=(nt, 4), in_specs=[g_spec, got_spec],
            out_specs=[pl.BlockSpec((None, tr, pc), lambda i, q, cp: (q, i, 0)),
                       pl.BlockSpec((None, tr, pc), lambda i, q, cp: (cp[1], i, 0))]),
        out_shape=[jax.ShapeDtypeStruct((4, pr, pc), BF16)] * 2,
        compiler_params=_cp("arbitrary", "arbitrary"),
    )(cp_arr, g, got)


def _rs_copies(ps, rb, ssem, rsem, mx, my, mc):
    p = 2 * mx + my
    out = []
    for w in range(len(ps)):
        for k, chip in enumerate(_other_chips(mx, my)):
            out.append(pltpu.make_async_remote_copy(
                src_ref=ps[w].at[2 * chip[0] + chip[1]], dst_ref=rb[w].at[p], send_sem=ssem.at[3 * w + k],
                recv_sem=rsem.at[3 * w + k], device_id=(*chip, mc), device_id_type=MESH))
    return out


def rs_start(ps, rb, after, name):
    n = len(ps)
    after = list(after)
    m = len(after)

    def body(*refs):
        ssem, rsem = refs[2 * n + m:2 * n + m + 2]
        ps_o = refs[2 * n + m + 2:3 * n + m + 2]
        rb_o = refs[3 * n + m + 2:4 * n + m + 2]
        token = refs[4 * n + m + 2]
        for cp in _rs_copies(ps_o, rb_o, ssem, rsem, *_mesh_pos()):
            cp.start()
        token[...] = jnp.zeros_like(token)

    both = list(ps) + list(rb)
    res = pl.pallas_call(
        body, name=name,
        out_shape=[pltpu.SemaphoreType.DMA((3 * n,)), pltpu.SemaphoreType.DMA((3 * n,))]
        + [pltpu.HBM(a.shape, a.dtype) for a in both] + [jax.ShapeDtypeStruct((8, 128), F32)],
        in_specs=[HBM_SPEC] * (2 * n) + [ANY] * m,
        out_specs=[SEM_SPEC, SEM_SPEC] + [HBM_SPEC] * (2 * n) + [pl.BlockSpec(memory_space=pltpu.VMEM)],
        input_output_aliases={w: 2 + w for w in range(2 * n)},
        compiler_params=pltpu.CompilerParams(has_side_effects=pltpu.SideEffectType.DATAFLOW_SIDE_EFFECTING),
    )(*[pltpu.with_memory_space_constraint(a, pltpu.HBM) for a in both], *after)
    return res[0], res[1], list(res[2:2 + n]), list(res[2 + n:2 + 2 * n]), res[2 + 2 * n]


def rs_wait(ps, rb, ssem, rsem, after, name):
    n = len(ps)
    after = list(after)
    m = len(after)

    def body(*refs):
        ps_i, rb_i = refs[:n], refs[n:2 * n]
        ssem_ref, rsem_ref = refs[2 * n], refs[2 * n + 1]
        for cp in _rs_copies(ps_i, rb_i, ssem_ref, rsem_ref, *_mesh_pos()):
            cp.wait_send()
            cp.wait_recv()

    both = list(ps) + list(rb)
    res = pl.pallas_call(
        body, name=name,
        out_shape=[pltpu.HBM(a.shape, a.dtype) for a in both],
        in_specs=[HBM_SPEC] * (2 * n) + [SEM_SPEC, SEM_SPEC] + [ANY] * m,
        out_specs=[HBM_SPEC] * (2 * n),
        input_output_aliases={w: w for w in range(2 * n)},
        compiler_params=pltpu.CompilerParams(has_side_effects=pltpu.SideEffectType.DATAFLOW_SIDE_EFFECTING),
    )(*both, ssem, rsem, *after)
    return list(res[n:])


def sum_share(parts, kind, R, C, name):
    _, pr, pc = parts.shape
    sr, sc = _shard_shape(kind, R, C)
    tr = _row_tile(pr, pc * 4, budget=4 * 1024 * 1024)
    nt = pr // tr

    def body(p_ref, fin_ref, stage, lsem, ssem, rsem):
        i = pl.program_id(0)
        slot = i % 2
        mx, my, mc = _mesh_pos()

        def region(h, t):
            r0 = pl.multiple_of(t * tr, 8)
            if kind == "col":
                return fin_ref.at[pl.ds(pl.multiple_of(h * pr + r0, 8), tr)]
            return fin_ref.at[pl.ds(r0, tr), pl.ds(h * pc, pc)]

        def copies(s, h, t):
            return (pltpu.make_async_copy(stage.at[s], region(h, t), lsem.at[s]),
                    pltpu.make_async_remote_copy(src_ref=stage.at[s], dst_ref=region(h, t), send_sem=ssem.at[s],
                                                 recv_sem=rsem, device_id=(mx, my, 1 - mc), device_id_type=MESH))

        def wait_sent(s):
            loc, rem = copies(s, 0, 0)
            loc.wait()
            rem.wait_send()

        @pl.when(i >= 2)
        def _():
            wait_sent(slot)

        acc = p_ref[0].astype(F32)
        for k in range(1, 4):
            acc = acc + p_ref[k].astype(F32)
        stage[slot] = acc
        if kind == "col":
            for cp in copies(slot, mc, i):
                cp.start()
        else:
            for h in range(2):
                @pl.when(mc == h)
                def _(h=h):
                    for cp in copies(slot, h, i):
                        cp.start()

        @pl.when(i == nt - 1)
        def _():
            wait_sent(slot)
            if nt >= 2:
                wait_sent(1 - slot)
            half = fin_ref.at[pl.ds(0, pr), pl.ds(0, pc)]
            pltpu.make_async_remote_copy(src_ref=half, dst_ref=half, send_sem=ssem.at[0], recv_sem=rsem,
                                         device_id=(mx, my, 1 - mc), device_id_type=MESH).wait_recv()

    return pl.pallas_call(
        body, name=name, grid=(nt,),
        in_specs=[pl.BlockSpec((4, tr, pc), lambda i: (0, i, 0))],
        out_specs=ANY,
        out_shape=jax.ShapeDtypeStruct((sr, sc), F32),
        scratch_shapes=[pltpu.VMEM((2, tr, pc), F32), pltpu.SemaphoreType.DMA((2,)), pltpu.SemaphoreType.DMA((2,)),
                        pltpu.SemaphoreType.DMA],
        compiler_params=_cp("arbitrary"),
    )(parts)


def _pack(parts, rows):
    flat = []
    for a in parts:
        a = jnp.ravel(a).astype(F32)
        flat.append(jnp.pad(a, (0, (-a.shape[0]) % 128)))
    v = jnp.concatenate(flat)
    return jnp.pad(v, (0, rows * 128 - v.shape[0])).reshape(rows, 128)


def _unpack(block, shapes):
    lead = block.shape[:-2]
    v = block.reshape(lead + (-1,))
    out, off = [], 0
    for shp in shapes:
        n = int(np.prod(shp))
        out.append(v[..., off:off + n].reshape(lead + tuple(shp)))
        off += n + (-n) % 128
    return out


def _block_diag4(w):
    w4 = w.reshape(4, 4, 64, 64)
    eye = jnp.eye(4, dtype=w.dtype)
    return (w4[:, :, :, None, :] * eye[None, :, None, :, None]).reshape(4, 256, 256)


def _diag_blocks(bd):
    b5 = bd.reshape(4, 4, 64, 4, 64)
    return jnp.stack([b5[:, i, :, i, :] for i in range(4)], axis=1).reshape(16, 64, 64)


def _bias_window(rel_bias):
    m = (np.arange(768) + 127) % 768 - 127
    w = rel_bias[:, np.clip(512 - m, -128, 128) + 128]
    win = jnp.tile(w, (1, 128))[:, :128 * 767].reshape(8, 128, 767)[:, :, :WIN]
    qh = np.arange(128)[:, None] // CHUNK
    kc = np.arange(WIN)[None, :] // CHUNK
    valid = (kc >= qh) & (kc <= qh + 8)
    return jnp.where(jnp.asarray(valid)[None], win, NEG)


SMALL = ("b_ada", "norm_pre", "norm_post", "rel_bias", "conv_w", "conv_b", "lru_wa", "lru_ba", "lru_wx",
         "lru_bx", "lru_lambda")
WEIGHTS = ("w_ada", "b_ada", "norm_pre", "norm_post", "ffn1_w_gu", "ffn1_w_down", "w_in", "rel_bias", "conv_w",
           "conv_b", "lru_wa", "lru_ba", "lru_wx", "lru_bx", "lru_lambda", "w_att_o", "w_rec_o", "w_out",
           "ffn2_w_gu", "ffn2_w_down")


def kernel(x, c, w_ada, b_ada, norm_pre, norm_post, ffn1_w_gu, ffn1_w_down, w_in, rel_bias, conv_w, conv_b, lru_wa, lru_ba, lru_wx, lru_bx, lru_lambda, w_att_o, w_rec_o, w_out, ffn2_w_gu, ffn2_w_down, loss_target, m_w_ada, m_b_ada, m_norm_pre, m_norm_post, m_ffn1_w_gu, m_ffn1_w_down, m_w_in, m_rel_bias, m_conv_w, m_conv_b, m_lru_wa, m_lru_ba, m_lru_wx, m_lru_bx, m_lru_lambda, m_w_att_o, m_w_rec_o, m_w_out, m_ffn2_w_gu, m_ffn2_w_down, v_w_ada, v_b_ada, v_norm_pre, v_norm_post, v_ffn1_w_gu, v_ffn1_w_down, v_w_in, v_rel_bias, v_conv_w, v_conv_b, v_lru_wa, v_lru_ba, v_lru_wx, v_lru_bx, v_lru_lambda, v_w_att_o, v_w_rec_o, v_w_out, v_ffn2_w_gu, v_ffn2_w_down):
    W = dict(w_ada=w_ada, b_ada=b_ada, norm_pre=norm_pre, norm_post=norm_post, ffn1_w_gu=ffn1_w_gu,
             ffn1_w_down=ffn1_w_down, w_in=w_in, rel_bias=rel_bias, conv_w=conv_w, conv_b=conv_b, lru_wa=lru_wa,
             lru_ba=lru_ba, lru_wx=lru_wx, lru_bx=lru_bx, lru_lambda=lru_lambda, w_att_o=w_att_o, w_rec_o=w_rec_o,
             w_out=w_out, ffn2_w_gu=ffn2_w_gu, ffn2_w_down=ffn2_w_down)
    M = dict(w_ada=m_w_ada, b_ada=m_b_ada, norm_pre=m_norm_pre, norm_post=m_norm_post, ffn1_w_gu=m_ffn1_w_gu,
             ffn1_w_down=m_ffn1_w_down, w_in=m_w_in, rel_bias=m_rel_bias, conv_w=m_conv_w, conv_b=m_conv_b,
             lru_wa=m_lru_wa, lru_ba=m_lru_ba, lru_wx=m_lru_wx, lru_bx=m_lru_bx, lru_lambda=m_lru_lambda,
             w_att_o=m_w_att_o, w_rec_o=m_w_rec_o, w_out=m_w_out, ffn2_w_gu=m_ffn2_w_gu, ffn2_w_down=m_ffn2_w_down)
    V = dict(w_ada=v_w_ada, b_ada=v_b_ada, norm_pre=v_norm_pre, norm_post=v_norm_post, ffn1_w_gu=v_ffn1_w_gu,
             ffn1_w_down=v_ffn1_w_down, w_in=v_w_in, rel_bias=v_rel_bias, conv_w=v_conv_w, conv_b=v_conv_b,
             lru_wa=v_lru_wa, lru_ba=v_lru_ba, lru_wx=v_lru_wx, lru_bx=v_lru_bx, lru_lambda=v_lru_lambda,
             w_att_o=v_w_att_o, w_rec_o=v_w_rec_o, w_out=v_w_out, ffn2_w_gu=v_ffn2_w_gu, ffn2_w_down=v_ffn2_w_down)
    mx, my, mc = _mesh_pos()
    p = 2 * mx + my
    e = 4 * mx + 2 * my + mc
    xs = x[0]

    c_arr = jnp.reshape(mc, (1,)).astype(jnp.int32)
    cp_arr = jnp.stack([mc, p]).astype(jnp.int32)
    p_arr = jnp.reshape(p, (1,)).astype(jnp.int32)
    direct = ("w_att_o", "w_rec_o", "w_out", "ffn2_w_gu", "ffn2_w_down")
    geoms = [(kind, R, C, n in direct) for (n, kind, R, C) in BIG]
    names = [b[0] for b in BIG]
    placed = [ag_local(W[n][0], kind, R, C, p_arr, "ag_local_" + n) for (n, kind, R, C) in BIG[:2]]

    def arrived(fly, lo, hi, ssem, rsem, after, tag):
        done = ag_wait(fly, geoms[lo:hi], ssem, rsem, after, "ag_wait_" + tag)
        return [a if both else ag_forward(a, kind, R, C, "ag_forward_" + n)
                for a, (kind, R, C, both), n in zip(done, geoms[lo:hi], names[lo:hi])]

    g1 = ag_small(_pack([c, norm_pre, norm_post, conv_w], 32), "ag_small_params")
    c_all, npre4, npost4, cw4 = _unpack(g1, [(D,), (3, 256), (3, 256), (4, 256)])
    chipwise = lambda a: jnp.moveaxis(a[0::2], 0, 1).reshape(a.shape[1], D)
    npre, npost, conv_full = chipwise(npre4), chipwise(npost4), chipwise(cw4)

    b_cols = lax.dynamic_slice(b_ada, (0, p * 2304), (1, 2304))
    mod_cols = ada_fwd(c_all, w_ada[0], b_cols, "ada_fwd")
    g2 = ag_small(mod_cols.reshape(144, 128), "ag_mod")
    mod_all = jnp.moveaxis(g2[0::2].reshape(4, 8, 2304), 0, 1).reshape(8, 9 * D)
    mod = lax.dynamic_index_in_dim(mod_all, e, 0, keepdims=False).reshape(3, 3, D)
    zeros3 = jnp.zeros((3, D), F32)
    vecs = [jnp.concatenate([npre[k:k + 1], npost[k:k + 1], mod[k], zeros3], axis=0) for k in range(3)]

    f1_s, f1_r, f1_fly, tok0 = ag_start(placed[:2], geoms[:2], [g2], "ag_start_ffn1")
    placed += [ag_local(W[n][0], kind, R, C, p_arr, "ag_local_" + n, after=[tok0]) for (n, kind, R, C) in BIG[2:]]
    f1_gu, f1_dn = arrived(f1_fly, 0, 2, f1_s, f1_r, placed[7], "ffn1")
    mix_s, mix_r, mix_fly, tok1 = ag_start(placed[2:6], geoms[2:6], [f1_gu, f1_dn], "ag_start_mixer")
    ffn_s, ffn_r, ffn_fly, tok2 = ag_start(placed[6:], geoms[6:], [tok1], "ag_start_ffn2")
    wa_bd = _block_diag4(lru_wa[0]).astype(BF16)
    wx_bd = _block_diag4(lru_wx[0]).astype(BF16)
    pvec = jnp.concatenate([conv_full, conv_b, lru_ba, lru_bx, lru_lambda], axis=0)
    bias = _bias_window(rel_bias[0]).reshape(4, 256, WIN)

    f1_u = f1_gu[:, FF:]
    x1, h1, g1_, u1, a1, f1 = ffn_fwd(xs, vecs[0] + tok2[0:1, 0:1], f1_gu, f1_u, f1_dn, 0.5, "ffn1_fwd")
    win, wao, wro, wout = arrived(mix_fly, 2, 6, mix_s, mix_r, x1, "mixer")
    h2, qkv, rest = proj_fwd(x1, vecs[1], win, "proj_fwd")
    ao = attn_fwd(qkv, bias, "attn_fwd")
    hl, hg = lru_fwd(rest, pvec, wa_bd, wx_bd, "lru_fwd")
    x2, att, rec, mg, f2 = mix_out_fwd(x1, ao, hg, rest, vecs[1], wao, wro, wout, "mix_out_fwd")
    f2_gu, f2_dn = arrived(ffn_fly, 6, 8, ffn_s, ffn_r, x2, "ffn2")
    f2_u = f2_gu[:, FF:]
    dy, h3, g3_, u3, a3, f3, lvec = ffn_fwd(x2, vecs[2], f2_gu, f2_u, f2_dn, 0.5, "ffn2_fwd", tgt=loss_target[0])
    loss = lax.psum(lvec[0, 0], ("x", "y", "c"))

    G, grads = {}, {}
    geo = {n: (kind, R, C) for (n, kind, R, C) in BIG}

    def reduce_begin(names, tag):
        ps, rb = [], []
        for n in names:
            got = pair_push(G[n], geo[n][0], c_arr, "rs_push_" + n)
            a, b = pair_add(G[n], got, geo[n][0], cp_arr, "rs_pair_sum_" + n)
            ps.append(a)
            rb.append(b)
        return rs_start(ps, rb, [], "rs_start_" + tag)

    def reduce_end(names, flight, after, tag):
        ssem, rsem, ps, rb, _ = flight
        for a, n in zip(rs_wait(ps, rb, ssem, rsem, after, "rs_wait_" + tag), names):
            grads[n] = sum_share(a, *geo[n], "rs_sum_share_" + n)[None]

    dx2, df3, dgu3, va2 = ffn_bwd(dy, x2, f3, g3_, u3, vecs[2], f2_gu, f2_u, f2_dn, 0.5, "ffn2_bwd")
    G["ffn2_w_gu"] = mm_tn(h3, dgu3, "dw_ffn2_gu", D, 1408, 1024)
    G["ffn2_w_down"] = mm_tn(a3, df3, "dw_ffn2_down", 1408, D, 1024)
    fly_ffn2 = reduce_begin(("ffn2_w_gu", "ffn2_w_down"), "ffn2")
    vec1 = vecs[1] + fly_ffn2[4][0:1, 0:1]
    df2, d_att, d_rec, dao, dhl, d3, va_out = mix_out_bwd(dx2, f2, att, rec, rest, hl, vec1, wao, wro, wout,
                                                          "mix_out_bwd")
    G["w_out"] = mm_tn(mg, df2, "dw_out", D, D, 1024)
    G["w_att_o"] = mm_tn(ao, d_att, "dw_att_o", 512, D, 1024)
    G["w_rec_o"] = mm_tn(hg, d_rec, "dw_rec_o", D, D, 1024)
    dq, db, dkv = attn_bwd(qkv, dao, bias, "attn_bwd")
    dxr, v_lru, dwa_bd, dwx_bd = lru_bwd(dhl, hl, rest, pvec, wa_bd, wx_bd, "lru_bwd")
    dx1, va_in = proj_bwd(dq, dkv, dxr, d3, win, x1, dx2, vecs[1], "proj_bwd")
    G["w_in"] = dw_in(h2, dq, dkv, dxr, d3, "dw_in")
    fly_mix = reduce_begin(("w_in", "w_att_o", "w_rec_o", "w_out"), "mixer")
    vec0 = vecs[0] + fly_mix[4][0:1, 0:1]
    dx0, df1, dgu1, va0 = ffn_bwd(dx1, xs, f1, g1_, u1, vec0, f1_gu, f1_u, f1_dn, 0.5, "ffn1_bwd")
    G["ffn1_w_gu"] = mm_tn(h1, dgu1, "dw_ffn1_gu", D, 1408, 1024)
    G["ffn1_w_down"] = mm_tn(a1, df1, "dw_ffn1_down", 1408, D, 1024)
    fly_ffn1 = reduce_begin(("ffn1_w_gu", "ffn1_w_down"), "ffn1")
    reduce_end(("ffn2_w_gu", "ffn2_w_down"), fly_ffn2, [fly_ffn1[4]], "ffn2")
    reduce_end(("w_in", "w_att_o", "w_rec_o", "w_out"), fly_mix, [fly_ffn1[4], grads["ffn2_w_down"]], "mixer")

    va1 = va_out + va_in
    vas = (va0, va1, va2)
    dmod = jnp.stack([v[2:5] for v in vas])
    part = {"b_ada": dmod, "norm_pre": jnp.stack([v[0] for v in vas]), "norm_post": jnp.stack([v[1] for v in vas]),
            "rel_bias": bias_grad(db.reshape(8, 128, WIN), "bias_grad")[:, :257], "conv_w": v_lru[0:4], "conv_b": v_lru[4],
            "lru_wa": _diag_blocks(dwa_bd), "lru_ba": v_lru[5], "lru_wx": _diag_blocks(dwx_bd), "lru_bx": v_lru[6],
            "lru_lambda": v_lru[7]}
    full_shapes = {"b_ada": (9 * D,), "norm_pre": (3, D), "norm_post": (3, D), "rel_bias": (8, 257),
                   "conv_w": (4, D), "conv_b": (D,), "lru_wa": (16, 64, 64), "lru_ba": (D,),
                   "lru_wx": (16, 64, 64), "lru_bx": (D,), "lru_lambda": (D,)}
    g3 = ag_small(_pack([part[n] for n in SMALL], 1232), "ag_small_grads")
    red = dict(zip(SMALL, _unpack(sum_lead(g3, "sum_small_grads"), [full_shapes[n] for n in SMALL])))
    cols = lambda a: lax.dynamic_slice(a, (0, p * 256), (a.shape[0], 256))
    grads.update({"b_ada": red["b_ada"][None], "norm_pre": cols(red["norm_pre"])[None],
                  "norm_post": cols(red["norm_post"])[None], "rel_bias": red["rel_bias"][None],
                  "conv_w": cols(red["conv_w"])[None], "conv_b": red["conv_b"][None], "lru_wa": red["lru_wa"][None],
                  "lru_ba": red["lru_ba"][None], "lru_wx": red["lru_wx"][None], "lru_bx": red["lru_bx"][None],
                  "lru_lambda": red["lru_lambda"][None]})

    dmod_all = g3[:, :72].reshape(8, 9 * D)
    dmod_cols = jnp.pad(lax.dynamic_slice(dmod_all, (0, p * 2304), (8, 2304)), ((0, 120), (0, 0)))
    c_all_t = jnp.pad(c_all.T, ((0, 0), (0, 120)))
    grads["w_ada"] = ada_bwd(c_all_t, dmod_cols, "ada_bwd")[None]

    delta, new_m, new_v = {}, {}, {}

    def update(n):
        shp = W[n].shape
        d_, m_, v_ = adamw(W[n][0], grads[n][0], M[n][0], V[n][0], "adamw_" + n)
        delta[n], new_m[n], new_v[n] = d_.reshape(shp), m_.reshape(shp), v_.reshape(shp)

    for n in ("w_ada", "ffn2_w_gu", "ffn2_w_down", "w_in", "w_att_o", "w_rec_o", "w_out"):
        update(n)
    packed = [_pack([src[n] for n in SMALL], 1168) for src in (W, grads, M, V)]
    outs = adamw(*packed, "adamw_small")
    for dst, blk in zip((delta, new_m, new_v), outs):
        for n, a in zip(SMALL, _unpack(blk, [W[n].shape for n in SMALL])):
            dst[n] = a
    reduce_end(("ffn1_w_gu", "ffn1_w_down"), fly_ffn1,
               [outs[0], delta["w_ada"], delta["ffn2_w_gu"], delta["ffn2_w_down"], delta["w_in"], delta["w_out"]], "ffn1")
    for n in ("ffn1_w_gu", "ffn1_w_down"):
        update(n)

    return (loss, dx0[None], *[grads[n] for n in WEIGHTS], *[delta[n] for n in WEIGHTS],
            *[new_m[n] for n in WEIGHTS], *[new_v[n] for n in WEIGHTS])
```

```python
import functools

import numpy as np
import jax
import jax.numpy as jnp
from jax import lax
from jax.experimental import pallas as pl
from jax.experimental.pallas import tpu as pltpu

F32 = jnp.float32
BF16 = jnp.bfloat16

D = 1024
FF = 2816
PW = 5632
HP = 128
CHUNK = 64
WIN = 640
TQ = 512
EPS = 1e-6
NEG = -1e30
LRU_C = 8.0
N_DEV = 8
VMEM_LIMIT = 56 * 1024 * 1024

ADAM_LR, ADAM_B1, ADAM_B2, ADAM_EPS, ADAM_WD, ADAM_STEP = 0.001, 0.9, 0.999, 1e-08, 0.01, 10

MESH = pl.DeviceIdType.MESH
ANY = pl.BlockSpec(memory_space=pl.ANY)


def _cp(*sem):
    return pltpu.CompilerParams(dimension_semantics=tuple(sem), vmem_limit_bytes=VMEM_LIMIT)


def _dot(a, b):
    return jnp.dot(a, b, preferred_element_type=F32)


def _dot_nt(a, b):
    return lax.dot_general(a, b, (((1,), (1,)), ((), ())), preferred_element_type=F32)


def _dot_tn(a, b):
    return lax.dot_general(a, b, (((0,), (0,)), ((), ())), preferred_element_type=F32)


def _mean(v):
    return jnp.mean(v, axis=-1, keepdims=True)


def _colsum(v):
    return jnp.sum(v, axis=0, keepdims=True)


def _sigmoid(v):
    return 0.5 * jnp.tanh(0.5 * v) + 0.5


_GK = 0.7978845608028654


def _gelu(v):
    t = jnp.tanh(_GK * (v + 0.044715 * v * v * v))
    return 0.5 * v * (1.0 + t)


def _pre_norm(xv, vec_ref):
    r = lax.rsqrt(_mean(xv * xv) + EPS)
    n = xv * r * vec_ref[0:1, :]
    return n * (1.0 + vec_ref[3:4, :]) + vec_ref[2:3, :]


def _pre_norm_bwd(dh, xv, dres, vec_ref, vacc_ref):
    r = lax.rsqrt(_mean(xv * xv) + EPS)
    xh = xv * r
    n = xh * vec_ref[0:1, :]
    vacc_ref[2:3, :] += _colsum(dh)
    vacc_ref[3:4, :] += _colsum(dh * n)
    dn = dh * (1.0 + vec_ref[3:4, :])
    vacc_ref[0:1, :] += _colsum(dn * xh)
    dxh = dn * vec_ref[0:1, :]
    return r * (dxh - xh * _mean(dxh * xh)) + dres


def _post_norm_bwd(dxo, fv, res, vec_ref, vacc_ref):
    rf = lax.rsqrt(_mean(fv * fv) + EPS)
    fh = fv * rf
    gp = vec_ref[1:2, :]
    vacc_ref[4:5, :] += _colsum(res * dxo * (fh * gp))
    dy = (res * vec_ref[4:5, :]) * dxo
    vacc_ref[1:2, :] += _colsum(dy * fh)
    dfn = dy * gp
    return rf * (dfn - fh * _mean(dfn * fh))


def ffn_fwd(x, vec, w_gu, w_u, w_dn, res, name, tgt=None, tm=1024, tf=512):
    S = x.shape[0]
    tm = min(tm, S)
    nt = S // tm
    nf = -(-FF // tf)
    tail = FF - tf * (nf - 1)
    head = tgt is not None

    def body(*refs):
        x_ref, vec_ref, wg_ref, wu_ref, wd_ref = refs[:5]
        if head:
            t_ref, xo_ref, h_ref, g_ref, u_ref, a_ref, f_ref, l_ref, hs, acc, lacc = refs[5:]
        else:
            xo_ref, h_ref, g_ref, u_ref, a_ref, f_ref, hs, acc = refs[5:]
        i, j = pl.program_id(0), pl.program_id(1)

        @pl.when(j == 0)
        def _():
            h = _pre_norm(x_ref[...], vec_ref).astype(BF16)
            hs[...] = h
            h_ref[...] = h
            acc[...] = jnp.zeros_like(acc)

        def chunk(w):
            h = hs[...]
            g = _dot(h, wg_ref[:, 0:w])
            u = _dot(h, wu_ref[:, 0:w])
            g_ref[:, 0:w] = g.astype(BF16)
            u_ref[:, 0:w] = u.astype(BF16)
            a = (g * _sigmoid(g) * u).astype(BF16)
            a_ref[:, 0:w] = a
            acc[...] += _dot(a, wd_ref[0:w, :])

        @pl.when(j < nf - 1)
        def _():
            chunk(tf)

        @pl.when(j == nf - 1)
        def _():
            chunk(tail)
            f = acc[...]
            f_ref[...] = f.astype(BF16)
            y = f * lax.rsqrt(_mean(f * f) + EPS) * vec_ref[1:2, :]
            xo = x_ref[...] + (res * vec_ref[4:5, :]) * y
            if head:
                @pl.when(i == 0)
                def _():
                    lacc[...] = jnp.zeros_like(lacc)

                d = xo - t_ref[...]
                xo_ref[...] = d * (1.0 / D)
                lacc[...] += _colsum(d * d)

                @pl.when(i == nt - 1)
                def _():
                    l_ref[...] = jnp.broadcast_to(0.5 * jnp.sum(lacc[...]) * (1.0 / D), (8, 128))
            else:
                xo_ref[...] = xo

    row = lambda i, j: (i, 0)
    col = lambda i, j: (i, j)
    once = dict(pipeline_mode=pl.Buffered(1)) if head else {}
    in_specs = [pl.BlockSpec((tm, D), row, **once), pl.BlockSpec((8, D), lambda i, j: (0, 0)),
                pl.BlockSpec((D, tf), lambda i, j: (0, j)), pl.BlockSpec((D, tf), lambda i, j: (0, j)),
                pl.BlockSpec((tf, D), lambda i, j: (j, 0))]
    out_specs = [pl.BlockSpec((tm, D), row), pl.BlockSpec((tm, D), row), pl.BlockSpec((tm, tf), col),
                 pl.BlockSpec((tm, tf), col), pl.BlockSpec((tm, tf), col), pl.BlockSpec((tm, D), row)]
    out_shape = [jax.ShapeDtypeStruct((S, D), F32), jax.ShapeDtypeStruct((S, D), BF16),
                 jax.ShapeDtypeStruct((S, FF), BF16), jax.ShapeDtypeStruct((S, FF), BF16),
                 jax.ShapeDtypeStruct((S, FF), BF16), jax.ShapeDtypeStruct((S, D), BF16)]
    scratch = [pltpu.VMEM((tm, D), BF16), pltpu.VMEM((tm, D), F32)]
    args = [x, vec, w_gu, w_u, w_dn]
    if head:
        in_specs.append(pl.BlockSpec((tm, D), row, **once))
        out_specs.append(pl.BlockSpec((8, 128), lambda i, j: (0, 0)))
        out_shape.append(jax.ShapeDtypeStruct((8, 128), F32))
        scratch.append(pltpu.VMEM((1, D), F32))
        args.append(tgt)
    return pl.pallas_call(
        body, name=name, grid=(nt, nf), in_specs=in_specs, out_specs=out_specs, out_shape=out_shape,
        scratch_shapes=scratch,
        compiler_params=_cp("arbitrary" if head else "parallel", "arbitrary"),
    )(*args)


def ffn_bwd(dxo, x, f, g, u, vec, w_gu, w_u, w_dn, res, name, tm=1024, tf=512):
    S = x.shape[0]
    tm = min(tm, S)
    nf = -(-FF // tf)
    tail = FF - tf * (nf - 1)

    def body(dxo_ref, x_ref, f_ref, g_ref, u_ref, vec_ref, wg_ref, wu_ref, wd_ref,
             dx_ref, df_ref, dgu_ref, vacc_ref, dfs, acc):
        i, j = pl.program_id(0), pl.program_id(1)

        @pl.when((i == 0) & (j == 0))
        def _():
            vacc_ref[...] = jnp.zeros_like(vacc_ref)

        @pl.when(j == 0)
        def _():
            df = _post_norm_bwd(dxo_ref[...], f_ref[...].astype(F32), res, vec_ref, vacc_ref).astype(BF16)
            dfs[...] = df
            df_ref[...] = df
            acc[...] = jnp.zeros_like(acc)

        def chunk(w):
            da = _dot_nt(dfs[...], wd_ref[0:w, :])
            gv, uv = g_ref[:, 0:w].astype(F32), u_ref[:, 0:w].astype(F32)
            sg = _sigmoid(gv)
            dg = (da * uv * (sg * (1.0 + gv * (1.0 - sg)))).astype(BF16)
            du = (da * (gv * sg)).astype(BF16)
            dgu_ref[0, :, 0:w] = dg
            dgu_ref[1, :, 0:w] = du
            acc[...] += _dot_nt(dg, wg_ref[:, 0:w]) + _dot_nt(du, wu_ref[:, 0:w])

        @pl.when(j < nf - 1)
        def _():
            chunk(tf)

        @pl.when(j == nf - 1)
        def _():
            chunk(tail)
            dx_ref[...] = _pre_norm_bwd(acc[...], x_ref[...], dxo_ref[...], vec_ref, vacc_ref)

    row = lambda i, j: (i, 0)
    col = lambda i, j: (i, j)
    return pl.pallas_call(
        body, name=name, grid=(S // tm, nf),
        in_specs=[pl.BlockSpec((tm, D), row, pipeline_mode=pl.Buffered(1)) for _ in range(3)]
        + [pl.BlockSpec((tm, tf), col), pl.BlockSpec((tm, tf), col),
                  pl.BlockSpec((8, D), lambda i, j: (0, 0)),
                  pl.BlockSpec((D, tf), lambda i, j: (0, j)), pl.BlockSpec((D, tf), lambda i, j: (0, j)),
                  pl.BlockSpec((tf, D), lambda i, j: (j, 0))],
        out_specs=[pl.BlockSpec((tm, D), row), pl.BlockSpec((tm, D), row),
                   pl.BlockSpec((2, tm, tf), lambda i, j: (0, i, j)),
                   pl.BlockSpec((8, D), lambda i, j: (0, 0))],
        out_shape=[jax.ShapeDtypeStruct((S, D), F32), jax.ShapeDtypeStruct((S, D), BF16),
                   jax.ShapeDtypeStruct((2, S, FF), BF16), jax.ShapeDtypeStruct((8, D), F32)],
        scratch_shapes=[pltpu.VMEM((tm, D), BF16), pltpu.VMEM((tm, D), F32)],
        compiler_params=_cp("arbitrary", "arbitrary"),
    )(dxo, x, f, g, u, vec, w_gu, w_u, w_dn)


def mm_tn(a, b, name, tm, tn, tk, out_dtype=BF16):
    S, M = a.shape
    if b.ndim == 3:
        G, _, Nf = b.shape
    else:
        G, Nf = 1, b.shape[1]
    N = G * Nf
    tk = min(tk, S)
    nbf = Nf // tn
    nk = S // tk

    def body(a_ref, b_ref, o_ref, acc):
        k = pl.program_id(2)

        @pl.when(k == 0)
        def _():
            acc[...] = jnp.zeros_like(acc)

        acc[...] += _dot_tn(a_ref[...], b_ref[...])

        @pl.when(k == nk - 1)
        def _():
            o_ref[...] = acc[...].astype(out_dtype)

    if b.ndim == 3:
        b_spec = pl.BlockSpec((None, tk, tn), lambda i, j, k: (j // nbf, k, j % nbf))
    else:
        b_spec = pl.BlockSpec((tk, tn), lambda i, j, k: (k, j))
    return pl.pallas_call(
        body, name=name, grid=(M // tm, N // tn, nk),
        in_specs=[pl.BlockSpec((tk, tm), lambda i, j, k: (k, i)), b_spec],
        out_specs=pl.BlockSpec((tm, tn), lambda i, j, k: (i, j)),
        out_shape=jax.ShapeDtypeStruct((M, N), out_dtype),
        scratch_shapes=[pltpu.VMEM((tm, tn), F32)],
        compiler_params=_cp("parallel", "parallel", "arbitrary"),
    )(a, b)


def proj_fwd(x, vec, w_in, name, tm=1024, tn=512):
    S = x.shape[0]
    tm = min(tm, S)
    nq = 1536 // tn

    def body(x_ref, vec_ref, w_ref, h_ref, qkv_ref, rest_ref, hs):
        j = pl.program_id(1)

        @pl.when(j == 0)
        def _():
            h = _pre_norm(x_ref[...], vec_ref).astype(BF16)
            hs[...] = h
            h_ref[...] = h

        r = _dot(hs[...], w_ref[...])

        @pl.when(j < nq)
        def _():
            qkv_ref[...] = r.astype(BF16)

        @pl.when(j >= nq)
        def _():
            rest_ref[...] = r.astype(BF16)

    row = lambda i, j: (i, 0)
    return pl.pallas_call(
        body, name=name, grid=(S // tm, PW // tn),
        in_specs=[pl.BlockSpec((tm, D), row), pl.BlockSpec((8, D), lambda i, j: (0, 0)),
                  pl.BlockSpec((D, tn), lambda i, j: (0, j))],
        out_specs=[pl.BlockSpec((tm, D), row),
                   pl.BlockSpec((tm, tn), lambda i, j: (i, jnp.minimum(j, nq - 1))),
                   pl.BlockSpec((tm, tn), lambda i, j: (i, jnp.maximum(j - nq, 0)))],
        out_shape=[jax.ShapeDtypeStruct((S, D), BF16), jax.ShapeDtypeStruct((S, 1536), BF16),
                   jax.ShapeDtypeStruct((S, 4096), BF16)],
        scratch_shapes=[pltpu.VMEM((tm, D), BF16)],
        compiler_params=_cp("parallel", "arbitrary"),
    )(x, vec, w_in)


def proj_bwd(dq, dkv, dxr, d3, w_in, x, dxo, vec, name, tm=1024, tk=512):
    S = x.shape[0]
    tm = min(tm, S)
    nk = PW // tk

    def body(dq_ref, dkv_ref, dxr_ref, d3_ref, w_ref, x_ref, dxo_ref, vec_ref, dx_ref, vacc_ref, acc):
        i, j = pl.program_id(0), pl.program_id(1)

        @pl.when((i == 0) & (j == 0))
        def _():
            vacc_ref[...] = jnp.zeros_like(vacc_ref)

        @pl.when(j == 0)
        def _():
            acc[...] = _dot_nt(dq_ref[...], w_ref[...])

        @pl.when((j >= 1) & (j < 3))
        def _():
            acc[...] += _dot_nt(dkv_ref[...], w_ref[...])

        @pl.when((j >= 3) & (j < 5))
        def _():
            acc[...] += _dot_nt(dxr_ref[...], w_ref[...])

        @pl.when(j >= 5)
        def _():
            acc[...] += _dot_nt(d3_ref[...], w_ref[...])

        @pl.when(j == nk - 1)
        def _():
            dx_ref[...] = _pre_norm_bwd(acc[...], x_ref[...], dxo_ref[...], vec_ref, vacc_ref)

    row = lambda i, j: (i, 0)
    return pl.pallas_call(
        body, name=name, grid=(S // tm, nk),
        in_specs=[pl.BlockSpec((None, tm, tk), lambda i, j: (0, i, 0)),
                  pl.BlockSpec((None, tm, tk), lambda i, j: (jnp.clip(j - 1, 0, 1), i, 0)),
                  pl.BlockSpec((tm, tk), lambda i, j: (i, jnp.clip(j - 3, 0, 1))),
                  pl.BlockSpec((None, tm, tk), lambda i, j: (jnp.clip(j - 5, 0, 5) // 2, i, jnp.clip(j - 5, 0, 5) % 2)),
                  pl.BlockSpec((D, tk), lambda i, j: (0, j)),
                  pl.BlockSpec((tm, D), row), pl.BlockSpec((tm, D), row),
                  pl.BlockSpec((8, D), lambda i, j: (0, 0))],
        out_specs=[pl.BlockSpec((tm, D), row), pl.BlockSpec((8, D), lambda i, j: (0, 0))],
        out_shape=[jax.ShapeDtypeStruct((S, D), F32), jax.ShapeDtypeStruct((8, D), F32)],
        scratch_shapes=[pltpu.VMEM((tm, D), F32)],
        compiler_params=_cp("arbitrary", "arbitrary"),
    )(dq, dkv, dxr, d3, w_in, x, dxo, vec)


def _two_heads(v, lane):
    zero = jnp.zeros((), v.dtype)
    return jnp.concatenate([jnp.where(lane < 64, v, zero), jnp.where(lane >= 64, v, zero)], axis=0)


def _attn_probs(qm, ka, bias_h, i, grp):
    s = _dot_nt(qm, ka) + bias_h
    col = lax.broadcasted_iota(jnp.int32, s.shape, 1)
    first_key = jnp.where(i == 0, 512 - 128 * grp, 0)
    s = jnp.where(col >= first_key, s, NEG)
    e = jnp.exp(s - jnp.max(s, axis=-1, keepdims=True))
    return e * (1.0 / jnp.sum(e, axis=-1, keepdims=True))


def attn_fwd(qkv, bias, name):
    S = qkv.shape[0]
    nb = S // TQ

    def body(q_ref, kp_ref, kc_ref, vp_ref, vc_ref, b_ref, o_ref, kw, vw):
        i = pl.program_id(1)
        kw[0:TQ, :] = kp_ref[...]
        kw[TQ:2 * TQ, :] = kc_ref[...]
        vw[0:TQ, :] = vp_ref[...]
        vw[TQ:2 * TQ, :] = vc_ref[...]
        lane = lax.broadcasted_iota(jnp.int32, (1, HP), 1)

        def group(a, carry):
            r0 = pl.multiple_of(a * 128, 128)
            qa = q_ref[pl.ds(r0, 128), :] * jnp.asarray(0.125, BF16)
            ka = kw[pl.ds(r0, WIN), :]
            va = vw[pl.ds(r0, WIN), :]
            p = _attn_probs(_two_heads(qa, lane), ka, b_ref[...], i, a)
            o2 = _dot(p.astype(BF16), va)
            o_ref[pl.ds(r0, 128), :] = jnp.where(lane < 64, o2[0:128], o2[128:256]).astype(BF16)
            return carry

        lax.fori_loop(0, TQ // 128, group, 0, unroll=True)

    prev = lambda h, i: (jnp.maximum(i - 1, 0), 0)
    return pl.pallas_call(
        body, name=name, grid=(4, nb),
        in_specs=[pl.BlockSpec((TQ, HP), lambda h, i: (i, h)),
                  pl.BlockSpec((TQ, HP), lambda h, i: (jnp.maximum(i - 1, 0), 4 + h)),
                  pl.BlockSpec((TQ, HP), lambda h, i: (i, 4 + h)),
                  pl.BlockSpec((TQ, HP), lambda h, i: (jnp.maximum(i - 1, 0), 8 + h)),
                  pl.BlockSpec((TQ, HP), lambda h, i: (i, 8 + h)),
                  pl.BlockSpec((None, 256, WIN), lambda h, i: (h, 0, 0))],
        out_specs=pl.BlockSpec((TQ, HP), lambda h, i: (i, h)),
        out_shape=jax.ShapeDtypeStruct((S, 512), BF16),
        scratch_shapes=[pltpu.VMEM((2 * TQ, HP), BF16), pltpu.VMEM((2 * TQ, HP), BF16)],
        compiler_params=_cp("parallel", "arbitrary"),
    )(qkv, qkv, qkv, qkv, qkv, bias)


def attn_bwd(qkv, do, bias, name):
    S = qkv.shape[0]
    nb = S // TQ

    def body(q_ref, kp_ref, kc_ref, vp_ref, vc_ref, do_ref, b_ref, dqkv_ref, db_ref, dkv_ref, kw, vw, ak, av):
        i = pl.program_id(1)

        @pl.when(i == 0)
        def _():
            db_ref[...] = jnp.zeros_like(db_ref)
            ak[...] = jnp.zeros_like(ak)
            av[...] = jnp.zeros_like(av)

        @pl.when(i > 0)
        def _():
            ak[0:TQ, :] = ak[TQ:2 * TQ, :]
            av[0:TQ, :] = av[TQ:2 * TQ, :]
            ak[TQ:2 * TQ, :] = jnp.zeros((TQ, HP), F32)
            av[TQ:2 * TQ, :] = jnp.zeros((TQ, HP), F32)

        @pl.when(i < nb)
        def _():
            kw[0:TQ, :] = kp_ref[...]
            kw[TQ:2 * TQ, :] = kc_ref[...]
            vw[0:TQ, :] = vp_ref[...]
            vw[TQ:2 * TQ, :] = vc_ref[...]
            lane = lax.broadcasted_iota(jnp.int32, (1, HP), 1)

            def group(a, carry):
                r0 = pl.multiple_of(a * 128, 128)
                q2 = _two_heads(q_ref[pl.ds(r0, 128), :] * jnp.asarray(0.125, BF16), lane)
                do2 = _two_heads(do_ref[pl.ds(r0, 128), :], lane)
                ka = kw[pl.ds(r0, WIN), :]
                va = vw[pl.ds(r0, WIN), :]
                p = _attn_probs(q2, ka, b_ref[...], i, a)
                dp = _dot_nt(do2, va)
                ds = p * (dp - jnp.sum(p * dp, axis=-1, keepdims=True))
                db_ref[...] += ds
                dsb = ds.astype(BF16)
                dq2 = _dot(dsb, ka)
                ak[pl.ds(r0, WIN), :] += _dot_tn(dsb, q2)
                av[pl.ds(r0, WIN), :] += _dot_tn(p.astype(BF16), do2)
                dq = jnp.where(lane < 64, dq2[0:128], dq2[128:256])
                dqkv_ref[0, pl.ds(r0, 128), :] = (dq * 0.125).astype(BF16)
                return carry

            lax.fori_loop(0, TQ // 128, group, 0, unroll=True)

        @pl.when(i > 0)
        def _():
            dkv_ref[0] = ak[0:TQ, :].astype(BF16)
            dkv_ref[1] = av[0:TQ, :].astype(BF16)

    cur = lambda i: jnp.minimum(i, nb - 1)
    prv = lambda i: jnp.clip(i - 1, 0, nb - 1)
    dq, db, dkv = pl.pallas_call(
        body, name=name, grid=(4, nb + 1),
        in_specs=[pl.BlockSpec((TQ, HP), lambda h, i: (cur(i), h)),
                  pl.BlockSpec((TQ, HP), lambda h, i: (prv(i), 4 + h)),
                  pl.BlockSpec((TQ, HP), lambda h, i: (cur(i), 4 + h)),
                  pl.BlockSpec((TQ, HP), lambda h, i: (prv(i), 8 + h)),
                  pl.BlockSpec((TQ, HP), lambda h, i: (cur(i), 8 + h)),
                  pl.BlockSpec((TQ, HP), lambda h, i: (cur(i), h)),
                  pl.BlockSpec((None, 256, WIN), lambda h, i: (h, 0, 0))],
        out_specs=[pl.BlockSpec((1, TQ, HP), lambda h, i: (0, cur(i), h)),
                   pl.BlockSpec((None, 256, WIN), lambda h, i: (h, 0, 0)),
                   pl.BlockSpec((2, TQ, HP), lambda h, i: (0, prv(i), h))],
        out_shape=[jax.ShapeDtypeStruct((1, S, 512), BF16), jax.ShapeDtypeStruct((4, 256, WIN), F32),
                   jax.ShapeDtypeStruct((2, S, 512), BF16)],
        scratch_shapes=[pltpu.VMEM((2 * TQ, HP), BF16), pltpu.VMEM((2 * TQ, HP), BF16),
                        pltpu.VMEM((2 * TQ, HP), F32), pltpu.VMEM((2 * TQ, HP), F32)],
        compiler_params=_cp("parallel", "arbitrary"),
    )(qkv, qkv, qkv, qkv, qkv, do, bias)
    return dq, db, dkv


def bias_grad(db, name):
    def body(db_ref, o_ref):
        r = lax.broadcasted_iota(jnp.int32, (128, 128), 0)
        c = lax.broadcasted_iota(jnp.int32, (128, 128), 1)
        flip = (r + c == 127).astype(BF16)
        lane = lax.broadcasted_iota(jnp.int32, (16, 384), 1)
        src = lax.broadcasted_iota(jnp.int32, (128, 384), 0)
        dst = lax.broadcasted_iota(jnp.int32, (128, 384), 1)

        def split_dot(v, m):
            hi = v.astype(BF16)
            r1 = v - hi.astype(F32)
            mid = r1.astype(BF16)
            lo = (r1 - mid.astype(F32)).astype(BF16)
            return _dot(hi, m) + _dot(mid, m) + _dot(lo, m)

        def diag_sums(w):
            y = pltpu.roll(split_dot(w, flip), 0, 1, stride=1, stride_axis=0)
            return jnp.broadcast_to(_colsum(y), (16, 128))

        w4 = db_ref[0, :, 512:640]
        w3 = db_ref[0, :, 384:512]
        far = jnp.sum(db_ref[0, :, 0:384]) + jnp.sum(jnp.where(r >= c, w3, 0.0))
        lo4 = diag_sums(jnp.where(r >= c, w4, 0.0))
        up4 = diag_sums(jnp.where(r < c, w4, 0.0))
        up3 = diag_sums(jnp.where(r < c, w3, 0.0))
        p_lo4 = (dst == 128 + (src + 1) % 128).astype(BF16)
        p_up4 = ((dst == src + 1) & (src < 127)).astype(BF16)
        p_up3 = ((dst == src + 129) & (src < 127)).astype(BF16)
        out = split_dot(lo4, p_lo4) + split_dot(up4, p_up4) + split_dot(up3, p_up3)
        o_ref[0] = out + jnp.where(lane == 256, far, 0.0)

    return pl.pallas_call(
        body, name=name, grid=(8,),
        in_specs=[pl.BlockSpec((1, 128, WIN), lambda h: (h, 0, 0))],
        out_specs=pl.BlockSpec((1, 16, 384), lambda h: (h, 0, 0)),
        out_shape=jax.ShapeDtypeStruct((8, 16, 384), F32),
        compiler_params=_cp("parallel"),
    )(db)[:, 0, :]


LT = 256
LC = 512


def _lru_gates(xs, pv_ref, wa_ref, wx_ref, tl):
    xc = (pv_ref[4:5, :] + pv_ref[3:4, :] * xs[pl.ds(8, tl), :] + pv_ref[2:3, :] * xs[pl.ds(7, tl), :]
          + pv_ref[1:2, :] * xs[pl.ds(6, tl), :] + pv_ref[0:1, :] * xs[pl.ds(5, tl), :])
    xcb = xc.astype(BF16)
    pa = jnp.concatenate([_dot(xcb[:, 0:256], wa_ref[0]), _dot(xcb[:, 256:512], wa_ref[1])], axis=1)
    px = jnp.concatenate([_dot(xcb[:, 0:256], wx_ref[0]), _dot(xcb[:, 256:512], wx_ref[1])], axis=1)
    r = _sigmoid(pa + pv_ref[5:6, :])
    ig = _sigmoid(px + pv_ref[6:7, :])
    z = -pv_ref[7:8, :]
    sp = jnp.maximum(z, 0.0) + jnp.log1p(jnp.exp(-jnp.abs(z)))
    log_a = (-LRU_C * r) * sp
    a = jnp.exp(log_a)
    s = jnp.tanh(-log_a) * (1.0 + a * a)
    inv_mult = lax.rsqrt(s)
    mult = jnp.where(s > 0.0, s * inv_mult, 0.0)
    return xc, xcb, r, ig, sp, a, mult, inv_mult


def lru_fwd(rest, pvec, wa, wx, name):
    S = rest.shape[0]
    tl = min(LT, S)
    nt = S // tl

    def body(xr_ref, halo_ref, yr_ref, pv_ref, wa_ref, wx_ref, h_ref, hg_ref, xs, a_s, u_s, h_s, carry):
        ti = pl.program_id(1)

        @pl.when(ti == 0)
        def _():
            carry[...] = jnp.zeros_like(carry)

        xs[0:8, :] = jnp.where(ti > 0, halo_ref[8:16, :].astype(F32), 0.0)
        xs[pl.ds(8, tl), :] = xr_ref[...].astype(F32)
        xc, _, _, ig, _, a, mult, _ = _lru_gates(xs, pv_ref, wa_ref, wx_ref, tl)
        a_s[...] = a
        u_s[...] = mult * (ig * xc)
        row = lax.broadcasted_iota(jnp.int32, (8, LC), 0)

        def blk(bi, c):
            o = pl.multiple_of(bi * 8, 8)
            av = a_s[pl.ds(o, 8), :]
            bv = u_s[pl.ds(o, 8), :]
            for d in (1, 2, 4):
                a_sh = pltpu.roll(av, d, 0)
                b_sh = pltpu.roll(bv, d, 0)
                m = row >= d
                bv = jnp.where(m, av * b_sh + bv, bv)
                av = jnp.where(m, av * a_sh, av)
            hv = bv + av * c
            h_s[pl.ds(o, 8), :] = hv
            return hv[7:8, :]

        carry[...] = lax.fori_loop(0, tl // 8, blk, carry[...])
        h = h_s[...]
        h_ref[...] = h
        hg_ref[...] = (h * _gelu(yr_ref[...].astype(F32))).astype(BF16)

    hb = tl // 16
    return pl.pallas_call(
        body, name=name, grid=(2, nt),
        in_specs=[pl.BlockSpec((tl, LC), lambda c, t: (t, c)),
                  pl.BlockSpec((16, LC), lambda c, t: (jnp.maximum(t * hb - 1, 0), c)),
                  pl.BlockSpec((tl, LC), lambda c, t: (t, 2 + c)),
                  pl.BlockSpec((8, LC), lambda c, t: (0, c)),
                  pl.BlockSpec((2, 256, 256), lambda c, t: (c, 0, 0)),
                  pl.BlockSpec((2, 256, 256), lambda c, t: (c, 0, 0))],
        out_specs=[pl.BlockSpec((tl, LC), lambda c, t: (t, c)), pl.BlockSpec((tl, LC), lambda c, t: (t, c))],
        out_shape=[jax.ShapeDtypeStruct((S, D), F32), jax.ShapeDtypeStruct((S, D), BF16)],
        scratch_shapes=[pltpu.VMEM((tl + 8, LC), F32), pltpu.VMEM((tl, LC), F32), pltpu.VMEM((tl, LC), F32),
                        pltpu.VMEM((tl, LC), F32), pltpu.VMEM((1, LC), F32)],
        compiler_params=_cp("parallel", "arbitrary"),
    )(rest, rest, rest, pvec, wa, wx)


def lru_bwd(dh, h, rest, pvec, wa, wx, name):
    S = rest.shape[0]
    tl = min(LT, S)
    nt = S // tl

    def body(dh_ref, h_ref, hhalo_ref, xr_ref, xhalo_ref, pv_ref, wa_ref, wx_ref,
             dxr_ref, vacc_ref, dwa_ref, dwx_ref,
             xs, hs, a_s, ash_s, b_s, lam_s, dxe, anext, lnext, dxnext):
        ti = pl.program_id(1)
        tr = nt - 1 - ti

        @pl.when(ti == 0)
        def _():
            anext[...] = jnp.zeros_like(anext)
            lnext[...] = jnp.zeros_like(lnext)
            dxnext[...] = jnp.zeros_like(dxnext)
            vacc_ref[...] = jnp.zeros_like(vacc_ref)
            dwa_ref[...] = jnp.zeros_like(dwa_ref)
            dwx_ref[...] = jnp.zeros_like(dwx_ref)

        xs[0:8, :] = jnp.where(tr > 0, xhalo_ref[8:16, :].astype(F32), 0.0)
        xs[pl.ds(8, tl), :] = xr_ref[...].astype(F32)
        xc, xcb, r, ig, sp, a, mult, inv_mult = _lru_gates(xs, pv_ref, wa_ref, wx_ref, tl)

        a_s[pl.ds(0, tl), :] = a
        a_s[pl.ds(tl, 8), :] = jnp.broadcast_to(anext[...], (8, LC))
        ash_s[...] = a_s[pl.ds(1, tl), :]
        b_s[...] = dh_ref[...]
        row = lax.broadcasted_iota(jnp.int32, (8, LC), 0)

        def blk(k, c):
            o = pl.multiple_of((tl // 8 - 1 - k) * 8, 8)
            av = ash_s[pl.ds(o, 8), :]
            bv = b_s[pl.ds(o, 8), :]
            for d in (1, 2, 4):
                a_sh = pltpu.roll(av, 8 - d, 0)
                b_sh = pltpu.roll(bv, 8 - d, 0)
                m = row < 8 - d
                bv = jnp.where(m, bv + av * b_sh, bv)
                av = jnp.where(m, av * a_sh, av)
            lv = bv + av * c
            lam_s[pl.ds(o, 8), :] = lv
            return lv[0:1, :]

        lnext[...] = lax.fori_loop(0, tl // 8, blk, lnext[...])
        anext[...] = a[0:1, :]
        lam = lam_s[...]

        hs[0:8, :] = jnp.where(tr > 0, hhalo_ref[...], 0.0)
        hs[pl.ds(8, tl), :] = h_ref[...]
        d_a = lam * hs[pl.ds(7, tl), :]
        d_mult = lam * (ig * xc)
        d_ig = lam * mult * xc
        dxc = lam * mult * ig
        d_log_a = d_a * a - d_mult * (a * a) * inv_mult
        d_r = d_log_a * (-LRU_C * sp)
        vacc_ref[7:8, :] += _colsum(d_log_a * (-LRU_C * r)) * (-_sigmoid(-pv_ref[7:8, :]))
        d_pa = d_r * r * (1.0 - r)
        d_px = d_ig * ig * (1.0 - ig)
        vacc_ref[5:6, :] += _colsum(d_pa)
        vacc_ref[6:7, :] += _colsum(d_px)
        dpa = d_pa.astype(BF16)
        dpx = d_px.astype(BF16)
        back = []
        for g in range(2):
            sl = slice(256 * g, 256 * g + 256)
            dwa_ref[g] += _dot_tn(xcb[:, sl], dpa[:, sl])
            dwx_ref[g] += _dot_tn(xcb[:, sl], dpx[:, sl])
            back.append(_dot_nt(dpa[:, sl], wa_ref[g]) + _dot_nt(dpx[:, sl], wx_ref[g]))
        dxc = dxc + jnp.concatenate(back, axis=1)
        vacc_ref[4:5, :] += _colsum(dxc)
        for k in range(4):
            vacc_ref[k:k + 1, :] += _colsum(dxc * xs[pl.ds(5 + k, tl), :])
        dxe[pl.ds(0, tl), :] = dxc
        dxe[pl.ds(tl, 8), :] = dxnext[...]
        dxr = (pv_ref[3:4, :] * dxc + pv_ref[2:3, :] * dxe[pl.ds(1, tl), :]
               + pv_ref[1:2, :] * dxe[pl.ds(2, tl), :] + pv_ref[0:1, :] * dxe[pl.ds(3, tl), :])
        dxr_ref[...] = dxr.astype(BF16)
        dxnext[...] = dxc[0:8, :]

    hb = tl // 8
    rev = lambda t: nt - 1 - t
    halo = lambda t: jnp.maximum(rev(t) * hb - 1, 0)
    big = lambda: pltpu.VMEM((tl + 8, LC), F32)
    til = lambda: pltpu.VMEM((tl, LC), F32)
    return pl.pallas_call(
        body, name=name, grid=(2, nt),
        in_specs=[pl.BlockSpec((tl, LC), lambda c, t: (rev(t), c)),
                  pl.BlockSpec((tl, LC), lambda c, t: (rev(t), c)),
                  pl.BlockSpec((8, LC), lambda c, t: (halo(t), c)),
                  pl.BlockSpec((tl, LC), lambda c, t: (rev(t), c)),
                  pl.BlockSpec((16, LC), lambda c, t: (jnp.maximum(rev(t) * (tl // 16) - 1, 0), c)),
                  pl.BlockSpec((8, LC), lambda c, t: (0, c)),
                  pl.BlockSpec((2, 256, 256), lambda c, t: (c, 0, 0)),
                  pl.BlockSpec((2, 256, 256), lambda c, t: (c, 0, 0))],
        out_specs=[pl.BlockSpec((tl, LC), lambda c, t: (rev(t), c)),
                   pl.BlockSpec((8, LC), lambda c, t: (0, c)),
                   pl.BlockSpec((2, 256, 256), lambda c, t: (c, 0, 0)),
                   pl.BlockSpec((2, 256, 256), lambda c, t: (c, 0, 0))],
        out_shape=[jax.ShapeDtypeStruct((S, D), BF16), jax.ShapeDtypeStruct((8, D), F32),
                   jax.ShapeDtypeStruct((4, 256, 256), F32), jax.ShapeDtypeStruct((4, 256, 256), F32)],
        scratch_shapes=[big(), big(), big(), til(), til(), til(), big(),
                        pltpu.VMEM((1, LC), F32), pltpu.VMEM((1, LC), F32), pltpu.VMEM((8, LC), F32)],
        compiler_params=_cp("parallel", "arbitrary"),
    )(dh, h, h, rest, rest, pvec, wa, wx)


def mix_out_fwd(x, ao, hg, rest, vec, w_att_o, w_rec_o, w_out, name, tm=256):
    S = x.shape[0]
    tm = min(tm, S)

    def body(x_ref, ao_ref, hg_ref, ga_ref, gr_ref, vec_ref, wa_ref, wr_ref, wo_ref,
             xo_ref, att_ref, rec_ref, mg_ref, f_ref):
        att = _dot(ao_ref[...], wa_ref[...])
        rec = _dot(hg_ref[...], wr_ref[...])
        att_ref[...] = att.astype(BF16)
        rec_ref[...] = rec.astype(BF16)
        mg = (_sigmoid(ga_ref[...].astype(F32)) * att + _sigmoid(gr_ref[...].astype(F32)) * rec).astype(BF16)
        mg_ref[...] = mg
        f = _dot(mg, wo_ref[...])
        f_ref[...] = f.astype(BF16)
        y = f * lax.rsqrt(_mean(f * f) + EPS) * vec_ref[1:2, :]
        xo_ref[...] = x_ref[...] + (1.0 * vec_ref[4:5, :]) * y

    row = lambda i: (i, 0)
    full = lambda r: pl.BlockSpec((r, D), lambda i: (0, 0))
    return pl.pallas_call(
        body, name=name, grid=(S // tm,),
        in_specs=[pl.BlockSpec((tm, D), row), pl.BlockSpec((tm, 512), row), pl.BlockSpec((tm, D), row),
                  pl.BlockSpec((tm, D), lambda i: (i, 2)), pl.BlockSpec((tm, D), lambda i: (i, 3)),
                  full(8), full(512), full(D), full(D)],
        out_specs=[pl.BlockSpec((tm, D), row)] * 5,
        out_shape=[jax.ShapeDtypeStruct((S, D), F32)] + [jax.ShapeDtypeStruct((S, D), BF16)] * 4,
        compiler_params=_cp("parallel"),
    )(x, ao, hg, rest, rest, vec, w_att_o, w_rec_o, w_out)


def mix_out_bwd(dxo, f, att, rec, rest, h, vec, w_att_o, w_rec_o, w_out, name, tm=256):
    S = dxo.shape[0]
    tm = min(tm, S)

    def body(dxo_ref, f_ref, att_ref, rec_ref, yr_ref, ga_ref, gr_ref, h_ref, vec_ref, wa_ref, wr_ref, wo_ref,
             df_ref, da_ref, dr_ref, dao_ref, dh_ref, d3_ref, vacc_ref):
        @pl.when(pl.program_id(0) == 0)
        def _():
            vacc_ref[...] = jnp.zeros_like(vacc_ref)

        df = _post_norm_bwd(dxo_ref[...], f_ref[...].astype(F32), 1.0, vec_ref, vacc_ref).astype(BF16)
        df_ref[...] = df
        dm = _dot_nt(df, wo_ref[...])
        sa = _sigmoid(ga_ref[...].astype(F32))
        sr = _sigmoid(gr_ref[...].astype(F32))
        d_att = (dm * sa).astype(BF16)
        d_rec = (dm * sr).astype(BF16)
        da_ref[...] = d_att
        dr_ref[...] = d_rec
        d3_ref[1] = (dm * att_ref[...].astype(F32) * (sa * (1.0 - sa))).astype(BF16)
        d3_ref[2] = (dm * rec_ref[...].astype(F32) * (sr * (1.0 - sr))).astype(BF16)
        dao_ref[...] = _dot_nt(d_att, wa_ref[...]).astype(BF16)
        d_hg = _dot_nt(d_rec, wr_ref[...])
        yr = yr_ref[...].astype(F32)
        t = jnp.tanh(_GK * (yr + 0.044715 * yr * yr * yr))
        dh_ref[...] = d_hg * (0.5 * yr * (1.0 + t))
        gelu_grad = 0.5 * (1.0 + t) + 0.5 * yr * (1.0 - t * t) * _GK * (1.0 + 3.0 * 0.044715 * yr * yr)
        d3_ref[0] = (d_hg * h_ref[...] * gelu_grad).astype(BF16)

    row = lambda i: (i, 0)
    full = lambda r: pl.BlockSpec((r, D), lambda i: (0, 0))
    return pl.pallas_call(
        body, name=name, grid=(S // tm,),
        in_specs=[pl.BlockSpec((tm, D), row)] * 4
        + [pl.BlockSpec((tm, D), lambda i: (i, 1)), pl.BlockSpec((tm, D), lambda i: (i, 2)),
           pl.BlockSpec((tm, D), lambda i: (i, 3)), pl.BlockSpec((tm, D), row),
           full(8), full(512), full(D), full(D)],
        out_specs=[pl.BlockSpec((tm, D), row)] * 3
        + [pl.BlockSpec((tm, 512), row), pl.BlockSpec((tm, D), row),
           pl.BlockSpec((3, tm, D), lambda i: (0, i, 0)), pl.BlockSpec((8, D), lambda i: (0, 0))],
        out_shape=[jax.ShapeDtypeStruct((S, D), BF16)] * 3
        + [jax.ShapeDtypeStruct((S, 512), BF16), jax.ShapeDtypeStruct((S, D), F32),
           jax.ShapeDtypeStruct((3, S, D), BF16), jax.ShapeDtypeStruct((8, D), F32)],
        compiler_params=_cp("arbitrary"),
    )(dxo, f, att, rec, rest, rest, rest, h, vec, w_att_o, w_rec_o, w_out)


def dw_in(h, dq, dkv, dxr, d3, name, tk=1024, tn=512):
    S = h.shape[0]
    tk = min(tk, S)
    nk = S // tk

    def body(h_ref, dq_ref, dkv_ref, dxr_ref, d3_ref, o_ref, acc):
        j, k = pl.program_id(0), pl.program_id(1)

        @pl.when(k == 0)
        def _():
            acc[...] = jnp.zeros_like(acc)

        @pl.when(j == 0)
        def _():
            acc[...] += _dot_tn(h_ref[...], dq_ref[...])

        @pl.when((j >= 1) & (j < 3))
        def _():
            acc[...] += _dot_tn(h_ref[...], dkv_ref[...])

        @pl.when((j >= 3) & (j < 5))
        def _():
            acc[...] += _dot_tn(h_ref[...], dxr_ref[...])

        @pl.when(j >= 5)
        def _():
            acc[...] += _dot_tn(h_ref[...], d3_ref[...])

        @pl.when(k == nk - 1)
        def _():
            o_ref[...] = acc[...].astype(BF16)

    use = lambda j, k, lo, hi: jnp.where((j >= lo) & (j < hi), k, 0)
    g3 = lambda j: jnp.clip(j - 5, 0, 5)
    return pl.pallas_call(
        body, name=name, grid=(PW // tn, nk),
        in_specs=[pl.BlockSpec((tk, D), lambda j, k: (k, 0)),
                  pl.BlockSpec((None, tk, tn), lambda j, k: (0, use(j, k, 0, 1), 0)),
                  pl.BlockSpec((None, tk, tn), lambda j, k: (jnp.clip(j - 1, 0, 1), use(j, k, 1, 3), 0)),
                  pl.BlockSpec((tk, tn), lambda j, k: (use(j, k, 3, 5), jnp.clip(j - 3, 0, 1))),
                  pl.BlockSpec((None, tk, tn), lambda j, k: (g3(j) // 2, use(j, k, 5, 11), g3(j) % 2))],
        out_specs=pl.BlockSpec((D, tn), lambda j, k: (0, j)),
        out_shape=jax.ShapeDtypeStruct((D, PW), BF16),
        scratch_shapes=[pltpu.VMEM((D, tn), F32)],
        compiler_params=_cp("parallel", "arbitrary"),
    )(h, dq, dkv, dxr, d3)


def ada_fwd(c_all, w_ada, b_ada, name, tn=768):
    n = w_ada.shape[1]

    def body(c_ref, w_ref, b_ref, o_ref):
        cv = c_ref[...]
        ca = (cv * _sigmoid(cv)).astype(BF16)
        o_ref[...] = _dot(ca, w_ref[...].astype(BF16)) + b_ref[...]

    return pl.pallas_call(
        body, name=name, grid=(n // tn,),
        in_specs=[pl.BlockSpec((8, D), lambda j: (0, 0)), pl.BlockSpec((D, tn), lambda j: (0, j)),
                  pl.BlockSpec((1, tn), lambda j: (0, j))],
        out_specs=pl.BlockSpec((8, tn), lambda j: (0, j)),
        out_shape=jax.ShapeDtypeStruct((8, n), F32),
        compiler_params=_cp("parallel"),
    )(c_all, w_ada, b_ada)


def ada_bwd(c_all_t, dmod, name, tn=768):
    n = dmod.shape[1]

    def body(c_ref, d_ref, o_ref):
        cv = c_ref[...]
        ca = (cv * _sigmoid(cv)).astype(BF16)
        o_ref[...] = _dot(ca, d_ref[...].astype(BF16))

    return pl.pallas_call(
        body, name=name, grid=(n // tn,),
        in_specs=[pl.BlockSpec((D, 128), lambda j: (0, 0)), pl.BlockSpec((128, tn), lambda j: (0, j))],
        out_specs=pl.BlockSpec((D, tn), lambda j: (0, j)),
        out_shape=jax.ShapeDtypeStruct((D, n), F32),
        compiler_params=_cp("parallel"),
    )(c_all_t, dmod)


def _row_tile(rows, cols, itemsize=4, budget=1536 * 1024):
    best = None
    for t in range(8, rows + 1, 8):
        if rows % t == 0 and t * cols * itemsize <= budget:
            best = t
    return rows if best is None else best


def sum_lead(parts, name, out_dtype=F32):
    n, R, C = parts.shape
    tr = _row_tile(R, C * n)

    def body(p_ref, o_ref):
        acc = p_ref[0].astype(F32)
        for k in range(1, n):
            acc = acc + p_ref[k].astype(F32)
        o_ref[...] = acc.astype(out_dtype)

    return pl.pallas_call(
        body, name=name, grid=(R // tr,),
        in_specs=[pl.BlockSpec((n, tr, C), lambda i: (0, i, 0))],
        out_specs=pl.BlockSpec((tr, C), lambda i: (i, 0)),
        out_shape=jax.ShapeDtypeStruct((R, C), out_dtype),
        compiler_params=_cp("parallel"),
    )(parts)


def adamw(w, g, m, v, name, emit_g=False):
    R, C = w.shape
    tr = _row_tile(R, C * 8, budget=8 * 1024 * 1024)

    def body(w_ref, g_ref, m_ref, v_ref, d_ref, mo_ref, vo_ref, *go_ref):
        gv = g_ref[...]
        if emit_g:
            go_ref[0][...] = gv
        mn = ADAM_B1 * m_ref[...] + (1.0 - ADAM_B1) * gv
        vn = ADAM_B2 * v_ref[...] + (1.0 - ADAM_B2) * (gv * gv)
        m_hat = mn / (1.0 - ADAM_B1 ** ADAM_STEP)
        v_hat = vn / (1.0 - ADAM_B2 ** ADAM_STEP)
        d_ref[...] = -ADAM_LR * (m_hat / (jnp.sqrt(v_hat) + ADAM_EPS) + ADAM_WD * w_ref[...])
        mo_ref[...] = mn
        vo_ref[...] = vn

    spec = pl.BlockSpec((tr, C), lambda i: (i, 0))
    return pl.pallas_call(
        body, name=name, grid=(R // tr,),
        in_specs=[spec] * 4, out_specs=[spec] * (4 if emit_g else 3),
        out_shape=[jax.ShapeDtypeStruct((R, C), F32)] * (4 if emit_g else 3),
        compiler_params=_cp("parallel"),
    )(w, g, m, v)


def _mesh_pos():
    return lax.axis_index("x"), lax.axis_index("y"), lax.axis_index("c")


def _other_chips(mx, my):
    return [(1 - mx, my), (mx, 1 - my), (1 - mx, 1 - my)]


def ag_small(x, name):
    R = x.shape[0]

    def body(x_ref, out_ref, send_sems, recv_sems, local_sem):
        mx, my, mc = _mesh_pos()
        me, sibling = (mx, my, mc), (mx, my, 1 - mc)
        chips = _other_chips(mx, my)

        def slot(px, py, pc):
            return out_ref.at[4 * px + 2 * py + pc]

        def copy(k, block, to, src=None):
            return pltpu.make_async_remote_copy(
                src_ref=slot(*block) if src is None else src, dst_ref=slot(*block),
                send_sem=send_sems.at[k], recv_sem=recv_sems.at[k], device_id=to, device_id_type=MESH)

        mine = pltpu.make_async_copy(x_ref, slot(*me), local_sem)
        mine.start()
        first = [copy(0, me, sibling, src=x_ref)]
        first += [copy(1 + j, me, (*chip, mc), src=x_ref) for j, chip in enumerate(chips)]
        for cp in first:
            cp.start()
        passed = [copy(4 + j, (*chip, mc), sibling) for j, chip in enumerate(chips)]
        for j, chip in enumerate(chips):
            copy(1 + j, (*chip, mc), me).wait_recv()
            passed[j].start()
        copy(0, sibling, me).wait_recv()
        for j, chip in enumerate(chips):
            copy(4 + j, (*chip, 1 - mc), me).wait_recv()
        for cp in first + passed:
            cp.wait_send()
        mine.wait()

    return pl.pallas_call(
        body, name=name,
        out_shape=jax.ShapeDtypeStruct((N_DEV, R, 128), F32),
        in_specs=[pl.BlockSpec(memory_space=pltpu.VMEM)],
        out_specs=pl.BlockSpec(memory_space=pltpu.VMEM),
        scratch_shapes=[pltpu.SemaphoreType.DMA((7,)), pltpu.SemaphoreType.DMA((7,)), pltpu.SemaphoreType.DMA],
        compiler_params=pltpu.CompilerParams(vmem_limit_bytes=VMEM_LIMIT),
    )(x)


BIG = (("ffn1_w_gu", "col", D, PW), ("ffn1_w_down", "row", FF, D), ("w_in", "col", D, PW),
       ("w_att_o", "col", 512, D), ("w_rec_o", "row", D, D), ("w_out", "row", D, D),
       ("ffn2_w_gu", "col", D, PW), ("ffn2_w_down", "row", FF, D))
NBIG = len(BIG)


def _shard_shape(kind, R, C):
    return (R, C // 4) if kind == "col" else (R // 4, C)


def _region(ref, kind, R, C, q, half, t, tr):
    sr, sc = _shard_shape(kind, R, C)
    if kind == "col":
        return ref.at[pl.ds(pl.multiple_of(half * (R // 2) + t * tr, 16), tr), pl.ds(q * sc, sc)]
    return ref.at[pl.ds(pl.multiple_of(q * sr + t * tr, 16), tr), pl.ds(half * (C // 2), C // 2)]


def ag_local(w, kind, R, C, p_arr, name, after=()):
    sr, sc = _shard_shape(kind, R, C)
    tr = _row_tile(sr, sc, budget=2 * 1024 * 1024)
    nt = sr // tr
    after = list(after)

    def body(p_ref, w_ref, *rest):
        rest[-1][...] = w_ref[...].astype(BF16)

    if kind == "col":
        o_spec = pl.BlockSpec((tr, sc), lambda i, p: (i, p[0]))
    else:
        o_spec = pl.BlockSpec((tr, sc), lambda i, p: (p[0] * nt + i, 0))
    return pl.pallas_call(
        body, name=name,
        grid_spec=pltpu.PrefetchScalarGridSpec(
            num_scalar_prefetch=1, grid=(nt,),
            in_specs=[pl.BlockSpec((tr, sc), lambda i, p: (i, 0))] + [ANY] * len(after), out_specs=o_spec),
        out_shape=jax.ShapeDtypeStruct((R, C), BF16),
        compiler_params=_cp("parallel"),
    )(p_arr, w, *after)


HBM_SPEC = pl.BlockSpec(memory_space=pltpu.HBM)
SEM_SPEC = pl.BlockSpec(memory_space=pltpu.SEMAPHORE)


def _ag_sems(geoms):
    return sum(6 if both else 3 for (_, _, _, both) in geoms)


def _ag_copies(fulls, geoms, ssem, rsem, mx, my, mc, q, h):
    chips = _other_chips(mx, my)
    out, base = [], 0
    for w, (kind, R, C, both) in enumerate(geoms):
        sr, sc = _shard_shape(kind, R, C)
        hr = sr // 2 if kind == "col" else sr
        reg = _region(fulls[w], kind, R, C, q, h, 0, hr)
        out.append([pltpu.make_async_remote_copy(
            src_ref=reg, dst_ref=reg, send_sem=ssem.at[base + 3 * t + k], recv_sem=rsem.at[base + 3 * t + k],
            device_id=(*chips[k], mc if t == 0 else 1 - mc), device_id_type=MESH)
            for t in range(2 if both else 1) for k in range(3)])
        base += 6 if both else 3
    return out


def ag_start(fulls, geoms, after, name):
    n = len(fulls)
    after = list(after)
    m = len(after)

    def body(*refs):
        ssem, rsem = refs[n + m:n + m + 2]
        outs, token = refs[n + m + 2:2 * n + m + 2], refs[2 * n + m + 2]
        mx, my, mc = _mesh_pos()
        p = 2 * mx + my
        col = [w for w, g in enumerate(geoms) if g[0] == "col"]
        row = [w for w, g in enumerate(geoms) if g[0] == "row"]
        for q in range(4):
            @pl.when(p == q)
            def _(q=q):
                cps = _ag_copies(outs, geoms, ssem, rsem, mx, my, mc, q, mc)
                for w in col:
                    for cp in cps[w]:
                        cp.start()
        for h in range(2):
            @pl.when(mc == h)
            def _(h=h):
                cps = _ag_copies(outs, geoms, ssem, rsem, mx, my, mc, p, h)
                for w in row:
                    for cp in cps[w]:
                        cp.start()
        token[...] = jnp.zeros_like(token)

    res = pl.pallas_call(
        body, name=name,
        out_shape=[pltpu.SemaphoreType.DMA((_ag_sems(geoms),)), pltpu.SemaphoreType.DMA((_ag_sems(geoms),))]
        + [pltpu.HBM(a.shape, a.dtype) for a in fulls] + [jax.ShapeDtypeStruct((8, 128), F32)],
        in_specs=[HBM_SPEC] * n + [ANY] * m,
        out_specs=[SEM_SPEC, SEM_SPEC] + [HBM_SPEC] * n + [pl.BlockSpec(memory_space=pltpu.VMEM)],
        input_output_aliases={w: 2 + w for w in range(n)},
        compiler_params=pltpu.CompilerParams(has_side_effects=pltpu.SideEffectType.DATAFLOW_SIDE_EFFECTING),
    )(*[pltpu.with_memory_space_constraint(a, pltpu.HBM) for a in fulls], *after)
    return res[0], res[1], list(res[2:2 + n]), res[2 + n]


def ag_wait(fulls, geoms, ssem, rsem, after, name):
    n = len(fulls)

    def body(*refs):
        ins, ssem_ref, rsem_ref = refs[:n], refs[n], refs[n + 1]
        mx, my, mc = _mesh_pos()
        for cps in _ag_copies(ins, geoms, ssem_ref, rsem_ref, mx, my, mc, 0, 0):
            for cp in cps:
                cp.wait_send()
                cp.wait_recv()

    return list(pl.pallas_call(
        body, name=name,
        out_shape=[pltpu.HBM(a.shape, a.dtype) for a in fulls],
        in_specs=[HBM_SPEC] * n + [SEM_SPEC, SEM_SPEC, ANY],
        out_specs=[HBM_SPEC] * n,
        input_output_aliases={w: w for w in range(n)},
        compiler_params=pltpu.CompilerParams(has_side_effects=pltpu.SideEffectType.DATAFLOW_SIDE_EFFECTING),
    )(*fulls, ssem, rsem, after))


def ag_forward(full, kind, R, C, name):
    sr, sc = _shard_shape(kind, R, C)
    hr, hc = (sr // 2, sc) if kind == "col" else (sr, sc // 2)
    tr = _row_tile(hr, hc, itemsize=2, budget=512 * 1024)
    nt = hr // tr

    total = 3 * nt

    def body(src_ref, full_ref, stage, lsem, ssem, rsem):
        step = pl.program_id(0) * nt + pl.program_id(1)
        par = step % 2
        mx, my, mc = _mesh_pos()

        def load(s, q, h, t):
            return pltpu.make_async_copy(_region(src_ref, kind, R, C, q, h, t, tr), stage.at[s], lsem.at[s])

        def push(s, q, h, t):
            return pltpu.make_async_remote_copy(src_ref=stage.at[s], dst_ref=_region(full_ref, kind, R, C, q, h, t, tr),
                                                send_sem=ssem.at[s], recv_sem=rsem, device_id=(mx, my, 1 - mc),
                                                device_id_type=MESH)

        def for_tile(stp, fn):
            q_k = _partner_chip(stp // nt, 2 * mx + my)
            if kind == "col":
                for q in range(4):
                    @pl.when(q_k == q)
                    def _(q=q):
                        fn(q, mc, stp % nt)
            else:
                for h in range(2):
                    @pl.when(mc == h)
                    def _(h=h):
                        fn(q_k, h, stp % nt)

        @pl.when(step == 0)
        def _():
            for_tile(step, lambda q, h, t: load(0, q, h, t).start())

        load(par, 0, 0, 0).wait()
        for_tile(step, lambda q, h, t: push(par, q, h, t).start())

        @pl.when(step + 1 < total)
        def _():
            @pl.when(step >= 1)
            def _():
                push(1 - par, 0, 0, 0).wait_send()
            for_tile(step + 1, lambda q, h, t: load(1 - par, q, h, t).start())

        @pl.when(step == total - 1)
        def _():
            push(par, 0, 0, 0).wait_send()
            push(1 - par, 0, 0, 0).wait_send()
            three = full_ref.at[pl.ds(0, hr), pl.ds(0, 3 * hc)] if kind == "col" else full_ref.at[pl.ds(0, 3 * hr), pl.ds(0, hc)]
            pltpu.make_async_remote_copy(src_ref=three, dst_ref=three, send_sem=ssem.at[0], recv_sem=rsem,
                                         device_id=(mx, my, 1 - mc), device_id_type=MESH).wait_recv()

    return pl.pallas_call(
        body, name=name, grid=(3, nt),
        in_specs=[ANY], out_specs=ANY,
        out_shape=jax.ShapeDtypeStruct((R, C), BF16),
        scratch_shapes=[pltpu.VMEM((2, tr, hc), BF16), pltpu.SemaphoreType.DMA((2,)), pltpu.SemaphoreType.DMA((2,)),
                        pltpu.SemaphoreType.DMA],
        input_output_aliases={0: 0},
        compiler_params=_cp("arbitrary", "arbitrary"),
    )(full)


def _half_shape(kind, R, C):
    return (R // 2, C) if kind == "col" else (R, C // 2)


def _piece_shape(kind, R, C):
    return (R // 2, C // 4) if kind == "col" else (R // 4, C // 2)


def pair_push(g, kind, c_arr, name):
    R, C = g.shape
    hr, hc = _half_shape(kind, R, C)
    tr = _row_tile(hr, hc, itemsize=2, budget=1024 * 1024)
    nt = hr // tr

    def body(c_ref, g_ref, out_ref, stage, ssem, rsem):
        i = pl.program_id(0)
        slot = i % 2
        mx, my, mc = _mesh_pos()

        def push(s, t):
            return pltpu.make_async_remote_copy(
                src_ref=stage.at[s], dst_ref=out_ref.at[pl.ds(pl.multiple_of(t * tr, 16), tr)],
                send_sem=ssem.at[s], recv_sem=rsem, device_id=(mx, my, 1 - mc), device_id_type=MESH)

        @pl.when(i >= 2)
        def _():
            push(slot, 0).wait_send()

        stage[slot] = g_ref[...]
        push(slot, i).start()

        @pl.when(i == nt - 1)
        def _():
            push(slot, 0).wait_send()
            if nt >= 2:
                push(1 - slot, 0).wait_send()
            pltpu.make_async_remote_copy(src_ref=out_ref, dst_ref=out_ref, send_sem=ssem.at[0], recv_sem=rsem,
                                         device_id=(mx, my, 1 - mc), device_id_type=MESH).wait_recv()

    if kind == "col":
        g_spec = pl.BlockSpec((tr, hc), lambda i, c: ((1 - c[0]) * nt + i, 0))
    else:
        g_spec = pl.BlockSpec((tr, hc), lambda i, c: (i, 1 - c[0]))
    return pl.pallas_call(
        body, name=name,
        grid_spec=pltpu.PrefetchScalarGridSpec(
            num_scalar_prefetch=1, grid=(nt,), in_specs=[g_spec], out_specs=ANY,
            scratch_shapes=[pltpu.VMEM((2, tr, hc), BF16), pltpu.SemaphoreType.DMA((2,)), pltpu.SemaphoreType.DMA]),
        out_shape=jax.ShapeDtypeStruct((hr, hc), BF16),
        compiler_params=_cp("arbitrary"),
    )(c_arr, g)


def _partner_chip(k, p):
    return p ^ jnp.where(k == 0, 2, jnp.where(k == 1, 1, jnp.where(k == 2, 3, 0)))


def pair_add(g, got, kind, cp_arr, name):
    R, C = g.shape
    pr, pc = _piece_shape(kind, R, C)
    tr = _row_tile(pr, pc, itemsize=2, budget=1024 * 1024)
    nt = pr // tr

    def body(cp_ref, g_ref, got_ref, ps_ref, rb_ref):
        tile = (g_ref[...].astype(F32) + got_ref[...].astype(F32)).astype(BF16)
        ps_ref[...] = tile

        @pl.when(pl.program_id(1) == cp_ref[1])
        def _():
            rb_ref[...] = tile

    if kind == "col":
        g_spec = pl.BlockSpec((tr, pc), lambda i, q, cp: (cp[0] * nt + i, q))
        got_spec = pl.BlockSpec((tr, pc), lambda i, q, cp: (i, q))
    else:
        g_spec = pl.BlockSpec((tr, pc), lambda i, q, cp: (q * nt + i, cp[0]))
        got_spec = pl.BlockSpec((tr, pc), lambda i, q, cp: (q * nt + i, 0))
    return pl.pallas_call(
        body, name=name,
        grid_spec=pltpu.PrefetchScalarGridSpec(
            num_scalar_prefetch=1, grid=(nt, 4), in_specs=[g_spec, got_spec],
            out_specs=[pl.BlockSpec((None, tr, pc), lambda i, q, cp: (q, i, 0)),
                       pl.BlockSpec((None, tr, pc), lambda i, q, cp: (cp[1], i, 0))]),
        out_shape=[jax.ShapeDtypeStruct((4, pr, pc), BF16)] * 2,
        compiler_params=_cp("arbitrary", "arbitrary"),
    )(cp_arr, g, got)


def _rs_copies(ps, rb, ssem, rsem, mx, my, mc):
    p = 2 * mx + my
    out = []
    for w in range(len(ps)):
        for k, chip in enumerate(_other_chips(mx, my)):
            out.append(pltpu.make_async_remote_copy(
                src_ref=ps[w].at[2 * chip[0] + chip[1]], dst_ref=rb[w].at[p], send_sem=ssem.at[3 * w + k],
                recv_sem=rsem.at[3 * w + k], device_id=(*chip, mc), device_id_type=MESH))
    return out


def rs_start(ps, rb, after, name):
    n = len(ps)
    after = list(after)
    m = len(after)

    def body(*refs):
        ssem, rsem = refs[2 * n + m:2 * n + m + 2]
        ps_o = refs[2 * n + m + 2:3 * n + m + 2]
        rb_o = refs[3 * n + m + 2:4 * n + m + 2]
        token = refs[4 * n + m + 2]
        for cp in _rs_copies(ps_o, rb_o, ssem, rsem, *_mesh_pos()):
            cp.start()
        token[...] = jnp.zeros_like(token)

    both = list(ps) + list(rb)
    res = pl.pallas_call(
        body, name=name,
        out_shape=[pltpu.SemaphoreType.DMA((3 * n,)), pltpu.SemaphoreType.DMA((3 * n,))]
        + [pltpu.HBM(a.shape, a.dtype) for a in both] + [jax.ShapeDtypeStruct((8, 128), F32)],
        in_specs=[HBM_SPEC] * (2 * n) + [ANY] * m,
        out_specs=[SEM_SPEC, SEM_SPEC] + [HBM_SPEC] * (2 * n) + [pl.BlockSpec(memory_space=pltpu.VMEM)],
        input_output_aliases={w: 2 + w for w in range(2 * n)},
        compiler_params=pltpu.CompilerParams(has_side_effects=pltpu.SideEffectType.DATAFLOW_SIDE_EFFECTING),
    )(*[pltpu.with_memory_space_constraint(a, pltpu.HBM) for a in both], *after)
    return res[0], res[1], list(res[2:2 + n]), list(res[2 + n:2 + 2 * n]), res[2 + 2 * n]


def rs_wait(ps, rb, ssem, rsem, after, name):
    n = len(ps)
    after = list(after)
    m = len(after)

    def body(*refs):
        ps_i, rb_i = refs[:n], refs[n:2 * n]
        ssem_ref, rsem_ref = refs[2 * n], refs[2 * n + 1]
        for cp in _rs_copies(ps_i, rb_i, ssem_ref, rsem_ref, *_mesh_pos()):
            cp.wait_send()
            cp.wait_recv()

    both = list(ps) + list(rb)
    res = pl.pallas_call(
        body, name=name,
        out_shape=[pltpu.HBM(a.shape, a.dtype) for a in both],
        in_specs=[HBM_SPEC] * (2 * n) + [SEM_SPEC, SEM_SPEC] + [ANY] * m,
        out_specs=[HBM_SPEC] * (2 * n),
        input_output_aliases={w: w for w in range(2 * n)},
        compiler_params=pltpu.CompilerParams(has_side_effects=pltpu.SideEffectType.DATAFLOW_SIDE_EFFECTING),
    )(*both, ssem, rsem, *after)
    return list(res[n:])


def sum_share(parts, kind, R, C, name):
    _, pr, pc = parts.shape
    sr, sc = _shard_shape(kind, R, C)
    tr = _row_tile(pr, pc * 4, budget=4 * 1024 * 1024)
    nt = pr // tr

    def body(p_ref, fin_ref, stage, lsem, ssem, rsem):
        i = pl.program_id(0)
        slot = i % 2
        mx, my, mc = _mesh_pos()

        def region(h, t):
            r0 = pl.multiple_of(t * tr, 8)
            if kind == "col":
                return fin_ref.at[pl.ds(pl.multiple_of(h * pr + r0, 8), tr)]
            return fin_ref.at[pl.ds(r0, tr), pl.ds(h * pc, pc)]

        def copies(s, h, t):
            return (pltpu.make_async_copy(stage.at[s], region(h, t), lsem.at[s]),
                    pltpu.make_async_remote_copy(src_ref=stage.at[s], dst_ref=region(h, t), send_sem=ssem.at[s],
                                                 recv_sem=rsem, device_id=(mx, my, 1 - mc), device_id_type=MESH))

        def wait_sent(s):
            loc, rem = copies(s, 0, 0)
            loc.wait()
            rem.wait_send()

        @pl.when(i >= 2)
        def _():
            wait_sent(slot)

        acc = p_ref[0].astype(F32)
        for k in range(1, 4):
            acc = acc + p_ref[k].astype(F32)
        stage[slot] = acc
        if kind == "col":
            for cp in copies(slot, mc, i):
                cp.start()
        else:
            for h in range(2):
                @pl.when(mc == h)
                def _(h=h):
                    for cp in copies(slot, h, i):
                        cp.start()

        @pl.when(i == nt - 1)
        def _():
            wait_sent(slot)
            if nt >= 2:
                wait_sent(1 - slot)
            half = fin_ref.at[pl.ds(0, pr), pl.ds(0, pc)]
            pltpu.make_async_remote_copy(src_ref=half, dst_ref=half, send_sem=ssem.at[0], recv_sem=rsem,
                                         device_id=(mx, my, 1 - mc), device_id_type=MESH).wait_recv()

    return pl.pallas_call(
        body, name=name, grid=(nt,),
        in_specs=[pl.BlockSpec((4, tr, pc), lambda i: (0, i, 0))],
        out_specs=ANY,
        out_shape=jax.ShapeDtypeStruct((sr, sc), F32),
        scratch_shapes=[pltpu.VMEM((2, tr, pc), F32), pltpu.SemaphoreType.DMA((2,)), pltpu.SemaphoreType.DMA((2,)),
                        pltpu.SemaphoreType.DMA],
        compiler_params=_cp("arbitrary"),
    )(parts)


def _pack(parts, rows):
    flat = []
    for a in parts:
        a = jnp.ravel(a).astype(F32)
        flat.append(jnp.pad(a, (0, (-a.shape[0]) % 128)))
    v = jnp.concatenate(flat)
    return jnp.pad(v, (0, rows * 128 - v.shape[0])).reshape(rows, 128)


def _unpack(block, shapes):
    lead = block.shape[:-2]
    v = block.reshape(lead + (-1,))
    out, off = [], 0
    for shp in shapes:
        n = int(np.prod(shp))
        out.append(v[..., off:off + n].reshape(lead + tuple(shp)))
        off += n + (-n) % 128
    return out


def _block_diag4(w):
    w4 = w.reshape(4, 4, 64, 64)
    eye = jnp.eye(4, dtype=w.dtype)
    return (w4[:, :, :, None, :] * eye[None, :, None, :, None]).reshape(4, 256, 256)


def _diag_blocks(bd):
    b5 = bd.reshape(4, 4, 64, 4, 64)
    return jnp.stack([b5[:, i, :, i, :] for i in range(4)], axis=1).reshape(16, 64, 64)


def _bias_window(rel_bias):
    m = (np.arange(768) + 127) % 768 - 127
    w = rel_bias[:, np.clip(512 - m, -128, 128) + 128]
    win = jnp.tile(w, (1, 128))[:, :128 * 767].reshape(8, 128, 767)[:, :, :WIN]
    qh = np.arange(128)[:, None] // CHUNK
    kc = np.arange(WIN)[None, :] // CHUNK
    valid = (kc >= qh) & (kc <= qh + 8)
    return jnp.where(jnp.asarray(valid)[None], win, NEG)


SMALL = ("b_ada", "norm_pre", "norm_post", "rel_bias", "conv_w", "conv_b", "lru_wa", "lru_ba", "lru_wx",
         "lru_bx", "lru_lambda")
WEIGHTS = ("w_ada", "b_ada", "norm_pre", "norm_post", "ffn1_w_gu", "ffn1_w_down", "w_in", "rel_bias", "conv_w",
           "conv_b", "lru_wa", "lru_ba", "lru_wx", "lru_bx", "lru_lambda", "w_att_o", "w_rec_o", "w_out",
           "ffn2_w_gu", "ffn2_w_down")


def kernel(x, c, w_ada, b_ada, norm_pre, norm_post, ffn1_w_gu, ffn1_w_down, w_in, rel_bias, conv_w, conv_b, lru_wa, lru_ba, lru_wx, lru_bx, lru_lambda, w_att_o, w_rec_o, w_out, ffn2_w_gu, ffn2_w_down, loss_target, m_w_ada, m_b_ada, m_norm_pre, m_norm_post, m_ffn1_w_gu, m_ffn1_w_down, m_w_in, m_rel_bias, m_conv_w, m_conv_b, m_lru_wa, m_lru_ba, m_lru_wx, m_lru_bx, m_lru_lambda, m_w_att_o, m_w_rec_o, m_w_out, m_ffn2_w_gu, m_ffn2_w_down, v_w_ada, v_b_ada, v_norm_pre, v_norm_post, v_ffn1_w_gu, v_ffn1_w_down, v_w_in, v_rel_bias, v_conv_w, v_conv_b, v_lru_wa, v_lru_ba, v_lru_wx, v_lru_bx, v_lru_lambda, v_w_att_o, v_w_rec_o, v_w_out, v_ffn2_w_gu, v_ffn2_w_down):
    W = dict(w_ada=w_ada, b_ada=b_ada, norm_pre=norm_pre, norm_post=norm_post, ffn1_w_gu=ffn1_w_gu,
             ffn1_w_down=ffn1_w_down, w_in=w_in, rel_bias=rel_bias, conv_w=conv_w, conv_b=conv_b, lru_wa=lru_wa,
             lru_ba=lru_ba, lru_wx=lru_wx, lru_bx=lru_bx, lru_lambda=lru_lambda, w_att_o=w_att_o, w_rec_o=w_rec_o,
             w_out=w_out, ffn2_w_gu=ffn2_w_gu, ffn2_w_down=ffn2_w_down)
    M = dict(w_ada=m_w_ada, b_ada=m_b_ada, norm_pre=m_norm_pre, norm_post=m_norm_post, ffn1_w_gu=m_ffn1_w_gu,
             ffn1_w_down=m_ffn1_w_down, w_in=m_w_in, rel_bias=m_rel_bias, conv_w=m_conv_w, conv_b=m_conv_b,
             lru_wa=m_lru_wa, lru_ba=m_lru_ba, lru_wx=m_lru_wx, lru_bx=m_lru_bx, lru_lambda=m_lru_lambda,
             w_att_o=m_w_att_o, w_rec_o=m_w_rec_o, w_out=m_w_out, ffn2_w_gu=m_ffn2_w_gu, ffn2_w_down=m_ffn2_w_down)
    V = dict(w_ada=v_w_ada, b_ada=v_b_ada, norm_pre=v_norm_pre, norm_post=v_norm_post, ffn1_w_gu=v_ffn1_w_gu,
             ffn1_w_down=v_ffn1_w_down, w_in=v_w_in, rel_bias=v_rel_bias, conv_w=v_conv_w, conv_b=v_conv_b,
             lru_wa=v_lru_wa, lru_ba=v_lru_ba, lru_wx=v_lru_wx, lru_bx=v_lru_bx, lru_lambda=v_lru_lambda,
             w_att_o=v_w_att_o, w_rec_o=v_w_rec_o, w_out=v_w_out, ffn2_w_gu=v_ffn2_w_gu, ffn2_w_down=v_ffn2_w_down)
    mx, my, mc = _mesh_pos()
    p = 2 * mx + my
    e = 4 * mx + 2 * my + mc
    xs = x[0]

    c_arr = jnp.reshape(mc, (1,)).astype(jnp.int32)
    cp_arr = jnp.stack([mc, p]).astype(jnp.int32)
    p_arr = jnp.reshape(p, (1,)).astype(jnp.int32)
    direct = ("w_att_o", "w_rec_o", "w_out", "ffn2_w_gu", "ffn2_w_down")
    geoms = [(kind, R, C, n in direct) for (n, kind, R, C) in BIG]
    names = [b[0] for b in BIG]
    placed = [ag_local(W[n][0], kind, R, C, p_arr, "ag_local_" + n) for (n, kind, R, C) in BIG[:2]]

    def arrived(fly, lo, hi, ssem, rsem, after, tag):
        done = ag_wait(fly, geoms[lo:hi], ssem, rsem, after, "ag_wait_" + tag)
        return [a if both else ag_forward(a, kind, R, C, "ag_forward_" + n)
                for a, (kind, R, C, both), n in zip(done, geoms[lo:hi], names[lo:hi])]

    g1 = ag_small(_pack([c, norm_pre, norm_post, conv_w], 32), "ag_small_params")
    c_all, npre4, npost4, cw4 = _unpack(g1, [(D,), (3, 256), (3, 256), (4, 256)])
    chipwise = lambda a: jnp.moveaxis(a[0::2], 0, 1).reshape(a.shape[1], D)
    npre, npost, conv_full = chipwise(npre4), chipwise(npost4), chipwise(cw4)

    b_cols = lax.dynamic_slice(b_ada, (0, p * 2304), (1, 2304))
    mod_cols = ada_fwd(c_all, w_ada[0], b_cols, "ada_fwd")
    g2 = ag_small(mod_cols.reshape(144, 128), "ag_mod")
    mod_all = jnp.moveaxis(g2[0::2].reshape(4, 8, 2304), 0, 1).reshape(8, 9 * D)
    mod = lax.dynamic_index_in_dim(mod_all, e, 0, keepdims=False).reshape(3, 3, D)
    zeros3 = jnp.zeros((3, D), F32)
    vecs = [jnp.concatenate([npre[k:k + 1], npost[k:k + 1], mod[k], zeros3], axis=0) for k in range(3)]

    f1_s, f1_r, f1_fly, tok0 = ag_start(placed[:2], geoms[:2], [g2], "ag_start_ffn1")
    placed += [ag_local(W[n][0], kind, R, C, p_arr, "ag_local_" + n, after=[tok0]) for (n, kind, R, C) in BIG[2:]]
    f1_gu, f1_dn = arrived(f1_fly, 0, 2, f1_s, f1_r, placed[7], "ffn1")
    mix_s, mix_r, mix_fly, tok1 = ag_start(placed[2:6], geoms[2:6], [f1_gu, f1_dn], "ag_start_mixer")
    ffn_s, ffn_r, ffn_fly, tok2 = ag_start(placed[6:], geoms[6:], [tok1], "ag_start_ffn2")
    wa_bd = _block_diag4(lru_wa[0]).astype(BF16)
    wx_bd = _block_diag4(lru_wx[0]).astype(BF16)
    pvec = jnp.concatenate([conv_full, conv_b, lru_ba, lru_bx, lru_lambda], axis=0)
    bias = _bias_window(rel_bias[0]).reshape(4, 256, WIN)

    f1_u = f1_gu[:, FF:]
    x1, h1, g1_, u1, a1, f1 = ffn_fwd(xs, vecs[0] + tok2[0:1, 0:1], f1_gu, f1_u, f1_dn, 0.5, "ffn1_fwd")
    win, wao, wro, wout = arrived(mix_fly, 2, 6, mix_s, mix_r, x1, "mixer")
    h2, qkv, rest = proj_fwd(x1, vecs[1], win, "proj_fwd")
    ao = attn_fwd(qkv, bias, "attn_fwd")
    hl, hg = lru_fwd(rest, pvec, wa_bd, wx_bd, "lru_fwd")
    x2, att, rec, mg, f2 = mix_out_fwd(x1, ao, hg, rest, vecs[1], wao, wro, wout, "mix_out_fwd")
    f2_gu, f2_dn = arrived(ffn_fly, 6, 8, ffn_s, ffn_r, x2, "ffn2")
    f2_u = f2_gu[:, FF:]
    dy, h3, g3_, u3, a3, f3, lvec = ffn_fwd(x2, vecs[2], f2_gu, f2_u, f2_dn, 0.5, "ffn2_fwd", tgt=loss_target[0])

    G, grads = {}, {}
    geo = {n: (kind, R, C) for (n, kind, R, C) in BIG}

    def reduce_begin(names, tag):
        ps, rb = [], []
        for n in names:
            got = pair_push(G[n], geo[n][0], c_arr, "rs_push_" + n)
            a, b = pair_add(G[n], got, geo[n][0], cp_arr, "rs_pair_sum_" + n)
            ps.append(a)
            rb.append(b)
        return rs_start(ps, rb, [], "rs_start_" + tag)

    def reduce_end(names, flight, after, tag):
        ssem, rsem, ps, rb, _ = flight
        for a, n in zip(rs_wait(ps, rb, ssem, rsem, after, "rs_wait_" + tag), names):
            grads[n] = sum_share(a, *geo[n], "rs_sum_share_" + n)[None]

    dx2, df3, dgu3, va2 = ffn_bwd(dy, x2, f3, g3_, u3, vecs[2], f2_gu, f2_u, f2_dn, 0.5, "ffn2_bwd")
    G["ffn2_w_gu"] = mm_tn(h3, dgu3, "dw_ffn2_gu", D, 1408, 1024)
    G["ffn2_w_down"] = mm_tn(a3, df3, "dw_ffn2_down", 1408, D, 1024)
    fly_ffn2 = reduce_begin(("ffn2_w_gu", "ffn2_w_down"), "ffn2")
    vec1 = vecs[1] + fly_ffn2[4][0:1, 0:1]
    df2, d_att, d_rec, dao, dhl, d3, va_out = mix_out_bwd(dx2, f2, att, rec, rest, hl, vec1, wao, wro, wout,
                                                          "mix_out_bwd")
    G["w_out"] = mm_tn(mg, df2, "dw_out", D, D, 1024)
    G["w_att_o"] = mm_tn(ao, d_att, "dw_att_o", 512, D, 1024)
    G["w_rec_o"] = mm_tn(hg, d_rec, "dw_rec_o", D, D, 1024)
    dq, db, dkv = attn_bwd(qkv, dao, bias, "attn_bwd")
    dxr, v_lru, dwa_bd, dwx_bd = lru_bwd(dhl, hl, rest, pvec, wa_bd, wx_bd, "lru_bwd")
    dx1, va_in = proj_bwd(dq, dkv, dxr, d3, win, x1, dx2, vecs[1], "proj_bwd")
    G["w_in"] = dw_in(h2, dq, dkv, dxr, d3, "dw_in")
    fly_mix = reduce_begin(("w_in", "w_att_o", "w_rec_o", "w_out"), "mixer")
    vec0 = vecs[0] + fly_mix[4][0:1, 0:1]
    dx0, df1, dgu1, va0 = ffn_bwd(dx1, xs, f1, g1_, u1, vec0, f1_gu, f1_u, f1_dn, 0.5, "ffn1_bwd")
    G["ffn1_w_gu"] = mm_tn(h1, dgu1, "dw_ffn1_gu", D, 1408, 1024)
    G["ffn1_w_down"] = mm_tn(a1, df1, "dw_ffn1_down", 1408, D, 1024)
    fly_ffn1 = reduce_begin(("ffn1_w_gu", "ffn1_w_down"), "ffn1")
    reduce_end(("ffn2_w_gu", "ffn2_w_down"), fly_ffn2, [fly_ffn1[4]], "ffn2")
    reduce_end(("w_in", "w_att_o", "w_rec_o", "w_out"), fly_mix, [fly_ffn1[4], grads["ffn2_w_down"]], "mixer")

    va1 = va_out + va_in
    vas = (va0, va1, va2)
    dmod = jnp.stack([v[2:5] for v in vas])
    part = {"b_ada": dmod, "norm_pre": jnp.stack([v[0] for v in vas]), "norm_post": jnp.stack([v[1] for v in vas]),
            "rel_bias": bias_grad(db.reshape(8, 128, WIN), "bias_grad")[:, :257], "conv_w": v_lru[0:4], "conv_b": v_lru[4],
            "lru_wa": _diag_blocks(dwa_bd), "lru_ba": v_lru[5], "lru_wx": _diag_blocks(dwx_bd), "lru_bx": v_lru[6],
            "lru_lambda": v_lru[7]}
    full_shapes = {"b_ada": (9 * D,), "norm_pre": (3, D), "norm_post": (3, D), "rel_bias": (8, 257),
                   "conv_w": (4, D), "conv_b": (D,), "lru_wa": (16, 64, 64), "lru_ba": (D,),
                   "lru_wx": (16, 64, 64), "lru_bx": (D,), "lru_lambda": (D,)}
    g3 = ag_small(_pack([part[n] for n in SMALL] + [lvec[0:1, 0:1]], 1232), "ag_small_grads")
    summed = _unpack(sum_lead(g3, "sum_small_grads"), [full_shapes[n] for n in SMALL] + [(1,)])
    red = dict(zip(SMALL, summed[:-1]))
    loss = summed[-1][0]
    cols = lambda a: lax.dynamic_slice(a, (0, p * 256), (a.shape[0], 256))
    grads.update({"b_ada": red["b_ada"][None], "norm_pre": cols(red["norm_pre"])[None],
                  "norm_post": cols(red["norm_post"])[None], "rel_bias": red["rel_bias"][None],
                  "conv_w": cols(red["conv_w"])[None], "conv_b": red["conv_b"][None], "lru_wa": red["lru_wa"][None],
                  "lru_ba": red["lru_ba"][None], "lru_wx": red["lru_wx"][None], "lru_bx": red["lru_bx"][None],
                  "lru_lambda": red["lru_lambda"][None]})

    dmod_all = g3[:, :72].reshape(8, 9 * D)
    dmod_cols = jnp.pad(lax.dynamic_slice(dmod_all, (0, p * 2304), (8, 2304)), ((0, 120), (0, 0)))
    c_all_t = jnp.pad(c_all.T, ((0, 0), (0, 120)))
    grads["w_ada"] = ada_bwd(c_all_t, dmod_cols, "ada_bwd")[None]

    delta, new_m, new_v = {}, {}, {}

    def update(n):
        shp = W[n].shape
        res = adamw(W[n][0], grads[n][0], M[n][0], V[n][0], "adamw_" + n, emit_g=n in geo)
        delta[n], new_m[n], new_v[n] = [a.reshape(shp) for a in res[:3]]
        if n in geo:
            grads[n] = res[3].reshape(shp)

    for n in ("w_ada", "ffn2_w_gu", "ffn2_w_down", "w_in", "w_att_o", "w_rec_o", "w_out"):
        update(n)
    packed = [_pack([src[n] for n in SMALL], 1168) for src in (W, grads, M, V)]
    outs = adamw(*packed, "adamw_small")
    for dst, blk in zip((delta, new_m, new_v), outs):
        for n, a in zip(SMALL, _unpack(blk, [W[n].shape for n in SMALL])):
            dst[n] = a
    reduce_end(("ffn1_w_gu", "ffn1_w_down"), fly_ffn1,
               [outs[0], delta["w_ada"], delta["ffn2_w_gu"], delta["ffn2_w_down"], delta["w_in"], delta["w_out"]], "ffn1")
    for n in ("ffn1_w_gu", "ffn1_w_down"):
        update(n)

    return (loss, dx0[None], *[grads[n] for n in WEIGHTS], *[delta[n] for n in WEIGHTS],
            *[new_m[n] for n in WEIGHTS], *[new_v[n] for n in WEIGHTS])
```

```python
import functools

import numpy as np
import jax
import jax.numpy as jnp
from jax import lax
from jax.experimental import pallas as pl
from jax.experimental.pallas import tpu as pltpu

F32 = jnp.float32
BF16 = jnp.bfloat16

D = 1024
FF = 2816
PW = 5632
HP = 128
CHUNK = 64
WIN = 640
TQ = 512
EPS = 1e-6
NEG = -1e30
LRU_C = 8.0
N_DEV = 8
VMEM_LIMIT = 56 * 1024 * 1024

ADAM_LR, ADAM_B1, ADAM_B2, ADAM_EPS, ADAM_WD, ADAM_STEP = 0.001, 0.9, 0.999, 1e-08, 0.01, 10

MESH = pl.DeviceIdType.MESH
ANY = pl.BlockSpec(memory_space=pl.ANY)


def _cp(*sem):
    return pltpu.CompilerParams(dimension_semantics=tuple(sem), vmem_limit_bytes=VMEM_LIMIT)


def _dot(a, b):
    return jnp.dot(a, b, preferred_element_type=F32)


def _dot_nt(a, b):
    return lax.dot_general(a, b, (((1,), (1,)), ((), ())), preferred_element_type=F32)


def _dot_tn(a, b):
    return lax.dot_general(a, b, (((0,), (0,)), ((), ())), preferred_element_type=F32)


def _mean(v):
    return jnp.mean(v, axis=-1, keepdims=True)


def _colsum(v):
    return jnp.sum(v, axis=0, keepdims=True)


def _sigmoid(v):
    return 0.5 * jnp.tanh(0.5 * v) + 0.5


_GK = 0.7978845608028654


def _gelu(v):
    t = jnp.tanh(_GK * (v + 0.044715 * v * v * v))
    return 0.5 * v * (1.0 + t)


def _pre_norm(xv, vec_ref):
    r = lax.rsqrt(_mean(xv * xv) + EPS)
    n = xv * r * vec_ref[0:1, :]
    return n * (1.0 + vec_ref[3:4, :]) + vec_ref[2:3, :]


def _pre_norm_bwd(dh, xv, dres, vec_ref, vacc_ref):
    r = lax.rsqrt(_mean(xv * xv) + EPS)
    xh = xv * r
    n = xh * vec_ref[0:1, :]
    vacc_ref[2:3, :] += _colsum(dh)
    vacc_ref[3:4, :] += _colsum(dh * n)
    dn = dh * (1.0 + vec_ref[3:4, :])
    vacc_ref[0:1, :] += _colsum(dn * xh)
    dxh = dn * vec_ref[0:1, :]
    return r * (dxh - xh * _mean(dxh * xh)) + dres


def _post_norm_bwd(dxo, fv, res, vec_ref, vacc_ref):
    rf = lax.rsqrt(_mean(fv * fv) + EPS)
    fh = fv * rf
    gp = vec_ref[1:2, :]
    vacc_ref[4:5, :] += _colsum(res * dxo * (fh * gp))
    dy = (res * vec_ref[4:5, :]) * dxo
    vacc_ref[1:2, :] += _colsum(dy * fh)
    dfn = dy * gp
    return rf * (dfn - fh * _mean(dfn * fh))


def ffn_fwd(x, vec, w_gu, w_u, w_dn, res, name, tgt=None, tm=1024, tf=512):
    S = x.shape[0]
    tm = min(tm, S)
    nt = S // tm
    nf = -(-FF // tf)
    tail = FF - tf * (nf - 1)
    head = tgt is not None

    def body(*refs):
        x_ref, vec_ref, wg_ref, wu_ref, wd_ref = refs[:5]
        if head:
            t_ref, xo_ref, h_ref, g_ref, u_ref, a_ref, f_ref, l_ref, hs, acc, lacc = refs[5:]
        else:
            xo_ref, h_ref, g_ref, u_ref, a_ref, f_ref, hs, acc = refs[5:]
        i, j = pl.program_id(0), pl.program_id(1)

        @pl.when(j == 0)
        def _():
            h = _pre_norm(x_ref[...], vec_ref).astype(BF16)
            hs[...] = h
            h_ref[...] = h
            acc[...] = jnp.zeros_like(acc)

        def chunk(w):
            h = hs[...]
            g = _dot(h, wg_ref[:, 0:w])
            u = _dot(h, wu_ref[:, 0:w])
            g_ref[:, 0:w] = g.astype(BF16)
            u_ref[:, 0:w] = u.astype(BF16)
            a = (g * _sigmoid(g) * u).astype(BF16)
            a_ref[:, 0:w] = a
            acc[...] += _dot(a, wd_ref[0:w, :])

        @pl.when(j < nf - 1)
        def _():
            chunk(tf)

        @pl.when(j == nf - 1)
        def _():
            chunk(tail)
            f = acc[...]
            f_ref[...] = f.astype(BF16)
            y = f * lax.rsqrt(_mean(f * f) + EPS) * vec_ref[1:2, :]
            xo = x_ref[...] + (res * vec_ref[4:5, :]) * y
            if head:
                @pl.when(i == 0)
                def _():
                    lacc[...] = jnp.zeros_like(lacc)

                d = xo - t_ref[...]
                xo_ref[...] = d * (1.0 / D)
                lacc[...] += _colsum(d * d)

                @pl.when(i == nt - 1)
                def _():
                    l_ref[...] = jnp.broadcast_to(0.5 * jnp.sum(lacc[...]) * (1.0 / D), (8, 128))
            else:
                xo_ref[...] = xo

    row = lambda i, j: (i, 0)
    col = lambda i, j: (i, j)
    once = dict(pipeline_mode=pl.Buffered(1)) if head else {}
    in_specs = [pl.BlockSpec((tm, D), row, **once), pl.BlockSpec((8, D), lambda i, j: (0, 0)),
                pl.BlockSpec((D, tf), lambda i, j: (0, j)), pl.BlockSpec((D, tf), lambda i, j: (0, j)),
                pl.BlockSpec((tf, D), lambda i, j: (j, 0))]
    out_specs = [pl.BlockSpec((tm, D), row), pl.BlockSpec((tm, D), row), pl.BlockSpec((tm, tf), col),
                 pl.BlockSpec((tm, tf), col), pl.BlockSpec((tm, tf), col), pl.BlockSpec((tm, D), row)]
    out_shape = [jax.ShapeDtypeStruct((S, D), F32), jax.ShapeDtypeStruct((S, D), BF16),
                 jax.ShapeDtypeStruct((S, FF), BF16), jax.ShapeDtypeStruct((S, FF), BF16),
                 jax.ShapeDtypeStruct((S, FF), BF16), jax.ShapeDtypeStruct((S, D), BF16)]
    scratch = [pltpu.VMEM((tm, D), BF16), pltpu.VMEM((tm, D), F32)]
    args = [x, vec, w_gu, w_u, w_dn]
    if head:
        in_specs.append(pl.BlockSpec((tm, D), row, **once))
        out_specs.append(pl.BlockSpec((8, 128), lambda i, j: (0, 0)))
        out_shape.append(jax.ShapeDtypeStruct((8, 128), F32))
        scratch.append(pltpu.VMEM((1, D), F32))
        args.append(tgt)
    return pl.pallas_call(
        body, name=name, grid=(nt, nf), in_specs=in_specs, out_specs=out_specs, out_shape=out_shape,
        scratch_shapes=scratch,
        compiler_params=_cp("arbitrary" if head else "parallel", "arbitrary"),
    )(*args)


def ffn_bwd(dxo, x, f, g, u, vec, w_gu, w_u, w_dn, res, name, tm=1024, tf=512):
    S = x.shape[0]
    tm = min(tm, S)
    nf = -(-FF // tf)
    tail = FF - tf * (nf - 1)

    def body(dxo_ref, x_ref, f_ref, g_ref, u_ref, vec_ref, wg_ref, wu_ref, wd_ref,
             dx_ref, df_ref, dgu_ref, vacc_ref, dfs, acc):
        i, j = pl.program_id(0), pl.program_id(1)

        @pl.when((i == 0) & (j == 0))
        def _():
            vacc_ref[...] = jnp.zeros_like(vacc_ref)

        @pl.when(j == 0)
        def _():
            df = _post_norm_bwd(dxo_ref[...], f_ref[...].astype(F32), res, vec_ref, vacc_ref).astype(BF16)
            dfs[...] = df
            df_ref[...] = df
            acc[...] = jnp.zeros_like(acc)

        def chunk(w):
            da = _dot_nt(dfs[...], wd_ref[0:w, :])
            gv, uv = g_ref[:, 0:w].astype(F32), u_ref[:, 0:w].astype(F32)
            sg = _sigmoid(gv)
            dg = (da * uv * (sg * (1.0 + gv * (1.0 - sg)))).astype(BF16)
            du = (da * (gv * sg)).astype(BF16)
            dgu_ref[0, :, 0:w] = dg
            dgu_ref[1, :, 0:w] = du
            acc[...] += _dot_nt(dg, wg_ref[:, 0:w]) + _dot_nt(du, wu_ref[:, 0:w])

        @pl.when(j < nf - 1)
        def _():
            chunk(tf)

        @pl.when(j == nf - 1)
        def _():
            chunk(tail)
            dx_ref[...] = _pre_norm_bwd(acc[...], x_ref[...], dxo_ref[...], vec_ref, vacc_ref)

    row = lambda i, j: (i, 0)
    col = lambda i, j: (i, j)
    return pl.pallas_call(
        body, name=name, grid=(S // tm, nf),
        in_specs=[pl.BlockSpec((tm, D), row, pipeline_mode=pl.Buffered(1)) for _ in range(3)]
        + [pl.BlockSpec((tm, tf), col), pl.BlockSpec((tm, tf), col),
                  pl.BlockSpec((8, D), lambda i, j: (0, 0)),
                  pl.BlockSpec((D, tf), lambda i, j: (0, j)), pl.BlockSpec((D, tf), lambda i, j: (0, j)),
                  pl.BlockSpec((tf, D), lambda i, j: (j, 0))],
        out_specs=[pl.BlockSpec((tm, D), row), pl.BlockSpec((tm, D), row),
                   pl.BlockSpec((2, tm, tf), lambda i, j: (0, i, j)),
                   pl.BlockSpec((8, D), lambda i, j: (0, 0))],
        out_shape=[jax.ShapeDtypeStruct((S, D), F32), jax.ShapeDtypeStruct((S, D), BF16),
                   jax.ShapeDtypeStruct((2, S, FF), BF16), jax.ShapeDtypeStruct((8, D), F32)],
        scratch_shapes=[pltpu.VMEM((tm, D), BF16), pltpu.VMEM((tm, D), F32)],
        compiler_params=_cp("arbitrary", "arbitrary"),
    )(dxo, x, f, g, u, vec, w_gu, w_u, w_dn)


def mm_tn(a, b, name, tm, tn, tk, out_dtype=BF16):
    S, M = a.shape
    if b.ndim == 3:
        G, _, Nf = b.shape
    else:
        G, Nf = 1, b.shape[1]
    N = G * Nf
    tk = min(tk, S)
    nbf = Nf // tn
    nk = S // tk

    def body(a_ref, b_ref, o_ref, acc):
        k = pl.program_id(2)

        @pl.when(k == 0)
        def _():
            acc[...] = jnp.zeros_like(acc)

        acc[...] += _dot_tn(a_ref[...], b_ref[...])

        @pl.when(k == nk - 1)
        def _():
            o_ref[...] = acc[...].astype(out_dtype)

    if b.ndim == 3:
        b_spec = pl.BlockSpec((None, tk, tn), lambda i, j, k: (j // nbf, k, j % nbf))
    else:
        b_spec = pl.BlockSpec((tk, tn), lambda i, j, k: (k, j))
    return pl.pallas_call(
        body, name=name, grid=(M // tm, N // tn, nk),
        in_specs=[pl.BlockSpec((tk, tm), lambda i, j, k: (k, i)), b_spec],
        out_specs=pl.BlockSpec((tm, tn), lambda i, j, k: (i, j)),
        out_shape=jax.ShapeDtypeStruct((M, N), out_dtype),
        scratch_shapes=[pltpu.VMEM((tm, tn), F32)],
        compiler_params=_cp("parallel", "parallel", "arbitrary"),
    )(a, b)


def proj_fwd(x, vec, w_in, name, tm=1024, tn=512):
    S = x.shape[0]
    tm = min(tm, S)
    nq = 1536 // tn

    def body(x_ref, vec_ref, w_ref, h_ref, qkv_ref, rest_ref, hs):
        j = pl.program_id(1)

        @pl.when(j == 0)
        def _():
            h = _pre_norm(x_ref[...], vec_ref).astype(BF16)
            hs[...] = h
            h_ref[...] = h

        r = _dot(hs[...], w_ref[...])

        @pl.when(j < nq)
        def _():
            qkv_ref[...] = r.astype(BF16)

        @pl.when(j >= nq)
        def _():
            rest_ref[...] = r.astype(BF16)

    row = lambda i, j: (i, 0)
    return pl.pallas_call(
        body, name=name, grid=(S // tm, PW // tn),
        in_specs=[pl.BlockSpec((tm, D), row), pl.BlockSpec((8, D), lambda i, j: (0, 0)),
                  pl.BlockSpec((D, tn), lambda i, j: (0, j))],
        out_specs=[pl.BlockSpec((tm, D), row),
                   pl.BlockSpec((tm, tn), lambda i, j: (i, jnp.minimum(j, nq - 1))),
                   pl.BlockSpec((tm, tn), lambda i, j: (i, jnp.maximum(j - nq, 0)))],
        out_shape=[jax.ShapeDtypeStruct((S, D), BF16), jax.ShapeDtypeStruct((S, 1536), BF16),
                   jax.ShapeDtypeStruct((S, 4096), BF16)],
        scratch_shapes=[pltpu.VMEM((tm, D), BF16)],
        compiler_params=_cp("parallel", "arbitrary"),
    )(x, vec, w_in)


def proj_bwd(dq, dkv, dxr, d3, w_in, x, dxo, vec, name, tm=1024, tk=512):
    S = x.shape[0]
    tm = min(tm, S)
    nk = PW // tk

    def body(dq_ref, dkv_ref, dxr_ref, d3_ref, w_ref, x_ref, dxo_ref, vec_ref, dx_ref, vacc_ref, acc):
        i, j = pl.program_id(0), pl.program_id(1)

        @pl.when((i == 0) & (j == 0))
        def _():
            vacc_ref[...] = jnp.zeros_like(vacc_ref)

        @pl.when(j == 0)
        def _():
            acc[...] = _dot_nt(dq_ref[...], w_ref[...])

        @pl.when((j >= 1) & (j < 3))
        def _():
            acc[...] += _dot_nt(dkv_ref[...], w_ref[...])

        @pl.when((j >= 3) & (j < 5))
        def _():
            acc[...] += _dot_nt(dxr_ref[...], w_ref[...])

        @pl.when(j >= 5)
        def _():
            acc[...] += _dot_nt(d3_ref[...], w_ref[...])

        @pl.when(j == nk - 1)
        def _():
            dx_ref[...] = _pre_norm_bwd(acc[...], x_ref[...], dxo_ref[...], vec_ref, vacc_ref)

    row = lambda i, j: (i, 0)
    return pl.pallas_call(
        body, name=name, grid=(S // tm, nk),
        in_specs=[pl.BlockSpec((None, tm, tk), lambda i, j: (0, i, 0)),
                  pl.BlockSpec((None, tm, tk), lambda i, j: (jnp.clip(j - 1, 0, 1), i, 0)),
                  pl.BlockSpec((tm, tk), lambda i, j: (i, jnp.clip(j - 3, 0, 1))),
                  pl.BlockSpec((None, tm, tk), lambda i, j: (jnp.clip(j - 5, 0, 5) // 2, i, jnp.clip(j - 5, 0, 5) % 2)),
                  pl.BlockSpec((D, tk), lambda i, j: (0, j)),
                  pl.BlockSpec((tm, D), row), pl.BlockSpec((tm, D), row),
                  pl.BlockSpec((8, D), lambda i, j: (0, 0))],
        out_specs=[pl.BlockSpec((tm, D), row), pl.BlockSpec((8, D), lambda i, j: (0, 0))],
        out_shape=[jax.ShapeDtypeStruct((S, D), F32), jax.ShapeDtypeStruct((8, D), F32)],
        scratch_shapes=[pltpu.VMEM((tm, D), F32)],
        compiler_params=_cp("arbitrary", "arbitrary"),
    )(dq, dkv, dxr, d3, w_in, x, dxo, vec)


def _two_heads(v, lane):
    zero = jnp.zeros((), v.dtype)
    return jnp.concatenate([jnp.where(lane < 64, v, zero), jnp.where(lane >= 64, v, zero)], axis=0)


def _attn_probs(qm, ka, bias_h, i, grp):
    s = _dot_nt(qm, ka) + bias_h
    col = lax.broadcasted_iota(jnp.int32, s.shape, 1)
    first_key = jnp.where(i == 0, 512 - 128 * grp, 0)
    s = jnp.where(col >= first_key, s, NEG)
    e = jnp.exp(s - jnp.max(s, axis=-1, keepdims=True))
    return e * (1.0 / jnp.sum(e, axis=-1, keepdims=True))


def attn_fwd(qkv, bias, name):
    S = qkv.shape[0]
    nb = S // TQ

    def body(q_ref, kp_ref, kc_ref, vp_ref, vc_ref, b_ref, o_ref, kw, vw):
        i = pl.program_id(1)
        kw[0:TQ, :] = kp_ref[...]
        kw[TQ:2 * TQ, :] = kc_ref[...]
        vw[0:TQ, :] = vp_ref[...]
        vw[TQ:2 * TQ, :] = vc_ref[...]
        lane = lax.broadcasted_iota(jnp.int32, (1, HP), 1)

        def group(a, carry):
            r0 = pl.multiple_of(a * 128, 128)
            qa = q_ref[pl.ds(r0, 128), :] * jnp.asarray(0.125, BF16)
            ka = kw[pl.ds(r0, WIN), :]
            va = vw[pl.ds(r0, WIN), :]
            p = _attn_probs(_two_heads(qa, lane), ka, b_ref[...], i, a)
            o2 = _dot(p.astype(BF16), va)
            o_ref[pl.ds(r0, 128), :] = jnp.where(lane < 64, o2[0:128], o2[128:256]).astype(BF16)
            return carry

        lax.fori_loop(0, TQ // 128, group, 0, unroll=True)

    prev = lambda h, i: (jnp.maximum(i - 1, 0), 0)
    return pl.pallas_call(
        body, name=name, grid=(4, nb),
        in_specs=[pl.BlockSpec((TQ, HP), lambda h, i: (i, h)),
                  pl.BlockSpec((TQ, HP), lambda h, i: (jnp.maximum(i - 1, 0), 4 + h)),
                  pl.BlockSpec((TQ, HP), lambda h, i: (i, 4 + h)),
                  pl.BlockSpec((TQ, HP), lambda h, i: (jnp.maximum(i - 1, 0), 8 + h)),
                  pl.BlockSpec((TQ, HP), lambda h, i: (i, 8 + h)),
                  pl.BlockSpec((None, 256, WIN), lambda h, i: (h, 0, 0))],
        out_specs=pl.BlockSpec((TQ, HP), lambda h, i: (i, h)),
        out_shape=jax.ShapeDtypeStruct((S, 512), BF16),
        scratch_shapes=[pltpu.VMEM((2 * TQ, HP), BF16), pltpu.VMEM((2 * TQ, HP), BF16)],
        compiler_params=_cp("parallel", "arbitrary"),
    )(qkv, qkv, qkv, qkv, qkv, bias)


def attn_bwd(qkv, do, bias, name):
    S = qkv.shape[0]
    nb = S // TQ

    def body(q_ref, kp_ref, kc_ref, vp_ref, vc_ref, do_ref, b_ref, dqkv_ref, db_ref, dkv_ref, kw, vw, ak, av):
        i = pl.program_id(1)

        @pl.when(i == 0)
        def _():
            db_ref[...] = jnp.zeros_like(db_ref)
            ak[...] = jnp.zeros_like(ak)
            av[...] = jnp.zeros_like(av)

        @pl.when(i > 0)
        def _():
            ak[0:TQ, :] = ak[TQ:2 * TQ, :]
            av[0:TQ, :] = av[TQ:2 * TQ, :]
            ak[TQ:2 * TQ, :] = jnp.zeros((TQ, HP), F32)
            av[TQ:2 * TQ, :] = jnp.zeros((TQ, HP), F32)

        @pl.when(i < nb)
        def _():
            kw[0:TQ, :] = kp_ref[...]
            kw[TQ:2 * TQ, :] = kc_ref[...]
            vw[0:TQ, :] = vp_ref[...]
            vw[TQ:2 * TQ, :] = vc_ref[...]
            lane = lax.broadcasted_iota(jnp.int32, (1, HP), 1)

            def group(a, carry):
                r0 = pl.multiple_of(a * 128, 128)
                q2 = _two_heads(q_ref[pl.ds(r0, 128), :] * jnp.asarray(0.125, BF16), lane)
                do2 = _two_heads(do_ref[pl.ds(r0, 128), :], lane)
                ka = kw[pl.ds(r0, WIN), :]
                va = vw[pl.ds(r0, WIN), :]
                p = _attn_probs(q2, ka, b_ref[...], i, a)
                dp = _dot_nt(do2, va)
                ds = p * (dp - jnp.sum(p * dp, axis=-1, keepdims=True))
                db_ref[...] += ds
                dsb = ds.astype(BF16)
                dq2 = _dot(dsb, ka)
                ak[pl.ds(r0, WIN), :] += _dot_tn(dsb, q2)
                av[pl.ds(r0, WIN), :] += _dot_tn(p.astype(BF16), do2)
                dq = jnp.where(lane < 64, dq2[0:128], dq2[128:256])
                dqkv_ref[0, pl.ds(r0, 128), :] = (dq * 0.125).astype(BF16)
                return carry

            lax.fori_loop(0, TQ // 128, group, 0, unroll=True)

        @pl.when(i > 0)
        def _():
            dkv_ref[0] = ak[0:TQ, :].astype(BF16)
            dkv_ref[1] = av[0:TQ, :].astype(BF16)

    cur = lambda i: jnp.minimum(i, nb - 1)
    prv = lambda i: jnp.clip(i - 1, 0, nb - 1)
    dq, db, dkv = pl.pallas_call(
        body, name=name, grid=(4, nb + 1),
        in_specs=[pl.BlockSpec((TQ, HP), lambda h, i: (cur(i), h)),
                  pl.BlockSpec((TQ, HP), lambda h, i: (prv(i), 4 + h)),
                  pl.BlockSpec((TQ, HP), lambda h, i: (cur(i), 4 + h)),
                  pl.BlockSpec((TQ, HP), lambda h, i: (prv(i), 8 + h)),
                  pl.BlockSpec((TQ, HP), lambda h, i: (cur(i), 8 + h)),
                  pl.BlockSpec((TQ, HP), lambda h, i: (cur(i), h)),
                  pl.BlockSpec((None, 256, WIN), lambda h, i: (h, 0, 0))],
        out_specs=[pl.BlockSpec((1, TQ, HP), lambda h, i: (0, cur(i), h)),
                   pl.BlockSpec((None, 256, WIN), lambda h, i: (h, 0, 0)),
                   pl.BlockSpec((2, TQ, HP), lambda h, i: (0, prv(i), h))],
        out_shape=[jax.ShapeDtypeStruct((1, S, 512), BF16), jax.ShapeDtypeStruct((4, 256, WIN), F32),
                   jax.ShapeDtypeStruct((2, S, 512), BF16)],
        scratch_shapes=[pltpu.VMEM((2 * TQ, HP), BF16), pltpu.VMEM((2 * TQ, HP), BF16),
                        pltpu.VMEM((2 * TQ, HP), F32), pltpu.VMEM((2 * TQ, HP), F32)],
        compiler_params=_cp("parallel", "arbitrary"),
    )(qkv, qkv, qkv, qkv, qkv, do, bias)
    return dq, db, dkv


def bias_grad(db, name):
    def body(db_ref, o_ref):
        r = lax.broadcasted_iota(jnp.int32, (128, 128), 0)
        c = lax.broadcasted_iota(jnp.int32, (128, 128), 1)
        flip = (r + c == 127).astype(BF16)
        lane = lax.broadcasted_iota(jnp.int32, (16, 384), 1)
        src = lax.broadcasted_iota(jnp.int32, (128, 384), 0)
        dst = lax.broadcasted_iota(jnp.int32, (128, 384), 1)

        def split_dot(v, m):
            hi = v.astype(BF16)
            r1 = v - hi.astype(F32)
            mid = r1.astype(BF16)
            lo = (r1 - mid.astype(F32)).astype(BF16)
            return _dot(hi, m) + _dot(mid, m) + _dot(lo, m)

        def diag_sums(w):
            y = pltpu.roll(split_dot(w, flip), 0, 1, stride=1, stride_axis=0)
            return jnp.broadcast_to(_colsum(y), (16, 128))

        w4 = db_ref[0, :, 512:640]
        w3 = db_ref[0, :, 384:512]
        far = jnp.sum(db_ref[0, :, 0:384]) + jnp.sum(jnp.where(r >= c, w3, 0.0))
        lo4 = diag_sums(jnp.where(r >= c, w4, 0.0))
        up4 = diag_sums(jnp.where(r < c, w4, 0.0))
        up3 = diag_sums(jnp.where(r < c, w3, 0.0))
        p_lo4 = (dst == 128 + (src + 1) % 128).astype(BF16)
        p_up4 = ((dst == src + 1) & (src < 127)).astype(BF16)
        p_up3 = ((dst == src + 129) & (src < 127)).astype(BF16)
        out = split_dot(lo4, p_lo4) + split_dot(up4, p_up4) + split_dot(up3, p_up3)
        o_ref[0] = out + jnp.where(lane == 256, far, 0.0)

    return pl.pallas_call(
        body, name=name, grid=(8,),
        in_specs=[pl.BlockSpec((1, 128, WIN), lambda h: (h, 0, 0))],
        out_specs=pl.BlockSpec((1, 16, 384), lambda h: (h, 0, 0)),
        out_shape=jax.ShapeDtypeStruct((8, 16, 384), F32),
        compiler_params=_cp("parallel"),
    )(db)[:, 0, :]


LT = 256
LC = 512


def _lru_gates(xs, pv_ref, wa_ref, wx_ref, tl):
    xc = (pv_ref[4:5, :] + pv_ref[3:4, :] * xs[pl.ds(8, tl), :] + pv_ref[2:3, :] * xs[pl.ds(7, tl), :]
          + pv_ref[1:2, :] * xs[pl.ds(6, tl), :] + pv_ref[0:1, :] * xs[pl.ds(5, tl), :])
    xcb = xc.astype(BF16)
    pa = jnp.concatenate([_dot(xcb[:, 0:256], wa_ref[0]), _dot(xcb[:, 256:512], wa_ref[1])], axis=1)
    px = jnp.concatenate([_dot(xcb[:, 0:256], wx_ref[0]), _dot(xcb[:, 256:512], wx_ref[1])], axis=1)
    r = _sigmoid(pa + pv_ref[5:6, :])
    ig = _sigmoid(px + pv_ref[6:7, :])
    z = -pv_ref[7:8, :]
    sp = jnp.maximum(z, 0.0) + jnp.log1p(jnp.exp(-jnp.abs(z)))
    log_a = (-LRU_C * r) * sp
    a = jnp.exp(log_a)
    s = jnp.tanh(-log_a) * (1.0 + a * a)
    inv_mult = lax.rsqrt(s)
    mult = jnp.where(s > 0.0, s * inv_mult, 0.0)
    return xc, xcb, r, ig, sp, a, mult, inv_mult


def lru_fwd(rest, pvec, wa, wx, name):
    S = rest.shape[0]
    tl = min(LT, S)
    nt = S // tl

    def body(xr_ref, halo_ref, yr_ref, pv_ref, wa_ref, wx_ref, h_ref, hg_ref, xs, a_s, u_s, h_s, carry):
        ti = pl.program_id(1)

        @pl.when(ti == 0)
        def _():
            carry[...] = jnp.zeros_like(carry)

        xs[0:8, :] = jnp.where(ti > 0, halo_ref[8:16, :].astype(F32), 0.0)
        xs[pl.ds(8, tl), :] = xr_ref[...].astype(F32)
        xc, _, _, ig, _, a, mult, _ = _lru_gates(xs, pv_ref, wa_ref, wx_ref, tl)
        a_s[...] = a
        u_s[...] = mult * (ig * xc)
        row = lax.broadcasted_iota(jnp.int32, (8, LC), 0)

        def blk(bi, c):
            o = pl.multiple_of(bi * 8, 8)
            av = a_s[pl.ds(o, 8), :]
            bv = u_s[pl.ds(o, 8), :]
            for d in (1, 2, 4):
                a_sh = pltpu.roll(av, d, 0)
                b_sh = pltpu.roll(bv, d, 0)
                m = row >= d
                bv = jnp.where(m, av * b_sh + bv, bv)
                av = jnp.where(m, av * a_sh, av)
            hv = bv + av * c
            h_s[pl.ds(o, 8), :] = hv
            return hv[7:8, :]

        carry[...] = lax.fori_loop(0, tl // 8, blk, carry[...])
        h = h_s[...]
        h_ref[...] = h
        hg_ref[...] = (h * _gelu(yr_ref[...].astype(F32))).astype(BF16)

    hb = tl // 16
    return pl.pallas_call(
        body, name=name, grid=(2, nt),
        in_specs=[pl.BlockSpec((tl, LC), lambda c, t: (t, c)),
                  pl.BlockSpec((16, LC), lambda c, t: (jnp.maximum(t * hb - 1, 0), c)),
                  pl.BlockSpec((tl, LC), lambda c, t: (t, 2 + c)),
                  pl.BlockSpec((8, LC), lambda c, t: (0, c)),
                  pl.BlockSpec((2, 256, 256), lambda c, t: (c, 0, 0)),
                  pl.BlockSpec((2, 256, 256), lambda c, t: (c, 0, 0))],
        out_specs=[pl.BlockSpec((tl, LC), lambda c, t: (t, c)), pl.BlockSpec((tl, LC), lambda c, t: (t, c))],
        out_shape=[jax.ShapeDtypeStruct((S, D), F32), jax.ShapeDtypeStruct((S, D), BF16)],
        scratch_shapes=[pltpu.VMEM((tl + 8, LC), F32), pltpu.VMEM((tl, LC), F32), pltpu.VMEM((tl, LC), F32),
                        pltpu.VMEM((tl, LC), F32), pltpu.VMEM((1, LC), F32)],
        compiler_params=_cp("parallel", "arbitrary"),
    )(rest, rest, rest, pvec, wa, wx)


def lru_bwd(dh, h, rest, pvec, wa, wx, name):
    S = rest.shape[0]
    tl = min(LT, S)
    nt = S // tl

    def body(dh_ref, h_ref, hhalo_ref, xr_ref, xhalo_ref, pv_ref, wa_ref, wx_ref,
             dxr_ref, vacc_ref, dwa_ref, dwx_ref,
             xs, hs, a_s, ash_s, b_s, lam_s, dxe, anext, lnext, dxnext):
        ti = pl.program_id(1)
        tr = nt - 1 - ti

        @pl.when(ti == 0)
        def _():
            anext[...] = jnp.zeros_like(anext)
            lnext[...] = jnp.zeros_like(lnext)
            dxnext[...] = jnp.zeros_like(dxnext)
            vacc_ref[...] = jnp.zeros_like(vacc_ref)
            dwa_ref[...] = jnp.zeros_like(dwa_ref)
            dwx_ref[...] = jnp.zeros_like(dwx_ref)

        xs[0:8, :] = jnp.where(tr > 0, xhalo_ref[8:16, :].astype(F32), 0.0)
        xs[pl.ds(8, tl), :] = xr_ref[...].astype(F32)
        xc, xcb, r, ig, sp, a, mult, inv_mult = _lru_gates(xs, pv_ref, wa_ref, wx_ref, tl)

        a_s[pl.ds(0, tl), :] = a
        a_s[pl.ds(tl, 8), :] = jnp.broadcast_to(anext[...], (8, LC))
        ash_s[...] = a_s[pl.ds(1, tl), :]
        b_s[...] = dh_ref[...]
        row = lax.broadcasted_iota(jnp.int32, (8, LC), 0)

        def blk(k, c):
            o = pl.multiple_of((tl // 8 - 1 - k) * 8, 8)
            av = ash_s[pl.ds(o, 8), :]
            bv = b_s[pl.ds(o, 8), :]
            for d in (1, 2, 4):
                a_sh = pltpu.roll(av, 8 - d, 0)
                b_sh = pltpu.roll(bv, 8 - d, 0)
                m = row < 8 - d
                bv = jnp.where(m, bv + av * b_sh, bv)
                av = jnp.where(m, av * a_sh, av)
            lv = bv + av * c
            lam_s[pl.ds(o, 8), :] = lv
            return lv[0:1, :]

        lnext[...] = lax.fori_loop(0, tl // 8, blk, lnext[...])
        anext[...] = a[0:1, :]
        lam = lam_s[...]

        hs[0:8, :] = jnp.where(tr > 0, hhalo_ref[...], 0.0)
        hs[pl.ds(8, tl), :] = h_ref[...]
        d_a = lam * hs[pl.ds(7, tl), :]
        d_mult = lam * (ig * xc)
        d_ig = lam * mult * xc
        dxc = lam * mult * ig
        d_log_a = d_a * a - d_mult * (a * a) * inv_mult
        d_r = d_log_a * (-LRU_C * sp)
        vacc_ref[7:8, :] += _colsum(d_log_a * (-LRU_C * r)) * (-_sigmoid(-pv_ref[7:8, :]))
        d_pa = d_r * r * (1.0 - r)
        d_px = d_ig * ig * (1.0 - ig)
        vacc_ref[5:6, :] += _colsum(d_pa)
        vacc_ref[6:7, :] += _colsum(d_px)
        dpa = d_pa.astype(BF16)
        dpx = d_px.astype(BF16)
        back = []
        for g in range(2):
            sl = slice(256 * g, 256 * g + 256)
            dwa_ref[g] += _dot_tn(xcb[:, sl], dpa[:, sl])
            dwx_ref[g] += _dot_tn(xcb[:, sl], dpx[:, sl])
            back.append(_dot_nt(dpa[:, sl], wa_ref[g]) + _dot_nt(dpx[:, sl], wx_ref[g]))
        dxc = dxc + jnp.concatenate(back, axis=1)
        vacc_ref[4:5, :] += _colsum(dxc)
        for k in range(4):
            vacc_ref[k:k + 1, :] += _colsum(dxc * xs[pl.ds(5 + k, tl), :])
        dxe[pl.ds(0, tl), :] = dxc
        dxe[pl.ds(tl, 8), :] = dxnext[...]
        dxr = (pv_ref[3:4, :] * dxc + pv_ref[2:3, :] * dxe[pl.ds(1, tl), :]
               + pv_ref[1:2, :] * dxe[pl.ds(2, tl), :] + pv_ref[0:1, :] * dxe[pl.ds(3, tl), :])
        dxr_ref[...] = dxr.astype(BF16)
        dxnext[...] = dxc[0:8, :]

    hb = tl // 8
    rev = lambda t: nt - 1 - t
    halo = lambda t: jnp.maximum(rev(t) * hb - 1, 0)
    big = lambda: pltpu.VMEM((tl + 8, LC), F32)
    til = lambda: pltpu.VMEM((tl, LC), F32)
    return pl.pallas_call(
        body, name=name, grid=(2, nt),
        in_specs=[pl.BlockSpec((tl, LC), lambda c, t: (rev(t), c)),
                  pl.BlockSpec((tl, LC), lambda c, t: (rev(t), c)),
                  pl.BlockSpec((8, LC), lambda c, t: (halo(t), c)),
                  pl.BlockSpec((tl, LC), lambda c, t: (rev(t), c)),
                  pl.BlockSpec((16, LC), lambda c, t: (jnp.maximum(rev(t) * (tl // 16) - 1, 0), c)),
                  pl.BlockSpec((8, LC), lambda c, t: (0, c)),
                  pl.BlockSpec((2, 256, 256), lambda c, t: (c, 0, 0)),
                  pl.BlockSpec((2, 256, 256), lambda c, t: (c, 0, 0))],
        out_specs=[pl.BlockSpec((tl, LC), lambda c, t: (rev(t), c)),
                   pl.BlockSpec((8, LC), lambda c, t: (0, c)),
                   pl.BlockSpec((2, 256, 256), lambda c, t: (c, 0, 0)),
                   pl.BlockSpec((2, 256, 256), lambda c, t: (c, 0, 0))],
        out_shape=[jax.ShapeDtypeStruct((S, D), BF16), jax.ShapeDtypeStruct((8, D), F32),
                   jax.ShapeDtypeStruct((4, 256, 256), F32), jax.ShapeDtypeStruct((4, 256, 256), F32)],
        scratch_shapes=[big(), big(), big(), til(), til(), til(), big(),
                        pltpu.VMEM((1, LC), F32), pltpu.VMEM((1, LC), F32), pltpu.VMEM((8, LC), F32)],
        compiler_params=_cp("parallel", "arbitrary"),
    )(dh, h, h, rest, rest, pvec, wa, wx)


def mix_out_fwd(x, ao, hg, rest, vec, w_att_o, w_rec_o, w_out, name, tm=256):
    S = x.shape[0]
    tm = min(tm, S)

    def body(x_ref, ao_ref, hg_ref, ga_ref, gr_ref, vec_ref, wa_ref, wr_ref, wo_ref,
             xo_ref, att_ref, rec_ref, mg_ref, f_ref):
        att = _dot(ao_ref[...], wa_ref[...])
        rec = _dot(hg_ref[...], wr_ref[...])
        att_ref[...] = att.astype(BF16)
        rec_ref[...] = rec.astype(BF16)
        mg = (_sigmoid(ga_ref[...].astype(F32)) * att + _sigmoid(gr_ref[...].astype(F32)) * rec).astype(BF16)
        mg_ref[...] = mg
        f = _dot(mg, wo_ref[...])
        f_ref[...] = f.astype(BF16)
        y = f * lax.rsqrt(_mean(f * f) + EPS) * vec_ref[1:2, :]
        xo_ref[...] = x_ref[...] + (1.0 * vec_ref[4:5, :]) * y

    row = lambda i: (i, 0)
    full = lambda r: pl.BlockSpec((r, D), lambda i: (0, 0))
    return pl.pallas_call(
        body, name=name, grid=(S // tm,),
        in_specs=[pl.BlockSpec((tm, D), row), pl.BlockSpec((tm, 512), row), pl.BlockSpec((tm, D), row),
                  pl.BlockSpec((tm, D), lambda i: (i, 2)), pl.BlockSpec((tm, D), lambda i: (i, 3)),
                  full(8), full(512), full(D), full(D)],
        out_specs=[pl.BlockSpec((tm, D), row)] * 5,
        out_shape=[jax.ShapeDtypeStruct((S, D), F32)] + [jax.ShapeDtypeStruct((S, D), BF16)] * 4,
        compiler_params=_cp("parallel"),
    )(x, ao, hg, rest, rest, vec, w_att_o, w_rec_o, w_out)


def mix_out_bwd(dxo, f, att, rec, rest, h, vec, w_att_o, w_rec_o, w_out, name, tm=256):
    S = dxo.shape[0]
    tm = min(tm, S)

    def body(dxo_ref, f_ref, att_ref, rec_ref, yr_ref, ga_ref, gr_ref, h_ref, vec_ref, wa_ref, wr_ref, wo_ref,
             df_ref, da_ref, dr_ref, dao_ref, dh_ref, d3_ref, vacc_ref):
        @pl.when(pl.program_id(0) == 0)
        def _():
            vacc_ref[...] = jnp.zeros_like(vacc_ref)

        df = _post_norm_bwd(dxo_ref[...], f_ref[...].astype(F32), 1.0, vec_ref, vacc_ref).astype(BF16)
        df_ref[...] = df
        dm = _dot_nt(df, wo_ref[...])
        sa = _sigmoid(ga_ref[...].astype(F32))
        sr = _sigmoid(gr_ref[...].astype(F32))
        d_att = (dm * sa).astype(BF16)
        d_rec = (dm * sr).astype(BF16)
        da_ref[...] = d_att
        dr_ref[...] = d_rec
        d3_ref[1] = (dm * att_ref[...].astype(F32) * (sa * (1.0 - sa))).astype(BF16)
        d3_ref[2] = (dm * rec_ref[...].astype(F32) * (sr * (1.0 - sr))).astype(BF16)
        dao_ref[...] = _dot_nt(d_att, wa_ref[...]).astype(BF16)
        d_hg = _dot_nt(d_rec, wr_ref[...])
        yr = yr_ref[...].astype(F32)
        t = jnp.tanh(_GK * (yr + 0.044715 * yr * yr * yr))
        dh_ref[...] = d_hg * (0.5 * yr * (1.0 + t))
        gelu_grad = 0.5 * (1.0 + t) + 0.5 * yr * (1.0 - t * t) * _GK * (1.0 + 3.0 * 0.044715 * yr * yr)
        d3_ref[0] = (d_hg * h_ref[...] * gelu_grad).astype(BF16)

    row = lambda i: (i, 0)
    full = lambda r: pl.BlockSpec((r, D), lambda i: (0, 0))
    return pl.pallas_call(
        body, name=name, grid=(S // tm,),
        in_specs=[pl.BlockSpec((tm, D), row)] * 4
        + [pl.BlockSpec((tm, D), lambda i: (i, 1)), pl.BlockSpec((tm, D), lambda i: (i, 2)),
           pl.BlockSpec((tm, D), lambda i: (i, 3)), pl.BlockSpec((tm, D), row),
           full(8), full(512), full(D), full(D)],
        out_specs=[pl.BlockSpec((tm, D), row)] * 3
        + [pl.BlockSpec((tm, 512), row), pl.BlockSpec((tm, D), row),
           pl.BlockSpec((3, tm, D), lambda i: (0, i, 0)), pl.BlockSpec((8, D), lambda i: (0, 0))],
        out_shape=[jax.ShapeDtypeStruct((S, D), BF16)] * 3
        + [jax.ShapeDtypeStruct((S, 512), BF16), jax.ShapeDtypeStruct((S, D), F32),
           jax.ShapeDtypeStruct((3, S, D), BF16), jax.ShapeDtypeStruct((8, D), F32)],
        compiler_params=_cp("arbitrary"),
    )(dxo, f, att, rec, rest, rest, rest, h, vec, w_att_o, w_rec_o, w_out)


def dw_in(h, dq, dkv, dxr, d3, name, tk=1024, tn=512):
    S = h.shape[0]
    tk = min(tk, S)
    nk = S // tk

    def body(h_ref, dq_ref, dkv_ref, dxr_ref, d3_ref, o_ref, acc):
        j, k = pl.program_id(0), pl.program_id(1)

        @pl.when(k == 0)
        def _():
            acc[...] = jnp.zeros_like(acc)

        @pl.when(j == 0)
        def _():
            acc[...] += _dot_tn(h_ref[...], dq_ref[...])

        @pl.when((j >= 1) & (j < 3))
        def _():
            acc[...] += _dot_tn(h_ref[...], dkv_ref[...])

        @pl.when((j >= 3) & (j < 5))
        def _():
            acc[...] += _dot_tn(h_ref[...], dxr_ref[...])

        @pl.when(j >= 5)
        def _():
            acc[...] += _dot_tn(h_ref[...], d3_ref[...])

        @pl.when(k == nk - 1)
        def _():
            o_ref[...] = acc[...].astype(BF16)

    use = lambda j, k, lo, hi: jnp.where((j >= lo) & (j < hi), k, 0)
    g3 = lambda j: jnp.clip(j - 5, 0, 5)
    return pl.pallas_call(
        body, name=name, grid=(PW // tn, nk),
        in_specs=[pl.BlockSpec((tk, D), lambda j, k: (k, 0)),
                  pl.BlockSpec((None, tk, tn), lambda j, k: (0, use(j, k, 0, 1), 0)),
                  pl.BlockSpec((None, tk, tn), lambda j, k: (jnp.clip(j - 1, 0, 1), use(j, k, 1, 3), 0)),
                  pl.BlockSpec((tk, tn), lambda j, k: (use(j, k, 3, 5), jnp.clip(j - 3, 0, 1))),
                  pl.BlockSpec((None, tk, tn), lambda j, k: (g3(j) // 2, use(j, k, 5, 11), g3(j) % 2))],
        out_specs=pl.BlockSpec((D, tn), lambda j, k: (0, j)),
        out_shape=jax.ShapeDtypeStruct((D, PW), BF16),
        scratch_shapes=[pltpu.VMEM((D, tn), F32)],
        compiler_params=_cp("parallel", "arbitrary"),
    )(h, dq, dkv, dxr, d3)


def ada_fwd(c_all, w_ada, b_ada, name, tn=768):
    n = w_ada.shape[1]

    def body(c_ref, w_ref, b_ref, o_ref):
        cv = c_ref[...]
        ca = (cv * _sigmoid(cv)).astype(BF16)
        o_ref[...] = _dot(ca, w_ref[...].astype(BF16)) + b_ref[...]

    return pl.pallas_call(
        body, name=name, grid=(n // tn,),
        in_specs=[pl.BlockSpec((8, D), lambda j: (0, 0)), pl.BlockSpec((D, tn), lambda j: (0, j)),
                  pl.BlockSpec((1, tn), lambda j: (0, j))],
        out_specs=pl.BlockSpec((8, tn), lambda j: (0, j)),
        out_shape=jax.ShapeDtypeStruct((8, n), F32),
        compiler_params=_cp("parallel"),
    )(c_all, w_ada, b_ada)


def ada_bwd(c_all_t, dmod, name, tn=768):
    n = dmod.shape[1]

    def body(c_ref, d_ref, o_ref):
        cv = c_ref[...]
        ca = (cv * _sigmoid(cv)).astype(BF16)
        o_ref[...] = _dot(ca, d_ref[...].astype(BF16))

    return pl.pallas_call(
        body, name=name, grid=(n // tn,),
        in_specs=[pl.BlockSpec((D, 128), lambda j: (0, 0)), pl.BlockSpec((128, tn), lambda j: (0, j))],
        out_specs=pl.BlockSpec((D, tn), lambda j: (0, j)),
        out_shape=jax.ShapeDtypeStruct((D, n), F32),
        compiler_params=_cp("parallel"),
    )(c_all_t, dmod)


def _row_tile(rows, cols, itemsize=4, budget=1536 * 1024):
    best = None
    for t in range(8, rows + 1, 8):
        if rows % t == 0 and t * cols * itemsize <= budget:
            best = t
    return rows if best is None else best


def sum_lead(parts, name, out_dtype=F32):
    n, R, C = parts.shape
    tr = _row_tile(R, C * n)

    def body(p_ref, o_ref):
        acc = p_ref[0].astype(F32)
        for k in range(1, n):
            acc = acc + p_ref[k].astype(F32)
        o_ref[...] = acc.astype(out_dtype)

    return pl.pallas_call(
        body, name=name, grid=(R // tr,),
        in_specs=[pl.BlockSpec((n, tr, C), lambda i: (0, i, 0))],
        out_specs=pl.BlockSpec((tr, C), lambda i: (i, 0)),
        out_shape=jax.ShapeDtypeStruct((R, C), out_dtype),
        compiler_params=_cp("parallel"),
    )(parts)


def adamw(w, g, m, v, name, emit_g=False):
    R, C = w.shape
    tr = _row_tile(R, C * 8, budget=8 * 1024 * 1024)

    def body(w_ref, g_ref, m_ref, v_ref, d_ref, mo_ref, vo_ref, *go_ref):
        gv = g_ref[...]
        if emit_g:
            go_ref[0][...] = gv
        mn = ADAM_B1 * m_ref[...] + (1.0 - ADAM_B1) * gv
        vn = ADAM_B2 * v_ref[...] + (1.0 - ADAM_B2) * (gv * gv)
        m_hat = mn / (1.0 - ADAM_B1 ** ADAM_STEP)
        v_hat = vn / (1.0 - ADAM_B2 ** ADAM_STEP)
        d_ref[...] = -ADAM_LR * (m_hat / (jnp.sqrt(v_hat) + ADAM_EPS) + ADAM_WD * w_ref[...])
        mo_ref[...] = mn
        vo_ref[...] = vn

    spec = pl.BlockSpec((tr, C), lambda i: (i, 0))
    return pl.pallas_call(
        body, name=name, grid=(R // tr,),
        in_specs=[spec] * 4, out_specs=[spec] * (4 if emit_g else 3),
        out_shape=[jax.ShapeDtypeStruct((R, C), F32)] * (4 if emit_g else 3),
        compiler_params=_cp("parallel"),
    )(w, g, m, v)


def _mesh_pos():
    return lax.axis_index("x"), lax.axis_index("y"), lax.axis_index("c")


def _other_chips(mx, my):
    return [(1 - mx, my), (mx, 1 - my), (1 - mx, 1 - my)]


def ag_small(x, name):
    R = x.shape[0]

    def body(x_ref, out_ref, send_sems, recv_sems, local_sem):
        mx, my, mc = _mesh_pos()
        me, sibling = (mx, my, mc), (mx, my, 1 - mc)
        chips = _other_chips(mx, my)

        def slot(px, py, pc):
            return out_ref.at[4 * px + 2 * py + pc]

        def copy(k, block, to, src=None):
            return pltpu.make_async_remote_copy(
                src_ref=slot(*block) if src is None else src, dst_ref=slot(*block),
                send_sem=send_sems.at[k], recv_sem=recv_sems.at[k], device_id=to, device_id_type=MESH)

        mine = pltpu.make_async_copy(x_ref, slot(*me), local_sem)
        mine.start()
        first = [copy(0, me, sibling, src=x_ref)]
        first += [copy(1 + j, me, (*chip, mc), src=x_ref) for j, chip in enumerate(chips)]
        for cp in first:
            cp.start()
        passed = [copy(4 + j, (*chip, mc), sibling) for j, chip in enumerate(chips)]
        for j, chip in enumerate(chips):
            copy(1 + j, (*chip, mc), me).wait_recv()
            passed[j].start()
        copy(0, sibling, me).wait_recv()
        for j, chip in enumerate(chips):
            copy(4 + j, (*chip, 1 - mc), me).wait_recv()
        for cp in first + passed:
            cp.wait_send()
        mine.wait()

    return pl.pallas_call(
        body, name=name,
        out_shape=jax.ShapeDtypeStruct((N_DEV, R, 128), F32),
        in_specs=[pl.BlockSpec(memory_space=pltpu.VMEM)],
        out_specs=pl.BlockSpec(memory_space=pltpu.VMEM),
        scratch_shapes=[pltpu.SemaphoreType.DMA((7,)), pltpu.SemaphoreType.DMA((7,)), pltpu.SemaphoreType.DMA],
        compiler_params=pltpu.CompilerParams(vmem_limit_bytes=VMEM_LIMIT),
    )(x)


BIG = (("ffn1_w_gu", "col", D, PW), ("ffn1_w_down", "row", FF, D), ("w_in", "col", D, PW),
       ("w_att_o", "col", 512, D), ("w_rec_o", "row", D, D), ("w_out", "row", D, D),
       ("ffn2_w_gu", "col", D, PW), ("ffn2_w_down", "row", FF, D))
NBIG = len(BIG)


def _shard_shape(kind, R, C):
    return (R, C // 4) if kind == "col" else (R // 4, C)


def _region(ref, kind, R, C, q, half, t, tr):
    sr, sc = _shard_shape(kind, R, C)
    if kind == "col":
        return ref.at[pl.ds(pl.multiple_of(half * (R // 2) + t * tr, 16), tr), pl.ds(q * sc, sc)]
    return ref.at[pl.ds(pl.multiple_of(q * sr + t * tr, 16), tr), pl.ds(half * (C // 2), C // 2)]


def ag_local(w, kind, R, C, p_arr, name, after=()):
    sr, sc = _shard_shape(kind, R, C)
    tr = _row_tile(sr, sc, budget=2 * 1024 * 1024)
    nt = sr // tr
    after = list(after)

    def body(p_ref, w_ref, *rest):
        rest[-1][...] = w_ref[...].astype(BF16)

    if kind == "col":
        o_spec = pl.BlockSpec((tr, sc), lambda i, p: (i, p[0]))
    else:
        o_spec = pl.BlockSpec((tr, sc), lambda i, p: (p[0] * nt + i, 0))
    return pl.pallas_call(
        body, name=name,
        grid_spec=pltpu.PrefetchScalarGridSpec(
            num_scalar_prefetch=1, grid=(nt,),
            in_specs=[pl.BlockSpec((tr, sc), lambda i, p: (i, 0))] + [ANY] * len(after), out_specs=o_spec),
        out_shape=jax.ShapeDtypeStruct((R, C), BF16),
        compiler_params=_cp("parallel"),
    )(p_arr, w, *after)


HBM_SPEC = pl.BlockSpec(memory_space=pltpu.HBM)
SEM_SPEC = pl.BlockSpec(memory_space=pltpu.SEMAPHORE)


def _ag_sems(geoms):
    return sum(6 if both else 3 for (_, _, _, both) in geoms)


def _ag_copies(fulls, geoms, ssem, rsem, mx, my, mc, q, h):
    chips = _other_chips(mx, my)
    out, base = [], 0
    for w, (kind, R, C, both) in enumerate(geoms):
        sr, sc = _shard_shape(kind, R, C)
        hr = sr // 2 if kind == "col" else sr
        reg = _region(fulls[w], kind, R, C, q, h, 0, hr)
        out.append([pltpu.make_async_remote_copy(
            src_ref=reg, dst_ref=reg, send_sem=ssem.at[base + 3 * t + k], recv_sem=rsem.at[base + 3 * t + k],
            device_id=(*chips[k], mc if t == 0 else 1 - mc), device_id_type=MESH)
            for t in range(2 if both else 1) for k in range(3)])
        base += 6 if both else 3
    return out


def ag_start(fulls, geoms, after, name):
    n = len(fulls)
    after = list(after)
    m = len(after)

    def body(*refs):
        ssem, rsem = refs[n + m:n + m + 2]
        outs, token = refs[n + m + 2:2 * n + m + 2], refs[2 * n + m + 2]
        mx, my, mc = _mesh_pos()
        p = 2 * mx + my
        col = [w for w, g in enumerate(geoms) if g[0] == "col"]
        row = [w for w, g in enumerate(geoms) if g[0] == "row"]
        for q in range(4):
            @pl.when(p == q)
            def _(q=q):
                cps = _ag_copies(outs, geoms, ssem, rsem, mx, my, mc, q, mc)
                for w in col:
                    for cp in cps[w]:
                        cp.start()
        for h in range(2):
            @pl.when(mc == h)
            def _(h=h):
                cps = _ag_copies(outs, geoms, ssem, rsem, mx, my, mc, p, h)
                for w in row:
                    for cp in cps[w]:
                        cp.start()
        token[...] = jnp.zeros_like(token)

    res = pl.pallas_call(
        body, name=name,
        out_shape=[pltpu.SemaphoreType.DMA((_ag_sems(geoms),)), pltpu.SemaphoreType.DMA((_ag_sems(geoms),))]
        + [pltpu.HBM(a.shape, a.dtype) for a in fulls] + [jax.ShapeDtypeStruct((8, 128), F32)],
        in_specs=[HBM_SPEC] * n + [ANY] * m,
        out_specs=[SEM_SPEC, SEM_SPEC] + [HBM_SPEC] * n + [pl.BlockSpec(memory_space=pltpu.VMEM)],
        input_output_aliases={w: 2 + w for w in range(n)},
        compiler_params=pltpu.CompilerParams(has_side_effects=pltpu.SideEffectType.DATAFLOW_SIDE_EFFECTING),
    )(*[pltpu.with_memory_space_constraint(a, pltpu.HBM) for a in fulls], *after)
    return res[0], res[1], list(res[2:2 + n]), res[2 + n]


def ag_wait(fulls, geoms, ssem, rsem, after, name):
    n = len(fulls)

    def body(*refs):
        ins, ssem_ref, rsem_ref = refs[:n], refs[n], refs[n + 1]
        mx, my, mc = _mesh_pos()
        for cps in _ag_copies(ins, geoms, ssem_ref, rsem_ref, mx, my, mc, 0, 0):
            for cp in cps:
                cp.wait_send()
                cp.wait_recv()

    return list(pl.pallas_call(
        body, name=name,
        out_shape=[pltpu.HBM(a.shape, a.dtype) for a in fulls],
        in_specs=[HBM_SPEC] * n + [SEM_SPEC, SEM_SPEC, ANY],
        out_specs=[HBM_SPEC] * n,
        input_output_aliases={w: w for w in range(n)},
        compiler_params=pltpu.CompilerParams(has_side_effects=pltpu.SideEffectType.DATAFLOW_SIDE_EFFECTING),
    )(*fulls, ssem, rsem, after))


def ag_forward(full, kind, R, C, name):
    sr, sc = _shard_shape(kind, R, C)
    hr, hc = (sr // 2, sc) if kind == "col" else (sr, sc // 2)
    tr = _row_tile(hr, hc, itemsize=2, budget=512 * 1024)
    nt = hr // tr

    total = 3 * nt

    def body(src_ref, full_ref, stage, lsem, ssem, rsem):
        step = pl.program_id(0) * nt + pl.program_id(1)
        par = step % 2
        mx, my, mc = _mesh_pos()

        def load(s, q, h, t):
            return pltpu.make_async_copy(_region(src_ref, kind, R, C, q, h, t, tr), stage.at[s], lsem.at[s])

        def push(s, q, h, t):
            return pltpu.make_async_remote_copy(src_ref=stage.at[s], dst_ref=_region(full_ref, kind, R, C, q, h, t, tr),
                                                send_sem=ssem.at[s], recv_sem=rsem, device_id=(mx, my, 1 - mc),
                                                device_id_type=MESH)

        def for_tile(stp, fn):
            q_k = _partner_chip(stp // nt, 2 * mx + my)
            if kind == "col":
                for q in range(4):
                    @pl.when(q_k == q)
                    def _(q=q):
                        fn(q, mc, stp % nt)
            else:
                for h in range(2):
                    @pl.when(mc == h)
                    def _(h=h):
                        fn(q_k, h, stp % nt)

        @pl.when(step == 0)
        def _():
            for_tile(step, lambda q, h, t: load(0, q, h, t).start())

        load(par, 0, 0, 0).wait()
        for_tile(step, lambda q, h, t: push(par, q, h, t).start())

        @pl.when(step + 1 < total)
        def _():
            @pl.when(step >= 1)
            def _():
                push(1 - par, 0, 0, 0).wait_send()
            for_tile(step + 1, lambda q, h, t: load(1 - par, q, h, t).start())

        @pl.when(step == total - 1)
        def _():
            push(par, 0, 0, 0).wait_send()
            push(1 - par, 0, 0, 0).wait_send()
            three = full_ref.at[pl.ds(0, hr), pl.ds(0, 3 * hc)] if kind == "col" else full_ref.at[pl.ds(0, 3 * hr), pl.ds(0, hc)]
            pltpu.make_async_remote_copy(src_ref=three, dst_ref=three, send_sem=ssem.at[0], recv_sem=rsem,
                                         device_id=(mx, my, 1 - mc), device_id_type=MESH).wait_recv()

    return pl.pallas_call(
        body, name=name, grid=(3, nt),
        in_specs=[ANY], out_specs=ANY,
        out_shape=jax.ShapeDtypeStruct((R, C), BF16),
        scratch_shapes=[pltpu.VMEM((2, tr, hc), BF16), pltpu.SemaphoreType.DMA((2,)), pltpu.SemaphoreType.DMA((2,)),
                        pltpu.SemaphoreType.DMA],
        input_output_aliases={0: 0},
        compiler_params=_cp("arbitrary", "arbitrary"),
    )(full)


def _half_shape(kind, R, C):
    return (R // 2, C) if kind == "col" else (R, C // 2)


def _piece_shape(kind, R, C):
    return (R // 2, C // 4) if kind == "col" else (R // 4, C // 2)


def pair_push(g, kind, c_arr, name):
    R, C = g.shape
    hr, hc = _half_shape(kind, R, C)
    tr = _row_tile(hr, hc, itemsize=2, budget=1024 * 1024)
    nt = hr // tr

    def body(c_ref, g_ref, out_ref, stage, ssem, rsem):
        i = pl.program_id(0)
        slot = i % 2
        mx, my, mc = _mesh_pos()

        def push(s, t):
            return pltpu.make_async_remote_copy(
                src_ref=stage.at[s], dst_ref=out_ref.at[pl.ds(pl.multiple_of(t * tr, 16), tr)],
                send_sem=ssem.at[s], recv_sem=rsem, device_id=(mx, my, 1 - mc), device_id_type=MESH)

        @pl.when(i >= 2)
        def _():
            push(slot, 0).wait_send()

        stage[slot] = g_ref[...]
        push(slot, i).start()

        @pl.when(i == nt - 1)
        def _():
            push(slot, 0).wait_send()
            if nt >= 2:
                push(1 - slot, 0).wait_send()
            pltpu.make_async_remote_copy(src_ref=out_ref, dst_ref=out_ref, send_sem=ssem.at[0], recv_sem=rsem,
                                         device_id=(mx, my, 1 - mc), device_id_type=MESH).wait_recv()

    if kind == "col":
        g_spec = pl.BlockSpec((tr, hc), lambda i, c: ((1 - c[0]) * nt + i, 0))
    else:
        g_spec = pl.BlockSpec((tr, hc), lambda i, c: (i, 1 - c[0]))
    return pl.pallas_call(
        body, name=name,
        grid_spec=pltpu.PrefetchScalarGridSpec(
            num_scalar_prefetch=1, grid=(nt,), in_specs=[g_spec], out_specs=ANY,
            scratch_shapes=[pltpu.VMEM((2, tr, hc), BF16), pltpu.SemaphoreType.DMA((2,)), pltpu.SemaphoreType.DMA]),
        out_shape=jax.ShapeDtypeStruct((hr, hc), BF16),
        compiler_params=_cp("arbitrary"),
    )(c_arr, g)


def _partner_chip(k, p):
    return p ^ jnp.where(k == 0, 2, jnp.where(k == 1, 1, jnp.where(k == 2, 3, 0)))


def pair_add(g, got, kind, cp_arr, name):
    R, C = g.shape
    pr, pc = _piece_shape(kind, R, C)
    tr = _row_tile(pr, pc, itemsize=2, budget=1024 * 1024)
    nt = pr // tr

    def body(cp_ref, g_ref, got_ref, ps_ref, rb_ref):
        tile = (g_ref[...].astype(F32) + got_ref[...].astype(F32)).astype(BF16)
        ps_ref[...] = tile

        @pl.when(pl.program_id(1) == cp_ref[1])
        def _():
            rb_ref[...] = tile

    if kind == "col":
        g_spec = pl.BlockSpec((tr, pc), lambda i, q, cp: (cp[0] * nt + i, q))
        got_spec = pl.BlockSpec((tr, pc), lambda i, q, cp: (i, q))
    else:
        g_spec = pl.BlockSpec((tr, pc), lambda i, q, cp: (q * nt + i, cp[0]))
        got_spec = pl.BlockSpec((tr, pc), lambda i, q, cp: (q * nt + i, 0))
    return pl.pallas_call(
        body, name=name,
        grid_spec=pltpu.PrefetchScalarGridSpec(
            num_scalar_prefetch=1, grid=(nt, 4), in_specs=[g_spec, got_spec],
            out_specs=[pl.BlockSpec((None, tr, pc), lambda i, q, cp: (q, i, 0)),
                       pl.BlockSpec((None, tr, pc), lambda i, q, cp: (cp[1], i, 0))]),
        out_shape=[jax.ShapeDtypeStruct((4, pr, pc), BF16)] * 2,
        compiler_params=_cp("arbitrary", "arbitrary"),
    )(cp_arr, g, got)


def _rs_copies(ps, rb, ssem, rsem, mx, my, mc):
    p = 2 * mx + my
    out = []
    for w in range(len(ps)):
        for k, chip in enumerate(_other_chips(mx, my)):
            out.append(pltpu.make_async_remote_copy(
                src_ref=ps[w].at[2 * chip[0] + chip[1]], dst_ref=rb[w].at[p], send_sem=ssem.at[3 * w + k],
                recv_sem=rsem.at[3 * w + k], device_id=(*chip, mc), device_id_type=MESH))
    return out


def rs_start(ps, rb, after, name):
    n = len(ps)
    after = list(after)
    m = len(after)

    def body(*refs):
        ssem, rsem = refs[2 * n + m:2 * n + m + 2]
        ps_o = refs[2 * n + m + 2:3 * n + m + 2]
        rb_o = refs[3 * n + m + 2:4 * n + m + 2]
        token = refs[4 * n + m + 2]
        for cp in _rs_copies(ps_o, rb_o, ssem, rsem, *_mesh_pos()):
            cp.start()
        token[...] = jnp.zeros_like(token)

    both = list(ps) + list(rb)
    res = pl.pallas_call(
        body, name=name,
        out_shape=[pltpu.SemaphoreType.DMA((3 * n,)), pltpu.SemaphoreType.DMA((3 * n,))]
        + [pltpu.HBM(a.shape, a.dtype) for a in both] + [jax.ShapeDtypeStruct((8, 128), F32)],
        in_specs=[HBM_SPEC] * (2 * n) + [ANY] * m,
        out_specs=[SEM_SPEC, SEM_SPEC] + [HBM_SPEC] * (2 * n) + [pl.BlockSpec(memory_space=pltpu.VMEM)],
        input_output_aliases={w: 2 + w for w in range(2 * n)},
        compiler_params=pltpu.CompilerParams(has_side_effects=pltpu.SideEffectType.DATAFLOW_SIDE_EFFECTING),
    )(*[pltpu.with_memory_space_constraint(a, pltpu.HBM) for a in both], *after)
    return res[0], res[1], list(res[2:2 + n]), list(res[2 + n:2 + 2 * n]), res[2 + 2 * n]


def rs_wait(ps, rb, ssem, rsem, after, name):
    n = len(ps)
    after = list(after)
    m = len(after)

    def body(*refs):
        ps_i, rb_i = refs[:n], refs[n:2 * n]
        ssem_ref, rsem_ref = refs[2 * n], refs[2 * n + 1]
        for cp in _rs_copies(ps_i, rb_i, ssem_ref, rsem_ref, *_mesh_pos()):
            cp.wait_send()
            cp.wait_recv()

    both = list(ps) + list(rb)
    res = pl.pallas_call(
        body, name=name,
        out_shape=[pltpu.HBM(a.shape, a.dtype) for a in both],
        in_specs=[HBM_SPEC] * (2 * n) + [SEM_SPEC, SEM_SPEC] + [ANY] * m,
        out_specs=[HBM_SPEC] * (2 * n),
        input_output_aliases={w: w for w in range(2 * n)},
        compiler_params=pltpu.CompilerParams(has_side_effects=pltpu.SideEffectType.DATAFLOW_SIDE_EFFECTING),
    )(*both, ssem, rsem, *after)
    return list(res[n:])


def sum_share(parts, kind, R, C, name):
    _, pr, pc = parts.shape
    sr, sc = _shard_shape(kind, R, C)
    tr = _row_tile(pr, pc * 4, budget=4 * 1024 * 1024)
    nt = pr // tr

    def body(p_ref, fin_ref, stage, lsem, ssem, rsem):
        i = pl.program_id(0)
        slot = i % 2
        mx, my, mc = _mesh_pos()

        def region(h, t):
            r0 = pl.multiple_of(t * tr, 8)
            if kind == "col":
                return fin_ref.at[pl.ds(pl.multiple_of(h * pr + r0, 8), tr)]
            return fin_ref.at[pl.ds(r0, tr), pl.ds(h * pc, pc)]

        def copies(s, h, t):
            return (pltpu.make_async_copy(stage.at[s], region(h, t), lsem.at[s]),
                    pltpu.make_async_remote_copy(src_ref=stage.at[s], dst_ref=region(h, t), send_sem=ssem.at[s],
                                                 recv_sem=rsem, device_id=(mx, my, 1 - mc), device_id_type=MESH))

        def wait_sent(s):
            loc, rem = copies(s, 0, 0)
            loc.wait()
            rem.wait_send()

        @pl.when(i >= 2)
        def _():
            wait_sent(slot)

        acc = p_ref[0].astype(F32)
        for k in range(1, 4):
            acc = acc + p_ref[k].astype(F32)
        stage[slot] = acc
        if kind == "col":
            for cp in copies(slot, mc, i):
                cp.start()
        else:
            for h in range(2):
                @pl.when(mc == h)
                def _(h=h):
                    for cp in copies(slot, h, i):
                        cp.start()

        @pl.when(i == nt - 1)
        def _():
            wait_sent(slot)
            if nt >= 2:
                wait_sent(1 - slot)
            half = fin_ref.at[pl.ds(0, pr), pl.ds(0, pc)]
            pltpu.make_async_remote_copy(src_ref=half, dst_ref=half, send_sem=ssem.at[0], recv_sem=rsem,
                                         device_id=(mx, my, 1 - mc), device_id_type=MESH).wait_recv()

    return pl.pallas_call(
        body, name=name, grid=(nt,),
        in_specs=[pl.BlockSpec((4, tr, pc), lambda i: (0, i, 0))],
        out_specs=ANY,
        out_shape=jax.ShapeDtypeStruct((sr, sc), F32),
        scratch_shapes=[pltpu.VMEM((2, tr, pc), F32), pltpu.SemaphoreType.DMA((2,)), pltpu.SemaphoreType.DMA((2,)),
                        pltpu.SemaphoreType.DMA],
        compiler_params=_cp("arbitrary"),
    )(parts)


def _pack(parts, rows):
    flat = []
    for a in parts:
        a = jnp.ravel(a).astype(F32)
        flat.append(jnp.pad(a, (0, (-a.shape[0]) % 128)))
    v = jnp.concatenate(flat)
    return jnp.pad(v, (0, rows * 128 - v.shape[0])).reshape(rows, 128)


def _unpack(block, shapes):
    lead = block.shape[:-2]
    v = block.reshape(lead + (-1,))
    out, off = [], 0
    for shp in shapes:
        n = int(np.prod(shp))
        out.append(v[..., off:off + n].reshape(lead + tuple(shp)))
        off += n + (-n) % 128
    return out


def _block_diag4(w):
    w4 = w.reshape(4, 4, 64, 64)
    eye = jnp.eye(4, dtype=w.dtype)
    return (w4[:, :, :, None, :] * eye[None, :, None, :, None]).reshape(4, 256, 256)


def _diag_blocks(bd):
    b5 = bd.reshape(4, 4, 64, 4, 64)
    return jnp.stack([b5[:, i, :, i, :] for i in range(4)], axis=1).reshape(16, 64, 64)


def _bias_window(rel_bias):
    m = (np.arange(768) + 127) % 768 - 127
    w = rel_bias[:, np.clip(512 - m, -128, 128) + 128]
    win = jnp.tile(w, (1, 128))[:, :128 * 767].reshape(8, 128, 767)[:, :, :WIN]
    qh = np.arange(128)[:, None] // CHUNK
    kc = np.arange(WIN)[None, :] // CHUNK
    valid = (kc >= qh) & (kc <= qh + 8)
    return jnp.where(jnp.asarray(valid)[None], win, NEG)


SMALL = ("b_ada", "norm_pre", "norm_post", "rel_bias", "conv_w", "conv_b", "lru_wa", "lru_ba", "lru_wx",
         "lru_bx", "lru_lambda")
WEIGHTS = ("w_ada", "b_ada", "norm_pre", "norm_post", "ffn1_w_gu", "ffn1_w_down", "w_in", "rel_bias", "conv_w",
           "conv_b", "lru_wa", "lru_ba", "lru_wx", "lru_bx", "lru_lambda", "w_att_o", "w_rec_o", "w_out",
           "ffn2_w_gu", "ffn2_w_down")


def kernel(x, c, w_ada, b_ada, norm_pre, norm_post, ffn1_w_gu, ffn1_w_down, w_in, rel_bias, conv_w, conv_b, lru_wa, lru_ba, lru_wx, lru_bx, lru_lambda, w_att_o, w_rec_o, w_out, ffn2_w_gu, ffn2_w_down, loss_target, m_w_ada, m_b_ada, m_norm_pre, m_norm_post, m_ffn1_w_gu, m_ffn1_w_down, m_w_in, m_rel_bias, m_conv_w, m_conv_b, m_lru_wa, m_lru_ba, m_lru_wx, m_lru_bx, m_lru_lambda, m_w_att_o, m_w_rec_o, m_w_out, m_ffn2_w_gu, m_ffn2_w_down, v_w_ada, v_b_ada, v_norm_pre, v_norm_post, v_ffn1_w_gu, v_ffn1_w_down, v_w_in, v_rel_bias, v_conv_w, v_conv_b, v_lru_wa, v_lru_ba, v_lru_wx, v_lru_bx, v_lru_lambda, v_w_att_o, v_w_rec_o, v_w_out, v_ffn2_w_gu, v_ffn2_w_down):
    W = dict(w_ada=w_ada, b_ada=b_ada, norm_pre=norm_pre, norm_post=norm_post, ffn1_w_gu=ffn1_w_gu,
             ffn1_w_down=ffn1_w_down, w_in=w_in, rel_bias=rel_bias, conv_w=conv_w, conv_b=conv_b, lru_wa=lru_wa,
             lru_ba=lru_ba, lru_wx=lru_wx, lru_bx=lru_bx, lru_lambda=lru_lambda, w_att_o=w_att_o, w_rec_o=w_rec_o,
             w_out=w_out, ffn2_w_gu=ffn2_w_gu, ffn2_w_down=ffn2_w_down)
    M = dict(w_ada=m_w_ada, b_ada=m_b_ada, norm_pre=m_norm_pre, norm_post=m_norm_post, ffn1_w_gu=m_ffn1_w_gu,
             ffn1_w_down=m_ffn1_w_down, w_in=m_w_in, rel_bias=m_rel_bias, conv_w=m_conv_w, conv_b=m_conv_b,
             lru_wa=m_lru_wa, lru_ba=m_lru_ba, lru_wx=m_lru_wx, lru_bx=m_lru_bx, lru_lambda=m_lru_lambda,
             w_att_o=m_w_att_o, w_rec_o=m_w_rec_o, w_out=m_w_out, ffn2_w_gu=m_ffn2_w_gu, ffn2_w_down=m_ffn2_w_down)
    V = dict(w_ada=v_w_ada, b_ada=v_b_ada, norm_pre=v_norm_pre, norm_post=v_norm_post, ffn1_w_gu=v_ffn1_w_gu,
             ffn1_w_down=v_ffn1_w_down, w_in=v_w_in, rel_bias=v_rel_bias, conv_w=v_conv_w, conv_b=v_conv_b,
             lru_wa=v_lru_wa, lru_ba=v_lru_ba, lru_wx=v_lru_wx, lru_bx=v_lru_bx, lru_lambda=v_lru_lambda,
             w_att_o=v_w_att_o, w_rec_o=v_w_rec_o, w_out=v_w_out, ffn2_w_gu=v_ffn2_w_gu, ffn2_w_down=v_ffn2_w_down)
    mx, my, mc = _mesh_pos()
    p = 2 * mx + my
    e = 4 * mx + 2 * my + mc
    xs = x[0]

    c_arr = jnp.reshape(mc, (1,)).astype(jnp.int32)
    cp_arr = jnp.stack([mc, p]).astype(jnp.int32)
    p_arr = jnp.reshape(p, (1,)).astype(jnp.int32)
    direct = ("w_att_o", "w_rec_o", "w_out", "ffn2_w_gu", "ffn2_w_down")
    geoms = [(kind, R, C, n in direct) for (n, kind, R, C) in BIG]
    names = [b[0] for b in BIG]
    placed = [ag_local(W[n][0], kind, R, C, p_arr, "ag_local_" + n) for (n, kind, R, C) in BIG[:2]]

    def arrived(fly, lo, hi, ssem, rsem, after, tag):
        done = ag_wait(fly, geoms[lo:hi], ssem, rsem, after, "ag_wait_" + tag)
        return [a if both else ag_forward(a, kind, R, C, "ag_forward_" + n)
                for a, (kind, R, C, both), n in zip(done, geoms[lo:hi], names[lo:hi])]

    g1 = ag_small(_pack([c, norm_pre, norm_post, conv_w], 32), "ag_small_params")
    c_all, npre4, npost4, cw4 = _unpack(g1, [(D,), (3, 256), (3, 256), (4, 256)])
    chipwise = lambda a: jnp.moveaxis(a[0::2], 0, 1).reshape(a.shape[1], D)
    npre, npost, conv_full = chipwise(npre4), chipwise(npost4), chipwise(cw4)

    b_cols = lax.dynamic_slice(b_ada, (0, p * 2304), (1, 2304))
    mod_cols = ada_fwd(c_all, w_ada[0], b_cols, "ada_fwd")
    g2 = ag_small(mod_cols.reshape(144, 128), "ag_mod")
    mod_all = jnp.moveaxis(g2[0::2].reshape(4, 8, 2304), 0, 1).reshape(8, 9 * D)
    mod = lax.dynamic_index_in_dim(mod_all, e, 0, keepdims=False).reshape(3, 3, D)
    zeros3 = jnp.zeros((3, D), F32)
    vecs = [jnp.concatenate([npre[k:k + 1], npost[k:k + 1], mod[k], zeros3], axis=0) for k in range(3)]

    gu_s, gu_r, gu_fly, tok_gu = ag_start(placed[:1], geoms[:1], [g2], "ag_start_ffn1_gu")
    dn_s, dn_r, dn_fly, tok0 = ag_start(placed[1:2], geoms[1:2], [tok_gu], "ag_start_ffn1_down")
    placed += [ag_local(W[n][0], kind, R, C, p_arr, "ag_local_" + n, after=[tok0]) for (n, kind, R, C) in BIG[2:]]
    f1_gu, = arrived(gu_fly, 0, 1, gu_s, gu_r, placed[7], "ffn1_gu")
    f1_dn, = arrived(dn_fly, 1, 2, dn_s, dn_r, f1_gu, "ffn1_down")
    mix_s, mix_r, mix_fly, tok1 = ag_start(placed[2:6], geoms[2:6], [f1_gu, f1_dn], "ag_start_mixer")
    ffn_s, ffn_r, ffn_fly, tok2 = ag_start(placed[6:], geoms[6:], [tok1], "ag_start_ffn2")
    wa_bd = _block_diag4(lru_wa[0]).astype(BF16)
    wx_bd = _block_diag4(lru_wx[0]).astype(BF16)
    pvec = jnp.concatenate([conv_full, conv_b, lru_ba, lru_bx, lru_lambda], axis=0)
    bias = _bias_window(rel_bias[0]).reshape(4, 256, WIN)

    f1_u = f1_gu[:, FF:]
    x1, h1, g1_, u1, a1, f1 = ffn_fwd(xs, vecs[0] + tok2[0:1, 0:1], f1_gu, f1_u, f1_dn, 0.5, "ffn1_fwd")
    win, wao, wro, wout = arrived(mix_fly, 2, 6, mix_s, mix_r, x1, "mixer")
    h2, qkv, rest = proj_fwd(x1, vecs[1], win, "proj_fwd")
    ao = attn_fwd(qkv, bias, "attn_fwd")
    hl, hg = lru_fwd(rest, pvec, wa_bd, wx_bd, "lru_fwd")
    x2, att, rec, mg, f2 = mix_out_fwd(x1, ao, hg, rest, vecs[1], wao, wro, wout, "mix_out_fwd")
    f2_gu, f2_dn = arrived(ffn_fly, 6, 8, ffn_s, ffn_r, x2, "ffn2")
    f2_u = f2_gu[:, FF:]
    dy, h3, g3_, u3, a3, f3, lvec = ffn_fwd(x2, vecs[2], f2_gu, f2_u, f2_dn, 0.5, "ffn2_fwd", tgt=loss_target[0])

    G, grads = {}, {}
    geo = {n: (kind, R, C) for (n, kind, R, C) in BIG}

    def reduce_begin(names, tag):
        ps, rb = [], []
        for n in names:
            got = pair_push(G[n], geo[n][0], c_arr, "rs_push_" + n)
            a, b = pair_add(G[n], got, geo[n][0], cp_arr, "rs_pair_sum_" + n)
            ps.append(a)
            rb.append(b)
        return rs_start(ps, rb, [], "rs_start_" + tag)

    def reduce_end(names, flight, after, tag):
        ssem, rsem, ps, rb, _ = flight
        for a, n in zip(rs_wait(ps, rb, ssem, rsem, after, "rs_wait_" + tag), names):
            grads[n] = sum_share(a, *geo[n], "rs_sum_share_" + n)[None]

    dx2, df3, dgu3, va2 = ffn_bwd(dy, x2, f3, g3_, u3, vecs[2], f2_gu, f2_u, f2_dn, 0.5, "ffn2_bwd")
    G["ffn2_w_gu"] = mm_tn(h3, dgu3, "dw_ffn2_gu", D, 1408, 1024)
    G["ffn2_w_down"] = mm_tn(a3, df3, "dw_ffn2_down", 1408, D, 1024)
    fly_ffn2 = reduce_begin(("ffn2_w_gu", "ffn2_w_down"), "ffn2")
    vec1 = vecs[1] + fly_ffn2[4][0:1, 0:1]
    df2, d_att, d_rec, dao, dhl, d3, va_out = mix_out_bwd(dx2, f2, att, rec, rest, hl, vec1, wao, wro, wout,
                                                          "mix_out_bwd")
    G["w_out"] = mm_tn(mg, df2, "dw_out", D, D, 1024)
    G["w_att_o"] = mm_tn(ao, d_att, "dw_att_o", 512, D, 1024)
    G["w_rec_o"] = mm_tn(hg, d_rec, "dw_rec_o", D, D, 1024)
    dq, db, dkv = attn_bwd(qkv, dao, bias, "attn_bwd")
    dxr, v_lru, dwa_bd, dwx_bd = lru_bwd(dhl, hl, rest, pvec, wa_bd, wx_bd, "lru_bwd")
    dx1, va_in = proj_bwd(dq, dkv, dxr, d3, win, x1, dx2, vecs[1], "proj_bwd")
    G["w_in"] = dw_in(h2, dq, dkv, dxr, d3, "dw_in")
    fly_mix = reduce_begin(("w_in", "w_att_o", "w_rec_o", "w_out"), "mixer")
    vec0 = vecs[0] + fly_mix[4][0:1, 0:1]
    dx0, df1, dgu1, va0 = ffn_bwd(dx1, xs, f1, g1_, u1, vec0, f1_gu, f1_u, f1_dn, 0.5, "ffn1_bwd")
    G["ffn1_w_gu"] = mm_tn(h1, dgu1, "dw_ffn1_gu", D, 1408, 1024)
    G["ffn1_w_down"] = mm_tn(a1, df1, "dw_ffn1_down", 1408, D, 1024)
    fly_ffn1 = reduce_begin(("ffn1_w_gu", "ffn1_w_down"), "ffn1")
    reduce_end(("ffn2_w_gu", "ffn2_w_down"), fly_ffn2, [fly_ffn1[4]], "ffn2")
    reduce_end(("w_in", "w_att_o", "w_rec_o", "w_out"), fly_mix, [fly_ffn1[4], grads["ffn2_w_down"]], "mixer")

    va1 = va_out + va_in
    vas = (va0, va1, va2)
    dmod = jnp.stack([v[2:5] for v in vas])
    part = {"b_ada": dmod, "norm_pre": jnp.stack([v[0] for v in vas]), "norm_post": jnp.stack([v[1] for v in vas]),
            "rel_bias": bias_grad(db.reshape(8, 128, WIN), "bias_grad")[:, :257], "conv_w": v_lru[0:4], "conv_b": v_lru[4],
            "lru_wa": _diag_blocks(dwa_bd), "lru_ba": v_lru[5], "lru_wx": _diag_blocks(dwx_bd), "lru_bx": v_lru[6],
            "lru_lambda": v_lru[7]}
    full_shapes = {"b_ada": (9 * D,), "norm_pre": (3, D), "norm_post": (3, D), "rel_bias": (8, 257),
                   "conv_w": (4, D), "conv_b": (D,), "lru_wa": (16, 64, 64), "lru_ba": (D,),
                   "lru_wx": (16, 64, 64), "lru_bx": (D,), "lru_lambda": (D,)}
    g3 = ag_small(_pack([part[n] for n in SMALL] + [lvec[0:1, 0:1]], 1232), "ag_small_grads")
    summed = _unpack(sum_lead(g3, "sum_small_grads"), [full_shapes[n] for n in SMALL] + [(1,)])
    red = dict(zip(SMALL, summed[:-1]))
    loss = summed[-1][0]
    cols = lambda a: lax.dynamic_slice(a, (0, p * 256), (a.shape[0], 256))
    grads.update({"b_ada": red["b_ada"][None], "norm_pre": cols(red["norm_pre"])[None],
                  "norm_post": cols(red["norm_post"])[None], "rel_bias": red["rel_bias"][None],
                  "conv_w": cols(red["conv_w"])[None], "conv_b": red["conv_b"][None], "lru_wa": red["lru_wa"][None],
                  "lru_ba": red["lru_ba"][None], "lru_wx": red["lru_wx"][None], "lru_bx": red["lru_bx"][None],
                  "lru_lambda": red["lru_lambda"][None]})

    dmod_all = g3[:, :72].reshape(8, 9 * D)
    dmod_cols = jnp.pad(lax.dynamic_slice(dmod_all, (0, p * 2304), (8, 2304)), ((0, 120), (0, 0)))
    c_all_t = jnp.pad(c_all.T, ((0, 0), (0, 120)))
    grads["w_ada"] = ada_bwd(c_all_t, dmod_cols, "ada_bwd")[None]

    delta, new_m, new_v = {}, {}, {}

    def update(n):
        shp = W[n].shape
        res = adamw(W[n][0], grads[n][0], M[n][0], V[n][0], "adamw_" + n, emit_g=n in geo)
        delta[n], new_m[n], new_v[n] = [a.reshape(shp) for a in res[:3]]
        if n in geo:
            grads[n] = res[3].reshape(shp)

    for n in ("w_ada", "ffn2_w_gu", "ffn2_w_down", "w_in", "w_att_o", "w_rec_o", "w_out"):
        update(n)
    packed = [_pack([src[n] for n in SMALL], 1168) for src in (W, grads, M, V)]
    outs = adamw(*packed, "adamw_small")
    for dst, blk in zip((delta, new_m, new_v), outs):
        for n, a in zip(SMALL, _unpack(blk, [W[n].shape for n in SMALL])):
            dst[n] = a
    reduce_end(("ffn1_w_gu", "ffn1_w_down"), fly_ffn1,
               [outs[0], delta["w_ada"], delta["ffn2_w_gu"], delta["ffn2_w_down"], delta["w_in"], delta["w_out"]], "ffn1")
    for n in ("ffn1_w_gu", "ffn1_w_down"):
        update(n)

    return (loss, dx0[None], *[grads[n] for n in WEIGHTS], *[delta[n] for n in WEIGHTS],
            *[new_m[n] for n in WEIGHTS], *[new_v[n] for n in WEIGHTS])
```

```python
import functools

import numpy as np
import jax
import jax.numpy as jnp
from jax import lax
from jax.experimental import pallas as pl
from jax.experimental.pallas import tpu as pltpu

F32 = jnp.float32
BF16 = jnp.bfloat16

D = 1024
FF = 2816
PW = 5632
HP = 128
CHUNK = 64
WIN = 640
TQ = 512
EPS = 1e-6
NEG = -1e30
LRU_C = 8.0
N_DEV = 8
VMEM_LIMIT = 56 * 1024 * 1024

ADAM_LR, ADAM_B1, ADAM_B2, ADAM_EPS, ADAM_WD, ADAM_STEP = 0.001, 0.9, 0.999, 1e-08, 0.01, 10

MESH = pl.DeviceIdType.MESH
ANY = pl.BlockSpec(memory_space=pl.ANY)


def _cp(*sem):
    return pltpu.CompilerParams(dimension_semantics=tuple(sem), vmem_limit_bytes=VMEM_LIMIT)


def _dot(a, b):
    return jnp.dot(a, b, preferred_element_type=F32)


def _dot_nt(a, b):
    return lax.dot_general(a, b, (((1,), (1,)), ((), ())), preferred_element_type=F32)


def _dot_tn(a, b):
    return lax.dot_general(a, b, (((0,), (0,)), ((), ())), preferred_element_type=F32)


def _mean(v):
    return jnp.mean(v, axis=-1, keepdims=True)


def _colsum(v):
    return jnp.sum(v, axis=0, keepdims=True)


def _sigmoid(v):
    return 0.5 * jnp.tanh(0.5 * v) + 0.5


_GK = 0.7978845608028654


def _gelu(v):
    t = jnp.tanh(_GK * (v + 0.044715 * v * v * v))
    return 0.5 * v * (1.0 + t)


def _pre_norm(xv, vec_ref):
    r = lax.rsqrt(_mean(xv * xv) + EPS)
    n = xv * r * vec_ref[0:1, :]
    return n * (1.0 + vec_ref[3:4, :]) + vec_ref[2:3, :]


def _pre_norm_bwd(dh, xv, dres, vec_ref, vacc_ref):
    r = lax.rsqrt(_mean(xv * xv) + EPS)
    xh = xv * r
    n = xh * vec_ref[0:1, :]
    vacc_ref[2:3, :] += _colsum(dh)
    vacc_ref[3:4, :] += _colsum(dh * n)
    dn = dh * (1.0 + vec_ref[3:4, :])
    vacc_ref[0:1, :] += _colsum(dn * xh)
    dxh = dn * vec_ref[0:1, :]
    return r * (dxh - xh * _mean(dxh * xh)) + dres


def _post_norm_bwd(dxo, fv, res, vec_ref, vacc_ref):
    rf = lax.rsqrt(_mean(fv * fv) + EPS)
    fh = fv * rf
    gp = vec_ref[1:2, :]
    vacc_ref[4:5, :] += _colsum(res * dxo * (fh * gp))
    dy = (res * vec_ref[4:5, :]) * dxo
    vacc_ref[1:2, :] += _colsum(dy * fh)
    dfn = dy * gp
    return rf * (dfn - fh * _mean(dfn * fh))


def ffn_fwd(x, vec, w_gu, w_u, w_dn, res, name, tgt=None, tm=1024, tf=512):
    S = x.shape[0]
    tm = min(tm, S)
    nt = S // tm
    nf = -(-FF // tf)
    tail = FF - tf * (nf - 1)
    head = tgt is not None

    def body(*refs):
        x_ref, vec_ref, wg_ref, wu_ref, wd_ref = refs[:5]
        if head:
            t_ref, xo_ref, h_ref, g_ref, u_ref, a_ref, f_ref, l_ref, hs, acc, lacc = refs[5:]
        else:
            xo_ref, h_ref, g_ref, u_ref, a_ref, f_ref, hs, acc = refs[5:]
        i, j = pl.program_id(0), pl.program_id(1)

        @pl.when(j == 0)
        def _():
            h = _pre_norm(x_ref[...], vec_ref).astype(BF16)
            hs[...] = h
            h_ref[...] = h
            acc[...] = jnp.zeros_like(acc)

        def chunk(w):
            h = hs[...]
            g = _dot(h, wg_ref[:, 0:w])
            u = _dot(h, wu_ref[:, 0:w])
            g_ref[:, 0:w] = g.astype(BF16)
            u_ref[:, 0:w] = u.astype(BF16)
            a = (g * _sigmoid(g) * u).astype(BF16)
            a_ref[:, 0:w] = a
            acc[...] += _dot(a, wd_ref[0:w, :])

        @pl.when(j < nf - 1)
        def _():
            chunk(tf)

        @pl.when(j == nf - 1)
        def _():
            chunk(tail)
            f = acc[...]
            f_ref[...] = f.astype(BF16)
            y = f * lax.rsqrt(_mean(f * f) + EPS) * vec_ref[1:2, :]
            xo = x_ref[...] + (res * vec_ref[4:5, :]) * y
            if head:
                @pl.when(i == 0)
                def _():
                    lacc[...] = jnp.zeros_like(lacc)

                d = xo - t_ref[...]
                xo_ref[...] = d * (1.0 / D)
                lacc[...] += _colsum(d * d)

                @pl.when(i == nt - 1)
                def _():
                    l_ref[...] = jnp.broadcast_to(0.5 * jnp.sum(lacc[...]) * (1.0 / D), (8, 128))
            else:
                xo_ref[...] = xo

    row = lambda i, j: (i, 0)
    col = lambda i, j: (i, j)
    once = dict(pipeline_mode=pl.Buffered(1)) if head else {}
    in_specs = [pl.BlockSpec((tm, D), row, **once), pl.BlockSpec((8, D), lambda i, j: (0, 0)),
                pl.BlockSpec((D, tf), lambda i, j: (0, j)), pl.BlockSpec((D, tf), lambda i, j: (0, j)),
                pl.BlockSpec((tf, D), lambda i, j: (j, 0))]
    out_specs = [pl.BlockSpec((tm, D), row), pl.BlockSpec((tm, D), row), pl.BlockSpec((tm, tf), col),
                 pl.BlockSpec((tm, tf), col), pl.BlockSpec((tm, tf), col), pl.BlockSpec((tm, D), row)]
    out_shape = [jax.ShapeDtypeStruct((S, D), F32), jax.ShapeDtypeStruct((S, D), BF16),
                 jax.ShapeDtypeStruct((S, FF), BF16), jax.ShapeDtypeStruct((S, FF), BF16),
                 jax.ShapeDtypeStruct((S, FF), BF16), jax.ShapeDtypeStruct((S, D), BF16)]
    scratch = [pltpu.VMEM((tm, D), BF16), pltpu.VMEM((tm, D), F32)]
    args = [x, vec, w_gu, w_u, w_dn]
    if head:
        in_specs.append(pl.BlockSpec((tm, D), row, **once))
        out_specs.append(pl.BlockSpec((8, 128), lambda i, j: (0, 0)))
        out_shape.append(jax.ShapeDtypeStruct((8, 128), F32))
        scratch.append(pltpu.VMEM((1, D), F32))
        args.append(tgt)
    return pl.pallas_call(
        body, name=name, grid=(nt, nf), in_specs=in_specs, out_specs=out_specs, out_shape=out_shape,
        scratch_shapes=scratch,
        compiler_params=_cp("arbitrary" if head else "parallel", "arbitrary"),
    )(*args)


def ffn_bwd(dxo, x, f, g, u, vec, w_gu, w_u, w_dn, res, name, tm=1024, tf=512):
    S = x.shape[0]
    tm = min(tm, S)
    nf = -(-FF // tf)
    tail = FF - tf * (nf - 1)

    def body(dxo_ref, x_ref, f_ref, g_ref, u_ref, vec_ref, wg_ref, wu_ref, wd_ref,
             dx_ref, df_ref, dgu_ref, vacc_ref, dfs, acc):
        i, j = pl.program_id(0), pl.program_id(1)

        @pl.when((i == 0) & (j == 0))
        def _():
            vacc_ref[...] = jnp.zeros_like(vacc_ref)

        @pl.when(j == 0)
        def _():
            df = _post_norm_bwd(dxo_ref[...], f_ref[...].astype(F32), res, vec_ref, vacc_ref).astype(BF16)
            dfs[...] = df
            df_ref[...] = df
            acc[...] = jnp.zeros_like(acc)

        def chunk(w):
            da = _dot_nt(dfs[...], wd_ref[0:w, :])
            gv, uv = g_ref[:, 0:w].astype(F32), u_ref[:, 0:w].astype(F32)
            sg = _sigmoid(gv)
            dg = (da * uv * (sg * (1.0 + gv * (1.0 - sg)))).astype(BF16)
            du = (da * (gv * sg)).astype(BF16)
            dgu_ref[0, :, 0:w] = dg
            dgu_ref[1, :, 0:w] = du
            acc[...] += _dot_nt(dg, wg_ref[:, 0:w]) + _dot_nt(du, wu_ref[:, 0:w])

        @pl.when(j < nf - 1)
        def _():
            chunk(tf)

        @pl.when(j == nf - 1)
        def _():
            chunk(tail)
            dx_ref[...] = _pre_norm_bwd(acc[...], x_ref[...], dxo_ref[...], vec_ref, vacc_ref)

    row = lambda i, j: (i, 0)
    col = lambda i, j: (i, j)
    return pl.pallas_call(
        body, name=name, grid=(S // tm, nf),
        in_specs=[pl.BlockSpec((tm, D), row), pl.BlockSpec((tm, D), row, pipeline_mode=pl.Buffered(1)),
                  pl.BlockSpec((tm, D), row)]
        + [pl.BlockSpec((tm, tf), col), pl.BlockSpec((tm, tf), col),
                  pl.BlockSpec((8, D), lambda i, j: (0, 0)),
                  pl.BlockSpec((D, tf), lambda i, j: (0, j)), pl.BlockSpec((D, tf), lambda i, j: (0, j)),
                  pl.BlockSpec((tf, D), lambda i, j: (j, 0))],
        out_specs=[pl.BlockSpec((tm, D), row), pl.BlockSpec((tm, D), row),
                   pl.BlockSpec((2, tm, tf), lambda i, j: (0, i, j)),
                   pl.BlockSpec((8, D), lambda i, j: (0, 0))],
        out_shape=[jax.ShapeDtypeStruct((S, D), F32), jax.ShapeDtypeStruct((S, D), BF16),
                   jax.ShapeDtypeStruct((2, S, FF), BF16), jax.ShapeDtypeStruct((8, D), F32)],
        scratch_shapes=[pltpu.VMEM((tm, D), BF16), pltpu.VMEM((tm, D), F32)],
        compiler_params=_cp("arbitrary", "arbitrary"),
    )(dxo, x, f, g, u, vec, w_gu, w_u, w_dn)


def mm_tn(a, b, name, tm, tn, tk, out_dtype=BF16):
    S, M = a.shape
    if b.ndim == 3:
        G, _, Nf = b.shape
    else:
        G, Nf = 1, b.shape[1]
    N = G * Nf
    tk = min(tk, S)
    nbf = Nf // tn
    nk = S // tk

    def body(a_ref, b_ref, o_ref, acc):
        k = pl.program_id(2)

        @pl.when(k == 0)
        def _():
            acc[...] = jnp.zeros_like(acc)

        acc[...] += _dot_tn(a_ref[...], b_ref[...])

        @pl.when(k == nk - 1)
        def _():
            o_ref[...] = acc[...].astype(out_dtype)

    if b.ndim == 3:
        b_spec = pl.BlockSpec((None, tk, tn), lambda i, j, k: (j // nbf, k, j % nbf))
    else:
        b_spec = pl.BlockSpec((tk, tn), lambda i, j, k: (k, j))
    return pl.pallas_call(
        body, name=name, grid=(M // tm, N // tn, nk),
        in_specs=[pl.BlockSpec((tk, tm), lambda i, j, k: (k, i)), b_spec],
        out_specs=pl.BlockSpec((tm, tn), lambda i, j, k: (i, j)),
        out_shape=jax.ShapeDtypeStruct((M, N), out_dtype),
        scratch_shapes=[pltpu.VMEM((tm, tn), F32)],
        compiler_params=_cp("parallel", "parallel", "arbitrary"),
    )(a, b)


def proj_fwd(x, vec, w_in, name, tm=2048, tn=512):
    S = x.shape[0]
    tm = min(tm, S)
    nq = 1536 // tn

    def body(x_ref, vec_ref, w_ref, h_ref, qkv_ref, rest_ref, hs):
        j = pl.program_id(1)

        @pl.when(j == 0)
        def _():
            h = _pre_norm(x_ref[...], vec_ref).astype(BF16)
            hs[...] = h
            h_ref[...] = h

        r = _dot(hs[...], w_ref[...])

        @pl.when(j < nq)
        def _():
            qkv_ref[...] = r.astype(BF16)

        @pl.when(j >= nq)
        def _():
            rest_ref[...] = r.astype(BF16)

    row = lambda i, j: (i, 0)
    return pl.pallas_call(
        body, name=name, grid=(S // tm, PW // tn),
        in_specs=[pl.BlockSpec((tm, D), row), pl.BlockSpec((8, D), lambda i, j: (0, 0)),
                  pl.BlockSpec((D, tn), lambda i, j: (0, j))],
        out_specs=[pl.BlockSpec((tm, D), row),
                   pl.BlockSpec((tm, tn), lambda i, j: (i, jnp.minimum(j, nq - 1))),
                   pl.BlockSpec((tm, tn), lambda i, j: (i, jnp.maximum(j - nq, 0)))],
        out_shape=[jax.ShapeDtypeStruct((S, D), BF16), jax.ShapeDtypeStruct((S, 1536), BF16),
                   jax.ShapeDtypeStruct((S, 4096), BF16)],
        scratch_shapes=[pltpu.VMEM((tm, D), BF16)],
        compiler_params=_cp("parallel", "arbitrary"),
    )(x, vec, w_in)


def proj_bwd(dq, dkv, dxr, d3, w_in, x, dxo, vec, name, tm=1024, tk=512):
    S = x.shape[0]
    tm = min(tm, S)
    nk = PW // tk

    def body(dq_ref, dkv_ref, dxr_ref, d3_ref, w_ref, x_ref, dxo_ref, vec_ref, dx_ref, vacc_ref, acc):
        i, j = pl.program_id(0), pl.program_id(1)

        @pl.when((i == 0) & (j == 0))
        def _():
            vacc_ref[...] = jnp.zeros_like(vacc_ref)

        @pl.when(j == 0)
        def _():
            acc[...] = _dot_nt(dq_ref[...], w_ref[...])

        @pl.when((j >= 1) & (j < 3))
        def _():
            acc[...] += _dot_nt(dkv_ref[...], w_ref[...])

        @pl.when((j >= 3) & (j < 5))
        def _():
            acc[...] += _dot_nt(dxr_ref[...], w_ref[...])

        @pl.when(j >= 5)
        def _():
            acc[...] += _dot_nt(d3_ref[...], w_ref[...])

        @pl.when(j == nk - 1)
        def _():
            dx_ref[...] = _pre_norm_bwd(acc[...], x_ref[...], dxo_ref[...], vec_ref, vacc_ref)

    row = lambda i, j: (i, 0)
    return pl.pallas_call(
        body, name=name, grid=(S // tm, nk),
        in_specs=[pl.BlockSpec((None, tm, tk), lambda i, j: (0, i, 0)),
                  pl.BlockSpec((None, tm, tk), lambda i, j: (jnp.clip(j - 1, 0, 1), i, 0)),
                  pl.BlockSpec((tm, tk), lambda i, j: (i, jnp.clip(j - 3, 0, 1))),
                  pl.BlockSpec((None, tm, tk), lambda i, j: (jnp.clip(j - 5, 0, 5) // 2, i, jnp.clip(j - 5, 0, 5) % 2)),
                  pl.BlockSpec((D, tk), lambda i, j: (0, j)),
                  pl.BlockSpec((tm, D), row), pl.BlockSpec((tm, D), row),
                  pl.BlockSpec((8, D), lambda i, j: (0, 0))],
        out_specs=[pl.BlockSpec((tm, D), row), pl.BlockSpec((8, D), lambda i, j: (0, 0))],
        out_shape=[jax.ShapeDtypeStruct((S, D), F32), jax.ShapeDtypeStruct((8, D), F32)],
        scratch_shapes=[pltpu.VMEM((tm, D), F32)],
        compiler_params=_cp("arbitrary", "arbitrary"),
    )(dq, dkv, dxr, d3, w_in, x, dxo, vec)


def _two_heads(v, lane):
    zero = jnp.zeros((), v.dtype)
    return jnp.concatenate([jnp.where(lane < 64, v, zero), jnp.where(lane >= 64, v, zero)], axis=0)


def _attn_probs(qm, ka, bias_h, i, grp):
    s = _dot_nt(qm, ka) + bias_h
    col = lax.broadcasted_iota(jnp.int32, s.shape, 1)
    first_key = jnp.where(i == 0, 512 - 128 * grp, 0)
    s = jnp.where(col >= first_key, s, NEG)
    e = jnp.exp(s - jnp.max(s, axis=-1, keepdims=True))
    return e * (1.0 / jnp.sum(e, axis=-1, keepdims=True))


def attn_fwd(qkv, bias, name):
    S = qkv.shape[0]
    nb = S // TQ

    def body(q_ref, kp_ref, kc_ref, vp_ref, vc_ref, b_ref, o_ref, kw, vw):
        i = pl.program_id(1)
        kw[0:TQ, :] = kp_ref[...]
        kw[TQ:2 * TQ, :] = kc_ref[...]
        vw[0:TQ, :] = vp_ref[...]
        vw[TQ:2 * TQ, :] = vc_ref[...]
        lane = lax.broadcasted_iota(jnp.int32, (1, HP), 1)

        def group(a, carry):
            r0 = pl.multiple_of(a * 128, 128)
            qa = q_ref[pl.ds(r0, 128), :] * jnp.asarray(0.125, BF16)
            ka = kw[pl.ds(r0, WIN), :]
            va = vw[pl.ds(r0, WIN), :]
            p = _attn_probs(_two_heads(qa, lane), ka, b_ref[...], i, a)
            o2 = _dot(p.astype(BF16), va)
            o_ref[pl.ds(r0, 128), :] = jnp.where(lane < 64, o2[0:128], o2[128:256]).astype(BF16)
            return carry

        lax.fori_loop(0, TQ // 128, group, 0, unroll=True)

    prev = lambda h, i: (jnp.maximum(i - 1, 0), 0)
    return pl.pallas_call(
        body, name=name, grid=(4, nb),
        in_specs=[pl.BlockSpec((TQ, HP), lambda h, i: (i, h)),
                  pl.BlockSpec((TQ, HP), lambda h, i: (jnp.maximum(i - 1, 0), 4 + h)),
                  pl.BlockSpec((TQ, HP), lambda h, i: (i, 4 + h)),
                  pl.BlockSpec((TQ, HP), lambda h, i: (jnp.maximum(i - 1, 0), 8 + h)),
                  pl.BlockSpec((TQ, HP), lambda h, i: (i, 8 + h)),
                  pl.BlockSpec((None, 256, WIN), lambda h, i: (h, 0, 0))],
        out_specs=pl.BlockSpec((TQ, HP), lambda h, i: (i, h)),
        out_shape=jax.ShapeDtypeStruct((S, 512), BF16),
        scratch_shapes=[pltpu.VMEM((2 * TQ, HP), BF16), pltpu.VMEM((2 * TQ, HP), BF16)],
        compiler_params=_cp("parallel", "arbitrary"),
    )(qkv, qkv, qkv, qkv, qkv, bias)


def attn_bwd(qkv, do, bias, name):
    S = qkv.shape[0]
    nb = S // TQ

    def body(q_ref, kp_ref, kc_ref, vp_ref, vc_ref, do_ref, b_ref, dqkv_ref, db_ref, dkv_ref, kw, vw, ak, av):
        i = pl.program_id(1)

        @pl.when(i == 0)
        def _():
            db_ref[...] = jnp.zeros_like(db_ref)
            ak[...] = jnp.zeros_like(ak)
            av[...] = jnp.zeros_like(av)

        @pl.when(i > 0)
        def _():
            ak[0:TQ, :] = ak[TQ:2 * TQ, :]
            av[0:TQ, :] = av[TQ:2 * TQ, :]
            ak[TQ:2 * TQ, :] = jnp.zeros((TQ, HP), F32)
            av[TQ:2 * TQ, :] = jnp.zeros((TQ, HP), F32)

        @pl.when(i < nb)
        def _():
            kw[0:TQ, :] = kp_ref[...]
            kw[TQ:2 * TQ, :] = kc_ref[...]
            vw[0:TQ, :] = vp_ref[...]
            vw[TQ:2 * TQ, :] = vc_ref[...]
            lane = lax.broadcasted_iota(jnp.int32, (1, HP), 1)

            def group(a, carry):
                r0 = pl.multiple_of(a * 128, 128)
                q2 = _two_heads(q_ref[pl.ds(r0, 128), :] * jnp.asarray(0.125, BF16), lane)
                do2 = _two_heads(do_ref[pl.ds(r0, 128), :], lane)
                ka = kw[pl.ds(r0, WIN), :]
                va = vw[pl.ds(r0, WIN), :]
                p = _attn_probs(q2, ka, b_ref[...], i, a)
                dp = _dot_nt(do2, va)
                ds = p * (dp - jnp.sum(p * dp, axis=-1, keepdims=True))
                db_ref[...] += ds
                dsb = ds.astype(BF16)
                dq2 = _dot(dsb, ka)
                ak[pl.ds(r0, WIN), :] += _dot_tn(dsb, q2)
                av[pl.ds(r0, WIN), :] += _dot_tn(p.astype(BF16), do2)
                dq = jnp.where(lane < 64, dq2[0:128], dq2[128:256])
                dqkv_ref[0, pl.ds(r0, 128), :] = (dq * 0.125).astype(BF16)
                return carry

            lax.fori_loop(0, TQ // 128, group, 0, unroll=True)

        @pl.when(i > 0)
        def _():
            dkv_ref[0] = ak[0:TQ, :].astype(BF16)
            dkv_ref[1] = av[0:TQ, :].astype(BF16)

    cur = lambda i: jnp.minimum(i, nb - 1)
    prv = lambda i: jnp.clip(i - 1, 0, nb - 1)
    dq, db, dkv = pl.pallas_call(
        body, name=name, grid=(4, nb + 1),
        in_specs=[pl.BlockSpec((TQ, HP), lambda h, i: (cur(i), h)),
                  pl.BlockSpec((TQ, HP), lambda h, i: (prv(i), 4 + h)),
                  pl.BlockSpec((TQ, HP), lambda h, i: (cur(i), 4 + h)),
                  pl.BlockSpec((TQ, HP), lambda h, i: (prv(i), 8 + h)),
                  pl.BlockSpec((TQ, HP), lambda h, i: (cur(i), 8 + h)),
                  pl.BlockSpec((TQ, HP), lambda h, i: (cur(i), h)),
                  pl.BlockSpec((None, 256, WIN), lambda h, i: (h, 0, 0))],
        out_specs=[pl.BlockSpec((1, TQ, HP), lambda h, i: (0, cur(i), h)),
                   pl.BlockSpec((None, 256, WIN), lambda h, i: (h, 0, 0)),
                   pl.BlockSpec((2, TQ, HP), lambda h, i: (0, prv(i), h))],
        out_shape=[jax.ShapeDtypeStruct((1, S, 512), BF16), jax.ShapeDtypeStruct((4, 256, WIN), F32),
                   jax.ShapeDtypeStruct((2, S, 512), BF16)],
        scratch_shapes=[pltpu.VMEM((2 * TQ, HP), BF16), pltpu.VMEM((2 * TQ, HP), BF16),
                        pltpu.VMEM((2 * TQ, HP), F32), pltpu.VMEM((2 * TQ, HP), F32)],
        compiler_params=_cp("parallel", "arbitrary"),
    )(qkv, qkv, qkv, qkv, qkv, do, bias)
    return dq, db, dkv


def bias_grad(db, name):
    def body(db_ref, o_ref):
        r = lax.broadcasted_iota(jnp.int32, (128, 128), 0)
        c = lax.broadcasted_iota(jnp.int32, (128, 128), 1)
        flip = (r + c == 127).astype(BF16)
        lane = lax.broadcasted_iota(jnp.int32, (16, 384), 1)
        src = lax.broadcasted_iota(jnp.int32, (128, 384), 0)
        dst = lax.broadcasted_iota(jnp.int32, (128, 384), 1)

        def split_dot(v, m):
            hi = v.astype(BF16)
            r1 = v - hi.astype(F32)
            mid = r1.astype(BF16)
            lo = (r1 - mid.astype(F32)).astype(BF16)
            return _dot(hi, m) + _dot(mid, m) + _dot(lo, m)

        def diag_sums(w):
            y = pltpu.roll(split_dot(w, flip), 0, 1, stride=1, stride_axis=0)
            return jnp.broadcast_to(_colsum(y), (16, 128))

        w4 = db_ref[0, :, 512:640]
        w3 = db_ref[0, :, 384:512]
        far = jnp.sum(db_ref[0, :, 0:384]) + jnp.sum(jnp.where(r >= c, w3, 0.0))
        lo4 = diag_sums(jnp.where(r >= c, w4, 0.0))
        up4 = diag_sums(jnp.where(r < c, w4, 0.0))
        up3 = diag_sums(jnp.where(r < c, w3, 0.0))
        p_lo4 = (dst == 128 + (src + 1) % 128).astype(BF16)
        p_up4 = ((dst == src + 1) & (src < 127)).astype(BF16)
        p_up3 = ((dst == src + 129) & (src < 127)).astype(BF16)
        out = split_dot(lo4, p_lo4) + split_dot(up4, p_up4) + split_dot(up3, p_up3)
        o_ref[0] = out + jnp.where(lane == 256, far, 0.0)

    return pl.pallas_call(
        body, name=name, grid=(8,),
        in_specs=[pl.BlockSpec((1, 128, WIN), lambda h: (h, 0, 0))],
        out_specs=pl.BlockSpec((1, 16, 384), lambda h: (h, 0, 0)),
        out_shape=jax.ShapeDtypeStruct((8, 16, 384), F32),
        compiler_params=_cp("parallel"),
    )(db)[:, 0, :]


LT = 256
LC = 512


def _lru_gates(xs, pv_ref, wa_ref, wx_ref, tl):
    xc = (pv_ref[4:5, :] + pv_ref[3:4, :] * xs[pl.ds(8, tl), :] + pv_ref[2:3, :] * xs[pl.ds(7, tl), :]
          + pv_ref[1:2, :] * xs[pl.ds(6, tl), :] + pv_ref[0:1, :] * xs[pl.ds(5, tl), :])
    xcb = xc.astype(BF16)
    pa = jnp.concatenate([_dot(xcb[:, 0:256], wa_ref[0]), _dot(xcb[:, 256:512], wa_ref[1])], axis=1)
    px = jnp.concatenate([_dot(xcb[:, 0:256], wx_ref[0]), _dot(xcb[:, 256:512], wx_ref[1])], axis=1)
    r = _sigmoid(pa + pv_ref[5:6, :])
    ig = _sigmoid(px + pv_ref[6:7, :])
    z = -pv_ref[7:8, :]
    sp = jnp.maximum(z, 0.0) + jnp.log1p(jnp.exp(-jnp.abs(z)))
    log_a = (-LRU_C * r) * sp
    a = jnp.exp(log_a)
    s = jnp.tanh(-log_a) * (1.0 + a * a)
    inv_mult = lax.rsqrt(s)
    mult = jnp.where(s > 0.0, s * inv_mult, 0.0)
    return xc, xcb, r, ig, sp, a, mult, inv_mult


def lru_fwd(rest, pvec, wa, wx, name):
    S = rest.shape[0]
    tl = min(LT, S)
    nt = S // tl

    def body(xr_ref, halo_ref, yr_ref, pv_ref, wa_ref, wx_ref, h_ref, hg_ref, xs, a_s, u_s, h_s, carry):
        ti = pl.program_id(1)

        @pl.when(ti == 0)
        def _():
            carry[...] = jnp.zeros_like(carry)

        xs[0:8, :] = jnp.where(ti > 0, halo_ref[8:16, :].astype(F32), 0.0)
        xs[pl.ds(8, tl), :] = xr_ref[...].astype(F32)
        xc, _, _, ig, _, a, mult, _ = _lru_gates(xs, pv_ref, wa_ref, wx_ref, tl)
        a_s[...] = a
        u_s[...] = mult * (ig * xc)
        row = lax.broadcasted_iota(jnp.int32, (8, LC), 0)

        def blk(bi, c):
            o = pl.multiple_of(bi * 8, 8)
            av = a_s[pl.ds(o, 8), :]
            bv = u_s[pl.ds(o, 8), :]
            for d in (1, 2, 4):
                a_sh = pltpu.roll(av, d, 0)
                b_sh = pltpu.roll(bv, d, 0)
                m = row >= d
                bv = jnp.where(m, av * b_sh + bv, bv)
                av = jnp.where(m, av * a_sh, av)
            hv = bv + av * c
            h_s[pl.ds(o, 8), :] = hv
            return hv[7:8, :]

        carry[...] = lax.fori_loop(0, tl // 8, blk, carry[...])
        h = h_s[...]
        h_ref[...] = h
        hg_ref[...] = (h * _gelu(yr_ref[...].astype(F32))).astype(BF16)

    hb = tl // 16
    return pl.pallas_call(
        body, name=name, grid=(2, nt),
        in_specs=[pl.BlockSpec((tl, LC), lambda c, t: (t, c)),
                  pl.BlockSpec((16, LC), lambda c, t: (jnp.maximum(t * hb - 1, 0), c)),
                  pl.BlockSpec((tl, LC), lambda c, t: (t, 2 + c)),
                  pl.BlockSpec((8, LC), lambda c, t: (0, c)),
                  pl.BlockSpec((2, 256, 256), lambda c, t: (c, 0, 0)),
                  pl.BlockSpec((2, 256, 256), lambda c, t: (c, 0, 0))],
        out_specs=[pl.BlockSpec((tl, LC), lambda c, t: (t, c)), pl.BlockSpec((tl, LC), lambda c, t: (t, c))],
        out_shape=[jax.ShapeDtypeStruct((S, D), F32), jax.ShapeDtypeStruct((S, D), BF16)],
        scratch_shapes=[pltpu.VMEM((tl + 8, LC), F32), pltpu.VMEM((tl, LC), F32), pltpu.VMEM((tl, LC), F32),
                        pltpu.VMEM((tl, LC), F32), pltpu.VMEM((1, LC), F32)],
        compiler_params=_cp("parallel", "arbitrary"),
    )(rest, rest, rest, pvec, wa, wx)


def lru_bwd(dh, h, rest, pvec, wa, wx, name):
    S = rest.shape[0]
    tl = min(LT, S)
    nt = S // tl

    def body(dh_ref, h_ref, hhalo_ref, xr_ref, xhalo_ref, pv_ref, wa_ref, wx_ref,
             dxr_ref, vacc_ref, dwa_ref, dwx_ref,
             xs, hs, a_s, ash_s, b_s, lam_s, dxe, anext, lnext, dxnext):
        ti = pl.program_id(1)
        tr = nt - 1 - ti

        @pl.when(ti == 0)
        def _():
            anext[...] = jnp.zeros_like(anext)
            lnext[...] = jnp.zeros_like(lnext)
            dxnext[...] = jnp.zeros_like(dxnext)
            vacc_ref[...] = jnp.zeros_like(vacc_ref)
            dwa_ref[...] = jnp.zeros_like(dwa_ref)
            dwx_ref[...] = jnp.zeros_like(dwx_ref)

        xs[0:8, :] = jnp.where(tr > 0, xhalo_ref[8:16, :].astype(F32), 0.0)
        xs[pl.ds(8, tl), :] = xr_ref[...].astype(F32)
        xc, xcb, r, ig, sp, a, mult, inv_mult = _lru_gates(xs, pv_ref, wa_ref, wx_ref, tl)

        a_s[pl.ds(0, tl), :] = a
        a_s[pl.ds(tl, 8), :] = jnp.broadcast_to(anext[...], (8, LC))
        ash_s[...] = a_s[pl.ds(1, tl), :]
        b_s[...] = dh_ref[...]
        row = lax.broadcasted_iota(jnp.int32, (8, LC), 0)

        def blk(k, c):
            o = pl.multiple_of((tl // 8 - 1 - k) * 8, 8)
            av = ash_s[pl.ds(o, 8), :]
            bv = b_s[pl.ds(o, 8), :]
            for d in (1, 2, 4):
                a_sh = pltpu.roll(av, 8 - d, 0)
                b_sh = pltpu.roll(bv, 8 - d, 0)
                m = row < 8 - d
                bv = jnp.where(m, bv + av * b_sh, bv)
                av = jnp.where(m, av * a_sh, av)
            lv = bv + av * c
            lam_s[pl.ds(o, 8), :] = lv
            return lv[0:1, :]

        lnext[...] = lax.fori_loop(0, tl // 8, blk, lnext[...])
        anext[...] = a[0:1, :]
        lam = lam_s[...]

        hs[0:8, :] = jnp.where(tr > 0, hhalo_ref[...], 0.0)
        hs[pl.ds(8, tl), :] = h_ref[...]
        d_a = lam * hs[pl.ds(7, tl), :]
        d_mult = lam * (ig * xc)
        d_ig = lam * mult * xc
        dxc = lam * mult * ig
        d_log_a = d_a * a - d_mult * (a * a) * inv_mult
        d_r = d_log_a * (-LRU_C * sp)
        vacc_ref[7:8, :] += _colsum(d_log_a * (-LRU_C * r)) * (-_sigmoid(-pv_ref[7:8, :]))
        d_pa = d_r * r * (1.0 - r)
        d_px = d_ig * ig * (1.0 - ig)
        vacc_ref[5:6, :] += _colsum(d_pa)
        vacc_ref[6:7, :] += _colsum(d_px)
        dpa = d_pa.astype(BF16)
        dpx = d_px.astype(BF16)
        back = []
        for g in range(2):
            sl = slice(256 * g, 256 * g + 256)
            dwa_ref[g] += _dot_tn(xcb[:, sl], dpa[:, sl])
            dwx_ref[g] += _dot_tn(xcb[:, sl], dpx[:, sl])
            back.append(_dot_nt(dpa[:, sl], wa_ref[g]) + _dot_nt(dpx[:, sl], wx_ref[g]))
        dxc = dxc + jnp.concatenate(back, axis=1)
        vacc_ref[4:5, :] += _colsum(dxc)
        for k in range(4):
            vacc_ref[k:k + 1, :] += _colsum(dxc * xs[pl.ds(5 + k, tl), :])
        dxe[pl.ds(0, tl), :] = dxc
        dxe[pl.ds(tl, 8), :] = dxnext[...]
        dxr = (pv_ref[3:4, :] * dxc + pv_ref[2:3, :] * dxe[pl.ds(1, tl), :]
               + pv_ref[1:2, :] * dxe[pl.ds(2, tl), :] + pv_ref[0:1, :] * dxe[pl.ds(3, tl), :])
        dxr_ref[...] = dxr.astype(BF16)
        dxnext[...] = dxc[0:8, :]

    hb = tl // 8
    rev = lambda t: nt - 1 - t
    halo = lambda t: jnp.maximum(rev(t) * hb - 1, 0)
    big = lambda: pltpu.VMEM((tl + 8, LC), F32)
    til = lambda: pltpu.VMEM((tl, LC), F32)
    return pl.pallas_call(
        body, name=name, grid=(2, nt),
        in_specs=[pl.BlockSpec((tl, LC), lambda c, t: (rev(t), c)),
                  pl.BlockSpec((tl, LC), lambda c, t: (rev(t), c)),
                  pl.BlockSpec((8, LC), lambda c, t: (halo(t), c)),
                  pl.BlockSpec((tl, LC), lambda c, t: (rev(t), c)),
                  pl.BlockSpec((16, LC), lambda c, t: (jnp.maximum(rev(t) * (tl // 16) - 1, 0), c)),
                  pl.BlockSpec((8, LC), lambda c, t: (0, c)),
                  pl.BlockSpec((2, 256, 256), lambda c, t: (c, 0, 0)),
                  pl.BlockSpec((2, 256, 256), lambda c, t: (c, 0, 0))],
        out_specs=[pl.BlockSpec((tl, LC), lambda c, t: (rev(t), c)),
                   pl.BlockSpec((8, LC), lambda c, t: (0, c)),
                   pl.BlockSpec((2, 256, 256), lambda c, t: (c, 0, 0)),
                   pl.BlockSpec((2, 256, 256), lambda c, t: (c, 0, 0))],
        out_shape=[jax.ShapeDtypeStruct((S, D), BF16), jax.ShapeDtypeStruct((8, D), F32),
                   jax.ShapeDtypeStruct((4, 256, 256), F32), jax.ShapeDtypeStruct((4, 256, 256), F32)],
        scratch_shapes=[big(), big(), big(), til(), til(), til(), big(),
                        pltpu.VMEM((1, LC), F32), pltpu.VMEM((1, LC), F32), pltpu.VMEM((8, LC), F32)],
        compiler_params=_cp("parallel", "arbitrary"),
    )(dh, h, h, rest, rest, pvec, wa, wx)


def mix_out_fwd(x, ao, hg, rest, vec, w_att_o, w_rec_o, w_out, name, tm=256):
    S = x.shape[0]
    tm = min(tm, S)

    def body(x_ref, ao_ref, hg_ref, ga_ref, gr_ref, vec_ref, wa_ref, wr_ref, wo_ref,
             xo_ref, att_ref, rec_ref, mg_ref, f_ref):
        att = _dot(ao_ref[...], wa_ref[...])
        rec = _dot(hg_ref[...], wr_ref[...])
        att_ref[...] = att.astype(BF16)
        rec_ref[...] = rec.astype(BF16)
        mg = (_sigmoid(ga_ref[...].astype(F32)) * att + _sigmoid(gr_ref[...].astype(F32)) * rec).astype(BF16)
        mg_ref[...] = mg
        f = _dot(mg, wo_ref[...])
        f_ref[...] = f.astype(BF16)
        y = f * lax.rsqrt(_mean(f * f) + EPS) * vec_ref[1:2, :]
        xo_ref[...] = x_ref[...] + (1.0 * vec_ref[4:5, :]) * y

    row = lambda i: (i, 0)
    full = lambda r: pl.BlockSpec((r, D), lambda i: (0, 0))
    return pl.pallas_call(
        body, name=name, grid=(S // tm,),
        in_specs=[pl.BlockSpec((tm, D), row), pl.BlockSpec((tm, 512), row), pl.BlockSpec((tm, D), row),
                  pl.BlockSpec((tm, D), lambda i: (i, 2)), pl.BlockSpec((tm, D), lambda i: (i, 3)),
                  full(8), full(512), full(D), full(D)],
        out_specs=[pl.BlockSpec((tm, D), row)] * 5,
        out_shape=[jax.ShapeDtypeStruct((S, D), F32)] + [jax.ShapeDtypeStruct((S, D), BF16)] * 4,
        compiler_params=_cp("parallel"),
    )(x, ao, hg, rest, rest, vec, w_att_o, w_rec_o, w_out)


def mix_out_bwd(dxo, f, att, rec, rest, h, vec, w_att_o, w_rec_o, w_out, name, tm=256):
    S = dxo.shape[0]
    tm = min(tm, S)

    def body(dxo_ref, f_ref, att_ref, rec_ref, yr_ref, ga_ref, gr_ref, h_ref, vec_ref, wa_ref, wr_ref, wo_ref,
             df_ref, da_ref, dr_ref, dao_ref, dh_ref, d3_ref, vacc_ref):
        @pl.when(pl.program_id(0) == 0)
        def _():
            vacc_ref[...] = jnp.zeros_like(vacc_ref)

        df = _post_norm_bwd(dxo_ref[...], f_ref[...].astype(F32), 1.0, vec_ref, vacc_ref).astype(BF16)
        df_ref[...] = df
        dm = _dot_nt(df, wo_ref[...])
        sa = _sigmoid(ga_ref[...].astype(F32))
        sr = _sigmoid(gr_ref[...].astype(F32))
        d_att = (dm * sa).astype(BF16)
        d_rec = (dm * sr).astype(BF16)
        da_ref[...] = d_att
        dr_ref[...] = d_rec
        d3_ref[1] = (dm * att_ref[...].astype(F32) * (sa * (1.0 - sa))).astype(BF16)
        d3_ref[2] = (dm * rec_ref[...].astype(F32) * (sr * (1.0 - sr))).astype(BF16)
        dao_ref[...] = _dot_nt(d_att, wa_ref[...]).astype(BF16)
        d_hg = _dot_nt(d_rec, wr_ref[...])
        yr = yr_ref[...].astype(F32)
        t = jnp.tanh(_GK * (yr + 0.044715 * yr * yr * yr))
        dh_ref[...] = d_hg * (0.5 * yr * (1.0 + t))
        gelu_grad = 0.5 * (1.0 + t) + 0.5 * yr * (1.0 - t * t) * _GK * (1.0 + 3.0 * 0.044715 * yr * yr)
        d3_ref[0] = (d_hg * h_ref[...] * gelu_grad).astype(BF16)

    row = lambda i: (i, 0)
    full = lambda r: pl.BlockSpec((r, D), lambda i: (0, 0))
    return pl.pallas_call(
        body, name=name, grid=(S // tm,),
        in_specs=[pl.BlockSpec((tm, D), row)] * 4
        + [pl.BlockSpec((tm, D), lambda i: (i, 1)), pl.BlockSpec((tm, D), lambda i: (i, 2)),
           pl.BlockSpec((tm, D), lambda i: (i, 3)), pl.BlockSpec((tm, D), row),
           full(8), full(512), full(D), full(D)],
        out_specs=[pl.BlockSpec((tm, D), row)] * 3
        + [pl.BlockSpec((tm, 512), row), pl.BlockSpec((tm, D), row),
           pl.BlockSpec((3, tm, D), lambda i: (0, i, 0)), pl.BlockSpec((8, D), lambda i: (0, 0))],
        out_shape=[jax.ShapeDtypeStruct((S, D), BF16)] * 3
        + [jax.ShapeDtypeStruct((S, 512), BF16), jax.ShapeDtypeStruct((S, D), F32),
           jax.ShapeDtypeStruct((3, S, D), BF16), jax.ShapeDtypeStruct((8, D), F32)],
        compiler_params=_cp("arbitrary"),
    )(dxo, f, att, rec, rest, rest, rest, h, vec, w_att_o, w_rec_o, w_out)


def dw_in(h, dq, dkv, dxr, d3, name, tk=1024, tn=512):
    S = h.shape[0]
    tk = min(tk, S)
    nk = S // tk

    def body(h_ref, dq_ref, dkv_ref, dxr_ref, d3_ref, o_ref, acc):
        j, k = pl.program_id(0), pl.program_id(1)

        @pl.when(k == 0)
        def _():
            acc[...] = jnp.zeros_like(acc)

        @pl.when(j == 0)
        def _():
            acc[...] += _dot_tn(h_ref[...], dq_ref[...])

        @pl.when((j >= 1) & (j < 3))
        def _():
            acc[...] += _dot_tn(h_ref[...], dkv_ref[...])

        @pl.when((j >= 3) & (j < 5))
        def _():
            acc[...] += _dot_tn(h_ref[...], dxr_ref[...])

        @pl.when(j >= 5)
        def _():
            acc[...] += _dot_tn(h_ref[...], d3_ref[...])

        @pl.when(k == nk - 1)
        def _():
            o_ref[...] = acc[...].astype(BF16)

    use = lambda j, k, lo, hi: jnp.where((j >= lo) & (j < hi), k, 0)
    g3 = lambda j: jnp.clip(j - 5, 0, 5)
    return pl.pallas_call(
        body, name=name, grid=(PW // tn, nk),
        in_specs=[pl.BlockSpec((tk, D), lambda j, k: (k, 0)),
                  pl.BlockSpec((None, tk, tn), lambda j, k: (0, use(j, k, 0, 1), 0)),
                  pl.BlockSpec((None, tk, tn), lambda j, k: (jnp.clip(j - 1, 0, 1), use(j, k, 1, 3), 0)),
                  pl.BlockSpec((tk, tn), lambda j, k: (use(j, k, 3, 5), jnp.clip(j - 3, 0, 1))),
                  pl.BlockSpec((None, tk, tn), lambda j, k: (g3(j) // 2, use(j, k, 5, 11), g3(j) % 2))],
        out_specs=pl.BlockSpec((D, tn), lambda j, k: (0, j)),
        out_shape=jax.ShapeDtypeStruct((D, PW), BF16),
        scratch_shapes=[pltpu.VMEM((D, tn), F32)],
        compiler_params=_cp("parallel", "arbitrary"),
    )(h, dq, dkv, dxr, d3)


def ada_fwd(c_all, w_ada, b_ada, name, tn=768):
    n = w_ada.shape[1]

    def body(c_ref, w_ref, b_ref, o_ref):
        cv = c_ref[...]
        ca = (cv * _sigmoid(cv)).astype(BF16)
        o_ref[...] = _dot(ca, w_ref[...].astype(BF16)) + b_ref[...]

    return pl.pallas_call(
        body, name=name, grid=(n // tn,),
        in_specs=[pl.BlockSpec((8, D), lambda j: (0, 0)), pl.BlockSpec((D, tn), lambda j: (0, j)),
                  pl.BlockSpec((1, tn), lambda j: (0, j))],
        out_specs=pl.BlockSpec((8, tn), lambda j: (0, j)),
        out_shape=jax.ShapeDtypeStruct((8, n), F32),
        compiler_params=_cp("parallel"),
    )(c_all, w_ada, b_ada)


def ada_bwd(c_all_t, dmod, name, tn=768):
    n = dmod.shape[1]

    def body(c_ref, d_ref, o_ref):
        cv = c_ref[...]
        ca = (cv * _sigmoid(cv)).astype(BF16)
        o_ref[...] = _dot(ca, d_ref[...].astype(BF16))

    return pl.pallas_call(
        body, name=name, grid=(n // tn,),
        in_specs=[pl.BlockSpec((D, 128), lambda j: (0, 0)), pl.BlockSpec((128, tn), lambda j: (0, j))],
        out_specs=pl.BlockSpec((D, tn), lambda j: (0, j)),
        out_shape=jax.ShapeDtypeStruct((D, n), F32),
        compiler_params=_cp("parallel"),
    )(c_all_t, dmod)


def _row_tile(rows, cols, itemsize=4, budget=1536 * 1024):
    best = None
    for t in range(8, rows + 1, 8):
        if rows % t == 0 and t * cols * itemsize <= budget:
            best = t
    return rows if best is None else best


def sum_lead(parts, name, out_dtype=F32):
    n, R, C = parts.shape
    tr = _row_tile(R, C * n)

    def body(p_ref, o_ref):
        acc = p_ref[0].astype(F32)
        for k in range(1, n):
            acc = acc + p_ref[k].astype(F32)
        o_ref[...] = acc.astype(out_dtype)

    return pl.pallas_call(
        body, name=name, grid=(R // tr,),
        in_specs=[pl.BlockSpec((n, tr, C), lambda i: (0, i, 0))],
        out_specs=pl.BlockSpec((tr, C), lambda i: (i, 0)),
        out_shape=jax.ShapeDtypeStruct((R, C), out_dtype),
        compiler_params=_cp("parallel"),
    )(parts)


def adamw(w, g, m, v, name, emit_g=False):
    R, C = w.shape
    tr = _row_tile(R, C * 8, budget=8 * 1024 * 1024)

    def body(w_ref, g_ref, m_ref, v_ref, d_ref, mo_ref, vo_ref, *go_ref):
        gv = g_ref[...]
        if emit_g:
            go_ref[0][...] = gv
        mn = ADAM_B1 * m_ref[...] + (1.0 - ADAM_B1) * gv
        vn = ADAM_B2 * v_ref[...] + (1.0 - ADAM_B2) * (gv * gv)
        m_hat = mn / (1.0 - ADAM_B1 ** ADAM_STEP)
        v_hat = vn / (1.0 - ADAM_B2 ** ADAM_STEP)
        d_ref[...] = -ADAM_LR * (m_hat / (jnp.sqrt(v_hat) + ADAM_EPS) + ADAM_WD * w_ref[...])
        mo_ref[...] = mn
        vo_ref[...] = vn

    spec = pl.BlockSpec((tr, C), lambda i: (i, 0))
    return pl.pallas_call(
        body, name=name, grid=(R // tr,),
        in_specs=[spec] * 4, out_specs=[spec] * (4 if emit_g else 3),
        out_shape=[jax.ShapeDtypeStruct((R, C), F32)] * (4 if emit_g else 3),
        compiler_params=_cp("parallel"),
    )(w, g, m, v)


def _mesh_pos():
    return lax.axis_index("x"), lax.axis_index("y"), lax.axis_index("c")


def _other_chips(mx, my):
    return [(1 - mx, my), (mx, 1 - my), (1 - mx, 1 - my)]


def ag_small(x, name):
    R = x.shape[0]

    def body(x_ref, out_ref, send_sems, recv_sems, local_sem):
        mx, my, mc = _mesh_pos()
        me, sibling = (mx, my, mc), (mx, my, 1 - mc)
        chips = _other_chips(mx, my)

        def slot(px, py, pc):
            return out_ref.at[4 * px + 2 * py + pc]

        def copy(k, block, to, src=None):
            return pltpu.make_async_remote_copy(
                src_ref=slot(*block) if src is None else src, dst_ref=slot(*block),
                send_sem=send_sems.at[k], recv_sem=recv_sems.at[k], device_id=to, device_id_type=MESH)

        mine = pltpu.make_async_copy(x_ref, slot(*me), local_sem)
        mine.start()
        first = [copy(0, me, sibling, src=x_ref)]
        first += [copy(1 + j, me, (*chip, mc), src=x_ref) for j, chip in enumerate(chips)]
        for cp in first:
            cp.start()
        passed = [copy(4 + j, (*chip, mc), sibling) for j, chip in enumerate(chips)]
        for j, chip in enumerate(chips):
            copy(1 + j, (*chip, mc), me).wait_recv()
            passed[j].start()
        copy(0, sibling, me).wait_recv()
        for j, chip in enumerate(chips):
            copy(4 + j, (*chip, 1 - mc), me).wait_recv()
        for cp in first + passed:
            cp.wait_send()
        mine.wait()

    return pl.pallas_call(
        body, name=name,
        out_shape=jax.ShapeDtypeStruct((N_DEV, R, 128), F32),
        in_specs=[pl.BlockSpec(memory_space=pltpu.VMEM)],
        out_specs=pl.BlockSpec(memory_space=pltpu.VMEM),
        scratch_shapes=[pltpu.SemaphoreType.DMA((7,)), pltpu.SemaphoreType.DMA((7,)), pltpu.SemaphoreType.DMA],
        compiler_params=pltpu.CompilerParams(vmem_limit_bytes=VMEM_LIMIT),
    )(x)


BIG = (("ffn1_w_gu", "col", D, PW), ("ffn1_w_down", "row", FF, D), ("w_in", "col", D, PW),
       ("w_att_o", "col", 512, D), ("w_rec_o", "row", D, D), ("w_out", "row", D, D),
       ("ffn2_w_gu", "col", D, PW), ("ffn2_w_down", "row", FF, D))
NBIG = len(BIG)


def _shard_shape(kind, R, C):
    return (R, C // 4) if kind == "col" else (R // 4, C)


def _region(ref, kind, R, C, q, half, t, tr):
    sr, sc = _shard_shape(kind, R, C)
    if kind == "col":
        return ref.at[pl.ds(pl.multiple_of(half * (R // 2) + t * tr, 16), tr), pl.ds(q * sc, sc)]
    return ref.at[pl.ds(pl.multiple_of(q * sr + t * tr, 16), tr), pl.ds(half * (C // 2), C // 2)]


def ag_local(w, kind, R, C, p_arr, name, after=()):
    sr, sc = _shard_shape(kind, R, C)
    tr = _row_tile(sr, sc, budget=2 * 1024 * 1024)
    nt = sr // tr
    after = list(after)

    def body(p_ref, w_ref, *rest):
        rest[-1][...] = w_ref[...].astype(BF16)

    if kind == "col":
        o_spec = pl.BlockSpec((tr, sc), lambda i, p: (i, p[0]))
    else:
        o_spec = pl.BlockSpec((tr, sc), lambda i, p: (p[0] * nt + i, 0))
    return pl.pallas_call(
        body, name=name,
        grid_spec=pltpu.PrefetchScalarGridSpec(
            num_scalar_prefetch=1, grid=(nt,),
            in_specs=[pl.BlockSpec((tr, sc), lambda i, p: (i, 0))] + [ANY] * len(after), out_specs=o_spec),
        out_shape=jax.ShapeDtypeStruct((R, C), BF16),
        compiler_params=_cp("parallel"),
    )(p_arr, w, *after)


HBM_SPEC = pl.BlockSpec(memory_space=pltpu.HBM)
SEM_SPEC = pl.BlockSpec(memory_space=pltpu.SEMAPHORE)


def _ag_sems(geoms):
    return sum(6 if both else 3 for (_, _, _, both) in geoms)


def _ag_copies(fulls, geoms, ssem, rsem, mx, my, mc, q, h):
    chips = _other_chips(mx, my)
    out, base = [], 0
    for w, (kind, R, C, both) in enumerate(geoms):
        sr, sc = _shard_shape(kind, R, C)
        hr = sr // 2 if kind == "col" else sr
        reg = _region(fulls[w], kind, R, C, q, h, 0, hr)
        out.append([pltpu.make_async_remote_copy(
            src_ref=reg, dst_ref=reg, send_sem=ssem.at[base + 3 * t + k], recv_sem=rsem.at[base + 3 * t + k],
            device_id=(*chips[k], mc if t == 0 else 1 - mc), device_id_type=MESH)
            for t in range(2 if both else 1) for k in range(3)])
        base += 6 if both else 3
    return out


def ag_start(fulls, geoms, after, name):
    n = len(fulls)
    after = list(after)
    m = len(after)

    def body(*refs):
        ssem, rsem = refs[n + m:n + m + 2]
        outs, token = refs[n + m + 2:2 * n + m + 2], refs[2 * n + m + 2]
        mx, my, mc = _mesh_pos()
        p = 2 * mx + my
        col = [w for w, g in enumerate(geoms) if g[0] == "col"]
        row = [w for w, g in enumerate(geoms) if g[0] == "row"]
        for q in range(4):
            @pl.when(p == q)
            def _(q=q):
                cps = _ag_copies(outs, geoms, ssem, rsem, mx, my, mc, q, mc)
                for w in col:
                    for cp in cps[w]:
                        cp.start()
        for h in range(2):
            @pl.when(mc == h)
            def _(h=h):
                cps = _ag_copies(outs, geoms, ssem, rsem, mx, my, mc, p, h)
                for w in row:
                    for cp in cps[w]:
                        cp.start()
        token[...] = jnp.zeros_like(token)

    res = pl.pallas_call(
        body, name=name,
        out_shape=[pltpu.SemaphoreType.DMA((_ag_sems(geoms),)), pltpu.SemaphoreType.DMA((_ag_sems(geoms),))]
        + [pltpu.HBM(a.shape, a.dtype) for a in fulls] + [jax.ShapeDtypeStruct((8, 128), F32)],
        in_specs=[HBM_SPEC] * n + [ANY] * m,
        out_specs=[SEM_SPEC, SEM_SPEC] + [HBM_SPEC] * n + [pl.BlockSpec(memory_space=pltpu.VMEM)],
        input_output_aliases={w: 2 + w for w in range(n)},
        compiler_params=pltpu.CompilerParams(has_side_effects=pltpu.SideEffectType.DATAFLOW_SIDE_EFFECTING),
    )(*[pltpu.with_memory_space_constraint(a, pltpu.HBM) for a in fulls], *after)
    return res[0], res[1], list(res[2:2 + n]), res[2 + n]


def ag_wait(fulls, geoms, ssem, rsem, after, name):
    n = len(fulls)
    after = list(after) if isinstance(after, (list, tuple)) else [after]

    def body(*refs):
        ins, ssem_ref, rsem_ref = refs[:n], refs[n], refs[n + 1]
        mx, my, mc = _mesh_pos()
        for cps in _ag_copies(ins, geoms, ssem_ref, rsem_ref, mx, my, mc, 0, 0):
            for cp in cps:
                cp.wait_send()
                cp.wait_recv()

    return list(pl.pallas_call(
        body, name=name,
        out_shape=[pltpu.HBM(a.shape, a.dtype) for a in fulls],
        in_specs=[HBM_SPEC] * n + [SEM_SPEC, SEM_SPEC] + [ANY] * len(after),
        out_specs=[HBM_SPEC] * n,
        input_output_aliases={w: w for w in range(n)},
        compiler_params=pltpu.CompilerParams(has_side_effects=pltpu.SideEffectType.DATAFLOW_SIDE_EFFECTING),
    )(*fulls, ssem, rsem, *after))


def ag_forward(full, kind, R, C, name):
    sr, sc = _shard_shape(kind, R, C)
    hr, hc = (sr // 2, sc) if kind == "col" else (sr, sc // 2)
    tr = _row_tile(hr, hc, itemsize=2, budget=512 * 1024)
    nt = hr // tr

    total = 3 * nt

    def body(src_ref, full_ref, stage, lsem, ssem, rsem):
        step = pl.program_id(0) * nt + pl.program_id(1)
        par = step % 2
        mx, my, mc = _mesh_pos()

        def load(s, q, h, t):
            return pltpu.make_async_copy(_region(src_ref, kind, R, C, q, h, t, tr), stage.at[s], lsem.at[s])

        def push(s, q, h, t):
            return pltpu.make_async_remote_copy(src_ref=stage.at[s], dst_ref=_region(full_ref, kind, R, C, q, h, t, tr),
                                                send_sem=ssem.at[s], recv_sem=rsem, device_id=(mx, my, 1 - mc),
                                                device_id_type=MESH)

        def for_tile(stp, fn):
            q_k = _partner_chip(stp // nt, 2 * mx + my)
            if kind == "col":
                for q in range(4):
                    @pl.when(q_k == q)
                    def _(q=q):
                        fn(q, mc, stp % nt)
            else:
                for h in range(2):
                    @pl.when(mc == h)
                    def _(h=h):
                        fn(q_k, h, stp % nt)

        @pl.when(step == 0)
        def _():
            for_tile(step, lambda q, h, t: load(0, q, h, t).start())

        load(par, 0, 0, 0).wait()
        for_tile(step, lambda q, h, t: push(par, q, h, t).start())

        @pl.when(step + 1 < total)
        def _():
            @pl.when(step >= 1)
            def _():
                push(1 - par, 0, 0, 0).wait_send()
            for_tile(step + 1, lambda q, h, t: load(1 - par, q, h, t).start())

        @pl.when(step == total - 1)
        def _():
            push(par, 0, 0, 0).wait_send()
            push(1 - par, 0, 0, 0).wait_send()
            three = full_ref.at[pl.ds(0, hr), pl.ds(0, 3 * hc)] if kind == "col" else full_ref.at[pl.ds(0, 3 * hr), pl.ds(0, hc)]
            pltpu.make_async_remote_copy(src_ref=three, dst_ref=three, send_sem=ssem.at[0], recv_sem=rsem,
                                         device_id=(mx, my, 1 - mc), device_id_type=MESH).wait_recv()

    return pl.pallas_call(
        body, name=name, grid=(3, nt),
        in_specs=[ANY], out_specs=ANY,
        out_shape=jax.ShapeDtypeStruct((R, C), BF16),
        scratch_shapes=[pltpu.VMEM((2, tr, hc), BF16), pltpu.SemaphoreType.DMA((2,)), pltpu.SemaphoreType.DMA((2,)),
                        pltpu.SemaphoreType.DMA],
        input_output_aliases={0: 0},
        compiler_params=_cp("arbitrary", "arbitrary"),
    )(full)


def _half_shape(kind, R, C):
    return (R // 2, C) if kind == "col" else (R, C // 2)


def _piece_shape(kind, R, C):
    return (R // 2, C // 4) if kind == "col" else (R // 4, C // 2)


def pair_push(g, kind, c_arr, name):
    R, C = g.shape
    hr, hc = _half_shape(kind, R, C)
    tr = _row_tile(hr, hc, itemsize=2, budget=1024 * 1024)
    nt = hr // tr

    def body(c_ref, g_ref, out_ref, stage, ssem, rsem):
        i = pl.program_id(0)
        slot = i % 2
        mx, my, mc = _mesh_pos()

        def push(s, t):
            return pltpu.make_async_remote_copy(
                src_ref=stage.at[s], dst_ref=out_ref.at[pl.ds(pl.multiple_of(t * tr, 16), tr)],
                send_sem=ssem.at[s], recv_sem=rsem, device_id=(mx, my, 1 - mc), device_id_type=MESH)

        @pl.when(i >= 2)
        def _():
            push(slot, 0).wait_send()

        stage[slot] = g_ref[...]
        push(slot, i).start()

        @pl.when(i == nt - 1)
        def _():
            push(slot, 0).wait_send()
            if nt >= 2:
                push(1 - slot, 0).wait_send()
            pltpu.make_async_remote_copy(src_ref=out_ref, dst_ref=out_ref, send_sem=ssem.at[0], recv_sem=rsem,
                                         device_id=(mx, my, 1 - mc), device_id_type=MESH).wait_recv()

    if kind == "col":
        g_spec = pl.BlockSpec((tr, hc), lambda i, c: ((1 - c[0]) * nt + i, 0))
    else:
        g_spec = pl.BlockSpec((tr, hc), lambda i, c: (i, 1 - c[0]))
    return pl.pallas_call(
        body, name=name,
        grid_spec=pltpu.PrefetchScalarGridSpec(
            num_scalar_prefetch=1, grid=(nt,), in_specs=[g_spec], out_specs=ANY,
            scratch_shapes=[pltpu.VMEM((2, tr, hc), BF16), pltpu.SemaphoreType.DMA((2,)), pltpu.SemaphoreType.DMA]),
        out_shape=jax.ShapeDtypeStruct((hr, hc), BF16),
        compiler_params=_cp("arbitrary"),
    )(c_arr, g)


def _partner_chip(k, p):
    return p ^ jnp.where(k == 0, 2, jnp.where(k == 1, 1, jnp.where(k == 2, 3, 0)))


def pair_add(g, got, kind, cp_arr, name):
    R, C = g.shape
    pr, pc = _piece_shape(kind, R, C)
    tr = _row_tile(pr, pc, itemsize=2, budget=1024 * 1024)
    nt = pr // tr

    def body(cp_ref, g_ref, got_ref, ps_ref, rb_ref):
        tile = (g_ref[...].astype(F32) + got_ref[...].astype(F32)).astype(BF16)
        ps_ref[...] = tile

        @pl.when(pl.program_id(1) == cp_ref[1])
        def _():
            rb_ref[...] = tile

    if kind == "col":
        g_spec = pl.BlockSpec((tr, pc), lambda i, q, cp: (cp[0] * nt + i, q))
        got_spec = pl.BlockSpec((tr, pc), lambda i, q, cp: (i, q))
    else:
        g_spec = pl.BlockSpec((tr, pc), lambda i, q, cp: (q * nt + i, cp[0]))
        got_spec = pl.BlockSpec((tr, pc), lambda i, q, cp: (q * nt + i, 0))
    return pl.pallas_call(
        body, name=name,
        grid_spec=pltpu.PrefetchScalarGridSpec(
            num_scalar_prefetch=1, grid=(nt, 4), in_specs=[g_spec, got_spec],
            out_specs=[pl.BlockSpec((None, tr, pc), lambda i, q, cp: (q, i, 0)),
                       pl.BlockSpec((None, tr, pc), lambda i, q, cp: (cp[1], i, 0))]),
        out_shape=[jax.ShapeDtypeStruct((4, pr, pc), BF16)] * 2,
        compiler_params=_cp("arbitrary", "arbitrary"),
    )(cp_arr, g, got)


def _rs_copies(ps, rb, ssem, rsem, mx, my, mc):
    p = 2 * mx + my
    out = []
    for w in range(len(ps)):
        for k, chip in enumerate(_other_chips(mx, my)):
            out.append(pltpu.make_async_remote_copy(
                src_ref=ps[w].at[2 * chip[0] + chip[1]], dst_ref=rb[w].at[p], send_sem=ssem.at[3 * w + k],
                recv_sem=rsem.at[3 * w + k], device_id=(*chip, mc), device_id_type=MESH))
    return out


def rs_start(ps, rb, after, name):
    n = len(ps)
    after = list(after)
    m = len(after)

    def body(*refs):
        ssem, rsem = refs[2 * n + m:2 * n + m + 2]
        ps_o = refs[2 * n + m + 2:3 * n + m + 2]
        rb_o = refs[3 * n + m + 2:4 * n + m + 2]
        token = refs[4 * n + m + 2]
        for cp in _rs_copies(ps_o, rb_o, ssem, rsem, *_mesh_pos()):
            cp.start()
        token[...] = jnp.zeros_like(token)

    both = list(ps) + list(rb)
    res = pl.pallas_call(
        body, name=name,
        out_shape=[pltpu.SemaphoreType.DMA((3 * n,)), pltpu.SemaphoreType.DMA((3 * n,))]
        + [pltpu.HBM(a.shape, a.dtype) for a in both] + [jax.ShapeDtypeStruct((8, 128), F32)],
        in_specs=[HBM_SPEC] * (2 * n) + [ANY] * m,
        out_specs=[SEM_SPEC, SEM_SPEC] + [HBM_SPEC] * (2 * n) + [pl.BlockSpec(memory_space=pltpu.VMEM)],
        input_output_aliases={w: 2 + w for w in range(2 * n)},
        compiler_params=pltpu.CompilerParams(has_side_effects=pltpu.SideEffectType.DATAFLOW_SIDE_EFFECTING),
    )(*[pltpu.with_memory_space_constraint(a, pltpu.HBM) for a in both], *after)
    return res[0], res[1], list(res[2:2 + n]), list(res[2 + n:2 + 2 * n]), res[2 + 2 * n]


def rs_wait(ps, rb, ssem, rsem, after, name):
    n = len(ps)
    after = list(after)
    m = len(after)

    def body(*refs):
        ps_i, rb_i = refs[:n], refs[n:2 * n]
        ssem_ref, rsem_ref = refs[2 * n], refs[2 * n + 1]
        for cp in _rs_copies(ps_i, rb_i, ssem_ref, rsem_ref, *_mesh_pos()):
            cp.wait_send()
            cp.wait_recv()

    both = list(ps) + list(rb)
    res = pl.pallas_call(
        body, name=name,
        out_shape=[pltpu.HBM(a.shape, a.dtype) for a in both],
        in_specs=[HBM_SPEC] * (2 * n) + [SEM_SPEC, SEM_SPEC] + [ANY] * m,
        out_specs=[HBM_SPEC] * (2 * n),
        input_output_aliases={w: w for w in range(2 * n)},
        compiler_params=pltpu.CompilerParams(has_side_effects=pltpu.SideEffectType.DATAFLOW_SIDE_EFFECTING),
    )(*both, ssem, rsem, *after)
    return list(res[n:])


def sum_share(parts, kind, R, C, name):
    _, pr, pc = parts.shape
    sr, sc = _shard_shape(kind, R, C)
    tr = _row_tile(pr, pc * 4, budget=4 * 1024 * 1024)
    nt = pr // tr

    def body(p_ref, fin_ref, stage, lsem, ssem, rsem):
        i = pl.program_id(0)
        slot = i % 2
        mx, my, mc = _mesh_pos()

        def region(h, t):
            r0 = pl.multiple_of(t * tr, 8)
            if kind == "col":
                return fin_ref.at[pl.ds(pl.multiple_of(h * pr + r0, 8), tr)]
            return fin_ref.at[pl.ds(r0, tr), pl.ds(h * pc, pc)]

        def copies(s, h, t):
            return (pltpu.make_async_copy(stage.at[s], region(h, t), lsem.at[s]),
                    pltpu.make_async_remote_copy(src_ref=stage.at[s], dst_ref=region(h, t), send_sem=ssem.at[s],
                                                 recv_sem=rsem, device_id=(mx, my, 1 - mc), device_id_type=MESH))

        def wait_sent(s):
            loc, rem = copies(s, 0, 0)
            loc.wait()
            rem.wait_send()

        @pl.when(i >= 2)
        def _():
            wait_sent(slot)

        acc = p_ref[0].astype(F32)
        for k in range(1, 4):
            acc = acc + p_ref[k].astype(F32)
        stage[slot] = acc
        if kind == "col":
            for cp in copies(slot, mc, i):
                cp.start()
        else:
            for h in range(2):
                @pl.when(mc == h)
                def _(h=h):
                    for cp in copies(slot, h, i):
                        cp.start()

        @pl.when(i == nt - 1)
        def _():
            wait_sent(slot)
            if nt >= 2:
                wait_sent(1 - slot)
            half = fin_ref.at[pl.ds(0, pr), pl.ds(0, pc)]
            pltpu.make_async_remote_copy(src_ref=half, dst_ref=half, send_sem=ssem.at[0], recv_sem=rsem,
                                         device_id=(mx, my, 1 - mc), device_id_type=MESH).wait_recv()

    return pl.pallas_call(
        body, name=name, grid=(nt,),
        in_specs=[pl.BlockSpec((4, tr, pc), lambda i: (0, i, 0))],
        out_specs=ANY,
        out_shape=jax.ShapeDtypeStruct((sr, sc), F32),
        scratch_shapes=[pltpu.VMEM((2, tr, pc), F32), pltpu.SemaphoreType.DMA((2,)), pltpu.SemaphoreType.DMA((2,)),
                        pltpu.SemaphoreType.DMA],
        compiler_params=_cp("arbitrary"),
    )(parts)


def _pack(parts, rows):
    flat = []
    for a in parts:
        a = jnp.ravel(a).astype(F32)
        flat.append(jnp.pad(a, (0, (-a.shape[0]) % 128)))
    v = jnp.concatenate(flat)
    return jnp.pad(v, (0, rows * 128 - v.shape[0])).reshape(rows, 128)


def _unpack(block, shapes):
    lead = block.shape[:-2]
    v = block.reshape(lead + (-1,))
    out, off = [], 0
    for shp in shapes:
        n = int(np.prod(shp))
        out.append(v[..., off:off + n].reshape(lead + tuple(shp)))
        off += n + (-n) % 128
    return out


def _block_diag4(w):
    w4 = w.reshape(4, 4, 64, 64)
    eye = jnp.eye(4, dtype=w.dtype)
    return (w4[:, :, :, None, :] * eye[None, :, None, :, None]).reshape(4, 256, 256)


def _diag_blocks(bd):
    b5 = bd.reshape(4, 4, 64, 4, 64)
    return jnp.stack([b5[:, i, :, i, :] for i in range(4)], axis=1).reshape(16, 64, 64)


def _bias_window(rel_bias):
    m = (np.arange(768) + 127) % 768 - 127
    w = rel_bias[:, np.clip(512 - m, -128, 128) + 128]
    win = jnp.tile(w, (1, 128))[:, :128 * 767].reshape(8, 128, 767)[:, :, :WIN]
    qh = np.arange(128)[:, None] // CHUNK
    kc = np.arange(WIN)[None, :] // CHUNK
    valid = (kc >= qh) & (kc <= qh + 8)
    return jnp.where(jnp.asarray(valid)[None], win, NEG)


SMALL = ("b_ada", "norm_pre", "norm_post", "rel_bias", "conv_w", "conv_b", "lru_wa", "lru_ba", "lru_wx",
         "lru_bx", "lru_lambda")
WEIGHTS = ("w_ada", "b_ada", "norm_pre", "norm_post", "ffn1_w_gu", "ffn1_w_down", "w_in", "rel_bias", "conv_w",
           "conv_b", "lru_wa", "lru_ba", "lru_wx", "lru_bx", "lru_lambda", "w_att_o", "w_rec_o", "w_out",
           "ffn2_w_gu", "ffn2_w_down")


def kernel(x, c, w_ada, b_ada, norm_pre, norm_post, ffn1_w_gu, ffn1_w_down, w_in, rel_bias, conv_w, conv_b, lru_wa, lru_ba, lru_wx, lru_bx, lru_lambda, w_att_o, w_rec_o, w_out, ffn2_w_gu, ffn2_w_down, loss_target, m_w_ada, m_b_ada, m_norm_pre, m_norm_post, m_ffn1_w_gu, m_ffn1_w_down, m_w_in, m_rel_bias, m_conv_w, m_conv_b, m_lru_wa, m_lru_ba, m_lru_wx, m_lru_bx, m_lru_lambda, m_w_att_o, m_w_rec_o, m_w_out, m_ffn2_w_gu, m_ffn2_w_down, v_w_ada, v_b_ada, v_norm_pre, v_norm_post, v_ffn1_w_gu, v_ffn1_w_down, v_w_in, v_rel_bias, v_conv_w, v_conv_b, v_lru_wa, v_lru_ba, v_lru_wx, v_lru_bx, v_lru_lambda, v_w_att_o, v_w_rec_o, v_w_out, v_ffn2_w_gu, v_ffn2_w_down):
    W = dict(w_ada=w_ada, b_ada=b_ada, norm_pre=norm_pre, norm_post=norm_post, ffn1_w_gu=ffn1_w_gu,
             ffn1_w_down=ffn1_w_down, w_in=w_in, rel_bias=rel_bias, conv_w=conv_w, conv_b=conv_b, lru_wa=lru_wa,
             lru_ba=lru_ba, lru_wx=lru_wx, lru_bx=lru_bx, lru_lambda=lru_lambda, w_att_o=w_att_o, w_rec_o=w_rec_o,
             w_out=w_out, ffn2_w_gu=ffn2_w_gu, ffn2_w_down=ffn2_w_down)
    M = dict(w_ada=m_w_ada, b_ada=m_b_ada, norm_pre=m_norm_pre, norm_post=m_norm_post, ffn1_w_gu=m_ffn1_w_gu,
             ffn1_w_down=m_ffn1_w_down, w_in=m_w_in, rel_bias=m_rel_bias, conv_w=m_conv_w, conv_b=m_conv_b,
             lru_wa=m_lru_wa, lru_ba=m_lru_ba, lru_wx=m_lru_wx, lru_bx=m_lru_bx, lru_lambda=m_lru_lambda,
             w_att_o=m_w_att_o, w_rec_o=m_w_rec_o, w_out=m_w_out, ffn2_w_gu=m_ffn2_w_gu, ffn2_w_down=m_ffn2_w_down)
    V = dict(w_ada=v_w_ada, b_ada=v_b_ada, norm_pre=v_norm_pre, norm_post=v_norm_post, ffn1_w_gu=v_ffn1_w_gu,
             ffn1_w_down=v_ffn1_w_down, w_in=v_w_in, rel_bias=v_rel_bias, conv_w=v_conv_w, conv_b=v_conv_b,
             lru_wa=v_lru_wa, lru_ba=v_lru_ba, lru_wx=v_lru_wx, lru_bx=v_lru_bx, lru_lambda=v_lru_lambda,
             w_att_o=v_w_att_o, w_rec_o=v_w_rec_o, w_out=v_w_out, ffn2_w_gu=v_ffn2_w_gu, ffn2_w_down=v_ffn2_w_down)
    mx, my, mc = _mesh_pos()
    p = 2 * mx + my
    e = 4 * mx + 2 * my + mc
    xs = x[0]

    c_arr = jnp.reshape(mc, (1,)).astype(jnp.int32)
    cp_arr = jnp.stack([mc, p]).astype(jnp.int32)
    p_arr = jnp.reshape(p, (1,)).astype(jnp.int32)
    direct = ("w_att_o", "w_rec_o", "w_out", "ffn2_w_gu", "ffn2_w_down")
    geoms = [(kind, R, C, n in direct) for (n, kind, R, C) in BIG]
    names = [b[0] for b in BIG]
    placed = [ag_local(W[n][0], kind, R, C, p_arr, "ag_local_" + n) for (n, kind, R, C) in BIG[:2]]

    def arrived(fly, lo, hi, ssem, rsem, after, tag):
        done = ag_wait(fly, geoms[lo:hi], ssem, rsem, after, "ag_wait_" + tag)
        return [a if both else ag_forward(a, kind, R, C, "ag_forward_" + n)
                for a, (kind, R, C, both), n in zip(done, geoms[lo:hi], names[lo:hi])]

    g1 = ag_small(_pack([c, norm_pre, norm_post, conv_w], 32), "ag_small_params")
    c_all, npre4, npost4, cw4 = _unpack(g1, [(D,), (3, 256), (3, 256), (4, 256)])
    chipwise = lambda a: jnp.moveaxis(a[0::2], 0, 1).reshape(a.shape[1], D)
    npre, npost, conv_full = chipwise(npre4), chipwise(npost4), chipwise(cw4)

    b_cols = lax.dynamic_slice(b_ada, (0, p * 2304), (1, 2304))
    mod_cols = ada_fwd(c_all, w_ada[0], b_cols, "ada_fwd")
    g2 = ag_small(mod_cols.reshape(144, 128), "ag_mod")
    mod_all = jnp.moveaxis(g2[0::2].reshape(4, 8, 2304), 0, 1).reshape(8, 9 * D)
    mod = lax.dynamic_index_in_dim(mod_all, e, 0, keepdims=False).reshape(3, 3, D)
    zeros3 = jnp.zeros((3, D), F32)
    vecs = [jnp.concatenate([npre[k:k + 1], npost[k:k + 1], mod[k], zeros3], axis=0) for k in range(3)]

    gu_s, gu_r, gu_fly, tok_gu = ag_start(placed[:1], geoms[:1], [g2], "ag_start_ffn1_gu")
    dn_s, dn_r, dn_fly, tok0 = ag_start(placed[1:2], geoms[1:2], [tok_gu], "ag_start_ffn1_down")
    placed += [ag_local(W[n][0], kind, R, C, p_arr, "ag_local_" + n, after=[tok0]) for (n, kind, R, C) in BIG[2:]]
    f1_gu, = arrived(gu_fly, 0, 1, gu_s, gu_r, placed[2:], "ffn1_gu")
    f1_dn, = arrived(dn_fly, 1, 2, dn_s, dn_r, f1_gu, "ffn1_down")
    mix_s, mix_r, mix_fly, tok1 = ag_start(placed[2:6], geoms[2:6], [f1_gu, f1_dn], "ag_start_mixer")
    ffn_s, ffn_r, ffn_fly, tok2 = ag_start(placed[6:], geoms[6:], [tok1], "ag_start_ffn2")
    wa_bd = _block_diag4(lru_wa[0]).astype(BF16)
    wx_bd = _block_diag4(lru_wx[0]).astype(BF16)
    pvec = jnp.concatenate([conv_full, conv_b, lru_ba, lru_bx, lru_lambda], axis=0)
    bias = _bias_window(rel_bias[0]).reshape(4, 256, WIN)

    f1_u = f1_gu[:, FF:]
    x1, h1, g1_, u1, a1, f1 = ffn_fwd(xs, vecs[0] + tok2[0:1, 0:1], f1_gu, f1_u, f1_dn, 0.5, "ffn1_fwd")
    win, wao, wro, wout = arrived(mix_fly, 2, 6, mix_s, mix_r, x1, "mixer")
    h2, qkv, rest = proj_fwd(x1, vecs[1], win, "proj_fwd")
    ao = attn_fwd(qkv, bias, "attn_fwd")
    hl, hg = lru_fwd(rest, pvec, wa_bd, wx_bd, "lru_fwd")
    x2, att, rec, mg, f2 = mix_out_fwd(x1, ao, hg, rest, vecs[1], wao, wro, wout, "mix_out_fwd")
    f2_gu, f2_dn = arrived(ffn_fly, 6, 8, ffn_s, ffn_r, x2, "ffn2")
    f2_u = f2_gu[:, FF:]
    dy, h3, g3_, u3, a3, f3, lvec = ffn_fwd(x2, vecs[2], f2_gu, f2_u, f2_dn, 0.5, "ffn2_fwd", tgt=loss_target[0])

    G, grads = {}, {}
    geo = {n: (kind, R, C) for (n, kind, R, C) in BIG}

    def reduce_begin(names, tag):
        ps, rb = [], []
        for n in names:
            got = pair_push(G[n], geo[n][0], c_arr, "rs_push_" + n)
            a, b = pair_add(G[n], got, geo[n][0], cp_arr, "rs_pair_sum_" + n)
            ps.append(a)
            rb.append(b)
        return rs_start(ps, rb, [], "rs_start_" + tag)

    def reduce_end(names, flight, after, tag):
        ssem, rsem, ps, rb, _ = flight
        for a, n in zip(rs_wait(ps, rb, ssem, rsem, after, "rs_wait_" + tag), names):
            grads[n] = sum_share(a, *geo[n], "rs_sum_share_" + n)[None]

    dx2, df3, dgu3, va2 = ffn_bwd(dy, x2, f3, g3_, u3, vecs[2], f2_gu, f2_u, f2_dn, 0.5, "ffn2_bwd")
    G["ffn2_w_gu"] = mm_tn(h3, dgu3, "dw_ffn2_gu", D, 1408, 1024)
    G["ffn2_w_down"] = mm_tn(a3, df3, "dw_ffn2_down", 1408, D, 1024)
    fly_ffn2 = reduce_begin(("ffn2_w_gu", "ffn2_w_down"), "ffn2")
    vec1 = vecs[1] + fly_ffn2[4][0:1, 0:1]
    df2, d_att, d_rec, dao, dhl, d3, va_out = mix_out_bwd(dx2, f2, att, rec, rest, hl, vec1, wao, wro, wout,
                                                          "mix_out_bwd")
    G["w_out"] = mm_tn(mg, df2, "dw_out", D, D, 1024)
    G["w_att_o"] = mm_tn(ao, d_att, "dw_att_o", 512, D, 1024)
    G["w_rec_o"] = mm_tn(hg, d_rec, "dw_rec_o", D, D, 1024)
    dq, db, dkv = attn_bwd(qkv, dao, bias, "attn_bwd")
    dxr, v_lru, dwa_bd, dwx_bd = lru_bwd(dhl, hl, rest, pvec, wa_bd, wx_bd, "lru_bwd")
    dx1, va_in = proj_bwd(dq, dkv, dxr, d3, win, x1, dx2, vecs[1], "proj_bwd")
    G["w_in"] = dw_in(h2, dq, dkv, dxr, d3, "dw_in")
    fly_mix = reduce_begin(("w_in", "w_att_o", "w_rec_o", "w_out"), "mixer")
    vec0 = vecs[0] + fly_mix[4][0:1, 0:1]
    dx0, df1, dgu1, va0 = ffn_bwd(dx1, xs, f1, g1_, u1, vec0, f1_gu, f1_u, f1_dn, 0.5, "ffn1_bwd")
    G["ffn1_w_gu"] = mm_tn(h1, dgu1, "dw_ffn1_gu", D, 1408, 1024)
    G["ffn1_w_down"] = mm_tn(a1, df1, "dw_ffn1_down", 1408, D, 1024)
    fly_ffn1 = reduce_begin(("ffn1_w_gu", "ffn1_w_down"), "ffn1")
    reduce_end(("ffn2_w_gu", "ffn2_w_down"), fly_ffn2, [fly_ffn1[4]], "ffn2")
    reduce_end(("w_in", "w_att_o", "w_rec_o", "w_out"), fly_mix, [fly_ffn1[4], grads["ffn2_w_down"]], "mixer")

    va1 = va_out + va_in
    vas = (va0, va1, va2)
    dmod = jnp.stack([v[2:5] for v in vas])
    part = {"b_ada": dmod, "norm_pre": jnp.stack([v[0] for v in vas]), "norm_post": jnp.stack([v[1] for v in vas]),
            "rel_bias": bias_grad(db.reshape(8, 128, WIN), "bias_grad")[:, :257], "conv_w": v_lru[0:4], "conv_b": v_lru[4],
            "lru_wa": _diag_blocks(dwa_bd), "lru_ba": v_lru[5], "lru_wx": _diag_blocks(dwx_bd), "lru_bx": v_lru[6],
            "lru_lambda": v_lru[7]}
    full_shapes = {"b_ada": (9 * D,), "norm_pre": (3, D), "norm_post": (3, D), "rel_bias": (8, 257),
                   "conv_w": (4, D), "conv_b": (D,), "lru_wa": (16, 64, 64), "lru_ba": (D,),
                   "lru_wx": (16, 64, 64), "lru_bx": (D,), "lru_lambda": (D,)}
    g3 = ag_small(_pack([part[n] for n in SMALL] + [lvec[0:1, 0:1]], 1232), "ag_small_grads")
    summed = _unpack(sum_lead(g3, "sum_small_grads"), [full_shapes[n] for n in SMALL] + [(1,)])
    red = dict(zip(SMALL, summed[:-1]))
    loss = summed[-1][0]
    cols = lambda a: lax.dynamic_slice(a, (0, p * 256), (a.shape[0], 256))
    grads.update({"b_ada": red["b_ada"][None], "norm_pre": cols(red["norm_pre"])[None],
                  "norm_post": cols(red["norm_post"])[None], "rel_bias": red["rel_bias"][None],
                  "conv_w": cols(red["conv_w"])[None], "conv_b": red["conv_b"][None], "lru_wa": red["lru_wa"][None],
                  "lru_ba": red["lru_ba"][None], "lru_wx": red["lru_wx"][None], "lru_bx": red["lru_bx"][None],
                  "lru_lambda": red["lru_lambda"][None]})

    dmod_all = g3[:, :72].reshape(8, 9 * D)
    dmod_cols = jnp.pad(lax.dynamic_slice(dmod_all, (0, p * 2304), (8, 2304)), ((0, 120), (0, 0)))
    c_all_t = jnp.pad(c_all.T, ((0, 0), (0, 120)))
    grads["w_ada"] = ada_bwd(c_all_t, dmod_cols, "ada_bwd")[None]

    delta, new_m, new_v = {}, {}, {}

    def update(n):
        shp = W[n].shape
        res = adamw(W[n][0], grads[n][0], M[n][0], V[n][0], "adamw_" + n, emit_g=n in geo)
        delta[n], new_m[n], new_v[n] = [a.reshape(shp) for a in res[:3]]
        if n in geo:
            grads[n] = res[3].reshape(shp)

    for n in ("w_ada", "ffn2_w_gu", "ffn2_w_down", "w_in", "w_att_o", "w_rec_o", "w_out"):
        update(n)
    packed = [_pack([src[n] for n in SMALL], 1168) for src in (W, grads, M, V)]
    outs = adamw(*packed, "adamw_small")
    for dst, blk in zip((delta, new_m, new_v), outs):
        for n, a in zip(SMALL, _unpack(blk, [W[n].shape for n in SMALL])):
            dst[n] = a
    reduce_end(("ffn1_w_gu", "ffn1_w_down"), fly_ffn1,
               [outs[0], delta["w_ada"], delta["ffn2_w_gu"], delta["ffn2_w_down"], delta["w_in"], delta["w_out"]], "ffn1")
    for n in ("ffn1_w_gu", "ffn1_w_down"):
        update(n)

    return (loss, dx0[None], *[grads[n] for n in WEIGHTS], *[delta[n] for n in WEIGHTS],
            *[new_m[n] for n in WEIGHTS], *[new_v[n] for n in WEIGHTS])
```

```python
import functools

import numpy as np
import jax
import jax.numpy as jnp
from jax import lax
from jax.experimental import pallas as pl
from jax.experimental.pallas import tpu as pltpu

F32 = jnp.float32
BF16 = jnp.bfloat16

D = 1024
FF = 2816
PW = 5632
HP = 128
CHUNK = 64
WIN = 640
TQ = 512
EPS = 1e-6
NEG = -1e30
LRU_C = 8.0
N_DEV = 8
VMEM_LIMIT = 56 * 1024 * 1024

ADAM_LR, ADAM_B1, ADAM_B2, ADAM_EPS, ADAM_WD, ADAM_STEP = 0.001, 0.9, 0.999, 1e-08, 0.01, 10

MESH = pl.DeviceIdType.MESH
ANY = pl.BlockSpec(memory_space=pl.ANY)


def _cp(*sem):
    return pltpu.CompilerParams(dimension_semantics=tuple(sem), vmem_limit_bytes=VMEM_LIMIT)


def _dot(a, b):
    return jnp.dot(a, b, preferred_element_type=F32)


def _dot_nt(a, b):
    return lax.dot_general(a, b, (((1,), (1,)), ((), ())), preferred_element_type=F32)


def _dot_tn(a, b):
    return lax.dot_general(a, b, (((0,), (0,)), ((), ())), preferred_element_type=F32)


def _mean(v):
    return jnp.mean(v, axis=-1, keepdims=True)


def _colsum(v):
    return jnp.sum(v, axis=0, keepdims=True)


def _sigmoid(v):
    return 0.5 * jnp.tanh(0.5 * v) + 0.5


_GK = 0.7978845608028654


def _gelu(v):
    t = jnp.tanh(_GK * (v + 0.044715 * v * v * v))
    return 0.5 * v * (1.0 + t)


def _pre_norm(xv, vec_ref):
    r = lax.rsqrt(_mean(xv * xv) + EPS)
    n = xv * r * vec_ref[0:1, :]
    return n * (1.0 + vec_ref[3:4, :]) + vec_ref[2:3, :]


def _pre_norm_bwd(dh, xv, dres, vec_ref, vacc_ref):
    r = lax.rsqrt(_mean(xv * xv) + EPS)
    xh = xv * r
    n = xh * vec_ref[0:1, :]
    vacc_ref[2:3, :] += _colsum(dh)
    vacc_ref[3:4, :] += _colsum(dh * n)
    dn = dh * (1.0 + vec_ref[3:4, :])
    vacc_ref[0:1, :] += _colsum(dn * xh)
    dxh = dn * vec_ref[0:1, :]
    return r * (dxh - xh * _mean(dxh * xh)) + dres


def _post_norm_bwd(dxo, fv, res, vec_ref, vacc_ref):
    rf = lax.rsqrt(_mean(fv * fv) + EPS)
    fh = fv * rf
    gp = vec_ref[1:2, :]
    vacc_ref[4:5, :] += _colsum(res * dxo * (fh * gp))
    dy = (res * vec_ref[4:5, :]) * dxo
    vacc_ref[1:2, :] += _colsum(dy * fh)
    dfn = dy * gp
    return rf * (dfn - fh * _mean(dfn * fh))


def ffn_fwd(x, vec, w_gu, w_u, w_dn, res, name, tgt=None, tm=1024, tf=512):
    S = x.shape[0]
    tm = min(tm, S)
    nt = S // tm
    nf = -(-FF // tf)
    tail = FF - tf * (nf - 1)
    head = tgt is not None

    def body(*refs):
        x_ref, vec_ref, wg_ref, wu_ref, wd_ref = refs[:5]
        if head:
            t_ref, xo_ref, h_ref, g_ref, u_ref, a_ref, f_ref, l_ref, hs, acc, lacc = refs[5:]
        else:
            xo_ref, h_ref, g_ref, u_ref, a_ref, f_ref, hs, acc = refs[5:]
        i, j = pl.program_id(0), pl.program_id(1)

        @pl.when(j == 0)
        def _():
            h = _pre_norm(x_ref[...], vec_ref).astype(BF16)
            hs[...] = h
            h_ref[...] = h
            acc[...] = jnp.zeros_like(acc)

        def chunk(w):
            h = hs[...]
            g = _dot(h, wg_ref[:, 0:w])
            u = _dot(h, wu_ref[:, 0:w])
            g_ref[:, 0:w] = g.astype(BF16)
            u_ref[:, 0:w] = u.astype(BF16)
            a = (g * _sigmoid(g) * u).astype(BF16)
            a_ref[:, 0:w] = a
            acc[...] += _dot(a, wd_ref[0:w, :])

        @pl.when(j < nf - 1)
        def _():
            chunk(tf)

        @pl.when(j == nf - 1)
        def _():
            chunk(tail)
            f = acc[...]
            f_ref[...] = f.astype(BF16)
            y = f * lax.rsqrt(_mean(f * f) + EPS) * vec_ref[1:2, :]
            xo = x_ref[...] + (res * vec_ref[4:5, :]) * y
            if head:
                @pl.when(i == 0)
                def _():
                    lacc[...] = jnp.zeros_like(lacc)

                d = xo - t_ref[...]
                xo_ref[...] = d * (1.0 / D)
                lacc[...] += _colsum(d * d)

                @pl.when(i == nt - 1)
                def _():
                    l_ref[...] = jnp.broadcast_to(0.5 * jnp.sum(lacc[...]) * (1.0 / D), (8, 128))
            else:
                xo_ref[...] = xo

    row = lambda i, j: (i, 0)
    col = lambda i, j: (i, j)
    once = dict(pipeline_mode=pl.Buffered(1)) if head else {}
    in_specs = [pl.BlockSpec((tm, D), row, **once), pl.BlockSpec((8, D), lambda i, j: (0, 0)),
                pl.BlockSpec((D, tf), lambda i, j: (0, j)), pl.BlockSpec((D, tf), lambda i, j: (0, j)),
                pl.BlockSpec((tf, D), lambda i, j: (j, 0))]
    out_specs = [pl.BlockSpec((tm, D), row), pl.BlockSpec((tm, D), row), pl.BlockSpec((tm, tf), col),
                 pl.BlockSpec((tm, tf), col), pl.BlockSpec((tm, tf), col), pl.BlockSpec((tm, D), row)]
    out_shape = [jax.ShapeDtypeStruct((S, D), F32), jax.ShapeDtypeStruct((S, D), BF16),
                 jax.ShapeDtypeStruct((S, FF), BF16), jax.ShapeDtypeStruct((S, FF), BF16),
                 jax.ShapeDtypeStruct((S, FF), BF16), jax.ShapeDtypeStruct((S, D), BF16)]
    scratch = [pltpu.VMEM((tm, D), BF16), pltpu.VMEM((tm, D), F32)]
    args = [x, vec, w_gu, w_u, w_dn]
    if head:
        in_specs.append(pl.BlockSpec((tm, D), row, **once))
        out_specs.append(pl.BlockSpec((8, 128), lambda i, j: (0, 0)))
        out_shape.append(jax.ShapeDtypeStruct((8, 128), F32))
        scratch.append(pltpu.VMEM((1, D), F32))
        args.append(tgt)
    return pl.pallas_call(
        body, name=name, grid=(nt, nf), in_specs=in_specs, out_specs=out_specs, out_shape=out_shape,
        scratch_shapes=scratch,
        compiler_params=_cp("arbitrary" if head else "parallel", "arbitrary"),
    )(*args)


def ffn_bwd(dxo, x, f, g, u, vec, w_gu, w_u, w_dn, res, name, tm=1024, tf=512):
    S = x.shape[0]
    tm = min(tm, S)
    nf = -(-FF // tf)
    tail = FF - tf * (nf - 1)

    def body(dxo_ref, x_ref, f_ref, g_ref, u_ref, vec_ref, wg_ref, wu_ref, wd_ref,
             dx_ref, df_ref, dgu_ref, vacc_ref, dfs, acc):
        i, j = pl.program_id(0), pl.program_id(1)

        @pl.when((i == 0) & (j == 0))
        def _():
            vacc_ref[...] = jnp.zeros_like(vacc_ref)

        @pl.when(j == 0)
        def _():
            df = _post_norm_bwd(dxo_ref[...], f_ref[...].astype(F32), res, vec_ref, vacc_ref).astype(BF16)
            dfs[...] = df
            df_ref[...] = df
            acc[...] = jnp.zeros_like(acc)

        def chunk(w):
            da = _dot_nt(dfs[...], wd_ref[0:w, :])
            gv, uv = g_ref[:, 0:w].astype(F32), u_ref[:, 0:w].astype(F32)
            sg = _sigmoid(gv)
            dg = (da * uv * (sg * (1.0 + gv * (1.0 - sg)))).astype(BF16)
            du = (da * (gv * sg)).astype(BF16)
            dgu_ref[0, :, 0:w] = dg
            dgu_ref[1, :, 0:w] = du
            acc[...] += _dot_nt(dg, wg_ref[:, 0:w]) + _dot_nt(du, wu_ref[:, 0:w])

        @pl.when(j < nf - 1)
        def _():
            chunk(tf)

        @pl.when(j == nf - 1)
        def _():
            chunk(tail)
            dx_ref[...] = _pre_norm_bwd(acc[...], x_ref[...], dxo_ref[...], vec_ref, vacc_ref)

    row = lambda i, j: (i, 0)
    col = lambda i, j: (i, j)
    return pl.pallas_call(
        body, name=name, grid=(S // tm, nf),
        in_specs=[pl.BlockSpec((tm, D), row), pl.BlockSpec((tm, D), row, pipeline_mode=pl.Buffered(1)),
                  pl.BlockSpec((tm, D), row)]
        + [pl.BlockSpec((tm, tf), col), pl.BlockSpec((tm, tf), col),
                  pl.BlockSpec((8, D), lambda i, j: (0, 0)),
                  pl.BlockSpec((D, tf), lambda i, j: (0, j)), pl.BlockSpec((D, tf), lambda i, j: (0, j)),
                  pl.BlockSpec((tf, D), lambda i, j: (j, 0))],
        out_specs=[pl.BlockSpec((tm, D), row), pl.BlockSpec((tm, D), row),
                   pl.BlockSpec((2, tm, tf), lambda i, j: (0, i, j)),
                   pl.BlockSpec((8, D), lambda i, j: (0, 0))],
        out_shape=[jax.ShapeDtypeStruct((S, D), F32), jax.ShapeDtypeStruct((S, D), BF16),
                   jax.ShapeDtypeStruct((2, S, FF), BF16), jax.ShapeDtypeStruct((8, D), F32)],
        scratch_shapes=[pltpu.VMEM((tm, D), BF16), pltpu.VMEM((tm, D), F32)],
        compiler_params=_cp("arbitrary", "arbitrary"),
    )(dxo, x, f, g, u, vec, w_gu, w_u, w_dn)


def mm_tn(a, b, name, tm, tn, tk, out_dtype=BF16):
    S, M = a.shape
    if b.ndim == 3:
        G, _, Nf = b.shape
    else:
        G, Nf = 1, b.shape[1]
    N = G * Nf
    tk = min(tk, S)
    nbf = Nf // tn
    nk = S // tk

    def body(a_ref, b_ref, o_ref, acc):
        k = pl.program_id(2)

        @pl.when(k == 0)
        def _():
            acc[...] = jnp.zeros_like(acc)

        acc[...] += _dot_tn(a_ref[...], b_ref[...])

        @pl.when(k == nk - 1)
        def _():
            o_ref[...] = acc[...].astype(out_dtype)

    if b.ndim == 3:
        b_spec = pl.BlockSpec((None, tk, tn), lambda i, j, k: (j // nbf, k, j % nbf))
    else:
        b_spec = pl.BlockSpec((tk, tn), lambda i, j, k: (k, j))
    return pl.pallas_call(
        body, name=name, grid=(M // tm, N // tn, nk),
        in_specs=[pl.BlockSpec((tk, tm), lambda i, j, k: (k, i)), b_spec],
        out_specs=pl.BlockSpec((tm, tn), lambda i, j, k: (i, j)),
        out_shape=jax.ShapeDtypeStruct((M, N), out_dtype),
        scratch_shapes=[pltpu.VMEM((tm, tn), F32)],
        compiler_params=_cp("parallel", "parallel", "arbitrary"),
    )(a, b)


def proj_fwd(x, vec, w_in, name, tm=2048, tn=512):
    S = x.shape[0]
    tm = min(tm, S)
    nq = 1536 // tn

    def body(x_ref, vec_ref, w_ref, h_ref, qkv_ref, rest_ref, hs):
        j = pl.program_id(1)

        @pl.when(j == 0)
        def _():
            h = _pre_norm(x_ref[...], vec_ref).astype(BF16)
            hs[...] = h
            h_ref[...] = h

        r = _dot(hs[...], w_ref[...])

        @pl.when(j < nq)
        def _():
            qkv_ref[...] = r.astype(BF16)

        @pl.when(j >= nq)
        def _():
            rest_ref[...] = r.astype(BF16)

    row = lambda i, j: (i, 0)
    return pl.pallas_call(
        body, name=name, grid=(S // tm, PW // tn),
        in_specs=[pl.BlockSpec((tm, D), row), pl.BlockSpec((8, D), lambda i, j: (0, 0)),
                  pl.BlockSpec((D, tn), lambda i, j: (0, j))],
        out_specs=[pl.BlockSpec((tm, D), row),
                   pl.BlockSpec((tm, tn), lambda i, j: (i, jnp.minimum(j, nq - 1))),
                   pl.BlockSpec((tm, tn), lambda i, j: (i, jnp.maximum(j - nq, 0)))],
        out_shape=[jax.ShapeDtypeStruct((S, D), BF16), jax.ShapeDtypeStruct((S, 1536), BF16),
                   jax.ShapeDtypeStruct((S, 4096), BF16)],
        scratch_shapes=[pltpu.VMEM((tm, D), BF16)],
        compiler_params=_cp("parallel", "arbitrary"),
    )(x, vec, w_in)


def proj_bwd(dq, dkv, dxr, d3, w_in, x, dxo, vec, name, tm=1024, tk=512):
    S = x.shape[0]
    tm = min(tm, S)
    nk = PW // tk

    def body(dq_ref, dkv_ref, dxr_ref, d3_ref, w_ref, x_ref, dxo_ref, vec_ref, dx_ref, vacc_ref, acc):
        i, j = pl.program_id(0), pl.program_id(1)

        @pl.when((i == 0) & (j == 0))
        def _():
            vacc_ref[...] = jnp.zeros_like(vacc_ref)

        @pl.when(j == 0)
        def _():
            acc[...] = _dot_nt(dq_ref[...], w_ref[...])

        @pl.when((j >= 1) & (j < 3))
        def _():
            acc[...] += _dot_nt(dkv_ref[...], w_ref[...])

        @pl.when((j >= 3) & (j < 5))
        def _():
            acc[...] += _dot_nt(dxr_ref[...], w_ref[...])

        @pl.when(j >= 5)
        def _():
            acc[...] += _dot_nt(d3_ref[...], w_ref[...])

        @pl.when(j == nk - 1)
        def _():
            dx_ref[...] = _pre_norm_bwd(acc[...], x_ref[...], dxo_ref[...], vec_ref, vacc_ref)

    row = lambda i, j: (i, 0)
    return pl.pallas_call(
        body, name=name, grid=(S // tm, nk),
        in_specs=[pl.BlockSpec((None, tm, tk), lambda i, j: (0, i, 0)),
                  pl.BlockSpec((None, tm, tk), lambda i, j: (jnp.clip(j - 1, 0, 1), i, 0)),
                  pl.BlockSpec((tm, tk), lambda i, j: (i, jnp.clip(j - 3, 0, 1))),
                  pl.BlockSpec((None, tm, tk), lambda i, j: (jnp.clip(j - 5, 0, 5) // 2, i, jnp.clip(j - 5, 0, 5) % 2)),
                  pl.BlockSpec((D, tk), lambda i, j: (0, j)),
                  pl.BlockSpec((tm, D), row), pl.BlockSpec((tm, D), row),
                  pl.BlockSpec((8, D), lambda i, j: (0, 0))],
        out_specs=[pl.BlockSpec((tm, D), row), pl.BlockSpec((8, D), lambda i, j: (0, 0))],
        out_shape=[jax.ShapeDtypeStruct((S, D), F32), jax.ShapeDtypeStruct((8, D), F32)],
        scratch_shapes=[pltpu.VMEM((tm, D), F32)],
        compiler_params=_cp("arbitrary", "arbitrary"),
    )(dq, dkv, dxr, d3, w_in, x, dxo, vec)


def _two_heads(v, lane):
    zero = jnp.zeros((), v.dtype)
    return jnp.concatenate([jnp.where(lane < 64, v, zero), jnp.where(lane >= 64, v, zero)], axis=0)


def _attn_probs(qm, ka, bias_h, i, grp):
    s = _dot_nt(qm, ka) + bias_h
    col = lax.broadcasted_iota(jnp.int32, s.shape, 1)
    first_key = jnp.where(i == 0, 512 - 128 * grp, 0)
    s = jnp.where(col >= first_key, s, NEG)
    e = jnp.exp(s - jnp.max(s, axis=-1, keepdims=True))
    return e * (1.0 / jnp.sum(e, axis=-1, keepdims=True))


def attn_fwd(qkv, bias, name):
    S = qkv.shape[0]
    nb = S // TQ

    def body(q_ref, kp_ref, kc_ref, vp_ref, vc_ref, b_ref, o_ref, kw, vw):
        i = pl.program_id(1)
        kw[0:TQ, :] = kp_ref[...]
        kw[TQ:2 * TQ, :] = kc_ref[...]
        vw[0:TQ, :] = vp_ref[...]
        vw[TQ:2 * TQ, :] = vc_ref[...]
        lane = lax.broadcasted_iota(jnp.int32, (1, HP), 1)

        def group(a, carry):
            r0 = pl.multiple_of(a * 128, 128)
            qa = q_ref[pl.ds(r0, 128), :] * jnp.asarray(0.125, BF16)
            ka = kw[pl.ds(r0, WIN), :]
            va = vw[pl.ds(r0, WIN), :]
            p = _attn_probs(_two_heads(qa, lane), ka, b_ref[...], i, a)
            o2 = _dot(p.astype(BF16), va)
            o_ref[pl.ds(r0, 128), :] = jnp.where(lane < 64, o2[0:128], o2[128:256]).astype(BF16)
            return carry

        lax.fori_loop(0, TQ // 128, group, 0, unroll=True)

    prev = lambda h, i: (jnp.maximum(i - 1, 0), 0)
    return pl.pallas_call(
        body, name=name, grid=(4, nb),
        in_specs=[pl.BlockSpec((TQ, HP), lambda h, i: (i, h)),
                  pl.BlockSpec((TQ, HP), lambda h, i: (jnp.maximum(i - 1, 0), 4 + h)),
                  pl.BlockSpec((TQ, HP), lambda h, i: (i, 4 + h)),
                  pl.BlockSpec((TQ, HP), lambda h, i: (jnp.maximum(i - 1, 0), 8 + h)),
                  pl.BlockSpec((TQ, HP), lambda h, i: (i, 8 + h)),
                  pl.BlockSpec((None, 256, WIN), lambda h, i: (h, 0, 0))],
        out_specs=pl.BlockSpec((TQ, HP), lambda h, i: (i, h)),
        out_shape=jax.ShapeDtypeStruct((S, 512), BF16),
        scratch_shapes=[pltpu.VMEM((2 * TQ, HP), BF16), pltpu.VMEM((2 * TQ, HP), BF16)],
        compiler_params=_cp("parallel", "arbitrary"),
    )(qkv, qkv, qkv, qkv, qkv, bias)


def attn_bwd(qkv, do, bias, name):
    S = qkv.shape[0]
    nb = S // TQ

    def body(q_ref, kp_ref, kc_ref, vp_ref, vc_ref, do_ref, b_ref, dqkv_ref, db_ref, dkv_ref, kw, vw, ak, av):
        i = pl.program_id(1)

        @pl.when(i == 0)
        def _():
            db_ref[...] = jnp.zeros_like(db_ref)
            ak[...] = jnp.zeros_like(ak)
            av[...] = jnp.zeros_like(av)

        @pl.when(i > 0)
        def _():
            ak[0:TQ, :] = ak[TQ:2 * TQ, :]
            av[0:TQ, :] = av[TQ:2 * TQ, :]
            ak[TQ:2 * TQ, :] = jnp.zeros((TQ, HP), F32)
            av[TQ:2 * TQ, :] = jnp.zeros((TQ, HP), F32)

        @pl.when(i < nb)
        def _():
            kw[0:TQ, :] = kp_ref[...]
            kw[TQ:2 * TQ, :] = kc_ref[...]
            vw[0:TQ, :] = vp_ref[...]
            vw[TQ:2 * TQ, :] = vc_ref[...]
            lane = lax.broadcasted_iota(jnp.int32, (1, HP), 1)

            def group(a, carry):
                r0 = pl.multiple_of(a * 128, 128)
                q2 = _two_heads(q_ref[pl.ds(r0, 128), :] * jnp.asarray(0.125, BF16), lane)
                do2 = _two_heads(do_ref[pl.ds(r0, 128), :], lane)
                ka = kw[pl.ds(r0, WIN), :]
                va = vw[pl.ds(r0, WIN), :]
                p = _attn_probs(q2, ka, b_ref[...], i, a)
                dp = _dot_nt(do2, va)
                ds = p * (dp - jnp.sum(p * dp, axis=-1, keepdims=True))
                db_ref[...] += ds
                dsb = ds.astype(BF16)
                dq2 = _dot(dsb, ka)
                ak[pl.ds(r0, WIN), :] += _dot_tn(dsb, q2)
                av[pl.ds(r0, WIN), :] += _dot_tn(p.astype(BF16), do2)
                dq = jnp.where(lane < 64, dq2[0:128], dq2[128:256])
                dqkv_ref[0, pl.ds(r0, 128), :] = (dq * 0.125).astype(BF16)
                return carry

            lax.fori_loop(0, TQ // 128, group, 0, unroll=True)

        @pl.when(i > 0)
        def _():
            dkv_ref[0] = ak[0:TQ, :].astype(BF16)
            dkv_ref[1] = av[0:TQ, :].astype(BF16)

    cur = lambda i: jnp.minimum(i, nb - 1)
    prv = lambda i: jnp.clip(i - 1, 0, nb - 1)
    dq, db, dkv = pl.pallas_call(
        body, name=name, grid=(4, nb + 1),
        in_specs=[pl.BlockSpec((TQ, HP), lambda h, i: (cur(i), h)),
                  pl.BlockSpec((TQ, HP), lambda h, i: (prv(i), 4 + h)),
                  pl.BlockSpec((TQ, HP), lambda h, i: (cur(i), 4 + h)),
                  pl.BlockSpec((TQ, HP), lambda h, i: (prv(i), 8 + h)),
                  pl.BlockSpec((TQ, HP), lambda h, i: (cur(i), 8 + h)),
                  pl.BlockSpec((TQ, HP), lambda h, i: (cur(i), h)),
                  pl.BlockSpec((None, 256, WIN), lambda h, i: (h, 0, 0))],
        out_specs=[pl.BlockSpec((1, TQ, HP), lambda h, i: (0, cur(i), h)),
                   pl.BlockSpec((None, 256, WIN), lambda h, i: (h, 0, 0)),
                   pl.BlockSpec((2, TQ, HP), lambda h, i: (0, prv(i), h))],
        out_shape=[jax.ShapeDtypeStruct((1, S, 512), BF16), jax.ShapeDtypeStruct((4, 256, WIN), F32),
                   jax.ShapeDtypeStruct((2, S, 512), BF16)],
        scratch_shapes=[pltpu.VMEM((2 * TQ, HP), BF16), pltpu.VMEM((2 * TQ, HP), BF16),
                        pltpu.VMEM((2 * TQ, HP), F32), pltpu.VMEM((2 * TQ, HP), F32)],
        compiler_params=_cp("parallel", "arbitrary"),
    )(qkv, qkv, qkv, qkv, qkv, do, bias)
    return dq, db, dkv


def bias_grad(db, name):
    def body(db_ref, o_ref):
        r = lax.broadcasted_iota(jnp.int32, (128, 128), 0)
        c = lax.broadcasted_iota(jnp.int32, (128, 128), 1)
        flip = (r + c == 127).astype(BF16)
        lane = lax.broadcasted_iota(jnp.int32, (16, 384), 1)
        src = lax.broadcasted_iota(jnp.int32, (128, 384), 0)
        dst = lax.broadcasted_iota(jnp.int32, (128, 384), 1)

        def split_dot(v, m):
            hi = v.astype(BF16)
            r1 = v - hi.astype(F32)
            mid = r1.astype(BF16)
            lo = (r1 - mid.astype(F32)).astype(BF16)
            return _dot(hi, m) + _dot(mid, m) + _dot(lo, m)

        def diag_sums(w):
            y = pltpu.roll(split_dot(w, flip), 0, 1, stride=1, stride_axis=0)
            return jnp.broadcast_to(_colsum(y), (16, 128))

        w4 = db_ref[0, :, 512:640]
        w3 = db_ref[0, :, 384:512]
        far = jnp.sum(db_ref[0, :, 0:384]) + jnp.sum(jnp.where(r >= c, w3, 0.0))
        lo4 = diag_sums(jnp.where(r >= c, w4, 0.0))
        up4 = diag_sums(jnp.where(r < c, w4, 0.0))
        up3 = diag_sums(jnp.where(r < c, w3, 0.0))
        p_lo4 = (dst == 128 + (src + 1) % 128).astype(BF16)
        p_up4 = ((dst == src + 1) & (src < 127)).astype(BF16)
        p_up3 = ((dst == src + 129) & (src < 127)).astype(BF16)
        out = split_dot(lo4, p_lo4) + split_dot(up4, p_up4) + split_dot(up3, p_up3)
        o_ref[0] = out + jnp.where(lane == 256, far, 0.0)

    return pl.pallas_call(
        body, name=name, grid=(8,),
        in_specs=[pl.BlockSpec((1, 128, WIN), lambda h: (h, 0, 0))],
        out_specs=pl.BlockSpec((1, 16, 384), lambda h: (h, 0, 0)),
        out_shape=jax.ShapeDtypeStruct((8, 16, 384), F32),
        compiler_params=_cp("parallel"),
    )(db)[:, 0, :]


LT = 512
LC = 512


def _lru_gates(xs, pv_ref, wa_ref, wx_ref, tl):
    xc = (pv_ref[4:5, :] + pv_ref[3:4, :] * xs[pl.ds(8, tl), :] + pv_ref[2:3, :] * xs[pl.ds(7, tl), :]
          + pv_ref[1:2, :] * xs[pl.ds(6, tl), :] + pv_ref[0:1, :] * xs[pl.ds(5, tl), :])
    xcb = xc.astype(BF16)
    pa = jnp.concatenate([_dot(xcb[:, 0:256], wa_ref[0]), _dot(xcb[:, 256:512], wa_ref[1])], axis=1)
    px = jnp.concatenate([_dot(xcb[:, 0:256], wx_ref[0]), _dot(xcb[:, 256:512], wx_ref[1])], axis=1)
    r = _sigmoid(pa + pv_ref[5:6, :])
    ig = _sigmoid(px + pv_ref[6:7, :])
    z = -pv_ref[7:8, :]
    sp = jnp.maximum(z, 0.0) + jnp.log1p(jnp.exp(-jnp.abs(z)))
    log_a = (-LRU_C * r) * sp
    a = jnp.exp(log_a)
    s = jnp.tanh(-log_a) * (1.0 + a * a)
    inv_mult = lax.rsqrt(s)
    mult = jnp.where(s > 0.0, s * inv_mult, 0.0)
    return xc, xcb, r, ig, sp, a, mult, inv_mult


def lru_fwd(rest, pvec, wa, wx, name):
    S = rest.shape[0]
    tl = min(LT, S)
    nt = S // tl

    def body(xr_ref, halo_ref, yr_ref, pv_ref, wa_ref, wx_ref, h_ref, hg_ref, xs, a_s, u_s, h_s, carry):
        ti = pl.program_id(1)

        @pl.when(ti == 0)
        def _():
            carry[...] = jnp.zeros_like(carry)

        xs[0:8, :] = jnp.where(ti > 0, halo_ref[8:16, :].astype(F32), 0.0)
        xs[pl.ds(8, tl), :] = xr_ref[...].astype(F32)
        xc, _, _, ig, _, a, mult, _ = _lru_gates(xs, pv_ref, wa_ref, wx_ref, tl)
        a_s[...] = a
        u_s[...] = mult * (ig * xc)
        row = lax.broadcasted_iota(jnp.int32, (8, LC), 0)

        def blk(bi, c):
            o = pl.multiple_of(bi * 8, 8)
            av = a_s[pl.ds(o, 8), :]
            bv = u_s[pl.ds(o, 8), :]
            for d in (1, 2, 4):
                a_sh = pltpu.roll(av, d, 0)
                b_sh = pltpu.roll(bv, d, 0)
                m = row >= d
                bv = jnp.where(m, av * b_sh + bv, bv)
                av = jnp.where(m, av * a_sh, av)
            hv = bv + av * c
            h_s[pl.ds(o, 8), :] = hv
            return hv[7:8, :]

        carry[...] = lax.fori_loop(0, tl // 8, blk, carry[...])
        h = h_s[...]
        h_ref[...] = h
        hg_ref[...] = (h * _gelu(yr_ref[...].astype(F32))).astype(BF16)

    hb = tl // 16
    return pl.pallas_call(
        body, name=name, grid=(2, nt),
        in_specs=[pl.BlockSpec((tl, LC), lambda c, t: (t, c)),
                  pl.BlockSpec((16, LC), lambda c, t: (jnp.maximum(t * hb - 1, 0), c)),
                  pl.BlockSpec((tl, LC), lambda c, t: (t, 2 + c)),
                  pl.BlockSpec((8, LC), lambda c, t: (0, c)),
                  pl.BlockSpec((2, 256, 256), lambda c, t: (c, 0, 0)),
                  pl.BlockSpec((2, 256, 256), lambda c, t: (c, 0, 0))],
        out_specs=[pl.BlockSpec((tl, LC), lambda c, t: (t, c)), pl.BlockSpec((tl, LC), lambda c, t: (t, c))],
        out_shape=[jax.ShapeDtypeStruct((S, D), F32), jax.ShapeDtypeStruct((S, D), BF16)],
        scratch_shapes=[pltpu.VMEM((tl + 8, LC), F32), pltpu.VMEM((tl, LC), F32), pltpu.VMEM((tl, LC), F32),
                        pltpu.VMEM((tl, LC), F32), pltpu.VMEM((1, LC), F32)],
        compiler_params=_cp("parallel", "arbitrary"),
    )(rest, rest, rest, pvec, wa, wx)


def lru_bwd(dh, h, rest, pvec, wa, wx, name):
    S = rest.shape[0]
    tl = min(LT, S)
    nt = S // tl

    def body(dh_ref, h_ref, hhalo_ref, xr_ref, xhalo_ref, pv_ref, wa_ref, wx_ref,
             dxr_ref, vacc_ref, dwa_ref, dwx_ref,
             xs, hs, a_s, ash_s, b_s, lam_s, dxe, anext, lnext, dxnext):
        ti = pl.program_id(1)
        tr = nt - 1 - ti

        @pl.when(ti == 0)
        def _():
            anext[...] = jnp.zeros_like(anext)
            lnext[...] = jnp.zeros_like(lnext)
            dxnext[...] = jnp.zeros_like(dxnext)
            vacc_ref[...] = jnp.zeros_like(vacc_ref)
            dwa_ref[...] = jnp.zeros_like(dwa_ref)
            dwx_ref[...] = jnp.zeros_like(dwx_ref)

        xs[0:8, :] = jnp.where(tr > 0, xhalo_ref[8:16, :].astype(F32), 0.0)
        xs[pl.ds(8, tl), :] = xr_ref[...].astype(F32)
        xc, xcb, r, ig, sp, a, mult, inv_mult = _lru_gates(xs, pv_ref, wa_ref, wx_ref, tl)

        a_s[pl.ds(0, tl), :] = a
        a_s[pl.ds(tl, 8), :] = jnp.broadcast_to(anext[...], (8, LC))
        ash_s[...] = a_s[pl.ds(1, tl), :]
        b_s[...] = dh_ref[...]
        row = lax.broadcasted_iota(jnp.int32, (8, LC), 0)

        def blk(k, c):
            o = pl.multiple_of((tl // 8 - 1 - k) * 8, 8)
            av = ash_s[pl.ds(o, 8), :]
            bv = b_s[pl.ds(o, 8), :]
            for d in (1, 2, 4):
                a_sh = pltpu.roll(av, 8 - d, 0)
                b_sh = pltpu.roll(bv, 8 - d, 0)
                m = row < 8 - d
                bv = jnp.where(m, bv + av * b_sh, bv)
                av = jnp.where(m, av * a_sh, av)
            lv = bv + av * c
            lam_s[pl.ds(o, 8), :] = lv
            return lv[0:1, :]

        lnext[...] = lax.fori_loop(0, tl // 8, blk, lnext[...])
        anext[...] = a[0:1, :]
        lam = lam_s[...]

        hs[0:8, :] = jnp.where(tr > 0, hhalo_ref[...], 0.0)
        hs[pl.ds(8, tl), :] = h_ref[...]
        d_a = lam * hs[pl.ds(7, tl), :]
        d_mult = lam * (ig * xc)
        d_ig = lam * mult * xc
        dxc = lam * mult * ig
        d_log_a = d_a * a - d_mult * (a * a) * inv_mult
        d_r = d_log_a * (-LRU_C * sp)
        vacc_ref[7:8, :] += _colsum(d_log_a * (-LRU_C * r)) * (-_sigmoid(-pv_ref[7:8, :]))
        d_pa = d_r * r * (1.0 - r)
        d_px = d_ig * ig * (1.0 - ig)
        vacc_ref[5:6, :] += _colsum(d_pa)
        vacc_ref[6:7, :] += _colsum(d_px)
        dpa = d_pa.astype(BF16)
        dpx = d_px.astype(BF16)
        back = []
        for g in range(2):
            sl = slice(256 * g, 256 * g + 256)
            dwa_ref[g] += _dot_tn(xcb[:, sl], dpa[:, sl])
            dwx_ref[g] += _dot_tn(xcb[:, sl], dpx[:, sl])
            back.append(_dot_nt(dpa[:, sl], wa_ref[g]) + _dot_nt(dpx[:, sl], wx_ref[g]))
        dxc = dxc + jnp.concatenate(back, axis=1)
        vacc_ref[4:5, :] += _colsum(dxc)
        for k in range(4):
            vacc_ref[k:k + 1, :] += _colsum(dxc * xs[pl.ds(5 + k, tl), :])
        dxe[pl.ds(0, tl), :] = dxc
        dxe[pl.ds(tl, 8), :] = dxnext[...]
        dxr = (pv_ref[3:4, :] * dxc + pv_ref[2:3, :] * dxe[pl.ds(1, tl), :]
               + pv_ref[1:2, :] * dxe[pl.ds(2, tl), :] + pv_ref[0:1, :] * dxe[pl.ds(3, tl), :])
        dxr_ref[...] = dxr.astype(BF16)
        dxnext[...] = dxc[0:8, :]

    hb = tl // 8
    rev = lambda t: nt - 1 - t
    halo = lambda t: jnp.maximum(rev(t) * hb - 1, 0)
    big = lambda: pltpu.VMEM((tl + 8, LC), F32)
    til = lambda: pltpu.VMEM((tl, LC), F32)
    return pl.pallas_call(
        body, name=name, grid=(2, nt),
        in_specs=[pl.BlockSpec((tl, LC), lambda c, t: (rev(t), c)),
                  pl.BlockSpec((tl, LC), lambda c, t: (rev(t), c)),
                  pl.BlockSpec((8, LC), lambda c, t: (halo(t), c)),
                  pl.BlockSpec((tl, LC), lambda c, t: (rev(t), c)),
                  pl.BlockSpec((16, LC), lambda c, t: (jnp.maximum(rev(t) * (tl // 16) - 1, 0), c)),
                  pl.BlockSpec((8, LC), lambda c, t: (0, c)),
                  pl.BlockSpec((2, 256, 256), lambda c, t: (c, 0, 0)),
                  pl.BlockSpec((2, 256, 256), lambda c, t: (c, 0, 0))],
        out_specs=[pl.BlockSpec((tl, LC), lambda c, t: (rev(t), c)),
                   pl.BlockSpec((8, LC), lambda c, t: (0, c)),
                   pl.BlockSpec((2, 256, 256), lambda c, t: (c, 0, 0)),
                   pl.BlockSpec((2, 256, 256), lambda c, t: (c, 0, 0))],
        out_shape=[jax.ShapeDtypeStruct((S, D), BF16), jax.ShapeDtypeStruct((8, D), F32),
                   jax.ShapeDtypeStruct((4, 256, 256), F32), jax.ShapeDtypeStruct((4, 256, 256), F32)],
        scratch_shapes=[big(), big(), big(), til(), til(), til(), big(),
                        pltpu.VMEM((1, LC), F32), pltpu.VMEM((1, LC), F32), pltpu.VMEM((8, LC), F32)],
        compiler_params=_cp("parallel", "arbitrary"),
    )(dh, h, h, rest, rest, pvec, wa, wx)


def mix_out_fwd(x, ao, hg, rest, vec, w_att_o, w_rec_o, w_out, name, tm=512):
    S = x.shape[0]
    tm = min(tm, S)

    def body(x_ref, ao_ref, hg_ref, ga_ref, gr_ref, vec_ref, wa_ref, wr_ref, wo_ref,
             xo_ref, att_ref, rec_ref, mg_ref, f_ref):
        att = _dot(ao_ref[...], wa_ref[...])
        rec = _dot(hg_ref[...], wr_ref[...])
        att_ref[...] = att.astype(BF16)
        rec_ref[...] = rec.astype(BF16)
        mg = (_sigmoid(ga_ref[...].astype(F32)) * att + _sigmoid(gr_ref[...].astype(F32)) * rec).astype(BF16)
        mg_ref[...] = mg
        f = _dot(mg, wo_ref[...])
        f_ref[...] = f.astype(BF16)
        y = f * lax.rsqrt(_mean(f * f) + EPS) * vec_ref[1:2, :]
        xo_ref[...] = x_ref[...] + (1.0 * vec_ref[4:5, :]) * y

    row = lambda i: (i, 0)
    full = lambda r: pl.BlockSpec((r, D), lambda i: (0, 0))
    return pl.pallas_call(
        body, name=name, grid=(S // tm,),
        in_specs=[pl.BlockSpec((tm, D), row), pl.BlockSpec((tm, 512), row), pl.BlockSpec((tm, D), row),
                  pl.BlockSpec((tm, D), lambda i: (i, 2)), pl.BlockSpec((tm, D), lambda i: (i, 3)),
                  full(8), full(512), full(D), full(D)],
        out_specs=[pl.BlockSpec((tm, D), row)] * 5,
        out_shape=[jax.ShapeDtypeStruct((S, D), F32)] + [jax.ShapeDtypeStruct((S, D), BF16)] * 4,
        compiler_params=_cp("parallel"),
    )(x, ao, hg, rest, rest, vec, w_att_o, w_rec_o, w_out)


def mix_out_bwd(dxo, f, att, rec, rest, h, vec, w_att_o, w_rec_o, w_out, name, tm=512):
    S = dxo.shape[0]
    tm = min(tm, S)

    def body(dxo_ref, f_ref, att_ref, rec_ref, yr_ref, ga_ref, gr_ref, h_ref, vec_ref, wa_ref, wr_ref, wo_ref,
             df_ref, da_ref, dr_ref, dao_ref, dh_ref, d3_ref, vacc_ref):
        @pl.when(pl.program_id(0) == 0)
        def _():
            vacc_ref[...] = jnp.zeros_like(vacc_ref)

        df = _post_norm_bwd(dxo_ref[...], f_ref[...].astype(F32), 1.0, vec_ref, vacc_ref).astype(BF16)
        df_ref[...] = df
        dm = _dot_nt(df, wo_ref[...])
        sa = _sigmoid(ga_ref[...].astype(F32))
        sr = _sigmoid(gr_ref[...].astype(F32))
        d_att = (dm * sa).astype(BF16)
        d_rec = (dm * sr).astype(BF16)
        da_ref[...] = d_att
        dr_ref[...] = d_rec
        d3_ref[1] = (dm * att_ref[...].astype(F32) * (sa * (1.0 - sa))).astype(BF16)
        d3_ref[2] = (dm * rec_ref[...].astype(F32) * (sr * (1.0 - sr))).astype(BF16)
        dao_ref[...] = _dot_nt(d_att, wa_ref[...]).astype(BF16)
        d_hg = _dot_nt(d_rec, wr_ref[...])
        yr = yr_ref[...].astype(F32)
        t = jnp.tanh(_GK * (yr + 0.044715 * yr * yr * yr))
        dh_ref[...] = d_hg * (0.5 * yr * (1.0 + t))
        gelu_grad = 0.5 * (1.0 + t) + 0.5 * yr * (1.0 - t * t) * _GK * (1.0 + 3.0 * 0.044715 * yr * yr)
        d3_ref[0] = (d_hg * h_ref[...] * gelu_grad).astype(BF16)

    row = lambda i: (i, 0)
    full = lambda r: pl.BlockSpec((r, D), lambda i: (0, 0))
    return pl.pallas_call(
        body, name=name, grid=(S // tm,),
        in_specs=[pl.BlockSpec((tm, D), row)] * 4
        + [pl.BlockSpec((tm, D), lambda i: (i, 1)), pl.BlockSpec((tm, D), lambda i: (i, 2)),
           pl.BlockSpec((tm, D), lambda i: (i, 3)), pl.BlockSpec((tm, D), row),
           full(8), full(512), full(D), full(D)],
        out_specs=[pl.BlockSpec((tm, D), row)] * 3
        + [pl.BlockSpec((tm, 512), row), pl.BlockSpec((tm, D), row),
           pl.BlockSpec((3, tm, D), lambda i: (0, i, 0)), pl.BlockSpec((8, D), lambda i: (0, 0))],
        out_shape=[jax.ShapeDtypeStruct((S, D), BF16)] * 3
        + [jax.ShapeDtypeStruct((S, 512), BF16), jax.ShapeDtypeStruct((S, D), F32),
           jax.ShapeDtypeStruct((3, S, D), BF16), jax.ShapeDtypeStruct((8, D), F32)],
        compiler_params=_cp("arbitrary"),
    )(dxo, f, att, rec, rest, rest, rest, h, vec, w_att_o, w_rec_o, w_out)


def dw_in(h, dq, dkv, dxr, d3, name, tk=1024, tn=512):
    S = h.shape[0]
    tk = min(tk, S)
    nk = S // tk

    def body(h_ref, dq_ref, dkv_ref, dxr_ref, d3_ref, o_ref, acc):
        j, k = pl.program_id(0), pl.program_id(1)

        @pl.when(k == 0)
        def _():
            acc[...] = jnp.zeros_like(acc)

        @pl.when(j == 0)
        def _():
            acc[...] += _dot_tn(h_ref[...], dq_ref[...])

        @pl.when((j >= 1) & (j < 3))
        def _():
            acc[...] += _dot_tn(h_ref[...], dkv_ref[...])

        @pl.when((j >= 3) & (j < 5))
        def _():
            acc[...] += _dot_tn(h_ref[...], dxr_ref[...])

        @pl.when(j >= 5)
        def _():
            acc[...] += _dot_tn(h_ref[...], d3_ref[...])

        @pl.when(k == nk - 1)
        def _():
            o_ref[...] = acc[...].astype(BF16)

    use = lambda j, k, lo, hi: jnp.where((j >= lo) & (j < hi), k, 0)
    g3 = lambda j: jnp.clip(j - 5, 0, 5)
    return pl.pallas_call(
        body, name=name, grid=(PW // tn, nk),
        in_specs=[pl.BlockSpec((tk, D), lambda j, k: (k, 0)),
                  pl.BlockSpec((None, tk, tn), lambda j, k: (0, use(j, k, 0, 1), 0)),
                  pl.BlockSpec((None, tk, tn), lambda j, k: (jnp.clip(j - 1, 0, 1), use(j, k, 1, 3), 0)),
                  pl.BlockSpec((tk, tn), lambda j, k: (use(j, k, 3, 5), jnp.clip(j - 3, 0, 1))),
                  pl.BlockSpec((None, tk, tn), lambda j, k: (g3(j) // 2, use(j, k, 5, 11), g3(j) % 2))],
        out_specs=pl.BlockSpec((D, tn), lambda j, k: (0, j)),
        out_shape=jax.ShapeDtypeStruct((D, PW), BF16),
        scratch_shapes=[pltpu.VMEM((D, tn), F32)],
        compiler_params=_cp("parallel", "arbitrary"),
    )(h, dq, dkv, dxr, d3)


def ada_fwd(c_all, w_ada, b_ada, name, tn=768):
    n = w_ada.shape[1]

    def body(c_ref, w_ref, b_ref, o_ref):
        cv = c_ref[...]
        ca = (cv * _sigmoid(cv)).astype(BF16)
        o_ref[...] = _dot(ca, w_ref[...].astype(BF16)) + b_ref[...]

    return pl.pallas_call(
        body, name=name, grid=(n // tn,),
        in_specs=[pl.BlockSpec((8, D), lambda j: (0, 0)), pl.BlockSpec((D, tn), lambda j: (0, j)),
                  pl.BlockSpec((1, tn), lambda j: (0, j))],
        out_specs=pl.BlockSpec((8, tn), lambda j: (0, j)),
        out_shape=jax.ShapeDtypeStruct((8, n), F32),
        compiler_params=_cp("parallel"),
    )(c_all, w_ada, b_ada)


def ada_bwd(c_all_t, dmod, name, tn=768):
    n = dmod.shape[1]

    def body(c_ref, d_ref, o_ref):
        cv = c_ref[...]
        ca = (cv * _sigmoid(cv)).astype(BF16)
        o_ref[...] = _dot(ca, d_ref[...].astype(BF16))

    return pl.pallas_call(
        body, name=name, grid=(n // tn,),
        in_specs=[pl.BlockSpec((D, 128), lambda j: (0, 0)), pl.BlockSpec((128, tn), lambda j: (0, j))],
        out_specs=pl.BlockSpec((D, tn), lambda j: (0, j)),
        out_shape=jax.ShapeDtypeStruct((D, n), F32),
        compiler_params=_cp("parallel"),
    )(c_all_t, dmod)


def _row_tile(rows, cols, itemsize=4, budget=1536 * 1024):
    best = None
    for t in range(8, rows + 1, 8):
        if rows % t == 0 and t * cols * itemsize <= budget:
            best = t
    return rows if best is None else best


def sum_lead(parts, name, out_dtype=F32):
    n, R, C = parts.shape
    tr = _row_tile(R, C * n)

    def body(p_ref, o_ref):
        acc = p_ref[0].astype(F32)
        for k in range(1, n):
            acc = acc + p_ref[k].astype(F32)
        o_ref[...] = acc.astype(out_dtype)

    return pl.pallas_call(
        body, name=name, grid=(R // tr,),
        in_specs=[pl.BlockSpec((n, tr, C), lambda i: (0, i, 0))],
        out_specs=pl.BlockSpec((tr, C), lambda i: (i, 0)),
        out_shape=jax.ShapeDtypeStruct((R, C), out_dtype),
        compiler_params=_cp("parallel"),
    )(parts)


def adamw(w, g, m, v, name, emit_g=False):
    R, C = w.shape
    tr = _row_tile(R, C * 8, budget=8 * 1024 * 1024)

    def body(w_ref, g_ref, m_ref, v_ref, d_ref, mo_ref, vo_ref, *go_ref):
        gv = g_ref[...]
        if emit_g:
            go_ref[0][...] = gv
        mn = ADAM_B1 * m_ref[...] + (1.0 - ADAM_B1) * gv
        vn = ADAM_B2 * v_ref[...] + (1.0 - ADAM_B2) * (gv * gv)
        m_hat = mn / (1.0 - ADAM_B1 ** ADAM_STEP)
        v_hat = vn / (1.0 - ADAM_B2 ** ADAM_STEP)
        d_ref[...] = -ADAM_LR * (m_hat / (jnp.sqrt(v_hat) + ADAM_EPS) + ADAM_WD * w_ref[...])
        mo_ref[...] = mn
        vo_ref[...] = vn

    spec = pl.BlockSpec((tr, C), lambda i: (i, 0))
    return pl.pallas_call(
        body, name=name, grid=(R // tr,),
        in_specs=[spec] * 4, out_specs=[spec] * (4 if emit_g else 3),
        out_shape=[jax.ShapeDtypeStruct((R, C), F32)] * (4 if emit_g else 3),
        compiler_params=_cp("parallel"),
    )(w, g, m, v)


def _mesh_pos():
    return lax.axis_index("x"), lax.axis_index("y"), lax.axis_index("c")


def _other_chips(mx, my):
    return [(1 - mx, my), (mx, 1 - my), (1 - mx, 1 - my)]


def ag_small(x, name):
    R = x.shape[0]

    def body(x_ref, out_ref, send_sems, recv_sems, local_sem):
        mx, my, mc = _mesh_pos()
        me, sibling = (mx, my, mc), (mx, my, 1 - mc)
        chips = _other_chips(mx, my)

        def slot(px, py, pc):
            return out_ref.at[4 * px + 2 * py + pc]

        def copy(k, block, to, src=None):
            return pltpu.make_async_remote_copy(
                src_ref=slot(*block) if src is None else src, dst_ref=slot(*block),
                send_sem=send_sems.at[k], recv_sem=recv_sems.at[k], device_id=to, device_id_type=MESH)

        mine = pltpu.make_async_copy(x_ref, slot(*me), local_sem)
        mine.start()
        first = [copy(0, me, sibling, src=x_ref)]
        first += [copy(1 + j, me, (*chip, mc), src=x_ref) for j, chip in enumerate(chips)]
        for cp in first:
            cp.start()
        passed = [copy(4 + j, (*chip, mc), sibling) for j, chip in enumerate(chips)]
        for j, chip in enumerate(chips):
            copy(1 + j, (*chip, mc), me).wait_recv()
            passed[j].start()
        copy(0, sibling, me).wait_recv()
        for j, chip in enumerate(chips):
            copy(4 + j, (*chip, 1 - mc), me).wait_recv()
        for cp in first + passed:
            cp.wait_send()
        mine.wait()

    return pl.pallas_call(
        body, name=name,
        out_shape=jax.ShapeDtypeStruct((N_DEV, R, 128), F32),
        in_specs=[pl.BlockSpec(memory_space=pltpu.VMEM)],
        out_specs=pl.BlockSpec(memory_space=pltpu.VMEM),
        scratch_shapes=[pltpu.SemaphoreType.DMA((7,)), pltpu.SemaphoreType.DMA((7,)), pltpu.SemaphoreType.DMA],
        compiler_params=pltpu.CompilerParams(vmem_limit_bytes=VMEM_LIMIT),
    )(x)


BIG = (("ffn1_w_gu", "col", D, PW), ("ffn1_w_down", "row", FF, D), ("w_in", "col", D, PW),
       ("w_att_o", "col", 512, D), ("w_rec_o", "row", D, D), ("w_out", "row", D, D),
       ("ffn2_w_gu", "col", D, PW), ("ffn2_w_down", "row", FF, D))
NBIG = len(BIG)


def _shard_shape(kind, R, C):
    return (R, C // 4) if kind == "col" else (R // 4, C)


def _region(ref, kind, R, C, q, half, t, tr):
    sr, sc = _shard_shape(kind, R, C)
    if kind == "col":
        return ref.at[pl.ds(pl.multiple_of(half * (R // 2) + t * tr, 16), tr), pl.ds(q * sc, sc)]
    return ref.at[pl.ds(pl.multiple_of(q * sr + t * tr, 16), tr), pl.ds(half * (C // 2), C // 2)]


def ag_local(w, kind, R, C, p_arr, name, after=()):
    sr, sc = _shard_shape(kind, R, C)
    tr = _row_tile(sr, sc, budget=2 * 1024 * 1024)
    nt = sr // tr
    after = list(after)

    def body(p_ref, w_ref, *rest):
        rest[-1][...] = w_ref[...].astype(BF16)

    if kind == "col":
        o_spec = pl.BlockSpec((tr, sc), lambda i, p: (i, p[0]))
    else:
        o_spec = pl.BlockSpec((tr, sc), lambda i, p: (p[0] * nt + i, 0))
    return pl.pallas_call(
        body, name=name,
        grid_spec=pltpu.PrefetchScalarGridSpec(
            num_scalar_prefetch=1, grid=(nt,),
            in_specs=[pl.BlockSpec((tr, sc), lambda i, p: (i, 0))] + [ANY] * len(after), out_specs=o_spec),
        out_shape=jax.ShapeDtypeStruct((R, C), BF16),
        compiler_params=_cp("parallel"),
    )(p_arr, w, *after)


HBM_SPEC = pl.BlockSpec(memory_space=pltpu.HBM)
SEM_SPEC = pl.BlockSpec(memory_space=pltpu.SEMAPHORE)


def _ag_sems(geoms):
    return sum(6 if both else 3 for (_, _, _, both) in geoms)


def _ag_copies(fulls, geoms, ssem, rsem, mx, my, mc, q, h):
    chips = _other_chips(mx, my)
    out, base = [], 0
    for w, (kind, R, C, both) in enumerate(geoms):
        sr, sc = _shard_shape(kind, R, C)
        hr = sr // 2 if kind == "col" else sr
        reg = _region(fulls[w], kind, R, C, q, h, 0, hr)
        out.append([pltpu.make_async_remote_copy(
            src_ref=reg, dst_ref=reg, send_sem=ssem.at[base + 3 * t + k], recv_sem=rsem.at[base + 3 * t + k],
            device_id=(*chips[k], mc if t == 0 else 1 - mc), device_id_type=MESH)
            for t in range(2 if both else 1) for k in range(3)])
        base += 6 if both else 3
    return out


def ag_start(fulls, geoms, after, name):
    n = len(fulls)
    after = list(after)
    m = len(after)

    def body(*refs):
        ssem, rsem = refs[n + m:n + m + 2]
        outs, token = refs[n + m + 2:2 * n + m + 2], refs[2 * n + m + 2]
        mx, my, mc = _mesh_pos()
        p = 2 * mx + my
        col = [w for w, g in enumerate(geoms) if g[0] == "col"]
        row = [w for w, g in enumerate(geoms) if g[0] == "row"]
        for q in range(4):
            @pl.when(p == q)
            def _(q=q):
                cps = _ag_copies(outs, geoms, ssem, rsem, mx, my, mc, q, mc)
                for w in col:
                    for cp in cps[w]:
                        cp.start()
        for h in range(2):
            @pl.when(mc == h)
            def _(h=h):
                cps = _ag_copies(outs, geoms, ssem, rsem, mx, my, mc, p, h)
                for w in row:
                    for cp in cps[w]:
                        cp.start()
        token[...] = jnp.zeros_like(token)

    res = pl.pallas_call(
        body, name=name,
        out_shape=[pltpu.SemaphoreType.DMA((_ag_sems(geoms),)), pltpu.SemaphoreType.DMA((_ag_sems(geoms),))]
        + [pltpu.HBM(a.shape, a.dtype) for a in fulls] + [jax.ShapeDtypeStruct((8, 128), F32)],
        in_specs=[HBM_SPEC] * n + [ANY] * m,
        out_specs=[SEM_SPEC, SEM_SPEC] + [HBM_SPEC] * n + [pl.BlockSpec(memory_space=pltpu.VMEM)],
        input_output_aliases={w: 2 + w for w in range(n)},
        compiler_params=pltpu.CompilerParams(has_side_effects=pltpu.SideEffectType.DATAFLOW_SIDE_EFFECTING),
    )(*[pltpu.with_memory_space_constraint(a, pltpu.HBM) for a in fulls], *after)
    return res[0], res[1], list(res[2:2 + n]), res[2 + n]


def ag_wait(fulls, geoms, ssem, rsem, after, name):
    n = len(fulls)
    after = list(after) if isinstance(after, (list, tuple)) else [after]

    def body(*refs):
        ins, ssem_ref, rsem_ref = refs[:n], refs[n], refs[n + 1]
        mx, my, mc = _mesh_pos()
        for cps in _ag_copies(ins, geoms, ssem_ref, rsem_ref, mx, my, mc, 0, 0):
            for cp in cps:
                cp.wait_send()
                cp.wait_recv()

    return list(pl.pallas_call(
        body, name=name,
        out_shape=[pltpu.HBM(a.shape, a.dtype) for a in fulls],
        in_specs=[HBM_SPEC] * n + [SEM_SPEC, SEM_SPEC] + [ANY] * len(after),
        out_specs=[HBM_SPEC] * n,
        input_output_aliases={w: w for w in range(n)},
        compiler_params=pltpu.CompilerParams(has_side_effects=pltpu.SideEffectType.DATAFLOW_SIDE_EFFECTING),
    )(*fulls, ssem, rsem, *after))


def ag_forward(full, kind, R, C, name):
    sr, sc = _shard_shape(kind, R, C)
    hr, hc = (sr // 2, sc) if kind == "col" else (sr, sc // 2)
    tr = _row_tile(hr, hc, itemsize=2, budget=512 * 1024)
    nt = hr // tr

    total = 3 * nt

    def body(src_ref, full_ref, stage, lsem, ssem, rsem):
        step = pl.program_id(0) * nt + pl.program_id(1)
        par = step % 2
        mx, my, mc = _mesh_pos()

        def load(s, q, h, t):
            return pltpu.make_async_copy(_region(src_ref, kind, R, C, q, h, t, tr), stage.at[s], lsem.at[s])

        def push(s, q, h, t):
            return pltpu.make_async_remote_copy(src_ref=stage.at[s], dst_ref=_region(full_ref, kind, R, C, q, h, t, tr),
                                                send_sem=ssem.at[s], recv_sem=rsem, device_id=(mx, my, 1 - mc),
                                                device_id_type=MESH)

        def for_tile(stp, fn):
            q_k = _partner_chip(stp // nt, 2 * mx + my)
            if kind == "col":
                for q in range(4):
                    @pl.when(q_k == q)
                    def _(q=q):
                        fn(q, mc, stp % nt)
            else:
                for h in range(2):
                    @pl.when(mc == h)
                    def _(h=h):
                        fn(q_k, h, stp % nt)

        @pl.when(step == 0)
        def _():
            for_tile(step, lambda q, h, t: load(0, q, h, t).start())

        load(par, 0, 0, 0).wait()
        for_tile(step, lambda q, h, t: push(par, q, h, t).start())

        @pl.when(step + 1 < total)
        def _():
            @pl.when(step >= 1)
            def _():
                push(1 - par, 0, 0, 0).wait_send()
            for_tile(step + 1, lambda q, h, t: load(1 - par, q, h, t).start())

        @pl.when(step == total - 1)
        def _():
            push(par, 0, 0, 0).wait_send()
            push(1 - par, 0, 0, 0).wait_send()
            three = full_ref.at[pl.ds(0, hr), pl.ds(0, 3 * hc)] if kind == "col" else full_ref.at[pl.ds(0, 3 * hr), pl.ds(0, hc)]
            pltpu.make_async_remote_copy(src_ref=three, dst_ref=three, send_sem=ssem.at[0], recv_sem=rsem,
                                         device_id=(mx, my, 1 - mc), device_id_type=MESH).wait_recv()

    return pl.pallas_call(
        body, name=name, grid=(3, nt),
        in_specs=[ANY], out_specs=ANY,
        out_shape=jax.ShapeDtypeStruct((R, C), BF16),
        scratch_shapes=[pltpu.VMEM((2, tr, hc), BF16), pltpu.SemaphoreType.DMA((2,)), pltpu.SemaphoreType.DMA((2,)),
                        pltpu.SemaphoreType.DMA],
        input_output_aliases={0: 0},
        compiler_params=_cp("arbitrary", "arbitrary"),
    )(full)


def _half_shape(kind, R, C):
    return (R // 2, C) if kind == "col" else (R, C // 2)


def _piece_shape(kind, R, C):
    return (R // 2, C // 4) if kind == "col" else (R // 4, C // 2)


def pair_push(g, kind, c_arr, name):
    R, C = g.shape
    hr, hc = _half_shape(kind, R, C)
    tr = _row_tile(hr, hc, itemsize=2, budget=1024 * 1024)
    nt = hr // tr

    def body(c_ref, g_ref, out_ref, stage, ssem, rsem):
        i = pl.program_id(0)
        slot = i % 2
        mx, my, mc = _mesh_pos()

        def push(s, t):
            return pltpu.make_async_remote_copy(
                src_ref=stage.at[s], dst_ref=out_ref.at[pl.ds(pl.multiple_of(t * tr, 16), tr)],
                send_sem=ssem.at[s], recv_sem=rsem, device_id=(mx, my, 1 - mc), device_id_type=MESH)

        @pl.when(i >= 2)
        def _():
            push(slot, 0).wait_send()

        stage[slot] = g_ref[...]
        push(slot, i).start()

        @pl.when(i == nt - 1)
        def _():
            push(slot, 0).wait_send()
            if nt >= 2:
                push(1 - slot, 0).wait_send()
            pltpu.make_async_remote_copy(src_ref=out_ref, dst_ref=out_ref, send_sem=ssem.at[0], recv_sem=rsem,
                                         device_id=(mx, my, 1 - mc), device_id_type=MESH).wait_recv()

    if kind == "col":
        g_spec = pl.BlockSpec((tr, hc), lambda i, c: ((1 - c[0]) * nt + i, 0))
    else:
        g_spec = pl.BlockSpec((tr, hc), lambda i, c: (i, 1 - c[0]))
    return pl.pallas_call(
        body, name=name,
        grid_spec=pltpu.PrefetchScalarGridSpec(
            num_scalar_prefetch=1, grid=(nt,), in_specs=[g_spec], out_specs=ANY,
            scratch_shapes=[pltpu.VMEM((2, tr, hc), BF16), pltpu.SemaphoreType.DMA((2,)), pltpu.SemaphoreType.DMA]),
        out_shape=jax.ShapeDtypeStruct((hr, hc), BF16),
        compiler_params=_cp("arbitrary"),
    )(c_arr, g)


def _partner_chip(k, p):
    return p ^ jnp.where(k == 0, 2, jnp.where(k == 1, 1, jnp.where(k == 2, 3, 0)))


def pair_add(g, got, kind, cp_arr, name):
    R, C = g.shape
    pr, pc = _piece_shape(kind, R, C)
    tr = _row_tile(pr, pc, itemsize=2, budget=1024 * 1024)
    nt = pr // tr

    def body(cp_ref, g_ref, got_ref, ps_ref, rb_ref):
        tile = (g_ref[...].astype(F32) + got_ref[...].astype(F32)).astype(BF16)
        ps_ref[...] = tile

        @pl.when(pl.program_id(1) == cp_ref[1])
        def _():
            rb_ref[...] = tile

    if kind == "col":
        g_spec = pl.BlockSpec((tr, pc), lambda i, q, cp: (cp[0] * nt + i, q))
        got_spec = pl.BlockSpec((tr, pc), lambda i, q, cp: (i, q))
    else:
        g_spec = pl.BlockSpec((tr, pc), lambda i, q, cp: (q * nt + i, cp[0]))
        got_spec = pl.BlockSpec((tr, pc), lambda i, q, cp: (q * nt + i, 0))
    return pl.pallas_call(
        body, name=name,
        grid_spec=pltpu.PrefetchScalarGridSpec(
            num_scalar_prefetch=1, grid=(nt, 4), in_specs=[g_spec, got_spec],
            out_specs=[pl.BlockSpec((None, tr, pc), lambda i, q, cp: (q, i, 0)),
                       pl.BlockSpec((None, tr, pc), lambda i, q, cp: (cp[1], i, 0))]),
        out_shape=[jax.ShapeDtypeStruct((4, pr, pc), BF16)] * 2,
        compiler_params=_cp("arbitrary", "arbitrary"),
    )(cp_arr, g, got)


def _rs_copies(ps, rb, ssem, rsem, mx, my, mc):
    p = 2 * mx + my
    out = []
    for w in range(len(ps)):
        for k, chip in enumerate(_other_chips(mx, my)):
            out.append(pltpu.make_async_remote_copy(
                src_ref=ps[w].at[2 * chip[0] + chip[1]], dst_ref=rb[w].at[p], send_sem=ssem.at[3 * w + k],
                recv_sem=rsem.at[3 * w + k], device_id=(*chip, mc), device_id_type=MESH))
    return out


def rs_start(ps, rb, after, name):
    n = len(ps)
    after = list(after)
    m = len(after)

    def body(*refs):
        ssem, rsem = refs[2 * n + m:2 * n + m + 2]
        ps_o = refs[2 * n + m + 2:3 * n + m + 2]
        rb_o = refs[3 * n + m + 2:4 * n + m + 2]
        token = refs[4 * n + m + 2]
        for cp in _rs_copies(ps_o, rb_o, ssem, rsem, *_mesh_pos()):
            cp.start()
        token[...] = jnp.zeros_like(token)

    both = list(ps) + list(rb)
    res = pl.pallas_call(
        body, name=name,
        out_shape=[pltpu.SemaphoreType.DMA((3 * n,)), pltpu.SemaphoreType.DMA((3 * n,))]
        + [pltpu.HBM(a.shape, a.dtype) for a in both] + [jax.ShapeDtypeStruct((8, 128), F32)],
        in_specs=[HBM_SPEC] * (2 * n) + [ANY] * m,
        out_specs=[SEM_SPEC, SEM_SPEC] + [HBM_SPEC] * (2 * n) + [pl.BlockSpec(memory_space=pltpu.VMEM)],
        input_output_aliases={w: 2 + w for w in range(2 * n)},
        compiler_params=pltpu.CompilerParams(has_side_effects=pltpu.SideEffectType.DATAFLOW_SIDE_EFFECTING),
    )(*[pltpu.with_memory_space_constraint(a, pltpu.HBM) for a in both], *after)
    return res[0], res[1], list(res[2:2 + n]), list(res[2 + n:2 + 2 * n]), res[2 + 2 * n]


def rs_wait(ps, rb, ssem, rsem, after, name):
    n = len(ps)
    after = list(after)
    m = len(after)

    def body(*refs):
        ps_i, rb_i = refs[:n], refs[n:2 * n]
        ssem_ref, rsem_ref = refs[2 * n], refs[2 * n + 1]
        for cp in _rs_copies(ps_i, rb_i, ssem_ref, rsem_ref, *_mesh_pos()):
            cp.wait_send()
            cp.wait_recv()

    both = list(ps) + list(rb)
    res = pl.pallas_call(
        body, name=name,
        out_shape=[pltpu.HBM(a.shape, a.dtype) for a in both],
        in_specs=[HBM_SPEC] * (2 * n) + [SEM_SPEC, SEM_SPEC] + [ANY] * m,
        out_specs=[HBM_SPEC] * (2 * n),
        input_output_aliases={w: w for w in range(2 * n)},
        compiler_params=pltpu.CompilerParams(has_side_effects=pltpu.SideEffectType.DATAFLOW_SIDE_EFFECTING),
    )(*both, ssem, rsem, *after)
    return list(res[n:])


def sum_share(parts, kind, R, C, name):
    _, pr, pc = parts.shape
    sr, sc = _shard_shape(kind, R, C)
    tr = _row_tile(pr, pc * 4, budget=4 * 1024 * 1024)
    nt = pr // tr

    def body(p_ref, fin_ref, stage, lsem, ssem, rsem):
        i = pl.program_id(0)
        slot = i % 2
        mx, my, mc = _mesh_pos()

        def region(h, t):
            r0 = pl.multiple_of(t * tr, 8)
            if kind == "col":
                return fin_ref.at[pl.ds(pl.multiple_of(h * pr + r0, 8), tr)]
            return fin_ref.at[pl.ds(r0, tr), pl.ds(h * pc, pc)]

        def copies(s, h, t):
            return (pltpu.make_async_copy(stage.at[s], region(h, t), lsem.at[s]),
                    pltpu.make_async_remote_copy(src_ref=stage.at[s], dst_ref=region(h, t), send_sem=ssem.at[s],
                                                 recv_sem=rsem, device_id=(mx, my, 1 - mc), device_id_type=MESH))

        def wait_sent(s):
            loc, rem = copies(s, 0, 0)
            loc.wait()
            rem.wait_send()

        @pl.when(i >= 2)
        def _():
            wait_sent(slot)

        acc = p_ref[0].astype(F32)
        for k in range(1, 4):
            acc = acc + p_ref[k].astype(F32)
        stage[slot] = acc
        if kind == "col":
            for cp in copies(slot, mc, i):
                cp.start()
        else:
            for h in range(2):
                @pl.when(mc == h)
                def _(h=h):
                    for cp in copies(slot, h, i):
                        cp.start()

        @pl.when(i == nt - 1)
        def _():
            wait_sent(slot)
            if nt >= 2:
                wait_sent(1 - slot)
            half = fin_ref.at[pl.ds(0, pr), pl.ds(0, pc)]
            pltpu.make_async_remote_copy(src_ref=half, dst_ref=half, send_sem=ssem.at[0], recv_sem=rsem,
                                         device_id=(mx, my, 1 - mc), device_id_type=MESH).wait_recv()

    return pl.pallas_call(
        body, name=name, grid=(nt,),
        in_specs=[pl.BlockSpec((4, tr, pc), lambda i: (0, i, 0))],
        out_specs=ANY,
        out_shape=jax.ShapeDtypeStruct((sr, sc), F32),
        scratch_shapes=[pltpu.VMEM((2, tr, pc), F32), pltpu.SemaphoreType.DMA((2,)), pltpu.SemaphoreType.DMA((2,)),
                        pltpu.SemaphoreType.DMA],
        compiler_params=_cp("arbitrary"),
    )(parts)


def _pack(parts, rows):
    flat = []
    for a in parts:
        a = jnp.ravel(a).astype(F32)
        flat.append(jnp.pad(a, (0, (-a.shape[0]) % 128)))
    v = jnp.concatenate(flat)
    return jnp.pad(v, (0, rows * 128 - v.shape[0])).reshape(rows, 128)


def _unpack(block, shapes):
    lead = block.shape[:-2]
    v = block.reshape(lead + (-1,))
    out, off = [], 0
    for shp in shapes:
        n = int(np.prod(shp))
        out.append(v[..., off:off + n].reshape(lead + tuple(shp)))
        off += n + (-n) % 128
    return out


def _block_diag4(w):
    w4 = w.reshape(4, 4, 64, 64)
    eye = jnp.eye(4, dtype=w.dtype)
    return (w4[:, :, :, None, :] * eye[None, :, None, :, None]).reshape(4, 256, 256)


def _diag_blocks(bd):
    b5 = bd.reshape(4, 4, 64, 4, 64)
    return jnp.stack([b5[:, i, :, i, :] for i in range(4)], axis=1).reshape(16, 64, 64)


def _bias_window(rel_bias):
    m = (np.arange(768) + 127) % 768 - 127
    w = rel_bias[:, np.clip(512 - m, -128, 128) + 128]
    win = jnp.tile(w, (1, 128))[:, :128 * 767].reshape(8, 128, 767)[:, :, :WIN]
    qh = np.arange(128)[:, None] // CHUNK
    kc = np.arange(WIN)[None, :] // CHUNK
    valid = (kc >= qh) & (kc <= qh + 8)
    return jnp.where(jnp.asarray(valid)[None], win, NEG)


SMALL = ("b_ada", "norm_pre", "norm_post", "rel_bias", "conv_w", "conv_b", "lru_wa", "lru_ba", "lru_wx",
         "lru_bx", "lru_lambda")
WEIGHTS = ("w_ada", "b_ada", "norm_pre", "norm_post", "ffn1_w_gu", "ffn1_w_down", "w_in", "rel_bias", "conv_w",
           "conv_b", "lru_wa", "lru_ba", "lru_wx", "lru_bx", "lru_lambda", "w_att_o", "w_rec_o", "w_out",
           "ffn2_w_gu", "ffn2_w_down")


def kernel(x, c, w_ada, b_ada, norm_pre, norm_post, ffn1_w_gu, ffn1_w_down, w_in, rel_bias, conv_w, conv_b, lru_wa, lru_ba, lru_wx, lru_bx, lru_lambda, w_att_o, w_rec_o, w_out, ffn2_w_gu, ffn2_w_down, loss_target, m_w_ada, m_b_ada, m_norm_pre, m_norm_post, m_ffn1_w_gu, m_ffn1_w_down, m_w_in, m_rel_bias, m_conv_w, m_conv_b, m_lru_wa, m_lru_ba, m_lru_wx, m_lru_bx, m_lru_lambda, m_w_att_o, m_w_rec_o, m_w_out, m_ffn2_w_gu, m_ffn2_w_down, v_w_ada, v_b_ada, v_norm_pre, v_norm_post, v_ffn1_w_gu, v_ffn1_w_down, v_w_in, v_rel_bias, v_conv_w, v_conv_b, v_lru_wa, v_lru_ba, v_lru_wx, v_lru_bx, v_lru_lambda, v_w_att_o, v_w_rec_o, v_w_out, v_ffn2_w_gu, v_ffn2_w_down):
    W = dict(w_ada=w_ada, b_ada=b_ada, norm_pre=norm_pre, norm_post=norm_post, ffn1_w_gu=ffn1_w_gu,
             ffn1_w_down=ffn1_w_down, w_in=w_in, rel_bias=rel_bias, conv_w=conv_w, conv_b=conv_b, lru_wa=lru_wa,
             lru_ba=lru_ba, lru_wx=lru_wx, lru_bx=lru_bx, lru_lambda=lru_lambda, w_att_o=w_att_o, w_rec_o=w_rec_o,
             w_out=w_out, ffn2_w_gu=ffn2_w_gu, ffn2_w_down=ffn2_w_down)
    M = dict(w_ada=m_w_ada, b_ada=m_b_ada, norm_pre=m_norm_pre, norm_post=m_norm_post, ffn1_w_gu=m_ffn1_w_gu,
             ffn1_w_down=m_ffn1_w_down, w_in=m_w_in, rel_bias=m_rel_bias, conv_w=m_conv_w, conv_b=m_conv_b,
             lru_wa=m_lru_wa, lru_ba=m_lru_ba, lru_wx=m_lru_wx, lru_bx=m_lru_bx, lru_lambda=m_lru_lambda,
             w_att_o=m_w_att_o, w_rec_o=m_w_rec_o, w_out=m_w_out, ffn2_w_gu=m_ffn2_w_gu, ffn2_w_down=m_ffn2_w_down)
    V = dict(w_ada=v_w_ada, b_ada=v_b_ada, norm_pre=v_norm_pre, norm_post=v_norm_post, ffn1_w_gu=v_ffn1_w_gu,
             ffn1_w_down=v_ffn1_w_down, w_in=v_w_in, rel_bias=v_rel_bias, conv_w=v_conv_w, conv_b=v_conv_b,
             lru_wa=v_lru_wa, lru_ba=v_lru_ba, lru_wx=v_lru_wx, lru_bx=v_lru_bx, lru_lambda=v_lru_lambda,
             w_att_o=v_w_att_o, w_rec_o=v_w_rec_o, w_out=v_w_out, ffn2_w_gu=v_ffn2_w_gu, ffn2_w_down=v_ffn2_w_down)
    mx, my, mc = _mesh_pos()
    p = 2 * mx + my
    e = 4 * mx + 2 * my + mc
    xs = x[0]

    c_arr = jnp.reshape(mc, (1,)).astype(jnp.int32)
    cp_arr = jnp.stack([mc, p]).astype(jnp.int32)
    p_arr = jnp.reshape(p, (1,)).astype(jnp.int32)
    direct = ("w_att_o", "w_rec_o", "w_out", "ffn2_w_gu", "ffn2_w_down")
    geoms = [(kind, R, C, n in direct) for (n, kind, R, C) in BIG]
    names = [b[0] for b in BIG]
    placed = [ag_local(W[n][0], kind, R, C, p_arr, "ag_local_" + n) for (n, kind, R, C) in BIG[:2]]

    def arrived(fly, lo, hi, ssem, rsem, after, tag):
        done = ag_wait(fly, geoms[lo:hi], ssem, rsem, after, "ag_wait_" + tag)
        return [a if both else ag_forward(a, kind, R, C, "ag_forward_" + n)
                for a, (kind, R, C, both), n in zip(done, geoms[lo:hi], names[lo:hi])]

    g1 = ag_small(_pack([c, norm_pre, norm_post, conv_w], 32), "ag_small_params")
    c_all, npre4, npost4, cw4 = _unpack(g1, [(D,), (3, 256), (3, 256), (4, 256)])
    chipwise = lambda a: jnp.moveaxis(a[0::2], 0, 1).reshape(a.shape[1], D)
    npre, npost, conv_full = chipwise(npre4), chipwise(npost4), chipwise(cw4)

    b_cols = lax.dynamic_slice(b_ada, (0, p * 2304), (1, 2304))
    mod_cols = ada_fwd(c_all, w_ada[0], b_cols, "ada_fwd")
    g2 = ag_small(mod_cols.reshape(144, 128), "ag_mod")
    mod_all = jnp.moveaxis(g2[0::2].reshape(4, 8, 2304), 0, 1).reshape(8, 9 * D)
    mod = lax.dynamic_index_in_dim(mod_all, e, 0, keepdims=False).reshape(3, 3, D)
    zeros3 = jnp.zeros((3, D), F32)
    vecs = [jnp.concatenate([npre[k:k + 1], npost[k:k + 1], mod[k], zeros3], axis=0) for k in range(3)]

    gu_s, gu_r, gu_fly, tok_gu = ag_start(placed[:1], geoms[:1], [g2], "ag_start_ffn1_gu")
    dn_s, dn_r, dn_fly, tok0 = ag_start(placed[1:2], geoms[1:2], [tok_gu], "ag_start_ffn1_down")
    placed += [ag_local(W[n][0], kind, R, C, p_arr, "ag_local_" + n, after=[tok0]) for (n, kind, R, C) in BIG[2:]]
    f1_gu, = arrived(gu_fly, 0, 1, gu_s, gu_r, placed[2:], "ffn1_gu")
    f1_dn, = arrived(dn_fly, 1, 2, dn_s, dn_r, f1_gu, "ffn1_down")
    mix_s, mix_r, mix_fly, tok1 = ag_start(placed[2:6], geoms[2:6], [f1_gu, f1_dn], "ag_start_mixer")
    ffn_s, ffn_r, ffn_fly, tok2 = ag_start(placed[6:], geoms[6:], [tok1], "ag_start_ffn2")
    wa_bd = _block_diag4(lru_wa[0]).astype(BF16)
    wx_bd = _block_diag4(lru_wx[0]).astype(BF16)
    pvec = jnp.concatenate([conv_full, conv_b, lru_ba, lru_bx, lru_lambda], axis=0)
    bias = _bias_window(rel_bias[0]).reshape(4, 256, WIN)

    f1_u = f1_gu[:, FF:]
    x1, h1, g1_, u1, a1, f1 = ffn_fwd(xs, vecs[0] + tok2[0:1, 0:1], f1_gu, f1_u, f1_dn, 0.5, "ffn1_fwd")
    win, wao, wro, wout = arrived(mix_fly, 2, 6, mix_s, mix_r, x1, "mixer")
    h2, qkv, rest = proj_fwd(x1, vecs[1], win, "proj_fwd")
    ao = attn_fwd(qkv, bias, "attn_fwd")
    hl, hg = lru_fwd(rest, pvec, wa_bd, wx_bd, "lru_fwd")
    x2, att, rec, mg, f2 = mix_out_fwd(x1, ao, hg, rest, vecs[1], wao, wro, wout, "mix_out_fwd")
    f2_gu, f2_dn = arrived(ffn_fly, 6, 8, ffn_s, ffn_r, x2, "ffn2")
    f2_u = f2_gu[:, FF:]
    dy, h3, g3_, u3, a3, f3, lvec = ffn_fwd(x2, vecs[2], f2_gu, f2_u, f2_dn, 0.5, "ffn2_fwd", tgt=loss_target[0])

    G, grads = {}, {}
    geo = {n: (kind, R, C) for (n, kind, R, C) in BIG}

    def reduce_begin(names, tag):
        ps, rb = [], []
        for n in names:
            got = pair_push(G[n], geo[n][0], c_arr, "rs_push_" + n)
            a, b = pair_add(G[n], got, geo[n][0], cp_arr, "rs_pair_sum_" + n)
            ps.append(a)
            rb.append(b)
        return rs_start(ps, rb, [], "rs_start_" + tag)

    def reduce_end(names, flight, after, tag):
        ssem, rsem, ps, rb, _ = flight
        for a, n in zip(rs_wait(ps, rb, ssem, rsem, after, "rs_wait_" + tag), names):
            grads[n] = sum_share(a, *geo[n], "rs_sum_share_" + n)[None]

    dx2, df3, dgu3, va2 = ffn_bwd(dy, x2, f3, g3_, u3, vecs[2], f2_gu, f2_u, f2_dn, 0.5, "ffn2_bwd")
    G["ffn2_w_gu"] = mm_tn(h3, dgu3, "dw_ffn2_gu", D, 1408, 1024)
    G["ffn2_w_down"] = mm_tn(a3, df3, "dw_ffn2_down", 1408, D, 1024)
    fly_ffn2 = reduce_begin(("ffn2_w_gu", "ffn2_w_down"), "ffn2")
    vec1 = vecs[1] + fly_ffn2[4][0:1, 0:1]
    df2, d_att, d_rec, dao, dhl, d3, va_out = mix_out_bwd(dx2, f2, att, rec, rest, hl, vec1, wao, wro, wout,
                                                          "mix_out_bwd")
    G["w_out"] = mm_tn(mg, df2, "dw_out", D, D, 1024)
    G["w_att_o"] = mm_tn(ao, d_att, "dw_att_o", 512, D, 1024)
    G["w_rec_o"] = mm_tn(hg, d_rec, "dw_rec_o", D, D, 1024)
    dq, db, dkv = attn_bwd(qkv, dao, bias, "attn_bwd")
    dxr, v_lru, dwa_bd, dwx_bd = lru_bwd(dhl, hl, rest, pvec, wa_bd, wx_bd, "lru_bwd")
    dx1, va_in = proj_bwd(dq, dkv, dxr, d3, win, x1, dx2, vecs[1], "proj_bwd")
    G["w_in"] = dw_in(h2, dq, dkv, dxr, d3, "dw_in")
    fly_mix = reduce_begin(("w_in", "w_att_o", "w_rec_o", "w_out"), "mixer")
    vec0 = vecs[0] + fly_mix[4][0:1, 0:1]
    dx0, df1, dgu1, va0 = ffn_bwd(dx1, xs, f1, g1_, u1, vec0, f1_gu, f1_u, f1_dn, 0.5, "ffn1_bwd")
    G["ffn1_w_gu"] = mm_tn(h1, dgu1, "dw_ffn1_gu", D, 1408, 1024)
    G["ffn1_w_down"] = mm_tn(a1, df1, "dw_ffn1_down", 1408, D, 1024)
    fly_ffn1 = reduce_begin(("ffn1_w_gu", "ffn1_w_down"), "ffn1")
    reduce_end(("ffn2_w_gu", "ffn2_w_down"), fly_ffn2, [fly_ffn1[4]], "ffn2")
    reduce_end(("w_in", "w_att_o", "w_rec_o", "w_out"), fly_mix, [fly_ffn1[4], grads["ffn2_w_down"]], "mixer")

    va1 = va_out + va_in
    vas = (va0, va1, va2)
    dmod = jnp.stack([v[2:5] for v in vas])
    part = {"b_ada": dmod, "norm_pre": jnp.stack([v[0] for v in vas]), "norm_post": jnp.stack([v[1] for v in vas]),
            "rel_bias": bias_grad(db.reshape(8, 128, WIN), "bias_grad")[:, :257], "conv_w": v_lru[0:4], "conv_b": v_lru[4],
            "lru_wa": _diag_blocks(dwa_bd), "lru_ba": v_lru[5], "lru_wx": _diag_blocks(dwx_bd), "lru_bx": v_lru[6],
            "lru_lambda": v_lru[7]}
    full_shapes = {"b_ada": (9 * D,), "norm_pre": (3, D), "norm_post": (3, D), "rel_bias": (8, 257),
                   "conv_w": (4, D), "conv_b": (D,), "lru_wa": (16, 64, 64), "lru_ba": (D,),
                   "lru_wx": (16, 64, 64), "lru_bx": (D,), "lru_lambda": (D,)}
    g3 = ag_small(_pack([part[n] for n in SMALL] + [lvec[0:1, 0:1]], 1232), "ag_small_grads")
    summed = _unpack(sum_lead(g3, "sum_small_grads"), [full_shapes[n] for n in SMALL] + [(1,)])
    red = dict(zip(SMALL, summed[:-1]))
    loss = summed[-1][0]
    cols = lambda a: lax.dynamic_slice(a, (0, p * 256), (a.shape[0], 256))
    grads.update({"b_ada": red["b_ada"][None], "norm_pre": cols(red["norm_pre"])[None],
                  "norm_post": cols(red["norm_post"])[None], "rel_bias": red["rel_bias"][None],
                  "conv_w": cols(red["conv_w"])[None], "conv_b": red["conv_b"][None], "lru_wa": red["lru_wa"][None],
                  "lru_ba": red["lru_ba"][None], "lru_wx": red["lru_wx"][None], "lru_bx": red["lru_bx"][None],
                  "lru_lambda": red["lru_lambda"][None]})

    dmod_all = g3[:, :72].reshape(8, 9 * D)
    dmod_cols = jnp.pad(lax.dynamic_slice(dmod_all, (0, p * 2304), (8, 2304)), ((0, 120), (0, 0)))
    c_all_t = jnp.pad(c_all.T, ((0, 0), (0, 120)))
    grads["w_ada"] = ada_bwd(c_all_t, dmod_cols, "ada_bwd")[None]

    delta, new_m, new_v = {}, {}, {}

    def update(n):
        shp = W[n].shape
        res = adamw(W[n][0], grads[n][0], M[n][0], V[n][0], "adamw_" + n, emit_g=n in geo)
        delta[n], new_m[n], new_v[n] = [a.reshape(shp) for a in res[:3]]
        if n in geo:
            grads[n] = res[3].reshape(shp)

    for n in ("w_ada", "ffn2_w_gu", "ffn2_w_down", "w_in", "w_att_o", "w_rec_o", "w_out"):
        update(n)
    packed = [_pack([src[n] for n in SMALL], 1168) for src in (W, grads, M, V)]
    outs = adamw(*packed, "adamw_small")
    for dst, blk in zip((delta, new_m, new_v), outs):
        for n, a in zip(SMALL, _unpack(blk, [W[n].shape for n in SMALL])):
            dst[n] = a
    reduce_end(("ffn1_w_gu", "ffn1_w_down"), fly_ffn1,
               [outs[0], delta["w_ada"], delta["ffn2_w_gu"], delta["ffn2_w_down"], delta["w_in"], delta["w_out"]], "ffn1")
    for n in ("ffn1_w_gu", "ffn1_w_down"):
        update(n)

    return (loss, dx0[None], *[grads[n] for n in WEIGHTS], *[delta[n] for n in WEIGHTS],
            *[new_m[n] for n in WEIGHTS], *[new_v[n] for n in WEIGHTS])
```

```python
import functools

import numpy as np
import jax
import jax.numpy as jnp
from jax import lax
from jax.experimental import pallas as pl
from jax.experimental.pallas import tpu as pltpu

F32 = jnp.float32
BF16 = jnp.bfloat16

D = 1024
FF = 2816
PW = 5632
HP = 128
CHUNK = 64
WIN = 640
TQ = 512
EPS = 1e-6
NEG = -1e30
LRU_C = 8.0
N_DEV = 8
VMEM_LIMIT = 56 * 1024 * 1024

ADAM_LR, ADAM_B1, ADAM_B2, ADAM_EPS, ADAM_WD, ADAM_STEP = 0.001, 0.9, 0.999, 1e-08, 0.01, 10

MESH = pl.DeviceIdType.MESH
ANY = pl.BlockSpec(memory_space=pl.ANY)


def _cp(*sem):
    return pltpu.CompilerParams(dimension_semantics=tuple(sem), vmem_limit_bytes=VMEM_LIMIT)


def _dot(a, b):
    return jnp.dot(a, b, preferred_element_type=F32)


def _dot_nt(a, b):
    return lax.dot_general(a, b, (((1,), (1,)), ((), ())), preferred_element_type=F32)


def _dot_tn(a, b):
    return lax.dot_general(a, b, (((0,), (0,)), ((), ())), preferred_element_type=F32)


def _mean(v):
    return jnp.mean(v, axis=-1, keepdims=True)


def _colsum(v):
    return jnp.sum(v, axis=0, keepdims=True)


def _sigmoid(v):
    return 0.5 * jnp.tanh(0.5 * v) + 0.5


_GK = 0.7978845608028654


def _gelu(v):
    t = jnp.tanh(_GK * (v + 0.044715 * v * v * v))
    return 0.5 * v * (1.0 + t)


def _pre_norm(xv, vec_ref):
    r = lax.rsqrt(_mean(xv * xv) + EPS)
    n = xv * r * vec_ref[0:1, :]
    return n * (1.0 + vec_ref[3:4, :]) + vec_ref[2:3, :]


def _pre_norm_bwd(dh, xv, dres, vec_ref, vacc_ref):
    r = lax.rsqrt(_mean(xv * xv) + EPS)
    xh = xv * r
    n = xh * vec_ref[0:1, :]
    vacc_ref[2:3, :] += _colsum(dh)
    vacc_ref[3:4, :] += _colsum(dh * n)
    dn = dh * (1.0 + vec_ref[3:4, :])
    vacc_ref[0:1, :] += _colsum(dn * xh)
    dxh = dn * vec_ref[0:1, :]
    return r * (dxh - xh * _mean(dxh * xh)) + dres


def _post_norm_bwd(dxo, fv, res, vec_ref, vacc_ref):
    rf = lax.rsqrt(_mean(fv * fv) + EPS)
    fh = fv * rf
    gp = vec_ref[1:2, :]
    vacc_ref[4:5, :] += _colsum(res * dxo * (fh * gp))
    dy = (res * vec_ref[4:5, :]) * dxo
    vacc_ref[1:2, :] += _colsum(dy * fh)
    dfn = dy * gp
    return rf * (dfn - fh * _mean(dfn * fh))


def ffn_fwd(x, vec, w_gu, w_u, w_dn, res, name, tgt=None, tm=1024, tf=512):
    S = x.shape[0]
    tm = min(tm, S)
    nt = S // tm
    nf = -(-FF // tf)
    tail = FF - tf * (nf - 1)
    head = tgt is not None

    def body(*refs):
        x_ref, vec_ref, wg_ref, wu_ref, wd_ref = refs[:5]
        if head:
            t_ref, xo_ref, h_ref, g_ref, u_ref, a_ref, f_ref, l_ref, hs, acc, lacc = refs[5:]
        else:
            xo_ref, h_ref, g_ref, u_ref, a_ref, f_ref, hs, acc = refs[5:]
        i, j = pl.program_id(0), pl.program_id(1)

        @pl.when(j == 0)
        def _():
            h = _pre_norm(x_ref[...], vec_ref).astype(BF16)
            hs[...] = h
            h_ref[...] = h
            acc[...] = jnp.zeros_like(acc)

        def chunk(w):
            h = hs[...]
            g = _dot(h, wg_ref[:, 0:w])
            u = _dot(h, wu_ref[:, 0:w])
            g_ref[:, 0:w] = g.astype(BF16)
            u_ref[:, 0:w] = u.astype(BF16)
            a = (g * _sigmoid(g) * u).astype(BF16)
            a_ref[:, 0:w] = a
            acc[...] += _dot(a, wd_ref[0:w, :])

        @pl.when(j < nf - 1)
        def _():
            chunk(tf)

        @pl.when(j == nf - 1)
        def _():
            chunk(tail)
            f = acc[...]
            f_ref[...] = f.astype(BF16)
            y = f * lax.rsqrt(_mean(f * f) + EPS) * vec_ref[1:2, :]
            xo = x_ref[...] + (res * vec_ref[4:5, :]) * y
            if head:
                @pl.when(i == 0)
                def _():
                    lacc[...] = jnp.zeros_like(lacc)

                d = xo - t_ref[...]
                xo_ref[...] = d * (1.0 / D)
                lacc[...] += _colsum(d * d)

                @pl.when(i == nt - 1)
                def _():
                    l_ref[...] = jnp.broadcast_to(0.5 * jnp.sum(lacc[...]) * (1.0 / D), (8, 128))
            else:
                xo_ref[...] = xo

    row = lambda i, j: (i, 0)
    col = lambda i, j: (i, j)
    once = dict(pipeline_mode=pl.Buffered(1)) if head else {}
    in_specs = [pl.BlockSpec((tm, D), row, **once), pl.BlockSpec((8, D), lambda i, j: (0, 0)),
                pl.BlockSpec((D, tf), lambda i, j: (0, j)), pl.BlockSpec((D, tf), lambda i, j: (0, j)),
                pl.BlockSpec((tf, D), lambda i, j: (j, 0))]
    out_specs = [pl.BlockSpec((tm, D), row), pl.BlockSpec((tm, D), row), pl.BlockSpec((tm, tf), col),
                 pl.BlockSpec((tm, tf), col), pl.BlockSpec((tm, tf), col), pl.BlockSpec((tm, D), row)]
    out_shape = [jax.ShapeDtypeStruct((S, D), F32), jax.ShapeDtypeStruct((S, D), BF16),
                 jax.ShapeDtypeStruct((S, FF), BF16), jax.ShapeDtypeStruct((S, FF), BF16),
                 jax.ShapeDtypeStruct((S, FF), BF16), jax.ShapeDtypeStruct((S, D), BF16)]
    scratch = [pltpu.VMEM((tm, D), BF16), pltpu.VMEM((tm, D), F32)]
    args = [x, vec, w_gu, w_u, w_dn]
    if head:
        in_specs.append(pl.BlockSpec((tm, D), row, **once))
        out_specs.append(pl.BlockSpec((8, 128), lambda i, j: (0, 0)))
        out_shape.append(jax.ShapeDtypeStruct((8, 128), F32))
        scratch.append(pltpu.VMEM((1, D), F32))
        args.append(tgt)
    return pl.pallas_call(
        body, name=name, grid=(nt, nf), in_specs=in_specs, out_specs=out_specs, out_shape=out_shape,
        scratch_shapes=scratch,
        compiler_params=_cp("arbitrary" if head else "parallel", "arbitrary"),
    )(*args)


def ffn_bwd(dxo, x, f, g, u, vec, w_gu, w_u, w_dn, res, name, tm=1024, tf=512):
    S = x.shape[0]
    tm = min(tm, S)
    nf = -(-FF // tf)
    tail = FF - tf * (nf - 1)

    def body(dxo_ref, x_ref, f_ref, g_ref, u_ref, vec_ref, wg_ref, wu_ref, wd_ref,
             dx_ref, df_ref, dgu_ref, vacc_ref, dfs, acc):
        i, j = pl.program_id(0), pl.program_id(1)

        @pl.when((i == 0) & (j == 0))
        def _():
            vacc_ref[...] = jnp.zeros_like(vacc_ref)

        @pl.when(j == 0)
        def _():
            df = _post_norm_bwd(dxo_ref[...], f_ref[...].astype(F32), res, vec_ref, vacc_ref).astype(BF16)
            dfs[...] = df
            df_ref[...] = df
            acc[...] = jnp.zeros_like(acc)

        def chunk(w):
            da = _dot_nt(dfs[...], wd_ref[0:w, :])
            gv, uv = g_ref[:, 0:w].astype(F32), u_ref[:, 0:w].astype(F32)
            sg = _sigmoid(gv)
            dg = (da * uv * (sg * (1.0 + gv * (1.0 - sg)))).astype(BF16)
            du = (da * (gv * sg)).astype(BF16)
            dgu_ref[0, :, 0:w] = dg
            dgu_ref[1, :, 0:w] = du
            acc[...] += _dot_nt(dg, wg_ref[:, 0:w]) + _dot_nt(du, wu_ref[:, 0:w])

        @pl.when(j < nf - 1)
        def _():
            chunk(tf)

        @pl.when(j == nf - 1)
        def _():
            chunk(tail)
            dx_ref[...] = _pre_norm_bwd(acc[...], x_ref[...], dxo_ref[...], vec_ref, vacc_ref)

    row = lambda i, j: (i, 0)
    col = lambda i, j: (i, j)
    return pl.pallas_call(
        body, name=name, grid=(S // tm, nf),
        in_specs=[pl.BlockSpec((tm, D), row), pl.BlockSpec((tm, D), row, pipeline_mode=pl.Buffered(1)),
                  pl.BlockSpec((tm, D), row)]
        + [pl.BlockSpec((tm, tf), col), pl.BlockSpec((tm, tf), col),
                  pl.BlockSpec((8, D), lambda i, j: (0, 0)),
                  pl.BlockSpec((D, tf), lambda i, j: (0, j)), pl.BlockSpec((D, tf), lambda i, j: (0, j)),
                  pl.BlockSpec((tf, D), lambda i, j: (j, 0))],
        out_specs=[pl.BlockSpec((tm, D), row), pl.BlockSpec((tm, D), row),
                   pl.BlockSpec((2, tm, tf), lambda i, j: (0, i, j)),
                   pl.BlockSpec((8, D), lambda i, j: (0, 0))],
        out_shape=[jax.ShapeDtypeStruct((S, D), F32), jax.ShapeDtypeStruct((S, D), BF16),
                   jax.ShapeDtypeStruct((2, S, FF), BF16), jax.ShapeDtypeStruct((8, D), F32)],
        scratch_shapes=[pltpu.VMEM((tm, D), BF16), pltpu.VMEM((tm, D), F32)],
        compiler_params=_cp("arbitrary", "arbitrary"),
    )(dxo, x, f, g, u, vec, w_gu, w_u, w_dn)


def mm_tn(a, b, name, tm, tn, tk, out_dtype=BF16):
    S, M = a.shape
    if b.ndim == 3:
        G, _, Nf = b.shape
    else:
        G, Nf = 1, b.shape[1]
    N = G * Nf
    tk = min(tk, S)
    nbf = Nf // tn
    nk = S // tk

    def body(a_ref, b_ref, o_ref, acc):
        k = pl.program_id(2)

        @pl.when(k == 0)
        def _():
            acc[...] = jnp.zeros_like(acc)

        acc[...] += _dot_tn(a_ref[...], b_ref[...])

        @pl.when(k == nk - 1)
        def _():
            o_ref[...] = acc[...].astype(out_dtype)

    if b.ndim == 3:
        b_spec = pl.BlockSpec((None, tk, tn), lambda i, j, k: (j // nbf, k, j % nbf))
    else:
        b_spec = pl.BlockSpec((tk, tn), lambda i, j, k: (k, j))
    return pl.pallas_call(
        body, name=name, grid=(M // tm, N // tn, nk),
        in_specs=[pl.BlockSpec((tk, tm), lambda i, j, k: (k, i)), b_spec],
        out_specs=pl.BlockSpec((tm, tn), lambda i, j, k: (i, j)),
        out_shape=jax.ShapeDtypeStruct((M, N), out_dtype),
        scratch_shapes=[pltpu.VMEM((tm, tn), F32)],
        compiler_params=_cp("parallel", "parallel", "arbitrary"),
    )(a, b)


def proj_fwd(x, vec, w_in, name, tm=2048, tn=512):
    S = x.shape[0]
    tm = min(tm, S)
    nq = 1536 // tn

    def body(x_ref, vec_ref, w_ref, h_ref, qkv_ref, rest_ref, hs):
        j = pl.program_id(1)

        @pl.when(j == 0)
        def _():
            h = _pre_norm(x_ref[...], vec_ref).astype(BF16)
            hs[...] = h
            h_ref[...] = h

        r = _dot(hs[...], w_ref[...])

        @pl.when(j < nq)
        def _():
            qkv_ref[...] = r.astype(BF16)

        @pl.when(j >= nq)
        def _():
            rest_ref[...] = r.astype(BF16)

    row = lambda i, j: (i, 0)
    return pl.pallas_call(
        body, name=name, grid=(S // tm, PW // tn),
        in_specs=[pl.BlockSpec((tm, D), row), pl.BlockSpec((8, D), lambda i, j: (0, 0)),
                  pl.BlockSpec((D, tn), lambda i, j: (0, j))],
        out_specs=[pl.BlockSpec((tm, D), row),
                   pl.BlockSpec((tm, tn), lambda i, j: (i, jnp.minimum(j, nq - 1))),
                   pl.BlockSpec((tm, tn), lambda i, j: (i, jnp.maximum(j - nq, 0)))],
        out_shape=[jax.ShapeDtypeStruct((S, D), BF16), jax.ShapeDtypeStruct((S, 1536), BF16),
                   jax.ShapeDtypeStruct((S, 4096), BF16)],
        scratch_shapes=[pltpu.VMEM((tm, D), BF16)],
        compiler_params=_cp("parallel", "arbitrary"),
    )(x, vec, w_in)


def proj_bwd(dq, dkv, dxr, d3, w_in, x, dxo, vec, name, tm=1024, tk=512):
    S = x.shape[0]
    tm = min(tm, S)
    nk = PW // tk

    def body(dq_ref, dkv_ref, dxr_ref, d3_ref, w_ref, x_ref, dxo_ref, vec_ref, dx_ref, vacc_ref, acc):
        i, j = pl.program_id(0), pl.program_id(1)

        @pl.when((i == 0) & (j == 0))
        def _():
            vacc_ref[...] = jnp.zeros_like(vacc_ref)

        @pl.when(j == 0)
        def _():
            acc[...] = _dot_nt(dq_ref[...], w_ref[...])

        @pl.when((j >= 1) & (j < 3))
        def _():
            acc[...] += _dot_nt(dkv_ref[...], w_ref[...])

        @pl.when((j >= 3) & (j < 5))
        def _():
            acc[...] += _dot_nt(dxr_ref[...], w_ref[...])

        @pl.when(j >= 5)
        def _():
            acc[...] += _dot_nt(d3_ref[...], w_ref[...])

        @pl.when(j == nk - 1)
        def _():
            dx_ref[...] = _pre_norm_bwd(acc[...], x_ref[...], dxo_ref[...], vec_ref, vacc_ref)

    row = lambda i, j: (i, 0)
    return pl.pallas_call(
        body, name=name, grid=(S // tm, nk),
        in_specs=[pl.BlockSpec((None, tm, tk), lambda i, j: (0, i, 0)),
                  pl.BlockSpec((None, tm, tk), lambda i, j: (jnp.clip(j - 1, 0, 1), i, 0)),
                  pl.BlockSpec((tm, tk), lambda i, j: (i, jnp.clip(j - 3, 0, 1))),
                  pl.BlockSpec((None, tm, tk), lambda i, j: (jnp.clip(j - 5, 0, 5) // 2, i, jnp.clip(j - 5, 0, 5) % 2)),
                  pl.BlockSpec((D, tk), lambda i, j: (0, j)),
                  pl.BlockSpec((tm, D), row), pl.BlockSpec((tm, D), row),
                  pl.BlockSpec((8, D), lambda i, j: (0, 0))],
        out_specs=[pl.BlockSpec((tm, D), row), pl.BlockSpec((8, D), lambda i, j: (0, 0))],
        out_shape=[jax.ShapeDtypeStruct((S, D), F32), jax.ShapeDtypeStruct((8, D), F32)],
        scratch_shapes=[pltpu.VMEM((tm, D), F32)],
        compiler_params=_cp("arbitrary", "arbitrary"),
    )(dq, dkv, dxr, d3, w_in, x, dxo, vec)


def _two_heads(v, lane):
    zero = jnp.zeros((), v.dtype)
    return jnp.concatenate([jnp.where(lane < 64, v, zero), jnp.where(lane >= 64, v, zero)], axis=0)


def _attn_probs(qm, ka, bias_h, i, grp):
    s = _dot_nt(qm, ka) + bias_h
    col = lax.broadcasted_iota(jnp.int32, s.shape, 1)
    first_key = jnp.where(i == 0, 512 - 128 * grp, 0)
    s = jnp.where(col >= first_key, s, NEG)
    e = jnp.exp(s - jnp.max(s, axis=-1, keepdims=True))
    return e * (1.0 / jnp.sum(e, axis=-1, keepdims=True))


def attn_fwd(qkv, bias, name):
    S = qkv.shape[0]
    nb = S // TQ

    def body(q_ref, kp_ref, kc_ref, vp_ref, vc_ref, b_ref, o_ref, kw, vw):
        i = pl.program_id(1)
        kw[0:TQ, :] = kp_ref[...]
        kw[TQ:2 * TQ, :] = kc_ref[...]
        vw[0:TQ, :] = vp_ref[...]
        vw[TQ:2 * TQ, :] = vc_ref[...]
        lane = lax.broadcasted_iota(jnp.int32, (1, HP), 1)

        def group(a, carry):
            r0 = pl.multiple_of(a * 128, 128)
            qa = q_ref[pl.ds(r0, 128), :] * jnp.asarray(0.125, BF16)
            ka = kw[pl.ds(r0, WIN), :]
            va = vw[pl.ds(r0, WIN), :]
            p = _attn_probs(_two_heads(qa, lane), ka, b_ref[...], i, a)
            o2 = _dot(p.astype(BF16), va)
            o_ref[pl.ds(r0, 128), :] = jnp.where(lane < 64, o2[0:128], o2[128:256]).astype(BF16)
            return carry

        lax.fori_loop(0, TQ // 128, group, 0, unroll=True)

    prev = lambda h, i: (jnp.maximum(i - 1, 0), 0)
    return pl.pallas_call(
        body, name=name, grid=(4, nb),
        in_specs=[pl.BlockSpec((TQ, HP), lambda h, i: (i, h)),
                  pl.BlockSpec((TQ, HP), lambda h, i: (jnp.maximum(i - 1, 0), 4 + h)),
                  pl.BlockSpec((TQ, HP), lambda h, i: (i, 4 + h)),
                  pl.BlockSpec((TQ, HP), lambda h, i: (jnp.maximum(i - 1, 0), 8 + h)),
                  pl.BlockSpec((TQ, HP), lambda h, i: (i, 8 + h)),
                  pl.BlockSpec((None, 256, WIN), lambda h, i: (h, 0, 0))],
        out_specs=pl.BlockSpec((TQ, HP), lambda h, i: (i, h)),
        out_shape=jax.ShapeDtypeStruct((S, 512), BF16),
        scratch_shapes=[pltpu.VMEM((2 * TQ, HP), BF16), pltpu.VMEM((2 * TQ, HP), BF16)],
        compiler_params=_cp("parallel", "arbitrary"),
    )(qkv, qkv, qkv, qkv, qkv, bias)


def attn_bwd(qkv, do, bias, name):
    S = qkv.shape[0]
    nb = S // TQ

    def body(q_ref, kp_ref, kc_ref, vp_ref, vc_ref, do_ref, b_ref, dqkv_ref, db_ref, dkv_ref, kw, vw, ak, av):
        i = pl.program_id(1)

        @pl.when(i == 0)
        def _():
            db_ref[...] = jnp.zeros_like(db_ref)
            ak[...] = jnp.zeros_like(ak)
            av[...] = jnp.zeros_like(av)

        @pl.when(i > 0)
        def _():
            ak[0:TQ, :] = ak[TQ:2 * TQ, :]
            av[0:TQ, :] = av[TQ:2 * TQ, :]
            ak[TQ:2 * TQ, :] = jnp.zeros((TQ, HP), F32)
            av[TQ:2 * TQ, :] = jnp.zeros((TQ, HP), F32)

        @pl.when(i < nb)
        def _():
            kw[0:TQ, :] = kp_ref[...]
            kw[TQ:2 * TQ, :] = kc_ref[...]
            vw[0:TQ, :] = vp_ref[...]
            vw[TQ:2 * TQ, :] = vc_ref[...]
            lane = lax.broadcasted_iota(jnp.int32, (1, HP), 1)

            def group(a, carry):
                r0 = pl.multiple_of(a * 128, 128)
                q2 = _two_heads(q_ref[pl.ds(r0, 128), :] * jnp.asarray(0.125, BF16), lane)
                do2 = _two_heads(do_ref[pl.ds(r0, 128), :], lane)
                ka = kw[pl.ds(r0, WIN), :]
                va = vw[pl.ds(r0, WIN), :]
                p = _attn_probs(q2, ka, b_ref[...], i, a)
                dp = _dot_nt(do2, va)
                ds = p * (dp - jnp.sum(p * dp, axis=-1, keepdims=True))
                db_ref[...] += ds
                dsb = ds.astype(BF16)
                dq2 = _dot(dsb, ka)
                ak[pl.ds(r0, WIN), :] += _dot_tn(dsb, q2)
                av[pl.ds(r0, WIN), :] += _dot_tn(p.astype(BF16), do2)
                dq = jnp.where(lane < 64, dq2[0:128], dq2[128:256])
                dqkv_ref[0, pl.ds(r0, 128), :] = (dq * 0.125).astype(BF16)
                return carry

            lax.fori_loop(0, TQ // 128, group, 0, unroll=True)

        @pl.when(i > 0)
        def _():
            dkv_ref[0] = ak[0:TQ, :].astype(BF16)
            dkv_ref[1] = av[0:TQ, :].astype(BF16)

    cur = lambda i: jnp.minimum(i, nb - 1)
    prv = lambda i: jnp.clip(i - 1, 0, nb - 1)
    dq, db, dkv = pl.pallas_call(
        body, name=name, grid=(4, nb + 1),
        in_specs=[pl.BlockSpec((TQ, HP), lambda h, i: (cur(i), h)),
                  pl.BlockSpec((TQ, HP), lambda h, i: (prv(i), 4 + h)),
                  pl.BlockSpec((TQ, HP), lambda h, i: (cur(i), 4 + h)),
                  pl.BlockSpec((TQ, HP), lambda h, i: (prv(i), 8 + h)),
                  pl.BlockSpec((TQ, HP), lambda h, i: (cur(i), 8 + h)),
                  pl.BlockSpec((TQ, HP), lambda h, i: (cur(i), h)),
                  pl.BlockSpec((None, 256, WIN), lambda h, i: (h, 0, 0))],
        out_specs=[pl.BlockSpec((1, TQ, HP), lambda h, i: (0, cur(i), h)),
                   pl.BlockSpec((None, 256, WIN), lambda h, i: (h, 0, 0)),
                   pl.BlockSpec((2, TQ, HP), lambda h, i: (0, prv(i), h))],
        out_shape=[jax.ShapeDtypeStruct((1, S, 512), BF16), jax.ShapeDtypeStruct((4, 256, WIN), F32),
                   jax.ShapeDtypeStruct((2, S, 512), BF16)],
        scratch_shapes=[pltpu.VMEM((2 * TQ, HP), BF16), pltpu.VMEM((2 * TQ, HP), BF16),
                        pltpu.VMEM((2 * TQ, HP), F32), pltpu.VMEM((2 * TQ, HP), F32)],
        compiler_params=_cp("parallel", "arbitrary"),
    )(qkv, qkv, qkv, qkv, qkv, do, bias)
    return dq, db, dkv


def bias_grad(db, name):
    def body(db_ref, o_ref):
        r = lax.broadcasted_iota(jnp.int32, (128, 128), 0)
        c = lax.broadcasted_iota(jnp.int32, (128, 128), 1)
        flip = (r + c == 127).astype(BF16)
        lane = lax.broadcasted_iota(jnp.int32, (16, 384), 1)
        src = lax.broadcasted_iota(jnp.int32, (128, 384), 0)
        dst = lax.broadcasted_iota(jnp.int32, (128, 384), 1)

        def split_dot(v, m):
            hi = v.astype(BF16)
            r1 = v - hi.astype(F32)
            mid = r1.astype(BF16)
            lo = (r1 - mid.astype(F32)).astype(BF16)
            return _dot(hi, m) + _dot(mid, m) + _dot(lo, m)

        def diag_sums(w):
            y = pltpu.roll(split_dot(w, flip), 0, 1, stride=1, stride_axis=0)
            return jnp.broadcast_to(_colsum(y), (16, 128))

        w4 = db_ref[0, :, 512:640]
        w3 = db_ref[0, :, 384:512]
        far = jnp.sum(db_ref[0, :, 0:384]) + jnp.sum(jnp.where(r >= c, w3, 0.0))
        lo4 = diag_sums(jnp.where(r >= c, w4, 0.0))
        up4 = diag_sums(jnp.where(r < c, w4, 0.0))
        up3 = diag_sums(jnp.where(r < c, w3, 0.0))
        p_lo4 = (dst == 128 + (src + 1) % 128).astype(BF16)
        p_up4 = ((dst == src + 1) & (src < 127)).astype(BF16)
        p_up3 = ((dst == src + 129) & (src < 127)).astype(BF16)
        out = split_dot(lo4, p_lo4) + split_dot(up4, p_up4) + split_dot(up3, p_up3)
        o_ref[0] = out + jnp.where(lane == 256, far, 0.0)

    return pl.pallas_call(
        body, name=name, grid=(8,),
        in_specs=[pl.BlockSpec((1, 128, WIN), lambda h: (h, 0, 0))],
        out_specs=pl.BlockSpec((1, 16, 384), lambda h: (h, 0, 0)),
        out_shape=jax.ShapeDtypeStruct((8, 16, 384), F32),
        compiler_params=_cp("parallel"),
    )(db)[:, 0, :]


LT = 1024
LC = 512


def _lru_gates(xs, pv_ref, wa_ref, wx_ref, tl):
    xc = (pv_ref[4:5, :] + pv_ref[3:4, :] * xs[pl.ds(8, tl), :] + pv_ref[2:3, :] * xs[pl.ds(7, tl), :]
          + pv_ref[1:2, :] * xs[pl.ds(6, tl), :] + pv_ref[0:1, :] * xs[pl.ds(5, tl), :])
    xcb = xc.astype(BF16)
    pa = jnp.concatenate([_dot(xcb[:, 0:256], wa_ref[0]), _dot(xcb[:, 256:512], wa_ref[1])], axis=1)
    px = jnp.concatenate([_dot(xcb[:, 0:256], wx_ref[0]), _dot(xcb[:, 256:512], wx_ref[1])], axis=1)
    r = _sigmoid(pa + pv_ref[5:6, :])
    ig = _sigmoid(px + pv_ref[6:7, :])
    z = -pv_ref[7:8, :]
    sp = jnp.maximum(z, 0.0) + jnp.log1p(jnp.exp(-jnp.abs(z)))
    log_a = (-LRU_C * r) * sp
    a = jnp.exp(log_a)
    s = jnp.tanh(-log_a) * (1.0 + a * a)
    inv_mult = lax.rsqrt(s)
    mult = jnp.where(s > 0.0, s * inv_mult, 0.0)
    return xc, xcb, r, ig, sp, a, mult, inv_mult


def lru_fwd(rest, pvec, wa, wx, name):
    S = rest.shape[0]
    tl = min(LT, S)
    nt = S // tl

    def body(xr_ref, halo_ref, yr_ref, pv_ref, wa_ref, wx_ref, h_ref, hg_ref, xs, a_s, u_s, h_s, carry):
        ti = pl.program_id(1)

        @pl.when(ti == 0)
        def _():
            carry[...] = jnp.zeros_like(carry)

        xs[0:8, :] = jnp.where(ti > 0, halo_ref[8:16, :].astype(F32), 0.0)
        xs[pl.ds(8, tl), :] = xr_ref[...].astype(F32)
        xc, _, _, ig, _, a, mult, _ = _lru_gates(xs, pv_ref, wa_ref, wx_ref, tl)
        a_s[...] = a
        u_s[...] = mult * (ig * xc)
        row = lax.broadcasted_iota(jnp.int32, (8, LC), 0)

        def blk(bi, c):
            o = pl.multiple_of(bi * 8, 8)
            av = a_s[pl.ds(o, 8), :]
            bv = u_s[pl.ds(o, 8), :]
            for d in (1, 2, 4):
                a_sh = pltpu.roll(av, d, 0)
                b_sh = pltpu.roll(bv, d, 0)
                m = row >= d
                bv = jnp.where(m, av * b_sh + bv, bv)
                av = jnp.where(m, av * a_sh, av)
            hv = bv + av * c
            h_s[pl.ds(o, 8), :] = hv
            return hv[7:8, :]

        carry[...] = lax.fori_loop(0, tl // 8, blk, carry[...])
        h = h_s[...]
        h_ref[...] = h
        hg_ref[...] = (h * _gelu(yr_ref[...].astype(F32))).astype(BF16)

    hb = tl // 16
    return pl.pallas_call(
        body, name=name, grid=(2, nt),
        in_specs=[pl.BlockSpec((tl, LC), lambda c, t: (t, c)),
                  pl.BlockSpec((16, LC), lambda c, t: (jnp.maximum(t * hb - 1, 0), c)),
                  pl.BlockSpec((tl, LC), lambda c, t: (t, 2 + c)),
                  pl.BlockSpec((8, LC), lambda c, t: (0, c)),
                  pl.BlockSpec((2, 256, 256), lambda c, t: (c, 0, 0)),
                  pl.BlockSpec((2, 256, 256), lambda c, t: (c, 0, 0))],
        out_specs=[pl.BlockSpec((tl, LC), lambda c, t: (t, c)), pl.BlockSpec((tl, LC), lambda c, t: (t, c))],
        out_shape=[jax.ShapeDtypeStruct((S, D), F32), jax.ShapeDtypeStruct((S, D), BF16)],
        scratch_shapes=[pltpu.VMEM((tl + 8, LC), F32), pltpu.VMEM((tl, LC), F32), pltpu.VMEM((tl, LC), F32),
                        pltpu.VMEM((tl, LC), F32), pltpu.VMEM((1, LC), F32)],
        compiler_params=_cp("parallel", "arbitrary"),
    )(rest, rest, rest, pvec, wa, wx)


def lru_bwd(dh, h, rest, pvec, wa, wx, name):
    S = rest.shape[0]
    tl = min(LT, S)
    nt = S // tl

    def body(dh_ref, h_ref, hhalo_ref, xr_ref, xhalo_ref, pv_ref, wa_ref, wx_ref,
             dxr_ref, vacc_ref, dwa_ref, dwx_ref,
             xs, hs, a_s, ash_s, b_s, lam_s, dxe, anext, lnext, dxnext):
        ti = pl.program_id(1)
        tr = nt - 1 - ti

        @pl.when(ti == 0)
        def _():
            anext[...] = jnp.zeros_like(anext)
            lnext[...] = jnp.zeros_like(lnext)
            dxnext[...] = jnp.zeros_like(dxnext)
            vacc_ref[...] = jnp.zeros_like(vacc_ref)
            dwa_ref[...] = jnp.zeros_like(dwa_ref)
            dwx_ref[...] = jnp.zeros_like(dwx_ref)

        xs[0:8, :] = jnp.where(tr > 0, xhalo_ref[8:16, :].astype(F32), 0.0)
        xs[pl.ds(8, tl), :] = xr_ref[...].astype(F32)
        xc, xcb, r, ig, sp, a, mult, inv_mult = _lru_gates(xs, pv_ref, wa_ref, wx_ref, tl)

        a_s[pl.ds(0, tl), :] = a
        a_s[pl.ds(tl, 8), :] = jnp.broadcast_to(anext[...], (8, LC))
        ash_s[...] = a_s[pl.ds(1, tl), :]
        b_s[...] = dh_ref[...]
        row = lax.broadcasted_iota(jnp.int32, (8, LC), 0)

        def blk(k, c):
            o = pl.multiple_of((tl // 8 - 1 - k) * 8, 8)
            av = ash_s[pl.ds(o, 8), :]
            bv = b_s[pl.ds(o, 8), :]
            for d in (1, 2, 4):
                a_sh = pltpu.roll(av, 8 - d, 0)
                b_sh = pltpu.roll(bv, 8 - d, 0)
                m = row < 8 - d
                bv = jnp.where(m, bv + av * b_sh, bv)
                av = jnp.where(m, av * a_sh, av)
            lv = bv + av * c
            lam_s[pl.ds(o, 8), :] = lv
            return lv[0:1, :]

        lnext[...] = lax.fori_loop(0, tl // 8, blk, lnext[...])
        anext[...] = a[0:1, :]
        lam = lam_s[...]

        hs[0:8, :] = jnp.where(tr > 0, hhalo_ref[...], 0.0)
        hs[pl.ds(8, tl), :] = h_ref[...]
        d_a = lam * hs[pl.ds(7, tl), :]
        d_mult = lam * (ig * xc)
        d_ig = lam * mult * xc
        dxc = lam * mult * ig
        d_log_a = d_a * a - d_mult * (a * a) * inv_mult
        d_r = d_log_a * (-LRU_C * sp)
        vacc_ref[7:8, :] += _colsum(d_log_a * (-LRU_C * r)) * (-_sigmoid(-pv_ref[7:8, :]))
        d_pa = d_r * r * (1.0 - r)
        d_px = d_ig * ig * (1.0 - ig)
        vacc_ref[5:6, :] += _colsum(d_pa)
        vacc_ref[6:7, :] += _colsum(d_px)
        dpa = d_pa.astype(BF16)
        dpx = d_px.astype(BF16)
        back = []
        for g in range(2):
            sl = slice(256 * g, 256 * g + 256)
            dwa_ref[g] += _dot_tn(xcb[:, sl], dpa[:, sl])
            dwx_ref[g] += _dot_tn(xcb[:, sl], dpx[:, sl])
            back.append(_dot_nt(dpa[:, sl], wa_ref[g]) + _dot_nt(dpx[:, sl], wx_ref[g]))
        dxc = dxc + jnp.concatenate(back, axis=1)
        vacc_ref[4:5, :] += _colsum(dxc)
        for k in range(4):
            vacc_ref[k:k + 1, :] += _colsum(dxc * xs[pl.ds(5 + k, tl), :])
        dxe[pl.ds(0, tl), :] = dxc
        dxe[pl.ds(tl, 8), :] = dxnext[...]
        dxr = (pv_ref[3:4, :] * dxc + pv_ref[2:3, :] * dxe[pl.ds(1, tl), :]
               + pv_ref[1:2, :] * dxe[pl.ds(2, tl), :] + pv_ref[0:1, :] * dxe[pl.ds(3, tl), :])
        dxr_ref[...] = dxr.astype(BF16)
        dxnext[...] = dxc[0:8, :]

    hb = tl // 8
    rev = lambda t: nt - 1 - t
    halo = lambda t: jnp.maximum(rev(t) * hb - 1, 0)
    big = lambda: pltpu.VMEM((tl + 8, LC), F32)
    til = lambda: pltpu.VMEM((tl, LC), F32)
    return pl.pallas_call(
        body, name=name, grid=(2, nt),
        in_specs=[pl.BlockSpec((tl, LC), lambda c, t: (rev(t), c)),
                  pl.BlockSpec((tl, LC), lambda c, t: (rev(t), c)),
                  pl.BlockSpec((8, LC), lambda c, t: (halo(t), c)),
                  pl.BlockSpec((tl, LC), lambda c, t: (rev(t), c)),
                  pl.BlockSpec((16, LC), lambda c, t: (jnp.maximum(rev(t) * (tl // 16) - 1, 0), c)),
                  pl.BlockSpec((8, LC), lambda c, t: (0, c)),
                  pl.BlockSpec((2, 256, 256), lambda c, t: (c, 0, 0)),
                  pl.BlockSpec((2, 256, 256), lambda c, t: (c, 0, 0))],
        out_specs=[pl.BlockSpec((tl, LC), lambda c, t: (rev(t), c)),
                   pl.BlockSpec((8, LC), lambda c, t: (0, c)),
                   pl.BlockSpec((2, 256, 256), lambda c, t: (c, 0, 0)),
                   pl.BlockSpec((2, 256, 256), lambda c, t: (c, 0, 0))],
        out_shape=[jax.ShapeDtypeStruct((S, D), BF16), jax.ShapeDtypeStruct((8, D), F32),
                   jax.ShapeDtypeStruct((4, 256, 256), F32), jax.ShapeDtypeStruct((4, 256, 256), F32)],
        scratch_shapes=[big(), big(), big(), til(), til(), til(), big(),
                        pltpu.VMEM((1, LC), F32), pltpu.VMEM((1, LC), F32), pltpu.VMEM((8, LC), F32)],
        compiler_params=_cp("parallel", "arbitrary"),
    )(dh, h, h, rest, rest, pvec, wa, wx)


def mix_out_fwd(x, ao, hg, rest, vec, w_att_o, w_rec_o, w_out, name, tm=512):
    S = x.shape[0]
    tm = min(tm, S)

    def body(x_ref, ao_ref, hg_ref, ga_ref, gr_ref, vec_ref, wa_ref, wr_ref, wo_ref,
             xo_ref, att_ref, rec_ref, mg_ref, f_ref):
        att = _dot(ao_ref[...], wa_ref[...])
        rec = _dot(hg_ref[...], wr_ref[...])
        att_ref[...] = att.astype(BF16)
        rec_ref[...] = rec.astype(BF16)
        mg = (_sigmoid(ga_ref[...].astype(F32)) * att + _sigmoid(gr_ref[...].astype(F32)) * rec).astype(BF16)
        mg_ref[...] = mg
        f = _dot(mg, wo_ref[...])
        f_ref[...] = f.astype(BF16)
        y = f * lax.rsqrt(_mean(f * f) + EPS) * vec_ref[1:2, :]
        xo_ref[...] = x_ref[...] + (1.0 * vec_ref[4:5, :]) * y

    row = lambda i: (i, 0)
    full = lambda r: pl.BlockSpec((r, D), lambda i: (0, 0))
    return pl.pallas_call(
        body, name=name, grid=(S // tm,),
        in_specs=[pl.BlockSpec((tm, D), row), pl.BlockSpec((tm, 512), row), pl.BlockSpec((tm, D), row),
                  pl.BlockSpec((tm, D), lambda i: (i, 2)), pl.BlockSpec((tm, D), lambda i: (i, 3)),
                  full(8), full(512), full(D), full(D)],
        out_specs=[pl.BlockSpec((tm, D), row)] * 5,
        out_shape=[jax.ShapeDtypeStruct((S, D), F32)] + [jax.ShapeDtypeStruct((S, D), BF16)] * 4,
        compiler_params=_cp("parallel"),
    )(x, ao, hg, rest, rest, vec, w_att_o, w_rec_o, w_out)


def mix_out_bwd(dxo, f, att, rec, rest, h, vec, w_att_o, w_rec_o, w_out, name, tm=512):
    S = dxo.shape[0]
    tm = min(tm, S)

    def body(dxo_ref, f_ref, att_ref, rec_ref, yr_ref, ga_ref, gr_ref, h_ref, vec_ref, wa_ref, wr_ref, wo_ref,
             df_ref, da_ref, dr_ref, dao_ref, dh_ref, d3_ref, vacc_ref):
        @pl.when(pl.program_id(0) == 0)
        def _():
            vacc_ref[...] = jnp.zeros_like(vacc_ref)

        df = _post_norm_bwd(dxo_ref[...], f_ref[...].astype(F32), 1.0, vec_ref, vacc_ref).astype(BF16)
        df_ref[...] = df
        dm = _dot_nt(df, wo_ref[...])
        sa = _sigmoid(ga_ref[...].astype(F32))
        sr = _sigmoid(gr_ref[...].astype(F32))
        d_att = (dm * sa).astype(BF16)
        d_rec = (dm * sr).astype(BF16)
        da_ref[...] = d_att
        dr_ref[...] = d_rec
        d3_ref[1] = (dm * att_ref[...].astype(F32) * (sa * (1.0 - sa))).astype(BF16)
        d3_ref[2] = (dm * rec_ref[...].astype(F32) * (sr * (1.0 - sr))).astype(BF16)
        dao_ref[...] = _dot_nt(d_att, wa_ref[...]).astype(BF16)
        d_hg = _dot_nt(d_rec, wr_ref[...])
        yr = yr_ref[...].astype(F32)
        t = jnp.tanh(_GK * (yr + 0.044715 * yr * yr * yr))
        dh_ref[...] = d_hg * (0.5 * yr * (1.0 + t))
        gelu_grad = 0.5 * (1.0 + t) + 0.5 * yr * (1.0 - t * t) * _GK * (1.0 + 3.0 * 0.044715 * yr * yr)
        d3_ref[0] = (d_hg * h_ref[...] * gelu_grad).astype(BF16)

    row = lambda i: (i, 0)
    full = lambda r: pl.BlockSpec((r, D), lambda i: (0, 0))
    return pl.pallas_call(
        body, name=name, grid=(S // tm,),
        in_specs=[pl.BlockSpec((tm, D), row)] * 4
        + [pl.BlockSpec((tm, D), lambda i: (i, 1)), pl.BlockSpec((tm, D), lambda i: (i, 2)),
           pl.BlockSpec((tm, D), lambda i: (i, 3)), pl.BlockSpec((tm, D), row),
           full(8), full(512), full(D), full(D)],
        out_specs=[pl.BlockSpec((tm, D), row)] * 3
        + [pl.BlockSpec((tm, 512), row), pl.BlockSpec((tm, D), row),
           pl.BlockSpec((3, tm, D), lambda i: (0, i, 0)), pl.BlockSpec((8, D), lambda i: (0, 0))],
        out_shape=[jax.ShapeDtypeStruct((S, D), BF16)] * 3
        + [jax.ShapeDtypeStruct((S, 512), BF16), jax.ShapeDtypeStruct((S, D), F32),
           jax.ShapeDtypeStruct((3, S, D), BF16), jax.ShapeDtypeStruct((8, D), F32)],
        compiler_params=_cp("arbitrary"),
    )(dxo, f, att, rec, rest, rest, rest, h, vec, w_att_o, w_rec_o, w_out)


def dw_in(h, dq, dkv, dxr, d3, name, tk=1024, tn=512):
    S = h.shape[0]
    tk = min(tk, S)
    nk = S // tk

    def body(h_ref, dq_ref, dkv_ref, dxr_ref, d3_ref, o_ref, acc):
        j, k = pl.program_id(0), pl.program_id(1)

        @pl.when(k == 0)
        def _():
            acc[...] = jnp.zeros_like(acc)

        @pl.when(j == 0)
        def _():
            acc[...] += _dot_tn(h_ref[...], dq_ref[...])

        @pl.when((j >= 1) & (j < 3))
        def _():
            acc[...] += _dot_tn(h_ref[...], dkv_ref[...])

        @pl.when((j >= 3) & (j < 5))
        def _():
            acc[...] += _dot_tn(h_ref[...], dxr_ref[...])

        @pl.when(j >= 5)
        def _():
            acc[...] += _dot_tn(h_ref[...], d3_ref[...])

        @pl.when(k == nk - 1)
        def _():
            o_ref[...] = acc[...].astype(BF16)

    use = lambda j, k, lo, hi: jnp.where((j >= lo) & (j < hi), k, 0)
    g3 = lambda j: jnp.clip(j - 5, 0, 5)
    return pl.pallas_call(
        body, name=name, grid=(PW // tn, nk),
        in_specs=[pl.BlockSpec((tk, D), lambda j, k: (k, 0)),
                  pl.BlockSpec((None, tk, tn), lambda j, k: (0, use(j, k, 0, 1), 0)),
                  pl.BlockSpec((None, tk, tn), lambda j, k: (jnp.clip(j - 1, 0, 1), use(j, k, 1, 3), 0)),
                  pl.BlockSpec((tk, tn), lambda j, k: (use(j, k, 3, 5), jnp.clip(j - 3, 0, 1))),
                  pl.BlockSpec((None, tk, tn), lambda j, k: (g3(j) // 2, use(j, k, 5, 11), g3(j) % 2))],
        out_specs=pl.BlockSpec((D, tn), lambda j, k: (0, j)),
        out_shape=jax.ShapeDtypeStruct((D, PW), BF16),
        scratch_shapes=[pltpu.VMEM((D, tn), F32)],
        compiler_params=_cp("parallel", "arbitrary"),
    )(h, dq, dkv, dxr, d3)


def ada_fwd(c_all, w_ada, b_ada, name, tn=768):
    n = w_ada.shape[1]

    def body(c_ref, w_ref, b_ref, o_ref):
        cv = c_ref[...]
        ca = (cv * _sigmoid(cv)).astype(BF16)
        o_ref[...] = _dot(ca, w_ref[...].astype(BF16)) + b_ref[...]

    return pl.pallas_call(
        body, name=name, grid=(n // tn,),
        in_specs=[pl.BlockSpec((8, D), lambda j: (0, 0)), pl.BlockSpec((D, tn), lambda j: (0, j)),
                  pl.BlockSpec((1, tn), lambda j: (0, j))],
        out_specs=pl.BlockSpec((8, tn), lambda j: (0, j)),
        out_shape=jax.ShapeDtypeStruct((8, n), F32),
        compiler_params=_cp("parallel"),
    )(c_all, w_ada, b_ada)


def ada_bwd(c_all_t, dmod, name, tn=768):
    n = dmod.shape[1]

    def body(c_ref, d_ref, o_ref):
        cv = c_ref[...]
        ca = (cv * _sigmoid(cv)).astype(BF16)
        o_ref[...] = _dot(ca, d_ref[...].astype(BF16))

    return pl.pallas_call(
        body, name=name, grid=(n // tn,),
        in_specs=[pl.BlockSpec((D, 128), lambda j: (0, 0)), pl.BlockSpec((128, tn), lambda j: (0, j))],
        out_specs=pl.BlockSpec((D, tn), lambda j: (0, j)),
        out_shape=jax.ShapeDtypeStruct((D, n), F32),
        compiler_params=_cp("parallel"),
    )(c_all_t, dmod)


def _row_tile(rows, cols, itemsize=4, budget=1536 * 1024):
    best = None
    for t in range(8, rows + 1, 8):
        if rows % t == 0 and t * cols * itemsize <= budget:
            best = t
    return rows if best is None else best


def sum_lead(parts, name, out_dtype=F32):
    n, R, C = parts.shape
    tr = _row_tile(R, C * n)

    def body(p_ref, o_ref):
        acc = p_ref[0].astype(F32)
        for k in range(1, n):
            acc = acc + p_ref[k].astype(F32)
        o_ref[...] = acc.astype(out_dtype)

    return pl.pallas_call(
        body, name=name, grid=(R // tr,),
        in_specs=[pl.BlockSpec((n, tr, C), lambda i: (0, i, 0))],
        out_specs=pl.BlockSpec((tr, C), lambda i: (i, 0)),
        out_shape=jax.ShapeDtypeStruct((R, C), out_dtype),
        compiler_params=_cp("parallel"),
    )(parts)


def adamw(w, g, m, v, name, emit_g=False):
    R, C = w.shape
    tr = _row_tile(R, C * 8, budget=8 * 1024 * 1024)

    def body(w_ref, g_ref, m_ref, v_ref, d_ref, mo_ref, vo_ref, *go_ref):
        gv = g_ref[...]
        if emit_g:
            go_ref[0][...] = gv
        mn = ADAM_B1 * m_ref[...] + (1.0 - ADAM_B1) * gv
        vn = ADAM_B2 * v_ref[...] + (1.0 - ADAM_B2) * (gv * gv)
        m_hat = mn / (1.0 - ADAM_B1 ** ADAM_STEP)
        v_hat = vn / (1.0 - ADAM_B2 ** ADAM_STEP)
        d_ref[...] = -ADAM_LR * (m_hat / (jnp.sqrt(v_hat) + ADAM_EPS) + ADAM_WD * w_ref[...])
        mo_ref[...] = mn
        vo_ref[...] = vn

    spec = pl.BlockSpec((tr, C), lambda i: (i, 0))
    return pl.pallas_call(
        body, name=name, grid=(R // tr,),
        in_specs=[spec] * 4, out_specs=[spec] * (4 if emit_g else 3),
        out_shape=[jax.ShapeDtypeStruct((R, C), F32)] * (4 if emit_g else 3),
        compiler_params=_cp("parallel"),
    )(w, g, m, v)


def _mesh_pos():
    return lax.axis_index("x"), lax.axis_index("y"), lax.axis_index("c")


def _other_chips(mx, my):
    return [(1 - mx, my), (mx, 1 - my), (1 - mx, 1 - my)]


def ag_small(x, name):
    R = x.shape[0]

    def body(x_ref, out_ref, send_sems, recv_sems, local_sem):
        mx, my, mc = _mesh_pos()
        me, sibling = (mx, my, mc), (mx, my, 1 - mc)
        chips = _other_chips(mx, my)

        def slot(px, py, pc):
            return out_ref.at[4 * px + 2 * py + pc]

        def copy(k, block, to, src=None):
            return pltpu.make_async_remote_copy(
                src_ref=slot(*block) if src is None else src, dst_ref=slot(*block),
                send_sem=send_sems.at[k], recv_sem=recv_sems.at[k], device_id=to, device_id_type=MESH)

        mine = pltpu.make_async_copy(x_ref, slot(*me), local_sem)
        mine.start()
        first = [copy(0, me, sibling, src=x_ref)]
        first += [copy(1 + j, me, (*chip, mc), src=x_ref) for j, chip in enumerate(chips)]
        for cp in first:
            cp.start()
        passed = [copy(4 + j, (*chip, mc), sibling) for j, chip in enumerate(chips)]
        for j, chip in enumerate(chips):
            copy(1 + j, (*chip, mc), me).wait_recv()
            passed[j].start()
        copy(0, sibling, me).wait_recv()
        for j, chip in enumerate(chips):
            copy(4 + j, (*chip, 1 - mc), me).wait_recv()
        for cp in first + passed:
            cp.wait_send()
        mine.wait()

    return pl.pallas_call(
        body, name=name,
        out_shape=jax.ShapeDtypeStruct((N_DEV, R, 128), F32),
        in_specs=[pl.BlockSpec(memory_space=pltpu.VMEM)],
        out_specs=pl.BlockSpec(memory_space=pltpu.VMEM),
        scratch_shapes=[pltpu.SemaphoreType.DMA((7,)), pltpu.SemaphoreType.DMA((7,)), pltpu.SemaphoreType.DMA],
        compiler_params=pltpu.CompilerParams(vmem_limit_bytes=VMEM_LIMIT),
    )(x)


BIG = (("ffn1_w_gu", "col", D, PW), ("ffn1_w_down", "row", FF, D), ("w_in", "col", D, PW),
       ("w_att_o", "col", 512, D), ("w_rec_o", "row", D, D), ("w_out", "row", D, D),
       ("ffn2_w_gu", "col", D, PW), ("ffn2_w_down", "row", FF, D))
NBIG = len(BIG)


def _shard_shape(kind, R, C):
    return (R, C // 4) if kind == "col" else (R // 4, C)


def _region(ref, kind, R, C, q, half, t, tr):
    sr, sc = _shard_shape(kind, R, C)
    if kind == "col":
        return ref.at[pl.ds(pl.multiple_of(half * (R // 2) + t * tr, 16), tr), pl.ds(q * sc, sc)]
    return ref.at[pl.ds(pl.multiple_of(q * sr + t * tr, 16), tr), pl.ds(half * (C // 2), C // 2)]


def ag_local(w, kind, R, C, p_arr, name, after=()):
    sr, sc = _shard_shape(kind, R, C)
    tr = _row_tile(sr, sc, budget=2 * 1024 * 1024)
    nt = sr // tr
    after = list(after)

    def body(p_ref, w_ref, *rest):
        rest[-1][...] = w_ref[...].astype(BF16)

    if kind == "col":
        o_spec = pl.BlockSpec((tr, sc), lambda i, p: (i, p[0]))
    else:
        o_spec = pl.BlockSpec((tr, sc), lambda i, p: (p[0] * nt + i, 0))
    return pl.pallas_call(
        body, name=name,
        grid_spec=pltpu.PrefetchScalarGridSpec(
            num_scalar_prefetch=1, grid=(nt,),
            in_specs=[pl.BlockSpec((tr, sc), lambda i, p: (i, 0))] + [ANY] * len(after), out_specs=o_spec),
        out_shape=jax.ShapeDtypeStruct((R, C), BF16),
        compiler_params=_cp("parallel"),
    )(p_arr, w, *after)


HBM_SPEC = pl.BlockSpec(memory_space=pltpu.HBM)
SEM_SPEC = pl.BlockSpec(memory_space=pltpu.SEMAPHORE)


def _ag_sems(geoms):
    return sum(6 if both else 3 for (_, _, _, both) in geoms)


def _ag_copies(fulls, geoms, ssem, rsem, mx, my, mc, q, h):
    chips = _other_chips(mx, my)
    out, base = [], 0
    for w, (kind, R, C, both) in enumerate(geoms):
        sr, sc = _shard_shape(kind, R, C)
        hr = sr // 2 if kind == "col" else sr
        reg = _region(fulls[w], kind, R, C, q, h, 0, hr)
        out.append([pltpu.make_async_remote_copy(
            src_ref=reg, dst_ref=reg, send_sem=ssem.at[base + 3 * t + k], recv_sem=rsem.at[base + 3 * t + k],
            device_id=(*chips[k], mc if t == 0 else 1 - mc), device_id_type=MESH)
            for t in range(2 if both else 1) for k in range(3)])
        base += 6 if both else 3
    return out


def ag_start(fulls, geoms, after, name):
    n = len(fulls)
    after = list(after)
    m = len(after)

    def body(*refs):
        ssem, rsem = refs[n + m:n + m + 2]
        outs, token = refs[n + m + 2:2 * n + m + 2], refs[2 * n + m + 2]
        mx, my, mc = _mesh_pos()
        p = 2 * mx + my
        col = [w for w, g in enumerate(geoms) if g[0] == "col"]
        row = [w for w, g in enumerate(geoms) if g[0] == "row"]
        for q in range(4):
            @pl.when(p == q)
            def _(q=q):
                cps = _ag_copies(outs, geoms, ssem, rsem, mx, my, mc, q, mc)
                for w in col:
                    for cp in cps[w]:
                        cp.start()
        for h in range(2):
            @pl.when(mc == h)
            def _(h=h):
                cps = _ag_copies(outs, geoms, ssem, rsem, mx, my, mc, p, h)
                for w in row:
                    for cp in cps[w]:
                        cp.start()
        token[...] = jnp.zeros_like(token)

    res = pl.pallas_call(
        body, name=name,
        out_shape=[pltpu.SemaphoreType.DMA((_ag_sems(geoms),)), pltpu.SemaphoreType.DMA((_ag_sems(geoms),))]
        + [pltpu.HBM(a.shape, a.dtype) for a in fulls] + [jax.ShapeDtypeStruct((8, 128), F32)],
        in_specs=[HBM_SPEC] * n + [ANY] * m,
        out_specs=[SEM_SPEC, SEM_SPEC] + [HBM_SPEC] * n + [pl.BlockSpec(memory_space=pltpu.VMEM)],
        input_output_aliases={w: 2 + w for w in range(n)},
        compiler_params=pltpu.CompilerParams(has_side_effects=pltpu.SideEffectType.DATAFLOW_SIDE_EFFECTING),
    )(*[pltpu.with_memory_space_constraint(a, pltpu.HBM) for a in fulls], *after)
    return res[0], res[1], list(res[2:2 + n]), res[2 + n]


def ag_wait(fulls, geoms, ssem, rsem, after, name):
    n = len(fulls)
    after = list(after) if isinstance(after, (list, tuple)) else [after]

    def body(*refs):
        ins, ssem_ref, rsem_ref = refs[:n], refs[n], refs[n + 1]
        mx, my, mc = _mesh_pos()
        for cps in _ag_copies(ins, geoms, ssem_ref, rsem_ref, mx, my, mc, 0, 0):
            for cp in cps:
                cp.wait_send()
                cp.wait_recv()

    return list(pl.pallas_call(
        body, name=name,
        out_shape=[pltpu.HBM(a.shape, a.dtype) for a in fulls],
        in_specs=[HBM_SPEC] * n + [SEM_SPEC, SEM_SPEC] + [ANY] * len(after),
        out_specs=[HBM_SPEC] * n,
        input_output_aliases={w: w for w in range(n)},
        compiler_params=pltpu.CompilerParams(has_side_effects=pltpu.SideEffectType.DATAFLOW_SIDE_EFFECTING),
    )(*fulls, ssem, rsem, *after))


def ag_forward(full, kind, R, C, name):
    sr, sc = _shard_shape(kind, R, C)
    hr, hc = (sr // 2, sc) if kind == "col" else (sr, sc // 2)
    tr = _row_tile(hr, hc, itemsize=2, budget=512 * 1024)
    nt = hr // tr

    total = 3 * nt

    def body(src_ref, full_ref, stage, lsem, ssem, rsem):
        step = pl.program_id(0) * nt + pl.program_id(1)
        par = step % 2
        mx, my, mc = _mesh_pos()

        def load(s, q, h, t):
            return pltpu.make_async_copy(_region(src_ref, kind, R, C, q, h, t, tr), stage.at[s], lsem.at[s])

        def push(s, q, h, t):
            return pltpu.make_async_remote_copy(src_ref=stage.at[s], dst_ref=_region(full_ref, kind, R, C, q, h, t, tr),
                                                send_sem=ssem.at[s], recv_sem=rsem, device_id=(mx, my, 1 - mc),
                                                device_id_type=MESH)

        def for_tile(stp, fn):
            q_k = _partner_chip(stp // nt, 2 * mx + my)
            if kind == "col":
                for q in range(4):
                    @pl.when(q_k == q)
                    def _(q=q):
                        fn(q, mc, stp % nt)
            else:
                for h in range(2):
                    @pl.when(mc == h)
                    def _(h=h):
                        fn(q_k, h, stp % nt)

        @pl.when(step == 0)
        def _():
            for_tile(step, lambda q, h, t: load(0, q, h, t).start())

        load(par, 0, 0, 0).wait()
        for_tile(step, lambda q, h, t: push(par, q, h, t).start())

        @pl.when(step + 1 < total)
        def _():
            @pl.when(step >= 1)
            def _():
                push(1 - par, 0, 0, 0).wait_send()
            for_tile(step + 1, lambda q, h, t: load(1 - par, q, h, t).start())

        @pl.when(step == total - 1)
        def _():
            push(par, 0, 0, 0).wait_send()
            push(1 - par, 0, 0, 0).wait_send()
            three = full_ref.at[pl.ds(0, hr), pl.ds(0, 3 * hc)] if kind == "col" else full_ref.at[pl.ds(0, 3 * hr), pl.ds(0, hc)]
            pltpu.make_async_remote_copy(src_ref=three, dst_ref=three, send_sem=ssem.at[0], recv_sem=rsem,
                                         device_id=(mx, my, 1 - mc), device_id_type=MESH).wait_recv()

    return pl.pallas_call(
        body, name=name, grid=(3, nt),
        in_specs=[ANY], out_specs=ANY,
        out_shape=jax.ShapeDtypeStruct((R, C), BF16),
        scratch_shapes=[pltpu.VMEM((2, tr, hc), BF16), pltpu.SemaphoreType.DMA((2,)), pltpu.SemaphoreType.DMA((2,)),
                        pltpu.SemaphoreType.DMA],
        input_output_aliases={0: 0},
        compiler_params=_cp("arbitrary", "arbitrary"),
    )(full)


def _half_shape(kind, R, C):
    return (R // 2, C) if kind == "col" else (R, C // 2)


def _piece_shape(kind, R, C):
    return (R // 2, C // 4) if kind == "col" else (R // 4, C // 2)


def pair_push(g, kind, c_arr, name):
    R, C = g.shape
    hr, hc = _half_shape(kind, R, C)
    tr = _row_tile(hr, hc, itemsize=2, budget=1024 * 1024)
    nt = hr // tr

    def body(c_ref, g_ref, out_ref, stage, ssem, rsem):
        i = pl.program_id(0)
        slot = i % 2
        mx, my, mc = _mesh_pos()

        def push(s, t):
            return pltpu.make_async_remote_copy(
                src_ref=stage.at[s], dst_ref=out_ref.at[pl.ds(pl.multiple_of(t * tr, 16), tr)],
                send_sem=ssem.at[s], recv_sem=rsem, device_id=(mx, my, 1 - mc), device_id_type=MESH)

        @pl.when(i >= 2)
        def _():
            push(slot, 0).wait_send()

        stage[slot] = g_ref[...]
        push(slot, i).start()

        @pl.when(i == nt - 1)
        def _():
            push(slot, 0).wait_send()
            if nt >= 2:
                push(1 - slot, 0).wait_send()
            pltpu.make_async_remote_copy(src_ref=out_ref, dst_ref=out_ref, send_sem=ssem.at[0], recv_sem=rsem,
                                         device_id=(mx, my, 1 - mc), device_id_type=MESH).wait_recv()

    if kind == "col":
        g_spec = pl.BlockSpec((tr, hc), lambda i, c: ((1 - c[0]) * nt + i, 0))
    else:
        g_spec = pl.BlockSpec((tr, hc), lambda i, c: (i, 1 - c[0]))
    return pl.pallas_call(
        body, name=name,
        grid_spec=pltpu.PrefetchScalarGridSpec(
            num_scalar_prefetch=1, grid=(nt,), in_specs=[g_spec], out_specs=ANY,
            scratch_shapes=[pltpu.VMEM((2, tr, hc), BF16), pltpu.SemaphoreType.DMA((2,)), pltpu.SemaphoreType.DMA]),
        out_shape=jax.ShapeDtypeStruct((hr, hc), BF16),
        compiler_params=_cp("arbitrary"),
    )(c_arr, g)


def _partner_chip(k, p):
    return p ^ jnp.where(k == 0, 2, jnp.where(k == 1, 1, jnp.where(k == 2, 3, 0)))


def pair_add(g, got, kind, cp_arr, name):
    R, C = g.shape
    pr, pc = _piece_shape(kind, R, C)
    tr = _row_tile(pr, pc, itemsize=2, budget=1024 * 1024)
    nt = pr // tr

    def body(cp_ref, g_ref, got_ref, ps_ref, rb_ref):
        tile = (g_ref[...].astype(F32) + got_ref[...].astype(F32)).astype(BF16)
        ps_ref[...] = tile

        @pl.when(pl.program_id(1) == cp_ref[1])
        def _():
            rb_ref[...] = tile

    if kind == "col":
        g_spec = pl.BlockSpec((tr, pc), lambda i, q, cp: (cp[0] * nt + i, q))
        got_spec = pl.BlockSpec((tr, pc), lambda i, q, cp: (i, q))
    else:
        g_spec = pl.BlockSpec((tr, pc), lambda i, q, cp: (q * nt + i, cp[0]))
        got_spec = pl.BlockSpec((tr, pc), lambda i, q, cp: (q * nt + i, 0))
    return pl.pallas_call(
        body, name=name,
        grid_spec=pltpu.PrefetchScalarGridSpec(
            num_scalar_prefetch=1, grid=(nt, 4), in_specs=[g_spec, got_spec],
            out_specs=[pl.BlockSpec((None, tr, pc), lambda i, q, cp: (q, i, 0)),
                       pl.BlockSpec((None, tr, pc), lambda i, q, cp: (cp[1], i, 0))]),
        out_shape=[jax.ShapeDtypeStruct((4, pr, pc), BF16)] * 2,
        compiler_params=_cp("arbitrary", "arbitrary"),
    )(cp_arr, g, got)


def _rs_copies(ps, rb, ssem, rsem, mx, my, mc):
    p = 2 * mx + my
    out = []
    for w in range(len(ps)):
        for k, chip in enumerate(_other_chips(mx, my)):
            out.append(pltpu.make_async_remote_copy(
                src_ref=ps[w].at[2 * chip[0] + chip[1]], dst_ref=rb[w].at[p], send_sem=ssem.at[3 * w + k],
                recv_sem=rsem.at[3 * w + k], device_id=(*chip, mc), device_id_type=MESH))
    return out


def rs_start(ps, rb, after, name):
    n = len(ps)
    after = list(after)
    m = len(after)

    def body(*refs):
        ssem, rsem = refs[2 * n + m:2 * n + m + 2]
        ps_o = refs[2 * n + m + 2:3 * n + m + 2]
        rb_o = refs[3 * n + m + 2:4 * n + m + 2]
        token = refs[4 * n + m + 2]
        for cp in _rs_copies(ps_o, rb_o, ssem, rsem, *_mesh_pos()):
            cp.start()
        token[...] = jnp.zeros_like(token)

    both = list(ps) + list(rb)
    res = pl.pallas_call(
        body, name=name,
        out_shape=[pltpu.SemaphoreType.DMA((3 * n,)), pltpu.SemaphoreType.DMA((3 * n,))]
        + [pltpu.HBM(a.shape, a.dtype) for a in both] + [jax.ShapeDtypeStruct((8, 128), F32)],
        in_specs=[HBM_SPEC] * (2 * n) + [ANY] * m,
        out_specs=[SEM_SPEC, SEM_SPEC] + [HBM_SPEC] * (2 * n) + [pl.BlockSpec(memory_space=pltpu.VMEM)],
        input_output_aliases={w: 2 + w for w in range(2 * n)},
        compiler_params=pltpu.CompilerParams(has_side_effects=pltpu.SideEffectType.DATAFLOW_SIDE_EFFECTING),
    )(*[pltpu.with_memory_space_constraint(a, pltpu.HBM) for a in both], *after)
    return res[0], res[1], list(res[2:2 + n]), list(res[2 + n:2 + 2 * n]), res[2 + 2 * n]


def rs_wait(ps, rb, ssem, rsem, after, name):
    n = len(ps)
    after = list(after)
    m = len(after)

    def body(*refs):
        ps_i, rb_i = refs[:n], refs[n:2 * n]
        ssem_ref, rsem_ref = refs[2 * n], refs[2 * n + 1]
        for cp in _rs_copies(ps_i, rb_i, ssem_ref, rsem_ref, *_mesh_pos()):
            cp.wait_send()
            cp.wait_recv()

    both = list(ps) + list(rb)
    res = pl.pallas_call(
        body, name=name,
        out_shape=[pltpu.HBM(a.shape, a.dtype) for a in both],
        in_specs=[HBM_SPEC] * (2 * n) + [SEM_SPEC, SEM_SPEC] + [ANY] * m,
        out_specs=[HBM_SPEC] * (2 * n),
        input_output_aliases={w: w for w in range(2 * n)},
        compiler_params=pltpu.CompilerParams(has_side_effects=pltpu.SideEffectType.DATAFLOW_SIDE_EFFECTING),
    )(*both, ssem, rsem, *after)
    return list(res[n:])


def sum_share(parts, kind, R, C, name):
    _, pr, pc = parts.shape
    sr, sc = _shard_shape(kind, R, C)
    tr = _row_tile(pr, pc * 4, budget=4 * 1024 * 1024)
    nt = pr // tr

    def body(p_ref, fin_ref, stage, lsem, ssem, rsem):
        i = pl.program_id(0)
        slot = i % 2
        mx, my, mc = _mesh_pos()

        def region(h, t):
            r0 = pl.multiple_of(t * tr, 8)
            if kind == "col":
                return fin_ref.at[pl.ds(pl.multiple_of(h * pr + r0, 8), tr)]
            return fin_ref.at[pl.ds(r0, tr), pl.ds(h * pc, pc)]

        def copies(s, h, t):
            return (pltpu.make_async_copy(stage.at[s], region(h, t), lsem.at[s]),
                    pltpu.make_async_remote_copy(src_ref=stage.at[s], dst_ref=region(h, t), send_sem=ssem.at[s],
                                                 recv_sem=rsem, device_id=(mx, my, 1 - mc), device_id_type=MESH))

        def wait_sent(s):
            loc, rem = copies(s, 0, 0)
            loc.wait()
            rem.wait_send()

        @pl.when(i >= 2)
        def _():
            wait_sent(slot)

        acc = p_ref[0].astype(F32)
        for k in range(1, 4):
            acc = acc + p_ref[k].astype(F32)
        stage[slot] = acc
        if kind == "col":
            for cp in copies(slot, mc, i):
                cp.start()
        else:
            for h in range(2):
                @pl.when(mc == h)
                def _(h=h):
                    for cp in copies(slot, h, i):
                        cp.start()

        @pl.when(i == nt - 1)
        def _():
            wait_sent(slot)
            if nt >= 2:
                wait_sent(1 - slot)
            half = fin_ref.at[pl.ds(0, pr), pl.ds(0, pc)]
            pltpu.make_async_remote_copy(src_ref=half, dst_ref=half, send_sem=ssem.at[0], recv_sem=rsem,
                                         device_id=(mx, my, 1 - mc), device_id_type=MESH).wait_recv()

    return pl.pallas_call(
        body, name=name, grid=(nt,),
        in_specs=[pl.BlockSpec((4, tr, pc), lambda i: (0, i, 0))],
        out_specs=ANY,
        out_shape=jax.ShapeDtypeStruct((sr, sc), F32),
        scratch_shapes=[pltpu.VMEM((2, tr, pc), F32), pltpu.SemaphoreType.DMA((2,)), pltpu.SemaphoreType.DMA((2,)),
                        pltpu.SemaphoreType.DMA],
        compiler_params=_cp("arbitrary"),
    )(parts)


def _pack(parts, rows):
    flat = []
    for a in parts:
        a = jnp.ravel(a).astype(F32)
        flat.append(jnp.pad(a, (0, (-a.shape[0]) % 128)))
    v = jnp.concatenate(flat)
    return jnp.pad(v, (0, rows * 128 - v.shape[0])).reshape(rows, 128)


def _unpack(block, shapes):
    lead = block.shape[:-2]
    v = block.reshape(lead + (-1,))
    out, off = [], 0
    for shp in shapes:
        n = int(np.prod(shp))
        out.append(v[..., off:off + n].reshape(lead + tuple(shp)))
        off += n + (-n) % 128
    return out


def _block_diag4(w):
    w4 = w.reshape(4, 4, 64, 64)
    eye = jnp.eye(4, dtype=w.dtype)
    return (w4[:, :, :, None, :] * eye[None, :, None, :, None]).reshape(4, 256, 256)


def _diag_blocks(bd):
    b5 = bd.reshape(4, 4, 64, 4, 64)
    return jnp.stack([b5[:, i, :, i, :] for i in range(4)], axis=1).reshape(16, 64, 64)


def _bias_window(rel_bias):
    m = (np.arange(768) + 127) % 768 - 127
    w = rel_bias[:, np.clip(512 - m, -128, 128) + 128]
    win = jnp.tile(w, (1, 128))[:, :128 * 767].reshape(8, 128, 767)[:, :, :WIN]
    qh = np.arange(128)[:, None] // CHUNK
    kc = np.arange(WIN)[None, :] // CHUNK
    valid = (kc >= qh) & (kc <= qh + 8)
    return jnp.where(jnp.asarray(valid)[None], win, NEG)


SMALL = ("b_ada", "norm_pre", "norm_post", "rel_bias", "conv_w", "conv_b", "lru_wa", "lru_ba", "lru_wx",
         "lru_bx", "lru_lambda")
WEIGHTS = ("w_ada", "b_ada", "norm_pre", "norm_post", "ffn1_w_gu", "ffn1_w_down", "w_in", "rel_bias", "conv_w",
           "conv_b", "lru_wa", "lru_ba", "lru_wx", "lru_bx", "lru_lambda", "w_att_o", "w_rec_o", "w_out",
           "ffn2_w_gu", "ffn2_w_down")


def kernel(x, c, w_ada, b_ada, norm_pre, norm_post, ffn1_w_gu, ffn1_w_down, w_in, rel_bias, conv_w, conv_b, lru_wa, lru_ba, lru_wx, lru_bx, lru_lambda, w_att_o, w_rec_o, w_out, ffn2_w_gu, ffn2_w_down, loss_target, m_w_ada, m_b_ada, m_norm_pre, m_norm_post, m_ffn1_w_gu, m_ffn1_w_down, m_w_in, m_rel_bias, m_conv_w, m_conv_b, m_lru_wa, m_lru_ba, m_lru_wx, m_lru_bx, m_lru_lambda, m_w_att_o, m_w_rec_o, m_w_out, m_ffn2_w_gu, m_ffn2_w_down, v_w_ada, v_b_ada, v_norm_pre, v_norm_post, v_ffn1_w_gu, v_ffn1_w_down, v_w_in, v_rel_bias, v_conv_w, v_conv_b, v_lru_wa, v_lru_ba, v_lru_wx, v_lru_bx, v_lru_lambda, v_w_att_o, v_w_rec_o, v_w_out, v_ffn2_w_gu, v_ffn2_w_down):
    W = dict(w_ada=w_ada, b_ada=b_ada, norm_pre=norm_pre, norm_post=norm_post, ffn1_w_gu=ffn1_w_gu,
             ffn1_w_down=ffn1_w_down, w_in=w_in, rel_bias=rel_bias, conv_w=conv_w, conv_b=conv_b, lru_wa=lru_wa,
             lru_ba=lru_ba, lru_wx=lru_wx, lru_bx=lru_bx, lru_lambda=lru_lambda, w_att_o=w_att_o, w_rec_o=w_rec_o,
             w_out=w_out, ffn2_w_gu=ffn2_w_gu, ffn2_w_down=ffn2_w_down)
    M = dict(w_ada=m_w_ada, b_ada=m_b_ada, norm_pre=m_norm_pre, norm_post=m_norm_post, ffn1_w_gu=m_ffn1_w_gu,
             ffn1_w_down=m_ffn1_w_down, w_in=m_w_in, rel_bias=m_rel_bias, conv_w=m_conv_w, conv_b=m_conv_b,
             lru_wa=m_lru_wa, lru_ba=m_lru_ba, lru_wx=m_lru_wx, lru_bx=m_lru_bx, lru_lambda=m_lru_lambda,
             w_att_o=m_w_att_o, w_rec_o=m_w_rec_o, w_out=m_w_out, ffn2_w_gu=m_ffn2_w_gu, ffn2_w_down=m_ffn2_w_down)
    V = dict(w_ada=v_w_ada, b_ada=v_b_ada, norm_pre=v_norm_pre, norm_post=v_norm_post, ffn1_w_gu=v_ffn1_w_gu,
             ffn1_w_down=v_ffn1_w_down, w_in=v_w_in, rel_bias=v_rel_bias, conv_w=v_conv_w, conv_b=v_conv_b,
             lru_wa=v_lru_wa, lru_ba=v_lru_ba, lru_wx=v_lru_wx, lru_bx=v_lru_bx, lru_lambda=v_lru_lambda,
             w_att_o=v_w_att_o, w_rec_o=v_w_rec_o, w_out=v_w_out, ffn2_w_gu=v_ffn2_w_gu, ffn2_w_down=v_ffn2_w_down)
    mx, my, mc = _mesh_pos()
    p = 2 * mx + my
    e = 4 * mx + 2 * my + mc
    xs = x[0]

    c_arr = jnp.reshape(mc, (1,)).astype(jnp.int32)
    cp_arr = jnp.stack([mc, p]).astype(jnp.int32)
    p_arr = jnp.reshape(p, (1,)).astype(jnp.int32)
    direct = ("w_att_o", "w_rec_o", "w_out", "ffn2_w_gu", "ffn2_w_down")
    geoms = [(kind, R, C, n in direct) for (n, kind, R, C) in BIG]
    names = [b[0] for b in BIG]
    placed = [ag_local(W[n][0], kind, R, C, p_arr, "ag_local_" + n) for (n, kind, R, C) in BIG[:2]]

    def arrived(fly, lo, hi, ssem, rsem, after, tag):
        done = ag_wait(fly, geoms[lo:hi], ssem, rsem, after, "ag_wait_" + tag)
        return [a if both else ag_forward(a, kind, R, C, "ag_forward_" + n)
                for a, (kind, R, C, both), n in zip(done, geoms[lo:hi], names[lo:hi])]

    g1 = ag_small(_pack([c, norm_pre, norm_post, conv_w], 32), "ag_small_params")
    c_all, npre4, npost4, cw4 = _unpack(g1, [(D,), (3, 256), (3, 256), (4, 256)])
    chipwise = lambda a: jnp.moveaxis(a[0::2], 0, 1).reshape(a.shape[1], D)
    npre, npost, conv_full = chipwise(npre4), chipwise(npost4), chipwise(cw4)

    b_cols = lax.dynamic_slice(b_ada, (0, p * 2304), (1, 2304))
    mod_cols = ada_fwd(c_all, w_ada[0], b_cols, "ada_fwd")
    g2 = ag_small(mod_cols.reshape(144, 128), "ag_mod")
    mod_all = jnp.moveaxis(g2[0::2].reshape(4, 8, 2304), 0, 1).reshape(8, 9 * D)
    mod = lax.dynamic_index_in_dim(mod_all, e, 0, keepdims=False).reshape(3, 3, D)
    zeros3 = jnp.zeros((3, D), F32)
    vecs = [jnp.concatenate([npre[k:k + 1], npost[k:k + 1], mod[k], zeros3], axis=0) for k in range(3)]

    gu_s, gu_r, gu_fly, tok_gu = ag_start(placed[:1], geoms[:1], [g2], "ag_start_ffn1_gu")
    dn_s, dn_r, dn_fly, tok0 = ag_start(placed[1:2], geoms[1:2], [tok_gu], "ag_start_ffn1_down")
    placed += [ag_local(W[n][0], kind, R, C, p_arr, "ag_local_" + n, after=[tok0]) for (n, kind, R, C) in BIG[2:]]
    f1_gu, = arrived(gu_fly, 0, 1, gu_s, gu_r, placed[2:], "ffn1_gu")
    f1_dn, = arrived(dn_fly, 1, 2, dn_s, dn_r, f1_gu, "ffn1_down")
    mix_s, mix_r, mix_fly, tok1 = ag_start(placed[2:6], geoms[2:6], [f1_gu, f1_dn], "ag_start_mixer")
    ffn_s, ffn_r, ffn_fly, tok2 = ag_start(placed[6:], geoms[6:], [tok1], "ag_start_ffn2")
    wa_bd = _block_diag4(lru_wa[0]).astype(BF16)
    wx_bd = _block_diag4(lru_wx[0]).astype(BF16)
    pvec = jnp.concatenate([conv_full, conv_b, lru_ba, lru_bx, lru_lambda], axis=0)
    bias = _bias_window(rel_bias[0]).reshape(4, 256, WIN)

    f1_u = f1_gu[:, FF:]
    x1, h1, g1_, u1, a1, f1 = ffn_fwd(xs, vecs[0] + tok2[0:1, 0:1], f1_gu, f1_u, f1_dn, 0.5, "ffn1_fwd")
    win, wao, wro, wout = arrived(mix_fly, 2, 6, mix_s, mix_r, x1, "mixer")
    h2, qkv, rest = proj_fwd(x1, vecs[1], win, "proj_fwd")
    ao = attn_fwd(qkv, bias, "attn_fwd")
    hl, hg = lru_fwd(rest, pvec, wa_bd, wx_bd, "lru_fwd")
    x2, att, rec, mg, f2 = mix_out_fwd(x1, ao, hg, rest, vecs[1], wao, wro, wout, "mix_out_fwd")
    f2_gu, f2_dn = arrived(ffn_fly, 6, 8, ffn_s, ffn_r, x2, "ffn2")
    f2_u = f2_gu[:, FF:]
    dy, h3, g3_, u3, a3, f3, lvec = ffn_fwd(x2, vecs[2], f2_gu, f2_u, f2_dn, 0.5, "ffn2_fwd", tgt=loss_target[0])

    G, grads = {}, {}
    geo = {n: (kind, R, C) for (n, kind, R, C) in BIG}

    def reduce_begin(names, tag):
        ps, rb = [], []
        for n in names:
            got = pair_push(G[n], geo[n][0], c_arr, "rs_push_" + n)
            a, b = pair_add(G[n], got, geo[n][0], cp_arr, "rs_pair_sum_" + n)
            ps.append(a)
            rb.append(b)
        return rs_start(ps, rb, [], "rs_start_" + tag)

    def reduce_end(names, flight, after, tag):
        ssem, rsem, ps, rb, _ = flight
        for a, n in zip(rs_wait(ps, rb, ssem, rsem, after, "rs_wait_" + tag), names):
            grads[n] = sum_share(a, *geo[n], "rs_sum_share_" + n)[None]

    dx2, df3, dgu3, va2 = ffn_bwd(dy, x2, f3, g3_, u3, vecs[2], f2_gu, f2_u, f2_dn, 0.5, "ffn2_bwd")
    G["ffn2_w_gu"] = mm_tn(h3, dgu3, "dw_ffn2_gu", D, 1408, 2048)
    G["ffn2_w_down"] = mm_tn(a3, df3, "dw_ffn2_down", 1408, D, 2048)
    fly_ffn2 = reduce_begin(("ffn2_w_gu", "ffn2_w_down"), "ffn2")
    vec1 = vecs[1] + fly_ffn2[4][0:1, 0:1]
    df2, d_att, d_rec, dao, dhl, d3, va_out = mix_out_bwd(dx2, f2, att, rec, rest, hl, vec1, wao, wro, wout,
                                                          "mix_out_bwd")
    G["w_out"] = mm_tn(mg, df2, "dw_out", D, D, 1024)
    G["w_att_o"] = mm_tn(ao, d_att, "dw_att_o", 512, D, 1024)
    G["w_rec_o"] = mm_tn(hg, d_rec, "dw_rec_o", D, D, 1024)
    dq, db, dkv = attn_bwd(qkv, dao, bias, "attn_bwd")
    dxr, v_lru, dwa_bd, dwx_bd = lru_bwd(dhl, hl, rest, pvec, wa_bd, wx_bd, "lru_bwd")
    dx1, va_in = proj_bwd(dq, dkv, dxr, d3, win, x1, dx2, vecs[1], "proj_bwd")
    G["w_in"] = dw_in(h2, dq, dkv, dxr, d3, "dw_in")
    fly_mix = reduce_begin(("w_in", "w_att_o", "w_rec_o", "w_out"), "mixer")
    vec0 = vecs[0] + fly_mix[4][0:1, 0:1]
    dx0, df1, dgu1, va0 = ffn_bwd(dx1, xs, f1, g1_, u1, vec0, f1_gu, f1_u, f1_dn, 0.5, "ffn1_bwd")
    G["ffn1_w_gu"] = mm_tn(h1, dgu1, "dw_ffn1_gu", D, 1408, 2048)
    G["ffn1_w_down"] = mm_tn(a1, df1, "dw_ffn1_down", 1408, D, 2048)
    fly_ffn1 = reduce_begin(("ffn1_w_gu", "ffn1_w_down"), "ffn1")
    reduce_end(("ffn2_w_gu", "ffn2_w_down"), fly_ffn2, [fly_ffn1[4]], "ffn2")
    reduce_end(("w_in", "w_att_o", "w_rec_o", "w_out"), fly_mix, [fly_ffn1[4], grads["ffn2_w_down"]], "mixer")

    va1 = va_out + va_in
    vas = (va0, va1, va2)
    dmod = jnp.stack([v[2:5] for v in vas])
    part = {"b_ada": dmod, "norm_pre": jnp.stack([v[0] for v in vas]), "norm_post": jnp.stack([v[1] for v in vas]),
            "rel_bias": bias_grad(db.reshape(8, 128, WIN), "bias_grad")[:, :257], "conv_w": v_lru[0:4], "conv_b": v_lru[4],
            "lru_wa": _diag_blocks(dwa_bd), "lru_ba": v_lru[5], "lru_wx": _diag_blocks(dwx_bd), "lru_bx": v_lru[6],
            "lru_lambda": v_lru[7]}
    full_shapes = {"b_ada": (9 * D,), "norm_pre": (3, D), "norm_post": (3, D), "rel_bias": (8, 257),
                   "conv_w": (4, D), "conv_b": (D,), "lru_wa": (16, 64, 64), "lru_ba": (D,),
                   "lru_wx": (16, 64, 64), "lru_bx": (D,), "lru_lambda": (D,)}
    g3 = ag_small(_pack([part[n] for n in SMALL] + [lvec[0:1, 0:1]], 1232), "ag_small_grads")
    summed = _unpack(sum_lead(g3, "sum_small_grads"), [full_shapes[n] for n in SMALL] + [(1,)])
    red = dict(zip(SMALL, summed[:-1]))
    loss = summed[-1][0]
    cols = lambda a: lax.dynamic_slice(a, (0, p * 256), (a.shape[0], 256))
    grads.update({"b_ada": red["b_ada"][None], "norm_pre": cols(red["norm_pre"])[None],
                  "norm_post": cols(red["norm_post"])[None], "rel_bias": red["rel_bias"][None],
                  "conv_w": cols(red["conv_w"])[None], "conv_b": red["conv_b"][None], "lru_wa": red["lru_wa"][None],
                  "lru_ba": red["lru_ba"][None], "lru_wx": red["lru_wx"][None], "lru_bx": red["lru_bx"][None],
                  "lru_lambda": red["lru_lambda"][None]})

    dmod_all = g3[:, :72].reshape(8, 9 * D)
    dmod_cols = jnp.pad(lax.dynamic_slice(dmod_all, (0, p * 2304), (8, 2304)), ((0, 120), (0, 0)))
    c_all_t = jnp.pad(c_all.T, ((0, 0), (0, 120)))
    grads["w_ada"] = ada_bwd(c_all_t, dmod_cols, "ada_bwd")[None]

    delta, new_m, new_v = {}, {}, {}

    def update(n):
        shp = W[n].shape
        res = adamw(W[n][0], grads[n][0], M[n][0], V[n][0], "adamw_" + n, emit_g=n in geo)
        delta[n], new_m[n], new_v[n] = [a.reshape(shp) for a in res[:3]]
        if n in geo:
            grads[n] = res[3].reshape(shp)

    for n in ("w_ada", "ffn2_w_gu", "ffn2_w_down", "w_in", "w_att_o", "w_rec_o", "w_out"):
        update(n)
    packed = [_pack([src[n] for n in SMALL], 1168) for src in (W, grads, M, V)]
    outs = adamw(*packed, "adamw_small")
    for dst, blk in zip((delta, new_m, new_v), outs):
        for n, a in zip(SMALL, _unpack(blk, [W[n].shape for n in SMALL])):
            dst[n] = a
    reduce_end(("ffn1_w_gu", "ffn1_w_down"), fly_ffn1,
               [outs[0], delta["w_ada"], delta["ffn2_w_gu"], delta["ffn2_w_down"], delta["w_in"], delta["w_out"]], "ffn1")
    for n in ("ffn1_w_gu", "ffn1_w_down"):
        update(n)

    return (loss, dx0[None], *[grads[n] for n in WEIGHTS], *[delta[n] for n in WEIGHTS],
            *[new_m[n] for n in WEIGHTS], *[new_v[n] for n in WEIGHTS])
```

```python
import functools

import numpy as np
import jax
import jax.numpy as jnp
from jax import lax
from jax.experimental import pallas as pl
from jax.experimental.pallas import tpu as pltpu

F32 = jnp.float32
BF16 = jnp.bfloat16

D = 1024
FF = 2816
PW = 5632
HP = 128
CHUNK = 64
WIN = 640
TQ = 512
EPS = 1e-6
NEG = -1e30
LRU_C = 8.0
N_DEV = 8
VMEM_LIMIT = 56 * 1024 * 1024

ADAM_LR, ADAM_B1, ADAM_B2, ADAM_EPS, ADAM_WD, ADAM_STEP = 0.001, 0.9, 0.999, 1e-08, 0.01, 10

MESH = pl.DeviceIdType.MESH
ANY = pl.BlockSpec(memory_space=pl.ANY)


def _cp(*sem):
    return pltpu.CompilerParams(dimension_semantics=tuple(sem), vmem_limit_bytes=VMEM_LIMIT)


def _dot(a, b):
    return jnp.dot(a, b, preferred_element_type=F32)


def _dot_nt(a, b):
    return lax.dot_general(a, b, (((1,), (1,)), ((), ())), preferred_element_type=F32)


def _dot_tn(a, b):
    return lax.dot_general(a, b, (((0,), (0,)), ((), ())), preferred_element_type=F32)


def _mean(v):
    return jnp.mean(v, axis=-1, keepdims=True)


def _colsum(v):
    return jnp.sum(v, axis=0, keepdims=True)


def _sigmoid(v):
    return 0.5 * jnp.tanh(0.5 * v) + 0.5


_GK = 0.7978845608028654


def _gelu(v):
    t = jnp.tanh(_GK * (v + 0.044715 * v * v * v))
    return 0.5 * v * (1.0 + t)


def _pre_norm(xv, vec_ref):
    r = lax.rsqrt(_mean(xv * xv) + EPS)
    n = xv * r * vec_ref[0:1, :]
    return n * (1.0 + vec_ref[3:4, :]) + vec_ref[2:3, :]


def _pre_norm_bwd(dh, xv, dres, vec_ref, vacc_ref):
    r = lax.rsqrt(_mean(xv * xv) + EPS)
    xh = xv * r
    n = xh * vec_ref[0:1, :]
    vacc_ref[2:3, :] += _colsum(dh)
    vacc_ref[3:4, :] += _colsum(dh * n)
    dn = dh * (1.0 + vec_ref[3:4, :])
    vacc_ref[0:1, :] += _colsum(dn * xh)
    dxh = dn * vec_ref[0:1, :]
    return r * (dxh - xh * _mean(dxh * xh)) + dres


def _post_norm_bwd(dxo, fv, res, vec_ref, vacc_ref):
    rf = lax.rsqrt(_mean(fv * fv) + EPS)
    fh = fv * rf
    gp = vec_ref[1:2, :]
    vacc_ref[4:5, :] += _colsum(res * dxo * (fh * gp))
    dy = (res * vec_ref[4:5, :]) * dxo
    vacc_ref[1:2, :] += _colsum(dy * fh)
    dfn = dy * gp
    return rf * (dfn - fh * _mean(dfn * fh))


def ffn_fwd(x, vec, w_gu, w_u, w_dn, res, name, tgt=None, tm=1024, tf=512):
    S = x.shape[0]
    tm = min(tm, S)
    nt = S // tm
    nf = -(-FF // tf)
    tail = FF - tf * (nf - 1)
    head = tgt is not None

    def body(*refs):
        x_ref, vec_ref, wg_ref, wu_ref, wd_ref = refs[:5]
        if head:
            t_ref, xo_ref, h_ref, g_ref, u_ref, a_ref, f_ref, l_ref, hs, acc, lacc = refs[5:]
        else:
            xo_ref, h_ref, g_ref, u_ref, a_ref, f_ref, hs, acc = refs[5:]
        i, j = pl.program_id(0), pl.program_id(1)

        @pl.when(j == 0)
        def _():
            h = _pre_norm(x_ref[...], vec_ref).astype(BF16)
            hs[...] = h
            h_ref[...] = h
            acc[...] = jnp.zeros_like(acc)

        def chunk(w):
            h = hs[...]
            g = _dot(h, wg_ref[:, 0:w])
            u = _dot(h, wu_ref[:, 0:w])
            g_ref[:, 0:w] = g.astype(BF16)
            u_ref[:, 0:w] = u.astype(BF16)
            a = (g * _sigmoid(g) * u).astype(BF16)
            a_ref[:, 0:w] = a
            acc[...] += _dot(a, wd_ref[0:w, :])

        @pl.when(j < nf - 1)
        def _():
            chunk(tf)

        @pl.when(j == nf - 1)
        def _():
            chunk(tail)
            f = acc[...]
            f_ref[...] = f.astype(BF16)
            y = f * lax.rsqrt(_mean(f * f) + EPS) * vec_ref[1:2, :]
            xo = x_ref[...] + (res * vec_ref[4:5, :]) * y
            if head:
                @pl.when(i == 0)
                def _():
                    lacc[...] = jnp.zeros_like(lacc)

                d = xo - t_ref[...]
                xo_ref[...] = d * (1.0 / D)
                lacc[...] += _colsum(d * d)

                @pl.when(i == nt - 1)
                def _():
                    l_ref[...] = jnp.broadcast_to(0.5 * jnp.sum(lacc[...]) * (1.0 / D), (8, 128))
            else:
                xo_ref[...] = xo

    row = lambda i, j: (i, 0)
    col = lambda i, j: (i, j)
    once = dict(pipeline_mode=pl.Buffered(1)) if head else {}
    in_specs = [pl.BlockSpec((tm, D), row, **once), pl.BlockSpec((8, D), lambda i, j: (0, 0)),
                pl.BlockSpec((D, tf), lambda i, j: (0, j)), pl.BlockSpec((D, tf), lambda i, j: (0, j)),
                pl.BlockSpec((tf, D), lambda i, j: (j, 0))]
    out_specs = [pl.BlockSpec((tm, D), row), pl.BlockSpec((tm, D), row), pl.BlockSpec((tm, tf), col),
                 pl.BlockSpec((tm, tf), col), pl.BlockSpec((tm, tf), col), pl.BlockSpec((tm, D), row)]
    out_shape = [jax.ShapeDtypeStruct((S, D), F32), jax.ShapeDtypeStruct((S, D), BF16),
                 jax.ShapeDtypeStruct((S, FF), BF16), jax.ShapeDtypeStruct((S, FF), BF16),
                 jax.ShapeDtypeStruct((S, FF), BF16), jax.ShapeDtypeStruct((S, D), BF16)]
    scratch = [pltpu.VMEM((tm, D), BF16), pltpu.VMEM((tm, D), F32)]
    args = [x, vec, w_gu, w_u, w_dn]
    if head:
        in_specs.append(pl.BlockSpec((tm, D), row, **once))
        out_specs.append(pl.BlockSpec((8, 128), lambda i, j: (0, 0)))
        out_shape.append(jax.ShapeDtypeStruct((8, 128), F32))
        scratch.append(pltpu.VMEM((1, D), F32))
        args.append(tgt)
    return pl.pallas_call(
        body, name=name, grid=(nt, nf), in_specs=in_specs, out_specs=out_specs, out_shape=out_shape,
        scratch_shapes=scratch,
        compiler_params=_cp("arbitrary" if head else "parallel", "arbitrary"),
    )(*args)


def ffn_bwd(dxo, x, f, g, u, vec, w_gu, w_u, w_dn, res, name, tm=1024, tf=512):
    S = x.shape[0]
    tm = min(tm, S)
    nf = -(-FF // tf)
    tail = FF - tf * (nf - 1)

    def body(dxo_ref, x_ref, f_ref, g_ref, u_ref, vec_ref, wg_ref, wu_ref, wd_ref,
             dx_ref, df_ref, dgu_ref, vacc_ref, dfs, acc):
        i, j = pl.program_id(0), pl.program_id(1)

        @pl.when((i == 0) & (j == 0))
        def _():
            vacc_ref[...] = jnp.zeros_like(vacc_ref)

        @pl.when(j == 0)
        def _():
            df = _post_norm_bwd(dxo_ref[...], f_ref[...].astype(F32), res, vec_ref, vacc_ref).astype(BF16)
            dfs[...] = df
            df_ref[...] = df
            acc[...] = jnp.zeros_like(acc)

        def chunk(w):
            da = _dot_nt(dfs[...], wd_ref[0:w, :])
            gv, uv = g_ref[:, 0:w].astype(F32), u_ref[:, 0:w].astype(F32)
            sg = _sigmoid(gv)
            dg = (da * uv * (sg * (1.0 + gv * (1.0 - sg)))).astype(BF16)
            du = (da * (gv * sg)).astype(BF16)
            dgu_ref[0, :, 0:w] = dg
            dgu_ref[1, :, 0:w] = du
            acc[...] += _dot_nt(dg, wg_ref[:, 0:w]) + _dot_nt(du, wu_ref[:, 0:w])

        @pl.when(j < nf - 1)
        def _():
            chunk(tf)

        @pl.when(j == nf - 1)
        def _():
            chunk(tail)
            dx_ref[...] = _pre_norm_bwd(acc[...], x_ref[...], dxo_ref[...], vec_ref, vacc_ref)

    row = lambda i, j: (i, 0)
    col = lambda i, j: (i, j)
    return pl.pallas_call(
        body, name=name, grid=(S // tm, nf),
        in_specs=[pl.BlockSpec((tm, D), row), pl.BlockSpec((tm, D), row, pipeline_mode=pl.Buffered(1)),
                  pl.BlockSpec((tm, D), row)]
        + [pl.BlockSpec((tm, tf), col), pl.BlockSpec((tm, tf), col),
                  pl.BlockSpec((8, D), lambda i, j: (0, 0)),
                  pl.BlockSpec((D, tf), lambda i, j: (0, j)), pl.BlockSpec((D, tf), lambda i, j: (0, j)),
                  pl.BlockSpec((tf, D), lambda i, j: (j, 0))],
        out_specs=[pl.BlockSpec((tm, D), row), pl.BlockSpec((tm, D), row),
                   pl.BlockSpec((2, tm, tf), lambda i, j: (0, i, j)),
                   pl.BlockSpec((8, D), lambda i, j: (0, 0))],
        out_shape=[jax.ShapeDtypeStruct((S, D), F32), jax.ShapeDtypeStruct((S, D), BF16),
                   jax.ShapeDtypeStruct((2, S, FF), BF16), jax.ShapeDtypeStruct((8, D), F32)],
        scratch_shapes=[pltpu.VMEM((tm, D), BF16), pltpu.VMEM((tm, D), F32)],
        compiler_params=_cp("arbitrary", "arbitrary"),
    )(dxo, x, f, g, u, vec, w_gu, w_u, w_dn)


def mm_tn(a, b, name, tm, tn, tk, out_dtype=BF16):
    S, M = a.shape
    if b.ndim == 3:
        G, _, Nf = b.shape
    else:
        G, Nf = 1, b.shape[1]
    N = G * Nf
    tk = min(tk, S)
    nbf = Nf // tn
    nk = S // tk

    def body(a_ref, b_ref, o_ref, acc):
        k = pl.program_id(2)

        @pl.when(k == 0)
        def _():
            acc[...] = jnp.zeros_like(acc)

        acc[...] += _dot_tn(a_ref[...], b_ref[...])

        @pl.when(k == nk - 1)
        def _():
            o_ref[...] = acc[...].astype(out_dtype)

    if b.ndim == 3:
        b_spec = pl.BlockSpec((None, tk, tn), lambda i, j, k: (j // nbf, k, j % nbf))
    else:
        b_spec = pl.BlockSpec((tk, tn), lambda i, j, k: (k, j))
    return pl.pallas_call(
        body, name=name, grid=(M // tm, N // tn, nk),
        in_specs=[pl.BlockSpec((tk, tm), lambda i, j, k: (k, i)), b_spec],
        out_specs=pl.BlockSpec((tm, tn), lambda i, j, k: (i, j)),
        out_shape=jax.ShapeDtypeStruct((M, N), out_dtype),
        scratch_shapes=[pltpu.VMEM((tm, tn), F32)],
        compiler_params=_cp("parallel", "parallel", "arbitrary"),
    )(a, b)


def proj_fwd(x, vec, w_in, name, tm=2048, tn=512):
    S = x.shape[0]
    tm = min(tm, S)
    nq = 1536 // tn

    def body(x_ref, vec_ref, w_ref, h_ref, qkv_ref, rest_ref, hs):
        j = pl.program_id(1)

        @pl.when(j == 0)
        def _():
            h = _pre_norm(x_ref[...], vec_ref).astype(BF16)
            hs[...] = h
            h_ref[...] = h

        r = _dot(hs[...], w_ref[...])

        @pl.when(j < nq)
        def _():
            qkv_ref[...] = r.astype(BF16)

        @pl.when(j >= nq)
        def _():
            rest_ref[...] = r.astype(BF16)

    row = lambda i, j: (i, 0)
    return pl.pallas_call(
        body, name=name, grid=(S // tm, PW // tn),
        in_specs=[pl.BlockSpec((tm, D), row), pl.BlockSpec((8, D), lambda i, j: (0, 0)),
                  pl.BlockSpec((D, tn), lambda i, j: (0, j))],
        out_specs=[pl.BlockSpec((tm, D), row),
                   pl.BlockSpec((tm, tn), lambda i, j: (i, jnp.minimum(j, nq - 1))),
                   pl.BlockSpec((tm, tn), lambda i, j: (i, jnp.maximum(j - nq, 0)))],
        out_shape=[jax.ShapeDtypeStruct((S, D), BF16), jax.ShapeDtypeStruct((S, 1536), BF16),
                   jax.ShapeDtypeStruct((S, 4096), BF16)],
        scratch_shapes=[pltpu.VMEM((tm, D), BF16)],
        compiler_params=_cp("parallel", "arbitrary"),
    )(x, vec, w_in)


def proj_bwd(dq, dkv, dxr, d3, w_in, x, dxo, vec, name, tm=1024, tk=512):
    S = x.shape[0]
    tm = min(tm, S)
    nk = PW // tk

    def body(dq_ref, dkv_ref, dxr_ref, d3_ref, w_ref, x_ref, dxo_ref, vec_ref, dx_ref, vacc_ref, acc):
        i, j = pl.program_id(0), pl.program_id(1)

        @pl.when((i == 0) & (j == 0))
        def _():
            vacc_ref[...] = jnp.zeros_like(vacc_ref)

        @pl.when(j == 0)
        def _():
            acc[...] = _dot_nt(dq_ref[...], w_ref[...])

        @pl.when((j >= 1) & (j < 3))
        def _():
            acc[...] += _dot_nt(dkv_ref[...], w_ref[...])

        @pl.when((j >= 3) & (j < 5))
        def _():
            acc[...] += _dot_nt(dxr_ref[...], w_ref[...])

        @pl.when(j >= 5)
        def _():
            acc[...] += _dot_nt(d3_ref[...], w_ref[...])

        @pl.when(j == nk - 1)
        def _():
            dx_ref[...] = _pre_norm_bwd(acc[...], x_ref[...], dxo_ref[...], vec_ref, vacc_ref)

    row = lambda i, j: (i, 0)
    return pl.pallas_call(
        body, name=name, grid=(S // tm, nk),
        in_specs=[pl.BlockSpec((None, tm, tk), lambda i, j: (0, i, 0)),
                  pl.BlockSpec((None, tm, tk), lambda i, j: (jnp.clip(j - 1, 0, 1), i, 0)),
                  pl.BlockSpec((tm, tk), lambda i, j: (i, jnp.clip(j - 3, 0, 1))),
                  pl.BlockSpec((None, tm, tk), lambda i, j: (jnp.clip(j - 5, 0, 5) // 2, i, jnp.clip(j - 5, 0, 5) % 2)),
                  pl.BlockSpec((D, tk), lambda i, j: (0, j)),
                  pl.BlockSpec((tm, D), row), pl.BlockSpec((tm, D), row),
                  pl.BlockSpec((8, D), lambda i, j: (0, 0))],
        out_specs=[pl.BlockSpec((tm, D), row), pl.BlockSpec((8, D), lambda i, j: (0, 0))],
        out_shape=[jax.ShapeDtypeStruct((S, D), F32), jax.ShapeDtypeStruct((8, D), F32)],
        scratch_shapes=[pltpu.VMEM((tm, D), F32)],
        compiler_params=_cp("arbitrary", "arbitrary"),
    )(dq, dkv, dxr, d3, w_in, x, dxo, vec)


def _two_heads(v, lane):
    zero = jnp.zeros((), v.dtype)
    return jnp.concatenate([jnp.where(lane < 64, v, zero), jnp.where(lane >= 64, v, zero)], axis=0)


def _attn_scores(qm, ka, bias_h, i, grp):
    s = _dot_nt(qm, ka) + bias_h
    col = lax.broadcasted_iota(jnp.int32, s.shape, 1)
    first_key = jnp.where(i == 0, 512 - 128 * grp, 0)
    return jnp.where(col >= first_key, s, NEG)


def _softmax(s):
    e = jnp.exp(s - jnp.max(s, axis=-1, keepdims=True))
    return e * (1.0 / jnp.sum(e, axis=-1, keepdims=True))


NG = TQ // 128


def attn_fwd(qkv, bias, name):
    S = qkv.shape[0]
    nb = S // TQ

    def body(q_ref, kp_ref, kc_ref, vp_ref, vc_ref, b_ref, o_ref, kw, vw):
        i = pl.program_id(1)
        kw[0:TQ, :] = kp_ref[...]
        kw[TQ:2 * TQ, :] = kc_ref[...]
        vw[0:TQ, :] = vp_ref[...]
        vw[TQ:2 * TQ, :] = vc_ref[...]
        lane = lax.broadcasted_iota(jnp.int32, (1, HP), 1)

        rows = [pl.ds(128 * a, 128) for a in range(NG)]
        keys = [pl.ds(128 * a, WIN) for a in range(NG)]
        q2 = [_two_heads(q_ref[r, :] * jnp.asarray(0.125, BF16), lane) for r in rows]
        s = [_attn_scores(q2[a], kw[keys[a], :], b_ref[...], i, a) for a in range(NG)]
        p = [_softmax(sa).astype(BF16) for sa in s]
        o2 = [_dot(p[a], vw[keys[a], :]) for a in range(NG)]
        for a in range(NG):
            o_ref[rows[a], :] = jnp.where(lane < 64, o2[a][0:128], o2[a][128:256]).astype(BF16)

    prev = lambda h, i: (jnp.maximum(i - 1, 0), 0)
    return pl.pallas_call(
        body, name=name, grid=(4, nb),
        in_specs=[pl.BlockSpec((TQ, HP), lambda h, i: (i, h)),
                  pl.BlockSpec((TQ, HP), lambda h, i: (jnp.maximum(i - 1, 0), 4 + h)),
                  pl.BlockSpec((TQ, HP), lambda h, i: (i, 4 + h)),
                  pl.BlockSpec((TQ, HP), lambda h, i: (jnp.maximum(i - 1, 0), 8 + h)),
                  pl.BlockSpec((TQ, HP), lambda h, i: (i, 8 + h)),
                  pl.BlockSpec((None, 256, WIN), lambda h, i: (h, 0, 0))],
        out_specs=pl.BlockSpec((TQ, HP), lambda h, i: (i, h)),
        out_shape=jax.ShapeDtypeStruct((S, 512), BF16),
        scratch_shapes=[pltpu.VMEM((2 * TQ, HP), BF16), pltpu.VMEM((2 * TQ, HP), BF16)],
        compiler_params=_cp("parallel", "arbitrary"),
    )(qkv, qkv, qkv, qkv, qkv, bias)


def attn_bwd(qkv, do, bias, name):
    S = qkv.shape[0]
    nb = S // TQ

    def body(q_ref, kp_ref, kc_ref, vp_ref, vc_ref, do_ref, b_ref, dqkv_ref, db_ref, dkv_ref, kw, vw, ak, av):
        i = pl.program_id(1)

        @pl.when(i == 0)
        def _():
            db_ref[...] = jnp.zeros_like(db_ref)
            ak[...] = jnp.zeros_like(ak)
            av[...] = jnp.zeros_like(av)

        @pl.when(i > 0)
        def _():
            ak[0:TQ, :] = ak[TQ:2 * TQ, :]
            av[0:TQ, :] = av[TQ:2 * TQ, :]
            ak[TQ:2 * TQ, :] = jnp.zeros((TQ, HP), F32)
            av[TQ:2 * TQ, :] = jnp.zeros((TQ, HP), F32)

        @pl.when(i < nb)
        def _():
            kw[0:TQ, :] = kp_ref[...]
            kw[TQ:2 * TQ, :] = kc_ref[...]
            vw[0:TQ, :] = vp_ref[...]
            vw[TQ:2 * TQ, :] = vc_ref[...]
            lane = lax.broadcasted_iota(jnp.int32, (1, HP), 1)

            rows = [pl.ds(128 * a, 128) for a in range(NG)]
            keys = [pl.ds(128 * a, WIN) for a in range(NG)]
            q2 = [_two_heads(q_ref[r, :] * jnp.asarray(0.125, BF16), lane) for r in rows]
            do2 = [_two_heads(do_ref[r, :], lane) for r in rows]
            s = [_attn_scores(q2[a], kw[keys[a], :], b_ref[...], i, a) for a in range(NG)]
            dp = [_dot_nt(do2[a], vw[keys[a], :]) for a in range(NG)]
            p = [_softmax(sa) for sa in s]
            ds = [p[a] * (dp[a] - jnp.sum(p[a] * dp[a], axis=-1, keepdims=True)) for a in range(NG)]
            db_ref[...] += (ds[0] + ds[1]) + (ds[2] + ds[3])
            dsb = [d.astype(BF16) for d in ds]
            dq2 = [_dot(dsb[a], kw[keys[a], :]) for a in range(NG)]
            dk = [_dot_tn(dsb[a], q2[a]) for a in range(NG)]
            dv = [_dot_tn(p[a].astype(BF16), do2[a]) for a in range(NG)]
            for a in range(NG):
                ak[keys[a], :] += dk[a]
                av[keys[a], :] += dv[a]
                dq = jnp.where(lane < 64, dq2[a][0:128], dq2[a][128:256])
                dqkv_ref[0, rows[a], :] = (dq * 0.125).astype(BF16)

        @pl.when(i > 0)
        def _():
            dkv_ref[0] = ak[0:TQ, :].astype(BF16)
            dkv_ref[1] = av[0:TQ, :].astype(BF16)

    cur = lambda i: jnp.minimum(i, nb - 1)
    prv = lambda i: jnp.clip(i - 1, 0, nb - 1)
    dq, db, dkv = pl.pallas_call(
        body, name=name, grid=(4, nb + 1),
        in_specs=[pl.BlockSpec((TQ, HP), lambda h, i: (cur(i), h)),
                  pl.BlockSpec((TQ, HP), lambda h, i: (prv(i), 4 + h)),
                  pl.BlockSpec((TQ, HP), lambda h, i: (cur(i), 4 + h)),
                  pl.BlockSpec((TQ, HP), lambda h, i: (prv(i), 8 + h)),
                  pl.BlockSpec((TQ, HP), lambda h, i: (cur(i), 8 + h)),
                  pl.BlockSpec((TQ, HP), lambda h, i: (cur(i), h)),
                  pl.BlockSpec((None, 256, WIN), lambda h, i: (h, 0, 0))],
        out_specs=[pl.BlockSpec((1, TQ, HP), lambda h, i: (0, cur(i), h)),
                   pl.BlockSpec((None, 256, WIN), lambda h, i: (h, 0, 0)),
                   pl.BlockSpec((2, TQ, HP), lambda h, i: (0, prv(i), h))],
        out_shape=[jax.ShapeDtypeStruct((1, S, 512), BF16), jax.ShapeDtypeStruct((4, 256, WIN), F32),
                   jax.ShapeDtypeStruct((2, S, 512), BF16)],
        scratch_shapes=[pltpu.VMEM((2 * TQ, HP), BF16), pltpu.VMEM((2 * TQ, HP), BF16),
                        pltpu.VMEM((2 * TQ, HP), F32), pltpu.VMEM((2 * TQ, HP), F32)],
        compiler_params=_cp("parallel", "arbitrary"),
    )(qkv, qkv, qkv, qkv, qkv, do, bias)
    return dq, db, dkv


def bias_grad(db, name):
    def body(db_ref, o_ref):
        r = lax.broadcasted_iota(jnp.int32, (128, 128), 0)
        c = lax.broadcasted_iota(jnp.int32, (128, 128), 1)
        flip = (r + c == 127).astype(BF16)
        lane = lax.broadcasted_iota(jnp.int32, (16, 384), 1)
        src = lax.broadcasted_iota(jnp.int32, (128, 384), 0)
        dst = lax.broadcasted_iota(jnp.int32, (128, 384), 1)

        def split_dot(v, m):
            hi = v.astype(BF16)
            r1 = v - hi.astype(F32)
            mid = r1.astype(BF16)
            lo = (r1 - mid.astype(F32)).astype(BF16)
            return _dot(hi, m) + _dot(mid, m) + _dot(lo, m)

        def diag_sums(w):
            y = pltpu.roll(split_dot(w, flip), 0, 1, stride=1, stride_axis=0)
            return jnp.broadcast_to(_colsum(y), (16, 128))

        w4 = db_ref[0, :, 512:640]
        w3 = db_ref[0, :, 384:512]
        far = jnp.sum(db_ref[0, :, 0:384]) + jnp.sum(jnp.where(r >= c, w3, 0.0))
        lo4 = diag_sums(jnp.where(r >= c, w4, 0.0))
        up4 = diag_sums(jnp.where(r < c, w4, 0.0))
        up3 = diag_sums(jnp.where(r < c, w3, 0.0))
        p_lo4 = (dst == 128 + (src + 1) % 128).astype(BF16)
        p_up4 = ((dst == src + 1) & (src < 127)).astype(BF16)
        p_up3 = ((dst == src + 129) & (src < 127)).astype(BF16)
        out = split_dot(lo4, p_lo4) + split_dot(up4, p_up4) + split_dot(up3, p_up3)
        o_ref[0] = out + jnp.where(lane == 256, far, 0.0)

    return pl.pallas_call(
        body, name=name, grid=(8,),
        in_specs=[pl.BlockSpec((1, 128, WIN), lambda h: (h, 0, 0))],
        out_specs=pl.BlockSpec((1, 16, 384), lambda h: (h, 0, 0)),
        out_shape=jax.ShapeDtypeStruct((8, 16, 384), F32),
        compiler_params=_cp("parallel"),
    )(db)[:, 0, :]


LT = 1024
LC = 512


def _lru_gates(xs, pv_ref, wa_ref, wx_ref, tl):
    xc = (pv_ref[4:5, :] + pv_ref[3:4, :] * xs[pl.ds(8, tl), :] + pv_ref[2:3, :] * xs[pl.ds(7, tl), :]
          + pv_ref[1:2, :] * xs[pl.ds(6, tl), :] + pv_ref[0:1, :] * xs[pl.ds(5, tl), :])
    xcb = xc.astype(BF16)
    pa = jnp.concatenate([_dot(xcb[:, 0:256], wa_ref[0]), _dot(xcb[:, 256:512], wa_ref[1])], axis=1)
    px = jnp.concatenate([_dot(xcb[:, 0:256], wx_ref[0]), _dot(xcb[:, 256:512], wx_ref[1])], axis=1)
    r = _sigmoid(pa + pv_ref[5:6, :])
    ig = _sigmoid(px + pv_ref[6:7, :])
    z = -pv_ref[7:8, :]
    sp = jnp.maximum(z, 0.0) + jnp.log1p(jnp.exp(-jnp.abs(z)))
    log_a = (-LRU_C * r) * sp
    a = jnp.exp(log_a)
    s = jnp.tanh(-log_a) * (1.0 + a * a)
    inv_mult = lax.rsqrt(s)
    mult = jnp.where(s > 0.0, s * inv_mult, 0.0)
    return xc, xcb, r, ig, sp, a, mult, inv_mult


def lru_fwd(rest, pvec, wa, wx, name):
    S = rest.shape[0]
    tl = min(LT, S)
    nt = S // tl

    def body(xr_ref, halo_ref, yr_ref, pv_ref, wa_ref, wx_ref, h_ref, hg_ref, xs, a_s, u_s, h_s, carry):
        ti = pl.program_id(1)

        @pl.when(ti == 0)
        def _():
            carry[...] = jnp.zeros_like(carry)

        xs[0:8, :] = jnp.where(ti > 0, halo_ref[8:16, :].astype(F32), 0.0)
        xs[pl.ds(8, tl), :] = xr_ref[...].astype(F32)
        xc, _, _, ig, _, a, mult, _ = _lru_gates(xs, pv_ref, wa_ref, wx_ref, tl)
        a_s[...] = a
        u_s[...] = mult * (ig * xc)
        row = lax.broadcasted_iota(jnp.int32, (8, LC), 0)

        def blk(bi, c):
            o = pl.multiple_of(bi * 8, 8)
            av = a_s[pl.ds(o, 8), :]
            bv = u_s[pl.ds(o, 8), :]
            for d in (1, 2, 4):
                a_sh = pltpu.roll(av, d, 0)
                b_sh = pltpu.roll(bv, d, 0)
                m = row >= d
                bv = jnp.where(m, av * b_sh + bv, bv)
                av = jnp.where(m, av * a_sh, av)
            hv = bv + av * c
            h_s[pl.ds(o, 8), :] = hv
            return hv[7:8, :]

        carry[...] = lax.fori_loop(0, tl // 8, blk, carry[...])
        h = h_s[...]
        h_ref[...] = h
        hg_ref[...] = (h * _gelu(yr_ref[...].astype(F32))).astype(BF16)

    hb = tl // 16
    return pl.pallas_call(
        body, name=name, grid=(2, nt),
        in_specs=[pl.BlockSpec((tl, LC), lambda c, t: (t, c)),
                  pl.BlockSpec((16, LC), lambda c, t: (jnp.maximum(t * hb - 1, 0), c)),
                  pl.BlockSpec((tl, LC), lambda c, t: (t, 2 + c)),
                  pl.BlockSpec((8, LC), lambda c, t: (0, c)),
                  pl.BlockSpec((2, 256, 256), lambda c, t: (c, 0, 0)),
                  pl.BlockSpec((2, 256, 256), lambda c, t: (c, 0, 0))],
        out_specs=[pl.BlockSpec((tl, LC), lambda c, t: (t, c)), pl.BlockSpec((tl, LC), lambda c, t: (t, c))],
        out_shape=[jax.ShapeDtypeStruct((S, D), F32), jax.ShapeDtypeStruct((S, D), BF16)],
        scratch_shapes=[pltpu.VMEM((tl + 8, LC), F32), pltpu.VMEM((tl, LC), F32), pltpu.VMEM((tl, LC), F32),
                        pltpu.VMEM((tl, LC), F32), pltpu.VMEM((1, LC), F32)],
        compiler_params=_cp("parallel", "arbitrary"),
    )(rest, rest, rest, pvec, wa, wx)


def lru_bwd(dh, h, rest, pvec, wa, wx, name):
    S = rest.shape[0]
    tl = min(LT, S)
    nt = S // tl

    def body(dh_ref, h_ref, hhalo_ref, xr_ref, xhalo_ref, pv_ref, wa_ref, wx_ref,
             dxr_ref, vacc_ref, dwa_ref, dwx_ref,
             xs, hs, a_s, ash_s, b_s, lam_s, dxe, anext, lnext, dxnext):
        ti = pl.program_id(1)
        tr = nt - 1 - ti

        @pl.when(ti == 0)
        def _():
            anext[...] = jnp.zeros_like(anext)
            lnext[...] = jnp.zeros_like(lnext)
            dxnext[...] = jnp.zeros_like(dxnext)
            vacc_ref[...] = jnp.zeros_like(vacc_ref)
            dwa_ref[...] = jnp.zeros_like(dwa_ref)
            dwx_ref[...] = jnp.zeros_like(dwx_ref)

        xs[0:8, :] = jnp.where(tr > 0, xhalo_ref[8:16, :].astype(F32), 0.0)
        xs[pl.ds(8, tl), :] = xr_ref[...].astype(F32)
        xc, xcb, r, ig, sp, a, mult, inv_mult = _lru_gates(xs, pv_ref, wa_ref, wx_ref, tl)

        a_s[pl.ds(0, tl), :] = a
        a_s[pl.ds(tl, 8), :] = jnp.broadcast_to(anext[...], (8, LC))
        ash_s[...] = a_s[pl.ds(1, tl), :]
        b_s[...] = dh_ref[...]
        row = lax.broadcasted_iota(jnp.int32, (8, LC), 0)

        def blk(k, c):
            o = pl.multiple_of((tl // 8 - 1 - k) * 8, 8)
            av = ash_s[pl.ds(o, 8), :]
            bv = b_s[pl.ds(o, 8), :]
            for d in (1, 2, 4):
                a_sh = pltpu.roll(av, 8 - d, 0)
                b_sh = pltpu.roll(bv, 8 - d, 0)
                m = row < 8 - d
                bv = jnp.where(m, bv + av * b_sh, bv)
                av = jnp.where(m, av * a_sh, av)
            lv = bv + av * c
            lam_s[pl.ds(o, 8), :] = lv
            return lv[0:1, :]

        lnext[...] = lax.fori_loop(0, tl // 8, blk, lnext[...])
        anext[...] = a[0:1, :]
        lam = lam_s[...]

        hs[0:8, :] = jnp.where(tr > 0, hhalo_ref[...], 0.0)
        hs[pl.ds(8, tl), :] = h_ref[...]
        d_a = lam * hs[pl.ds(7, tl), :]
        d_mult = lam * (ig * xc)
        d_ig = lam * mult * xc
        dxc = lam * mult * ig
        d_log_a = d_a * a - d_mult * (a * a) * inv_mult
        d_r = d_log_a * (-LRU_C * sp)
        vacc_ref[7:8, :] += _colsum(d_log_a * (-LRU_C * r)) * (-_sigmoid(-pv_ref[7:8, :]))
        d_pa = d_r * r * (1.0 - r)
        d_px = d_ig * ig * (1.0 - ig)
        vacc_ref[5:6, :] += _colsum(d_pa)
        vacc_ref[6:7, :] += _colsum(d_px)
        dpa = d_pa.astype(BF16)
        dpx = d_px.astype(BF16)
        back = []
        for g in range(2):
            sl = slice(256 * g, 256 * g + 256)
            dwa_ref[g] += _dot_tn(xcb[:, sl], dpa[:, sl])
            dwx_ref[g] += _dot_tn(xcb[:, sl], dpx[:, sl])
            back.append(_dot_nt(dpa[:, sl], wa_ref[g]) + _dot_nt(dpx[:, sl], wx_ref[g]))
        dxc = dxc + jnp.concatenate(back, axis=1)
        vacc_ref[4:5, :] += _colsum(dxc)
        for k in range(4):
            vacc_ref[k:k + 1, :] += _colsum(dxc * xs[pl.ds(5 + k, tl), :])
        dxe[pl.ds(0, tl), :] = dxc
        dxe[pl.ds(tl, 8), :] = dxnext[...]
        dxr = (pv_ref[3:4, :] * dxc + pv_ref[2:3, :] * dxe[pl.ds(1, tl), :]
               + pv_ref[1:2, :] * dxe[pl.ds(2, tl), :] + pv_ref[0:1, :] * dxe[pl.ds(3, tl), :])
        dxr_ref[...] = dxr.astype(BF16)
        dxnext[...] = dxc[0:8, :]

    hb = tl // 8
    rev = lambda t: nt - 1 - t
    halo = lambda t: jnp.maximum(rev(t) * hb - 1, 0)
    big = lambda: pltpu.VMEM((tl + 8, LC), F32)
    til = lambda: pltpu.VMEM((tl, LC), F32)
    return pl.pallas_call(
        body, name=name, grid=(2, nt),
        in_specs=[pl.BlockSpec((tl, LC), lambda c, t: (rev(t), c)),
                  pl.BlockSpec((tl, LC), lambda c, t: (rev(t), c)),
                  pl.BlockSpec((8, LC), lambda c, t: (halo(t), c)),
                  pl.BlockSpec((tl, LC), lambda c, t: (rev(t), c)),
                  pl.BlockSpec((16, LC), lambda c, t: (jnp.maximum(rev(t) * (tl // 16) - 1, 0), c)),
                  pl.BlockSpec((8, LC), lambda c, t: (0, c)),
                  pl.BlockSpec((2, 256, 256), lambda c, t: (c, 0, 0)),
                  pl.BlockSpec((2, 256, 256), lambda c, t: (c, 0, 0))],
        out_specs=[pl.BlockSpec((tl, LC), lambda c, t: (rev(t), c)),
                   pl.BlockSpec((8, LC), lambda c, t: (0, c)),
                   pl.BlockSpec((2, 256, 256), lambda c, t: (c, 0, 0)),
                   pl.BlockSpec((2, 256, 256), lambda c, t: (c, 0, 0))],
        out_shape=[jax.ShapeDtypeStruct((S, D), BF16), jax.ShapeDtypeStruct((8, D), F32),
                   jax.ShapeDtypeStruct((4, 256, 256), F32), jax.ShapeDtypeStruct((4, 256, 256), F32)],
        scratch_shapes=[big(), big(), big(), til(), til(), til(), big(),
                        pltpu.VMEM((1, LC), F32), pltpu.VMEM((1, LC), F32), pltpu.VMEM((8, LC), F32)],
        compiler_params=_cp("parallel", "arbitrary"),
    )(dh, h, h, rest, rest, pvec, wa, wx)


def mix_out_fwd(x, ao, hg, rest, vec, w_att_o, w_rec_o, w_out, name, tm=512):
    S = x.shape[0]
    tm = min(tm, S)

    def body(x_ref, ao_ref, hg_ref, ga_ref, gr_ref, vec_ref, wa_ref, wr_ref, wo_ref,
             xo_ref, att_ref, rec_ref, mg_ref, f_ref):
        att = _dot(ao_ref[...], wa_ref[...])
        rec = _dot(hg_ref[...], wr_ref[...])
        att_ref[...] = att.astype(BF16)
        rec_ref[...] = rec.astype(BF16)
        mg = (_sigmoid(ga_ref[...].astype(F32)) * att + _sigmoid(gr_ref[...].astype(F32)) * rec).astype(BF16)
        mg_ref[...] = mg
        f = _dot(mg, wo_ref[...])
        f_ref[...] = f.astype(BF16)
        y = f * lax.rsqrt(_mean(f * f) + EPS) * vec_ref[1:2, :]
        xo_ref[...] = x_ref[...] + (1.0 * vec_ref[4:5, :]) * y

    row = lambda i: (i, 0)
    full = lambda r: pl.BlockSpec((r, D), lambda i: (0, 0))
    return pl.pallas_call(
        body, name=name, grid=(S // tm,),
        in_specs=[pl.BlockSpec((tm, D), row), pl.BlockSpec((tm, 512), row), pl.BlockSpec((tm, D), row),
                  pl.BlockSpec((tm, D), lambda i: (i, 2)), pl.BlockSpec((tm, D), lambda i: (i, 3)),
                  full(8), full(512), full(D), full(D)],
        out_specs=[pl.BlockSpec((tm, D), row)] * 5,
        out_shape=[jax.ShapeDtypeStruct((S, D), F32)] + [jax.ShapeDtypeStruct((S, D), BF16)] * 4,
        compiler_params=_cp("parallel"),
    )(x, ao, hg, rest, rest, vec, w_att_o, w_rec_o, w_out)


def mix_out_bwd(dxo, f, att, rec, rest, h, vec, w_att_o, w_rec_o, w_out, name, tm=512):
    S = dxo.shape[0]
    tm = min(tm, S)

    def body(dxo_ref, f_ref, att_ref, rec_ref, yr_ref, ga_ref, gr_ref, h_ref, vec_ref, wa_ref, wr_ref, wo_ref,
             df_ref, da_ref, dr_ref, dao_ref, dh_ref, d3_ref, vacc_ref):
        @pl.when(pl.program_id(0) == 0)
        def _():
            vacc_ref[...] = jnp.zeros_like(vacc_ref)

        df = _post_norm_bwd(dxo_ref[...], f_ref[...].astype(F32), 1.0, vec_ref, vacc_ref).astype(BF16)
        df_ref[...] = df
        dm = _dot_nt(df, wo_ref[...])
        sa = _sigmoid(ga_ref[...].astype(F32))
        sr = _sigmoid(gr_ref[...].astype(F32))
        d_att = (dm * sa).astype(BF16)
        d_rec = (dm * sr).astype(BF16)
        da_ref[...] = d_att
        dr_ref[...] = d_rec
        d3_ref[1] = (dm * att_ref[...].astype(F32) * (sa * (1.0 - sa))).astype(BF16)
        d3_ref[2] = (dm * rec_ref[...].astype(F32) * (sr * (1.0 - sr))).astype(BF16)
        dao_ref[...] = _dot_nt(d_att, wa_ref[...]).astype(BF16)
        d_hg = _dot_nt(d_rec, wr_ref[...])
        yr = yr_ref[...].astype(F32)
        t = jnp.tanh(_GK * (yr + 0.044715 * yr * yr * yr))
        dh_ref[...] = d_hg * (0.5 * yr * (1.0 + t))
        gelu_grad = 0.5 * (1.0 + t) + 0.5 * yr * (1.0 - t * t) * _GK * (1.0 + 3.0 * 0.044715 * yr * yr)
        d3_ref[0] = (d_hg * h_ref[...] * gelu_grad).astype(BF16)

    row = lambda i: (i, 0)
    full = lambda r: pl.BlockSpec((r, D), lambda i: (0, 0))
    return pl.pallas_call(
        body, name=name, grid=(S // tm,),
        in_specs=[pl.BlockSpec((tm, D), row)] * 4
        + [pl.BlockSpec((tm, D), lambda i: (i, 1)), pl.BlockSpec((tm, D), lambda i: (i, 2)),
           pl.BlockSpec((tm, D), lambda i: (i, 3)), pl.BlockSpec((tm, D), row),
           full(8), full(512), full(D), full(D)],
        out_specs=[pl.BlockSpec((tm, D), row)] * 3
        + [pl.BlockSpec((tm, 512), row), pl.BlockSpec((tm, D), row),
           pl.BlockSpec((3, tm, D), lambda i: (0, i, 0)), pl.BlockSpec((8, D), lambda i: (0, 0))],
        out_shape=[jax.ShapeDtypeStruct((S, D), BF16)] * 3
        + [jax.ShapeDtypeStruct((S, 512), BF16), jax.ShapeDtypeStruct((S, D), F32),
           jax.ShapeDtypeStruct((3, S, D), BF16), jax.ShapeDtypeStruct((8, D), F32)],
        compiler_params=_cp("arbitrary"),
    )(dxo, f, att, rec, rest, rest, rest, h, vec, w_att_o, w_rec_o, w_out)


def dw_in(h, dq, dkv, dxr, d3, name, tk=1024, tn=512):
    S = h.shape[0]
    tk = min(tk, S)
    nk = S // tk

    def body(h_ref, dq_ref, dkv_ref, dxr_ref, d3_ref, o_ref, acc):
        j, k = pl.program_id(0), pl.program_id(1)

        @pl.when(k == 0)
        def _():
            acc[...] = jnp.zeros_like(acc)

        @pl.when(j == 0)
        def _():
            acc[...] += _dot_tn(h_ref[...], dq_ref[...])

        @pl.when((j >= 1) & (j < 3))
        def _():
            acc[...] += _dot_tn(h_ref[...], dkv_ref[...])

        @pl.when((j >= 3) & (j < 5))
        def _():
            acc[...] += _dot_tn(h_ref[...], dxr_ref[...])

        @pl.when(j >= 5)
        def _():
            acc[...] += _dot_tn(h_ref[...], d3_ref[...])

        @pl.when(k == nk - 1)
        def _():
            o_ref[...] = acc[...].astype(BF16)

    use = lambda j, k, lo, hi: jnp.where((j >= lo) & (j < hi), k, 0)
    g3 = lambda j: jnp.clip(j - 5, 0, 5)
    return pl.pallas_call(
        body, name=name, grid=(PW // tn, nk),
        in_specs=[pl.BlockSpec((tk, D), lambda j, k: (k, 0)),
                  pl.BlockSpec((None, tk, tn), lambda j, k: (0, use(j, k, 0, 1), 0)),
                  pl.BlockSpec((None, tk, tn), lambda j, k: (jnp.clip(j - 1, 0, 1), use(j, k, 1, 3), 0)),
                  pl.BlockSpec((tk, tn), lambda j, k: (use(j, k, 3, 5), jnp.clip(j - 3, 0, 1))),
                  pl.BlockSpec((None, tk, tn), lambda j, k: (g3(j) // 2, use(j, k, 5, 11), g3(j) % 2))],
        out_specs=pl.BlockSpec((D, tn), lambda j, k: (0, j)),
        out_shape=jax.ShapeDtypeStruct((D, PW), BF16),
        scratch_shapes=[pltpu.VMEM((D, tn), F32)],
        compiler_params=_cp("parallel", "arbitrary"),
    )(h, dq, dkv, dxr, d3)


def ada_fwd(c_all, w_ada, b_ada, name, tn=768):
    n = w_ada.shape[1]

    def body(c_ref, w_ref, b_ref, o_ref):
        cv = c_ref[...]
        ca = (cv * _sigmoid(cv)).astype(BF16)
        o_ref[...] = _dot(ca, w_ref[...].astype(BF16)) + b_ref[...]

    return pl.pallas_call(
        body, name=name, grid=(n // tn,),
        in_specs=[pl.BlockSpec((8, D), lambda j: (0, 0)), pl.BlockSpec((D, tn), lambda j: (0, j)),
                  pl.BlockSpec((1, tn), lambda j: (0, j))],
        out_specs=pl.BlockSpec((8, tn), lambda j: (0, j)),
        out_shape=jax.ShapeDtypeStruct((8, n), F32),
        compiler_params=_cp("parallel"),
    )(c_all, w_ada, b_ada)


def ada_bwd(c_all_t, dmod, name, tn=768):
    n = dmod.shape[1]

    def body(c_ref, d_ref, o_ref):
        cv = c_ref[...]
        ca = (cv * _sigmoid(cv)).astype(BF16)
        o_ref[...] = _dot(ca, d_ref[...].astype(BF16))

    return pl.pallas_call(
        body, name=name, grid=(n // tn,),
        in_specs=[pl.BlockSpec((D, 128), lambda j: (0, 0)), pl.BlockSpec((128, tn), lambda j: (0, j))],
        out_specs=pl.BlockSpec((D, tn), lambda j: (0, j)),
        out_shape=jax.ShapeDtypeStruct((D, n), F32),
        compiler_params=_cp("parallel"),
    )(c_all_t, dmod)


def _row_tile(rows, cols, itemsize=4, budget=1536 * 1024):
    best = None
    for t in range(8, rows + 1, 8):
        if rows % t == 0 and t * cols * itemsize <= budget:
            best = t
    return rows if best is None else best


def sum_lead(parts, name, out_dtype=F32):
    n, R, C = parts.shape
    tr = _row_tile(R, C * n)

    def body(p_ref, o_ref):
        acc = p_ref[0].astype(F32)
        for k in range(1, n):
            acc = acc + p_ref[k].astype(F32)
        o_ref[...] = acc.astype(out_dtype)

    return pl.pallas_call(
        body, name=name, grid=(R // tr,),
        in_specs=[pl.BlockSpec((n, tr, C), lambda i: (0, i, 0))],
        out_specs=pl.BlockSpec((tr, C), lambda i: (i, 0)),
        out_shape=jax.ShapeDtypeStruct((R, C), out_dtype),
        compiler_params=_cp("parallel"),
    )(parts)


def adamw(w, g, m, v, name, emit_g=False):
    R, C = w.shape
    tr = _row_tile(R, C * 8, budget=8 * 1024 * 1024)

    def body(w_ref, g_ref, m_ref, v_ref, d_ref, mo_ref, vo_ref, *go_ref):
        gv = g_ref[...]
        if emit_g:
            go_ref[0][...] = gv
        mn = ADAM_B1 * m_ref[...] + (1.0 - ADAM_B1) * gv
        vn = ADAM_B2 * v_ref[...] + (1.0 - ADAM_B2) * (gv * gv)
        m_hat = mn / (1.0 - ADAM_B1 ** ADAM_STEP)
        v_hat = vn / (1.0 - ADAM_B2 ** ADAM_STEP)
        d_ref[...] = -ADAM_LR * (m_hat / (jnp.sqrt(v_hat) + ADAM_EPS) + ADAM_WD * w_ref[...])
        mo_ref[...] = mn
        vo_ref[...] = vn

    spec = pl.BlockSpec((tr, C), lambda i: (i, 0))
    return pl.pallas_call(
        body, name=name, grid=(R // tr,),
        in_specs=[spec] * 4, out_specs=[spec] * (4 if emit_g else 3),
        out_shape=[jax.ShapeDtypeStruct((R, C), F32)] * (4 if emit_g else 3),
        compiler_params=_cp("parallel"),
    )(w, g, m, v)


def _mesh_pos():
    return lax.axis_index("x"), lax.axis_index("y"), lax.axis_index("c")


def _other_chips(mx, my):
    return [(1 - mx, my), (mx, 1 - my), (1 - mx, 1 - my)]


def ag_small(x, name):
    R = x.shape[0]

    def body(x_ref, out_ref, send_sems, recv_sems, local_sem):
        mx, my, mc = _mesh_pos()
        me, sibling = (mx, my, mc), (mx, my, 1 - mc)
        chips = _other_chips(mx, my)

        def slot(px, py, pc):
            return out_ref.at[4 * px + 2 * py + pc]

        def copy(k, block, to, src=None):
            return pltpu.make_async_remote_copy(
                src_ref=slot(*block) if src is None else src, dst_ref=slot(*block),
                send_sem=send_sems.at[k], recv_sem=recv_sems.at[k], device_id=to, device_id_type=MESH)

        mine = pltpu.make_async_copy(x_ref, slot(*me), local_sem)
        mine.start()
        first = [copy(0, me, sibling, src=x_ref)]
        first += [copy(1 + j, me, (*chip, mc), src=x_ref) for j, chip in enumerate(chips)]
        for cp in first:
            cp.start()
        passed = [copy(4 + j, (*chip, mc), sibling) for j, chip in enumerate(chips)]
        for j, chip in enumerate(chips):
            copy(1 + j, (*chip, mc), me).wait_recv()
            passed[j].start()
        copy(0, sibling, me).wait_recv()
        for j, chip in enumerate(chips):
            copy(4 + j, (*chip, 1 - mc), me).wait_recv()
        for cp in first + passed:
            cp.wait_send()
        mine.wait()

    return pl.pallas_call(
        body, name=name,
        out_shape=jax.ShapeDtypeStruct((N_DEV, R, 128), F32),
        in_specs=[pl.BlockSpec(memory_space=pltpu.VMEM)],
        out_specs=pl.BlockSpec(memory_space=pltpu.VMEM),
        scratch_shapes=[pltpu.SemaphoreType.DMA((7,)), pltpu.SemaphoreType.DMA((7,)), pltpu.SemaphoreType.DMA],
        compiler_params=pltpu.CompilerParams(vmem_limit_bytes=VMEM_LIMIT),
    )(x)


BIG = (("ffn1_w_gu", "col", D, PW), ("ffn1_w_down", "row", FF, D), ("w_in", "col", D, PW),
       ("w_att_o", "col", 512, D), ("w_rec_o", "row", D, D), ("w_out", "row", D, D),
       ("ffn2_w_gu", "col", D, PW), ("ffn2_w_down", "row", FF, D))
NBIG = len(BIG)


def _shard_shape(kind, R, C):
    return (R, C // 4) if kind == "col" else (R // 4, C)


def _region(ref, kind, R, C, q, half, t, tr):
    sr, sc = _shard_shape(kind, R, C)
    if kind == "col":
        return ref.at[pl.ds(pl.multiple_of(half * (R // 2) + t * tr, 16), tr), pl.ds(q * sc, sc)]
    return ref.at[pl.ds(pl.multiple_of(q * sr + t * tr, 16), tr), pl.ds(half * (C // 2), C // 2)]


def ag_local(w, kind, R, C, p_arr, name, after=()):
    sr, sc = _shard_shape(kind, R, C)
    tr = _row_tile(sr, sc, budget=2 * 1024 * 1024)
    nt = sr // tr
    after = list(after)

    def body(p_ref, w_ref, *rest):
        rest[-1][...] = w_ref[...].astype(BF16)

    if kind == "col":
        o_spec = pl.BlockSpec((tr, sc), lambda i, p: (i, p[0]))
    else:
        o_spec = pl.BlockSpec((tr, sc), lambda i, p: (p[0] * nt + i, 0))
    return pl.pallas_call(
        body, name=name,
        grid_spec=pltpu.PrefetchScalarGridSpec(
            num_scalar_prefetch=1, grid=(nt,),
            in_specs=[pl.BlockSpec((tr, sc), lambda i, p: (i, 0))] + [ANY] * len(after), out_specs=o_spec),
        out_shape=jax.ShapeDtypeStruct((R, C), BF16),
        compiler_params=_cp("parallel"),
    )(p_arr, w, *after)


HBM_SPEC = pl.BlockSpec(memory_space=pltpu.HBM)
SEM_SPEC = pl.BlockSpec(memory_space=pltpu.SEMAPHORE)


def _ag_sems(geoms):
    return sum(6 if both else 3 for (_, _, _, both) in geoms)


def _ag_copies(fulls, geoms, ssem, rsem, mx, my, mc, q, h):
    chips = _other_chips(mx, my)
    out, base = [], 0
    for w, (kind, R, C, both) in enumerate(geoms):
        sr, sc = _shard_shape(kind, R, C)
        hr = sr // 2 if kind == "col" else sr
        reg = _region(fulls[w], kind, R, C, q, h, 0, hr)
        out.append([pltpu.make_async_remote_copy(
            src_ref=reg, dst_ref=reg, send_sem=ssem.at[base + 3 * t + k], recv_sem=rsem.at[base + 3 * t + k],
            device_id=(*chips[k], mc if t == 0 else 1 - mc), device_id_type=MESH)
            for t in range(2 if both else 1) for k in range(3)])
        base += 6 if both else 3
    return out


def ag_start(fulls, geoms, after, name):
    n = len(fulls)
    after = list(after)
    m = len(after)

    def body(*refs):
        ssem, rsem = refs[n + m:n + m + 2]
        outs, token = refs[n + m + 2:2 * n + m + 2], refs[2 * n + m + 2]
        mx, my, mc = _mesh_pos()
        p = 2 * mx + my
        col = [w for w, g in enumerate(geoms) if g[0] == "col"]
        row = [w for w, g in enumerate(geoms) if g[0] == "row"]
        for q in range(4):
            @pl.when(p == q)
            def _(q=q):
                cps = _ag_copies(outs, geoms, ssem, rsem, mx, my, mc, q, mc)
                for w in col:
                    for cp in cps[w]:
                        cp.start()
        for h in range(2):
            @pl.when(mc == h)
            def _(h=h):
                cps = _ag_copies(outs, geoms, ssem, rsem, mx, my, mc, p, h)
                for w in row:
                    for cp in cps[w]:
                        cp.start()
        token[...] = jnp.zeros_like(token)

    res = pl.pallas_call(
        body, name=name,
        out_shape=[pltpu.SemaphoreType.DMA((_ag_sems(geoms),)), pltpu.SemaphoreType.DMA((_ag_sems(geoms),))]
        + [pltpu.HBM(a.shape, a.dtype) for a in fulls] + [jax.ShapeDtypeStruct((8, 128), F32)],
        in_specs=[HBM_SPEC] * n + [ANY] * m,
        out_specs=[SEM_SPEC, SEM_SPEC] + [HBM_SPEC] * n + [pl.BlockSpec(memory_space=pltpu.VMEM)],
        input_output_aliases={w: 2 + w for w in range(n)},
        compiler_params=pltpu.CompilerParams(has_side_effects=pltpu.SideEffectType.DATAFLOW_SIDE_EFFECTING),
    )(*[pltpu.with_memory_space_constraint(a, pltpu.HBM) for a in fulls], *after)
    return res[0], res[1], list(res[2:2 + n]), res[2 + n]


def ag_wait(fulls, geoms, ssem, rsem, after, name):
    n = len(fulls)
    after = list(after) if isinstance(after, (list, tuple)) else [after]

    def body(*refs):
        ins, ssem_ref, rsem_ref = refs[:n], refs[n], refs[n + 1]
        mx, my, mc = _mesh_pos()
        for cps in _ag_copies(ins, geoms, ssem_ref, rsem_ref, mx, my, mc, 0, 0):
            for cp in cps:
                cp.wait_send()
                cp.wait_recv()

    return list(pl.pallas_call(
        body, name=name,
        out_shape=[pltpu.HBM(a.shape, a.dtype) for a in fulls],
        in_specs=[HBM_SPEC] * n + [SEM_SPEC, SEM_SPEC] + [ANY] * len(after),
        out_specs=[HBM_SPEC] * n,
        input_output_aliases={w: w for w in range(n)},
        compiler_params=pltpu.CompilerParams(has_side_effects=pltpu.SideEffectType.DATAFLOW_SIDE_EFFECTING),
    )(*fulls, ssem, rsem, *after))


def ag_forward(full, kind, R, C, name):
    sr, sc = _shard_shape(kind, R, C)
    hr, hc = (sr // 2, sc) if kind == "col" else (sr, sc // 2)
    tr = _row_tile(hr, hc, itemsize=2, budget=512 * 1024)
    nt = hr // tr

    total = 3 * nt

    def body(src_ref, full_ref, stage, lsem, ssem, rsem):
        step = pl.program_id(0) * nt + pl.program_id(1)
        par = step % 2
        mx, my, mc = _mesh_pos()

        def load(s, q, h, t):
            return pltpu.make_async_copy(_region(src_ref, kind, R, C, q, h, t, tr), stage.at[s], lsem.at[s])

        def push(s, q, h, t):
            return pltpu.make_async_remote_copy(src_ref=stage.at[s], dst_ref=_region(full_ref, kind, R, C, q, h, t, tr),
                                                send_sem=ssem.at[s], recv_sem=rsem, device_id=(mx, my, 1 - mc),
                                                device_id_type=MESH)

        def for_tile(stp, fn):
            q_k = _partner_chip(stp // nt, 2 * mx + my)
            if kind == "col":
                for q in range(4):
                    @pl.when(q_k == q)
                    def _(q=q):
                        fn(q, mc, stp % nt)
            else:
                for h in range(2):
                    @pl.when(mc == h)
                    def _(h=h):
                        fn(q_k, h, stp % nt)

        @pl.when(step == 0)
        def _():
            for_tile(step, lambda q, h, t: load(0, q, h, t).start())

        load(par, 0, 0, 0).wait()
        for_tile(step, lambda q, h, t: push(par, q, h, t).start())

        @pl.when(step + 1 < total)
        def _():
            @pl.when(step >= 1)
            def _():
                push(1 - par, 0, 0, 0).wait_send()
            for_tile(step + 1, lambda q, h, t: load(1 - par, q, h, t).start())

        @pl.when(step == total - 1)
        def _():
            push(par, 0, 0, 0).wait_send()
            push(1 - par, 0, 0, 0).wait_send()
            three = full_ref.at[pl.ds(0, hr), pl.ds(0, 3 * hc)] if kind == "col" else full_ref.at[pl.ds(0, 3 * hr), pl.ds(0, hc)]
            pltpu.make_async_remote_copy(src_ref=three, dst_ref=three, send_sem=ssem.at[0], recv_sem=rsem,
                                         device_id=(mx, my, 1 - mc), device_id_type=MESH).wait_recv()

    return pl.pallas_call(
        body, name=name, grid=(3, nt),
        in_specs=[ANY], out_specs=ANY,
        out_shape=jax.ShapeDtypeStruct((R, C), BF16),
        scratch_shapes=[pltpu.VMEM((2, tr, hc), BF16), pltpu.SemaphoreType.DMA((2,)), pltpu.SemaphoreType.DMA((2,)),
                        pltpu.SemaphoreType.DMA],
        input_output_aliases={0: 0},
        compiler_params=_cp("arbitrary", "arbitrary"),
    )(full)


def _half_shape(kind, R, C):
    return (R // 2, C) if kind == "col" else (R, C // 2)


def _piece_shape(kind, R, C):
    return (R // 2, C // 4) if kind == "col" else (R // 4, C // 2)


def pair_push(g, kind, c_arr, name):
    R, C = g.shape
    hr, hc = _half_shape(kind, R, C)
    tr = _row_tile(hr, hc, itemsize=2, budget=1024 * 1024)
    nt = hr // tr

    def body(c_ref, g_ref, out_ref, stage, ssem, rsem):
        i = pl.program_id(0)
        slot = i % 2
        mx, my, mc = _mesh_pos()

        def push(s, t):
            return pltpu.make_async_remote_copy(
                src_ref=stage.at[s], dst_ref=out_ref.at[pl.ds(pl.multiple_of(t * tr, 16), tr)],
                send_sem=ssem.at[s], recv_sem=rsem, device_id=(mx, my, 1 - mc), device_id_type=MESH)

        @pl.when(i >= 2)
        def _():
            push(slot, 0).wait_send()

        stage[slot] = g_ref[...]
        push(slot, i).start()

        @pl.when(i == nt - 1)
        def _():
            push(slot, 0).wait_send()
            if nt >= 2:
                push(1 - slot, 0).wait_send()
            pltpu.make_async_remote_copy(src_ref=out_ref, dst_ref=out_ref, send_sem=ssem.at[0], recv_sem=rsem,
                                         device_id=(mx, my, 1 - mc), device_id_type=MESH).wait_recv()

    if kind == "col":
        g_spec = pl.BlockSpec((tr, hc), lambda i, c: ((1 - c[0]) * nt + i, 0))
    else:
        g_spec = pl.BlockSpec((tr, hc), lambda i, c: (i, 1 - c[0]))
    return pl.pallas_call(
        body, name=name,
        grid_spec=pltpu.PrefetchScalarGridSpec(
            num_scalar_prefetch=1, grid=(nt,), in_specs=[g_spec], out_specs=ANY,
            scratch_shapes=[pltpu.VMEM((2, tr, hc), BF16), pltpu.SemaphoreType.DMA((2,)), pltpu.SemaphoreType.DMA]),
        out_shape=jax.ShapeDtypeStruct((hr, hc), BF16),
        compiler_params=_cp("arbitrary"),
    )(c_arr, g)


def _partner_chip(k, p):
    return p ^ jnp.where(k == 0, 2, jnp.where(k == 1, 1, jnp.where(k == 2, 3, 0)))


def pair_add(g, got, kind, cp_arr, name):
    R, C = g.shape
    pr, pc = _piece_shape(kind, R, C)
    tr = _row_tile(pr, pc, itemsize=2, budget=1024 * 1024)
    nt = pr // tr

    def body(cp_ref, g_ref, got_ref, ps_ref, rb_ref):
        tile = (g_ref[...].astype(F32) + got_ref[...].astype(F32)).astype(BF16)
        ps_ref[...] = tile

        @pl.when(pl.program_id(1) == cp_ref[1])
        def _():
            rb_ref[...] = tile

    if kind == "col":
        g_spec = pl.BlockSpec((tr, pc), lambda i, q, cp: (cp[0] * nt + i, q))
        got_spec = pl.BlockSpec((tr, pc), lambda i, q, cp: (i, q))
    else:
        g_spec = pl.BlockSpec((tr, pc), lambda i, q, cp: (q * nt + i, cp[0]))
        got_spec = pl.BlockSpec((tr, pc), lambda i, q, cp: (q * nt + i, 0))
    return pl.pallas_call(
        body, name=name,
        grid_spec=pltpu.PrefetchScalarGridSpec(
            num_scalar_prefetch=1, grid=(nt, 4), in_specs=[g_spec, got_spec],
            out_specs=[pl.BlockSpec((None, tr, pc), lambda i, q, cp: (q, i, 0)),
                       pl.BlockSpec((None, tr, pc), lambda i, q, cp: (cp[1], i, 0))]),
        out_shape=[jax.ShapeDtypeStruct((4, pr, pc), BF16)] * 2,
        compiler_params=_cp("arbitrary", "arbitrary"),
    )(cp_arr, g, got)


def _rs_copies(ps, rb, ssem, rsem, mx, my, mc):
    p = 2 * mx + my
    out = []
    for w in range(len(ps)):
        for k, chip in enumerate(_other_chips(mx, my)):
            out.append(pltpu.make_async_remote_copy(
                src_ref=ps[w].at[2 * chip[0] + chip[1]], dst_ref=rb[w].at[p], send_sem=ssem.at[3 * w + k],
                recv_sem=rsem.at[3 * w + k], device_id=(*chip, mc), device_id_type=MESH))
    return out


def rs_start(ps, rb, after, name):
    n = len(ps)
    after = list(after)
    m = len(after)

    def body(*refs):
        ssem, rsem = refs[2 * n + m:2 * n + m + 2]
        ps_o = refs[2 * n + m + 2:3 * n + m + 2]
        rb_o = refs[3 * n + m + 2:4 * n + m + 2]
        token = refs[4 * n + m + 2]
        for cp in _rs_copies(ps_o, rb_o, ssem, rsem, *_mesh_pos()):
            cp.start()
        token[...] = jnp.zeros_like(token)

    both = list(ps) + list(rb)
    res = pl.pallas_call(
        body, name=name,
        out_shape=[pltpu.SemaphoreType.DMA((3 * n,)), pltpu.SemaphoreType.DMA((3 * n,))]
        + [pltpu.HBM(a.shape, a.dtype) for a in both] + [jax.ShapeDtypeStruct((8, 128), F32)],
        in_specs=[HBM_SPEC] * (2 * n) + [ANY] * m,
        out_specs=[SEM_SPEC, SEM_SPEC] + [HBM_SPEC] * (2 * n) + [pl.BlockSpec(memory_space=pltpu.VMEM)],
        input_output_aliases={w: 2 + w for w in range(2 * n)},
        compiler_params=pltpu.CompilerParams(has_side_effects=pltpu.SideEffectType.DATAFLOW_SIDE_EFFECTING),
    )(*[pltpu.with_memory_space_constraint(a, pltpu.HBM) for a in both], *after)
    return res[0], res[1], list(res[2:2 + n]), list(res[2 + n:2 + 2 * n]), res[2 + 2 * n]


def rs_wait(ps, rb, ssem, rsem, after, name):
    n = len(ps)
    after = list(after)
    m = len(after)

    def body(*refs):
        ps_i, rb_i = refs[:n], refs[n:2 * n]
        ssem_ref, rsem_ref = refs[2 * n], refs[2 * n + 1]
        for cp in _rs_copies(ps_i, rb_i, ssem_ref, rsem_ref, *_mesh_pos()):
            cp.wait_send()
            cp.wait_recv()

    both = list(ps) + list(rb)
    res = pl.pallas_call(
        body, name=name,
        out_shape=[pltpu.HBM(a.shape, a.dtype) for a in both],
        in_specs=[HBM_SPEC] * (2 * n) + [SEM_SPEC, SEM_SPEC] + [ANY] * m,
        out_specs=[HBM_SPEC] * (2 * n),
        input_output_aliases={w: w for w in range(2 * n)},
        compiler_params=pltpu.CompilerParams(has_side_effects=pltpu.SideEffectType.DATAFLOW_SIDE_EFFECTING),
    )(*both, ssem, rsem, *after)
    return list(res[n:])


def sum_share(parts, kind, R, C, name):
    _, pr, pc = parts.shape
    sr, sc = _shard_shape(kind, R, C)
    tr = _row_tile(pr, pc * 4, budget=4 * 1024 * 1024)
    nt = pr // tr

    def body(p_ref, fin_ref, stage, lsem, ssem, rsem):
        i = pl.program_id(0)
        slot = i % 2
        mx, my, mc = _mesh_pos()

        def region(h, t):
            r0 = pl.multiple_of(t * tr, 8)
            if kind == "col":
                return fin_ref.at[pl.ds(pl.multiple_of(h * pr + r0, 8), tr)]
            return fin_ref.at[pl.ds(r0, tr), pl.ds(h * pc, pc)]

        def copies(s, h, t):
            return (pltpu.make_async_copy(stage.at[s], region(h, t), lsem.at[s]),
                    pltpu.make_async_remote_copy(src_ref=stage.at[s], dst_ref=region(h, t), send_sem=ssem.at[s],
                                                 recv_sem=rsem, device_id=(mx, my, 1 - mc), device_id_type=MESH))

        def wait_sent(s):
            loc, rem = copies(s, 0, 0)
            loc.wait()
            rem.wait_send()

        @pl.when(i >= 2)
        def _():
            wait_sent(slot)

        acc = p_ref[0].astype(F32)
        for k in range(1, 4):
            acc = acc + p_ref[k].astype(F32)
        stage[slot] = acc
        if kind == "col":
            for cp in copies(slot, mc, i):
                cp.start()
        else:
            for h in range(2):
                @pl.when(mc == h)
                def _(h=h):
                    for cp in copies(slot, h, i):
                        cp.start()

        @pl.when(i == nt - 1)
        def _():
            wait_sent(slot)
            if nt >= 2:
                wait_sent(1 - slot)
            half = fin_ref.at[pl.ds(0, pr), pl.ds(0, pc)]
            pltpu.make_async_remote_copy(src_ref=half, dst_ref=half, send_sem=ssem.at[0], recv_sem=rsem,
                                         device_id=(mx, my, 1 - mc), device_id_type=MESH).wait_recv()

    return pl.pallas_call(
        body, name=name, grid=(nt,),
        in_specs=[pl.BlockSpec((4, tr, pc), lambda i: (0, i, 0))],
        out_specs=ANY,
        out_shape=jax.ShapeDtypeStruct((sr, sc), F32),
        scratch_shapes=[pltpu.VMEM((2, tr, pc), F32), pltpu.SemaphoreType.DMA((2,)), pltpu.SemaphoreType.DMA((2,)),
                        pltpu.SemaphoreType.DMA],
        compiler_params=_cp("arbitrary"),
    )(parts)


def _pack(parts, rows):
    flat = []
    for a in parts:
        a = jnp.ravel(a).astype(F32)
        flat.append(jnp.pad(a, (0, (-a.shape[0]) % 128)))
    v = jnp.concatenate(flat)
    return jnp.pad(v, (0, rows * 128 - v.shape[0])).reshape(rows, 128)


def _unpack(block, shapes):
    lead = block.shape[:-2]
    v = block.reshape(lead + (-1,))
    out, off = [], 0
    for shp in shapes:
        n = int(np.prod(shp))
        out.append(v[..., off:off + n].reshape(lead + tuple(shp)))
        off += n + (-n) % 128
    return out


def _block_diag4(w):
    w4 = w.reshape(4, 4, 64, 64)
    eye = jnp.eye(4, dtype=w.dtype)
    return (w4[:, :, :, None, :] * eye[None, :, None, :, None]).reshape(4, 256, 256)


def _diag_blocks(bd):
    b5 = bd.reshape(4, 4, 64, 4, 64)
    return jnp.stack([b5[:, i, :, i, :] for i in range(4)], axis=1).reshape(16, 64, 64)


def _bias_window(rel_bias):
    m = (np.arange(768) + 127) % 768 - 127
    w = rel_bias[:, np.clip(512 - m, -128, 128) + 128]
    win = jnp.tile(w, (1, 128))[:, :128 * 767].reshape(8, 128, 767)[:, :, :WIN]
    qh = np.arange(128)[:, None] // CHUNK
    kc = np.arange(WIN)[None, :] // CHUNK
    valid = (kc >= qh) & (kc <= qh + 8)
    return jnp.where(jnp.asarray(valid)[None], win, NEG)


SMALL = ("b_ada", "norm_pre", "norm_post", "rel_bias", "conv_w", "conv_b", "lru_wa", "lru_ba", "lru_wx",
         "lru_bx", "lru_lambda")
WEIGHTS = ("w_ada", "b_ada", "norm_pre", "norm_post", "ffn1_w_gu", "ffn1_w_down", "w_in", "rel_bias", "conv_w",
           "conv_b", "lru_wa", "lru_ba", "lru_wx", "lru_bx", "lru_lambda", "w_att_o", "w_rec_o", "w_out",
           "ffn2_w_gu", "ffn2_w_down")


def kernel(x, c, w_ada, b_ada, norm_pre, norm_post, ffn1_w_gu, ffn1_w_down, w_in, rel_bias, conv_w, conv_b, lru_wa, lru_ba, lru_wx, lru_bx, lru_lambda, w_att_o, w_rec_o, w_out, ffn2_w_gu, ffn2_w_down, loss_target, m_w_ada, m_b_ada, m_norm_pre, m_norm_post, m_ffn1_w_gu, m_ffn1_w_down, m_w_in, m_rel_bias, m_conv_w, m_conv_b, m_lru_wa, m_lru_ba, m_lru_wx, m_lru_bx, m_lru_lambda, m_w_att_o, m_w_rec_o, m_w_out, m_ffn2_w_gu, m_ffn2_w_down, v_w_ada, v_b_ada, v_norm_pre, v_norm_post, v_ffn1_w_gu, v_ffn1_w_down, v_w_in, v_rel_bias, v_conv_w, v_conv_b, v_lru_wa, v_lru_ba, v_lru_wx, v_lru_bx, v_lru_lambda, v_w_att_o, v_w_rec_o, v_w_out, v_ffn2_w_gu, v_ffn2_w_down):
    W = dict(w_ada=w_ada, b_ada=b_ada, norm_pre=norm_pre, norm_post=norm_post, ffn1_w_gu=ffn1_w_gu,
             ffn1_w_down=ffn1_w_down, w_in=w_in, rel_bias=rel_bias, conv_w=conv_w, conv_b=conv_b, lru_wa=lru_wa,
             lru_ba=lru_ba, lru_wx=lru_wx, lru_bx=lru_bx, lru_lambda=lru_lambda, w_att_o=w_att_o, w_rec_o=w_rec_o,
             w_out=w_out, ffn2_w_gu=ffn2_w_gu, ffn2_w_down=ffn2_w_down)
    M = dict(w_ada=m_w_ada, b_ada=m_b_ada, norm_pre=m_norm_pre, norm_post=m_norm_post, ffn1_w_gu=m_ffn1_w_gu,
             ffn1_w_down=m_ffn1_w_down, w_in=m_w_in, rel_bias=m_rel_bias, conv_w=m_conv_w, conv_b=m_conv_b,
             lru_wa=m_lru_wa, lru_ba=m_lru_ba, lru_wx=m_lru_wx, lru_bx=m_lru_bx, lru_lambda=m_lru_lambda,
             w_att_o=m_w_att_o, w_rec_o=m_w_rec_o, w_out=m_w_out, ffn2_w_gu=m_ffn2_w_gu, ffn2_w_down=m_ffn2_w_down)
    V = dict(w_ada=v_w_ada, b_ada=v_b_ada, norm_pre=v_norm_pre, norm_post=v_norm_post, ffn1_w_gu=v_ffn1_w_gu,
             ffn1_w_down=v_ffn1_w_down, w_in=v_w_in, rel_bias=v_rel_bias, conv_w=v_conv_w, conv_b=v_conv_b,
             lru_wa=v_lru_wa, lru_ba=v_lru_ba, lru_wx=v_lru_wx, lru_bx=v_lru_bx, lru_lambda=v_lru_lambda,
             w_att_o=v_w_att_o, w_rec_o=v_w_rec_o, w_out=v_w_out, ffn2_w_gu=v_ffn2_w_gu, ffn2_w_down=v_ffn2_w_down)
    mx, my, mc = _mesh_pos()
    p = 2 * mx + my
    e = 4 * mx + 2 * my + mc
    xs = x[0]

    c_arr = jnp.reshape(mc, (1,)).astype(jnp.int32)
    cp_arr = jnp.stack([mc, p]).astype(jnp.int32)
    p_arr = jnp.reshape(p, (1,)).astype(jnp.int32)
    direct = ("w_att_o", "w_rec_o", "w_out", "ffn2_w_gu", "ffn2_w_down")
    geoms = [(kind, R, C, n in direct) for (n, kind, R, C) in BIG]
    names = [b[0] for b in BIG]
    placed = [ag_local(W[n][0], kind, R, C, p_arr, "ag_local_" + n) for (n, kind, R, C) in BIG[:2]]

    def arrived(fly, lo, hi, ssem, rsem, after, tag):
        done = ag_wait(fly, geoms[lo:hi], ssem, rsem, after, "ag_wait_" + tag)
        return [a if both else ag_forward(a, kind, R, C, "ag_forward_" + n)
                for a, (kind, R, C, both), n in zip(done, geoms[lo:hi], names[lo:hi])]

    g1 = ag_small(_pack([c, norm_pre, norm_post, conv_w], 32), "ag_small_params")
    c_all, npre4, npost4, cw4 = _unpack(g1, [(D,), (3, 256), (3, 256), (4, 256)])
    chipwise = lambda a: jnp.moveaxis(a[0::2], 0, 1).reshape(a.shape[1], D)
    npre, npost, conv_full = chipwise(npre4), chipwise(npost4), chipwise(cw4)

    b_cols = lax.dynamic_slice(b_ada, (0, p * 2304), (1, 2304))
    mod_cols = ada_fwd(c_all, w_ada[0], b_cols, "ada_fwd")
    g2 = ag_small(mod_cols.reshape(144, 128), "ag_mod")
    mod_all = jnp.moveaxis(g2[0::2].reshape(4, 8, 2304), 0, 1).reshape(8, 9 * D)
    mod = lax.dynamic_index_in_dim(mod_all, e, 0, keepdims=False).reshape(3, 3, D)
    zeros3 = jnp.zeros((3, D), F32)
    vecs = [jnp.concatenate([npre[k:k + 1], npost[k:k + 1], mod[k], zeros3], axis=0) for k in range(3)]

    gu_s, gu_r, gu_fly, tok_gu = ag_start(placed[:1], geoms[:1], [g2], "ag_start_ffn1_gu")
    dn_s, dn_r, dn_fly, tok0 = ag_start(placed[1:2], geoms[1:2], [tok_gu], "ag_start_ffn1_down")
    placed += [ag_local(W[n][0], kind, R, C, p_arr, "ag_local_" + n, after=[tok0]) for (n, kind, R, C) in BIG[2:]]
    f1_gu, = arrived(gu_fly, 0, 1, gu_s, gu_r, placed[2:], "ffn1_gu")
    f1_dn, = arrived(dn_fly, 1, 2, dn_s, dn_r, f1_gu, "ffn1_down")
    mix_s, mix_r, mix_fly, tok1 = ag_start(placed[2:6], geoms[2:6], [f1_gu, f1_dn], "ag_start_mixer")
    ffn_s, ffn_r, ffn_fly, tok2 = ag_start(placed[6:], geoms[6:], [tok1], "ag_start_ffn2")
    wa_bd = _block_diag4(lru_wa[0]).astype(BF16)
    wx_bd = _block_diag4(lru_wx[0]).astype(BF16)
    pvec = jnp.concatenate([conv_full, conv_b, lru_ba, lru_bx, lru_lambda], axis=0)
    bias = _bias_window(rel_bias[0]).reshape(4, 256, WIN)

    f1_u = f1_gu[:, FF:]
    x1, h1, g1_, u1, a1, f1 = ffn_fwd(xs, vecs[0] + tok2[0:1, 0:1], f1_gu, f1_u, f1_dn, 0.5, "ffn1_fwd")
    win, wao, wro, wout = arrived(mix_fly, 2, 6, mix_s, mix_r, x1, "mixer")
    h2, qkv, rest = proj_fwd(x1, vecs[1], win, "proj_fwd")
    ao = attn_fwd(qkv, bias, "attn_fwd")
    hl, hg = lru_fwd(rest, pvec, wa_bd, wx_bd, "lru_fwd")
    x2, att, rec, mg, f2 = mix_out_fwd(x1, ao, hg, rest, vecs[1], wao, wro, wout, "mix_out_fwd")
    f2_gu, f2_dn = arrived(ffn_fly, 6, 8, ffn_s, ffn_r, x2, "ffn2")
    f2_u = f2_gu[:, FF:]
    dy, h3, g3_, u3, a3, f3, lvec = ffn_fwd(x2, vecs[2], f2_gu, f2_u, f2_dn, 0.5, "ffn2_fwd", tgt=loss_target[0])

    G, grads = {}, {}
    geo = {n: (kind, R, C) for (n, kind, R, C) in BIG}

    def reduce_begin(names, tag):
        ps, rb = [], []
        for n in names:
            got = pair_push(G[n], geo[n][0], c_arr, "rs_push_" + n)
            a, b = pair_add(G[n], got, geo[n][0], cp_arr, "rs_pair_sum_" + n)
            ps.append(a)
            rb.append(b)
        return rs_start(ps, rb, [], "rs_start_" + tag)

    def reduce_end(names, flight, after, tag):
        ssem, rsem, ps, rb, _ = flight
        for a, n in zip(rs_wait(ps, rb, ssem, rsem, after, "rs_wait_" + tag), names):
            grads[n] = sum_share(a, *geo[n], "rs_sum_share_" + n)[None]

    dx2, df3, dgu3, va2 = ffn_bwd(dy, x2, f3, g3_, u3, vecs[2], f2_gu, f2_u, f2_dn, 0.5, "ffn2_bwd")
    G["ffn2_w_gu"] = mm_tn(h3, dgu3, "dw_ffn2_gu", D, 1408, 2048)
    G["ffn2_w_down"] = mm_tn(a3, df3, "dw_ffn2_down", 1408, D, 2048)
    fly_ffn2 = reduce_begin(("ffn2_w_gu", "ffn2_w_down"), "ffn2")
    vec1 = vecs[1] + fly_ffn2[4][0:1, 0:1]
    df2, d_att, d_rec, dao, dhl, d3, va_out = mix_out_bwd(dx2, f2, att, rec, rest, hl, vec1, wao, wro, wout,
                                                          "mix_out_bwd")
    G["w_out"] = mm_tn(mg, df2, "dw_out", D, D, 1024)
    G["w_att_o"] = mm_tn(ao, d_att, "dw_att_o", 512, D, 1024)
    G["w_rec_o"] = mm_tn(hg, d_rec, "dw_rec_o", D, D, 1024)
    dq, db, dkv = attn_bwd(qkv, dao, bias, "attn_bwd")
    dxr, v_lru, dwa_bd, dwx_bd = lru_bwd(dhl, hl, rest, pvec, wa_bd, wx_bd, "lru_bwd")
    dx1, va_in = proj_bwd(dq, dkv, dxr, d3, win, x1, dx2, vecs[1], "proj_bwd")
    G["w_in"] = dw_in(h2, dq, dkv, dxr, d3, "dw_in")
    fly_mix = reduce_begin(("w_in", "w_att_o", "w_rec_o", "w_out"), "mixer")
    vec0 = vecs[0] + fly_mix[4][0:1, 0:1]
    dx0, df1, dgu1, va0 = ffn_bwd(dx1, xs, f1, g1_, u1, vec0, f1_gu, f1_u, f1_dn, 0.5, "ffn1_bwd")
    G["ffn1_w_gu"] = mm_tn(h1, dgu1, "dw_ffn1_gu", D, 1408, 2048)
    G["ffn1_w_down"] = mm_tn(a1, df1, "dw_ffn1_down", 1408, D, 2048)
    fly_ffn1 = reduce_begin(("ffn1_w_gu", "ffn1_w_down"), "ffn1")
    reduce_end(("ffn2_w_gu", "ffn2_w_down"), fly_ffn2, [fly_ffn1[4]], "ffn2")
    reduce_end(("w_in", "w_att_o", "w_rec_o", "w_out"), fly_mix, [fly_ffn1[4], grads["ffn2_w_down"]], "mixer")

    va1 = va_out + va_in
    vas = (va0, va1, va2)
    dmod = jnp.stack([v[2:5] for v in vas])
    part = {"b_ada": dmod, "norm_pre": jnp.stack([v[0] for v in vas]), "norm_post": jnp.stack([v[1] for v in vas]),
            "rel_bias": bias_grad(db.reshape(8, 128, WIN), "bias_grad")[:, :257], "conv_w": v_lru[0:4], "conv_b": v_lru[4],
            "lru_wa": _diag_blocks(dwa_bd), "lru_ba": v_lru[5], "lru_wx": _diag_blocks(dwx_bd), "lru_bx": v_lru[6],
            "lru_lambda": v_lru[7]}
    full_shapes = {"b_ada": (9 * D,), "norm_pre": (3, D), "norm_post": (3, D), "rel_bias": (8, 257),
                   "conv_w": (4, D), "conv_b": (D,), "lru_wa": (16, 64, 64), "lru_ba": (D,),
                   "lru_wx": (16, 64, 64), "lru_bx": (D,), "lru_lambda": (D,)}
    g3 = ag_small(_pack([part[n] for n in SMALL] + [lvec[0:1, 0:1]], 1232), "ag_small_grads")
    summed = _unpack(sum_lead(g3, "sum_small_grads"), [full_shapes[n] for n in SMALL] + [(1,)])
    red = dict(zip(SMALL, summed[:-1]))
    loss = summed[-1][0]
    cols = lambda a: lax.dynamic_slice(a, (0, p * 256), (a.shape[0], 256))
    grads.update({"b_ada": red["b_ada"][None], "norm_pre": cols(red["norm_pre"])[None],
                  "norm_post": cols(red["norm_post"])[None], "rel_bias": red["rel_bias"][None],
                  "conv_w": cols(red["conv_w"])[None], "conv_b": red["conv_b"][None], "lru_wa": red["lru_wa"][None],
                  "lru_ba": red["lru_ba"][None], "lru_wx": red["lru_wx"][None], "lru_bx": red["lru_bx"][None],
                  "lru_lambda": red["lru_lambda"][None]})

    dmod_all = g3[:, :72].reshape(8, 9 * D)
    dmod_cols = jnp.pad(lax.dynamic_slice(dmod_all, (0, p * 2304), (8, 2304)), ((0, 120), (0, 0)))
    c_all_t = jnp.pad(c_all.T, ((0, 0), (0, 120)))
    grads["w_ada"] = ada_bwd(c_all_t, dmod_cols, "ada_bwd")[None]

    delta, new_m, new_v = {}, {}, {}

    def update(n):
        shp = W[n].shape
        res = adamw(W[n][0], grads[n][0], M[n][0], V[n][0], "adamw_" + n, emit_g=n in geo)
        delta[n], new_m[n], new_v[n] = [a.reshape(shp) for a in res[:3]]
        if n in geo:
            grads[n] = res[3].reshape(shp)

    for n in ("w_ada", "ffn2_w_gu", "ffn2_w_down", "w_in", "w_att_o", "w_rec_o", "w_out"):
        update(n)
    packed = [_pack([src[n] for n in SMALL], 1168) for src in (W, grads, M, V)]
    outs = adamw(*packed, "adamw_small")
    for dst, blk in zip((delta, new_m, new_v), outs):
        for n, a in zip(SMALL, _unpack(blk, [W[n].shape for n in SMALL])):
            dst[n] = a
    reduce_end(("ffn1_w_gu", "ffn1_w_down"), fly_ffn1,
               [outs[0], delta["w_ada"], delta["ffn2_w_gu"], delta["ffn2_w_down"], delta["w_in"], delta["w_out"]], "ffn1")
    for n in ("ffn1_w_gu", "ffn1_w_down"):
        update(n)

    return (loss, dx0[None], *[grads[n] for n in WEIGHTS], *[delta[n] for n in WEIGHTS],
            *[new_m[n] for n in WEIGHTS], *[new_v[n] for n in WEIGHTS])
```

```python
import functools

import numpy as np
import jax
import jax.numpy as jnp
from jax import lax
from jax.experimental import pallas as pl
from jax.experimental.pallas import tpu as pltpu

F32 = jnp.float32
BF16 = jnp.bfloat16

D = 1024
FF = 2816
PW = 5632
HP = 128
CHUNK = 64
WIN = 640
TQ = 512
EPS = 1e-6
NEG = -1e30
LRU_C = 8.0
N_DEV = 8
VMEM_LIMIT = 56 * 1024 * 1024

ADAM_LR, ADAM_B1, ADAM_B2, ADAM_EPS, ADAM_WD, ADAM_STEP = 0.001, 0.9, 0.999, 1e-08, 0.01, 10

MESH = pl.DeviceIdType.MESH
ANY = pl.BlockSpec(memory_space=pl.ANY)


def _cp(*sem):
    return pltpu.CompilerParams(dimension_semantics=tuple(sem), vmem_limit_bytes=VMEM_LIMIT)


def _dot(a, b):
    return jnp.dot(a, b, preferred_element_type=F32)


def _dot_nt(a, b):
    return lax.dot_general(a, b, (((1,), (1,)), ((), ())), preferred_element_type=F32)


def _dot_tn(a, b):
    return lax.dot_general(a, b, (((0,), (0,)), ((), ())), preferred_element_type=F32)


def _mean(v):
    return jnp.mean(v, axis=-1, keepdims=True)


def _colsum(v):
    return jnp.sum(v, axis=0, keepdims=True)


def _sigmoid(v):
    return 0.5 * jnp.tanh(0.5 * v) + 0.5


_GK = 0.7978845608028654


def _gelu(v):
    t = jnp.tanh(_GK * (v + 0.044715 * v * v * v))
    return 0.5 * v * (1.0 + t)


def _pre_norm(xv, vec_ref):
    r = lax.rsqrt(_mean(xv * xv) + EPS)
    n = xv * r * vec_ref[0:1, :]
    return n * (1.0 + vec_ref[3:4, :]) + vec_ref[2:3, :]


def _pre_norm_bwd(dh, xv, dres, vec_ref, vacc_ref):
    r = lax.rsqrt(_mean(xv * xv) + EPS)
    xh = xv * r
    n = xh * vec_ref[0:1, :]
    vacc_ref[2:3, :] += _colsum(dh)
    vacc_ref[3:4, :] += _colsum(dh * n)
    dn = dh * (1.0 + vec_ref[3:4, :])
    vacc_ref[0:1, :] += _colsum(dn * xh)
    dxh = dn * vec_ref[0:1, :]
    return r * (dxh - xh * _mean(dxh * xh)) + dres


def _post_norm_bwd(dxo, fv, res, vec_ref, vacc_ref):
    rf = lax.rsqrt(_mean(fv * fv) + EPS)
    fh = fv * rf
    gp = vec_ref[1:2, :]
    vacc_ref[4:5, :] += _colsum(res * dxo * (fh * gp))
    dy = (res * vec_ref[4:5, :]) * dxo
    vacc_ref[1:2, :] += _colsum(dy * fh)
    dfn = dy * gp
    return rf * (dfn - fh * _mean(dfn * fh))


def ffn_fwd(x, vec, w_gu, w_u, w_dn, res, name, tgt=None, tm=1024, tf=512):
    S = x.shape[0]
    tm = min(tm, S)
    nt = S // tm
    nf = -(-FF // tf)
    tail = FF - tf * (nf - 1)
    halves = [pl.ds(r * (tm // 2), tm // 2) for r in range(2)]
    head = tgt is not None

    def body(*refs):
        x_ref, vec_ref, wg_ref, wu_ref, wd_ref = refs[:5]
        if head:
            t_ref, xo_ref, h_ref, g_ref, u_ref, a_ref, f_ref, l_ref, hs, acc, lacc = refs[5:]
        else:
            xo_ref, h_ref, g_ref, u_ref, a_ref, f_ref, hs, acc = refs[5:]
        i, j = pl.program_id(0), pl.program_id(1)

        @pl.when(j == 0)
        def _():
            h = _pre_norm(x_ref[...], vec_ref).astype(BF16)
            hs[...] = h
            h_ref[...] = h
            acc[...] = jnp.zeros_like(acc)

        def chunk(w):
            gu = [(_dot(hs[r, :], wg_ref[:, 0:w]), _dot(hs[r, :], wu_ref[:, 0:w])) for r in halves]
            acts = []
            for r, (g, u) in zip(halves, gu):
                g_ref[r, 0:w] = g.astype(BF16)
                u_ref[r, 0:w] = u.astype(BF16)
                a = (g * _sigmoid(g) * u).astype(BF16)
                a_ref[r, 0:w] = a
                acts.append(a)
            for r, a in zip(halves, acts):
                acc[r, :] += _dot(a, wd_ref[0:w, :])

        @pl.when(j < nf - 1)
        def _():
            chunk(tf)

        @pl.when(j == nf - 1)
        def _():
            chunk(tail)
            f = acc[...]
            f_ref[...] = f.astype(BF16)
            y = f * lax.rsqrt(_mean(f * f) + EPS) * vec_ref[1:2, :]
            xo = x_ref[...] + (res * vec_ref[4:5, :]) * y
            if head:
                @pl.when(i == 0)
                def _():
                    lacc[...] = jnp.zeros_like(lacc)

                d = xo - t_ref[...]
                xo_ref[...] = d * (1.0 / D)
                lacc[...] += _colsum(d * d)

                @pl.when(i == nt - 1)
                def _():
                    l_ref[...] = jnp.broadcast_to(0.5 * jnp.sum(lacc[...]) * (1.0 / D), (8, 128))
            else:
                xo_ref[...] = xo

    row = lambda i, j: (i, 0)
    col = lambda i, j: (i, j)
    once = dict(pipeline_mode=pl.Buffered(1)) if head else {}
    in_specs = [pl.BlockSpec((tm, D), row, **once), pl.BlockSpec((8, D), lambda i, j: (0, 0)),
                pl.BlockSpec((D, tf), lambda i, j: (0, j)), pl.BlockSpec((D, tf), lambda i, j: (0, j)),
                pl.BlockSpec((tf, D), lambda i, j: (j, 0))]
    out_specs = [pl.BlockSpec((tm, D), row), pl.BlockSpec((tm, D), row), pl.BlockSpec((tm, tf), col),
                 pl.BlockSpec((tm, tf), col), pl.BlockSpec((tm, tf), col), pl.BlockSpec((tm, D), row)]
    out_shape = [jax.ShapeDtypeStruct((S, D), F32), jax.ShapeDtypeStruct((S, D), BF16),
                 jax.ShapeDtypeStruct((S, FF), BF16), jax.ShapeDtypeStruct((S, FF), BF16),
                 jax.ShapeDtypeStruct((S, FF), BF16), jax.ShapeDtypeStruct((S, D), BF16)]
    scratch = [pltpu.VMEM((tm, D), BF16), pltpu.VMEM((tm, D), F32)]
    args = [x, vec, w_gu, w_u, w_dn]
    if head:
        in_specs.append(pl.BlockSpec((tm, D), row, **once))
        out_specs.append(pl.BlockSpec((8, 128), lambda i, j: (0, 0)))
        out_shape.append(jax.ShapeDtypeStruct((8, 128), F32))
        scratch.append(pltpu.VMEM((1, D), F32))
        args.append(tgt)
    return pl.pallas_call(
        body, name=name, grid=(nt, nf), in_specs=in_specs, out_specs=out_specs, out_shape=out_shape,
        scratch_shapes=scratch,
        compiler_params=_cp("arbitrary" if head else "parallel", "arbitrary"),
    )(*args)


def ffn_bwd(dxo, x, f, g, u, vec, w_gu, w_u, w_dn, res, name, tm=1024, tf=512):
    S = x.shape[0]
    tm = min(tm, S)
    nf = -(-FF // tf)
    tail = FF - tf * (nf - 1)
    halves = [pl.ds(r * (tm // 2), tm // 2) for r in range(2)]

    def body(dxo_ref, x_ref, f_ref, g_ref, u_ref, vec_ref, wg_ref, wu_ref, wd_ref,
             dx_ref, df_ref, dgu_ref, vacc_ref, dfs, acc):
        i, j = pl.program_id(0), pl.program_id(1)

        @pl.when((i == 0) & (j == 0))
        def _():
            vacc_ref[...] = jnp.zeros_like(vacc_ref)

        @pl.when(j == 0)
        def _():
            df = _post_norm_bwd(dxo_ref[...], f_ref[...].astype(F32), res, vec_ref, vacc_ref).astype(BF16)
            dfs[...] = df
            df_ref[...] = df
            acc[...] = jnp.zeros_like(acc)

        def chunk(w):
            da = [_dot_nt(dfs[h, :], wd_ref[0:w, :]) for h in halves]
            dgu = []
            for h, d in zip(halves, da):
                gv, uv = g_ref[h, 0:w].astype(F32), u_ref[h, 0:w].astype(F32)
                sg = _sigmoid(gv)
                dg = (d * uv * (sg * (1.0 + gv * (1.0 - sg)))).astype(BF16)
                du = (d * (gv * sg)).astype(BF16)
                dgu_ref[0, h, 0:w] = dg
                dgu_ref[1, h, 0:w] = du
                dgu.append((dg, du))
            for h, (dg, du) in zip(halves, dgu):
                acc[h, :] += _dot_nt(dg, wg_ref[:, 0:w]) + _dot_nt(du, wu_ref[:, 0:w])

        @pl.when(j < nf - 1)
        def _():
            chunk(tf)

        @pl.when(j == nf - 1)
        def _():
            chunk(tail)
            dx_ref[...] = _pre_norm_bwd(acc[...], x_ref[...], dxo_ref[...], vec_ref, vacc_ref)

    row = lambda i, j: (i, 0)
    col = lambda i, j: (i, j)
    return pl.pallas_call(
        body, name=name, grid=(S // tm, nf),
        in_specs=[pl.BlockSpec((tm, D), row), pl.BlockSpec((tm, D), row, pipeline_mode=pl.Buffered(1)),
                  pl.BlockSpec((tm, D), row)]
        + [pl.BlockSpec((tm, tf), col), pl.BlockSpec((tm, tf), col),
                  pl.BlockSpec((8, D), lambda i, j: (0, 0)),
                  pl.BlockSpec((D, tf), lambda i, j: (0, j)), pl.BlockSpec((D, tf), lambda i, j: (0, j)),
                  pl.BlockSpec((tf, D), lambda i, j: (j, 0))],
        out_specs=[pl.BlockSpec((tm, D), row), pl.BlockSpec((tm, D), row),
                   pl.BlockSpec((2, tm, tf), lambda i, j: (0, i, j)),
                   pl.BlockSpec((8, D), lambda i, j: (0, 0))],
        out_shape=[jax.ShapeDtypeStruct((S, D), F32), jax.ShapeDtypeStruct((S, D), BF16),
                   jax.ShapeDtypeStruct((2, S, FF), BF16), jax.ShapeDtypeStruct((8, D), F32)],
        scratch_shapes=[pltpu.VMEM((tm, D), BF16), pltpu.VMEM((tm, D), F32)],
        compiler_params=_cp("arbitrary", "arbitrary"),
    )(dxo, x, f, g, u, vec, w_gu, w_u, w_dn)


def mm_tn(a, b, name, tm, tn, tk, out_dtype=BF16):
    S, M = a.shape
    if b.ndim == 3:
        G, _, Nf = b.shape
    else:
        G, Nf = 1, b.shape[1]
    N = G * Nf
    tk = min(tk, S)
    nbf = Nf // tn
    nk = S // tk

    def body(a_ref, b_ref, o_ref, acc):
        k = pl.program_id(2)

        @pl.when(k == 0)
        def _():
            acc[...] = jnp.zeros_like(acc)

        acc[...] += _dot_tn(a_ref[...], b_ref[...])

        @pl.when(k == nk - 1)
        def _():
            o_ref[...] = acc[...].astype(out_dtype)

    if b.ndim == 3:
        b_spec = pl.BlockSpec((None, tk, tn), lambda i, j, k: (j // nbf, k, j % nbf))
    else:
        b_spec = pl.BlockSpec((tk, tn), lambda i, j, k: (k, j))
    return pl.pallas_call(
        body, name=name, grid=(M // tm, N // tn, nk),
        in_specs=[pl.BlockSpec((tk, tm), lambda i, j, k: (k, i)), b_spec],
        out_specs=pl.BlockSpec((tm, tn), lambda i, j, k: (i, j)),
        out_shape=jax.ShapeDtypeStruct((M, N), out_dtype),
        scratch_shapes=[pltpu.VMEM((tm, tn), F32)],
        compiler_params=_cp("parallel", "parallel", "arbitrary"),
    )(a, b)


def proj_fwd(x, vec, w_in, name, tm=2048, tn=512):
    S = x.shape[0]
    tm = min(tm, S)
    nq = 1536 // tn

    def body(x_ref, vec_ref, w_ref, h_ref, qkv_ref, rest_ref, hs):
        j = pl.program_id(1)

        @pl.when(j == 0)
        def _():
            h = _pre_norm(x_ref[...], vec_ref).astype(BF16)
            hs[...] = h
            h_ref[...] = h

        r = _dot(hs[...], w_ref[...])

        @pl.when(j < nq)
        def _():
            qkv_ref[...] = r.astype(BF16)

        @pl.when(j >= nq)
        def _():
            rest_ref[...] = r.astype(BF16)

    row = lambda i, j: (i, 0)
    return pl.pallas_call(
        body, name=name, grid=(S // tm, PW // tn),
        in_specs=[pl.BlockSpec((tm, D), row), pl.BlockSpec((8, D), lambda i, j: (0, 0)),
                  pl.BlockSpec((D, tn), lambda i, j: (0, j))],
        out_specs=[pl.BlockSpec((tm, D), row),
                   pl.BlockSpec((tm, tn), lambda i, j: (i, jnp.minimum(j, nq - 1))),
                   pl.BlockSpec((tm, tn), lambda i, j: (i, jnp.maximum(j - nq, 0)))],
        out_shape=[jax.ShapeDtypeStruct((S, D), BF16), jax.ShapeDtypeStruct((S, 1536), BF16),
                   jax.ShapeDtypeStruct((S, 4096), BF16)],
        scratch_shapes=[pltpu.VMEM((tm, D), BF16)],
        compiler_params=_cp("parallel", "arbitrary"),
    )(x, vec, w_in)


def proj_bwd(dq, dkv, dxr, d3, w_in, x, dxo, vec, name, tm=1024, tk=512):
    S = x.shape[0]
    tm = min(tm, S)
    nk = PW // tk

    def body(dq_ref, dkv_ref, dxr_ref, d3_ref, w_ref, x_ref, dxo_ref, vec_ref, dx_ref, vacc_ref, acc):
        i, j = pl.program_id(0), pl.program_id(1)

        @pl.when((i == 0) & (j == 0))
        def _():
            vacc_ref[...] = jnp.zeros_like(vacc_ref)

        @pl.when(j == 0)
        def _():
            acc[...] = _dot_nt(dq_ref[...], w_ref[...])

        @pl.when((j >= 1) & (j < 3))
        def _():
            acc[...] += _dot_nt(dkv_ref[...], w_ref[...])

        @pl.when((j >= 3) & (j < 5))
        def _():
            acc[...] += _dot_nt(dxr_ref[...], w_ref[...])

        @pl.when(j >= 5)
        def _():
            acc[...] += _dot_nt(d3_ref[...], w_ref[...])

        @pl.when(j == nk - 1)
        def _():
            dx_ref[...] = _pre_norm_bwd(acc[...], x_ref[...], dxo_ref[...], vec_ref, vacc_ref)

    row = lambda i, j: (i, 0)
    return pl.pallas_call(
        body, name=name, grid=(S // tm, nk),
        in_specs=[pl.BlockSpec((None, tm, tk), lambda i, j: (0, i, 0)),
                  pl.BlockSpec((None, tm, tk), lambda i, j: (jnp.clip(j - 1, 0, 1), i, 0)),
                  pl.BlockSpec((tm, tk), lambda i, j: (i, jnp.clip(j - 3, 0, 1))),
                  pl.BlockSpec((None, tm, tk), lambda i, j: (jnp.clip(j - 5, 0, 5) // 2, i, jnp.clip(j - 5, 0, 5) % 2)),
                  pl.BlockSpec((D, tk), lambda i, j: (0, j)),
                  pl.BlockSpec((tm, D), row), pl.BlockSpec((tm, D), row),
                  pl.BlockSpec((8, D), lambda i, j: (0, 0))],
        out_specs=[pl.BlockSpec((tm, D), row), pl.BlockSpec((8, D), lambda i, j: (0, 0))],
        out_shape=[jax.ShapeDtypeStruct((S, D), F32), jax.ShapeDtypeStruct((8, D), F32)],
        scratch_shapes=[pltpu.VMEM((tm, D), F32)],
        compiler_params=_cp("arbitrary", "arbitrary"),
    )(dq, dkv, dxr, d3, w_in, x, dxo, vec)


def _two_heads(v, lane):
    zero = jnp.zeros((), v.dtype)
    return jnp.concatenate([jnp.where(lane < 64, v, zero), jnp.where(lane >= 64, v, zero)], axis=0)


def _attn_scores(qm, ka, bias_h, i, grp):
    s = _dot_nt(qm, ka) + bias_h
    col = lax.broadcasted_iota(jnp.int32, s.shape, 1)
    first_key = jnp.where(i == 0, 512 - 128 * grp, 0)
    return jnp.where(col >= first_key, s, NEG)


def _softmax(s):
    e = jnp.exp(s - jnp.max(s, axis=-1, keepdims=True))
    return e * (1.0 / jnp.sum(e, axis=-1, keepdims=True))


NG = TQ // 128


def attn_fwd(qkv, bias, name):
    S = qkv.shape[0]
    nb = S // TQ

    def body(q_ref, kp_ref, kc_ref, vp_ref, vc_ref, b_ref, o_ref, kw, vw):
        i = pl.program_id(1)
        kw[0:TQ, :] = kp_ref[...]
        kw[TQ:2 * TQ, :] = kc_ref[...]
        vw[0:TQ, :] = vp_ref[...]
        vw[TQ:2 * TQ, :] = vc_ref[...]
        lane = lax.broadcasted_iota(jnp.int32, (1, HP), 1)

        rows = [pl.ds(128 * a, 128) for a in range(NG)]
        keys = [pl.ds(128 * a, WIN) for a in range(NG)]
        q2 = [_two_heads(q_ref[r, :] * jnp.asarray(0.125, BF16), lane) for r in rows]
        s = [_attn_scores(q2[a], kw[keys[a], :], b_ref[...], i, a) for a in range(NG)]
        p = [_softmax(sa).astype(BF16) for sa in s]
        o2 = [_dot(p[a], vw[keys[a], :]) for a in range(NG)]
        for a in range(NG):
            o_ref[rows[a], :] = jnp.where(lane < 64, o2[a][0:128], o2[a][128:256]).astype(BF16)

    prev = lambda h, i: (jnp.maximum(i - 1, 0), 0)
    return pl.pallas_call(
        body, name=name, grid=(4, nb),
        in_specs=[pl.BlockSpec((TQ, HP), lambda h, i: (i, h)),
                  pl.BlockSpec((TQ, HP), lambda h, i: (jnp.maximum(i - 1, 0), 4 + h)),
                  pl.BlockSpec((TQ, HP), lambda h, i: (i, 4 + h)),
                  pl.BlockSpec((TQ, HP), lambda h, i: (jnp.maximum(i - 1, 0), 8 + h)),
                  pl.BlockSpec((TQ, HP), lambda h, i: (i, 8 + h)),
                  pl.BlockSpec((None, 256, WIN), lambda h, i: (h, 0, 0))],
        out_specs=pl.BlockSpec((TQ, HP), lambda h, i: (i, h)),
        out_shape=jax.ShapeDtypeStruct((S, 512), BF16),
        scratch_shapes=[pltpu.VMEM((2 * TQ, HP), BF16), pltpu.VMEM((2 * TQ, HP), BF16)],
        compiler_params=_cp("parallel", "arbitrary"),
    )(qkv, qkv, qkv, qkv, qkv, bias)


def attn_bwd(qkv, do, bias, name):
    S = qkv.shape[0]
    nb = S // TQ

    def body(q_ref, kp_ref, kc_ref, vp_ref, vc_ref, do_ref, b_ref, dqkv_ref, db_ref, dkv_ref, kw, vw, ak, av):
        i = pl.program_id(1)

        @pl.when(i == 0)
        def _():
            db_ref[...] = jnp.zeros_like(db_ref)
            ak[...] = jnp.zeros_like(ak)
            av[...] = jnp.zeros_like(av)

        @pl.when(i > 0)
        def _():
            ak[0:TQ, :] = ak[TQ:2 * TQ, :]
            av[0:TQ, :] = av[TQ:2 * TQ, :]
            ak[TQ:2 * TQ, :] = jnp.zeros((TQ, HP), F32)
            av[TQ:2 * TQ, :] = jnp.zeros((TQ, HP), F32)

        @pl.when(i < nb)
        def _():
            kw[0:TQ, :] = kp_ref[...]
            kw[TQ:2 * TQ, :] = kc_ref[...]
            vw[0:TQ, :] = vp_ref[...]
            vw[TQ:2 * TQ, :] = vc_ref[...]
            lane = lax.broadcasted_iota(jnp.int32, (1, HP), 1)

            rows = [pl.ds(128 * a, 128) for a in range(NG)]
            keys = [pl.ds(128 * a, WIN) for a in range(NG)]
            q2 = [_two_heads(q_ref[r, :] * jnp.asarray(0.125, BF16), lane) for r in rows]
            do2 = [_two_heads(do_ref[r, :], lane) for r in rows]
            s = [_attn_scores(q2[a], kw[keys[a], :], b_ref[...], i, a) for a in range(NG)]
            dp = [_dot_nt(do2[a], vw[keys[a], :]) for a in range(NG)]
            p = [_softmax(sa) for sa in s]
            ds = [p[a] * (dp[a] - jnp.sum(p[a] * dp[a], axis=-1, keepdims=True)) for a in range(NG)]
            db_ref[...] += (ds[0] + ds[1]) + (ds[2] + ds[3])
            dsb = [d.astype(BF16) for d in ds]
            dq2 = [_dot(dsb[a], kw[keys[a], :]) for a in range(NG)]
            dk = [_dot_tn(dsb[a], q2[a]) for a in range(NG)]
            dv = [_dot_tn(p[a].astype(BF16), do2[a]) for a in range(NG)]
            for a in range(NG):
                ak[keys[a], :] += dk[a]
                av[keys[a], :] += dv[a]
                dq = jnp.where(lane < 64, dq2[a][0:128], dq2[a][128:256])
                dqkv_ref[0, rows[a], :] = (dq * 0.125).astype(BF16)

        @pl.when(i > 0)
        def _():
            dkv_ref[0] = ak[0:TQ, :].astype(BF16)
            dkv_ref[1] = av[0:TQ, :].astype(BF16)

    cur = lambda i: jnp.minimum(i, nb - 1)
    prv = lambda i: jnp.clip(i - 1, 0, nb - 1)
    dq, db, dkv = pl.pallas_call(
        body, name=name, grid=(4, nb + 1),
        in_specs=[pl.BlockSpec((TQ, HP), lambda h, i: (cur(i), h)),
                  pl.BlockSpec((TQ, HP), lambda h, i: (prv(i), 4 + h)),
                  pl.BlockSpec((TQ, HP), lambda h, i: (cur(i), 4 + h)),
                  pl.BlockSpec((TQ, HP), lambda h, i: (prv(i), 8 + h)),
                  pl.BlockSpec((TQ, HP), lambda h, i: (cur(i), 8 + h)),
                  pl.BlockSpec((TQ, HP), lambda h, i: (cur(i), h)),
                  pl.BlockSpec((None, 256, WIN), lambda h, i: (h, 0, 0))],
        out_specs=[pl.BlockSpec((1, TQ, HP), lambda h, i: (0, cur(i), h)),
                   pl.BlockSpec((None, 256, WIN), lambda h, i: (h, 0, 0)),
                   pl.BlockSpec((2, TQ, HP), lambda h, i: (0, prv(i), h))],
        out_shape=[jax.ShapeDtypeStruct((1, S, 512), BF16), jax.ShapeDtypeStruct((4, 256, WIN), F32),
                   jax.ShapeDtypeStruct((2, S, 512), BF16)],
        scratch_shapes=[pltpu.VMEM((2 * TQ, HP), BF16), pltpu.VMEM((2 * TQ, HP), BF16),
                        pltpu.VMEM((2 * TQ, HP), F32), pltpu.VMEM((2 * TQ, HP), F32)],
        compiler_params=_cp("parallel", "arbitrary"),
    )(qkv, qkv, qkv, qkv, qkv, do, bias)
    return dq, db, dkv


def bias_grad(db, name):
    def body(db_ref, o_ref):
        r = lax.broadcasted_iota(jnp.int32, (128, 128), 0)
        c = lax.broadcasted_iota(jnp.int32, (128, 128), 1)
        flip = (r + c == 127).astype(BF16)
        lane = lax.broadcasted_iota(jnp.int32, (16, 384), 1)
        src = lax.broadcasted_iota(jnp.int32, (128, 384), 0)
        dst = lax.broadcasted_iota(jnp.int32, (128, 384), 1)

        def split_dot(v, m):
            hi = v.astype(BF16)
            r1 = v - hi.astype(F32)
            mid = r1.astype(BF16)
            lo = (r1 - mid.astype(F32)).astype(BF16)
            return _dot(hi, m) + _dot(mid, m) + _dot(lo, m)

        def diag_sums(w):
            y = pltpu.roll(split_dot(w, flip), 0, 1, stride=1, stride_axis=0)
            return jnp.broadcast_to(_colsum(y), (16, 128))

        w4 = db_ref[0, :, 512:640]
        w3 = db_ref[0, :, 384:512]
        far = jnp.sum(db_ref[0, :, 0:384]) + jnp.sum(jnp.where(r >= c, w3, 0.0))
        lo4 = diag_sums(jnp.where(r >= c, w4, 0.0))
        up4 = diag_sums(jnp.where(r < c, w4, 0.0))
        up3 = diag_sums(jnp.where(r < c, w3, 0.0))
        p_lo4 = (dst == 128 + (src + 1) % 128).astype(BF16)
        p_up4 = ((dst == src + 1) & (src < 127)).astype(BF16)
        p_up3 = ((dst == src + 129) & (src < 127)).astype(BF16)
        out = split_dot(lo4, p_lo4) + split_dot(up4, p_up4) + split_dot(up3, p_up3)
        o_ref[0] = out + jnp.where(lane == 256, far, 0.0)

    return pl.pallas_call(
        body, name=name, grid=(8,),
        in_specs=[pl.BlockSpec((1, 128, WIN), lambda h: (h, 0, 0))],
        out_specs=pl.BlockSpec((1, 16, 384), lambda h: (h, 0, 0)),
        out_shape=jax.ShapeDtypeStruct((8, 16, 384), F32),
        compiler_params=_cp("parallel"),
    )(db)[:, 0, :]


LT = 1024
LC = 512


def _lru_gates(xs, pv_ref, wa_ref, wx_ref, tl):
    xc = (pv_ref[4:5, :] + pv_ref[3:4, :] * xs[pl.ds(8, tl), :] + pv_ref[2:3, :] * xs[pl.ds(7, tl), :]
          + pv_ref[1:2, :] * xs[pl.ds(6, tl), :] + pv_ref[0:1, :] * xs[pl.ds(5, tl), :])
    xcb = xc.astype(BF16)
    pa = jnp.concatenate([_dot(xcb[:, 0:256], wa_ref[0]), _dot(xcb[:, 256:512], wa_ref[1])], axis=1)
    px = jnp.concatenate([_dot(xcb[:, 0:256], wx_ref[0]), _dot(xcb[:, 256:512], wx_ref[1])], axis=1)
    r = _sigmoid(pa + pv_ref[5:6, :])
    ig = _sigmoid(px + pv_ref[6:7, :])
    z = -pv_ref[7:8, :]
    sp = jnp.maximum(z, 0.0) + jnp.log1p(jnp.exp(-jnp.abs(z)))
    log_a = (-LRU_C * r) * sp
    a = jnp.exp(log_a)
    s = jnp.tanh(-log_a) * (1.0 + a * a)
    inv_mult = lax.rsqrt(s)
    mult = jnp.where(s > 0.0, s * inv_mult, 0.0)
    return xc, xcb, r, ig, sp, a, mult, inv_mult


def lru_fwd(rest, pvec, wa, wx, name):
    S = rest.shape[0]
    tl = min(LT, S)
    nt = S // tl

    def body(xr_ref, halo_ref, yr_ref, pv_ref, wa_ref, wx_ref, h_ref, hg_ref, xs, a_s, u_s, h_s, carry):
        ti = pl.program_id(1)

        @pl.when(ti == 0)
        def _():
            carry[...] = jnp.zeros_like(carry)

        xs[0:8, :] = jnp.where(ti > 0, halo_ref[8:16, :].astype(F32), 0.0)
        xs[pl.ds(8, tl), :] = xr_ref[...].astype(F32)
        xc, _, _, ig, _, a, mult, _ = _lru_gates(xs, pv_ref, wa_ref, wx_ref, tl)
        a_s[...] = a
        u_s[...] = mult * (ig * xc)
        row = lax.broadcasted_iota(jnp.int32, (8, LC), 0)

        def blk(bi, c):
            o = pl.multiple_of(bi * 8, 8)
            av = a_s[pl.ds(o, 8), :]
            bv = u_s[pl.ds(o, 8), :]
            for d in (1, 2, 4):
                a_sh = pltpu.roll(av, d, 0)
                b_sh = pltpu.roll(bv, d, 0)
                m = row >= d
                bv = jnp.where(m, av * b_sh + bv, bv)
                av = jnp.where(m, av * a_sh, av)
            hv = bv + av * c
            h_s[pl.ds(o, 8), :] = hv
            return hv[7:8, :]

        carry[...] = lax.fori_loop(0, tl // 8, blk, carry[...])
        h = h_s[...]
        h_ref[...] = h
        hg_ref[...] = (h * _gelu(yr_ref[...].astype(F32))).astype(BF16)

    hb = tl // 16
    return pl.pallas_call(
        body, name=name, grid=(2, nt),
        in_specs=[pl.BlockSpec((tl, LC), lambda c, t: (t, c)),
                  pl.BlockSpec((16, LC), lambda c, t: (jnp.maximum(t * hb - 1, 0), c)),
                  pl.BlockSpec((tl, LC), lambda c, t: (t, 2 + c)),
                  pl.BlockSpec((8, LC), lambda c, t: (0, c)),
                  pl.BlockSpec((2, 256, 256), lambda c, t: (c, 0, 0)),
                  pl.BlockSpec((2, 256, 256), lambda c, t: (c, 0, 0))],
        out_specs=[pl.BlockSpec((tl, LC), lambda c, t: (t, c)), pl.BlockSpec((tl, LC), lambda c, t: (t, c))],
        out_shape=[jax.ShapeDtypeStruct((S, D), F32), jax.ShapeDtypeStruct((S, D), BF16)],
        scratch_shapes=[pltpu.VMEM((tl + 8, LC), F32), pltpu.VMEM((tl, LC), F32), pltpu.VMEM((tl, LC), F32),
                        pltpu.VMEM((tl, LC), F32), pltpu.VMEM((1, LC), F32)],
        compiler_params=_cp("parallel", "arbitrary"),
    )(rest, rest, rest, pvec, wa, wx)


def lru_bwd(dh, h, rest, pvec, wa, wx, name):
    S = rest.shape[0]
    tl = min(LT, S)
    nt = S // tl

    def body(dh_ref, h_ref, hhalo_ref, xr_ref, xhalo_ref, pv_ref, wa_ref, wx_ref,
             dxr_ref, vacc_ref, dwa_ref, dwx_ref,
             xs, hs, a_s, ash_s, b_s, lam_s, dxe, anext, lnext, dxnext):
        ti = pl.program_id(1)
        tr = nt - 1 - ti

        @pl.when(ti == 0)
        def _():
            anext[...] = jnp.zeros_like(anext)
            lnext[...] = jnp.zeros_like(lnext)
            dxnext[...] = jnp.zeros_like(dxnext)
            vacc_ref[...] = jnp.zeros_like(vacc_ref)
            dwa_ref[...] = jnp.zeros_like(dwa_ref)
            dwx_ref[...] = jnp.zeros_like(dwx_ref)

        xs[0:8, :] = jnp.where(tr > 0, xhalo_ref[8:16, :].astype(F32), 0.0)
        xs[pl.ds(8, tl), :] = xr_ref[...].astype(F32)
        xc, xcb, r, ig, sp, a, mult, inv_mult = _lru_gates(xs, pv_ref, wa_ref, wx_ref, tl)

        a_s[pl.ds(0, tl), :] = a
        a_s[pl.ds(tl, 8), :] = jnp.broadcast_to(anext[...], (8, LC))
        ash_s[...] = a_s[pl.ds(1, tl), :]
        b_s[...] = dh_ref[...]
        row = lax.broadcasted_iota(jnp.int32, (8, LC), 0)

        def blk(k, c):
            o = pl.multiple_of((tl // 8 - 1 - k) * 8, 8)
            av = ash_s[pl.ds(o, 8), :]
            bv = b_s[pl.ds(o, 8), :]
            for d in (1, 2, 4):
                a_sh = pltpu.roll(av, 8 - d, 0)
                b_sh = pltpu.roll(bv, 8 - d, 0)
                m = row < 8 - d
                bv = jnp.where(m, bv + av * b_sh, bv)
                av = jnp.where(m, av * a_sh, av)
            lv = bv + av * c
            lam_s[pl.ds(o, 8), :] = lv
            return lv[0:1, :]

        lnext[...] = lax.fori_loop(0, tl // 8, blk, lnext[...])
        anext[...] = a[0:1, :]
        lam = lam_s[...]

        hs[0:8, :] = jnp.where(tr > 0, hhalo_ref[...], 0.0)
        hs[pl.ds(8, tl), :] = h_ref[...]
        d_a = lam * hs[pl.ds(7, tl), :]
        d_mult = lam * (ig * xc)
        d_ig = lam * mult * xc
        dxc = lam * mult * ig
        d_log_a = d_a * a - d_mult * (a * a) * inv_mult
        d_r = d_log_a * (-LRU_C * sp)
        vacc_ref[7:8, :] += _colsum(d_log_a * (-LRU_C * r)) * (-_sigmoid(-pv_ref[7:8, :]))
        d_pa = d_r * r * (1.0 - r)
        d_px = d_ig * ig * (1.0 - ig)
        vacc_ref[5:6, :] += _colsum(d_pa)
        vacc_ref[6:7, :] += _colsum(d_px)
        dpa = d_pa.astype(BF16)
        dpx = d_px.astype(BF16)
        back = []
        for g in range(2):
            sl = slice(256 * g, 256 * g + 256)
            dwa_ref[g] += _dot_tn(xcb[:, sl], dpa[:, sl])
            dwx_ref[g] += _dot_tn(xcb[:, sl], dpx[:, sl])
            back.append(_dot_nt(dpa[:, sl], wa_ref[g]) + _dot_nt(dpx[:, sl], wx_ref[g]))
        dxc = dxc + jnp.concatenate(back, axis=1)
        vacc_ref[4:5, :] += _colsum(dxc)
        for k in range(4):
            vacc_ref[k:k + 1, :] += _colsum(dxc * xs[pl.ds(5 + k, tl), :])
        dxe[pl.ds(0, tl), :] = dxc
        dxe[pl.ds(tl, 8), :] = dxnext[...]
        dxr = (pv_ref[3:4, :] * dxc + pv_ref[2:3, :] * dxe[pl.ds(1, tl), :]
               + pv_ref[1:2, :] * dxe[pl.ds(2, tl), :] + pv_ref[0:1, :] * dxe[pl.ds(3, tl), :])
        dxr_ref[...] = dxr.astype(BF16)
        dxnext[...] = dxc[0:8, :]

    hb = tl // 8
    rev = lambda t: nt - 1 - t
    halo = lambda t: jnp.maximum(rev(t) * hb - 1, 0)
    big = lambda: pltpu.VMEM((tl + 8, LC), F32)
    til = lambda: pltpu.VMEM((tl, LC), F32)
    return pl.pallas_call(
        body, name=name, grid=(2, nt),
        in_specs=[pl.BlockSpec((tl, LC), lambda c, t: (rev(t), c)),
                  pl.BlockSpec((tl, LC), lambda c, t: (rev(t), c)),
                  pl.BlockSpec((8, LC), lambda c, t: (halo(t), c)),
                  pl.BlockSpec((tl, LC), lambda c, t: (rev(t), c)),
                  pl.BlockSpec((16, LC), lambda c, t: (jnp.maximum(rev(t) * (tl // 16) - 1, 0), c)),
                  pl.BlockSpec((8, LC), lambda c, t: (0, c)),
                  pl.BlockSpec((2, 256, 256), lambda c, t: (c, 0, 0)),
                  pl.BlockSpec((2, 256, 256), lambda c, t: (c, 0, 0))],
        out_specs=[pl.BlockSpec((tl, LC), lambda c, t: (rev(t), c)),
                   pl.BlockSpec((8, LC), lambda c, t: (0, c)),
                   pl.BlockSpec((2, 256, 256), lambda c, t: (c, 0, 0)),
                   pl.BlockSpec((2, 256, 256), lambda c, t: (c, 0, 0))],
        out_shape=[jax.ShapeDtypeStruct((S, D), BF16), jax.ShapeDtypeStruct((8, D), F32),
                   jax.ShapeDtypeStruct((4, 256, 256), F32), jax.ShapeDtypeStruct((4, 256, 256), F32)],
        scratch_shapes=[big(), big(), big(), til(), til(), til(), big(),
                        pltpu.VMEM((1, LC), F32), pltpu.VMEM((1, LC), F32), pltpu.VMEM((8, LC), F32)],
        compiler_params=_cp("parallel", "arbitrary"),
    )(dh, h, h, rest, rest, pvec, wa, wx)


def mix_out_fwd(x, ao, hg, rest, vec, w_att_o, w_rec_o, w_out, name, tm=512):
    S = x.shape[0]
    tm = min(tm, S)

    def body(x_ref, ao_ref, hg_ref, ga_ref, gr_ref, vec_ref, wa_ref, wr_ref, wo_ref,
             xo_ref, att_ref, rec_ref, mg_ref, f_ref):
        att = _dot(ao_ref[...], wa_ref[...])
        rec = _dot(hg_ref[...], wr_ref[...])
        att_ref[...] = att.astype(BF16)
        rec_ref[...] = rec.astype(BF16)
        mg = (_sigmoid(ga_ref[...].astype(F32)) * att + _sigmoid(gr_ref[...].astype(F32)) * rec).astype(BF16)
        mg_ref[...] = mg
        f = _dot(mg, wo_ref[...])
        f_ref[...] = f.astype(BF16)
        y = f * lax.rsqrt(_mean(f * f) + EPS) * vec_ref[1:2, :]
        xo_ref[...] = x_ref[...] + (1.0 * vec_ref[4:5, :]) * y

    row = lambda i: (i, 0)
    full = lambda r: pl.BlockSpec((r, D), lambda i: (0, 0))
    return pl.pallas_call(
        body, name=name, grid=(S // tm,),
        in_specs=[pl.BlockSpec((tm, D), row), pl.BlockSpec((tm, 512), row), pl.BlockSpec((tm, D), row),
                  pl.BlockSpec((tm, D), lambda i: (i, 2)), pl.BlockSpec((tm, D), lambda i: (i, 3)),
                  full(8), full(512), full(D), full(D)],
        out_specs=[pl.BlockSpec((tm, D), row)] * 5,
        out_shape=[jax.ShapeDtypeStruct((S, D), F32)] + [jax.ShapeDtypeStruct((S, D), BF16)] * 4,
        compiler_params=_cp("parallel"),
    )(x, ao, hg, rest, rest, vec, w_att_o, w_rec_o, w_out)


def mix_out_bwd(dxo, f, att, rec, rest, h, vec, w_att_o, w_rec_o, w_out, name, tm=512):
    S = dxo.shape[0]
    tm = min(tm, S)

    def body(dxo_ref, f_ref, att_ref, rec_ref, yr_ref, ga_ref, gr_ref, h_ref, vec_ref, wa_ref, wr_ref, wo_ref,
             df_ref, da_ref, dr_ref, dao_ref, dh_ref, d3_ref, vacc_ref):
        @pl.when(pl.program_id(0) == 0)
        def _():
            vacc_ref[...] = jnp.zeros_like(vacc_ref)

        df = _post_norm_bwd(dxo_ref[...], f_ref[...].astype(F32), 1.0, vec_ref, vacc_ref).astype(BF16)
        df_ref[...] = df
        dm = _dot_nt(df, wo_ref[...])
        sa = _sigmoid(ga_ref[...].astype(F32))
        sr = _sigmoid(gr_ref[...].astype(F32))
        d_att = (dm * sa).astype(BF16)
        d_rec = (dm * sr).astype(BF16)
        da_ref[...] = d_att
        dr_ref[...] = d_rec
        d3_ref[1] = (dm * att_ref[...].astype(F32) * (sa * (1.0 - sa))).astype(BF16)
        d3_ref[2] = (dm * rec_ref[...].astype(F32) * (sr * (1.0 - sr))).astype(BF16)
        dao_ref[...] = _dot_nt(d_att, wa_ref[...]).astype(BF16)
        d_hg = _dot_nt(d_rec, wr_ref[...])
        yr = yr_ref[...].astype(F32)
        t = jnp.tanh(_GK * (yr + 0.044715 * yr * yr * yr))
        dh_ref[...] = d_hg * (0.5 * yr * (1.0 + t))
        gelu_grad = 0.5 * (1.0 + t) + 0.5 * yr * (1.0 - t * t) * _GK * (1.0 + 3.0 * 0.044715 * yr * yr)
        d3_ref[0] = (d_hg * h_ref[...] * gelu_grad).astype(BF16)

    row = lambda i: (i, 0)
    full = lambda r: pl.BlockSpec((r, D), lambda i: (0, 0))
    return pl.pallas_call(
        body, name=name, grid=(S // tm,),
        in_specs=[pl.BlockSpec((tm, D), row)] * 4
        + [pl.BlockSpec((tm, D), lambda i: (i, 1)), pl.BlockSpec((tm, D), lambda i: (i, 2)),
           pl.BlockSpec((tm, D), lambda i: (i, 3)), pl.BlockSpec((tm, D), row),
           full(8), full(512), full(D), full(D)],
        out_specs=[pl.BlockSpec((tm, D), row)] * 3
        + [pl.BlockSpec((tm, 512), row), pl.BlockSpec((tm, D), row),
           pl.BlockSpec((3, tm, D), lambda i: (0, i, 0)), pl.BlockSpec((8, D), lambda i: (0, 0))],
        out_shape=[jax.ShapeDtypeStruct((S, D), BF16)] * 3
        + [jax.ShapeDtypeStruct((S, 512), BF16), jax.ShapeDtypeStruct((S, D), F32),
           jax.ShapeDtypeStruct((3, S, D), BF16), jax.ShapeDtypeStruct((8, D), F32)],
        compiler_params=_cp("arbitrary"),
    )(dxo, f, att, rec, rest, rest, rest, h, vec, w_att_o, w_rec_o, w_out)


def dw_in(h, dq, dkv, dxr, d3, name, tk=1024, tn=512):
    S = h.shape[0]
    tk = min(tk, S)
    nk = S // tk

    def body(h_ref, dq_ref, dkv_ref, dxr_ref, d3_ref, o_ref, acc):
        j, k = pl.program_id(0), pl.program_id(1)

        @pl.when(k == 0)
        def _():
            acc[...] = jnp.zeros_like(acc)

        @pl.when(j == 0)
        def _():
            acc[...] += _dot_tn(h_ref[...], dq_ref[...])

        @pl.when((j >= 1) & (j < 3))
        def _():
            acc[...] += _dot_tn(h_ref[...], dkv_ref[...])

        @pl.when((j >= 3) & (j < 5))
        def _():
            acc[...] += _dot_tn(h_ref[...], dxr_ref[...])

        @pl.when(j >= 5)
        def _():
            acc[...] += _dot_tn(h_ref[...], d3_ref[...])

        @pl.when(k == nk - 1)
        def _():
            o_ref[...] = acc[...].astype(BF16)

    use = lambda j, k, lo, hi: jnp.where((j >= lo) & (j < hi), k, 0)
    g3 = lambda j: jnp.clip(j - 5, 0, 5)
    return pl.pallas_call(
        body, name=name, grid=(PW // tn, nk),
        in_specs=[pl.BlockSpec((tk, D), lambda j, k: (k, 0)),
                  pl.BlockSpec((None, tk, tn), lambda j, k: (0, use(j, k, 0, 1), 0)),
                  pl.BlockSpec((None, tk, tn), lambda j, k: (jnp.clip(j - 1, 0, 1), use(j, k, 1, 3), 0)),
                  pl.BlockSpec((tk, tn), lambda j, k: (use(j, k, 3, 5), jnp.clip(j - 3, 0, 1))),
                  pl.BlockSpec((None, tk, tn), lambda j, k: (g3(j) // 2, use(j, k, 5, 11), g3(j) % 2))],
        out_specs=pl.BlockSpec((D, tn), lambda j, k: (0, j)),
        out_shape=jax.ShapeDtypeStruct((D, PW), BF16),
        scratch_shapes=[pltpu.VMEM((D, tn), F32)],
        compiler_params=_cp("parallel", "arbitrary"),
    )(h, dq, dkv, dxr, d3)


def ada_fwd(c_all, w_ada, b_ada, name, tn=768):
    n = w_ada.shape[1]

    def body(c_ref, w_ref, b_ref, o_ref):
        cv = c_ref[...]
        ca = (cv * _sigmoid(cv)).astype(BF16)
        o_ref[...] = _dot(ca, w_ref[...].astype(BF16)) + b_ref[...]

    return pl.pallas_call(
        body, name=name, grid=(n // tn,),
        in_specs=[pl.BlockSpec((8, D), lambda j: (0, 0)), pl.BlockSpec((D, tn), lambda j: (0, j)),
                  pl.BlockSpec((1, tn), lambda j: (0, j))],
        out_specs=pl.BlockSpec((8, tn), lambda j: (0, j)),
        out_shape=jax.ShapeDtypeStruct((8, n), F32),
        compiler_params=_cp("parallel"),
    )(c_all, w_ada, b_ada)


def ada_bwd(c_all_t, dmod, name, tn=768):
    n = dmod.shape[1]

    def body(c_ref, d_ref, o_ref):
        cv = c_ref[...]
        ca = (cv * _sigmoid(cv)).astype(BF16)
        o_ref[...] = _dot(ca, d_ref[...].astype(BF16))

    return pl.pallas_call(
        body, name=name, grid=(n // tn,),
        in_specs=[pl.BlockSpec((D, 128), lambda j: (0, 0)), pl.BlockSpec((128, tn), lambda j: (0, j))],
        out_specs=pl.BlockSpec((D, tn), lambda j: (0, j)),
        out_shape=jax.ShapeDtypeStruct((D, n), F32),
        compiler_params=_cp("parallel"),
    )(c_all_t, dmod)


def _row_tile(rows, cols, itemsize=4, budget=1536 * 1024):
    best = None
    for t in range(8, rows + 1, 8):
        if rows % t == 0 and t * cols * itemsize <= budget:
            best = t
    return rows if best is None else best


def sum_lead(parts, name, out_dtype=F32):
    n, R, C = parts.shape
    tr = _row_tile(R, C * n)

    def body(p_ref, o_ref):
        acc = p_ref[0].astype(F32)
        for k in range(1, n):
            acc = acc + p_ref[k].astype(F32)
        o_ref[...] = acc.astype(out_dtype)

    return pl.pallas_call(
        body, name=name, grid=(R // tr,),
        in_specs=[pl.BlockSpec((n, tr, C), lambda i: (0, i, 0))],
        out_specs=pl.BlockSpec((tr, C), lambda i: (i, 0)),
        out_shape=jax.ShapeDtypeStruct((R, C), out_dtype),
        compiler_params=_cp("parallel"),
    )(parts)


def adamw(w, g, m, v, name, emit_g=False):
    R, C = w.shape
    tr = _row_tile(R, C * 8, budget=8 * 1024 * 1024)

    def body(w_ref, g_ref, m_ref, v_ref, d_ref, mo_ref, vo_ref, *go_ref):
        gv = g_ref[...]
        if emit_g:
            go_ref[0][...] = gv
        mn = ADAM_B1 * m_ref[...] + (1.0 - ADAM_B1) * gv
        vn = ADAM_B2 * v_ref[...] + (1.0 - ADAM_B2) * (gv * gv)
        m_hat = mn / (1.0 - ADAM_B1 ** ADAM_STEP)
        v_hat = vn / (1.0 - ADAM_B2 ** ADAM_STEP)
        d_ref[...] = -ADAM_LR * (m_hat / (jnp.sqrt(v_hat) + ADAM_EPS) + ADAM_WD * w_ref[...])
        mo_ref[...] = mn
        vo_ref[...] = vn

    spec = pl.BlockSpec((tr, C), lambda i: (i, 0))
    return pl.pallas_call(
        body, name=name, grid=(R // tr,),
        in_specs=[spec] * 4, out_specs=[spec] * (4 if emit_g else 3),
        out_shape=[jax.ShapeDtypeStruct((R, C), F32)] * (4 if emit_g else 3),
        compiler_params=_cp("parallel"),
    )(w, g, m, v)


def _mesh_pos():
    return lax.axis_index("x"), lax.axis_index("y"), lax.axis_index("c")


def _other_chips(mx, my):
    return [(1 - mx, my), (mx, 1 - my), (1 - mx, 1 - my)]


def ag_small(x, name):
    R = x.shape[0]

    def body(x_ref, out_ref, send_sems, recv_sems, local_sem):
        mx, my, mc = _mesh_pos()
        me, sibling = (mx, my, mc), (mx, my, 1 - mc)
        chips = _other_chips(mx, my)

        def slot(px, py, pc):
            return out_ref.at[4 * px + 2 * py + pc]

        def copy(k, block, to, src=None):
            return pltpu.make_async_remote_copy(
                src_ref=slot(*block) if src is None else src, dst_ref=slot(*block),
                send_sem=send_sems.at[k], recv_sem=recv_sems.at[k], device_id=to, device_id_type=MESH)

        mine = pltpu.make_async_copy(x_ref, slot(*me), local_sem)
        mine.start()
        first = [copy(0, me, sibling, src=x_ref)]
        first += [copy(1 + j, me, (*chip, mc), src=x_ref) for j, chip in enumerate(chips)]
        for cp in first:
            cp.start()
        passed = [copy(4 + j, (*chip, mc), sibling) for j, chip in enumerate(chips)]
        for j, chip in enumerate(chips):
            copy(1 + j, (*chip, mc), me).wait_recv()
            passed[j].start()
        copy(0, sibling, me).wait_recv()
        for j, chip in enumerate(chips):
            copy(4 + j, (*chip, 1 - mc), me).wait_recv()
        for cp in first + passed:
            cp.wait_send()
        mine.wait()

    return pl.pallas_call(
        body, name=name,
        out_shape=jax.ShapeDtypeStruct((N_DEV, R, 128), F32),
        in_specs=[pl.BlockSpec(memory_space=pltpu.VMEM)],
        out_specs=pl.BlockSpec(memory_space=pltpu.VMEM),
        scratch_shapes=[pltpu.SemaphoreType.DMA((7,)), pltpu.SemaphoreType.DMA((7,)), pltpu.SemaphoreType.DMA],
        compiler_params=pltpu.CompilerParams(vmem_limit_bytes=VMEM_LIMIT),
    )(x)


BIG = (("ffn1_w_gu", "col", D, PW), ("ffn1_w_down", "row", FF, D), ("w_in", "col", D, PW),
       ("w_att_o", "col", 512, D), ("w_rec_o", "row", D, D), ("w_out", "row", D, D),
       ("ffn2_w_gu", "col", D, PW), ("ffn2_w_down", "row", FF, D))
NBIG = len(BIG)


def _shard_shape(kind, R, C):
    return (R, C // 4) if kind == "col" else (R // 4, C)


def _region(ref, kind, R, C, q, half, t, tr):
    sr, sc = _shard_shape(kind, R, C)
    if kind == "col":
        return ref.at[pl.ds(pl.multiple_of(half * (R // 2) + t * tr, 16), tr), pl.ds(q * sc, sc)]
    return ref.at[pl.ds(pl.multiple_of(q * sr + t * tr, 16), tr), pl.ds(half * (C // 2), C // 2)]


def ag_local(w, kind, R, C, p_arr, name, after=()):
    sr, sc = _shard_shape(kind, R, C)
    tr = _row_tile(sr, sc, budget=2 * 1024 * 1024)
    nt = sr // tr
    after = list(after)

    def body(p_ref, w_ref, *rest):
        rest[-1][...] = w_ref[...].astype(BF16)

    if kind == "col":
        o_spec = pl.BlockSpec((tr, sc), lambda i, p: (i, p[0]))
    else:
        o_spec = pl.BlockSpec((tr, sc), lambda i, p: (p[0] * nt + i, 0))
    return pl.pallas_call(
        body, name=name,
        grid_spec=pltpu.PrefetchScalarGridSpec(
            num_scalar_prefetch=1, grid=(nt,),
            in_specs=[pl.BlockSpec((tr, sc), lambda i, p: (i, 0))] + [ANY] * len(after), out_specs=o_spec),
        out_shape=jax.ShapeDtypeStruct((R, C), BF16),
        compiler_params=_cp("parallel"),
    )(p_arr, w, *after)


HBM_SPEC = pl.BlockSpec(memory_space=pltpu.HBM)
SEM_SPEC = pl.BlockSpec(memory_space=pltpu.SEMAPHORE)


def _ag_sems(geoms):
    return sum(6 if both else 3 for (_, _, _, both) in geoms)


def _ag_copies(fulls, geoms, ssem, rsem, mx, my, mc, q, h):
    chips = _other_chips(mx, my)
    out, base = [], 0
    for w, (kind, R, C, both) in enumerate(geoms):
        sr, sc = _shard_shape(kind, R, C)
        hr = sr // 2 if kind == "col" else sr
        reg = _region(fulls[w], kind, R, C, q, h, 0, hr)
        out.append([pltpu.make_async_remote_copy(
            src_ref=reg, dst_ref=reg, send_sem=ssem.at[base + 3 * t + k], recv_sem=rsem.at[base + 3 * t + k],
            device_id=(*chips[k], mc if t == 0 else 1 - mc), device_id_type=MESH)
            for t in range(2 if both else 1) for k in range(3)])
        base += 6 if both else 3
    return out


def ag_start(fulls, geoms, after, name):
    n = len(fulls)
    after = list(after)
    m = len(after)

    def body(*refs):
        ssem, rsem = refs[n + m:n + m + 2]
        outs, token = refs[n + m + 2:2 * n + m + 2], refs[2 * n + m + 2]
        mx, my, mc = _mesh_pos()
        p = 2 * mx + my
        col = [w for w, g in enumerate(geoms) if g[0] == "col"]
        row = [w for w, g in enumerate(geoms) if g[0] == "row"]
        for q in range(4):
            @pl.when(p == q)
            def _(q=q):
                cps = _ag_copies(outs, geoms, ssem, rsem, mx, my, mc, q, mc)
                for w in col:
                    for cp in cps[w]:
                        cp.start()
        for h in range(2):
            @pl.when(mc == h)
            def _(h=h):
                cps = _ag_copies(outs, geoms, ssem, rsem, mx, my, mc, p, h)
                for w in row:
                    for cp in cps[w]:
                        cp.start()
        token[...] = jnp.zeros_like(token)

    res = pl.pallas_call(
        body, name=name,
        out_shape=[pltpu.SemaphoreType.DMA((_ag_sems(geoms),)), pltpu.SemaphoreType.DMA((_ag_sems(geoms),))]
        + [pltpu.HBM(a.shape, a.dtype) for a in fulls] + [jax.ShapeDtypeStruct((8, 128), F32)],
        in_specs=[HBM_SPEC] * n + [ANY] * m,
        out_specs=[SEM_SPEC, SEM_SPEC] + [HBM_SPEC] * n + [pl.BlockSpec(memory_space=pltpu.VMEM)],
        input_output_aliases={w: 2 + w for w in range(n)},
        compiler_params=pltpu.CompilerParams(has_side_effects=pltpu.SideEffectType.DATAFLOW_SIDE_EFFECTING),
    )(*[pltpu.with_memory_space_constraint(a, pltpu.HBM) for a in fulls], *after)
    return res[0], res[1], list(res[2:2 + n]), res[2 + n]


def ag_wait(fulls, geoms, ssem, rsem, after, name):
    n = len(fulls)
    after = list(after) if isinstance(after, (list, tuple)) else [after]

    def body(*refs):
        ins, ssem_ref, rsem_ref = refs[:n], refs[n], refs[n + 1]
        mx, my, mc = _mesh_pos()
        for cps in _ag_copies(ins, geoms, ssem_ref, rsem_ref, mx, my, mc, 0, 0):
            for cp in cps:
                cp.wait_send()
                cp.wait_recv()

    return list(pl.pallas_call(
        body, name=name,
        out_shape=[pltpu.HBM(a.shape, a.dtype) for a in fulls],
        in_specs=[HBM_SPEC] * n + [SEM_SPEC, SEM_SPEC] + [ANY] * len(after),
        out_specs=[HBM_SPEC] * n,
        input_output_aliases={w: w for w in range(n)},
        compiler_params=pltpu.CompilerParams(has_side_effects=pltpu.SideEffectType.DATAFLOW_SIDE_EFFECTING),
    )(*fulls, ssem, rsem, *after))


def ag_forward(full, kind, R, C, name):
    sr, sc = _shard_shape(kind, R, C)
    hr, hc = (sr // 2, sc) if kind == "col" else (sr, sc // 2)
    tr = _row_tile(hr, hc, itemsize=2, budget=512 * 1024)
    nt = hr // tr

    total = 3 * nt

    def body(src_ref, full_ref, stage, lsem, ssem, rsem):
        step = pl.program_id(0) * nt + pl.program_id(1)
        par = step % 2
        mx, my, mc = _mesh_pos()

        def load(s, q, h, t):
            return pltpu.make_async_copy(_region(src_ref, kind, R, C, q, h, t, tr), stage.at[s], lsem.at[s])

        def push(s, q, h, t):
            return pltpu.make_async_remote_copy(src_ref=stage.at[s], dst_ref=_region(full_ref, kind, R, C, q, h, t, tr),
                                                send_sem=ssem.at[s], recv_sem=rsem, device_id=(mx, my, 1 - mc),
                                                device_id_type=MESH)

        def for_tile(stp, fn):
            q_k = _partner_chip(stp // nt, 2 * mx + my)
            if kind == "col":
                for q in range(4):
                    @pl.when(q_k == q)
                    def _(q=q):
                        fn(q, mc, stp % nt)
            else:
                for h in range(2):
                    @pl.when(mc == h)
                    def _(h=h):
                        fn(q_k, h, stp % nt)

        @pl.when(step == 0)
        def _():
            for_tile(step, lambda q, h, t: load(0, q, h, t).start())

        load(par, 0, 0, 0).wait()
        for_tile(step, lambda q, h, t: push(par, q, h, t).start())

        @pl.when(step + 1 < total)
        def _():
            @pl.when(step >= 1)
            def _():
                push(1 - par, 0, 0, 0).wait_send()
            for_tile(step + 1, lambda q, h, t: load(1 - par, q, h, t).start())

        @pl.when(step == total - 1)
        def _():
            push(par, 0, 0, 0).wait_send()
            push(1 - par, 0, 0, 0).wait_send()
            three = full_ref.at[pl.ds(0, hr), pl.ds(0, 3 * hc)] if kind == "col" else full_ref.at[pl.ds(0, 3 * hr), pl.ds(0, hc)]
            pltpu.make_async_remote_copy(src_ref=three, dst_ref=three, send_sem=ssem.at[0], recv_sem=rsem,
                                         device_id=(mx, my, 1 - mc), device_id_type=MESH).wait_recv()

    return pl.pallas_call(
        body, name=name, grid=(3, nt),
        in_specs=[ANY], out_specs=ANY,
        out_shape=jax.ShapeDtypeStruct((R, C), BF16),
        scratch_shapes=[pltpu.VMEM((2, tr, hc), BF16), pltpu.SemaphoreType.DMA((2,)), pltpu.SemaphoreType.DMA((2,)),
                        pltpu.SemaphoreType.DMA],
        input_output_aliases={0: 0},
        compiler_params=_cp("arbitrary", "arbitrary"),
    )(full)


def _half_shape(kind, R, C):
    return (R // 2, C) if kind == "col" else (R, C // 2)


def _piece_shape(kind, R, C):
    return (R // 2, C // 4) if kind == "col" else (R // 4, C // 2)


def pair_push(g, kind, c_arr, name):
    R, C = g.shape
    hr, hc = _half_shape(kind, R, C)
    tr = _row_tile(hr, hc, itemsize=2, budget=1024 * 1024)
    nt = hr // tr

    def body(c_ref, g_ref, out_ref, stage, ssem, rsem):
        i = pl.program_id(0)
        slot = i % 2
        mx, my, mc = _mesh_pos()

        def push(s, t):
            return pltpu.make_async_remote_copy(
                src_ref=stage.at[s], dst_ref=out_ref.at[pl.ds(pl.multiple_of(t * tr, 16), tr)],
                send_sem=ssem.at[s], recv_sem=rsem, device_id=(mx, my, 1 - mc), device_id_type=MESH)

        @pl.when(i >= 2)
        def _():
            push(slot, 0).wait_send()

        stage[slot] = g_ref[...]
        push(slot, i).start()

        @pl.when(i == nt - 1)
        def _():
            push(slot, 0).wait_send()
            if nt >= 2:
                push(1 - slot, 0).wait_send()
            pltpu.make_async_remote_copy(src_ref=out_ref, dst_ref=out_ref, send_sem=ssem.at[0], recv_sem=rsem,
                                         device_id=(mx, my, 1 - mc), device_id_type=MESH).wait_recv()

    if kind == "col":
        g_spec = pl.BlockSpec((tr, hc), lambda i, c: ((1 - c[0]) * nt + i, 0))
    else:
        g_spec = pl.BlockSpec((tr, hc), lambda i, c: (i, 1 - c[0]))
    return pl.pallas_call(
        body, name=name,
        grid_spec=pltpu.PrefetchScalarGridSpec(
            num_scalar_prefetch=1, grid=(nt,), in_specs=[g_spec], out_specs=ANY,
            scratch_shapes=[pltpu.VMEM((2, tr, hc), BF16), pltpu.SemaphoreType.DMA((2,)), pltpu.SemaphoreType.DMA]),
        out_shape=jax.ShapeDtypeStruct((hr, hc), BF16),
        compiler_params=_cp("arbitrary"),
    )(c_arr, g)


def _partner_chip(k, p):
    return p ^ jnp.where(k == 0, 2, jnp.where(k == 1, 1, jnp.where(k == 2, 3, 0)))


def pair_add(g, got, kind, cp_arr, name):
    R, C = g.shape
    pr, pc = _piece_shape(kind, R, C)
    tr = _row_tile(pr, pc, itemsize=2, budget=1024 * 1024)
    nt = pr // tr

    def body(cp_ref, g_ref, got_ref, ps_ref, rb_ref):
        tile = (g_ref[...].astype(F32) + got_ref[...].astype(F32)).astype(BF16)
        ps_ref[...] = tile

        @pl.when(pl.program_id(1) == cp_ref[1])
        def _():
            rb_ref[...] = tile

    if kind == "col":
        g_spec = pl.BlockSpec((tr, pc), lambda i, q, cp: (cp[0] * nt + i, q))
        got_spec = pl.BlockSpec((tr, pc), lambda i, q, cp: (i, q))
    else:
        g_spec = pl.BlockSpec((tr, pc), lambda i, q, cp: (q * nt + i, cp[0]))
        got_spec = pl.BlockSpec((tr, pc), lambda i, q, cp: (q * nt + i, 0))
    return pl.pallas_call(
        body, name=name,
        grid_spec=pltpu.PrefetchScalarGridSpec(
            num_scalar_prefetch=1, grid=(nt, 4), in_specs=[g_spec, got_spec],
            out_specs=[pl.BlockSpec((None, tr, pc), lambda i, q, cp: (q, i, 0)),
                       pl.BlockSpec((None, tr, pc), lambda i, q, cp: (cp[1], i, 0))]),
        out_shape=[jax.ShapeDtypeStruct((4, pr, pc), BF16)] * 2,
        compiler_params=_cp("arbitrary", "arbitrary"),
    )(cp_arr, g, got)


def _rs_copies(ps, rb, ssem, rsem, mx, my, mc):
    p = 2 * mx + my
    out = []
    for w in range(len(ps)):
        for k, chip in enumerate(_other_chips(mx, my)):
            out.append(pltpu.make_async_remote_copy(
                src_ref=ps[w].at[2 * chip[0] + chip[1]], dst_ref=rb[w].at[p], send_sem=ssem.at[3 * w + k],
                recv_sem=rsem.at[3 * w + k], device_id=(*chip, mc), device_id_type=MESH))
    return out


def rs_start(ps, rb, after, name):
    n = len(ps)
    after = list(after)
    m = len(after)

    def body(*refs):
        ssem, rsem = refs[2 * n + m:2 * n + m + 2]
        ps_o = refs[2 * n + m + 2:3 * n + m + 2]
        rb_o = refs[3 * n + m + 2:4 * n + m + 2]
        token = refs[4 * n + m + 2]
        for cp in _rs_copies(ps_o, rb_o, ssem, rsem, *_mesh_pos()):
            cp.start()
        token[...] = jnp.zeros_like(token)

    both = list(ps) + list(rb)
    res = pl.pallas_call(
        body, name=name,
        out_shape=[pltpu.SemaphoreType.DMA((3 * n,)), pltpu.SemaphoreType.DMA((3 * n,))]
        + [pltpu.HBM(a.shape, a.dtype) for a in both] + [jax.ShapeDtypeStruct((8, 128), F32)],
        in_specs=[HBM_SPEC] * (2 * n) + [ANY] * m,
        out_specs=[SEM_SPEC, SEM_SPEC] + [HBM_SPEC] * (2 * n) + [pl.BlockSpec(memory_space=pltpu.VMEM)],
        input_output_aliases={w: 2 + w for w in range(2 * n)},
        compiler_params=pltpu.CompilerParams(has_side_effects=pltpu.SideEffectType.DATAFLOW_SIDE_EFFECTING),
    )(*[pltpu.with_memory_space_constraint(a, pltpu.HBM) for a in both], *after)
    return res[0], res[1], list(res[2:2 + n]), list(res[2 + n:2 + 2 * n]), res[2 + 2 * n]


def rs_wait(ps, rb, ssem, rsem, after, name):
    n = len(ps)
    after = list(after)
    m = len(after)

    def body(*refs):
        ps_i, rb_i = refs[:n], refs[n:2 * n]
        ssem_ref, rsem_ref = refs[2 * n], refs[2 * n + 1]
        for cp in _rs_copies(ps_i, rb_i, ssem_ref, rsem_ref, *_mesh_pos()):
            cp.wait_send()
            cp.wait_recv()

    both = list(ps) + list(rb)
    res = pl.pallas_call(
        body, name=name,
        out_shape=[pltpu.HBM(a.shape, a.dtype) for a in both],
        in_specs=[HBM_SPEC] * (2 * n) + [SEM_SPEC, SEM_SPEC] + [ANY] * m,
        out_specs=[HBM_SPEC] * (2 * n),
        input_output_aliases={w: w for w in range(2 * n)},
        compiler_params=pltpu.CompilerParams(has_side_effects=pltpu.SideEffectType.DATAFLOW_SIDE_EFFECTING),
    )(*both, ssem, rsem, *after)
    return list(res[n:])


def sum_share(parts, kind, R, C, name):
    _, pr, pc = parts.shape
    sr, sc = _shard_shape(kind, R, C)
    tr = _row_tile(pr, pc * 4, budget=4 * 1024 * 1024)
    nt = pr // tr

    def body(p_ref, fin_ref, stage, lsem, ssem, rsem):
        i = pl.program_id(0)
        slot = i % 2
        mx, my, mc = _mesh_pos()

        def region(h, t):
            r0 = pl.multiple_of(t * tr, 8)
            if kind == "col":
                return fin_ref.at[pl.ds(pl.multiple_of(h * pr + r0, 8), tr)]
            return fin_ref.at[pl.ds(r0, tr), pl.ds(h * pc, pc)]

        def copies(s, h, t):
            return (pltpu.make_async_copy(stage.at[s], region(h, t), lsem.at[s]),
                    pltpu.make_async_remote_copy(src_ref=stage.at[s], dst_ref=region(h, t), send_sem=ssem.at[s],
                                                 recv_sem=rsem, device_id=(mx, my, 1 - mc), device_id_type=MESH))

        def wait_sent(s):
            loc, rem = copies(s, 0, 0)
            loc.wait()
            rem.wait_send()

        @pl.when(i >= 2)
        def _():
            wait_sent(slot)

        acc = p_ref[0].astype(F32)
        for k in range(1, 4):
            acc = acc + p_ref[k].astype(F32)
        stage[slot] = acc
        if kind == "col":
            for cp in copies(slot, mc, i):
                cp.start()
        else:
            for h in range(2):
                @pl.when(mc == h)
                def _(h=h):
                    for cp in copies(slot, h, i):
                        cp.start()

        @pl.when(i == nt - 1)
        def _():
            wait_sent(slot)
            if nt >= 2:
                wait_sent(1 - slot)
            half = fin_ref.at[pl.ds(0, pr), pl.ds(0, pc)]
            pltpu.make_async_remote_copy(src_ref=half, dst_ref=half, send_sem=ssem.at[0], recv_sem=rsem,
                                         device_id=(mx, my, 1 - mc), device_id_type=MESH).wait_recv()

    return pl.pallas_call(
        body, name=name, grid=(nt,),
        in_specs=[pl.BlockSpec((4, tr, pc), lambda i: (0, i, 0))],
        out_specs=ANY,
        out_shape=jax.ShapeDtypeStruct((sr, sc), F32),
        scratch_shapes=[pltpu.VMEM((2, tr, pc), F32), pltpu.SemaphoreType.DMA((2,)), pltpu.SemaphoreType.DMA((2,)),
                        pltpu.SemaphoreType.DMA],
        compiler_params=_cp("arbitrary"),
    )(parts)


def _pack(parts, rows):
    flat = []
    for a in parts:
        a = jnp.ravel(a).astype(F32)
        flat.append(jnp.pad(a, (0, (-a.shape[0]) % 128)))
    v = jnp.concatenate(flat)
    return jnp.pad(v, (0, rows * 128 - v.shape[0])).reshape(rows, 128)


def _unpack(block, shapes):
    lead = block.shape[:-2]
    v = block.reshape(lead + (-1,))
    out, off = [], 0
    for shp in shapes:
        n = int(np.prod(shp))
        out.append(v[..., off:off + n].reshape(lead + tuple(shp)))
        off += n + (-n) % 128
    return out


def _block_diag4(w):
    w4 = w.reshape(4, 4, 64, 64)
    eye = jnp.eye(4, dtype=w.dtype)
    return (w4[:, :, :, None, :] * eye[None, :, None, :, None]).reshape(4, 256, 256)


def _diag_blocks(bd):
    b5 = bd.reshape(4, 4, 64, 4, 64)
    return jnp.stack([b5[:, i, :, i, :] for i in range(4)], axis=1).reshape(16, 64, 64)


def _bias_window(rel_bias):
    m = (np.arange(768) + 127) % 768 - 127
    w = rel_bias[:, np.clip(512 - m, -128, 128) + 128]
    win = jnp.tile(w, (1, 128))[:, :128 * 767].reshape(8, 128, 767)[:, :, :WIN]
    qh = np.arange(128)[:, None] // CHUNK
    kc = np.arange(WIN)[None, :] // CHUNK
    valid = (kc >= qh) & (kc <= qh + 8)
    return jnp.where(jnp.asarray(valid)[None], win, NEG)


SMALL = ("b_ada", "norm_pre", "norm_post", "rel_bias", "conv_w", "conv_b", "lru_wa", "lru_ba", "lru_wx",
         "lru_bx", "lru_lambda")
WEIGHTS = ("w_ada", "b_ada", "norm_pre", "norm_post", "ffn1_w_gu", "ffn1_w_down", "w_in", "rel_bias", "conv_w",
           "conv_b", "lru_wa", "lru_ba", "lru_wx", "lru_bx", "lru_lambda", "w_att_o", "w_rec_o", "w_out",
           "ffn2_w_gu", "ffn2_w_down")


def kernel(x, c, w_ada, b_ada, norm_pre, norm_post, ffn1_w_gu, ffn1_w_down, w_in, rel_bias, conv_w, conv_b, lru_wa, lru_ba, lru_wx, lru_bx, lru_lambda, w_att_o, w_rec_o, w_out, ffn2_w_gu, ffn2_w_down, loss_target, m_w_ada, m_b_ada, m_norm_pre, m_norm_post, m_ffn1_w_gu, m_ffn1_w_down, m_w_in, m_rel_bias, m_conv_w, m_conv_b, m_lru_wa, m_lru_ba, m_lru_wx, m_lru_bx, m_lru_lambda, m_w_att_o, m_w_rec_o, m_w_out, m_ffn2_w_gu, m_ffn2_w_down, v_w_ada, v_b_ada, v_norm_pre, v_norm_post, v_ffn1_w_gu, v_ffn1_w_down, v_w_in, v_rel_bias, v_conv_w, v_conv_b, v_lru_wa, v_lru_ba, v_lru_wx, v_lru_bx, v_lru_lambda, v_w_att_o, v_w_rec_o, v_w_out, v_ffn2_w_gu, v_ffn2_w_down):
    W = dict(w_ada=w_ada, b_ada=b_ada, norm_pre=norm_pre, norm_post=norm_post, ffn1_w_gu=ffn1_w_gu,
             ffn1_w_down=ffn1_w_down, w_in=w_in, rel_bias=rel_bias, conv_w=conv_w, conv_b=conv_b, lru_wa=lru_wa,
             lru_ba=lru_ba, lru_wx=lru_wx, lru_bx=lru_bx, lru_lambda=lru_lambda, w_att_o=w_att_o, w_rec_o=w_rec_o,
             w_out=w_out, ffn2_w_gu=ffn2_w_gu, ffn2_w_down=ffn2_w_down)
    M = dict(w_ada=m_w_ada, b_ada=m_b_ada, norm_pre=m_norm_pre, norm_post=m_norm_post, ffn1_w_gu=m_ffn1_w_gu,
             ffn1_w_down=m_ffn1_w_down, w_in=m_w_in, rel_bias=m_rel_bias, conv_w=m_conv_w, conv_b=m_conv_b,
             lru_wa=m_lru_wa, lru_ba=m_lru_ba, lru_wx=m_lru_wx, lru_bx=m_lru_bx, lru_lambda=m_lru_lambda,
             w_att_o=m_w_att_o, w_rec_o=m_w_rec_o, w_out=m_w_out, ffn2_w_gu=m_ffn2_w_gu, ffn2_w_down=m_ffn2_w_down)
    V = dict(w_ada=v_w_ada, b_ada=v_b_ada, norm_pre=v_norm_pre, norm_post=v_norm_post, ffn1_w_gu=v_ffn1_w_gu,
             ffn1_w_down=v_ffn1_w_down, w_in=v_w_in, rel_bias=v_rel_bias, conv_w=v_conv_w, conv_b=v_conv_b,
             lru_wa=v_lru_wa, lru_ba=v_lru_ba, lru_wx=v_lru_wx, lru_bx=v_lru_bx, lru_lambda=v_lru_lambda,
             w_att_o=v_w_att_o, w_rec_o=v_w_rec_o, w_out=v_w_out, ffn2_w_gu=v_ffn2_w_gu, ffn2_w_down=v_ffn2_w_down)
    mx, my, mc = _mesh_pos()
    p = 2 * mx + my
    e = 4 * mx + 2 * my + mc
    xs = x[0]

    c_arr = jnp.reshape(mc, (1,)).astype(jnp.int32)
    cp_arr = jnp.stack([mc, p]).astype(jnp.int32)
    p_arr = jnp.reshape(p, (1,)).astype(jnp.int32)
    direct = ("w_att_o", "w_rec_o", "w_out", "ffn2_w_gu", "ffn2_w_down")
    geoms = [(kind, R, C, n in direct) for (n, kind, R, C) in BIG]
    names = [b[0] for b in BIG]
    placed = [ag_local(W[n][0], kind, R, C, p_arr, "ag_local_" + n) for (n, kind, R, C) in BIG[:2]]

    def arrived(fly, lo, hi, ssem, rsem, after, tag):
        done = ag_wait(fly, geoms[lo:hi], ssem, rsem, after, "ag_wait_" + tag)
        return [a if both else ag_forward(a, kind, R, C, "ag_forward_" + n)
                for a, (kind, R, C, both), n in zip(done, geoms[lo:hi], names[lo:hi])]

    g1 = ag_small(_pack([c, norm_pre, norm_post, conv_w], 32), "ag_small_params")
    c_all, npre4, npost4, cw4 = _unpack(g1, [(D,), (3, 256), (3, 256), (4, 256)])
    chipwise = lambda a: jnp.moveaxis(a[0::2], 0, 1).reshape(a.shape[1], D)
    npre, npost, conv_full = chipwise(npre4), chipwise(npost4), chipwise(cw4)

    b_cols = lax.dynamic_slice(b_ada, (0, p * 2304), (1, 2304))
    mod_cols = ada_fwd(c_all, w_ada[0], b_cols, "ada_fwd")
    g2 = ag_small(mod_cols.reshape(144, 128), "ag_mod")
    mod_all = jnp.moveaxis(g2[0::2].reshape(4, 8, 2304), 0, 1).reshape(8, 9 * D)
    mod = lax.dynamic_index_in_dim(mod_all, e, 0, keepdims=False).reshape(3, 3, D)
    zeros3 = jnp.zeros((3, D), F32)
    vecs = [jnp.concatenate([npre[k:k + 1], npost[k:k + 1], mod[k], zeros3], axis=0) for k in range(3)]

    gu_s, gu_r, gu_fly, tok_gu = ag_start(placed[:1], geoms[:1], [g2], "ag_start_ffn1_gu")
    dn_s, dn_r, dn_fly, tok0 = ag_start(placed[1:2], geoms[1:2], [tok_gu], "ag_start_ffn1_down")
    placed += [ag_local(W[n][0], kind, R, C, p_arr, "ag_local_" + n, after=[tok0]) for (n, kind, R, C) in BIG[2:]]
    f1_gu, = arrived(gu_fly, 0, 1, gu_s, gu_r, placed[2:], "ffn1_gu")
    f1_dn, = arrived(dn_fly, 1, 2, dn_s, dn_r, f1_gu, "ffn1_down")
    mix_s, mix_r, mix_fly, tok1 = ag_start(placed[2:6], geoms[2:6], [f1_gu, f1_dn], "ag_start_mixer")
    ffn_s, ffn_r, ffn_fly, tok2 = ag_start(placed[6:], geoms[6:], [tok1], "ag_start_ffn2")
    wa_bd = _block_diag4(lru_wa[0]).astype(BF16)
    wx_bd = _block_diag4(lru_wx[0]).astype(BF16)
    pvec = jnp.concatenate([conv_full, conv_b, lru_ba, lru_bx, lru_lambda], axis=0)
    bias = _bias_window(rel_bias[0]).reshape(4, 256, WIN)

    f1_u = f1_gu[:, FF:]
    x1, h1, g1_, u1, a1, f1 = ffn_fwd(xs, vecs[0] + tok2[0:1, 0:1], f1_gu, f1_u, f1_dn, 0.5, "ffn1_fwd")
    win, wao, wro, wout = arrived(mix_fly, 2, 6, mix_s, mix_r, x1, "mixer")
    h2, qkv, rest = proj_fwd(x1, vecs[1], win, "proj_fwd")
    ao = attn_fwd(qkv, bias, "attn_fwd")
    hl, hg = lru_fwd(rest, pvec, wa_bd, wx_bd, "lru_fwd")
    x2, att, rec, mg, f2 = mix_out_fwd(x1, ao, hg, rest, vecs[1], wao, wro, wout, "mix_out_fwd")
    f2_gu, f2_dn = arrived(ffn_fly, 6, 8, ffn_s, ffn_r, x2, "ffn2")
    f2_u = f2_gu[:, FF:]
    dy, h3, g3_, u3, a3, f3, lvec = ffn_fwd(x2, vecs[2], f2_gu, f2_u, f2_dn, 0.5, "ffn2_fwd", tgt=loss_target[0])

    G, grads = {}, {}
    geo = {n: (kind, R, C) for (n, kind, R, C) in BIG}

    def reduce_begin(names, tag):
        ps, rb = [], []
        for n in names:
            got = pair_push(G[n], geo[n][0], c_arr, "rs_push_" + n)
            a, b = pair_add(G[n], got, geo[n][0], cp_arr, "rs_pair_sum_" + n)
            ps.append(a)
            rb.append(b)
        return rs_start(ps, rb, [], "rs_start_" + tag)

    def reduce_end(names, flight, after, tag):
        ssem, rsem, ps, rb, _ = flight
        for a, n in zip(rs_wait(ps, rb, ssem, rsem, after, "rs_wait_" + tag), names):
            grads[n] = sum_share(a, *geo[n], "rs_sum_share_" + n)[None]

    dx2, df3, dgu3, va2 = ffn_bwd(dy, x2, f3, g3_, u3, vecs[2], f2_gu, f2_u, f2_dn, 0.5, "ffn2_bwd")
    G["ffn2_w_gu"] = mm_tn(h3, dgu3, "dw_ffn2_gu", D, 1408, 2048)
    G["ffn2_w_down"] = mm_tn(a3, df3, "dw_ffn2_down", 1408, D, 2048)
    fly_ffn2 = reduce_begin(("ffn2_w_gu", "ffn2_w_down"), "ffn2")
    vec1 = vecs[1] + fly_ffn2[4][0:1, 0:1]
    df2, d_att, d_rec, dao, dhl, d3, va_out = mix_out_bwd(dx2, f2, att, rec, rest, hl, vec1, wao, wro, wout,
                                                          "mix_out_bwd")
    G["w_out"] = mm_tn(mg, df2, "dw_out", D, D, 1024)
    G["w_att_o"] = mm_tn(ao, d_att, "dw_att_o", 512, D, 1024)
    G["w_rec_o"] = mm_tn(hg, d_rec, "dw_rec_o", D, D, 1024)
    dq, db, dkv = attn_bwd(qkv, dao, bias, "attn_bwd")
    dxr, v_lru, dwa_bd, dwx_bd = lru_bwd(dhl, hl, rest, pvec, wa_bd, wx_bd, "lru_bwd")
    dx1, va_in = proj_bwd(dq, dkv, dxr, d3, win, x1, dx2, vecs[1], "proj_bwd")
    G["w_in"] = dw_in(h2, dq, dkv, dxr, d3, "dw_in")
    fly_mix = reduce_begin(("w_in", "w_att_o", "w_rec_o", "w_out"), "mixer")
    vec0 = vecs[0] + fly_mix[4][0:1, 0:1]
    dx0, df1, dgu1, va0 = ffn_bwd(dx1, xs, f1, g1_, u1, vec0, f1_gu, f1_u, f1_dn, 0.5, "ffn1_bwd")
    G["ffn1_w_gu"] = mm_tn(h1, dgu1, "dw_ffn1_gu", D, 1408, 2048)
    G["ffn1_w_down"] = mm_tn(a1, df1, "dw_ffn1_down", 1408, D, 2048)
    fly_ffn1 = reduce_begin(("ffn1_w_gu", "ffn1_w_down"), "ffn1")
    reduce_end(("ffn2_w_gu", "ffn2_w_down"), fly_ffn2, [fly_ffn1[4]], "ffn2")
    reduce_end(("w_in", "w_att_o", "w_rec_o", "w_out"), fly_mix, [fly_ffn1[4], grads["ffn2_w_down"]], "mixer")

    va1 = va_out + va_in
    vas = (va0, va1, va2)
    dmod = jnp.stack([v[2:5] for v in vas])
    part = {"b_ada": dmod, "norm_pre": jnp.stack([v[0] for v in vas]), "norm_post": jnp.stack([v[1] for v in vas]),
            "rel_bias": bias_grad(db.reshape(8, 128, WIN), "bias_grad")[:, :257], "conv_w": v_lru[0:4], "conv_b": v_lru[4],
            "lru_wa": _diag_blocks(dwa_bd), "lru_ba": v_lru[5], "lru_wx": _diag_blocks(dwx_bd), "lru_bx": v_lru[6],
            "lru_lambda": v_lru[7]}
    full_shapes = {"b_ada": (9 * D,), "norm_pre": (3, D), "norm_post": (3, D), "rel_bias": (8, 257),
                   "conv_w": (4, D), "conv_b": (D,), "lru_wa": (16, 64, 64), "lru_ba": (D,),
                   "lru_wx": (16, 64, 64), "lru_bx": (D,), "lru_lambda": (D,)}
    g3 = ag_small(_pack([part[n] for n in SMALL] + [lvec[0:1, 0:1]], 1232), "ag_small_grads")
    summed = _unpack(sum_lead(g3, "sum_small_grads"), [full_shapes[n] for n in SMALL] + [(1,)])
    red = dict(zip(SMALL, summed[:-1]))
    loss = summed[-1][0]
    cols = lambda a: lax.dynamic_slice(a, (0, p * 256), (a.shape[0], 256))
    grads.update({"b_ada": red["b_ada"][None], "norm_pre": cols(red["norm_pre"])[None],
                  "norm_post": cols(red["norm_post"])[None], "rel_bias": red["rel_bias"][None],
                  "conv_w": cols(red["conv_w"])[None], "conv_b": red["conv_b"][None], "lru_wa": red["lru_wa"][None],
                  "lru_ba": red["lru_ba"][None], "lru_wx": red["lru_wx"][None], "lru_bx": red["lru_bx"][None],
                  "lru_lambda": red["lru_lambda"][None]})

    dmod_all = g3[:, :72].reshape(8, 9 * D)
    dmod_cols = jnp.pad(lax.dynamic_slice(dmod_all, (0, p * 2304), (8, 2304)), ((0, 120), (0, 0)))
    c_all_t = jnp.pad(c_all.T, ((0, 0), (0, 120)))
    grads["w_ada"] = ada_bwd(c_all_t, dmod_cols, "ada_bwd")[None]

    delta, new_m, new_v = {}, {}, {}

    def update(n):
        shp = W[n].shape
        res = adamw(W[n][0], grads[n][0], M[n][0], V[n][0], "adamw_" + n, emit_g=n in geo)
        delta[n], new_m[n], new_v[n] = [a.reshape(shp) for a in res[:3]]
        if n in geo:
            grads[n] = res[3].reshape(shp)

    for n in ("w_ada", "ffn2_w_gu", "ffn2_w_down", "w_in", "w_att_o", "w_rec_o", "w_out"):
        update(n)
    packed = [_pack([src[n] for n in SMALL], 1168) for src in (W, grads, M, V)]
    outs = adamw(*packed, "adamw_small")
    for dst, blk in zip((delta, new_m, new_v), outs):
        for n, a in zip(SMALL, _unpack(blk, [W[n].shape for n in SMALL])):
            dst[n] = a
    reduce_end(("ffn1_w_gu", "ffn1_w_down"), fly_ffn1,
               [outs[0], delta["w_ada"], delta["ffn2_w_gu"], delta["ffn2_w_down"], delta["w_in"], delta["w_out"]], "ffn1")
    for n in ("ffn1_w_gu", "ffn1_w_down"):
        update(n)

    return (loss, dx0[None], *[grads[n] for n in WEIGHTS], *[delta[n] for n in WEIGHTS],
            *[new_m[n] for n in WEIGHTS], *[new_v[n] for n in WEIGHTS])
```

```python
import functools

import numpy as np
import jax
import jax.numpy as jnp
from jax import lax
from jax.experimental import pallas as pl
from jax.experimental.pallas import tpu as pltpu

F32 = jnp.float32
BF16 = jnp.bfloat16

D = 1024
FF = 2816
PW = 5632
HP = 128
CHUNK = 64
WIN = 640
TQ = 512
EPS = 1e-6
NEG = -1e30
LRU_C = 8.0
N_DEV = 8
VMEM_LIMIT = 56 * 1024 * 1024

ADAM_LR, ADAM_B1, ADAM_B2, ADAM_EPS, ADAM_WD, ADAM_STEP = 0.001, 0.9, 0.999, 1e-08, 0.01, 10

MESH = pl.DeviceIdType.MESH
ANY = pl.BlockSpec(memory_space=pl.ANY)


def _cp(*sem):
    return pltpu.CompilerParams(dimension_semantics=tuple(sem), vmem_limit_bytes=VMEM_LIMIT)


def _dot(a, b):
    return jnp.dot(a, b, preferred_element_type=F32)


def _dot_nt(a, b):
    return lax.dot_general(a, b, (((1,), (1,)), ((), ())), preferred_element_type=F32)


def _dot_tn(a, b):
    return lax.dot_general(a, b, (((0,), (0,)), ((), ())), preferred_element_type=F32)


def _mean(v):
    return jnp.mean(v, axis=-1, keepdims=True)


def _colsum(v):
    return jnp.sum(v, axis=0, keepdims=True)


def _sigmoid(v):
    return 0.5 * jnp.tanh(0.5 * v) + 0.5


_GK = 0.7978845608028654


def _gelu(v):
    t = jnp.tanh(_GK * (v + 0.044715 * v * v * v))
    return 0.5 * v * (1.0 + t)


def _pre_norm(xv, vec_ref):
    r = lax.rsqrt(_mean(xv * xv) + EPS)
    n = xv * r * vec_ref[0:1, :]
    return n * (1.0 + vec_ref[3:4, :]) + vec_ref[2:3, :]


def _pre_norm_bwd(dh, xv, dres, vec_ref, vacc_ref):
    r = lax.rsqrt(_mean(xv * xv) + EPS)
    xh = xv * r
    n = xh * vec_ref[0:1, :]
    vacc_ref[2:3, :] += _colsum(dh)
    vacc_ref[3:4, :] += _colsum(dh * n)
    dn = dh * (1.0 + vec_ref[3:4, :])
    vacc_ref[0:1, :] += _colsum(dn * xh)
    dxh = dn * vec_ref[0:1, :]
    return r * (dxh - xh * _mean(dxh * xh)) + dres


def _post_norm_bwd(dxo, fv, res, vec_ref, vacc_ref):
    rf = lax.rsqrt(_mean(fv * fv) + EPS)
    fh = fv * rf
    gp = vec_ref[1:2, :]
    vacc_ref[4:5, :] += _colsum(res * dxo * (fh * gp))
    dy = (res * vec_ref[4:5, :]) * dxo
    vacc_ref[1:2, :] += _colsum(dy * fh)
    dfn = dy * gp
    return rf * (dfn - fh * _mean(dfn * fh))


def ffn_fwd(x, vec, w_gu, w_u, w_dn, res, name, tgt=None, tm=1024, tf=512):
    S = x.shape[0]
    tm = min(tm, S)
    nt = S // tm
    nf = -(-FF // tf)
    tail = FF - tf * (nf - 1)
    halves = [pl.ds(r * (tm // 2), tm // 2) for r in range(2)]
    head = tgt is not None

    def body(*refs):
        x_ref, vec_ref, wg_ref, wu_ref, wd_ref = refs[:5]
        if head:
            t_ref, xo_ref, h_ref, g_ref, u_ref, a_ref, f_ref, l_ref, hs, acc, lacc = refs[5:]
        else:
            xo_ref, h_ref, g_ref, u_ref, a_ref, f_ref, hs, acc = refs[5:]
        i, j = pl.program_id(0), pl.program_id(1)

        @pl.when(j == 0)
        def _():
            h = _pre_norm(x_ref[...], vec_ref).astype(BF16)
            hs[...] = h
            h_ref[...] = h
            acc[...] = jnp.zeros_like(acc)

        def chunk(w):
            gu = [(_dot(hs[r, :], wg_ref[:, 0:w]), _dot(hs[r, :], wu_ref[:, 0:w])) for r in halves]
            acts = []
            for r, (g, u) in zip(halves, gu):
                g_ref[r, 0:w] = g.astype(BF16)
                u_ref[r, 0:w] = u.astype(BF16)
                a = (g * _sigmoid(g) * u).astype(BF16)
                a_ref[r, 0:w] = a
                acts.append(a)
            for r, a in zip(halves, acts):
                acc[r, :] += _dot(a, wd_ref[0:w, :])

        @pl.when(j < nf - 1)
        def _():
            chunk(tf)

        @pl.when(j == nf - 1)
        def _():
            chunk(tail)
            f = acc[...]
            f_ref[...] = f.astype(BF16)
            y = f * lax.rsqrt(_mean(f * f) + EPS) * vec_ref[1:2, :]
            xo = x_ref[...] + (res * vec_ref[4:5, :]) * y
            if head:
                @pl.when(i == 0)
                def _():
                    lacc[...] = jnp.zeros_like(lacc)

                d = xo - t_ref[...]
                xo_ref[...] = d * (1.0 / D)
                lacc[...] += _colsum(d * d)

                @pl.when(i == nt - 1)
                def _():
                    l_ref[...] = jnp.broadcast_to(0.5 * jnp.sum(lacc[...]) * (1.0 / D), (8, 128))
            else:
                xo_ref[...] = xo

    row = lambda i, j: (i, 0)
    col = lambda i, j: (i, j)
    once = dict(pipeline_mode=pl.Buffered(1)) if head else {}
    in_specs = [pl.BlockSpec((tm, D), row, **once), pl.BlockSpec((8, D), lambda i, j: (0, 0)),
                pl.BlockSpec((D, tf), lambda i, j: (0, j)), pl.BlockSpec((D, tf), lambda i, j: (0, j)),
                pl.BlockSpec((tf, D), lambda i, j: (j, 0))]
    out_specs = [pl.BlockSpec((tm, D), row), pl.BlockSpec((tm, D), row), pl.BlockSpec((tm, tf), col),
                 pl.BlockSpec((tm, tf), col), pl.BlockSpec((tm, tf), col), pl.BlockSpec((tm, D), row)]
    out_shape = [jax.ShapeDtypeStruct((S, D), F32), jax.ShapeDtypeStruct((S, D), BF16),
                 jax.ShapeDtypeStruct((S, FF), BF16), jax.ShapeDtypeStruct((S, FF), BF16),
                 jax.ShapeDtypeStruct((S, FF), BF16), jax.ShapeDtypeStruct((S, D), BF16)]
    scratch = [pltpu.VMEM((tm, D), BF16), pltpu.VMEM((tm, D), F32)]
    args = [x, vec, w_gu, w_u, w_dn]
    if head:
        in_specs.append(pl.BlockSpec((tm, D), row, **once))
        out_specs.append(pl.BlockSpec((8, 128), lambda i, j: (0, 0)))
        out_shape.append(jax.ShapeDtypeStruct((8, 128), F32))
        scratch.append(pltpu.VMEM((1, D), F32))
        args.append(tgt)
    return pl.pallas_call(
        body, name=name, grid=(nt, nf), in_specs=in_specs, out_specs=out_specs, out_shape=out_shape,
        scratch_shapes=scratch,
        compiler_params=_cp("arbitrary" if head else "parallel", "arbitrary"),
    )(*args)


def ffn_bwd(dxo, x, f, g, u, vec, w_gu, w_u, w_dn, res, name, tm=1024, tf=512):
    S = x.shape[0]
    tm = min(tm, S)
    nf = -(-FF // tf)
    tail = FF - tf * (nf - 1)
    halves = [pl.ds(r * (tm // 2), tm // 2) for r in range(2)]

    def body(dxo_ref, x_ref, f_ref, g_ref, u_ref, vec_ref, wg_ref, wu_ref, wd_ref,
             dx_ref, df_ref, dgu_ref, vacc_ref, dfs, acc):
        i, j = pl.program_id(0), pl.program_id(1)

        @pl.when((i == 0) & (j == 0))
        def _():
            vacc_ref[...] = jnp.zeros_like(vacc_ref)

        @pl.when(j == 0)
        def _():
            df = _post_norm_bwd(dxo_ref[...], f_ref[...].astype(F32), res, vec_ref, vacc_ref).astype(BF16)
            dfs[...] = df
            df_ref[...] = df
            acc[...] = jnp.zeros_like(acc)

        def chunk(w):
            da = [_dot_nt(dfs[h, :], wd_ref[0:w, :]) for h in halves]
            dgu = []
            for h, d in zip(halves, da):
                gv, uv = g_ref[h, 0:w].astype(F32), u_ref[h, 0:w].astype(F32)
                sg = _sigmoid(gv)
                dg = (d * uv * (sg * (1.0 + gv * (1.0 - sg)))).astype(BF16)
                du = (d * (gv * sg)).astype(BF16)
                dgu_ref[0, h, 0:w] = dg
                dgu_ref[1, h, 0:w] = du
                dgu.append((dg, du))
            for h, (dg, du) in zip(halves, dgu):
                acc[h, :] += _dot_nt(dg, wg_ref[:, 0:w]) + _dot_nt(du, wu_ref[:, 0:w])

        @pl.when(j < nf - 1)
        def _():
            chunk(tf)

        @pl.when(j == nf - 1)
        def _():
            chunk(tail)
            dx_ref[...] = _pre_norm_bwd(acc[...], x_ref[...], dxo_ref[...], vec_ref, vacc_ref)

    row = lambda i, j: (i, 0)
    col = lambda i, j: (i, j)
    return pl.pallas_call(
        body, name=name, grid=(S // tm, nf),
        in_specs=[pl.BlockSpec((tm, D), row), pl.BlockSpec((tm, D), row, pipeline_mode=pl.Buffered(1)),
                  pl.BlockSpec((tm, D), row)]
        + [pl.BlockSpec((tm, tf), col), pl.BlockSpec((tm, tf), col),
                  pl.BlockSpec((8, D), lambda i, j: (0, 0)),
                  pl.BlockSpec((D, tf), lambda i, j: (0, j)), pl.BlockSpec((D, tf), lambda i, j: (0, j)),
                  pl.BlockSpec((tf, D), lambda i, j: (j, 0))],
        out_specs=[pl.BlockSpec((tm, D), row), pl.BlockSpec((tm, D), row),
                   pl.BlockSpec((2, tm, tf), lambda i, j: (0, i, j)),
                   pl.BlockSpec((8, D), lambda i, j: (0, 0))],
        out_shape=[jax.ShapeDtypeStruct((S, D), F32), jax.ShapeDtypeStruct((S, D), BF16),
                   jax.ShapeDtypeStruct((2, S, FF), BF16), jax.ShapeDtypeStruct((8, D), F32)],
        scratch_shapes=[pltpu.VMEM((tm, D), BF16), pltpu.VMEM((tm, D), F32)],
        compiler_params=_cp("arbitrary", "arbitrary"),
    )(dxo, x, f, g, u, vec, w_gu, w_u, w_dn)


def mm_tn(a, b, name, tm, tn, tk, out_dtype=BF16):
    S, M = a.shape
    if b.ndim == 3:
        G, _, Nf = b.shape
    else:
        G, Nf = 1, b.shape[1]
    N = G * Nf
    tk = min(tk, S)
    nbf = Nf // tn
    nk = S // tk

    def body(a_ref, b_ref, o_ref, acc):
        k = pl.program_id(2)

        @pl.when(k == 0)
        def _():
            acc[...] = jnp.zeros_like(acc)

        acc[...] += _dot_tn(a_ref[...], b_ref[...])

        @pl.when(k == nk - 1)
        def _():
            o_ref[...] = acc[...].astype(out_dtype)

    if b.ndim == 3:
        b_spec = pl.BlockSpec((None, tk, tn), lambda i, j, k: (j // nbf, k, j % nbf))
    else:
        b_spec = pl.BlockSpec((tk, tn), lambda i, j, k: (k, j))
    return pl.pallas_call(
        body, name=name, grid=(M // tm, N // tn, nk),
        in_specs=[pl.BlockSpec((tk, tm), lambda i, j, k: (k, i)), b_spec],
        out_specs=pl.BlockSpec((tm, tn), lambda i, j, k: (i, j)),
        out_shape=jax.ShapeDtypeStruct((M, N), out_dtype),
        scratch_shapes=[pltpu.VMEM((tm, tn), F32)],
        compiler_params=_cp("parallel", "parallel", "arbitrary"),
    )(a, b)


def proj_fwd(x, vec, w_in, name, tm=2048, tn=512):
    S = x.shape[0]
    tm = min(tm, S)
    nq = 1536 // tn

    def body(x_ref, vec_ref, w_ref, h_ref, qkv_ref, rest_ref, hs):
        j = pl.program_id(1)

        @pl.when(j == 0)
        def _():
            h = _pre_norm(x_ref[...], vec_ref).astype(BF16)
            hs[...] = h
            h_ref[...] = h

        r = _dot(hs[...], w_ref[...])

        @pl.when(j < nq)
        def _():
            qkv_ref[...] = r.astype(BF16)

        @pl.when(j >= nq)
        def _():
            rest_ref[...] = r.astype(BF16)

    row = lambda i, j: (i, 0)
    return pl.pallas_call(
        body, name=name, grid=(S // tm, PW // tn),
        in_specs=[pl.BlockSpec((tm, D), row), pl.BlockSpec((8, D), lambda i, j: (0, 0)),
                  pl.BlockSpec((D, tn), lambda i, j: (0, j))],
        out_specs=[pl.BlockSpec((tm, D), row),
                   pl.BlockSpec((tm, tn), lambda i, j: (i, jnp.minimum(j, nq - 1))),
                   pl.BlockSpec((tm, tn), lambda i, j: (i, jnp.maximum(j - nq, 0)))],
        out_shape=[jax.ShapeDtypeStruct((S, D), BF16), jax.ShapeDtypeStruct((S, 1536), BF16),
                   jax.ShapeDtypeStruct((S, 4096), BF16)],
        scratch_shapes=[pltpu.VMEM((tm, D), BF16)],
        compiler_params=_cp("parallel", "arbitrary"),
    )(x, vec, w_in)


def proj_bwd(dq, dkv, dxr, d3, w_in, x, dxo, vec, name, tm=2048, tk=512):
    S = x.shape[0]
    tm = min(tm, S)
    nk = PW // tk

    def body(dq_ref, dkv_ref, dxr_ref, d3_ref, w_ref, x_ref, dxo_ref, vec_ref, dx_ref, vacc_ref, acc):
        i, j = pl.program_id(0), pl.program_id(1)

        @pl.when((i == 0) & (j == 0))
        def _():
            vacc_ref[...] = jnp.zeros_like(vacc_ref)

        @pl.when(j == 0)
        def _():
            acc[...] = _dot_nt(dq_ref[...], w_ref[...])

        @pl.when((j >= 1) & (j < 3))
        def _():
            acc[...] += _dot_nt(dkv_ref[...], w_ref[...])

        @pl.when((j >= 3) & (j < 5))
        def _():
            acc[...] += _dot_nt(dxr_ref[...], w_ref[...])

        @pl.when(j >= 5)
        def _():
            acc[...] += _dot_nt(d3_ref[...], w_ref[...])

        @pl.when(j == nk - 1)
        def _():
            dx_ref[...] = _pre_norm_bwd(acc[...], x_ref[...], dxo_ref[...], vec_ref, vacc_ref)

    row = lambda i, j: (i, 0)
    return pl.pallas_call(
        body, name=name, grid=(S // tm, nk),
        in_specs=[pl.BlockSpec((None, tm, tk), lambda i, j: (0, i, 0)),
                  pl.BlockSpec((None, tm, tk), lambda i, j: (jnp.clip(j - 1, 0, 1), i, 0)),
                  pl.BlockSpec((tm, tk), lambda i, j: (i, jnp.clip(j - 3, 0, 1))),
                  pl.BlockSpec((None, tm, tk), lambda i, j: (jnp.clip(j - 5, 0, 5) // 2, i, jnp.clip(j - 5, 0, 5) % 2)),
                  pl.BlockSpec((D, tk), lambda i, j: (0, j)),
                  pl.BlockSpec((tm, D), row, pipeline_mode=pl.Buffered(1)),
                  pl.BlockSpec((tm, D), row, pipeline_mode=pl.Buffered(1)),
                  pl.BlockSpec((8, D), lambda i, j: (0, 0))],
        out_specs=[pl.BlockSpec((tm, D), row, pipeline_mode=pl.Buffered(1)),
                   pl.BlockSpec((8, D), lambda i, j: (0, 0))],
        out_shape=[jax.ShapeDtypeStruct((S, D), F32), jax.ShapeDtypeStruct((8, D), F32)],
        scratch_shapes=[pltpu.VMEM((tm, D), F32)],
        compiler_params=_cp("arbitrary", "arbitrary"),
    )(dq, dkv, dxr, d3, w_in, x, dxo, vec)


def _two_heads(v, lane):
    zero = jnp.zeros((), v.dtype)
    return jnp.concatenate([jnp.where(lane < 64, v, zero), jnp.where(lane >= 64, v, zero)], axis=0)


def _attn_scores(qm, ka, bias_h, i, grp):
    s = _dot_nt(qm, ka) + bias_h
    col = lax.broadcasted_iota(jnp.int32, s.shape, 1)
    first_key = jnp.where(i == 0, 512 - 128 * grp, 0)
    return jnp.where(col >= first_key, s, NEG)


def _softmax(s):
    e = jnp.exp(s - jnp.max(s, axis=-1, keepdims=True))
    return e * (1.0 / jnp.sum(e, axis=-1, keepdims=True))


NG = TQ // 128


def attn_fwd(qkv, bias, name):
    S = qkv.shape[0]
    nb = S // TQ

    def body(q_ref, kp_ref, kc_ref, vp_ref, vc_ref, b_ref, o_ref, kw, vw):
        i = pl.program_id(1)
        kw[0:TQ, :] = kp_ref[...]
        kw[TQ:2 * TQ, :] = kc_ref[...]
        vw[0:TQ, :] = vp_ref[...]
        vw[TQ:2 * TQ, :] = vc_ref[...]
        lane = lax.broadcasted_iota(jnp.int32, (1, HP), 1)

        rows = [pl.ds(128 * a, 128) for a in range(NG)]
        keys = [pl.ds(128 * a, WIN) for a in range(NG)]
        q2 = [_two_heads(q_ref[r, :] * jnp.asarray(0.125, BF16), lane) for r in rows]
        s = [_attn_scores(q2[a], kw[keys[a], :], b_ref[...], i, a) for a in range(NG)]
        p = [_softmax(sa).astype(BF16) for sa in s]
        o2 = [_dot(p[a], vw[keys[a], :]) for a in range(NG)]
        for a in range(NG):
            o_ref[rows[a], :] = jnp.where(lane < 64, o2[a][0:128], o2[a][128:256]).astype(BF16)

    prev = lambda h, i: (jnp.maximum(i - 1, 0), 0)
    return pl.pallas_call(
        body, name=name, grid=(4, nb),
        in_specs=[pl.BlockSpec((TQ, HP), lambda h, i: (i, h)),
                  pl.BlockSpec((TQ, HP), lambda h, i: (jnp.maximum(i - 1, 0), 4 + h)),
                  pl.BlockSpec((TQ, HP), lambda h, i: (i, 4 + h)),
                  pl.BlockSpec((TQ, HP), lambda h, i: (jnp.maximum(i - 1, 0), 8 + h)),
                  pl.BlockSpec((TQ, HP), lambda h, i: (i, 8 + h)),
                  pl.BlockSpec((None, 256, WIN), lambda h, i: (h, 0, 0))],
        out_specs=pl.BlockSpec((TQ, HP), lambda h, i: (i, h)),
        out_shape=jax.ShapeDtypeStruct((S, 512), BF16),
        scratch_shapes=[pltpu.VMEM((2 * TQ, HP), BF16), pltpu.VMEM((2 * TQ, HP), BF16)],
        compiler_params=_cp("parallel", "arbitrary"),
    )(qkv, qkv, qkv, qkv, qkv, bias)


def attn_bwd(qkv, do, bias, name):
    S = qkv.shape[0]
    nb = S // TQ

    def body(q_ref, kp_ref, kc_ref, vp_ref, vc_ref, do_ref, b_ref, dqkv_ref, db_ref, dkv_ref, kw, vw, ak, av):
        i = pl.program_id(1)

        @pl.when(i == 0)
        def _():
            db_ref[...] = jnp.zeros_like(db_ref)
            ak[...] = jnp.zeros_like(ak)
            av[...] = jnp.zeros_like(av)

        @pl.when(i > 0)
        def _():
            ak[0:TQ, :] = ak[TQ:2 * TQ, :]
            av[0:TQ, :] = av[TQ:2 * TQ, :]
            ak[TQ:2 * TQ, :] = jnp.zeros((TQ, HP), F32)
            av[TQ:2 * TQ, :] = jnp.zeros((TQ, HP), F32)

        @pl.when(i < nb)
        def _():
            kw[0:TQ, :] = kp_ref[...]
            kw[TQ:2 * TQ, :] = kc_ref[...]
            vw[0:TQ, :] = vp_ref[...]
            vw[TQ:2 * TQ, :] = vc_ref[...]
            lane = lax.broadcasted_iota(jnp.int32, (1, HP), 1)

            rows = [pl.ds(128 * a, 128) for a in range(NG)]
            keys = [pl.ds(128 * a, WIN) for a in range(NG)]
            q2 = [_two_heads(q_ref[r, :] * jnp.asarray(0.125, BF16), lane) for r in rows]
            do2 = [_two_heads(do_ref[r, :], lane) for r in rows]
            s = [_attn_scores(q2[a], kw[keys[a], :], b_ref[...], i, a) for a in range(NG)]
            dp = [_dot_nt(do2[a], vw[keys[a], :]) for a in range(NG)]
            p = [_softmax(sa) for sa in s]
            ds = [p[a] * (dp[a] - jnp.sum(p[a] * dp[a], axis=-1, keepdims=True)) for a in range(NG)]
            db_ref[...] += (ds[0] + ds[1]) + (ds[2] + ds[3])
            dsb = [d.astype(BF16) for d in ds]
            dq2 = [_dot(dsb[a], kw[keys[a], :]) for a in range(NG)]
            dk = [_dot_tn(dsb[a], q2[a]) for a in range(NG)]
            dv = [_dot_tn(p[a].astype(BF16), do2[a]) for a in range(NG)]
            for a in range(NG):
                ak[keys[a], :] += dk[a]
                av[keys[a], :] += dv[a]
                dq = jnp.where(lane < 64, dq2[a][0:128], dq2[a][128:256])
                dqkv_ref[0, rows[a], :] = (dq * 0.125).astype(BF16)

        @pl.when(i > 0)
        def _():
            dkv_ref[0] = ak[0:TQ, :].astype(BF16)
            dkv_ref[1] = av[0:TQ, :].astype(BF16)

    cur = lambda i: jnp.minimum(i, nb - 1)
    prv = lambda i: jnp.clip(i - 1, 0, nb - 1)
    dq, db, dkv = pl.pallas_call(
        body, name=name, grid=(4, nb + 1),
        in_specs=[pl.BlockSpec((TQ, HP), lambda h, i: (cur(i), h)),
                  pl.BlockSpec((TQ, HP), lambda h, i: (prv(i), 4 + h)),
                  pl.BlockSpec((TQ, HP), lambda h, i: (cur(i), 4 + h)),
                  pl.BlockSpec((TQ, HP), lambda h, i: (prv(i), 8 + h)),
                  pl.BlockSpec((TQ, HP), lambda h, i: (cur(i), 8 + h)),
                  pl.BlockSpec((TQ, HP), lambda h, i: (cur(i), h)),
                  pl.BlockSpec((None, 256, WIN), lambda h, i: (h, 0, 0))],
        out_specs=[pl.BlockSpec((1, TQ, HP), lambda h, i: (0, cur(i), h)),
                   pl.BlockSpec((None, 256, WIN), lambda h, i: (h, 0, 0)),
                   pl.BlockSpec((2, TQ, HP), lambda h, i: (0, prv(i), h))],
        out_shape=[jax.ShapeDtypeStruct((1, S, 512), BF16), jax.ShapeDtypeStruct((4, 256, WIN), F32),
                   jax.ShapeDtypeStruct((2, S, 512), BF16)],
        scratch_shapes=[pltpu.VMEM((2 * TQ, HP), BF16), pltpu.VMEM((2 * TQ, HP), BF16),
                        pltpu.VMEM((2 * TQ, HP), F32), pltpu.VMEM((2 * TQ, HP), F32)],
        compiler_params=_cp("parallel", "arbitrary"),
    )(qkv, qkv, qkv, qkv, qkv, do, bias)
    return dq, db, dkv


def bias_grad(db, name):
    def body(db_ref, o_ref):
        r = lax.broadcasted_iota(jnp.int32, (128, 128), 0)
        c = lax.broadcasted_iota(jnp.int32, (128, 128), 1)
        flip = (r + c == 127).astype(BF16)
        lane = lax.broadcasted_iota(jnp.int32, (16, 384), 1)
        src = lax.broadcasted_iota(jnp.int32, (128, 384), 0)
        dst = lax.broadcasted_iota(jnp.int32, (128, 384), 1)

        def split_dot(v, m):
            hi = v.astype(BF16)
            r1 = v - hi.astype(F32)
            mid = r1.astype(BF16)
            lo = (r1 - mid.astype(F32)).astype(BF16)
            return _dot(hi, m) + _dot(mid, m) + _dot(lo, m)

        def diag_sums(w):
            y = pltpu.roll(split_dot(w, flip), 0, 1, stride=1, stride_axis=0)
            return jnp.broadcast_to(_colsum(y), (16, 128))

        w4 = db_ref[0, :, 512:640]
        w3 = db_ref[0, :, 384:512]
        far = jnp.sum(db_ref[0, :, 0:384]) + jnp.sum(jnp.where(r >= c, w3, 0.0))
        lo4 = diag_sums(jnp.where(r >= c, w4, 0.0))
        up4 = diag_sums(jnp.where(r < c, w4, 0.0))
        up3 = diag_sums(jnp.where(r < c, w3, 0.0))
        p_lo4 = (dst == 128 + (src + 1) % 128).astype(BF16)
        p_up4 = ((dst == src + 1) & (src < 127)).astype(BF16)
        p_up3 = ((dst == src + 129) & (src < 127)).astype(BF16)
        out = split_dot(lo4, p_lo4) + split_dot(up4, p_up4) + split_dot(up3, p_up3)
        o_ref[0] = out + jnp.where(lane == 256, far, 0.0)

    return pl.pallas_call(
        body, name=name, grid=(8,),
        in_specs=[pl.BlockSpec((1, 128, WIN), lambda h: (h, 0, 0))],
        out_specs=pl.BlockSpec((1, 16, 384), lambda h: (h, 0, 0)),
        out_shape=jax.ShapeDtypeStruct((8, 16, 384), F32),
        compiler_params=_cp("parallel"),
    )(db)[:, 0, :]


LT = 1024
LC = 512


def _lru_gates(xs, pv_ref, wa_ref, wx_ref, tl):
    xc = (pv_ref[4:5, :] + pv_ref[3:4, :] * xs[pl.ds(8, tl), :] + pv_ref[2:3, :] * xs[pl.ds(7, tl), :]
          + pv_ref[1:2, :] * xs[pl.ds(6, tl), :] + pv_ref[0:1, :] * xs[pl.ds(5, tl), :])
    xcb = xc.astype(BF16)
    pa = jnp.concatenate([_dot(xcb[:, 0:256], wa_ref[0]), _dot(xcb[:, 256:512], wa_ref[1])], axis=1)
    px = jnp.concatenate([_dot(xcb[:, 0:256], wx_ref[0]), _dot(xcb[:, 256:512], wx_ref[1])], axis=1)
    r = _sigmoid(pa + pv_ref[5:6, :])
    ig = _sigmoid(px + pv_ref[6:7, :])
    z = -pv_ref[7:8, :]
    sp = jnp.maximum(z, 0.0) + jnp.log1p(jnp.exp(-jnp.abs(z)))
    log_a = (-LRU_C * r) * sp
    a = jnp.exp(log_a)
    s = jnp.tanh(-log_a) * (1.0 + a * a)
    inv_mult = lax.rsqrt(s)
    mult = jnp.where(s > 0.0, s * inv_mult, 0.0)
    return xc, xcb, r, ig, sp, a, mult, inv_mult


def lru_fwd(rest, pvec, wa, wx, name):
    S = rest.shape[0]
    tl = min(LT, S)
    nt = S // tl

    def body(xr_ref, halo_ref, yr_ref, pv_ref, wa_ref, wx_ref, h_ref, hg_ref, xs, a_s, u_s, h_s, carry):
        ti = pl.program_id(1)

        @pl.when(ti == 0)
        def _():
            carry[...] = jnp.zeros_like(carry)

        xs[0:8, :] = jnp.where(ti > 0, halo_ref[8:16, :].astype(F32), 0.0)
        xs[pl.ds(8, tl), :] = xr_ref[...].astype(F32)
        xc, _, _, ig, _, a, mult, _ = _lru_gates(xs, pv_ref, wa_ref, wx_ref, tl)
        a_s[...] = a
        u_s[...] = mult * (ig * xc)
        row = lax.broadcasted_iota(jnp.int32, (8, LC), 0)

        def blk(bi, c):
            o = pl.multiple_of(bi * 8, 8)
            av = a_s[pl.ds(o, 8), :]
            bv = u_s[pl.ds(o, 8), :]
            for d in (1, 2, 4):
                a_sh = pltpu.roll(av, d, 0)
                b_sh = pltpu.roll(bv, d, 0)
                m = row >= d
                bv = jnp.where(m, av * b_sh + bv, bv)
                av = jnp.where(m, av * a_sh, av)
            hv = bv + av * c
            h_s[pl.ds(o, 8), :] = hv
            return hv[7:8, :]

        carry[...] = lax.fori_loop(0, tl // 8, blk, carry[...])
        h = h_s[...]
        h_ref[...] = h
        hg_ref[...] = (h * _gelu(yr_ref[...].astype(F32))).astype(BF16)

    hb = tl // 16
    return pl.pallas_call(
        body, name=name, grid=(2, nt),
        in_specs=[pl.BlockSpec((tl, LC), lambda c, t: (t, c)),
                  pl.BlockSpec((16, LC), lambda c, t: (jnp.maximum(t * hb - 1, 0), c)),
                  pl.BlockSpec((tl, LC), lambda c, t: (t, 2 + c)),
                  pl.BlockSpec((8, LC), lambda c, t: (0, c)),
                  pl.BlockSpec((2, 256, 256), lambda c, t: (c, 0, 0)),
                  pl.BlockSpec((2, 256, 256), lambda c, t: (c, 0, 0))],
        out_specs=[pl.BlockSpec((tl, LC), lambda c, t: (t, c)), pl.BlockSpec((tl, LC), lambda c, t: (t, c))],
        out_shape=[jax.ShapeDtypeStruct((S, D), F32), jax.ShapeDtypeStruct((S, D), BF16)],
        scratch_shapes=[pltpu.VMEM((tl + 8, LC), F32), pltpu.VMEM((tl, LC), F32), pltpu.VMEM((tl, LC), F32),
                        pltpu.VMEM((tl, LC), F32), pltpu.VMEM((1, LC), F32)],
        compiler_params=_cp("parallel", "arbitrary"),
    )(rest, rest, rest, pvec, wa, wx)


def lru_bwd(dh, h, rest, pvec, wa, wx, name):
    S = rest.shape[0]
    tl = min(LT, S)
    nt = S // tl

    def body(dh_ref, h_ref, hhalo_ref, xr_ref, xhalo_ref, pv_ref, wa_ref, wx_ref,
             dxr_ref, vacc_ref, dwa_ref, dwx_ref,
             xs, hs, a_s, ash_s, b_s, lam_s, dxe, anext, lnext, dxnext):
        ti = pl.program_id(1)
        tr = nt - 1 - ti

        @pl.when(ti == 0)
        def _():
            anext[...] = jnp.zeros_like(anext)
            lnext[...] = jnp.zeros_like(lnext)
            dxnext[...] = jnp.zeros_like(dxnext)
            vacc_ref[...] = jnp.zeros_like(vacc_ref)
            dwa_ref[...] = jnp.zeros_like(dwa_ref)
            dwx_ref[...] = jnp.zeros_like(dwx_ref)

        xs[0:8, :] = jnp.where(tr > 0, xhalo_ref[8:16, :].astype(F32), 0.0)
        xs[pl.ds(8, tl), :] = xr_ref[...].astype(F32)
        xc, xcb, r, ig, sp, a, mult, inv_mult = _lru_gates(xs, pv_ref, wa_ref, wx_ref, tl)

        a_s[pl.ds(0, tl), :] = a
        a_s[pl.ds(tl, 8), :] = jnp.broadcast_to(anext[...], (8, LC))
        ash_s[...] = a_s[pl.ds(1, tl), :]
        b_s[...] = dh_ref[...]
        row = lax.broadcasted_iota(jnp.int32, (8, LC), 0)

        def blk(k, c):
            o = pl.multiple_of((tl // 8 - 1 - k) * 8, 8)
            av = ash_s[pl.ds(o, 8), :]
            bv = b_s[pl.ds(o, 8), :]
            for d in (1, 2, 4):
                a_sh = pltpu.roll(av, 8 - d, 0)
                b_sh = pltpu.roll(bv, 8 - d, 0)
                m = row < 8 - d
                bv = jnp.where(m, bv + av * b_sh, bv)
                av = jnp.where(m, av * a_sh, av)
            lv = bv + av * c
            lam_s[pl.ds(o, 8), :] = lv
            return lv[0:1, :]

        lnext[...] = lax.fori_loop(0, tl // 8, blk, lnext[...])
        anext[...] = a[0:1, :]
        lam = lam_s[...]

        hs[0:8, :] = jnp.where(tr > 0, hhalo_ref[...], 0.0)
        hs[pl.ds(8, tl), :] = h_ref[...]
        d_a = lam * hs[pl.ds(7, tl), :]
        d_mult = lam * (ig * xc)
        d_ig = lam * mult * xc
        dxc = lam * mult * ig
        d_log_a = d_a * a - d_mult * (a * a) * inv_mult
        d_r = d_log_a * (-LRU_C * sp)
        vacc_ref[7:8, :] += _colsum(d_log_a * (-LRU_C * r)) * (-_sigmoid(-pv_ref[7:8, :]))
        d_pa = d_r * r * (1.0 - r)
        d_px = d_ig * ig * (1.0 - ig)
        vacc_ref[5:6, :] += _colsum(d_pa)
        vacc_ref[6:7, :] += _colsum(d_px)
        dpa = d_pa.astype(BF16)
        dpx = d_px.astype(BF16)
        back = []
        for g in range(2):
            sl = slice(256 * g, 256 * g + 256)
            dwa_ref[g] += _dot_tn(xcb[:, sl], dpa[:, sl])
            dwx_ref[g] += _dot_tn(xcb[:, sl], dpx[:, sl])
            back.append(_dot_nt(dpa[:, sl], wa_ref[g]) + _dot_nt(dpx[:, sl], wx_ref[g]))
        dxc = dxc + jnp.concatenate(back, axis=1)
        vacc_ref[4:5, :] += _colsum(dxc)
        for k in range(4):
            vacc_ref[k:k + 1, :] += _colsum(dxc * xs[pl.ds(5 + k, tl), :])
        dxe[pl.ds(0, tl), :] = dxc
        dxe[pl.ds(tl, 8), :] = dxnext[...]
        dxr = (pv_ref[3:4, :] * dxc + pv_ref[2:3, :] * dxe[pl.ds(1, tl), :]
               + pv_ref[1:2, :] * dxe[pl.ds(2, tl), :] + pv_ref[0:1, :] * dxe[pl.ds(3, tl), :])
        dxr_ref[...] = dxr.astype(BF16)
        dxnext[...] = dxc[0:8, :]

    hb = tl // 8
    rev = lambda t: nt - 1 - t
    halo = lambda t: jnp.maximum(rev(t) * hb - 1, 0)
    big = lambda: pltpu.VMEM((tl + 8, LC), F32)
    til = lambda: pltpu.VMEM((tl, LC), F32)
    return pl.pallas_call(
        body, name=name, grid=(2, nt),
        in_specs=[pl.BlockSpec((tl, LC), lambda c, t: (rev(t), c)),
                  pl.BlockSpec((tl, LC), lambda c, t: (rev(t), c)),
                  pl.BlockSpec((8, LC), lambda c, t: (halo(t), c)),
                  pl.BlockSpec((tl, LC), lambda c, t: (rev(t), c)),
                  pl.BlockSpec((16, LC), lambda c, t: (jnp.maximum(rev(t) * (tl // 16) - 1, 0), c)),
                  pl.BlockSpec((8, LC), lambda c, t: (0, c)),
                  pl.BlockSpec((2, 256, 256), lambda c, t: (c, 0, 0)),
                  pl.BlockSpec((2, 256, 256), lambda c, t: (c, 0, 0))],
        out_specs=[pl.BlockSpec((tl, LC), lambda c, t: (rev(t), c)),
                   pl.BlockSpec((8, LC), lambda c, t: (0, c)),
                   pl.BlockSpec((2, 256, 256), lambda c, t: (c, 0, 0)),
                   pl.BlockSpec((2, 256, 256), lambda c, t: (c, 0, 0))],
        out_shape=[jax.ShapeDtypeStruct((S, D), BF16), jax.ShapeDtypeStruct((8, D), F32),
                   jax.ShapeDtypeStruct((4, 256, 256), F32), jax.ShapeDtypeStruct((4, 256, 256), F32)],
        scratch_shapes=[big(), big(), big(), til(), til(), til(), big(),
                        pltpu.VMEM((1, LC), F32), pltpu.VMEM((1, LC), F32), pltpu.VMEM((8, LC), F32)],
        compiler_params=_cp("parallel", "arbitrary"),
    )(dh, h, h, rest, rest, pvec, wa, wx)


def mix_out_fwd(x, ao, hg, rest, vec, w_att_o, w_rec_o, w_out, name, tm=512):
    S = x.shape[0]
    tm = min(tm, S)

    def body(x_ref, ao_ref, hg_ref, ga_ref, gr_ref, vec_ref, wa_ref, wr_ref, wo_ref,
             xo_ref, att_ref, rec_ref, mg_ref, f_ref):
        att = _dot(ao_ref[...], wa_ref[...])
        rec = _dot(hg_ref[...], wr_ref[...])
        att_ref[...] = att.astype(BF16)
        rec_ref[...] = rec.astype(BF16)
        mg = (_sigmoid(ga_ref[...].astype(F32)) * att + _sigmoid(gr_ref[...].astype(F32)) * rec).astype(BF16)
        mg_ref[...] = mg
        f = _dot(mg, wo_ref[...])
        f_ref[...] = f.astype(BF16)
        y = f * lax.rsqrt(_mean(f * f) + EPS) * vec_ref[1:2, :]
        xo_ref[...] = x_ref[...] + (1.0 * vec_ref[4:5, :]) * y

    row = lambda i: (i, 0)
    full = lambda r: pl.BlockSpec((r, D), lambda i: (0, 0))
    return pl.pallas_call(
        body, name=name, grid=(S // tm,),
        in_specs=[pl.BlockSpec((tm, D), row), pl.BlockSpec((tm, 512), row), pl.BlockSpec((tm, D), row),
                  pl.BlockSpec((tm, D), lambda i: (i, 2)), pl.BlockSpec((tm, D), lambda i: (i, 3)),
                  full(8), full(512), full(D), full(D)],
        out_specs=[pl.BlockSpec((tm, D), row)] * 5,
        out_shape=[jax.ShapeDtypeStruct((S, D), F32)] + [jax.ShapeDtypeStruct((S, D), BF16)] * 4,
        compiler_params=_cp("parallel"),
    )(x, ao, hg, rest, rest, vec, w_att_o, w_rec_o, w_out)


def mix_out_bwd(dxo, f, att, rec, rest, h, vec, w_att_o, w_rec_o, w_out, name, tm=512):
    S = dxo.shape[0]
    tm = min(tm, S)

    def body(dxo_ref, f_ref, att_ref, rec_ref, yr_ref, ga_ref, gr_ref, h_ref, vec_ref, wa_ref, wr_ref, wo_ref,
             df_ref, da_ref, dr_ref, dao_ref, dh_ref, d3_ref, vacc_ref):
        @pl.when(pl.program_id(0) == 0)
        def _():
            vacc_ref[...] = jnp.zeros_like(vacc_ref)

        df = _post_norm_bwd(dxo_ref[...], f_ref[...].astype(F32), 1.0, vec_ref, vacc_ref).astype(BF16)
        df_ref[...] = df
        dm = _dot_nt(df, wo_ref[...])
        sa = _sigmoid(ga_ref[...].astype(F32))
        sr = _sigmoid(gr_ref[...].astype(F32))
        d_att = (dm * sa).astype(BF16)
        d_rec = (dm * sr).astype(BF16)
        da_ref[...] = d_att
        dr_ref[...] = d_rec
        d3_ref[1] = (dm * att_ref[...].astype(F32) * (sa * (1.0 - sa))).astype(BF16)
        d3_ref[2] = (dm * rec_ref[...].astype(F32) * (sr * (1.0 - sr))).astype(BF16)
        dao_ref[...] = _dot_nt(d_att, wa_ref[...]).astype(BF16)
        d_hg = _dot_nt(d_rec, wr_ref[...])
        yr = yr_ref[...].astype(F32)
        t = jnp.tanh(_GK * (yr + 0.044715 * yr * yr * yr))
        dh_ref[...] = d_hg * (0.5 * yr * (1.0 + t))
        gelu_grad = 0.5 * (1.0 + t) + 0.5 * yr * (1.0 - t * t) * _GK * (1.0 + 3.0 * 0.044715 * yr * yr)
        d3_ref[0] = (d_hg * h_ref[...] * gelu_grad).astype(BF16)

    row = lambda i: (i, 0)
    full = lambda r: pl.BlockSpec((r, D), lambda i: (0, 0))
    return pl.pallas_call(
        body, name=name, grid=(S // tm,),
        in_specs=[pl.BlockSpec((tm, D), row)] * 4
        + [pl.BlockSpec((tm, D), lambda i: (i, 1)), pl.BlockSpec((tm, D), lambda i: (i, 2)),
           pl.BlockSpec((tm, D), lambda i: (i, 3)), pl.BlockSpec((tm, D), row),
           full(8), full(512), full(D), full(D)],
        out_specs=[pl.BlockSpec((tm, D), row)] * 3
        + [pl.BlockSpec((tm, 512), row), pl.BlockSpec((tm, D), row),
           pl.BlockSpec((3, tm, D), lambda i: (0, i, 0)), pl.BlockSpec((8, D), lambda i: (0, 0))],
        out_shape=[jax.ShapeDtypeStruct((S, D), BF16)] * 3
        + [jax.ShapeDtypeStruct((S, 512), BF16), jax.ShapeDtypeStruct((S, D), F32),
           jax.ShapeDtypeStruct((3, S, D), BF16), jax.ShapeDtypeStruct((8, D), F32)],
        compiler_params=_cp("arbitrary"),
    )(dxo, f, att, rec, rest, rest, rest, h, vec, w_att_o, w_rec_o, w_out)


def dw_in(h, dq, dkv, dxr, d3, name, tk=1024, tn=512):
    S = h.shape[0]
    tk = min(tk, S)
    nk = S // tk

    def body(h_ref, dq_ref, dkv_ref, dxr_ref, d3_ref, o_ref, acc):
        j, k = pl.program_id(0), pl.program_id(1)

        @pl.when(k == 0)
        def _():
            acc[...] = jnp.zeros_like(acc)

        @pl.when(j == 0)
        def _():
            acc[...] += _dot_tn(h_ref[pl.ds(pl.multiple_of(k * tk, 16), tk), :], dq_ref[...])

        @pl.when((j >= 1) & (j < 3))
        def _():
            acc[...] += _dot_tn(h_ref[pl.ds(pl.multiple_of(k * tk, 16), tk), :], dkv_ref[...])

        @pl.when((j >= 3) & (j < 5))
        def _():
            acc[...] += _dot_tn(h_ref[pl.ds(pl.multiple_of(k * tk, 16), tk), :], dxr_ref[...])

        @pl.when(j >= 5)
        def _():
            acc[...] += _dot_tn(h_ref[pl.ds(pl.multiple_of(k * tk, 16), tk), :], d3_ref[...])

        @pl.when(k == nk - 1)
        def _():
            o_ref[...] = acc[...].astype(BF16)

    use = lambda j, k, lo, hi: jnp.where((j >= lo) & (j < hi), k, 0)
    g3 = lambda j: jnp.clip(j - 5, 0, 5)
    return pl.pallas_call(
        body, name=name, grid=(PW // tn, nk),
        in_specs=[pl.BlockSpec((S, D), lambda j, k: (0, 0), pipeline_mode=pl.Buffered(1)),
                  pl.BlockSpec((None, tk, tn), lambda j, k: (0, use(j, k, 0, 1), 0)),
                  pl.BlockSpec((None, tk, tn), lambda j, k: (jnp.clip(j - 1, 0, 1), use(j, k, 1, 3), 0)),
                  pl.BlockSpec((tk, tn), lambda j, k: (use(j, k, 3, 5), jnp.clip(j - 3, 0, 1))),
                  pl.BlockSpec((None, tk, tn), lambda j, k: (g3(j) // 2, use(j, k, 5, 11), g3(j) % 2))],
        out_specs=pl.BlockSpec((D, tn), lambda j, k: (0, j)),
        out_shape=jax.ShapeDtypeStruct((D, PW), BF16),
        scratch_shapes=[pltpu.VMEM((D, tn), F32)],
        compiler_params=_cp("parallel", "arbitrary"),
    )(h, dq, dkv, dxr, d3)


def ada_fwd(c_all, w_ada, b_ada, name, tn=768):
    n = w_ada.shape[1]

    def body(c_ref, w_ref, b_ref, o_ref):
        cv = c_ref[...]
        ca = (cv * _sigmoid(cv)).astype(BF16)
        o_ref[...] = _dot(ca, w_ref[...].astype(BF16)) + b_ref[...]

    return pl.pallas_call(
        body, name=name, grid=(n // tn,),
        in_specs=[pl.BlockSpec((8, D), lambda j: (0, 0)), pl.BlockSpec((D, tn), lambda j: (0, j)),
                  pl.BlockSpec((1, tn), lambda j: (0, j))],
        out_specs=pl.BlockSpec((8, tn), lambda j: (0, j)),
        out_shape=jax.ShapeDtypeStruct((8, n), F32),
        compiler_params=_cp("parallel"),
    )(c_all, w_ada, b_ada)


def ada_bwd(c_all_t, dmod, name, tn=768):
    n = dmod.shape[1]

    def body(c_ref, d_ref, o_ref):
        cv = c_ref[...]
        ca = (cv * _sigmoid(cv)).astype(BF16)
        o_ref[...] = _dot(ca, d_ref[...].astype(BF16))

    return pl.pallas_call(
        body, name=name, grid=(n // tn,),
        in_specs=[pl.BlockSpec((D, 128), lambda j: (0, 0)), pl.BlockSpec((128, tn), lambda j: (0, j))],
        out_specs=pl.BlockSpec((D, tn), lambda j: (0, j)),
        out_shape=jax.ShapeDtypeStruct((D, n), F32),
        compiler_params=_cp("parallel"),
    )(c_all_t, dmod)


def _row_tile(rows, cols, itemsize=4, budget=1536 * 1024):
    best = None
    for t in range(8, rows + 1, 8):
        if rows % t == 0 and t * cols * itemsize <= budget:
            best = t
    return rows if best is None else best


def sum_lead(parts, name, out_dtype=F32):
    n, R, C = parts.shape
    tr = _row_tile(R, C * n)

    def body(p_ref, o_ref):
        acc = p_ref[0].astype(F32)
        for k in range(1, n):
            acc = acc + p_ref[k].astype(F32)
        o_ref[...] = acc.astype(out_dtype)

    return pl.pallas_call(
        body, name=name, grid=(R // tr,),
        in_specs=[pl.BlockSpec((n, tr, C), lambda i: (0, i, 0))],
        out_specs=pl.BlockSpec((tr, C), lambda i: (i, 0)),
        out_shape=jax.ShapeDtypeStruct((R, C), out_dtype),
        compiler_params=_cp("parallel"),
    )(parts)


def adamw(w, g, m, v, name, emit_g=False):
    R, C = w.shape
    tr = _row_tile(R, C * 8, budget=8 * 1024 * 1024)

    def body(w_ref, g_ref, m_ref, v_ref, d_ref, mo_ref, vo_ref, *go_ref):
        gv = g_ref[...]
        if emit_g:
            go_ref[0][...] = gv
        mn = ADAM_B1 * m_ref[...] + (1.0 - ADAM_B1) * gv
        vn = ADAM_B2 * v_ref[...] + (1.0 - ADAM_B2) * (gv * gv)
        m_hat = mn / (1.0 - ADAM_B1 ** ADAM_STEP)
        v_hat = vn / (1.0 - ADAM_B2 ** ADAM_STEP)
        d_ref[...] = -ADAM_LR * (m_hat / (jnp.sqrt(v_hat) + ADAM_EPS) + ADAM_WD * w_ref[...])
        mo_ref[...] = mn
        vo_ref[...] = vn

    spec = pl.BlockSpec((tr, C), lambda i: (i, 0))
    return pl.pallas_call(
        body, name=name, grid=(R // tr,),
        in_specs=[spec] * 4, out_specs=[spec] * (4 if emit_g else 3),
        out_shape=[jax.ShapeDtypeStruct((R, C), F32)] * (4 if emit_g else 3),
        compiler_params=_cp("parallel"),
    )(w, g, m, v)


def _mesh_pos():
    return lax.axis_index("x"), lax.axis_index("y"), lax.axis_index("c")


def _other_chips(mx, my):
    return [(1 - mx, my), (mx, 1 - my), (1 - mx, 1 - my)]


def ag_small(x, name):
    R = x.shape[0]

    def body(x_ref, out_ref, send_sems, recv_sems, local_sem):
        mx, my, mc = _mesh_pos()
        me, sibling = (mx, my, mc), (mx, my, 1 - mc)
        chips = _other_chips(mx, my)

        def slot(px, py, pc):
            return out_ref.at[4 * px + 2 * py + pc]

        def copy(k, block, to, src=None):
            return pltpu.make_async_remote_copy(
                src_ref=slot(*block) if src is None else src, dst_ref=slot(*block),
                send_sem=send_sems.at[k], recv_sem=recv_sems.at[k], device_id=to, device_id_type=MESH)

        mine = pltpu.make_async_copy(x_ref, slot(*me), local_sem)
        mine.start()
        first = [copy(0, me, sibling, src=x_ref)]
        first += [copy(1 + j, me, (*chip, mc), src=x_ref) for j, chip in enumerate(chips)]
        for cp in first:
            cp.start()
        passed = [copy(4 + j, (*chip, mc), sibling) for j, chip in enumerate(chips)]
        for j, chip in enumerate(chips):
            copy(1 + j, (*chip, mc), me).wait_recv()
            passed[j].start()
        copy(0, sibling, me).wait_recv()
        for j, chip in enumerate(chips):
            copy(4 + j, (*chip, 1 - mc), me).wait_recv()
        for cp in first + passed:
            cp.wait_send()
        mine.wait()

    return pl.pallas_call(
        body, name=name,
        out_shape=jax.ShapeDtypeStruct((N_DEV, R, 128), F32),
        in_specs=[pl.BlockSpec(memory_space=pltpu.VMEM)],
        out_specs=pl.BlockSpec(memory_space=pltpu.VMEM),
        scratch_shapes=[pltpu.SemaphoreType.DMA((7,)), pltpu.SemaphoreType.DMA((7,)), pltpu.SemaphoreType.DMA],
        compiler_params=pltpu.CompilerParams(vmem_limit_bytes=VMEM_LIMIT),
    )(x)


BIG = (("ffn1_w_gu", "col", D, PW), ("ffn1_w_down", "row", FF, D), ("w_in", "col", D, PW),
       ("w_att_o", "col", 512, D), ("w_rec_o", "row", D, D), ("w_out", "row", D, D),
       ("ffn2_w_gu", "col", D, PW), ("ffn2_w_down", "row", FF, D))
NBIG = len(BIG)


def _shard_shape(kind, R, C):
    return (R, C // 4) if kind == "col" else (R // 4, C)


def _region(ref, kind, R, C, q, half, t, tr):
    sr, sc = _shard_shape(kind, R, C)
    if kind == "col":
        return ref.at[pl.ds(pl.multiple_of(half * (R // 2) + t * tr, 16), tr), pl.ds(q * sc, sc)]
    return ref.at[pl.ds(pl.multiple_of(q * sr + t * tr, 16), tr), pl.ds(half * (C // 2), C // 2)]


def ag_local(w, kind, R, C, p_arr, name, after=()):
    sr, sc = _shard_shape(kind, R, C)
    tr = _row_tile(sr, sc, budget=2 * 1024 * 1024)
    nt = sr // tr
    after = list(after)

    def body(p_ref, w_ref, *rest):
        rest[-1][...] = w_ref[...].astype(BF16)

    if kind == "col":
        o_spec = pl.BlockSpec((tr, sc), lambda i, p: (i, p[0]))
    else:
        o_spec = pl.BlockSpec((tr, sc), lambda i, p: (p[0] * nt + i, 0))
    return pl.pallas_call(
        body, name=name,
        grid_spec=pltpu.PrefetchScalarGridSpec(
            num_scalar_prefetch=1, grid=(nt,),
            in_specs=[pl.BlockSpec((tr, sc), lambda i, p: (i, 0))] + [ANY] * len(after), out_specs=o_spec),
        out_shape=jax.ShapeDtypeStruct((R, C), BF16),
        compiler_params=_cp("parallel"),
    )(p_arr, w, *after)


HBM_SPEC = pl.BlockSpec(memory_space=pltpu.HBM)
SEM_SPEC = pl.BlockSpec(memory_space=pltpu.SEMAPHORE)


def _ag_sems(geoms):
    return sum(6 if both else 3 for (_, _, _, both) in geoms)


def _ag_copies(fulls, geoms, ssem, rsem, mx, my, mc, q, h):
    chips = _other_chips(mx, my)
    out, base = [], 0
    for w, (kind, R, C, both) in enumerate(geoms):
        sr, sc = _shard_shape(kind, R, C)
        hr = sr // 2 if kind == "col" else sr
        reg = _region(fulls[w], kind, R, C, q, h, 0, hr)
        out.append([pltpu.make_async_remote_copy(
            src_ref=reg, dst_ref=reg, send_sem=ssem.at[base + 3 * t + k], recv_sem=rsem.at[base + 3 * t + k],
            device_id=(*chips[k], mc if t == 0 else 1 - mc), device_id_type=MESH)
            for t in range(2 if both else 1) for k in range(3)])
        base += 6 if both else 3
    return out


def ag_start(fulls, geoms, after, name):
    n = len(fulls)
    after = list(after)
    m = len(after)

    def body(*refs):
        ssem, rsem = refs[n + m:n + m + 2]
        outs, token = refs[n + m + 2:2 * n + m + 2], refs[2 * n + m + 2]
        mx, my, mc = _mesh_pos()
        p = 2 * mx + my
        col = [w for w, g in enumerate(geoms) if g[0] == "col"]
        row = [w for w, g in enumerate(geoms) if g[0] == "row"]
        for q in range(4):
            @pl.when(p == q)
            def _(q=q):
                cps = _ag_copies(outs, geoms, ssem, rsem, mx, my, mc, q, mc)
                for w in col:
                    for cp in cps[w]:
                        cp.start()
        for h in range(2):
            @pl.when(mc == h)
            def _(h=h):
                cps = _ag_copies(outs, geoms, ssem, rsem, mx, my, mc, p, h)
                for w in row:
                    for cp in cps[w]:
                        cp.start()
        token[...] = jnp.zeros_like(token)

    res = pl.pallas_call(
        body, name=name,
        out_shape=[pltpu.SemaphoreType.DMA((_ag_sems(geoms),)), pltpu.SemaphoreType.DMA((_ag_sems(geoms),))]
        + [pltpu.HBM(a.shape, a.dtype) for a in fulls] + [jax.ShapeDtypeStruct((8, 128), F32)],
        in_specs=[HBM_SPEC] * n + [ANY] * m,
        out_specs=[SEM_SPEC, SEM_SPEC] + [HBM_SPEC] * n + [pl.BlockSpec(memory_space=pltpu.VMEM)],
        input_output_aliases={w: 2 + w for w in range(n)},
        compiler_params=pltpu.CompilerParams(has_side_effects=pltpu.SideEffectType.DATAFLOW_SIDE_EFFECTING),
    )(*[pltpu.with_memory_space_constraint(a, pltpu.HBM) for a in fulls], *after)
    return res[0], res[1], list(res[2:2 + n]), res[2 + n]


def ag_wait(fulls, geoms, ssem, rsem, after, name):
    n = len(fulls)
    after = list(after) if isinstance(after, (list, tuple)) else [after]

    def body(*refs):
        ins, ssem_ref, rsem_ref = refs[:n], refs[n], refs[n + 1]
        mx, my, mc = _mesh_pos()
        for cps in _ag_copies(ins, geoms, ssem_ref, rsem_ref, mx, my, mc, 0, 0):
            for cp in cps:
                cp.wait_send()
                cp.wait_recv()

    return list(pl.pallas_call(
        body, name=name,
        out_shape=[pltpu.HBM(a.shape, a.dtype) for a in fulls],
        in_specs=[HBM_SPEC] * n + [SEM_SPEC, SEM_SPEC] + [ANY] * len(after),
        out_specs=[HBM_SPEC] * n,
        input_output_aliases={w: w for w in range(n)},
        compiler_params=pltpu.CompilerParams(has_side_effects=pltpu.SideEffectType.DATAFLOW_SIDE_EFFECTING),
    )(*fulls, ssem, rsem, *after))


def ag_forward(full, kind, R, C, name):
    sr, sc = _shard_shape(kind, R, C)
    hr, hc = (sr // 2, sc) if kind == "col" else (sr, sc // 2)
    tr = _row_tile(hr, hc, itemsize=2, budget=512 * 1024)
    nt = hr // tr

    total = 3 * nt

    def body(src_ref, full_ref, stage, lsem, ssem, rsem):
        step = pl.program_id(0) * nt + pl.program_id(1)
        par = step % 2
        mx, my, mc = _mesh_pos()

        def load(s, q, h, t):
            return pltpu.make_async_copy(_region(src_ref, kind, R, C, q, h, t, tr), stage.at[s], lsem.at[s])

        def push(s, q, h, t):
            return pltpu.make_async_remote_copy(src_ref=stage.at[s], dst_ref=_region(full_ref, kind, R, C, q, h, t, tr),
                                                send_sem=ssem.at[s], recv_sem=rsem, device_id=(mx, my, 1 - mc),
                                                device_id_type=MESH)

        def for_tile(stp, fn):
            q_k = _partner_chip(stp // nt, 2 * mx + my)
            if kind == "col":
                for q in range(4):
                    @pl.when(q_k == q)
                    def _(q=q):
                        fn(q, mc, stp % nt)
            else:
                for h in range(2):
                    @pl.when(mc == h)
                    def _(h=h):
                        fn(q_k, h, stp % nt)

        @pl.when(step == 0)
        def _():
            for_tile(step, lambda q, h, t: load(0, q, h, t).start())

        load(par, 0, 0, 0).wait()
        for_tile(step, lambda q, h, t: push(par, q, h, t).start())

        @pl.when(step + 1 < total)
        def _():
            @pl.when(step >= 1)
            def _():
                push(1 - par, 0, 0, 0).wait_send()
            for_tile(step + 1, lambda q, h, t: load(1 - par, q, h, t).start())

        @pl.when(step == total - 1)
        def _():
            push(par, 0, 0, 0).wait_send()
            push(1 - par, 0, 0, 0).wait_send()
            three = full_ref.at[pl.ds(0, hr), pl.ds(0, 3 * hc)] if kind == "col" else full_ref.at[pl.ds(0, 3 * hr), pl.ds(0, hc)]
            pltpu.make_async_remote_copy(src_ref=three, dst_ref=three, send_sem=ssem.at[0], recv_sem=rsem,
                                         device_id=(mx, my, 1 - mc), device_id_type=MESH).wait_recv()

    return pl.pallas_call(
        body, name=name, grid=(3, nt),
        in_specs=[ANY], out_specs=ANY,
        out_shape=jax.ShapeDtypeStruct((R, C), BF16),
        scratch_shapes=[pltpu.VMEM((2, tr, hc), BF16), pltpu.SemaphoreType.DMA((2,)), pltpu.SemaphoreType.DMA((2,)),
                        pltpu.SemaphoreType.DMA],
        input_output_aliases={0: 0},
        compiler_params=_cp("arbitrary", "arbitrary"),
    )(full)


def _half_shape(kind, R, C):
    return (R // 2, C) if kind == "col" else (R, C // 2)


def _piece_shape(kind, R, C):
    return (R // 2, C // 4) if kind == "col" else (R // 4, C // 2)


def pair_push(g, kind, c_arr, name):
    R, C = g.shape
    hr, hc = _half_shape(kind, R, C)
    tr = _row_tile(hr, hc, itemsize=2, budget=1024 * 1024)
    nt = hr // tr

    def body(c_ref, g_ref, out_ref, stage, ssem, rsem):
        i = pl.program_id(0)
        slot = i % 2
        mx, my, mc = _mesh_pos()

        def push(s, t):
            return pltpu.make_async_remote_copy(
                src_ref=stage.at[s], dst_ref=out_ref.at[pl.ds(pl.multiple_of(t * tr, 16), tr)],
                send_sem=ssem.at[s], recv_sem=rsem, device_id=(mx, my, 1 - mc), device_id_type=MESH)

        @pl.when(i >= 2)
        def _():
            push(slot, 0).wait_send()

        stage[slot] = g_ref[...]
        push(slot, i).start()

        @pl.when(i == nt - 1)
        def _():
            push(slot, 0).wait_send()
            if nt >= 2:
                push(1 - slot, 0).wait_send()
            pltpu.make_async_remote_copy(src_ref=out_ref, dst_ref=out_ref, send_sem=ssem.at[0], recv_sem=rsem,
                                         device_id=(mx, my, 1 - mc), device_id_type=MESH).wait_recv()

    if kind == "col":
        g_spec = pl.BlockSpec((tr, hc), lambda i, c: ((1 - c[0]) * nt + i, 0))
    else:
        g_spec = pl.BlockSpec((tr, hc), lambda i, c: (i, 1 - c[0]))
    return pl.pallas_call(
        body, name=name,
        grid_spec=pltpu.PrefetchScalarGridSpec(
            num_scalar_prefetch=1, grid=(nt,), in_specs=[g_spec], out_specs=ANY,
            scratch_shapes=[pltpu.VMEM((2, tr, hc), BF16), pltpu.SemaphoreType.DMA((2,)), pltpu.SemaphoreType.DMA]),
        out_shape=jax.ShapeDtypeStruct((hr, hc), BF16),
        compiler_params=_cp("arbitrary"),
    )(c_arr, g)


def _partner_chip(k, p):
    return p ^ jnp.where(k == 0, 2, jnp.where(k == 1, 1, jnp.where(k == 2, 3, 0)))


def pair_add(g, got, kind, cp_arr, name):
    R, C = g.shape
    pr, pc = _piece_shape(kind, R, C)
    tr = _row_tile(pr, pc, itemsize=2, budget=1024 * 1024)
    nt = pr // tr

    def body(cp_ref, g_ref, got_ref, ps_ref, rb_ref):
        tile = (g_ref[...].astype(F32) + got_ref[...].astype(F32)).astype(BF16)
        ps_ref[...] = tile

        @pl.when(pl.program_id(1) == cp_ref[1])
        def _():
            rb_ref[...] = tile

    if kind == "col":
        g_spec = pl.BlockSpec((tr, pc), lambda i, q, cp: (cp[0] * nt + i, q))
        got_spec = pl.BlockSpec((tr, pc), lambda i, q, cp: (i, q))
    else:
        g_spec = pl.BlockSpec((tr, pc), lambda i, q, cp: (q * nt + i, cp[0]))
        got_spec = pl.BlockSpec((tr, pc), lambda i, q, cp: (q * nt + i, 0))
    return pl.pallas_call(
        body, name=name,
        grid_spec=pltpu.PrefetchScalarGridSpec(
            num_scalar_prefetch=1, grid=(nt, 4), in_specs=[g_spec, got_spec],
            out_specs=[pl.BlockSpec((None, tr, pc), lambda i, q, cp: (q, i, 0)),
                       pl.BlockSpec((None, tr, pc), lambda i, q, cp: (cp[1], i, 0))]),
        out_shape=[jax.ShapeDtypeStruct((4, pr, pc), BF16)] * 2,
        compiler_params=_cp("arbitrary", "arbitrary"),
    )(cp_arr, g, got)


def _rs_copies(ps, rb, ssem, rsem, mx, my, mc):
    p = 2 * mx + my
    out = []
    for w in range(len(ps)):
        for k, chip in enumerate(_other_chips(mx, my)):
            out.append(pltpu.make_async_remote_copy(
                src_ref=ps[w].at[2 * chip[0] + chip[1]], dst_ref=rb[w].at[p], send_sem=ssem.at[3 * w + k],
                recv_sem=rsem.at[3 * w + k], device_id=(*chip, mc), device_id_type=MESH))
    return out


def rs_start(ps, rb, after, name):
    n = len(ps)
    after = list(after)
    m = len(after)

    def body(*refs):
        ssem, rsem = refs[2 * n + m:2 * n + m + 2]
        ps_o = refs[2 * n + m + 2:3 * n + m + 2]
        rb_o = refs[3 * n + m + 2:4 * n + m + 2]
        token = refs[4 * n + m + 2]
        for cp in _rs_copies(ps_o, rb_o, ssem, rsem, *_mesh_pos()):
            cp.start()
        token[...] = jnp.zeros_like(token)

    both = list(ps) + list(rb)
    res = pl.pallas_call(
        body, name=name,
        out_shape=[pltpu.SemaphoreType.DMA((3 * n,)), pltpu.SemaphoreType.DMA((3 * n,))]
        + [pltpu.HBM(a.shape, a.dtype) for a in both] + [jax.ShapeDtypeStruct((8, 128), F32)],
        in_specs=[HBM_SPEC] * (2 * n) + [ANY] * m,
        out_specs=[SEM_SPEC, SEM_SPEC] + [HBM_SPEC] * (2 * n) + [pl.BlockSpec(memory_space=pltpu.VMEM)],
        input_output_aliases={w: 2 + w for w in range(2 * n)},
        compiler_params=pltpu.CompilerParams(has_side_effects=pltpu.SideEffectType.DATAFLOW_SIDE_EFFECTING),
    )(*[pltpu.with_memory_space_constraint(a, pltpu.HBM) for a in both], *after)
    return res[0], res[1], list(res[2:2 + n]), list(res[2 + n:2 + 2 * n]), res[2 + 2 * n]


def rs_wait(ps, rb, ssem, rsem, after, name):
    n = len(ps)
    after = list(after)
    m = len(after)

    def body(*refs):
        ps_i, rb_i = refs[:n], refs[n:2 * n]
        ssem_ref, rsem_ref = refs[2 * n], refs[2 * n + 1]
        for cp in _rs_copies(ps_i, rb_i, ssem_ref, rsem_ref, *_mesh_pos()):
            cp.wait_send()
            cp.wait_recv()

    both = list(ps) + list(rb)
    res = pl.pallas_call(
        body, name=name,
        out_shape=[pltpu.HBM(a.shape, a.dtype) for a in both],
        in_specs=[HBM_SPEC] * (2 * n) + [SEM_SPEC, SEM_SPEC] + [ANY] * m,
        out_specs=[HBM_SPEC] * (2 * n),
        input_output_aliases={w: w for w in range(2 * n)},
        compiler_params=pltpu.CompilerParams(has_side_effects=pltpu.SideEffectType.DATAFLOW_SIDE_EFFECTING),
    )(*both, ssem, rsem, *after)
    return list(res[n:])


def sum_share(parts, kind, R, C, name):
    _, pr, pc = parts.shape
    sr, sc = _shard_shape(kind, R, C)
    tr = _row_tile(pr, pc * 4, budget=4 * 1024 * 1024)
    nt = pr // tr

    def body(p_ref, fin_ref, stage, lsem, ssem, rsem):
        i = pl.program_id(0)
        slot = i % 2
        mx, my, mc = _mesh_pos()

        def region(h, t):
            r0 = pl.multiple_of(t * tr, 8)
            if kind == "col":
                return fin_ref.at[pl.ds(pl.multiple_of(h * pr + r0, 8), tr)]
            return fin_ref.at[pl.ds(r0, tr), pl.ds(h * pc, pc)]

        def copies(s, h, t):
            return (pltpu.make_async_copy(stage.at[s], region(h, t), lsem.at[s]),
                    pltpu.make_async_remote_copy(src_ref=stage.at[s], dst_ref=region(h, t), send_sem=ssem.at[s],
                                                 recv_sem=rsem, device_id=(mx, my, 1 - mc), device_id_type=MESH))

        def wait_sent(s):
            loc, rem = copies(s, 0, 0)
            loc.wait()
            rem.wait_send()

        @pl.when(i >= 2)
        def _():
            wait_sent(slot)

        acc = p_ref[0].astype(F32)
        for k in range(1, 4):
            acc = acc + p_ref[k].astype(F32)
        stage[slot] = acc
        if kind == "col":
            for cp in copies(slot, mc, i):
                cp.start()
        else:
            for h in range(2):
                @pl.when(mc == h)
                def _(h=h):
                    for cp in copies(slot, h, i):
                        cp.start()

        @pl.when(i == nt - 1)
        def _():
            wait_sent(slot)
            if nt >= 2:
                wait_sent(1 - slot)
            half = fin_ref.at[pl.ds(0, pr), pl.ds(0, pc)]
            pltpu.make_async_remote_copy(src_ref=half, dst_ref=half, send_sem=ssem.at[0], recv_sem=rsem,
                                         device_id=(mx, my, 1 - mc), device_id_type=MESH).wait_recv()

    return pl.pallas_call(
        body, name=name, grid=(nt,),
        in_specs=[pl.BlockSpec((4, tr, pc), lambda i: (0, i, 0))],
        out_specs=ANY,
        out_shape=jax.ShapeDtypeStruct((sr, sc), F32),
        scratch_shapes=[pltpu.VMEM((2, tr, pc), F32), pltpu.SemaphoreType.DMA((2,)), pltpu.SemaphoreType.DMA((2,)),
                        pltpu.SemaphoreType.DMA],
        compiler_params=_cp("arbitrary"),
    )(parts)


def _pack(parts, rows):
    flat = []
    for a in parts:
        a = jnp.ravel(a).astype(F32)
        flat.append(jnp.pad(a, (0, (-a.shape[0]) % 128)))
    v = jnp.concatenate(flat)
    return jnp.pad(v, (0, rows * 128 - v.shape[0])).reshape(rows, 128)


def _unpack(block, shapes):
    lead = block.shape[:-2]
    v = block.reshape(lead + (-1,))
    out, off = [], 0
    for shp in shapes:
        n = int(np.prod(shp))
        out.append(v[..., off:off + n].reshape(lead + tuple(shp)))
        off += n + (-n) % 128
    return out


def _block_diag4(w):
    w4 = w.reshape(4, 4, 64, 64)
    eye = jnp.eye(4, dtype=w.dtype)
    return (w4[:, :, :, None, :] * eye[None, :, None, :, None]).reshape(4, 256, 256)


def _diag_blocks(bd):
    b5 = bd.reshape(4, 4, 64, 4, 64)
    return jnp.stack([b5[:, i, :, i, :] for i in range(4)], axis=1).reshape(16, 64, 64)


def _bias_window(rel_bias):
    m = (np.arange(768) + 127) % 768 - 127
    w = rel_bias[:, np.clip(512 - m, -128, 128) + 128]
    win = jnp.tile(w, (1, 128))[:, :128 * 767].reshape(8, 128, 767)[:, :, :WIN]
    qh = np.arange(128)[:, None] // CHUNK
    kc = np.arange(WIN)[None, :] // CHUNK
    valid = (kc >= qh) & (kc <= qh + 8)
    return jnp.where(jnp.asarray(valid)[None], win, NEG)


SMALL = ("b_ada", "norm_pre", "norm_post", "rel_bias", "conv_w", "conv_b", "lru_wa", "lru_ba", "lru_wx",
         "lru_bx", "lru_lambda")
WEIGHTS = ("w_ada", "b_ada", "norm_pre", "norm_post", "ffn1_w_gu", "ffn1_w_down", "w_in", "rel_bias", "conv_w",
           "conv_b", "lru_wa", "lru_ba", "lru_wx", "lru_bx", "lru_lambda", "w_att_o", "w_rec_o", "w_out",
           "ffn2_w_gu", "ffn2_w_down")


def kernel(x, c, w_ada, b_ada, norm_pre, norm_post, ffn1_w_gu, ffn1_w_down, w_in, rel_bias, conv_w, conv_b, lru_wa, lru_ba, lru_wx, lru_bx, lru_lambda, w_att_o, w_rec_o, w_out, ffn2_w_gu, ffn2_w_down, loss_target, m_w_ada, m_b_ada, m_norm_pre, m_norm_post, m_ffn1_w_gu, m_ffn1_w_down, m_w_in, m_rel_bias, m_conv_w, m_conv_b, m_lru_wa, m_lru_ba, m_lru_wx, m_lru_bx, m_lru_lambda, m_w_att_o, m_w_rec_o, m_w_out, m_ffn2_w_gu, m_ffn2_w_down, v_w_ada, v_b_ada, v_norm_pre, v_norm_post, v_ffn1_w_gu, v_ffn1_w_down, v_w_in, v_rel_bias, v_conv_w, v_conv_b, v_lru_wa, v_lru_ba, v_lru_wx, v_lru_bx, v_lru_lambda, v_w_att_o, v_w_rec_o, v_w_out, v_ffn2_w_gu, v_ffn2_w_down):
    W = dict(w_ada=w_ada, b_ada=b_ada, norm_pre=norm_pre, norm_post=norm_post, ffn1_w_gu=ffn1_w_gu,
             ffn1_w_down=ffn1_w_down, w_in=w_in, rel_bias=rel_bias, conv_w=conv_w, conv_b=conv_b, lru_wa=lru_wa,
             lru_ba=lru_ba, lru_wx=lru_wx, lru_bx=lru_bx, lru_lambda=lru_lambda, w_att_o=w_att_o, w_rec_o=w_rec_o,
             w_out=w_out, ffn2_w_gu=ffn2_w_gu, ffn2_w_down=ffn2_w_down)
    M = dict(w_ada=m_w_ada, b_ada=m_b_ada, norm_pre=m_norm_pre, norm_post=m_norm_post, ffn1_w_gu=m_ffn1_w_gu,
             ffn1_w_down=m_ffn1_w_down, w_in=m_w_in, rel_bias=m_rel_bias, conv_w=m_conv_w, conv_b=m_conv_b,
             lru_wa=m_lru_wa, lru_ba=m_lru_ba, lru_wx=m_lru_wx, lru_bx=m_lru_bx, lru_lambda=m_lru_lambda,
             w_att_o=m_w_att_o, w_rec_o=m_w_rec_o, w_out=m_w_out, ffn2_w_gu=m_ffn2_w_gu, ffn2_w_down=m_ffn2_w_down)
    V = dict(w_ada=v_w_ada, b_ada=v_b_ada, norm_pre=v_norm_pre, norm_post=v_norm_post, ffn1_w_gu=v_ffn1_w_gu,
             ffn1_w_down=v_ffn1_w_down, w_in=v_w_in, rel_bias=v_rel_bias, conv_w=v_conv_w, conv_b=v_conv_b,
             lru_wa=v_lru_wa, lru_ba=v_lru_ba, lru_wx=v_lru_wx, lru_bx=v_lru_bx, lru_lambda=v_lru_lambda,
             w_att_o=v_w_att_o, w_rec_o=v_w_rec_o, w_out=v_w_out, ffn2_w_gu=v_ffn2_w_gu, ffn2_w_down=v_ffn2_w_down)
    mx, my, mc = _mesh_pos()
    p = 2 * mx + my
    e = 4 * mx + 2 * my + mc
    xs = x[0]

    c_arr = jnp.reshape(mc, (1,)).astype(jnp.int32)
    cp_arr = jnp.stack([mc, p]).astype(jnp.int32)
    p_arr = jnp.reshape(p, (1,)).astype(jnp.int32)
    direct = ("w_att_o", "w_rec_o", "w_out", "ffn2_w_gu", "ffn2_w_down")
    geoms = [(kind, R, C, n in direct) for (n, kind, R, C) in BIG]
    names = [b[0] for b in BIG]
    placed = [ag_local(W[n][0], kind, R, C, p_arr, "ag_local_" + n) for (n, kind, R, C) in BIG[:2]]

    def arrived(fly, lo, hi, ssem, rsem, after, tag):
        done = ag_wait(fly, geoms[lo:hi], ssem, rsem, after, "ag_wait_" + tag)
        return [a if both else ag_forward(a, kind, R, C, "ag_forward_" + n)
                for a, (kind, R, C, both), n in zip(done, geoms[lo:hi], names[lo:hi])]

    g1 = ag_small(_pack([c, norm_pre, norm_post, conv_w], 32), "ag_small_params")
    c_all, npre4, npost4, cw4 = _unpack(g1, [(D,), (3, 256), (3, 256), (4, 256)])
    chipwise = lambda a: jnp.moveaxis(a[0::2], 0, 1).reshape(a.shape[1], D)
    npre, npost, conv_full = chipwise(npre4), chipwise(npost4), chipwise(cw4)

    b_cols = lax.dynamic_slice(b_ada, (0, p * 2304), (1, 2304))
    mod_cols = ada_fwd(c_all, w_ada[0], b_cols, "ada_fwd")
    g2 = ag_small(mod_cols.reshape(144, 128), "ag_mod")
    mod_all = jnp.moveaxis(g2[0::2].reshape(4, 8, 2304), 0, 1).reshape(8, 9 * D)
    mod = lax.dynamic_index_in_dim(mod_all, e, 0, keepdims=False).reshape(3, 3, D)
    zeros3 = jnp.zeros((3, D), F32)
    vecs = [jnp.concatenate([npre[k:k + 1], npost[k:k + 1], mod[k], zeros3], axis=0) for k in range(3)]

    gu_s, gu_r, gu_fly, tok_gu = ag_start(placed[:1], geoms[:1], [g2], "ag_start_ffn1_gu")
    dn_s, dn_r, dn_fly, tok0 = ag_start(placed[1:2], geoms[1:2], [tok_gu], "ag_start_ffn1_down")
    placed += [ag_local(W[n][0], kind, R, C, p_arr, "ag_local_" + n, after=[tok0]) for (n, kind, R, C) in BIG[2:]]
    f1_gu, = arrived(gu_fly, 0, 1, gu_s, gu_r, placed[2:], "ffn1_gu")
    f1_dn, = arrived(dn_fly, 1, 2, dn_s, dn_r, f1_gu, "ffn1_down")
    mix_s, mix_r, mix_fly, tok1 = ag_start(placed[2:6], geoms[2:6], [f1_gu, f1_dn], "ag_start_mixer")
    ffn_s, ffn_r, ffn_fly, tok2 = ag_start(placed[6:], geoms[6:], [tok1], "ag_start_ffn2")
    wa_bd = _block_diag4(lru_wa[0]).astype(BF16)
    wx_bd = _block_diag4(lru_wx[0]).astype(BF16)
    pvec = jnp.concatenate([conv_full, conv_b, lru_ba, lru_bx, lru_lambda], axis=0)
    bias = _bias_window(rel_bias[0]).reshape(4, 256, WIN)

    f1_u = f1_gu[:, FF:]
    x1, h1, g1_, u1, a1, f1 = ffn_fwd(xs, vecs[0] + tok2[0:1, 0:1], f1_gu, f1_u, f1_dn, 0.5, "ffn1_fwd")
    win, wao, wro, wout = arrived(mix_fly, 2, 6, mix_s, mix_r, x1, "mixer")
    h2, qkv, rest = proj_fwd(x1, vecs[1], win, "proj_fwd")
    ao = attn_fwd(qkv, bias, "attn_fwd")
    hl, hg = lru_fwd(rest, pvec, wa_bd, wx_bd, "lru_fwd")
    x2, att, rec, mg, f2 = mix_out_fwd(x1, ao, hg, rest, vecs[1], wao, wro, wout, "mix_out_fwd")
    f2_gu, f2_dn = arrived(ffn_fly, 6, 8, ffn_s, ffn_r, x2, "ffn2")
    f2_u = f2_gu[:, FF:]
    dy, h3, g3_, u3, a3, f3, lvec = ffn_fwd(x2, vecs[2], f2_gu, f2_u, f2_dn, 0.5, "ffn2_fwd", tgt=loss_target[0])

    G, grads = {}, {}
    geo = {n: (kind, R, C) for (n, kind, R, C) in BIG}

    def reduce_begin(names, tag):
        ps, rb = [], []
        for n in names:
            got = pair_push(G[n], geo[n][0], c_arr, "rs_push_" + n)
            a, b = pair_add(G[n], got, geo[n][0], cp_arr, "rs_pair_sum_" + n)
            ps.append(a)
            rb.append(b)
        return rs_start(ps, rb, [], "rs_start_" + tag)

    def reduce_end(names, flight, after, tag):
        ssem, rsem, ps, rb, _ = flight
        for a, n in zip(rs_wait(ps, rb, ssem, rsem, after, "rs_wait_" + tag), names):
            grads[n] = sum_share(a, *geo[n], "rs_sum_share_" + n)[None]

    dx2, df3, dgu3, va2 = ffn_bwd(dy, x2, f3, g3_, u3, vecs[2], f2_gu, f2_u, f2_dn, 0.5, "ffn2_bwd")
    G["ffn2_w_gu"] = mm_tn(h3, dgu3, "dw_ffn2_gu", D, 1408, 2048)
    G["ffn2_w_down"] = mm_tn(a3, df3, "dw_ffn2_down", 1408, D, 2048)
    fly_ffn2 = reduce_begin(("ffn2_w_gu", "ffn2_w_down"), "ffn2")
    vec1 = vecs[1] + fly_ffn2[4][0:1, 0:1]
    df2, d_att, d_rec, dao, dhl, d3, va_out = mix_out_bwd(dx2, f2, att, rec, rest, hl, vec1, wao, wro, wout,
                                                          "mix_out_bwd")
    G["w_out"] = mm_tn(mg, df2, "dw_out", D, D, 1024)
    G["w_att_o"] = mm_tn(ao, d_att, "dw_att_o", 512, D, 1024)
    G["w_rec_o"] = mm_tn(hg, d_rec, "dw_rec_o", D, D, 1024)
    dq, db, dkv = attn_bwd(qkv, dao, bias, "attn_bwd")
    dxr, v_lru, dwa_bd, dwx_bd = lru_bwd(dhl, hl, rest, pvec, wa_bd, wx_bd, "lru_bwd")
    dx1, va_in = proj_bwd(dq, dkv, dxr, d3, win, x1, dx2, vecs[1], "proj_bwd")
    G["w_in"] = dw_in(h2, dq, dkv, dxr, d3, "dw_in")
    fly_mix = reduce_begin(("w_in", "w_att_o", "w_rec_o", "w_out"), "mixer")
    vec0 = vecs[0] + fly_mix[4][0:1, 0:1]
    dx0, df1, dgu1, va0 = ffn_bwd(dx1, xs, f1, g1_, u1, vec0, f1_gu, f1_u, f1_dn, 0.5, "ffn1_bwd")
    G["ffn1_w_gu"] = mm_tn(h1, dgu1, "dw_ffn1_gu", D, 1408, 2048)
    G["ffn1_w_down"] = mm_tn(a1, df1, "dw_ffn1_down", 1408, D, 2048)
    fly_ffn1 = reduce_begin(("ffn1_w_gu", "ffn1_w_down"), "ffn1")
    reduce_end(("ffn2_w_gu", "ffn2_w_down"), fly_ffn2, [fly_ffn1[4]], "ffn2")
    reduce_end(("w_in", "w_att_o", "w_rec_o", "w_out"), fly_mix, [fly_ffn1[4], grads["ffn2_w_down"]], "mixer")

    va1 = va_out + va_in
    vas = (va0, va1, va2)
    dmod = jnp.stack([v[2:5] for v in vas])
    part = {"b_ada": dmod, "norm_pre": jnp.stack([v[0] for v in vas]), "norm_post": jnp.stack([v[1] for v in vas]),
            "rel_bias": bias_grad(db.reshape(8, 128, WIN), "bias_grad")[:, :257], "conv_w": v_lru[0:4], "conv_b": v_lru[4],
            "lru_wa": _diag_blocks(dwa_bd), "lru_ba": v_lru[5], "lru_wx": _diag_blocks(dwx_bd), "lru_bx": v_lru[6],
            "lru_lambda": v_lru[7]}
    full_shapes = {"b_ada": (9 * D,), "norm_pre": (3, D), "norm_post": (3, D), "rel_bias": (8, 257),
                   "conv_w": (4, D), "conv_b": (D,), "lru_wa": (16, 64, 64), "lru_ba": (D,),
                   "lru_wx": (16, 64, 64), "lru_bx": (D,), "lru_lambda": (D,)}
    g3 = ag_small(_pack([part[n] for n in SMALL] + [lvec[0:1, 0:1]], 1232), "ag_small_grads")
    summed = _unpack(sum_lead(g3, "sum_small_grads"), [full_shapes[n] for n in SMALL] + [(1,)])
    red = dict(zip(SMALL, summed[:-1]))
    loss = summed[-1][0]
    cols = lambda a: lax.dynamic_slice(a, (0, p * 256), (a.shape[0], 256))
    grads.update({"b_ada": red["b_ada"][None], "norm_pre": cols(red["norm_pre"])[None],
                  "norm_post": cols(red["norm_post"])[None], "rel_bias": red["rel_bias"][None],
                  "conv_w": cols(red["conv_w"])[None], "conv_b": red["conv_b"][None], "lru_wa": red["lru_wa"][None],
                  "lru_ba": red["lru_ba"][None], "lru_wx": red["lru_wx"][None], "lru_bx": red["lru_bx"][None],
                  "lru_lambda": red["lru_lambda"][None]})

    dmod_all = g3[:, :72].reshape(8, 9 * D)
    dmod_cols = jnp.pad(lax.dynamic_slice(dmod_all, (0, p * 2304), (8, 2304)), ((0, 120), (0, 0)))
    c_all_t = jnp.pad(c_all.T, ((0, 0), (0, 120)))
    grads["w_ada"] = ada_bwd(c_all_t, dmod_cols, "ada_bwd")[None]

    delta, new_m, new_v = {}, {}, {}

    def update(n):
        shp = W[n].shape
        res = adamw(W[n][0], grads[n][0], M[n][0], V[n][0], "adamw_" + n, emit_g=n in geo)
        delta[n], new_m[n], new_v[n] = [a.reshape(shp) for a in res[:3]]
        if n in geo:
            grads[n] = res[3].reshape(shp)

    for n in ("w_ada", "ffn2_w_gu", "ffn2_w_down", "w_in", "w_att_o", "w_rec_o", "w_out"):
        update(n)
    packed = [_pack([src[n] for n in SMALL], 1168) for src in (W, grads, M, V)]
    outs = adamw(*packed, "adamw_small")
    for dst, blk in zip((delta, new_m, new_v), outs):
        for n, a in zip(SMALL, _unpack(blk, [W[n].shape for n in SMALL])):
            dst[n] = a
    reduce_end(("ffn1_w_gu", "ffn1_w_down"), fly_ffn1,
               [outs[0], delta["w_ada"], delta["ffn2_w_gu"], delta["ffn2_w_down"], delta["w_in"], delta["w_out"]], "ffn1")
    for n in ("ffn1_w_gu", "ffn1_w_down"):
        update(n)

    return (loss, dx0[None], *[grads[n] for n in WEIGHTS], *[delta[n] for n in WEIGHTS],
            *[new_m[n] for n in WEIGHTS], *[new_v[n] for n in WEIGHTS])
```

```python
import functools

import numpy as np
import jax
import jax.numpy as jnp
from jax import lax
from jax.experimental import pallas as pl
from jax.experimental.pallas import tpu as pltpu

F32 = jnp.float32
BF16 = jnp.bfloat16

D = 1024
FF = 2816
PW = 5632
HP = 128
CHUNK = 64
WIN = 640
TQ = 512
EPS = 1e-6
NEG = -1e30
LRU_C = 8.0
N_DEV = 8
VMEM_LIMIT = 56 * 1024 * 1024

ADAM_LR, ADAM_B1, ADAM_B2, ADAM_EPS, ADAM_WD, ADAM_STEP = 0.001, 0.9, 0.999, 1e-08, 0.01, 10

MESH = pl.DeviceIdType.MESH
ANY = pl.BlockSpec(memory_space=pl.ANY)


def _cp(*sem):
    return pltpu.CompilerParams(dimension_semantics=tuple(sem), vmem_limit_bytes=VMEM_LIMIT)


def _dot(a, b):
    return jnp.dot(a, b, preferred_element_type=F32)


def _dot_nt(a, b):
    return lax.dot_general(a, b, (((1,), (1,)), ((), ())), preferred_element_type=F32)


def _dot_tn(a, b):
    return lax.dot_general(a, b, (((0,), (0,)), ((), ())), preferred_element_type=F32)


def _mean(v):
    return jnp.mean(v, axis=-1, keepdims=True)


def _colsum(v):
    return jnp.sum(v, axis=0, keepdims=True)


def _sigmoid(v):
    return 0.5 * jnp.tanh(0.5 * v) + 0.5


_GK = 0.7978845608028654


def _gelu(v):
    t = jnp.tanh(_GK * (v + 0.044715 * v * v * v))
    return 0.5 * v * (1.0 + t)


def _pre_norm(xv, vec_ref):
    r = lax.rsqrt(_mean(xv * xv) + EPS)
    n = xv * r * vec_ref[0:1, :]
    return n * (1.0 + vec_ref[3:4, :]) + vec_ref[2:3, :]


def _pre_norm_bwd(dh, xv, dres, vec_ref, vacc_ref):
    r = lax.rsqrt(_mean(xv * xv) + EPS)
    xh = xv * r
    n = xh * vec_ref[0:1, :]
    vacc_ref[2:3, :] += _colsum(dh)
    vacc_ref[3:4, :] += _colsum(dh * n)
    dn = dh * (1.0 + vec_ref[3:4, :])
    vacc_ref[0:1, :] += _colsum(dn * xh)
    dxh = dn * vec_ref[0:1, :]
    return r * (dxh - xh * _mean(dxh * xh)) + dres


def _post_norm_bwd(dxo, fv, res, vec_ref, vacc_ref):
    rf = lax.rsqrt(_mean(fv * fv) + EPS)
    fh = fv * rf
    gp = vec_ref[1:2, :]
    vacc_ref[4:5, :] += _colsum(res * dxo * (fh * gp))
    dy = (res * vec_ref[4:5, :]) * dxo
    vacc_ref[1:2, :] += _colsum(dy * fh)
    dfn = dy * gp
    return rf * (dfn - fh * _mean(dfn * fh))


def ffn_fwd(x, vec, w_gu, w_u, w_dn, res, name, tgt=None, tm=1024, tf=512):
    S = x.shape[0]
    tm = min(tm, S)
    nt = S // tm
    nf = -(-FF // tf)
    tail = FF - tf * (nf - 1)
    halves = [pl.ds(r * (tm // 2), tm // 2) for r in range(2)]
    head = tgt is not None

    def body(*refs):
        x_ref, vec_ref, wg_ref, wu_ref, wd_ref = refs[:5]
        if head:
            t_hbm, xo_ref, h_ref, g_ref, u_ref, a_ref, f_ref, l_ref, hs, acc, lacc, ts, tsem = refs[5:]
        else:
            xo_ref, h_ref, g_ref, u_ref, a_ref, f_ref, hs, acc = refs[5:]
        i, j = pl.program_id(0), pl.program_id(1)
        if head:
            late = pltpu.make_async_copy(t_hbm.at[pl.ds(pl.multiple_of(i * tm, 8), tm)], ts, tsem)

        @pl.when(j == 0)
        def _():
            if head:
                late.start()
            h = _pre_norm(x_ref[...], vec_ref).astype(BF16)
            hs[...] = h
            h_ref[...] = h
            acc[...] = jnp.zeros_like(acc)

        def chunk(w):
            gu = [(_dot(hs[r, :], wg_ref[:, 0:w]), _dot(hs[r, :], wu_ref[:, 0:w])) for r in halves]
            acts = []
            for r, (g, u) in zip(halves, gu):
                g_ref[r, 0:w] = g.astype(BF16)
                u_ref[r, 0:w] = u.astype(BF16)
                a = (g * _sigmoid(g) * u).astype(BF16)
                a_ref[r, 0:w] = a
                acts.append(a)
            for r, a in zip(halves, acts):
                acc[r, :] += _dot(a, wd_ref[0:w, :])

        @pl.when(j < nf - 1)
        def _():
            chunk(tf)

        @pl.when(j == nf - 1)
        def _():
            chunk(tail)
            f = acc[...]
            f_ref[...] = f.astype(BF16)
            y = f * lax.rsqrt(_mean(f * f) + EPS) * vec_ref[1:2, :]
            xo = x_ref[...] + (res * vec_ref[4:5, :]) * y
            if head:
                @pl.when(i == 0)
                def _():
                    lacc[...] = jnp.zeros_like(lacc)

                late.wait()
                d = xo - ts[...]
                xo_ref[...] = d * (1.0 / D)
                lacc[...] += _colsum(d * d)

                @pl.when(i == nt - 1)
                def _():
                    l_ref[...] = jnp.broadcast_to(0.5 * jnp.sum(lacc[...]) * (1.0 / D), (8, 128))
            else:
                xo_ref[...] = xo

    row = lambda i, j: (i, 0)
    col = lambda i, j: (i, j)
    once = dict(pipeline_mode=pl.Buffered(1)) if head else {}
    in_specs = [pl.BlockSpec((tm, D), row, **once), pl.BlockSpec((8, D), lambda i, j: (0, 0)),
                pl.BlockSpec((D, tf), lambda i, j: (0, j)), pl.BlockSpec((D, tf), lambda i, j: (0, j)),
                pl.BlockSpec((tf, D), lambda i, j: (j, 0))]
    out_specs = [pl.BlockSpec((tm, D), row), pl.BlockSpec((tm, D), row), pl.BlockSpec((tm, tf), col),
                 pl.BlockSpec((tm, tf), col), pl.BlockSpec((tm, tf), col), pl.BlockSpec((tm, D), row)]
    out_shape = [jax.ShapeDtypeStruct((S, D), F32), jax.ShapeDtypeStruct((S, D), BF16),
                 jax.ShapeDtypeStruct((S, FF), BF16), jax.ShapeDtypeStruct((S, FF), BF16),
                 jax.ShapeDtypeStruct((S, FF), BF16), jax.ShapeDtypeStruct((S, D), BF16)]
    scratch = [pltpu.VMEM((tm, D), BF16), pltpu.VMEM((tm, D), F32)]
    args = [x, vec, w_gu, w_u, w_dn]
    if head:
        in_specs.append(ANY)
        out_specs.append(pl.BlockSpec((8, 128), lambda i, j: (0, 0)))
        out_shape.append(jax.ShapeDtypeStruct((8, 128), F32))
        scratch += [pltpu.VMEM((1, D), F32), pltpu.VMEM((tm, D), F32), pltpu.SemaphoreType.DMA]
        args.append(tgt)
    return pl.pallas_call(
        body, name=name, grid=(nt, nf), in_specs=in_specs, out_specs=out_specs, out_shape=out_shape,
        scratch_shapes=scratch,
        compiler_params=_cp("arbitrary" if head else "parallel", "arbitrary"),
    )(*args)


def ffn_bwd(dxo, x, f, g, u, vec, w_gu, w_u, w_dn, res, name, tm=1024, tf=512):
    S = x.shape[0]
    tm = min(tm, S)
    nf = -(-FF // tf)
    tail = FF - tf * (nf - 1)
    halves = [pl.ds(r * (tm // 2), tm // 2) for r in range(2)]

    def body(dxo_ref, x_hbm, f_ref, g_ref, u_ref, vec_ref, wg_ref, wu_ref, wd_ref,
             dx_ref, df_ref, dgu_ref, vacc_ref, dfs, acc, xs, xsem):
        i, j = pl.program_id(0), pl.program_id(1)
        late = pltpu.make_async_copy(x_hbm.at[pl.ds(pl.multiple_of(i * tm, 8), tm)], xs, xsem)

        @pl.when((i == 0) & (j == 0))
        def _():
            vacc_ref[...] = jnp.zeros_like(vacc_ref)

        @pl.when(j == 0)
        def _():
            late.start()
            df = _post_norm_bwd(dxo_ref[...], f_ref[...].astype(F32), res, vec_ref, vacc_ref).astype(BF16)
            dfs[...] = df
            df_ref[...] = df
            acc[...] = jnp.zeros_like(acc)

        def chunk(w):
            da = [_dot_nt(dfs[h, :], wd_ref[0:w, :]) for h in halves]
            dgu = []
            for h, d in zip(halves, da):
                gv, uv = g_ref[h, 0:w].astype(F32), u_ref[h, 0:w].astype(F32)
                sg = _sigmoid(gv)
                dg = (d * uv * (sg * (1.0 + gv * (1.0 - sg)))).astype(BF16)
                du = (d * (gv * sg)).astype(BF16)
                dgu_ref[0, h, 0:w] = dg
                dgu_ref[1, h, 0:w] = du
                dgu.append((dg, du))
            for h, (dg, du) in zip(halves, dgu):
                acc[h, :] += _dot_nt(dg, wg_ref[:, 0:w]) + _dot_nt(du, wu_ref[:, 0:w])

        @pl.when(j < nf - 1)
        def _():
            chunk(tf)

        @pl.when(j == nf - 1)
        def _():
            chunk(tail)
            late.wait()
            dx_ref[...] = _pre_norm_bwd(acc[...], xs[...], dxo_ref[...], vec_ref, vacc_ref)

    row = lambda i, j: (i, 0)
    col = lambda i, j: (i, j)
    return pl.pallas_call(
        body, name=name, grid=(S // tm, nf),
        in_specs=[pl.BlockSpec((tm, D), row), ANY, pl.BlockSpec((tm, D), row),
                  pl.BlockSpec((tm, tf), col), pl.BlockSpec((tm, tf), col),
                  pl.BlockSpec((8, D), lambda i, j: (0, 0)),
                  pl.BlockSpec((D, tf), lambda i, j: (0, j)), pl.BlockSpec((D, tf), lambda i, j: (0, j)),
                  pl.BlockSpec((tf, D), lambda i, j: (j, 0))],
        out_specs=[pl.BlockSpec((tm, D), row), pl.BlockSpec((tm, D), row),
                   pl.BlockSpec((2, tm, tf), lambda i, j: (0, i, j)),
                   pl.BlockSpec((8, D), lambda i, j: (0, 0))],
        out_shape=[jax.ShapeDtypeStruct((S, D), F32), jax.ShapeDtypeStruct((S, D), BF16),
                   jax.ShapeDtypeStruct((2, S, FF), BF16), jax.ShapeDtypeStruct((8, D), F32)],
        scratch_shapes=[pltpu.VMEM((tm, D), BF16), pltpu.VMEM((tm, D), F32), pltpu.VMEM((tm, D), F32),
                        pltpu.SemaphoreType.DMA],
        compiler_params=_cp("arbitrary", "arbitrary"),
    )(dxo, x, f, g, u, vec, w_gu, w_u, w_dn)


def mm_tn(a, b, name, tm, tn, tk, out_dtype=BF16):
    S, M = a.shape
    if b.ndim == 3:
        G, _, Nf = b.shape
    else:
        G, Nf = 1, b.shape[1]
    N = G * Nf
    tk = min(tk, S)
    nbf = Nf // tn
    nk = S // tk

    def body(a_ref, b_ref, o_ref, acc):
        k = pl.program_id(2)

        @pl.when(k == 0)
        def _():
            acc[...] = jnp.zeros_like(acc)

        acc[...] += _dot_tn(a_ref[...], b_ref[...])

        @pl.when(k == nk - 1)
        def _():
            o_ref[...] = acc[...].astype(out_dtype)

    if b.ndim == 3:
        b_spec = pl.BlockSpec((None, tk, tn), lambda i, j, k: (j // nbf, k, j % nbf))
    else:
        b_spec = pl.BlockSpec((tk, tn), lambda i, j, k: (k, j))
    return pl.pallas_call(
        body, name=name, grid=(M // tm, N // tn, nk),
        in_specs=[pl.BlockSpec((tk, tm), lambda i, j, k: (k, i)), b_spec],
        out_specs=pl.BlockSpec((tm, tn), lambda i, j, k: (i, j)),
        out_shape=jax.ShapeDtypeStruct((M, N), out_dtype),
        scratch_shapes=[pltpu.VMEM((tm, tn), F32)],
        compiler_params=_cp("parallel", "parallel", "arbitrary"),
    )(a, b)


def proj_fwd(x, vec, w_in, name, tm=2048, tn=512):
    S = x.shape[0]
    tm = min(tm, S)
    nq = 1536 // tn

    def body(x_ref, vec_ref, w_ref, h_ref, qkv_ref, rest_ref, hs):
        j = pl.program_id(1)

        @pl.when(j == 0)
        def _():
            h = _pre_norm(x_ref[...], vec_ref).astype(BF16)
            hs[...] = h
            h_ref[...] = h

        r = _dot(hs[...], w_ref[...])

        @pl.when(j < nq)
        def _():
            qkv_ref[...] = r.astype(BF16)

        @pl.when(j >= nq)
        def _():
            rest_ref[...] = r.astype(BF16)

    row = lambda i, j: (i, 0)
    return pl.pallas_call(
        body, name=name, grid=(S // tm, PW // tn),
        in_specs=[pl.BlockSpec((tm, D), row), pl.BlockSpec((8, D), lambda i, j: (0, 0)),
                  pl.BlockSpec((D, tn), lambda i, j: (0, j))],
        out_specs=[pl.BlockSpec((tm, D), row),
                   pl.BlockSpec((tm, tn), lambda i, j: (i, jnp.minimum(j, nq - 1))),
                   pl.BlockSpec((tm, tn), lambda i, j: (i, jnp.maximum(j - nq, 0)))],
        out_shape=[jax.ShapeDtypeStruct((S, D), BF16), jax.ShapeDtypeStruct((S, 1536), BF16),
                   jax.ShapeDtypeStruct((S, 4096), BF16)],
        scratch_shapes=[pltpu.VMEM((tm, D), BF16)],
        compiler_params=_cp("parallel", "arbitrary"),
    )(x, vec, w_in)


def proj_bwd(dq, dkv, dxr, d3, w_in, x, dxo, vec, name, tm=2048, tk=512):
    S = x.shape[0]
    tm = min(tm, S)
    nk = PW // tk

    def body(dq_ref, dkv_ref, dxr_ref, d3_ref, w_ref, x_hbm, dxo_hbm, vec_ref, dx_ref, vacc_ref, acc, xs, dxos, sems):
        i, j = pl.program_id(0), pl.program_id(1)
        tok = pl.ds(pl.multiple_of(i * tm, 8), tm)
        late = (pltpu.make_async_copy(x_hbm.at[tok], xs, sems.at[0]),
                pltpu.make_async_copy(dxo_hbm.at[tok], dxos, sems.at[1]))

        @pl.when((i == 0) & (j == 0))
        def _():
            vacc_ref[...] = jnp.zeros_like(vacc_ref)

        @pl.when(j == 0)
        def _():
            for cp in late:
                cp.start()
            acc[...] = _dot_nt(dq_ref[...], w_ref[...])

        @pl.when((j >= 1) & (j < 3))
        def _():
            acc[...] += _dot_nt(dkv_ref[...], w_ref[...])

        @pl.when((j >= 3) & (j < 5))
        def _():
            acc[...] += _dot_nt(dxr_ref[...], w_ref[...])

        @pl.when(j >= 5)
        def _():
            acc[...] += _dot_nt(d3_ref[...], w_ref[...])

        @pl.when(j == nk - 1)
        def _():
            for cp in late:
                cp.wait()
            dx_ref[...] = _pre_norm_bwd(acc[...], xs[...], dxos[...], vec_ref, vacc_ref)

    row = lambda i, j: (i, 0)
    return pl.pallas_call(
        body, name=name, grid=(S // tm, nk),
        in_specs=[pl.BlockSpec((None, tm, tk), lambda i, j: (0, i, 0)),
                  pl.BlockSpec((None, tm, tk), lambda i, j: (jnp.clip(j - 1, 0, 1), i, 0)),
                  pl.BlockSpec((tm, tk), lambda i, j: (i, jnp.clip(j - 3, 0, 1))),
                  pl.BlockSpec((None, tm, tk), lambda i, j: (jnp.clip(j - 5, 0, 5) // 2, i, jnp.clip(j - 5, 0, 5) % 2)),
                  pl.BlockSpec((D, tk), lambda i, j: (0, j)),
                  ANY, ANY,
                  pl.BlockSpec((8, D), lambda i, j: (0, 0))],
        out_specs=[pl.BlockSpec((tm, D), row, pipeline_mode=pl.Buffered(1)),
                   pl.BlockSpec((8, D), lambda i, j: (0, 0))],
        out_shape=[jax.ShapeDtypeStruct((S, D), F32), jax.ShapeDtypeStruct((8, D), F32)],
        scratch_shapes=[pltpu.VMEM((tm, D), F32), pltpu.VMEM((tm, D), F32), pltpu.VMEM((tm, D), F32),
                        pltpu.SemaphoreType.DMA((2,))],
        compiler_params=_cp("arbitrary", "arbitrary"),
    )(dq, dkv, dxr, d3, w_in, x, dxo, vec)


def _two_heads(v, lane):
    zero = jnp.zeros((), v.dtype)
    return jnp.concatenate([jnp.where(lane < 64, v, zero), jnp.where(lane >= 64, v, zero)], axis=0)


def _attn_scores(qm, ka, bias_h, i, grp):
    s = _dot_nt(qm, ka) + bias_h
    col = lax.broadcasted_iota(jnp.int32, s.shape, 1)
    first_key = jnp.where(i == 0, 512 - 128 * grp, 0)
    return jnp.where(col >= first_key, s, NEG)


def _softmax(s):
    e = jnp.exp(s - jnp.max(s, axis=-1, keepdims=True))
    return e * (1.0 / jnp.sum(e, axis=-1, keepdims=True))


NG = TQ // 128


def attn_fwd(qkv, bias, name):
    S = qkv.shape[0]
    nb = S // TQ

    def body(q_ref, kp_ref, kc_ref, vp_ref, vc_ref, b_ref, o_ref, kw, vw):
        i = pl.program_id(1)
        kw[0:TQ, :] = kp_ref[...]
        kw[TQ:2 * TQ, :] = kc_ref[...]
        vw[0:TQ, :] = vp_ref[...]
        vw[TQ:2 * TQ, :] = vc_ref[...]
        lane = lax.broadcasted_iota(jnp.int32, (1, HP), 1)

        rows = [pl.ds(128 * a, 128) for a in range(NG)]
        keys = [pl.ds(128 * a, WIN) for a in range(NG)]
        q2 = [_two_heads(q_ref[r, :] * jnp.asarray(0.125, BF16), lane) for r in rows]
        s = [_attn_scores(q2[a], kw[keys[a], :], b_ref[...], i, a) for a in range(NG)]
        p = [_softmax(sa).astype(BF16) for sa in s]
        o2 = [_dot(p[a], vw[keys[a], :]) for a in range(NG)]
        for a in range(NG):
            o_ref[rows[a], :] = jnp.where(lane < 64, o2[a][0:128], o2[a][128:256]).astype(BF16)

    prev = lambda h, i: (jnp.maximum(i - 1, 0), 0)
    return pl.pallas_call(
        body, name=name, grid=(4, nb),
        in_specs=[pl.BlockSpec((TQ, HP), lambda h, i: (i, h)),
                  pl.BlockSpec((TQ, HP), lambda h, i: (jnp.maximum(i - 1, 0), 4 + h)),
                  pl.BlockSpec((TQ, HP), lambda h, i: (i, 4 + h)),
                  pl.BlockSpec((TQ, HP), lambda h, i: (jnp.maximum(i - 1, 0), 8 + h)),
                  pl.BlockSpec((TQ, HP), lambda h, i: (i, 8 + h)),
                  pl.BlockSpec((None, 256, WIN), lambda h, i: (h, 0, 0))],
        out_specs=pl.BlockSpec((TQ, HP), lambda h, i: (i, h)),
        out_shape=jax.ShapeDtypeStruct((S, 512), BF16),
        scratch_shapes=[pltpu.VMEM((2 * TQ, HP), BF16), pltpu.VMEM((2 * TQ, HP), BF16)],
        compiler_params=_cp("parallel", "arbitrary"),
    )(qkv, qkv, qkv, qkv, qkv, bias)


def attn_bwd(qkv, do, bias, name):
    S = qkv.shape[0]
    nb = S // TQ

    def body(q_ref, kp_ref, kc_ref, vp_ref, vc_ref, do_ref, b_ref, dqkv_ref, db_ref, dkv_ref, kw, vw, ak, av):
        i = pl.program_id(1)

        @pl.when(i == 0)
        def _():
            db_ref[...] = jnp.zeros_like(db_ref)
            ak[...] = jnp.zeros_like(ak)
            av[...] = jnp.zeros_like(av)

        @pl.when(i > 0)
        def _():
            ak[0:TQ, :] = ak[TQ:2 * TQ, :]
            av[0:TQ, :] = av[TQ:2 * TQ, :]
            ak[TQ:2 * TQ, :] = jnp.zeros((TQ, HP), F32)
            av[TQ:2 * TQ, :] = jnp.zeros((TQ, HP), F32)

        @pl.when(i < nb)
        def _():
            kw[0:TQ, :] = kp_ref[...]
            kw[TQ:2 * TQ, :] = kc_ref[...]
            vw[0:TQ, :] = vp_ref[...]
            vw[TQ:2 * TQ, :] = vc_ref[...]
            lane = lax.broadcasted_iota(jnp.int32, (1, HP), 1)

            rows = [pl.ds(128 * a, 128) for a in range(NG)]
            keys = [pl.ds(128 * a, WIN) for a in range(NG)]
            q2 = [_two_heads(q_ref[r, :] * jnp.asarray(0.125, BF16), lane) for r in rows]
            do2 = [_two_heads(do_ref[r, :], lane) for r in rows]
            s = [_attn_scores(q2[a], kw[keys[a], :], b_ref[...], i, a) for a in range(NG)]
            dp = [_dot_nt(do2[a], vw[keys[a], :]) for a in range(NG)]
            p = [_softmax(sa) for sa in s]
            ds = [p[a] * (dp[a] - jnp.sum(p[a] * dp[a], axis=-1, keepdims=True)) for a in range(NG)]
            db_ref[...] += (ds[0] + ds[1]) + (ds[2] + ds[3])
            dsb = [d.astype(BF16) for d in ds]
            dq2 = [_dot(dsb[a], kw[keys[a], :]) for a in range(NG)]
            dk = [_dot_tn(dsb[a], q2[a]) for a in range(NG)]
            dv = [_dot_tn(p[a].astype(BF16), do2[a]) for a in range(NG)]
            for a in range(NG):
                ak[keys[a], :] += dk[a]
                av[keys[a], :] += dv[a]
                dq = jnp.where(lane < 64, dq2[a][0:128], dq2[a][128:256])
                dqkv_ref[0, rows[a], :] = (dq * 0.125).astype(BF16)

        @pl.when(i > 0)
        def _():
            dkv_ref[0] = ak[0:TQ, :].astype(BF16)
            dkv_ref[1] = av[0:TQ, :].astype(BF16)

    cur = lambda i: jnp.minimum(i, nb - 1)
    prv = lambda i: jnp.clip(i - 1, 0, nb - 1)
    dq, db, dkv = pl.pallas_call(
        body, name=name, grid=(4, nb + 1),
        in_specs=[pl.BlockSpec((TQ, HP), lambda h, i: (cur(i), h)),
                  pl.BlockSpec((TQ, HP), lambda h, i: (prv(i), 4 + h)),
                  pl.BlockSpec((TQ, HP), lambda h, i: (cur(i), 4 + h)),
                  pl.BlockSpec((TQ, HP), lambda h, i: (prv(i), 8 + h)),
                  pl.BlockSpec((TQ, HP), lambda h, i: (cur(i), 8 + h)),
                  pl.BlockSpec((TQ, HP), lambda h, i: (cur(i), h)),
                  pl.BlockSpec((None, 256, WIN), lambda h, i: (h, 0, 0))],
        out_specs=[pl.BlockSpec((1, TQ, HP), lambda h, i: (0, cur(i), h)),
                   pl.BlockSpec((None, 256, WIN), lambda h, i: (h, 0, 0)),
                   pl.BlockSpec((2, TQ, HP), lambda h, i: (0, prv(i), h))],
        out_shape=[jax.ShapeDtypeStruct((1, S, 512), BF16), jax.ShapeDtypeStruct((4, 256, WIN), F32),
                   jax.ShapeDtypeStruct((2, S, 512), BF16)],
        scratch_shapes=[pltpu.VMEM((2 * TQ, HP), BF16), pltpu.VMEM((2 * TQ, HP), BF16),
                        pltpu.VMEM((2 * TQ, HP), F32), pltpu.VMEM((2 * TQ, HP), F32)],
        compiler_params=_cp("parallel", "arbitrary"),
    )(qkv, qkv, qkv, qkv, qkv, do, bias)
    return dq, db, dkv


def bias_grad(db, name):
    def body(db_ref, o_ref):
        r = lax.broadcasted_iota(jnp.int32, (128, 128), 0)
        c = lax.broadcasted_iota(jnp.int32, (128, 128), 1)
        flip = (r + c == 127).astype(BF16)
        lane = lax.broadcasted_iota(jnp.int32, (16, 384), 1)
        src = lax.broadcasted_iota(jnp.int32, (128, 384), 0)
        dst = lax.broadcasted_iota(jnp.int32, (128, 384), 1)

        def split_dot(v, m):
            hi = v.astype(BF16)
            r1 = v - hi.astype(F32)
            mid = r1.astype(BF16)
            lo = (r1 - mid.astype(F32)).astype(BF16)
            return _dot(hi, m) + _dot(mid, m) + _dot(lo, m)

        def diag_sums(w):
            y = pltpu.roll(split_dot(w, flip), 0, 1, stride=1, stride_axis=0)
            return jnp.broadcast_to(_colsum(y), (16, 128))

        w4 = db_ref[0, :, 512:640]
        w3 = db_ref[0, :, 384:512]
        far = jnp.sum(db_ref[0, :, 0:384]) + jnp.sum(jnp.where(r >= c, w3, 0.0))
        lo4 = diag_sums(jnp.where(r >= c, w4, 0.0))
        up4 = diag_sums(jnp.where(r < c, w4, 0.0))
        up3 = diag_sums(jnp.where(r < c, w3, 0.0))
        p_lo4 = (dst == 128 + (src + 1) % 128).astype(BF16)
        p_up4 = ((dst == src + 1) & (src < 127)).astype(BF16)
        p_up3 = ((dst == src + 129) & (src < 127)).astype(BF16)
        out = split_dot(lo4, p_lo4) + split_dot(up4, p_up4) + split_dot(up3, p_up3)
        o_ref[0] = out + jnp.where(lane == 256, far, 0.0)

    return pl.pallas_call(
        body, name=name, grid=(8,),
        in_specs=[pl.BlockSpec((1, 128, WIN), lambda h: (h, 0, 0))],
        out_specs=pl.BlockSpec((1, 16, 384), lambda h: (h, 0, 0)),
        out_shape=jax.ShapeDtypeStruct((8, 16, 384), F32),
        compiler_params=_cp("parallel"),
    )(db)[:, 0, :]


LT = 1024
LC = 512


def _lru_gates(xs, pv_ref, wa_ref, wx_ref, tl):
    xc = (pv_ref[4:5, :] + pv_ref[3:4, :] * xs[pl.ds(8, tl), :] + pv_ref[2:3, :] * xs[pl.ds(7, tl), :]
          + pv_ref[1:2, :] * xs[pl.ds(6, tl), :] + pv_ref[0:1, :] * xs[pl.ds(5, tl), :])
    xcb = xc.astype(BF16)
    pa = jnp.concatenate([_dot(xcb[:, 0:256], wa_ref[0]), _dot(xcb[:, 256:512], wa_ref[1])], axis=1)
    px = jnp.concatenate([_dot(xcb[:, 0:256], wx_ref[0]), _dot(xcb[:, 256:512], wx_ref[1])], axis=1)
    r = _sigmoid(pa + pv_ref[5:6, :])
    ig = _sigmoid(px + pv_ref[6:7, :])
    z = -pv_ref[7:8, :]
    sp = jnp.maximum(z, 0.0) + jnp.log1p(jnp.exp(-jnp.abs(z)))
    log_a = (-LRU_C * r) * sp
    a = jnp.exp(log_a)
    s = jnp.tanh(-log_a) * (1.0 + a * a)
    inv_mult = lax.rsqrt(s)
    mult = jnp.where(s > 0.0, s * inv_mult, 0.0)
    return xc, xcb, r, ig, sp, a, mult, inv_mult


def lru_fwd(rest, pvec, wa, wx, name):
    S = rest.shape[0]
    tl = min(LT, S)
    nt = S // tl

    def body(xr_ref, halo_ref, yr_ref, pv_ref, wa_ref, wx_ref, h_ref, hg_ref, xs, a_s, u_s, h_s, carry):
        ti = pl.program_id(1)

        @pl.when(ti == 0)
        def _():
            carry[...] = jnp.zeros_like(carry)

        xs[0:8, :] = jnp.where(ti > 0, halo_ref[8:16, :].astype(F32), 0.0)
        xs[pl.ds(8, tl), :] = xr_ref[...].astype(F32)
        xc, _, _, ig, _, a, mult, _ = _lru_gates(xs, pv_ref, wa_ref, wx_ref, tl)
        a_s[...] = a
        u_s[...] = mult * (ig * xc)
        row = lax.broadcasted_iota(jnp.int32, (8, LC), 0)

        def blk(bi, c):
            o = pl.multiple_of(bi * 8, 8)
            av = a_s[pl.ds(o, 8), :]
            bv = u_s[pl.ds(o, 8), :]
            for d in (1, 2, 4):
                a_sh = pltpu.roll(av, d, 0)
                b_sh = pltpu.roll(bv, d, 0)
                m = row >= d
                bv = jnp.where(m, av * b_sh + bv, bv)
                av = jnp.where(m, av * a_sh, av)
            hv = bv + av * c
            h_s[pl.ds(o, 8), :] = hv
            return hv[7:8, :]

        carry[...] = lax.fori_loop(0, tl // 8, blk, carry[...])
        h = h_s[...]
        h_ref[...] = h
        hg_ref[...] = (h * _gelu(yr_ref[...].astype(F32))).astype(BF16)

    hb = tl // 16
    return pl.pallas_call(
        body, name=name, grid=(2, nt),
        in_specs=[pl.BlockSpec((tl, LC), lambda c, t: (t, c)),
                  pl.BlockSpec((16, LC), lambda c, t: (jnp.maximum(t * hb - 1, 0), c)),
                  pl.BlockSpec((tl, LC), lambda c, t: (t, 2 + c)),
                  pl.BlockSpec((8, LC), lambda c, t: (0, c)),
                  pl.BlockSpec((2, 256, 256), lambda c, t: (c, 0, 0)),
                  pl.BlockSpec((2, 256, 256), lambda c, t: (c, 0, 0))],
        out_specs=[pl.BlockSpec((tl, LC), lambda c, t: (t, c)), pl.BlockSpec((tl, LC), lambda c, t: (t, c))],
        out_shape=[jax.ShapeDtypeStruct((S, D), F32), jax.ShapeDtypeStruct((S, D), BF16)],
        scratch_shapes=[pltpu.VMEM((tl + 8, LC), F32), pltpu.VMEM((tl, LC), F32), pltpu.VMEM((tl, LC), F32),
                        pltpu.VMEM((tl, LC), F32), pltpu.VMEM((1, LC), F32)],
        compiler_params=_cp("parallel", "arbitrary"),
    )(rest, rest, rest, pvec, wa, wx)


def lru_bwd(dh, h, rest, pvec, wa, wx, name):
    S = rest.shape[0]
    tl = min(LT, S)
    nt = S // tl

    def body(dh_ref, h_ref, hhalo_ref, xr_ref, xhalo_ref, pv_ref, wa_ref, wx_ref,
             dxr_ref, vacc_ref, dwa_ref, dwx_ref,
             xs, hs, a_s, ash_s, b_s, lam_s, dxe, anext, lnext, dxnext):
        ti = pl.program_id(1)
        tr = nt - 1 - ti

        @pl.when(ti == 0)
        def _():
            anext[...] = jnp.zeros_like(anext)
            lnext[...] = jnp.zeros_like(lnext)
            dxnext[...] = jnp.zeros_like(dxnext)
            vacc_ref[...] = jnp.zeros_like(vacc_ref)
            dwa_ref[...] = jnp.zeros_like(dwa_ref)
            dwx_ref[...] = jnp.zeros_like(dwx_ref)

        xs[0:8, :] = jnp.where(tr > 0, xhalo_ref[8:16, :].astype(F32), 0.0)
        xs[pl.ds(8, tl), :] = xr_ref[...].astype(F32)
        xc, xcb, r, ig, sp, a, mult, inv_mult = _lru_gates(xs, pv_ref, wa_ref, wx_ref, tl)

        a_s[pl.ds(0, tl), :] = a
        a_s[pl.ds(tl, 8), :] = jnp.broadcast_to(anext[...], (8, LC))
        ash_s[...] = a_s[pl.ds(1, tl), :]
        b_s[...] = dh_ref[...]
        row = lax.broadcasted_iota(jnp.int32, (8, LC), 0)

        def blk(k, c):
            o = pl.multiple_of((tl // 8 - 1 - k) * 8, 8)
            av = ash_s[pl.ds(o, 8), :]
            bv = b_s[pl.ds(o, 8), :]
            for d in (1, 2, 4):
                a_sh = pltpu.roll(av, 8 - d, 0)
                b_sh = pltpu.roll(bv, 8 - d, 0)
                m = row < 8 - d
                bv = jnp.where(m, bv + av * b_sh, bv)
                av = jnp.where(m, av * a_sh, av)
            lv = bv + av * c
            lam_s[pl.ds(o, 8), :] = lv
            return lv[0:1, :]

        lnext[...] = lax.fori_loop(0, tl // 8, blk, lnext[...])
        anext[...] = a[0:1, :]
        lam = lam_s[...]

        hs[0:8, :] = jnp.where(tr > 0, hhalo_ref[...], 0.0)
        hs[pl.ds(8, tl), :] = h_ref[...]
        d_a = lam * hs[pl.ds(7, tl), :]
        d_mult = lam * (ig * xc)
        d_ig = lam * mult * xc
        dxc = lam * mult * ig
        d_log_a = d_a * a - d_mult * (a * a) * inv_mult
        d_r = d_log_a * (-LRU_C * sp)
        vacc_ref[7:8, :] += _colsum(d_log_a * (-LRU_C * r)) * (-_sigmoid(-pv_ref[7:8, :]))
        d_pa = d_r * r * (1.0 - r)
        d_px = d_ig * ig * (1.0 - ig)
        vacc_ref[5:6, :] += _colsum(d_pa)
        vacc_ref[6:7, :] += _colsum(d_px)
        dpa = d_pa.astype(BF16)
        dpx = d_px.astype(BF16)
        back = []
        for g in range(2):
            sl = slice(256 * g, 256 * g + 256)
            dwa_ref[g] += _dot_tn(xcb[:, sl], dpa[:, sl])
            dwx_ref[g] += _dot_tn(xcb[:, sl], dpx[:, sl])
            back.append(_dot_nt(dpa[:, sl], wa_ref[g]) + _dot_nt(dpx[:, sl], wx_ref[g]))
        dxc = dxc + jnp.concatenate(back, axis=1)
        vacc_ref[4:5, :] += _colsum(dxc)
        for k in range(4):
            vacc_ref[k:k + 1, :] += _colsum(dxc * xs[pl.ds(5 + k, tl), :])
        dxe[pl.ds(0, tl), :] = dxc
        dxe[pl.ds(tl, 8), :] = dxnext[...]
        dxr = (pv_ref[3:4, :] * dxc + pv_ref[2:3, :] * dxe[pl.ds(1, tl), :]
               + pv_ref[1:2, :] * dxe[pl.ds(2, tl), :] + pv_ref[0:1, :] * dxe[pl.ds(3, tl), :])
        dxr_ref[...] = dxr.astype(BF16)
        dxnext[...] = dxc[0:8, :]

    hb = tl // 8
    rev = lambda t: nt - 1 - t
    halo = lambda t: jnp.maximum(rev(t) * hb - 1, 0)
    big = lambda: pltpu.VMEM((tl + 8, LC), F32)
    til = lambda: pltpu.VMEM((tl, LC), F32)
    return pl.pallas_call(
        body, name=name, grid=(2, nt),
        in_specs=[pl.BlockSpec((tl, LC), lambda c, t: (rev(t), c)),
                  pl.BlockSpec((tl, LC), lambda c, t: (rev(t), c)),
                  pl.BlockSpec((8, LC), lambda c, t: (halo(t), c)),
                  pl.BlockSpec((tl, LC), lambda c, t: (rev(t), c)),
                  pl.BlockSpec((16, LC), lambda c, t: (jnp.maximum(rev(t) * (tl // 16) - 1, 0), c)),
                  pl.BlockSpec((8, LC), lambda c, t: (0, c)),
                  pl.BlockSpec((2, 256, 256), lambda c, t: (c, 0, 0)),
                  pl.BlockSpec((2, 256, 256), lambda c, t: (c, 0, 0))],
        out_specs=[pl.BlockSpec((tl, LC), lambda c, t: (rev(t), c)),
                   pl.BlockSpec((8, LC), lambda c, t: (0, c)),
                   pl.BlockSpec((2, 256, 256), lambda c, t: (c, 0, 0)),
                   pl.BlockSpec((2, 256, 256), lambda c, t: (c, 0, 0))],
        out_shape=[jax.ShapeDtypeStruct((S, D), BF16), jax.ShapeDtypeStruct((8, D), F32),
                   jax.ShapeDtypeStruct((4, 256, 256), F32), jax.ShapeDtypeStruct((4, 256, 256), F32)],
        scratch_shapes=[big(), big(), big(), til(), til(), til(), big(),
                        pltpu.VMEM((1, LC), F32), pltpu.VMEM((1, LC), F32), pltpu.VMEM((8, LC), F32)],
        compiler_params=_cp("parallel", "arbitrary"),
    )(dh, h, h, rest, rest, pvec, wa, wx)


def mix_out_fwd(x, ao, hg, rest, vec, w_att_o, w_rec_o, w_out, name, tm=512):
    S = x.shape[0]
    tm = min(tm, S)

    def body(x_ref, ao_ref, hg_ref, ga_ref, gr_ref, vec_ref, wa_ref, wr_ref, wo_ref,
             xo_ref, att_ref, rec_ref, mg_ref, f_ref):
        att = _dot(ao_ref[...], wa_ref[...])
        rec = _dot(hg_ref[...], wr_ref[...])
        att_ref[...] = att.astype(BF16)
        rec_ref[...] = rec.astype(BF16)
        mg = (_sigmoid(ga_ref[...].astype(F32)) * att + _sigmoid(gr_ref[...].astype(F32)) * rec).astype(BF16)
        mg_ref[...] = mg
        f = _dot(mg, wo_ref[...])
        f_ref[...] = f.astype(BF16)
        y = f * lax.rsqrt(_mean(f * f) + EPS) * vec_ref[1:2, :]
        xo_ref[...] = x_ref[...] + (1.0 * vec_ref[4:5, :]) * y

    row = lambda i: (i, 0)
    full = lambda r: pl.BlockSpec((r, D), lambda i: (0, 0))
    return pl.pallas_call(
        body, name=name, grid=(S // tm,),
        in_specs=[pl.BlockSpec((tm, D), row), pl.BlockSpec((tm, 512), row), pl.BlockSpec((tm, D), row),
                  pl.BlockSpec((tm, D), lambda i: (i, 2)), pl.BlockSpec((tm, D), lambda i: (i, 3)),
                  full(8), full(512), full(D), full(D)],
        out_specs=[pl.BlockSpec((tm, D), row)] * 5,
        out_shape=[jax.ShapeDtypeStruct((S, D), F32)] + [jax.ShapeDtypeStruct((S, D), BF16)] * 4,
        compiler_params=_cp("parallel"),
    )(x, ao, hg, rest, rest, vec, w_att_o, w_rec_o, w_out)


def mix_out_bwd(dxo, f, att, rec, rest, h, vec, w_att_o, w_rec_o, w_out, name, tm=512):
    S = dxo.shape[0]
    tm = min(tm, S)

    def body(dxo_ref, f_ref, att_ref, rec_ref, yr_ref, ga_ref, gr_ref, h_ref, vec_ref, wa_ref, wr_ref, wo_ref,
             df_ref, da_ref, dr_ref, dao_ref, dh_ref, d3_ref, vacc_ref):
        @pl.when(pl.program_id(0) == 0)
        def _():
            vacc_ref[...] = jnp.zeros_like(vacc_ref)

        df = _post_norm_bwd(dxo_ref[...], f_ref[...].astype(F32), 1.0, vec_ref, vacc_ref).astype(BF16)
        df_ref[...] = df
        dm = _dot_nt(df, wo_ref[...])
        sa = _sigmoid(ga_ref[...].astype(F32))
        sr = _sigmoid(gr_ref[...].astype(F32))
        d_att = (dm * sa).astype(BF16)
        d_rec = (dm * sr).astype(BF16)
        da_ref[...] = d_att
        dr_ref[...] = d_rec
        d3_ref[1] = (dm * att_ref[...].astype(F32) * (sa * (1.0 - sa))).astype(BF16)
        d3_ref[2] = (dm * rec_ref[...].astype(F32) * (sr * (1.0 - sr))).astype(BF16)
        dao_ref[...] = _dot_nt(d_att, wa_ref[...]).astype(BF16)
        d_hg = _dot_nt(d_rec, wr_ref[...])
        yr = yr_ref[...].astype(F32)
        t = jnp.tanh(_GK * (yr + 0.044715 * yr * yr * yr))
        dh_ref[...] = d_hg * (0.5 * yr * (1.0 + t))
        gelu_grad = 0.5 * (1.0 + t) + 0.5 * yr * (1.0 - t * t) * _GK * (1.0 + 3.0 * 0.044715 * yr * yr)
        d3_ref[0] = (d_hg * h_ref[...] * gelu_grad).astype(BF16)

    row = lambda i: (i, 0)
    full = lambda r: pl.BlockSpec((r, D), lambda i: (0, 0))
    return pl.pallas_call(
        body, name=name, grid=(S // tm,),
        in_specs=[pl.BlockSpec((tm, D), row)] * 4
        + [pl.BlockSpec((tm, D), lambda i: (i, 1)), pl.BlockSpec((tm, D), lambda i: (i, 2)),
           pl.BlockSpec((tm, D), lambda i: (i, 3)), pl.BlockSpec((tm, D), row),
           full(8), full(512), full(D), full(D)],
        out_specs=[pl.BlockSpec((tm, D), row)] * 3
        + [pl.BlockSpec((tm, 512), row), pl.BlockSpec((tm, D), row),
           pl.BlockSpec((3, tm, D), lambda i: (0, i, 0)), pl.BlockSpec((8, D), lambda i: (0, 0))],
        out_shape=[jax.ShapeDtypeStruct((S, D), BF16)] * 3
        + [jax.ShapeDtypeStruct((S, 512), BF16), jax.ShapeDtypeStruct((S, D), F32),
           jax.ShapeDtypeStruct((3, S, D), BF16), jax.ShapeDtypeStruct((8, D), F32)],
        compiler_params=_cp("arbitrary"),
    )(dxo, f, att, rec, rest, rest, rest, h, vec, w_att_o, w_rec_o, w_out)


def dw_in(h, dq, dkv, dxr, d3, name, tk=1024, tn=512):
    S = h.shape[0]
    tk = min(tk, S)
    nk = S // tk

    def body(h_ref, dq_ref, dkv_ref, dxr_ref, d3_ref, o_ref, acc):
        j, k = pl.program_id(0), pl.program_id(1)

        @pl.when(k == 0)
        def _():
            acc[...] = jnp.zeros_like(acc)

        @pl.when(j == 0)
        def _():
            acc[...] += _dot_tn(h_ref[pl.ds(pl.multiple_of(k * tk, 16), tk), :], dq_ref[...])

        @pl.when((j >= 1) & (j < 3))
        def _():
            acc[...] += _dot_tn(h_ref[pl.ds(pl.multiple_of(k * tk, 16), tk), :], dkv_ref[...])

        @pl.when((j >= 3) & (j < 5))
        def _():
            acc[...] += _dot_tn(h_ref[pl.ds(pl.multiple_of(k * tk, 16), tk), :], dxr_ref[...])

        @pl.when(j >= 5)
        def _():
            acc[...] += _dot_tn(h_ref[pl.ds(pl.multiple_of(k * tk, 16), tk), :], d3_ref[...])

        @pl.when(k == nk - 1)
        def _():
            o_ref[...] = acc[...].astype(BF16)

    use = lambda j, k, lo, hi: jnp.where((j >= lo) & (j < hi), k, 0)
    g3 = lambda j: jnp.clip(j - 5, 0, 5)
    return pl.pallas_call(
        body, name=name, grid=(PW // tn, nk),
        in_specs=[pl.BlockSpec((S, D), lambda j, k: (0, 0), pipeline_mode=pl.Buffered(1)),
                  pl.BlockSpec((None, tk, tn), lambda j, k: (0, use(j, k, 0, 1), 0)),
                  pl.BlockSpec((None, tk, tn), lambda j, k: (jnp.clip(j - 1, 0, 1), use(j, k, 1, 3), 0)),
                  pl.BlockSpec((tk, tn), lambda j, k: (use(j, k, 3, 5), jnp.clip(j - 3, 0, 1))),
                  pl.BlockSpec((None, tk, tn), lambda j, k: (g3(j) // 2, use(j, k, 5, 11), g3(j) % 2))],
        out_specs=pl.BlockSpec((D, tn), lambda j, k: (0, j)),
        out_shape=jax.ShapeDtypeStruct((D, PW), BF16),
        scratch_shapes=[pltpu.VMEM((D, tn), F32)],
        compiler_params=_cp("parallel", "arbitrary"),
    )(h, dq, dkv, dxr, d3)


def ada_fwd(c_all, w_ada, b_ada, name, tn=768):
    n = w_ada.shape[1]

    def body(c_ref, w_ref, b_ref, o_ref):
        cv = c_ref[...]
        ca = (cv * _sigmoid(cv)).astype(BF16)
        o_ref[...] = _dot(ca, w_ref[...].astype(BF16)) + b_ref[...]

    return pl.pallas_call(
        body, name=name, grid=(n // tn,),
        in_specs=[pl.BlockSpec((8, D), lambda j: (0, 0)), pl.BlockSpec((D, tn), lambda j: (0, j)),
                  pl.BlockSpec((1, tn), lambda j: (0, j))],
        out_specs=pl.BlockSpec((8, tn), lambda j: (0, j)),
        out_shape=jax.ShapeDtypeStruct((8, n), F32),
        compiler_params=_cp("parallel"),
    )(c_all, w_ada, b_ada)


def ada_bwd(c_all_t, dmod, name, tn=768):
    n = dmod.shape[1]

    def body(c_ref, d_ref, o_ref):
        cv = c_ref[...]
        ca = (cv * _sigmoid(cv)).astype(BF16)
        o_ref[...] = _dot(ca, d_ref[...].astype(BF16))

    return pl.pallas_call(
        body, name=name, grid=(n // tn,),
        in_specs=[pl.BlockSpec((D, 128), lambda j: (0, 0)), pl.BlockSpec((128, tn), lambda j: (0, j))],
        out_specs=pl.BlockSpec((D, tn), lambda j: (0, j)),
        out_shape=jax.ShapeDtypeStruct((D, n), F32),
        compiler_params=_cp("parallel"),
    )(c_all_t, dmod)


def _row_tile(rows, cols, itemsize=4, budget=1536 * 1024):
    best = None
    for t in range(8, rows + 1, 8):
        if rows % t == 0 and t * cols * itemsize <= budget:
            best = t
    return rows if best is None else best


def sum_lead(parts, name, out_dtype=F32):
    n, R, C = parts.shape
    tr = _row_tile(R, C * n)

    def body(p_ref, o_ref):
        acc = p_ref[0].astype(F32)
        for k in range(1, n):
            acc = acc + p_ref[k].astype(F32)
        o_ref[...] = acc.astype(out_dtype)

    return pl.pallas_call(
        body, name=name, grid=(R // tr,),
        in_specs=[pl.BlockSpec((n, tr, C), lambda i: (0, i, 0))],
        out_specs=pl.BlockSpec((tr, C), lambda i: (i, 0)),
        out_shape=jax.ShapeDtypeStruct((R, C), out_dtype),
        compiler_params=_cp("parallel"),
    )(parts)


def adamw(w, g, m, v, name, emit_g=False):
    R, C = w.shape
    tr = _row_tile(R, C * 8, budget=8 * 1024 * 1024)

    def body(w_ref, g_ref, m_ref, v_ref, d_ref, mo_ref, vo_ref, *go_ref):
        gv = g_ref[...]
        if emit_g:
            go_ref[0][...] = gv
        mn = ADAM_B1 * m_ref[...] + (1.0 - ADAM_B1) * gv
        vn = ADAM_B2 * v_ref[...] + (1.0 - ADAM_B2) * (gv * gv)
        m_hat = mn / (1.0 - ADAM_B1 ** ADAM_STEP)
        v_hat = vn / (1.0 - ADAM_B2 ** ADAM_STEP)
        d_ref[...] = -ADAM_LR * (m_hat / (jnp.sqrt(v_hat) + ADAM_EPS) + ADAM_WD * w_ref[...])
        mo_ref[...] = mn
        vo_ref[...] = vn

    spec = pl.BlockSpec((tr, C), lambda i: (i, 0))
    return pl.pallas_call(
        body, name=name, grid=(R // tr,),
        in_specs=[spec] * 4, out_specs=[spec] * (4 if emit_g else 3),
        out_shape=[jax.ShapeDtypeStruct((R, C), F32)] * (4 if emit_g else 3),
        compiler_params=_cp("parallel"),
    )(w, g, m, v)


def _mesh_pos():
    return lax.axis_index("x"), lax.axis_index("y"), lax.axis_index("c")


def _other_chips(mx, my):
    return [(1 - mx, my), (mx, 1 - my), (1 - mx, 1 - my)]


def ag_small(x, name):
    R = x.shape[0]

    def body(x_ref, out_ref, send_sems, recv_sems, local_sem):
        mx, my, mc = _mesh_pos()
        me, sibling = (mx, my, mc), (mx, my, 1 - mc)
        chips = _other_chips(mx, my)

        def slot(px, py, pc):
            return out_ref.at[4 * px + 2 * py + pc]

        def copy(k, block, to, src=None):
            return pltpu.make_async_remote_copy(
                src_ref=slot(*block) if src is None else src, dst_ref=slot(*block),
                send_sem=send_sems.at[k], recv_sem=recv_sems.at[k], device_id=to, device_id_type=MESH)

        mine = pltpu.make_async_copy(x_ref, slot(*me), local_sem)
        mine.start()
        first = [copy(0, me, sibling, src=x_ref)]
        first += [copy(1 + j, me, (*chip, mc), src=x_ref) for j, chip in enumerate(chips)]
        for cp in first:
            cp.start()
        passed = [copy(4 + j, (*chip, mc), sibling) for j, chip in enumerate(chips)]
        for j, chip in enumerate(chips):
            copy(1 + j, (*chip, mc), me).wait_recv()
            passed[j].start()
        copy(0, sibling, me).wait_recv()
        for j, chip in enumerate(chips):
            copy(4 + j, (*chip, 1 - mc), me).wait_recv()
        for cp in first + passed:
            cp.wait_send()
        mine.wait()

    return pl.pallas_call(
        body, name=name,
        out_shape=jax.ShapeDtypeStruct((N_DEV, R, 128), F32),
        in_specs=[pl.BlockSpec(memory_space=pltpu.VMEM)],
        out_specs=pl.BlockSpec(memory_space=pltpu.VMEM),
        scratch_shapes=[pltpu.SemaphoreType.DMA((7,)), pltpu.SemaphoreType.DMA((7,)), pltpu.SemaphoreType.DMA],
        compiler_params=pltpu.CompilerParams(vmem_limit_bytes=VMEM_LIMIT),
    )(x)


BIG = (("ffn1_w_gu", "col", D, PW), ("ffn1_w_down", "row", FF, D), ("w_in", "col", D, PW),
       ("w_att_o", "col", 512, D), ("w_rec_o", "row", D, D), ("w_out", "row", D, D),
       ("ffn2_w_gu", "col", D, PW), ("ffn2_w_down", "row", FF, D))
NBIG = len(BIG)


def _shard_shape(kind, R, C):
    return (R, C // 4) if kind == "col" else (R // 4, C)


def _region(ref, kind, R, C, q, half, t, tr):
    sr, sc = _shard_shape(kind, R, C)
    if kind == "col":
        return ref.at[pl.ds(pl.multiple_of(half * (R // 2) + t * tr, 16), tr), pl.ds(q * sc, sc)]
    return ref.at[pl.ds(pl.multiple_of(q * sr + t * tr, 16), tr), pl.ds(half * (C // 2), C // 2)]


def ag_local(w, kind, R, C, p_arr, name, after=()):
    sr, sc = _shard_shape(kind, R, C)
    tr = _row_tile(sr, sc, budget=2 * 1024 * 1024)
    nt = sr // tr
    after = list(after)

    def body(p_ref, w_ref, *rest):
        rest[-1][...] = w_ref[...].astype(BF16)

    if kind == "col":
        o_spec = pl.BlockSpec((tr, sc), lambda i, p: (i, p[0]))
    else:
        o_spec = pl.BlockSpec((tr, sc), lambda i, p: (p[0] * nt + i, 0))
    return pl.pallas_call(
        body, name=name,
        grid_spec=pltpu.PrefetchScalarGridSpec(
            num_scalar_prefetch=1, grid=(nt,),
            in_specs=[pl.BlockSpec((tr, sc), lambda i, p: (i, 0))] + [ANY] * len(after), out_specs=o_spec),
        out_shape=jax.ShapeDtypeStruct((R, C), BF16),
        compiler_params=_cp("parallel"),
    )(p_arr, w, *after)


HBM_SPEC = pl.BlockSpec(memory_space=pltpu.HBM)
SEM_SPEC = pl.BlockSpec(memory_space=pltpu.SEMAPHORE)


def _ag_sems(geoms):
    return sum(6 if both else 3 for (_, _, _, both) in geoms)


def _ag_copies(fulls, geoms, ssem, rsem, mx, my, mc, q, h):
    chips = _other_chips(mx, my)
    out, base = [], 0
    for w, (kind, R, C, both) in enumerate(geoms):
        sr, sc = _shard_shape(kind, R, C)
        hr = sr // 2 if kind == "col" else sr
        reg = _region(fulls[w], kind, R, C, q, h, 0, hr)
        out.append([pltpu.make_async_remote_copy(
            src_ref=reg, dst_ref=reg, send_sem=ssem.at[base + 3 * t + k], recv_sem=rsem.at[base + 3 * t + k],
            device_id=(*chips[k], mc if t == 0 else 1 - mc), device_id_type=MESH)
            for t in range(2 if both else 1) for k in range(3)])
        base += 6 if both else 3
    return out


def ag_start(fulls, geoms, after, name):
    n = len(fulls)
    after = list(after)
    m = len(after)

    def body(*refs):
        ssem, rsem = refs[n + m:n + m + 2]
        outs, token = refs[n + m + 2:2 * n + m + 2], refs[2 * n + m + 2]
        mx, my, mc = _mesh_pos()
        p = 2 * mx + my
        col = [w for w, g in enumerate(geoms) if g[0] == "col"]
        row = [w for w, g in enumerate(geoms) if g[0] == "row"]
        for q in range(4):
            @pl.when(p == q)
            def _(q=q):
                cps = _ag_copies(outs, geoms, ssem, rsem, mx, my, mc, q, mc)
                for w in col:
                    for cp in cps[w]:
                        cp.start()
        for h in range(2):
            @pl.when(mc == h)
            def _(h=h):
                cps = _ag_copies(outs, geoms, ssem, rsem, mx, my, mc, p, h)
                for w in row:
                    for cp in cps[w]:
                        cp.start()
        token[...] = jnp.zeros_like(token)

    res = pl.pallas_call(
        body, name=name,
        out_shape=[pltpu.SemaphoreType.DMA((_ag_sems(geoms),)), pltpu.SemaphoreType.DMA((_ag_sems(geoms),))]
        + [pltpu.HBM(a.shape, a.dtype) for a in fulls] + [jax.ShapeDtypeStruct((8, 128), F32)],
        in_specs=[HBM_SPEC] * n + [ANY] * m,
        out_specs=[SEM_SPEC, SEM_SPEC] + [HBM_SPEC] * n + [pl.BlockSpec(memory_space=pltpu.VMEM)],
        input_output_aliases={w: 2 + w for w in range(n)},
        compiler_params=pltpu.CompilerParams(has_side_effects=pltpu.SideEffectType.DATAFLOW_SIDE_EFFECTING),
    )(*[pltpu.with_memory_space_constraint(a, pltpu.HBM) for a in fulls], *after)
    return res[0], res[1], list(res[2:2 + n]), res[2 + n]


def ag_wait(fulls, geoms, ssem, rsem, after, name):
    n = len(fulls)
    after = list(after) if isinstance(after, (list, tuple)) else [after]

    def body(*refs):
        ins, ssem_ref, rsem_ref = refs[:n], refs[n], refs[n + 1]
        mx, my, mc = _mesh_pos()
        for cps in _ag_copies(ins, geoms, ssem_ref, rsem_ref, mx, my, mc, 0, 0):
            for cp in cps:
                cp.wait_send()
                cp.wait_recv()

    return list(pl.pallas_call(
        body, name=name,
        out_shape=[pltpu.HBM(a.shape, a.dtype) for a in fulls],
        in_specs=[HBM_SPEC] * n + [SEM_SPEC, SEM_SPEC] + [ANY] * len(after),
        out_specs=[HBM_SPEC] * n,
        input_output_aliases={w: w for w in range(n)},
        compiler_params=pltpu.CompilerParams(has_side_effects=pltpu.SideEffectType.DATAFLOW_SIDE_EFFECTING),
    )(*fulls, ssem, rsem, *after))


def ag_forward(full, kind, R, C, name):
    sr, sc = _shard_shape(kind, R, C)
    hr, hc = (sr // 2, sc) if kind == "col" else (sr, sc // 2)
    tr = _row_tile(hr, hc, itemsize=2, budget=512 * 1024)
    nt = hr // tr

    total = 3 * nt

    def body(src_ref, full_ref, stage, lsem, ssem, rsem):
        step = pl.program_id(0) * nt + pl.program_id(1)
        par = step % 2
        mx, my, mc = _mesh_pos()

        def load(s, q, h, t):
            return pltpu.make_async_copy(_region(src_ref, kind, R, C, q, h, t, tr), stage.at[s], lsem.at[s])

        def push(s, q, h, t):
            return pltpu.make_async_remote_copy(src_ref=stage.at[s], dst_ref=_region(full_ref, kind, R, C, q, h, t, tr),
                                                send_sem=ssem.at[s], recv_sem=rsem, device_id=(mx, my, 1 - mc),
                                                device_id_type=MESH)

        def for_tile(stp, fn):
            q_k = _partner_chip(stp // nt, 2 * mx + my)
            if kind == "col":
                for q in range(4):
                    @pl.when(q_k == q)
                    def _(q=q):
                        fn(q, mc, stp % nt)
            else:
                for h in range(2):
                    @pl.when(mc == h)
                    def _(h=h):
                        fn(q_k, h, stp % nt)

        @pl.when(step == 0)
        def _():
            for_tile(step, lambda q, h, t: load(0, q, h, t).start())

        load(par, 0, 0, 0).wait()
        for_tile(step, lambda q, h, t: push(par, q, h, t).start())

        @pl.when(step + 1 < total)
        def _():
            @pl.when(step >= 1)
            def _():
                push(1 - par, 0, 0, 0).wait_send()
            for_tile(step + 1, lambda q, h, t: load(1 - par, q, h, t).start())

        @pl.when(step == total - 1)
        def _():
            push(par, 0, 0, 0).wait_send()
            push(1 - par, 0, 0, 0).wait_send()
            three = full_ref.at[pl.ds(0, hr), pl.ds(0, 3 * hc)] if kind == "col" else full_ref.at[pl.ds(0, 3 * hr), pl.ds(0, hc)]
            pltpu.make_async_remote_copy(src_ref=three, dst_ref=three, send_sem=ssem.at[0], recv_sem=rsem,
                                         device_id=(mx, my, 1 - mc), device_id_type=MESH).wait_recv()

    return pl.pallas_call(
        body, name=name, grid=(3, nt),
        in_specs=[ANY], out_specs=ANY,
        out_shape=jax.ShapeDtypeStruct((R, C), BF16),
        scratch_shapes=[pltpu.VMEM((2, tr, hc), BF16), pltpu.SemaphoreType.DMA((2,)), pltpu.SemaphoreType.DMA((2,)),
                        pltpu.SemaphoreType.DMA],
        input_output_aliases={0: 0},
        compiler_params=_cp("arbitrary", "arbitrary"),
    )(full)


def _half_shape(kind, R, C):
    return (R // 2, C) if kind == "col" else (R, C // 2)


def _piece_shape(kind, R, C):
    return (R // 2, C // 4) if kind == "col" else (R // 4, C // 2)


def pair_push(g, kind, c_arr, name):
    R, C = g.shape
    hr, hc = _half_shape(kind, R, C)
    tr = _row_tile(hr, hc, itemsize=2, budget=1024 * 1024)
    nt = hr // tr

    def body(c_ref, g_ref, out_ref, stage, ssem, rsem):
        i = pl.program_id(0)
        slot = i % 2
        mx, my, mc = _mesh_pos()

        def push(s, t):
            return pltpu.make_async_remote_copy(
                src_ref=stage.at[s], dst_ref=out_ref.at[pl.ds(pl.multiple_of(t * tr, 16), tr)],
                send_sem=ssem.at[s], recv_sem=rsem, device_id=(mx, my, 1 - mc), device_id_type=MESH)

        @pl.when(i >= 2)
        def _():
            push(slot, 0).wait_send()

        stage[slot] = g_ref[...]
        push(slot, i).start()

        @pl.when(i == nt - 1)
        def _():
            push(slot, 0).wait_send()
            if nt >= 2:
                push(1 - slot, 0).wait_send()
            pltpu.make_async_remote_copy(src_ref=out_ref, dst_ref=out_ref, send_sem=ssem.at[0], recv_sem=rsem,
                                         device_id=(mx, my, 1 - mc), device_id_type=MESH).wait_recv()

    if kind == "col":
        g_spec = pl.BlockSpec((tr, hc), lambda i, c: ((1 - c[0]) * nt + i, 0))
    else:
        g_spec = pl.BlockSpec((tr, hc), lambda i, c: (i, 1 - c[0]))
    return pl.pallas_call(
        body, name=name,
        grid_spec=pltpu.PrefetchScalarGridSpec(
            num_scalar_prefetch=1, grid=(nt,), in_specs=[g_spec], out_specs=ANY,
            scratch_shapes=[pltpu.VMEM((2, tr, hc), BF16), pltpu.SemaphoreType.DMA((2,)), pltpu.SemaphoreType.DMA]),
        out_shape=jax.ShapeDtypeStruct((hr, hc), BF16),
        compiler_params=_cp("arbitrary"),
    )(c_arr, g)


def _partner_chip(k, p):
    return p ^ jnp.where(k == 0, 2, jnp.where(k == 1, 1, jnp.where(k == 2, 3, 0)))


def pair_add(g, got, kind, cp_arr, name):
    R, C = g.shape
    pr, pc = _piece_shape(kind, R, C)
    tr = _row_tile(pr, pc, itemsize=2, budget=1024 * 1024)
    nt = pr // tr

    def body(cp_ref, g_ref, got_ref, ps_ref, rb_ref):
        tile = (g_ref[...].astype(F32) + got_ref[...].astype(F32)).astype(BF16)
        ps_ref[...] = tile

        @pl.when(pl.program_id(1) == cp_ref[1])
        def _():
            rb_ref[...] = tile

    if kind == "col":
        g_spec = pl.BlockSpec((tr, pc), lambda i, q, cp: (cp[0] * nt + i, q))
        got_spec = pl.BlockSpec((tr, pc), lambda i, q, cp: (i, q))
    else:
        g_spec = pl.BlockSpec((tr, pc), lambda i, q, cp: (q * nt + i, cp[0]))
        got_spec = pl.BlockSpec((tr, pc), lambda i, q, cp: (q * nt + i, 0))
    return pl.pallas_call(
        body, name=name,
        grid_spec=pltpu.PrefetchScalarGridSpec(
            num_scalar_prefetch=1, grid=(nt, 4), in_specs=[g_spec, got_spec],
            out_specs=[pl.BlockSpec((None, tr, pc), lambda i, q, cp: (q, i, 0)),
                       pl.BlockSpec((None, tr, pc), lambda i, q, cp: (cp[1], i, 0))]),
        out_shape=[jax.ShapeDtypeStruct((4, pr, pc), BF16)] * 2,
        compiler_params=_cp("arbitrary", "arbitrary"),
    )(cp_arr, g, got)


def _rs_copies(ps, rb, ssem, rsem, mx, my, mc):
    p = 2 * mx + my
    out = []
    for w in range(len(ps)):
        for k, chip in enumerate(_other_chips(mx, my)):
            out.append(pltpu.make_async_remote_copy(
                src_ref=ps[w].at[2 * chip[0] + chip[1]], dst_ref=rb[w].at[p], send_sem=ssem.at[3 * w + k],
                recv_sem=rsem.at[3 * w + k], device_id=(*chip, mc), device_id_type=MESH))
    return out


def rs_start(ps, rb, after, name):
    n = len(ps)
    after = list(after)
    m = len(after)

    def body(*refs):
        ssem, rsem = refs[2 * n + m:2 * n + m + 2]
        ps_o = refs[2 * n + m + 2:3 * n + m + 2]
        rb_o = refs[3 * n + m + 2:4 * n + m + 2]
        token = refs[4 * n + m + 2]
        for cp in _rs_copies(ps_o, rb_o, ssem, rsem, *_mesh_pos()):
            cp.start()
        token[...] = jnp.zeros_like(token)

    both = list(ps) + list(rb)
    res = pl.pallas_call(
        body, name=name,
        out_shape=[pltpu.SemaphoreType.DMA((3 * n,)), pltpu.SemaphoreType.DMA((3 * n,))]
        + [pltpu.HBM(a.shape, a.dtype) for a in both] + [jax.ShapeDtypeStruct((8, 128), F32)],
        in_specs=[HBM_SPEC] * (2 * n) + [ANY] * m,
        out_specs=[SEM_SPEC, SEM_SPEC] + [HBM_SPEC] * (2 * n) + [pl.BlockSpec(memory_space=pltpu.VMEM)],
        input_output_aliases={w: 2 + w for w in range(2 * n)},
        compiler_params=pltpu.CompilerParams(has_side_effects=pltpu.SideEffectType.DATAFLOW_SIDE_EFFECTING),
    )(*[pltpu.with_memory_space_constraint(a, pltpu.HBM) for a in both], *after)
    return res[0], res[1], list(res[2:2 + n]), list(res[2 + n:2 + 2 * n]), res[2 + 2 * n]


def rs_wait(ps, rb, ssem, rsem, after, name):
    n = len(ps)
    after = list(after)
    m = len(after)

    def body(*refs):
        ps_i, rb_i = refs[:n], refs[n:2 * n]
        ssem_ref, rsem_ref = refs[2 * n], refs[2 * n + 1]
        for cp in _rs_copies(ps_i, rb_i, ssem_ref, rsem_ref, *_mesh_pos()):
            cp.wait_send()
            cp.wait_recv()

    both = list(ps) + list(rb)
    res = pl.pallas_call(
        body, name=name,
        out_shape=[pltpu.HBM(a.shape, a.dtype) for a in both],
        in_specs=[HBM_SPEC] * (2 * n) + [SEM_SPEC, SEM_SPEC] + [ANY] * m,
        out_specs=[HBM_SPEC] * (2 * n),
        input_output_aliases={w: w for w in range(2 * n)},
        compiler_params=pltpu.CompilerParams(has_side_effects=pltpu.SideEffectType.DATAFLOW_SIDE_EFFECTING),
    )(*both, ssem, rsem, *after)
    return list(res[n:])


def sum_share(parts, kind, R, C, name):
    _, pr, pc = parts.shape
    sr, sc = _shard_shape(kind, R, C)
    tr = _row_tile(pr, pc * 4, budget=4 * 1024 * 1024)
    nt = pr // tr

    def body(p_ref, fin_ref, stage, lsem, ssem, rsem):
        i = pl.program_id(0)
        slot = i % 2
        mx, my, mc = _mesh_pos()

        def region(h, t):
            r0 = pl.multiple_of(t * tr, 8)
            if kind == "col":
                return fin_ref.at[pl.ds(pl.multiple_of(h * pr + r0, 8), tr)]
            return fin_ref.at[pl.ds(r0, tr), pl.ds(h * pc, pc)]

        def copies(s, h, t):
            return (pltpu.make_async_copy(stage.at[s], region(h, t), lsem.at[s]),
                    pltpu.make_async_remote_copy(src_ref=stage.at[s], dst_ref=region(h, t), send_sem=ssem.at[s],
                                                 recv_sem=rsem, device_id=(mx, my, 1 - mc), device_id_type=MESH))

        def wait_sent(s):
            loc, rem = copies(s, 0, 0)
            loc.wait()
            rem.wait_send()

        @pl.when(i >= 2)
        def _():
            wait_sent(slot)

        acc = p_ref[0].astype(F32)
        for k in range(1, 4):
            acc = acc + p_ref[k].astype(F32)
        stage[slot] = acc
        if kind == "col":
            for cp in copies(slot, mc, i):
                cp.start()
        else:
            for h in range(2):
                @pl.when(mc == h)
                def _(h=h):
                    for cp in copies(slot, h, i):
                        cp.start()

        @pl.when(i == nt - 1)
        def _():
            wait_sent(slot)
            if nt >= 2:
                wait_sent(1 - slot)
            half = fin_ref.at[pl.ds(0, pr), pl.ds(0, pc)]
            pltpu.make_async_remote_copy(src_ref=half, dst_ref=half, send_sem=ssem.at[0], recv_sem=rsem,
                                         device_id=(mx, my, 1 - mc), device_id_type=MESH).wait_recv()

    return pl.pallas_call(
        body, name=name, grid=(nt,),
        in_specs=[pl.BlockSpec((4, tr, pc), lambda i: (0, i, 0))],
        out_specs=ANY,
        out_shape=jax.ShapeDtypeStruct((sr, sc), F32),
        scratch_shapes=[pltpu.VMEM((2, tr, pc), F32), pltpu.SemaphoreType.DMA((2,)), pltpu.SemaphoreType.DMA((2,)),
                        pltpu.SemaphoreType.DMA],
        compiler_params=_cp("arbitrary"),
    )(parts)


def _pack(parts, rows):
    flat = []
    for a in parts:
        a = jnp.ravel(a).astype(F32)
        flat.append(jnp.pad(a, (0, (-a.shape[0]) % 128)))
    v = jnp.concatenate(flat)
    return jnp.pad(v, (0, rows * 128 - v.shape[0])).reshape(rows, 128)


def _unpack(block, shapes):
    lead = block.shape[:-2]
    v = block.reshape(lead + (-1,))
    out, off = [], 0
    for shp in shapes:
        n = int(np.prod(shp))
        out.append(v[..., off:off + n].reshape(lead + tuple(shp)))
        off += n + (-n) % 128
    return out


def _block_diag4(w):
    w4 = w.reshape(4, 4, 64, 64)
    eye = jnp.eye(4, dtype=w.dtype)
    return (w4[:, :, :, None, :] * eye[None, :, None, :, None]).reshape(4, 256, 256)


def _diag_blocks(bd):
    b5 = bd.reshape(4, 4, 64, 4, 64)
    return jnp.stack([b5[:, i, :, i, :] for i in range(4)], axis=1).reshape(16, 64, 64)


def _bias_window(rel_bias):
    m = (np.arange(768) + 127) % 768 - 127
    w = rel_bias[:, np.clip(512 - m, -128, 128) + 128]
    win = jnp.tile(w, (1, 128))[:, :128 * 767].reshape(8, 128, 767)[:, :, :WIN]
    qh = np.arange(128)[:, None] // CHUNK
    kc = np.arange(WIN)[None, :] // CHUNK
    valid = (kc >= qh) & (kc <= qh + 8)
    return jnp.where(jnp.asarray(valid)[None], win, NEG)


SMALL = ("b_ada", "norm_pre", "norm_post", "rel_bias", "conv_w", "conv_b", "lru_wa", "lru_ba", "lru_wx",
         "lru_bx", "lru_lambda")
WEIGHTS = ("w_ada", "b_ada", "norm_pre", "norm_post", "ffn1_w_gu", "ffn1_w_down", "w_in", "rel_bias", "conv_w",
           "conv_b", "lru_wa", "lru_ba", "lru_wx", "lru_bx", "lru_lambda", "w_att_o", "w_rec_o", "w_out",
           "ffn2_w_gu", "ffn2_w_down")


def kernel(x, c, w_ada, b_ada, norm_pre, norm_post, ffn1_w_gu, ffn1_w_down, w_in, rel_bias, conv_w, conv_b, lru_wa, lru_ba, lru_wx, lru_bx, lru_lambda, w_att_o, w_rec_o, w_out, ffn2_w_gu, ffn2_w_down, loss_target, m_w_ada, m_b_ada, m_norm_pre, m_norm_post, m_ffn1_w_gu, m_ffn1_w_down, m_w_in, m_rel_bias, m_conv_w, m_conv_b, m_lru_wa, m_lru_ba, m_lru_wx, m_lru_bx, m_lru_lambda, m_w_att_o, m_w_rec_o, m_w_out, m_ffn2_w_gu, m_ffn2_w_down, v_w_ada, v_b_ada, v_norm_pre, v_norm_post, v_ffn1_w_gu, v_ffn1_w_down, v_w_in, v_rel_bias, v_conv_w, v_conv_b, v_lru_wa, v_lru_ba, v_lru_wx, v_lru_bx, v_lru_lambda, v_w_att_o, v_w_rec_o, v_w_out, v_ffn2_w_gu, v_ffn2_w_down):
    W = dict(w_ada=w_ada, b_ada=b_ada, norm_pre=norm_pre, norm_post=norm_post, ffn1_w_gu=ffn1_w_gu,
             ffn1_w_down=ffn1_w_down, w_in=w_in, rel_bias=rel_bias, conv_w=conv_w, conv_b=conv_b, lru_wa=lru_wa,
             lru_ba=lru_ba, lru_wx=lru_wx, lru_bx=lru_bx, lru_lambda=lru_lambda, w_att_o=w_att_o, w_rec_o=w_rec_o,
             w_out=w_out, ffn2_w_gu=ffn2_w_gu, ffn2_w_down=ffn2_w_down)
    M = dict(w_ada=m_w_ada, b_ada=m_b_ada, norm_pre=m_norm_pre, norm_post=m_norm_post, ffn1_w_gu=m_ffn1_w_gu,
             ffn1_w_down=m_ffn1_w_down, w_in=m_w_in, rel_bias=m_rel_bias, conv_w=m_conv_w, conv_b=m_conv_b,
             lru_wa=m_lru_wa, lru_ba=m_lru_ba, lru_wx=m_lru_wx, lru_bx=m_lru_bx, lru_lambda=m_lru_lambda,
             w_att_o=m_w_att_o, w_rec_o=m_w_rec_o, w_out=m_w_out, ffn2_w_gu=m_ffn2_w_gu, ffn2_w_down=m_ffn2_w_down)
    V = dict(w_ada=v_w_ada, b_ada=v_b_ada, norm_pre=v_norm_pre, norm_post=v_norm_post, ffn1_w_gu=v_ffn1_w_gu,
             ffn1_w_down=v_ffn1_w_down, w_in=v_w_in, rel_bias=v_rel_bias, conv_w=v_conv_w, conv_b=v_conv_b,
             lru_wa=v_lru_wa, lru_ba=v_lru_ba, lru_wx=v_lru_wx, lru_bx=v_lru_bx, lru_lambda=v_lru_lambda,
             w_att_o=v_w_att_o, w_rec_o=v_w_rec_o, w_out=v_w_out, ffn2_w_gu=v_ffn2_w_gu, ffn2_w_down=v_ffn2_w_down)
    mx, my, mc = _mesh_pos()
    p = 2 * mx + my
    e = 4 * mx + 2 * my + mc
    xs = x[0]

    c_arr = jnp.reshape(mc, (1,)).astype(jnp.int32)
    cp_arr = jnp.stack([mc, p]).astype(jnp.int32)
    p_arr = jnp.reshape(p, (1,)).astype(jnp.int32)
    direct = ("w_att_o", "w_rec_o", "w_out", "ffn2_w_gu", "ffn2_w_down")
    geoms = [(kind, R, C, n in direct) for (n, kind, R, C) in BIG]
    names = [b[0] for b in BIG]
    placed = [ag_local(W[n][0], kind, R, C, p_arr, "ag_local_" + n) for (n, kind, R, C) in BIG[:2]]

    def arrived(fly, lo, hi, ssem, rsem, after, tag):
        done = ag_wait(fly, geoms[lo:hi], ssem, rsem, after, "ag_wait_" + tag)
        return [a if both else ag_forward(a, kind, R, C, "ag_forward_" + n)
                for a, (kind, R, C, both), n in zip(done, geoms[lo:hi], names[lo:hi])]

    g1 = ag_small(_pack([c, norm_pre, norm_post, conv_w], 32), "ag_small_params")
    c_all, npre4, npost4, cw4 = _unpack(g1, [(D,), (3, 256), (3, 256), (4, 256)])
    chipwise = lambda a: jnp.moveaxis(a[0::2], 0, 1).reshape(a.shape[1], D)
    npre, npost, conv_full = chipwise(npre4), chipwise(npost4), chipwise(cw4)

    b_cols = lax.dynamic_slice(b_ada, (0, p * 2304), (1, 2304))
    mod_cols = ada_fwd(c_all, w_ada[0], b_cols, "ada_fwd")
    g2 = ag_small(mod_cols.reshape(144, 128), "ag_mod")
    mod_all = jnp.moveaxis(g2[0::2].reshape(4, 8, 2304), 0, 1).reshape(8, 9 * D)
    mod = lax.dynamic_index_in_dim(mod_all, e, 0, keepdims=False).reshape(3, 3, D)
    zeros3 = jnp.zeros((3, D), F32)
    vecs = [jnp.concatenate([npre[k:k + 1], npost[k:k + 1], mod[k], zeros3], axis=0) for k in range(3)]

    gu_s, gu_r, gu_fly, tok_gu = ag_start(placed[:1], geoms[:1], [g2], "ag_start_ffn1_gu")
    dn_s, dn_r, dn_fly, tok0 = ag_start(placed[1:2], geoms[1:2], [tok_gu], "ag_start_ffn1_down")
    placed += [ag_local(W[n][0], kind, R, C, p_arr, "ag_local_" + n, after=[tok0]) for (n, kind, R, C) in BIG[2:]]
    f1_gu, = arrived(gu_fly, 0, 1, gu_s, gu_r, placed[2:], "ffn1_gu")
    f1_dn, = arrived(dn_fly, 1, 2, dn_s, dn_r, f1_gu, "ffn1_down")
    mix_s, mix_r, mix_fly, tok1 = ag_start(placed[2:6], geoms[2:6], [f1_gu, f1_dn], "ag_start_mixer")
    ffn_s, ffn_r, ffn_fly, tok2 = ag_start(placed[6:], geoms[6:], [tok1], "ag_start_ffn2")
    wa_bd = _block_diag4(lru_wa[0]).astype(BF16)
    wx_bd = _block_diag4(lru_wx[0]).astype(BF16)
    pvec = jnp.concatenate([conv_full, conv_b, lru_ba, lru_bx, lru_lambda], axis=0)
    bias = _bias_window(rel_bias[0]).reshape(4, 256, WIN)

    f1_u = f1_gu[:, FF:]
    x1, h1, g1_, u1, a1, f1 = ffn_fwd(xs, vecs[0] + tok2[0:1, 0:1], f1_gu, f1_u, f1_dn, 0.5, "ffn1_fwd")
    win, wao, wro, wout = arrived(mix_fly, 2, 6, mix_s, mix_r, x1, "mixer")
    h2, qkv, rest = proj_fwd(x1, vecs[1], win, "proj_fwd")
    ao = attn_fwd(qkv, bias, "attn_fwd")
    hl, hg = lru_fwd(rest, pvec, wa_bd, wx_bd, "lru_fwd")
    x2, att, rec, mg, f2 = mix_out_fwd(x1, ao, hg, rest, vecs[1], wao, wro, wout, "mix_out_fwd")
    f2_gu, f2_dn = arrived(ffn_fly, 6, 8, ffn_s, ffn_r, x2, "ffn2")
    f2_u = f2_gu[:, FF:]
    dy, h3, g3_, u3, a3, f3, lvec = ffn_fwd(x2, vecs[2], f2_gu, f2_u, f2_dn, 0.5, "ffn2_fwd", tgt=loss_target[0])

    G, grads = {}, {}
    geo = {n: (kind, R, C) for (n, kind, R, C) in BIG}

    def reduce_begin(names, tag):
        ps, rb = [], []
        for n in names:
            got = pair_push(G[n], geo[n][0], c_arr, "rs_push_" + n)
            a, b = pair_add(G[n], got, geo[n][0], cp_arr, "rs_pair_sum_" + n)
            ps.append(a)
            rb.append(b)
        return rs_start(ps, rb, [], "rs_start_" + tag)

    def reduce_end(names, flight, after, tag):
        ssem, rsem, ps, rb, _ = flight
        for a, n in zip(rs_wait(ps, rb, ssem, rsem, after, "rs_wait_" + tag), names):
            grads[n] = sum_share(a, *geo[n], "rs_sum_share_" + n)[None]

    dx2, df3, dgu3, va2 = ffn_bwd(dy, x2, f3, g3_, u3, vecs[2], f2_gu, f2_u, f2_dn, 0.5, "ffn2_bwd")
    G["ffn2_w_gu"] = mm_tn(h3, dgu3, "dw_ffn2_gu", D, 1408, 2048)
    G["ffn2_w_down"] = mm_tn(a3, df3, "dw_ffn2_down", 1408, D, 2048)
    fly_ffn2 = reduce_begin(("ffn2_w_gu", "ffn2_w_down"), "ffn2")
    vec1 = vecs[1] + fly_ffn2[4][0:1, 0:1]
    df2, d_att, d_rec, dao, dhl, d3, va_out = mix_out_bwd(dx2, f2, att, rec, rest, hl, vec1, wao, wro, wout,
                                                          "mix_out_bwd")
    G["w_out"] = mm_tn(mg, df2, "dw_out", D, D, 1024)
    G["w_att_o"] = mm_tn(ao, d_att, "dw_att_o", 512, D, 1024)
    G["w_rec_o"] = mm_tn(hg, d_rec, "dw_rec_o", D, D, 1024)
    dq, db, dkv = attn_bwd(qkv, dao, bias, "attn_bwd")
    dxr, v_lru, dwa_bd, dwx_bd = lru_bwd(dhl, hl, rest, pvec, wa_bd, wx_bd, "lru_bwd")
    dx1, va_in = proj_bwd(dq, dkv, dxr, d3, win, x1, dx2, vecs[1], "proj_bwd")
    G["w_in"] = dw_in(h2, dq, dkv, dxr, d3, "dw_in")
    fly_mix = reduce_begin(("w_in", "w_att_o", "w_rec_o", "w_out"), "mixer")
    vec0 = vecs[0] + fly_mix[4][0:1, 0:1]
    dx0, df1, dgu1, va0 = ffn_bwd(dx1, xs, f1, g1_, u1, vec0, f1_gu, f1_u, f1_dn, 0.5, "ffn1_bwd")
    G["ffn1_w_gu"] = mm_tn(h1, dgu1, "dw_ffn1_gu", D, 1408, 2048)
    G["ffn1_w_down"] = mm_tn(a1, df1, "dw_ffn1_down", 1408, D, 2048)
    fly_ffn1 = reduce_begin(("ffn1_w_gu", "ffn1_w_down"), "ffn1")
    reduce_end(("ffn2_w_gu", "ffn2_w_down"), fly_ffn2, [fly_ffn1[4]], "ffn2")
    reduce_end(("w_in", "w_att_o", "w_rec_o", "w_out"), fly_mix, [fly_ffn1[4], grads["ffn2_w_down"]], "mixer")

    va1 = va_out + va_in
    vas = (va0, va1, va2)
    dmod = jnp.stack([v[2:5] for v in vas])
    part = {"b_ada": dmod, "norm_pre": jnp.stack([v[0] for v in vas]), "norm_post": jnp.stack([v[1] for v in vas]),
            "rel_bias": bias_grad(db.reshape(8, 128, WIN), "bias_grad")[:, :257], "conv_w": v_lru[0:4], "conv_b": v_lru[4],
            "lru_wa": _diag_blocks(dwa_bd), "lru_ba": v_lru[5], "lru_wx": _diag_blocks(dwx_bd), "lru_bx": v_lru[6],
            "lru_lambda": v_lru[7]}
    full_shapes = {"b_ada": (9 * D,), "norm_pre": (3, D), "norm_post": (3, D), "rel_bias": (8, 257),
                   "conv_w": (4, D), "conv_b": (D,), "lru_wa": (16, 64, 64), "lru_ba": (D,),
                   "lru_wx": (16, 64, 64), "lru_bx": (D,), "lru_lambda": (D,)}
    g3 = ag_small(_pack([part[n] for n in SMALL] + [lvec[0:1, 0:1]], 1232), "ag_small_grads")
    summed = _unpack(sum_lead(g3, "sum_small_grads"), [full_shapes[n] for n in SMALL] + [(1,)])
    red = dict(zip(SMALL, summed[:-1]))
    loss = summed[-1][0]
    cols = lambda a: lax.dynamic_slice(a, (0, p * 256), (a.shape[0], 256))
    grads.update({"b_ada": red["b_ada"][None], "norm_pre": cols(red["norm_pre"])[None],
                  "norm_post": cols(red["norm_post"])[None], "rel_bias": red["rel_bias"][None],
                  "conv_w": cols(red["conv_w"])[None], "conv_b": red["conv_b"][None], "lru_wa": red["lru_wa"][None],
                  "lru_ba": red["lru_ba"][None], "lru_wx": red["lru_wx"][None], "lru_bx": red["lru_bx"][None],
                  "lru_lambda": red["lru_lambda"][None]})

    dmod_all = g3[:, :72].reshape(8, 9 * D)
    dmod_cols = jnp.pad(lax.dynamic_slice(dmod_all, (0, p * 2304), (8, 2304)), ((0, 120), (0, 0)))
    c_all_t = jnp.pad(c_all.T, ((0, 0), (0, 120)))
    grads["w_ada"] = ada_bwd(c_all_t, dmod_cols, "ada_bwd")[None]

    delta, new_m, new_v = {}, {}, {}

    def update(n):
        shp = W[n].shape
        res = adamw(W[n][0], grads[n][0], M[n][0], V[n][0], "adamw_" + n, emit_g=n in geo)
        delta[n], new_m[n], new_v[n] = [a.reshape(shp) for a in res[:3]]
        if n in geo:
            grads[n] = res[3].reshape(shp)

    for n in ("w_ada", "ffn2_w_gu", "ffn2_w_down", "w_in", "w_att_o", "w_rec_o", "w_out"):
        update(n)
    packed = [_pack([src[n] for n in SMALL], 1168) for src in (W, grads, M, V)]
    outs = adamw(*packed, "adamw_small")
    for dst, blk in zip((delta, new_m, new_v), outs):
        for n, a in zip(SMALL, _unpack(blk, [W[n].shape for n in SMALL])):
            dst[n] = a
    reduce_end(("ffn1_w_gu", "ffn1_w_down"), fly_ffn1,
               [outs[0], delta["w_ada"], delta["ffn2_w_gu"], delta["ffn2_w_down"], delta["w_in"], delta["w_out"]], "ffn1")
    for n in ("ffn1_w_gu", "ffn1_w_down"):
        update(n)

    return (loss, dx0[None], *[grads[n] for n in WEIGHTS], *[delta[n] for n in WEIGHTS],
            *[new_m[n] for n in WEIGHTS], *[new_v[n] for n in WEIGHTS])
```

```python
import functools

import numpy as np
import jax
import jax.numpy as jnp
from jax import lax
from jax.experimental import pallas as pl
from jax.experimental.pallas import tpu as pltpu

F32 = jnp.float32
BF16 = jnp.bfloat16

D = 1024
FF = 2816
PW = 5632
HP = 128
CHUNK = 64
WIN = 640
TQ = 512
EPS = 1e-6
NEG = -1e30
LRU_C = 8.0
N_DEV = 8
VMEM_LIMIT = 56 * 1024 * 1024

ADAM_LR, ADAM_B1, ADAM_B2, ADAM_EPS, ADAM_WD, ADAM_STEP = 0.001, 0.9, 0.999, 1e-08, 0.01, 10

MESH = pl.DeviceIdType.MESH
ANY = pl.BlockSpec(memory_space=pl.ANY)


def _cp(*sem):
    return pltpu.CompilerParams(dimension_semantics=tuple(sem), vmem_limit_bytes=VMEM_LIMIT)


def _dot(a, b):
    return jnp.dot(a, b, preferred_element_type=F32)


def _dot_nt(a, b):
    return lax.dot_general(a, b, (((1,), (1,)), ((), ())), preferred_element_type=F32)


def _dot_tn(a, b):
    return lax.dot_general(a, b, (((0,), (0,)), ((), ())), preferred_element_type=F32)


def _mean(v):
    return jnp.mean(v, axis=-1, keepdims=True)


def _colsum(v):
    return jnp.sum(v, axis=0, keepdims=True)


def _sigmoid(v):
    return 0.5 * jnp.tanh(0.5 * v) + 0.5


_GK = 0.7978845608028654


def _gelu(v):
    t = jnp.tanh(_GK * (v + 0.044715 * v * v * v))
    return 0.5 * v * (1.0 + t)


def _pre_norm(xv, vec_ref):
    r = lax.rsqrt(_mean(xv * xv) + EPS)
    n = xv * r * vec_ref[0:1, :]
    return n * (1.0 + vec_ref[3:4, :]) + vec_ref[2:3, :]


def _pre_norm_bwd(dh, xv, dres, vec_ref, vacc_ref):
    r = lax.rsqrt(_mean(xv * xv) + EPS)
    xh = xv * r
    n = xh * vec_ref[0:1, :]
    vacc_ref[2:3, :] += _colsum(dh)
    vacc_ref[3:4, :] += _colsum(dh * n)
    dn = dh * (1.0 + vec_ref[3:4, :])
    vacc_ref[0:1, :] += _colsum(dn * xh)
    dxh = dn * vec_ref[0:1, :]
    return r * (dxh - xh * _mean(dxh * xh)) + dres


def _post_norm_bwd(dxo, fv, res, vec_ref, vacc_ref):
    rf = lax.rsqrt(_mean(fv * fv) + EPS)
    fh = fv * rf
    gp = vec_ref[1:2, :]
    vacc_ref[4:5, :] += _colsum(res * dxo * (fh * gp))
    dy = (res * vec_ref[4:5, :]) * dxo
    vacc_ref[1:2, :] += _colsum(dy * fh)
    dfn = dy * gp
    return rf * (dfn - fh * _mean(dfn * fh))


def _u_spec(tf):
    return pl.BlockSpec((pl.Element(D), pl.Element(tf)),
                        lambda i, j: (0, pl.multiple_of(jnp.minimum(FF + j * tf, 2 * FF - tf), 128)))


def ffn_fwd(x, vec, w_gu, w_dn, res, name, tgt=None, tm=1024, tf=512):
    S = x.shape[0]
    tm = min(tm, S)
    nt = S // tm
    nf = -(-FF // tf)
    tail = FF - tf * (nf - 1)
    halves = [pl.ds(r * (tm // 2), tm // 2) for r in range(2)]
    head = tgt is not None

    def body(*refs):
        x_ref, vec_ref, wg_ref, wu_ref, wd_ref = refs[:5]
        if head:
            t_hbm, xo_ref, h_ref, g_ref, u_ref, a_ref, f_ref, l_ref, hs, acc, lacc, ts, tsem = refs[5:]
        else:
            xo_ref, h_ref, g_ref, u_ref, a_ref, f_ref, hs, acc = refs[5:]
        i, j = pl.program_id(0), pl.program_id(1)
        if head:
            late = pltpu.make_async_copy(t_hbm.at[pl.ds(pl.multiple_of(i * tm, 8), tm)], ts, tsem)

        @pl.when(j == 0)
        def _():
            if head:
                late.start()
            h = _pre_norm(x_ref[...], vec_ref).astype(BF16)
            hs[...] = h
            h_ref[...] = h
            acc[...] = jnp.zeros_like(acc)

        def chunk(w):
            gu = [(_dot(hs[r, :], wg_ref[:, 0:w]), _dot(hs[r, :], wu_ref[:, tf - w:tf])) for r in halves]
            acts = []
            for r, (g, u) in zip(halves, gu):
                g_ref[r, 0:w] = g.astype(BF16)
                u_ref[r, 0:w] = u.astype(BF16)
                a = (g * _sigmoid(g) * u).astype(BF16)
                a_ref[r, 0:w] = a
                acts.append(a)
            for r, a in zip(halves, acts):
                acc[r, :] += _dot(a, wd_ref[0:w, :])

        @pl.when(j < nf - 1)
        def _():
            chunk(tf)

        @pl.when(j == nf - 1)
        def _():
            chunk(tail)
            f = acc[...]
            f_ref[...] = f.astype(BF16)
            y = f * lax.rsqrt(_mean(f * f) + EPS) * vec_ref[1:2, :]
            xo = x_ref[...] + (res * vec_ref[4:5, :]) * y
            if head:
                @pl.when(i == 0)
                def _():
                    lacc[...] = jnp.zeros_like(lacc)

                late.wait()
                d = xo - ts[...]
                xo_ref[...] = d * (1.0 / D)
                lacc[...] += _colsum(d * d)

                @pl.when(i == nt - 1)
                def _():
                    l_ref[...] = jnp.broadcast_to(0.5 * jnp.sum(lacc[...]) * (1.0 / D), (8, 128))
            else:
                xo_ref[...] = xo

    row = lambda i, j: (i, 0)
    col = lambda i, j: (i, j)
    in_specs = [pl.BlockSpec((tm, D), row), pl.BlockSpec((8, D), lambda i, j: (0, 0)),
                pl.BlockSpec((D, tf), lambda i, j: (0, j)), _u_spec(tf),
                pl.BlockSpec((tf, D), lambda i, j: (j, 0))]
    out_specs = [pl.BlockSpec((tm, D), row), pl.BlockSpec((tm, D), row), pl.BlockSpec((tm, tf), col),
                 pl.BlockSpec((tm, tf), col), pl.BlockSpec((tm, tf), col), pl.BlockSpec((tm, D), row)]
    out_shape = [jax.ShapeDtypeStruct((S, D), F32), jax.ShapeDtypeStruct((S, D), BF16),
                 jax.ShapeDtypeStruct((S, FF), BF16), jax.ShapeDtypeStruct((S, FF), BF16),
                 jax.ShapeDtypeStruct((S, FF), BF16), jax.ShapeDtypeStruct((S, D), BF16)]
    scratch = [pltpu.VMEM((tm, D), BF16), pltpu.VMEM((tm, D), F32)]
    args = [x, vec, w_gu, w_gu, w_dn]
    if head:
        in_specs.append(ANY)
        out_specs.append(pl.BlockSpec((8, 128), lambda i, j: (0, 0)))
        out_shape.append(jax.ShapeDtypeStruct((8, 128), F32))
        scratch += [pltpu.VMEM((1, D), F32), pltpu.VMEM((tm, D), F32), pltpu.SemaphoreType.DMA]
        args.append(tgt)
    return pl.pallas_call(
        body, name=name, grid=(nt, nf), in_specs=in_specs, out_specs=out_specs, out_shape=out_shape,
        scratch_shapes=scratch,
        compiler_params=_cp("arbitrary" if head else "parallel", "arbitrary"),
    )(*args)


def ffn_bwd(dxo, x, f, g, u, vec, w_gu, w_dn, res, name, tm=1024, tf=512):
    S = x.shape[0]
    tm = min(tm, S)
    nf = -(-FF // tf)
    tail = FF - tf * (nf - 1)
    halves = [pl.ds(r * (tm // 2), tm // 2) for r in range(2)]

    def body(dxo_ref, x_hbm, f_ref, g_ref, u_ref, vec_ref, wg_ref, wu_ref, wd_ref,
             dx_ref, df_ref, dgu_ref, vacc_ref, dfs, acc, xs, xsem):
        i, j = pl.program_id(0), pl.program_id(1)
        late = pltpu.make_async_copy(x_hbm.at[pl.ds(pl.multiple_of(i * tm, 8), tm)], xs, xsem)

        @pl.when((i == 0) & (j == 0))
        def _():
            vacc_ref[...] = jnp.zeros_like(vacc_ref)

        @pl.when(j == 0)
        def _():
            late.start()
            df = _post_norm_bwd(dxo_ref[...], f_ref[...].astype(F32), res, vec_ref, vacc_ref).astype(BF16)
            dfs[...] = df
            df_ref[...] = df
            acc[...] = jnp.zeros_like(acc)

        def chunk(w):
            da = [_dot_nt(dfs[h, :], wd_ref[0:w, :]) for h in halves]
            dgu = []
            for h, d in zip(halves, da):
                gv, uv = g_ref[h, 0:w].astype(F32), u_ref[h, 0:w].astype(F32)
                sg = _sigmoid(gv)
                dg = (d * uv * (sg * (1.0 + gv * (1.0 - sg)))).astype(BF16)
                du = (d * (gv * sg)).astype(BF16)
                dgu_ref[0, h, 0:w] = dg
                dgu_ref[1, h, 0:w] = du
                dgu.append((dg, du))
            for h, (dg, du) in zip(halves, dgu):
                acc[h, :] += _dot_nt(dg, wg_ref[:, 0:w]) + _dot_nt(du, wu_ref[:, tf - w:tf])

        @pl.when(j < nf - 1)
        def _():
            chunk(tf)

        @pl.when(j == nf - 1)
        def _():
            chunk(tail)
            late.wait()
            dx_ref[...] = _pre_norm_bwd(acc[...], xs[...], dxo_ref[...], vec_ref, vacc_ref)

    row = lambda i, j: (i, 0)
    col = lambda i, j: (i, j)
    return pl.pallas_call(
        body, name=name, grid=(S // tm, nf),
        in_specs=[pl.BlockSpec((tm, D), row), ANY, pl.BlockSpec((tm, D), row),
                  pl.BlockSpec((tm, tf), col), pl.BlockSpec((tm, tf), col),
                  pl.BlockSpec((8, D), lambda i, j: (0, 0)),
                  pl.BlockSpec((D, tf), lambda i, j: (0, j)), _u_spec(tf),
                  pl.BlockSpec((tf, D), lambda i, j: (j, 0))],
        out_specs=[pl.BlockSpec((tm, D), row), pl.BlockSpec((tm, D), row),
                   pl.BlockSpec((2, tm, tf), lambda i, j: (0, i, j)),
                   pl.BlockSpec((8, D), lambda i, j: (0, 0))],
        out_shape=[jax.ShapeDtypeStruct((S, D), F32), jax.ShapeDtypeStruct((S, D), BF16),
                   jax.ShapeDtypeStruct((2, S, FF), BF16), jax.ShapeDtypeStruct((8, D), F32)],
        scratch_shapes=[pltpu.VMEM((tm, D), BF16), pltpu.VMEM((tm, D), F32), pltpu.VMEM((tm, D), F32),
                        pltpu.SemaphoreType.DMA],
        compiler_params=_cp("arbitrary", "arbitrary"),
    )(dxo, x, f, g, u, vec, w_gu, w_gu, w_dn)


def mm_tn(a, b, name, tm, tn, tk, out_dtype=BF16):
    S, M = a.shape
    if b.ndim == 3:
        G, _, Nf = b.shape
    else:
        G, Nf = 1, b.shape[1]
    N = G * Nf
    tk = min(tk, S)
    nbf = Nf // tn
    nk = S // tk

    def body(a_ref, b_ref, o_ref, acc):
        k = pl.program_id(2)

        @pl.when(k == 0)
        def _():
            acc[...] = jnp.zeros_like(acc)

        acc[...] += _dot_tn(a_ref[...], b_ref[...])

        @pl.when(k == nk - 1)
        def _():
            o_ref[...] = acc[...].astype(out_dtype)

    if b.ndim == 3:
        b_spec = pl.BlockSpec((None, tk, tn), lambda i, j, k: (j // nbf, k, j % nbf))
    else:
        b_spec = pl.BlockSpec((tk, tn), lambda i, j, k: (k, j))
    return pl.pallas_call(
        body, name=name, grid=(M // tm, N // tn, nk),
        in_specs=[pl.BlockSpec((tk, tm), lambda i, j, k: (k, i)), b_spec],
        out_specs=pl.BlockSpec((tm, tn), lambda i, j, k: (i, j)),
        out_shape=jax.ShapeDtypeStruct((M, N), out_dtype),
        scratch_shapes=[pltpu.VMEM((tm, tn), F32)],
        compiler_params=_cp("parallel", "parallel", "arbitrary"),
    )(a, b)


def proj_fwd(x, vec, w_in, name, tm=2048, tn=512):
    S = x.shape[0]
    tm = min(tm, S)
    nq = 1536 // tn

    def body(x_ref, vec_ref, w_ref, h_ref, qkv_ref, rest_ref, hs):
        j = pl.program_id(1)

        @pl.when(j == 0)
        def _():
            h = _pre_norm(x_ref[...], vec_ref).astype(BF16)
            hs[...] = h
            h_ref[...] = h

        r = _dot(hs[...], w_ref[...])

        @pl.when(j < nq)
        def _():
            qkv_ref[...] = r.astype(BF16)

        @pl.when(j >= nq)
        def _():
            rest_ref[...] = r.astype(BF16)

    row = lambda i, j: (i, 0)
    return pl.pallas_call(
        body, name=name, grid=(S // tm, PW // tn),
        in_specs=[pl.BlockSpec((tm, D), row), pl.BlockSpec((8, D), lambda i, j: (0, 0)),
                  pl.BlockSpec((D, tn), lambda i, j: (0, j))],
        out_specs=[pl.BlockSpec((tm, D), row),
                   pl.BlockSpec((tm, tn), lambda i, j: (i, jnp.minimum(j, nq - 1))),
                   pl.BlockSpec((tm, tn), lambda i, j: (i, jnp.maximum(j - nq, 0)))],
        out_shape=[jax.ShapeDtypeStruct((S, D), BF16), jax.ShapeDtypeStruct((S, 1536), BF16),
                   jax.ShapeDtypeStruct((S, 4096), BF16)],
        scratch_shapes=[pltpu.VMEM((tm, D), BF16)],
        compiler_params=_cp("parallel", "arbitrary"),
    )(x, vec, w_in)


def proj_bwd(dq, dkv, dxr, d3, w_in, x, dxo, vec, name, tm=2048, tk=512):
    S = x.shape[0]
    tm = min(tm, S)
    nk = PW // tk

    def body(dq_ref, dkv_ref, dxr_ref, d3_ref, w_ref, x_hbm, dxo_hbm, vec_ref, dx_ref, vacc_ref, acc, xs, dxos, sems):
        i, j = pl.program_id(0), pl.program_id(1)
        tok = pl.ds(pl.multiple_of(i * tm, 8), tm)
        late = (pltpu.make_async_copy(x_hbm.at[tok], xs, sems.at[0]),
                pltpu.make_async_copy(dxo_hbm.at[tok], dxos, sems.at[1]))

        @pl.when((i == 0) & (j == 0))
        def _():
            vacc_ref[...] = jnp.zeros_like(vacc_ref)

        @pl.when(j == 0)
        def _():
            for cp in late:
                cp.start()
            acc[...] = _dot_nt(dq_ref[...], w_ref[...])

        @pl.when((j >= 1) & (j < 3))
        def _():
            acc[...] += _dot_nt(dkv_ref[...], w_ref[...])

        @pl.when((j >= 3) & (j < 5))
        def _():
            acc[...] += _dot_nt(dxr_ref[...], w_ref[...])

        @pl.when(j >= 5)
        def _():
            acc[...] += _dot_nt(d3_ref[...], w_ref[...])

        @pl.when(j == nk - 1)
        def _():
            for cp in late:
                cp.wait()
            dx_ref[...] = _pre_norm_bwd(acc[...], xs[...], dxos[...], vec_ref, vacc_ref)

    row = lambda i, j: (i, 0)
    return pl.pallas_call(
        body, name=name, grid=(S // tm, nk),
        in_specs=[pl.BlockSpec((None, tm, tk), lambda i, j: (0, i, 0)),
                  pl.BlockSpec((None, tm, tk), lambda i, j: (jnp.clip(j - 1, 0, 1), i, 0)),
                  pl.BlockSpec((tm, tk), lambda i, j: (i, jnp.clip(j - 3, 0, 1))),
                  pl.BlockSpec((None, tm, tk), lambda i, j: (jnp.clip(j - 5, 0, 5) // 2, i, jnp.clip(j - 5, 0, 5) % 2)),
                  pl.BlockSpec((D, tk), lambda i, j: (0, j)),
                  ANY, ANY,
                  pl.BlockSpec((8, D), lambda i, j: (0, 0))],
        out_specs=[pl.BlockSpec((tm, D), row, pipeline_mode=pl.Buffered(1)),
                   pl.BlockSpec((8, D), lambda i, j: (0, 0))],
        out_shape=[jax.ShapeDtypeStruct((S, D), F32), jax.ShapeDtypeStruct((8, D), F32)],
        scratch_shapes=[pltpu.VMEM((tm, D), F32), pltpu.VMEM((tm, D), F32), pltpu.VMEM((tm, D), F32),
                        pltpu.SemaphoreType.DMA((2,))],
        compiler_params=_cp("arbitrary", "arbitrary"),
    )(dq, dkv, dxr, d3, w_in, x, dxo, vec)


def _two_heads(v, lane):
    zero = jnp.zeros((), v.dtype)
    return jnp.concatenate([jnp.where(lane < 64, v, zero), jnp.where(lane >= 64, v, zero)], axis=0)


def _attn_scores(qm, ka, bias_h, i, grp):
    s = _dot_nt(qm, ka) + bias_h
    col = lax.broadcasted_iota(jnp.int32, s.shape, 1)
    first_key = jnp.where(i == 0, 512 - 128 * grp, 0)
    return jnp.where(col >= first_key, s, NEG)


def _softmax(s):
    e = jnp.exp(s - jnp.max(s, axis=-1, keepdims=True))
    return e * (1.0 / jnp.sum(e, axis=-1, keepdims=True))


NG = TQ // 128


def attn_fwd(qkv, bias, name):
    S = qkv.shape[0]
    nb = S // TQ

    def body(q_ref, kp_ref, kc_ref, vp_ref, vc_ref, b_ref, o_ref, kw, vw):
        i = pl.program_id(1)
        kw[0:TQ, :] = kp_ref[...]
        kw[TQ:2 * TQ, :] = kc_ref[...]
        vw[0:TQ, :] = vp_ref[...]
        vw[TQ:2 * TQ, :] = vc_ref[...]
        lane = lax.broadcasted_iota(jnp.int32, (1, HP), 1)

        rows = [pl.ds(128 * a, 128) for a in range(NG)]
        keys = [pl.ds(128 * a, WIN) for a in range(NG)]
        q2 = [_two_heads(q_ref[r, :] * jnp.asarray(0.125, BF16), lane) for r in rows]
        s = [_attn_scores(q2[a], kw[keys[a], :], b_ref[...], i, a) for a in range(NG)]
        p = [_softmax(sa).astype(BF16) for sa in s]
        o2 = [_dot(p[a], vw[keys[a], :]) for a in range(NG)]
        for a in range(NG):
            o_ref[rows[a], :] = jnp.where(lane < 64, o2[a][0:128], o2[a][128:256]).astype(BF16)

    prev = lambda h, i: (jnp.maximum(i - 1, 0), 0)
    return pl.pallas_call(
        body, name=name, grid=(4, nb),
        in_specs=[pl.BlockSpec((TQ, HP), lambda h, i: (i, h)),
                  pl.BlockSpec((TQ, HP), lambda h, i: (jnp.maximum(i - 1, 0), 4 + h)),
                  pl.BlockSpec((TQ, HP), lambda h, i: (i, 4 + h)),
                  pl.BlockSpec((TQ, HP), lambda h, i: (jnp.maximum(i - 1, 0), 8 + h)),
                  pl.BlockSpec((TQ, HP), lambda h, i: (i, 8 + h)),
                  pl.BlockSpec((None, 256, WIN), lambda h, i: (h, 0, 0))],
        out_specs=pl.BlockSpec((TQ, HP), lambda h, i: (i, h)),
        out_shape=jax.ShapeDtypeStruct((S, 512), BF16),
        scratch_shapes=[pltpu.VMEM((2 * TQ, HP), BF16), pltpu.VMEM((2 * TQ, HP), BF16)],
        compiler_params=_cp("parallel", "arbitrary"),
    )(qkv, qkv, qkv, qkv, qkv, bias)


def attn_bwd(qkv, do, bias, name):
    S = qkv.shape[0]
    nb = S // TQ

    def body(q_ref, kp_ref, kc_ref, vp_ref, vc_ref, do_ref, b_ref, dqkv_ref, db_ref, dkv_ref, kw, vw, ak, av):
        i = pl.program_id(1)

        @pl.when(i == 0)
        def _():
            db_ref[...] = jnp.zeros_like(db_ref)
            ak[...] = jnp.zeros_like(ak)
            av[...] = jnp.zeros_like(av)

        @pl.when(i > 0)
        def _():
            ak[0:TQ, :] = ak[TQ:2 * TQ, :]
            av[0:TQ, :] = av[TQ:2 * TQ, :]
            ak[TQ:2 * TQ, :] = jnp.zeros((TQ, HP), F32)
            av[TQ:2 * TQ, :] = jnp.zeros((TQ, HP), F32)

        @pl.when(i < nb)
        def _():
            kw[0:TQ, :] = kp_ref[...]
            kw[TQ:2 * TQ, :] = kc_ref[...]
            vw[0:TQ, :] = vp_ref[...]
            vw[TQ:2 * TQ, :] = vc_ref[...]
            lane = lax.broadcasted_iota(jnp.int32, (1, HP), 1)

            rows = [pl.ds(128 * a, 128) for a in range(NG)]
            keys = [pl.ds(128 * a, WIN) for a in range(NG)]
            q2 = [_two_heads(q_ref[r, :] * jnp.asarray(0.125, BF16), lane) for r in rows]
            do2 = [_two_heads(do_ref[r, :], lane) for r in rows]
            s = [_attn_scores(q2[a], kw[keys[a], :], b_ref[...], i, a) for a in range(NG)]
            dp = [_dot_nt(do2[a], vw[keys[a], :]) for a in range(NG)]
            p = [_softmax(sa) for sa in s]
            ds = [p[a] * (dp[a] - jnp.sum(p[a] * dp[a], axis=-1, keepdims=True)) for a in range(NG)]
            db_ref[...] += (ds[0] + ds[1]) + (ds[2] + ds[3])
            dsb = [d.astype(BF16) for d in ds]
            dq2 = [_dot(dsb[a], kw[keys[a], :]) for a in range(NG)]
            dk = [_dot_tn(dsb[a], q2[a]) for a in range(NG)]
            dv = [_dot_tn(p[a].astype(BF16), do2[a]) for a in range(NG)]
            for a in range(NG):
                ak[keys[a], :] += dk[a]
                av[keys[a], :] += dv[a]
                dq = jnp.where(lane < 64, dq2[a][0:128], dq2[a][128:256])
                dqkv_ref[0, rows[a], :] = (dq * 0.125).astype(BF16)

        @pl.when(i > 0)
        def _():
            dkv_ref[0] = ak[0:TQ, :].astype(BF16)
            dkv_ref[1] = av[0:TQ, :].astype(BF16)

    cur = lambda i: jnp.minimum(i, nb - 1)
    prv = lambda i: jnp.clip(i - 1, 0, nb - 1)
    dq, db, dkv = pl.pallas_call(
        body, name=name, grid=(4, nb + 1),
        in_specs=[pl.BlockSpec((TQ, HP), lambda h, i: (cur(i), h)),
                  pl.BlockSpec((TQ, HP), lambda h, i: (prv(i), 4 + h)),
                  pl.BlockSpec((TQ, HP), lambda h, i: (cur(i), 4 + h)),
                  pl.BlockSpec((TQ, HP), lambda h, i: (prv(i), 8 + h)),
                  pl.BlockSpec((TQ, HP), lambda h, i: (cur(i), 8 + h)),
                  pl.BlockSpec((TQ, HP), lambda h, i: (cur(i), h)),
                  pl.BlockSpec((None, 256, WIN), lambda h, i: (h, 0, 0))],
        out_specs=[pl.BlockSpec((1, TQ, HP), lambda h, i: (0, cur(i), h)),
                   pl.BlockSpec((None, 256, WIN), lambda h, i: (h, 0, 0)),
                   pl.BlockSpec((2, TQ, HP), lambda h, i: (0, prv(i), h))],
        out_shape=[jax.ShapeDtypeStruct((1, S, 512), BF16), jax.ShapeDtypeStruct((4, 256, WIN), F32),
                   jax.ShapeDtypeStruct((2, S, 512), BF16)],
        scratch_shapes=[pltpu.VMEM((2 * TQ, HP), BF16), pltpu.VMEM((2 * TQ, HP), BF16),
                        pltpu.VMEM((2 * TQ, HP), F32), pltpu.VMEM((2 * TQ, HP), F32)],
        compiler_params=_cp("parallel", "arbitrary"),
    )(qkv, qkv, qkv, qkv, qkv, do, bias)
    return dq, db, dkv


def bias_grad(db, name):
    def body(db_ref, o_ref):
        r = lax.broadcasted_iota(jnp.int32, (128, 128), 0)
        c = lax.broadcasted_iota(jnp.int32, (128, 128), 1)
        flip = (r + c == 127).astype(BF16)
        lane = lax.broadcasted_iota(jnp.int32, (16, 384), 1)
        src = lax.broadcasted_iota(jnp.int32, (128, 384), 0)
        dst = lax.broadcasted_iota(jnp.int32, (128, 384), 1)

        def split_dot(v, m):
            hi = v.astype(BF16)
            r1 = v - hi.astype(F32)
            mid = r1.astype(BF16)
            lo = (r1 - mid.astype(F32)).astype(BF16)
            return _dot(hi, m) + _dot(mid, m) + _dot(lo, m)

        def diag_sums(w):
            y = pltpu.roll(split_dot(w, flip), 0, 1, stride=1, stride_axis=0)
            return jnp.broadcast_to(_colsum(y), (16, 128))

        w4 = db_ref[0, :, 512:640]
        w3 = db_ref[0, :, 384:512]
        far = jnp.sum(db_ref[0, :, 0:384]) + jnp.sum(jnp.where(r >= c, w3, 0.0))
        lo4 = diag_sums(jnp.where(r >= c, w4, 0.0))
        up4 = diag_sums(jnp.where(r < c, w4, 0.0))
        up3 = diag_sums(jnp.where(r < c, w3, 0.0))
        p_lo4 = (dst == 128 + (src + 1) % 128).astype(BF16)
        p_up4 = ((dst == src + 1) & (src < 127)).astype(BF16)
        p_up3 = ((dst == src + 129) & (src < 127)).astype(BF16)
        out = split_dot(lo4, p_lo4) + split_dot(up4, p_up4) + split_dot(up3, p_up3)
        o_ref[0] = out + jnp.where(lane == 256, far, 0.0)

    return pl.pallas_call(
        body, name=name, grid=(8,),
        in_specs=[pl.BlockSpec((1, 128, WIN), lambda h: (h, 0, 0))],
        out_specs=pl.BlockSpec((1, 16, 384), lambda h: (h, 0, 0)),
        out_shape=jax.ShapeDtypeStruct((8, 16, 384), F32),
        compiler_params=_cp("parallel"),
    )(db)[:, 0, :]


LT = 1024
LC = 512


def _lru_gates(xs, pv_ref, wa_ref, wx_ref, tl):
    xc = (pv_ref[4:5, :] + pv_ref[3:4, :] * xs[pl.ds(8, tl), :] + pv_ref[2:3, :] * xs[pl.ds(7, tl), :]
          + pv_ref[1:2, :] * xs[pl.ds(6, tl), :] + pv_ref[0:1, :] * xs[pl.ds(5, tl), :])
    xcb = xc.astype(BF16)
    pa = jnp.concatenate([_dot(xcb[:, 0:256], wa_ref[0]), _dot(xcb[:, 256:512], wa_ref[1])], axis=1)
    px = jnp.concatenate([_dot(xcb[:, 0:256], wx_ref[0]), _dot(xcb[:, 256:512], wx_ref[1])], axis=1)
    r = _sigmoid(pa + pv_ref[5:6, :])
    ig = _sigmoid(px + pv_ref[6:7, :])
    z = -pv_ref[7:8, :]
    sp = jnp.maximum(z, 0.0) + jnp.log1p(jnp.exp(-jnp.abs(z)))
    log_a = (-LRU_C * r) * sp
    a = jnp.exp(log_a)
    s = jnp.tanh(-log_a) * (1.0 + a * a)
    inv_mult = lax.rsqrt(s)
    mult = jnp.where(s > 0.0, s * inv_mult, 0.0)
    return xc, xcb, r, ig, sp, a, mult, inv_mult


def lru_fwd(rest, pvec, wa, wx, name):
    S = rest.shape[0]
    tl = min(LT, S)
    nt = S // tl

    def body(xr_ref, halo_ref, yr_ref, pv_ref, wa_ref, wx_ref, h_ref, hg_ref, xs, a_s, u_s, h_s, carry):
        ti = pl.program_id(1)

        @pl.when(ti == 0)
        def _():
            carry[...] = jnp.zeros_like(carry)

        xs[0:8, :] = jnp.where(ti > 0, halo_ref[8:16, :].astype(F32), 0.0)
        xs[pl.ds(8, tl), :] = xr_ref[...].astype(F32)
        xc, _, _, ig, _, a, mult, _ = _lru_gates(xs, pv_ref, wa_ref, wx_ref, tl)
        a_s[...] = a
        u_s[...] = mult * (ig * xc)
        row = lax.broadcasted_iota(jnp.int32, (8, LC), 0)

        def blk(bi, c):
            o = pl.multiple_of(bi * 8, 8)
            av = a_s[pl.ds(o, 8), :]
            bv = u_s[pl.ds(o, 8), :]
            for d in (1, 2, 4):
                a_sh = pltpu.roll(av, d, 0)
                b_sh = pltpu.roll(bv, d, 0)
                m = row >= d
                bv = jnp.where(m, av * b_sh + bv, bv)
                av = jnp.where(m, av * a_sh, av)
            hv = bv + av * c
            h_s[pl.ds(o, 8), :] = hv
            return hv[7:8, :]

        carry[...] = lax.fori_loop(0, tl // 8, blk, carry[...])
        h = h_s[...]
        h_ref[...] = h
        hg_ref[...] = (h * _gelu(yr_ref[...].astype(F32))).astype(BF16)

    hb = tl // 16
    return pl.pallas_call(
        body, name=name, grid=(2, nt),
        in_specs=[pl.BlockSpec((tl, LC), lambda c, t: (t, c)),
                  pl.BlockSpec((16, LC), lambda c, t: (jnp.maximum(t * hb - 1, 0), c)),
                  pl.BlockSpec((tl, LC), lambda c, t: (t, 2 + c)),
                  pl.BlockSpec((8, LC), lambda c, t: (0, c)),
                  pl.BlockSpec((2, 256, 256), lambda c, t: (c, 0, 0)),
                  pl.BlockSpec((2, 256, 256), lambda c, t: (c, 0, 0))],
        out_specs=[pl.BlockSpec((tl, LC), lambda c, t: (t, c)), pl.BlockSpec((tl, LC), lambda c, t: (t, c))],
        out_shape=[jax.ShapeDtypeStruct((S, D), F32), jax.ShapeDtypeStruct((S, D), BF16)],
        scratch_shapes=[pltpu.VMEM((tl + 8, LC), F32), pltpu.VMEM((tl, LC), F32), pltpu.VMEM((tl, LC), F32),
                        pltpu.VMEM((tl, LC), F32), pltpu.VMEM((1, LC), F32)],
        compiler_params=_cp("parallel", "arbitrary"),
    )(rest, rest, rest, pvec, wa, wx)


def lru_bwd(dh, h, rest, pvec, wa, wx, name):
    S = rest.shape[0]
    tl = min(LT, S)
    nt = S // tl

    def body(dh_ref, h_ref, hhalo_ref, xr_ref, xhalo_ref, pv_ref, wa_ref, wx_ref,
             dxr_ref, vacc_ref, dwa_ref, dwx_ref,
             xs, hs, a_s, ash_s, b_s, lam_s, dxe, anext, lnext, dxnext):
        ti = pl.program_id(1)
        tr = nt - 1 - ti

        @pl.when(ti == 0)
        def _():
            anext[...] = jnp.zeros_like(anext)
            lnext[...] = jnp.zeros_like(lnext)
            dxnext[...] = jnp.zeros_like(dxnext)
            vacc_ref[...] = jnp.zeros_like(vacc_ref)
            dwa_ref[...] = jnp.zeros_like(dwa_ref)
            dwx_ref[...] = jnp.zeros_like(dwx_ref)

        xs[0:8, :] = jnp.where(tr > 0, xhalo_ref[8:16, :].astype(F32), 0.0)
        xs[pl.ds(8, tl), :] = xr_ref[...].astype(F32)
        xc, xcb, r, ig, sp, a, mult, inv_mult = _lru_gates(xs, pv_ref, wa_ref, wx_ref, tl)

        a_s[pl.ds(0, tl), :] = a
        a_s[pl.ds(tl, 8), :] = jnp.broadcast_to(anext[...], (8, LC))
        ash_s[...] = a_s[pl.ds(1, tl), :]
        b_s[...] = dh_ref[...]
        row = lax.broadcasted_iota(jnp.int32, (8, LC), 0)

        def blk(k, c):
            o = pl.multiple_of((tl // 8 - 1 - k) * 8, 8)
            av = ash_s[pl.ds(o, 8), :]
            bv = b_s[pl.ds(o, 8), :]
            for d in (1, 2, 4):
                a_sh = pltpu.roll(av, 8 - d, 0)
                b_sh = pltpu.roll(bv, 8 - d, 0)
                m = row < 8 - d
                bv = jnp.where(m, bv + av * b_sh, bv)
                av = jnp.where(m, av * a_sh, av)
            lv = bv + av * c
            lam_s[pl.ds(o, 8), :] = lv
            return lv[0:1, :]

        lnext[...] = lax.fori_loop(0, tl // 8, blk, lnext[...])
        anext[...] = a[0:1, :]
        lam = lam_s[...]

        hs[0:8, :] = jnp.where(tr > 0, hhalo_ref[...], 0.0)
        hs[pl.ds(8, tl), :] = h_ref[...]
        d_a = lam * hs[pl.ds(7, tl), :]
        d_mult = lam * (ig * xc)
        d_ig = lam * mult * xc
        dxc = lam * mult * ig
        d_log_a = d_a * a - d_mult * (a * a) * inv_mult
        d_r = d_log_a * (-LRU_C * sp)
        vacc_ref[7:8, :] += _colsum(d_log_a * (-LRU_C * r)) * (-_sigmoid(-pv_ref[7:8, :]))
        d_pa = d_r * r * (1.0 - r)
        d_px = d_ig * ig * (1.0 - ig)
        vacc_ref[5:6, :] += _colsum(d_pa)
        vacc_ref[6:7, :] += _colsum(d_px)
        dpa = d_pa.astype(BF16)
        dpx = d_px.astype(BF16)
        back = []
        for g in range(2):
            sl = slice(256 * g, 256 * g + 256)
            dwa_ref[g] += _dot_tn(xcb[:, sl], dpa[:, sl])
            dwx_ref[g] += _dot_tn(xcb[:, sl], dpx[:, sl])
            back.append(_dot_nt(dpa[:, sl], wa_ref[g]) + _dot_nt(dpx[:, sl], wx_ref[g]))
        dxc = dxc + jnp.concatenate(back, axis=1)
        vacc_ref[4:5, :] += _colsum(dxc)
        for k in range(4):
            vacc_ref[k:k + 1, :] += _colsum(dxc * xs[pl.ds(5 + k, tl), :])
        dxe[pl.ds(0, tl), :] = dxc
        dxe[pl.ds(tl, 8), :] = dxnext[...]
        dxr = (pv_ref[3:4, :] * dxc + pv_ref[2:3, :] * dxe[pl.ds(1, tl), :]
               + pv_ref[1:2, :] * dxe[pl.ds(2, tl), :] + pv_ref[0:1, :] * dxe[pl.ds(3, tl), :])
        dxr_ref[...] = dxr.astype(BF16)
        dxnext[...] = dxc[0:8, :]

    hb = tl // 8
    rev = lambda t: nt - 1 - t
    halo = lambda t: jnp.maximum(rev(t) * hb - 1, 0)
    big = lambda: pltpu.VMEM((tl + 8, LC), F32)
    til = lambda: pltpu.VMEM((tl, LC), F32)
    return pl.pallas_call(
        body, name=name, grid=(2, nt),
        in_specs=[pl.BlockSpec((tl, LC), lambda c, t: (rev(t), c)),
                  pl.BlockSpec((tl, LC), lambda c, t: (rev(t), c)),
                  pl.BlockSpec((8, LC), lambda c, t: (halo(t), c)),
                  pl.BlockSpec((tl, LC), lambda c, t: (rev(t), c)),
                  pl.BlockSpec((16, LC), lambda c, t: (jnp.maximum(rev(t) * (tl // 16) - 1, 0), c)),
                  pl.BlockSpec((8, LC), lambda c, t: (0, c)),
                  pl.BlockSpec((2, 256, 256), lambda c, t: (c, 0, 0)),
                  pl.BlockSpec((2, 256, 256), lambda c, t: (c, 0, 0))],
        out_specs=[pl.BlockSpec((tl, LC), lambda c, t: (rev(t), c)),
                   pl.BlockSpec((8, LC), lambda c, t: (0, c)),
                   pl.BlockSpec((2, 256, 256), lambda c, t: (c, 0, 0)),
                   pl.BlockSpec((2, 256, 256), lambda c, t: (c, 0, 0))],
        out_shape=[jax.ShapeDtypeStruct((S, D), BF16), jax.ShapeDtypeStruct((8, D), F32),
                   jax.ShapeDtypeStruct((4, 256, 256), F32), jax.ShapeDtypeStruct((4, 256, 256), F32)],
        scratch_shapes=[big(), big(), big(), til(), til(), til(), big(),
                        pltpu.VMEM((1, LC), F32), pltpu.VMEM((1, LC), F32), pltpu.VMEM((8, LC), F32)],
        compiler_params=_cp("parallel", "arbitrary"),
    )(dh, h, h, rest, rest, pvec, wa, wx)


def mix_out_fwd(x, ao, hg, rest, vec, w_att_o, w_rec_o, w_out, name, tm=512):
    S = x.shape[0]
    tm = min(tm, S)

    def body(x_ref, ao_ref, hg_ref, ga_ref, gr_ref, vec_ref, wa_ref, wr_ref, wo_ref,
             xo_ref, att_ref, rec_ref, mg_ref, f_ref):
        att = _dot(ao_ref[...], wa_ref[...])
        rec = _dot(hg_ref[...], wr_ref[...])
        att_ref[...] = att.astype(BF16)
        rec_ref[...] = rec.astype(BF16)
        mg = (_sigmoid(ga_ref[...].astype(F32)) * att + _sigmoid(gr_ref[...].astype(F32)) * rec).astype(BF16)
        mg_ref[...] = mg
        f = _dot(mg, wo_ref[...])
        f_ref[...] = f.astype(BF16)
        y = f * lax.rsqrt(_mean(f * f) + EPS) * vec_ref[1:2, :]
        xo_ref[...] = x_ref[...] + (1.0 * vec_ref[4:5, :]) * y

    row = lambda i: (i, 0)
    full = lambda r: pl.BlockSpec((r, D), lambda i: (0, 0))
    return pl.pallas_call(
        body, name=name, grid=(S // tm,),
        in_specs=[pl.BlockSpec((tm, D), row), pl.BlockSpec((tm, 512), row), pl.BlockSpec((tm, D), row),
                  pl.BlockSpec((tm, D), lambda i: (i, 2)), pl.BlockSpec((tm, D), lambda i: (i, 3)),
                  full(8), full(512), full(D), full(D)],
        out_specs=[pl.BlockSpec((tm, D), row)] * 5,
        out_shape=[jax.ShapeDtypeStruct((S, D), F32)] + [jax.ShapeDtypeStruct((S, D), BF16)] * 4,
        compiler_params=_cp("parallel"),
    )(x, ao, hg, rest, rest, vec, w_att_o, w_rec_o, w_out)


def mix_out_bwd(dxo, f, att, rec, rest, h, vec, w_att_o, w_rec_o, w_out, name, tm=512):
    S = dxo.shape[0]
    tm = min(tm, S)

    def body(dxo_ref, f_ref, att_ref, rec_ref, yr_ref, ga_ref, gr_ref, h_ref, vec_ref, wa_ref, wr_ref, wo_ref,
             df_ref, da_ref, dr_ref, dao_ref, dh_ref, d3_ref, vacc_ref):
        @pl.when(pl.program_id(0) == 0)
        def _():
            vacc_ref[...] = jnp.zeros_like(vacc_ref)

        df = _post_norm_bwd(dxo_ref[...], f_ref[...].astype(F32), 1.0, vec_ref, vacc_ref).astype(BF16)
        df_ref[...] = df
        dm = _dot_nt(df, wo_ref[...])
        sa = _sigmoid(ga_ref[...].astype(F32))
        sr = _sigmoid(gr_ref[...].astype(F32))
        d_att = (dm * sa).astype(BF16)
        d_rec = (dm * sr).astype(BF16)
        da_ref[...] = d_att
        dr_ref[...] = d_rec
        d3_ref[1] = (dm * att_ref[...].astype(F32) * (sa * (1.0 - sa))).astype(BF16)
        d3_ref[2] = (dm * rec_ref[...].astype(F32) * (sr * (1.0 - sr))).astype(BF16)
        dao_ref[...] = _dot_nt(d_att, wa_ref[...]).astype(BF16)
        d_hg = _dot_nt(d_rec, wr_ref[...])
        yr = yr_ref[...].astype(F32)
        t = jnp.tanh(_GK * (yr + 0.044715 * yr * yr * yr))
        dh_ref[...] = d_hg * (0.5 * yr * (1.0 + t))
        gelu_grad = 0.5 * (1.0 + t) + 0.5 * yr * (1.0 - t * t) * _GK * (1.0 + 3.0 * 0.044715 * yr * yr)
        d3_ref[0] = (d_hg * h_ref[...] * gelu_grad).astype(BF16)

    row = lambda i: (i, 0)
    full = lambda r: pl.BlockSpec((r, D), lambda i: (0, 0))
    return pl.pallas_call(
        body, name=name, grid=(S // tm,),
        in_specs=[pl.BlockSpec((tm, D), row)] * 4
        + [pl.BlockSpec((tm, D), lambda i: (i, 1)), pl.BlockSpec((tm, D), lambda i: (i, 2)),
           pl.BlockSpec((tm, D), lambda i: (i, 3)), pl.BlockSpec((tm, D), row),
           full(8), full(512), full(D), full(D)],
        out_specs=[pl.BlockSpec((tm, D), row)] * 3
        + [pl.BlockSpec((tm, 512), row), pl.BlockSpec((tm, D), row),
           pl.BlockSpec((3, tm, D), lambda i: (0, i, 0)), pl.BlockSpec((8, D), lambda i: (0, 0))],
        out_shape=[jax.ShapeDtypeStruct((S, D), BF16)] * 3
        + [jax.ShapeDtypeStruct((S, 512), BF16), jax.ShapeDtypeStruct((S, D), F32),
           jax.ShapeDtypeStruct((3, S, D), BF16), jax.ShapeDtypeStruct((8, D), F32)],
        compiler_params=_cp("arbitrary"),
    )(dxo, f, att, rec, rest, rest, rest, h, vec, w_att_o, w_rec_o, w_out)


def dw_in(h, dq, dkv, dxr, d3, name, tk=1024, tn=512):
    S = h.shape[0]
    tk = min(tk, S)
    nk = S // tk

    def body(h_ref, dq_ref, dkv_ref, dxr_ref, d3_ref, o_ref, acc):
        j, k = pl.program_id(0), pl.program_id(1)

        @pl.when(k == 0)
        def _():
            acc[...] = jnp.zeros_like(acc)

        @pl.when(j == 0)
        def _():
            acc[...] += _dot_tn(h_ref[pl.ds(pl.multiple_of(k * tk, 16), tk), :], dq_ref[...])

        @pl.when((j >= 1) & (j < 3))
        def _():
            acc[...] += _dot_tn(h_ref[pl.ds(pl.multiple_of(k * tk, 16), tk), :], dkv_ref[...])

        @pl.when((j >= 3) & (j < 5))
        def _():
            acc[...] += _dot_tn(h_ref[pl.ds(pl.multiple_of(k * tk, 16), tk), :], dxr_ref[...])

        @pl.when(j >= 5)
        def _():
            acc[...] += _dot_tn(h_ref[pl.ds(pl.multiple_of(k * tk, 16), tk), :], d3_ref[...])

        @pl.when(k == nk - 1)
        def _():
            o_ref[...] = acc[...].astype(BF16)

    use = lambda j, k, lo, hi: jnp.where((j >= lo) & (j < hi), k, 0)
    g3 = lambda j: jnp.clip(j - 5, 0, 5)
    return pl.pallas_call(
        body, name=name, grid=(PW // tn, nk),
        in_specs=[pl.BlockSpec((S, D), lambda j, k: (0, 0), pipeline_mode=pl.Buffered(1)),
                  pl.BlockSpec((None, tk, tn), lambda j, k: (0, use(j, k, 0, 1), 0)),
                  pl.BlockSpec((None, tk, tn), lambda j, k: (jnp.clip(j - 1, 0, 1), use(j, k, 1, 3), 0)),
                  pl.BlockSpec((tk, tn), lambda j, k: (use(j, k, 3, 5), jnp.clip(j - 3, 0, 1))),
                  pl.BlockSpec((None, tk, tn), lambda j, k: (g3(j) // 2, use(j, k, 5, 11), g3(j) % 2))],
        out_specs=pl.BlockSpec((D, tn), lambda j, k: (0, j)),
        out_shape=jax.ShapeDtypeStruct((D, PW), BF16),
        scratch_shapes=[pltpu.VMEM((D, tn), F32)],
        compiler_params=_cp("parallel", "arbitrary"),
    )(h, dq, dkv, dxr, d3)


def ada_fwd(c_all, w_ada, b_ada, name, tn=768):
    n = w_ada.shape[1]

    def body(c_ref, w_ref, b_ref, o_ref):
        cv = c_ref[...]
        ca = (cv * _sigmoid(cv)).astype(BF16)
        o_ref[...] = _dot(ca, w_ref[...].astype(BF16)) + b_ref[...]

    return pl.pallas_call(
        body, name=name, grid=(n // tn,),
        in_specs=[pl.BlockSpec((8, D), lambda j: (0, 0)), pl.BlockSpec((D, tn), lambda j: (0, j)),
                  pl.BlockSpec((1, tn), lambda j: (0, j))],
        out_specs=pl.BlockSpec((8, tn), lambda j: (0, j)),
        out_shape=jax.ShapeDtypeStruct((8, n), F32),
        compiler_params=_cp("parallel"),
    )(c_all, w_ada, b_ada)


def ada_bwd(c_all_t, dmod, name, tn=768):
    n = dmod.shape[1]

    def body(c_ref, d_ref, o_ref):
        cv = c_ref[...]
        ca = (cv * _sigmoid(cv)).astype(BF16)
        o_ref[...] = _dot(ca, d_ref[...].astype(BF16))

    return pl.pallas_call(
        body, name=name, grid=(n // tn,),
        in_specs=[pl.BlockSpec((D, 128), lambda j: (0, 0)), pl.BlockSpec((128, tn), lambda j: (0, j))],
        out_specs=pl.BlockSpec((D, tn), lambda j: (0, j)),
        out_shape=jax.ShapeDtypeStruct((D, n), F32),
        compiler_params=_cp("parallel"),
    )(c_all_t, dmod)


def _row_tile(rows, cols, itemsize=4, budget=1536 * 1024):
    best = None
    for t in range(8, rows + 1, 8):
        if rows % t == 0 and t * cols * itemsize <= budget:
            best = t
    return rows if best is None else best


def sum_lead(parts, name, out_dtype=F32):
    n, R, C = parts.shape
    tr = _row_tile(R, C * n)

    def body(p_ref, o_ref):
        acc = p_ref[0].astype(F32)
        for k in range(1, n):
            acc = acc + p_ref[k].astype(F32)
        o_ref[...] = acc.astype(out_dtype)

    return pl.pallas_call(
        body, name=name, grid=(R // tr,),
        in_specs=[pl.BlockSpec((n, tr, C), lambda i: (0, i, 0))],
        out_specs=pl.BlockSpec((tr, C), lambda i: (i, 0)),
        out_shape=jax.ShapeDtypeStruct((R, C), out_dtype),
        compiler_params=_cp("parallel"),
    )(parts)


def adamw(w, g, m, v, name, emit_g=False):
    R, C = w.shape
    tr = _row_tile(R, C * 8, budget=8 * 1024 * 1024)

    def body(w_ref, g_ref, m_ref, v_ref, d_ref, mo_ref, vo_ref, *go_ref):
        gv = g_ref[...]
        if emit_g:
            go_ref[0][...] = gv
        mn = ADAM_B1 * m_ref[...] + (1.0 - ADAM_B1) * gv
        vn = ADAM_B2 * v_ref[...] + (1.0 - ADAM_B2) * (gv * gv)
        m_hat = mn / (1.0 - ADAM_B1 ** ADAM_STEP)
        v_hat = vn / (1.0 - ADAM_B2 ** ADAM_STEP)
        d_ref[...] = -ADAM_LR * (m_hat / (jnp.sqrt(v_hat) + ADAM_EPS) + ADAM_WD * w_ref[...])
        mo_ref[...] = mn
        vo_ref[...] = vn

    spec = pl.BlockSpec((tr, C), lambda i: (i, 0))
    return pl.pallas_call(
        body, name=name, grid=(R // tr,),
        in_specs=[spec] * 4, out_specs=[spec] * (4 if emit_g else 3),
        out_shape=[jax.ShapeDtypeStruct((R, C), F32)] * (4 if emit_g else 3),
        compiler_params=_cp("parallel"),
    )(w, g, m, v)


def _mesh_pos():
    return lax.axis_index("x"), lax.axis_index("y"), lax.axis_index("c")


def _other_chips(mx, my):
    return [(1 - mx, my), (mx, 1 - my), (1 - mx, 1 - my)]


def ag_small(x, name):
    R = x.shape[0]

    def body(x_ref, out_ref, send_sems, recv_sems, local_sem):
        mx, my, mc = _mesh_pos()
        me, sibling = (mx, my, mc), (mx, my, 1 - mc)
        chips = _other_chips(mx, my)

        def slot(px, py, pc):
            return out_ref.at[4 * px + 2 * py + pc]

        def copy(k, block, to, src=None):
            return pltpu.make_async_remote_copy(
                src_ref=slot(*block) if src is None else src, dst_ref=slot(*block),
                send_sem=send_sems.at[k], recv_sem=recv_sems.at[k], device_id=to, device_id_type=MESH)

        mine = pltpu.make_async_copy(x_ref, slot(*me), local_sem)
        mine.start()
        first = [copy(0, me, sibling, src=x_ref)]
        first += [copy(1 + j, me, (*chip, mc), src=x_ref) for j, chip in enumerate(chips)]
        for cp in first:
            cp.start()
        passed = [copy(4 + j, (*chip, mc), sibling) for j, chip in enumerate(chips)]
        for j, chip in enumerate(chips):
            copy(1 + j, (*chip, mc), me).wait_recv()
            passed[j].start()
        copy(0, sibling, me).wait_recv()
        for j, chip in enumerate(chips):
            copy(4 + j, (*chip, 1 - mc), me).wait_recv()
        for cp in first + passed:
            cp.wait_send()
        mine.wait()

    return pl.pallas_call(
        body, name=name,
        out_shape=jax.ShapeDtypeStruct((N_DEV, R, 128), F32),
        in_specs=[pl.BlockSpec(memory_space=pltpu.VMEM)],
        out_specs=pl.BlockSpec(memory_space=pltpu.VMEM),
        scratch_shapes=[pltpu.SemaphoreType.DMA((7,)), pltpu.SemaphoreType.DMA((7,)), pltpu.SemaphoreType.DMA],
        compiler_params=pltpu.CompilerParams(vmem_limit_bytes=VMEM_LIMIT),
    )(x)


BIG = (("ffn1_w_gu", "col", D, PW), ("ffn1_w_down", "row", FF, D), ("w_in", "col", D, PW),
       ("w_att_o", "col", 512, D), ("w_rec_o", "row", D, D), ("w_out", "row", D, D),
       ("ffn2_w_gu", "col", D, PW), ("ffn2_w_down", "row", FF, D))
NBIG = len(BIG)


def _shard_shape(kind, R, C):
    return (R, C // 4) if kind == "col" else (R // 4, C)


def _region(ref, kind, R, C, q, half, t, tr):
    sr, sc = _shard_shape(kind, R, C)
    if kind == "col":
        return ref.at[pl.ds(pl.multiple_of(half * (R // 2) + t * tr, 16), tr), pl.ds(q * sc, sc)]
    return ref.at[pl.ds(pl.multiple_of(q * sr + t * tr, 16), tr), pl.ds(half * (C // 2), C // 2)]


def ag_local(w, kind, R, C, p_arr, name, after=()):
    sr, sc = _shard_shape(kind, R, C)
    tr = _row_tile(sr, sc, budget=2 * 1024 * 1024)
    nt = sr // tr
    after = list(after)

    def body(p_ref, w_ref, *rest):
        rest[-1][...] = w_ref[...].astype(BF16)

    if kind == "col":
        o_spec = pl.BlockSpec((tr, sc), lambda i, p: (i, p[0]))
    else:
        o_spec = pl.BlockSpec((tr, sc), lambda i, p: (p[0] * nt + i, 0))
    return pl.pallas_call(
        body, name=name,
        grid_spec=pltpu.PrefetchScalarGridSpec(
            num_scalar_prefetch=1, grid=(nt,),
            in_specs=[pl.BlockSpec((tr, sc), lambda i, p: (i, 0))] + [ANY] * len(after), out_specs=o_spec),
        out_shape=jax.ShapeDtypeStruct((R, C), BF16),
        compiler_params=_cp("parallel"),
    )(p_arr, w, *after)


HBM_SPEC = pl.BlockSpec(memory_space=pltpu.HBM)
SEM_SPEC = pl.BlockSpec(memory_space=pltpu.SEMAPHORE)


def _ag_sems(geoms):
    return sum(6 if both else 3 for (_, _, _, both) in geoms)


def _ag_copies(fulls, geoms, ssem, rsem, mx, my, mc, q, h):
    chips = _other_chips(mx, my)
    out, base = [], 0
    for w, (kind, R, C, both) in enumerate(geoms):
        sr, sc = _shard_shape(kind, R, C)
        hr = sr // 2 if kind == "col" else sr
        reg = _region(fulls[w], kind, R, C, q, h, 0, hr)
        out.append([pltpu.make_async_remote_copy(
            src_ref=reg, dst_ref=reg, send_sem=ssem.at[base + 3 * t + k], recv_sem=rsem.at[base + 3 * t + k],
            device_id=(*chips[k], mc if t == 0 else 1 - mc), device_id_type=MESH)
            for t in range(2 if both else 1) for k in range(3)])
        base += 6 if both else 3
    return out


def ag_start(fulls, geoms, after, name):
    n = len(fulls)
    after = list(after)
    m = len(after)

    def body(*refs):
        ssem, rsem = refs[n + m:n + m + 2]
        outs, token = refs[n + m + 2:2 * n + m + 2], refs[2 * n + m + 2]
        mx, my, mc = _mesh_pos()
        p = 2 * mx + my
        col = [w for w, g in enumerate(geoms) if g[0] == "col"]
        row = [w for w, g in enumerate(geoms) if g[0] == "row"]
        for q in range(4):
            @pl.when(p == q)
            def _(q=q):
                cps = _ag_copies(outs, geoms, ssem, rsem, mx, my, mc, q, mc)
                for w in col:
                    for cp in cps[w]:
                        cp.start()
        for h in range(2):
            @pl.when(mc == h)
            def _(h=h):
                cps = _ag_copies(outs, geoms, ssem, rsem, mx, my, mc, p, h)
                for w in row:
                    for cp in cps[w]:
                        cp.start()
        token[...] = jnp.zeros_like(token)

    res = pl.pallas_call(
        body, name=name,
        out_shape=[pltpu.SemaphoreType.DMA((_ag_sems(geoms),)), pltpu.SemaphoreType.DMA((_ag_sems(geoms),))]
        + [pltpu.HBM(a.shape, a.dtype) for a in fulls] + [jax.ShapeDtypeStruct((8, 128), F32)],
        in_specs=[HBM_SPEC] * n + [ANY] * m,
        out_specs=[SEM_SPEC, SEM_SPEC] + [HBM_SPEC] * n + [pl.BlockSpec(memory_space=pltpu.VMEM)],
        input_output_aliases={w: 2 + w for w in range(n)},
        compiler_params=pltpu.CompilerParams(has_side_effects=pltpu.SideEffectType.DATAFLOW_SIDE_EFFECTING),
    )(*[pltpu.with_memory_space_constraint(a, pltpu.HBM) for a in fulls], *after)
    return res[0], res[1], list(res[2:2 + n]), res[2 + n]


def ag_wait(fulls, geoms, ssem, rsem, after, name):
    n = len(fulls)
    after = list(after) if isinstance(after, (list, tuple)) else [after]

    def body(*refs):
        ins, ssem_ref, rsem_ref = refs[:n], refs[n], refs[n + 1]
        mx, my, mc = _mesh_pos()
        for cps in _ag_copies(ins, geoms, ssem_ref, rsem_ref, mx, my, mc, 0, 0):
            for cp in cps:
                cp.wait_send()
                cp.wait_recv()

    return list(pl.pallas_call(
        body, name=name,
        out_shape=[pltpu.HBM(a.shape, a.dtype) for a in fulls],
        in_specs=[HBM_SPEC] * n + [SEM_SPEC, SEM_SPEC] + [ANY] * len(after),
        out_specs=[HBM_SPEC] * n,
        input_output_aliases={w: w for w in range(n)},
        compiler_params=pltpu.CompilerParams(has_side_effects=pltpu.SideEffectType.DATAFLOW_SIDE_EFFECTING),
    )(*fulls, ssem, rsem, *after))


def ag_forward(full, kind, R, C, name):
    sr, sc = _shard_shape(kind, R, C)
    hr, hc = (sr // 2, sc) if kind == "col" else (sr, sc // 2)
    tr = _row_tile(hr, hc, itemsize=2, budget=512 * 1024)
    nt = hr // tr

    total = 3 * nt

    def body(src_ref, full_ref, stage, lsem, ssem, rsem):
        step = pl.program_id(0) * nt + pl.program_id(1)
        par = step % 2
        mx, my, mc = _mesh_pos()

        def load(s, q, h, t):
            return pltpu.make_async_copy(_region(src_ref, kind, R, C, q, h, t, tr), stage.at[s], lsem.at[s])

        def push(s, q, h, t):
            return pltpu.make_async_remote_copy(src_ref=stage.at[s], dst_ref=_region(full_ref, kind, R, C, q, h, t, tr),
                                                send_sem=ssem.at[s], recv_sem=rsem, device_id=(mx, my, 1 - mc),
                                                device_id_type=MESH)

        def for_tile(stp, fn):
            q_k = _partner_chip(stp // nt, 2 * mx + my)
            if kind == "col":
                for q in range(4):
                    @pl.when(q_k == q)
                    def _(q=q):
                        fn(q, mc, stp % nt)
            else:
                for h in range(2):
                    @pl.when(mc == h)
                    def _(h=h):
                        fn(q_k, h, stp % nt)

        @pl.when(step == 0)
        def _():
            for_tile(step, lambda q, h, t: load(0, q, h, t).start())

        load(par, 0, 0, 0).wait()
        for_tile(step, lambda q, h, t: push(par, q, h, t).start())

        @pl.when(step + 1 < total)
        def _():
            @pl.when(step >= 1)
            def _():
                push(1 - par, 0, 0, 0).wait_send()
            for_tile(step + 1, lambda q, h, t: load(1 - par, q, h, t).start())

        @pl.when(step == total - 1)
        def _():
            push(par, 0, 0, 0).wait_send()
            push(1 - par, 0, 0, 0).wait_send()
            three = full_ref.at[pl.ds(0, hr), pl.ds(0, 3 * hc)] if kind == "col" else full_ref.at[pl.ds(0, 3 * hr), pl.ds(0, hc)]
            pltpu.make_async_remote_copy(src_ref=three, dst_ref=three, send_sem=ssem.at[0], recv_sem=rsem,
                                         device_id=(mx, my, 1 - mc), device_id_type=MESH).wait_recv()

    return pl.pallas_call(
        body, name=name, grid=(3, nt),
        in_specs=[ANY], out_specs=ANY,
        out_shape=jax.ShapeDtypeStruct((R, C), BF16),
        scratch_shapes=[pltpu.VMEM((2, tr, hc), BF16), pltpu.SemaphoreType.DMA((2,)), pltpu.SemaphoreType.DMA((2,)),
                        pltpu.SemaphoreType.DMA],
        input_output_aliases={0: 0},
        compiler_params=_cp("arbitrary", "arbitrary"),
    )(full)


def _half_shape(kind, R, C):
    return (R // 2, C) if kind == "col" else (R, C // 2)


def _piece_shape(kind, R, C):
    return (R // 2, C // 4) if kind == "col" else (R // 4, C // 2)


def pair_push(g, kind, c_arr, name):
    R, C = g.shape
    hr, hc = _half_shape(kind, R, C)
    tr = _row_tile(hr, hc, itemsize=2, budget=1024 * 1024)
    nt = hr // tr

    def body(c_ref, g_ref, out_ref, stage, ssem, rsem):
        i = pl.program_id(0)
        slot = i % 2
        mx, my, mc = _mesh_pos()

        def push(s, t):
            return pltpu.make_async_remote_copy(
                src_ref=stage.at[s], dst_ref=out_ref.at[pl.ds(pl.multiple_of(t * tr, 16), tr)],
                send_sem=ssem.at[s], recv_sem=rsem, device_id=(mx, my, 1 - mc), device_id_type=MESH)

        @pl.when(i >= 2)
        def _():
            push(slot, 0).wait_send()

        stage[slot] = g_ref[...]
        push(slot, i).start()

        @pl.when(i == nt - 1)
        def _():
            push(slot, 0).wait_send()
            if nt >= 2:
                push(1 - slot, 0).wait_send()
            pltpu.make_async_remote_copy(src_ref=out_ref, dst_ref=out_ref, send_sem=ssem.at[0], recv_sem=rsem,
                                         device_id=(mx, my, 1 - mc), device_id_type=MESH).wait_recv()

    if kind == "col":
        g_spec = pl.BlockSpec((tr, hc), lambda i, c: ((1 - c[0]) * nt + i, 0))
    else:
        g_spec = pl.BlockSpec((tr, hc), lambda i, c: (i, 1 - c[0]))
    return pl.pallas_call(
        body, name=name,
        grid_spec=pltpu.PrefetchScalarGridSpec(
            num_scalar_prefetch=1, grid=(nt,), in_specs=[g_spec], out_specs=ANY,
            scratch_shapes=[pltpu.VMEM((2, tr, hc), BF16), pltpu.SemaphoreType.DMA((2,)), pltpu.SemaphoreType.DMA]),
        out_shape=jax.ShapeDtypeStruct((hr, hc), BF16),
        compiler_params=_cp("arbitrary"),
    )(c_arr, g)


def _partner_chip(k, p):
    return p ^ jnp.where(k == 0, 2, jnp.where(k == 1, 1, jnp.where(k == 2, 3, 0)))


def pair_add(g, got, kind, cp_arr, name):
    R, C = g.shape
    pr, pc = _piece_shape(kind, R, C)
    tr = _row_tile(pr, pc, itemsize=2, budget=1024 * 1024)
    nt = pr // tr

    def body(cp_ref, g_ref, got_ref, ps_ref, rb_ref):
        tile = (g_ref[...].astype(F32) + got_ref[...].astype(F32)).astype(BF16)
        ps_ref[...] = tile

        @pl.when(pl.program_id(1) == cp_ref[1])
        def _():
            rb_ref[...] = tile

    if kind == "col":
        g_spec = pl.BlockSpec((tr, pc), lambda i, q, cp: (cp[0] * nt + i, q))
        got_spec = pl.BlockSpec((tr, pc), lambda i, q, cp: (i, q))
    else:
        g_spec = pl.BlockSpec((tr, pc), lambda i, q, cp: (q * nt + i, cp[0]))
        got_spec = pl.BlockSpec((tr, pc), lambda i, q, cp: (q * nt + i, 0))
    return pl.pallas_call(
        body, name=name,
        grid_spec=pltpu.PrefetchScalarGridSpec(
            num_scalar_prefetch=1, grid=(nt, 4), in_specs=[g_spec, got_spec],
            out_specs=[pl.BlockSpec((None, tr, pc), lambda i, q, cp: (q, i, 0)),
                       pl.BlockSpec((None, tr, pc), lambda i, q, cp: (cp[1], i, 0))]),
        out_shape=[jax.ShapeDtypeStruct((4, pr, pc), BF16)] * 2,
        compiler_params=_cp("arbitrary", "arbitrary"),
    )(cp_arr, g, got)


def _rs_copies(ps, rb, ssem, rsem, mx, my, mc):
    p = 2 * mx + my
    out = []
    for w in range(len(ps)):
        for k, chip in enumerate(_other_chips(mx, my)):
            out.append(pltpu.make_async_remote_copy(
                src_ref=ps[w].at[2 * chip[0] + chip[1]], dst_ref=rb[w].at[p], send_sem=ssem.at[3 * w + k],
                recv_sem=rsem.at[3 * w + k], device_id=(*chip, mc), device_id_type=MESH))
    return out


def rs_start(ps, rb, after, name):
    n = len(ps)
    after = list(after)
    m = len(after)

    def body(*refs):
        ssem, rsem = refs[2 * n + m:2 * n + m + 2]
        ps_o = refs[2 * n + m + 2:3 * n + m + 2]
        rb_o = refs[3 * n + m + 2:4 * n + m + 2]
        token = refs[4 * n + m + 2]
        for cp in _rs_copies(ps_o, rb_o, ssem, rsem, *_mesh_pos()):
            cp.start()
        token[...] = jnp.zeros_like(token)

    both = list(ps) + list(rb)
    res = pl.pallas_call(
        body, name=name,
        out_shape=[pltpu.SemaphoreType.DMA((3 * n,)), pltpu.SemaphoreType.DMA((3 * n,))]
        + [pltpu.HBM(a.shape, a.dtype) for a in both] + [jax.ShapeDtypeStruct((8, 128), F32)],
        in_specs=[HBM_SPEC] * (2 * n) + [ANY] * m,
        out_specs=[SEM_SPEC, SEM_SPEC] + [HBM_SPEC] * (2 * n) + [pl.BlockSpec(memory_space=pltpu.VMEM)],
        input_output_aliases={w: 2 + w for w in range(2 * n)},
        compiler_params=pltpu.CompilerParams(has_side_effects=pltpu.SideEffectType.DATAFLOW_SIDE_EFFECTING),
    )(*[pltpu.with_memory_space_constraint(a, pltpu.HBM) for a in both], *after)
    return res[0], res[1], list(res[2:2 + n]), list(res[2 + n:2 + 2 * n]), res[2 + 2 * n]


def rs_wait(ps, rb, ssem, rsem, after, name):
    n = len(ps)
    after = list(after)
    m = len(after)

    def body(*refs):
        ps_i, rb_i = refs[:n], refs[n:2 * n]
        ssem_ref, rsem_ref = refs[2 * n], refs[2 * n + 1]
        for cp in _rs_copies(ps_i, rb_i, ssem_ref, rsem_ref, *_mesh_pos()):
            cp.wait_send()
            cp.wait_recv()

    both = list(ps) + list(rb)
    res = pl.pallas_call(
        body, name=name,
        out_shape=[pltpu.HBM(a.shape, a.dtype) for a in both],
        in_specs=[HBM_SPEC] * (2 * n) + [SEM_SPEC, SEM_SPEC] + [ANY] * m,
        out_specs=[HBM_SPEC] * (2 * n),
        input_output_aliases={w: w for w in range(2 * n)},
        compiler_params=pltpu.CompilerParams(has_side_effects=pltpu.SideEffectType.DATAFLOW_SIDE_EFFECTING),
    )(*both, ssem, rsem, *after)
    return list(res[n:])


def sum_share(parts, kind, R, C, name):
    _, pr, pc = parts.shape
    sr, sc = _shard_shape(kind, R, C)
    tr = _row_tile(pr, pc * 4, budget=4 * 1024 * 1024)
    nt = pr // tr

    def body(p_ref, fin_ref, stage, lsem, ssem, rsem):
        i = pl.program_id(0)
        slot = i % 2
        mx, my, mc = _mesh_pos()

        def region(h, t):
            r0 = pl.multiple_of(t * tr, 8)
            if kind == "col":
                return fin_ref.at[pl.ds(pl.multiple_of(h * pr + r0, 8), tr)]
            return fin_ref.at[pl.ds(r0, tr), pl.ds(h * pc, pc)]

        def copies(s, h, t):
            return (pltpu.make_async_copy(stage.at[s], region(h, t), lsem.at[s]),
                    pltpu.make_async_remote_copy(src_ref=stage.at[s], dst_ref=region(h, t), send_sem=ssem.at[s],
                                                 recv_sem=rsem, device_id=(mx, my, 1 - mc), device_id_type=MESH))

        def wait_sent(s):
            loc, rem = copies(s, 0, 0)
            loc.wait()
            rem.wait_send()

        @pl.when(i >= 2)
        def _():
            wait_sent(slot)

        acc = p_ref[0].astype(F32)
        for k in range(1, 4):
            acc = acc + p_ref[k].astype(F32)
        stage[slot] = acc
        if kind == "col":
            for cp in copies(slot, mc, i):
                cp.start()
        else:
            for h in range(2):
                @pl.when(mc == h)
                def _(h=h):
                    for cp in copies(slot, h, i):
                        cp.start()

        @pl.when(i == nt - 1)
        def _():
            wait_sent(slot)
            if nt >= 2:
                wait_sent(1 - slot)
            half = fin_ref.at[pl.ds(0, pr), pl.ds(0, pc)]
            pltpu.make_async_remote_copy(src_ref=half, dst_ref=half, send_sem=ssem.at[0], recv_sem=rsem,
                                         device_id=(mx, my, 1 - mc), device_id_type=MESH).wait_recv()

    return pl.pallas_call(
        body, name=name, grid=(nt,),
        in_specs=[pl.BlockSpec((4, tr, pc), lambda i: (0, i, 0))],
        out_specs=ANY,
        out_shape=jax.ShapeDtypeStruct((sr, sc), F32),
        scratch_shapes=[pltpu.VMEM((2, tr, pc), F32), pltpu.SemaphoreType.DMA((2,)), pltpu.SemaphoreType.DMA((2,)),
                        pltpu.SemaphoreType.DMA],
        compiler_params=_cp("arbitrary"),
    )(parts)


def _pack(parts, rows):
    flat = []
    for a in parts:
        a = jnp.ravel(a).astype(F32)
        flat.append(jnp.pad(a, (0, (-a.shape[0]) % 128)))
    v = jnp.concatenate(flat)
    return jnp.pad(v, (0, rows * 128 - v.shape[0])).reshape(rows, 128)


def _unpack(block, shapes):
    lead = block.shape[:-2]
    v = block.reshape(lead + (-1,))
    out, off = [], 0
    for shp in shapes:
        n = int(np.prod(shp))
        out.append(v[..., off:off + n].reshape(lead + tuple(shp)))
        off += n + (-n) % 128
    return out


def _block_diag4(w):
    w4 = w.reshape(4, 4, 64, 64)
    eye = jnp.eye(4, dtype=w.dtype)
    return (w4[:, :, :, None, :] * eye[None, :, None, :, None]).reshape(4, 256, 256)


def _diag_blocks(bd):
    b5 = bd.reshape(4, 4, 64, 4, 64)
    return jnp.stack([b5[:, i, :, i, :] for i in range(4)], axis=1).reshape(16, 64, 64)


def _bias_window(rel_bias):
    m = (np.arange(768) + 127) % 768 - 127
    w = rel_bias[:, np.clip(512 - m, -128, 128) + 128]
    win = jnp.tile(w, (1, 128))[:, :128 * 767].reshape(8, 128, 767)[:, :, :WIN]
    qh = np.arange(128)[:, None] // CHUNK
    kc = np.arange(WIN)[None, :] // CHUNK
    valid = (kc >= qh) & (kc <= qh + 8)
    return jnp.where(jnp.asarray(valid)[None], win, NEG)


SMALL = ("b_ada", "norm_pre", "norm_post", "rel_bias", "conv_w", "conv_b", "lru_wa", "lru_ba", "lru_wx",
         "lru_bx", "lru_lambda")
WEIGHTS = ("w_ada", "b_ada", "norm_pre", "norm_post", "ffn1_w_gu", "ffn1_w_down", "w_in", "rel_bias", "conv_w",
           "conv_b", "lru_wa", "lru_ba", "lru_wx", "lru_bx", "lru_lambda", "w_att_o", "w_rec_o", "w_out",
           "ffn2_w_gu", "ffn2_w_down")


def kernel(x, c, w_ada, b_ada, norm_pre, norm_post, ffn1_w_gu, ffn1_w_down, w_in, rel_bias, conv_w, conv_b, lru_wa, lru_ba, lru_wx, lru_bx, lru_lambda, w_att_o, w_rec_o, w_out, ffn2_w_gu, ffn2_w_down, loss_target, m_w_ada, m_b_ada, m_norm_pre, m_norm_post, m_ffn1_w_gu, m_ffn1_w_down, m_w_in, m_rel_bias, m_conv_w, m_conv_b, m_lru_wa, m_lru_ba, m_lru_wx, m_lru_bx, m_lru_lambda, m_w_att_o, m_w_rec_o, m_w_out, m_ffn2_w_gu, m_ffn2_w_down, v_w_ada, v_b_ada, v_norm_pre, v_norm_post, v_ffn1_w_gu, v_ffn1_w_down, v_w_in, v_rel_bias, v_conv_w, v_conv_b, v_lru_wa, v_lru_ba, v_lru_wx, v_lru_bx, v_lru_lambda, v_w_att_o, v_w_rec_o, v_w_out, v_ffn2_w_gu, v_ffn2_w_down):
    W = dict(w_ada=w_ada, b_ada=b_ada, norm_pre=norm_pre, norm_post=norm_post, ffn1_w_gu=ffn1_w_gu,
             ffn1_w_down=ffn1_w_down, w_in=w_in, rel_bias=rel_bias, conv_w=conv_w, conv_b=conv_b, lru_wa=lru_wa,
             lru_ba=lru_ba, lru_wx=lru_wx, lru_bx=lru_bx, lru_lambda=lru_lambda, w_att_o=w_att_o, w_rec_o=w_rec_o,
             w_out=w_out, ffn2_w_gu=ffn2_w_gu, ffn2_w_down=ffn2_w_down)
    M = dict(w_ada=m_w_ada, b_ada=m_b_ada, norm_pre=m_norm_pre, norm_post=m_norm_post, ffn1_w_gu=m_ffn1_w_gu,
             ffn1_w_down=m_ffn1_w_down, w_in=m_w_in, rel_bias=m_rel_bias, conv_w=m_conv_w, conv_b=m_conv_b,
             lru_wa=m_lru_wa, lru_ba=m_lru_ba, lru_wx=m_lru_wx, lru_bx=m_lru_bx, lru_lambda=m_lru_lambda,
             w_att_o=m_w_att_o, w_rec_o=m_w_rec_o, w_out=m_w_out, ffn2_w_gu=m_ffn2_w_gu, ffn2_w_down=m_ffn2_w_down)
    V = dict(w_ada=v_w_ada, b_ada=v_b_ada, norm_pre=v_norm_pre, norm_post=v_norm_post, ffn1_w_gu=v_ffn1_w_gu,
             ffn1_w_down=v_ffn1_w_down, w_in=v_w_in, rel_bias=v_rel_bias, conv_w=v_conv_w, conv_b=v_conv_b,
             lru_wa=v_lru_wa, lru_ba=v_lru_ba, lru_wx=v_lru_wx, lru_bx=v_lru_bx, lru_lambda=v_lru_lambda,
             w_att_o=v_w_att_o, w_rec_o=v_w_rec_o, w_out=v_w_out, ffn2_w_gu=v_ffn2_w_gu, ffn2_w_down=v_ffn2_w_down)
    mx, my, mc = _mesh_pos()
    p = 2 * mx + my
    e = 4 * mx + 2 * my + mc
    xs = x[0]

    c_arr = jnp.reshape(mc, (1,)).astype(jnp.int32)
    cp_arr = jnp.stack([mc, p]).astype(jnp.int32)
    p_arr = jnp.reshape(p, (1,)).astype(jnp.int32)
    direct = ("w_att_o", "w_rec_o", "w_out", "ffn2_w_gu", "ffn2_w_down")
    geoms = [(kind, R, C, n in direct) for (n, kind, R, C) in BIG]
    names = [b[0] for b in BIG]
    placed = [ag_local(W[n][0], kind, R, C, p_arr, "ag_local_" + n) for (n, kind, R, C) in BIG[:2]]

    def arrived(fly, lo, hi, ssem, rsem, after, tag):
        done = ag_wait(fly, geoms[lo:hi], ssem, rsem, after, "ag_wait_" + tag)
        return [a if both else ag_forward(a, kind, R, C, "ag_forward_" + n)
                for a, (kind, R, C, both), n in zip(done, geoms[lo:hi], names[lo:hi])]

    g1 = ag_small(_pack([c, norm_pre, norm_post, conv_w], 32), "ag_small_params")
    c_all, npre4, npost4, cw4 = _unpack(g1, [(D,), (3, 256), (3, 256), (4, 256)])
    chipwise = lambda a: jnp.moveaxis(a[0::2], 0, 1).reshape(a.shape[1], D)
    npre, npost, conv_full = chipwise(npre4), chipwise(npost4), chipwise(cw4)

    b_cols = lax.dynamic_slice(b_ada, (0, p * 2304), (1, 2304))
    mod_cols = ada_fwd(c_all, w_ada[0], b_cols, "ada_fwd")
    g2 = ag_small(mod_cols.reshape(144, 128), "ag_mod")
    mod_all = jnp.moveaxis(g2[0::2].reshape(4, 8, 2304), 0, 1).reshape(8, 9 * D)
    mod = lax.dynamic_index_in_dim(mod_all, e, 0, keepdims=False).reshape(3, 3, D)
    zeros3 = jnp.zeros((3, D), F32)
    vecs = [jnp.concatenate([npre[k:k + 1], npost[k:k + 1], mod[k], zeros3], axis=0) for k in range(3)]

    gu_s, gu_r, gu_fly, tok_gu = ag_start(placed[:1], geoms[:1], [g2], "ag_start_ffn1_gu")
    dn_s, dn_r, dn_fly, tok0 = ag_start(placed[1:2], geoms[1:2], [tok_gu], "ag_start_ffn1_down")
    placed += [ag_local(W[n][0], kind, R, C, p_arr, "ag_local_" + n, after=[tok0]) for (n, kind, R, C) in BIG[2:]]
    f1_gu, = arrived(gu_fly, 0, 1, gu_s, gu_r, placed[2:], "ffn1_gu")
    f1_dn, = arrived(dn_fly, 1, 2, dn_s, dn_r, f1_gu, "ffn1_down")
    mix_s, mix_r, mix_fly, tok1 = ag_start(placed[2:6], geoms[2:6], [f1_gu, f1_dn], "ag_start_mixer")
    ffn_s, ffn_r, ffn_fly, tok2 = ag_start(placed[6:], geoms[6:], [tok1], "ag_start_ffn2")
    wa_bd = _block_diag4(lru_wa[0]).astype(BF16)
    wx_bd = _block_diag4(lru_wx[0]).astype(BF16)
    pvec = jnp.concatenate([conv_full, conv_b, lru_ba, lru_bx, lru_lambda], axis=0)
    bias = _bias_window(rel_bias[0]).reshape(4, 256, WIN)

    x1, h1, g1_, u1, a1, f1 = ffn_fwd(xs, vecs[0] + tok2[0:1, 0:1], f1_gu, f1_dn, 0.5, "ffn1_fwd")
    win, wao, wro, wout = arrived(mix_fly, 2, 6, mix_s, mix_r, x1, "mixer")
    h2, qkv, rest = proj_fwd(x1, vecs[1], win, "proj_fwd")
    ao = attn_fwd(qkv, bias, "attn_fwd")
    hl, hg = lru_fwd(rest, pvec, wa_bd, wx_bd, "lru_fwd")
    x2, att, rec, mg, f2 = mix_out_fwd(x1, ao, hg, rest, vecs[1], wao, wro, wout, "mix_out_fwd")
    f2_gu, f2_dn = arrived(ffn_fly, 6, 8, ffn_s, ffn_r, x2, "ffn2")
    dy, h3, g3_, u3, a3, f3, lvec = ffn_fwd(x2, vecs[2], f2_gu, f2_dn, 0.5, "ffn2_fwd", tgt=loss_target[0])

    G, grads = {}, {}
    geo = {n: (kind, R, C) for (n, kind, R, C) in BIG}

    def reduce_begin(names, tag):
        ps, rb = [], []
        for n in names:
            got = pair_push(G[n], geo[n][0], c_arr, "rs_push_" + n)
            a, b = pair_add(G[n], got, geo[n][0], cp_arr, "rs_pair_sum_" + n)
            ps.append(a)
            rb.append(b)
        return rs_start(ps, rb, [], "rs_start_" + tag)

    def reduce_end(names, flight, after, tag):
        ssem, rsem, ps, rb, _ = flight
        for a, n in zip(rs_wait(ps, rb, ssem, rsem, after, "rs_wait_" + tag), names):
            grads[n] = sum_share(a, *geo[n], "rs_sum_share_" + n)[None]

    dx2, df3, dgu3, va2 = ffn_bwd(dy, x2, f3, g3_, u3, vecs[2], f2_gu, f2_dn, 0.5, "ffn2_bwd")
    G["ffn2_w_gu"] = mm_tn(h3, dgu3, "dw_ffn2_gu", D, 1408, 2048)
    G["ffn2_w_down"] = mm_tn(a3, df3, "dw_ffn2_down", 1408, D, 2048)
    fly_ffn2 = reduce_begin(("ffn2_w_gu", "ffn2_w_down"), "ffn2")
    vec1 = vecs[1] + fly_ffn2[4][0:1, 0:1]
    df2, d_att, d_rec, dao, dhl, d3, va_out = mix_out_bwd(dx2, f2, att, rec, rest, hl, vec1, wao, wro, wout,
                                                          "mix_out_bwd")
    G["w_out"] = mm_tn(mg, df2, "dw_out", D, D, 1024)
    G["w_att_o"] = mm_tn(ao, d_att, "dw_att_o", 512, D, 1024)
    G["w_rec_o"] = mm_tn(hg, d_rec, "dw_rec_o", D, D, 1024)
    dq, db, dkv = attn_bwd(qkv, dao, bias, "attn_bwd")
    dxr, v_lru, dwa_bd, dwx_bd = lru_bwd(dhl, hl, rest, pvec, wa_bd, wx_bd, "lru_bwd")
    dx1, va_in = proj_bwd(dq, dkv, dxr, d3, win, x1, dx2, vecs[1], "proj_bwd")
    G["w_in"] = dw_in(h2, dq, dkv, dxr, d3, "dw_in")
    fly_mix = reduce_begin(("w_in", "w_att_o", "w_rec_o", "w_out"), "mixer")
    vec0 = vecs[0] + fly_mix[4][0:1, 0:1]
    dx0, df1, dgu1, va0 = ffn_bwd(dx1, xs, f1, g1_, u1, vec0, f1_gu, f1_dn, 0.5, "ffn1_bwd")
    G["ffn1_w_gu"] = mm_tn(h1, dgu1, "dw_ffn1_gu", D, 1408, 2048)
    G["ffn1_w_down"] = mm_tn(a1, df1, "dw_ffn1_down", 1408, D, 2048)
    fly_ffn1 = reduce_begin(("ffn1_w_gu", "ffn1_w_down"), "ffn1")
    reduce_end(("ffn2_w_gu", "ffn2_w_down"), fly_ffn2, [fly_ffn1[4]], "ffn2")
    reduce_end(("w_in", "w_att_o", "w_rec_o", "w_out"), fly_mix, [fly_ffn1[4], grads["ffn2_w_down"]], "mixer")

    va1 = va_out + va_in
    vas = (va0, va1, va2)
    dmod = jnp.stack([v[2:5] for v in vas])
    part = {"b_ada": dmod, "norm_pre": jnp.stack([v[0] for v in vas]), "norm_post": jnp.stack([v[1] for v in vas]),
            "rel_bias": bias_grad(db.reshape(8, 128, WIN), "bias_grad")[:, :257], "conv_w": v_lru[0:4], "conv_b": v_lru[4],
            "lru_wa": _diag_blocks(dwa_bd), "lru_ba": v_lru[5], "lru_wx": _diag_blocks(dwx_bd), "lru_bx": v_lru[6],
            "lru_lambda": v_lru[7]}
    full_shapes = {"b_ada": (9 * D,), "norm_pre": (3, D), "norm_post": (3, D), "rel_bias": (8, 257),
                   "conv_w": (4, D), "conv_b": (D,), "lru_wa": (16, 64, 64), "lru_ba": (D,),
                   "lru_wx": (16, 64, 64), "lru_bx": (D,), "lru_lambda": (D,)}
    g3 = ag_small(_pack([part[n] for n in SMALL] + [lvec[0:1, 0:1]], 1232), "ag_small_grads")
    summed = _unpack(sum_lead(g3, "sum_small_grads"), [full_shapes[n] for n in SMALL] + [(1,)])
    red = dict(zip(SMALL, summed[:-1]))
    loss = summed[-1][0]
    cols = lambda a: lax.dynamic_slice(a, (0, p * 256), (a.shape[0], 256))
    grads.update({"b_ada": red["b_ada"][None], "norm_pre": cols(red["norm_pre"])[None],
                  "norm_post": cols(red["norm_post"])[None], "rel_bias": red["rel_bias"][None],
                  "conv_w": cols(red["conv_w"])[None], "conv_b": red["conv_b"][None], "lru_wa": red["lru_wa"][None],
                  "lru_ba": red["lru_ba"][None], "lru_wx": red["lru_wx"][None], "lru_bx": red["lru_bx"][None],
                  "lru_lambda": red["lru_lambda"][None]})

    dmod_all = g3[:, :72].reshape(8, 9 * D)
    dmod_cols = jnp.pad(lax.dynamic_slice(dmod_all, (0, p * 2304), (8, 2304)), ((0, 120), (0, 0)))
    c_all_t = jnp.pad(c_all.T, ((0, 0), (0, 120)))
    grads["w_ada"] = ada_bwd(c_all_t, dmod_cols, "ada_bwd")[None]

    delta, new_m, new_v = {}, {}, {}

    def update(n):
        shp = W[n].shape
        res = adamw(W[n][0], grads[n][0], M[n][0], V[n][0], "adamw_" + n, emit_g=n in geo)
        delta[n], new_m[n], new_v[n] = [a.reshape(shp) for a in res[:3]]
        if n in geo:
            grads[n] = res[3].reshape(shp)

    for n in ("w_ada", "ffn2_w_gu", "ffn2_w_down", "w_in", "w_att_o", "w_rec_o", "w_out"):
        update(n)
    packed = [_pack([src[n] for n in SMALL], 1168) for src in (W, grads, M, V)]
    outs = adamw(*packed, "adamw_small")
    for dst, blk in zip((delta, new_m, new_v), outs):
        for n, a in zip(SMALL, _unpack(blk, [W[n].shape for n in SMALL])):
            dst[n] = a
    reduce_end(("ffn1_w_gu", "ffn1_w_down"), fly_ffn1,
               [outs[0], delta["w_ada"], delta["ffn2_w_gu"], delta["ffn2_w_down"], delta["w_in"], delta["w_out"]], "ffn1")
    for n in ("ffn1_w_gu", "ffn1_w_down"):
        update(n)

    return (loss, dx0[None], *[grads[n] for n in WEIGHTS], *[delta[n] for n in WEIGHTS],
            *[new_m[n] for n in WEIGHTS], *[new_v[n] for n in WEIGHTS])
```

```python
import functools

import numpy as np
import jax
import jax.numpy as jnp
from jax import lax
from jax.experimental import pallas as pl
from jax.experimental.pallas import tpu as pltpu

F32 = jnp.float32
BF16 = jnp.bfloat16

D = 1024
FF = 2816
PW = 5632
HP = 128
CHUNK = 64
WIN = 640
TQ = 512
EPS = 1e-6
NEG = -1e30
LRU_C = 8.0
N_DEV = 8
VMEM_LIMIT = 56 * 1024 * 1024

ADAM_LR, ADAM_B1, ADAM_B2, ADAM_EPS, ADAM_WD, ADAM_STEP = 0.001, 0.9, 0.999, 1e-08, 0.01, 10

MESH = pl.DeviceIdType.MESH
ANY = pl.BlockSpec(memory_space=pl.ANY)


def _cp(*sem):
    return pltpu.CompilerParams(dimension_semantics=tuple(sem), vmem_limit_bytes=VMEM_LIMIT)


def _dot(a, b):
    return jnp.dot(a, b, preferred_element_type=F32)


def _dot_nt(a, b):
    return lax.dot_general(a, b, (((1,), (1,)), ((), ())), preferred_element_type=F32)


def _dot_tn(a, b):
    return lax.dot_general(a, b, (((0,), (0,)), ((), ())), preferred_element_type=F32)


def _mean(v):
    return jnp.mean(v, axis=-1, keepdims=True)


def _colsum(v):
    return jnp.sum(v, axis=0, keepdims=True)


def _sigmoid(v):
    return 0.5 * jnp.tanh(0.5 * v) + 0.5


_GK = 0.7978845608028654


def _gelu(v):
    t = jnp.tanh(_GK * (v + 0.044715 * v * v * v))
    return 0.5 * v * (1.0 + t)


def _pre_norm(xv, vec_ref):
    r = lax.rsqrt(_mean(xv * xv) + EPS)
    n = xv * r * vec_ref[0:1, :]
    return n * (1.0 + vec_ref[3:4, :]) + vec_ref[2:3, :]


def _pre_norm_bwd(dh, xv, dres, vec_ref, vacc_ref):
    r = lax.rsqrt(_mean(xv * xv) + EPS)
    xh = xv * r
    n = xh * vec_ref[0:1, :]
    vacc_ref[2:3, :] += _colsum(dh)
    vacc_ref[3:4, :] += _colsum(dh * n)
    dn = dh * (1.0 + vec_ref[3:4, :])
    vacc_ref[0:1, :] += _colsum(dn * xh)
    dxh = dn * vec_ref[0:1, :]
    return r * (dxh - xh * _mean(dxh * xh)) + dres


def _post_norm_bwd(dxo, fv, res, vec_ref, vacc_ref):
    rf = lax.rsqrt(_mean(fv * fv) + EPS)
    fh = fv * rf
    gp = vec_ref[1:2, :]
    vacc_ref[4:5, :] += _colsum(res * dxo * (fh * gp))
    dy = (res * vec_ref[4:5, :]) * dxo
    vacc_ref[1:2, :] += _colsum(dy * fh)
    dfn = dy * gp
    return rf * (dfn - fh * _mean(dfn * fh))


def _u_spec(tf):
    return pl.BlockSpec((pl.Element(D), pl.Element(tf)),
                        lambda i, j: (0, pl.multiple_of(jnp.minimum(FF + j * tf, 2 * FF - tf), 128)))


def ffn_fwd(x, vec, w_gu, w_dn, res, name, tgt=None, tm=1024, tf=512):
    S = x.shape[0]
    tm = min(tm, S)
    nt = S // tm
    nf = -(-FF // tf)
    tail = FF - tf * (nf - 1)
    halves = [pl.ds(r * (tm // 2), tm // 2) for r in range(2)]
    head = tgt is not None

    def body(*refs):
        x_ref, vec_ref, wg_ref, wu_ref, wd_ref = refs[:5]
        if head:
            t_hbm, xo_ref, h_ref, g_ref, u_ref, a_ref, f_ref, l_ref, hs, acc, lacc, ts, tsem = refs[5:]
        else:
            xo_ref, h_ref, g_ref, u_ref, a_ref, f_ref, hs, acc = refs[5:]
        i, j = pl.program_id(0), pl.program_id(1)
        if head:
            late = pltpu.make_async_copy(t_hbm.at[pl.ds(pl.multiple_of(i * tm, 8), tm)], ts, tsem)

        @pl.when(j == 0)
        def _():
            if head:
                late.start()
            h = _pre_norm(x_ref[...], vec_ref).astype(BF16)
            hs[...] = h
            h_ref[...] = h
            acc[...] = jnp.zeros_like(acc)

        def chunk(w):
            gu = [(_dot(hs[r, :], wg_ref[:, 0:w]), _dot(hs[r, :], wu_ref[:, tf - w:tf])) for r in halves]
            acts = []
            for r, (g, u) in zip(halves, gu):
                g_ref[r, 0:w] = g.astype(BF16)
                u_ref[r, 0:w] = u.astype(BF16)
                a = (g * _sigmoid(g) * u).astype(BF16)
                a_ref[r, 0:w] = a
                acts.append(a)
            for r, a in zip(halves, acts):
                acc[r, :] += _dot(a, wd_ref[0:w, :])

        @pl.when(j < nf - 1)
        def _():
            chunk(tf)

        @pl.when(j == nf - 1)
        def _():
            chunk(tail)
            f = acc[...]
            f_ref[...] = f.astype(BF16)
            y = f * lax.rsqrt(_mean(f * f) + EPS) * vec_ref[1:2, :]
            xo = x_ref[...] + (res * vec_ref[4:5, :]) * y
            if head:
                @pl.when(i == 0)
                def _():
                    lacc[...] = jnp.zeros_like(lacc)

                late.wait()
                d = xo - ts[...]
                xo_ref[...] = d * (1.0 / D)
                lacc[...] += _colsum(d * d)

                @pl.when(i == nt - 1)
                def _():
                    l_ref[...] = jnp.broadcast_to(0.5 * jnp.sum(lacc[...]) * (1.0 / D), (8, 128))
            else:
                xo_ref[...] = xo

    row = lambda i, j: (i, 0)
    col = lambda i, j: (i, j)
    in_specs = [pl.BlockSpec((tm, D), row), pl.BlockSpec((8, D), lambda i, j: (0, 0)),
                pl.BlockSpec((D, tf), lambda i, j: (0, j)), _u_spec(tf),
                pl.BlockSpec((tf, D), lambda i, j: (j, 0))]
    out_specs = [pl.BlockSpec((tm, D), row), pl.BlockSpec((tm, D), row), pl.BlockSpec((tm, tf), col),
                 pl.BlockSpec((tm, tf), col), pl.BlockSpec((tm, tf), col), pl.BlockSpec((tm, D), row)]
    out_shape = [jax.ShapeDtypeStruct((S, D), F32), jax.ShapeDtypeStruct((S, D), BF16),
                 jax.ShapeDtypeStruct((S, FF), BF16), jax.ShapeDtypeStruct((S, FF), BF16),
                 jax.ShapeDtypeStruct((S, FF), BF16), jax.ShapeDtypeStruct((S, D), BF16)]
    scratch = [pltpu.VMEM((tm, D), BF16), pltpu.VMEM((tm, D), F32)]
    args = [x, vec, w_gu, w_gu, w_dn]
    if head:
        in_specs.append(ANY)
        out_specs.append(pl.BlockSpec((8, 128), lambda i, j: (0, 0)))
        out_shape.append(jax.ShapeDtypeStruct((8, 128), F32))
        scratch += [pltpu.VMEM((1, D), F32), pltpu.VMEM((tm, D), F32), pltpu.SemaphoreType.DMA]
        args.append(tgt)
    return pl.pallas_call(
        body, name=name, grid=(nt, nf), in_specs=in_specs, out_specs=out_specs, out_shape=out_shape,
        scratch_shapes=scratch,
        compiler_params=_cp("arbitrary" if head else "parallel", "arbitrary"),
    )(*args)


def ffn_bwd(dxo, x, f, g, u, vec, w_gu, w_dn, res, name, tm=1024, tf=512):
    S = x.shape[0]
    tm = min(tm, S)
    nf = -(-FF // tf)
    tail = FF - tf * (nf - 1)
    halves = [pl.ds(r * (tm // 2), tm // 2) for r in range(2)]

    def body(dxo_ref, x_hbm, f_ref, g_ref, u_ref, vec_ref, wg_ref, wu_ref, wd_ref,
             dx_ref, df_ref, dgu_ref, vacc_ref, dfs, acc, xs, xsem):
        i, j = pl.program_id(0), pl.program_id(1)
        late = pltpu.make_async_copy(x_hbm.at[pl.ds(pl.multiple_of(i * tm, 8), tm)], xs, xsem)

        @pl.when((i == 0) & (j == 0))
        def _():
            vacc_ref[...] = jnp.zeros_like(vacc_ref)

        @pl.when(j == 0)
        def _():
            late.start()
            df = _post_norm_bwd(dxo_ref[...], f_ref[...].astype(F32), res, vec_ref, vacc_ref).astype(BF16)
            dfs[...] = df
            df_ref[...] = df
            acc[...] = jnp.zeros_like(acc)

        def chunk(w):
            da = [_dot_nt(dfs[h, :], wd_ref[0:w, :]) for h in halves]
            dgu = []
            for h, d in zip(halves, da):
                gv, uv = g_ref[h, 0:w].astype(F32), u_ref[h, 0:w].astype(F32)
                sg = _sigmoid(gv)
                dg = (d * uv * (sg * (1.0 + gv * (1.0 - sg)))).astype(BF16)
                du = (d * (gv * sg)).astype(BF16)
                dgu_ref[0, h, 0:w] = dg
                dgu_ref[1, h, 0:w] = du
                dgu.append((dg, du))
            for h, (dg, du) in zip(halves, dgu):
                acc[h, :] += _dot_nt(dg, wg_ref[:, 0:w]) + _dot_nt(du, wu_ref[:, tf - w:tf])

        @pl.when(j < nf - 1)
        def _():
            chunk(tf)

        @pl.when(j == nf - 1)
        def _():
            chunk(tail)
            late.wait()
            dx_ref[...] = _pre_norm_bwd(acc[...], xs[...], dxo_ref[...], vec_ref, vacc_ref)

    row = lambda i, j: (i, 0)
    col = lambda i, j: (i, j)
    return pl.pallas_call(
        body, name=name, grid=(S // tm, nf),
        in_specs=[pl.BlockSpec((tm, D), row), ANY, pl.BlockSpec((tm, D), row),
                  pl.BlockSpec((tm, tf), col), pl.BlockSpec((tm, tf), col),
                  pl.BlockSpec((8, D), lambda i, j: (0, 0)),
                  pl.BlockSpec((D, tf), lambda i, j: (0, j)), _u_spec(tf),
                  pl.BlockSpec((tf, D), lambda i, j: (j, 0))],
        out_specs=[pl.BlockSpec((tm, D), row), pl.BlockSpec((tm, D), row),
                   pl.BlockSpec((2, tm, tf), lambda i, j: (0, i, j)),
                   pl.BlockSpec((8, D), lambda i, j: (0, 0))],
        out_shape=[jax.ShapeDtypeStruct((S, D), F32), jax.ShapeDtypeStruct((S, D), BF16),
                   jax.ShapeDtypeStruct((2, S, FF), BF16), jax.ShapeDtypeStruct((8, D), F32)],
        scratch_shapes=[pltpu.VMEM((tm, D), BF16), pltpu.VMEM((tm, D), F32), pltpu.VMEM((tm, D), F32),
                        pltpu.SemaphoreType.DMA],
        compiler_params=_cp("arbitrary", "arbitrary"),
    )(dxo, x, f, g, u, vec, w_gu, w_gu, w_dn)


def mm_tn(a, b, name, tm, tn, tk, out_dtype=BF16, a_resident=False):
    S, M = a.shape
    if b.ndim == 3:
        G, _, Nf = b.shape
    else:
        G, Nf = 1, b.shape[1]
    N = G * Nf
    tk = min(tk, S)
    nbf = Nf // tn
    nk = S // tk

    def body(a_ref, b_ref, o_ref, acc):
        k = pl.program_id(2)

        @pl.when(k == 0)
        def _():
            acc[...] = jnp.zeros_like(acc)

        a_blk = a_ref[pl.ds(pl.multiple_of(k * tk, 16), tk), :] if a_resident else a_ref[...]
        acc[...] += _dot_tn(a_blk, b_ref[...])

        @pl.when(k == nk - 1)
        def _():
            o_ref[...] = acc[...].astype(out_dtype)

    if b.ndim == 3:
        b_spec = pl.BlockSpec((None, tk, tn), lambda i, j, k: (j // nbf, k, j % nbf))
    else:
        b_spec = pl.BlockSpec((tk, tn), lambda i, j, k: (k, j))
    if a_resident:
        a_spec = pl.BlockSpec((S, M), lambda i, j, k: (0, 0), pipeline_mode=pl.Buffered(1))
    else:
        a_spec = pl.BlockSpec((tk, tm), lambda i, j, k: (k, i))
    return pl.pallas_call(
        body, name=name, grid=(M // tm, N // tn, nk),
        in_specs=[a_spec, b_spec],
        out_specs=pl.BlockSpec((tm, tn), lambda i, j, k: (i, j)),
        out_shape=jax.ShapeDtypeStruct((M, N), out_dtype),
        scratch_shapes=[pltpu.VMEM((tm, tn), F32)],
        compiler_params=_cp("parallel", "parallel", "arbitrary"),
    )(a, b)


def proj_fwd(x, vec, w_in, name, tm=2048, tn=512):
    S = x.shape[0]
    tm = min(tm, S)
    nq = 1536 // tn

    def body(x_ref, vec_ref, w_ref, h_ref, qkv_ref, rest_ref, hs):
        j = pl.program_id(1)

        @pl.when(j == 0)
        def _():
            h = _pre_norm(x_ref[...], vec_ref).astype(BF16)
            hs[...] = h
            h_ref[...] = h

        r = _dot(hs[...], w_ref[...])

        @pl.when(j < nq)
        def _():
            qkv_ref[...] = r.astype(BF16)

        @pl.when(j >= nq)
        def _():
            rest_ref[...] = r.astype(BF16)

    row = lambda i, j: (i, 0)
    return pl.pallas_call(
        body, name=name, grid=(S // tm, PW // tn),
        in_specs=[pl.BlockSpec((tm, D), row), pl.BlockSpec((8, D), lambda i, j: (0, 0)),
                  pl.BlockSpec((D, tn), lambda i, j: (0, j))],
        out_specs=[pl.BlockSpec((tm, D), row),
                   pl.BlockSpec((tm, tn), lambda i, j: (i, jnp.minimum(j, nq - 1))),
                   pl.BlockSpec((tm, tn), lambda i, j: (i, jnp.maximum(j - nq, 0)))],
        out_shape=[jax.ShapeDtypeStruct((S, D), BF16), jax.ShapeDtypeStruct((S, 1536), BF16),
                   jax.ShapeDtypeStruct((S, 4096), BF16)],
        scratch_shapes=[pltpu.VMEM((tm, D), BF16)],
        compiler_params=_cp("parallel", "arbitrary"),
    )(x, vec, w_in)


def proj_bwd(dq, dkv, dxr, d3, w_in, x, dxo, vec, name, tm=2048, tk=512):
    S = x.shape[0]
    tm = min(tm, S)
    nk = PW // tk

    def body(dq_ref, dkv_ref, dxr_ref, d3_ref, w_ref, x_hbm, dxo_hbm, vec_ref, dx_ref, vacc_ref, acc, xs, dxos, sems):
        i, j = pl.program_id(0), pl.program_id(1)
        tok = pl.ds(pl.multiple_of(i * tm, 8), tm)
        late = (pltpu.make_async_copy(x_hbm.at[tok], xs, sems.at[0]),
                pltpu.make_async_copy(dxo_hbm.at[tok], dxos, sems.at[1]))

        @pl.when((i == 0) & (j == 0))
        def _():
            vacc_ref[...] = jnp.zeros_like(vacc_ref)

        @pl.when(j == 0)
        def _():
            for cp in late:
                cp.start()
            acc[...] = _dot_nt(dq_ref[...], w_ref[...])

        @pl.when((j >= 1) & (j < 3))
        def _():
            acc[...] += _dot_nt(dkv_ref[...], w_ref[...])

        @pl.when((j >= 3) & (j < 5))
        def _():
            acc[...] += _dot_nt(dxr_ref[...], w_ref[...])

        @pl.when(j >= 5)
        def _():
            acc[...] += _dot_nt(d3_ref[...], w_ref[...])

        @pl.when(j == nk - 1)
        def _():
            for cp in late:
                cp.wait()
            dx_ref[...] = _pre_norm_bwd(acc[...], xs[...], dxos[...], vec_ref, vacc_ref)

    row = lambda i, j: (i, 0)
    return pl.pallas_call(
        body, name=name, grid=(S // tm, nk),
        in_specs=[pl.BlockSpec((None, tm, tk), lambda i, j: (0, i, 0)),
                  pl.BlockSpec((None, tm, tk), lambda i, j: (jnp.clip(j - 1, 0, 1), i, 0)),
                  pl.BlockSpec((tm, tk), lambda i, j: (i, jnp.clip(j - 3, 0, 1))),
                  pl.BlockSpec((None, tm, tk), lambda i, j: (jnp.clip(j - 5, 0, 5) // 2, i, jnp.clip(j - 5, 0, 5) % 2)),
                  pl.BlockSpec((D, tk), lambda i, j: (0, j)),
                  ANY, ANY,
                  pl.BlockSpec((8, D), lambda i, j: (0, 0))],
        out_specs=[pl.BlockSpec((tm, D), row, pipeline_mode=pl.Buffered(1)),
                   pl.BlockSpec((8, D), lambda i, j: (0, 0))],
        out_shape=[jax.ShapeDtypeStruct((S, D), F32), jax.ShapeDtypeStruct((8, D), F32)],
        scratch_shapes=[pltpu.VMEM((tm, D), F32), pltpu.VMEM((tm, D), F32), pltpu.VMEM((tm, D), F32),
                        pltpu.SemaphoreType.DMA((2,))],
        compiler_params=_cp("arbitrary", "arbitrary"),
    )(dq, dkv, dxr, d3, w_in, x, dxo, vec)


def _two_heads(v, lane):
    zero = jnp.zeros((), v.dtype)
    return jnp.concatenate([jnp.where(lane < 64, v, zero), jnp.where(lane >= 64, v, zero)], axis=0)


def _attn_scores(qm, ka, bias_h, i, grp):
    s = _dot_nt(qm, ka) + bias_h
    col = lax.broadcasted_iota(jnp.int32, s.shape, 1)
    first_key = jnp.where(i == 0, 512 - 128 * grp, 0)
    return jnp.where(col >= first_key, s, NEG)


def _softmax(s):
    e = jnp.exp(s - jnp.max(s, axis=-1, keepdims=True))
    return e * (1.0 / jnp.sum(e, axis=-1, keepdims=True))


NG = TQ // 128


def attn_fwd(qkv, bias, name):
    S = qkv.shape[0]
    nb = S // TQ

    def body(q_ref, kp_ref, kc_ref, vp_ref, vc_ref, b_ref, o_ref, kw, vw):
        i = pl.program_id(1)
        kw[0:TQ, :] = kp_ref[...]
        kw[TQ:2 * TQ, :] = kc_ref[...]
        vw[0:TQ, :] = vp_ref[...]
        vw[TQ:2 * TQ, :] = vc_ref[...]
        lane = lax.broadcasted_iota(jnp.int32, (1, HP), 1)

        rows = [pl.ds(128 * a, 128) for a in range(NG)]
        keys = [pl.ds(128 * a, WIN) for a in range(NG)]
        q2 = [_two_heads(q_ref[r, :] * jnp.asarray(0.125, BF16), lane) for r in rows]
        s = [_attn_scores(q2[a], kw[keys[a], :], b_ref[...], i, a) for a in range(NG)]
        p = [_softmax(sa).astype(BF16) for sa in s]
        o2 = [_dot(p[a], vw[keys[a], :]) for a in range(NG)]
        for a in range(NG):
            o_ref[rows[a], :] = jnp.where(lane < 64, o2[a][0:128], o2[a][128:256]).astype(BF16)

    prev = lambda h, i: (jnp.maximum(i - 1, 0), 0)
    return pl.pallas_call(
        body, name=name, grid=(4, nb),
        in_specs=[pl.BlockSpec((TQ, HP), lambda h, i: (i, h)),
                  pl.BlockSpec((TQ, HP), lambda h, i: (jnp.maximum(i - 1, 0), 4 + h)),
                  pl.BlockSpec((TQ, HP), lambda h, i: (i, 4 + h)),
                  pl.BlockSpec((TQ, HP), lambda h, i: (jnp.maximum(i - 1, 0), 8 + h)),
                  pl.BlockSpec((TQ, HP), lambda h, i: (i, 8 + h)),
                  pl.BlockSpec((None, 256, WIN), lambda h, i: (h, 0, 0))],
        out_specs=pl.BlockSpec((TQ, HP), lambda h, i: (i, h)),
        out_shape=jax.ShapeDtypeStruct((S, 512), BF16),
        scratch_shapes=[pltpu.VMEM((2 * TQ, HP), BF16), pltpu.VMEM((2 * TQ, HP), BF16)],
        compiler_params=_cp("parallel", "arbitrary"),
    )(qkv, qkv, qkv, qkv, qkv, bias)


def attn_bwd(qkv, do, bias, name):
    S = qkv.shape[0]
    nb = S // TQ

    def body(q_ref, kp_ref, kc_ref, vp_ref, vc_ref, do_ref, b_ref, dqkv_ref, db_ref, dkv_ref, kw, vw, ak, av):
        i = pl.program_id(1)

        @pl.when(i == 0)
        def _():
            db_ref[...] = jnp.zeros_like(db_ref)
            ak[...] = jnp.zeros_like(ak)
            av[...] = jnp.zeros_like(av)

        @pl.when(i > 0)
        def _():
            ak[0:TQ, :] = ak[TQ:2 * TQ, :]
            av[0:TQ, :] = av[TQ:2 * TQ, :]
            ak[TQ:2 * TQ, :] = jnp.zeros((TQ, HP), F32)
            av[TQ:2 * TQ, :] = jnp.zeros((TQ, HP), F32)

        @pl.when(i < nb)
        def _():
            kw[0:TQ, :] = kp_ref[...]
            kw[TQ:2 * TQ, :] = kc_ref[...]
            vw[0:TQ, :] = vp_ref[...]
            vw[TQ:2 * TQ, :] = vc_ref[...]
            lane = lax.broadcasted_iota(jnp.int32, (1, HP), 1)

            rows = [pl.ds(128 * a, 128) for a in range(NG)]
            keys = [pl.ds(128 * a, WIN) for a in range(NG)]
            q2 = [_two_heads(q_ref[r, :] * jnp.asarray(0.125, BF16), lane) for r in rows]
            do2 = [_two_heads(do_ref[r, :], lane) for r in rows]
            s = [_attn_scores(q2[a], kw[keys[a], :], b_ref[...], i, a) for a in range(NG)]
            dp = [_dot_nt(do2[a], vw[keys[a], :]) for a in range(NG)]
            p = [_softmax(sa) for sa in s]
            ds = [p[a] * (dp[a] - jnp.sum(p[a] * dp[a], axis=-1, keepdims=True)) for a in range(NG)]
            db_ref[...] += (ds[0] + ds[1]) + (ds[2] + ds[3])
            dsb = [d.astype(BF16) for d in ds]
            dq2 = [_dot(dsb[a], kw[keys[a], :]) for a in range(NG)]
            dk = [_dot_tn(dsb[a], q2[a]) for a in range(NG)]
            dv = [_dot_tn(p[a].astype(BF16), do2[a]) for a in range(NG)]
            for a in range(NG):
                ak[keys[a], :] += dk[a]
                av[keys[a], :] += dv[a]
                dq = jnp.where(lane < 64, dq2[a][0:128], dq2[a][128:256])
                dqkv_ref[0, rows[a], :] = (dq * 0.125).astype(BF16)

        @pl.when(i > 0)
        def _():
            dkv_ref[0] = ak[0:TQ, :].astype(BF16)
            dkv_ref[1] = av[0:TQ, :].astype(BF16)

    cur = lambda i: jnp.minimum(i, nb - 1)
    prv = lambda i: jnp.clip(i - 1, 0, nb - 1)
    dq, db, dkv = pl.pallas_call(
        body, name=name, grid=(4, nb + 1),
        in_specs=[pl.BlockSpec((TQ, HP), lambda h, i: (cur(i), h)),
                  pl.BlockSpec((TQ, HP), lambda h, i: (prv(i), 4 + h)),
                  pl.BlockSpec((TQ, HP), lambda h, i: (cur(i), 4 + h)),
                  pl.BlockSpec((TQ, HP), lambda h, i: (prv(i), 8 + h)),
                  pl.BlockSpec((TQ, HP), lambda h, i: (cur(i), 8 + h)),
                  pl.BlockSpec((TQ, HP), lambda h, i: (cur(i), h)),
                  pl.BlockSpec((None, 256, WIN), lambda h, i: (h, 0, 0))],
        out_specs=[pl.BlockSpec((1, TQ, HP), lambda h, i: (0, cur(i), h)),
                   pl.BlockSpec((None, 256, WIN), lambda h, i: (h, 0, 0)),
                   pl.BlockSpec((2, TQ, HP), lambda h, i: (0, prv(i), h))],
        out_shape=[jax.ShapeDtypeStruct((1, S, 512), BF16), jax.ShapeDtypeStruct((4, 256, WIN), F32),
                   jax.ShapeDtypeStruct((2, S, 512), BF16)],
        scratch_shapes=[pltpu.VMEM((2 * TQ, HP), BF16), pltpu.VMEM((2 * TQ, HP), BF16),
                        pltpu.VMEM((2 * TQ, HP), F32), pltpu.VMEM((2 * TQ, HP), F32)],
        compiler_params=_cp("parallel", "arbitrary"),
    )(qkv, qkv, qkv, qkv, qkv, do, bias)
    return dq, db, dkv


def bias_grad(db, name):
    def body(db_ref, o_ref):
        r = lax.broadcasted_iota(jnp.int32, (128, 128), 0)
        c = lax.broadcasted_iota(jnp.int32, (128, 128), 1)
        flip = (r + c == 127).astype(BF16)
        lane = lax.broadcasted_iota(jnp.int32, (16, 384), 1)
        src = lax.broadcasted_iota(jnp.int32, (128, 384), 0)
        dst = lax.broadcasted_iota(jnp.int32, (128, 384), 1)

        def split_dot(v, m):
            hi = v.astype(BF16)
            r1 = v - hi.astype(F32)
            mid = r1.astype(BF16)
            lo = (r1 - mid.astype(F32)).astype(BF16)
            return _dot(hi, m) + _dot(mid, m) + _dot(lo, m)

        def diag_sums(w):
            y = pltpu.roll(split_dot(w, flip), 0, 1, stride=1, stride_axis=0)
            return jnp.broadcast_to(_colsum(y), (16, 128))

        w4 = db_ref[0, :, 512:640]
        w3 = db_ref[0, :, 384:512]
        far = jnp.sum(db_ref[0, :, 0:384]) + jnp.sum(jnp.where(r >= c, w3, 0.0))
        lo4 = diag_sums(jnp.where(r >= c, w4, 0.0))
        up4 = diag_sums(jnp.where(r < c, w4, 0.0))
        up3 = diag_sums(jnp.where(r < c, w3, 0.0))
        p_lo4 = (dst == 128 + (src + 1) % 128).astype(BF16)
        p_up4 = ((dst == src + 1) & (src < 127)).astype(BF16)
        p_up3 = ((dst == src + 129) & (src < 127)).astype(BF16)
        out = split_dot(lo4, p_lo4) + split_dot(up4, p_up4) + split_dot(up3, p_up3)
        o_ref[0] = out + jnp.where(lane == 256, far, 0.0)

    return pl.pallas_call(
        body, name=name, grid=(8,),
        in_specs=[pl.BlockSpec((1, 128, WIN), lambda h: (h, 0, 0))],
        out_specs=pl.BlockSpec((1, 16, 384), lambda h: (h, 0, 0)),
        out_shape=jax.ShapeDtypeStruct((8, 16, 384), F32),
        compiler_params=_cp("parallel"),
    )(db)[:, 0, :]


LT = 1024
LC = 512


def _lru_gates(xs, pv_ref, wa_ref, wx_ref, tl):
    xc = (pv_ref[4:5, :] + pv_ref[3:4, :] * xs[pl.ds(8, tl), :] + pv_ref[2:3, :] * xs[pl.ds(7, tl), :]
          + pv_ref[1:2, :] * xs[pl.ds(6, tl), :] + pv_ref[0:1, :] * xs[pl.ds(5, tl), :])
    xcb = xc.astype(BF16)
    pa = jnp.concatenate([_dot(xcb[:, 0:256], wa_ref[0]), _dot(xcb[:, 256:512], wa_ref[1])], axis=1)
    px = jnp.concatenate([_dot(xcb[:, 0:256], wx_ref[0]), _dot(xcb[:, 256:512], wx_ref[1])], axis=1)
    r = _sigmoid(pa + pv_ref[5:6, :])
    ig = _sigmoid(px + pv_ref[6:7, :])
    z = -pv_ref[7:8, :]
    sp = jnp.maximum(z, 0.0) + jnp.log1p(jnp.exp(-jnp.abs(z)))
    log_a = (-LRU_C * r) * sp
    a = jnp.exp(log_a)
    s = jnp.tanh(-log_a) * (1.0 + a * a)
    inv_mult = lax.rsqrt(s)
    mult = jnp.where(s > 0.0, s * inv_mult, 0.0)
    return xc, xcb, r, ig, sp, a, mult, inv_mult


def lru_fwd(rest, pvec, wa, wx, name):
    S = rest.shape[0]
    tl = min(LT, S)
    nt = S // tl

    def body(xr_ref, halo_ref, yr_ref, pv_ref, wa_ref, wx_ref, h_ref, hg_ref, xs, a_s, u_s, h_s, carry):
        ti = pl.program_id(1)

        @pl.when(ti == 0)
        def _():
            carry[...] = jnp.zeros_like(carry)

        xs[0:8, :] = jnp.where(ti > 0, halo_ref[8:16, :].astype(F32), 0.0)
        xs[pl.ds(8, tl), :] = xr_ref[...].astype(F32)
        xc, _, _, ig, _, a, mult, _ = _lru_gates(xs, pv_ref, wa_ref, wx_ref, tl)
        a_s[...] = a
        u_s[...] = mult * (ig * xc)
        row = lax.broadcasted_iota(jnp.int32, (8, LC), 0)

        def blk(bi, c):
            o = pl.multiple_of(bi * 8, 8)
            av = a_s[pl.ds(o, 8), :]
            bv = u_s[pl.ds(o, 8), :]
            for d in (1, 2, 4):
                a_sh = pltpu.roll(av, d, 0)
                b_sh = pltpu.roll(bv, d, 0)
                m = row >= d
                bv = jnp.where(m, av * b_sh + bv, bv)
                av = jnp.where(m, av * a_sh, av)
            hv = bv + av * c
            h_s[pl.ds(o, 8), :] = hv
            return hv[7:8, :]

        carry[...] = lax.fori_loop(0, tl // 8, blk, carry[...])
        h = h_s[...]
        h_ref[...] = h
        hg_ref[...] = (h * _gelu(yr_ref[...].astype(F32))).astype(BF16)

    hb = tl // 16
    return pl.pallas_call(
        body, name=name, grid=(2, nt),
        in_specs=[pl.BlockSpec((tl, LC), lambda c, t: (t, c)),
                  pl.BlockSpec((16, LC), lambda c, t: (jnp.maximum(t * hb - 1, 0), c)),
                  pl.BlockSpec((tl, LC), lambda c, t: (t, 2 + c)),
                  pl.BlockSpec((8, LC), lambda c, t: (0, c)),
                  pl.BlockSpec((2, 256, 256), lambda c, t: (c, 0, 0)),
                  pl.BlockSpec((2, 256, 256), lambda c, t: (c, 0, 0))],
        out_specs=[pl.BlockSpec((tl, LC), lambda c, t: (t, c)), pl.BlockSpec((tl, LC), lambda c, t: (t, c))],
        out_shape=[jax.ShapeDtypeStruct((S, D), F32), jax.ShapeDtypeStruct((S, D), BF16)],
        scratch_shapes=[pltpu.VMEM((tl + 8, LC), F32), pltpu.VMEM((tl, LC), F32), pltpu.VMEM((tl, LC), F32),
                        pltpu.VMEM((tl, LC), F32), pltpu.VMEM((1, LC), F32)],
        compiler_params=_cp("parallel", "arbitrary"),
    )(rest, rest, rest, pvec, wa, wx)


def lru_bwd(dh, h, rest, pvec, wa, wx, name):
    S = rest.shape[0]
    tl = min(LT, S)
    nt = S // tl

    def body(dh_ref, h_ref, hhalo_ref, xr_ref, xhalo_ref, pv_ref, wa_ref, wx_ref,
             dxr_ref, vacc_ref, dwa_ref, dwx_ref,
             xs, hs, a_s, ash_s, b_s, lam_s, dxe, anext, lnext, dxnext):
        ti = pl.program_id(1)
        tr = nt - 1 - ti

        @pl.when(ti == 0)
        def _():
            anext[...] = jnp.zeros_like(anext)
            lnext[...] = jnp.zeros_like(lnext)
            dxnext[...] = jnp.zeros_like(dxnext)
            vacc_ref[...] = jnp.zeros_like(vacc_ref)
            dwa_ref[...] = jnp.zeros_like(dwa_ref)
            dwx_ref[...] = jnp.zeros_like(dwx_ref)

        xs[0:8, :] = jnp.where(tr > 0, xhalo_ref[8:16, :].astype(F32), 0.0)
        xs[pl.ds(8, tl), :] = xr_ref[...].astype(F32)
        xc, xcb, r, ig, sp, a, mult, inv_mult = _lru_gates(xs, pv_ref, wa_ref, wx_ref, tl)

        a_s[pl.ds(0, tl), :] = a
        a_s[pl.ds(tl, 8), :] = jnp.broadcast_to(anext[...], (8, LC))
        ash_s[...] = a_s[pl.ds(1, tl), :]
        b_s[...] = dh_ref[...]
        row = lax.broadcasted_iota(jnp.int32, (8, LC), 0)

        def blk(k, c):
            o = pl.multiple_of((tl // 8 - 1 - k) * 8, 8)
            av = ash_s[pl.ds(o, 8), :]
            bv = b_s[pl.ds(o, 8), :]
            for d in (1, 2, 4):
                a_sh = pltpu.roll(av, 8 - d, 0)
                b_sh = pltpu.roll(bv, 8 - d, 0)
                m = row < 8 - d
                bv = jnp.where(m, bv + av * b_sh, bv)
                av = jnp.where(m, av * a_sh, av)
            lv = bv + av * c
            lam_s[pl.ds(o, 8), :] = lv
            return lv[0:1, :]

        lnext[...] = lax.fori_loop(0, tl // 8, blk, lnext[...])
        anext[...] = a[0:1, :]
        lam = lam_s[...]

        hs[0:8, :] = jnp.where(tr > 0, hhalo_ref[...], 0.0)
        hs[pl.ds(8, tl), :] = h_ref[...]
        d_a = lam * hs[pl.ds(7, tl), :]
        d_mult = lam * (ig * xc)
        d_ig = lam * mult * xc
        dxc = lam * mult * ig
        d_log_a = d_a * a - d_mult * (a * a) * inv_mult
        d_r = d_log_a * (-LRU_C * sp)
        vacc_ref[7:8, :] += _colsum(d_log_a * (-LRU_C * r)) * (-_sigmoid(-pv_ref[7:8, :]))
        d_pa = d_r * r * (1.0 - r)
        d_px = d_ig * ig * (1.0 - ig)
        vacc_ref[5:6, :] += _colsum(d_pa)
        vacc_ref[6:7, :] += _colsum(d_px)
        dpa = d_pa.astype(BF16)
        dpx = d_px.astype(BF16)
        back = []
        for g in range(2):
            sl = slice(256 * g, 256 * g + 256)
            dwa_ref[g] += _dot_tn(xcb[:, sl], dpa[:, sl])
            dwx_ref[g] += _dot_tn(xcb[:, sl], dpx[:, sl])
            back.append(_dot_nt(dpa[:, sl], wa_ref[g]) + _dot_nt(dpx[:, sl], wx_ref[g]))
        dxc = dxc + jnp.concatenate(back, axis=1)
        vacc_ref[4:5, :] += _colsum(dxc)
        for k in range(4):
            vacc_ref[k:k + 1, :] += _colsum(dxc * xs[pl.ds(5 + k, tl), :])
        dxe[pl.ds(0, tl), :] = dxc
        dxe[pl.ds(tl, 8), :] = dxnext[...]
        dxr = (pv_ref[3:4, :] * dxc + pv_ref[2:3, :] * dxe[pl.ds(1, tl), :]
               + pv_ref[1:2, :] * dxe[pl.ds(2, tl), :] + pv_ref[0:1, :] * dxe[pl.ds(3, tl), :])
        dxr_ref[...] = dxr.astype(BF16)
        dxnext[...] = dxc[0:8, :]

    hb = tl // 8
    rev = lambda t: nt - 1 - t
    halo = lambda t: jnp.maximum(rev(t) * hb - 1, 0)
    big = lambda: pltpu.VMEM((tl + 8, LC), F32)
    til = lambda: pltpu.VMEM((tl, LC), F32)
    return pl.pallas_call(
        body, name=name, grid=(2, nt),
        in_specs=[pl.BlockSpec((tl, LC), lambda c, t: (rev(t), c)),
                  pl.BlockSpec((tl, LC), lambda c, t: (rev(t), c)),
                  pl.BlockSpec((8, LC), lambda c, t: (halo(t), c)),
                  pl.BlockSpec((tl, LC), lambda c, t: (rev(t), c)),
                  pl.BlockSpec((16, LC), lambda c, t: (jnp.maximum(rev(t) * (tl // 16) - 1, 0), c)),
                  pl.BlockSpec((8, LC), lambda c, t: (0, c)),
                  pl.BlockSpec((2, 256, 256), lambda c, t: (c, 0, 0)),
                  pl.BlockSpec((2, 256, 256), lambda c, t: (c, 0, 0))],
        out_specs=[pl.BlockSpec((tl, LC), lambda c, t: (rev(t), c)),
                   pl.BlockSpec((8, LC), lambda c, t: (0, c)),
                   pl.BlockSpec((2, 256, 256), lambda c, t: (c, 0, 0)),
                   pl.BlockSpec((2, 256, 256), lambda c, t: (c, 0, 0))],
        out_shape=[jax.ShapeDtypeStruct((S, D), BF16), jax.ShapeDtypeStruct((8, D), F32),
                   jax.ShapeDtypeStruct((4, 256, 256), F32), jax.ShapeDtypeStruct((4, 256, 256), F32)],
        scratch_shapes=[big(), big(), big(), til(), til(), til(), big(),
                        pltpu.VMEM((1, LC), F32), pltpu.VMEM((1, LC), F32), pltpu.VMEM((8, LC), F32)],
        compiler_params=_cp("parallel", "arbitrary"),
    )(dh, h, h, rest, rest, pvec, wa, wx)


def mix_out_fwd(x, ao, hg, rest, vec, w_att_o, w_rec_o, w_out, name, tm=512):
    S = x.shape[0]
    tm = min(tm, S)

    def body(x_ref, ao_ref, hg_ref, ga_ref, gr_ref, vec_ref, wa_ref, wr_ref, wo_ref,
             xo_ref, att_ref, rec_ref, mg_ref, f_ref):
        att = _dot(ao_ref[...], wa_ref[...])
        rec = _dot(hg_ref[...], wr_ref[...])
        att_ref[...] = att.astype(BF16)
        rec_ref[...] = rec.astype(BF16)
        mg = (_sigmoid(ga_ref[...].astype(F32)) * att + _sigmoid(gr_ref[...].astype(F32)) * rec).astype(BF16)
        mg_ref[...] = mg
        f = _dot(mg, wo_ref[...])
        f_ref[...] = f.astype(BF16)
        y = f * lax.rsqrt(_mean(f * f) + EPS) * vec_ref[1:2, :]
        xo_ref[...] = x_ref[...] + (1.0 * vec_ref[4:5, :]) * y

    row = lambda i: (i, 0)
    full = lambda r: pl.BlockSpec((r, D), lambda i: (0, 0))
    return pl.pallas_call(
        body, name=name, grid=(S // tm,),
        in_specs=[pl.BlockSpec((tm, D), row), pl.BlockSpec((tm, 512), row), pl.BlockSpec((tm, D), row),
                  pl.BlockSpec((tm, D), lambda i: (i, 2)), pl.BlockSpec((tm, D), lambda i: (i, 3)),
                  full(8), full(512), full(D), full(D)],
        out_specs=[pl.BlockSpec((tm, D), row)] * 5,
        out_shape=[jax.ShapeDtypeStruct((S, D), F32)] + [jax.ShapeDtypeStruct((S, D), BF16)] * 4,
        compiler_params=_cp("parallel"),
    )(x, ao, hg, rest, rest, vec, w_att_o, w_rec_o, w_out)


def mix_out_bwd(dxo, f, att, rec, rest, h, vec, w_att_o, w_rec_o, w_out, name, tm=512):
    S = dxo.shape[0]
    tm = min(tm, S)

    def body(dxo_ref, f_ref, att_ref, rec_ref, yr_ref, ga_ref, gr_ref, h_ref, vec_ref, wa_ref, wr_ref, wo_ref,
             df_ref, da_ref, dr_ref, dao_ref, dh_ref, d3_ref, vacc_ref):
        @pl.when(pl.program_id(0) == 0)
        def _():
            vacc_ref[...] = jnp.zeros_like(vacc_ref)

        df = _post_norm_bwd(dxo_ref[...], f_ref[...].astype(F32), 1.0, vec_ref, vacc_ref).astype(BF16)
        df_ref[...] = df
        dm = _dot_nt(df, wo_ref[...])
        sa = _sigmoid(ga_ref[...].astype(F32))
        sr = _sigmoid(gr_ref[...].astype(F32))
        d_att = (dm * sa).astype(BF16)
        d_rec = (dm * sr).astype(BF16)
        da_ref[...] = d_att
        dr_ref[...] = d_rec
        d3_ref[1] = (dm * att_ref[...].astype(F32) * (sa * (1.0 - sa))).astype(BF16)
        d3_ref[2] = (dm * rec_ref[...].astype(F32) * (sr * (1.0 - sr))).astype(BF16)
        dao_ref[...] = _dot_nt(d_att, wa_ref[...]).astype(BF16)
        d_hg = _dot_nt(d_rec, wr_ref[...])
        yr = yr_ref[...].astype(F32)
        t = jnp.tanh(_GK * (yr + 0.044715 * yr * yr * yr))
        dh_ref[...] = d_hg * (0.5 * yr * (1.0 + t))
        gelu_grad = 0.5 * (1.0 + t) + 0.5 * yr * (1.0 - t * t) * _GK * (1.0 + 3.0 * 0.044715 * yr * yr)
        d3_ref[0] = (d_hg * h_ref[...] * gelu_grad).astype(BF16)

    row = lambda i: (i, 0)
    full = lambda r: pl.BlockSpec((r, D), lambda i: (0, 0))
    return pl.pallas_call(
        body, name=name, grid=(S // tm,),
        in_specs=[pl.BlockSpec((tm, D), row)] * 4
        + [pl.BlockSpec((tm, D), lambda i: (i, 1)), pl.BlockSpec((tm, D), lambda i: (i, 2)),
           pl.BlockSpec((tm, D), lambda i: (i, 3)), pl.BlockSpec((tm, D), row),
           full(8), full(512), full(D), full(D)],
        out_specs=[pl.BlockSpec((tm, D), row)] * 3
        + [pl.BlockSpec((tm, 512), row), pl.BlockSpec((tm, D), row),
           pl.BlockSpec((3, tm, D), lambda i: (0, i, 0)), pl.BlockSpec((8, D), lambda i: (0, 0))],
        out_shape=[jax.ShapeDtypeStruct((S, D), BF16)] * 3
        + [jax.ShapeDtypeStruct((S, 512), BF16), jax.ShapeDtypeStruct((S, D), F32),
           jax.ShapeDtypeStruct((3, S, D), BF16), jax.ShapeDtypeStruct((8, D), F32)],
        compiler_params=_cp("arbitrary"),
    )(dxo, f, att, rec, rest, rest, rest, h, vec, w_att_o, w_rec_o, w_out)


def dw_in(h, dq, dkv, dxr, d3, name, tk=1024, tn=512):
    S = h.shape[0]
    tk = min(tk, S)
    nk = S // tk

    def body(h_ref, dq_ref, dkv_ref, dxr_ref, d3_ref, o_ref, acc):
        j, k = pl.program_id(0), pl.program_id(1)

        @pl.when(k == 0)
        def _():
            acc[...] = jnp.zeros_like(acc)

        @pl.when(j == 0)
        def _():
            acc[...] += _dot_tn(h_ref[pl.ds(pl.multiple_of(k * tk, 16), tk), :], dq_ref[...])

        @pl.when((j >= 1) & (j < 3))
        def _():
            acc[...] += _dot_tn(h_ref[pl.ds(pl.multiple_of(k * tk, 16), tk), :], dkv_ref[...])

        @pl.when((j >= 3) & (j < 5))
        def _():
            acc[...] += _dot_tn(h_ref[pl.ds(pl.multiple_of(k * tk, 16), tk), :], dxr_ref[...])

        @pl.when(j >= 5)
        def _():
            acc[...] += _dot_tn(h_ref[pl.ds(pl.multiple_of(k * tk, 16), tk), :], d3_ref[...])

        @pl.when(k == nk - 1)
        def _():
            o_ref[...] = acc[...].astype(BF16)

    use = lambda j, k, lo, hi: jnp.where((j >= lo) & (j < hi), k, 0)
    g3 = lambda j: jnp.clip(j - 5, 0, 5)
    return pl.pallas_call(
        body, name=name, grid=(PW // tn, nk),
        in_specs=[pl.BlockSpec((S, D), lambda j, k: (0, 0), pipeline_mode=pl.Buffered(1)),
                  pl.BlockSpec((None, tk, tn), lambda j, k: (0, use(j, k, 0, 1), 0)),
                  pl.BlockSpec((None, tk, tn), lambda j, k: (jnp.clip(j - 1, 0, 1), use(j, k, 1, 3), 0)),
                  pl.BlockSpec((tk, tn), lambda j, k: (use(j, k, 3, 5), jnp.clip(j - 3, 0, 1))),
                  pl.BlockSpec((None, tk, tn), lambda j, k: (g3(j) // 2, use(j, k, 5, 11), g3(j) % 2))],
        out_specs=pl.BlockSpec((D, tn), lambda j, k: (0, j)),
        out_shape=jax.ShapeDtypeStruct((D, PW), BF16),
        scratch_shapes=[pltpu.VMEM((D, tn), F32)],
        compiler_params=_cp("parallel", "arbitrary"),
    )(h, dq, dkv, dxr, d3)


def ada_fwd(c_all, w_ada, b_ada, name, tn=768):
    n = w_ada.shape[1]

    def body(c_ref, w_ref, b_ref, o_ref):
        cv = c_ref[...]
        ca = (cv * _sigmoid(cv)).astype(BF16)
        o_ref[...] = _dot(ca, w_ref[...].astype(BF16)) + b_ref[...]

    return pl.pallas_call(
        body, name=name, grid=(n // tn,),
        in_specs=[pl.BlockSpec((8, D), lambda j: (0, 0)), pl.BlockSpec((D, tn), lambda j: (0, j)),
                  pl.BlockSpec((1, tn), lambda j: (0, j))],
        out_specs=pl.BlockSpec((8, tn), lambda j: (0, j)),
        out_shape=jax.ShapeDtypeStruct((8, n), F32),
        compiler_params=_cp("parallel"),
    )(c_all, w_ada, b_ada)


def ada_bwd(c_all_t, dmod, name, tn=768):
    n = dmod.shape[1]

    def body(c_ref, d_ref, o_ref):
        cv = c_ref[...]
        ca = (cv * _sigmoid(cv)).astype(BF16)
        o_ref[...] = _dot(ca, d_ref[...].astype(BF16))

    return pl.pallas_call(
        body, name=name, grid=(n // tn,),
        in_specs=[pl.BlockSpec((D, 128), lambda j: (0, 0)), pl.BlockSpec((128, tn), lambda j: (0, j))],
        out_specs=pl.BlockSpec((D, tn), lambda j: (0, j)),
        out_shape=jax.ShapeDtypeStruct((D, n), F32),
        compiler_params=_cp("parallel"),
    )(c_all_t, dmod)


def _row_tile(rows, cols, itemsize=4, budget=1536 * 1024):
    best = None
    for t in range(8, rows + 1, 8):
        if rows % t == 0 and t * cols * itemsize <= budget:
            best = t
    return rows if best is None else best


def sum_lead(parts, name, out_dtype=F32):
    n, R, C = parts.shape
    tr = _row_tile(R, C * n)

    def body(p_ref, o_ref):
        acc = p_ref[0].astype(F32)
        for k in range(1, n):
            acc = acc + p_ref[k].astype(F32)
        o_ref[...] = acc.astype(out_dtype)

    return pl.pallas_call(
        body, name=name, grid=(R // tr,),
        in_specs=[pl.BlockSpec((n, tr, C), lambda i: (0, i, 0))],
        out_specs=pl.BlockSpec((tr, C), lambda i: (i, 0)),
        out_shape=jax.ShapeDtypeStruct((R, C), out_dtype),
        compiler_params=_cp("parallel"),
    )(parts)


def adamw(w, g, m, v, name, emit_g=False):
    R, C = w.shape
    tr = _row_tile(R, C * 8, budget=8 * 1024 * 1024)

    def body(w_ref, g_ref, m_ref, v_ref, d_ref, mo_ref, vo_ref, *go_ref):
        gv = g_ref[...]
        if emit_g:
            go_ref[0][...] = gv
        mn = ADAM_B1 * m_ref[...] + (1.0 - ADAM_B1) * gv
        vn = ADAM_B2 * v_ref[...] + (1.0 - ADAM_B2) * (gv * gv)
        m_hat = mn / (1.0 - ADAM_B1 ** ADAM_STEP)
        v_hat = vn / (1.0 - ADAM_B2 ** ADAM_STEP)
        d_ref[...] = -ADAM_LR * (m_hat / (jnp.sqrt(v_hat) + ADAM_EPS) + ADAM_WD * w_ref[...])
        mo_ref[...] = mn
        vo_ref[...] = vn

    spec = pl.BlockSpec((tr, C), lambda i: (i, 0))
    return pl.pallas_call(
        body, name=name, grid=(R // tr,),
        in_specs=[spec] * 4, out_specs=[spec] * (4 if emit_g else 3),
        out_shape=[jax.ShapeDtypeStruct((R, C), F32)] * (4 if emit_g else 3),
        compiler_params=_cp("parallel"),
    )(w, g, m, v)


def _mesh_pos():
    return lax.axis_index("x"), lax.axis_index("y"), lax.axis_index("c")


def _other_chips(mx, my):
    return [(1 - mx, my), (mx, 1 - my), (1 - mx, 1 - my)]


def ag_small(x, name):
    R = x.shape[0]

    def body(x_ref, out_ref, send_sems, recv_sems, local_sem):
        mx, my, mc = _mesh_pos()
        me, sibling = (mx, my, mc), (mx, my, 1 - mc)
        chips = _other_chips(mx, my)

        def slot(px, py, pc):
            return out_ref.at[4 * px + 2 * py + pc]

        def copy(k, block, to, src=None):
            return pltpu.make_async_remote_copy(
                src_ref=slot(*block) if src is None else src, dst_ref=slot(*block),
                send_sem=send_sems.at[k], recv_sem=recv_sems.at[k], device_id=to, device_id_type=MESH)

        mine = pltpu.make_async_copy(x_ref, slot(*me), local_sem)
        mine.start()
        first = [copy(0, me, sibling, src=x_ref)]
        first += [copy(1 + j, me, (*chip, mc), src=x_ref) for j, chip in enumerate(chips)]
        for cp in first:
            cp.start()
        passed = [copy(4 + j, (*chip, mc), sibling) for j, chip in enumerate(chips)]
        for j, chip in enumerate(chips):
            copy(1 + j, (*chip, mc), me).wait_recv()
            passed[j].start()
        copy(0, sibling, me).wait_recv()
        for j, chip in enumerate(chips):
            copy(4 + j, (*chip, 1 - mc), me).wait_recv()
        for cp in first + passed:
            cp.wait_send()
        mine.wait()

    return pl.pallas_call(
        body, name=name,
        out_shape=jax.ShapeDtypeStruct((N_DEV, R, 128), F32),
        in_specs=[pl.BlockSpec(memory_space=pltpu.VMEM)],
        out_specs=pl.BlockSpec(memory_space=pltpu.VMEM),
        scratch_shapes=[pltpu.SemaphoreType.DMA((7,)), pltpu.SemaphoreType.DMA((7,)), pltpu.SemaphoreType.DMA],
        compiler_params=pltpu.CompilerParams(vmem_limit_bytes=VMEM_LIMIT),
    )(x)


BIG = (("ffn1_w_gu", "col", D, PW), ("ffn1_w_down", "row", FF, D), ("w_in", "col", D, PW),
       ("w_att_o", "col", 512, D), ("w_rec_o", "row", D, D), ("w_out", "row", D, D),
       ("ffn2_w_gu", "col", D, PW), ("ffn2_w_down", "row", FF, D))
NBIG = len(BIG)


def _shard_shape(kind, R, C):
    return (R, C // 4) if kind == "col" else (R // 4, C)


def _region(ref, kind, R, C, q, half, t, tr):
    sr, sc = _shard_shape(kind, R, C)
    if kind == "col":
        return ref.at[pl.ds(pl.multiple_of(half * (R // 2) + t * tr, 16), tr), pl.ds(q * sc, sc)]
    return ref.at[pl.ds(pl.multiple_of(q * sr + t * tr, 16), tr), pl.ds(half * (C // 2), C // 2)]


def ag_local(w, kind, R, C, p_arr, name, after=()):
    sr, sc = _shard_shape(kind, R, C)
    tr = _row_tile(sr, sc, budget=2 * 1024 * 1024)
    nt = sr // tr
    after = list(after)

    def body(p_ref, w_ref, *rest):
        rest[-1][...] = w_ref[...].astype(BF16)

    if kind == "col":
        o_spec = pl.BlockSpec((tr, sc), lambda i, p: (i, p[0]))
    else:
        o_spec = pl.BlockSpec((tr, sc), lambda i, p: (p[0] * nt + i, 0))
    return pl.pallas_call(
        body, name=name,
        grid_spec=pltpu.PrefetchScalarGridSpec(
            num_scalar_prefetch=1, grid=(nt,),
            in_specs=[pl.BlockSpec((tr, sc), lambda i, p: (i, 0))] + [ANY] * len(after), out_specs=o_spec),
        out_shape=jax.ShapeDtypeStruct((R, C), BF16),
        compiler_params=_cp("parallel"),
    )(p_arr, w, *after)


HBM_SPEC = pl.BlockSpec(memory_space=pltpu.HBM)
SEM_SPEC = pl.BlockSpec(memory_space=pltpu.SEMAPHORE)


def _ag_sems(geoms):
    return sum(6 if both else 3 for (_, _, _, both) in geoms)


def _ag_copies(fulls, geoms, ssem, rsem, mx, my, mc, q, h):
    chips = _other_chips(mx, my)
    out, base = [], 0
    for w, (kind, R, C, both) in enumerate(geoms):
        sr, sc = _shard_shape(kind, R, C)
        hr = sr // 2 if kind == "col" else sr
        reg = _region(fulls[w], kind, R, C, q, h, 0, hr)
        out.append([pltpu.make_async_remote_copy(
            src_ref=reg, dst_ref=reg, send_sem=ssem.at[base + 3 * t + k], recv_sem=rsem.at[base + 3 * t + k],
            device_id=(*chips[k], mc if t == 0 else 1 - mc), device_id_type=MESH)
            for t in range(2 if both else 1) for k in range(3)])
        base += 6 if both else 3
    return out


def ag_start(fulls, geoms, after, name):
    n = len(fulls)
    after = list(after)
    m = len(after)

    def body(*refs):
        ssem, rsem = refs[n + m:n + m + 2]
        outs, token = refs[n + m + 2:2 * n + m + 2], refs[2 * n + m + 2]
        mx, my, mc = _mesh_pos()
        p = 2 * mx + my
        col = [w for w, g in enumerate(geoms) if g[0] == "col"]
        row = [w for w, g in enumerate(geoms) if g[0] == "row"]
        for q in range(4):
            @pl.when(p == q)
            def _(q=q):
                cps = _ag_copies(outs, geoms, ssem, rsem, mx, my, mc, q, mc)
                for w in col:
                    for cp in cps[w]:
                        cp.start()
        for h in range(2):
            @pl.when(mc == h)
            def _(h=h):
                cps = _ag_copies(outs, geoms, ssem, rsem, mx, my, mc, p, h)
                for w in row:
                    for cp in cps[w]:
                        cp.start()
        token[...] = jnp.zeros_like(token)

    res = pl.pallas_call(
        body, name=name,
        out_shape=[pltpu.SemaphoreType.DMA((_ag_sems(geoms),)), pltpu.SemaphoreType.DMA((_ag_sems(geoms),))]
        + [pltpu.HBM(a.shape, a.dtype) for a in fulls] + [jax.ShapeDtypeStruct((8, 128), F32)],
        in_specs=[HBM_SPEC] * n + [ANY] * m,
        out_specs=[SEM_SPEC, SEM_SPEC] + [HBM_SPEC] * n + [pl.BlockSpec(memory_space=pltpu.VMEM)],
        input_output_aliases={w: 2 + w for w in range(n)},
        compiler_params=pltpu.CompilerParams(has_side_effects=pltpu.SideEffectType.DATAFLOW_SIDE_EFFECTING),
    )(*[pltpu.with_memory_space_constraint(a, pltpu.HBM) for a in fulls], *after)
    return res[0], res[1], list(res[2:2 + n]), res[2 + n]


def ag_wait(fulls, geoms, ssem, rsem, after, name):
    n = len(fulls)
    after = list(after) if isinstance(after, (list, tuple)) else [after]

    def body(*refs):
        ins, ssem_ref, rsem_ref = refs[:n], refs[n], refs[n + 1]
        mx, my, mc = _mesh_pos()
        for cps in _ag_copies(ins, geoms, ssem_ref, rsem_ref, mx, my, mc, 0, 0):
            for cp in cps:
                cp.wait_send()
                cp.wait_recv()

    return list(pl.pallas_call(
        body, name=name,
        out_shape=[pltpu.HBM(a.shape, a.dtype) for a in fulls],
        in_specs=[HBM_SPEC] * n + [SEM_SPEC, SEM_SPEC] + [ANY] * len(after),
        out_specs=[HBM_SPEC] * n,
        input_output_aliases={w: w for w in range(n)},
        compiler_params=pltpu.CompilerParams(has_side_effects=pltpu.SideEffectType.DATAFLOW_SIDE_EFFECTING),
    )(*fulls, ssem, rsem, *after))


def ag_forward(full, kind, R, C, name):
    sr, sc = _shard_shape(kind, R, C)
    hr, hc = (sr // 2, sc) if kind == "col" else (sr, sc // 2)
    tr = _row_tile(hr, hc, itemsize=2, budget=1024 * 1024)
    nt = hr // tr
    total = 3 * nt

    def body(src_ref, full_ref, stage, lsem, ssem, rsem):
        step = pl.program_id(0) * nt + pl.program_id(1)
        par = step % 2
        mx, my, mc = _mesh_pos()

        def load(s, q, h, t):
            return pltpu.make_async_copy(_region(src_ref, kind, R, C, q, h, t, tr), stage.at[s], lsem.at[s])

        def push(s, q, h, t):
            return pltpu.make_async_remote_copy(src_ref=stage.at[s], dst_ref=_region(full_ref, kind, R, C, q, h, t, tr),
                                                send_sem=ssem.at[s], recv_sem=rsem, device_id=(mx, my, 1 - mc),
                                                device_id_type=MESH)

        def for_tile(stp, fn):
            q_k = _partner_chip(stp // nt, 2 * mx + my)
            if kind == "col":
                for q in range(4):
                    @pl.when(q_k == q)
                    def _(q=q):
                        fn(q, mc, stp % nt)
            else:
                for h in range(2):
                    @pl.when(mc == h)
                    def _(h=h):
                        fn(q_k, h, stp % nt)

        @pl.when(step == 0)
        def _():
            for_tile(step, lambda q, h, t: load(0, q, h, t).start())

        load(par, 0, 0, 0).wait()
        for_tile(step, lambda q, h, t: push(par, q, h, t).start())

        @pl.when(step + 1 < total)
        def _():
            @pl.when(step >= 1)
            def _():
                push(1 - par, 0, 0, 0).wait_send()
            for_tile(step + 1, lambda q, h, t: load(1 - par, q, h, t).start())

        @pl.when(step == total - 1)
        def _():
            push(par, 0, 0, 0).wait_send()
            push(1 - par, 0, 0, 0).wait_send()
            three = full_ref.at[pl.ds(0, hr), pl.ds(0, 3 * hc)] if kind == "col" else full_ref.at[pl.ds(0, 3 * hr), pl.ds(0, hc)]
            pltpu.make_async_remote_copy(src_ref=three, dst_ref=three, send_sem=ssem.at[0], recv_sem=rsem,
                                         device_id=(mx, my, 1 - mc), device_id_type=MESH).wait_recv()

    return pl.pallas_call(
        body, name=name, grid=(3, nt),
        in_specs=[ANY], out_specs=ANY,
        out_shape=jax.ShapeDtypeStruct((R, C), BF16),
        scratch_shapes=[pltpu.VMEM((2, tr, hc), BF16), pltpu.SemaphoreType.DMA((2,)), pltpu.SemaphoreType.DMA((2,)),
                        pltpu.SemaphoreType.DMA],
        input_output_aliases={0: 0},
        compiler_params=_cp("arbitrary", "arbitrary"),
    )(full)


def _half_shape(kind, R, C):
    return (R // 2, C) if kind == "col" else (R, C // 2)


def _piece_shape(kind, R, C):
    return (R // 2, C // 4) if kind == "col" else (R // 4, C // 2)


def pair_push(g, kind, c_arr, name):
    R, C = g.shape
    hr, hc = _half_shape(kind, R, C)
    tr = _row_tile(hr, hc, itemsize=2, budget=1024 * 1024)
    nt = hr // tr

    def body(c_ref, g_ref, out_ref, stage, ssem, rsem):
        i = pl.program_id(0)
        slot = i % 2
        mx, my, mc = _mesh_pos()

        def push(s, t):
            return pltpu.make_async_remote_copy(
                src_ref=stage.at[s], dst_ref=out_ref.at[pl.ds(pl.multiple_of(t * tr, 16), tr)],
                send_sem=ssem.at[s], recv_sem=rsem, device_id=(mx, my, 1 - mc), device_id_type=MESH)

        @pl.when(i >= 2)
        def _():
            push(slot, 0).wait_send()

        stage[slot] = g_ref[...]
        push(slot, i).start()

        @pl.when(i == nt - 1)
        def _():
            push(slot, 0).wait_send()
            if nt >= 2:
                push(1 - slot, 0).wait_send()
            pltpu.make_async_remote_copy(src_ref=out_ref, dst_ref=out_ref, send_sem=ssem.at[0], recv_sem=rsem,
                                         device_id=(mx, my, 1 - mc), device_id_type=MESH).wait_recv()

    if kind == "col":
        g_spec = pl.BlockSpec((tr, hc), lambda i, c: ((1 - c[0]) * nt + i, 0))
    else:
        g_spec = pl.BlockSpec((tr, hc), lambda i, c: (i, 1 - c[0]))
    return pl.pallas_call(
        body, name=name,
        grid_spec=pltpu.PrefetchScalarGridSpec(
            num_scalar_prefetch=1, grid=(nt,), in_specs=[g_spec], out_specs=ANY,
            scratch_shapes=[pltpu.VMEM((2, tr, hc), BF16), pltpu.SemaphoreType.DMA((2,)), pltpu.SemaphoreType.DMA]),
        out_shape=jax.ShapeDtypeStruct((hr, hc), BF16),
        compiler_params=_cp("arbitrary"),
    )(c_arr, g)


def _partner_chip(k, p):
    return p ^ jnp.where(k == 0, 2, jnp.where(k == 1, 1, jnp.where(k == 2, 3, 0)))


def pair_add(g, got, kind, cp_arr, name):
    R, C = g.shape
    pr, pc = _piece_shape(kind, R, C)
    tr = _row_tile(pr, pc, itemsize=2, budget=1024 * 1024)
    nt = pr // tr

    def body(cp_ref, g_ref, got_ref, ps_ref, rb_ref):
        tile = (g_ref[...].astype(F32) + got_ref[...].astype(F32)).astype(BF16)
        ps_ref[...] = tile

        @pl.when(pl.program_id(1) == cp_ref[1])
        def _():
            rb_ref[...] = tile

    if kind == "col":
        g_spec = pl.BlockSpec((tr, pc), lambda i, q, cp: (cp[0] * nt + i, q))
        got_spec = pl.BlockSpec((tr, pc), lambda i, q, cp: (i, q))
    else:
        g_spec = pl.BlockSpec((tr, pc), lambda i, q, cp: (q * nt + i, cp[0]))
        got_spec = pl.BlockSpec((tr, pc), lambda i, q, cp: (q * nt + i, 0))
    return pl.pallas_call(
        body, name=name,
        grid_spec=pltpu.PrefetchScalarGridSpec(
            num_scalar_prefetch=1, grid=(nt, 4), in_specs=[g_spec, got_spec],
            out_specs=[pl.BlockSpec((None, tr, pc), lambda i, q, cp: (q, i, 0)),
                       pl.BlockSpec((None, tr, pc), lambda i, q, cp: (cp[1], i, 0))]),
        out_shape=[jax.ShapeDtypeStruct((4, pr, pc), BF16)] * 2,
        compiler_params=_cp("arbitrary", "arbitrary"),
    )(cp_arr, g, got)


def _rs_copies(ps, rb, ssem, rsem, mx, my, mc):
    p = 2 * mx + my
    out = []
    for w in range(len(ps)):
        for k, chip in enumerate(_other_chips(mx, my)):
            out.append(pltpu.make_async_remote_copy(
                src_ref=ps[w].at[2 * chip[0] + chip[1]], dst_ref=rb[w].at[p], send_sem=ssem.at[3 * w + k],
                recv_sem=rsem.at[3 * w + k], device_id=(*chip, mc), device_id_type=MESH))
    return out


def rs_start(ps, rb, after, name):
    n = len(ps)
    after = list(after)
    m = len(after)

    def body(*refs):
        ssem, rsem = refs[2 * n + m:2 * n + m + 2]
        ps_o = refs[2 * n + m + 2:3 * n + m + 2]
        rb_o = refs[3 * n + m + 2:4 * n + m + 2]
        token = refs[4 * n + m + 2]
        for cp in _rs_copies(ps_o, rb_o, ssem, rsem, *_mesh_pos()):
            cp.start()
        token[...] = jnp.zeros_like(token)

    both = list(ps) + list(rb)
    res = pl.pallas_call(
        body, name=name,
        out_shape=[pltpu.SemaphoreType.DMA((3 * n,)), pltpu.SemaphoreType.DMA((3 * n,))]
        + [pltpu.HBM(a.shape, a.dtype) for a in both] + [jax.ShapeDtypeStruct((8, 128), F32)],
        in_specs=[HBM_SPEC] * (2 * n) + [ANY] * m,
        out_specs=[SEM_SPEC, SEM_SPEC] + [HBM_SPEC] * (2 * n) + [pl.BlockSpec(memory_space=pltpu.VMEM)],
        input_output_aliases={w: 2 + w for w in range(2 * n)},
        compiler_params=pltpu.CompilerParams(has_side_effects=pltpu.SideEffectType.DATAFLOW_SIDE_EFFECTING),
    )(*[pltpu.with_memory_space_constraint(a, pltpu.HBM) for a in both], *after)
    return res[0], res[1], list(res[2:2 + n]), list(res[2 + n:2 + 2 * n]), res[2 + 2 * n]


def rs_wait(ps, rb, ssem, rsem, after, name):
    n = len(ps)
    after = list(after)
    m = len(after)

    def body(*refs):
        ps_i, rb_i = refs[:n], refs[n:2 * n]
        ssem_ref, rsem_ref = refs[2 * n], refs[2 * n + 1]
        for cp in _rs_copies(ps_i, rb_i, ssem_ref, rsem_ref, *_mesh_pos()):
            cp.wait_send()
            cp.wait_recv()

    both = list(ps) + list(rb)
    res = pl.pallas_call(
        body, name=name,
        out_shape=[pltpu.HBM(a.shape, a.dtype) for a in both],
        in_specs=[HBM_SPEC] * (2 * n) + [SEM_SPEC, SEM_SPEC] + [ANY] * m,
        out_specs=[HBM_SPEC] * (2 * n),
        input_output_aliases={w: w for w in range(2 * n)},
        compiler_params=pltpu.CompilerParams(has_side_effects=pltpu.SideEffectType.DATAFLOW_SIDE_EFFECTING),
    )(*both, ssem, rsem, *after)
    return list(res[n:])


def sum_share(parts, kind, R, C, name):
    _, pr, pc = parts.shape
    sr, sc = _shard_shape(kind, R, C)
    tr = _row_tile(pr, pc * 4, budget=4 * 1024 * 1024)
    nt = pr // tr

    def body(p_ref, fin_ref, stage, lsem, ssem, rsem):
        i = pl.program_id(0)
        slot = i % 2
        mx, my, mc = _mesh_pos()

        def region(h, t):
            r0 = pl.multiple_of(t * tr, 8)
            if kind == "col":
                return fin_ref.at[pl.ds(pl.multiple_of(h * pr + r0, 8), tr)]
            return fin_ref.at[pl.ds(r0, tr), pl.ds(h * pc, pc)]

        def copies(s, h, t):
            return (pltpu.make_async_copy(stage.at[s], region(h, t), lsem.at[s]),
                    pltpu.make_async_remote_copy(src_ref=stage.at[s], dst_ref=region(h, t), send_sem=ssem.at[s],
                                                 recv_sem=rsem, device_id=(mx, my, 1 - mc), device_id_type=MESH))

        def wait_sent(s):
            loc, rem = copies(s, 0, 0)
            loc.wait()
            rem.wait_send()

        @pl.when(i >= 2)
        def _():
            wait_sent(slot)

        acc = p_ref[0].astype(F32)
        for k in range(1, 4):
            acc = acc + p_ref[k].astype(F32)
        stage[slot] = acc
        if kind == "col":
            for cp in copies(slot, mc, i):
                cp.start()
        else:
            for h in range(2):
                @pl.when(mc == h)
                def _(h=h):
                    for cp in copies(slot, h, i):
                        cp.start()

        @pl.when(i == nt - 1)
        def _():
            wait_sent(slot)
            if nt >= 2:
                wait_sent(1 - slot)
            half = fin_ref.at[pl.ds(0, pr), pl.ds(0, pc)]
            pltpu.make_async_remote_copy(src_ref=half, dst_ref=half, send_sem=ssem.at[0], recv_sem=rsem,
                                         device_id=(mx, my, 1 - mc), device_id_type=MESH).wait_recv()

    return pl.pallas_call(
        body, name=name, grid=(nt,),
        in_specs=[pl.BlockSpec((4, tr, pc), lambda i: (0, i, 0))],
        out_specs=ANY,
        out_shape=jax.ShapeDtypeStruct((sr, sc), F32),
        scratch_shapes=[pltpu.VMEM((2, tr, pc), F32), pltpu.SemaphoreType.DMA((2,)), pltpu.SemaphoreType.DMA((2,)),
                        pltpu.SemaphoreType.DMA],
        compiler_params=_cp("arbitrary"),
    )(parts)


def _pack(parts, rows):
    flat = []
    for a in parts:
        a = jnp.ravel(a).astype(F32)
        flat.append(jnp.pad(a, (0, (-a.shape[0]) % 128)))
    v = jnp.concatenate(flat)
    return jnp.pad(v, (0, rows * 128 - v.shape[0])).reshape(rows, 128)


def _unpack(block, shapes):
    lead = block.shape[:-2]
    v = block.reshape(lead + (-1,))
    out, off = [], 0
    for shp in shapes:
        n = int(np.prod(shp))
        out.append(v[..., off:off + n].reshape(lead + tuple(shp)))
        off += n + (-n) % 128
    return out


def _block_diag4(w):
    w4 = w.reshape(4, 4, 64, 64)
    eye = jnp.eye(4, dtype=w.dtype)
    return (w4[:, :, :, None, :] * eye[None, :, None, :, None]).reshape(4, 256, 256)


def _diag_blocks(bd):
    b5 = bd.reshape(4, 4, 64, 4, 64)
    return jnp.stack([b5[:, i, :, i, :] for i in range(4)], axis=1).reshape(16, 64, 64)


def _bias_window(rel_bias):
    m = (np.arange(768) + 127) % 768 - 127
    w = rel_bias[:, np.clip(512 - m, -128, 128) + 128]
    win = jnp.tile(w, (1, 128))[:, :128 * 767].reshape(8, 128, 767)[:, :, :WIN]
    qh = np.arange(128)[:, None] // CHUNK
    kc = np.arange(WIN)[None, :] // CHUNK
    valid = (kc >= qh) & (kc <= qh + 8)
    return jnp.where(jnp.asarray(valid)[None], win, NEG)


SMALL = ("b_ada", "norm_pre", "norm_post", "rel_bias", "conv_w", "conv_b", "lru_wa", "lru_ba", "lru_wx",
         "lru_bx", "lru_lambda")
WEIGHTS = ("w_ada", "b_ada", "norm_pre", "norm_post", "ffn1_w_gu", "ffn1_w_down", "w_in", "rel_bias", "conv_w",
           "conv_b", "lru_wa", "lru_ba", "lru_wx", "lru_bx", "lru_lambda", "w_att_o", "w_rec_o", "w_out",
           "ffn2_w_gu", "ffn2_w_down")


def kernel(x, c, w_ada, b_ada, norm_pre, norm_post, ffn1_w_gu, ffn1_w_down, w_in, rel_bias, conv_w, conv_b, lru_wa, lru_ba, lru_wx, lru_bx, lru_lambda, w_att_o, w_rec_o, w_out, ffn2_w_gu, ffn2_w_down, loss_target, m_w_ada, m_b_ada, m_norm_pre, m_norm_post, m_ffn1_w_gu, m_ffn1_w_down, m_w_in, m_rel_bias, m_conv_w, m_conv_b, m_lru_wa, m_lru_ba, m_lru_wx, m_lru_bx, m_lru_lambda, m_w_att_o, m_w_rec_o, m_w_out, m_ffn2_w_gu, m_ffn2_w_down, v_w_ada, v_b_ada, v_norm_pre, v_norm_post, v_ffn1_w_gu, v_ffn1_w_down, v_w_in, v_rel_bias, v_conv_w, v_conv_b, v_lru_wa, v_lru_ba, v_lru_wx, v_lru_bx, v_lru_lambda, v_w_att_o, v_w_rec_o, v_w_out, v_ffn2_w_gu, v_ffn2_w_down):
    W = dict(w_ada=w_ada, b_ada=b_ada, norm_pre=norm_pre, norm_post=norm_post, ffn1_w_gu=ffn1_w_gu,
             ffn1_w_down=ffn1_w_down, w_in=w_in, rel_bias=rel_bias, conv_w=conv_w, conv_b=conv_b, lru_wa=lru_wa,
             lru_ba=lru_ba, lru_wx=lru_wx, lru_bx=lru_bx, lru_lambda=lru_lambda, w_att_o=w_att_o, w_rec_o=w_rec_o,
             w_out=w_out, ffn2_w_gu=ffn2_w_gu, ffn2_w_down=ffn2_w_down)
    M = dict(w_ada=m_w_ada, b_ada=m_b_ada, norm_pre=m_norm_pre, norm_post=m_norm_post, ffn1_w_gu=m_ffn1_w_gu,
             ffn1_w_down=m_ffn1_w_down, w_in=m_w_in, rel_bias=m_rel_bias, conv_w=m_conv_w, conv_b=m_conv_b,
             lru_wa=m_lru_wa, lru_ba=m_lru_ba, lru_wx=m_lru_wx, lru_bx=m_lru_bx, lru_lambda=m_lru_lambda,
             w_att_o=m_w_att_o, w_rec_o=m_w_rec_o, w_out=m_w_out, ffn2_w_gu=m_ffn2_w_gu, ffn2_w_down=m_ffn2_w_down)
    V = dict(w_ada=v_w_ada, b_ada=v_b_ada, norm_pre=v_norm_pre, norm_post=v_norm_post, ffn1_w_gu=v_ffn1_w_gu,
             ffn1_w_down=v_ffn1_w_down, w_in=v_w_in, rel_bias=v_rel_bias, conv_w=v_conv_w, conv_b=v_conv_b,
             lru_wa=v_lru_wa, lru_ba=v_lru_ba, lru_wx=v_lru_wx, lru_bx=v_lru_bx, lru_lambda=v_lru_lambda,
             w_att_o=v_w_att_o, w_rec_o=v_w_rec_o, w_out=v_w_out, ffn2_w_gu=v_ffn2_w_gu, ffn2_w_down=v_ffn2_w_down)
    mx, my, mc = _mesh_pos()
    p = 2 * mx + my
    e = 4 * mx + 2 * my + mc
    xs = x[0]

    c_arr = jnp.reshape(mc, (1,)).astype(jnp.int32)
    cp_arr = jnp.stack([mc, p]).astype(jnp.int32)
    p_arr = jnp.reshape(p, (1,)).astype(jnp.int32)
    direct = ("w_att_o", "w_rec_o", "w_out", "ffn2_w_gu", "ffn2_w_down")
    geoms = [(kind, R, C, n in direct) for (n, kind, R, C) in BIG]
    names = [b[0] for b in BIG]
    placed = [ag_local(W[n][0], kind, R, C, p_arr, "ag_local_" + n) for (n, kind, R, C) in BIG[:2]]

    def arrived(fly, lo, hi, ssem, rsem, after, tag):
        done = ag_wait(fly, geoms[lo:hi], ssem, rsem, after, "ag_wait_" + tag)
        return [a if both else ag_forward(a, kind, R, C, "ag_forward_" + n)
                for a, (kind, R, C, both), n in zip(done, geoms[lo:hi], names[lo:hi])]

    g1 = ag_small(_pack([c, norm_pre, norm_post, conv_w], 32), "ag_small_params")
    c_all, npre4, npost4, cw4 = _unpack(g1, [(D,), (3, 256), (3, 256), (4, 256)])
    chipwise = lambda a: jnp.moveaxis(a[0::2], 0, 1).reshape(a.shape[1], D)
    npre, npost, conv_full = chipwise(npre4), chipwise(npost4), chipwise(cw4)

    b_cols = lax.dynamic_slice(b_ada, (0, p * 2304), (1, 2304))
    mod_cols = ada_fwd(c_all, w_ada[0], b_cols, "ada_fwd")
    g2 = ag_small(mod_cols.reshape(144, 128), "ag_mod")
    mod_all = jnp.moveaxis(g2[0::2].reshape(4, 8, 2304), 0, 1).reshape(8, 9 * D)
    mod = lax.dynamic_index_in_dim(mod_all, e, 0, keepdims=False).reshape(3, 3, D)
    zeros3 = jnp.zeros((3, D), F32)
    vecs = [jnp.concatenate([npre[k:k + 1], npost[k:k + 1], mod[k], zeros3], axis=0) for k in range(3)]

    gu_s, gu_r, gu_fly, tok_gu = ag_start(placed[:1], geoms[:1], [g2], "ag_start_ffn1_gu")
    dn_s, dn_r, dn_fly, tok0 = ag_start(placed[1:2], geoms[1:2], [tok_gu], "ag_start_ffn1_down")
    placed += [ag_local(W[n][0], kind, R, C, p_arr, "ag_local_" + n, after=[tok0]) for (n, kind, R, C) in BIG[2:]]
    f1_gu, = arrived(gu_fly, 0, 1, gu_s, gu_r, placed[2:], "ffn1_gu")
    f1_dn, = arrived(dn_fly, 1, 2, dn_s, dn_r, f1_gu, "ffn1_down")
    mix_s, mix_r, mix_fly, tok1 = ag_start(placed[2:6], geoms[2:6], [f1_gu, f1_dn], "ag_start_mixer")
    ffn_s, ffn_r, ffn_fly, tok2 = ag_start(placed[6:], geoms[6:], [tok1], "ag_start_ffn2")
    wa_bd = _block_diag4(lru_wa[0]).astype(BF16)
    wx_bd = _block_diag4(lru_wx[0]).astype(BF16)
    pvec = jnp.concatenate([conv_full, conv_b, lru_ba, lru_bx, lru_lambda], axis=0)
    bias = _bias_window(rel_bias[0]).reshape(4, 256, WIN)

    x1, h1, g1_, u1, a1, f1 = ffn_fwd(xs, vecs[0] + tok2[0:1, 0:1], f1_gu, f1_dn, 0.5, "ffn1_fwd")
    win, wao, wro, wout = arrived(mix_fly, 2, 6, mix_s, mix_r, x1, "mixer")
    h2, qkv, rest = proj_fwd(x1, vecs[1], win, "proj_fwd")
    ao = attn_fwd(qkv, bias, "attn_fwd")
    hl, hg = lru_fwd(rest, pvec, wa_bd, wx_bd, "lru_fwd")
    x2, att, rec, mg, f2 = mix_out_fwd(x1, ao, hg, rest, vecs[1], wao, wro, wout, "mix_out_fwd")
    f2_gu, f2_dn = arrived(ffn_fly, 6, 8, ffn_s, ffn_r, x2, "ffn2")
    dy, h3, g3_, u3, a3, f3, lvec = ffn_fwd(x2, vecs[2], f2_gu, f2_dn, 0.5, "ffn2_fwd", tgt=loss_target[0])

    G, grads = {}, {}
    geo = {n: (kind, R, C) for (n, kind, R, C) in BIG}

    def reduce_begin(names, tag):
        ps, rb = [], []
        for n in names:
            got = pair_push(G[n], geo[n][0], c_arr, "rs_push_" + n)
            a, b = pair_add(G[n], got, geo[n][0], cp_arr, "rs_pair_sum_" + n)
            ps.append(a)
            rb.append(b)
        return rs_start(ps, rb, [], "rs_start_" + tag)

    def reduce_end(names, flight, after, tag):
        ssem, rsem, ps, rb, _ = flight
        for a, n in zip(rs_wait(ps, rb, ssem, rsem, after, "rs_wait_" + tag), names):
            grads[n] = sum_share(a, *geo[n], "rs_sum_share_" + n)[None]

    dx2, df3, dgu3, va2 = ffn_bwd(dy, x2, f3, g3_, u3, vecs[2], f2_gu, f2_dn, 0.5, "ffn2_bwd")
    G["ffn2_w_gu"] = mm_tn(h3, dgu3, "dw_ffn2_gu", D, 1408, 2048, a_resident=True)
    G["ffn2_w_down"] = mm_tn(a3, df3, "dw_ffn2_down", 1408, D, 2048)
    fly_ffn2 = reduce_begin(("ffn2_w_gu", "ffn2_w_down"), "ffn2")
    vec1 = vecs[1] + fly_ffn2[4][0:1, 0:1]
    df2, d_att, d_rec, dao, dhl, d3, va_out = mix_out_bwd(dx2, f2, att, rec, rest, hl, vec1, wao, wro, wout,
                                                          "mix_out_bwd")
    G["w_out"] = mm_tn(mg, df2, "dw_out", D, D, 1024)
    G["w_att_o"] = mm_tn(ao, d_att, "dw_att_o", 512, D, 1024)
    G["w_rec_o"] = mm_tn(hg, d_rec, "dw_rec_o", D, D, 1024)
    dq, db, dkv = attn_bwd(qkv, dao, bias, "attn_bwd")
    dxr, v_lru, dwa_bd, dwx_bd = lru_bwd(dhl, hl, rest, pvec, wa_bd, wx_bd, "lru_bwd")
    dx1, va_in = proj_bwd(dq, dkv, dxr, d3, win, x1, dx2, vecs[1], "proj_bwd")
    G["w_in"] = dw_in(h2, dq, dkv, dxr, d3, "dw_in")
    fly_mix = reduce_begin(("w_in", "w_att_o", "w_rec_o", "w_out"), "mixer")
    vec0 = vecs[0] + fly_mix[4][0:1, 0:1]
    dx0, df1, dgu1, va0 = ffn_bwd(dx1, xs, f1, g1_, u1, vec0, f1_gu, f1_dn, 0.5, "ffn1_bwd")
    G["ffn1_w_gu"] = mm_tn(h1, dgu1, "dw_ffn1_gu", D, 1408, 2048, a_resident=True)
    G["ffn1_w_down"] = mm_tn(a1, df1, "dw_ffn1_down", 1408, D, 2048)
    fly_ffn1 = reduce_begin(("ffn1_w_gu", "ffn1_w_down"), "ffn1")
    reduce_end(("ffn2_w_gu", "ffn2_w_down"), fly_ffn2, [fly_ffn1[4]], "ffn2")
    reduce_end(("w_in", "w_att_o", "w_rec_o", "w_out"), fly_mix, [fly_ffn1[4], grads["ffn2_w_down"]], "mixer")

    va1 = va_out + va_in
    vas = (va0, va1, va2)
    dmod = jnp.stack([v[2:5] for v in vas])
    part = {"b_ada": dmod, "norm_pre": jnp.stack([v[0] for v in vas]), "norm_post": jnp.stack([v[1] for v in vas]),
            "rel_bias": bias_grad(db.reshape(8, 128, WIN), "bias_grad")[:, :257], "conv_w": v_lru[0:4], "conv_b": v_lru[4],
            "lru_wa": _diag_blocks(dwa_bd), "lru_ba": v_lru[5], "lru_wx": _diag_blocks(dwx_bd), "lru_bx": v_lru[6],
            "lru_lambda": v_lru[7]}
    full_shapes = {"b_ada": (9 * D,), "norm_pre": (3, D), "norm_post": (3, D), "rel_bias": (8, 257),
                   "conv_w": (4, D), "conv_b": (D,), "lru_wa": (16, 64, 64), "lru_ba": (D,),
                   "lru_wx": (16, 64, 64), "lru_bx": (D,), "lru_lambda": (D,)}
    g3 = ag_small(_pack([part[n] for n in SMALL] + [lvec[0:1, 0:1]], 1232), "ag_small_grads")
    summed = _unpack(sum_lead(g3, "sum_small_grads"), [full_shapes[n] for n in SMALL] + [(1,)])
    red = dict(zip(SMALL, summed[:-1]))
    loss = summed[-1][0]
    cols = lambda a: lax.dynamic_slice(a, (0, p * 256), (a.shape[0], 256))
    grads.update({"b_ada": red["b_ada"][None], "norm_pre": cols(red["norm_pre"])[None],
                  "norm_post": cols(red["norm_post"])[None], "rel_bias": red["rel_bias"][None],
                  "conv_w": cols(red["conv_w"])[None], "conv_b": red["conv_b"][None], "lru_wa": red["lru_wa"][None],
                  "lru_ba": red["lru_ba"][None], "lru_wx": red["lru_wx"][None], "lru_bx": red["lru_bx"][None],
                  "lru_lambda": red["lru_lambda"][None]})

    dmod_all = g3[:, :72].reshape(8, 9 * D)
    dmod_cols = jnp.pad(lax.dynamic_slice(dmod_all, (0, p * 2304), (8, 2304)), ((0, 120), (0, 0)))
    c_all_t = jnp.pad(c_all.T, ((0, 0), (0, 120)))
    grads["w_ada"] = ada_bwd(c_all_t, dmod_cols, "ada_bwd")[None]

    delta, new_m, new_v = {}, {}, {}

    def update(n):
        shp = W[n].shape
        res = adamw(W[n][0], grads[n][0], M[n][0], V[n][0], "adamw_" + n, emit_g=n in geo)
        delta[n], new_m[n], new_v[n] = [a.reshape(shp) for a in res[:3]]
        if n in geo:
            grads[n] = res[3].reshape(shp)

    for n in ("w_ada", "ffn2_w_gu", "ffn2_w_down", "w_in", "w_att_o", "w_rec_o", "w_out"):
        update(n)
    packed = [_pack([src[n] for n in SMALL], 1168) for src in (W, grads, M, V)]
    outs = adamw(*packed, "adamw_small")
    for dst, blk in zip((delta, new_m, new_v), outs):
        for n, a in zip(SMALL, _unpack(blk, [W[n].shape for n in SMALL])):
            dst[n] = a
    reduce_end(("ffn1_w_gu", "ffn1_w_down"), fly_ffn1,
               [outs[0], delta["w_ada"], delta["ffn2_w_gu"], delta["ffn2_w_down"], delta["w_in"], delta["w_out"]], "ffn1")
    for n in ("ffn1_w_gu", "ffn1_w_down"):
        update(n)

    return (loss, dx0[None], *[grads[n] for n in WEIGHTS], *[delta[n] for n in WEIGHTS],
            *[new_m[n] for n in WEIGHTS], *[new_v[n] for n in WEIGHTS])
```

```python
import functools

import numpy as np
import jax
import jax.numpy as jnp
from jax import lax
from jax.experimental import pallas as pl
from jax.experimental.pallas import tpu as pltpu

F32 = jnp.float32
BF16 = jnp.bfloat16

D = 1024
FF = 2816
PW = 5632
HP = 128
CHUNK = 64
WIN = 640
TQ = 512
EPS = 1e-6
NEG = -1e30
LRU_C = 8.0
N_DEV = 8
VMEM_LIMIT = 56 * 1024 * 1024

ADAM_LR, ADAM_B1, ADAM_B2, ADAM_EPS, ADAM_WD, ADAM_STEP = 0.001, 0.9, 0.999, 1e-08, 0.01, 10

MESH = pl.DeviceIdType.MESH
ANY = pl.BlockSpec(memory_space=pl.ANY)


def _cp(*sem):
    return pltpu.CompilerParams(dimension_semantics=tuple(sem), vmem_limit_bytes=VMEM_LIMIT)


def _dot(a, b):
    return jnp.dot(a, b, preferred_element_type=F32)


def _dot_nt(a, b):
    return lax.dot_general(a, b, (((1,), (1,)), ((), ())), preferred_element_type=F32)


def _dot_tn(a, b):
    return lax.dot_general(a, b, (((0,), (0,)), ((), ())), preferred_element_type=F32)


def _mean(v):
    return jnp.mean(v, axis=-1, keepdims=True)


def _colsum(v):
    return jnp.sum(v, axis=0, keepdims=True)


def _sigmoid(v):
    return 0.5 * jnp.tanh(0.5 * v) + 0.5


_GK = 0.7978845608028654


def _gelu(v):
    t = jnp.tanh(_GK * (v + 0.044715 * v * v * v))
    return 0.5 * v * (1.0 + t)


def _pre_norm(xv, vec_ref):
    r = lax.rsqrt(_mean(xv * xv) + EPS)
    n = xv * r * vec_ref[0:1, :]
    return n * (1.0 + vec_ref[3:4, :]) + vec_ref[2:3, :]


def _pre_norm_bwd(dh, xv, dres, vec_ref, vacc_ref):
    r = lax.rsqrt(_mean(xv * xv) + EPS)
    xh = xv * r
    n = xh * vec_ref[0:1, :]
    vacc_ref[2:3, :] += _colsum(dh)
    vacc_ref[3:4, :] += _colsum(dh * n)
    dn = dh * (1.0 + vec_ref[3:4, :])
    vacc_ref[0:1, :] += _colsum(dn * xh)
    dxh = dn * vec_ref[0:1, :]
    return r * (dxh - xh * _mean(dxh * xh)) + dres


def _post_norm_bwd(dxo, fv, res, vec_ref, vacc_ref):
    rf = lax.rsqrt(_mean(fv * fv) + EPS)
    fh = fv * rf
    gp = vec_ref[1:2, :]
    vacc_ref[4:5, :] += _colsum(res * dxo * (fh * gp))
    dy = (res * vec_ref[4:5, :]) * dxo
    vacc_ref[1:2, :] += _colsum(dy * fh)
    dfn = dy * gp
    return rf * (dfn - fh * _mean(dfn * fh))


def _u_spec(tf):
    return pl.BlockSpec((pl.Element(D), pl.Element(tf)),
                        lambda i, j: (0, pl.multiple_of(jnp.minimum(FF + j * tf, 2 * FF - tf), 128)))


def ffn_fwd(x, vec, w_gu, w_dn, res, name, tgt=None, tm=1024, tf=512):
    S = x.shape[0]
    tm = min(tm, S)
    nt = S // tm
    nf = -(-FF // tf)
    tail = FF - tf * (nf - 1)
    halves = [pl.ds(r * (tm // 2), tm // 2) for r in range(2)]
    head = tgt is not None

    def body(*refs):
        x_ref, vec_ref, wg_ref, wu_ref, wd_ref = refs[:5]
        if head:
            t_hbm, xo_ref, h_ref, g_ref, u_ref, a_ref, f_ref, l_ref, hs, acc, lacc, ts, tsem = refs[5:]
        else:
            xo_ref, h_ref, g_ref, u_ref, a_ref, f_ref, hs, acc = refs[5:]
        i, j = pl.program_id(0), pl.program_id(1)
        if head:
            late = pltpu.make_async_copy(t_hbm.at[pl.ds(pl.multiple_of(i * tm, 8), tm)], ts, tsem)

        @pl.when(j == 0)
        def _():
            if head:
                late.start()
            h = _pre_norm(x_ref[...], vec_ref).astype(BF16)
            hs[...] = h
            h_ref[...] = h
            acc[...] = jnp.zeros_like(acc)

        def chunk(w):
            gu = [(_dot(hs[r, :], wg_ref[:, 0:w]), _dot(hs[r, :], wu_ref[:, tf - w:tf])) for r in halves]
            acts = []
            for r, (g, u) in zip(halves, gu):
                g_ref[r, 0:w] = g.astype(BF16)
                u_ref[r, 0:w] = u.astype(BF16)
                a = (g * _sigmoid(g) * u).astype(BF16)
                a_ref[r, 0:w] = a
                acts.append(a)
            for r, a in zip(halves, acts):
                acc[r, :] += _dot(a, wd_ref[0:w, :])

        @pl.when(j < nf - 1)
        def _():
            chunk(tf)

        @pl.when(j == nf - 1)
        def _():
            chunk(tail)
            f = acc[...]
            f_ref[...] = f.astype(BF16)
            y = f * lax.rsqrt(_mean(f * f) + EPS) * vec_ref[1:2, :]
            xo = x_ref[...] + (res * vec_ref[4:5, :]) * y
            if head:
                @pl.when(i == 0)
                def _():
                    lacc[...] = jnp.zeros_like(lacc)

                late.wait()
                d = xo - ts[...]
                xo_ref[...] = d * (1.0 / D)
                lacc[...] += _colsum(d * d)

                @pl.when(i == nt - 1)
                def _():
                    l_ref[...] = jnp.broadcast_to(0.5 * jnp.sum(lacc[...]) * (1.0 / D), (8, 128))
            else:
                xo_ref[...] = xo

    row = lambda i, j: (i, 0)
    col = lambda i, j: (i, j)
    in_specs = [pl.BlockSpec((tm, D), row), pl.BlockSpec((8, D), lambda i, j: (0, 0)),
                pl.BlockSpec((D, tf), lambda i, j: (0, j)), _u_spec(tf),
                pl.BlockSpec((tf, D), lambda i, j: (j, 0))]
    out_specs = [pl.BlockSpec((tm, D), row), pl.BlockSpec((tm, D), row), pl.BlockSpec((tm, tf), col),
                 pl.BlockSpec((tm, tf), col), pl.BlockSpec((tm, tf), col), pl.BlockSpec((tm, D), row)]
    out_shape = [jax.ShapeDtypeStruct((S, D), F32), jax.ShapeDtypeStruct((S, D), BF16),
                 jax.ShapeDtypeStruct((S, FF), BF16), jax.ShapeDtypeStruct((S, FF), BF16),
                 jax.ShapeDtypeStruct((S, FF), BF16), jax.ShapeDtypeStruct((S, D), BF16)]
    scratch = [pltpu.VMEM((tm, D), BF16), pltpu.VMEM((tm, D), F32)]
    args = [x, vec, w_gu, w_gu, w_dn]
    if head:
        in_specs.append(ANY)
        out_specs.append(pl.BlockSpec((8, 128), lambda i, j: (0, 0)))
        out_shape.append(jax.ShapeDtypeStruct((8, 128), F32))
        scratch += [pltpu.VMEM((1, D), F32), pltpu.VMEM((tm, D), F32), pltpu.SemaphoreType.DMA]
        args.append(tgt)
    return pl.pallas_call(
        body, name=name, grid=(nt, nf), in_specs=in_specs, out_specs=out_specs, out_shape=out_shape,
        scratch_shapes=scratch,
        compiler_params=_cp("arbitrary" if head else "parallel", "arbitrary"),
    )(*args)


def ffn_bwd(dxo, x, f, g, u, vec, w_gu, w_dn, res, name, tm=1024, tf=512):
    S = x.shape[0]
    tm = min(tm, S)
    nf = -(-FF // tf)
    tail = FF - tf * (nf - 1)
    halves = [pl.ds(r * (tm // 2), tm // 2) for r in range(2)]

    def body(dxo_ref, x_hbm, f_ref, g_ref, u_ref, vec_ref, wg_ref, wu_ref, wd_ref,
             dx_ref, df_ref, dgu_ref, vacc_ref, dfs, acc, xs, xsem):
        i, j = pl.program_id(0), pl.program_id(1)
        late = pltpu.make_async_copy(x_hbm.at[pl.ds(pl.multiple_of(i * tm, 8), tm)], xs, xsem)

        @pl.when((i == 0) & (j == 0))
        def _():
            vacc_ref[...] = jnp.zeros_like(vacc_ref)

        @pl.when(j == 0)
        def _():
            late.start()
            df = _post_norm_bwd(dxo_ref[...], f_ref[...].astype(F32), res, vec_ref, vacc_ref).astype(BF16)
            dfs[...] = df
            df_ref[...] = df
            acc[...] = jnp.zeros_like(acc)

        def chunk(w):
            da = [_dot_nt(dfs[h, :], wd_ref[0:w, :]) for h in halves]
            dgu = []
            for h, d in zip(halves, da):
                gv, uv = g_ref[h, 0:w].astype(F32), u_ref[h, 0:w].astype(F32)
                sg = _sigmoid(gv)
                dg = (d * uv * (sg * (1.0 + gv * (1.0 - sg)))).astype(BF16)
                du = (d * (gv * sg)).astype(BF16)
                dgu_ref[0, h, 0:w] = dg
                dgu_ref[1, h, 0:w] = du
                dgu.append((dg, du))
            for h, (dg, du) in zip(halves, dgu):
                acc[h, :] += _dot_nt(dg, wg_ref[:, 0:w]) + _dot_nt(du, wu_ref[:, tf - w:tf])

        @pl.when(j < nf - 1)
        def _():
            chunk(tf)

        @pl.when(j == nf - 1)
        def _():
            chunk(tail)
            late.wait()
            dx_ref[...] = _pre_norm_bwd(acc[...], xs[...], dxo_ref[...], vec_ref, vacc_ref)

    row = lambda i, j: (i, 0)
    col = lambda i, j: (i, j)
    return pl.pallas_call(
        body, name=name, grid=(S // tm, nf),
        in_specs=[pl.BlockSpec((tm, D), row), ANY, pl.BlockSpec((tm, D), row),
                  pl.BlockSpec((tm, tf), col), pl.BlockSpec((tm, tf), col),
                  pl.BlockSpec((8, D), lambda i, j: (0, 0)),
                  pl.BlockSpec((D, tf), lambda i, j: (0, j)), _u_spec(tf),
                  pl.BlockSpec((tf, D), lambda i, j: (j, 0))],
        out_specs=[pl.BlockSpec((tm, D), row), pl.BlockSpec((tm, D), row),
                   pl.BlockSpec((2, tm, tf), lambda i, j: (0, i, j)),
                   pl.BlockSpec((8, D), lambda i, j: (0, 0))],
        out_shape=[jax.ShapeDtypeStruct((S, D), F32), jax.ShapeDtypeStruct((S, D), BF16),
                   jax.ShapeDtypeStruct((2, S, FF), BF16), jax.ShapeDtypeStruct((8, D), F32)],
        scratch_shapes=[pltpu.VMEM((tm, D), BF16), pltpu.VMEM((tm, D), F32), pltpu.VMEM((tm, D), F32),
                        pltpu.SemaphoreType.DMA],
        compiler_params=_cp("arbitrary", "arbitrary"),
    )(dxo, x, f, g, u, vec, w_gu, w_gu, w_dn)


def mm_tn(a, b, name, tm, tn, tk, out_dtype=BF16, a_resident=False):
    S, M = a.shape
    if b.ndim == 3:
        G, _, Nf = b.shape
    else:
        G, Nf = 1, b.shape[1]
    N = G * Nf
    tk = min(tk, S)
    nbf = Nf // tn
    nk = S // tk

    def body(a_ref, b_ref, o_ref, acc):
        k = pl.program_id(2)

        @pl.when(k == 0)
        def _():
            acc[...] = jnp.zeros_like(acc)

        a_blk = a_ref[pl.ds(pl.multiple_of(k * tk, 16), tk), :] if a_resident else a_ref[...]
        acc[...] += _dot_tn(a_blk, b_ref[...])

        @pl.when(k == nk - 1)
        def _():
            o_ref[...] = acc[...].astype(out_dtype)

    if b.ndim == 3:
        b_spec = pl.BlockSpec((None, tk, tn), lambda i, j, k: (j // nbf, k, j % nbf))
    else:
        b_spec = pl.BlockSpec((tk, tn), lambda i, j, k: (k, j))
    if a_resident:
        a_spec = pl.BlockSpec((S, M), lambda i, j, k: (0, 0), pipeline_mode=pl.Buffered(1))
    else:
        a_spec = pl.BlockSpec((tk, tm), lambda i, j, k: (k, i))
    return pl.pallas_call(
        body, name=name, grid=(M // tm, N // tn, nk),
        in_specs=[a_spec, b_spec],
        out_specs=pl.BlockSpec((tm, tn), lambda i, j, k: (i, j)),
        out_shape=jax.ShapeDtypeStruct((M, N), out_dtype),
        scratch_shapes=[pltpu.VMEM((tm, tn), F32)],
        compiler_params=_cp("parallel", "parallel", "arbitrary"),
    )(a, b)


def proj_fwd(x, vec, w_in, name, tm=2048, tn=512):
    S = x.shape[0]
    tm = min(tm, S)
    nq = 1536 // tn

    def body(x_ref, vec_ref, w_ref, h_ref, qkv_ref, rest_ref, hs):
        j = pl.program_id(1)

        @pl.when(j == 0)
        def _():
            h = _pre_norm(x_ref[...], vec_ref).astype(BF16)
            hs[...] = h
            h_ref[...] = h

        r = _dot(hs[...], w_ref[...])

        @pl.when(j < nq)
        def _():
            qkv_ref[...] = r.astype(BF16)

        @pl.when(j >= nq)
        def _():
            rest_ref[...] = r.astype(BF16)

    row = lambda i, j: (i, 0)
    return pl.pallas_call(
        body, name=name, grid=(S // tm, PW // tn),
        in_specs=[pl.BlockSpec((tm, D), row), pl.BlockSpec((8, D), lambda i, j: (0, 0)),
                  pl.BlockSpec((D, tn), lambda i, j: (0, j))],
        out_specs=[pl.BlockSpec((tm, D), row),
                   pl.BlockSpec((tm, tn), lambda i, j: (i, jnp.minimum(j, nq - 1))),
                   pl.BlockSpec((tm, tn), lambda i, j: (i, jnp.maximum(j - nq, 0)))],
        out_shape=[jax.ShapeDtypeStruct((S, D), BF16), jax.ShapeDtypeStruct((S, 1536), BF16),
                   jax.ShapeDtypeStruct((S, 4096), BF16)],
        scratch_shapes=[pltpu.VMEM((tm, D), BF16)],
        compiler_params=_cp("parallel", "arbitrary"),
    )(x, vec, w_in)


def proj_bwd(dq, dkv, dxr, d3, w_in, x, dxo, vec, name, tm=2048, tk=512):
    S = x.shape[0]
    tm = min(tm, S)
    nk = PW // tk

    def body(dq_ref, dkv_ref, dxr_ref, d3_ref, w_ref, x_hbm, dxo_hbm, vec_ref, dx_ref, vacc_ref, acc, xs, dxos, sems):
        i, j = pl.program_id(0), pl.program_id(1)
        tok = pl.ds(pl.multiple_of(i * tm, 8), tm)
        late = (pltpu.make_async_copy(x_hbm.at[tok], xs, sems.at[0]),
                pltpu.make_async_copy(dxo_hbm.at[tok], dxos, sems.at[1]))

        @pl.when((i == 0) & (j == 0))
        def _():
            vacc_ref[...] = jnp.zeros_like(vacc_ref)

        @pl.when(j == 0)
        def _():
            for cp in late:
                cp.start()
            acc[...] = _dot_nt(dq_ref[...], w_ref[...])

        @pl.when((j >= 1) & (j < 3))
        def _():
            acc[...] += _dot_nt(dkv_ref[...], w_ref[...])

        @pl.when((j >= 3) & (j < 5))
        def _():
            acc[...] += _dot_nt(dxr_ref[...], w_ref[...])

        @pl.when(j >= 5)
        def _():
            acc[...] += _dot_nt(d3_ref[...], w_ref[...])

        @pl.when(j == nk - 1)
        def _():
            for cp in late:
                cp.wait()
            dx_ref[...] = _pre_norm_bwd(acc[...], xs[...], dxos[...], vec_ref, vacc_ref)

    row = lambda i, j: (i, 0)
    return pl.pallas_call(
        body, name=name, grid=(S // tm, nk),
        in_specs=[pl.BlockSpec((None, tm, tk), lambda i, j: (0, i, 0)),
                  pl.BlockSpec((None, tm, tk), lambda i, j: (jnp.clip(j - 1, 0, 1), i, 0)),
                  pl.BlockSpec((tm, tk), lambda i, j: (i, jnp.clip(j - 3, 0, 1))),
                  pl.BlockSpec((None, tm, tk), lambda i, j: (jnp.clip(j - 5, 0, 5) // 2, i, jnp.clip(j - 5, 0, 5) % 2)),
                  pl.BlockSpec((D, tk), lambda i, j: (0, j)),
                  ANY, ANY,
                  pl.BlockSpec((8, D), lambda i, j: (0, 0))],
        out_specs=[pl.BlockSpec((tm, D), row, pipeline_mode=pl.Buffered(1)),
                   pl.BlockSpec((8, D), lambda i, j: (0, 0))],
        out_shape=[jax.ShapeDtypeStruct((S, D), F32), jax.ShapeDtypeStruct((8, D), F32)],
        scratch_shapes=[pltpu.VMEM((tm, D), F32), pltpu.VMEM((tm, D), F32), pltpu.VMEM((tm, D), F32),
                        pltpu.SemaphoreType.DMA((2,))],
        compiler_params=_cp("arbitrary", "arbitrary"),
    )(dq, dkv, dxr, d3, w_in, x, dxo, vec)


def _two_heads(v, lane):
    zero = jnp.zeros((), v.dtype)
    return jnp.concatenate([jnp.where(lane < 64, v, zero), jnp.where(lane >= 64, v, zero)], axis=0)


def _attn_scores(qm, ka, bias_h, i, grp):
    s = _dot_nt(qm, ka) + bias_h
    col = lax.broadcasted_iota(jnp.int32, s.shape, 1)
    first_key = jnp.where(i == 0, 512 - 128 * grp, 0)
    return jnp.where(col >= first_key, s, NEG)


def _softmax(s):
    e = jnp.exp(s - jnp.max(s, axis=-1, keepdims=True))
    return e * (1.0 / jnp.sum(e, axis=-1, keepdims=True))


NG = TQ // 128


def attn_fwd(qkv, bias, name):
    S = qkv.shape[0]
    nb = S // TQ

    def body(q_ref, kp_ref, kc_ref, vp_ref, vc_ref, b_ref, o_ref, kw, vw):
        i = pl.program_id(1)
        kw[0:TQ, :] = kp_ref[...]
        kw[TQ:2 * TQ, :] = kc_ref[...]
        vw[0:TQ, :] = vp_ref[...]
        vw[TQ:2 * TQ, :] = vc_ref[...]
        lane = lax.broadcasted_iota(jnp.int32, (1, HP), 1)

        rows = [pl.ds(128 * a, 128) for a in range(NG)]
        keys = [pl.ds(128 * a, WIN) for a in range(NG)]
        q2 = [_two_heads(q_ref[r, :] * jnp.asarray(0.125, BF16), lane) for r in rows]
        s = [_attn_scores(q2[a], kw[keys[a], :], b_ref[...], i, a) for a in range(NG)]
        p = [_softmax(sa).astype(BF16) for sa in s]
        o2 = [_dot(p[a], vw[keys[a], :]) for a in range(NG)]
        for a in range(NG):
            o_ref[rows[a], :] = jnp.where(lane < 64, o2[a][0:128], o2[a][128:256]).astype(BF16)

    prev = lambda h, i: (jnp.maximum(i - 1, 0), 0)
    return pl.pallas_call(
        body, name=name, grid=(4, nb),
        in_specs=[pl.BlockSpec((TQ, HP), lambda h, i: (i, h)),
                  pl.BlockSpec((TQ, HP), lambda h, i: (jnp.maximum(i - 1, 0), 4 + h)),
                  pl.BlockSpec((TQ, HP), lambda h, i: (i, 4 + h)),
                  pl.BlockSpec((TQ, HP), lambda h, i: (jnp.maximum(i - 1, 0), 8 + h)),
                  pl.BlockSpec((TQ, HP), lambda h, i: (i, 8 + h)),
                  pl.BlockSpec((None, 256, WIN), lambda h, i: (h, 0, 0))],
        out_specs=pl.BlockSpec((TQ, HP), lambda h, i: (i, h)),
        out_shape=jax.ShapeDtypeStruct((S, 512), BF16),
        scratch_shapes=[pltpu.VMEM((2 * TQ, HP), BF16), pltpu.VMEM((2 * TQ, HP), BF16)],
        compiler_params=_cp("parallel", "arbitrary"),
    )(qkv, qkv, qkv, qkv, qkv, bias)


def attn_bwd(qkv, do, bias, name):
    S = qkv.shape[0]
    nb = S // TQ

    def body(q_ref, kp_ref, kc_ref, vp_ref, vc_ref, do_ref, b_ref, dqkv_ref, db_ref, dkv_ref, kw, vw, ak, av):
        i = pl.program_id(1)

        @pl.when(i == 0)
        def _():
            db_ref[...] = jnp.zeros_like(db_ref)
            ak[...] = jnp.zeros_like(ak)
            av[...] = jnp.zeros_like(av)

        @pl.when(i > 0)
        def _():
            ak[0:TQ, :] = ak[TQ:2 * TQ, :]
            av[0:TQ, :] = av[TQ:2 * TQ, :]
            ak[TQ:2 * TQ, :] = jnp.zeros((TQ, HP), F32)
            av[TQ:2 * TQ, :] = jnp.zeros((TQ, HP), F32)

        @pl.when(i < nb)
        def _():
            kw[0:TQ, :] = kp_ref[...]
            kw[TQ:2 * TQ, :] = kc_ref[...]
            vw[0:TQ, :] = vp_ref[...]
            vw[TQ:2 * TQ, :] = vc_ref[...]
            lane = lax.broadcasted_iota(jnp.int32, (1, HP), 1)

            rows = [pl.ds(128 * a, 128) for a in range(NG)]
            keys = [pl.ds(128 * a, WIN) for a in range(NG)]
            q2 = [_two_heads(q_ref[r, :] * jnp.asarray(0.125, BF16), lane) for r in rows]
            do2 = [_two_heads(do_ref[r, :], lane) for r in rows]
            s = [_attn_scores(q2[a], kw[keys[a], :], b_ref[...], i, a) for a in range(NG)]
            dp = [_dot_nt(do2[a], vw[keys[a], :]) for a in range(NG)]
            p = [_softmax(sa) for sa in s]
            ds = [p[a] * (dp[a] - jnp.sum(p[a] * dp[a], axis=-1, keepdims=True)) for a in range(NG)]
            db_ref[...] += (ds[0] + ds[1]) + (ds[2] + ds[3])
            dsb = [d.astype(BF16) for d in ds]
            dq2 = [_dot(dsb[a], kw[keys[a], :]) for a in range(NG)]
            dk = [_dot_tn(dsb[a], q2[a]) for a in range(NG)]
            dv = [_dot_tn(p[a].astype(BF16), do2[a]) for a in range(NG)]
            for a in range(NG):
                ak[keys[a], :] += dk[a]
                av[keys[a], :] += dv[a]
                dq = jnp.where(lane < 64, dq2[a][0:128], dq2[a][128:256])
                dqkv_ref[0, rows[a], :] = (dq * 0.125).astype(BF16)

        @pl.when(i > 0)
        def _():
            dkv_ref[0] = ak[0:TQ, :].astype(BF16)
            dkv_ref[1] = av[0:TQ, :].astype(BF16)

    cur = lambda i: jnp.minimum(i, nb - 1)
    prv = lambda i: jnp.clip(i - 1, 0, nb - 1)
    dq, db, dkv = pl.pallas_call(
        body, name=name, grid=(4, nb + 1),
        in_specs=[pl.BlockSpec((TQ, HP), lambda h, i: (cur(i), h)),
                  pl.BlockSpec((TQ, HP), lambda h, i: (prv(i), 4 + h)),
                  pl.BlockSpec((TQ, HP), lambda h, i: (cur(i), 4 + h)),
                  pl.BlockSpec((TQ, HP), lambda h, i: (prv(i), 8 + h)),
                  pl.BlockSpec((TQ, HP), lambda h, i: (cur(i), 8 + h)),
                  pl.BlockSpec((TQ, HP), lambda h, i: (cur(i), h)),
                  pl.BlockSpec((None, 256, WIN), lambda h, i: (h, 0, 0))],
        out_specs=[pl.BlockSpec((1, TQ, HP), lambda h, i: (0, cur(i), h)),
                   pl.BlockSpec((None, 256, WIN), lambda h, i: (h, 0, 0)),
                   pl.BlockSpec((2, TQ, HP), lambda h, i: (0, prv(i), h))],
        out_shape=[jax.ShapeDtypeStruct((1, S, 512), BF16), jax.ShapeDtypeStruct((4, 256, WIN), F32),
                   jax.ShapeDtypeStruct((2, S, 512), BF16)],
        scratch_shapes=[pltpu.VMEM((2 * TQ, HP), BF16), pltpu.VMEM((2 * TQ, HP), BF16),
                        pltpu.VMEM((2 * TQ, HP), F32), pltpu.VMEM((2 * TQ, HP), F32)],
        compiler_params=_cp("parallel", "arbitrary"),
    )(qkv, qkv, qkv, qkv, qkv, do, bias)
    return dq, db, dkv


def bias_grad(db, name):
    def body(db_ref, o_ref):
        r = lax.broadcasted_iota(jnp.int32, (128, 128), 0)
        c = lax.broadcasted_iota(jnp.int32, (128, 128), 1)
        flip = (r + c == 127).astype(BF16)
        lane = lax.broadcasted_iota(jnp.int32, (16, 384), 1)
        src = lax.broadcasted_iota(jnp.int32, (128, 384), 0)
        dst = lax.broadcasted_iota(jnp.int32, (128, 384), 1)

        def split_dot(v, m):
            hi = v.astype(BF16)
            r1 = v - hi.astype(F32)
            mid = r1.astype(BF16)
            lo = (r1 - mid.astype(F32)).astype(BF16)
            return _dot(hi, m) + _dot(mid, m) + _dot(lo, m)

        def diag_sums(w):
            y = pltpu.roll(split_dot(w, flip), 0, 1, stride=1, stride_axis=0)
            return jnp.broadcast_to(_colsum(y), (16, 128))

        w4 = db_ref[0, :, 512:640]
        w3 = db_ref[0, :, 384:512]
        far = jnp.sum(db_ref[0, :, 0:384]) + jnp.sum(jnp.where(r >= c, w3, 0.0))
        lo4 = diag_sums(jnp.where(r >= c, w4, 0.0))
        up4 = diag_sums(jnp.where(r < c, w4, 0.0))
        up3 = diag_sums(jnp.where(r < c, w3, 0.0))
        p_lo4 = (dst == 128 + (src + 1) % 128).astype(BF16)
        p_up4 = ((dst == src + 1) & (src < 127)).astype(BF16)
        p_up3 = ((dst == src + 129) & (src < 127)).astype(BF16)
        out = split_dot(lo4, p_lo4) + split_dot(up4, p_up4) + split_dot(up3, p_up3)
        o_ref[0] = out + jnp.where(lane == 256, far, 0.0)

    return pl.pallas_call(
        body, name=name, grid=(8,),
        in_specs=[pl.BlockSpec((1, 128, WIN), lambda h: (h, 0, 0))],
        out_specs=pl.BlockSpec((1, 16, 384), lambda h: (h, 0, 0)),
        out_shape=jax.ShapeDtypeStruct((8, 16, 384), F32),
        compiler_params=_cp("parallel"),
    )(db)[:, 0, :]


LT = 1024
LC = 512


def _lru_gates(xs, pv_ref, wa_ref, wx_ref, tl):
    xc = (pv_ref[4:5, :] + pv_ref[3:4, :] * xs[pl.ds(8, tl), :] + pv_ref[2:3, :] * xs[pl.ds(7, tl), :]
          + pv_ref[1:2, :] * xs[pl.ds(6, tl), :] + pv_ref[0:1, :] * xs[pl.ds(5, tl), :])
    xcb = xc.astype(BF16)
    pa = jnp.concatenate([_dot(xcb[:, 0:256], wa_ref[0]), _dot(xcb[:, 256:512], wa_ref[1])], axis=1)
    px = jnp.concatenate([_dot(xcb[:, 0:256], wx_ref[0]), _dot(xcb[:, 256:512], wx_ref[1])], axis=1)
    r = _sigmoid(pa + pv_ref[5:6, :])
    ig = _sigmoid(px + pv_ref[6:7, :])
    z = -pv_ref[7:8, :]
    sp = jnp.maximum(z, 0.0) + jnp.log1p(jnp.exp(-jnp.abs(z)))
    log_a = (-LRU_C * r) * sp
    a = jnp.exp(log_a)
    s = jnp.tanh(-log_a) * (1.0 + a * a)
    inv_mult = lax.rsqrt(s)
    mult = jnp.where(s > 0.0, s * inv_mult, 0.0)
    return xc, xcb, r, ig, sp, a, mult, inv_mult


def lru_fwd(rest, pvec, wa, wx, name):
    S = rest.shape[0]
    tl = min(LT, S)
    nt = S // tl

    def body(xr_ref, halo_ref, yr_ref, pv_ref, wa_ref, wx_ref, h_ref, hg_ref, xs, a_s, u_s, h_s, carry):
        ti = pl.program_id(1)

        @pl.when(ti == 0)
        def _():
            carry[...] = jnp.zeros_like(carry)

        xs[0:8, :] = jnp.where(ti > 0, halo_ref[8:16, :].astype(F32), 0.0)
        xs[pl.ds(8, tl), :] = xr_ref[...].astype(F32)
        xc, _, _, ig, _, a, mult, _ = _lru_gates(xs, pv_ref, wa_ref, wx_ref, tl)
        a_s[...] = a
        u_s[...] = mult * (ig * xc)
        row = lax.broadcasted_iota(jnp.int32, (8, LC), 0)

        def blk(bi, c):
            o = pl.multiple_of(bi * 8, 8)
            av = a_s[pl.ds(o, 8), :]
            bv = u_s[pl.ds(o, 8), :]
            for d in (1, 2, 4):
                a_sh = pltpu.roll(av, d, 0)
                b_sh = pltpu.roll(bv, d, 0)
                m = row >= d
                bv = jnp.where(m, av * b_sh + bv, bv)
                av = jnp.where(m, av * a_sh, av)
            hv = bv + av * c
            h_s[pl.ds(o, 8), :] = hv
            return hv[7:8, :]

        carry[...] = lax.fori_loop(0, tl // 8, blk, carry[...])
        h = h_s[...]
        h_ref[...] = h
        hg_ref[...] = (h * _gelu(yr_ref[...].astype(F32))).astype(BF16)

    hb = tl // 16
    return pl.pallas_call(
        body, name=name, grid=(2, nt),
        in_specs=[pl.BlockSpec((tl, LC), lambda c, t: (t, c)),
                  pl.BlockSpec((16, LC), lambda c, t: (jnp.maximum(t * hb - 1, 0), c)),
                  pl.BlockSpec((tl, LC), lambda c, t: (t, 2 + c)),
                  pl.BlockSpec((8, LC), lambda c, t: (0, c)),
                  pl.BlockSpec((2, 256, 256), lambda c, t: (c, 0, 0)),
                  pl.BlockSpec((2, 256, 256), lambda c, t: (c, 0, 0))],
        out_specs=[pl.BlockSpec((tl, LC), lambda c, t: (t, c)), pl.BlockSpec((tl, LC), lambda c, t: (t, c))],
        out_shape=[jax.ShapeDtypeStruct((S, D), F32), jax.ShapeDtypeStruct((S, D), BF16)],
        scratch_shapes=[pltpu.VMEM((tl + 8, LC), F32), pltpu.VMEM((tl, LC), F32), pltpu.VMEM((tl, LC), F32),
                        pltpu.VMEM((tl, LC), F32), pltpu.VMEM((1, LC), F32)],
        compiler_params=_cp("parallel", "arbitrary"),
    )(rest, rest, rest, pvec, wa, wx)


def lru_bwd(dh, h, rest, pvec, wa, wx, name):
    S = rest.shape[0]
    tl = min(LT, S)
    nt = S // tl

    def body(dh_ref, h_ref, hhalo_ref, xr_ref, xhalo_ref, pv_ref, wa_ref, wx_ref,
             dxr_ref, vacc_ref, dwa_ref, dwx_ref,
             xs, hs, a_s, ash_s, b_s, lam_s, dxe, anext, lnext, dxnext):
        ti = pl.program_id(1)
        tr = nt - 1 - ti

        @pl.when(ti == 0)
        def _():
            anext[...] = jnp.zeros_like(anext)
            lnext[...] = jnp.zeros_like(lnext)
            dxnext[...] = jnp.zeros_like(dxnext)
            vacc_ref[...] = jnp.zeros_like(vacc_ref)
            dwa_ref[...] = jnp.zeros_like(dwa_ref)
            dwx_ref[...] = jnp.zeros_like(dwx_ref)

        xs[0:8, :] = jnp.where(tr > 0, xhalo_ref[8:16, :].astype(F32), 0.0)
        xs[pl.ds(8, tl), :] = xr_ref[...].astype(F32)
        xc, xcb, r, ig, sp, a, mult, inv_mult = _lru_gates(xs, pv_ref, wa_ref, wx_ref, tl)

        a_s[pl.ds(0, tl), :] = a
        a_s[pl.ds(tl, 8), :] = jnp.broadcast_to(anext[...], (8, LC))
        ash_s[...] = a_s[pl.ds(1, tl), :]
        b_s[...] = dh_ref[...]
        row = lax.broadcasted_iota(jnp.int32, (8, LC), 0)

        def blk(k, c):
            o = pl.multiple_of((tl // 8 - 1 - k) * 8, 8)
            av = ash_s[pl.ds(o, 8), :]
            bv = b_s[pl.ds(o, 8), :]
            for d in (1, 2, 4):
                a_sh = pltpu.roll(av, 8 - d, 0)
                b_sh = pltpu.roll(bv, 8 - d, 0)
                m = row < 8 - d
                bv = jnp.where(m, bv + av * b_sh, bv)
                av = jnp.where(m, av * a_sh, av)
            lv = bv + av * c
            lam_s[pl.ds(o, 8), :] = lv
            return lv[0:1, :]

        lnext[...] = lax.fori_loop(0, tl // 8, blk, lnext[...])
        anext[...] = a[0:1, :]
        lam = lam_s[...]

        hs[0:8, :] = jnp.where(tr > 0, hhalo_ref[...], 0.0)
        hs[pl.ds(8, tl), :] = h_ref[...]
        d_a = lam * hs[pl.ds(7, tl), :]
        d_mult = lam * (ig * xc)
        d_ig = lam * mult * xc
        dxc = lam * mult * ig
        d_log_a = d_a * a - d_mult * (a * a) * inv_mult
        d_r = d_log_a * (-LRU_C * sp)
        vacc_ref[7:8, :] += _colsum(d_log_a * (-LRU_C * r)) * (-_sigmoid(-pv_ref[7:8, :]))
        d_pa = d_r * r * (1.0 - r)
        d_px = d_ig * ig * (1.0 - ig)
        vacc_ref[5:6, :] += _colsum(d_pa)
        vacc_ref[6:7, :] += _colsum(d_px)
        dpa = d_pa.astype(BF16)
        dpx = d_px.astype(BF16)
        back = []
        for g in range(2):
            sl = slice(256 * g, 256 * g + 256)
            dwa_ref[g] += _dot_tn(xcb[:, sl], dpa[:, sl])
            dwx_ref[g] += _dot_tn(xcb[:, sl], dpx[:, sl])
            back.append(_dot_nt(dpa[:, sl], wa_ref[g]) + _dot_nt(dpx[:, sl], wx_ref[g]))
        dxc = dxc + jnp.concatenate(back, axis=1)
        vacc_ref[4:5, :] += _colsum(dxc)
        for k in range(4):
            vacc_ref[k:k + 1, :] += _colsum(dxc * xs[pl.ds(5 + k, tl), :])
        dxe[pl.ds(0, tl), :] = dxc
        dxe[pl.ds(tl, 8), :] = dxnext[...]
        dxr = (pv_ref[3:4, :] * dxc + pv_ref[2:3, :] * dxe[pl.ds(1, tl), :]
               + pv_ref[1:2, :] * dxe[pl.ds(2, tl), :] + pv_ref[0:1, :] * dxe[pl.ds(3, tl), :])
        dxr_ref[...] = dxr.astype(BF16)
        dxnext[...] = dxc[0:8, :]

    hb = tl // 8
    rev = lambda t: nt - 1 - t
    halo = lambda t: jnp.maximum(rev(t) * hb - 1, 0)
    big = lambda: pltpu.VMEM((tl + 8, LC), F32)
    til = lambda: pltpu.VMEM((tl, LC), F32)
    return pl.pallas_call(
        body, name=name, grid=(2, nt),
        in_specs=[pl.BlockSpec((tl, LC), lambda c, t: (rev(t), c)),
                  pl.BlockSpec((tl, LC), lambda c, t: (rev(t), c)),
                  pl.BlockSpec((8, LC), lambda c, t: (halo(t), c)),
                  pl.BlockSpec((tl, LC), lambda c, t: (rev(t), c)),
                  pl.BlockSpec((16, LC), lambda c, t: (jnp.maximum(rev(t) * (tl // 16) - 1, 0), c)),
                  pl.BlockSpec((8, LC), lambda c, t: (0, c)),
                  pl.BlockSpec((2, 256, 256), lambda c, t: (c, 0, 0)),
                  pl.BlockSpec((2, 256, 256), lambda c, t: (c, 0, 0))],
        out_specs=[pl.BlockSpec((tl, LC), lambda c, t: (rev(t), c)),
                   pl.BlockSpec((8, LC), lambda c, t: (0, c)),
                   pl.BlockSpec((2, 256, 256), lambda c, t: (c, 0, 0)),
                   pl.BlockSpec((2, 256, 256), lambda c, t: (c, 0, 0))],
        out_shape=[jax.ShapeDtypeStruct((S, D), BF16), jax.ShapeDtypeStruct((8, D), F32),
                   jax.ShapeDtypeStruct((4, 256, 256), F32), jax.ShapeDtypeStruct((4, 256, 256), F32)],
        scratch_shapes=[big(), big(), big(), til(), til(), til(), big(),
                        pltpu.VMEM((1, LC), F32), pltpu.VMEM((1, LC), F32), pltpu.VMEM((8, LC), F32)],
        compiler_params=_cp("parallel", "arbitrary"),
    )(dh, h, h, rest, rest, pvec, wa, wx)


def mix_out_fwd(x, ao, hg, rest, vec, w_att_o, w_rec_o, w_out, name, tm=512):
    S = x.shape[0]
    tm = min(tm, S)

    def body(x_ref, ao_ref, hg_ref, ga_ref, gr_ref, vec_ref, wa_ref, wr_ref, wo_ref,
             xo_ref, att_ref, rec_ref, mg_ref, f_ref):
        att = _dot(ao_ref[...], wa_ref[...])
        rec = _dot(hg_ref[...], wr_ref[...])
        att_ref[...] = att.astype(BF16)
        rec_ref[...] = rec.astype(BF16)
        mg = (_sigmoid(ga_ref[...].astype(F32)) * att + _sigmoid(gr_ref[...].astype(F32)) * rec).astype(BF16)
        mg_ref[...] = mg
        f = _dot(mg, wo_ref[...])
        f_ref[...] = f.astype(BF16)
        y = f * lax.rsqrt(_mean(f * f) + EPS) * vec_ref[1:2, :]
        xo_ref[...] = x_ref[...] + (1.0 * vec_ref[4:5, :]) * y

    row = lambda i: (i, 0)
    full = lambda r: pl.BlockSpec((r, D), lambda i: (0, 0))
    return pl.pallas_call(
        body, name=name, grid=(S // tm,),
        in_specs=[pl.BlockSpec((tm, D), row), pl.BlockSpec((tm, 512), row), pl.BlockSpec((tm, D), row),
                  pl.BlockSpec((tm, D), lambda i: (i, 2)), pl.BlockSpec((tm, D), lambda i: (i, 3)),
                  full(8), full(512), full(D), full(D)],
        out_specs=[pl.BlockSpec((tm, D), row)] * 5,
        out_shape=[jax.ShapeDtypeStruct((S, D), F32)] + [jax.ShapeDtypeStruct((S, D), BF16)] * 4,
        compiler_params=_cp("parallel"),
    )(x, ao, hg, rest, rest, vec, w_att_o, w_rec_o, w_out)


def mix_out_bwd(dxo, f, att, rec, rest, h, vec, w_att_o, w_rec_o, w_out, name, tm=512):
    S = dxo.shape[0]
    tm = min(tm, S)

    def body(dxo_ref, f_ref, att_ref, rec_ref, yr_ref, ga_ref, gr_ref, h_ref, vec_ref, wa_ref, wr_ref, wo_ref,
             df_ref, da_ref, dr_ref, dao_ref, dh_ref, d3_ref, vacc_ref):
        @pl.when(pl.program_id(0) == 0)
        def _():
            vacc_ref[...] = jnp.zeros_like(vacc_ref)

        df = _post_norm_bwd(dxo_ref[...], f_ref[...].astype(F32), 1.0, vec_ref, vacc_ref).astype(BF16)
        df_ref[...] = df
        dm = _dot_nt(df, wo_ref[...])
        sa = _sigmoid(ga_ref[...].astype(F32))
        sr = _sigmoid(gr_ref[...].astype(F32))
        d_att = (dm * sa).astype(BF16)
        d_rec = (dm * sr).astype(BF16)
        da_ref[...] = d_att
        dr_ref[...] = d_rec
        d3_ref[1] = (dm * att_ref[...].astype(F32) * (sa * (1.0 - sa))).astype(BF16)
        d3_ref[2] = (dm * rec_ref[...].astype(F32) * (sr * (1.0 - sr))).astype(BF16)
        dao_ref[...] = _dot_nt(d_att, wa_ref[...]).astype(BF16)
        d_hg = _dot_nt(d_rec, wr_ref[...])
        yr = yr_ref[...].astype(F32)
        t = jnp.tanh(_GK * (yr + 0.044715 * yr * yr * yr))
        dh_ref[...] = d_hg * (0.5 * yr * (1.0 + t))
        gelu_grad = 0.5 * (1.0 + t) + 0.5 * yr * (1.0 - t * t) * _GK * (1.0 + 3.0 * 0.044715 * yr * yr)
        d3_ref[0] = (d_hg * h_ref[...] * gelu_grad).astype(BF16)

    row = lambda i: (i, 0)
    full = lambda r: pl.BlockSpec((r, D), lambda i: (0, 0))
    return pl.pallas_call(
        body, name=name, grid=(S // tm,),
        in_specs=[pl.BlockSpec((tm, D), row)] * 4
        + [pl.BlockSpec((tm, D), lambda i: (i, 1)), pl.BlockSpec((tm, D), lambda i: (i, 2)),
           pl.BlockSpec((tm, D), lambda i: (i, 3)), pl.BlockSpec((tm, D), row),
           full(8), full(512), full(D), full(D)],
        out_specs=[pl.BlockSpec((tm, D), row)] * 3
        + [pl.BlockSpec((tm, 512), row), pl.BlockSpec((tm, D), row),
           pl.BlockSpec((3, tm, D), lambda i: (0, i, 0)), pl.BlockSpec((8, D), lambda i: (0, 0))],
        out_shape=[jax.ShapeDtypeStruct((S, D), BF16)] * 3
        + [jax.ShapeDtypeStruct((S, 512), BF16), jax.ShapeDtypeStruct((S, D), F32),
           jax.ShapeDtypeStruct((3, S, D), BF16), jax.ShapeDtypeStruct((8, D), F32)],
        compiler_params=_cp("arbitrary"),
    )(dxo, f, att, rec, rest, rest, rest, h, vec, w_att_o, w_rec_o, w_out)


def dw_in(h, dq, dkv, dxr, d3, name, tk=1024, tn=512):
    S = h.shape[0]
    tk = min(tk, S)
    nk = S // tk

    def body(h_ref, dq_ref, dkv_ref, dxr_ref, d3_ref, o_ref, acc):
        j, k = pl.program_id(0), pl.program_id(1)

        @pl.when(k == 0)
        def _():
            acc[...] = jnp.zeros_like(acc)

        @pl.when(j == 0)
        def _():
            acc[...] += _dot_tn(h_ref[pl.ds(pl.multiple_of(k * tk, 16), tk), :], dq_ref[...])

        @pl.when((j >= 1) & (j < 3))
        def _():
            acc[...] += _dot_tn(h_ref[pl.ds(pl.multiple_of(k * tk, 16), tk), :], dkv_ref[...])

        @pl.when((j >= 3) & (j < 5))
        def _():
            acc[...] += _dot_tn(h_ref[pl.ds(pl.multiple_of(k * tk, 16), tk), :], dxr_ref[...])

        @pl.when(j >= 5)
        def _():
            acc[...] += _dot_tn(h_ref[pl.ds(pl.multiple_of(k * tk, 16), tk), :], d3_ref[...])

        @pl.when(k == nk - 1)
        def _():
            o_ref[...] = acc[...].astype(BF16)

    use = lambda j, k, lo, hi: jnp.where((j >= lo) & (j < hi), k, 0)
    g3 = lambda j: jnp.clip(j - 5, 0, 5)
    return pl.pallas_call(
        body, name=name, grid=(PW // tn, nk),
        in_specs=[pl.BlockSpec((S, D), lambda j, k: (0, 0), pipeline_mode=pl.Buffered(1)),
                  pl.BlockSpec((None, tk, tn), lambda j, k: (0, use(j, k, 0, 1), 0)),
                  pl.BlockSpec((None, tk, tn), lambda j, k: (jnp.clip(j - 1, 0, 1), use(j, k, 1, 3), 0)),
                  pl.BlockSpec((tk, tn), lambda j, k: (use(j, k, 3, 5), jnp.clip(j - 3, 0, 1))),
                  pl.BlockSpec((None, tk, tn), lambda j, k: (g3(j) // 2, use(j, k, 5, 11), g3(j) % 2))],
        out_specs=pl.BlockSpec((D, tn), lambda j, k: (0, j)),
        out_shape=jax.ShapeDtypeStruct((D, PW), BF16),
        scratch_shapes=[pltpu.VMEM((D, tn), F32)],
        compiler_params=_cp("parallel", "arbitrary"),
    )(h, dq, dkv, dxr, d3)


def ada_fwd(c_all, w_ada, b_ada, name, tn=768):
    n = w_ada.shape[1]

    def body(c_ref, w_ref, b_ref, o_ref):
        cv = c_ref[...]
        ca = (cv * _sigmoid(cv)).astype(BF16)
        o_ref[...] = _dot(ca, w_ref[...].astype(BF16)) + b_ref[...]

    return pl.pallas_call(
        body, name=name, grid=(n // tn,),
        in_specs=[pl.BlockSpec((8, D), lambda j: (0, 0)), pl.BlockSpec((D, tn), lambda j: (0, j)),
                  pl.BlockSpec((1, tn), lambda j: (0, j))],
        out_specs=pl.BlockSpec((8, tn), lambda j: (0, j)),
        out_shape=jax.ShapeDtypeStruct((8, n), F32),
        compiler_params=_cp("parallel"),
    )(c_all, w_ada, b_ada)


def ada_bwd(c_all_t, dmod, name, tn=768):
    n = dmod.shape[1]

    def body(c_ref, d_ref, o_ref):
        cv = c_ref[...]
        ca = (cv * _sigmoid(cv)).astype(BF16)
        o_ref[...] = _dot(ca, d_ref[...].astype(BF16))

    return pl.pallas_call(
        body, name=name, grid=(n // tn,),
        in_specs=[pl.BlockSpec((D, 128), lambda j: (0, 0)), pl.BlockSpec((128, tn), lambda j: (0, j))],
        out_specs=pl.BlockSpec((D, tn), lambda j: (0, j)),
        out_shape=jax.ShapeDtypeStruct((D, n), F32),
        compiler_params=_cp("parallel"),
    )(c_all_t, dmod)


def _row_tile(rows, cols, itemsize=4, budget=1536 * 1024):
    best = None
    for t in range(8, rows + 1, 8):
        if rows % t == 0 and t * cols * itemsize <= budget:
            best = t
    return rows if best is None else best


def sum_lead(parts, name, out_dtype=F32):
    n, R, C = parts.shape
    tr = _row_tile(R, C * n)

    def body(p_ref, o_ref):
        acc = p_ref[0].astype(F32)
        for k in range(1, n):
            acc = acc + p_ref[k].astype(F32)
        o_ref[...] = acc.astype(out_dtype)

    return pl.pallas_call(
        body, name=name, grid=(R // tr,),
        in_specs=[pl.BlockSpec((n, tr, C), lambda i: (0, i, 0))],
        out_specs=pl.BlockSpec((tr, C), lambda i: (i, 0)),
        out_shape=jax.ShapeDtypeStruct((R, C), out_dtype),
        compiler_params=_cp("parallel"),
    )(parts)


def adamw(w, g, m, v, name, emit_g=False):
    R, C = w.shape
    tr = _row_tile(R, C * 8, budget=8 * 1024 * 1024)

    def body(w_ref, g_ref, m_ref, v_ref, d_ref, mo_ref, vo_ref, *go_ref):
        gv = g_ref[...]
        if emit_g:
            go_ref[0][...] = gv
        mn = ADAM_B1 * m_ref[...] + (1.0 - ADAM_B1) * gv
        vn = ADAM_B2 * v_ref[...] + (1.0 - ADAM_B2) * (gv * gv)
        m_hat = mn / (1.0 - ADAM_B1 ** ADAM_STEP)
        v_hat = vn / (1.0 - ADAM_B2 ** ADAM_STEP)
        d_ref[...] = -ADAM_LR * (m_hat / (jnp.sqrt(v_hat) + ADAM_EPS) + ADAM_WD * w_ref[...])
        mo_ref[...] = mn
        vo_ref[...] = vn

    spec = pl.BlockSpec((tr, C), lambda i: (i, 0))
    return pl.pallas_call(
        body, name=name, grid=(R // tr,),
        in_specs=[spec] * 4, out_specs=[spec] * (4 if emit_g else 3),
        out_shape=[jax.ShapeDtypeStruct((R, C), F32)] * (4 if emit_g else 3),
        compiler_params=_cp("parallel"),
    )(w, g, m, v)


def _mesh_pos():
    return lax.axis_index("x"), lax.axis_index("y"), lax.axis_index("c")


def _other_chips(mx, my):
    return [(1 - mx, my), (mx, 1 - my), (1 - mx, 1 - my)]


def ag_small(x, name):
    R = x.shape[0]

    def body(x_ref, out_ref, send_sems, recv_sems, local_sem):
        mx, my, mc = _mesh_pos()
        me, sibling = (mx, my, mc), (mx, my, 1 - mc)
        chips = _other_chips(mx, my)

        def slot(px, py, pc):
            return out_ref.at[4 * px + 2 * py + pc]

        def copy(k, block, to, src=None):
            return pltpu.make_async_remote_copy(
                src_ref=slot(*block) if src is None else src, dst_ref=slot(*block),
                send_sem=send_sems.at[k], recv_sem=recv_sems.at[k], device_id=to, device_id_type=MESH)

        mine = pltpu.make_async_copy(x_ref, slot(*me), local_sem)
        mine.start()
        first = [copy(0, me, sibling, src=x_ref)]
        first += [copy(1 + j, me, (*chip, mc), src=x_ref) for j, chip in enumerate(chips)]
        for cp in first:
            cp.start()
        passed = [copy(4 + j, (*chip, mc), sibling) for j, chip in enumerate(chips)]
        for j, chip in enumerate(chips):
            copy(1 + j, (*chip, mc), me).wait_recv()
            passed[j].start()
        copy(0, sibling, me).wait_recv()
        for j, chip in enumerate(chips):
            copy(4 + j, (*chip, 1 - mc), me).wait_recv()
        for cp in first + passed:
            cp.wait_send()
        mine.wait()

    return pl.pallas_call(
        body, name=name,
        out_shape=jax.ShapeDtypeStruct((N_DEV, R, 128), F32),
        in_specs=[pl.BlockSpec(memory_space=pltpu.VMEM)],
        out_specs=pl.BlockSpec(memory_space=pltpu.VMEM),
        scratch_shapes=[pltpu.SemaphoreType.DMA((7,)), pltpu.SemaphoreType.DMA((7,)), pltpu.SemaphoreType.DMA],
        compiler_params=pltpu.CompilerParams(vmem_limit_bytes=VMEM_LIMIT),
    )(x)


BIG = (("ffn1_w_gu", "col", D, PW), ("ffn1_w_down", "row", FF, D), ("w_in", "col", D, PW),
       ("w_att_o", "col", 512, D), ("w_rec_o", "row", D, D), ("w_out", "row", D, D),
       ("ffn2_w_gu", "col", D, PW), ("ffn2_w_down", "row", FF, D))
NBIG = len(BIG)


def _shard_shape(kind, R, C):
    return (R, C // 4) if kind == "col" else (R // 4, C)


def _region(ref, kind, R, C, q, half, t, tr):
    sr, sc = _shard_shape(kind, R, C)
    if kind == "col":
        return ref.at[pl.ds(pl.multiple_of(half * (R // 2) + t * tr, 16), tr), pl.ds(q * sc, sc)]
    return ref.at[pl.ds(pl.multiple_of(q * sr + t * tr, 16), tr), pl.ds(half * (C // 2), C // 2)]


def ag_local(w, kind, R, C, p_arr, name, after=()):
    sr, sc = _shard_shape(kind, R, C)
    tr = _row_tile(sr, sc, budget=2 * 1024 * 1024)
    nt = sr // tr
    after = list(after)

    def body(p_ref, w_ref, *rest):
        rest[-1][...] = w_ref[...].astype(BF16)

    if kind == "col":
        o_spec = pl.BlockSpec((tr, sc), lambda i, p: (i, p[0]))
    else:
        o_spec = pl.BlockSpec((tr, sc), lambda i, p: (p[0] * nt + i, 0))
    return pl.pallas_call(
        body, name=name,
        grid_spec=pltpu.PrefetchScalarGridSpec(
            num_scalar_prefetch=1, grid=(nt,),
            in_specs=[pl.BlockSpec((tr, sc), lambda i, p: (i, 0))] + [ANY] * len(after), out_specs=o_spec),
        out_shape=jax.ShapeDtypeStruct((R, C), BF16),
        compiler_params=_cp("parallel"),
    )(p_arr, w, *after)


HBM_SPEC = pl.BlockSpec(memory_space=pltpu.HBM)
SEM_SPEC = pl.BlockSpec(memory_space=pltpu.SEMAPHORE)


def _ag_sems(geoms):
    return sum(6 if both else 3 for (_, _, _, both) in geoms)


def _ag_copies(fulls, geoms, ssem, rsem, mx, my, mc, q, h):
    chips = _other_chips(mx, my)
    out, base = [], 0
    for w, (kind, R, C, both) in enumerate(geoms):
        sr, sc = _shard_shape(kind, R, C)
        hr = sr // 2 if kind == "col" else sr
        reg = _region(fulls[w], kind, R, C, q, h, 0, hr)
        out.append([pltpu.make_async_remote_copy(
            src_ref=reg, dst_ref=reg, send_sem=ssem.at[base + 3 * t + k], recv_sem=rsem.at[base + 3 * t + k],
            device_id=(*chips[k], mc if t == 0 else 1 - mc), device_id_type=MESH)
            for t in range(2 if both else 1) for k in range(3)])
        base += 6 if both else 3
    return out


def ag_start(fulls, geoms, after, name):
    n = len(fulls)
    after = list(after)
    m = len(after)

    def body(*refs):
        ssem, rsem = refs[n + m:n + m + 2]
        outs, token = refs[n + m + 2:2 * n + m + 2], refs[2 * n + m + 2]
        mx, my, mc = _mesh_pos()
        p = 2 * mx + my
        col = [w for w, g in enumerate(geoms) if g[0] == "col"]
        row = [w for w, g in enumerate(geoms) if g[0] == "row"]
        for q in range(4):
            @pl.when(p == q)
            def _(q=q):
                cps = _ag_copies(outs, geoms, ssem, rsem, mx, my, mc, q, mc)
                for w in col:
                    for cp in cps[w]:
                        cp.start()
        for h in range(2):
            @pl.when(mc == h)
            def _(h=h):
                cps = _ag_copies(outs, geoms, ssem, rsem, mx, my, mc, p, h)
                for w in row:
                    for cp in cps[w]:
                        cp.start()
        token[...] = jnp.zeros_like(token)

    res = pl.pallas_call(
        body, name=name,
        out_shape=[pltpu.SemaphoreType.DMA((_ag_sems(geoms),)), pltpu.SemaphoreType.DMA((_ag_sems(geoms),))]
        + [pltpu.HBM(a.shape, a.dtype) for a in fulls] + [jax.ShapeDtypeStruct((8, 128), F32)],
        in_specs=[HBM_SPEC] * n + [ANY] * m,
        out_specs=[SEM_SPEC, SEM_SPEC] + [HBM_SPEC] * n + [pl.BlockSpec(memory_space=pltpu.VMEM)],
        input_output_aliases={w: 2 + w for w in range(n)},
        compiler_params=pltpu.CompilerParams(has_side_effects=pltpu.SideEffectType.DATAFLOW_SIDE_EFFECTING),
    )(*[pltpu.with_memory_space_constraint(a, pltpu.HBM) for a in fulls], *after)
    return res[0], res[1], list(res[2:2 + n]), res[2 + n]


def ag_wait(fulls, geoms, ssem, rsem, after, name):
    n = len(fulls)
    after = list(after) if isinstance(after, (list, tuple)) else [after]

    def body(*refs):
        ins, ssem_ref, rsem_ref = refs[:n], refs[n], refs[n + 1]
        mx, my, mc = _mesh_pos()
        for cps in _ag_copies(ins, geoms, ssem_ref, rsem_ref, mx, my, mc, 0, 0):
            for cp in cps:
                cp.wait_send()
                cp.wait_recv()

    return list(pl.pallas_call(
        body, name=name,
        out_shape=[pltpu.HBM(a.shape, a.dtype) for a in fulls],
        in_specs=[HBM_SPEC] * n + [SEM_SPEC, SEM_SPEC] + [ANY] * len(after),
        out_specs=[HBM_SPEC] * n,
        input_output_aliases={w: w for w in range(n)},
        compiler_params=pltpu.CompilerParams(has_side_effects=pltpu.SideEffectType.DATAFLOW_SIDE_EFFECTING),
    )(*fulls, ssem, rsem, *after))


def ag_forward(full, kind, R, C, name):
    sr, sc = _shard_shape(kind, R, C)
    hr, hc = (sr // 2, sc) if kind == "col" else (sr, sc // 2)
    tr = _row_tile(hr, hc, itemsize=2, budget=2 * 1024 * 1024)
    nt = hr // tr
    total = 3 * nt

    def body(src_ref, full_ref, stage, lsem, ssem, rsem):
        step = pl.program_id(0) * nt + pl.program_id(1)
        par = step % 2
        mx, my, mc = _mesh_pos()

        def load(s, q, h, t):
            return pltpu.make_async_copy(_region(src_ref, kind, R, C, q, h, t, tr), stage.at[s], lsem.at[s])

        def push(s, q, h, t):
            return pltpu.make_async_remote_copy(src_ref=stage.at[s], dst_ref=_region(full_ref, kind, R, C, q, h, t, tr),
                                                send_sem=ssem.at[s], recv_sem=rsem, device_id=(mx, my, 1 - mc),
                                                device_id_type=MESH)

        def for_tile(stp, fn):
            q_k = _partner_chip(stp // nt, 2 * mx + my)
            if kind == "col":
                for q in range(4):
                    @pl.when(q_k == q)
                    def _(q=q):
                        fn(q, mc, stp % nt)
            else:
                for h in range(2):
                    @pl.when(mc == h)
                    def _(h=h):
                        fn(q_k, h, stp % nt)

        @pl.when(step == 0)
        def _():
            for_tile(step, lambda q, h, t: load(0, q, h, t).start())

        load(par, 0, 0, 0).wait()
        for_tile(step, lambda q, h, t: push(par, q, h, t).start())

        @pl.when(step + 1 < total)
        def _():
            @pl.when(step >= 1)
            def _():
                push(1 - par, 0, 0, 0).wait_send()
            for_tile(step + 1, lambda q, h, t: load(1 - par, q, h, t).start())

        @pl.when(step == total - 1)
        def _():
            push(par, 0, 0, 0).wait_send()
            push(1 - par, 0, 0, 0).wait_send()
            three = full_ref.at[pl.ds(0, hr), pl.ds(0, 3 * hc)] if kind == "col" else full_ref.at[pl.ds(0, 3 * hr), pl.ds(0, hc)]
            pltpu.make_async_remote_copy(src_ref=three, dst_ref=three, send_sem=ssem.at[0], recv_sem=rsem,
                                         device_id=(mx, my, 1 - mc), device_id_type=MESH).wait_recv()

    return pl.pallas_call(
        body, name=name, grid=(3, nt),
        in_specs=[ANY], out_specs=ANY,
        out_shape=jax.ShapeDtypeStruct((R, C), BF16),
        scratch_shapes=[pltpu.VMEM((2, tr, hc), BF16), pltpu.SemaphoreType.DMA((2,)), pltpu.SemaphoreType.DMA((2,)),
                        pltpu.SemaphoreType.DMA],
        input_output_aliases={0: 0},
        compiler_params=_cp("arbitrary", "arbitrary"),
    )(full)


def _half_shape(kind, R, C):
    return (R // 2, C) if kind == "col" else (R, C // 2)


def _piece_shape(kind, R, C):
    return (R // 2, C // 4) if kind == "col" else (R // 4, C // 2)


def pair_push(g, kind, c_arr, name):
    R, C = g.shape
    hr, hc = _half_shape(kind, R, C)
    tr = _row_tile(hr, hc, itemsize=2, budget=2 * 1024 * 1024)
    nt = hr // tr

    def body(c_ref, g_ref, out_ref, stage, ssem, rsem):
        i = pl.program_id(0)
        slot = i % 2
        mx, my, mc = _mesh_pos()

        def push(s, t):
            return pltpu.make_async_remote_copy(
                src_ref=stage.at[s], dst_ref=out_ref.at[pl.ds(pl.multiple_of(t * tr, 16), tr)],
                send_sem=ssem.at[s], recv_sem=rsem, device_id=(mx, my, 1 - mc), device_id_type=MESH)

        @pl.when(i >= 2)
        def _():
            push(slot, 0).wait_send()

        stage[slot] = g_ref[...]
        push(slot, i).start()

        @pl.when(i == nt - 1)
        def _():
            push(slot, 0).wait_send()
            if nt >= 2:
                push(1 - slot, 0).wait_send()
            pltpu.make_async_remote_copy(src_ref=out_ref, dst_ref=out_ref, send_sem=ssem.at[0], recv_sem=rsem,
                                         device_id=(mx, my, 1 - mc), device_id_type=MESH).wait_recv()

    if kind == "col":
        g_spec = pl.BlockSpec((tr, hc), lambda i, c: ((1 - c[0]) * nt + i, 0))
    else:
        g_spec = pl.BlockSpec((tr, hc), lambda i, c: (i, 1 - c[0]))
    return pl.pallas_call(
        body, name=name,
        grid_spec=pltpu.PrefetchScalarGridSpec(
            num_scalar_prefetch=1, grid=(nt,), in_specs=[g_spec], out_specs=ANY,
            scratch_shapes=[pltpu.VMEM((2, tr, hc), BF16), pltpu.SemaphoreType.DMA((2,)), pltpu.SemaphoreType.DMA]),
        out_shape=jax.ShapeDtypeStruct((hr, hc), BF16),
        compiler_params=_cp("arbitrary"),
    )(c_arr, g)


def _partner_chip(k, p):
    return p ^ jnp.where(k == 0, 2, jnp.where(k == 1, 1, jnp.where(k == 2, 3, 0)))


def pair_add(g, got, kind, cp_arr, name):
    R, C = g.shape
    pr, pc = _piece_shape(kind, R, C)
    tr = _row_tile(pr, pc, itemsize=2, budget=2 * 1024 * 1024)
    nt = pr // tr

    def body(cp_ref, g_ref, got_ref, ps_ref, rb_ref):
        tile = (g_ref[...].astype(F32) + got_ref[...].astype(F32)).astype(BF16)
        ps_ref[...] = tile

        @pl.when(pl.program_id(1) == cp_ref[1])
        def _():
            rb_ref[...] = tile

    if kind == "col":
        g_spec = pl.BlockSpec((tr, pc), lambda i, q, cp: (cp[0] * nt + i, q))
        got_spec = pl.BlockSpec((tr, pc), lambda i, q, cp: (i, q))
    else:
        g_spec = pl.BlockSpec((tr, pc), lambda i, q, cp: (q * nt + i, cp[0]))
        got_spec = pl.BlockSpec((tr, pc), lambda i, q, cp: (q * nt + i, 0))
    return pl.pallas_call(
        body, name=name,
        grid_spec=pltpu.PrefetchScalarGridSpec(
            num_scalar_prefetch=1, grid=(nt, 4), in_specs=[g_spec, got_spec],
            out_specs=[pl.BlockSpec((None, tr, pc), lambda i, q, cp: (q, i, 0)),
                       pl.BlockSpec((None, tr, pc), lambda i, q, cp: (cp[1], i, 0))]),
        out_shape=[jax.ShapeDtypeStruct((4, pr, pc), BF16)] * 2,
        compiler_params=_cp("arbitrary", "arbitrary"),
    )(cp_arr, g, got)


def _rs_copies(ps, rb, ssem, rsem, mx, my, mc):
    p = 2 * mx + my
    out = []
    for w in range(len(ps)):
        for k, chip in enumerate(_other_chips(mx, my)):
            out.append(pltpu.make_async_remote_copy(
                src_ref=ps[w].at[2 * chip[0] + chip[1]], dst_ref=rb[w].at[p], send_sem=ssem.at[3 * w + k],
                recv_sem=rsem.at[3 * w + k], device_id=(*chip, mc), device_id_type=MESH))
    return out


def rs_start(ps, rb, after, name):
    n = len(ps)
    after = list(after)
    m = len(after)

    def body(*refs):
        ssem, rsem = refs[2 * n + m:2 * n + m + 2]
        ps_o = refs[2 * n + m + 2:3 * n + m + 2]
        rb_o = refs[3 * n + m + 2:4 * n + m + 2]
        token = refs[4 * n + m + 2]
        for cp in _rs_copies(ps_o, rb_o, ssem, rsem, *_mesh_pos()):
            cp.start()
        token[...] = jnp.zeros_like(token)

    both = list(ps) + list(rb)
    res = pl.pallas_call(
        body, name=name,
        out_shape=[pltpu.SemaphoreType.DMA((3 * n,)), pltpu.SemaphoreType.DMA((3 * n,))]
        + [pltpu.HBM(a.shape, a.dtype) for a in both] + [jax.ShapeDtypeStruct((8, 128), F32)],
        in_specs=[HBM_SPEC] * (2 * n) + [ANY] * m,
        out_specs=[SEM_SPEC, SEM_SPEC] + [HBM_SPEC] * (2 * n) + [pl.BlockSpec(memory_space=pltpu.VMEM)],
        input_output_aliases={w: 2 + w for w in range(2 * n)},
        compiler_params=pltpu.CompilerParams(has_side_effects=pltpu.SideEffectType.DATAFLOW_SIDE_EFFECTING),
    )(*[pltpu.with_memory_space_constraint(a, pltpu.HBM) for a in both], *after)
    return res[0], res[1], list(res[2:2 + n]), list(res[2 + n:2 + 2 * n]), res[2 + 2 * n]


def rs_wait(ps, rb, ssem, rsem, after, name):
    n = len(ps)
    after = list(after)
    m = len(after)

    def body(*refs):
        ps_i, rb_i = refs[:n], refs[n:2 * n]
        ssem_ref, rsem_ref = refs[2 * n], refs[2 * n + 1]
        for cp in _rs_copies(ps_i, rb_i, ssem_ref, rsem_ref, *_mesh_pos()):
            cp.wait_send()
            cp.wait_recv()

    both = list(ps) + list(rb)
    res = pl.pallas_call(
        body, name=name,
        out_shape=[pltpu.HBM(a.shape, a.dtype) for a in both],
        in_specs=[HBM_SPEC] * (2 * n) + [SEM_SPEC, SEM_SPEC] + [ANY] * m,
        out_specs=[HBM_SPEC] * (2 * n),
        input_output_aliases={w: w for w in range(2 * n)},
        compiler_params=pltpu.CompilerParams(has_side_effects=pltpu.SideEffectType.DATAFLOW_SIDE_EFFECTING),
    )(*both, ssem, rsem, *after)
    return list(res[n:])


def sum_share(parts, kind, R, C, name):
    _, pr, pc = parts.shape
    sr, sc = _shard_shape(kind, R, C)
    tr = _row_tile(pr, pc * 4, budget=8 * 1024 * 1024)
    nt = pr // tr

    def body(p_ref, fin_ref, stage, lsem, ssem, rsem):
        i = pl.program_id(0)
        slot = i % 2
        mx, my, mc = _mesh_pos()

        def region(h, t):
            r0 = pl.multiple_of(t * tr, 8)
            if kind == "col":
                return fin_ref.at[pl.ds(pl.multiple_of(h * pr + r0, 8), tr)]
            return fin_ref.at[pl.ds(r0, tr), pl.ds(h * pc, pc)]

        def copies(s, h, t):
            return (pltpu.make_async_copy(stage.at[s], region(h, t), lsem.at[s]),
                    pltpu.make_async_remote_copy(src_ref=stage.at[s], dst_ref=region(h, t), send_sem=ssem.at[s],
                                                 recv_sem=rsem, device_id=(mx, my, 1 - mc), device_id_type=MESH))

        def wait_sent(s):
            loc, rem = copies(s, 0, 0)
            loc.wait()
            rem.wait_send()

        @pl.when(i >= 2)
        def _():
            wait_sent(slot)

        acc = p_ref[0].astype(F32)
        for k in range(1, 4):
            acc = acc + p_ref[k].astype(F32)
        stage[slot] = acc
        if kind == "col":
            for cp in copies(slot, mc, i):
                cp.start()
        else:
            for h in range(2):
                @pl.when(mc == h)
                def _(h=h):
                    for cp in copies(slot, h, i):
                        cp.start()

        @pl.when(i == nt - 1)
        def _():
            wait_sent(slot)
            if nt >= 2:
                wait_sent(1 - slot)
            half = fin_ref.at[pl.ds(0, pr), pl.ds(0, pc)]
            pltpu.make_async_remote_copy(src_ref=half, dst_ref=half, send_sem=ssem.at[0], recv_sem=rsem,
                                         device_id=(mx, my, 1 - mc), device_id_type=MESH).wait_recv()

    return pl.pallas_call(
        body, name=name, grid=(nt,),
        in_specs=[pl.BlockSpec((4, tr, pc), lambda i: (0, i, 0))],
        out_specs=ANY,
        out_shape=jax.ShapeDtypeStruct((sr, sc), F32),
        scratch_shapes=[pltpu.VMEM((2, tr, pc), F32), pltpu.SemaphoreType.DMA((2,)), pltpu.SemaphoreType.DMA((2,)),
                        pltpu.SemaphoreType.DMA],
        compiler_params=_cp("arbitrary"),
    )(parts)


def _pack(parts, rows):
    flat = []
    for a in parts:
        a = jnp.ravel(a).astype(F32)
        flat.append(jnp.pad(a, (0, (-a.shape[0]) % 128)))
    v = jnp.concatenate(flat)
    return jnp.pad(v, (0, rows * 128 - v.shape[0])).reshape(rows, 128)


def _unpack(block, shapes):
    lead = block.shape[:-2]
    v = block.reshape(lead + (-1,))
    out, off = [], 0
    for shp in shapes:
        n = int(np.prod(shp))
        out.append(v[..., off:off + n].reshape(lead + tuple(shp)))
        off += n + (-n) % 128
    return out


def _block_diag4(w):
    w4 = w.reshape(4, 4, 64, 64)
    eye = jnp.eye(4, dtype=w.dtype)
    return (w4[:, :, :, None, :] * eye[None, :, None, :, None]).reshape(4, 256, 256)


def _diag_blocks(bd):
    b5 = bd.reshape(4, 4, 64, 4, 64)
    return jnp.stack([b5[:, i, :, i, :] for i in range(4)], axis=1).reshape(16, 64, 64)


def _bias_window(rel_bias):
    m = (np.arange(768) + 127) % 768 - 127
    w = rel_bias[:, np.clip(512 - m, -128, 128) + 128]
    win = jnp.tile(w, (1, 128))[:, :128 * 767].reshape(8, 128, 767)[:, :, :WIN]
    qh = np.arange(128)[:, None] // CHUNK
    kc = np.arange(WIN)[None, :] // CHUNK
    valid = (kc >= qh) & (kc <= qh + 8)
    return jnp.where(jnp.asarray(valid)[None], win, NEG)


SMALL = ("b_ada", "norm_pre", "norm_post", "rel_bias", "conv_w", "conv_b", "lru_wa", "lru_ba", "lru_wx",
         "lru_bx", "lru_lambda")
WEIGHTS = ("w_ada", "b_ada", "norm_pre", "norm_post", "ffn1_w_gu", "ffn1_w_down", "w_in", "rel_bias", "conv_w",
           "conv_b", "lru_wa", "lru_ba", "lru_wx", "lru_bx", "lru_lambda", "w_att_o", "w_rec_o", "w_out",
           "ffn2_w_gu", "ffn2_w_down")


def kernel(x, c, w_ada, b_ada, norm_pre, norm_post, ffn1_w_gu, ffn1_w_down, w_in, rel_bias, conv_w, conv_b, lru_wa, lru_ba, lru_wx, lru_bx, lru_lambda, w_att_o, w_rec_o, w_out, ffn2_w_gu, ffn2_w_down, loss_target, m_w_ada, m_b_ada, m_norm_pre, m_norm_post, m_ffn1_w_gu, m_ffn1_w_down, m_w_in, m_rel_bias, m_conv_w, m_conv_b, m_lru_wa, m_lru_ba, m_lru_wx, m_lru_bx, m_lru_lambda, m_w_att_o, m_w_rec_o, m_w_out, m_ffn2_w_gu, m_ffn2_w_down, v_w_ada, v_b_ada, v_norm_pre, v_norm_post, v_ffn1_w_gu, v_ffn1_w_down, v_w_in, v_rel_bias, v_conv_w, v_conv_b, v_lru_wa, v_lru_ba, v_lru_wx, v_lru_bx, v_lru_lambda, v_w_att_o, v_w_rec_o, v_w_out, v_ffn2_w_gu, v_ffn2_w_down):
    W = dict(w_ada=w_ada, b_ada=b_ada, norm_pre=norm_pre, norm_post=norm_post, ffn1_w_gu=ffn1_w_gu,
             ffn1_w_down=ffn1_w_down, w_in=w_in, rel_bias=rel_bias, conv_w=conv_w, conv_b=conv_b, lru_wa=lru_wa,
             lru_ba=lru_ba, lru_wx=lru_wx, lru_bx=lru_bx, lru_lambda=lru_lambda, w_att_o=w_att_o, w_rec_o=w_rec_o,
             w_out=w_out, ffn2_w_gu=ffn2_w_gu, ffn2_w_down=ffn2_w_down)
    M = dict(w_ada=m_w_ada, b_ada=m_b_ada, norm_pre=m_norm_pre, norm_post=m_norm_post, ffn1_w_gu=m_ffn1_w_gu,
             ffn1_w_down=m_ffn1_w_down, w_in=m_w_in, rel_bias=m_rel_bias, conv_w=m_conv_w, conv_b=m_conv_b,
             lru_wa=m_lru_wa, lru_ba=m_lru_ba, lru_wx=m_lru_wx, lru_bx=m_lru_bx, lru_lambda=m_lru_lambda,
             w_att_o=m_w_att_o, w_rec_o=m_w_rec_o, w_out=m_w_out, ffn2_w_gu=m_ffn2_w_gu, ffn2_w_down=m_ffn2_w_down)
    V = dict(w_ada=v_w_ada, b_ada=v_b_ada, norm_pre=v_norm_pre, norm_post=v_norm_post, ffn1_w_gu=v_ffn1_w_gu,
             ffn1_w_down=v_ffn1_w_down, w_in=v_w_in, rel_bias=v_rel_bias, conv_w=v_conv_w, conv_b=v_conv_b,
             lru_wa=v_lru_wa, lru_ba=v_lru_ba, lru_wx=v_lru_wx, lru_bx=v_lru_bx, lru_lambda=v_lru_lambda,
             w_att_o=v_w_att_o, w_rec_o=v_w_rec_o, w_out=v_w_out, ffn2_w_gu=v_ffn2_w_gu, ffn2_w_down=v_ffn2_w_down)
    mx, my, mc = _mesh_pos()
    p = 2 * mx + my
    e = 4 * mx + 2 * my + mc
    xs = x[0]

    c_arr = jnp.reshape(mc, (1,)).astype(jnp.int32)
    cp_arr = jnp.stack([mc, p]).astype(jnp.int32)
    p_arr = jnp.reshape(p, (1,)).astype(jnp.int32)
    direct = ("w_att_o", "w_rec_o", "w_out", "ffn2_w_gu", "ffn2_w_down")
    geoms = [(kind, R, C, n in direct) for (n, kind, R, C) in BIG]
    names = [b[0] for b in BIG]
    placed = [ag_local(W[n][0], kind, R, C, p_arr, "ag_local_" + n) for (n, kind, R, C) in BIG[:2]]

    def arrived(fly, lo, hi, ssem, rsem, after, tag):
        done = ag_wait(fly, geoms[lo:hi], ssem, rsem, after, "ag_wait_" + tag)
        return [a if both else ag_forward(a, kind, R, C, "ag_forward_" + n)
                for a, (kind, R, C, both), n in zip(done, geoms[lo:hi], names[lo:hi])]

    g1 = ag_small(_pack([c, norm_pre, norm_post, conv_w], 32), "ag_small_params")
    c_all, npre4, npost4, cw4 = _unpack(g1, [(D,), (3, 256), (3, 256), (4, 256)])
    chipwise = lambda a: jnp.moveaxis(a[0::2], 0, 1).reshape(a.shape[1], D)
    npre, npost, conv_full = chipwise(npre4), chipwise(npost4), chipwise(cw4)

    b_cols = lax.dynamic_slice(b_ada, (0, p * 2304), (1, 2304))
    mod_cols = ada_fwd(c_all, w_ada[0], b_cols, "ada_fwd")
    g2 = ag_small(mod_cols.reshape(144, 128), "ag_mod")
    mod_all = jnp.moveaxis(g2[0::2].reshape(4, 8, 2304), 0, 1).reshape(8, 9 * D)
    mod = lax.dynamic_index_in_dim(mod_all, e, 0, keepdims=False).reshape(3, 3, D)
    zeros3 = jnp.zeros((3, D), F32)
    vecs = [jnp.concatenate([npre[k:k + 1], npost[k:k + 1], mod[k], zeros3], axis=0) for k in range(3)]

    gu_s, gu_r, gu_fly, tok_gu = ag_start(placed[:1], geoms[:1], [g2], "ag_start_ffn1_gu")
    dn_s, dn_r, dn_fly, tok0 = ag_start(placed[1:2], geoms[1:2], [tok_gu], "ag_start_ffn1_down")
    placed += [ag_local(W[n][0], kind, R, C, p_arr, "ag_local_" + n, after=[tok0]) for (n, kind, R, C) in BIG[2:]]
    f1_gu, = arrived(gu_fly, 0, 1, gu_s, gu_r, placed[2:], "ffn1_gu")
    f1_dn, = arrived(dn_fly, 1, 2, dn_s, dn_r, f1_gu, "ffn1_down")
    mix_s, mix_r, mix_fly, tok1 = ag_start(placed[2:6], geoms[2:6], [f1_gu, f1_dn], "ag_start_mixer")
    ffn_s, ffn_r, ffn_fly, tok2 = ag_start(placed[6:], geoms[6:], [tok1], "ag_start_ffn2")
    wa_bd = _block_diag4(lru_wa[0]).astype(BF16)
    wx_bd = _block_diag4(lru_wx[0]).astype(BF16)
    pvec = jnp.concatenate([conv_full, conv_b, lru_ba, lru_bx, lru_lambda], axis=0)
    bias = _bias_window(rel_bias[0]).reshape(4, 256, WIN)

    x1, h1, g1_, u1, a1, f1 = ffn_fwd(xs, vecs[0] + tok2[0:1, 0:1], f1_gu, f1_dn, 0.5, "ffn1_fwd")
    win, wao, wro, wout = arrived(mix_fly, 2, 6, mix_s, mix_r, x1, "mixer")
    h2, qkv, rest = proj_fwd(x1, vecs[1], win, "proj_fwd")
    ao = attn_fwd(qkv, bias, "attn_fwd")
    hl, hg = lru_fwd(rest, pvec, wa_bd, wx_bd, "lru_fwd")
    x2, att, rec, mg, f2 = mix_out_fwd(x1, ao, hg, rest, vecs[1], wao, wro, wout, "mix_out_fwd")
    f2_gu, f2_dn = arrived(ffn_fly, 6, 8, ffn_s, ffn_r, x2, "ffn2")
    dy, h3, g3_, u3, a3, f3, lvec = ffn_fwd(x2, vecs[2], f2_gu, f2_dn, 0.5, "ffn2_fwd", tgt=loss_target[0])

    G, grads = {}, {}
    geo = {n: (kind, R, C) for (n, kind, R, C) in BIG}

    def reduce_begin(names, tag):
        ps, rb = [], []
        for n in names:
            got = pair_push(G[n], geo[n][0], c_arr, "rs_push_" + n)
            a, b = pair_add(G[n], got, geo[n][0], cp_arr, "rs_pair_sum_" + n)
            ps.append(a)
            rb.append(b)
        return rs_start(ps, rb, [], "rs_start_" + tag)

    def reduce_end(names, flight, after, tag):
        ssem, rsem, ps, rb, _ = flight
        for a, n in zip(rs_wait(ps, rb, ssem, rsem, after, "rs_wait_" + tag), names):
            grads[n] = sum_share(a, *geo[n], "rs_sum_share_" + n)[None]

    dx2, df3, dgu3, va2 = ffn_bwd(dy, x2, f3, g3_, u3, vecs[2], f2_gu, f2_dn, 0.5, "ffn2_bwd")
    G["ffn2_w_gu"] = mm_tn(h3, dgu3, "dw_ffn2_gu", D, 1408, 2048, a_resident=True)
    G["ffn2_w_down"] = mm_tn(a3, df3, "dw_ffn2_down", 1408, D, 2048)
    fly_ffn2 = reduce_begin(("ffn2_w_gu", "ffn2_w_down"), "ffn2")
    vec1 = vecs[1] + fly_ffn2[4][0:1, 0:1]
    df2, d_att, d_rec, dao, dhl, d3, va_out = mix_out_bwd(dx2, f2, att, rec, rest, hl, vec1, wao, wro, wout,
                                                          "mix_out_bwd")
    G["w_out"] = mm_tn(mg, df2, "dw_out", D, D, 1024)
    G["w_att_o"] = mm_tn(ao, d_att, "dw_att_o", 512, D, 1024)
    G["w_rec_o"] = mm_tn(hg, d_rec, "dw_rec_o", D, D, 1024)
    dq, db, dkv = attn_bwd(qkv, dao, bias, "attn_bwd")
    dxr, v_lru, dwa_bd, dwx_bd = lru_bwd(dhl, hl, rest, pvec, wa_bd, wx_bd, "lru_bwd")
    dx1, va_in = proj_bwd(dq, dkv, dxr, d3, win, x1, dx2, vecs[1], "proj_bwd")
    G["w_in"] = dw_in(h2, dq, dkv, dxr, d3, "dw_in")
    fly_mix = reduce_begin(("w_in", "w_att_o", "w_rec_o", "w_out"), "mixer")
    vec0 = vecs[0] + fly_mix[4][0:1, 0:1]
    dx0, df1, dgu1, va0 = ffn_bwd(dx1, xs, f1, g1_, u1, vec0, f1_gu, f1_dn, 0.5, "ffn1_bwd")
    G["ffn1_w_gu"] = mm_tn(h1, dgu1, "dw_ffn1_gu", D, 1408, 2048, a_resident=True)
    G["ffn1_w_down"] = mm_tn(a1, df1, "dw_ffn1_down", 1408, D, 2048)
    fly_ffn1 = reduce_begin(("ffn1_w_gu", "ffn1_w_down"), "ffn1")
    reduce_end(("ffn2_w_gu", "ffn2_w_down"), fly_ffn2, [fly_ffn1[4]], "ffn2")
    reduce_end(("w_in", "w_att_o", "w_rec_o", "w_out"), fly_mix, [fly_ffn1[4], grads["ffn2_w_down"]], "mixer")

    va1 = va_out + va_in
    vas = (va0, va1, va2)
    dmod = jnp.stack([v[2:5] for v in vas])
    part = {"b_ada": dmod, "norm_pre": jnp.stack([v[0] for v in vas]), "norm_post": jnp.stack([v[1] for v in vas]),
            "rel_bias": bias_grad(db.reshape(8, 128, WIN), "bias_grad")[:, :257], "conv_w": v_lru[0:4], "conv_b": v_lru[4],
            "lru_wa": _diag_blocks(dwa_bd), "lru_ba": v_lru[5], "lru_wx": _diag_blocks(dwx_bd), "lru_bx": v_lru[6],
            "lru_lambda": v_lru[7]}
    full_shapes = {"b_ada": (9 * D,), "norm_pre": (3, D), "norm_post": (3, D), "rel_bias": (8, 257),
                   "conv_w": (4, D), "conv_b": (D,), "lru_wa": (16, 64, 64), "lru_ba": (D,),
                   "lru_wx": (16, 64, 64), "lru_bx": (D,), "lru_lambda": (D,)}
    g3 = ag_small(_pack([part[n] for n in SMALL] + [lvec[0:1, 0:1]], 1232), "ag_small_grads")
    summed = _unpack(sum_lead(g3, "sum_small_grads"), [full_shapes[n] for n in SMALL] + [(1,)])
    red = dict(zip(SMALL, summed[:-1]))
    loss = summed[-1][0]
    cols = lambda a: lax.dynamic_slice(a, (0, p * 256), (a.shape[0], 256))
    grads.update({"b_ada": red["b_ada"][None], "norm_pre": cols(red["norm_pre"])[None],
                  "norm_post": cols(red["norm_post"])[None], "rel_bias": red["rel_bias"][None],
                  "conv_w": cols(red["conv_w"])[None], "conv_b": red["conv_b"][None], "lru_wa": red["lru_wa"][None],
                  "lru_ba": red["lru_ba"][None], "lru_wx": red["lru_wx"][None], "lru_bx": red["lru_bx"][None],
                  "lru_lambda": red["lru_lambda"][None]})

    dmod_all = g3[:, :72].reshape(8, 9 * D)
    dmod_cols = jnp.pad(lax.dynamic_slice(dmod_all, (0, p * 2304), (8, 2304)), ((0, 120), (0, 0)))
    c_all_t = jnp.pad(c_all.T, ((0, 0), (0, 120)))
    grads["w_ada"] = ada_bwd(c_all_t, dmod_cols, "ada_bwd")[None]

    delta, new_m, new_v = {}, {}, {}

    def update(n):
        shp = W[n].shape
        res = adamw(W[n][0], grads[n][0], M[n][0], V[n][0], "adamw_" + n, emit_g=n in geo)
        delta[n], new_m[n], new_v[n] = [a.reshape(shp) for a in res[:3]]
        if n in geo:
            grads[n] = res[3].reshape(shp)

    for n in ("w_ada", "ffn2_w_gu", "ffn2_w_down", "w_in", "w_att_o", "w_rec_o", "w_out"):
        update(n)
    packed = [_pack([src[n] for n in SMALL], 1168) for src in (W, grads, M, V)]
    outs = adamw(*packed, "adamw_small")
    for dst, blk in zip((delta, new_m, new_v), outs):
        for n, a in zip(SMALL, _unpack(blk, [W[n].shape for n in SMALL])):
            dst[n] = a
    reduce_end(("ffn1_w_gu", "ffn1_w_down"), fly_ffn1,
               [outs[0], delta["w_ada"], delta["ffn2_w_gu"], delta["ffn2_w_down"], delta["w_in"], delta["w_out"]], "ffn1")
    for n in ("ffn1_w_gu", "ffn1_w_down"):
        update(n)

    return (loss, dx0[None], *[grads[n] for n in WEIGHTS], *[delta[n] for n in WEIGHTS],
            *[new_m[n] for n in WEIGHTS], *[new_v[n] for n in WEIGHTS])
```

```python
import functools

import numpy as np
import jax
import jax.numpy as jnp
from jax import lax
from jax.experimental import pallas as pl
from jax.experimental.pallas import tpu as pltpu

F32 = jnp.float32
BF16 = jnp.bfloat16

D = 1024
FF = 2816
PW = 5632
HP = 128
CHUNK = 64
WIN = 640
TQ = 512
EPS = 1e-6
NEG = -1e30
LRU_C = 8.0
N_DEV = 8
VMEM_LIMIT = 56 * 1024 * 1024

ADAM_LR, ADAM_B1, ADAM_B2, ADAM_EPS, ADAM_WD, ADAM_STEP = 0.001, 0.9, 0.999, 1e-08, 0.01, 10

MESH = pl.DeviceIdType.MESH
ANY = pl.BlockSpec(memory_space=pl.ANY)


def _cp(*sem):
    return pltpu.CompilerParams(dimension_semantics=tuple(sem), vmem_limit_bytes=VMEM_LIMIT)


def _dot(a, b):
    return jnp.dot(a, b, preferred_element_type=F32)


def _dot_nt(a, b):
    return lax.dot_general(a, b, (((1,), (1,)), ((), ())), preferred_element_type=F32)


def _dot_tn(a, b):
    return lax.dot_general(a, b, (((0,), (0,)), ((), ())), preferred_element_type=F32)


def _mean(v):
    return jnp.mean(v, axis=-1, keepdims=True)


def _colsum(v):
    return jnp.sum(v, axis=0, keepdims=True)


def _sigmoid(v):
    return 0.5 * jnp.tanh(0.5 * v) + 0.5


_GK = 0.7978845608028654


def _gelu(v):
    t = jnp.tanh(_GK * (v + 0.044715 * v * v * v))
    return 0.5 * v * (1.0 + t)


def _pre_norm(xv, vec_ref):
    r = lax.rsqrt(_mean(xv * xv) + EPS)
    n = xv * r * vec_ref[0:1, :]
    return n * (1.0 + vec_ref[3:4, :]) + vec_ref[2:3, :]


def _pre_norm_bwd(dh, xv, dres, vec_ref, vacc_ref):
    r = lax.rsqrt(_mean(xv * xv) + EPS)
    xh = xv * r
    n = xh * vec_ref[0:1, :]
    vacc_ref[2:3, :] += _colsum(dh)
    vacc_ref[3:4, :] += _colsum(dh * n)
    dn = dh * (1.0 + vec_ref[3:4, :])
    vacc_ref[0:1, :] += _colsum(dn * xh)
    dxh = dn * vec_ref[0:1, :]
    return r * (dxh - xh * _mean(dxh * xh)) + dres


def _post_norm_bwd(dxo, fv, res, vec_ref, vacc_ref):
    rf = lax.rsqrt(_mean(fv * fv) + EPS)
    fh = fv * rf
    gp = vec_ref[1:2, :]
    vacc_ref[4:5, :] += _colsum(res * dxo * (fh * gp))
    dy = (res * vec_ref[4:5, :]) * dxo
    vacc_ref[1:2, :] += _colsum(dy * fh)
    dfn = dy * gp
    return rf * (dfn - fh * _mean(dfn * fh))


def _u_spec(tf):
    return pl.BlockSpec((pl.Element(D), pl.Element(tf)),
                        lambda i, j: (0, pl.multiple_of(jnp.minimum(FF + j * tf, 2 * FF - tf), 128)))


def ffn_fwd(x, vec, w_gu, w_dn, res, name, tgt=None, tm=1024, tf=512):
    S = x.shape[0]
    tm = min(tm, S)
    nt = S // tm
    nf = -(-FF // tf)
    tail = FF - tf * (nf - 1)
    halves = [pl.ds(r * (tm // 2), tm // 2) for r in range(2)]
    head = tgt is not None

    def body(*refs):
        x_ref, vec_ref, wg_ref, wu_ref, wd_ref = refs[:5]
        if head:
            t_hbm, xo_ref, h_ref, g_ref, u_ref, a_ref, f_ref, l_ref, hs, acc, lacc, ts, tsem = refs[5:]
        else:
            xo_ref, h_ref, g_ref, u_ref, a_ref, f_ref, hs, acc = refs[5:]
        i, j = pl.program_id(0), pl.program_id(1)
        if head:
            late = pltpu.make_async_copy(t_hbm.at[pl.ds(pl.multiple_of(i * tm, 8), tm)], ts, tsem)

        @pl.when(j == 0)
        def _():
            if head:
                late.start()
            h = _pre_norm(x_ref[...], vec_ref).astype(BF16)
            hs[...] = h
            h_ref[...] = h
            acc[...] = jnp.zeros_like(acc)

        def chunk(w):
            gu = [(_dot(hs[r, :], wg_ref[:, 0:w]), _dot(hs[r, :], wu_ref[:, tf - w:tf])) for r in halves]
            acts = []
            for r, (g, u) in zip(halves, gu):
                g_ref[r, 0:w] = g.astype(BF16)
                u_ref[r, 0:w] = u.astype(BF16)
                a = (g * _sigmoid(g) * u).astype(BF16)
                a_ref[r, 0:w] = a
                acts.append(a)
            for r, a in zip(halves, acts):
                acc[r, :] += _dot(a, wd_ref[0:w, :])

        @pl.when(j < nf - 1)
        def _():
            chunk(tf)

        @pl.when(j == nf - 1)
        def _():
            chunk(tail)
            f = acc[...]
            f_ref[...] = f.astype(BF16)
            y = f * lax.rsqrt(_mean(f * f) + EPS) * vec_ref[1:2, :]
            xo = x_ref[...] + (res * vec_ref[4:5, :]) * y
            if head:
                @pl.when(i == 0)
                def _():
                    lacc[...] = jnp.zeros_like(lacc)

                late.wait()
                d = xo - ts[...]
                xo_ref[...] = d * (1.0 / D)
                lacc[...] += _colsum(d * d)

                @pl.when(i == nt - 1)
                def _():
                    l_ref[...] = jnp.broadcast_to(0.5 * jnp.sum(lacc[...]) * (1.0 / D), (8, 128))
            else:
                xo_ref[...] = xo

    row = lambda i, j: (i, 0)
    col = lambda i, j: (i, j)
    in_specs = [pl.BlockSpec((tm, D), row), pl.BlockSpec((8, D), lambda i, j: (0, 0)),
                pl.BlockSpec((D, tf), lambda i, j: (0, j)), _u_spec(tf),
                pl.BlockSpec((tf, D), lambda i, j: (j, 0))]
    out_specs = [pl.BlockSpec((tm, D), row), pl.BlockSpec((tm, D), row), pl.BlockSpec((tm, tf), col),
                 pl.BlockSpec((tm, tf), col), pl.BlockSpec((tm, tf), col), pl.BlockSpec((tm, D), row)]
    out_shape = [jax.ShapeDtypeStruct((S, D), F32), jax.ShapeDtypeStruct((S, D), BF16),
                 jax.ShapeDtypeStruct((S, FF), BF16), jax.ShapeDtypeStruct((S, FF), BF16),
                 jax.ShapeDtypeStruct((S, FF), BF16), jax.ShapeDtypeStruct((S, D), BF16)]
    scratch = [pltpu.VMEM((tm, D), BF16), pltpu.VMEM((tm, D), F32)]
    args = [x, vec, w_gu, w_gu, w_dn]
    if head:
        in_specs.append(ANY)
        out_specs.append(pl.BlockSpec((8, 128), lambda i, j: (0, 0)))
        out_shape.append(jax.ShapeDtypeStruct((8, 128), F32))
        scratch += [pltpu.VMEM((1, D), F32), pltpu.VMEM((tm, D), F32), pltpu.SemaphoreType.DMA]
        args.append(tgt)
    return pl.pallas_call(
        body, name=name, grid=(nt, nf), in_specs=in_specs, out_specs=out_specs, out_shape=out_shape,
        scratch_shapes=scratch,
        compiler_params=_cp("arbitrary" if head else "parallel", "arbitrary"),
    )(*args)


def ffn_bwd(dxo, x, f, g, u, vec, w_gu, w_dn, res, name, tm=1024, tf=512):
    S = x.shape[0]
    tm = min(tm, S)
    nf = -(-FF // tf)
    tail = FF - tf * (nf - 1)
    halves = [pl.ds(r * (tm // 2), tm // 2) for r in range(2)]

    def body(dxo_ref, x_hbm, f_ref, g_ref, u_ref, vec_ref, wg_ref, wu_ref, wd_ref,
             dx_ref, df_ref, dgu_ref, vacc_ref, dfs, acc, xs, xsem):
        i, j = pl.program_id(0), pl.program_id(1)
        late = pltpu.make_async_copy(x_hbm.at[pl.ds(pl.multiple_of(i * tm, 8), tm)], xs, xsem)

        @pl.when((i == 0) & (j == 0))
        def _():
            vacc_ref[...] = jnp.zeros_like(vacc_ref)

        @pl.when(j == 0)
        def _():
            late.start()
            df = _post_norm_bwd(dxo_ref[...], f_ref[...].astype(F32), res, vec_ref, vacc_ref).astype(BF16)
            dfs[...] = df
            df_ref[...] = df
            acc[...] = jnp.zeros_like(acc)

        def chunk(w):
            da = [_dot_nt(dfs[h, :], wd_ref[0:w, :]) for h in halves]
            dgu = []
            for h, d in zip(halves, da):
                gv, uv = g_ref[h, 0:w].astype(F32), u_ref[h, 0:w].astype(F32)
                sg = _sigmoid(gv)
                dg = (d * uv * (sg * (1.0 + gv * (1.0 - sg)))).astype(BF16)
                du = (d * (gv * sg)).astype(BF16)
                dgu_ref[0, h, 0:w] = dg
                dgu_ref[1, h, 0:w] = du
                dgu.append((dg, du))
            for h, (dg, du) in zip(halves, dgu):
                acc[h, :] += _dot_nt(dg, wg_ref[:, 0:w]) + _dot_nt(du, wu_ref[:, tf - w:tf])

        @pl.when(j < nf - 1)
        def _():
            chunk(tf)

        @pl.when(j == nf - 1)
        def _():
            chunk(tail)
            late.wait()
            dx_ref[...] = _pre_norm_bwd(acc[...], xs[...], dxo_ref[...], vec_ref, vacc_ref)

    row = lambda i, j: (i, 0)
    col = lambda i, j: (i, j)
    return pl.pallas_call(
        body, name=name, grid=(S // tm, nf),
        in_specs=[pl.BlockSpec((tm, D), row), ANY, pl.BlockSpec((tm, D), row),
                  pl.BlockSpec((tm, tf), col), pl.BlockSpec((tm, tf), col),
                  pl.BlockSpec((8, D), lambda i, j: (0, 0)),
                  pl.BlockSpec((D, tf), lambda i, j: (0, j)), _u_spec(tf),
                  pl.BlockSpec((tf, D), lambda i, j: (j, 0))],
        out_specs=[pl.BlockSpec((tm, D), row), pl.BlockSpec((tm, D), row),
                   pl.BlockSpec((2, tm, tf), lambda i, j: (0, i, j)),
                   pl.BlockSpec((8, D), lambda i, j: (0, 0))],
        out_shape=[jax.ShapeDtypeStruct((S, D), F32), jax.ShapeDtypeStruct((S, D), BF16),
                   jax.ShapeDtypeStruct((2, S, FF), BF16), jax.ShapeDtypeStruct((8, D), F32)],
        scratch_shapes=[pltpu.VMEM((tm, D), BF16), pltpu.VMEM((tm, D), F32), pltpu.VMEM((tm, D), F32),
                        pltpu.SemaphoreType.DMA],
        compiler_params=_cp("arbitrary", "arbitrary"),
    )(dxo, x, f, g, u, vec, w_gu, w_gu, w_dn)


def mm_tn(a, b, name, tm, tn, tk, out_dtype=BF16, a_resident=False):
    S, M = a.shape
    if b.ndim == 3:
        G, _, Nf = b.shape
    else:
        G, Nf = 1, b.shape[1]
    N = G * Nf
    tk = min(tk, S)
    nbf = Nf // tn
    nk = S // tk

    def body(a_ref, b_ref, o_ref, acc):
        k = pl.program_id(2)

        @pl.when(k == 0)
        def _():
            acc[...] = jnp.zeros_like(acc)

        a_blk = a_ref[pl.ds(pl.multiple_of(k * tk, 16), tk), :] if a_resident else a_ref[...]
        acc[...] += _dot_tn(a_blk, b_ref[...])

        @pl.when(k == nk - 1)
        def _():
            o_ref[...] = acc[...].astype(out_dtype)

    if b.ndim == 3:
        b_spec = pl.BlockSpec((None, tk, tn), lambda i, j, k: (j // nbf, k, j % nbf))
    else:
        b_spec = pl.BlockSpec((tk, tn), lambda i, j, k: (k, j))
    if a_resident:
        a_spec = pl.BlockSpec((S, M), lambda i, j, k: (0, 0), pipeline_mode=pl.Buffered(1))
    else:
        a_spec = pl.BlockSpec((tk, tm), lambda i, j, k: (k, i))
    return pl.pallas_call(
        body, name=name, grid=(M // tm, N // tn, nk),
        in_specs=[a_spec, b_spec],
        out_specs=pl.BlockSpec((tm, tn), lambda i, j, k: (i, j)),
        out_shape=jax.ShapeDtypeStruct((M, N), out_dtype),
        scratch_shapes=[pltpu.VMEM((tm, tn), F32)],
        compiler_params=_cp("parallel", "parallel", "arbitrary"),
    )(a, b)


def proj_fwd(x, vec, w_in, name, tm=2048, tn=512):
    S = x.shape[0]
    tm = min(tm, S)
    nq = 1536 // tn

    def body(x_ref, vec_ref, w_ref, h_ref, qkv_ref, rest_ref, hs):
        j = pl.program_id(1)

        @pl.when(j == 0)
        def _():
            h = _pre_norm(x_ref[...], vec_ref).astype(BF16)
            hs[...] = h
            h_ref[...] = h

        r = _dot(hs[...], w_ref[...])

        @pl.when(j < nq)
        def _():
            qkv_ref[...] = r.astype(BF16)

        @pl.when(j >= nq)
        def _():
            rest_ref[...] = r.astype(BF16)

    row = lambda i, j: (i, 0)
    return pl.pallas_call(
        body, name=name, grid=(S // tm, PW // tn),
        in_specs=[pl.BlockSpec((tm, D), row), pl.BlockSpec((8, D), lambda i, j: (0, 0)),
                  pl.BlockSpec((D, tn), lambda i, j: (0, j))],
        out_specs=[pl.BlockSpec((tm, D), row),
                   pl.BlockSpec((tm, tn), lambda i, j: (i, jnp.minimum(j, nq - 1))),
                   pl.BlockSpec((tm, tn), lambda i, j: (i, jnp.maximum(j - nq, 0)))],
        out_shape=[jax.ShapeDtypeStruct((S, D), BF16), jax.ShapeDtypeStruct((S, 1536), BF16),
                   jax.ShapeDtypeStruct((S, 4096), BF16)],
        scratch_shapes=[pltpu.VMEM((tm, D), BF16)],
        compiler_params=_cp("parallel", "arbitrary"),
    )(x, vec, w_in)


def proj_bwd(dq, dkv, dxr, d3, w_in, x, dxo, vec, name, tm=2048, tk=512):
    S = x.shape[0]
    tm = min(tm, S)
    nk = PW // tk

    def body(dq_ref, dkv_ref, dxr_ref, d3_ref, w_ref, x_hbm, dxo_hbm, vec_ref, dx_ref, vacc_ref, acc, xs, dxos, sems):
        i, j = pl.program_id(0), pl.program_id(1)
        tok = pl.ds(pl.multiple_of(i * tm, 8), tm)
        late = (pltpu.make_async_copy(x_hbm.at[tok], xs, sems.at[0]),
                pltpu.make_async_copy(dxo_hbm.at[tok], dxos, sems.at[1]))

        @pl.when((i == 0) & (j == 0))
        def _():
            vacc_ref[...] = jnp.zeros_like(vacc_ref)

        @pl.when(j == 0)
        def _():
            for cp in late:
                cp.start()
            acc[...] = _dot_nt(dq_ref[...], w_ref[...])

        @pl.when((j >= 1) & (j < 3))
        def _():
            acc[...] += _dot_nt(dkv_ref[...], w_ref[...])

        @pl.when((j >= 3) & (j < 5))
        def _():
            acc[...] += _dot_nt(dxr_ref[...], w_ref[...])

        @pl.when(j >= 5)
        def _():
            acc[...] += _dot_nt(d3_ref[...], w_ref[...])

        @pl.when(j == nk - 1)
        def _():
            for cp in late:
                cp.wait()
            dx_ref[...] = _pre_norm_bwd(acc[...], xs[...], dxos[...], vec_ref, vacc_ref)

    row = lambda i, j: (i, 0)
    return pl.pallas_call(
        body, name=name, grid=(S // tm, nk),
        in_specs=[pl.BlockSpec((None, tm, tk), lambda i, j: (0, i, 0)),
                  pl.BlockSpec((None, tm, tk), lambda i, j: (jnp.clip(j - 1, 0, 1), i, 0)),
                  pl.BlockSpec((tm, tk), lambda i, j: (i, jnp.clip(j - 3, 0, 1))),
                  pl.BlockSpec((None, tm, tk), lambda i, j: (jnp.clip(j - 5, 0, 5) // 2, i, jnp.clip(j - 5, 0, 5) % 2)),
                  pl.BlockSpec((D, tk), lambda i, j: (0, j)),
                  ANY, ANY,
                  pl.BlockSpec((8, D), lambda i, j: (0, 0))],
        out_specs=[pl.BlockSpec((tm, D), row, pipeline_mode=pl.Buffered(1)),
                   pl.BlockSpec((8, D), lambda i, j: (0, 0))],
        out_shape=[jax.ShapeDtypeStruct((S, D), F32), jax.ShapeDtypeStruct((8, D), F32)],
        scratch_shapes=[pltpu.VMEM((tm, D), F32), pltpu.VMEM((tm, D), F32), pltpu.VMEM((tm, D), F32),
                        pltpu.SemaphoreType.DMA((2,))],
        compiler_params=_cp("arbitrary", "arbitrary"),
    )(dq, dkv, dxr, d3, w_in, x, dxo, vec)


def _two_heads(v, lane):
    zero = jnp.zeros((), v.dtype)
    return jnp.concatenate([jnp.where(lane < 64, v, zero), jnp.where(lane >= 64, v, zero)], axis=0)


def _attn_scores(qm, ka, bias_h, i, grp):
    s = _dot_nt(qm, ka) + bias_h
    col = lax.broadcasted_iota(jnp.int32, s.shape, 1)
    first_key = jnp.where(i == 0, 512 - 128 * grp, 0)
    return jnp.where(col >= first_key, s, NEG)


def _softmax(s):
    e = jnp.exp(s - jnp.max(s, axis=-1, keepdims=True))
    return e * (1.0 / jnp.sum(e, axis=-1, keepdims=True))


NG = TQ // 128


def attn_fwd(qkv, bias, name):
    S = qkv.shape[0]
    nb = S // TQ

    def body(q_ref, kp_ref, kc_ref, vp_ref, vc_ref, b_ref, o_ref, kw, vw):
        i = pl.program_id(1)
        kw[0:TQ, :] = kp_ref[...]
        kw[TQ:2 * TQ, :] = kc_ref[...]
        vw[0:TQ, :] = vp_ref[...]
        vw[TQ:2 * TQ, :] = vc_ref[...]
        lane = lax.broadcasted_iota(jnp.int32, (1, HP), 1)

        rows = [pl.ds(128 * a, 128) for a in range(NG)]
        keys = [pl.ds(128 * a, WIN) for a in range(NG)]
        q2 = [_two_heads(q_ref[r, :] * jnp.asarray(0.125, BF16), lane) for r in rows]
        s = [_attn_scores(q2[a], kw[keys[a], :], b_ref[...], i, a) for a in range(NG)]
        p = [_softmax(sa).astype(BF16) for sa in s]
        o2 = [_dot(p[a], vw[keys[a], :]) for a in range(NG)]
        for a in range(NG):
            o_ref[rows[a], :] = jnp.where(lane < 64, o2[a][0:128], o2[a][128:256]).astype(BF16)

    prev = lambda h, i: (jnp.maximum(i - 1, 0), 0)
    return pl.pallas_call(
        body, name=name, grid=(4, nb),
        in_specs=[pl.BlockSpec((TQ, HP), lambda h, i: (i, h)),
                  pl.BlockSpec((TQ, HP), lambda h, i: (jnp.maximum(i - 1, 0), 4 + h)),
                  pl.BlockSpec((TQ, HP), lambda h, i: (i, 4 + h)),
                  pl.BlockSpec((TQ, HP), lambda h, i: (jnp.maximum(i - 1, 0), 8 + h)),
                  pl.BlockSpec((TQ, HP), lambda h, i: (i, 8 + h)),
                  pl.BlockSpec((None, 256, WIN), lambda h, i: (h, 0, 0))],
        out_specs=pl.BlockSpec((TQ, HP), lambda h, i: (i, h)),
        out_shape=jax.ShapeDtypeStruct((S, 512), BF16),
        scratch_shapes=[pltpu.VMEM((2 * TQ, HP), BF16), pltpu.VMEM((2 * TQ, HP), BF16)],
        compiler_params=_cp("parallel", "arbitrary"),
    )(qkv, qkv, qkv, qkv, qkv, bias)


def attn_bwd(qkv, do, bias, name):
    S = qkv.shape[0]
    nb = S // TQ

    def body(q_ref, kp_ref, kc_ref, vp_ref, vc_ref, do_ref, b_ref, dqkv_ref, db_ref, dkv_ref, kw, vw, ak, av):
        i = pl.program_id(1)

        @pl.when(i == 0)
        def _():
            db_ref[...] = jnp.zeros_like(db_ref)
            ak[...] = jnp.zeros_like(ak)
            av[...] = jnp.zeros_like(av)

        @pl.when(i > 0)
        def _():
            ak[0:TQ, :] = ak[TQ:2 * TQ, :]
            av[0:TQ, :] = av[TQ:2 * TQ, :]
            ak[TQ:2 * TQ, :] = jnp.zeros((TQ, HP), F32)
            av[TQ:2 * TQ, :] = jnp.zeros((TQ, HP), F32)

        @pl.when(i < nb)
        def _():
            kw[0:TQ, :] = kp_ref[...]
            kw[TQ:2 * TQ, :] = kc_ref[...]
            vw[0:TQ, :] = vp_ref[...]
            vw[TQ:2 * TQ, :] = vc_ref[...]
            lane = lax.broadcasted_iota(jnp.int32, (1, HP), 1)

            rows = [pl.ds(128 * a, 128) for a in range(NG)]
            keys = [pl.ds(128 * a, WIN) for a in range(NG)]
            q2 = [_two_heads(q_ref[r, :] * jnp.asarray(0.125, BF16), lane) for r in rows]
            do2 = [_two_heads(do_ref[r, :], lane) for r in rows]
            s = [_attn_scores(q2[a], kw[keys[a], :], b_ref[...], i, a) for a in range(NG)]
            dp = [_dot_nt(do2[a], vw[keys[a], :]) for a in range(NG)]
            p = [_softmax(sa) for sa in s]
            ds = [p[a] * (dp[a] - jnp.sum(p[a] * dp[a], axis=-1, keepdims=True)) for a in range(NG)]
            db_ref[...] += (ds[0] + ds[1]) + (ds[2] + ds[3])
            dsb = [d.astype(BF16) for d in ds]
            dq2 = [_dot(dsb[a], kw[keys[a], :]) for a in range(NG)]
            dk = [_dot_tn(dsb[a], q2[a]) for a in range(NG)]
            dv = [_dot_tn(p[a].astype(BF16), do2[a]) for a in range(NG)]
            for a in range(NG):
                ak[keys[a], :] += dk[a]
                av[keys[a], :] += dv[a]
                dq = jnp.where(lane < 64, dq2[a][0:128], dq2[a][128:256])
                dqkv_ref[0, rows[a], :] = (dq * 0.125).astype(BF16)

        @pl.when(i > 0)
        def _():
            dkv_ref[0] = ak[0:TQ, :].astype(BF16)
            dkv_ref[1] = av[0:TQ, :].astype(BF16)

    cur = lambda i: jnp.minimum(i, nb - 1)
    prv = lambda i: jnp.clip(i - 1, 0, nb - 1)
    dq, db, dkv = pl.pallas_call(
        body, name=name, grid=(4, nb + 1),
        in_specs=[pl.BlockSpec((TQ, HP), lambda h, i: (cur(i), h)),
                  pl.BlockSpec((TQ, HP), lambda h, i: (prv(i), 4 + h)),
                  pl.BlockSpec((TQ, HP), lambda h, i: (cur(i), 4 + h)),
                  pl.BlockSpec((TQ, HP), lambda h, i: (prv(i), 8 + h)),
                  pl.BlockSpec((TQ, HP), lambda h, i: (cur(i), 8 + h)),
                  pl.BlockSpec((TQ, HP), lambda h, i: (cur(i), h)),
                  pl.BlockSpec((None, 256, WIN), lambda h, i: (h, 0, 0))],
        out_specs=[pl.BlockSpec((1, TQ, HP), lambda h, i: (0, cur(i), h)),
                   pl.BlockSpec((None, 256, WIN), lambda h, i: (h, 0, 0)),
                   pl.BlockSpec((2, TQ, HP), lambda h, i: (0, prv(i), h))],
        out_shape=[jax.ShapeDtypeStruct((1, S, 512), BF16), jax.ShapeDtypeStruct((4, 256, WIN), F32),
                   jax.ShapeDtypeStruct((2, S, 512), BF16)],
        scratch_shapes=[pltpu.VMEM((2 * TQ, HP), BF16), pltpu.VMEM((2 * TQ, HP), BF16),
                        pltpu.VMEM((2 * TQ, HP), F32), pltpu.VMEM((2 * TQ, HP), F32)],
        compiler_params=_cp("parallel", "arbitrary"),
    )(qkv, qkv, qkv, qkv, qkv, do, bias)
    return dq, db, dkv


def bias_grad(db, name):
    def body(db_ref, o_ref):
        r = lax.broadcasted_iota(jnp.int32, (128, 128), 0)
        c = lax.broadcasted_iota(jnp.int32, (128, 128), 1)
        flip = (r + c == 127).astype(BF16)
        lane = lax.broadcasted_iota(jnp.int32, (16, 384), 1)
        src = lax.broadcasted_iota(jnp.int32, (128, 384), 0)
        dst = lax.broadcasted_iota(jnp.int32, (128, 384), 1)

        def split_dot(v, m):
            hi = v.astype(BF16)
            r1 = v - hi.astype(F32)
            mid = r1.astype(BF16)
            lo = (r1 - mid.astype(F32)).astype(BF16)
            return _dot(hi, m) + _dot(mid, m) + _dot(lo, m)

        def diag_sums(w):
            y = pltpu.roll(split_dot(w, flip), 0, 1, stride=1, stride_axis=0)
            return jnp.broadcast_to(_colsum(y), (16, 128))

        w4 = db_ref[0, :, 512:640]
        w3 = db_ref[0, :, 384:512]
        far = jnp.sum(db_ref[0, :, 0:384]) + jnp.sum(jnp.where(r >= c, w3, 0.0))
        lo4 = diag_sums(jnp.where(r >= c, w4, 0.0))
        up4 = diag_sums(jnp.where(r < c, w4, 0.0))
        up3 = diag_sums(jnp.where(r < c, w3, 0.0))
        p_lo4 = (dst == 128 + (src + 1) % 128).astype(BF16)
        p_up4 = ((dst == src + 1) & (src < 127)).astype(BF16)
        p_up3 = ((dst == src + 129) & (src < 127)).astype(BF16)
        out = split_dot(lo4, p_lo4) + split_dot(up4, p_up4) + split_dot(up3, p_up3)
        o_ref[0] = out + jnp.where(lane == 256, far, 0.0)

    return pl.pallas_call(
        body, name=name, grid=(8,),
        in_specs=[pl.BlockSpec((1, 128, WIN), lambda h: (h, 0, 0))],
        out_specs=pl.BlockSpec((1, 16, 384), lambda h: (h, 0, 0)),
        out_shape=jax.ShapeDtypeStruct((8, 16, 384), F32),
        compiler_params=_cp("parallel"),
    )(db)[:, 0, :]


LT = 1024
LC = 512


def _lru_gates(xs, pv_ref, wa_ref, wx_ref, tl):
    xc = (pv_ref[4:5, :] + pv_ref[3:4, :] * xs[pl.ds(8, tl), :] + pv_ref[2:3, :] * xs[pl.ds(7, tl), :]
          + pv_ref[1:2, :] * xs[pl.ds(6, tl), :] + pv_ref[0:1, :] * xs[pl.ds(5, tl), :])
    xcb = xc.astype(BF16)
    pa = jnp.concatenate([_dot(xcb[:, 0:256], wa_ref[0]), _dot(xcb[:, 256:512], wa_ref[1])], axis=1)
    px = jnp.concatenate([_dot(xcb[:, 0:256], wx_ref[0]), _dot(xcb[:, 256:512], wx_ref[1])], axis=1)
    r = _sigmoid(pa + pv_ref[5:6, :])
    ig = _sigmoid(px + pv_ref[6:7, :])
    z = -pv_ref[7:8, :]
    sp = jnp.maximum(z, 0.0) + jnp.log1p(jnp.exp(-jnp.abs(z)))
    log_a = (-LRU_C * r) * sp
    a = jnp.exp(log_a)
    s = jnp.tanh(-log_a) * (1.0 + a * a)
    inv_mult = lax.rsqrt(s)
    mult = jnp.where(s > 0.0, s * inv_mult, 0.0)
    return xc, xcb, r, ig, sp, a, mult, inv_mult


def lru_fwd(rest, pvec, wa, wx, name):
    S = rest.shape[0]
    tl = min(LT, S)
    nt = S // tl

    def body(xr_ref, halo_ref, yr_ref, pv_ref, wa_ref, wx_ref, h_ref, hg_ref, xs, a_s, u_s, h_s, carry):
        ti = pl.program_id(1)

        @pl.when(ti == 0)
        def _():
            carry[...] = jnp.zeros_like(carry)

        xs[0:8, :] = jnp.where(ti > 0, halo_ref[8:16, :].astype(F32), 0.0)
        xs[pl.ds(8, tl), :] = xr_ref[...].astype(F32)
        xc, _, _, ig, _, a, mult, _ = _lru_gates(xs, pv_ref, wa_ref, wx_ref, tl)
        a_s[...] = a
        u_s[...] = mult * (ig * xc)
        row = lax.broadcasted_iota(jnp.int32, (8, LC), 0)

        def blk(bi, c):
            o = pl.multiple_of(bi * 8, 8)
            av = a_s[pl.ds(o, 8), :]
            bv = u_s[pl.ds(o, 8), :]
            for d in (1, 2, 4):
                a_sh = pltpu.roll(av, d, 0)
                b_sh = pltpu.roll(bv, d, 0)
                m = row >= d
                bv = jnp.where(m, av * b_sh + bv, bv)
                av = jnp.where(m, av * a_sh, av)
            hv = bv + av * c
            h_s[pl.ds(o, 8), :] = hv
            return hv[7:8, :]

        carry[...] = lax.fori_loop(0, tl // 8, blk, carry[...])
        h = h_s[...]
        h_ref[...] = h
        hg_ref[...] = (h * _gelu(yr_ref[...].astype(F32))).astype(BF16)

    hb = tl // 16
    return pl.pallas_call(
        body, name=name, grid=(2, nt),
        in_specs=[pl.BlockSpec((tl, LC), lambda c, t: (t, c)),
                  pl.BlockSpec((16, LC), lambda c, t: (jnp.maximum(t * hb - 1, 0), c)),
                  pl.BlockSpec((tl, LC), lambda c, t: (t, 2 + c)),
                  pl.BlockSpec((8, LC), lambda c, t: (0, c)),
                  pl.BlockSpec((2, 256, 256), lambda c, t: (c, 0, 0)),
                  pl.BlockSpec((2, 256, 256), lambda c, t: (c, 0, 0))],
        out_specs=[pl.BlockSpec((tl, LC), lambda c, t: (t, c)), pl.BlockSpec((tl, LC), lambda c, t: (t, c))],
        out_shape=[jax.ShapeDtypeStruct((S, D), F32), jax.ShapeDtypeStruct((S, D), BF16)],
        scratch_shapes=[pltpu.VMEM((tl + 8, LC), F32), pltpu.VMEM((tl, LC), F32), pltpu.VMEM((tl, LC), F32),
                        pltpu.VMEM((tl, LC), F32), pltpu.VMEM((1, LC), F32)],
        compiler_params=_cp("parallel", "arbitrary"),
    )(rest, rest, rest, pvec, wa, wx)


def lru_bwd(dh, h, rest, pvec, wa, wx, name):
    S = rest.shape[0]
    tl = min(LT, S)
    nt = S // tl

    def body(dh_ref, h_ref, hhalo_ref, xr_ref, xhalo_ref, pv_ref, wa_ref, wx_ref,
             dxr_ref, vacc_ref, dwa_ref, dwx_ref,
             xs, hs, a_s, ash_s, b_s, lam_s, dxe, anext, lnext, dxnext):
        ti = pl.program_id(1)
        tr = nt - 1 - ti

        @pl.when(ti == 0)
        def _():
            anext[...] = jnp.zeros_like(anext)
            lnext[...] = jnp.zeros_like(lnext)
            dxnext[...] = jnp.zeros_like(dxnext)
            vacc_ref[...] = jnp.zeros_like(vacc_ref)
            dwa_ref[...] = jnp.zeros_like(dwa_ref)
            dwx_ref[...] = jnp.zeros_like(dwx_ref)

        xs[0:8, :] = jnp.where(tr > 0, xhalo_ref[8:16, :].astype(F32), 0.0)
        xs[pl.ds(8, tl), :] = xr_ref[...].astype(F32)
        xc, xcb, r, ig, sp, a, mult, inv_mult = _lru_gates(xs, pv_ref, wa_ref, wx_ref, tl)

        a_s[pl.ds(0, tl), :] = a
        a_s[pl.ds(tl, 8), :] = jnp.broadcast_to(anext[...], (8, LC))
        ash_s[...] = a_s[pl.ds(1, tl), :]
        b_s[...] = dh_ref[...]
        row = lax.broadcasted_iota(jnp.int32, (8, LC), 0)

        def blk(k, c):
            o = pl.multiple_of((tl // 8 - 1 - k) * 8, 8)
            av = ash_s[pl.ds(o, 8), :]
            bv = b_s[pl.ds(o, 8), :]
            for d in (1, 2, 4):
                a_sh = pltpu.roll(av, 8 - d, 0)
                b_sh = pltpu.roll(bv, 8 - d, 0)
                m = row < 8 - d
                bv = jnp.where(m, bv + av * b_sh, bv)
                av = jnp.where(m, av * a_sh, av)
            lv = bv + av * c
            lam_s[pl.ds(o, 8), :] = lv
            return lv[0:1, :]

        lnext[...] = lax.fori_loop(0, tl // 8, blk, lnext[...])
        anext[...] = a[0:1, :]
        lam = lam_s[...]

        hs[0:8, :] = jnp.where(tr > 0, hhalo_ref[...], 0.0)
        hs[pl.ds(8, tl), :] = h_ref[...]
        d_a = lam * hs[pl.ds(7, tl), :]
        d_mult = lam * (ig * xc)
        d_ig = lam * mult * xc
        dxc = lam * mult * ig
        d_log_a = d_a * a - d_mult * (a * a) * inv_mult
        d_r = d_log_a * (-LRU_C * sp)
        vacc_ref[7:8, :] += _colsum(d_log_a * (-LRU_C * r)) * (-_sigmoid(-pv_ref[7:8, :]))
        d_pa = d_r * r * (1.0 - r)
        d_px = d_ig * ig * (1.0 - ig)
        vacc_ref[5:6, :] += _colsum(d_pa)
        vacc_ref[6:7, :] += _colsum(d_px)
        dpa = d_pa.astype(BF16)
        dpx = d_px.astype(BF16)
        back = []
        for g in range(2):
            sl = slice(256 * g, 256 * g + 256)
            dwa_ref[g] += _dot_tn(xcb[:, sl], dpa[:, sl])
            dwx_ref[g] += _dot_tn(xcb[:, sl], dpx[:, sl])
            back.append(_dot_nt(dpa[:, sl], wa_ref[g]) + _dot_nt(dpx[:, sl], wx_ref[g]))
        dxc = dxc + jnp.concatenate(back, axis=1)
        vacc_ref[4:5, :] += _colsum(dxc)
        for k in range(4):
            vacc_ref[k:k + 1, :] += _colsum(dxc * xs[pl.ds(5 + k, tl), :])
        dxe[pl.ds(0, tl), :] = dxc
        dxe[pl.ds(tl, 8), :] = dxnext[...]
        dxr = (pv_ref[3:4, :] * dxc + pv_ref[2:3, :] * dxe[pl.ds(1, tl), :]
               + pv_ref[1:2, :] * dxe[pl.ds(2, tl), :] + pv_ref[0:1, :] * dxe[pl.ds(3, tl), :])
        dxr_ref[...] = dxr.astype(BF16)
        dxnext[...] = dxc[0:8, :]

    hb = tl // 8
    rev = lambda t: nt - 1 - t
    halo = lambda t: jnp.maximum(rev(t) * hb - 1, 0)
    big = lambda: pltpu.VMEM((tl + 8, LC), F32)
    til = lambda: pltpu.VMEM((tl, LC), F32)
    return pl.pallas_call(
        body, name=name, grid=(2, nt),
        in_specs=[pl.BlockSpec((tl, LC), lambda c, t: (rev(t), c)),
                  pl.BlockSpec((tl, LC), lambda c, t: (rev(t), c)),
                  pl.BlockSpec((8, LC), lambda c, t: (halo(t), c)),
                  pl.BlockSpec((tl, LC), lambda c, t: (rev(t), c)),
                  pl.BlockSpec((16, LC), lambda c, t: (jnp.maximum(rev(t) * (tl // 16) - 1, 0), c)),
                  pl.BlockSpec((8, LC), lambda c, t: (0, c)),
                  pl.BlockSpec((2, 256, 256), lambda c, t: (c, 0, 0)),
                  pl.BlockSpec((2, 256, 256), lambda c, t: (c, 0, 0))],
        out_specs=[pl.BlockSpec((tl, LC), lambda c, t: (rev(t), c)),
                   pl.BlockSpec((8, LC), lambda c, t: (0, c)),
                   pl.BlockSpec((2, 256, 256), lambda c, t: (c, 0, 0)),
                   pl.BlockSpec((2, 256, 256), lambda c, t: (c, 0, 0))],
        out_shape=[jax.ShapeDtypeStruct((S, D), BF16), jax.ShapeDtypeStruct((8, D), F32),
                   jax.ShapeDtypeStruct((4, 256, 256), F32), jax.ShapeDtypeStruct((4, 256, 256), F32)],
        scratch_shapes=[big(), big(), big(), til(), til(), til(), big(),
                        pltpu.VMEM((1, LC), F32), pltpu.VMEM((1, LC), F32), pltpu.VMEM((8, LC), F32)],
        compiler_params=_cp("parallel", "arbitrary"),
    )(dh, h, h, rest, rest, pvec, wa, wx)


def mix_out_fwd(x, ao, hg, rest, vec, w_att_o, w_rec_o, w_out, name, tm=512):
    S = x.shape[0]
    tm = min(tm, S)

    def body(x_ref, ao_ref, hg_ref, ga_ref, gr_ref, vec_ref, wa_ref, wr_ref, wo_ref,
             xo_ref, att_ref, rec_ref, mg_ref, f_ref):
        att = _dot(ao_ref[...], wa_ref[...])
        rec = _dot(hg_ref[...], wr_ref[...])
        att_ref[...] = att.astype(BF16)
        rec_ref[...] = rec.astype(BF16)
        mg = (_sigmoid(ga_ref[...].astype(F32)) * att + _sigmoid(gr_ref[...].astype(F32)) * rec).astype(BF16)
        mg_ref[...] = mg
        f = _dot(mg, wo_ref[...])
        f_ref[...] = f.astype(BF16)
        y = f * lax.rsqrt(_mean(f * f) + EPS) * vec_ref[1:2, :]
        xo_ref[...] = x_ref[...] + (1.0 * vec_ref[4:5, :]) * y

    row = lambda i: (i, 0)
    full = lambda r: pl.BlockSpec((r, D), lambda i: (0, 0))
    return pl.pallas_call(
        body, name=name, grid=(S // tm,),
        in_specs=[pl.BlockSpec((tm, D), row), pl.BlockSpec((tm, 512), row), pl.BlockSpec((tm, D), row),
                  pl.BlockSpec((tm, D), lambda i: (i, 2)), pl.BlockSpec((tm, D), lambda i: (i, 3)),
                  full(8), full(512), full(D), full(D)],
        out_specs=[pl.BlockSpec((tm, D), row)] * 5,
        out_shape=[jax.ShapeDtypeStruct((S, D), F32)] + [jax.ShapeDtypeStruct((S, D), BF16)] * 4,
        compiler_params=_cp("parallel"),
    )(x, ao, hg, rest, rest, vec, w_att_o, w_rec_o, w_out)


def mix_out_bwd(dxo, f, att, rec, rest, h, vec, w_att_o, w_rec_o, w_out, name, tm=512):
    S = dxo.shape[0]
    tm = min(tm, S)

    def body(dxo_ref, f_ref, att_ref, rec_ref, yr_ref, ga_ref, gr_ref, h_ref, vec_ref, wa_ref, wr_ref, wo_ref,
             df_ref, da_ref, dr_ref, dao_ref, dh_ref, d3_ref, vacc_ref):
        @pl.when(pl.program_id(0) == 0)
        def _():
            vacc_ref[...] = jnp.zeros_like(vacc_ref)

        df = _post_norm_bwd(dxo_ref[...], f_ref[...].astype(F32), 1.0, vec_ref, vacc_ref).astype(BF16)
        df_ref[...] = df
        dm = _dot_nt(df, wo_ref[...])
        sa = _sigmoid(ga_ref[...].astype(F32))
        sr = _sigmoid(gr_ref[...].astype(F32))
        d_att = (dm * sa).astype(BF16)
        d_rec = (dm * sr).astype(BF16)
        da_ref[...] = d_att
        dr_ref[...] = d_rec
        d3_ref[1] = (dm * att_ref[...].astype(F32) * (sa * (1.0 - sa))).astype(BF16)
        d3_ref[2] = (dm * rec_ref[...].astype(F32) * (sr * (1.0 - sr))).astype(BF16)
        dao_ref[...] = _dot_nt(d_att, wa_ref[...]).astype(BF16)
        d_hg = _dot_nt(d_rec, wr_ref[...])
        yr = yr_ref[...].astype(F32)
        t = jnp.tanh(_GK * (yr + 0.044715 * yr * yr * yr))
        dh_ref[...] = d_hg * (0.5 * yr * (1.0 + t))
        gelu_grad = 0.5 * (1.0 + t) + 0.5 * yr * (1.0 - t * t) * _GK * (1.0 + 3.0 * 0.044715 * yr * yr)
        d3_ref[0] = (d_hg * h_ref[...] * gelu_grad).astype(BF16)

    row = lambda i: (i, 0)
    full = lambda r: pl.BlockSpec((r, D), lambda i: (0, 0))
    return pl.pallas_call(
        body, name=name, grid=(S // tm,),
        in_specs=[pl.BlockSpec((tm, D), row)] * 4
        + [pl.BlockSpec((tm, D), lambda i: (i, 1)), pl.BlockSpec((tm, D), lambda i: (i, 2)),
           pl.BlockSpec((tm, D), lambda i: (i, 3)), pl.BlockSpec((tm, D), row),
           full(8), full(512), full(D), full(D)],
        out_specs=[pl.BlockSpec((tm, D), row)] * 3
        + [pl.BlockSpec((tm, 512), row), pl.BlockSpec((tm, D), row),
           pl.BlockSpec((3, tm, D), lambda i: (0, i, 0)), pl.BlockSpec((8, D), lambda i: (0, 0))],
        out_shape=[jax.ShapeDtypeStruct((S, D), BF16)] * 3
        + [jax.ShapeDtypeStruct((S, 512), BF16), jax.ShapeDtypeStruct((S, D), F32),
           jax.ShapeDtypeStruct((3, S, D), BF16), jax.ShapeDtypeStruct((8, D), F32)],
        compiler_params=_cp("arbitrary"),
    )(dxo, f, att, rec, rest, rest, rest, h, vec, w_att_o, w_rec_o, w_out)


def dw_in(h, dq, dkv, dxr, d3, name, tk=1024, tn=512):
    S = h.shape[0]
    tk = min(tk, S)
    nk = S // tk

    def body(h_ref, dq_ref, dkv_ref, dxr_ref, d3_ref, o_ref, acc):
        j, k = pl.program_id(0), pl.program_id(1)

        @pl.when(k == 0)
        def _():
            acc[...] = jnp.zeros_like(acc)

        @pl.when(j == 0)
        def _():
            acc[...] += _dot_tn(h_ref[pl.ds(pl.multiple_of(k * tk, 16), tk), :], dq_ref[...])

        @pl.when((j >= 1) & (j < 3))
        def _():
            acc[...] += _dot_tn(h_ref[pl.ds(pl.multiple_of(k * tk, 16), tk), :], dkv_ref[...])

        @pl.when((j >= 3) & (j < 5))
        def _():
            acc[...] += _dot_tn(h_ref[pl.ds(pl.multiple_of(k * tk, 16), tk), :], dxr_ref[...])

        @pl.when(j >= 5)
        def _():
            acc[...] += _dot_tn(h_ref[pl.ds(pl.multiple_of(k * tk, 16), tk), :], d3_ref[...])

        @pl.when(k == nk - 1)
        def _():
            o_ref[...] = acc[...].astype(BF16)

    use = lambda j, k, lo, hi: jnp.where((j >= lo) & (j < hi), k, 0)
    g3 = lambda j: jnp.clip(j - 5, 0, 5)
    return pl.pallas_call(
        body, name=name, grid=(PW // tn, nk),
        in_specs=[pl.BlockSpec((S, D), lambda j, k: (0, 0), pipeline_mode=pl.Buffered(1)),
                  pl.BlockSpec((None, tk, tn), lambda j, k: (0, use(j, k, 0, 1), 0)),
                  pl.BlockSpec((None, tk, tn), lambda j, k: (jnp.clip(j - 1, 0, 1), use(j, k, 1, 3), 0)),
                  pl.BlockSpec((tk, tn), lambda j, k: (use(j, k, 3, 5), jnp.clip(j - 3, 0, 1))),
                  pl.BlockSpec((None, tk, tn), lambda j, k: (g3(j) // 2, use(j, k, 5, 11), g3(j) % 2))],
        out_specs=pl.BlockSpec((D, tn), lambda j, k: (0, j)),
        out_shape=jax.ShapeDtypeStruct((D, PW), BF16),
        scratch_shapes=[pltpu.VMEM((D, tn), F32)],
        compiler_params=_cp("parallel", "arbitrary"),
    )(h, dq, dkv, dxr, d3)


def ada_fwd(c_all, w_ada, b_ada, name, tn=768):
    n = w_ada.shape[1]

    def body(c_ref, w_ref, b_ref, o_ref):
        cv = c_ref[...]
        ca = (cv * _sigmoid(cv)).astype(BF16)
        o_ref[...] = _dot(ca, w_ref[...].astype(BF16)) + b_ref[...]

    return pl.pallas_call(
        body, name=name, grid=(n // tn,),
        in_specs=[pl.BlockSpec((8, D), lambda j: (0, 0)), pl.BlockSpec((D, tn), lambda j: (0, j)),
                  pl.BlockSpec((1, tn), lambda j: (0, j))],
        out_specs=pl.BlockSpec((8, tn), lambda j: (0, j)),
        out_shape=jax.ShapeDtypeStruct((8, n), F32),
        compiler_params=_cp("parallel"),
    )(c_all, w_ada, b_ada)


def ada_bwd(c_all_t, dmod, name, tn=768):
    n = dmod.shape[1]

    def body(c_ref, d_ref, o_ref):
        cv = c_ref[...]
        ca = (cv * _sigmoid(cv)).astype(BF16)
        o_ref[...] = _dot(ca, d_ref[...].astype(BF16))

    return pl.pallas_call(
        body, name=name, grid=(n // tn,),
        in_specs=[pl.BlockSpec((D, 128), lambda j: (0, 0)), pl.BlockSpec((128, tn), lambda j: (0, j))],
        out_specs=pl.BlockSpec((D, tn), lambda j: (0, j)),
        out_shape=jax.ShapeDtypeStruct((D, n), F32),
        compiler_params=_cp("parallel"),
    )(c_all_t, dmod)


def _row_tile(rows, cols, itemsize=4, budget=1536 * 1024):
    best = None
    for t in range(8, rows + 1, 8):
        if rows % t == 0 and t * cols * itemsize <= budget:
            best = t
    return rows if best is None else best


def sum_lead(parts, name, out_dtype=F32):
    n, R, C = parts.shape
    tr = _row_tile(R, C * n)

    def body(p_ref, o_ref):
        acc = p_ref[0].astype(F32)
        for k in range(1, n):
            acc = acc + p_ref[k].astype(F32)
        o_ref[...] = acc.astype(out_dtype)

    return pl.pallas_call(
        body, name=name, grid=(R // tr,),
        in_specs=[pl.BlockSpec((n, tr, C), lambda i: (0, i, 0))],
        out_specs=pl.BlockSpec((tr, C), lambda i: (i, 0)),
        out_shape=jax.ShapeDtypeStruct((R, C), out_dtype),
        compiler_params=_cp("parallel"),
    )(parts)


def adamw(w, g, m, v, name, emit_g=False):
    R, C = w.shape
    tr = _row_tile(R, C * 8, budget=8 * 1024 * 1024)

    def body(w_ref, g_ref, m_ref, v_ref, d_ref, mo_ref, vo_ref, *go_ref):
        gv = g_ref[...]
        if emit_g:
            go_ref[0][...] = gv
        mn = ADAM_B1 * m_ref[...] + (1.0 - ADAM_B1) * gv
        vn = ADAM_B2 * v_ref[...] + (1.0 - ADAM_B2) * (gv * gv)
        m_hat = mn / (1.0 - ADAM_B1 ** ADAM_STEP)
        v_hat = vn / (1.0 - ADAM_B2 ** ADAM_STEP)
        d_ref[...] = -ADAM_LR * (m_hat / (jnp.sqrt(v_hat) + ADAM_EPS) + ADAM_WD * w_ref[...])
        mo_ref[...] = mn
        vo_ref[...] = vn

    spec = pl.BlockSpec((tr, C), lambda i: (i, 0))
    return pl.pallas_call(
        body, name=name, grid=(R // tr,),
        in_specs=[spec] * 4, out_specs=[spec] * (4 if emit_g else 3),
        out_shape=[jax.ShapeDtypeStruct((R, C), F32)] * (4 if emit_g else 3),
        compiler_params=_cp("parallel"),
    )(w, g, m, v)


def _mesh_pos():
    return lax.axis_index("x"), lax.axis_index("y"), lax.axis_index("c")


def _other_chips(mx, my):
    return [(1 - mx, my), (mx, 1 - my), (1 - mx, 1 - my)]


def ag_small(x, name):
    R = x.shape[0]

    def body(x_ref, out_ref, send_sems, recv_sems, local_sem):
        mx, my, mc = _mesh_pos()
        me, sibling = (mx, my, mc), (mx, my, 1 - mc)
        chips = _other_chips(mx, my)

        def slot(px, py, pc):
            return out_ref.at[4 * px + 2 * py + pc]

        def copy(k, block, to, src=None):
            return pltpu.make_async_remote_copy(
                src_ref=slot(*block) if src is None else src, dst_ref=slot(*block),
                send_sem=send_sems.at[k], recv_sem=recv_sems.at[k], device_id=to, device_id_type=MESH)

        mine = pltpu.make_async_copy(x_ref, slot(*me), local_sem)
        mine.start()
        first = [copy(0, me, sibling, src=x_ref)]
        first += [copy(1 + j, me, (*chip, mc), src=x_ref) for j, chip in enumerate(chips)]
        for cp in first:
            cp.start()
        passed = [copy(4 + j, (*chip, mc), sibling) for j, chip in enumerate(chips)]
        for j, chip in enumerate(chips):
            copy(1 + j, (*chip, mc), me).wait_recv()
            passed[j].start()
        copy(0, sibling, me).wait_recv()
        for j, chip in enumerate(chips):
            copy(4 + j, (*chip, 1 - mc), me).wait_recv()
        for cp in first + passed:
            cp.wait_send()
        mine.wait()

    return pl.pallas_call(
        body, name=name,
        out_shape=jax.ShapeDtypeStruct((N_DEV, R, 128), F32),
        in_specs=[pl.BlockSpec(memory_space=pltpu.VMEM)],
        out_specs=pl.BlockSpec(memory_space=pltpu.VMEM),
        scratch_shapes=[pltpu.SemaphoreType.DMA((7,)), pltpu.SemaphoreType.DMA((7,)), pltpu.SemaphoreType.DMA],
        compiler_params=pltpu.CompilerParams(vmem_limit_bytes=VMEM_LIMIT),
    )(x)


BIG = (("ffn1_w_gu", "col", D, PW), ("ffn1_w_down", "row", FF, D), ("w_in", "col", D, PW),
       ("w_att_o", "col", 512, D), ("w_rec_o", "row", D, D), ("w_out", "row", D, D),
       ("ffn2_w_gu", "col", D, PW), ("ffn2_w_down", "row", FF, D))
NBIG = len(BIG)


def _shard_shape(kind, R, C):
    return (R, C // 4) if kind == "col" else (R // 4, C)


def _region(ref, kind, R, C, q, half, t, tr):
    sr, sc = _shard_shape(kind, R, C)
    if kind == "col":
        return ref.at[pl.ds(pl.multiple_of(half * (R // 2) + t * tr, 16), tr), pl.ds(q * sc, sc)]
    return ref.at[pl.ds(pl.multiple_of(q * sr + t * tr, 16), tr), pl.ds(half * (C // 2), C // 2)]


def ag_local(w, kind, R, C, p_arr, name, after=()):
    sr, sc = _shard_shape(kind, R, C)
    tr = _row_tile(sr, sc, budget=2 * 1024 * 1024)
    nt = sr // tr
    after = list(after)

    def body(p_ref, w_ref, *rest):
        rest[-1][...] = w_ref[...].astype(BF16)

    if kind == "col":
        o_spec = pl.BlockSpec((tr, sc), lambda i, p: (i, p[0]))
    else:
        o_spec = pl.BlockSpec((tr, sc), lambda i, p: (p[0] * nt + i, 0))
    return pl.pallas_call(
        body, name=name,
        grid_spec=pltpu.PrefetchScalarGridSpec(
            num_scalar_prefetch=1, grid=(nt,),
            in_specs=[pl.BlockSpec((tr, sc), lambda i, p: (i, 0))] + [ANY] * len(after), out_specs=o_spec),
        out_shape=jax.ShapeDtypeStruct((R, C), BF16),
        compiler_params=_cp("parallel"),
    )(p_arr, w, *after)


HBM_SPEC = pl.BlockSpec(memory_space=pltpu.HBM)
SEM_SPEC = pl.BlockSpec(memory_space=pltpu.SEMAPHORE)


def _ag_sems(geoms):
    return sum(6 if both else 3 for (_, _, _, both) in geoms)


def _ag_copies(fulls, geoms, ssem, rsem, mx, my, mc, q, h):
    chips = _other_chips(mx, my)
    out, base = [], 0
    for w, (kind, R, C, both) in enumerate(geoms):
        sr, sc = _shard_shape(kind, R, C)
        hr = sr // 2 if kind == "col" else sr
        reg = _region(fulls[w], kind, R, C, q, h, 0, hr)
        out.append([pltpu.make_async_remote_copy(
            src_ref=reg, dst_ref=reg, send_sem=ssem.at[base + 3 * t + k], recv_sem=rsem.at[base + 3 * t + k],
            device_id=(*chips[k], mc if t == 0 else 1 - mc), device_id_type=MESH)
            for t in range(2 if both else 1) for k in range(3)])
        base += 6 if both else 3
    return out


def ag_start(fulls, geoms, after, name):
    n = len(fulls)
    after = list(after)
    m = len(after)

    def body(*refs):
        ssem, rsem = refs[n + m:n + m + 2]
        outs, token = refs[n + m + 2:2 * n + m + 2], refs[2 * n + m + 2]
        mx, my, mc = _mesh_pos()
        p = 2 * mx + my
        col = [w for w, g in enumerate(geoms) if g[0] == "col"]
        row = [w for w, g in enumerate(geoms) if g[0] == "row"]
        for q in range(4):
            @pl.when(p == q)
            def _(q=q):
                cps = _ag_copies(outs, geoms, ssem, rsem, mx, my, mc, q, mc)
                for w in col:
                    for cp in cps[w]:
                        cp.start()
        for h in range(2):
            @pl.when(mc == h)
            def _(h=h):
                cps = _ag_copies(outs, geoms, ssem, rsem, mx, my, mc, p, h)
                for w in row:
                    for cp in cps[w]:
                        cp.start()
        token[...] = jnp.zeros_like(token)

    res = pl.pallas_call(
        body, name=name,
        out_shape=[pltpu.SemaphoreType.DMA((_ag_sems(geoms),)), pltpu.SemaphoreType.DMA((_ag_sems(geoms),))]
        + [pltpu.HBM(a.shape, a.dtype) for a in fulls] + [jax.ShapeDtypeStruct((8, 128), F32)],
        in_specs=[HBM_SPEC] * n + [ANY] * m,
        out_specs=[SEM_SPEC, SEM_SPEC] + [HBM_SPEC] * n + [pl.BlockSpec(memory_space=pltpu.VMEM)],
        input_output_aliases={w: 2 + w for w in range(n)},
        compiler_params=pltpu.CompilerParams(has_side_effects=pltpu.SideEffectType.DATAFLOW_SIDE_EFFECTING),
    )(*[pltpu.with_memory_space_constraint(a, pltpu.HBM) for a in fulls], *after)
    return res[0], res[1], list(res[2:2 + n]), res[2 + n]


def ag_wait(fulls, geoms, ssem, rsem, after, name):
    n = len(fulls)
    after = list(after) if isinstance(after, (list, tuple)) else [after]

    def body(*refs):
        ins, ssem_ref, rsem_ref = refs[:n], refs[n], refs[n + 1]
        mx, my, mc = _mesh_pos()
        for cps in _ag_copies(ins, geoms, ssem_ref, rsem_ref, mx, my, mc, 0, 0):
            for cp in cps:
                cp.wait_send()
                cp.wait_recv()

    return list(pl.pallas_call(
        body, name=name,
        out_shape=[pltpu.HBM(a.shape, a.dtype) for a in fulls],
        in_specs=[HBM_SPEC] * n + [SEM_SPEC, SEM_SPEC] + [ANY] * len(after),
        out_specs=[HBM_SPEC] * n,
        input_output_aliases={w: w for w in range(n)},
        compiler_params=pltpu.CompilerParams(has_side_effects=pltpu.SideEffectType.DATAFLOW_SIDE_EFFECTING),
    )(*fulls, ssem, rsem, *after))


def ag_forward(full, kind, R, C, name):
    sr, sc = _shard_shape(kind, R, C)
    hr, hc = (sr // 2, sc) if kind == "col" else (sr, sc // 2)
    tr = _row_tile(hr, hc, itemsize=2, budget=2 * 1024 * 1024)
    nt = hr // tr
    total = 3 * nt

    def body(src_ref, full_ref, stage, lsem, ssem, rsem):
        step = pl.program_id(0) * nt + pl.program_id(1)
        par = step % 2
        mx, my, mc = _mesh_pos()

        def load(s, q, h, t):
            return pltpu.make_async_copy(_region(src_ref, kind, R, C, q, h, t, tr), stage.at[s], lsem.at[s])

        def push(s, q, h, t):
            return pltpu.make_async_remote_copy(src_ref=stage.at[s], dst_ref=_region(full_ref, kind, R, C, q, h, t, tr),
                                                send_sem=ssem.at[s], recv_sem=rsem, device_id=(mx, my, 1 - mc),
                                                device_id_type=MESH)

        def for_tile(stp, fn):
            q_k = _partner_chip(stp // nt, 2 * mx + my)
            if kind == "col":
                for q in range(4):
                    @pl.when(q_k == q)
                    def _(q=q):
                        fn(q, mc, stp % nt)
            else:
                for h in range(2):
                    @pl.when(mc == h)
                    def _(h=h):
                        fn(q_k, h, stp % nt)

        @pl.when(step == 0)
        def _():
            for_tile(step, lambda q, h, t: load(0, q, h, t).start())

        load(par, 0, 0, 0).wait()
        for_tile(step, lambda q, h, t: push(par, q, h, t).start())

        @pl.when(step + 1 < total)
        def _():
            @pl.when(step >= 1)
            def _():
                push(1 - par, 0, 0, 0).wait_send()
            for_tile(step + 1, lambda q, h, t: load(1 - par, q, h, t).start())

        @pl.when(step == total - 1)
        def _():
            push(par, 0, 0, 0).wait_send()
            push(1 - par, 0, 0, 0).wait_send()
            three = full_ref.at[pl.ds(0, hr), pl.ds(0, 3 * hc)] if kind == "col" else full_ref.at[pl.ds(0, 3 * hr), pl.ds(0, hc)]
            pltpu.make_async_remote_copy(src_ref=three, dst_ref=three, send_sem=ssem.at[0], recv_sem=rsem,
                                         device_id=(mx, my, 1 - mc), device_id_type=MESH).wait_recv()

    return pl.pallas_call(
        body, name=name, grid=(3, nt),
        in_specs=[ANY], out_specs=ANY,
        out_shape=jax.ShapeDtypeStruct((R, C), BF16),
        scratch_shapes=[pltpu.VMEM((2, tr, hc), BF16), pltpu.SemaphoreType.DMA((2,)), pltpu.SemaphoreType.DMA((2,)),
                        pltpu.SemaphoreType.DMA],
        input_output_aliases={0: 0},
        compiler_params=_cp("arbitrary", "arbitrary"),
    )(full)


def _half_shape(kind, R, C):
    return (R // 2, C) if kind == "col" else (R, C // 2)


def _piece_shape(kind, R, C):
    return (R // 2, C // 4) if kind == "col" else (R // 4, C // 2)


def pair_push(g, kind, c_arr, name):
    R, C = g.shape
    hr, hc = _half_shape(kind, R, C)
    tr = _row_tile(hr, hc, itemsize=2, budget=2 * 1024 * 1024)
    nt = hr // tr

    def body(c_ref, g_ref, out_ref, stage, ssem, rsem):
        i = pl.program_id(0)
        slot = i % 2
        mx, my, mc = _mesh_pos()

        def push(s, t):
            return pltpu.make_async_remote_copy(
                src_ref=stage.at[s], dst_ref=out_ref.at[pl.ds(pl.multiple_of(t * tr, 16), tr)],
                send_sem=ssem.at[s], recv_sem=rsem, device_id=(mx, my, 1 - mc), device_id_type=MESH)

        @pl.when(i >= 2)
        def _():
            push(slot, 0).wait_send()

        stage[slot] = g_ref[...]
        push(slot, i).start()

        @pl.when(i == nt - 1)
        def _():
            push(slot, 0).wait_send()
            if nt >= 2:
                push(1 - slot, 0).wait_send()
            pltpu.make_async_remote_copy(src_ref=out_ref, dst_ref=out_ref, send_sem=ssem.at[0], recv_sem=rsem,
                                         device_id=(mx, my, 1 - mc), device_id_type=MESH).wait_recv()

    if kind == "col":
        g_spec = pl.BlockSpec((tr, hc), lambda i, c: ((1 - c[0]) * nt + i, 0))
    else:
        g_spec = pl.BlockSpec((tr, hc), lambda i, c: (i, 1 - c[0]))
    return pl.pallas_call(
        body, name=name,
        grid_spec=pltpu.PrefetchScalarGridSpec(
            num_scalar_prefetch=1, grid=(nt,), in_specs=[g_spec], out_specs=ANY,
            scratch_shapes=[pltpu.VMEM((2, tr, hc), BF16), pltpu.SemaphoreType.DMA((2,)), pltpu.SemaphoreType.DMA]),
        out_shape=jax.ShapeDtypeStruct((hr, hc), BF16),
        compiler_params=_cp("arbitrary"),
    )(c_arr, g)


def _partner_chip(k, p):
    return p ^ jnp.where(k == 0, 2, jnp.where(k == 1, 1, jnp.where(k == 2, 3, 0)))


def pair_add(g, got, kind, cp_arr, name):
    R, C = g.shape
    pr, pc = _piece_shape(kind, R, C)
    tr = _row_tile(pr, pc, itemsize=2, budget=2 * 1024 * 1024)
    nt = pr // tr

    def body(cp_ref, g_ref, got_ref, ps_ref, rb_ref):
        tile = (g_ref[...].astype(F32) + got_ref[...].astype(F32)).astype(BF16)
        ps_ref[...] = tile

        @pl.when(pl.program_id(1) == cp_ref[1])
        def _():
            rb_ref[...] = tile

    if kind == "col":
        g_spec = pl.BlockSpec((tr, pc), lambda i, q, cp: (cp[0] * nt + i, q))
        got_spec = pl.BlockSpec((tr, pc), lambda i, q, cp: (i, q))
    else:
        g_spec = pl.BlockSpec((tr, pc), lambda i, q, cp: (q * nt + i, cp[0]))
        got_spec = pl.BlockSpec((tr, pc), lambda i, q, cp: (q * nt + i, 0))
    return pl.pallas_call(
        body, name=name,
        grid_spec=pltpu.PrefetchScalarGridSpec(
            num_scalar_prefetch=1, grid=(nt, 4), in_specs=[g_spec, got_spec],
            out_specs=[pl.BlockSpec((None, tr, pc), lambda i, q, cp: (q, i, 0)),
                       pl.BlockSpec((None, tr, pc), lambda i, q, cp: (cp[1], i, 0))]),
        out_shape=[jax.ShapeDtypeStruct((4, pr, pc), BF16)] * 2,
        compiler_params=_cp("arbitrary", "arbitrary"),
    )(cp_arr, g, got)


def _rs_copies(ps, rb, ssem, rsem, mx, my, mc):
    p = 2 * mx + my
    out = []
    for w in range(len(ps)):
        for k, chip in enumerate(_other_chips(mx, my)):
            out.append(pltpu.make_async_remote_copy(
                src_ref=ps[w].at[2 * chip[0] + chip[1]], dst_ref=rb[w].at[p], send_sem=ssem.at[3 * w + k],
                recv_sem=rsem.at[3 * w + k], device_id=(*chip, mc), device_id_type=MESH))
    return out


def rs_start(ps, rb, after, name):
    n = len(ps)
    after = list(after)
    m = len(after)

    def body(*refs):
        ssem, rsem = refs[2 * n + m:2 * n + m + 2]
        ps_o = refs[2 * n + m + 2:3 * n + m + 2]
        rb_o = refs[3 * n + m + 2:4 * n + m + 2]
        token = refs[4 * n + m + 2]
        for cp in _rs_copies(ps_o, rb_o, ssem, rsem, *_mesh_pos()):
            cp.start()
        token[...] = jnp.zeros_like(token)

    both = list(ps) + list(rb)
    res = pl.pallas_call(
        body, name=name,
        out_shape=[pltpu.SemaphoreType.DMA((3 * n,)), pltpu.SemaphoreType.DMA((3 * n,))]
        + [pltpu.HBM(a.shape, a.dtype) for a in both] + [jax.ShapeDtypeStruct((8, 128), F32)],
        in_specs=[HBM_SPEC] * (2 * n) + [ANY] * m,
        out_specs=[SEM_SPEC, SEM_SPEC] + [HBM_SPEC] * (2 * n) + [pl.BlockSpec(memory_space=pltpu.VMEM)],
        input_output_aliases={w: 2 + w for w in range(2 * n)},
        compiler_params=pltpu.CompilerParams(has_side_effects=pltpu.SideEffectType.DATAFLOW_SIDE_EFFECTING),
    )(*[pltpu.with_memory_space_constraint(a, pltpu.HBM) for a in both], *after)
    return res[0], res[1], list(res[2:2 + n]), list(res[2 + n:2 + 2 * n]), res[2 + 2 * n]


def rs_wait(ps, rb, ssem, rsem, after, name):
    n = len(ps)
    after = list(after)
    m = len(after)

    def body(*refs):
        ps_i, rb_i = refs[:n], refs[n:2 * n]
        ssem_ref, rsem_ref = refs[2 * n], refs[2 * n + 1]
        for cp in _rs_copies(ps_i, rb_i, ssem_ref, rsem_ref, *_mesh_pos()):
            cp.wait_send()
            cp.wait_recv()

    both = list(ps) + list(rb)
    res = pl.pallas_call(
        body, name=name,
        out_shape=[pltpu.HBM(a.shape, a.dtype) for a in both],
        in_specs=[HBM_SPEC] * (2 * n) + [SEM_SPEC, SEM_SPEC] + [ANY] * m,
        out_specs=[HBM_SPEC] * (2 * n),
        input_output_aliases={w: w for w in range(2 * n)},
        compiler_params=pltpu.CompilerParams(has_side_effects=pltpu.SideEffectType.DATAFLOW_SIDE_EFFECTING),
    )(*both, ssem, rsem, *after)
    return list(res[n:])


def _slot_copies(blk, ssem, rsem):
    mx, my, mc = _mesh_pos()
    mine = blk.at[4 * mx + 2 * my + mc]
    peers = [(mx, my, 1 - mc)] + [(*chip, mc) for chip in _other_chips(mx, my)] \
        + [(*chip, 1 - mc) for chip in _other_chips(mx, my)]
    return [pltpu.make_async_remote_copy(src_ref=mine, dst_ref=mine, send_sem=ssem.at[k], recv_sem=rsem.at[k],
                                         device_id=peer, device_id_type=MESH) for k, peer in enumerate(peers)]


def slot_start(blk, name):
    def body(_, ssem, rsem, out, token):
        for cp in _slot_copies(out, ssem, rsem):
            cp.start()
        token[...] = jnp.zeros_like(token)

    return pl.pallas_call(
        body, name=name,
        out_shape=[pltpu.SemaphoreType.DMA((7,)), pltpu.SemaphoreType.DMA((7,)), pltpu.HBM(blk.shape, blk.dtype),
                   jax.ShapeDtypeStruct((8, 128), F32)],
        in_specs=[HBM_SPEC],
        out_specs=[SEM_SPEC, SEM_SPEC, HBM_SPEC, pl.BlockSpec(memory_space=pltpu.VMEM)],
        input_output_aliases={0: 2},
        compiler_params=pltpu.CompilerParams(has_side_effects=pltpu.SideEffectType.DATAFLOW_SIDE_EFFECTING),
    )(pltpu.with_memory_space_constraint(blk, pltpu.HBM))


def slot_wait(blk, ssem, rsem, after, name):
    after = list(after)

    def body(blk_ref, ssem_ref, rsem_ref, *_):
        for cp in _slot_copies(blk_ref, ssem_ref, rsem_ref):
            cp.wait_send()
            cp.wait_recv()

    return pl.pallas_call(
        body, name=name,
        out_shape=pltpu.HBM(blk.shape, blk.dtype),
        in_specs=[HBM_SPEC, SEM_SPEC, SEM_SPEC] + [ANY] * len(after),
        out_specs=HBM_SPEC,
        input_output_aliases={0: 0},
        compiler_params=pltpu.CompilerParams(has_side_effects=pltpu.SideEffectType.DATAFLOW_SIDE_EFFECTING),
    )(blk, ssem, rsem, *after)


def sum_share(parts, kind, R, C, name):
    _, pr, pc = parts.shape
    sr, sc = _shard_shape(kind, R, C)
    tr = _row_tile(pr, pc * 4, budget=8 * 1024 * 1024)
    nt = pr // tr

    def body(p_ref, fin_ref, stage, lsem, ssem, rsem):
        i = pl.program_id(0)
        slot = i % 2
        mx, my, mc = _mesh_pos()

        def region(h, t):
            r0 = pl.multiple_of(t * tr, 8)
            if kind == "col":
                return fin_ref.at[pl.ds(pl.multiple_of(h * pr + r0, 8), tr)]
            return fin_ref.at[pl.ds(r0, tr), pl.ds(h * pc, pc)]

        def copies(s, h, t):
            return (pltpu.make_async_copy(stage.at[s], region(h, t), lsem.at[s]),
                    pltpu.make_async_remote_copy(src_ref=stage.at[s], dst_ref=region(h, t), send_sem=ssem.at[s],
                                                 recv_sem=rsem, device_id=(mx, my, 1 - mc), device_id_type=MESH))

        def wait_sent(s):
            loc, rem = copies(s, 0, 0)
            loc.wait()
            rem.wait_send()

        @pl.when(i >= 2)
        def _():
            wait_sent(slot)

        acc = p_ref[0].astype(F32)
        for k in range(1, 4):
            acc = acc + p_ref[k].astype(F32)
        stage[slot] = acc
        if kind == "col":
            for cp in copies(slot, mc, i):
                cp.start()
        else:
            for h in range(2):
                @pl.when(mc == h)
                def _(h=h):
                    for cp in copies(slot, h, i):
                        cp.start()

        @pl.when(i == nt - 1)
        def _():
            wait_sent(slot)
            if nt >= 2:
                wait_sent(1 - slot)
            half = fin_ref.at[pl.ds(0, pr), pl.ds(0, pc)]
            pltpu.make_async_remote_copy(src_ref=half, dst_ref=half, send_sem=ssem.at[0], recv_sem=rsem,
                                         device_id=(mx, my, 1 - mc), device_id_type=MESH).wait_recv()

    return pl.pallas_call(
        body, name=name, grid=(nt,),
        in_specs=[pl.BlockSpec((4, tr, pc), lambda i: (0, i, 0))],
        out_specs=ANY,
        out_shape=jax.ShapeDtypeStruct((sr, sc), F32),
        scratch_shapes=[pltpu.VMEM((2, tr, pc), F32), pltpu.SemaphoreType.DMA((2,)), pltpu.SemaphoreType.DMA((2,)),
                        pltpu.SemaphoreType.DMA],
        compiler_params=_cp("arbitrary"),
    )(parts)


def _pack(parts, rows):
    flat = []
    for a in parts:
        a = jnp.ravel(a).astype(F32)
        flat.append(jnp.pad(a, (0, (-a.shape[0]) % 128)))
    v = jnp.concatenate(flat)
    return jnp.pad(v, (0, rows * 128 - v.shape[0])).reshape(rows, 128)


def _unpack(block, shapes):
    lead = block.shape[:-2]
    v = block.reshape(lead + (-1,))
    out, off = [], 0
    for shp in shapes:
        n = int(np.prod(shp))
        out.append(v[..., off:off + n].reshape(lead + tuple(shp)))
        off += n + (-n) % 128
    return out


def _block_diag4(w):
    w4 = w.reshape(4, 4, 64, 64)
    eye = jnp.eye(4, dtype=w.dtype)
    return (w4[:, :, :, None, :] * eye[None, :, None, :, None]).reshape(4, 256, 256)


def _diag_blocks(bd):
    b5 = bd.reshape(4, 4, 64, 4, 64)
    return jnp.stack([b5[:, i, :, i, :] for i in range(4)], axis=1).reshape(16, 64, 64)


def _bias_window(rel_bias):
    m = (np.arange(768) + 127) % 768 - 127
    w = rel_bias[:, np.clip(512 - m, -128, 128) + 128]
    win = jnp.tile(w, (1, 128))[:, :128 * 767].reshape(8, 128, 767)[:, :, :WIN]
    qh = np.arange(128)[:, None] // CHUNK
    kc = np.arange(WIN)[None, :] // CHUNK
    valid = (kc >= qh) & (kc <= qh + 8)
    return jnp.where(jnp.asarray(valid)[None], win, NEG)


SMALL = ("b_ada", "norm_pre", "norm_post", "rel_bias", "conv_w", "conv_b", "lru_wa", "lru_ba", "lru_wx",
         "lru_bx", "lru_lambda")
WEIGHTS = ("w_ada", "b_ada", "norm_pre", "norm_post", "ffn1_w_gu", "ffn1_w_down", "w_in", "rel_bias", "conv_w",
           "conv_b", "lru_wa", "lru_ba", "lru_wx", "lru_bx", "lru_lambda", "w_att_o", "w_rec_o", "w_out",
           "ffn2_w_gu", "ffn2_w_down")


def kernel(x, c, w_ada, b_ada, norm_pre, norm_post, ffn1_w_gu, ffn1_w_down, w_in, rel_bias, conv_w, conv_b, lru_wa, lru_ba, lru_wx, lru_bx, lru_lambda, w_att_o, w_rec_o, w_out, ffn2_w_gu, ffn2_w_down, loss_target, m_w_ada, m_b_ada, m_norm_pre, m_norm_post, m_ffn1_w_gu, m_ffn1_w_down, m_w_in, m_rel_bias, m_conv_w, m_conv_b, m_lru_wa, m_lru_ba, m_lru_wx, m_lru_bx, m_lru_lambda, m_w_att_o, m_w_rec_o, m_w_out, m_ffn2_w_gu, m_ffn2_w_down, v_w_ada, v_b_ada, v_norm_pre, v_norm_post, v_ffn1_w_gu, v_ffn1_w_down, v_w_in, v_rel_bias, v_conv_w, v_conv_b, v_lru_wa, v_lru_ba, v_lru_wx, v_lru_bx, v_lru_lambda, v_w_att_o, v_w_rec_o, v_w_out, v_ffn2_w_gu, v_ffn2_w_down):
    W = dict(w_ada=w_ada, b_ada=b_ada, norm_pre=norm_pre, norm_post=norm_post, ffn1_w_gu=ffn1_w_gu,
             ffn1_w_down=ffn1_w_down, w_in=w_in, rel_bias=rel_bias, conv_w=conv_w, conv_b=conv_b, lru_wa=lru_wa,
             lru_ba=lru_ba, lru_wx=lru_wx, lru_bx=lru_bx, lru_lambda=lru_lambda, w_att_o=w_att_o, w_rec_o=w_rec_o,
             w_out=w_out, ffn2_w_gu=ffn2_w_gu, ffn2_w_down=ffn2_w_down)
    M = dict(w_ada=m_w_ada, b_ada=m_b_ada, norm_pre=m_norm_pre, norm_post=m_norm_post, ffn1_w_gu=m_ffn1_w_gu,
             ffn1_w_down=m_ffn1_w_down, w_in=m_w_in, rel_bias=m_rel_bias, conv_w=m_conv_w, conv_b=m_conv_b,
             lru_wa=m_lru_wa, lru_ba=m_lru_ba, lru_wx=m_lru_wx, lru_bx=m_lru_bx, lru_lambda=m_lru_lambda,
             w_att_o=m_w_att_o, w_rec_o=m_w_rec_o, w_out=m_w_out, ffn2_w_gu=m_ffn2_w_gu, ffn2_w_down=m_ffn2_w_down)
    V = dict(w_ada=v_w_ada, b_ada=v_b_ada, norm_pre=v_norm_pre, norm_post=v_norm_post, ffn1_w_gu=v_ffn1_w_gu,
             ffn1_w_down=v_ffn1_w_down, w_in=v_w_in, rel_bias=v_rel_bias, conv_w=v_conv_w, conv_b=v_conv_b,
             lru_wa=v_lru_wa, lru_ba=v_lru_ba, lru_wx=v_lru_wx, lru_bx=v_lru_bx, lru_lambda=v_lru_lambda,
             w_att_o=v_w_att_o, w_rec_o=v_w_rec_o, w_out=v_w_out, ffn2_w_gu=v_ffn2_w_gu, ffn2_w_down=v_ffn2_w_down)
    mx, my, mc = _mesh_pos()
    p = 2 * mx + my
    e = 4 * mx + 2 * my + mc
    xs = x[0]

    c_arr = jnp.reshape(mc, (1,)).astype(jnp.int32)
    cp_arr = jnp.stack([mc, p]).astype(jnp.int32)
    p_arr = jnp.reshape(p, (1,)).astype(jnp.int32)
    direct = ("w_att_o", "w_rec_o", "w_out", "ffn2_w_gu", "ffn2_w_down")
    geoms = [(kind, R, C, n in direct) for (n, kind, R, C) in BIG]
    names = [b[0] for b in BIG]
    placed = [ag_local(W[n][0], kind, R, C, p_arr, "ag_local_" + n) for (n, kind, R, C) in BIG[:2]]

    def arrived(fly, lo, hi, ssem, rsem, after, tag):
        done = ag_wait(fly, geoms[lo:hi], ssem, rsem, after, "ag_wait_" + tag)
        return [a if both else ag_forward(a, kind, R, C, "ag_forward_" + n)
                for a, (kind, R, C, both), n in zip(done, geoms[lo:hi], names[lo:hi])]

    g1 = ag_small(_pack([c, norm_pre, norm_post, conv_w], 32), "ag_small_params")
    c_all, npre4, npost4, cw4 = _unpack(g1, [(D,), (3, 256), (3, 256), (4, 256)])
    chipwise = lambda a: jnp.moveaxis(a[0::2], 0, 1).reshape(a.shape[1], D)
    npre, npost, conv_full = chipwise(npre4), chipwise(npost4), chipwise(cw4)

    b_cols = lax.dynamic_slice(b_ada, (0, p * 2304), (1, 2304))
    mod_cols = ada_fwd(c_all, w_ada[0], b_cols, "ada_fwd")
    g2 = ag_small(mod_cols.reshape(144, 128), "ag_mod")
    mod_all = jnp.moveaxis(g2[0::2].reshape(4, 8, 2304), 0, 1).reshape(8, 9 * D)
    mod = lax.dynamic_index_in_dim(mod_all, e, 0, keepdims=False).reshape(3, 3, D)
    zeros3 = jnp.zeros((3, D), F32)
    vecs = [jnp.concatenate([npre[k:k + 1], npost[k:k + 1], mod[k], zeros3], axis=0) for k in range(3)]

    gu_s, gu_r, gu_fly, tok_gu = ag_start(placed[:1], geoms[:1], [g2], "ag_start_ffn1_gu")
    dn_s, dn_r, dn_fly, tok0 = ag_start(placed[1:2], geoms[1:2], [tok_gu], "ag_start_ffn1_down")
    placed += [ag_local(W[n][0], kind, R, C, p_arr, "ag_local_" + n, after=[tok0]) for (n, kind, R, C) in BIG[2:]]
    f1_gu, = arrived(gu_fly, 0, 1, gu_s, gu_r, placed[2:], "ffn1_gu")
    f1_dn, = arrived(dn_fly, 1, 2, dn_s, dn_r, f1_gu, "ffn1_down")
    mix_s, mix_r, mix_fly, tok1 = ag_start(placed[2:6], geoms[2:6], [f1_gu, f1_dn], "ag_start_mixer")
    ffn_s, ffn_r, ffn_fly, tok2 = ag_start(placed[6:], geoms[6:], [tok1], "ag_start_ffn2")
    wa_bd = _block_diag4(lru_wa[0]).astype(BF16)
    wx_bd = _block_diag4(lru_wx[0]).astype(BF16)
    pvec = jnp.concatenate([conv_full, conv_b, lru_ba, lru_bx, lru_lambda], axis=0)
    bias = _bias_window(rel_bias[0]).reshape(4, 256, WIN)

    x1, h1, g1_, u1, a1, f1 = ffn_fwd(xs, vecs[0] + tok2[0:1, 0:1], f1_gu, f1_dn, 0.5, "ffn1_fwd")
    win, wao, wro, wout = arrived(mix_fly, 2, 6, mix_s, mix_r, x1, "mixer")
    h2, qkv, rest = proj_fwd(x1, vecs[1], win, "proj_fwd")
    ao = attn_fwd(qkv, bias, "attn_fwd")
    hl, hg = lru_fwd(rest, pvec, wa_bd, wx_bd, "lru_fwd")
    x2, att, rec, mg, f2 = mix_out_fwd(x1, ao, hg, rest, vecs[1], wao, wro, wout, "mix_out_fwd")
    f2_gu, f2_dn = arrived(ffn_fly, 6, 8, ffn_s, ffn_r, x2, "ffn2")
    dy, h3, g3_, u3, a3, f3, lvec = ffn_fwd(x2, vecs[2], f2_gu, f2_dn, 0.5, "ffn2_fwd", tgt=loss_target[0])

    G, grads = {}, {}
    geo = {n: (kind, R, C) for (n, kind, R, C) in BIG}

    def reduce_begin(names, tag):
        ps, rb = [], []
        for n in names:
            got = pair_push(G[n], geo[n][0], c_arr, "rs_push_" + n)
            a, b = pair_add(G[n], got, geo[n][0], cp_arr, "rs_pair_sum_" + n)
            ps.append(a)
            rb.append(b)
        return rs_start(ps, rb, [], "rs_start_" + tag)

    def reduce_end(names, flight, after, tag):
        ssem, rsem, ps, rb, _ = flight
        for a, n in zip(rs_wait(ps, rb, ssem, rsem, after, "rs_wait_" + tag), names):
            grads[n] = sum_share(a, *geo[n], "rs_sum_share_" + n)[None]

    dx2, df3, dgu3, va2 = ffn_bwd(dy, x2, f3, g3_, u3, vecs[2], f2_gu, f2_dn, 0.5, "ffn2_bwd")
    G["ffn2_w_gu"] = mm_tn(h3, dgu3, "dw_ffn2_gu", D, 1408, 2048, a_resident=True)
    G["ffn2_w_down"] = mm_tn(a3, df3, "dw_ffn2_down", 1408, D, 2048)
    fly_ffn2 = reduce_begin(("ffn2_w_gu", "ffn2_w_down"), "ffn2")
    vec1 = vecs[1] + fly_ffn2[4][0:1, 0:1]
    df2, d_att, d_rec, dao, dhl, d3, va_out = mix_out_bwd(dx2, f2, att, rec, rest, hl, vec1, wao, wro, wout,
                                                          "mix_out_bwd")
    G["w_out"] = mm_tn(mg, df2, "dw_out", D, D, 1024)
    G["w_att_o"] = mm_tn(ao, d_att, "dw_att_o", 512, D, 1024)
    G["w_rec_o"] = mm_tn(hg, d_rec, "dw_rec_o", D, D, 1024)
    dq, db, dkv = attn_bwd(qkv, dao, bias, "attn_bwd")
    dxr, v_lru, dwa_bd, dwx_bd = lru_bwd(dhl, hl, rest, pvec, wa_bd, wx_bd, "lru_bwd")
    lru_w = jnp.concatenate([_diag_blocks(dwa_bd), _diag_blocks(dwx_bd)]).reshape(1, 1024, 128)
    lru_slots = lax.dynamic_update_slice(jnp.zeros((N_DEV, 1024, 128), F32), lru_w, (e, 0, 0))
    lw_s, lw_r, lw_fly, lw_tok = slot_start(lru_slots, "lru_w_start")
    dx1, va_in = proj_bwd(dq, dkv, dxr, d3, win, x1, dx2, vecs[1] + lw_tok[0:1, 0:1], "proj_bwd")
    G["w_in"] = dw_in(h2, dq, dkv, dxr, d3, "dw_in")
    fly_mix = reduce_begin(("w_in", "w_att_o", "w_rec_o", "w_out"), "mixer")
    vec0 = vecs[0] + fly_mix[4][0:1, 0:1]
    dx0, df1, dgu1, va0 = ffn_bwd(dx1, xs, f1, g1_, u1, vec0, f1_gu, f1_dn, 0.5, "ffn1_bwd")
    G["ffn1_w_gu"] = mm_tn(h1, dgu1, "dw_ffn1_gu", D, 1408, 2048, a_resident=True)
    G["ffn1_w_down"] = mm_tn(a1, df1, "dw_ffn1_down", 1408, D, 2048)
    fly_ffn1 = reduce_begin(("ffn1_w_gu", "ffn1_w_down"), "ffn1")
    reduce_end(("ffn2_w_gu", "ffn2_w_down"), fly_ffn2, [fly_ffn1[4]], "ffn2")
    reduce_end(("w_in", "w_att_o", "w_rec_o", "w_out"), fly_mix, [fly_ffn1[4], grads["ffn2_w_down"]], "mixer")

    va1 = va_out + va_in
    vas = (va0, va1, va2)
    dmod = jnp.stack([v[2:5] for v in vas])
    part = {"b_ada": dmod, "norm_pre": jnp.stack([v[0] for v in vas]), "norm_post": jnp.stack([v[1] for v in vas]),
            "rel_bias": bias_grad(db.reshape(8, 128, WIN), "bias_grad")[:, :257], "conv_w": v_lru[0:4], "conv_b": v_lru[4],
            "lru_ba": v_lru[5], "lru_bx": v_lru[6], "lru_lambda": v_lru[7]}
    full_shapes = {"b_ada": (9 * D,), "norm_pre": (3, D), "norm_post": (3, D), "rel_bias": (8, 257),
                   "conv_w": (4, D), "conv_b": (D,), "lru_wa": (16, 64, 64), "lru_ba": (D,),
                   "lru_wx": (16, 64, 64), "lru_bx": (D,), "lru_lambda": (D,)}
    gathered = [n for n in SMALL if n in part]
    g3 = ag_small(_pack([part[n] for n in gathered] + [lvec[0:1, 0:1]], 208), "ag_small_grads")
    summed = _unpack(sum_lead(g3, "sum_small_grads"), [full_shapes[n] for n in gathered] + [(1,)])
    red = dict(zip(gathered, summed[:-1]))
    loss = summed[-1][0]
    lru_all = slot_wait(lw_fly, lw_s, lw_r, [dx0], "lru_w_wait")
    red["lru_wa"], red["lru_wx"] = sum_lead(lru_all, "sum_lru_w").reshape(2, 16, 64, 64)
    cols = lambda a: lax.dynamic_slice(a, (0, p * 256), (a.shape[0], 256))
    grads.update({"b_ada": red["b_ada"][None], "norm_pre": cols(red["norm_pre"])[None],
                  "norm_post": cols(red["norm_post"])[None], "rel_bias": red["rel_bias"][None],
                  "conv_w": cols(red["conv_w"])[None], "conv_b": red["conv_b"][None], "lru_wa": red["lru_wa"][None],
                  "lru_ba": red["lru_ba"][None], "lru_wx": red["lru_wx"][None], "lru_bx": red["lru_bx"][None],
                  "lru_lambda": red["lru_lambda"][None]})

    dmod_all = g3[:, :72].reshape(8, 9 * D)
    dmod_cols = jnp.pad(lax.dynamic_slice(dmod_all, (0, p * 2304), (8, 2304)), ((0, 120), (0, 0)))
    c_all_t = jnp.pad(c_all.T, ((0, 0), (0, 120)))
    grads["w_ada"] = ada_bwd(c_all_t, dmod_cols, "ada_bwd")[None]

    delta, new_m, new_v = {}, {}, {}

    def update(n):
        shp = W[n].shape
        res = adamw(W[n][0], grads[n][0], M[n][0], V[n][0], "adamw_" + n, emit_g=n in geo)
        delta[n], new_m[n], new_v[n] = [a.reshape(shp) for a in res[:3]]
        if n in geo:
            grads[n] = res[3].reshape(shp)

    for n in ("w_ada", "ffn2_w_gu", "ffn2_w_down", "w_in", "w_att_o", "w_rec_o", "w_out"):
        update(n)
    packed = [_pack([src[n] for n in SMALL], 1168) for src in (W, grads, M, V)]
    outs = adamw(*packed, "adamw_small")
    for dst, blk in zip((delta, new_m, new_v), outs):
        for n, a in zip(SMALL, _unpack(blk, [W[n].shape for n in SMALL])):
            dst[n] = a
    reduce_end(("ffn1_w_gu", "ffn1_w_down"), fly_ffn1,
               [outs[0], delta["w_ada"], delta["ffn2_w_gu"], delta["ffn2_w_down"], delta["w_in"], delta["w_out"]], "ffn1")
    for n in ("ffn1_w_gu", "ffn1_w_down"):
        update(n)

    return (loss, dx0[None], *[grads[n] for n in WEIGHTS], *[delta[n] for n in WEIGHTS],
            *[new_m[n] for n in WEIGHTS], *[new_v[n] for n in WEIGHTS])
```

```python
import numpy as np
import jax
import jax.numpy as jnp
from jax import lax
from jax.experimental import pallas as pl
from jax.experimental.pallas import tpu as pltpu

F32 = jnp.float32
BF16 = jnp.bfloat16

D = 1024
FF = 2816
PW = 5632
HP = 128
CHUNK = 64
WIN = 640
TQ = 512
EPS = 1e-6
NEG = -1e30
LRU_C = 8.0
N_DEV = 8
VMEM_LIMIT = 56 * 1024 * 1024

ADAM_LR, ADAM_B1, ADAM_B2, ADAM_EPS, ADAM_WD, ADAM_STEP = 0.001, 0.9, 0.999, 1e-08, 0.01, 10

MESH = pl.DeviceIdType.MESH
ANY = pl.BlockSpec(memory_space=pl.ANY)


def _cp(*sem):
    return pltpu.CompilerParams(dimension_semantics=tuple(sem), vmem_limit_bytes=VMEM_LIMIT)


def _dot(a, b):
    return jnp.dot(a, b, preferred_element_type=F32)


def _dot_nt(a, b):
    return lax.dot_general(a, b, (((1,), (1,)), ((), ())), preferred_element_type=F32)


def _dot_tn(a, b):
    return lax.dot_general(a, b, (((0,), (0,)), ((), ())), preferred_element_type=F32)


def _mean(v):
    return jnp.mean(v, axis=-1, keepdims=True)


def _colsum(v):
    return jnp.sum(v, axis=0, keepdims=True)


def _sigmoid(v):
    return 0.5 * jnp.tanh(0.5 * v) + 0.5


_GK = 0.7978845608028654


def _gelu(v):
    t = jnp.tanh(_GK * (v + 0.044715 * v * v * v))
    return 0.5 * v * (1.0 + t)


def _pre_norm(xv, vec_ref):
    r = lax.rsqrt(_mean(xv * xv) + EPS)
    n = xv * r * vec_ref[0:1, :]
    return n * (1.0 + vec_ref[3:4, :]) + vec_ref[2:3, :]


def _pre_norm_bwd(dh, xv, dres, vec_ref, vacc_ref):
    r = lax.rsqrt(_mean(xv * xv) + EPS)
    xh = xv * r
    n = xh * vec_ref[0:1, :]
    vacc_ref[2:3, :] += _colsum(dh)
    vacc_ref[3:4, :] += _colsum(dh * n)
    dn = dh * (1.0 + vec_ref[3:4, :])
    vacc_ref[0:1, :] += _colsum(dn * xh)
    dxh = dn * vec_ref[0:1, :]
    return r * (dxh - xh * _mean(dxh * xh)) + dres


def _post_norm_bwd(dxo, fv, res, vec_ref, vacc_ref):
    rf = lax.rsqrt(_mean(fv * fv) + EPS)
    fh = fv * rf
    gp = vec_ref[1:2, :]
    vacc_ref[4:5, :] += _colsum(res * dxo * (fh * gp))
    dy = (res * vec_ref[4:5, :]) * dxo
    vacc_ref[1:2, :] += _colsum(dy * fh)
    dfn = dy * gp
    return rf * (dfn - fh * _mean(dfn * fh))


def _u_spec(tf):
    return pl.BlockSpec((pl.Element(D), pl.Element(tf)),
                        lambda i, j: (0, pl.multiple_of(jnp.minimum(FF + j * tf, 2 * FF - tf), 128)))


def ffn_fwd(x, vec, w_gu, w_dn, res, name, tgt=None, tm=1024, tf=512):
    S = x.shape[0]
    tm = min(tm, S)
    nt = S // tm
    nf = -(-FF // tf)
    tail = FF - tf * (nf - 1)
    halves = [pl.ds(r * (tm // 2), tm // 2) for r in range(2)]
    head = tgt is not None

    def body(*refs):
        x_ref, vec_ref, wg_ref, wu_ref, wd_ref = refs[:5]
        if head:
            t_hbm, xo_ref, h_ref, g_ref, u_ref, a_ref, f_ref, l_ref, hs, acc, lacc, ts, tsem = refs[5:]
        else:
            xo_ref, h_ref, g_ref, u_ref, a_ref, f_ref, hs, acc = refs[5:]
        i, j = pl.program_id(0), pl.program_id(1)
        if head:
            late = pltpu.make_async_copy(t_hbm.at[pl.ds(pl.multiple_of(i * tm, 8), tm)], ts, tsem)

        @pl.when(j == 0)
        def _():
            if head:
                late.start()
            h = _pre_norm(x_ref[...], vec_ref).astype(BF16)
            hs[...] = h
            h_ref[...] = h
            acc[...] = jnp.zeros_like(acc)

        def chunk(w):
            gu = [(_dot(hs[r, :], wg_ref[:, 0:w]), _dot(hs[r, :], wu_ref[:, tf - w:tf])) for r in halves]
            acts = []
            for r, (g, u) in zip(halves, gu):
                g_ref[r, 0:w] = g.astype(BF16)
                u_ref[r, 0:w] = u.astype(BF16)
                a = (g * _sigmoid(g) * u).astype(BF16)
                a_ref[r, 0:w] = a
                acts.append(a)
            for r, a in zip(halves, acts):
                acc[r, :] += _dot(a, wd_ref[0:w, :])

        @pl.when(j < nf - 1)
        def _():
            chunk(tf)

        @pl.when(j == nf - 1)
        def _():
            chunk(tail)
            f = acc[...]
            f_ref[...] = f.astype(BF16)
            y = f * lax.rsqrt(_mean(f * f) + EPS) * vec_ref[1:2, :]
            xo = x_ref[...] + (res * vec_ref[4:5, :]) * y
            if head:
                @pl.when(i == 0)
                def _():
                    lacc[...] = jnp.zeros_like(lacc)

                late.wait()
                d = xo - ts[...]
                xo_ref[...] = d * (1.0 / D)
                lacc[...] += _colsum(d * d)

                @pl.when(i == nt - 1)
                def _():
                    l_ref[...] = jnp.broadcast_to(0.5 * jnp.sum(lacc[...]) * (1.0 / D), (8, 128))
            else:
                xo_ref[...] = xo

    row = lambda i, j: (i, 0)
    col = lambda i, j: (i, j)
    in_specs = [pl.BlockSpec((tm, D), row), pl.BlockSpec((8, D), lambda i, j: (0, 0)),
                pl.BlockSpec((D, tf), lambda i, j: (0, j)), _u_spec(tf),
                pl.BlockSpec((tf, D), lambda i, j: (j, 0))]
    out_specs = [pl.BlockSpec((tm, D), row), pl.BlockSpec((tm, D), row), pl.BlockSpec((tm, tf), col),
                 pl.BlockSpec((tm, tf), col), pl.BlockSpec((tm, tf), col), pl.BlockSpec((tm, D), row)]
    out_shape = [jax.ShapeDtypeStruct((S, D), F32), jax.ShapeDtypeStruct((S, D), BF16),
                 jax.ShapeDtypeStruct((S, FF), BF16), jax.ShapeDtypeStruct((S, FF), BF16),
                 jax.ShapeDtypeStruct((S, FF), BF16), jax.ShapeDtypeStruct((S, D), BF16)]
    scratch = [pltpu.VMEM((tm, D), BF16), pltpu.VMEM((tm, D), F32)]
    args = [x, vec, w_gu, w_gu, w_dn]
    if head:
        in_specs.append(ANY)
        out_specs.append(pl.BlockSpec((8, 128), lambda i, j: (0, 0)))
        out_shape.append(jax.ShapeDtypeStruct((8, 128), F32))
        scratch += [pltpu.VMEM((1, D), F32), pltpu.VMEM((tm, D), F32), pltpu.SemaphoreType.DMA]
        args.append(tgt)
    return pl.pallas_call(
        body, name=name, grid=(nt, nf), in_specs=in_specs, out_specs=out_specs, out_shape=out_shape,
        scratch_shapes=scratch,
        compiler_params=_cp("arbitrary" if head else "parallel", "arbitrary"),
    )(*args)


def ffn_bwd(dxo, x, f, g, u, vec, w_gu, w_dn, res, name, tm=1024, tf=512):
    S = x.shape[0]
    tm = min(tm, S)
    nf = -(-FF // tf)
    tail = FF - tf * (nf - 1)
    halves = [pl.ds(r * (tm // 2), tm // 2) for r in range(2)]

    def body(dxo_ref, x_hbm, f_ref, g_ref, u_ref, vec_ref, wg_ref, wu_ref, wd_ref,
             dx_ref, df_ref, dgu_ref, vacc_ref, dfs, acc, xs, xsem):
        i, j = pl.program_id(0), pl.program_id(1)
        late = pltpu.make_async_copy(x_hbm.at[pl.ds(pl.multiple_of(i * tm, 8), tm)], xs, xsem)

        @pl.when((i == 0) & (j == 0))
        def _():
            vacc_ref[...] = jnp.zeros_like(vacc_ref)

        @pl.when(j == 0)
        def _():
            late.start()
            df = _post_norm_bwd(dxo_ref[...], f_ref[...].astype(F32), res, vec_ref, vacc_ref).astype(BF16)
            dfs[...] = df
            df_ref[...] = df
            acc[...] = jnp.zeros_like(acc)

        def chunk(w):
            da = [_dot_nt(dfs[h, :], wd_ref[0:w, :]) for h in halves]
            dgu = []
            for h, d in zip(halves, da):
                gv, uv = g_ref[h, 0:w].astype(F32), u_ref[h, 0:w].astype(F32)
                sg = _sigmoid(gv)
                dg = (d * uv * (sg * (1.0 + gv * (1.0 - sg)))).astype(BF16)
                du = (d * (gv * sg)).astype(BF16)
                dgu_ref[0, h, 0:w] = dg
                dgu_ref[1, h, 0:w] = du
                dgu.append((dg, du))
            for h, (dg, du) in zip(halves, dgu):
                acc[h, :] += _dot_nt(dg, wg_ref[:, 0:w]) + _dot_nt(du, wu_ref[:, tf - w:tf])

        @pl.when(j < nf - 1)
        def _():
            chunk(tf)

        @pl.when(j == nf - 1)
        def _():
            chunk(tail)
            late.wait()
            dx_ref[...] = _pre_norm_bwd(acc[...], xs[...], dxo_ref[...], vec_ref, vacc_ref)

    row = lambda i, j: (i, 0)
    col = lambda i, j: (i, j)
    return pl.pallas_call(
        body, name=name, grid=(S // tm, nf),
        in_specs=[pl.BlockSpec((tm, D), row), ANY, pl.BlockSpec((tm, D), row),
                  pl.BlockSpec((tm, tf), col), pl.BlockSpec((tm, tf), col),
                  pl.BlockSpec((8, D), lambda i, j: (0, 0)),
                  pl.BlockSpec((D, tf), lambda i, j: (0, j)), _u_spec(tf),
                  pl.BlockSpec((tf, D), lambda i, j: (j, 0))],
        out_specs=[pl.BlockSpec((tm, D), row), pl.BlockSpec((tm, D), row),
                   pl.BlockSpec((2, tm, tf), lambda i, j: (0, i, j)),
                   pl.BlockSpec((8, D), lambda i, j: (0, 0))],
        out_shape=[jax.ShapeDtypeStruct((S, D), F32), jax.ShapeDtypeStruct((S, D), BF16),
                   jax.ShapeDtypeStruct((2, S, FF), BF16), jax.ShapeDtypeStruct((8, D), F32)],
        scratch_shapes=[pltpu.VMEM((tm, D), BF16), pltpu.VMEM((tm, D), F32), pltpu.VMEM((tm, D), F32),
                        pltpu.SemaphoreType.DMA],
        compiler_params=_cp("arbitrary", "arbitrary"),
    )(dxo, x, f, g, u, vec, w_gu, w_gu, w_dn)


def mm_tn(a, b, name, tm, tn, tk, out_dtype=BF16, a_resident=False):
    S, M = a.shape
    if b.ndim == 3:
        G, _, Nf = b.shape
    else:
        G, Nf = 1, b.shape[1]
    N = G * Nf
    tk = min(tk, S)
    nbf = Nf // tn
    nk = S // tk

    def body(a_ref, b_ref, o_ref, acc):
        k = pl.program_id(2)

        @pl.when(k == 0)
        def _():
            acc[...] = jnp.zeros_like(acc)

        a_blk = a_ref[pl.ds(pl.multiple_of(k * tk, 16), tk), :] if a_resident else a_ref[...]
        acc[...] += _dot_tn(a_blk, b_ref[...])

        @pl.when(k == nk - 1)
        def _():
            o_ref[...] = acc[...].astype(out_dtype)

    if b.ndim == 3:
        b_spec = pl.BlockSpec((None, tk, tn), lambda i, j, k: (j // nbf, k, j % nbf))
    else:
        b_spec = pl.BlockSpec((tk, tn), lambda i, j, k: (k, j))
    if a_resident:
        a_spec = pl.BlockSpec((S, M), lambda i, j, k: (0, 0), pipeline_mode=pl.Buffered(1))
    else:
        a_spec = pl.BlockSpec((tk, tm), lambda i, j, k: (k, i))
    return pl.pallas_call(
        body, name=name, grid=(M // tm, N // tn, nk),
        in_specs=[a_spec, b_spec],
        out_specs=pl.BlockSpec((tm, tn), lambda i, j, k: (i, j)),
        out_shape=jax.ShapeDtypeStruct((M, N), out_dtype),
        scratch_shapes=[pltpu.VMEM((tm, tn), F32)],
        compiler_params=_cp("parallel", "parallel", "arbitrary"),
    )(a, b)


def proj_fwd(x, vec, w_in, name, tm=2048, tn=512):
    S = x.shape[0]
    tm = min(tm, S)
    nq = 1536 // tn

    def body(x_ref, vec_ref, w_ref, h_ref, qkv_ref, rest_ref, hs):
        j = pl.program_id(1)

        @pl.when(j == 0)
        def _():
            h = _pre_norm(x_ref[...], vec_ref).astype(BF16)
            hs[...] = h
            h_ref[...] = h

        r = _dot(hs[...], w_ref[...])

        @pl.when(j < nq)
        def _():
            qkv_ref[...] = r.astype(BF16)

        @pl.when(j >= nq)
        def _():
            rest_ref[...] = r.astype(BF16)

    row = lambda i, j: (i, 0)
    return pl.pallas_call(
        body, name=name, grid=(S // tm, PW // tn),
        in_specs=[pl.BlockSpec((tm, D), row), pl.BlockSpec((8, D), lambda i, j: (0, 0)),
                  pl.BlockSpec((D, tn), lambda i, j: (0, j))],
        out_specs=[pl.BlockSpec((tm, D), row),
                   pl.BlockSpec((tm, tn), lambda i, j: (i, jnp.minimum(j, nq - 1))),
                   pl.BlockSpec((tm, tn), lambda i, j: (i, jnp.maximum(j - nq, 0)))],
        out_shape=[jax.ShapeDtypeStruct((S, D), BF16), jax.ShapeDtypeStruct((S, 1536), BF16),
                   jax.ShapeDtypeStruct((S, 4096), BF16)],
        scratch_shapes=[pltpu.VMEM((tm, D), BF16)],
        compiler_params=_cp("parallel", "arbitrary"),
    )(x, vec, w_in)


def proj_bwd(dq, dkv, dxr, d3, w_in, x, dxo, vec, name, tm=2048, tk=512):
    S = x.shape[0]
    tm = min(tm, S)
    nk = PW // tk

    def body(dq_ref, dkv_ref, dxr_ref, d3_ref, w_ref, x_hbm, dxo_hbm, vec_ref, dx_ref, vacc_ref, acc, xs, dxos, sems):
        i, j = pl.program_id(0), pl.program_id(1)
        tok = pl.ds(pl.multiple_of(i * tm, 8), tm)
        late = (pltpu.make_async_copy(x_hbm.at[tok], xs, sems.at[0]),
                pltpu.make_async_copy(dxo_hbm.at[tok], dxos, sems.at[1]))

        @pl.when((i == 0) & (j == 0))
        def _():
            vacc_ref[...] = jnp.zeros_like(vacc_ref)

        @pl.when(j == 0)
        def _():
            for cp in late:
                cp.start()
            acc[...] = _dot_nt(dq_ref[...], w_ref[...])

        @pl.when((j >= 1) & (j < 3))
        def _():
            acc[...] += _dot_nt(dkv_ref[...], w_ref[...])

        @pl.when((j >= 3) & (j < 5))
        def _():
            acc[...] += _dot_nt(dxr_ref[...], w_ref[...])

        @pl.when(j >= 5)
        def _():
            acc[...] += _dot_nt(d3_ref[...], w_ref[...])

        @pl.when(j == nk - 1)
        def _():
            for cp in late:
                cp.wait()
            dx_ref[...] = _pre_norm_bwd(acc[...], xs[...], dxos[...], vec_ref, vacc_ref)

    row = lambda i, j: (i, 0)
    return pl.pallas_call(
        body, name=name, grid=(S // tm, nk),
        in_specs=[pl.BlockSpec((None, tm, tk), lambda i, j: (0, i, 0)),
                  pl.BlockSpec((None, tm, tk), lambda i, j: (jnp.clip(j - 1, 0, 1), i, 0)),
                  pl.BlockSpec((tm, tk), lambda i, j: (i, jnp.clip(j - 3, 0, 1))),
                  pl.BlockSpec((None, tm, tk), lambda i, j: (jnp.clip(j - 5, 0, 5) // 2, i, jnp.clip(j - 5, 0, 5) % 2)),
                  pl.BlockSpec((D, tk), lambda i, j: (0, j)),
                  ANY, ANY,
                  pl.BlockSpec((8, D), lambda i, j: (0, 0))],
        out_specs=[pl.BlockSpec((tm, D), row, pipeline_mode=pl.Buffered(1)),
                   pl.BlockSpec((8, D), lambda i, j: (0, 0))],
        out_shape=[jax.ShapeDtypeStruct((S, D), F32), jax.ShapeDtypeStruct((8, D), F32)],
        scratch_shapes=[pltpu.VMEM((tm, D), F32), pltpu.VMEM((tm, D), F32), pltpu.VMEM((tm, D), F32),
                        pltpu.SemaphoreType.DMA((2,))],
        compiler_params=_cp("arbitrary", "arbitrary"),
    )(dq, dkv, dxr, d3, w_in, x, dxo, vec)


def _two_heads(v, lane):
    zero = jnp.zeros((), v.dtype)
    return jnp.concatenate([jnp.where(lane < 64, v, zero), jnp.where(lane >= 64, v, zero)], axis=0)


def _attn_scores(qm, ka, bias_h, i, grp):
    s = _dot_nt(qm, ka) + bias_h
    col = lax.broadcasted_iota(jnp.int32, s.shape, 1)
    first_key = jnp.where(i == 0, 512 - 128 * grp, 0)
    return jnp.where(col >= first_key, s, NEG)


def _softmax(s):
    e = jnp.exp(s - jnp.max(s, axis=-1, keepdims=True))
    return e * (1.0 / jnp.sum(e, axis=-1, keepdims=True))


NG = TQ // 128


def attn_fwd(qkv, bias, name):
    S = qkv.shape[0]
    nb = S // TQ

    def body(q_ref, kp_ref, kc_ref, vp_ref, vc_ref, b_ref, o_ref, kw, vw):
        i = pl.program_id(1)
        kw[0:TQ, :] = kp_ref[...]
        kw[TQ:2 * TQ, :] = kc_ref[...]
        vw[0:TQ, :] = vp_ref[...]
        vw[TQ:2 * TQ, :] = vc_ref[...]
        lane = lax.broadcasted_iota(jnp.int32, (1, HP), 1)

        rows = [pl.ds(128 * a, 128) for a in range(NG)]
        keys = [pl.ds(128 * a, WIN) for a in range(NG)]
        q2 = [_two_heads(q_ref[r, :] * jnp.asarray(0.125, BF16), lane) for r in rows]
        s = [_attn_scores(q2[a], kw[keys[a], :], b_ref[...], i, a) for a in range(NG)]
        p = [_softmax(sa).astype(BF16) for sa in s]
        o2 = [_dot(p[a], vw[keys[a], :]) for a in range(NG)]
        for a in range(NG):
            o_ref[rows[a], :] = jnp.where(lane < 64, o2[a][0:128], o2[a][128:256]).astype(BF16)

    prev = lambda h, i: (jnp.maximum(i - 1, 0), 0)
    return pl.pallas_call(
        body, name=name, grid=(4, nb),
        in_specs=[pl.BlockSpec((TQ, HP), lambda h, i: (i, h)),
                  pl.BlockSpec((TQ, HP), lambda h, i: (jnp.maximum(i - 1, 0), 4 + h)),
                  pl.BlockSpec((TQ, HP), lambda h, i: (i, 4 + h)),
                  pl.BlockSpec((TQ, HP), lambda h, i: (jnp.maximum(i - 1, 0), 8 + h)),
                  pl.BlockSpec((TQ, HP), lambda h, i: (i, 8 + h)),
                  pl.BlockSpec((None, 256, WIN), lambda h, i: (h, 0, 0))],
        out_specs=pl.BlockSpec((TQ, HP), lambda h, i: (i, h)),
        out_shape=jax.ShapeDtypeStruct((S, 512), BF16),
        scratch_shapes=[pltpu.VMEM((2 * TQ, HP), BF16), pltpu.VMEM((2 * TQ, HP), BF16)],
        compiler_params=_cp("parallel", "arbitrary"),
    )(qkv, qkv, qkv, qkv, qkv, bias)


def attn_bwd(qkv, do, bias, name):
    S = qkv.shape[0]
    nb = S // TQ

    def body(q_ref, kp_ref, kc_ref, vp_ref, vc_ref, do_ref, b_ref, dqkv_ref, db_ref, dkv_ref, kw, vw, ak, av):
        i = pl.program_id(1)

        @pl.when(i == 0)
        def _():
            db_ref[...] = jnp.zeros_like(db_ref)
            ak[...] = jnp.zeros_like(ak)
            av[...] = jnp.zeros_like(av)

        @pl.when(i > 0)
        def _():
            ak[0:TQ, :] = ak[TQ:2 * TQ, :]
            av[0:TQ, :] = av[TQ:2 * TQ, :]
            ak[TQ:2 * TQ, :] = jnp.zeros((TQ, HP), F32)
            av[TQ:2 * TQ, :] = jnp.zeros((TQ, HP), F32)

        @pl.when(i < nb)
        def _():
            kw[0:TQ, :] = kp_ref[...]
            kw[TQ:2 * TQ, :] = kc_ref[...]
            vw[0:TQ, :] = vp_ref[...]
            vw[TQ:2 * TQ, :] = vc_ref[...]
            lane = lax.broadcasted_iota(jnp.int32, (1, HP), 1)

            rows = [pl.ds(128 * a, 128) for a in range(NG)]
            keys = [pl.ds(128 * a, WIN) for a in range(NG)]
            q2 = [_two_heads(q_ref[r, :] * jnp.asarray(0.125, BF16), lane) for r in rows]
            do2 = [_two_heads(do_ref[r, :], lane) for r in rows]
            s = [_attn_scores(q2[a], kw[keys[a], :], b_ref[...], i, a) for a in range(NG)]
            dp = [_dot_nt(do2[a], vw[keys[a], :]) for a in range(NG)]
            p = [_softmax(sa) for sa in s]
            ds = [p[a] * (dp[a] - jnp.sum(p[a] * dp[a], axis=-1, keepdims=True)) for a in range(NG)]
            db_ref[...] += (ds[0] + ds[1]) + (ds[2] + ds[3])
            dsb = [d.astype(BF16) for d in ds]
            dq2 = [_dot(dsb[a], kw[keys[a], :]) for a in range(NG)]
            dk = [_dot_tn(dsb[a], q2[a]) for a in range(NG)]
            dv = [_dot_tn(p[a].astype(BF16), do2[a]) for a in range(NG)]
            for a in range(NG):
                ak[keys[a], :] += dk[a]
                av[keys[a], :] += dv[a]
                dq = jnp.where(lane < 64, dq2[a][0:128], dq2[a][128:256])
                dqkv_ref[0, rows[a], :] = (dq * 0.125).astype(BF16)

        @pl.when(i > 0)
        def _():
            dkv_ref[0] = ak[0:TQ, :].astype(BF16)
            dkv_ref[1] = av[0:TQ, :].astype(BF16)

    cur = lambda i: jnp.minimum(i, nb - 1)
    prv = lambda i: jnp.clip(i - 1, 0, nb - 1)
    dq, db, dkv = pl.pallas_call(
        body, name=name, grid=(4, nb + 1),
        in_specs=[pl.BlockSpec((TQ, HP), lambda h, i: (cur(i), h)),
                  pl.BlockSpec((TQ, HP), lambda h, i: (prv(i), 4 + h)),
                  pl.BlockSpec((TQ, HP), lambda h, i: (cur(i), 4 + h)),
                  pl.BlockSpec((TQ, HP), lambda h, i: (prv(i), 8 + h)),
                  pl.BlockSpec((TQ, HP), lambda h, i: (cur(i), 8 + h)),
                  pl.BlockSpec((TQ, HP), lambda h, i: (cur(i), h)),
                  pl.BlockSpec((None, 256, WIN), lambda h, i: (h, 0, 0))],
        out_specs=[pl.BlockSpec((1, TQ, HP), lambda h, i: (0, cur(i), h)),
                   pl.BlockSpec((None, 256, WIN), lambda h, i: (h, 0, 0)),
                   pl.BlockSpec((2, TQ, HP), lambda h, i: (0, prv(i), h))],
        out_shape=[jax.ShapeDtypeStruct((1, S, 512), BF16), jax.ShapeDtypeStruct((4, 256, WIN), F32),
                   jax.ShapeDtypeStruct((2, S, 512), BF16)],
        scratch_shapes=[pltpu.VMEM((2 * TQ, HP), BF16), pltpu.VMEM((2 * TQ, HP), BF16),
                        pltpu.VMEM((2 * TQ, HP), F32), pltpu.VMEM((2 * TQ, HP), F32)],
        compiler_params=_cp("parallel", "arbitrary"),
    )(qkv, qkv, qkv, qkv, qkv, do, bias)
    return dq, db, dkv


def bias_grad(db, name):
    def body(db_ref, o_ref):
        r = lax.broadcasted_iota(jnp.int32, (128, 128), 0)
        c = lax.broadcasted_iota(jnp.int32, (128, 128), 1)
        flip = (r + c == 127).astype(BF16)
        lane = lax.broadcasted_iota(jnp.int32, (16, 384), 1)
        src = lax.broadcasted_iota(jnp.int32, (128, 384), 0)
        dst = lax.broadcasted_iota(jnp.int32, (128, 384), 1)

        def split_dot(v, m):
            hi = v.astype(BF16)
            r1 = v - hi.astype(F32)
            mid = r1.astype(BF16)
            lo = (r1 - mid.astype(F32)).astype(BF16)
            return _dot(hi, m) + _dot(mid, m) + _dot(lo, m)

        def diag_sums(w):
            y = pltpu.roll(split_dot(w, flip), 0, 1, stride=1, stride_axis=0)
            return jnp.broadcast_to(_colsum(y), (16, 128))

        w4 = db_ref[0, :, 512:640]
        w3 = db_ref[0, :, 384:512]
        far = jnp.sum(db_ref[0, :, 0:384]) + jnp.sum(jnp.where(r >= c, w3, 0.0))
        lo4 = diag_sums(jnp.where(r >= c, w4, 0.0))
        up4 = diag_sums(jnp.where(r < c, w4, 0.0))
        up3 = diag_sums(jnp.where(r < c, w3, 0.0))
        p_lo4 = (dst == 128 + (src + 1) % 128).astype(BF16)
        p_up4 = ((dst == src + 1) & (src < 127)).astype(BF16)
        p_up3 = ((dst == src + 129) & (src < 127)).astype(BF16)
        out = split_dot(lo4, p_lo4) + split_dot(up4, p_up4) + split_dot(up3, p_up3)
        o_ref[0] = out + jnp.where(lane == 256, far, 0.0)

    return pl.pallas_call(
        body, name=name, grid=(8,),
        in_specs=[pl.BlockSpec((1, 128, WIN), lambda h: (h, 0, 0))],
        out_specs=pl.BlockSpec((1, 16, 384), lambda h: (h, 0, 0)),
        out_shape=jax.ShapeDtypeStruct((8, 16, 384), F32),
        compiler_params=_cp("parallel"),
    )(db)[:, 0, :]


LT = 1024
LC = 512


def _lru_gates(xs, pv_ref, wa_ref, wx_ref, tl):
    xc = (pv_ref[4:5, :] + pv_ref[3:4, :] * xs[pl.ds(8, tl), :] + pv_ref[2:3, :] * xs[pl.ds(7, tl), :]
          + pv_ref[1:2, :] * xs[pl.ds(6, tl), :] + pv_ref[0:1, :] * xs[pl.ds(5, tl), :])
    xcb = xc.astype(BF16)
    pa = jnp.concatenate([_dot(xcb[:, 0:256], wa_ref[0]), _dot(xcb[:, 256:512], wa_ref[1])], axis=1)
    px = jnp.concatenate([_dot(xcb[:, 0:256], wx_ref[0]), _dot(xcb[:, 256:512], wx_ref[1])], axis=1)
    r = _sigmoid(pa + pv_ref[5:6, :])
    ig = _sigmoid(px + pv_ref[6:7, :])
    z = -pv_ref[7:8, :]
    sp = jnp.maximum(z, 0.0) + jnp.log1p(jnp.exp(-jnp.abs(z)))
    log_a = (-LRU_C * r) * sp
    a = jnp.exp(log_a)
    s = jnp.tanh(-log_a) * (1.0 + a * a)
    inv_mult = lax.rsqrt(s)
    mult = jnp.where(s > 0.0, s * inv_mult, 0.0)
    return xc, xcb, r, ig, sp, a, mult, inv_mult


def lru_fwd(rest, pvec, wa, wx, name):
    S = rest.shape[0]
    tl = min(LT, S)
    nt = S // tl

    def body(xr_ref, halo_ref, yr_ref, pv_ref, wa_ref, wx_ref, h_ref, hg_ref, xs, a_s, u_s, h_s, carry):
        ti = pl.program_id(1)

        @pl.when(ti == 0)
        def _():
            carry[...] = jnp.zeros_like(carry)

        xs[0:8, :] = jnp.where(ti > 0, halo_ref[8:16, :].astype(F32), 0.0)
        xs[pl.ds(8, tl), :] = xr_ref[...].astype(F32)
        xc, _, _, ig, _, a, mult, _ = _lru_gates(xs, pv_ref, wa_ref, wx_ref, tl)
        a_s[...] = a
        u_s[...] = mult * (ig * xc)
        row = lax.broadcasted_iota(jnp.int32, (8, LC), 0)

        def blk(bi, c):
            o = pl.multiple_of(bi * 8, 8)
            av = a_s[pl.ds(o, 8), :]
            bv = u_s[pl.ds(o, 8), :]
            for d in (1, 2, 4):
                a_sh = pltpu.roll(av, d, 0)
                b_sh = pltpu.roll(bv, d, 0)
                m = row >= d
                bv = jnp.where(m, av * b_sh + bv, bv)
                av = jnp.where(m, av * a_sh, av)
            hv = bv + av * c
            h_s[pl.ds(o, 8), :] = hv
            return hv[7:8, :]

        carry[...] = lax.fori_loop(0, tl // 8, blk, carry[...])
        h = h_s[...]
        h_ref[...] = h
        hg_ref[...] = (h * _gelu(yr_ref[...].astype(F32))).astype(BF16)

    hb = tl // 16
    return pl.pallas_call(
        body, name=name, grid=(2, nt),
        in_specs=[pl.BlockSpec((tl, LC), lambda c, t: (t, c)),
                  pl.BlockSpec((16, LC), lambda c, t: (jnp.maximum(t * hb - 1, 0), c)),
                  pl.BlockSpec((tl, LC), lambda c, t: (t, 2 + c)),
                  pl.BlockSpec((8, LC), lambda c, t: (0, c)),
                  pl.BlockSpec((2, 256, 256), lambda c, t: (c, 0, 0)),
                  pl.BlockSpec((2, 256, 256), lambda c, t: (c, 0, 0))],
        out_specs=[pl.BlockSpec((tl, LC), lambda c, t: (t, c)), pl.BlockSpec((tl, LC), lambda c, t: (t, c))],
        out_shape=[jax.ShapeDtypeStruct((S, D), F32), jax.ShapeDtypeStruct((S, D), BF16)],
        scratch_shapes=[pltpu.VMEM((tl + 8, LC), F32), pltpu.VMEM((tl, LC), F32), pltpu.VMEM((tl, LC), F32),
                        pltpu.VMEM((tl, LC), F32), pltpu.VMEM((1, LC), F32)],
        compiler_params=_cp("parallel", "arbitrary"),
    )(rest, rest, rest, pvec, wa, wx)


def lru_bwd(dh, h, rest, pvec, wa, wx, name):
    S = rest.shape[0]
    tl = min(LT, S)
    nt = S // tl

    def body(dh_ref, h_ref, hhalo_ref, xr_ref, xhalo_ref, pv_ref, wa_ref, wx_ref,
             dxr_ref, vacc_ref, dwa_ref, dwx_ref,
             xs, hs, a_s, ash_s, b_s, lam_s, dxe, anext, lnext, dxnext):
        ti = pl.program_id(1)
        tr = nt - 1 - ti

        @pl.when(ti == 0)
        def _():
            anext[...] = jnp.zeros_like(anext)
            lnext[...] = jnp.zeros_like(lnext)
            dxnext[...] = jnp.zeros_like(dxnext)
            vacc_ref[...] = jnp.zeros_like(vacc_ref)
            dwa_ref[...] = jnp.zeros_like(dwa_ref)
            dwx_ref[...] = jnp.zeros_like(dwx_ref)

        xs[0:8, :] = jnp.where(tr > 0, xhalo_ref[8:16, :].astype(F32), 0.0)
        xs[pl.ds(8, tl), :] = xr_ref[...].astype(F32)
        xc, xcb, r, ig, sp, a, mult, inv_mult = _lru_gates(xs, pv_ref, wa_ref, wx_ref, tl)

        a_s[pl.ds(0, tl), :] = a
        a_s[pl.ds(tl, 8), :] = jnp.broadcast_to(anext[...], (8, LC))
        ash_s[...] = a_s[pl.ds(1, tl), :]
        b_s[...] = dh_ref[...]
        row = lax.broadcasted_iota(jnp.int32, (8, LC), 0)

        def blk(k, c):
            o = pl.multiple_of((tl // 8 - 1 - k) * 8, 8)
            av = ash_s[pl.ds(o, 8), :]
            bv = b_s[pl.ds(o, 8), :]
            for d in (1, 2, 4):
                a_sh = pltpu.roll(av, 8 - d, 0)
                b_sh = pltpu.roll(bv, 8 - d, 0)
                m = row < 8 - d
                bv = jnp.where(m, bv + av * b_sh, bv)
                av = jnp.where(m, av * a_sh, av)
            lv = bv + av * c
            lam_s[pl.ds(o, 8), :] = lv
            return lv[0:1, :]

        lnext[...] = lax.fori_loop(0, tl // 8, blk, lnext[...])
        anext[...] = a[0:1, :]
        lam = lam_s[...]

        hs[0:8, :] = jnp.where(tr > 0, hhalo_ref[...], 0.0)
        hs[pl.ds(8, tl), :] = h_ref[...]
        d_a = lam * hs[pl.ds(7, tl), :]
        d_mult = lam * (ig * xc)
        d_ig = lam * mult * xc
        dxc = lam * mult * ig
        d_log_a = d_a * a - d_mult * (a * a) * inv_mult
        d_r = d_log_a * (-LRU_C * sp)
        vacc_ref[7:8, :] += _colsum(d_log_a * (-LRU_C * r)) * (-_sigmoid(-pv_ref[7:8, :]))
        d_pa = d_r * r * (1.0 - r)
        d_px = d_ig * ig * (1.0 - ig)
        vacc_ref[5:6, :] += _colsum(d_pa)
        vacc_ref[6:7, :] += _colsum(d_px)
        dpa = d_pa.astype(BF16)
        dpx = d_px.astype(BF16)
        back = []
        for g in range(2):
            sl = slice(256 * g, 256 * g + 256)
            dwa_ref[g] += _dot_tn(xcb[:, sl], dpa[:, sl])
            dwx_ref[g] += _dot_tn(xcb[:, sl], dpx[:, sl])
            back.append(_dot_nt(dpa[:, sl], wa_ref[g]) + _dot_nt(dpx[:, sl], wx_ref[g]))
        dxc = dxc + jnp.concatenate(back, axis=1)
        vacc_ref[4:5, :] += _colsum(dxc)
        for k in range(4):
            vacc_ref[k:k + 1, :] += _colsum(dxc * xs[pl.ds(5 + k, tl), :])
        dxe[pl.ds(0, tl), :] = dxc
        dxe[pl.ds(tl, 8), :] = dxnext[...]
        dxr = (pv_ref[3:4, :] * dxc + pv_ref[2:3, :] * dxe[pl.ds(1, tl), :]
               + pv_ref[1:2, :] * dxe[pl.ds(2, tl), :] + pv_ref[0:1, :] * dxe[pl.ds(3, tl), :])
        dxr_ref[...] = dxr.astype(BF16)
        dxnext[...] = dxc[0:8, :]

    hb = tl // 8
    rev = lambda t: nt - 1 - t
    halo = lambda t: jnp.maximum(rev(t) * hb - 1, 0)
    big = lambda: pltpu.VMEM((tl + 8, LC), F32)
    til = lambda: pltpu.VMEM((tl, LC), F32)
    return pl.pallas_call(
        body, name=name, grid=(2, nt),
        in_specs=[pl.BlockSpec((tl, LC), lambda c, t: (rev(t), c)),
                  pl.BlockSpec((tl, LC), lambda c, t: (rev(t), c)),
                  pl.BlockSpec((8, LC), lambda c, t: (halo(t), c)),
                  pl.BlockSpec((tl, LC), lambda c, t: (rev(t), c)),
                  pl.BlockSpec((16, LC), lambda c, t: (jnp.maximum(rev(t) * (tl // 16) - 1, 0), c)),
                  pl.BlockSpec((8, LC), lambda c, t: (0, c)),
                  pl.BlockSpec((2, 256, 256), lambda c, t: (c, 0, 0)),
                  pl.BlockSpec((2, 256, 256), lambda c, t: (c, 0, 0))],
        out_specs=[pl.BlockSpec((tl, LC), lambda c, t: (rev(t), c)),
                   pl.BlockSpec((8, LC), lambda c, t: (0, c)),
                   pl.BlockSpec((2, 256, 256), lambda c, t: (c, 0, 0)),
                   pl.BlockSpec((2, 256, 256), lambda c, t: (c, 0, 0))],
        out_shape=[jax.ShapeDtypeStruct((S, D), BF16), jax.ShapeDtypeStruct((8, D), F32),
                   jax.ShapeDtypeStruct((4, 256, 256), F32), jax.ShapeDtypeStruct((4, 256, 256), F32)],
        scratch_shapes=[big(), big(), big(), til(), til(), til(), big(),
                        pltpu.VMEM((1, LC), F32), pltpu.VMEM((1, LC), F32), pltpu.VMEM((8, LC), F32)],
        compiler_params=_cp("parallel", "arbitrary"),
    )(dh, h, h, rest, rest, pvec, wa, wx)


def mix_out_fwd(x, ao, hg, rest, vec, w_att_o, w_rec_o, w_out, name, tm=512):
    S = x.shape[0]
    tm = min(tm, S)

    def body(x_ref, ao_ref, hg_ref, ga_ref, gr_ref, vec_ref, wa_ref, wr_ref, wo_ref,
             xo_ref, att_ref, rec_ref, mg_ref, f_ref):
        att = _dot(ao_ref[...], wa_ref[...])
        rec = _dot(hg_ref[...], wr_ref[...])
        att_ref[...] = att.astype(BF16)
        rec_ref[...] = rec.astype(BF16)
        mg = (_sigmoid(ga_ref[...].astype(F32)) * att + _sigmoid(gr_ref[...].astype(F32)) * rec).astype(BF16)
        mg_ref[...] = mg
        f = _dot(mg, wo_ref[...])
        f_ref[...] = f.astype(BF16)
        y = f * lax.rsqrt(_mean(f * f) + EPS) * vec_ref[1:2, :]
        xo_ref[...] = x_ref[...] + (1.0 * vec_ref[4:5, :]) * y

    row = lambda i: (i, 0)
    full = lambda r: pl.BlockSpec((r, D), lambda i: (0, 0))
    return pl.pallas_call(
        body, name=name, grid=(S // tm,),
        in_specs=[pl.BlockSpec((tm, D), row), pl.BlockSpec((tm, 512), row), pl.BlockSpec((tm, D), row),
                  pl.BlockSpec((tm, D), lambda i: (i, 2)), pl.BlockSpec((tm, D), lambda i: (i, 3)),
                  full(8), full(512), full(D), full(D)],
        out_specs=[pl.BlockSpec((tm, D), row)] * 5,
        out_shape=[jax.ShapeDtypeStruct((S, D), F32)] + [jax.ShapeDtypeStruct((S, D), BF16)] * 4,
        compiler_params=_cp("parallel"),
    )(x, ao, hg, rest, rest, vec, w_att_o, w_rec_o, w_out)


def mix_out_bwd(dxo, f, att, rec, rest, h, vec, w_att_o, w_rec_o, w_out, name, tm=512):
    S = dxo.shape[0]
    tm = min(tm, S)

    def body(dxo_ref, f_ref, att_ref, rec_ref, yr_ref, ga_ref, gr_ref, h_ref, vec_ref, wa_ref, wr_ref, wo_ref,
             df_ref, da_ref, dr_ref, dao_ref, dh_ref, d3_ref, vacc_ref):
        @pl.when(pl.program_id(0) == 0)
        def _():
            vacc_ref[...] = jnp.zeros_like(vacc_ref)

        df = _post_norm_bwd(dxo_ref[...], f_ref[...].astype(F32), 1.0, vec_ref, vacc_ref).astype(BF16)
        df_ref[...] = df
        dm = _dot_nt(df, wo_ref[...])
        sa = _sigmoid(ga_ref[...].astype(F32))
        sr = _sigmoid(gr_ref[...].astype(F32))
        d_att = (dm * sa).astype(BF16)
        d_rec = (dm * sr).astype(BF16)
        da_ref[...] = d_att
        dr_ref[...] = d_rec
        d3_ref[1] = (dm * att_ref[...].astype(F32) * (sa * (1.0 - sa))).astype(BF16)
        d3_ref[2] = (dm * rec_ref[...].astype(F32) * (sr * (1.0 - sr))).astype(BF16)
        dao_ref[...] = _dot_nt(d_att, wa_ref[...]).astype(BF16)
        d_hg = _dot_nt(d_rec, wr_ref[...])
        yr = yr_ref[...].astype(F32)
        t = jnp.tanh(_GK * (yr + 0.044715 * yr * yr * yr))
        dh_ref[...] = d_hg * (0.5 * yr * (1.0 + t))
        gelu_grad = 0.5 * (1.0 + t) + 0.5 * yr * (1.0 - t * t) * _GK * (1.0 + 3.0 * 0.044715 * yr * yr)
        d3_ref[0] = (d_hg * h_ref[...] * gelu_grad).astype(BF16)

    row = lambda i: (i, 0)
    full = lambda r: pl.BlockSpec((r, D), lambda i: (0, 0))
    return pl.pallas_call(
        body, name=name, grid=(S // tm,),
        in_specs=[pl.BlockSpec((tm, D), row)] * 4
        + [pl.BlockSpec((tm, D), lambda i: (i, 1)), pl.BlockSpec((tm, D), lambda i: (i, 2)),
           pl.BlockSpec((tm, D), lambda i: (i, 3)), pl.BlockSpec((tm, D), row),
           full(8), full(512), full(D), full(D)],
        out_specs=[pl.BlockSpec((tm, D), row)] * 3
        + [pl.BlockSpec((tm, 512), row), pl.BlockSpec((tm, D), row),
           pl.BlockSpec((3, tm, D), lambda i: (0, i, 0)), pl.BlockSpec((8, D), lambda i: (0, 0))],
        out_shape=[jax.ShapeDtypeStruct((S, D), BF16)] * 3
        + [jax.ShapeDtypeStruct((S, 512), BF16), jax.ShapeDtypeStruct((S, D), F32),
           jax.ShapeDtypeStruct((3, S, D), BF16), jax.ShapeDtypeStruct((8, D), F32)],
        compiler_params=_cp("arbitrary"),
    )(dxo, f, att, rec, rest, rest, rest, h, vec, w_att_o, w_rec_o, w_out)


def dw_in(h, dq, dkv, dxr, d3, name, tk=1024, tn=512):
    S = h.shape[0]
    tk = min(tk, S)
    nk = S // tk

    def body(h_ref, dq_ref, dkv_ref, dxr_ref, d3_ref, o_ref, acc):
        j, k = pl.program_id(0), pl.program_id(1)

        @pl.when(k == 0)
        def _():
            acc[...] = jnp.zeros_like(acc)

        @pl.when(j == 0)
        def _():
            acc[...] += _dot_tn(h_ref[pl.ds(pl.multiple_of(k * tk, 16), tk), :], dq_ref[...])

        @pl.when((j >= 1) & (j < 3))
        def _():
            acc[...] += _dot_tn(h_ref[pl.ds(pl.multiple_of(k * tk, 16), tk), :], dkv_ref[...])

        @pl.when((j >= 3) & (j < 5))
        def _():
            acc[...] += _dot_tn(h_ref[pl.ds(pl.multiple_of(k * tk, 16), tk), :], dxr_ref[...])

        @pl.when(j >= 5)
        def _():
            acc[...] += _dot_tn(h_ref[pl.ds(pl.multiple_of(k * tk, 16), tk), :], d3_ref[...])

        @pl.when(k == nk - 1)
        def _():
            o_ref[...] = acc[...].astype(BF16)

    use = lambda j, k, lo, hi: jnp.where((j >= lo) & (j < hi), k, 0)
    g3 = lambda j: jnp.clip(j - 5, 0, 5)
    return pl.pallas_call(
        body, name=name, grid=(PW // tn, nk),
        in_specs=[pl.BlockSpec((S, D), lambda j, k: (0, 0), pipeline_mode=pl.Buffered(1)),
                  pl.BlockSpec((None, tk, tn), lambda j, k: (0, use(j, k, 0, 1), 0)),
                  pl.BlockSpec((None, tk, tn), lambda j, k: (jnp.clip(j - 1, 0, 1), use(j, k, 1, 3), 0)),
                  pl.BlockSpec((tk, tn), lambda j, k: (use(j, k, 3, 5), jnp.clip(j - 3, 0, 1))),
                  pl.BlockSpec((None, tk, tn), lambda j, k: (g3(j) // 2, use(j, k, 5, 11), g3(j) % 2))],
        out_specs=pl.BlockSpec((D, tn), lambda j, k: (0, j)),
        out_shape=jax.ShapeDtypeStruct((D, PW), BF16),
        scratch_shapes=[pltpu.VMEM((D, tn), F32)],
        compiler_params=_cp("parallel", "arbitrary"),
    )(h, dq, dkv, dxr, d3)


def ada_fwd(c_all, w_ada, b_ada, name, tn=768):
    n = w_ada.shape[1]

    def body(c_ref, w_ref, b_ref, o_ref):
        cv = c_ref[...]
        ca = (cv * _sigmoid(cv)).astype(BF16)
        o_ref[...] = _dot(ca, w_ref[...].astype(BF16)) + b_ref[...]

    return pl.pallas_call(
        body, name=name, grid=(n // tn,),
        in_specs=[pl.BlockSpec((8, D), lambda j: (0, 0)), pl.BlockSpec((D, tn), lambda j: (0, j)),
                  pl.BlockSpec((1, tn), lambda j: (0, j))],
        out_specs=pl.BlockSpec((8, tn), lambda j: (0, j)),
        out_shape=jax.ShapeDtypeStruct((8, n), F32),
        compiler_params=_cp("parallel"),
    )(c_all, w_ada, b_ada)


def ada_bwd(c_all_t, dmod, name, tn=768):
    n = dmod.shape[1]

    def body(c_ref, d_ref, o_ref):
        cv = c_ref[...]
        ca = (cv * _sigmoid(cv)).astype(BF16)
        o_ref[...] = _dot(ca, d_ref[...].astype(BF16))

    return pl.pallas_call(
        body, name=name, grid=(n // tn,),
        in_specs=[pl.BlockSpec((D, 128), lambda j: (0, 0)), pl.BlockSpec((128, tn), lambda j: (0, j))],
        out_specs=pl.BlockSpec((D, tn), lambda j: (0, j)),
        out_shape=jax.ShapeDtypeStruct((D, n), F32),
        compiler_params=_cp("parallel"),
    )(c_all_t, dmod)


def _row_tile(rows, cols, itemsize=4, budget=1536 * 1024):
    best = None
    for t in range(8, rows + 1, 8):
        if rows % t == 0 and t * cols * itemsize <= budget:
            best = t
    return rows if best is None else best


def sum_lead(parts, name, out_dtype=F32):
    n, R, C = parts.shape
    tr = _row_tile(R, C * n)

    def body(p_ref, o_ref):
        acc = p_ref[0].astype(F32)
        for k in range(1, n):
            acc = acc + p_ref[k].astype(F32)
        o_ref[...] = acc.astype(out_dtype)

    return pl.pallas_call(
        body, name=name, grid=(R // tr,),
        in_specs=[pl.BlockSpec((n, tr, C), lambda i: (0, i, 0))],
        out_specs=pl.BlockSpec((tr, C), lambda i: (i, 0)),
        out_shape=jax.ShapeDtypeStruct((R, C), out_dtype),
        compiler_params=_cp("parallel"),
    )(parts)


def adamw(w, g, m, v, name, emit_g=False):
    R, C = w.shape
    tr = _row_tile(R, C * 8, budget=16 * 1024 * 1024)

    def body(w_ref, g_ref, m_ref, v_ref, d_ref, mo_ref, vo_ref, *go_ref):
        gv = g_ref[...]
        if emit_g:
            go_ref[0][...] = gv
        mn = ADAM_B1 * m_ref[...] + (1.0 - ADAM_B1) * gv
        vn = ADAM_B2 * v_ref[...] + (1.0 - ADAM_B2) * (gv * gv)
        m_hat = mn / (1.0 - ADAM_B1 ** ADAM_STEP)
        v_hat = vn / (1.0 - ADAM_B2 ** ADAM_STEP)
        d_ref[...] = -ADAM_LR * (m_hat / (jnp.sqrt(v_hat) + ADAM_EPS) + ADAM_WD * w_ref[...])
        mo_ref[...] = mn
        vo_ref[...] = vn

    spec = pl.BlockSpec((tr, C), lambda i: (i, 0))
    return pl.pallas_call(
        body, name=name, grid=(R // tr,),
        in_specs=[spec] * 4, out_specs=[spec] * (4 if emit_g else 3),
        out_shape=[jax.ShapeDtypeStruct((R, C), F32)] * (4 if emit_g else 3),
        compiler_params=_cp("parallel"),
    )(w, g, m, v)


def _mesh_pos():
    return lax.axis_index("x"), lax.axis_index("y"), lax.axis_index("c")


def _other_chips(mx, my):
    return [(1 - mx, my), (mx, 1 - my), (1 - mx, 1 - my)]


def ag_small(x, name):
    R = x.shape[0]

    def body(x_ref, out_ref, send_sems, recv_sems, local_sem):
        mx, my, mc = _mesh_pos()
        me, sibling = (mx, my, mc), (mx, my, 1 - mc)
        chips = _other_chips(mx, my)

        def slot(px, py, pc):
            return out_ref.at[4 * px + 2 * py + pc]

        def copy(k, block, to, src=None):
            return pltpu.make_async_remote_copy(
                src_ref=slot(*block) if src is None else src, dst_ref=slot(*block),
                send_sem=send_sems.at[k], recv_sem=recv_sems.at[k], device_id=to, device_id_type=MESH)

        mine = pltpu.make_async_copy(x_ref, slot(*me), local_sem)
        mine.start()
        first = [copy(0, me, sibling, src=x_ref)]
        first += [copy(1 + j, me, (*chip, mc), src=x_ref) for j, chip in enumerate(chips)]
        for cp in first:
            cp.start()
        passed = [copy(4 + j, (*chip, mc), sibling) for j, chip in enumerate(chips)]
        for j, chip in enumerate(chips):
            copy(1 + j, (*chip, mc), me).wait_recv()
            passed[j].start()
        copy(0, sibling, me).wait_recv()
        for j, chip in enumerate(chips):
            copy(4 + j, (*chip, 1 - mc), me).wait_recv()
        for cp in first + passed:
            cp.wait_send()
        mine.wait()

    return pl.pallas_call(
        body, name=name,
        out_shape=jax.ShapeDtypeStruct((N_DEV, R, 128), F32),
        in_specs=[pl.BlockSpec(memory_space=pltpu.VMEM)],
        out_specs=pl.BlockSpec(memory_space=pltpu.VMEM),
        scratch_shapes=[pltpu.SemaphoreType.DMA((7,)), pltpu.SemaphoreType.DMA((7,)), pltpu.SemaphoreType.DMA],
        compiler_params=pltpu.CompilerParams(vmem_limit_bytes=VMEM_LIMIT),
    )(x)


BIG = (("ffn1_w_gu", "col", D, PW), ("ffn1_w_down", "row", FF, D), ("w_in", "col", D, PW),
       ("w_att_o", "col", 512, D), ("w_rec_o", "row", D, D), ("w_out", "row", D, D),
       ("ffn2_w_gu", "col", D, PW), ("ffn2_w_down", "row", FF, D))
NBIG = len(BIG)


def _shard_shape(kind, R, C):
    return (R, C // 4) if kind == "col" else (R // 4, C)


def _region(ref, kind, R, C, q, half, t, tr):
    sr, sc = _shard_shape(kind, R, C)
    if kind == "col":
        return ref.at[pl.ds(pl.multiple_of(half * (R // 2) + t * tr, 16), tr), pl.ds(q * sc, sc)]
    return ref.at[pl.ds(pl.multiple_of(q * sr + t * tr, 16), tr), pl.ds(half * (C // 2), C // 2)]


def ag_local(w, kind, R, C, p_arr, name, after=()):
    sr, sc = _shard_shape(kind, R, C)
    tr = _row_tile(sr, sc, budget=2 * 1024 * 1024)
    nt = sr // tr
    after = list(after)

    def body(p_ref, w_ref, *rest):
        rest[-1][...] = w_ref[...].astype(BF16)

    if kind == "col":
        o_spec = pl.BlockSpec((tr, sc), lambda i, p: (i, p[0]))
    else:
        o_spec = pl.BlockSpec((tr, sc), lambda i, p: (p[0] * nt + i, 0))
    return pl.pallas_call(
        body, name=name,
        grid_spec=pltpu.PrefetchScalarGridSpec(
            num_scalar_prefetch=1, grid=(nt,),
            in_specs=[pl.BlockSpec((tr, sc), lambda i, p: (i, 0))] + [ANY] * len(after), out_specs=o_spec),
        out_shape=jax.ShapeDtypeStruct((R, C), BF16),
        compiler_params=_cp("parallel"),
    )(p_arr, w, *after)


HBM_SPEC = pl.BlockSpec(memory_space=pltpu.HBM)
SEM_SPEC = pl.BlockSpec(memory_space=pltpu.SEMAPHORE)


def _ag_sems(geoms):
    return sum(6 if both else 3 for (_, _, _, both) in geoms)


def _ag_copies(fulls, geoms, ssem, rsem, mx, my, mc, q, h):
    chips = _other_chips(mx, my)
    out, base = [], 0
    for w, (kind, R, C, both) in enumerate(geoms):
        sr, sc = _shard_shape(kind, R, C)
        hr = sr // 2 if kind == "col" else sr
        reg = _region(fulls[w], kind, R, C, q, h, 0, hr)
        out.append([pltpu.make_async_remote_copy(
            src_ref=reg, dst_ref=reg, send_sem=ssem.at[base + 3 * t + k], recv_sem=rsem.at[base + 3 * t + k],
            device_id=(*chips[k], mc if t == 0 else 1 - mc), device_id_type=MESH)
            for t in range(2 if both else 1) for k in range(3)])
        base += 6 if both else 3
    return out


def ag_start(fulls, geoms, after, name):
    n = len(fulls)
    after = list(after)
    m = len(after)

    def body(*refs):
        ssem, rsem = refs[n + m:n + m + 2]
        outs, token = refs[n + m + 2:2 * n + m + 2], refs[2 * n + m + 2]
        mx, my, mc = _mesh_pos()
        p = 2 * mx + my
        col = [w for w, g in enumerate(geoms) if g[0] == "col"]
        row = [w for w, g in enumerate(geoms) if g[0] == "row"]
        for q in range(4):
            @pl.when(p == q)
            def _(q=q):
                cps = _ag_copies(outs, geoms, ssem, rsem, mx, my, mc, q, mc)
                for w in col:
                    for cp in cps[w]:
                        cp.start()
        for h in range(2):
            @pl.when(mc == h)
            def _(h=h):
                cps = _ag_copies(outs, geoms, ssem, rsem, mx, my, mc, p, h)
                for w in row:
                    for cp in cps[w]:
                        cp.start()
        token[...] = jnp.zeros_like(token)

    res = pl.pallas_call(
        body, name=name,
        out_shape=[pltpu.SemaphoreType.DMA((_ag_sems(geoms),)), pltpu.SemaphoreType.DMA((_ag_sems(geoms),))]
        + [pltpu.HBM(a.shape, a.dtype) for a in fulls] + [jax.ShapeDtypeStruct((8, 128), F32)],
        in_specs=[HBM_SPEC] * n + [ANY] * m,
        out_specs=[SEM_SPEC, SEM_SPEC] + [HBM_SPEC] * n + [pl.BlockSpec(memory_space=pltpu.VMEM)],
        input_output_aliases={w: 2 + w for w in range(n)},
        compiler_params=pltpu.CompilerParams(has_side_effects=pltpu.SideEffectType.DATAFLOW_SIDE_EFFECTING),
    )(*[pltpu.with_memory_space_constraint(a, pltpu.HBM) for a in fulls], *after)
    return res[0], res[1], list(res[2:2 + n]), res[2 + n]


def ag_wait(fulls, geoms, ssem, rsem, after, name):
    n = len(fulls)
    after = list(after) if isinstance(after, (list, tuple)) else [after]

    def body(*refs):
        ins, ssem_ref, rsem_ref = refs[:n], refs[n], refs[n + 1]
        mx, my, mc = _mesh_pos()
        for cps in _ag_copies(ins, geoms, ssem_ref, rsem_ref, mx, my, mc, 0, 0):
            for cp in cps:
                cp.wait_send()
                cp.wait_recv()

    return list(pl.pallas_call(
        body, name=name,
        out_shape=[pltpu.HBM(a.shape, a.dtype) for a in fulls],
        in_specs=[HBM_SPEC] * n + [SEM_SPEC, SEM_SPEC] + [ANY] * len(after),
        out_specs=[HBM_SPEC] * n,
        input_output_aliases={w: w for w in range(n)},
        compiler_params=pltpu.CompilerParams(has_side_effects=pltpu.SideEffectType.DATAFLOW_SIDE_EFFECTING),
    )(*fulls, ssem, rsem, *after))


def ag_forward(full, kind, R, C, name):
    sr, sc = _shard_shape(kind, R, C)
    hr, hc = (sr // 2, sc) if kind == "col" else (sr, sc // 2)
    tr = _row_tile(hr, hc, itemsize=2, budget=2 * 1024 * 1024)
    nt = hr // tr
    total = 3 * nt

    def body(src_ref, full_ref, stage, lsem, ssem, rsem):
        step = pl.program_id(0) * nt + pl.program_id(1)
        par = step % 2
        mx, my, mc = _mesh_pos()

        def load(s, q, h, t):
            return pltpu.make_async_copy(_region(src_ref, kind, R, C, q, h, t, tr), stage.at[s], lsem.at[s])

        def push(s, q, h, t):
            return pltpu.make_async_remote_copy(src_ref=stage.at[s], dst_ref=_region(full_ref, kind, R, C, q, h, t, tr),
                                                send_sem=ssem.at[s], recv_sem=rsem, device_id=(mx, my, 1 - mc),
                                                device_id_type=MESH)

        def for_tile(stp, fn):
            q_k = _partner_chip(stp // nt, 2 * mx + my)
            if kind == "col":
                for q in range(4):
                    @pl.when(q_k == q)
                    def _(q=q):
                        fn(q, mc, stp % nt)
            else:
                for h in range(2):
                    @pl.when(mc == h)
                    def _(h=h):
                        fn(q_k, h, stp % nt)

        @pl.when(step == 0)
        def _():
            for_tile(step, lambda q, h, t: load(0, q, h, t).start())

        load(par, 0, 0, 0).wait()
        for_tile(step, lambda q, h, t: push(par, q, h, t).start())

        @pl.when(step + 1 < total)
        def _():
            @pl.when(step >= 1)
            def _():
                push(1 - par, 0, 0, 0).wait_send()
            for_tile(step + 1, lambda q, h, t: load(1 - par, q, h, t).start())

        @pl.when(step == total - 1)
        def _():
            push(par, 0, 0, 0).wait_send()
            push(1 - par, 0, 0, 0).wait_send()
            three = full_ref.at[pl.ds(0, hr), pl.ds(0, 3 * hc)] if kind == "col" else full_ref.at[pl.ds(0, 3 * hr), pl.ds(0, hc)]
            pltpu.make_async_remote_copy(src_ref=three, dst_ref=three, send_sem=ssem.at[0], recv_sem=rsem,
                                         device_id=(mx, my, 1 - mc), device_id_type=MESH).wait_recv()

    return pl.pallas_call(
        body, name=name, grid=(3, nt),
        in_specs=[ANY], out_specs=ANY,
        out_shape=jax.ShapeDtypeStruct((R, C), BF16),
        scratch_shapes=[pltpu.VMEM((2, tr, hc), BF16), pltpu.SemaphoreType.DMA((2,)), pltpu.SemaphoreType.DMA((2,)),
                        pltpu.SemaphoreType.DMA],
        input_output_aliases={0: 0},
        compiler_params=_cp("arbitrary", "arbitrary"),
    )(full)


def _half_shape(kind, R, C):
    return (R // 2, C) if kind == "col" else (R, C // 2)


def _piece_shape(kind, R, C):
    return (R // 2, C // 4) if kind == "col" else (R // 4, C // 2)


def pair_push(g, kind, c_arr, name):
    R, C = g.shape
    hr, hc = _half_shape(kind, R, C)
    tr = _row_tile(hr, hc, itemsize=2, budget=2 * 1024 * 1024)
    nt = hr // tr

    def body(c_ref, g_ref, out_ref, stage, ssem, rsem):
        i = pl.program_id(0)
        slot = i % 2
        mx, my, mc = _mesh_pos()

        def push(s, t):
            return pltpu.make_async_remote_copy(
                src_ref=stage.at[s], dst_ref=out_ref.at[pl.ds(pl.multiple_of(t * tr, 16), tr)],
                send_sem=ssem.at[s], recv_sem=rsem, device_id=(mx, my, 1 - mc), device_id_type=MESH)

        @pl.when(i >= 2)
        def _():
            push(slot, 0).wait_send()

        stage[slot] = g_ref[...]
        push(slot, i).start()

        @pl.when(i == nt - 1)
        def _():
            push(slot, 0).wait_send()
            if nt >= 2:
                push(1 - slot, 0).wait_send()
            pltpu.make_async_remote_copy(src_ref=out_ref, dst_ref=out_ref, send_sem=ssem.at[0], recv_sem=rsem,
                                         device_id=(mx, my, 1 - mc), device_id_type=MESH).wait_recv()

    if kind == "col":
        g_spec = pl.BlockSpec((tr, hc), lambda i, c: ((1 - c[0]) * nt + i, 0))
    else:
        g_spec = pl.BlockSpec((tr, hc), lambda i, c: (i, 1 - c[0]))
    return pl.pallas_call(
        body, name=name,
        grid_spec=pltpu.PrefetchScalarGridSpec(
            num_scalar_prefetch=1, grid=(nt,), in_specs=[g_spec], out_specs=ANY,
            scratch_shapes=[pltpu.VMEM((2, tr, hc), BF16), pltpu.SemaphoreType.DMA((2,)), pltpu.SemaphoreType.DMA]),
        out_shape=jax.ShapeDtypeStruct((hr, hc), BF16),
        compiler_params=_cp("arbitrary"),
    )(c_arr, g)


def _partner_chip(k, p):
    return p ^ jnp.where(k == 0, 2, jnp.where(k == 1, 1, jnp.where(k == 2, 3, 0)))


def pair_add(g, got, kind, cp_arr, name):
    R, C = g.shape
    pr, pc = _piece_shape(kind, R, C)
    tr = _row_tile(pr, pc, itemsize=2, budget=2 * 1024 * 1024)
    nt = pr // tr

    def body(cp_ref, g_ref, got_ref, ps_ref, rb_ref):
        tile = (g_ref[...].astype(F32) + got_ref[...].astype(F32)).astype(BF16)
        ps_ref[...] = tile

        @pl.when(pl.program_id(1) == cp_ref[1])
        def _():
            rb_ref[...] = tile

    if kind == "col":
        g_spec = pl.BlockSpec((tr, pc), lambda i, q, cp: (cp[0] * nt + i, q))
        got_spec = pl.BlockSpec((tr, pc), lambda i, q, cp: (i, q))
    else:
        g_spec = pl.BlockSpec((tr, pc), lambda i, q, cp: (q * nt + i, cp[0]))
        got_spec = pl.BlockSpec((tr, pc), lambda i, q, cp: (q * nt + i, 0))
    return pl.pallas_call(
        body, name=name,
        grid_spec=pltpu.PrefetchScalarGridSpec(
            num_scalar_prefetch=1, grid=(nt, 4), in_specs=[g_spec, got_spec],
            out_specs=[pl.BlockSpec((None, tr, pc), lambda i, q, cp: (q, i, 0)),
                       pl.BlockSpec((None, tr, pc), lambda i, q, cp: (cp[1], i, 0))]),
        out_shape=[jax.ShapeDtypeStruct((4, pr, pc), BF16)] * 2,
        compiler_params=_cp("arbitrary", "arbitrary"),
    )(cp_arr, g, got)


def _rs_copies(ps, rb, ssem, rsem, mx, my, mc):
    p = 2 * mx + my
    out = []
    for w in range(len(ps)):
        for k, chip in enumerate(_other_chips(mx, my)):
            out.append(pltpu.make_async_remote_copy(
                src_ref=ps[w].at[2 * chip[0] + chip[1]], dst_ref=rb[w].at[p], send_sem=ssem.at[3 * w + k],
                recv_sem=rsem.at[3 * w + k], device_id=(*chip, mc), device_id_type=MESH))
    return out


def rs_start(ps, rb, after, name):
    n = len(ps)
    after = list(after)
    m = len(after)

    def body(*refs):
        ssem, rsem = refs[2 * n + m:2 * n + m + 2]
        ps_o = refs[2 * n + m + 2:3 * n + m + 2]
        rb_o = refs[3 * n + m + 2:4 * n + m + 2]
        token = refs[4 * n + m + 2]
        for cp in _rs_copies(ps_o, rb_o, ssem, rsem, *_mesh_pos()):
            cp.start()
        token[...] = jnp.zeros_like(token)

    both = list(ps) + list(rb)
    res = pl.pallas_call(
        body, name=name,
        out_shape=[pltpu.SemaphoreType.DMA((3 * n,)), pltpu.SemaphoreType.DMA((3 * n,))]
        + [pltpu.HBM(a.shape, a.dtype) for a in both] + [jax.ShapeDtypeStruct((8, 128), F32)],
        in_specs=[HBM_SPEC] * (2 * n) + [ANY] * m,
        out_specs=[SEM_SPEC, SEM_SPEC] + [HBM_SPEC] * (2 * n) + [pl.BlockSpec(memory_space=pltpu.VMEM)],
        input_output_aliases={w: 2 + w for w in range(2 * n)},
        compiler_params=pltpu.CompilerParams(has_side_effects=pltpu.SideEffectType.DATAFLOW_SIDE_EFFECTING),
    )(*[pltpu.with_memory_space_constraint(a, pltpu.HBM) for a in both], *after)
    return res[0], res[1], list(res[2:2 + n]), list(res[2 + n:2 + 2 * n]), res[2 + 2 * n]


def rs_wait(ps, rb, ssem, rsem, after, name):
    n = len(ps)
    after = list(after)
    m = len(after)

    def body(*refs):
        ps_i, rb_i = refs[:n], refs[n:2 * n]
        ssem_ref, rsem_ref = refs[2 * n], refs[2 * n + 1]
        for cp in _rs_copies(ps_i, rb_i, ssem_ref, rsem_ref, *_mesh_pos()):
            cp.wait_send()
            cp.wait_recv()

    both = list(ps) + list(rb)
    res = pl.pallas_call(
        body, name=name,
        out_shape=[pltpu.HBM(a.shape, a.dtype) for a in both],
        in_specs=[HBM_SPEC] * (2 * n) + [SEM_SPEC, SEM_SPEC] + [ANY] * m,
        out_specs=[HBM_SPEC] * (2 * n),
        input_output_aliases={w: w for w in range(2 * n)},
        compiler_params=pltpu.CompilerParams(has_side_effects=pltpu.SideEffectType.DATAFLOW_SIDE_EFFECTING),
    )(*both, ssem, rsem, *after)
    return list(res[n:])


def _slot_copies(blk, ssem, rsem):
    mx, my, mc = _mesh_pos()
    mine = blk.at[4 * mx + 2 * my + mc]
    peers = [(mx, my, 1 - mc)] + [(*chip, mc) for chip in _other_chips(mx, my)] \
        + [(*chip, 1 - mc) for chip in _other_chips(mx, my)]
    return [pltpu.make_async_remote_copy(src_ref=mine, dst_ref=mine, send_sem=ssem.at[k], recv_sem=rsem.at[k],
                                         device_id=peer, device_id_type=MESH) for k, peer in enumerate(peers)]


def slot_start(blk, name):
    def body(_, ssem, rsem, out, token):
        for cp in _slot_copies(out, ssem, rsem):
            cp.start()
        token[...] = jnp.zeros_like(token)

    return pl.pallas_call(
        body, name=name,
        out_shape=[pltpu.SemaphoreType.DMA((7,)), pltpu.SemaphoreType.DMA((7,)), pltpu.HBM(blk.shape, blk.dtype),
                   jax.ShapeDtypeStruct((8, 128), F32)],
        in_specs=[HBM_SPEC],
        out_specs=[SEM_SPEC, SEM_SPEC, HBM_SPEC, pl.BlockSpec(memory_space=pltpu.VMEM)],
        input_output_aliases={0: 2},
        compiler_params=pltpu.CompilerParams(has_side_effects=pltpu.SideEffectType.DATAFLOW_SIDE_EFFECTING),
    )(pltpu.with_memory_space_constraint(blk, pltpu.HBM))


def slot_wait(blk, ssem, rsem, after, name):
    after = list(after)

    def body(blk_ref, ssem_ref, rsem_ref, *_):
        for cp in _slot_copies(blk_ref, ssem_ref, rsem_ref):
            cp.wait_send()
            cp.wait_recv()

    return pl.pallas_call(
        body, name=name,
        out_shape=pltpu.HBM(blk.shape, blk.dtype),
        in_specs=[HBM_SPEC, SEM_SPEC, SEM_SPEC] + [ANY] * len(after),
        out_specs=HBM_SPEC,
        input_output_aliases={0: 0},
        compiler_params=pltpu.CompilerParams(has_side_effects=pltpu.SideEffectType.DATAFLOW_SIDE_EFFECTING),
    )(blk, ssem, rsem, *after)


def sum_share(parts, kind, R, C, name):
    _, pr, pc = parts.shape
    sr, sc = _shard_shape(kind, R, C)
    tr = _row_tile(pr, pc * 4, budget=8 * 1024 * 1024)
    nt = pr // tr

    def body(p_ref, fin_ref, stage, lsem, ssem, rsem):
        i = pl.program_id(0)
        slot = i % 2
        mx, my, mc = _mesh_pos()

        def region(h, t):
            r0 = pl.multiple_of(t * tr, 8)
            if kind == "col":
                return fin_ref.at[pl.ds(pl.multiple_of(h * pr + r0, 8), tr)]
            return fin_ref.at[pl.ds(r0, tr), pl.ds(h * pc, pc)]

        def copies(s, h, t):
            return (pltpu.make_async_copy(stage.at[s], region(h, t), lsem.at[s]),
                    pltpu.make_async_remote_copy(src_ref=stage.at[s], dst_ref=region(h, t), send_sem=ssem.at[s],
                                                 recv_sem=rsem, device_id=(mx, my, 1 - mc), device_id_type=MESH))

        def wait_sent(s):
            loc, rem = copies(s, 0, 0)
            loc.wait()
            rem.wait_send()

        @pl.when(i >= 2)
        def _():
            wait_sent(slot)

        acc = p_ref[0].astype(F32)
        for k in range(1, 4):
            acc = acc + p_ref[k].astype(F32)
        stage[slot] = acc
        if kind == "col":
            for cp in copies(slot, mc, i):
                cp.start()
        else:
            for h in range(2):
                @pl.when(mc == h)
                def _(h=h):
                    for cp in copies(slot, h, i):
                        cp.start()

        @pl.when(i == nt - 1)
        def _():
            wait_sent(slot)
            if nt >= 2:
                wait_sent(1 - slot)
            half = fin_ref.at[pl.ds(0, pr), pl.ds(0, pc)]
            pltpu.make_async_remote_copy(src_ref=half, dst_ref=half, send_sem=ssem.at[0], recv_sem=rsem,
                                         device_id=(mx, my, 1 - mc), device_id_type=MESH).wait_recv()

    return pl.pallas_call(
        body, name=name, grid=(nt,),
        in_specs=[pl.BlockSpec((4, tr, pc), lambda i: (0, i, 0))],
        out_specs=ANY,
        out_shape=jax.ShapeDtypeStruct((sr, sc), F32),
        scratch_shapes=[pltpu.VMEM((2, tr, pc), F32), pltpu.SemaphoreType.DMA((2,)), pltpu.SemaphoreType.DMA((2,)),
                        pltpu.SemaphoreType.DMA],
        compiler_params=_cp("arbitrary"),
    )(parts)


def _pack(parts, rows):
    flat = []
    for a in parts:
        a = jnp.ravel(a).astype(F32)
        flat.append(jnp.pad(a, (0, (-a.shape[0]) % 128)))
    v = jnp.concatenate(flat)
    return jnp.pad(v, (0, rows * 128 - v.shape[0])).reshape(rows, 128)


def _unpack(block, shapes):
    lead = block.shape[:-2]
    v = block.reshape(lead + (-1,))
    out, off = [], 0
    for shp in shapes:
        n = int(np.prod(shp))
        out.append(v[..., off:off + n].reshape(lead + tuple(shp)))
        off += n + (-n) % 128
    return out


def _block_diag4(w):
    w4 = w.reshape(4, 4, 64, 64)
    eye = jnp.eye(4, dtype=w.dtype)
    return (w4[:, :, :, None, :] * eye[None, :, None, :, None]).reshape(4, 256, 256)


def _diag_blocks(bd):
    b5 = bd.reshape(4, 4, 64, 4, 64)
    return jnp.stack([b5[:, i, :, i, :] for i in range(4)], axis=1).reshape(16, 64, 64)


def _bias_window(rel_bias):
    m = (np.arange(768) + 127) % 768 - 127
    w = rel_bias[:, np.clip(512 - m, -128, 128) + 128]
    win = jnp.tile(w, (1, 128))[:, :128 * 767].reshape(8, 128, 767)[:, :, :WIN]
    qh = np.arange(128)[:, None] // CHUNK
    kc = np.arange(WIN)[None, :] // CHUNK
    valid = (kc >= qh) & (kc <= qh + 8)
    return jnp.where(jnp.asarray(valid)[None], win, NEG)


SMALL = ("b_ada", "norm_pre", "norm_post", "rel_bias", "conv_w", "conv_b", "lru_wa", "lru_ba", "lru_wx",
         "lru_bx", "lru_lambda")
WEIGHTS = ("w_ada", "b_ada", "norm_pre", "norm_post", "ffn1_w_gu", "ffn1_w_down", "w_in", "rel_bias", "conv_w",
           "conv_b", "lru_wa", "lru_ba", "lru_wx", "lru_bx", "lru_lambda", "w_att_o", "w_rec_o", "w_out",
           "ffn2_w_gu", "ffn2_w_down")


def kernel(x, c, w_ada, b_ada, norm_pre, norm_post, ffn1_w_gu, ffn1_w_down, w_in, rel_bias, conv_w, conv_b, lru_wa, lru_ba, lru_wx, lru_bx, lru_lambda, w_att_o, w_rec_o, w_out, ffn2_w_gu, ffn2_w_down, loss_target, m_w_ada, m_b_ada, m_norm_pre, m_norm_post, m_ffn1_w_gu, m_ffn1_w_down, m_w_in, m_rel_bias, m_conv_w, m_conv_b, m_lru_wa, m_lru_ba, m_lru_wx, m_lru_bx, m_lru_lambda, m_w_att_o, m_w_rec_o, m_w_out, m_ffn2_w_gu, m_ffn2_w_down, v_w_ada, v_b_ada, v_norm_pre, v_norm_post, v_ffn1_w_gu, v_ffn1_w_down, v_w_in, v_rel_bias, v_conv_w, v_conv_b, v_lru_wa, v_lru_ba, v_lru_wx, v_lru_bx, v_lru_lambda, v_w_att_o, v_w_rec_o, v_w_out, v_ffn2_w_gu, v_ffn2_w_down):
    W = dict(w_ada=w_ada, b_ada=b_ada, norm_pre=norm_pre, norm_post=norm_post, ffn1_w_gu=ffn1_w_gu,
             ffn1_w_down=ffn1_w_down, w_in=w_in, rel_bias=rel_bias, conv_w=conv_w, conv_b=conv_b, lru_wa=lru_wa,
             lru_ba=lru_ba, lru_wx=lru_wx, lru_bx=lru_bx, lru_lambda=lru_lambda, w_att_o=w_att_o, w_rec_o=w_rec_o,
             w_out=w_out, ffn2_w_gu=ffn2_w_gu, ffn2_w_down=ffn2_w_down)
    M = dict(w_ada=m_w_ada, b_ada=m_b_ada, norm_pre=m_norm_pre, norm_post=m_norm_post, ffn1_w_gu=m_ffn1_w_gu,
             ffn1_w_down=m_ffn1_w_down, w_in=m_w_in, rel_bias=m_rel_bias, conv_w=m_conv_w, conv_b=m_conv_b,
             lru_wa=m_lru_wa, lru_ba=m_lru_ba, lru_wx=m_lru_wx, lru_bx=m_lru_bx, lru_lambda=m_lru_lambda,
             w_att_o=m_w_att_o, w_rec_o=m_w_rec_o, w_out=m_w_out, ffn2_w_gu=m_ffn2_w_gu, ffn2_w_down=m_ffn2_w_down)
    V = dict(w_ada=v_w_ada, b_ada=v_b_ada, norm_pre=v_norm_pre, norm_post=v_norm_post, ffn1_w_gu=v_ffn1_w_gu,
             ffn1_w_down=v_ffn1_w_down, w_in=v_w_in, rel_bias=v_rel_bias, conv_w=v_conv_w, conv_b=v_conv_b,
             lru_wa=v_lru_wa, lru_ba=v_lru_ba, lru_wx=v_lru_wx, lru_bx=v_lru_bx, lru_lambda=v_lru_lambda,
             w_att_o=v_w_att_o, w_rec_o=v_w_rec_o, w_out=v_w_out, ffn2_w_gu=v_ffn2_w_gu, ffn2_w_down=v_ffn2_w_down)
    mx, my, mc = _mesh_pos()
    p = 2 * mx + my
    e = 4 * mx + 2 * my + mc
    xs = x[0]

    c_arr = jnp.reshape(mc, (1,)).astype(jnp.int32)
    cp_arr = jnp.stack([mc, p]).astype(jnp.int32)
    p_arr = jnp.reshape(p, (1,)).astype(jnp.int32)
    direct = ("w_att_o", "w_rec_o", "w_out", "ffn2_w_gu", "ffn2_w_down")
    geoms = [(kind, R, C, n in direct) for (n, kind, R, C) in BIG]
    names = [b[0] for b in BIG]
    placed = [ag_local(W[n][0], kind, R, C, p_arr, "ag_local_" + n) for (n, kind, R, C) in BIG[:2]]

    def arrived(fly, lo, hi, ssem, rsem, after, tag):
        done = ag_wait(fly, geoms[lo:hi], ssem, rsem, after, "ag_wait_" + tag)
        return [a if both else ag_forward(a, kind, R, C, "ag_forward_" + n)
                for a, (kind, R, C, both), n in zip(done, geoms[lo:hi], names[lo:hi])]

    g1 = ag_small(_pack([c, norm_pre, norm_post, conv_w], 32), "ag_small_params")
    c_all, npre4, npost4, cw4 = _unpack(g1, [(D,), (3, 256), (3, 256), (4, 256)])
    chipwise = lambda a: jnp.moveaxis(a[0::2], 0, 1).reshape(a.shape[1], D)
    npre, npost, conv_full = chipwise(npre4), chipwise(npost4), chipwise(cw4)

    b_cols = lax.dynamic_slice(b_ada, (0, p * 2304), (1, 2304))
    mod_cols = ada_fwd(c_all, w_ada[0], b_cols, "ada_fwd")
    g2 = ag_small(mod_cols.reshape(144, 128), "ag_mod")
    mod_all = jnp.moveaxis(g2[0::2].reshape(4, 8, 2304), 0, 1).reshape(8, 9 * D)
    mod = lax.dynamic_index_in_dim(mod_all, e, 0, keepdims=False).reshape(3, 3, D)
    zeros3 = jnp.zeros((3, D), F32)
    vecs = [jnp.concatenate([npre[k:k + 1], npost[k:k + 1], mod[k], zeros3], axis=0) for k in range(3)]

    gu_s, gu_r, gu_fly, tok_gu = ag_start(placed[:1], geoms[:1], [g2], "ag_start_ffn1_gu")
    dn_s, dn_r, dn_fly, tok0 = ag_start(placed[1:2], geoms[1:2], [tok_gu], "ag_start_ffn1_down")
    placed += [ag_local(W[n][0], kind, R, C, p_arr, "ag_local_" + n, after=[tok0]) for (n, kind, R, C) in BIG[2:]]
    f1_gu, = arrived(gu_fly, 0, 1, gu_s, gu_r, placed[2:], "ffn1_gu")
    f1_dn, = arrived(dn_fly, 1, 2, dn_s, dn_r, f1_gu, "ffn1_down")
    mix_s, mix_r, mix_fly, tok1 = ag_start(placed[2:6], geoms[2:6], [f1_gu, f1_dn], "ag_start_mixer")
    ffn_s, ffn_r, ffn_fly, tok2 = ag_start(placed[6:], geoms[6:], [tok1], "ag_start_ffn2")
    wa_bd = _block_diag4(lru_wa[0]).astype(BF16)
    wx_bd = _block_diag4(lru_wx[0]).astype(BF16)
    pvec = jnp.concatenate([conv_full, conv_b, lru_ba, lru_bx, lru_lambda], axis=0)
    bias = _bias_window(rel_bias[0]).reshape(4, 256, WIN)

    x1, h1, g1_, u1, a1, f1 = ffn_fwd(xs, vecs[0] + tok2[0:1, 0:1], f1_gu, f1_dn, 0.5, "ffn1_fwd")
    win, wao, wro, wout = arrived(mix_fly, 2, 6, mix_s, mix_r, x1, "mixer")
    h2, qkv, rest = proj_fwd(x1, vecs[1], win, "proj_fwd")
    ao = attn_fwd(qkv, bias, "attn_fwd")
    hl, hg = lru_fwd(rest, pvec, wa_bd, wx_bd, "lru_fwd")
    x2, att, rec, mg, f2 = mix_out_fwd(x1, ao, hg, rest, vecs[1], wao, wro, wout, "mix_out_fwd")
    f2_gu, f2_dn = arrived(ffn_fly, 6, 8, ffn_s, ffn_r, x2, "ffn2")
    dy, h3, g3_, u3, a3, f3, lvec = ffn_fwd(x2, vecs[2], f2_gu, f2_dn, 0.5, "ffn2_fwd", tgt=loss_target[0])

    G, grads = {}, {}
    geo = {n: (kind, R, C) for (n, kind, R, C) in BIG}

    def reduce_begin(names, tag):
        ps, rb = [], []
        for n in names:
            got = pair_push(G[n], geo[n][0], c_arr, "rs_push_" + n)
            a, b = pair_add(G[n], got, geo[n][0], cp_arr, "rs_pair_sum_" + n)
            ps.append(a)
            rb.append(b)
        return rs_start(ps, rb, [], "rs_start_" + tag)

    def reduce_end(names, flight, after, tag):
        ssem, rsem, ps, rb, _ = flight
        for a, n in zip(rs_wait(ps, rb, ssem, rsem, after, "rs_wait_" + tag), names):
            grads[n] = sum_share(a, *geo[n], "rs_sum_share_" + n)[None]

    dx2, df3, dgu3, va2 = ffn_bwd(dy, x2, f3, g3_, u3, vecs[2], f2_gu, f2_dn, 0.5, "ffn2_bwd")
    G["ffn2_w_gu"] = mm_tn(h3, dgu3, "dw_ffn2_gu", D, 1408, 2048, a_resident=True)
    G["ffn2_w_down"] = mm_tn(a3, df3, "dw_ffn2_down", 1408, D, 2048)
    fly_ffn2 = reduce_begin(("ffn2_w_gu", "ffn2_w_down"), "ffn2")
    vec1 = vecs[1] + fly_ffn2[4][0:1, 0:1]
    df2, d_att, d_rec, dao, dhl, d3, va_out = mix_out_bwd(dx2, f2, att, rec, rest, hl, vec1, wao, wro, wout,
                                                          "mix_out_bwd")
    G["w_out"] = mm_tn(mg, df2, "dw_out", D, D, 1024)
    G["w_att_o"] = mm_tn(ao, d_att, "dw_att_o", 512, D, 1024)
    G["w_rec_o"] = mm_tn(hg, d_rec, "dw_rec_o", D, D, 1024)
    dq, db, dkv = attn_bwd(qkv, dao, bias, "attn_bwd")
    dxr, v_lru, dwa_bd, dwx_bd = lru_bwd(dhl, hl, rest, pvec, wa_bd, wx_bd, "lru_bwd")
    lru_w = jnp.concatenate([_diag_blocks(dwa_bd), _diag_blocks(dwx_bd)]).reshape(1, 1024, 128)
    lru_slots = lax.dynamic_update_slice(jnp.zeros((N_DEV, 1024, 128), F32), lru_w, (e, 0, 0))
    lw_s, lw_r, lw_fly, lw_tok = slot_start(lru_slots, "lru_w_start")
    dx1, va_in = proj_bwd(dq, dkv, dxr, d3, win, x1, dx2, vecs[1] + lw_tok[0:1, 0:1], "proj_bwd")
    G["w_in"] = dw_in(h2, dq, dkv, dxr, d3, "dw_in")
    fly_mix = reduce_begin(("w_in", "w_att_o", "w_rec_o", "w_out"), "mixer")
    vec0 = vecs[0] + fly_mix[4][0:1, 0:1]
    dx0, df1, dgu1, va0 = ffn_bwd(dx1, xs, f1, g1_, u1, vec0, f1_gu, f1_dn, 0.5, "ffn1_bwd")
    G["ffn1_w_gu"] = mm_tn(h1, dgu1, "dw_ffn1_gu", D, 1408, 2048, a_resident=True)
    G["ffn1_w_down"] = mm_tn(a1, df1, "dw_ffn1_down", 1408, D, 2048)
    fly_ffn1 = reduce_begin(("ffn1_w_gu", "ffn1_w_down"), "ffn1")
    reduce_end(("ffn2_w_gu", "ffn2_w_down"), fly_ffn2, [fly_ffn1[4]], "ffn2")
    reduce_end(("w_in", "w_att_o", "w_rec_o", "w_out"), fly_mix, [fly_ffn1[4], grads["ffn2_w_down"]], "mixer")

    va1 = va_out + va_in
    vas = (va0, va1, va2)
    dmod = jnp.stack([v[2:5] for v in vas])
    part = {"b_ada": dmod, "norm_pre": jnp.stack([v[0] for v in vas]), "norm_post": jnp.stack([v[1] for v in vas]),
            "rel_bias": bias_grad(db.reshape(8, 128, WIN), "bias_grad")[:, :257], "conv_w": v_lru[0:4], "conv_b": v_lru[4],
            "lru_ba": v_lru[5], "lru_bx": v_lru[6], "lru_lambda": v_lru[7]}
    full_shapes = {"b_ada": (9 * D,), "norm_pre": (3, D), "norm_post": (3, D), "rel_bias": (8, 257),
                   "conv_w": (4, D), "conv_b": (D,), "lru_wa": (16, 64, 64), "lru_ba": (D,),
                   "lru_wx": (16, 64, 64), "lru_bx": (D,), "lru_lambda": (D,)}
    gathered = [n for n in SMALL if n in part]
    g3 = ag_small(_pack([part[n] for n in gathered] + [lvec[0:1, 0:1]], 208), "ag_small_grads")
    summed = _unpack(sum_lead(g3, "sum_small_grads"), [full_shapes[n] for n in gathered] + [(1,)])
    red = dict(zip(gathered, summed[:-1]))
    loss = summed[-1][0]
    lru_all = slot_wait(lw_fly, lw_s, lw_r, [dx0], "lru_w_wait")
    red["lru_wa"], red["lru_wx"] = sum_lead(lru_all, "sum_lru_w").reshape(2, 16, 64, 64)
    cols = lambda a: lax.dynamic_slice(a, (0, p * 256), (a.shape[0], 256))
    grads.update({"b_ada": red["b_ada"][None], "norm_pre": cols(red["norm_pre"])[None],
                  "norm_post": cols(red["norm_post"])[None], "rel_bias": red["rel_bias"][None],
                  "conv_w": cols(red["conv_w"])[None], "conv_b": red["conv_b"][None], "lru_wa": red["lru_wa"][None],
                  "lru_ba": red["lru_ba"][None], "lru_wx": red["lru_wx"][None], "lru_bx": red["lru_bx"][None],
                  "lru_lambda": red["lru_lambda"][None]})

    dmod_all = g3[:, :72].reshape(8, 9 * D)
    dmod_cols = jnp.pad(lax.dynamic_slice(dmod_all, (0, p * 2304), (8, 2304)), ((0, 120), (0, 0)))
    c_all_t = jnp.pad(c_all.T, ((0, 0), (0, 120)))
    grads["w_ada"] = ada_bwd(c_all_t, dmod_cols, "ada_bwd")[None]

    delta, new_m, new_v = {}, {}, {}

    def update(n):
        shp = W[n].shape
        res = adamw(W[n][0], grads[n][0], M[n][0], V[n][0], "adamw_" + n, emit_g=n in geo)
        delta[n], new_m[n], new_v[n] = [a.reshape(shp) for a in res[:3]]
        if n in geo:
            grads[n] = res[3].reshape(shp)

    for n in ("w_ada", "ffn2_w_gu", "ffn2_w_down", "w_in", "w_att_o", "w_rec_o", "w_out"):
        update(n)
    packed = [_pack([src[n] for n in SMALL], 1168) for src in (W, grads, M, V)]
    outs = adamw(*packed, "adamw_small")
    for dst, blk in zip((delta, new_m, new_v), outs):
        for n, a in zip(SMALL, _unpack(blk, [W[n].shape for n in SMALL])):
            dst[n] = a
    reduce_end(("ffn1_w_gu", "ffn1_w_down"), fly_ffn1,
               [outs[0], delta["w_ada"], delta["ffn2_w_gu"], delta["ffn2_w_down"], delta["w_in"], delta["w_out"]], "ffn1")
    for n in ("ffn1_w_gu", "ffn1_w_down"):
        update(n)

    return (loss, dx0[None], *[grads[n] for n in WEIGHTS], *[delta[n] for n in WEIGHTS],
            *[new_m[n] for n in WEIGHTS], *[new_v[n] for n in WEIGHTS])
```

```python
import numpy as np
import jax
import jax.numpy as jnp
from jax import lax
from jax.experimental import pallas as pl
from jax.experimental.pallas import tpu as pltpu

F32 = jnp.float32
BF16 = jnp.bfloat16

D = 1024
FF = 2816
PW = 5632
HP = 128
CHUNK = 64
WIN = 640
TQ = 512
EPS = 1e-6
NEG = -1e30
LRU_C = 8.0
N_DEV = 8
VMEM_LIMIT = 56 * 1024 * 1024

ADAM_LR, ADAM_B1, ADAM_B2, ADAM_EPS, ADAM_WD, ADAM_STEP = 0.001, 0.9, 0.999, 1e-08, 0.01, 10

MESH = pl.DeviceIdType.MESH
ANY = pl.BlockSpec(memory_space=pl.ANY)


def _cp(*sem):
    return pltpu.CompilerParams(dimension_semantics=tuple(sem), vmem_limit_bytes=VMEM_LIMIT)


def _dot(a, b):
    return jnp.dot(a, b, preferred_element_type=F32)


def _dot_nt(a, b):
    return lax.dot_general(a, b, (((1,), (1,)), ((), ())), preferred_element_type=F32)


def _dot_tn(a, b):
    return lax.dot_general(a, b, (((0,), (0,)), ((), ())), preferred_element_type=F32)


def _mean(v):
    return jnp.mean(v, axis=-1, keepdims=True)


def _colsum(v):
    return jnp.sum(v, axis=0, keepdims=True)


def _sigmoid(v):
    return 0.5 * jnp.tanh(0.5 * v) + 0.5


_GK = 0.7978845608028654


def _gelu(v):
    t = jnp.tanh(_GK * (v + 0.044715 * v * v * v))
    return 0.5 * v * (1.0 + t)


def _pre_norm(xv, vec_ref):
    r = lax.rsqrt(_mean(xv * xv) + EPS)
    n = xv * r * vec_ref[0:1, :]
    return n * (1.0 + vec_ref[3:4, :]) + vec_ref[2:3, :]


def _pre_norm_bwd(dh, xv, dres, vec_ref, vacc_ref):
    r = lax.rsqrt(_mean(xv * xv) + EPS)
    xh = xv * r
    n = xh * vec_ref[0:1, :]
    vacc_ref[2:3, :] += _colsum(dh)
    vacc_ref[3:4, :] += _colsum(dh * n)
    dn = dh * (1.0 + vec_ref[3:4, :])
    vacc_ref[0:1, :] += _colsum(dn * xh)
    dxh = dn * vec_ref[0:1, :]
    return r * (dxh - xh * _mean(dxh * xh)) + dres


def _post_norm_bwd(dxo, fv, res, vec_ref, vacc_ref):
    rf = lax.rsqrt(_mean(fv * fv) + EPS)
    fh = fv * rf
    gp = vec_ref[1:2, :]
    vacc_ref[4:5, :] += _colsum(res * dxo * (fh * gp))
    dy = (res * vec_ref[4:5, :]) * dxo
    vacc_ref[1:2, :] += _colsum(dy * fh)
    dfn = dy * gp
    return rf * (dfn - fh * _mean(dfn * fh))


def _u_spec(tf):
    return pl.BlockSpec((pl.Element(D), pl.Element(tf)),
                        lambda i, j: (0, pl.multiple_of(jnp.minimum(FF + j * tf, 2 * FF - tf), 128)))


def ffn_fwd(x, vec, w_gu, w_dn, res, name, tgt=None, tm=1024, tf=512):
    S = x.shape[0]
    tm = min(tm, S)
    nt = S // tm
    nf = -(-FF // tf)
    tail = FF - tf * (nf - 1)
    halves = [pl.ds(r * (tm // 2), tm // 2) for r in range(2)]
    head = tgt is not None

    def body(*refs):
        x_ref, vec_ref, wg_ref, wu_ref, wd_ref = refs[:5]
        if head:
            t_hbm, xo_ref, h_ref, g_ref, u_ref, a_ref, f_ref, l_ref, hs, acc, lacc, ts, tsem = refs[5:]
        else:
            xo_ref, h_ref, g_ref, u_ref, a_ref, f_ref, hs, acc = refs[5:]
        i, j = pl.program_id(0), pl.program_id(1)
        if head:
            late = pltpu.make_async_copy(t_hbm.at[pl.ds(pl.multiple_of(i * tm, 8), tm)], ts, tsem)

        @pl.when(j == 0)
        def _():
            if head:
                late.start()
            h = _pre_norm(x_ref[...], vec_ref).astype(BF16)
            hs[...] = h
            h_ref[...] = h
            acc[...] = jnp.zeros_like(acc)

        def chunk(w):
            gu = [(_dot(hs[r, :], wg_ref[:, 0:w]), _dot(hs[r, :], wu_ref[:, tf - w:tf])) for r in halves]
            acts = []
            for r, (g, u) in zip(halves, gu):
                g_ref[r, 0:w] = g.astype(BF16)
                u_ref[r, 0:w] = u.astype(BF16)
                a = (g * _sigmoid(g) * u).astype(BF16)
                a_ref[r, 0:w] = a
                acts.append(a)
            for r, a in zip(halves, acts):
                acc[r, :] += _dot(a, wd_ref[0:w, :])

        @pl.when(j < nf - 1)
        def _():
            chunk(tf)

        @pl.when(j == nf - 1)
        def _():
            chunk(tail)
            f = acc[...]
            f_ref[...] = f.astype(BF16)
            y = f * lax.rsqrt(_mean(f * f) + EPS) * vec_ref[1:2, :]
            xo = x_ref[...] + (res * vec_ref[4:5, :]) * y
            if head:
                @pl.when(i == 0)
                def _():
                    lacc[...] = jnp.zeros_like(lacc)

                late.wait()
                d = xo - ts[...]
                xo_ref[...] = d * (1.0 / D)
                lacc[...] += _colsum(d * d)

                @pl.when(i == nt - 1)
                def _():
                    l_ref[...] = jnp.broadcast_to(0.5 * jnp.sum(lacc[...]) * (1.0 / D), (8, 128))
            else:
                xo_ref[...] = xo

    row = lambda i, j: (i, 0)
    col = lambda i, j: (i, j)
    in_specs = [pl.BlockSpec((tm, D), row), pl.BlockSpec((8, D), lambda i, j: (0, 0)),
                pl.BlockSpec((D, tf), lambda i, j: (0, j)), _u_spec(tf),
                pl.BlockSpec((tf, D), lambda i, j: (j, 0))]
    out_specs = [pl.BlockSpec((tm, D), row), pl.BlockSpec((tm, D), row), pl.BlockSpec((tm, tf), col),
                 pl.BlockSpec((tm, tf), col), pl.BlockSpec((tm, tf), col), pl.BlockSpec((tm, D), row)]
    out_shape = [jax.ShapeDtypeStruct((S, D), F32), jax.ShapeDtypeStruct((S, D), BF16),
                 jax.ShapeDtypeStruct((S, FF), BF16), jax.ShapeDtypeStruct((S, FF), BF16),
                 jax.ShapeDtypeStruct((S, FF), BF16), jax.ShapeDtypeStruct((S, D), BF16)]
    scratch = [pltpu.VMEM((tm, D), BF16), pltpu.VMEM((tm, D), F32)]
    args = [x, vec, w_gu, w_gu, w_dn]
    if head:
        in_specs.append(ANY)
        out_specs.append(pl.BlockSpec((8, 128), lambda i, j: (0, 0)))
        out_shape.append(jax.ShapeDtypeStruct((8, 128), F32))
        scratch += [pltpu.VMEM((1, D), F32), pltpu.VMEM((tm, D), F32), pltpu.SemaphoreType.DMA]
        args.append(tgt)
    return pl.pallas_call(
        body, name=name, grid=(nt, nf), in_specs=in_specs, out_specs=out_specs, out_shape=out_shape,
        scratch_shapes=scratch,
        compiler_params=_cp("arbitrary" if head else "parallel", "arbitrary"),
    )(*args)


def ffn_bwd(dxo, x, f, g, u, vec, w_gu, w_dn, res, name, tm=1024, tf=512):
    S = x.shape[0]
    tm = min(tm, S)
    nf = -(-FF // tf)
    tail = FF - tf * (nf - 1)
    halves = [pl.ds(r * (tm // 2), tm // 2) for r in range(2)]

    def body(dxo_ref, x_hbm, f_ref, g_ref, u_ref, vec_ref, wg_ref, wu_ref, wd_ref,
             dx_ref, df_ref, dgu_ref, vacc_ref, dfs, acc, xs, xsem):
        i, j = pl.program_id(0), pl.program_id(1)
        late = pltpu.make_async_copy(x_hbm.at[pl.ds(pl.multiple_of(i * tm, 8), tm)], xs, xsem)

        @pl.when((i == 0) & (j == 0))
        def _():
            vacc_ref[...] = jnp.zeros_like(vacc_ref)

        @pl.when(j == 0)
        def _():
            late.start()
            df = _post_norm_bwd(dxo_ref[...], f_ref[...].astype(F32), res, vec_ref, vacc_ref).astype(BF16)
            dfs[...] = df
            df_ref[...] = df
            acc[...] = jnp.zeros_like(acc)

        def chunk(w):
            da = [_dot_nt(dfs[h, :], wd_ref[0:w, :]) for h in halves]
            dgu = []
            for h, d in zip(halves, da):
                gv, uv = g_ref[h, 0:w].astype(F32), u_ref[h, 0:w].astype(F32)
                sg = _sigmoid(gv)
                dg = (d * uv * (sg * (1.0 + gv * (1.0 - sg)))).astype(BF16)
                du = (d * (gv * sg)).astype(BF16)
                dgu_ref[0, h, 0:w] = dg
                dgu_ref[1, h, 0:w] = du
                dgu.append((dg, du))
            for h, (dg, du) in zip(halves, dgu):
                acc[h, :] += _dot_nt(dg, wg_ref[:, 0:w]) + _dot_nt(du, wu_ref[:, tf - w:tf])

        @pl.when(j < nf - 1)
        def _():
            chunk(tf)

        @pl.when(j == nf - 1)
        def _():
            chunk(tail)
            late.wait()
            dx_ref[...] = _pre_norm_bwd(acc[...], xs[...], dxo_ref[...], vec_ref, vacc_ref)

    row = lambda i, j: (i, 0)
    col = lambda i, j: (i, j)
    return pl.pallas_call(
        body, name=name, grid=(S // tm, nf),
        in_specs=[pl.BlockSpec((tm, D), row), ANY, pl.BlockSpec((tm, D), row),
                  pl.BlockSpec((tm, tf), col), pl.BlockSpec((tm, tf), col),
                  pl.BlockSpec((8, D), lambda i, j: (0, 0)),
                  pl.BlockSpec((D, tf), lambda i, j: (0, j)), _u_spec(tf),
                  pl.BlockSpec((tf, D), lambda i, j: (j, 0))],
        out_specs=[pl.BlockSpec((tm, D), row), pl.BlockSpec((tm, D), row),
                   pl.BlockSpec((2, tm, tf), lambda i, j: (0, i, j)),
                   pl.BlockSpec((8, D), lambda i, j: (0, 0))],
        out_shape=[jax.ShapeDtypeStruct((S, D), F32), jax.ShapeDtypeStruct((S, D), BF16),
                   jax.ShapeDtypeStruct((2, S, FF), BF16), jax.ShapeDtypeStruct((8, D), F32)],
        scratch_shapes=[pltpu.VMEM((tm, D), BF16), pltpu.VMEM((tm, D), F32), pltpu.VMEM((tm, D), F32),
                        pltpu.SemaphoreType.DMA],
        compiler_params=_cp("arbitrary", "arbitrary"),
    )(dxo, x, f, g, u, vec, w_gu, w_gu, w_dn)


def mm_tn(a, b, name, tm, tn, tk, out_dtype=BF16, a_resident=False):
    S, M = a.shape
    if b.ndim == 3:
        G, _, Nf = b.shape
    else:
        G, Nf = 1, b.shape[1]
    N = G * Nf
    tk = min(tk, S)
    nbf = Nf // tn
    nk = S // tk

    def body(a_ref, b_ref, o_ref, acc):
        k = pl.program_id(2)

        @pl.when(k == 0)
        def _():
            acc[...] = jnp.zeros_like(acc)

        a_blk = a_ref[pl.ds(pl.multiple_of(k * tk, 16), tk), :] if a_resident else a_ref[...]
        acc[...] += _dot_tn(a_blk, b_ref[...])

        @pl.when(k == nk - 1)
        def _():
            o_ref[...] = acc[...].astype(out_dtype)

    if b.ndim == 3:
        b_spec = pl.BlockSpec((None, tk, tn), lambda i, j, k: (j // nbf, k, j % nbf))
    else:
        b_spec = pl.BlockSpec((tk, tn), lambda i, j, k: (k, j))
    if a_resident:
        a_spec = pl.BlockSpec((S, M), lambda i, j, k: (0, 0), pipeline_mode=pl.Buffered(1))
    else:
        a_spec = pl.BlockSpec((tk, tm), lambda i, j, k: (k, i))
    return pl.pallas_call(
        body, name=name, grid=(M // tm, N // tn, nk),
        in_specs=[a_spec, b_spec],
        out_specs=pl.BlockSpec((tm, tn), lambda i, j, k: (i, j)),
        out_shape=jax.ShapeDtypeStruct((M, N), out_dtype),
        scratch_shapes=[pltpu.VMEM((tm, tn), F32)],
        compiler_params=_cp("parallel", "parallel", "arbitrary"),
    )(a, b)


def proj_fwd(x, vec, w_in, name, tm=2048, tn=512):
    S = x.shape[0]
    tm = min(tm, S)
    nq = 1536 // tn

    def body(x_ref, vec_ref, w_ref, h_ref, qkv_ref, rest_ref, hs):
        j = pl.program_id(1)

        @pl.when(j == 0)
        def _():
            h = _pre_norm(x_ref[...], vec_ref).astype(BF16)
            hs[...] = h
            h_ref[...] = h

        r = _dot(hs[...], w_ref[...])

        @pl.when(j < nq)
        def _():
            qkv_ref[...] = r.astype(BF16)

        @pl.when(j >= nq)
        def _():
            rest_ref[...] = r.astype(BF16)

    row = lambda i, j: (i, 0)
    return pl.pallas_call(
        body, name=name, grid=(S // tm, PW // tn),
        in_specs=[pl.BlockSpec((tm, D), row), pl.BlockSpec((8, D), lambda i, j: (0, 0)),
                  pl.BlockSpec((D, tn), lambda i, j: (0, j))],
        out_specs=[pl.BlockSpec((tm, D), row),
                   pl.BlockSpec((tm, tn), lambda i, j: (i, jnp.minimum(j, nq - 1))),
                   pl.BlockSpec((tm, tn), lambda i, j: (i, jnp.maximum(j - nq, 0)))],
        out_shape=[jax.ShapeDtypeStruct((S, D), BF16), jax.ShapeDtypeStruct((S, 1536), BF16),
                   jax.ShapeDtypeStruct((S, 4096), BF16)],
        scratch_shapes=[pltpu.VMEM((tm, D), BF16)],
        compiler_params=_cp("parallel", "arbitrary"),
    )(x, vec, w_in)


def proj_bwd(dq, dkv, dxr, d3, w_in, x, dxo, vec, name, tm=2048, tk=512):
    S = x.shape[0]
    tm = min(tm, S)
    nk = PW // tk

    def body(dq_ref, dkv_ref, dxr_ref, d3_ref, w_ref, x_hbm, dxo_hbm, vec_ref, dx_ref, vacc_ref, acc, xs, dxos, sems):
        i, j = pl.program_id(0), pl.program_id(1)
        tok = pl.ds(pl.multiple_of(i * tm, 8), tm)
        late = (pltpu.make_async_copy(x_hbm.at[tok], xs, sems.at[0]),
                pltpu.make_async_copy(dxo_hbm.at[tok], dxos, sems.at[1]))

        @pl.when((i == 0) & (j == 0))
        def _():
            vacc_ref[...] = jnp.zeros_like(vacc_ref)

        @pl.when(j == 0)
        def _():
            for cp in late:
                cp.start()
            acc[...] = _dot_nt(dq_ref[...], w_ref[...])

        @pl.when((j >= 1) & (j < 3))
        def _():
            acc[...] += _dot_nt(dkv_ref[...], w_ref[...])

        @pl.when((j >= 3) & (j < 5))
        def _():
            acc[...] += _dot_nt(dxr_ref[...], w_ref[...])

        @pl.when(j >= 5)
        def _():
            acc[...] += _dot_nt(d3_ref[...], w_ref[...])

        @pl.when(j == nk - 1)
        def _():
            for cp in late:
                cp.wait()
            dx_ref[...] = _pre_norm_bwd(acc[...], xs[...], dxos[...], vec_ref, vacc_ref)

    row = lambda i, j: (i, 0)
    return pl.pallas_call(
        body, name=name, grid=(S // tm, nk),
        in_specs=[pl.BlockSpec((None, tm, tk), lambda i, j: (0, i, 0)),
                  pl.BlockSpec((None, tm, tk), lambda i, j: (jnp.clip(j - 1, 0, 1), i, 0)),
                  pl.BlockSpec((tm, tk), lambda i, j: (i, jnp.clip(j - 3, 0, 1))),
                  pl.BlockSpec((None, tm, tk), lambda i, j: (jnp.clip(j - 5, 0, 5) // 2, i, jnp.clip(j - 5, 0, 5) % 2)),
                  pl.BlockSpec((D, tk), lambda i, j: (0, j)),
                  ANY, ANY,
                  pl.BlockSpec((8, D), lambda i, j: (0, 0))],
        out_specs=[pl.BlockSpec((tm, D), row, pipeline_mode=pl.Buffered(1)),
                   pl.BlockSpec((8, D), lambda i, j: (0, 0))],
        out_shape=[jax.ShapeDtypeStruct((S, D), F32), jax.ShapeDtypeStruct((8, D), F32)],
        scratch_shapes=[pltpu.VMEM((tm, D), F32), pltpu.VMEM((tm, D), F32), pltpu.VMEM((tm, D), F32),
                        pltpu.SemaphoreType.DMA((2,))],
        compiler_params=_cp("arbitrary", "arbitrary"),
    )(dq, dkv, dxr, d3, w_in, x, dxo, vec)


def _two_heads(v, lane):
    zero = jnp.zeros((), v.dtype)
    return jnp.concatenate([jnp.where(lane < 64, v, zero), jnp.where(lane >= 64, v, zero)], axis=0)


def _attn_scores(qm, ka, bias_h, i, grp):
    s = _dot_nt(qm, ka) + bias_h
    col = lax.broadcasted_iota(jnp.int32, s.shape, 1)
    first_key = jnp.where(i == 0, 512 - 128 * grp, 0)
    return jnp.where(col >= first_key, s, NEG)


def _softmax(s):
    e = jnp.exp(s - jnp.max(s, axis=-1, keepdims=True))
    return e * (1.0 / jnp.sum(e, axis=-1, keepdims=True))


NG = TQ // 128


def attn_fwd(qkv, bias, name):
    S = qkv.shape[0]
    nb = S // TQ

    def body(q_ref, kp_ref, kc_ref, vp_ref, vc_ref, b_ref, o_ref, kw, vw):
        i = pl.program_id(1)
        kw[0:TQ, :] = kp_ref[...]
        kw[TQ:2 * TQ, :] = kc_ref[...]
        vw[0:TQ, :] = vp_ref[...]
        vw[TQ:2 * TQ, :] = vc_ref[...]
        lane = lax.broadcasted_iota(jnp.int32, (1, HP), 1)

        rows = [pl.ds(128 * a, 128) for a in range(NG)]
        keys = [pl.ds(128 * a, WIN) for a in range(NG)]
        q2 = [_two_heads(q_ref[r, :] * jnp.asarray(0.125, BF16), lane) for r in rows]
        s = [_attn_scores(q2[a], kw[keys[a], :], b_ref[...], i, a) for a in range(NG)]
        p = [_softmax(sa).astype(BF16) for sa in s]
        o2 = [_dot(p[a], vw[keys[a], :]) for a in range(NG)]
        for a in range(NG):
            o_ref[rows[a], :] = jnp.where(lane < 64, o2[a][0:128], o2[a][128:256]).astype(BF16)

    prev = lambda h, i: (jnp.maximum(i - 1, 0), 0)
    return pl.pallas_call(
        body, name=name, grid=(4, nb),
        in_specs=[pl.BlockSpec((TQ, HP), lambda h, i: (i, h)),
                  pl.BlockSpec((TQ, HP), lambda h, i: (jnp.maximum(i - 1, 0), 4 + h)),
                  pl.BlockSpec((TQ, HP), lambda h, i: (i, 4 + h)),
                  pl.BlockSpec((TQ, HP), lambda h, i: (jnp.maximum(i - 1, 0), 8 + h)),
                  pl.BlockSpec((TQ, HP), lambda h, i: (i, 8 + h)),
                  pl.BlockSpec((None, 256, WIN), lambda h, i: (h, 0, 0))],
        out_specs=pl.BlockSpec((TQ, HP), lambda h, i: (i, h)),
        out_shape=jax.ShapeDtypeStruct((S, 512), BF16),
        scratch_shapes=[pltpu.VMEM((2 * TQ, HP), BF16), pltpu.VMEM((2 * TQ, HP), BF16)],
        compiler_params=_cp("parallel", "arbitrary"),
    )(qkv, qkv, qkv, qkv, qkv, bias)


def attn_bwd(qkv, do, bias, name):
    S = qkv.shape[0]
    nb = S // TQ

    def body(q_ref, kp_ref, kc_ref, vp_ref, vc_ref, do_ref, b_ref, dqkv_ref, db_ref, dkv_ref, kw, vw, ak, av):
        i = pl.program_id(1)

        @pl.when(i == 0)
        def _():
            db_ref[...] = jnp.zeros_like(db_ref)
            ak[...] = jnp.zeros_like(ak)
            av[...] = jnp.zeros_like(av)

        @pl.when(i > 0)
        def _():
            ak[0:TQ, :] = ak[TQ:2 * TQ, :]
            av[0:TQ, :] = av[TQ:2 * TQ, :]
            ak[TQ:2 * TQ, :] = jnp.zeros((TQ, HP), F32)
            av[TQ:2 * TQ, :] = jnp.zeros((TQ, HP), F32)

        @pl.when(i < nb)
        def _():
            kw[0:TQ, :] = kp_ref[...]
            kw[TQ:2 * TQ, :] = kc_ref[...]
            vw[0:TQ, :] = vp_ref[...]
            vw[TQ:2 * TQ, :] = vc_ref[...]
            lane = lax.broadcasted_iota(jnp.int32, (1, HP), 1)

            rows = [pl.ds(128 * a, 128) for a in range(NG)]
            keys = [pl.ds(128 * a, WIN) for a in range(NG)]
            q2 = [_two_heads(q_ref[r, :] * jnp.asarray(0.125, BF16), lane) for r in rows]
            do2 = [_two_heads(do_ref[r, :], lane) for r in rows]
            s = [_attn_scores(q2[a], kw[keys[a], :], b_ref[...], i, a) for a in range(NG)]
            dp = [_dot_nt(do2[a], vw[keys[a], :]) for a in range(NG)]
            p = [_softmax(sa) for sa in s]
            ds = [p[a] * (dp[a] - jnp.sum(p[a] * dp[a], axis=-1, keepdims=True)) for a in range(NG)]
            db_ref[...] += (ds[0] + ds[1]) + (ds[2] + ds[3])
            dsb = [d.astype(BF16) for d in ds]
            dq2 = [_dot(dsb[a], kw[keys[a], :]) for a in range(NG)]
            dk = [_dot_tn(dsb[a], q2[a]) for a in range(NG)]
            dv = [_dot_tn(p[a].astype(BF16), do2[a]) for a in range(NG)]
            for a in range(NG):
                ak[keys[a], :] += dk[a]
                av[keys[a], :] += dv[a]
                dq = jnp.where(lane < 64, dq2[a][0:128], dq2[a][128:256])
                dqkv_ref[0, rows[a], :] = (dq * 0.125).astype(BF16)

        @pl.when(i > 0)
        def _():
            dkv_ref[0] = ak[0:TQ, :].astype(BF16)
            dkv_ref[1] = av[0:TQ, :].astype(BF16)

    cur = lambda i: jnp.minimum(i, nb - 1)
    prv = lambda i: jnp.clip(i - 1, 0, nb - 1)
    dq, db, dkv = pl.pallas_call(
        body, name=name, grid=(4, nb + 1),
        in_specs=[pl.BlockSpec((TQ, HP), lambda h, i: (cur(i), h)),
                  pl.BlockSpec((TQ, HP), lambda h, i: (prv(i), 4 + h)),
                  pl.BlockSpec((TQ, HP), lambda h, i: (cur(i), 4 + h)),
                  pl.BlockSpec((TQ, HP), lambda h, i: (prv(i), 8 + h)),
                  pl.BlockSpec((TQ, HP), lambda h, i: (cur(i), 8 + h)),
                  pl.BlockSpec((TQ, HP), lambda h, i: (cur(i), h)),
                  pl.BlockSpec((None, 256, WIN), lambda h, i: (h, 0, 0))],
        out_specs=[pl.BlockSpec((1, TQ, HP), lambda h, i: (0, cur(i), h)),
                   pl.BlockSpec((None, 256, WIN), lambda h, i: (h, 0, 0)),
                   pl.BlockSpec((2, TQ, HP), lambda h, i: (0, prv(i), h))],
        out_shape=[jax.ShapeDtypeStruct((1, S, 512), BF16), jax.ShapeDtypeStruct((4, 256, WIN), F32),
                   jax.ShapeDtypeStruct((2, S, 512), BF16)],
        scratch_shapes=[pltpu.VMEM((2 * TQ, HP), BF16), pltpu.VMEM((2 * TQ, HP), BF16),
                        pltpu.VMEM((2 * TQ, HP), F32), pltpu.VMEM((2 * TQ, HP), F32)],
        compiler_params=_cp("parallel", "arbitrary"),
    )(qkv, qkv, qkv, qkv, qkv, do, bias)
    return dq, db, dkv


def bias_grad(db, name):
    def body(db_ref, o_ref):
        r = lax.broadcasted_iota(jnp.int32, (128, 128), 0)
        c = lax.broadcasted_iota(jnp.int32, (128, 128), 1)
        flip = (r + c == 127).astype(BF16)
        lane = lax.broadcasted_iota(jnp.int32, (16, 384), 1)
        src = lax.broadcasted_iota(jnp.int32, (128, 384), 0)
        dst = lax.broadcasted_iota(jnp.int32, (128, 384), 1)

        def split_dot(v, m):
            hi = v.astype(BF16)
            r1 = v - hi.astype(F32)
            mid = r1.astype(BF16)
            lo = (r1 - mid.astype(F32)).astype(BF16)
            return _dot(hi, m) + _dot(mid, m) + _dot(lo, m)

        def diag_sums(w):
            y = pltpu.roll(split_dot(w, flip), 0, 1, stride=1, stride_axis=0)
            return jnp.broadcast_to(_colsum(y), (16, 128))

        w4 = db_ref[0, :, 512:640]
        w3 = db_ref[0, :, 384:512]
        far = jnp.sum(db_ref[0, :, 0:384]) + jnp.sum(jnp.where(r >= c, w3, 0.0))
        lo4 = diag_sums(jnp.where(r >= c, w4, 0.0))
        up4 = diag_sums(jnp.where(r < c, w4, 0.0))
        up3 = diag_sums(jnp.where(r < c, w3, 0.0))
        p_lo4 = (dst == 128 + (src + 1) % 128).astype(BF16)
        p_up4 = ((dst == src + 1) & (src < 127)).astype(BF16)
        p_up3 = ((dst == src + 129) & (src < 127)).astype(BF16)
        out = split_dot(lo4, p_lo4) + split_dot(up4, p_up4) + split_dot(up3, p_up3)
        o_ref[0] = out + jnp.where(lane == 256, far, 0.0)

    return pl.pallas_call(
        body, name=name, grid=(8,),
        in_specs=[pl.BlockSpec((1, 128, WIN), lambda h: (h, 0, 0))],
        out_specs=pl.BlockSpec((1, 16, 384), lambda h: (h, 0, 0)),
        out_shape=jax.ShapeDtypeStruct((8, 16, 384), F32),
        compiler_params=_cp("parallel"),
    )(db)[:, 0, :]


LT = 1024
LC = 512


def _lru_gates(xs, pv_ref, wa_ref, wx_ref, tl):
    xc = (pv_ref[4:5, :] + pv_ref[3:4, :] * xs[pl.ds(8, tl), :] + pv_ref[2:3, :] * xs[pl.ds(7, tl), :]
          + pv_ref[1:2, :] * xs[pl.ds(6, tl), :] + pv_ref[0:1, :] * xs[pl.ds(5, tl), :])
    xcb = xc.astype(BF16)
    pa = jnp.concatenate([_dot(xcb[:, 0:256], wa_ref[0]), _dot(xcb[:, 256:512], wa_ref[1])], axis=1)
    px = jnp.concatenate([_dot(xcb[:, 0:256], wx_ref[0]), _dot(xcb[:, 256:512], wx_ref[1])], axis=1)
    r = _sigmoid(pa + pv_ref[5:6, :])
    ig = _sigmoid(px + pv_ref[6:7, :])
    z = -pv_ref[7:8, :]
    sp = jnp.maximum(z, 0.0) + jnp.log1p(jnp.exp(-jnp.abs(z)))
    log_a = (-LRU_C * r) * sp
    a = jnp.exp(log_a)
    s = jnp.tanh(-log_a) * (1.0 + a * a)
    inv_mult = lax.rsqrt(s)
    mult = jnp.where(s > 0.0, s * inv_mult, 0.0)
    return xc, xcb, r, ig, sp, a, mult, inv_mult


def lru_fwd(rest, pvec, wa, wx, name):
    S = rest.shape[0]
    tl = min(LT, S)
    nt = S // tl

    def body(xr_ref, halo_ref, yr_ref, pv_ref, wa_ref, wx_ref, h_ref, hg_ref, xs, a_s, u_s, h_s, carry):
        ti = pl.program_id(1)

        @pl.when(ti == 0)
        def _():
            carry[...] = jnp.zeros_like(carry)

        xs[0:8, :] = jnp.where(ti > 0, halo_ref[8:16, :].astype(F32), 0.0)
        xs[pl.ds(8, tl), :] = xr_ref[...].astype(F32)
        xc, _, _, ig, _, a, mult, _ = _lru_gates(xs, pv_ref, wa_ref, wx_ref, tl)
        a_s[...] = a
        u_s[...] = mult * (ig * xc)
        row = lax.broadcasted_iota(jnp.int32, (8, LC), 0)

        def blk(bi, c):
            o = pl.multiple_of(bi * 8, 8)
            av = a_s[pl.ds(o, 8), :]
            bv = u_s[pl.ds(o, 8), :]
            for d in (1, 2, 4):
                a_sh = pltpu.roll(av, d, 0)
                b_sh = pltpu.roll(bv, d, 0)
                m = row >= d
                bv = jnp.where(m, av * b_sh + bv, bv)
                av = jnp.where(m, av * a_sh, av)
            hv = bv + av * c
            h_s[pl.ds(o, 8), :] = hv
            return hv[7:8, :]

        carry[...] = lax.fori_loop(0, tl // 8, blk, carry[...])
        h = h_s[...]
        h_ref[...] = h
        hg_ref[...] = (h * _gelu(yr_ref[...].astype(F32))).astype(BF16)

    hb = tl // 16
    return pl.pallas_call(
        body, name=name, grid=(2, nt),
        in_specs=[pl.BlockSpec((tl, LC), lambda c, t: (t, c)),
                  pl.BlockSpec((16, LC), lambda c, t: (jnp.maximum(t * hb - 1, 0), c)),
                  pl.BlockSpec((tl, LC), lambda c, t: (t, 2 + c)),
                  pl.BlockSpec((8, LC), lambda c, t: (0, c)),
                  pl.BlockSpec((2, 256, 256), lambda c, t: (c, 0, 0)),
                  pl.BlockSpec((2, 256, 256), lambda c, t: (c, 0, 0))],
        out_specs=[pl.BlockSpec((tl, LC), lambda c, t: (t, c)), pl.BlockSpec((tl, LC), lambda c, t: (t, c))],
        out_shape=[jax.ShapeDtypeStruct((S, D), F32), jax.ShapeDtypeStruct((S, D), BF16)],
        scratch_shapes=[pltpu.VMEM((tl + 8, LC), F32), pltpu.VMEM((tl, LC), F32), pltpu.VMEM((tl, LC), F32),
                        pltpu.VMEM((tl, LC), F32), pltpu.VMEM((1, LC), F32)],
        compiler_params=_cp("parallel", "arbitrary"),
    )(rest, rest, rest, pvec, wa, wx)


def lru_bwd(dh, h, rest, pvec, wa, wx, name):
    S = rest.shape[0]
    tl = min(LT, S)
    nt = S // tl

    def body(dh_ref, h_ref, hhalo_ref, xr_ref, xhalo_ref, pv_ref, wa_ref, wx_ref,
             dxr_ref, vacc_ref, dwa_ref, dwx_ref,
             xs, hs, a_s, ash_s, b_s, lam_s, dxe, anext, lnext, dxnext):
        ti = pl.program_id(1)
        tr = nt - 1 - ti

        @pl.when(ti == 0)
        def _():
            anext[...] = jnp.zeros_like(anext)
            lnext[...] = jnp.zeros_like(lnext)
            dxnext[...] = jnp.zeros_like(dxnext)
            vacc_ref[...] = jnp.zeros_like(vacc_ref)
            dwa_ref[...] = jnp.zeros_like(dwa_ref)
            dwx_ref[...] = jnp.zeros_like(dwx_ref)

        xs[0:8, :] = jnp.where(tr > 0, xhalo_ref[8:16, :].astype(F32), 0.0)
        xs[pl.ds(8, tl), :] = xr_ref[...].astype(F32)
        xc, xcb, r, ig, sp, a, mult, inv_mult = _lru_gates(xs, pv_ref, wa_ref, wx_ref, tl)

        a_s[pl.ds(0, tl), :] = a
        a_s[pl.ds(tl, 8), :] = jnp.broadcast_to(anext[...], (8, LC))
        ash_s[...] = a_s[pl.ds(1, tl), :]
        b_s[...] = dh_ref[...]
        row = lax.broadcasted_iota(jnp.int32, (8, LC), 0)

        def blk(k, c):
            o = pl.multiple_of((tl // 8 - 1 - k) * 8, 8)
            av = ash_s[pl.ds(o, 8), :]
            bv = b_s[pl.ds(o, 8), :]
            for d in (1, 2, 4):
                a_sh = pltpu.roll(av, 8 - d, 0)
                b_sh = pltpu.roll(bv, 8 - d, 0)
                m = row < 8 - d
                bv = jnp.where(m, bv + av * b_sh, bv)
                av = jnp.where(m, av * a_sh, av)
            lv = bv + av * c
            lam_s[pl.ds(o, 8), :] = lv
            return lv[0:1, :]

        lnext[...] = lax.fori_loop(0, tl // 8, blk, lnext[...])
        anext[...] = a[0:1, :]
        lam = lam_s[...]

        hs[0:8, :] = jnp.where(tr > 0, hhalo_ref[...], 0.0)
        hs[pl.ds(8, tl), :] = h_ref[...]
        d_a = lam * hs[pl.ds(7, tl), :]
        d_mult = lam * (ig * xc)
        d_ig = lam * mult * xc
        dxc = lam * mult * ig
        d_log_a = d_a * a - d_mult * (a * a) * inv_mult
        d_r = d_log_a * (-LRU_C * sp)
        vacc_ref[7:8, :] += _colsum(d_log_a * (-LRU_C * r)) * (-_sigmoid(-pv_ref[7:8, :]))
        d_pa = d_r * r * (1.0 - r)
        d_px = d_ig * ig * (1.0 - ig)
        vacc_ref[5:6, :] += _colsum(d_pa)
        vacc_ref[6:7, :] += _colsum(d_px)
        dpa = d_pa.astype(BF16)
        dpx = d_px.astype(BF16)
        back = []
        for g in range(2):
            sl = slice(256 * g, 256 * g + 256)
            dwa_ref[g] += _dot_tn(xcb[:, sl], dpa[:, sl])
            dwx_ref[g] += _dot_tn(xcb[:, sl], dpx[:, sl])
            back.append(_dot_nt(dpa[:, sl], wa_ref[g]) + _dot_nt(dpx[:, sl], wx_ref[g]))
        dxc = dxc + jnp.concatenate(back, axis=1)
        vacc_ref[4:5, :] += _colsum(dxc)
        for k in range(4):
            vacc_ref[k:k + 1, :] += _colsum(dxc * xs[pl.ds(5 + k, tl), :])
        dxe[pl.ds(0, tl), :] = dxc
        dxe[pl.ds(tl, 8), :] = dxnext[...]
        dxr = (pv_ref[3:4, :] * dxc + pv_ref[2:3, :] * dxe[pl.ds(1, tl), :]
               + pv_ref[1:2, :] * dxe[pl.ds(2, tl), :] + pv_ref[0:1, :] * dxe[pl.ds(3, tl), :])
        dxr_ref[...] = dxr.astype(BF16)
        dxnext[...] = dxc[0:8, :]

    hb = tl // 8
    rev = lambda t: nt - 1 - t
    halo = lambda t: jnp.maximum(rev(t) * hb - 1, 0)
    big = lambda: pltpu.VMEM((tl + 8, LC), F32)
    til = lambda: pltpu.VMEM((tl, LC), F32)
    return pl.pallas_call(
        body, name=name, grid=(2, nt),
        in_specs=[pl.BlockSpec((tl, LC), lambda c, t: (rev(t), c)),
                  pl.BlockSpec((tl, LC), lambda c, t: (rev(t), c)),
                  pl.BlockSpec((8, LC), lambda c, t: (halo(t), c)),
                  pl.BlockSpec((tl, LC), lambda c, t: (rev(t), c)),
                  pl.BlockSpec((16, LC), lambda c, t: (jnp.maximum(rev(t) * (tl // 16) - 1, 0), c)),
                  pl.BlockSpec((8, LC), lambda c, t: (0, c)),
                  pl.BlockSpec((2, 256, 256), lambda c, t: (c, 0, 0)),
                  pl.BlockSpec((2, 256, 256), lambda c, t: (c, 0, 0))],
        out_specs=[pl.BlockSpec((tl, LC), lambda c, t: (rev(t), c)),
                   pl.BlockSpec((8, LC), lambda c, t: (0, c)),
                   pl.BlockSpec((2, 256, 256), lambda c, t: (c, 0, 0)),
                   pl.BlockSpec((2, 256, 256), lambda c, t: (c, 0, 0))],
        out_shape=[jax.ShapeDtypeStruct((S, D), BF16), jax.ShapeDtypeStruct((8, D), F32),
                   jax.ShapeDtypeStruct((4, 256, 256), F32), jax.ShapeDtypeStruct((4, 256, 256), F32)],
        scratch_shapes=[big(), big(), big(), til(), til(), til(), big(),
                        pltpu.VMEM((1, LC), F32), pltpu.VMEM((1, LC), F32), pltpu.VMEM((8, LC), F32)],
        compiler_params=_cp("parallel", "arbitrary"),
    )(dh, h, h, rest, rest, pvec, wa, wx)


def mix_out_fwd(x, ao, hg, rest, vec, w_att_o, w_rec_o, w_out, name, tm=512):
    S = x.shape[0]
    tm = min(tm, S)

    def body(x_ref, ao_ref, hg_ref, ga_ref, gr_ref, vec_ref, wa_ref, wr_ref, wo_ref,
             xo_ref, att_ref, rec_ref, mg_ref, f_ref):
        att = _dot(ao_ref[...], wa_ref[...])
        rec = _dot(hg_ref[...], wr_ref[...])
        att_ref[...] = att.astype(BF16)
        rec_ref[...] = rec.astype(BF16)
        mg = (_sigmoid(ga_ref[...].astype(F32)) * att + _sigmoid(gr_ref[...].astype(F32)) * rec).astype(BF16)
        mg_ref[...] = mg
        f = _dot(mg, wo_ref[...])
        f_ref[...] = f.astype(BF16)
        y = f * lax.rsqrt(_mean(f * f) + EPS) * vec_ref[1:2, :]
        xo_ref[...] = x_ref[...] + (1.0 * vec_ref[4:5, :]) * y

    row = lambda i: (i, 0)
    full = lambda r: pl.BlockSpec((r, D), lambda i: (0, 0))
    return pl.pallas_call(
        body, name=name, grid=(S // tm,),
        in_specs=[pl.BlockSpec((tm, D), row), pl.BlockSpec((tm, 512), row), pl.BlockSpec((tm, D), row),
                  pl.BlockSpec((tm, D), lambda i: (i, 2)), pl.BlockSpec((tm, D), lambda i: (i, 3)),
                  full(8), full(512), full(D), full(D)],
        out_specs=[pl.BlockSpec((tm, D), row)] * 5,
        out_shape=[jax.ShapeDtypeStruct((S, D), F32)] + [jax.ShapeDtypeStruct((S, D), BF16)] * 4,
        compiler_params=_cp("parallel"),
    )(x, ao, hg, rest, rest, vec, w_att_o, w_rec_o, w_out)


def mix_out_bwd(dxo, f, att, rec, rest, h, vec, w_att_o, w_rec_o, w_out, name, tm=512):
    S = dxo.shape[0]
    tm = min(tm, S)

    def body(dxo_ref, f_ref, att_ref, rec_ref, yr_ref, ga_ref, gr_ref, h_ref, vec_ref, wa_ref, wr_ref, wo_ref,
             df_ref, da_ref, dr_ref, dao_ref, dh_ref, d3_ref, vacc_ref):
        @pl.when(pl.program_id(0) == 0)
        def _():
            vacc_ref[...] = jnp.zeros_like(vacc_ref)

        df = _post_norm_bwd(dxo_ref[...], f_ref[...].astype(F32), 1.0, vec_ref, vacc_ref).astype(BF16)
        df_ref[...] = df
        dm = _dot_nt(df, wo_ref[...])
        sa = _sigmoid(ga_ref[...].astype(F32))
        sr = _sigmoid(gr_ref[...].astype(F32))
        d_att = (dm * sa).astype(BF16)
        d_rec = (dm * sr).astype(BF16)
        da_ref[...] = d_att
        dr_ref[...] = d_rec
        d3_ref[1] = (dm * att_ref[...].astype(F32) * (sa * (1.0 - sa))).astype(BF16)
        d3_ref[2] = (dm * rec_ref[...].astype(F32) * (sr * (1.0 - sr))).astype(BF16)
        dao_ref[...] = _dot_nt(d_att, wa_ref[...]).astype(BF16)
        d_hg = _dot_nt(d_rec, wr_ref[...])
        yr = yr_ref[...].astype(F32)
        t = jnp.tanh(_GK * (yr + 0.044715 * yr * yr * yr))
        dh_ref[...] = d_hg * (0.5 * yr * (1.0 + t))
        gelu_grad = 0.5 * (1.0 + t) + 0.5 * yr * (1.0 - t * t) * _GK * (1.0 + 3.0 * 0.044715 * yr * yr)
        d3_ref[0] = (d_hg * h_ref[...] * gelu_grad).astype(BF16)

    row = lambda i: (i, 0)
    full = lambda r: pl.BlockSpec((r, D), lambda i: (0, 0))
    return pl.pallas_call(
        body, name=name, grid=(S // tm,),
        in_specs=[pl.BlockSpec((tm, D), row)] * 4
        + [pl.BlockSpec((tm, D), lambda i: (i, 1)), pl.BlockSpec((tm, D), lambda i: (i, 2)),
           pl.BlockSpec((tm, D), lambda i: (i, 3)), pl.BlockSpec((tm, D), row),
           full(8), full(512), full(D), full(D)],
        out_specs=[pl.BlockSpec((tm, D), row)] * 3
        + [pl.BlockSpec((tm, 512), row), pl.BlockSpec((tm, D), row),
           pl.BlockSpec((3, tm, D), lambda i: (0, i, 0)), pl.BlockSpec((8, D), lambda i: (0, 0))],
        out_shape=[jax.ShapeDtypeStruct((S, D), BF16)] * 3
        + [jax.ShapeDtypeStruct((S, 512), BF16), jax.ShapeDtypeStruct((S, D), F32),
           jax.ShapeDtypeStruct((3, S, D), BF16), jax.ShapeDtypeStruct((8, D), F32)],
        compiler_params=_cp("arbitrary"),
    )(dxo, f, att, rec, rest, rest, rest, h, vec, w_att_o, w_rec_o, w_out)


def dw_in(h, dq, dkv, dxr, d3, name, tk=1024, tn=512):
    S = h.shape[0]
    tk = min(tk, S)
    nk = S // tk

    def body(h_ref, dq_ref, dkv_ref, dxr_ref, d3_ref, o_ref, acc):
        j, k = pl.program_id(0), pl.program_id(1)

        @pl.when(k == 0)
        def _():
            acc[...] = jnp.zeros_like(acc)

        @pl.when(j == 0)
        def _():
            acc[...] += _dot_tn(h_ref[pl.ds(pl.multiple_of(k * tk, 16), tk), :], dq_ref[...])

        @pl.when((j >= 1) & (j < 3))
        def _():
            acc[...] += _dot_tn(h_ref[pl.ds(pl.multiple_of(k * tk, 16), tk), :], dkv_ref[...])

        @pl.when((j >= 3) & (j < 5))
        def _():
            acc[...] += _dot_tn(h_ref[pl.ds(pl.multiple_of(k * tk, 16), tk), :], dxr_ref[...])

        @pl.when(j >= 5)
        def _():
            acc[...] += _dot_tn(h_ref[pl.ds(pl.multiple_of(k * tk, 16), tk), :], d3_ref[...])

        @pl.when(k == nk - 1)
        def _():
            o_ref[...] = acc[...].astype(BF16)

    use = lambda j, k, lo, hi: jnp.where((j >= lo) & (j < hi), k, 0)
    g3 = lambda j: jnp.clip(j - 5, 0, 5)
    return pl.pallas_call(
        body, name=name, grid=(PW // tn, nk),
        in_specs=[pl.BlockSpec((S, D), lambda j, k: (0, 0), pipeline_mode=pl.Buffered(1)),
                  pl.BlockSpec((None, tk, tn), lambda j, k: (0, use(j, k, 0, 1), 0)),
                  pl.BlockSpec((None, tk, tn), lambda j, k: (jnp.clip(j - 1, 0, 1), use(j, k, 1, 3), 0)),
                  pl.BlockSpec((tk, tn), lambda j, k: (use(j, k, 3, 5), jnp.clip(j - 3, 0, 1))),
                  pl.BlockSpec((None, tk, tn), lambda j, k: (g3(j) // 2, use(j, k, 5, 11), g3(j) % 2))],
        out_specs=pl.BlockSpec((D, tn), lambda j, k: (0, j)),
        out_shape=jax.ShapeDtypeStruct((D, PW), BF16),
        scratch_shapes=[pltpu.VMEM((D, tn), F32)],
        compiler_params=_cp("parallel", "arbitrary"),
    )(h, dq, dkv, dxr, d3)


def ada_fwd(c_all, w_ada, b_ada, name, tn=768):
    n = w_ada.shape[1]

    def body(c_ref, w_ref, b_ref, o_ref):
        cv = c_ref[...]
        ca = (cv * _sigmoid(cv)).astype(BF16)
        o_ref[...] = _dot(ca, w_ref[...].astype(BF16)) + b_ref[...]

    return pl.pallas_call(
        body, name=name, grid=(n // tn,),
        in_specs=[pl.BlockSpec((8, D), lambda j: (0, 0)), pl.BlockSpec((D, tn), lambda j: (0, j)),
                  pl.BlockSpec((1, tn), lambda j: (0, j))],
        out_specs=pl.BlockSpec((8, tn), lambda j: (0, j)),
        out_shape=jax.ShapeDtypeStruct((8, n), F32),
        compiler_params=_cp("parallel"),
    )(c_all, w_ada, b_ada)


def ada_bwd(c_all_t, dmod, name, tn=768):
    n = dmod.shape[1]

    def body(c_ref, d_ref, o_ref):
        cv = c_ref[...]
        ca = (cv * _sigmoid(cv)).astype(BF16)
        o_ref[...] = _dot(ca, d_ref[...].astype(BF16))

    return pl.pallas_call(
        body, name=name, grid=(n // tn,),
        in_specs=[pl.BlockSpec((D, 128), lambda j: (0, 0)), pl.BlockSpec((128, tn), lambda j: (0, j))],
        out_specs=pl.BlockSpec((D, tn), lambda j: (0, j)),
        out_shape=jax.ShapeDtypeStruct((D, n), F32),
        compiler_params=_cp("parallel"),
    )(c_all_t, dmod)


def _row_tile(rows, cols, itemsize=4, budget=1536 * 1024):
    best = None
    for t in range(8, rows + 1, 8):
        if rows % t == 0 and t * cols * itemsize <= budget:
            best = t
    return rows if best is None else best


def sum_lead(parts, name, out_dtype=F32):
    n, R, C = parts.shape
    tr = _row_tile(R, C * n)

    def body(p_ref, o_ref):
        acc = p_ref[0].astype(F32)
        for k in range(1, n):
            acc = acc + p_ref[k].astype(F32)
        o_ref[...] = acc.astype(out_dtype)

    return pl.pallas_call(
        body, name=name, grid=(R // tr,),
        in_specs=[pl.BlockSpec((n, tr, C), lambda i: (0, i, 0))],
        out_specs=pl.BlockSpec((tr, C), lambda i: (i, 0)),
        out_shape=jax.ShapeDtypeStruct((R, C), out_dtype),
        compiler_params=_cp("parallel"),
    )(parts)


def adamw(w, g, m, v, name, emit_g=False):
    R, C = w.shape
    tr = _row_tile(R, C * 8, budget=24 * 1024 * 1024)

    def body(w_ref, g_ref, m_ref, v_ref, d_ref, mo_ref, vo_ref, *go_ref):
        gv = g_ref[...]
        if emit_g:
            go_ref[0][...] = gv
        mn = ADAM_B1 * m_ref[...] + (1.0 - ADAM_B1) * gv
        vn = ADAM_B2 * v_ref[...] + (1.0 - ADAM_B2) * (gv * gv)
        m_hat = mn / (1.0 - ADAM_B1 ** ADAM_STEP)
        v_hat = vn / (1.0 - ADAM_B2 ** ADAM_STEP)
        d_ref[...] = -ADAM_LR * (m_hat / (jnp.sqrt(v_hat) + ADAM_EPS) + ADAM_WD * w_ref[...])
        mo_ref[...] = mn
        vo_ref[...] = vn

    spec = pl.BlockSpec((tr, C), lambda i: (i, 0))
    return pl.pallas_call(
        body, name=name, grid=(R // tr,),
        in_specs=[spec] * 4, out_specs=[spec] * (4 if emit_g else 3),
        out_shape=[jax.ShapeDtypeStruct((R, C), F32)] * (4 if emit_g else 3),
        compiler_params=_cp("parallel"),
    )(w, g, m, v)


def _mesh_pos():
    return lax.axis_index("x"), lax.axis_index("y"), lax.axis_index("c")


def _other_chips(mx, my):
    return [(1 - mx, my), (mx, 1 - my), (1 - mx, 1 - my)]


def ag_small(x, name):
    R = x.shape[0]

    def body(x_ref, out_ref, send_sems, recv_sems, local_sem):
        mx, my, mc = _mesh_pos()
        me, sibling = (mx, my, mc), (mx, my, 1 - mc)
        chips = _other_chips(mx, my)

        def slot(px, py, pc):
            return out_ref.at[4 * px + 2 * py + pc]

        def copy(k, block, to, src=None):
            return pltpu.make_async_remote_copy(
                src_ref=slot(*block) if src is None else src, dst_ref=slot(*block),
                send_sem=send_sems.at[k], recv_sem=recv_sems.at[k], device_id=to, device_id_type=MESH)

        mine = pltpu.make_async_copy(x_ref, slot(*me), local_sem)
        mine.start()
        first = [copy(0, me, sibling, src=x_ref)]
        first += [copy(1 + j, me, (*chip, mc), src=x_ref) for j, chip in enumerate(chips)]
        for cp in first:
            cp.start()
        passed = [copy(4 + j, (*chip, mc), sibling) for j, chip in enumerate(chips)]
        for j, chip in enumerate(chips):
            copy(1 + j, (*chip, mc), me).wait_recv()
            passed[j].start()
        copy(0, sibling, me).wait_recv()
        for j, chip in enumerate(chips):
            copy(4 + j, (*chip, 1 - mc), me).wait_recv()
        for cp in first + passed:
            cp.wait_send()
        mine.wait()

    return pl.pallas_call(
        body, name=name,
        out_shape=jax.ShapeDtypeStruct((N_DEV, R, 128), F32),
        in_specs=[pl.BlockSpec(memory_space=pltpu.VMEM)],
        out_specs=pl.BlockSpec(memory_space=pltpu.VMEM),
        scratch_shapes=[pltpu.SemaphoreType.DMA((7,)), pltpu.SemaphoreType.DMA((7,)), pltpu.SemaphoreType.DMA],
        compiler_params=pltpu.CompilerParams(vmem_limit_bytes=VMEM_LIMIT),
    )(x)


BIG = (("ffn1_w_gu", "col", D, PW), ("ffn1_w_down", "row", FF, D), ("w_in", "col", D, PW),
       ("w_att_o", "col", 512, D), ("w_rec_o", "row", D, D), ("w_out", "row", D, D),
       ("ffn2_w_gu", "col", D, PW), ("ffn2_w_down", "row", FF, D))
NBIG = len(BIG)


def _shard_shape(kind, R, C):
    return (R, C // 4) if kind == "col" else (R // 4, C)


def _region(ref, kind, R, C, q, half, t, tr):
    sr, sc = _shard_shape(kind, R, C)
    if kind == "col":
        return ref.at[pl.ds(pl.multiple_of(half * (R // 2) + t * tr, 16), tr), pl.ds(q * sc, sc)]
    return ref.at[pl.ds(pl.multiple_of(q * sr + t * tr, 16), tr), pl.ds(half * (C // 2), C // 2)]


def ag_local(w, kind, R, C, p_arr, name, after=()):
    sr, sc = _shard_shape(kind, R, C)
    tr = _row_tile(sr, sc, budget=2 * 1024 * 1024)
    nt = sr // tr
    after = list(after)

    def body(p_ref, w_ref, *rest):
        rest[-1][...] = w_ref[...].astype(BF16)

    if kind == "col":
        o_spec = pl.BlockSpec((tr, sc), lambda i, p: (i, p[0]))
    else:
        o_spec = pl.BlockSpec((tr, sc), lambda i, p: (p[0] * nt + i, 0))
    return pl.pallas_call(
        body, name=name,
        grid_spec=pltpu.PrefetchScalarGridSpec(
            num_scalar_prefetch=1, grid=(nt,),
            in_specs=[pl.BlockSpec((tr, sc), lambda i, p: (i, 0))] + [ANY] * len(after), out_specs=o_spec),
        out_shape=jax.ShapeDtypeStruct((R, C), BF16),
        compiler_params=_cp("parallel"),
    )(p_arr, w, *after)


HBM_SPEC = pl.BlockSpec(memory_space=pltpu.HBM)
SEM_SPEC = pl.BlockSpec(memory_space=pltpu.SEMAPHORE)


def _ag_sems(geoms):
    return sum(6 if both else 3 for (_, _, _, both) in geoms)


def _ag_copies(fulls, geoms, ssem, rsem, mx, my, mc, q, h):
    chips = _other_chips(mx, my)
    out, base = [], 0
    for w, (kind, R, C, both) in enumerate(geoms):
        sr, sc = _shard_shape(kind, R, C)
        hr = sr // 2 if kind == "col" else sr
        reg = _region(fulls[w], kind, R, C, q, h, 0, hr)
        out.append([pltpu.make_async_remote_copy(
            src_ref=reg, dst_ref=reg, send_sem=ssem.at[base + 3 * t + k], recv_sem=rsem.at[base + 3 * t + k],
            device_id=(*chips[k], mc if t == 0 else 1 - mc), device_id_type=MESH)
            for t in range(2 if both else 1) for k in range(3)])
        base += 6 if both else 3
    return out


def ag_start(fulls, geoms, after, name):
    n = len(fulls)
    after = list(after)
    m = len(after)

    def body(*refs):
        ssem, rsem = refs[n + m:n + m + 2]
        outs, token = refs[n + m + 2:2 * n + m + 2], refs[2 * n + m + 2]
        mx, my, mc = _mesh_pos()
        p = 2 * mx + my
        col = [w for w, g in enumerate(geoms) if g[0] == "col"]
        row = [w for w, g in enumerate(geoms) if g[0] == "row"]
        for q in range(4):
            @pl.when(p == q)
            def _(q=q):
                cps = _ag_copies(outs, geoms, ssem, rsem, mx, my, mc, q, mc)
                for w in col:
                    for cp in cps[w]:
                        cp.start()
        for h in range(2):
            @pl.when(mc == h)
            def _(h=h):
                cps = _ag_copies(outs, geoms, ssem, rsem, mx, my, mc, p, h)
                for w in row:
                    for cp in cps[w]:
                        cp.start()
        token[...] = jnp.zeros_like(token)

    res = pl.pallas_call(
        body, name=name,
        out_shape=[pltpu.SemaphoreType.DMA((_ag_sems(geoms),)), pltpu.SemaphoreType.DMA((_ag_sems(geoms),))]
        + [pltpu.HBM(a.shape, a.dtype) for a in fulls] + [jax.ShapeDtypeStruct((8, 128), F32)],
        in_specs=[HBM_SPEC] * n + [ANY] * m,
        out_specs=[SEM_SPEC, SEM_SPEC] + [HBM_SPEC] * n + [pl.BlockSpec(memory_space=pltpu.VMEM)],
        input_output_aliases={w: 2 + w for w in range(n)},
        compiler_params=pltpu.CompilerParams(has_side_effects=pltpu.SideEffectType.DATAFLOW_SIDE_EFFECTING),
    )(*[pltpu.with_memory_space_constraint(a, pltpu.HBM) for a in fulls], *after)
    return res[0], res[1], list(res[2:2 + n]), res[2 + n]


def ag_wait(fulls, geoms, ssem, rsem, after, name):
    n = len(fulls)
    after = list(after) if isinstance(after, (list, tuple)) else [after]

    def body(*refs):
        ins, ssem_ref, rsem_ref = refs[:n], refs[n], refs[n + 1]
        mx, my, mc = _mesh_pos()
        for cps in _ag_copies(ins, geoms, ssem_ref, rsem_ref, mx, my, mc, 0, 0):
            for cp in cps:
                cp.wait_send()
                cp.wait_recv()

    return list(pl.pallas_call(
        body, name=name,
        out_shape=[pltpu.HBM(a.shape, a.dtype) for a in fulls],
        in_specs=[HBM_SPEC] * n + [SEM_SPEC, SEM_SPEC] + [ANY] * len(after),
        out_specs=[HBM_SPEC] * n,
        input_output_aliases={w: w for w in range(n)},
        compiler_params=pltpu.CompilerParams(has_side_effects=pltpu.SideEffectType.DATAFLOW_SIDE_EFFECTING),
    )(*fulls, ssem, rsem, *after))


def ag_forward(full, kind, R, C, name):
    sr, sc = _shard_shape(kind, R, C)
    hr, hc = (sr // 2, sc) if kind == "col" else (sr, sc // 2)
    tr = _row_tile(hr, hc, itemsize=2, budget=2 * 1024 * 1024)
    nt = hr // tr
    total = 3 * nt

    def body(src_ref, full_ref, stage, lsem, ssem, rsem):
        step = pl.program_id(0) * nt + pl.program_id(1)
        par = step % 2
        mx, my, mc = _mesh_pos()

        def load(s, q, h, t):
            return pltpu.make_async_copy(_region(src_ref, kind, R, C, q, h, t, tr), stage.at[s], lsem.at[s])

        def push(s, q, h, t):
            return pltpu.make_async_remote_copy(src_ref=stage.at[s], dst_ref=_region(full_ref, kind, R, C, q, h, t, tr),
                                                send_sem=ssem.at[s], recv_sem=rsem, device_id=(mx, my, 1 - mc),
                                                device_id_type=MESH)

        def for_tile(stp, fn):
            q_k = _partner_chip(stp // nt, 2 * mx + my)
            if kind == "col":
                for q in range(4):
                    @pl.when(q_k == q)
                    def _(q=q):
                        fn(q, mc, stp % nt)
            else:
                for h in range(2):
                    @pl.when(mc == h)
                    def _(h=h):
                        fn(q_k, h, stp % nt)

        @pl.when(step == 0)
        def _():
            for_tile(step, lambda q, h, t: load(0, q, h, t).start())

        load(par, 0, 0, 0).wait()
        for_tile(step, lambda q, h, t: push(par, q, h, t).start())

        @pl.when(step + 1 < total)
        def _():
            @pl.when(step >= 1)
            def _():
                push(1 - par, 0, 0, 0).wait_send()
            for_tile(step + 1, lambda q, h, t: load(1 - par, q, h, t).start())

        @pl.when(step == total - 1)
        def _():
            push(par, 0, 0, 0).wait_send()
            push(1 - par, 0, 0, 0).wait_send()
            three = full_ref.at[pl.ds(0, hr), pl.ds(0, 3 * hc)] if kind == "col" else full_ref.at[pl.ds(0, 3 * hr), pl.ds(0, hc)]
            pltpu.make_async_remote_copy(src_ref=three, dst_ref=three, send_sem=ssem.at[0], recv_sem=rsem,
                                         device_id=(mx, my, 1 - mc), device_id_type=MESH).wait_recv()

    return pl.pallas_call(
        body, name=name, grid=(3, nt),
        in_specs=[ANY], out_specs=ANY,
        out_shape=jax.ShapeDtypeStruct((R, C), BF16),
        scratch_shapes=[pltpu.VMEM((2, tr, hc), BF16), pltpu.SemaphoreType.DMA((2,)), pltpu.SemaphoreType.DMA((2,)),
                        pltpu.SemaphoreType.DMA],
        input_output_aliases={0: 0},
        compiler_params=_cp("arbitrary", "arbitrary"),
    )(full)


def _half_shape(kind, R, C):
    return (R // 2, C) if kind == "col" else (R, C // 2)


def _piece_shape(kind, R, C):
    return (R // 2, C // 4) if kind == "col" else (R // 4, C // 2)


def pair_push(g, kind, c_arr, name):
    R, C = g.shape
    hr, hc = _half_shape(kind, R, C)
    tr = _row_tile(hr, hc, itemsize=2, budget=4 * 1024 * 1024)
    nt = hr // tr

    def body(c_ref, g_ref, out_ref, stage, ssem, rsem):
        i = pl.program_id(0)
        slot = i % 2
        mx, my, mc = _mesh_pos()

        def push(s, t):
            return pltpu.make_async_remote_copy(
                src_ref=stage.at[s], dst_ref=out_ref.at[pl.ds(pl.multiple_of(t * tr, 16), tr)],
                send_sem=ssem.at[s], recv_sem=rsem, device_id=(mx, my, 1 - mc), device_id_type=MESH)

        @pl.when(i >= 2)
        def _():
            push(slot, 0).wait_send()

        stage[slot] = g_ref[...]
        push(slot, i).start()

        @pl.when(i == nt - 1)
        def _():
            push(slot, 0).wait_send()
            if nt >= 2:
                push(1 - slot, 0).wait_send()
            pltpu.make_async_remote_copy(src_ref=out_ref, dst_ref=out_ref, send_sem=ssem.at[0], recv_sem=rsem,
                                         device_id=(mx, my, 1 - mc), device_id_type=MESH).wait_recv()

    if kind == "col":
        g_spec = pl.BlockSpec((tr, hc), lambda i, c: ((1 - c[0]) * nt + i, 0))
    else:
        g_spec = pl.BlockSpec((tr, hc), lambda i, c: (i, 1 - c[0]))
    return pl.pallas_call(
        body, name=name,
        grid_spec=pltpu.PrefetchScalarGridSpec(
            num_scalar_prefetch=1, grid=(nt,), in_specs=[g_spec], out_specs=ANY,
            scratch_shapes=[pltpu.VMEM((2, tr, hc), BF16), pltpu.SemaphoreType.DMA((2,)), pltpu.SemaphoreType.DMA]),
        out_shape=jax.ShapeDtypeStruct((hr, hc), BF16),
        compiler_params=_cp("arbitrary"),
    )(c_arr, g)


def _partner_chip(k, p):
    return p ^ jnp.where(k == 0, 2, jnp.where(k == 1, 1, jnp.where(k == 2, 3, 0)))


def pair_add(g, got, kind, cp_arr, name):
    R, C = g.shape
    pr, pc = _piece_shape(kind, R, C)
    tr = _row_tile(pr, pc, itemsize=2, budget=2 * 1024 * 1024)
    nt = pr // tr

    def body(cp_ref, g_ref, got_ref, ps_ref, rb_ref):
        tile = (g_ref[...].astype(F32) + got_ref[...].astype(F32)).astype(BF16)
        ps_ref[...] = tile

        @pl.when(pl.program_id(1) == cp_ref[1])
        def _():
            rb_ref[...] = tile

    if kind == "col":
        g_spec = pl.BlockSpec((tr, pc), lambda i, q, cp: (cp[0] * nt + i, q))
        got_spec = pl.BlockSpec((tr, pc), lambda i, q, cp: (i, q))
    else:
        g_spec = pl.BlockSpec((tr, pc), lambda i, q, cp: (q * nt + i, cp[0]))
        got_spec = pl.BlockSpec((tr, pc), lambda i, q, cp: (q * nt + i, 0))
    return pl.pallas_call(
        body, name=name,
        grid_spec=pltpu.PrefetchScalarGridSpec(
            num_scalar_prefetch=1, grid=(nt, 4), in_specs=[g_spec, got_spec],
            out_specs=[pl.BlockSpec((None, tr, pc), lambda i, q, cp: (q, i, 0)),
                       pl.BlockSpec((None, tr, pc), lambda i, q, cp: (cp[1], i, 0))]),
        out_shape=[jax.ShapeDtypeStruct((4, pr, pc), BF16)] * 2,
        compiler_params=_cp("arbitrary", "arbitrary"),
    )(cp_arr, g, got)


def _rs_copies(ps, rb, ssem, rsem, mx, my, mc):
    p = 2 * mx + my
    out = []
    for w in range(len(ps)):
        for k, chip in enumerate(_other_chips(mx, my)):
            out.append(pltpu.make_async_remote_copy(
                src_ref=ps[w].at[2 * chip[0] + chip[1]], dst_ref=rb[w].at[p], send_sem=ssem.at[3 * w + k],
                recv_sem=rsem.at[3 * w + k], device_id=(*chip, mc), device_id_type=MESH))
    return out


def rs_start(ps, rb, after, name):
    n = len(ps)
    after = list(after)
    m = len(after)

    def body(*refs):
        ssem, rsem = refs[2 * n + m:2 * n + m + 2]
        ps_o = refs[2 * n + m + 2:3 * n + m + 2]
        rb_o = refs[3 * n + m + 2:4 * n + m + 2]
        token = refs[4 * n + m + 2]
        for cp in _rs_copies(ps_o, rb_o, ssem, rsem, *_mesh_pos()):
            cp.start()
        token[...] = jnp.zeros_like(token)

    both = list(ps) + list(rb)
    res = pl.pallas_call(
        body, name=name,
        out_shape=[pltpu.SemaphoreType.DMA((3 * n,)), pltpu.SemaphoreType.DMA((3 * n,))]
        + [pltpu.HBM(a.shape, a.dtype) for a in both] + [jax.ShapeDtypeStruct((8, 128), F32)],
        in_specs=[HBM_SPEC] * (2 * n) + [ANY] * m,
        out_specs=[SEM_SPEC, SEM_SPEC] + [HBM_SPEC] * (2 * n) + [pl.BlockSpec(memory_space=pltpu.VMEM)],
        input_output_aliases={w: 2 + w for w in range(2 * n)},
        compiler_params=pltpu.CompilerParams(has_side_effects=pltpu.SideEffectType.DATAFLOW_SIDE_EFFECTING),
    )(*[pltpu.with_memory_space_constraint(a, pltpu.HBM) for a in both], *after)
    return res[0], res[1], list(res[2:2 + n]), list(res[2 + n:2 + 2 * n]), res[2 + 2 * n]


def rs_wait(ps, rb, ssem, rsem, after, name):
    n = len(ps)
    after = list(after)
    m = len(after)

    def body(*refs):
        ps_i, rb_i = refs[:n], refs[n:2 * n]
        ssem_ref, rsem_ref = refs[2 * n], refs[2 * n + 1]
        for cp in _rs_copies(ps_i, rb_i, ssem_ref, rsem_ref, *_mesh_pos()):
            cp.wait_send()
            cp.wait_recv()

    both = list(ps) + list(rb)
    res = pl.pallas_call(
        body, name=name,
        out_shape=[pltpu.HBM(a.shape, a.dtype) for a in both],
        in_specs=[HBM_SPEC] * (2 * n) + [SEM_SPEC, SEM_SPEC] + [ANY] * m,
        out_specs=[HBM_SPEC] * (2 * n),
        input_output_aliases={w: w for w in range(2 * n)},
        compiler_params=pltpu.CompilerParams(has_side_effects=pltpu.SideEffectType.DATAFLOW_SIDE_EFFECTING),
    )(*both, ssem, rsem, *after)
    return list(res[n:])


def _slot_copies(blk, ssem, rsem):
    mx, my, mc = _mesh_pos()
    mine = blk.at[4 * mx + 2 * my + mc]
    peers = [(mx, my, 1 - mc)] + [(*chip, mc) for chip in _other_chips(mx, my)] \
        + [(*chip, 1 - mc) for chip in _other_chips(mx, my)]
    return [pltpu.make_async_remote_copy(src_ref=mine, dst_ref=mine, send_sem=ssem.at[k], recv_sem=rsem.at[k],
                                         device_id=peer, device_id_type=MESH) for k, peer in enumerate(peers)]


def slot_start(blk, name):
    def body(_, ssem, rsem, out, token):
        for cp in _slot_copies(out, ssem, rsem):
            cp.start()
        token[...] = jnp.zeros_like(token)

    return pl.pallas_call(
        body, name=name,
        out_shape=[pltpu.SemaphoreType.DMA((7,)), pltpu.SemaphoreType.DMA((7,)), pltpu.HBM(blk.shape, blk.dtype),
                   jax.ShapeDtypeStruct((8, 128), F32)],
        in_specs=[HBM_SPEC],
        out_specs=[SEM_SPEC, SEM_SPEC, HBM_SPEC, pl.BlockSpec(memory_space=pltpu.VMEM)],
        input_output_aliases={0: 2},
        compiler_params=pltpu.CompilerParams(has_side_effects=pltpu.SideEffectType.DATAFLOW_SIDE_EFFECTING),
    )(pltpu.with_memory_space_constraint(blk, pltpu.HBM))


def slot_wait(blk, ssem, rsem, after, name):
    after = list(after)

    def body(blk_ref, ssem_ref, rsem_ref, *_):
        for cp in _slot_copies(blk_ref, ssem_ref, rsem_ref):
            cp.wait_send()
            cp.wait_recv()

    return pl.pallas_call(
        body, name=name,
        out_shape=pltpu.HBM(blk.shape, blk.dtype),
        in_specs=[HBM_SPEC, SEM_SPEC, SEM_SPEC] + [ANY] * len(after),
        out_specs=HBM_SPEC,
        input_output_aliases={0: 0},
        compiler_params=pltpu.CompilerParams(has_side_effects=pltpu.SideEffectType.DATAFLOW_SIDE_EFFECTING),
    )(blk, ssem, rsem, *after)


def sum_share(parts, kind, R, C, name):
    _, pr, pc = parts.shape
    sr, sc = _shard_shape(kind, R, C)
    tr = _row_tile(pr, pc * 4, budget=16 * 1024 * 1024)
    nt = pr // tr

    def body(p_ref, fin_ref, stage, lsem, ssem, rsem):
        i = pl.program_id(0)
        slot = i % 2
        mx, my, mc = _mesh_pos()

        def region(h, t):
            r0 = pl.multiple_of(t * tr, 8)
            if kind == "col":
                return fin_ref.at[pl.ds(pl.multiple_of(h * pr + r0, 8), tr)]
            return fin_ref.at[pl.ds(r0, tr), pl.ds(h * pc, pc)]

        def copies(s, h, t):
            return (pltpu.make_async_copy(stage.at[s], region(h, t), lsem.at[s]),
                    pltpu.make_async_remote_copy(src_ref=stage.at[s], dst_ref=region(h, t), send_sem=ssem.at[s],
                                                 recv_sem=rsem, device_id=(mx, my, 1 - mc), device_id_type=MESH))

        def wait_sent(s):
            loc, rem = copies(s, 0, 0)
            loc.wait()
            rem.wait_send()

        @pl.when(i >= 2)
        def _():
            wait_sent(slot)

        acc = p_ref[0].astype(F32)
        for k in range(1, 4):
            acc = acc + p_ref[k].astype(F32)
        stage[slot] = acc
        if kind == "col":
            for cp in copies(slot, mc, i):
                cp.start()
        else:
            for h in range(2):
                @pl.when(mc == h)
                def _(h=h):
                    for cp in copies(slot, h, i):
                        cp.start()

        @pl.when(i == nt - 1)
        def _():
            wait_sent(slot)
            if nt >= 2:
                wait_sent(1 - slot)
            half = fin_ref.at[pl.ds(0, pr), pl.ds(0, pc)]
            pltpu.make_async_remote_copy(src_ref=half, dst_ref=half, send_sem=ssem.at[0], recv_sem=rsem,
                                         device_id=(mx, my, 1 - mc), device_id_type=MESH).wait_recv()

    return pl.pallas_call(
        body, name=name, grid=(nt,),
        in_specs=[pl.BlockSpec((4, tr, pc), lambda i: (0, i, 0))],
        out_specs=ANY,
        out_shape=jax.ShapeDtypeStruct((sr, sc), F32),
        scratch_shapes=[pltpu.VMEM((2, tr, pc), F32), pltpu.SemaphoreType.DMA((2,)), pltpu.SemaphoreType.DMA((2,)),
                        pltpu.SemaphoreType.DMA],
        compiler_params=_cp("arbitrary"),
    )(parts)


def _pack(parts, rows):
    flat = []
    for a in parts:
        a = jnp.ravel(a).astype(F32)
        flat.append(jnp.pad(a, (0, (-a.shape[0]) % 128)))
    v = jnp.concatenate(flat)
    return jnp.pad(v, (0, rows * 128 - v.shape[0])).reshape(rows, 128)


def _unpack(block, shapes):
    lead = block.shape[:-2]
    v = block.reshape(lead + (-1,))
    out, off = [], 0
    for shp in shapes:
        n = int(np.prod(shp))
        out.append(v[..., off:off + n].reshape(lead + tuple(shp)))
        off += n + (-n) % 128
    return out


def _block_diag4(w):
    w4 = w.reshape(4, 4, 64, 64)
    eye = jnp.eye(4, dtype=w.dtype)
    return (w4[:, :, :, None, :] * eye[None, :, None, :, None]).reshape(4, 256, 256)


def _diag_blocks(bd):
    b5 = bd.reshape(4, 4, 64, 4, 64)
    return jnp.stack([b5[:, i, :, i, :] for i in range(4)], axis=1).reshape(16, 64, 64)


def _bias_window(rel_bias):
    m = (np.arange(768) + 127) % 768 - 127
    w = rel_bias[:, np.clip(512 - m, -128, 128) + 128]
    win = jnp.tile(w, (1, 128))[:, :128 * 767].reshape(8, 128, 767)[:, :, :WIN]
    qh = np.arange(128)[:, None] // CHUNK
    kc = np.arange(WIN)[None, :] // CHUNK
    valid = (kc >= qh) & (kc <= qh + 8)
    return jnp.where(jnp.asarray(valid)[None], win, NEG)


SMALL = ("b_ada", "norm_pre", "norm_post", "rel_bias", "conv_w", "conv_b", "lru_wa", "lru_ba", "lru_wx",
         "lru_bx", "lru_lambda")
WEIGHTS = ("w_ada", "b_ada", "norm_pre", "norm_post", "ffn1_w_gu", "ffn1_w_down", "w_in", "rel_bias", "conv_w",
           "conv_b", "lru_wa", "lru_ba", "lru_wx", "lru_bx", "lru_lambda", "w_att_o", "w_rec_o", "w_out",
           "ffn2_w_gu", "ffn2_w_down")


def kernel(x, c, w_ada, b_ada, norm_pre, norm_post, ffn1_w_gu, ffn1_w_down, w_in, rel_bias, conv_w, conv_b, lru_wa, lru_ba, lru_wx, lru_bx, lru_lambda, w_att_o, w_rec_o, w_out, ffn2_w_gu, ffn2_w_down, loss_target, m_w_ada, m_b_ada, m_norm_pre, m_norm_post, m_ffn1_w_gu, m_ffn1_w_down, m_w_in, m_rel_bias, m_conv_w, m_conv_b, m_lru_wa, m_lru_ba, m_lru_wx, m_lru_bx, m_lru_lambda, m_w_att_o, m_w_rec_o, m_w_out, m_ffn2_w_gu, m_ffn2_w_down, v_w_ada, v_b_ada, v_norm_pre, v_norm_post, v_ffn1_w_gu, v_ffn1_w_down, v_w_in, v_rel_bias, v_conv_w, v_conv_b, v_lru_wa, v_lru_ba, v_lru_wx, v_lru_bx, v_lru_lambda, v_w_att_o, v_w_rec_o, v_w_out, v_ffn2_w_gu, v_ffn2_w_down):
    W = dict(w_ada=w_ada, b_ada=b_ada, norm_pre=norm_pre, norm_post=norm_post, ffn1_w_gu=ffn1_w_gu,
             ffn1_w_down=ffn1_w_down, w_in=w_in, rel_bias=rel_bias, conv_w=conv_w, conv_b=conv_b, lru_wa=lru_wa,
             lru_ba=lru_ba, lru_wx=lru_wx, lru_bx=lru_bx, lru_lambda=lru_lambda, w_att_o=w_att_o, w_rec_o=w_rec_o,
             w_out=w_out, ffn2_w_gu=ffn2_w_gu, ffn2_w_down=ffn2_w_down)
    M = dict(w_ada=m_w_ada, b_ada=m_b_ada, norm_pre=m_norm_pre, norm_post=m_norm_post, ffn1_w_gu=m_ffn1_w_gu,
             ffn1_w_down=m_ffn1_w_down, w_in=m_w_in, rel_bias=m_rel_bias, conv_w=m_conv_w, conv_b=m_conv_b,
             lru_wa=m_lru_wa, lru_ba=m_lru_ba, lru_wx=m_lru_wx, lru_bx=m_lru_bx, lru_lambda=m_lru_lambda,
             w_att_o=m_w_att_o, w_rec_o=m_w_rec_o, w_out=m_w_out, ffn2_w_gu=m_ffn2_w_gu, ffn2_w_down=m_ffn2_w_down)
    V = dict(w_ada=v_w_ada, b_ada=v_b_ada, norm_pre=v_norm_pre, norm_post=v_norm_post, ffn1_w_gu=v_ffn1_w_gu,
             ffn1_w_down=v_ffn1_w_down, w_in=v_w_in, rel_bias=v_rel_bias, conv_w=v_conv_w, conv_b=v_conv_b,
             lru_wa=v_lru_wa, lru_ba=v_lru_ba, lru_wx=v_lru_wx, lru_bx=v_lru_bx, lru_lambda=v_lru_lambda,
             w_att_o=v_w_att_o, w_rec_o=v_w_rec_o, w_out=v_w_out, ffn2_w_gu=v_ffn2_w_gu, ffn2_w_down=v_ffn2_w_down)
    mx, my, mc = _mesh_pos()
    p = 2 * mx + my
    e = 4 * mx + 2 * my + mc
    xs = x[0]

    c_arr = jnp.reshape(mc, (1,)).astype(jnp.int32)
    cp_arr = jnp.stack([mc, p]).astype(jnp.int32)
    p_arr = jnp.reshape(p, (1,)).astype(jnp.int32)
    direct = ("w_att_o", "w_rec_o", "w_out", "ffn2_w_gu", "ffn2_w_down")
    geoms = [(kind, R, C, n in direct) for (n, kind, R, C) in BIG]
    names = [b[0] for b in BIG]
    placed = [ag_local(W[n][0], kind, R, C, p_arr, "ag_local_" + n) for (n, kind, R, C) in BIG[:2]]

    def arrived(fly, lo, hi, ssem, rsem, after, tag):
        done = ag_wait(fly, geoms[lo:hi], ssem, rsem, after, "ag_wait_" + tag)
        return [a if both else ag_forward(a, kind, R, C, "ag_forward_" + n)
                for a, (kind, R, C, both), n in zip(done, geoms[lo:hi], names[lo:hi])]

    g1 = ag_small(_pack([c, norm_pre, norm_post, conv_w], 32), "ag_small_params")
    c_all, npre4, npost4, cw4 = _unpack(g1, [(D,), (3, 256), (3, 256), (4, 256)])
    chipwise = lambda a: jnp.moveaxis(a[0::2], 0, 1).reshape(a.shape[1], D)
    npre, npost, conv_full = chipwise(npre4), chipwise(npost4), chipwise(cw4)

    b_cols = lax.dynamic_slice(b_ada, (0, p * 2304), (1, 2304))
    mod_cols = ada_fwd(c_all, w_ada[0], b_cols, "ada_fwd")
    g2 = ag_small(mod_cols.reshape(144, 128), "ag_mod")
    mod_all = jnp.moveaxis(g2[0::2].reshape(4, 8, 2304), 0, 1).reshape(8, 9 * D)
    mod = lax.dynamic_index_in_dim(mod_all, e, 0, keepdims=False).reshape(3, 3, D)
    zeros3 = jnp.zeros((3, D), F32)
    vecs = [jnp.concatenate([npre[k:k + 1], npost[k:k + 1], mod[k], zeros3], axis=0) for k in range(3)]

    gu_s, gu_r, gu_fly, tok_gu = ag_start(placed[:1], geoms[:1], [g2], "ag_start_ffn1_gu")
    dn_s, dn_r, dn_fly, tok0 = ag_start(placed[1:2], geoms[1:2], [tok_gu], "ag_start_ffn1_down")
    placed += [ag_local(W[n][0], kind, R, C, p_arr, "ag_local_" + n, after=[tok0]) for (n, kind, R, C) in BIG[2:]]
    f1_gu, = arrived(gu_fly, 0, 1, gu_s, gu_r, placed[2:], "ffn1_gu")
    f1_dn, = arrived(dn_fly, 1, 2, dn_s, dn_r, f1_gu, "ffn1_down")
    mix_s, mix_r, mix_fly, tok1 = ag_start(placed[2:6], geoms[2:6], [f1_gu, f1_dn], "ag_start_mixer")
    ffn_s, ffn_r, ffn_fly, tok2 = ag_start(placed[6:], geoms[6:], [tok1], "ag_start_ffn2")
    wa_bd = _block_diag4(lru_wa[0]).astype(BF16)
    wx_bd = _block_diag4(lru_wx[0]).astype(BF16)
    pvec = jnp.concatenate([conv_full, conv_b, lru_ba, lru_bx, lru_lambda], axis=0)
    bias = _bias_window(rel_bias[0]).reshape(4, 256, WIN)

    x1, h1, g1_, u1, a1, f1 = ffn_fwd(xs, vecs[0] + tok2[0:1, 0:1], f1_gu, f1_dn, 0.5, "ffn1_fwd")
    win, wao, wro, wout = arrived(mix_fly, 2, 6, mix_s, mix_r, x1, "mixer")
    h2, qkv, rest = proj_fwd(x1, vecs[1], win, "proj_fwd")
    ao = attn_fwd(qkv, bias, "attn_fwd")
    hl, hg = lru_fwd(rest, pvec, wa_bd, wx_bd, "lru_fwd")
    x2, att, rec, mg, f2 = mix_out_fwd(x1, ao, hg, rest, vecs[1], wao, wro, wout, "mix_out_fwd")
    f2_gu, f2_dn = arrived(ffn_fly, 6, 8, ffn_s, ffn_r, x2, "ffn2")
    dy, h3, g3_, u3, a3, f3, lvec = ffn_fwd(x2, vecs[2], f2_gu, f2_dn, 0.5, "ffn2_fwd", tgt=loss_target[0])

    G, grads = {}, {}
    geo = {n: (kind, R, C) for (n, kind, R, C) in BIG}

    def reduce_begin(names, tag):
        ps, rb = [], []
        for n in names:
            got = pair_push(G[n], geo[n][0], c_arr, "rs_push_" + n)
            a, b = pair_add(G[n], got, geo[n][0], cp_arr, "rs_pair_sum_" + n)
            ps.append(a)
            rb.append(b)
        return rs_start(ps, rb, [], "rs_start_" + tag)

    def reduce_end(names, flight, after, tag):
        ssem, rsem, ps, rb, _ = flight
        for a, n in zip(rs_wait(ps, rb, ssem, rsem, after, "rs_wait_" + tag), names):
            grads[n] = sum_share(a, *geo[n], "rs_sum_share_" + n)[None]

    dx2, df3, dgu3, va2 = ffn_bwd(dy, x2, f3, g3_, u3, vecs[2], f2_gu, f2_dn, 0.5, "ffn2_bwd")
    G["ffn2_w_gu"] = mm_tn(h3, dgu3, "dw_ffn2_gu", D, 1408, 2048, a_resident=True)
    G["ffn2_w_down"] = mm_tn(a3, df3, "dw_ffn2_down", 1408, D, 2048)
    fly_ffn2 = reduce_begin(("ffn2_w_gu", "ffn2_w_down"), "ffn2")
    vec1 = vecs[1] + fly_ffn2[4][0:1, 0:1]
    df2, d_att, d_rec, dao, dhl, d3, va_out = mix_out_bwd(dx2, f2, att, rec, rest, hl, vec1, wao, wro, wout,
                                                          "mix_out_bwd")
    G["w_out"] = mm_tn(mg, df2, "dw_out", D, D, 1024)
    G["w_att_o"] = mm_tn(ao, d_att, "dw_att_o", 512, D, 1024)
    G["w_rec_o"] = mm_tn(hg, d_rec, "dw_rec_o", D, D, 1024)
    dq, db, dkv = attn_bwd(qkv, dao, bias, "attn_bwd")
    dxr, v_lru, dwa_bd, dwx_bd = lru_bwd(dhl, hl, rest, pvec, wa_bd, wx_bd, "lru_bwd")
    lru_w = jnp.concatenate([_diag_blocks(dwa_bd), _diag_blocks(dwx_bd)]).reshape(1, 1024, 128)
    lru_slots = lax.dynamic_update_slice(jnp.zeros((N_DEV, 1024, 128), F32), lru_w, (e, 0, 0))
    lw_s, lw_r, lw_fly, lw_tok = slot_start(lru_slots, "lru_w_start")
    dx1, va_in = proj_bwd(dq, dkv, dxr, d3, win, x1, dx2, vecs[1] + lw_tok[0:1, 0:1], "proj_bwd")
    G["w_in"] = dw_in(h2, dq, dkv, dxr, d3, "dw_in")
    fly_mix = reduce_begin(("w_in", "w_att_o", "w_rec_o", "w_out"), "mixer")
    vec0 = vecs[0] + fly_mix[4][0:1, 0:1]
    dx0, df1, dgu1, va0 = ffn_bwd(dx1, xs, f1, g1_, u1, vec0, f1_gu, f1_dn, 0.5, "ffn1_bwd")
    G["ffn1_w_gu"] = mm_tn(h1, dgu1, "dw_ffn1_gu", D, 1408, 2048, a_resident=True)
    G["ffn1_w_down"] = mm_tn(a1, df1, "dw_ffn1_down", 1408, D, 2048)
    fly_ffn1 = reduce_begin(("ffn1_w_gu", "ffn1_w_down"), "ffn1")
    reduce_end(("ffn2_w_gu", "ffn2_w_down"), fly_ffn2, [fly_ffn1[4]], "ffn2")
    reduce_end(("w_in", "w_att_o", "w_rec_o", "w_out"), fly_mix, [fly_ffn1[4], grads["ffn2_w_down"]], "mixer")

    va1 = va_out + va_in
    vas = (va0, va1, va2)
    dmod = jnp.stack([v[2:5] for v in vas])
    part = {"b_ada": dmod, "norm_pre": jnp.stack([v[0] for v in vas]), "norm_post": jnp.stack([v[1] for v in vas]),
            "rel_bias": bias_grad(db.reshape(8, 128, WIN), "bias_grad")[:, :257], "conv_w": v_lru[0:4], "conv_b": v_lru[4],
            "lru_ba": v_lru[5], "lru_bx": v_lru[6], "lru_lambda": v_lru[7]}
    full_shapes = {"b_ada": (9 * D,), "norm_pre": (3, D), "norm_post": (3, D), "rel_bias": (8, 257),
                   "conv_w": (4, D), "conv_b": (D,), "lru_wa": (16, 64, 64), "lru_ba": (D,),
                   "lru_wx": (16, 64, 64), "lru_bx": (D,), "lru_lambda": (D,)}
    gathered = [n for n in SMALL if n in part]
    g3 = ag_small(_pack([part[n] for n in gathered] + [lvec[0:1, 0:1]], 208), "ag_small_grads")
    summed = _unpack(sum_lead(g3, "sum_small_grads"), [full_shapes[n] for n in gathered] + [(1,)])
    red = dict(zip(gathered, summed[:-1]))
    loss = summed[-1][0]
    lru_all = slot_wait(lw_fly, lw_s, lw_r, [dx0], "lru_w_wait")
    red["lru_wa"], red["lru_wx"] = sum_lead(lru_all, "sum_lru_w").reshape(2, 16, 64, 64)
    cols = lambda a: lax.dynamic_slice(a, (0, p * 256), (a.shape[0], 256))
    grads.update({"b_ada": red["b_ada"][None], "norm_pre": cols(red["norm_pre"])[None],
                  "norm_post": cols(red["norm_post"])[None], "rel_bias": red["rel_bias"][None],
                  "conv_w": cols(red["conv_w"])[None], "conv_b": red["conv_b"][None], "lru_wa": red["lru_wa"][None],
                  "lru_ba": red["lru_ba"][None], "lru_wx": red["lru_wx"][None], "lru_bx": red["lru_bx"][None],
                  "lru_lambda": red["lru_lambda"][None]})

    dmod_all = g3[:, :72].reshape(8, 9 * D)
    dmod_cols = jnp.pad(lax.dynamic_slice(dmod_all, (0, p * 2304), (8, 2304)), ((0, 120), (0, 0)))
    c_all_t = jnp.pad(c_all.T, ((0, 0), (0, 120)))
    grads["w_ada"] = ada_bwd(c_all_t, dmod_cols, "ada_bwd")[None]

    delta, new_m, new_v = {}, {}, {}

    def update(n):
        shp = W[n].shape
        res = adamw(W[n][0], grads[n][0], M[n][0], V[n][0], "adamw_" + n, emit_g=n in geo)
        delta[n], new_m[n], new_v[n] = [a.reshape(shp) for a in res[:3]]
        if n in geo:
            grads[n] = res[3].reshape(shp)

    for n in ("w_ada", "ffn2_w_gu", "ffn2_w_down", "w_in", "w_att_o", "w_rec_o", "w_out"):
        update(n)
    packed = [_pack([src[n] for n in SMALL], 1168) for src in (W, grads, M, V)]
    outs = adamw(*packed, "adamw_small")
    for dst, blk in zip((delta, new_m, new_v), outs):
        for n, a in zip(SMALL, _unpack(blk, [W[n].shape for n in SMALL])):
            dst[n] = a
    reduce_end(("ffn1_w_gu", "ffn1_w_down"), fly_ffn1,
               [outs[0], delta["w_ada"], delta["ffn2_w_gu"], delta["ffn2_w_down"], delta["w_in"], delta["w_out"]], "ffn1")
    for n in ("ffn1_w_gu", "ffn1_w_down"):
        update(n)

    return (loss, dx0[None], *[grads[n] for n in WEIGHTS], *[delta[n] for n in WEIGHTS],
            *[new_m[n] for n in WEIGHTS], *[new_v[n] for n in WEIGHTS])
```

```python
import numpy as np
import jax
import jax.numpy as jnp
from jax import lax
from jax.experimental import pallas as pl
from jax.experimental.pallas import tpu as pltpu

F32 = jnp.float32
BF16 = jnp.bfloat16

D = 1024
FF = 2816
PW = 5632
HP = 128
CHUNK = 64
WIN = 640
TQ = 512
EPS = 1e-6
NEG = -1e30
LRU_C = 8.0
N_DEV = 8
VMEM_LIMIT = 56 * 1024 * 1024

ADAM_LR, ADAM_B1, ADAM_B2, ADAM_EPS, ADAM_WD, ADAM_STEP = 0.001, 0.9, 0.999, 1e-08, 0.01, 10

MESH = pl.DeviceIdType.MESH
ANY = pl.BlockSpec(memory_space=pl.ANY)


def _cp(*sem):
    return pltpu.CompilerParams(dimension_semantics=tuple(sem), vmem_limit_bytes=VMEM_LIMIT)


def _dot(a, b):
    return jnp.dot(a, b, preferred_element_type=F32)


def _dot_nt(a, b):
    return lax.dot_general(a, b, (((1,), (1,)), ((), ())), preferred_element_type=F32)


def _dot_tn(a, b):
    return lax.dot_general(a, b, (((0,), (0,)), ((), ())), preferred_element_type=F32)


def _mean(v):
    return jnp.mean(v, axis=-1, keepdims=True)


def _colsum(v):
    return jnp.sum(v, axis=0, keepdims=True)


def _sigmoid(v):
    return 0.5 * jnp.tanh(0.5 * v) + 0.5


_GK = 0.7978845608028654


def _gelu(v):
    t = jnp.tanh(_GK * (v + 0.044715 * v * v * v))
    return 0.5 * v * (1.0 + t)


def _pre_norm(xv, vec_ref):
    r = lax.rsqrt(_mean(xv * xv) + EPS)
    n = xv * r * vec_ref[0:1, :]
    return n * (1.0 + vec_ref[3:4, :]) + vec_ref[2:3, :]


def _pre_norm_bwd(dh, xv, dres, vec_ref, vacc_ref):
    r = lax.rsqrt(_mean(xv * xv) + EPS)
    xh = xv * r
    n = xh * vec_ref[0:1, :]
    vacc_ref[2:3, :] += _colsum(dh)
    vacc_ref[3:4, :] += _colsum(dh * n)
    dn = dh * (1.0 + vec_ref[3:4, :])
    vacc_ref[0:1, :] += _colsum(dn * xh)
    dxh = dn * vec_ref[0:1, :]
    return r * (dxh - xh * _mean(dxh * xh)) + dres


def _post_norm_bwd(dxo, fv, res, vec_ref, vacc_ref):
    rf = lax.rsqrt(_mean(fv * fv) + EPS)
    fh = fv * rf
    gp = vec_ref[1:2, :]
    vacc_ref[4:5, :] += _colsum(res * dxo * (fh * gp))
    dy = (res * vec_ref[4:5, :]) * dxo
    vacc_ref[1:2, :] += _colsum(dy * fh)
    dfn = dy * gp
    return rf * (dfn - fh * _mean(dfn * fh))


def _u_spec(tf):
    return pl.BlockSpec((pl.Element(D), pl.Element(tf)),
                        lambda i, j: (0, pl.multiple_of(jnp.minimum(FF + j * tf, 2 * FF - tf), 128)))


def ffn_fwd(x, vec, w_gu, w_dn, res, name, tgt=None, tm=1024, tf=512):
    S = x.shape[0]
    tm = min(tm, S)
    nt = S // tm
    nf = -(-FF // tf)
    tail = FF - tf * (nf - 1)
    halves = [pl.ds(r * (tm // 2), tm // 2) for r in range(2)]
    head = tgt is not None

    def body(*refs):
        x_ref, vec_ref, wg_ref, wu_ref, wd_ref = refs[:5]
        if head:
            t_hbm, xo_ref, h_ref, g_ref, u_ref, a_ref, f_ref, l_ref, hs, acc, lacc, ts, tsem = refs[5:]
        else:
            xo_ref, h_ref, g_ref, u_ref, a_ref, f_ref, hs, acc = refs[5:]
        i, j = pl.program_id(0), pl.program_id(1)
        if head:
            late = pltpu.make_async_copy(t_hbm.at[pl.ds(pl.multiple_of(i * tm, 8), tm)], ts, tsem)

        @pl.when(j == 0)
        def _():
            if head:
                late.start()
            h = _pre_norm(x_ref[...], vec_ref).astype(BF16)
            hs[...] = h
            h_ref[...] = h
            acc[...] = jnp.zeros_like(acc)

        def chunk(w):
            gu = [(_dot(hs[r, :], wg_ref[:, 0:w]), _dot(hs[r, :], wu_ref[:, tf - w:tf])) for r in halves]
            acts = []
            for r, (g, u) in zip(halves, gu):
                g_ref[r, 0:w] = g.astype(BF16)
                u_ref[r, 0:w] = u.astype(BF16)
                a = (g * _sigmoid(g) * u).astype(BF16)
                a_ref[r, 0:w] = a
                acts.append(a)
            for r, a in zip(halves, acts):
                acc[r, :] += _dot(a, wd_ref[0:w, :])

        @pl.when(j < nf - 1)
        def _():
            chunk(tf)

        @pl.when(j == nf - 1)
        def _():
            chunk(tail)
            f = acc[...]
            f_ref[...] = f.astype(BF16)
            y = f * lax.rsqrt(_mean(f * f) + EPS) * vec_ref[1:2, :]
            xo = x_ref[...] + (res * vec_ref[4:5, :]) * y
            if head:
                @pl.when(i == 0)
                def _():
                    lacc[...] = jnp.zeros_like(lacc)

                late.wait()
                d = xo - ts[...]
                xo_ref[...] = d * (1.0 / D)
                lacc[...] += _colsum(d * d)

                @pl.when(i == nt - 1)
                def _():
                    l_ref[...] = jnp.broadcast_to(0.5 * jnp.sum(lacc[...]) * (1.0 / D), (8, 128))
            else:
                xo_ref[...] = xo

    row = lambda i, j: (i, 0)
    col = lambda i, j: (i, j)
    in_specs = [pl.BlockSpec((tm, D), row), pl.BlockSpec((8, D), lambda i, j: (0, 0)),
                pl.BlockSpec((D, tf), lambda i, j: (0, j)), _u_spec(tf),
                pl.BlockSpec((tf, D), lambda i, j: (j, 0))]
    out_specs = [pl.BlockSpec((tm, D), row), pl.BlockSpec((tm, D), row), pl.BlockSpec((tm, tf), col),
                 pl.BlockSpec((tm, tf), col), pl.BlockSpec((tm, tf), col), pl.BlockSpec((tm, D), row)]
    out_shape = [jax.ShapeDtypeStruct((S, D), F32), jax.ShapeDtypeStruct((S, D), BF16),
                 jax.ShapeDtypeStruct((S, FF), BF16), jax.ShapeDtypeStruct((S, FF), BF16),
                 jax.ShapeDtypeStruct((S, FF), BF16), jax.ShapeDtypeStruct((S, D), BF16)]
    scratch = [pltpu.VMEM((tm, D), BF16), pltpu.VMEM((tm, D), F32)]
    args = [x, vec, w_gu, w_gu, w_dn]
    if head:
        in_specs.append(ANY)
        out_specs.append(pl.BlockSpec((8, 128), lambda i, j: (0, 0)))
        out_shape.append(jax.ShapeDtypeStruct((8, 128), F32))
        scratch += [pltpu.VMEM((1, D), F32), pltpu.VMEM((tm, D), F32), pltpu.SemaphoreType.DMA]
        args.append(tgt)
    return pl.pallas_call(
        body, name=name, grid=(nt, nf), in_specs=in_specs, out_specs=out_specs, out_shape=out_shape,
        scratch_shapes=scratch,
        compiler_params=_cp("arbitrary" if head else "parallel", "arbitrary"),
    )(*args)


def ffn_bwd(dxo, x, f, g, u, vec, w_gu, w_dn, res, name, tm=1024, tf=512):
    S = x.shape[0]
    tm = min(tm, S)
    nf = -(-FF // tf)
    tail = FF - tf * (nf - 1)
    halves = [pl.ds(r * (tm // 4), tm // 4) for r in range(4)]

    def body(dxo_ref, x_hbm, f_ref, g_ref, u_ref, vec_ref, wg_ref, wu_ref, wd_ref,
             dx_ref, df_ref, dgu_ref, vacc_ref, dfs, acc, xs, xsem):
        i, j = pl.program_id(0), pl.program_id(1)
        late = pltpu.make_async_copy(x_hbm.at[pl.ds(pl.multiple_of(i * tm, 8), tm)], xs, xsem)

        @pl.when((i == 0) & (j == 0))
        def _():
            vacc_ref[...] = jnp.zeros_like(vacc_ref)

        @pl.when(j == 0)
        def _():
            late.start()
            df = _post_norm_bwd(dxo_ref[...], f_ref[...].astype(F32), res, vec_ref, vacc_ref).astype(BF16)
            dfs[...] = df
            df_ref[...] = df
            acc[...] = jnp.zeros_like(acc)

        def chunk(w):
            da = [_dot_nt(dfs[h, :], wd_ref[0:w, :]) for h in halves]
            dgu = []
            for h, d in zip(halves, da):
                gv, uv = g_ref[h, 0:w].astype(F32), u_ref[h, 0:w].astype(F32)
                sg = _sigmoid(gv)
                dg = (d * uv * (sg * (1.0 + gv * (1.0 - sg)))).astype(BF16)
                du = (d * (gv * sg)).astype(BF16)
                dgu_ref[0, h, 0:w] = dg
                dgu_ref[1, h, 0:w] = du
                dgu.append((dg, du))
            for h, (dg, du) in zip(halves, dgu):
                acc[h, :] += _dot_nt(dg, wg_ref[:, 0:w]) + _dot_nt(du, wu_ref[:, tf - w:tf])

        @pl.when(j < nf - 1)
        def _():
            chunk(tf)

        @pl.when(j == nf - 1)
        def _():
            chunk(tail)
            late.wait()
            dx_ref[...] = _pre_norm_bwd(acc[...], xs[...], dxo_ref[...], vec_ref, vacc_ref)

    row = lambda i, j: (i, 0)
    col = lambda i, j: (i, j)
    return pl.pallas_call(
        body, name=name, grid=(S // tm, nf),
        in_specs=[pl.BlockSpec((tm, D), row), ANY, pl.BlockSpec((tm, D), row),
                  pl.BlockSpec((tm, tf), col), pl.BlockSpec((tm, tf), col),
                  pl.BlockSpec((8, D), lambda i, j: (0, 0)),
                  pl.BlockSpec((D, tf), lambda i, j: (0, j)), _u_spec(tf),
                  pl.BlockSpec((tf, D), lambda i, j: (j, 0))],
        out_specs=[pl.BlockSpec((tm, D), row), pl.BlockSpec((tm, D), row),
                   pl.BlockSpec((2, tm, tf), lambda i, j: (0, i, j)),
                   pl.BlockSpec((8, D), lambda i, j: (0, 0))],
        out_shape=[jax.ShapeDtypeStruct((S, D), F32), jax.ShapeDtypeStruct((S, D), BF16),
                   jax.ShapeDtypeStruct((2, S, FF), BF16), jax.ShapeDtypeStruct((8, D), F32)],
        scratch_shapes=[pltpu.VMEM((tm, D), BF16), pltpu.VMEM((tm, D), F32), pltpu.VMEM((tm, D), F32),
                        pltpu.SemaphoreType.DMA],
        compiler_params=_cp("arbitrary", "arbitrary"),
    )(dxo, x, f, g, u, vec, w_gu, w_gu, w_dn)


def mm_tn(a, b, name, tm, tn, tk, out_dtype=BF16, a_resident=False):
    S, M = a.shape
    if b.ndim == 3:
        G, _, Nf = b.shape
    else:
        G, Nf = 1, b.shape[1]
    N = G * Nf
    tk = min(tk, S)
    nbf = Nf // tn
    nk = S // tk

    def body(a_ref, b_ref, o_ref, acc):
        k = pl.program_id(2)

        @pl.when(k == 0)
        def _():
            acc[...] = jnp.zeros_like(acc)

        a_blk = a_ref[pl.ds(pl.multiple_of(k * tk, 16), tk), :] if a_resident else a_ref[...]
        acc[...] += _dot_tn(a_blk, b_ref[...])

        @pl.when(k == nk - 1)
        def _():
            o_ref[...] = acc[...].astype(out_dtype)

    if b.ndim == 3:
        b_spec = pl.BlockSpec((None, tk, tn), lambda i, j, k: (j // nbf, k, j % nbf))
    else:
        b_spec = pl.BlockSpec((tk, tn), lambda i, j, k: (k, j))
    if a_resident:
        a_spec = pl.BlockSpec((S, M), lambda i, j, k: (0, 0), pipeline_mode=pl.Buffered(1))
    else:
        a_spec = pl.BlockSpec((tk, tm), lambda i, j, k: (k, i))
    return pl.pallas_call(
        body, name=name, grid=(M // tm, N // tn, nk),
        in_specs=[a_spec, b_spec],
        out_specs=pl.BlockSpec((tm, tn), lambda i, j, k: (i, j)),
        out_shape=jax.ShapeDtypeStruct((M, N), out_dtype),
        scratch_shapes=[pltpu.VMEM((tm, tn), F32)],
        compiler_params=_cp("parallel", "parallel", "arbitrary"),
    )(a, b)


def proj_fwd(x, vec, w_in, name, tm=2048, tn=512):
    S = x.shape[0]
    tm = min(tm, S)
    nq = 1536 // tn

    def body(x_ref, vec_ref, w_ref, h_ref, qkv_ref, rest_ref, hs):
        j = pl.program_id(1)

        @pl.when(j == 0)
        def _():
            h = _pre_norm(x_ref[...], vec_ref).astype(BF16)
            hs[...] = h
            h_ref[...] = h

        r = _dot(hs[...], w_ref[...])

        @pl.when(j < nq)
        def _():
            qkv_ref[...] = r.astype(BF16)

        @pl.when(j >= nq)
        def _():
            rest_ref[...] = r.astype(BF16)

    row = lambda i, j: (i, 0)
    return pl.pallas_call(
        body, name=name, grid=(S // tm, PW // tn),
        in_specs=[pl.BlockSpec((tm, D), row), pl.BlockSpec((8, D), lambda i, j: (0, 0)),
                  pl.BlockSpec((D, tn), lambda i, j: (0, j))],
        out_specs=[pl.BlockSpec((tm, D), row),
                   pl.BlockSpec((tm, tn), lambda i, j: (i, jnp.minimum(j, nq - 1))),
                   pl.BlockSpec((tm, tn), lambda i, j: (i, jnp.maximum(j - nq, 0)))],
        out_shape=[jax.ShapeDtypeStruct((S, D), BF16), jax.ShapeDtypeStruct((S, 1536), BF16),
                   jax.ShapeDtypeStruct((S, 4096), BF16)],
        scratch_shapes=[pltpu.VMEM((tm, D), BF16)],
        compiler_params=_cp("parallel", "arbitrary"),
    )(x, vec, w_in)


def proj_bwd(dq, dkv, dxr, d3, w_in, x, dxo, vec, name, tm=2048, tk=512):
    S = x.shape[0]
    tm = min(tm, S)
    nk = PW // tk

    def body(dq_ref, dkv_ref, dxr_ref, d3_ref, w_ref, x_hbm, dxo_hbm, vec_ref, dx_ref, vacc_ref, acc, xs, dxos, sems):
        i, j = pl.program_id(0), pl.program_id(1)
        tok = pl.ds(pl.multiple_of(i * tm, 8), tm)
        late = (pltpu.make_async_copy(x_hbm.at[tok], xs, sems.at[0]),
                pltpu.make_async_copy(dxo_hbm.at[tok], dxos, sems.at[1]))

        @pl.when((i == 0) & (j == 0))
        def _():
            vacc_ref[...] = jnp.zeros_like(vacc_ref)

        @pl.when(j == 0)
        def _():
            for cp in late:
                cp.start()
            acc[...] = _dot_nt(dq_ref[...], w_ref[...])

        @pl.when((j >= 1) & (j < 3))
        def _():
            acc[...] += _dot_nt(dkv_ref[...], w_ref[...])

        @pl.when((j >= 3) & (j < 5))
        def _():
            acc[...] += _dot_nt(dxr_ref[...], w_ref[...])

        @pl.when(j >= 5)
        def _():
            acc[...] += _dot_nt(d3_ref[...], w_ref[...])

        @pl.when(j == nk - 1)
        def _():
            for cp in late:
                cp.wait()
            dx_ref[...] = _pre_norm_bwd(acc[...], xs[...], dxos[...], vec_ref, vacc_ref)

    row = lambda i, j: (i, 0)
    return pl.pallas_call(
        body, name=name, grid=(S // tm, nk),
        in_specs=[pl.BlockSpec((None, tm, tk), lambda i, j: (0, i, 0)),
                  pl.BlockSpec((None, tm, tk), lambda i, j: (jnp.clip(j - 1, 0, 1), i, 0)),
                  pl.BlockSpec((tm, tk), lambda i, j: (i, jnp.clip(j - 3, 0, 1))),
                  pl.BlockSpec((None, tm, tk), lambda i, j: (jnp.clip(j - 5, 0, 5) // 2, i, jnp.clip(j - 5, 0, 5) % 2)),
                  pl.BlockSpec((D, tk), lambda i, j: (0, j)),
                  ANY, ANY,
                  pl.BlockSpec((8, D), lambda i, j: (0, 0))],
        out_specs=[pl.BlockSpec((tm, D), row, pipeline_mode=pl.Buffered(1)),
                   pl.BlockSpec((8, D), lambda i, j: (0, 0))],
        out_shape=[jax.ShapeDtypeStruct((S, D), F32), jax.ShapeDtypeStruct((8, D), F32)],
        scratch_shapes=[pltpu.VMEM((tm, D), F32), pltpu.VMEM((tm, D), F32), pltpu.VMEM((tm, D), F32),
                        pltpu.SemaphoreType.DMA((2,))],
        compiler_params=_cp("arbitrary", "arbitrary"),
    )(dq, dkv, dxr, d3, w_in, x, dxo, vec)


def _two_heads(v, lane):
    zero = jnp.zeros((), v.dtype)
    return jnp.concatenate([jnp.where(lane < 64, v, zero), jnp.where(lane >= 64, v, zero)], axis=0)


def _attn_scores(qm, ka, bias_h, i, grp):
    s = _dot_nt(qm, ka) + bias_h
    col = lax.broadcasted_iota(jnp.int32, s.shape, 1)
    first_key = jnp.where(i == 0, 512 - 128 * grp, 0)
    return jnp.where(col >= first_key, s, NEG)


def _softmax(s):
    e = jnp.exp(s - jnp.max(s, axis=-1, keepdims=True))
    return e * (1.0 / jnp.sum(e, axis=-1, keepdims=True))


NG = TQ // 128


def attn_fwd(qkv, bias, name):
    S = qkv.shape[0]
    nb = S // TQ

    def body(q_ref, kp_ref, kc_ref, vp_ref, vc_ref, b_ref, o_ref, kw, vw):
        i = pl.program_id(1)
        kw[0:TQ, :] = kp_ref[...]
        kw[TQ:2 * TQ, :] = kc_ref[...]
        vw[0:TQ, :] = vp_ref[...]
        vw[TQ:2 * TQ, :] = vc_ref[...]
        lane = lax.broadcasted_iota(jnp.int32, (1, HP), 1)

        rows = [pl.ds(128 * a, 128) for a in range(NG)]
        keys = [pl.ds(128 * a, WIN) for a in range(NG)]
        q2 = [_two_heads(q_ref[r, :] * jnp.asarray(0.125, BF16), lane) for r in rows]
        s = [_attn_scores(q2[a], kw[keys[a], :], b_ref[...], i, a) for a in range(NG)]
        p = [_softmax(sa).astype(BF16) for sa in s]
        o2 = [_dot(p[a], vw[keys[a], :]) for a in range(NG)]
        for a in range(NG):
            o_ref[rows[a], :] = jnp.where(lane < 64, o2[a][0:128], o2[a][128:256]).astype(BF16)

    prev = lambda h, i: (jnp.maximum(i - 1, 0), 0)
    return pl.pallas_call(
        body, name=name, grid=(4, nb),
        in_specs=[pl.BlockSpec((TQ, HP), lambda h, i: (i, h)),
                  pl.BlockSpec((TQ, HP), lambda h, i: (jnp.maximum(i - 1, 0), 4 + h)),
                  pl.BlockSpec((TQ, HP), lambda h, i: (i, 4 + h)),
                  pl.BlockSpec((TQ, HP), lambda h, i: (jnp.maximum(i - 1, 0), 8 + h)),
                  pl.BlockSpec((TQ, HP), lambda h, i: (i, 8 + h)),
                  pl.BlockSpec((None, 256, WIN), lambda h, i: (h, 0, 0))],
        out_specs=pl.BlockSpec((TQ, HP), lambda h, i: (i, h)),
        out_shape=jax.ShapeDtypeStruct((S, 512), BF16),
        scratch_shapes=[pltpu.VMEM((2 * TQ, HP), BF16), pltpu.VMEM((2 * TQ, HP), BF16)],
        compiler_params=_cp("parallel", "arbitrary"),
    )(qkv, qkv, qkv, qkv, qkv, bias)


def attn_bwd(qkv, do, bias, name):
    S = qkv.shape[0]
    nb = S // TQ

    def body(q_ref, kp_ref, kc_ref, vp_ref, vc_ref, do_ref, b_ref, dqkv_ref, db_ref, dkv_ref, kw, vw, ak, av):
        i = pl.program_id(1)

        @pl.when(i == 0)
        def _():
            db_ref[...] = jnp.zeros_like(db_ref)
            ak[...] = jnp.zeros_like(ak)
            av[...] = jnp.zeros_like(av)

        @pl.when(i > 0)
        def _():
            ak[0:TQ, :] = ak[TQ:2 * TQ, :]
            av[0:TQ, :] = av[TQ:2 * TQ, :]
            ak[TQ:2 * TQ, :] = jnp.zeros((TQ, HP), F32)
            av[TQ:2 * TQ, :] = jnp.zeros((TQ, HP), F32)

        @pl.when(i < nb)
        def _():
            kw[0:TQ, :] = kp_ref[...]
            kw[TQ:2 * TQ, :] = kc_ref[...]
            vw[0:TQ, :] = vp_ref[...]
            vw[TQ:2 * TQ, :] = vc_ref[...]
            lane = lax.broadcasted_iota(jnp.int32, (1, HP), 1)

            rows = [pl.ds(128 * a, 128) for a in range(NG)]
            keys = [pl.ds(128 * a, WIN) for a in range(NG)]
            q2 = [_two_heads(q_ref[r, :] * jnp.asarray(0.125, BF16), lane) for r in rows]
            do2 = [_two_heads(do_ref[r, :], lane) for r in rows]
            s = [_attn_scores(q2[a], kw[keys[a], :], b_ref[...], i, a) for a in range(NG)]
            dp = [_dot_nt(do2[a], vw[keys[a], :]) for a in range(NG)]
            p = [_softmax(sa) for sa in s]
            ds = [p[a] * (dp[a] - jnp.sum(p[a] * dp[a], axis=-1, keepdims=True)) for a in range(NG)]
            db_ref[...] += (ds[0] + ds[1]) + (ds[2] + ds[3])
            dsb = [d.astype(BF16) for d in ds]
            dq2 = [_dot(dsb[a], kw[keys[a], :]) for a in range(NG)]
            dk = [_dot_tn(dsb[a], q2[a]) for a in range(NG)]
            dv = [_dot_tn(p[a].astype(BF16), do2[a]) for a in range(NG)]
            for a in range(NG):
                ak[keys[a], :] += dk[a]
                av[keys[a], :] += dv[a]
                dq = jnp.where(lane < 64, dq2[a][0:128], dq2[a][128:256])
                dqkv_ref[0, rows[a], :] = (dq * 0.125).astype(BF16)

        @pl.when(i > 0)
        def _():
            dkv_ref[0] = ak[0:TQ, :].astype(BF16)
            dkv_ref[1] = av[0:TQ, :].astype(BF16)

    cur = lambda i: jnp.minimum(i, nb - 1)
    prv = lambda i: jnp.clip(i - 1, 0, nb - 1)
    dq, db, dkv = pl.pallas_call(
        body, name=name, grid=(4, nb + 1),
        in_specs=[pl.BlockSpec((TQ, HP), lambda h, i: (cur(i), h)),
                  pl.BlockSpec((TQ, HP), lambda h, i: (prv(i), 4 + h)),
                  pl.BlockSpec((TQ, HP), lambda h, i: (cur(i), 4 + h)),
                  pl.BlockSpec((TQ, HP), lambda h, i: (prv(i), 8 + h)),
                  pl.BlockSpec((TQ, HP), lambda h, i: (cur(i), 8 + h)),
                  pl.BlockSpec((TQ, HP), lambda h, i: (cur(i), h)),
                  pl.BlockSpec((None, 256, WIN), lambda h, i: (h, 0, 0))],
        out_specs=[pl.BlockSpec((1, TQ, HP), lambda h, i: (0, cur(i), h)),
                   pl.BlockSpec((None, 256, WIN), lambda h, i: (h, 0, 0)),
                   pl.BlockSpec((2, TQ, HP), lambda h, i: (0, prv(i), h))],
        out_shape=[jax.ShapeDtypeStruct((1, S, 512), BF16), jax.ShapeDtypeStruct((4, 256, WIN), F32),
                   jax.ShapeDtypeStruct((2, S, 512), BF16)],
        scratch_shapes=[pltpu.VMEM((2 * TQ, HP), BF16), pltpu.VMEM((2 * TQ, HP), BF16),
                        pltpu.VMEM((2 * TQ, HP), F32), pltpu.VMEM((2 * TQ, HP), F32)],
        compiler_params=_cp("parallel", "arbitrary"),
    )(qkv, qkv, qkv, qkv, qkv, do, bias)
    return dq, db, dkv


def bias_grad(db, name):
    def body(db_ref, o_ref):
        r = lax.broadcasted_iota(jnp.int32, (128, 128), 0)
        c = lax.broadcasted_iota(jnp.int32, (128, 128), 1)
        flip = (r + c == 127).astype(BF16)
        lane = lax.broadcasted_iota(jnp.int32, (16, 384), 1)
        src = lax.broadcasted_iota(jnp.int32, (128, 384), 0)
        dst = lax.broadcasted_iota(jnp.int32, (128, 384), 1)

        def split_dot(v, m):
            hi = v.astype(BF16)
            r1 = v - hi.astype(F32)
            mid = r1.astype(BF16)
            lo = (r1 - mid.astype(F32)).astype(BF16)
            return _dot(hi, m) + _dot(mid, m) + _dot(lo, m)

        def diag_sums(w):
            y = pltpu.roll(split_dot(w, flip), 0, 1, stride=1, stride_axis=0)
            return jnp.broadcast_to(_colsum(y), (16, 128))

        w4 = db_ref[0, :, 512:640]
        w3 = db_ref[0, :, 384:512]
        far = jnp.sum(db_ref[0, :, 0:384]) + jnp.sum(jnp.where(r >= c, w3, 0.0))
        lo4 = diag_sums(jnp.where(r >= c, w4, 0.0))
        up4 = diag_sums(jnp.where(r < c, w4, 0.0))
        up3 = diag_sums(jnp.where(r < c, w3, 0.0))
        p_lo4 = (dst == 128 + (src + 1) % 128).astype(BF16)
        p_up4 = ((dst == src + 1) & (src < 127)).astype(BF16)
        p_up3 = ((dst == src + 129) & (src < 127)).astype(BF16)
        out = split_dot(lo4, p_lo4) + split_dot(up4, p_up4) + split_dot(up3, p_up3)
        o_ref[0] = out + jnp.where(lane == 256, far, 0.0)

    return pl.pallas_call(
        body, name=name, grid=(8,),
        in_specs=[pl.BlockSpec((1, 128, WIN), lambda h: (h, 0, 0))],
        out_specs=pl.BlockSpec((1, 16, 384), lambda h: (h, 0, 0)),
        out_shape=jax.ShapeDtypeStruct((8, 16, 384), F32),
        compiler_params=_cp("parallel"),
    )(db)[:, 0, :]


LT = 1024
LC = 512


def _lru_gates(xs, pv_ref, wa_ref, wx_ref, tl):
    xc = (pv_ref[4:5, :] + pv_ref[3:4, :] * xs[pl.ds(8, tl), :] + pv_ref[2:3, :] * xs[pl.ds(7, tl), :]
          + pv_ref[1:2, :] * xs[pl.ds(6, tl), :] + pv_ref[0:1, :] * xs[pl.ds(5, tl), :])
    xcb = xc.astype(BF16)
    pa = jnp.concatenate([_dot(xcb[:, 0:256], wa_ref[0]), _dot(xcb[:, 256:512], wa_ref[1])], axis=1)
    px = jnp.concatenate([_dot(xcb[:, 0:256], wx_ref[0]), _dot(xcb[:, 256:512], wx_ref[1])], axis=1)
    r = _sigmoid(pa + pv_ref[5:6, :])
    ig = _sigmoid(px + pv_ref[6:7, :])
    z = -pv_ref[7:8, :]
    sp = jnp.maximum(z, 0.0) + jnp.log1p(jnp.exp(-jnp.abs(z)))
    log_a = (-LRU_C * r) * sp
    a = jnp.exp(log_a)
    s = jnp.tanh(-log_a) * (1.0 + a * a)
    inv_mult = lax.rsqrt(s)
    mult = jnp.where(s > 0.0, s * inv_mult, 0.0)
    return xc, xcb, r, ig, sp, a, mult, inv_mult


def lru_fwd(rest, pvec, wa, wx, name):
    S = rest.shape[0]
    tl = min(LT, S)
    nt = S // tl

    def body(xr_ref, halo_ref, yr_ref, pv_ref, wa_ref, wx_ref, h_ref, hg_ref, xs, a_s, u_s, h_s, carry):
        ti = pl.program_id(1)

        @pl.when(ti == 0)
        def _():
            carry[...] = jnp.zeros_like(carry)

        xs[0:8, :] = jnp.where(ti > 0, halo_ref[8:16, :].astype(F32), 0.0)
        xs[pl.ds(8, tl), :] = xr_ref[...].astype(F32)
        xc, _, _, ig, _, a, mult, _ = _lru_gates(xs, pv_ref, wa_ref, wx_ref, tl)
        a_s[...] = a
        u_s[...] = mult * (ig * xc)
        row = lax.broadcasted_iota(jnp.int32, (8, LC), 0)

        def blk(bi, c):
            o = pl.multiple_of(bi * 8, 8)
            av = a_s[pl.ds(o, 8), :]
            bv = u_s[pl.ds(o, 8), :]
            for d in (1, 2, 4):
                a_sh = pltpu.roll(av, d, 0)
                b_sh = pltpu.roll(bv, d, 0)
                m = row >= d
                bv = jnp.where(m, av * b_sh + bv, bv)
                av = jnp.where(m, av * a_sh, av)
            hv = bv + av * c
            h_s[pl.ds(o, 8), :] = hv
            return hv[7:8, :]

        carry[...] = lax.fori_loop(0, tl // 8, blk, carry[...])
        h = h_s[...]
        h_ref[...] = h
        hg_ref[...] = (h * _gelu(yr_ref[...].astype(F32))).astype(BF16)

    hb = tl // 16
    return pl.pallas_call(
        body, name=name, grid=(2, nt),
        in_specs=[pl.BlockSpec((tl, LC), lambda c, t: (t, c)),
                  pl.BlockSpec((16, LC), lambda c, t: (jnp.maximum(t * hb - 1, 0), c)),
                  pl.BlockSpec((tl, LC), lambda c, t: (t, 2 + c)),
                  pl.BlockSpec((8, LC), lambda c, t: (0, c)),
                  pl.BlockSpec((2, 256, 256), lambda c, t: (c, 0, 0)),
                  pl.BlockSpec((2, 256, 256), lambda c, t: (c, 0, 0))],
        out_specs=[pl.BlockSpec((tl, LC), lambda c, t: (t, c)), pl.BlockSpec((tl, LC), lambda c, t: (t, c))],
        out_shape=[jax.ShapeDtypeStruct((S, D), F32), jax.ShapeDtypeStruct((S, D), BF16)],
        scratch_shapes=[pltpu.VMEM((tl + 8, LC), F32), pltpu.VMEM((tl, LC), F32), pltpu.VMEM((tl, LC), F32),
                        pltpu.VMEM((tl, LC), F32), pltpu.VMEM((1, LC), F32)],
        compiler_params=_cp("parallel", "arbitrary"),
    )(rest, rest, rest, pvec, wa, wx)


def lru_bwd(dh, h, rest, pvec, wa, wx, name):
    S = rest.shape[0]
    tl = min(LT, S)
    nt = S // tl

    def body(dh_ref, h_ref, hhalo_ref, xr_ref, xhalo_ref, pv_ref, wa_ref, wx_ref,
             dxr_ref, vacc_ref, dwa_ref, dwx_ref,
             xs, hs, a_s, ash_s, b_s, lam_s, dxe, anext, lnext, dxnext):
        ti = pl.program_id(1)
        tr = nt - 1 - ti

        @pl.when(ti == 0)
        def _():
            anext[...] = jnp.zeros_like(anext)
            lnext[...] = jnp.zeros_like(lnext)
            dxnext[...] = jnp.zeros_like(dxnext)
            vacc_ref[...] = jnp.zeros_like(vacc_ref)
            dwa_ref[...] = jnp.zeros_like(dwa_ref)
            dwx_ref[...] = jnp.zeros_like(dwx_ref)

        xs[0:8, :] = jnp.where(tr > 0, xhalo_ref[8:16, :].astype(F32), 0.0)
        xs[pl.ds(8, tl), :] = xr_ref[...].astype(F32)
        xc, xcb, r, ig, sp, a, mult, inv_mult = _lru_gates(xs, pv_ref, wa_ref, wx_ref, tl)

        a_s[pl.ds(0, tl), :] = a
        a_s[pl.ds(tl, 8), :] = jnp.broadcast_to(anext[...], (8, LC))
        ash_s[...] = a_s[pl.ds(1, tl), :]
        b_s[...] = dh_ref[...]
        row = lax.broadcasted_iota(jnp.int32, (8, LC), 0)

        def blk(k, c):
            o = pl.multiple_of((tl // 8 - 1 - k) * 8, 8)
            av = ash_s[pl.ds(o, 8), :]
            bv = b_s[pl.ds(o, 8), :]
            for d in (1, 2, 4):
                a_sh = pltpu.roll(av, 8 - d, 0)
                b_sh = pltpu.roll(bv, 8 - d, 0)
                m = row < 8 - d
                bv = jnp.where(m, bv + av * b_sh, bv)
                av = jnp.where(m, av * a_sh, av)
            lv = bv + av * c
            lam_s[pl.ds(o, 8), :] = lv
            return lv[0:1, :]

        lnext[...] = lax.fori_loop(0, tl // 8, blk, lnext[...])
        anext[...] = a[0:1, :]
        lam = lam_s[...]

        hs[0:8, :] = jnp.where(tr > 0, hhalo_ref[...], 0.0)
        hs[pl.ds(8, tl), :] = h_ref[...]
        d_a = lam * hs[pl.ds(7, tl), :]
        d_mult = lam * (ig * xc)
        d_ig = lam * mult * xc
        dxc = lam * mult * ig
        d_log_a = d_a * a - d_mult * (a * a) * inv_mult
        d_r = d_log_a * (-LRU_C * sp)
        vacc_ref[7:8, :] += _colsum(d_log_a * (-LRU_C * r)) * (-_sigmoid(-pv_ref[7:8, :]))
        d_pa = d_r * r * (1.0 - r)
        d_px = d_ig * ig * (1.0 - ig)
        vacc_ref[5:6, :] += _colsum(d_pa)
        vacc_ref[6:7, :] += _colsum(d_px)
        dpa = d_pa.astype(BF16)
        dpx = d_px.astype(BF16)
        back = []
        for g in range(2):
            sl = slice(256 * g, 256 * g + 256)
            dwa_ref[g] += _dot_tn(xcb[:, sl], dpa[:, sl])
            dwx_ref[g] += _dot_tn(xcb[:, sl], dpx[:, sl])
            back.append(_dot_nt(dpa[:, sl], wa_ref[g]) + _dot_nt(dpx[:, sl], wx_ref[g]))
        dxc = dxc + jnp.concatenate(back, axis=1)
        vacc_ref[4:5, :] += _colsum(dxc)
        for k in range(4):
            vacc_ref[k:k + 1, :] += _colsum(dxc * xs[pl.ds(5 + k, tl), :])
        dxe[pl.ds(0, tl), :] = dxc
        dxe[pl.ds(tl, 8), :] = dxnext[...]
        dxr = (pv_ref[3:4, :] * dxc + pv_ref[2:3, :] * dxe[pl.ds(1, tl), :]
               + pv_ref[1:2, :] * dxe[pl.ds(2, tl), :] + pv_ref[0:1, :] * dxe[pl.ds(3, tl), :])
        dxr_ref[...] = dxr.astype(BF16)
        dxnext[...] = dxc[0:8, :]

    hb = tl // 8
    rev = lambda t: nt - 1 - t
    halo = lambda t: jnp.maximum(rev(t) * hb - 1, 0)
    big = lambda: pltpu.VMEM((tl + 8, LC), F32)
    til = lambda: pltpu.VMEM((tl, LC), F32)
    return pl.pallas_call(
        body, name=name, grid=(2, nt),
        in_specs=[pl.BlockSpec((tl, LC), lambda c, t: (rev(t), c)),
                  pl.BlockSpec((tl, LC), lambda c, t: (rev(t), c)),
                  pl.BlockSpec((8, LC), lambda c, t: (halo(t), c)),
                  pl.BlockSpec((tl, LC), lambda c, t: (rev(t), c)),
                  pl.BlockSpec((16, LC), lambda c, t: (jnp.maximum(rev(t) * (tl // 16) - 1, 0), c)),
                  pl.BlockSpec((8, LC), lambda c, t: (0, c)),
                  pl.BlockSpec((2, 256, 256), lambda c, t: (c, 0, 0)),
                  pl.BlockSpec((2, 256, 256), lambda c, t: (c, 0, 0))],
        out_specs=[pl.BlockSpec((tl, LC), lambda c, t: (rev(t), c)),
                   pl.BlockSpec((8, LC), lambda c, t: (0, c)),
                   pl.BlockSpec((2, 256, 256), lambda c, t: (c, 0, 0)),
                   pl.BlockSpec((2, 256, 256), lambda c, t: (c, 0, 0))],
        out_shape=[jax.ShapeDtypeStruct((S, D), BF16), jax.ShapeDtypeStruct((8, D), F32),
                   jax.ShapeDtypeStruct((4, 256, 256), F32), jax.ShapeDtypeStruct((4, 256, 256), F32)],
        scratch_shapes=[big(), big(), big(), til(), til(), til(), big(),
                        pltpu.VMEM((1, LC), F32), pltpu.VMEM((1, LC), F32), pltpu.VMEM((8, LC), F32)],
        compiler_params=_cp("parallel", "arbitrary"),
    )(dh, h, h, rest, rest, pvec, wa, wx)


def mix_out_fwd(x, ao, hg, rest, vec, w_att_o, w_rec_o, w_out, name, tm=512):
    S = x.shape[0]
    tm = min(tm, S)

    def body(x_ref, ao_ref, hg_ref, ga_ref, gr_ref, vec_ref, wa_ref, wr_ref, wo_ref,
             xo_ref, att_ref, rec_ref, mg_ref, f_ref):
        att = _dot(ao_ref[...], wa_ref[...])
        rec = _dot(hg_ref[...], wr_ref[...])
        att_ref[...] = att.astype(BF16)
        rec_ref[...] = rec.astype(BF16)
        mg = (_sigmoid(ga_ref[...].astype(F32)) * att + _sigmoid(gr_ref[...].astype(F32)) * rec).astype(BF16)
        mg_ref[...] = mg
        f = _dot(mg, wo_ref[...])
        f_ref[...] = f.astype(BF16)
        y = f * lax.rsqrt(_mean(f * f) + EPS) * vec_ref[1:2, :]
        xo_ref[...] = x_ref[...] + (1.0 * vec_ref[4:5, :]) * y

    row = lambda i: (i, 0)
    full = lambda r: pl.BlockSpec((r, D), lambda i: (0, 0))
    return pl.pallas_call(
        body, name=name, grid=(S // tm,),
        in_specs=[pl.BlockSpec((tm, D), row), pl.BlockSpec((tm, 512), row), pl.BlockSpec((tm, D), row),
                  pl.BlockSpec((tm, D), lambda i: (i, 2)), pl.BlockSpec((tm, D), lambda i: (i, 3)),
                  full(8), full(512), full(D), full(D)],
        out_specs=[pl.BlockSpec((tm, D), row)] * 5,
        out_shape=[jax.ShapeDtypeStruct((S, D), F32)] + [jax.ShapeDtypeStruct((S, D), BF16)] * 4,
        compiler_params=_cp("parallel"),
    )(x, ao, hg, rest, rest, vec, w_att_o, w_rec_o, w_out)


def mix_out_bwd(dxo, f, att, rec, rest, h, vec, w_att_o, w_rec_o, w_out, name, tm=512):
    S = dxo.shape[0]
    tm = min(tm, S)

    def body(dxo_ref, f_ref, att_ref, rec_ref, yr_ref, ga_ref, gr_ref, h_ref, vec_ref, wa_ref, wr_ref, wo_ref,
             df_ref, da_ref, dr_ref, dao_ref, dh_ref, d3_ref, vacc_ref):
        @pl.when(pl.program_id(0) == 0)
        def _():
            vacc_ref[...] = jnp.zeros_like(vacc_ref)

        df = _post_norm_bwd(dxo_ref[...], f_ref[...].astype(F32), 1.0, vec_ref, vacc_ref).astype(BF16)
        df_ref[...] = df
        dm = _dot_nt(df, wo_ref[...])
        sa = _sigmoid(ga_ref[...].astype(F32))
        sr = _sigmoid(gr_ref[...].astype(F32))
        d_att = (dm * sa).astype(BF16)
        d_rec = (dm * sr).astype(BF16)
        da_ref[...] = d_att
        dr_ref[...] = d_rec
        d3_ref[1] = (dm * att_ref[...].astype(F32) * (sa * (1.0 - sa))).astype(BF16)
        d3_ref[2] = (dm * rec_ref[...].astype(F32) * (sr * (1.0 - sr))).astype(BF16)
        dao_ref[...] = _dot_nt(d_att, wa_ref[...]).astype(BF16)
        d_hg = _dot_nt(d_rec, wr_ref[...])
        yr = yr_ref[...].astype(F32)
        t = jnp.tanh(_GK * (yr + 0.044715 * yr * yr * yr))
        dh_ref[...] = d_hg * (0.5 * yr * (1.0 + t))
        gelu_grad = 0.5 * (1.0 + t) + 0.5 * yr * (1.0 - t * t) * _GK * (1.0 + 3.0 * 0.044715 * yr * yr)
        d3_ref[0] = (d_hg * h_ref[...] * gelu_grad).astype(BF16)

    row = lambda i: (i, 0)
    full = lambda r: pl.BlockSpec((r, D), lambda i: (0, 0))
    return pl.pallas_call(
        body, name=name, grid=(S // tm,),
        in_specs=[pl.BlockSpec((tm, D), row)] * 4
        + [pl.BlockSpec((tm, D), lambda i: (i, 1)), pl.BlockSpec((tm, D), lambda i: (i, 2)),
           pl.BlockSpec((tm, D), lambda i: (i, 3)), pl.BlockSpec((tm, D), row),
           full(8), full(512), full(D), full(D)],
        out_specs=[pl.BlockSpec((tm, D), row)] * 3
        + [pl.BlockSpec((tm, 512), row), pl.BlockSpec((tm, D), row),
           pl.BlockSpec((3, tm, D), lambda i: (0, i, 0)), pl.BlockSpec((8, D), lambda i: (0, 0))],
        out_shape=[jax.ShapeDtypeStruct((S, D), BF16)] * 3
        + [jax.ShapeDtypeStruct((S, 512), BF16), jax.ShapeDtypeStruct((S, D), F32),
           jax.ShapeDtypeStruct((3, S, D), BF16), jax.ShapeDtypeStruct((8, D), F32)],
        compiler_params=_cp("arbitrary"),
    )(dxo, f, att, rec, rest, rest, rest, h, vec, w_att_o, w_rec_o, w_out)


def dw_in(h, dq, dkv, dxr, d3, name, tk=1024, tn=512):
    S = h.shape[0]
    tk = min(tk, S)
    nk = S // tk

    def body(h_ref, dq_ref, dkv_ref, dxr_ref, d3_ref, o_ref, acc):
        j, k = pl.program_id(0), pl.program_id(1)

        @pl.when(k == 0)
        def _():
            acc[...] = jnp.zeros_like(acc)

        @pl.when(j == 0)
        def _():
            acc[...] += _dot_tn(h_ref[pl.ds(pl.multiple_of(k * tk, 16), tk), :], dq_ref[...])

        @pl.when((j >= 1) & (j < 3))
        def _():
            acc[...] += _dot_tn(h_ref[pl.ds(pl.multiple_of(k * tk, 16), tk), :], dkv_ref[...])

        @pl.when((j >= 3) & (j < 5))
        def _():
            acc[...] += _dot_tn(h_ref[pl.ds(pl.multiple_of(k * tk, 16), tk), :], dxr_ref[...])

        @pl.when(j >= 5)
        def _():
            acc[...] += _dot_tn(h_ref[pl.ds(pl.multiple_of(k * tk, 16), tk), :], d3_ref[...])

        @pl.when(k == nk - 1)
        def _():
            o_ref[...] = acc[...].astype(BF16)

    use = lambda j, k, lo, hi: jnp.where((j >= lo) & (j < hi), k, 0)
    g3 = lambda j: jnp.clip(j - 5, 0, 5)
    return pl.pallas_call(
        body, name=name, grid=(PW // tn, nk),
        in_specs=[pl.BlockSpec((S, D), lambda j, k: (0, 0), pipeline_mode=pl.Buffered(1)),
                  pl.BlockSpec((None, tk, tn), lambda j, k: (0, use(j, k, 0, 1), 0)),
                  pl.BlockSpec((None, tk, tn), lambda j, k: (jnp.clip(j - 1, 0, 1), use(j, k, 1, 3), 0)),
                  pl.BlockSpec((tk, tn), lambda j, k: (use(j, k, 3, 5), jnp.clip(j - 3, 0, 1))),
                  pl.BlockSpec((None, tk, tn), lambda j, k: (g3(j) // 2, use(j, k, 5, 11), g3(j) % 2))],
        out_specs=pl.BlockSpec((D, tn), lambda j, k: (0, j)),
        out_shape=jax.ShapeDtypeStruct((D, PW), BF16),
        scratch_shapes=[pltpu.VMEM((D, tn), F32)],
        compiler_params=_cp("parallel", "arbitrary"),
    )(h, dq, dkv, dxr, d3)


def ada_fwd(c_all, w_ada, b_ada, name, tn=768):
    n = w_ada.shape[1]

    def body(c_ref, w_ref, b_ref, o_ref):
        cv = c_ref[...]
        ca = (cv * _sigmoid(cv)).astype(BF16)
        o_ref[...] = _dot(ca, w_ref[...].astype(BF16)) + b_ref[...]

    return pl.pallas_call(
        body, name=name, grid=(n // tn,),
        in_specs=[pl.BlockSpec((8, D), lambda j: (0, 0)), pl.BlockSpec((D, tn), lambda j: (0, j)),
                  pl.BlockSpec((1, tn), lambda j: (0, j))],
        out_specs=pl.BlockSpec((8, tn), lambda j: (0, j)),
        out_shape=jax.ShapeDtypeStruct((8, n), F32),
        compiler_params=_cp("parallel"),
    )(c_all, w_ada, b_ada)


def ada_bwd(c_all_t, dmod, name, tn=768):
    n = dmod.shape[1]

    def body(c_ref, d_ref, o_ref):
        cv = c_ref[...]
        ca = (cv * _sigmoid(cv)).astype(BF16)
        o_ref[...] = _dot(ca, d_ref[...].astype(BF16))

    return pl.pallas_call(
        body, name=name, grid=(n // tn,),
        in_specs=[pl.BlockSpec((D, 128), lambda j: (0, 0)), pl.BlockSpec((128, tn), lambda j: (0, j))],
        out_specs=pl.BlockSpec((D, tn), lambda j: (0, j)),
        out_shape=jax.ShapeDtypeStruct((D, n), F32),
        compiler_params=_cp("parallel"),
    )(c_all_t, dmod)


def _row_tile(rows, cols, itemsize=4, budget=1536 * 1024):
    best = None
    for t in range(8, rows + 1, 8):
        if rows % t == 0 and t * cols * itemsize <= budget:
            best = t
    return rows if best is None else best


def sum_lead(parts, name, out_dtype=F32):
    n, R, C = parts.shape
    tr = _row_tile(R, C * n)

    def body(p_ref, o_ref):
        acc = p_ref[0].astype(F32)
        for k in range(1, n):
            acc = acc + p_ref[k].astype(F32)
        o_ref[...] = acc.astype(out_dtype)

    return pl.pallas_call(
        body, name=name, grid=(R // tr,),
        in_specs=[pl.BlockSpec((n, tr, C), lambda i: (0, i, 0))],
        out_specs=pl.BlockSpec((tr, C), lambda i: (i, 0)),
        out_shape=jax.ShapeDtypeStruct((R, C), out_dtype),
        compiler_params=_cp("parallel"),
    )(parts)


def adamw(w, g, m, v, name, emit_g=False):
    R, C = w.shape
    tr = _row_tile(R, C * 8, budget=16 * 1024 * 1024)

    def body(w_ref, g_ref, m_ref, v_ref, d_ref, mo_ref, vo_ref, *go_ref):
        gv = g_ref[...]
        if emit_g:
            go_ref[0][...] = gv
        mn = ADAM_B1 * m_ref[...] + (1.0 - ADAM_B1) * gv
        vn = ADAM_B2 * v_ref[...] + (1.0 - ADAM_B2) * (gv * gv)
        m_hat = mn / (1.0 - ADAM_B1 ** ADAM_STEP)
        v_hat = vn / (1.0 - ADAM_B2 ** ADAM_STEP)
        d_ref[...] = -ADAM_LR * (m_hat / (jnp.sqrt(v_hat) + ADAM_EPS) + ADAM_WD * w_ref[...])
        mo_ref[...] = mn
        vo_ref[...] = vn

    spec = pl.BlockSpec((tr, C), lambda i: (i, 0))
    return pl.pallas_call(
        body, name=name, grid=(R // tr,),
        in_specs=[spec] * 4, out_specs=[spec] * (4 if emit_g else 3),
        out_shape=[jax.ShapeDtypeStruct((R, C), F32)] * (4 if emit_g else 3),
        compiler_params=_cp("parallel"),
    )(w, g, m, v)


def _mesh_pos():
    return lax.axis_index("x"), lax.axis_index("y"), lax.axis_index("c")


def _other_chips(mx, my):
    return [(1 - mx, my), (mx, 1 - my), (1 - mx, 1 - my)]


def ag_small(x, name):
    R = x.shape[0]

    def body(x_ref, out_ref, send_sems, recv_sems, local_sem):
        mx, my, mc = _mesh_pos()
        me, sibling = (mx, my, mc), (mx, my, 1 - mc)
        chips = _other_chips(mx, my)

        def slot(px, py, pc):
            return out_ref.at[4 * px + 2 * py + pc]

        def copy(k, block, to, src=None):
            return pltpu.make_async_remote_copy(
                src_ref=slot(*block) if src is None else src, dst_ref=slot(*block),
                send_sem=send_sems.at[k], recv_sem=recv_sems.at[k], device_id=to, device_id_type=MESH)

        mine = pltpu.make_async_copy(x_ref, slot(*me), local_sem)
        mine.start()
        first = [copy(0, me, sibling, src=x_ref)]
        first += [copy(1 + j, me, (*chip, mc), src=x_ref) for j, chip in enumerate(chips)]
        for cp in first:
            cp.start()
        passed = [copy(4 + j, (*chip, mc), sibling) for j, chip in enumerate(chips)]
        for j, chip in enumerate(chips):
            copy(1 + j, (*chip, mc), me).wait_recv()
            passed[j].start()
        copy(0, sibling, me).wait_recv()
        for j, chip in enumerate(chips):
            copy(4 + j, (*chip, 1 - mc), me).wait_recv()
        for cp in first + passed:
            cp.wait_send()
        mine.wait()

    return pl.pallas_call(
        body, name=name,
        out_shape=jax.ShapeDtypeStruct((N_DEV, R, 128), F32),
        in_specs=[pl.BlockSpec(memory_space=pltpu.VMEM)],
        out_specs=pl.BlockSpec(memory_space=pltpu.VMEM),
        scratch_shapes=[pltpu.SemaphoreType.DMA((7,)), pltpu.SemaphoreType.DMA((7,)), pltpu.SemaphoreType.DMA],
        compiler_params=pltpu.CompilerParams(vmem_limit_bytes=VMEM_LIMIT),
    )(x)


BIG = (("ffn1_w_gu", "col", D, PW), ("ffn1_w_down", "row", FF, D), ("w_in", "col", D, PW),
       ("w_att_o", "col", 512, D), ("w_rec_o", "row", D, D), ("w_out", "row", D, D),
       ("ffn2_w_gu", "col", D, PW), ("ffn2_w_down", "row", FF, D))
NBIG = len(BIG)


def _shard_shape(kind, R, C):
    return (R, C // 4) if kind == "col" else (R // 4, C)


def _region(ref, kind, R, C, q, half, t, tr):
    sr, sc = _shard_shape(kind, R, C)
    if kind == "col":
        return ref.at[pl.ds(pl.multiple_of(half * (R // 2) + t * tr, 16), tr), pl.ds(q * sc, sc)]
    return ref.at[pl.ds(pl.multiple_of(q * sr + t * tr, 16), tr), pl.ds(half * (C // 2), C // 2)]


def ag_local(w, kind, R, C, p_arr, name, after=()):
    sr, sc = _shard_shape(kind, R, C)
    tr = _row_tile(sr, sc, budget=2 * 1024 * 1024)
    nt = sr // tr
    after = list(after)

    def body(p_ref, w_ref, *rest):
        rest[-1][...] = w_ref[...].astype(BF16)

    if kind == "col":
        o_spec = pl.BlockSpec((tr, sc), lambda i, p: (i, p[0]))
    else:
        o_spec = pl.BlockSpec((tr, sc), lambda i, p: (p[0] * nt + i, 0))
    return pl.pallas_call(
        body, name=name,
        grid_spec=pltpu.PrefetchScalarGridSpec(
            num_scalar_prefetch=1, grid=(nt,),
            in_specs=[pl.BlockSpec((tr, sc), lambda i, p: (i, 0))] + [ANY] * len(after), out_specs=o_spec),
        out_shape=jax.ShapeDtypeStruct((R, C), BF16),
        compiler_params=_cp("parallel"),
    )(p_arr, w, *after)


HBM_SPEC = pl.BlockSpec(memory_space=pltpu.HBM)
SEM_SPEC = pl.BlockSpec(memory_space=pltpu.SEMAPHORE)


def _ag_sems(geoms):
    return sum(6 if both else 3 for (_, _, _, both) in geoms)


def _ag_copies(fulls, geoms, ssem, rsem, mx, my, mc, q, h):
    chips = _other_chips(mx, my)
    out, base = [], 0
    for w, (kind, R, C, both) in enumerate(geoms):
        sr, sc = _shard_shape(kind, R, C)
        hr = sr // 2 if kind == "col" else sr
        reg = _region(fulls[w], kind, R, C, q, h, 0, hr)
        out.append([pltpu.make_async_remote_copy(
            src_ref=reg, dst_ref=reg, send_sem=ssem.at[base + 3 * t + k], recv_sem=rsem.at[base + 3 * t + k],
            device_id=(*chips[k], mc if t == 0 else 1 - mc), device_id_type=MESH)
            for t in range(2 if both else 1) for k in range(3)])
        base += 6 if both else 3
    return out


def ag_start(fulls, geoms, after, name):
    n = len(fulls)
    after = list(after)
    m = len(after)

    def body(*refs):
        ssem, rsem = refs[n + m:n + m + 2]
        outs, token = refs[n + m + 2:2 * n + m + 2], refs[2 * n + m + 2]
        mx, my, mc = _mesh_pos()
        p = 2 * mx + my
        col = [w for w, g in enumerate(geoms) if g[0] == "col"]
        row = [w for w, g in enumerate(geoms) if g[0] == "row"]
        for q in range(4):
            @pl.when(p == q)
            def _(q=q):
                cps = _ag_copies(outs, geoms, ssem, rsem, mx, my, mc, q, mc)
                for w in col:
                    for cp in cps[w]:
                        cp.start()
        for h in range(2):
            @pl.when(mc == h)
            def _(h=h):
                cps = _ag_copies(outs, geoms, ssem, rsem, mx, my, mc, p, h)
                for w in row:
                    for cp in cps[w]:
                        cp.start()
        token[...] = jnp.zeros_like(token)

    res = pl.pallas_call(
        body, name=name,
        out_shape=[pltpu.SemaphoreType.DMA((_ag_sems(geoms),)), pltpu.SemaphoreType.DMA((_ag_sems(geoms),))]
        + [pltpu.HBM(a.shape, a.dtype) for a in fulls] + [jax.ShapeDtypeStruct((8, 128), F32)],
        in_specs=[HBM_SPEC] * n + [ANY] * m,
        out_specs=[SEM_SPEC, SEM_SPEC] + [HBM_SPEC] * n + [pl.BlockSpec(memory_space=pltpu.VMEM)],
        input_output_aliases={w: 2 + w for w in range(n)},
        compiler_params=pltpu.CompilerParams(has_side_effects=pltpu.SideEffectType.DATAFLOW_SIDE_EFFECTING),
    )(*[pltpu.with_memory_space_constraint(a, pltpu.HBM) for a in fulls], *after)
    return res[0], res[1], list(res[2:2 + n]), res[2 + n]


def ag_wait(fulls, geoms, ssem, rsem, after, name):
    n = len(fulls)
    after = list(after) if isinstance(after, (list, tuple)) else [after]

    def body(*refs):
        ins, ssem_ref, rsem_ref = refs[:n], refs[n], refs[n + 1]
        mx, my, mc = _mesh_pos()
        for cps in _ag_copies(ins, geoms, ssem_ref, rsem_ref, mx, my, mc, 0, 0):
            for cp in cps:
                cp.wait_send()
                cp.wait_recv()

    return list(pl.pallas_call(
        body, name=name,
        out_shape=[pltpu.HBM(a.shape, a.dtype) for a in fulls],
        in_specs=[HBM_SPEC] * n + [SEM_SPEC, SEM_SPEC] + [ANY] * len(after),
        out_specs=[HBM_SPEC] * n,
        input_output_aliases={w: w for w in range(n)},
        compiler_params=pltpu.CompilerParams(has_side_effects=pltpu.SideEffectType.DATAFLOW_SIDE_EFFECTING),
    )(*fulls, ssem, rsem, *after))


def ag_forward(full, kind, R, C, name):
    sr, sc = _shard_shape(kind, R, C)
    hr, hc = (sr // 2, sc) if kind == "col" else (sr, sc // 2)
    tr = _row_tile(hr, hc, itemsize=2, budget=2 * 1024 * 1024)
    nt = hr // tr
    total = 3 * nt

    def body(src_ref, full_ref, stage, lsem, ssem, rsem):
        step = pl.program_id(0) * nt + pl.program_id(1)
        par = step % 2
        mx, my, mc = _mesh_pos()

        def load(s, q, h, t):
            return pltpu.make_async_copy(_region(src_ref, kind, R, C, q, h, t, tr), stage.at[s], lsem.at[s])

        def push(s, q, h, t):
            return pltpu.make_async_remote_copy(src_ref=stage.at[s], dst_ref=_region(full_ref, kind, R, C, q, h, t, tr),
                                                send_sem=ssem.at[s], recv_sem=rsem, device_id=(mx, my, 1 - mc),
                                                device_id_type=MESH)

        def for_tile(stp, fn):
            q_k = _partner_chip(stp // nt, 2 * mx + my)
            if kind == "col":
                for q in range(4):
                    @pl.when(q_k == q)
                    def _(q=q):
                        fn(q, mc, stp % nt)
            else:
                for h in range(2):
                    @pl.when(mc == h)
                    def _(h=h):
                        fn(q_k, h, stp % nt)

        @pl.when(step == 0)
        def _():
            for_tile(step, lambda q, h, t: load(0, q, h, t).start())

        load(par, 0, 0, 0).wait()
        for_tile(step, lambda q, h, t: push(par, q, h, t).start())

        @pl.when(step + 1 < total)
        def _():
            @pl.when(step >= 1)
            def _():
                push(1 - par, 0, 0, 0).wait_send()
            for_tile(step + 1, lambda q, h, t: load(1 - par, q, h, t).start())

        @pl.when(step == total - 1)
        def _():
            push(par, 0, 0, 0).wait_send()
            push(1 - par, 0, 0, 0).wait_send()
            three = full_ref.at[pl.ds(0, hr), pl.ds(0, 3 * hc)] if kind == "col" else full_ref.at[pl.ds(0, 3 * hr), pl.ds(0, hc)]
            pltpu.make_async_remote_copy(src_ref=three, dst_ref=three, send_sem=ssem.at[0], recv_sem=rsem,
                                         device_id=(mx, my, 1 - mc), device_id_type=MESH).wait_recv()

    return pl.pallas_call(
        body, name=name, grid=(3, nt),
        in_specs=[ANY], out_specs=ANY,
        out_shape=jax.ShapeDtypeStruct((R, C), BF16),
        scratch_shapes=[pltpu.VMEM((2, tr, hc), BF16), pltpu.SemaphoreType.DMA((2,)), pltpu.SemaphoreType.DMA((2,)),
                        pltpu.SemaphoreType.DMA],
        input_output_aliases={0: 0},
        compiler_params=_cp("arbitrary", "arbitrary"),
    )(full)


def _half_shape(kind, R, C):
    return (R // 2, C) if kind == "col" else (R, C // 2)


def _piece_shape(kind, R, C):
    return (R // 2, C // 4) if kind == "col" else (R // 4, C // 2)


def pair_push(g, kind, c_arr, name):
    R, C = g.shape
    hr, hc = _half_shape(kind, R, C)
    tr = _row_tile(hr, hc, itemsize=2, budget=2 * 1024 * 1024)
    nt = hr // tr

    def body(c_ref, g_ref, out_ref, stage, ssem, rsem):
        i = pl.program_id(0)
        slot = i % 2
        mx, my, mc = _mesh_pos()

        def push(s, t):
            return pltpu.make_async_remote_copy(
                src_ref=stage.at[s], dst_ref=out_ref.at[pl.ds(pl.multiple_of(t * tr, 16), tr)],
                send_sem=ssem.at[s], recv_sem=rsem, device_id=(mx, my, 1 - mc), device_id_type=MESH)

        @pl.when(i >= 2)
        def _():
            push(slot, 0).wait_send()

        stage[slot] = g_ref[...]
        push(slot, i).start()

        @pl.when(i == nt - 1)
        def _():
            push(slot, 0).wait_send()
            if nt >= 2:
                push(1 - slot, 0).wait_send()
            pltpu.make_async_remote_copy(src_ref=out_ref, dst_ref=out_ref, send_sem=ssem.at[0], recv_sem=rsem,
                                         device_id=(mx, my, 1 - mc), device_id_type=MESH).wait_recv()

    if kind == "col":
        g_spec = pl.BlockSpec((tr, hc), lambda i, c: ((1 - c[0]) * nt + i, 0))
    else:
        g_spec = pl.BlockSpec((tr, hc), lambda i, c: (i, 1 - c[0]))
    return pl.pallas_call(
        body, name=name,
        grid_spec=pltpu.PrefetchScalarGridSpec(
            num_scalar_prefetch=1, grid=(nt,), in_specs=[g_spec], out_specs=ANY,
            scratch_shapes=[pltpu.VMEM((2, tr, hc), BF16), pltpu.SemaphoreType.DMA((2,)), pltpu.SemaphoreType.DMA]),
        out_shape=jax.ShapeDtypeStruct((hr, hc), BF16),
        compiler_params=_cp("arbitrary"),
    )(c_arr, g)


def _partner_chip(k, p):
    return p ^ jnp.where(k == 0, 2, jnp.where(k == 1, 1, jnp.where(k == 2, 3, 0)))


def pair_add(g, got, kind, cp_arr, name):
    R, C = g.shape
    pr, pc = _piece_shape(kind, R, C)
    tr = _row_tile(pr, pc, itemsize=2, budget=2 * 1024 * 1024)
    nt = pr // tr

    def body(cp_ref, g_ref, got_ref, ps_ref, rb_ref):
        tile = (g_ref[...].astype(F32) + got_ref[...].astype(F32)).astype(BF16)
        ps_ref[...] = tile

        @pl.when(pl.program_id(1) == cp_ref[1])
        def _():
            rb_ref[...] = tile

    if kind == "col":
        g_spec = pl.BlockSpec((tr, pc), lambda i, q, cp: (cp[0] * nt + i, q))
        got_spec = pl.BlockSpec((tr, pc), lambda i, q, cp: (i, q))
    else:
        g_spec = pl.BlockSpec((tr, pc), lambda i, q, cp: (q * nt + i, cp[0]))
        got_spec = pl.BlockSpec((tr, pc), lambda i, q, cp: (q * nt + i, 0))
    return pl.pallas_call(
        body, name=name,
        grid_spec=pltpu.PrefetchScalarGridSpec(
            num_scalar_prefetch=1, grid=(nt, 4), in_specs=[g_spec, got_spec],
            out_specs=[pl.BlockSpec((None, tr, pc), lambda i, q, cp: (q, i, 0)),
                       pl.BlockSpec((None, tr, pc), lambda i, q, cp: (cp[1], i, 0))]),
        out_shape=[jax.ShapeDtypeStruct((4, pr, pc), BF16)] * 2,
        compiler_params=_cp("arbitrary", "arbitrary"),
    )(cp_arr, g, got)


def _rs_copies(ps, rb, ssem, rsem, mx, my, mc):
    p = 2 * mx + my
    out = []
    for w in range(len(ps)):
        for k, chip in enumerate(_other_chips(mx, my)):
            out.append(pltpu.make_async_remote_copy(
                src_ref=ps[w].at[2 * chip[0] + chip[1]], dst_ref=rb[w].at[p], send_sem=ssem.at[3 * w + k],
                recv_sem=rsem.at[3 * w + k], device_id=(*chip, mc), device_id_type=MESH))
    return out


def rs_start(ps, rb, after, name):
    n = len(ps)
    after = list(after)
    m = len(after)

    def body(*refs):
        ssem, rsem = refs[2 * n + m:2 * n + m + 2]
        ps_o = refs[2 * n + m + 2:3 * n + m + 2]
        rb_o = refs[3 * n + m + 2:4 * n + m + 2]
        token = refs[4 * n + m + 2]
        for cp in _rs_copies(ps_o, rb_o, ssem, rsem, *_mesh_pos()):
            cp.start()
        token[...] = jnp.zeros_like(token)

    both = list(ps) + list(rb)
    res = pl.pallas_call(
        body, name=name,
        out_shape=[pltpu.SemaphoreType.DMA((3 * n,)), pltpu.SemaphoreType.DMA((3 * n,))]
        + [pltpu.HBM(a.shape, a.dtype) for a in both] + [jax.ShapeDtypeStruct((8, 128), F32)],
        in_specs=[HBM_SPEC] * (2 * n) + [ANY] * m,
        out_specs=[SEM_SPEC, SEM_SPEC] + [HBM_SPEC] * (2 * n) + [pl.BlockSpec(memory_space=pltpu.VMEM)],
        input_output_aliases={w: 2 + w for w in range(2 * n)},
        compiler_params=pltpu.CompilerParams(has_side_effects=pltpu.SideEffectType.DATAFLOW_SIDE_EFFECTING),
    )(*[pltpu.with_memory_space_constraint(a, pltpu.HBM) for a in both], *after)
    return res[0], res[1], list(res[2:2 + n]), list(res[2 + n:2 + 2 * n]), res[2 + 2 * n]


def rs_wait(ps, rb, ssem, rsem, after, name):
    n = len(ps)
    after = list(after)
    m = len(after)

    def body(*refs):
        ps_i, rb_i = refs[:n], refs[n:2 * n]
        ssem_ref, rsem_ref = refs[2 * n], refs[2 * n + 1]
        for cp in _rs_copies(ps_i, rb_i, ssem_ref, rsem_ref, *_mesh_pos()):
            cp.wait_send()
            cp.wait_recv()

    both = list(ps) + list(rb)
    res = pl.pallas_call(
        body, name=name,
        out_shape=[pltpu.HBM(a.shape, a.dtype) for a in both],
        in_specs=[HBM_SPEC] * (2 * n) + [SEM_SPEC, SEM_SPEC] + [ANY] * m,
        out_specs=[HBM_SPEC] * (2 * n),
        input_output_aliases={w: w for w in range(2 * n)},
        compiler_params=pltpu.CompilerParams(has_side_effects=pltpu.SideEffectType.DATAFLOW_SIDE_EFFECTING),
    )(*both, ssem, rsem, *after)
    return list(res[n:])


def _slot_copies(blk, ssem, rsem):
    mx, my, mc = _mesh_pos()
    mine = blk.at[4 * mx + 2 * my + mc]
    peers = [(mx, my, 1 - mc)] + [(*chip, mc) for chip in _other_chips(mx, my)] \
        + [(*chip, 1 - mc) for chip in _other_chips(mx, my)]
    return [pltpu.make_async_remote_copy(src_ref=mine, dst_ref=mine, send_sem=ssem.at[k], recv_sem=rsem.at[k],
                                         device_id=peer, device_id_type=MESH) for k, peer in enumerate(peers)]


def slot_start(blk, name):
    def body(_, ssem, rsem, out, token):
        for cp in _slot_copies(out, ssem, rsem):
            cp.start()
        token[...] = jnp.zeros_like(token)

    return pl.pallas_call(
        body, name=name,
        out_shape=[pltpu.SemaphoreType.DMA((7,)), pltpu.SemaphoreType.DMA((7,)), pltpu.HBM(blk.shape, blk.dtype),
                   jax.ShapeDtypeStruct((8, 128), F32)],
        in_specs=[HBM_SPEC],
        out_specs=[SEM_SPEC, SEM_SPEC, HBM_SPEC, pl.BlockSpec(memory_space=pltpu.VMEM)],
        input_output_aliases={0: 2},
        compiler_params=pltpu.CompilerParams(has_side_effects=pltpu.SideEffectType.DATAFLOW_SIDE_EFFECTING),
    )(pltpu.with_memory_space_constraint(blk, pltpu.HBM))


def slot_wait(blk, ssem, rsem, after, name):
    after = list(after)

    def body(blk_ref, ssem_ref, rsem_ref, *_):
        for cp in _slot_copies(blk_ref, ssem_ref, rsem_ref):
            cp.wait_send()
            cp.wait_recv()

    return pl.pallas_call(
        body, name=name,
        out_shape=pltpu.HBM(blk.shape, blk.dtype),
        in_specs=[HBM_SPEC, SEM_SPEC, SEM_SPEC] + [ANY] * len(after),
        out_specs=HBM_SPEC,
        input_output_aliases={0: 0},
        compiler_params=pltpu.CompilerParams(has_side_effects=pltpu.SideEffectType.DATAFLOW_SIDE_EFFECTING),
    )(blk, ssem, rsem, *after)


def sum_share(parts, kind, R, C, name):
    _, pr, pc = parts.shape
    sr, sc = _shard_shape(kind, R, C)
    tr = _row_tile(pr, pc * 4, budget=8 * 1024 * 1024)
    nt = pr // tr

    def body(p_ref, fin_ref, stage, lsem, ssem, rsem):
        i = pl.program_id(0)
        slot = i % 2
        mx, my, mc = _mesh_pos()

        def region(h, t):
            r0 = pl.multiple_of(t * tr, 8)
            if kind == "col":
                return fin_ref.at[pl.ds(pl.multiple_of(h * pr + r0, 8), tr)]
            return fin_ref.at[pl.ds(r0, tr), pl.ds(h * pc, pc)]

        def copies(s, h, t):
            return (pltpu.make_async_copy(stage.at[s], region(h, t), lsem.at[s]),
                    pltpu.make_async_remote_copy(src_ref=stage.at[s], dst_ref=region(h, t), send_sem=ssem.at[s],
                                                 recv_sem=rsem, device_id=(mx, my, 1 - mc), device_id_type=MESH))

        def wait_sent(s):
            loc, rem = copies(s, 0, 0)
            loc.wait()
            rem.wait_send()

        @pl.when(i >= 2)
        def _():
            wait_sent(slot)

        acc = p_ref[0].astype(F32)
        for k in range(1, 4):
            acc = acc + p_ref[k].astype(F32)
        stage[slot] = acc
        if kind == "col":
            for cp in copies(slot, mc, i):
                cp.start()
        else:
            for h in range(2):
                @pl.when(mc == h)
                def _(h=h):
                    for cp in copies(slot, h, i):
                        cp.start()

        @pl.when(i == nt - 1)
        def _():
            wait_sent(slot)
            if nt >= 2:
                wait_sent(1 - slot)
            half = fin_ref.at[pl.ds(0, pr), pl.ds(0, pc)]
            pltpu.make_async_remote_copy(src_ref=half, dst_ref=half, send_sem=ssem.at[0], recv_sem=rsem,
                                         device_id=(mx, my, 1 - mc), device_id_type=MESH).wait_recv()

    return pl.pallas_call(
        body, name=name, grid=(nt,),
        in_specs=[pl.BlockSpec((4, tr, pc), lambda i: (0, i, 0))],
        out_specs=ANY,
        out_shape=jax.ShapeDtypeStruct((sr, sc), F32),
        scratch_shapes=[pltpu.VMEM((2, tr, pc), F32), pltpu.SemaphoreType.DMA((2,)), pltpu.SemaphoreType.DMA((2,)),
                        pltpu.SemaphoreType.DMA],
        compiler_params=_cp("arbitrary"),
    )(parts)


def _pack(parts, rows):
    flat = []
    for a in parts:
        a = jnp.ravel(a).astype(F32)
        flat.append(jnp.pad(a, (0, (-a.shape[0]) % 128)))
    v = jnp.concatenate(flat)
    return jnp.pad(v, (0, rows * 128 - v.shape[0])).reshape(rows, 128)


def _unpack(block, shapes):
    lead = block.shape[:-2]
    v = block.reshape(lead + (-1,))
    out, off = [], 0
    for shp in shapes:
        n = int(np.prod(shp))
        out.append(v[..., off:off + n].reshape(lead + tuple(shp)))
        off += n + (-n) % 128
    return out


def _block_diag4(w):
    w4 = w.reshape(4, 4, 64, 64)
    eye = jnp.eye(4, dtype=w.dtype)
    return (w4[:, :, :, None, :] * eye[None, :, None, :, None]).reshape(4, 256, 256)


def _diag_blocks(bd):
    b5 = bd.reshape(4, 4, 64, 4, 64)
    return jnp.stack([b5[:, i, :, i, :] for i in range(4)], axis=1).reshape(16, 64, 64)


def _bias_window(rel_bias):
    m = (np.arange(768) + 127) % 768 - 127
    w = rel_bias[:, np.clip(512 - m, -128, 128) + 128]
    win = jnp.tile(w, (1, 128))[:, :128 * 767].reshape(8, 128, 767)[:, :, :WIN]
    qh = np.arange(128)[:, None] // CHUNK
    kc = np.arange(WIN)[None, :] // CHUNK
    valid = (kc >= qh) & (kc <= qh + 8)
    return jnp.where(jnp.asarray(valid)[None], win, NEG)


SMALL = ("b_ada", "norm_pre", "norm_post", "rel_bias", "conv_w", "conv_b", "lru_wa", "lru_ba", "lru_wx",
         "lru_bx", "lru_lambda")
WEIGHTS = ("w_ada", "b_ada", "norm_pre", "norm_post", "ffn1_w_gu", "ffn1_w_down", "w_in", "rel_bias", "conv_w",
           "conv_b", "lru_wa", "lru_ba", "lru_wx", "lru_bx", "lru_lambda", "w_att_o", "w_rec_o", "w_out",
           "ffn2_w_gu", "ffn2_w_down")


def kernel(x, c, w_ada, b_ada, norm_pre, norm_post, ffn1_w_gu, ffn1_w_down, w_in, rel_bias, conv_w, conv_b, lru_wa, lru_ba, lru_wx, lru_bx, lru_lambda, w_att_o, w_rec_o, w_out, ffn2_w_gu, ffn2_w_down, loss_target, m_w_ada, m_b_ada, m_norm_pre, m_norm_post, m_ffn1_w_gu, m_ffn1_w_down, m_w_in, m_rel_bias, m_conv_w, m_conv_b, m_lru_wa, m_lru_ba, m_lru_wx, m_lru_bx, m_lru_lambda, m_w_att_o, m_w_rec_o, m_w_out, m_ffn2_w_gu, m_ffn2_w_down, v_w_ada, v_b_ada, v_norm_pre, v_norm_post, v_ffn1_w_gu, v_ffn1_w_down, v_w_in, v_rel_bias, v_conv_w, v_conv_b, v_lru_wa, v_lru_ba, v_lru_wx, v_lru_bx, v_lru_lambda, v_w_att_o, v_w_rec_o, v_w_out, v_ffn2_w_gu, v_ffn2_w_down):
    W = dict(w_ada=w_ada, b_ada=b_ada, norm_pre=norm_pre, norm_post=norm_post, ffn1_w_gu=ffn1_w_gu,
             ffn1_w_down=ffn1_w_down, w_in=w_in, rel_bias=rel_bias, conv_w=conv_w, conv_b=conv_b, lru_wa=lru_wa,
             lru_ba=lru_ba, lru_wx=lru_wx, lru_bx=lru_bx, lru_lambda=lru_lambda, w_att_o=w_att_o, w_rec_o=w_rec_o,
             w_out=w_out, ffn2_w_gu=ffn2_w_gu, ffn2_w_down=ffn2_w_down)
    M = dict(w_ada=m_w_ada, b_ada=m_b_ada, norm_pre=m_norm_pre, norm_post=m_norm_post, ffn1_w_gu=m_ffn1_w_gu,
             ffn1_w_down=m_ffn1_w_down, w_in=m_w_in, rel_bias=m_rel_bias, conv_w=m_conv_w, conv_b=m_conv_b,
             lru_wa=m_lru_wa, lru_ba=m_lru_ba, lru_wx=m_lru_wx, lru_bx=m_lru_bx, lru_lambda=m_lru_lambda,
             w_att_o=m_w_att_o, w_rec_o=m_w_rec_o, w_out=m_w_out, ffn2_w_gu=m_ffn2_w_gu, ffn2_w_down=m_ffn2_w_down)
    V = dict(w_ada=v_w_ada, b_ada=v_b_ada, norm_pre=v_norm_pre, norm_post=v_norm_post, ffn1_w_gu=v_ffn1_w_gu,
             ffn1_w_down=v_ffn1_w_down, w_in=v_w_in, rel_bias=v_rel_bias, conv_w=v_conv_w, conv_b=v_conv_b,
             lru_wa=v_lru_wa, lru_ba=v_lru_ba, lru_wx=v_lru_wx, lru_bx=v_lru_bx, lru_lambda=v_lru_lambda,
             w_att_o=v_w_att_o, w_rec_o=v_w_rec_o, w_out=v_w_out, ffn2_w_gu=v_ffn2_w_gu, ffn2_w_down=v_ffn2_w_down)
    mx, my, mc = _mesh_pos()
    p = 2 * mx + my
    e = 4 * mx + 2 * my + mc
    xs = x[0]

    c_arr = jnp.reshape(mc, (1,)).astype(jnp.int32)
    cp_arr = jnp.stack([mc, p]).astype(jnp.int32)
    p_arr = jnp.reshape(p, (1,)).astype(jnp.int32)
    direct = ("w_att_o", "w_rec_o", "w_out", "ffn2_w_gu", "ffn2_w_down")
    geoms = [(kind, R, C, n in direct) for (n, kind, R, C) in BIG]
    names = [b[0] for b in BIG]
    placed = [ag_local(W[n][0], kind, R, C, p_arr, "ag_local_" + n) for (n, kind, R, C) in BIG[:2]]

    def arrived(fly, lo, hi, ssem, rsem, after, tag):
        done = ag_wait(fly, geoms[lo:hi], ssem, rsem, after, "ag_wait_" + tag)
        return [a if both else ag_forward(a, kind, R, C, "ag_forward_" + n)
                for a, (kind, R, C, both), n in zip(done, geoms[lo:hi], names[lo:hi])]

    g1 = ag_small(_pack([c, norm_pre, norm_post, conv_w], 32), "ag_small_params")
    c_all, npre4, npost4, cw4 = _unpack(g1, [(D,), (3, 256), (3, 256), (4, 256)])
    chipwise = lambda a: jnp.moveaxis(a[0::2], 0, 1).reshape(a.shape[1], D)
    npre, npost, conv_full = chipwise(npre4), chipwise(npost4), chipwise(cw4)

    b_cols = lax.dynamic_slice(b_ada, (0, p * 2304), (1, 2304))
    mod_cols = ada_fwd(c_all, w_ada[0], b_cols, "ada_fwd")
    g2 = ag_small(mod_cols.reshape(144, 128), "ag_mod")
    mod_all = jnp.moveaxis(g2[0::2].reshape(4, 8, 2304), 0, 1).reshape(8, 9 * D)
    mod = lax.dynamic_index_in_dim(mod_all, e, 0, keepdims=False).reshape(3, 3, D)
    zeros3 = jnp.zeros((3, D), F32)
    vecs = [jnp.concatenate([npre[k:k + 1], npost[k:k + 1], mod[k], zeros3], axis=0) for k in range(3)]

    gu_s, gu_r, gu_fly, tok_gu = ag_start(placed[:1], geoms[:1], [g2], "ag_start_ffn1_gu")
    dn_s, dn_r, dn_fly, tok0 = ag_start(placed[1:2], geoms[1:2], [tok_gu], "ag_start_ffn1_down")
    placed += [ag_local(W[n][0], kind, R, C, p_arr, "ag_local_" + n, after=[tok0]) for (n, kind, R, C) in BIG[2:]]
    f1_gu, = arrived(gu_fly, 0, 1, gu_s, gu_r, placed[2:], "ffn1_gu")
    f1_dn, = arrived(dn_fly, 1, 2, dn_s, dn_r, f1_gu, "ffn1_down")
    mix_s, mix_r, mix_fly, tok1 = ag_start(placed[2:6], geoms[2:6], [f1_gu, f1_dn], "ag_start_mixer")
    ffn_s, ffn_r, ffn_fly, tok2 = ag_start(placed[6:], geoms[6:], [tok1], "ag_start_ffn2")
    wa_bd = _block_diag4(lru_wa[0]).astype(BF16)
    wx_bd = _block_diag4(lru_wx[0]).astype(BF16)
    pvec = jnp.concatenate([conv_full, conv_b, lru_ba, lru_bx, lru_lambda], axis=0)
    bias = _bias_window(rel_bias[0]).reshape(4, 256, WIN)

    x1, h1, g1_, u1, a1, f1 = ffn_fwd(xs, vecs[0] + tok2[0:1, 0:1], f1_gu, f1_dn, 0.5, "ffn1_fwd")
    win, wao, wro, wout = arrived(mix_fly, 2, 6, mix_s, mix_r, x1, "mixer")
    h2, qkv, rest = proj_fwd(x1, vecs[1], win, "proj_fwd")
    ao = attn_fwd(qkv, bias, "attn_fwd")
    hl, hg = lru_fwd(rest, pvec, wa_bd, wx_bd, "lru_fwd")
    x2, att, rec, mg, f2 = mix_out_fwd(x1, ao, hg, rest, vecs[1], wao, wro, wout, "mix_out_fwd")
    f2_gu, f2_dn = arrived(ffn_fly, 6, 8, ffn_s, ffn_r, x2, "ffn2")
    dy, h3, g3_, u3, a3, f3, lvec = ffn_fwd(x2, vecs[2], f2_gu, f2_dn, 0.5, "ffn2_fwd", tgt=loss_target[0])

    G, grads = {}, {}
    geo = {n: (kind, R, C) for (n, kind, R, C) in BIG}

    def reduce_begin(names, tag):
        ps, rb = [], []
        for n in names:
            got = pair_push(G[n], geo[n][0], c_arr, "rs_push_" + n)
            a, b = pair_add(G[n], got, geo[n][0], cp_arr, "rs_pair_sum_" + n)
            ps.append(a)
            rb.append(b)
        return rs_start(ps, rb, [], "rs_start_" + tag)

    def reduce_end(names, flight, after, tag):
        ssem, rsem, ps, rb, _ = flight
        for a, n in zip(rs_wait(ps, rb, ssem, rsem, after, "rs_wait_" + tag), names):
            grads[n] = sum_share(a, *geo[n], "rs_sum_share_" + n)[None]

    dx2, df3, dgu3, va2 = ffn_bwd(dy, x2, f3, g3_, u3, vecs[2], f2_gu, f2_dn, 0.5, "ffn2_bwd")
    G["ffn2_w_gu"] = mm_tn(h3, dgu3, "dw_ffn2_gu", D, 1408, 2048, a_resident=True)
    G["ffn2_w_down"] = mm_tn(a3, df3, "dw_ffn2_down", 1408, D, 2048)
    fly_ffn2 = reduce_begin(("ffn2_w_gu", "ffn2_w_down"), "ffn2")
    vec1 = vecs[1] + fly_ffn2[4][0:1, 0:1]
    df2, d_att, d_rec, dao, dhl, d3, va_out = mix_out_bwd(dx2, f2, att, rec, rest, hl, vec1, wao, wro, wout,
                                                          "mix_out_bwd")
    G["w_out"] = mm_tn(mg, df2, "dw_out", D, D, 1024)
    G["w_att_o"] = mm_tn(ao, d_att, "dw_att_o", 512, D, 1024)
    G["w_rec_o"] = mm_tn(hg, d_rec, "dw_rec_o", D, D, 1024)
    dq, db, dkv = attn_bwd(qkv, dao, bias, "attn_bwd")
    dxr, v_lru, dwa_bd, dwx_bd = lru_bwd(dhl, hl, rest, pvec, wa_bd, wx_bd, "lru_bwd")
    lru_w = jnp.concatenate([_diag_blocks(dwa_bd), _diag_blocks(dwx_bd)]).reshape(1, 1024, 128)
    lru_slots = lax.dynamic_update_slice(jnp.zeros((N_DEV, 1024, 128), F32), lru_w, (e, 0, 0))
    lw_s, lw_r, lw_fly, lw_tok = slot_start(lru_slots, "lru_w_start")
    dx1, va_in = proj_bwd(dq, dkv, dxr, d3, win, x1, dx2, vecs[1] + lw_tok[0:1, 0:1], "proj_bwd")
    G["w_in"] = dw_in(h2, dq, dkv, dxr, d3, "dw_in")
    fly_mix = reduce_begin(("w_in", "w_att_o", "w_rec_o", "w_out"), "mixer")
    vec0 = vecs[0] + fly_mix[4][0:1, 0:1]
    dx0, df1, dgu1, va0 = ffn_bwd(dx1, xs, f1, g1_, u1, vec0, f1_gu, f1_dn, 0.5, "ffn1_bwd")
    G["ffn1_w_gu"] = mm_tn(h1, dgu1, "dw_ffn1_gu", D, 1408, 2048, a_resident=True)
    G["ffn1_w_down"] = mm_tn(a1, df1, "dw_ffn1_down", 1408, D, 2048)
    fly_ffn1 = reduce_begin(("ffn1_w_gu", "ffn1_w_down"), "ffn1")
    reduce_end(("ffn2_w_gu", "ffn2_w_down"), fly_ffn2, [fly_ffn1[4]], "ffn2")
    reduce_end(("w_in", "w_att_o", "w_rec_o", "w_out"), fly_mix, [fly_ffn1[4], grads["ffn2_w_down"]], "mixer")

    va1 = va_out + va_in
    vas = (va0, va1, va2)
    dmod = jnp.stack([v[2:5] for v in vas])
    part = {"b_ada": dmod, "norm_pre": jnp.stack([v[0] for v in vas]), "norm_post": jnp.stack([v[1] for v in vas]),
            "rel_bias": bias_grad(db.reshape(8, 128, WIN), "bias_grad")[:, :257], "conv_w": v_lru[0:4], "conv_b": v_lru[4],
            "lru_ba": v_lru[5], "lru_bx": v_lru[6], "lru_lambda": v_lru[7]}
    full_shapes = {"b_ada": (9 * D,), "norm_pre": (3, D), "norm_post": (3, D), "rel_bias": (8, 257),
                   "conv_w": (4, D), "conv_b": (D,), "lru_wa": (16, 64, 64), "lru_ba": (D,),
                   "lru_wx": (16, 64, 64), "lru_bx": (D,), "lru_lambda": (D,)}
    gathered = [n for n in SMALL if n in part]
    g3 = ag_small(_pack([part[n] for n in gathered] + [lvec[0:1, 0:1]], 208), "ag_small_grads")
    summed = _unpack(sum_lead(g3, "sum_small_grads"), [full_shapes[n] for n in gathered] + [(1,)])
    red = dict(zip(gathered, summed[:-1]))
    loss = summed[-1][0]
    lru_all = slot_wait(lw_fly, lw_s, lw_r, [dx0], "lru_w_wait")
    red["lru_wa"], red["lru_wx"] = sum_lead(lru_all, "sum_lru_w").reshape(2, 16, 64, 64)
    cols = lambda a: lax.dynamic_slice(a, (0, p * 256), (a.shape[0], 256))
    grads.update({"b_ada": red["b_ada"][None], "norm_pre": cols(red["norm_pre"])[None],
                  "norm_post": cols(red["norm_post"])[None], "rel_bias": red["rel_bias"][None],
                  "conv_w": cols(red["conv_w"])[None], "conv_b": red["conv_b"][None], "lru_wa": red["lru_wa"][None],
                  "lru_ba": red["lru_ba"][None], "lru_wx": red["lru_wx"][None], "lru_bx": red["lru_bx"][None],
                  "lru_lambda": red["lru_lambda"][None]})

    dmod_all = g3[:, :72].reshape(8, 9 * D)
    dmod_cols = jnp.pad(lax.dynamic_slice(dmod_all, (0, p * 2304), (8, 2304)), ((0, 120), (0, 0)))
    c_all_t = jnp.pad(c_all.T, ((0, 0), (0, 120)))
    grads["w_ada"] = ada_bwd(c_all_t, dmod_cols, "ada_bwd")[None]

    delta, new_m, new_v = {}, {}, {}

    def update(n):
        shp = W[n].shape
        res = adamw(W[n][0], grads[n][0], M[n][0], V[n][0], "adamw_" + n, emit_g=n in geo)
        delta[n], new_m[n], new_v[n] = [a.reshape(shp) for a in res[:3]]
        if n in geo:
            grads[n] = res[3].reshape(shp)

    for n in ("w_ada", "ffn2_w_gu", "ffn2_w_down", "w_in", "w_att_o", "w_rec_o", "w_out"):
        update(n)
    packed = [_pack([src[n] for n in SMALL], 1168) for src in (W, grads, M, V)]
    outs = adamw(*packed, "adamw_small")
    for dst, blk in zip((delta, new_m, new_v), outs):
        for n, a in zip(SMALL, _unpack(blk, [W[n].shape for n in SMALL])):
            dst[n] = a
    reduce_end(("ffn1_w_gu", "ffn1_w_down"), fly_ffn1,
               [outs[0], delta["w_ada"], delta["ffn2_w_gu"], delta["ffn2_w_down"], delta["w_in"], delta["w_out"]], "ffn1")
    for n in ("ffn1_w_gu", "ffn1_w_down"):
        update(n)

    return (loss, dx0[None], *[grads[n] for n in WEIGHTS], *[delta[n] for n in WEIGHTS],
            *[new_m[n] for n in WEIGHTS], *[new_v[n] for n in WEIGHTS])
```

```python
import numpy as np
import jax
import jax.numpy as jnp
from jax import lax
from jax.experimental import pallas as pl
from jax.experimental.pallas import tpu as pltpu

F32 = jnp.float32
BF16 = jnp.bfloat16

D = 1024
FF = 2816
PW = 5632
HP = 128
CHUNK = 64
WIN = 640
TQ = 512
EPS = 1e-6
NEG = -1e30
LRU_C = 8.0
N_DEV = 8
VMEM_LIMIT = 56 * 1024 * 1024

ADAM_LR, ADAM_B1, ADAM_B2, ADAM_EPS, ADAM_WD, ADAM_STEP = 0.001, 0.9, 0.999, 1e-08, 0.01, 10

MESH = pl.DeviceIdType.MESH
ANY = pl.BlockSpec(memory_space=pl.ANY)


def _cp(*sem):
    return pltpu.CompilerParams(dimension_semantics=tuple(sem), vmem_limit_bytes=VMEM_LIMIT)


def _dot(a, b):
    return jnp.dot(a, b, preferred_element_type=F32)


def _dot_nt(a, b):
    return lax.dot_general(a, b, (((1,), (1,)), ((), ())), preferred_element_type=F32)


def _dot_tn(a, b):
    return lax.dot_general(a, b, (((0,), (0,)), ((), ())), preferred_element_type=F32)


def _mean(v):
    return jnp.mean(v, axis=-1, keepdims=True)


def _colsum(v):
    return jnp.sum(v, axis=0, keepdims=True)


def _sigmoid(v):
    return 0.5 * jnp.tanh(0.5 * v) + 0.5


_GK = 0.7978845608028654


def _gelu(v):
    t = jnp.tanh(_GK * (v + 0.044715 * v * v * v))
    return 0.5 * v * (1.0 + t)


def _pre_norm(xv, vec_ref):
    r = lax.rsqrt(_mean(xv * xv) + EPS)
    n = xv * r * vec_ref[0:1, :]
    return n * (1.0 + vec_ref[3:4, :]) + vec_ref[2:3, :]


def _pre_norm_bwd(dh, xv, dres, vec_ref, vacc_ref):
    r = lax.rsqrt(_mean(xv * xv) + EPS)
    xh = xv * r
    n = xh * vec_ref[0:1, :]
    vacc_ref[2:3, :] += _colsum(dh)
    vacc_ref[3:4, :] += _colsum(dh * n)
    dn = dh * (1.0 + vec_ref[3:4, :])
    vacc_ref[0:1, :] += _colsum(dn * xh)
    dxh = dn * vec_ref[0:1, :]
    return r * (dxh - xh * _mean(dxh * xh)) + dres


def _post_norm_bwd(dxo, fv, res, vec_ref, vacc_ref):
    rf = lax.rsqrt(_mean(fv * fv) + EPS)
    fh = fv * rf
    gp = vec_ref[1:2, :]
    vacc_ref[4:5, :] += _colsum(res * dxo * (fh * gp))
    dy = (res * vec_ref[4:5, :]) * dxo
    vacc_ref[1:2, :] += _colsum(dy * fh)
    dfn = dy * gp
    return rf * (dfn - fh * _mean(dfn * fh))


def _u_spec(tf):
    return pl.BlockSpec((pl.Element(D), pl.Element(tf)),
                        lambda i, j: (0, pl.multiple_of(jnp.minimum(FF + j * tf, 2 * FF - tf), 128)))


def ffn_fwd(x, vec, w_gu, w_dn, res, name, tgt=None, tm=1024, tf=512):
    S = x.shape[0]
    tm = min(tm, S)
    nt = S // tm
    nf = -(-FF // tf)
    tail = FF - tf * (nf - 1)
    halves = [pl.ds(r * (tm // 2), tm // 2) for r in range(2)]
    head = tgt is not None

    def body(*refs):
        x_ref, vec_ref, wg_ref, wu_ref, wd_ref = refs[:5]
        if head:
            t_hbm, xo_ref, h_ref, g_ref, u_ref, a_ref, f_ref, l_ref, hs, acc, lacc, ts, tsem = refs[5:]
        else:
            xo_ref, h_ref, g_ref, u_ref, a_ref, f_ref, hs, acc = refs[5:]
        i, j = pl.program_id(0), pl.program_id(1)
        if head:
            late = pltpu.make_async_copy(t_hbm.at[pl.ds(pl.multiple_of(i * tm, 8), tm)], ts, tsem)

        @pl.when(j == 0)
        def _():
            if head:
                late.start()
            h = _pre_norm(x_ref[...], vec_ref).astype(BF16)
            hs[...] = h
            h_ref[...] = h
            acc[...] = jnp.zeros_like(acc)

        def chunk(w):
            gu = [(_dot(hs[r, :], wg_ref[:, 0:w]), _dot(hs[r, :], wu_ref[:, tf - w:tf])) for r in halves]
            acts = []
            for r, (g, u) in zip(halves, gu):
                g_ref[r, 0:w] = g.astype(BF16)
                u_ref[r, 0:w] = u.astype(BF16)
                a = (g * _sigmoid(g) * u).astype(BF16)
                a_ref[r, 0:w] = a
                acts.append(a)
            for r, a in zip(halves, acts):
                acc[r, :] += _dot(a, wd_ref[0:w, :])

        @pl.when(j < nf - 1)
        def _():
            chunk(tf)

        @pl.when(j == nf - 1)
        def _():
            chunk(tail)
            f = acc[...]
            f_ref[...] = f.astype(BF16)
            y = f * lax.rsqrt(_mean(f * f) + EPS) * vec_ref[1:2, :]
            xo = x_ref[...] + (res * vec_ref[4:5, :]) * y
            if head:
                @pl.when(i == 0)
                def _():
                    lacc[...] = jnp.zeros_like(lacc)

                late.wait()
                d = xo - ts[...]
                xo_ref[...] = d * (1.0 / D)
                lacc[...] += _colsum(d * d)

                @pl.when(i == nt - 1)
                def _():
                    l_ref[...] = jnp.broadcast_to(0.5 * jnp.sum(lacc[...]) * (1.0 / D), (8, 128))
            else:
                xo_ref[...] = xo

    row = lambda i, j: (i, 0)
    col = lambda i, j: (i, j)
    in_specs = [pl.BlockSpec((tm, D), row), pl.BlockSpec((8, D), lambda i, j: (0, 0)),
                pl.BlockSpec((D, tf), lambda i, j: (0, j)), _u_spec(tf),
                pl.BlockSpec((tf, D), lambda i, j: (j, 0))]
    out_specs = [pl.BlockSpec((tm, D), row), pl.BlockSpec((tm, D), row), pl.BlockSpec((tm, tf), col),
                 pl.BlockSpec((tm, tf), col), pl.BlockSpec((tm, tf), col), pl.BlockSpec((tm, D), row)]
    out_shape = [jax.ShapeDtypeStruct((S, D), F32), jax.ShapeDtypeStruct((S, D), BF16),
                 jax.ShapeDtypeStruct((S, FF), BF16), jax.ShapeDtypeStruct((S, FF), BF16),
                 jax.ShapeDtypeStruct((S, FF), BF16), jax.ShapeDtypeStruct((S, D), BF16)]
    scratch = [pltpu.VMEM((tm, D), BF16), pltpu.VMEM((tm, D), F32)]
    args = [x, vec, w_gu, w_gu, w_dn]
    if head:
        in_specs.append(ANY)
        out_specs.append(pl.BlockSpec((8, 128), lambda i, j: (0, 0)))
        out_shape.append(jax.ShapeDtypeStruct((8, 128), F32))
        scratch += [pltpu.VMEM((1, D), F32), pltpu.VMEM((tm, D), F32), pltpu.SemaphoreType.DMA]
        args.append(tgt)
    return pl.pallas_call(
        body, name=name, grid=(nt, nf), in_specs=in_specs, out_specs=out_specs, out_shape=out_shape,
        scratch_shapes=scratch,
        compiler_params=_cp("arbitrary" if head else "parallel", "arbitrary"),
    )(*args)


def ffn_bwd(dxo, x, f, g, u, vec, w_gu, w_dn, res, name, tm=1024, tf=512):
    S = x.shape[0]
    tm = min(tm, S)
    nf = -(-FF // tf)
    tail = FF - tf * (nf - 1)
    halves = [pl.ds(r * (tm // 2), tm // 2) for r in range(2)]

    def body(dxo_ref, x_hbm, f_ref, g_ref, u_ref, vec_ref, wg_ref, wu_ref, wd_ref,
             dx_ref, df_ref, dgu_ref, vacc_ref, dfs, acc, xs, xsem):
        i, j = pl.program_id(0), pl.program_id(1)
        late = pltpu.make_async_copy(x_hbm.at[pl.ds(pl.multiple_of(i * tm, 8), tm)], xs, xsem)

        @pl.when((i == 0) & (j == 0))
        def _():
            vacc_ref[...] = jnp.zeros_like(vacc_ref)

        @pl.when(j == 0)
        def _():
            late.start()
            df = _post_norm_bwd(dxo_ref[...], f_ref[...].astype(F32), res, vec_ref, vacc_ref).astype(BF16)
            dfs[...] = df
            df_ref[...] = df
            acc[...] = jnp.zeros_like(acc)

        def chunk(w):
            da = [_dot_nt(dfs[h, :], wd_ref[0:w, :]) for h in halves]
            dgu = []
            for h, d in zip(halves, da):
                gv, uv = g_ref[h, 0:w].astype(F32), u_ref[h, 0:w].astype(F32)
                sg = _sigmoid(gv)
                dg = (d * uv * (sg * (1.0 + gv * (1.0 - sg)))).astype(BF16)
                du = (d * (gv * sg)).astype(BF16)
                dgu_ref[0, h, 0:w] = dg
                dgu_ref[1, h, 0:w] = du
                dgu.append((dg, du))
            for h, (dg, du) in zip(halves, dgu):
                acc[h, :] += _dot_nt(dg, wg_ref[:, 0:w]) + _dot_nt(du, wu_ref[:, tf - w:tf])

        @pl.when(j < nf - 1)
        def _():
            chunk(tf)

        @pl.when(j == nf - 1)
        def _():
            chunk(tail)
            late.wait()
            dx_ref[...] = _pre_norm_bwd(acc[...], xs[...], dxo_ref[...], vec_ref, vacc_ref)

    row = lambda i, j: (i, 0)
    col = lambda i, j: (i, j)
    return pl.pallas_call(
        body, name=name, grid=(S // tm, nf),
        in_specs=[pl.BlockSpec((tm, D), row), ANY, pl.BlockSpec((tm, D), row),
                  pl.BlockSpec((tm, tf), col), pl.BlockSpec((tm, tf), col),
                  pl.BlockSpec((8, D), lambda i, j: (0, 0)),
                  pl.BlockSpec((D, tf), lambda i, j: (0, j)), _u_spec(tf),
                  pl.BlockSpec((tf, D), lambda i, j: (j, 0))],
        out_specs=[pl.BlockSpec((tm, D), row), pl.BlockSpec((tm, D), row),
                   pl.BlockSpec((2, tm, tf), lambda i, j: (0, i, j)),
                   pl.BlockSpec((8, D), lambda i, j: (0, 0))],
        out_shape=[jax.ShapeDtypeStruct((S, D), F32), jax.ShapeDtypeStruct((S, D), BF16),
                   jax.ShapeDtypeStruct((2, S, FF), BF16), jax.ShapeDtypeStruct((8, D), F32)],
        scratch_shapes=[pltpu.VMEM((tm, D), BF16), pltpu.VMEM((tm, D), F32), pltpu.VMEM((tm, D), F32),
                        pltpu.SemaphoreType.DMA],
        compiler_params=_cp("arbitrary", "arbitrary"),
    )(dxo, x, f, g, u, vec, w_gu, w_gu, w_dn)


def mm_tn(a, b, name, tm, tn, tk, out_dtype=BF16, a_resident=False):
    S, M = a.shape
    if b.ndim == 3:
        G, _, Nf = b.shape
    else:
        G, Nf = 1, b.shape[1]
    N = G * Nf
    tk = min(tk, S)
    nbf = Nf // tn
    nk = S // tk

    def body(a_ref, b_ref, o_ref, acc):
        k = pl.program_id(2)

        @pl.when(k == 0)
        def _():
            acc[...] = jnp.zeros_like(acc)

        a_blk = a_ref[pl.ds(pl.multiple_of(k * tk, 16), tk), :] if a_resident else a_ref[...]
        acc[...] += _dot_tn(a_blk, b_ref[...])

        @pl.when(k == nk - 1)
        def _():
            o_ref[...] = acc[...].astype(out_dtype)

    if b.ndim == 3:
        b_spec = pl.BlockSpec((None, tk, tn), lambda i, j, k: (j // nbf, k, j % nbf))
    else:
        b_spec = pl.BlockSpec((tk, tn), lambda i, j, k: (k, j))
    if a_resident:
        a_spec = pl.BlockSpec((S, M), lambda i, j, k: (0, 0), pipeline_mode=pl.Buffered(1))
    else:
        a_spec = pl.BlockSpec((tk, tm), lambda i, j, k: (k, i))
    return pl.pallas_call(
        body, name=name, grid=(M // tm, N // tn, nk),
        in_specs=[a_spec, b_spec],
        out_specs=pl.BlockSpec((tm, tn), lambda i, j, k: (i, j)),
        out_shape=jax.ShapeDtypeStruct((M, N), out_dtype),
        scratch_shapes=[pltpu.VMEM((tm, tn), F32)],
        compiler_params=_cp("parallel", "parallel", "arbitrary"),
    )(a, b)


def proj_fwd(x, vec, w_in, name, tm=2048, tn=512):
    S = x.shape[0]
    tm = min(tm, S)
    nq = 1536 // tn

    def body(x_ref, vec_ref, w_ref, h_ref, qkv_ref, rest_ref, hs):
        j = pl.program_id(1)

        @pl.when(j == 0)
        def _():
            h = _pre_norm(x_ref[...], vec_ref).astype(BF16)
            hs[...] = h
            h_ref[...] = h

        r = _dot(hs[...], w_ref[...])

        @pl.when(j < nq)
        def _():
            qkv_ref[...] = r.astype(BF16)

        @pl.when(j >= nq)
        def _():
            rest_ref[...] = r.astype(BF16)

    row = lambda i, j: (i, 0)
    return pl.pallas_call(
        body, name=name, grid=(S // tm, PW // tn),
        in_specs=[pl.BlockSpec((tm, D), row), pl.BlockSpec((8, D), lambda i, j: (0, 0)),
                  pl.BlockSpec((D, tn), lambda i, j: (0, j))],
        out_specs=[pl.BlockSpec((tm, D), row),
                   pl.BlockSpec((tm, tn), lambda i, j: (i, jnp.minimum(j, nq - 1))),
                   pl.BlockSpec((tm, tn), lambda i, j: (i, jnp.maximum(j - nq, 0)))],
        out_shape=[jax.ShapeDtypeStruct((S, D), BF16), jax.ShapeDtypeStruct((S, 1536), BF16),
                   jax.ShapeDtypeStruct((S, 4096), BF16)],
        scratch_shapes=[pltpu.VMEM((tm, D), BF16)],
        compiler_params=_cp("parallel", "arbitrary"),
    )(x, vec, w_in)


def proj_bwd(dq, dkv, dxr, d3, w_in, x, dxo, vec, name, tm=2048, tk=512):
    S = x.shape[0]
    tm = min(tm, S)
    nk = PW // tk

    def body(dq_ref, dkv_ref, dxr_ref, d3_ref, w_ref, x_hbm, dxo_hbm, vec_ref, dx_ref, vacc_ref, acc, xs, dxos, sems):
        i, j = pl.program_id(0), pl.program_id(1)
        tok = pl.ds(pl.multiple_of(i * tm, 8), tm)
        late = (pltpu.make_async_copy(x_hbm.at[tok], xs, sems.at[0]),
                pltpu.make_async_copy(dxo_hbm.at[tok], dxos, sems.at[1]))

        @pl.when((i == 0) & (j == 0))
        def _():
            vacc_ref[...] = jnp.zeros_like(vacc_ref)

        @pl.when(j == 0)
        def _():
            for cp in late:
                cp.start()
            acc[...] = _dot_nt(dq_ref[...], w_ref[...])

        @pl.when((j >= 1) & (j < 3))
        def _():
            acc[...] += _dot_nt(dkv_ref[...], w_ref[...])

        @pl.when((j >= 3) & (j < 5))
        def _():
            acc[...] += _dot_nt(dxr_ref[...], w_ref[...])

        @pl.when(j >= 5)
        def _():
            acc[...] += _dot_nt(d3_ref[...], w_ref[...])

        @pl.when(j == nk - 1)
        def _():
            for cp in late:
                cp.wait()
            dx_ref[...] = _pre_norm_bwd(acc[...], xs[...], dxos[...], vec_ref, vacc_ref)

    row = lambda i, j: (i, 0)
    return pl.pallas_call(
        body, name=name, grid=(S // tm, nk),
        in_specs=[pl.BlockSpec((None, tm, tk), lambda i, j: (0, i, 0)),
                  pl.BlockSpec((None, tm, tk), lambda i, j: (jnp.clip(j - 1, 0, 1), i, 0)),
                  pl.BlockSpec((tm, tk), lambda i, j: (i, jnp.clip(j - 3, 0, 1))),
                  pl.BlockSpec((None, tm, tk), lambda i, j: (jnp.clip(j - 5, 0, 5) // 2, i, jnp.clip(j - 5, 0, 5) % 2)),
                  pl.BlockSpec((D, tk), lambda i, j: (0, j)),
                  ANY, ANY,
                  pl.BlockSpec((8, D), lambda i, j: (0, 0))],
        out_specs=[pl.BlockSpec((tm, D), row, pipeline_mode=pl.Buffered(1)),
                   pl.BlockSpec((8, D), lambda i, j: (0, 0))],
        out_shape=[jax.ShapeDtypeStruct((S, D), F32), jax.ShapeDtypeStruct((8, D), F32)],
        scratch_shapes=[pltpu.VMEM((tm, D), F32), pltpu.VMEM((tm, D), F32), pltpu.VMEM((tm, D), F32),
                        pltpu.SemaphoreType.DMA((2,))],
        compiler_params=_cp("arbitrary", "arbitrary"),
    )(dq, dkv, dxr, d3, w_in, x, dxo, vec)


def _two_heads(v, lane):
    zero = jnp.zeros((), v.dtype)
    return jnp.concatenate([jnp.where(lane < 64, v, zero), jnp.where(lane >= 64, v, zero)], axis=0)


def _attn_scores(qm, ka, bias_h, i, grp):
    s = _dot_nt(qm, ka) + bias_h
    col = lax.broadcasted_iota(jnp.int32, s.shape, 1)
    first_key = jnp.where(i == 0, 512 - 128 * grp, 0)
    return jnp.where(col >= first_key, s, NEG)


def _softmax(s):
    e = jnp.exp(s - jnp.max(s, axis=-1, keepdims=True))
    return e * (1.0 / jnp.sum(e, axis=-1, keepdims=True))


NG = TQ // 128


def attn_fwd(qkv, bias, name):
    S = qkv.shape[0]
    nb = S // TQ

    def body(q_ref, kp_ref, kc_ref, vp_ref, vc_ref, b_ref, o_ref, kw, vw):
        i = pl.program_id(1)
        kw[0:TQ, :] = kp_ref[...]
        kw[TQ:2 * TQ, :] = kc_ref[...]
        vw[0:TQ, :] = vp_ref[...]
        vw[TQ:2 * TQ, :] = vc_ref[...]
        lane = lax.broadcasted_iota(jnp.int32, (1, HP), 1)

        rows = [pl.ds(128 * a, 128) for a in range(NG)]
        keys = [pl.ds(128 * a, WIN) for a in range(NG)]
        q2 = [_two_heads(q_ref[r, :] * jnp.asarray(0.125, BF16), lane) for r in rows]
        s = [_attn_scores(q2[a], kw[keys[a], :], b_ref[...], i, a) for a in range(NG)]
        p = [_softmax(sa).astype(BF16) for sa in s]
        o2 = [_dot(p[a], vw[keys[a], :]) for a in range(NG)]
        for a in range(NG):
            o_ref[rows[a], :] = jnp.where(lane < 64, o2[a][0:128], o2[a][128:256]).astype(BF16)

    prev = lambda h, i: (jnp.maximum(i - 1, 0), 0)
    return pl.pallas_call(
        body, name=name, grid=(4, nb),
        in_specs=[pl.BlockSpec((TQ, HP), lambda h, i: (i, h)),
                  pl.BlockSpec((TQ, HP), lambda h, i: (jnp.maximum(i - 1, 0), 4 + h)),
                  pl.BlockSpec((TQ, HP), lambda h, i: (i, 4 + h)),
                  pl.BlockSpec((TQ, HP), lambda h, i: (jnp.maximum(i - 1, 0), 8 + h)),
                  pl.BlockSpec((TQ, HP), lambda h, i: (i, 8 + h)),
                  pl.BlockSpec((None, 256, WIN), lambda h, i: (h, 0, 0))],
        out_specs=pl.BlockSpec((TQ, HP), lambda h, i: (i, h)),
        out_shape=jax.ShapeDtypeStruct((S, 512), BF16),
        scratch_shapes=[pltpu.VMEM((2 * TQ, HP), BF16), pltpu.VMEM((2 * TQ, HP), BF16)],
        compiler_params=_cp("parallel", "arbitrary"),
    )(qkv, qkv, qkv, qkv, qkv, bias)


def attn_bwd(qkv, do, bias, name):
    S = qkv.shape[0]
    nb = S // TQ

    def body(q_ref, kp_ref, kc_ref, vp_ref, vc_ref, do_ref, b_ref, dqkv_ref, db_ref, dkv_ref, kw, vw, ak, av):
        i = pl.program_id(1)

        @pl.when(i == 0)
        def _():
            db_ref[...] = jnp.zeros_like(db_ref)
            ak[...] = jnp.zeros_like(ak)
            av[...] = jnp.zeros_like(av)

        @pl.when(i > 0)
        def _():
            ak[0:TQ, :] = ak[TQ:2 * TQ, :]
            av[0:TQ, :] = av[TQ:2 * TQ, :]
            ak[TQ:2 * TQ, :] = jnp.zeros((TQ, HP), F32)
            av[TQ:2 * TQ, :] = jnp.zeros((TQ, HP), F32)

        @pl.when(i < nb)
        def _():
            kw[0:TQ, :] = kp_ref[...]
            kw[TQ:2 * TQ, :] = kc_ref[...]
            vw[0:TQ, :] = vp_ref[...]
            vw[TQ:2 * TQ, :] = vc_ref[...]
            lane = lax.broadcasted_iota(jnp.int32, (1, HP), 1)

            rows = [pl.ds(128 * a, 128) for a in range(NG)]
            keys = [pl.ds(128 * a, WIN) for a in range(NG)]
            q2 = [_two_heads(q_ref[r, :] * jnp.asarray(0.125, BF16), lane) for r in rows]
            do2 = [_two_heads(do_ref[r, :], lane) for r in rows]
            s = [_attn_scores(q2[a], kw[keys[a], :], b_ref[...], i, a) for a in range(NG)]
            dp = [_dot_nt(do2[a], vw[keys[a], :]) for a in range(NG)]
            p = [_softmax(sa) for sa in s]
            ds = [p[a] * (dp[a] - jnp.sum(p[a] * dp[a], axis=-1, keepdims=True)) for a in range(NG)]
            db_ref[...] += (ds[0] + ds[1]) + (ds[2] + ds[3])
            dsb = [d.astype(BF16) for d in ds]
            dq2 = [_dot(dsb[a], kw[keys[a], :]) for a in range(NG)]
            dk = [_dot_tn(dsb[a], q2[a]) for a in range(NG)]
            dv = [_dot_tn(p[a].astype(BF16), do2[a]) for a in range(NG)]
            for a in range(NG):
                ak[keys[a], :] += dk[a]
                av[keys[a], :] += dv[a]
                dq = jnp.where(lane < 64, dq2[a][0:128], dq2[a][128:256])
                dqkv_ref[0, rows[a], :] = (dq * 0.125).astype(BF16)

        @pl.when(i > 0)
        def _():
            dkv_ref[0] = ak[0:TQ, :].astype(BF16)
            dkv_ref[1] = av[0:TQ, :].astype(BF16)

    cur = lambda i: jnp.minimum(i, nb - 1)
    prv = lambda i: jnp.clip(i - 1, 0, nb - 1)
    dq, db, dkv = pl.pallas_call(
        body, name=name, grid=(4, nb + 1),
        in_specs=[pl.BlockSpec((TQ, HP), lambda h, i: (cur(i), h)),
                  pl.BlockSpec((TQ, HP), lambda h, i: (prv(i), 4 + h)),
                  pl.BlockSpec((TQ, HP), lambda h, i: (cur(i), 4 + h)),
                  pl.BlockSpec((TQ, HP), lambda h, i: (prv(i), 8 + h)),
                  pl.BlockSpec((TQ, HP), lambda h, i: (cur(i), 8 + h)),
                  pl.BlockSpec((TQ, HP), lambda h, i: (cur(i), h)),
                  pl.BlockSpec((None, 256, WIN), lambda h, i: (h, 0, 0))],
        out_specs=[pl.BlockSpec((1, TQ, HP), lambda h, i: (0, cur(i), h)),
                   pl.BlockSpec((None, 256, WIN), lambda h, i: (h, 0, 0)),
                   pl.BlockSpec((2, TQ, HP), lambda h, i: (0, prv(i), h))],
        out_shape=[jax.ShapeDtypeStruct((1, S, 512), BF16), jax.ShapeDtypeStruct((4, 256, WIN), F32),
                   jax.ShapeDtypeStruct((2, S, 512), BF16)],
        scratch_shapes=[pltpu.VMEM((2 * TQ, HP), BF16), pltpu.VMEM((2 * TQ, HP), BF16),
                        pltpu.VMEM((2 * TQ, HP), F32), pltpu.VMEM((2 * TQ, HP), F32)],
        compiler_params=_cp("parallel", "arbitrary"),
    )(qkv, qkv, qkv, qkv, qkv, do, bias)
    return dq, db, dkv


def bias_grad(db, name):
    def body(db_ref, o_ref):
        r = lax.broadcasted_iota(jnp.int32, (128, 128), 0)
        c = lax.broadcasted_iota(jnp.int32, (128, 128), 1)
        flip = (r + c == 127).astype(BF16)
        lane = lax.broadcasted_iota(jnp.int32, (16, 384), 1)
        src = lax.broadcasted_iota(jnp.int32, (128, 384), 0)
        dst = lax.broadcasted_iota(jnp.int32, (128, 384), 1)

        def split_dot(v, m):
            hi = v.astype(BF16)
            r1 = v - hi.astype(F32)
            mid = r1.astype(BF16)
            lo = (r1 - mid.astype(F32)).astype(BF16)
            return _dot(hi, m) + _dot(mid, m) + _dot(lo, m)

        def diag_sums(w):
            y = pltpu.roll(split_dot(w, flip), 0, 1, stride=1, stride_axis=0)
            return jnp.broadcast_to(_colsum(y), (16, 128))

        w4 = db_ref[0, :, 512:640]
        w3 = db_ref[0, :, 384:512]
        far = jnp.sum(db_ref[0, :, 0:384]) + jnp.sum(jnp.where(r >= c, w3, 0.0))
        lo4 = diag_sums(jnp.where(r >= c, w4, 0.0))
        up4 = diag_sums(jnp.where(r < c, w4, 0.0))
        up3 = diag_sums(jnp.where(r < c, w3, 0.0))
        p_lo4 = (dst == 128 + (src + 1) % 128).astype(BF16)
        p_up4 = ((dst == src + 1) & (src < 127)).astype(BF16)
        p_up3 = ((dst == src + 129) & (src < 127)).astype(BF16)
        out = split_dot(lo4, p_lo4) + split_dot(up4, p_up4) + split_dot(up3, p_up3)
        o_ref[0] = out + jnp.where(lane == 256, far, 0.0)

    return pl.pallas_call(
        body, name=name, grid=(8,),
        in_specs=[pl.BlockSpec((1, 128, WIN), lambda h: (h, 0, 0))],
        out_specs=pl.BlockSpec((1, 16, 384), lambda h: (h, 0, 0)),
        out_shape=jax.ShapeDtypeStruct((8, 16, 384), F32),
        compiler_params=_cp("parallel"),
    )(db)[:, 0, :]


LT = 1024
LC = 512
SG = 4


def _lru_gates(xs, pv_ref, wa_ref, wx_ref, tl):
    xc = (pv_ref[4:5, :] + pv_ref[3:4, :] * xs[pl.ds(8, tl), :] + pv_ref[2:3, :] * xs[pl.ds(7, tl), :]
          + pv_ref[1:2, :] * xs[pl.ds(6, tl), :] + pv_ref[0:1, :] * xs[pl.ds(5, tl), :])
    xcb = xc.astype(BF16)
    pa = jnp.concatenate([_dot(xcb[:, 0:256], wa_ref[0]), _dot(xcb[:, 256:512], wa_ref[1])], axis=1)
    px = jnp.concatenate([_dot(xcb[:, 0:256], wx_ref[0]), _dot(xcb[:, 256:512], wx_ref[1])], axis=1)
    r = _sigmoid(pa + pv_ref[5:6, :])
    ig = _sigmoid(px + pv_ref[6:7, :])
    z = -pv_ref[7:8, :]
    sp = jnp.maximum(z, 0.0) + jnp.log1p(jnp.exp(-jnp.abs(z)))
    log_a = (-LRU_C * r) * sp
    a = jnp.exp(log_a)
    s = jnp.tanh(-log_a) * (1.0 + a * a)
    inv_mult = lax.rsqrt(s)
    mult = jnp.where(s > 0.0, s * inv_mult, 0.0)
    return xc, xcb, r, ig, sp, a, mult, inv_mult


def lru_fwd(rest, pvec, wa, wx, name):
    S = rest.shape[0]
    tl = min(LT, S)
    nt = S // tl

    def body(xr_ref, halo_ref, yr_ref, pv_ref, wa_ref, wx_ref, h_ref, hg_ref, xs, a_s, u_s, h_s, carry):
        ti = pl.program_id(1)

        @pl.when(ti == 0)
        def _():
            carry[...] = jnp.zeros_like(carry)

        xs[0:8, :] = jnp.where(ti > 0, halo_ref[8:16, :].astype(F32), 0.0)
        xs[pl.ds(8, tl), :] = xr_ref[...].astype(F32)
        xc, _, _, ig, _, a, mult, _ = _lru_gates(xs, pv_ref, wa_ref, wx_ref, tl)
        a_s[...] = a
        u_s[...] = mult * (ig * xc)
        row = lax.broadcasted_iota(jnp.int32, (8, LC), 0)

        def local(o):
            av = a_s[pl.ds(o, 8), :]
            bv = u_s[pl.ds(o, 8), :]
            for d in (1, 2, 4):
                a_sh = pltpu.roll(av, d, 0)
                b_sh = pltpu.roll(bv, d, 0)
                m = row >= d
                bv = jnp.where(m, av * b_sh + bv, bv)
                av = jnp.where(m, av * a_sh, av)
            return av, bv

        def blk(gi, c):
            os_ = [pl.multiple_of(gi * (8 * SG) + 8 * q, 8) for q in range(SG)]
            loc = [local(o) for o in os_]
            for o, (av, bv) in zip(os_, loc):
                hv = bv + av * c
                h_s[pl.ds(o, 8), :] = hv
                c = hv[7:8, :]
            return c

        carry[...] = lax.fori_loop(0, tl // (8 * SG), blk, carry[...])
        h = h_s[...]
        h_ref[...] = h
        hg_ref[...] = (h * _gelu(yr_ref[...].astype(F32))).astype(BF16)

    hb = tl // 16
    return pl.pallas_call(
        body, name=name, grid=(2, nt),
        in_specs=[pl.BlockSpec((tl, LC), lambda c, t: (t, c)),
                  pl.BlockSpec((16, LC), lambda c, t: (jnp.maximum(t * hb - 1, 0), c)),
                  pl.BlockSpec((tl, LC), lambda c, t: (t, 2 + c)),
                  pl.BlockSpec((8, LC), lambda c, t: (0, c)),
                  pl.BlockSpec((2, 256, 256), lambda c, t: (c, 0, 0)),
                  pl.BlockSpec((2, 256, 256), lambda c, t: (c, 0, 0))],
        out_specs=[pl.BlockSpec((tl, LC), lambda c, t: (t, c)), pl.BlockSpec((tl, LC), lambda c, t: (t, c))],
        out_shape=[jax.ShapeDtypeStruct((S, D), F32), jax.ShapeDtypeStruct((S, D), BF16)],
        scratch_shapes=[pltpu.VMEM((tl + 8, LC), F32), pltpu.VMEM((tl, LC), F32), pltpu.VMEM((tl, LC), F32),
                        pltpu.VMEM((tl, LC), F32), pltpu.VMEM((1, LC), F32)],
        compiler_params=_cp("parallel", "arbitrary"),
    )(rest, rest, rest, pvec, wa, wx)


def lru_bwd(dh, h, rest, pvec, wa, wx, name):
    S = rest.shape[0]
    tl = min(LT, S)
    nt = S // tl

    def body(dh_ref, h_ref, hhalo_ref, xr_ref, xhalo_ref, pv_ref, wa_ref, wx_ref,
             dxr_ref, vacc_ref, dwa_ref, dwx_ref,
             xs, hs, a_s, ash_s, b_s, lam_s, dxe, anext, lnext, dxnext):
        ti = pl.program_id(1)
        tr = nt - 1 - ti

        @pl.when(ti == 0)
        def _():
            anext[...] = jnp.zeros_like(anext)
            lnext[...] = jnp.zeros_like(lnext)
            dxnext[...] = jnp.zeros_like(dxnext)
            vacc_ref[...] = jnp.zeros_like(vacc_ref)
            dwa_ref[...] = jnp.zeros_like(dwa_ref)
            dwx_ref[...] = jnp.zeros_like(dwx_ref)

        xs[0:8, :] = jnp.where(tr > 0, xhalo_ref[8:16, :].astype(F32), 0.0)
        xs[pl.ds(8, tl), :] = xr_ref[...].astype(F32)
        xc, xcb, r, ig, sp, a, mult, inv_mult = _lru_gates(xs, pv_ref, wa_ref, wx_ref, tl)

        a_s[pl.ds(0, tl), :] = a
        a_s[pl.ds(tl, 8), :] = jnp.broadcast_to(anext[...], (8, LC))
        ash_s[...] = a_s[pl.ds(1, tl), :]
        b_s[...] = dh_ref[...]
        row = lax.broadcasted_iota(jnp.int32, (8, LC), 0)

        def local(o):
            av = ash_s[pl.ds(o, 8), :]
            bv = b_s[pl.ds(o, 8), :]
            for d in (1, 2, 4):
                a_sh = pltpu.roll(av, 8 - d, 0)
                b_sh = pltpu.roll(bv, 8 - d, 0)
                m = row < 8 - d
                bv = jnp.where(m, bv + av * b_sh, bv)
                av = jnp.where(m, av * a_sh, av)
            return av, bv

        def blk(k, c):
            os_ = [pl.multiple_of((tl // 8 - 1 - (k * SG + q)) * 8, 8) for q in range(SG)]
            loc = [local(o) for o in os_]
            for o, (av, bv) in zip(os_, loc):
                lv = bv + av * c
                lam_s[pl.ds(o, 8), :] = lv
                c = lv[0:1, :]
            return c

        lnext[...] = lax.fori_loop(0, tl // (8 * SG), blk, lnext[...])
        anext[...] = a[0:1, :]
        lam = lam_s[...]

        hs[0:8, :] = jnp.where(tr > 0, hhalo_ref[...], 0.0)
        hs[pl.ds(8, tl), :] = h_ref[...]
        d_a = lam * hs[pl.ds(7, tl), :]
        d_mult = lam * (ig * xc)
        d_ig = lam * mult * xc
        dxc = lam * mult * ig
        d_log_a = d_a * a - d_mult * (a * a) * inv_mult
        d_r = d_log_a * (-LRU_C * sp)
        vacc_ref[7:8, :] += _colsum(d_log_a * (-LRU_C * r)) * (-_sigmoid(-pv_ref[7:8, :]))
        d_pa = d_r * r * (1.0 - r)
        d_px = d_ig * ig * (1.0 - ig)
        vacc_ref[5:6, :] += _colsum(d_pa)
        vacc_ref[6:7, :] += _colsum(d_px)
        dpa = d_pa.astype(BF16)
        dpx = d_px.astype(BF16)
        back = []
        for g in range(2):
            sl = slice(256 * g, 256 * g + 256)
            dwa_ref[g] += _dot_tn(xcb[:, sl], dpa[:, sl])
            dwx_ref[g] += _dot_tn(xcb[:, sl], dpx[:, sl])
            back.append(_dot_nt(dpa[:, sl], wa_ref[g]) + _dot_nt(dpx[:, sl], wx_ref[g]))
        dxc = dxc + jnp.concatenate(back, axis=1)
        vacc_ref[4:5, :] += _colsum(dxc)
        for k in range(4):
            vacc_ref[k:k + 1, :] += _colsum(dxc * xs[pl.ds(5 + k, tl), :])
        dxe[pl.ds(0, tl), :] = dxc
        dxe[pl.ds(tl, 8), :] = dxnext[...]
        dxr = (pv_ref[3:4, :] * dxc + pv_ref[2:3, :] * dxe[pl.ds(1, tl), :]
               + pv_ref[1:2, :] * dxe[pl.ds(2, tl), :] + pv_ref[0:1, :] * dxe[pl.ds(3, tl), :])
        dxr_ref[...] = dxr.astype(BF16)
        dxnext[...] = dxc[0:8, :]

    hb = tl // 8
    rev = lambda t: nt - 1 - t
    halo = lambda t: jnp.maximum(rev(t) * hb - 1, 0)
    big = lambda: pltpu.VMEM((tl + 8, LC), F32)
    til = lambda: pltpu.VMEM((tl, LC), F32)
    return pl.pallas_call(
        body, name=name, grid=(2, nt),
        in_specs=[pl.BlockSpec((tl, LC), lambda c, t: (rev(t), c)),
                  pl.BlockSpec((tl, LC), lambda c, t: (rev(t), c)),
                  pl.BlockSpec((8, LC), lambda c, t: (halo(t), c)),
                  pl.BlockSpec((tl, LC), lambda c, t: (rev(t), c)),
                  pl.BlockSpec((16, LC), lambda c, t: (jnp.maximum(rev(t) * (tl // 16) - 1, 0), c)),
                  pl.BlockSpec((8, LC), lambda c, t: (0, c)),
                  pl.BlockSpec((2, 256, 256), lambda c, t: (c, 0, 0)),
                  pl.BlockSpec((2, 256, 256), lambda c, t: (c, 0, 0))],
        out_specs=[pl.BlockSpec((tl, LC), lambda c, t: (rev(t), c)),
                   pl.BlockSpec((8, LC), lambda c, t: (0, c)),
                   pl.BlockSpec((2, 256, 256), lambda c, t: (c, 0, 0)),
                   pl.BlockSpec((2, 256, 256), lambda c, t: (c, 0, 0))],
        out_shape=[jax.ShapeDtypeStruct((S, D), BF16), jax.ShapeDtypeStruct((8, D), F32),
                   jax.ShapeDtypeStruct((4, 256, 256), F32), jax.ShapeDtypeStruct((4, 256, 256), F32)],
        scratch_shapes=[big(), big(), big(), til(), til(), til(), big(),
                        pltpu.VMEM((1, LC), F32), pltpu.VMEM((1, LC), F32), pltpu.VMEM((8, LC), F32)],
        compiler_params=_cp("parallel", "arbitrary"),
    )(dh, h, h, rest, rest, pvec, wa, wx)


def mix_out_fwd(x, ao, hg, rest, vec, w_att_o, w_rec_o, w_out, name, tm=512):
    S = x.shape[0]
    tm = min(tm, S)

    def body(x_ref, ao_ref, hg_ref, ga_ref, gr_ref, vec_ref, wa_ref, wr_ref, wo_ref,
             xo_ref, att_ref, rec_ref, mg_ref, f_ref):
        att = _dot(ao_ref[...], wa_ref[...])
        rec = _dot(hg_ref[...], wr_ref[...])
        att_ref[...] = att.astype(BF16)
        rec_ref[...] = rec.astype(BF16)
        mg = (_sigmoid(ga_ref[...].astype(F32)) * att + _sigmoid(gr_ref[...].astype(F32)) * rec).astype(BF16)
        mg_ref[...] = mg
        f = _dot(mg, wo_ref[...])
        f_ref[...] = f.astype(BF16)
        y = f * lax.rsqrt(_mean(f * f) + EPS) * vec_ref[1:2, :]
        xo_ref[...] = x_ref[...] + (1.0 * vec_ref[4:5, :]) * y

    row = lambda i: (i, 0)
    full = lambda r: pl.BlockSpec((r, D), lambda i: (0, 0))
    return pl.pallas_call(
        body, name=name, grid=(S // tm,),
        in_specs=[pl.BlockSpec((tm, D), row), pl.BlockSpec((tm, 512), row), pl.BlockSpec((tm, D), row),
                  pl.BlockSpec((tm, D), lambda i: (i, 2)), pl.BlockSpec((tm, D), lambda i: (i, 3)),
                  full(8), full(512), full(D), full(D)],
        out_specs=[pl.BlockSpec((tm, D), row)] * 5,
        out_shape=[jax.ShapeDtypeStruct((S, D), F32)] + [jax.ShapeDtypeStruct((S, D), BF16)] * 4,
        compiler_params=_cp("parallel"),
    )(x, ao, hg, rest, rest, vec, w_att_o, w_rec_o, w_out)


def mix_out_bwd(dxo, f, att, rec, rest, h, vec, w_att_o, w_rec_o, w_out, name, tm=512):
    S = dxo.shape[0]
    tm = min(tm, S)

    def body(dxo_ref, f_ref, att_ref, rec_ref, yr_ref, ga_ref, gr_ref, h_ref, vec_ref, wa_ref, wr_ref, wo_ref,
             df_ref, da_ref, dr_ref, dao_ref, dh_ref, d3_ref, vacc_ref):
        @pl.when(pl.program_id(0) == 0)
        def _():
            vacc_ref[...] = jnp.zeros_like(vacc_ref)

        df = _post_norm_bwd(dxo_ref[...], f_ref[...].astype(F32), 1.0, vec_ref, vacc_ref).astype(BF16)
        df_ref[...] = df
        dm = _dot_nt(df, wo_ref[...])
        sa = _sigmoid(ga_ref[...].astype(F32))
        sr = _sigmoid(gr_ref[...].astype(F32))
        d_att = (dm * sa).astype(BF16)
        d_rec = (dm * sr).astype(BF16)
        da_ref[...] = d_att
        dr_ref[...] = d_rec
        d3_ref[1] = (dm * att_ref[...].astype(F32) * (sa * (1.0 - sa))).astype(BF16)
        d3_ref[2] = (dm * rec_ref[...].astype(F32) * (sr * (1.0 - sr))).astype(BF16)
        dao_ref[...] = _dot_nt(d_att, wa_ref[...]).astype(BF16)
        d_hg = _dot_nt(d_rec, wr_ref[...])
        yr = yr_ref[...].astype(F32)
        t = jnp.tanh(_GK * (yr + 0.044715 * yr * yr * yr))
        dh_ref[...] = d_hg * (0.5 * yr * (1.0 + t))
        gelu_grad = 0.5 * (1.0 + t) + 0.5 * yr * (1.0 - t * t) * _GK * (1.0 + 3.0 * 0.044715 * yr * yr)
        d3_ref[0] = (d_hg * h_ref[...] * gelu_grad).astype(BF16)

    row = lambda i: (i, 0)
    full = lambda r: pl.BlockSpec((r, D), lambda i: (0, 0))
    return pl.pallas_call(
        body, name=name, grid=(S // tm,),
        in_specs=[pl.BlockSpec((tm, D), row)] * 4
        + [pl.BlockSpec((tm, D), lambda i: (i, 1)), pl.BlockSpec((tm, D), lambda i: (i, 2)),
           pl.BlockSpec((tm, D), lambda i: (i, 3)), pl.BlockSpec((tm, D), row),
           full(8), full(512), full(D), full(D)],
        out_specs=[pl.BlockSpec((tm, D), row)] * 3
        + [pl.BlockSpec((tm, 512), row), pl.BlockSpec((tm, D), row),
           pl.BlockSpec((3, tm, D), lambda i: (0, i, 0)), pl.BlockSpec((8, D), lambda i: (0, 0))],
        out_shape=[jax.ShapeDtypeStruct((S, D), BF16)] * 3
        + [jax.ShapeDtypeStruct((S, 512), BF16), jax.ShapeDtypeStruct((S, D), F32),
           jax.ShapeDtypeStruct((3, S, D), BF16), jax.ShapeDtypeStruct((8, D), F32)],
        compiler_params=_cp("arbitrary"),
    )(dxo, f, att, rec, rest, rest, rest, h, vec, w_att_o, w_rec_o, w_out)


def dw_in(h, dq, dkv, dxr, d3, name, tk=1024, tn=512):
    S = h.shape[0]
    tk = min(tk, S)
    nk = S // tk

    def body(h_ref, dq_ref, dkv_ref, dxr_ref, d3_ref, o_ref, acc):
        j, k = pl.program_id(0), pl.program_id(1)

        @pl.when(k == 0)
        def _():
            acc[...] = jnp.zeros_like(acc)

        @pl.when(j == 0)
        def _():
            acc[...] += _dot_tn(h_ref[pl.ds(pl.multiple_of(k * tk, 16), tk), :], dq_ref[...])

        @pl.when((j >= 1) & (j < 3))
        def _():
            acc[...] += _dot_tn(h_ref[pl.ds(pl.multiple_of(k * tk, 16), tk), :], dkv_ref[...])

        @pl.when((j >= 3) & (j < 5))
        def _():
            acc[...] += _dot_tn(h_ref[pl.ds(pl.multiple_of(k * tk, 16), tk), :], dxr_ref[...])

        @pl.when(j >= 5)
        def _():
            acc[...] += _dot_tn(h_ref[pl.ds(pl.multiple_of(k * tk, 16), tk), :], d3_ref[...])

        @pl.when(k == nk - 1)
        def _():
            o_ref[...] = acc[...].astype(BF16)

    use = lambda j, k, lo, hi: jnp.where((j >= lo) & (j < hi), k, 0)
    g3 = lambda j: jnp.clip(j - 5, 0, 5)
    return pl.pallas_call(
        body, name=name, grid=(PW // tn, nk),
        in_specs=[pl.BlockSpec((S, D), lambda j, k: (0, 0), pipeline_mode=pl.Buffered(1)),
                  pl.BlockSpec((None, tk, tn), lambda j, k: (0, use(j, k, 0, 1), 0)),
                  pl.BlockSpec((None, tk, tn), lambda j, k: (jnp.clip(j - 1, 0, 1), use(j, k, 1, 3), 0)),
                  pl.BlockSpec((tk, tn), lambda j, k: (use(j, k, 3, 5), jnp.clip(j - 3, 0, 1))),
                  pl.BlockSpec((None, tk, tn), lambda j, k: (g3(j) // 2, use(j, k, 5, 11), g3(j) % 2))],
        out_specs=pl.BlockSpec((D, tn), lambda j, k: (0, j)),
        out_shape=jax.ShapeDtypeStruct((D, PW), BF16),
        scratch_shapes=[pltpu.VMEM((D, tn), F32)],
        compiler_params=_cp("parallel", "arbitrary"),
    )(h, dq, dkv, dxr, d3)


def ada_fwd(c_all, w_ada, b_ada, name, tn=768):
    n = w_ada.shape[1]

    def body(c_ref, w_ref, b_ref, o_ref):
        cv = c_ref[...]
        ca = (cv * _sigmoid(cv)).astype(BF16)
        o_ref[...] = _dot(ca, w_ref[...].astype(BF16)) + b_ref[...]

    return pl.pallas_call(
        body, name=name, grid=(n // tn,),
        in_specs=[pl.BlockSpec((8, D), lambda j: (0, 0)), pl.BlockSpec((D, tn), lambda j: (0, j)),
                  pl.BlockSpec((1, tn), lambda j: (0, j))],
        out_specs=pl.BlockSpec((8, tn), lambda j: (0, j)),
        out_shape=jax.ShapeDtypeStruct((8, n), F32),
        compiler_params=_cp("parallel"),
    )(c_all, w_ada, b_ada)


def ada_bwd(c_all_t, dmod, name, tn=768):
    n = dmod.shape[1]

    def body(c_ref, d_ref, o_ref):
        cv = c_ref[...]
        ca = (cv * _sigmoid(cv)).astype(BF16)
        o_ref[...] = _dot(ca, d_ref[...].astype(BF16))

    return pl.pallas_call(
        body, name=name, grid=(n // tn,),
        in_specs=[pl.BlockSpec((D, 128), lambda j: (0, 0)), pl.BlockSpec((128, tn), lambda j: (0, j))],
        out_specs=pl.BlockSpec((D, tn), lambda j: (0, j)),
        out_shape=jax.ShapeDtypeStruct((D, n), F32),
        compiler_params=_cp("parallel"),
    )(c_all_t, dmod)


def _row_tile(rows, cols, itemsize=4, budget=1536 * 1024):
    best = None
    for t in range(8, rows + 1, 8):
        if rows % t == 0 and t * cols * itemsize <= budget:
            best = t
    return rows if best is None else best


def sum_lead(parts, name, out_dtype=F32):
    n, R, C = parts.shape
    tr = _row_tile(R, C * n)

    def body(p_ref, o_ref):
        acc = p_ref[0].astype(F32)
        for k in range(1, n):
            acc = acc + p_ref[k].astype(F32)
        o_ref[...] = acc.astype(out_dtype)

    return pl.pallas_call(
        body, name=name, grid=(R // tr,),
        in_specs=[pl.BlockSpec((n, tr, C), lambda i: (0, i, 0))],
        out_specs=pl.BlockSpec((tr, C), lambda i: (i, 0)),
        out_shape=jax.ShapeDtypeStruct((R, C), out_dtype),
        compiler_params=_cp("parallel"),
    )(parts)


def adamw(w, g, m, v, name, emit_g=False):
    R, C = w.shape
    tr = _row_tile(R, C * 8, budget=16 * 1024 * 1024)

    def body(w_ref, g_ref, m_ref, v_ref, d_ref, mo_ref, vo_ref, *go_ref):
        gv = g_ref[...]
        if emit_g:
            go_ref[0][...] = gv
        mn = ADAM_B1 * m_ref[...] + (1.0 - ADAM_B1) * gv
        vn = ADAM_B2 * v_ref[...] + (1.0 - ADAM_B2) * (gv * gv)
        m_hat = mn / (1.0 - ADAM_B1 ** ADAM_STEP)
        v_hat = vn / (1.0 - ADAM_B2 ** ADAM_STEP)
        d_ref[...] = -ADAM_LR * (m_hat / (jnp.sqrt(v_hat) + ADAM_EPS) + ADAM_WD * w_ref[...])
        mo_ref[...] = mn
        vo_ref[...] = vn

    spec = pl.BlockSpec((tr, C), lambda i: (i, 0))
    return pl.pallas_call(
        body, name=name, grid=(R // tr,),
        in_specs=[spec] * 4, out_specs=[spec] * (4 if emit_g else 3),
        out_shape=[jax.ShapeDtypeStruct((R, C), F32)] * (4 if emit_g else 3),
        compiler_params=_cp("parallel"),
    )(w, g, m, v)


def _mesh_pos():
    return lax.axis_index("x"), lax.axis_index("y"), lax.axis_index("c")


def _other_chips(mx, my):
    return [(1 - mx, my), (mx, 1 - my), (1 - mx, 1 - my)]


def ag_small(x, name):
    R = x.shape[0]

    def body(x_ref, out_ref, send_sems, recv_sems, local_sem):
        mx, my, mc = _mesh_pos()
        me, sibling = (mx, my, mc), (mx, my, 1 - mc)
        chips = _other_chips(mx, my)

        def slot(px, py, pc):
            return out_ref.at[4 * px + 2 * py + pc]

        def copy(k, block, to, src=None):
            return pltpu.make_async_remote_copy(
                src_ref=slot(*block) if src is None else src, dst_ref=slot(*block),
                send_sem=send_sems.at[k], recv_sem=recv_sems.at[k], device_id=to, device_id_type=MESH)

        mine = pltpu.make_async_copy(x_ref, slot(*me), local_sem)
        mine.start()
        first = [copy(0, me, sibling, src=x_ref)]
        first += [copy(1 + j, me, (*chip, mc), src=x_ref) for j, chip in enumerate(chips)]
        for cp in first:
            cp.start()
        passed = [copy(4 + j, (*chip, mc), sibling) for j, chip in enumerate(chips)]
        for j, chip in enumerate(chips):
            copy(1 + j, (*chip, mc), me).wait_recv()
            passed[j].start()
        copy(0, sibling, me).wait_recv()
        for j, chip in enumerate(chips):
            copy(4 + j, (*chip, 1 - mc), me).wait_recv()
        for cp in first + passed:
            cp.wait_send()
        mine.wait()

    return pl.pallas_call(
        body, name=name,
        out_shape=jax.ShapeDtypeStruct((N_DEV, R, 128), F32),
        in_specs=[pl.BlockSpec(memory_space=pltpu.VMEM)],
        out_specs=pl.BlockSpec(memory_space=pltpu.VMEM),
        scratch_shapes=[pltpu.SemaphoreType.DMA((7,)), pltpu.SemaphoreType.DMA((7,)), pltpu.SemaphoreType.DMA],
        compiler_params=pltpu.CompilerParams(vmem_limit_bytes=VMEM_LIMIT),
    )(x)


BIG = (("ffn1_w_gu", "col", D, PW), ("ffn1_w_down", "row", FF, D), ("w_in", "col", D, PW),
       ("w_att_o", "col", 512, D), ("w_rec_o", "row", D, D), ("w_out", "row", D, D),
       ("ffn2_w_gu", "col", D, PW), ("ffn2_w_down", "row", FF, D))
NBIG = len(BIG)


def _shard_shape(kind, R, C):
    return (R, C // 4) if kind == "col" else (R // 4, C)


def _region(ref, kind, R, C, q, half, t, tr):
    sr, sc = _shard_shape(kind, R, C)
    if kind == "col":
        return ref.at[pl.ds(pl.multiple_of(half * (R // 2) + t * tr, 16), tr), pl.ds(q * sc, sc)]
    return ref.at[pl.ds(pl.multiple_of(q * sr + t * tr, 16), tr), pl.ds(half * (C // 2), C // 2)]


def ag_local(w, kind, R, C, p_arr, name, after=()):
    sr, sc = _shard_shape(kind, R, C)
    tr = _row_tile(sr, sc, budget=2 * 1024 * 1024)
    nt = sr // tr
    after = list(after)

    def body(p_ref, w_ref, *rest):
        rest[-1][...] = w_ref[...].astype(BF16)

    if kind == "col":
        o_spec = pl.BlockSpec((tr, sc), lambda i, p: (i, p[0]))
    else:
        o_spec = pl.BlockSpec((tr, sc), lambda i, p: (p[0] * nt + i, 0))
    return pl.pallas_call(
        body, name=name,
        grid_spec=pltpu.PrefetchScalarGridSpec(
            num_scalar_prefetch=1, grid=(nt,),
            in_specs=[pl.BlockSpec((tr, sc), lambda i, p: (i, 0))] + [ANY] * len(after), out_specs=o_spec),
        out_shape=jax.ShapeDtypeStruct((R, C), BF16),
        compiler_params=_cp("parallel"),
    )(p_arr, w, *after)


HBM_SPEC = pl.BlockSpec(memory_space=pltpu.HBM)
SEM_SPEC = pl.BlockSpec(memory_space=pltpu.SEMAPHORE)


def _ag_sems(geoms):
    return sum(6 if both else 3 for (_, _, _, both) in geoms)


def _ag_copies(fulls, geoms, ssem, rsem, mx, my, mc, q, h):
    chips = _other_chips(mx, my)
    out, base = [], 0
    for w, (kind, R, C, both) in enumerate(geoms):
        sr, sc = _shard_shape(kind, R, C)
        hr = sr // 2 if kind == "col" else sr
        reg = _region(fulls[w], kind, R, C, q, h, 0, hr)
        out.append([pltpu.make_async_remote_copy(
            src_ref=reg, dst_ref=reg, send_sem=ssem.at[base + 3 * t + k], recv_sem=rsem.at[base + 3 * t + k],
            device_id=(*chips[k], mc if t == 0 else 1 - mc), device_id_type=MESH)
            for t in range(2 if both else 1) for k in range(3)])
        base += 6 if both else 3
    return out


def ag_start(fulls, geoms, after, name):
    n = len(fulls)
    after = list(after)
    m = len(after)

    def body(*refs):
        ssem, rsem = refs[n + m:n + m + 2]
        outs, token = refs[n + m + 2:2 * n + m + 2], refs[2 * n + m + 2]
        mx, my, mc = _mesh_pos()
        p = 2 * mx + my
        col = [w for w, g in enumerate(geoms) if g[0] == "col"]
        row = [w for w, g in enumerate(geoms) if g[0] == "row"]
        for q in range(4):
            @pl.when(p == q)
            def _(q=q):
                cps = _ag_copies(outs, geoms, ssem, rsem, mx, my, mc, q, mc)
                for w in col:
                    for cp in cps[w]:
                        cp.start()
        for h in range(2):
            @pl.when(mc == h)
            def _(h=h):
                cps = _ag_copies(outs, geoms, ssem, rsem, mx, my, mc, p, h)
                for w in row:
                    for cp in cps[w]:
                        cp.start()
        token[...] = jnp.zeros_like(token)

    res = pl.pallas_call(
        body, name=name,
        out_shape=[pltpu.SemaphoreType.DMA((_ag_sems(geoms),)), pltpu.SemaphoreType.DMA((_ag_sems(geoms),))]
        + [pltpu.HBM(a.shape, a.dtype) for a in fulls] + [jax.ShapeDtypeStruct((8, 128), F32)],
        in_specs=[HBM_SPEC] * n + [ANY] * m,
        out_specs=[SEM_SPEC, SEM_SPEC] + [HBM_SPEC] * n + [pl.BlockSpec(memory_space=pltpu.VMEM)],
        input_output_aliases={w: 2 + w for w in range(n)},
        compiler_params=pltpu.CompilerParams(has_side_effects=pltpu.SideEffectType.DATAFLOW_SIDE_EFFECTING),
    )(*[pltpu.with_memory_space_constraint(a, pltpu.HBM) for a in fulls], *after)
    return res[0], res[1], list(res[2:2 + n]), res[2 + n]


def ag_wait(fulls, geoms, ssem, rsem, after, name):
    n = len(fulls)
    after = list(after) if isinstance(after, (list, tuple)) else [after]

    def body(*refs):
        ins, ssem_ref, rsem_ref = refs[:n], refs[n], refs[n + 1]
        mx, my, mc = _mesh_pos()
        for cps in _ag_copies(ins, geoms, ssem_ref, rsem_ref, mx, my, mc, 0, 0):
            for cp in cps:
                cp.wait_send()
                cp.wait_recv()

    return list(pl.pallas_call(
        body, name=name,
        out_shape=[pltpu.HBM(a.shape, a.dtype) for a in fulls],
        in_specs=[HBM_SPEC] * n + [SEM_SPEC, SEM_SPEC] + [ANY] * len(after),
        out_specs=[HBM_SPEC] * n,
        input_output_aliases={w: w for w in range(n)},
        compiler_params=pltpu.CompilerParams(has_side_effects=pltpu.SideEffectType.DATAFLOW_SIDE_EFFECTING),
    )(*fulls, ssem, rsem, *after))


def ag_forward(full, kind, R, C, name):
    sr, sc = _shard_shape(kind, R, C)
    hr, hc = (sr // 2, sc) if kind == "col" else (sr, sc // 2)
    tr = _row_tile(hr, hc, itemsize=2, budget=2 * 1024 * 1024)
    nt = hr // tr
    total = 3 * nt

    def body(src_ref, full_ref, stage, lsem, ssem, rsem):
        step = pl.program_id(0) * nt + pl.program_id(1)
        par = step % 2
        mx, my, mc = _mesh_pos()

        def load(s, q, h, t):
            return pltpu.make_async_copy(_region(src_ref, kind, R, C, q, h, t, tr), stage.at[s], lsem.at[s])

        def push(s, q, h, t):
            return pltpu.make_async_remote_copy(src_ref=stage.at[s], dst_ref=_region(full_ref, kind, R, C, q, h, t, tr),
                                                send_sem=ssem.at[s], recv_sem=rsem, device_id=(mx, my, 1 - mc),
                                                device_id_type=MESH)

        def for_tile(stp, fn):
            q_k = _partner_chip(stp // nt, 2 * mx + my)
            if kind == "col":
                for q in range(4):
                    @pl.when(q_k == q)
                    def _(q=q):
                        fn(q, mc, stp % nt)
            else:
                for h in range(2):
                    @pl.when(mc == h)
                    def _(h=h):
                        fn(q_k, h, stp % nt)

        @pl.when(step == 0)
        def _():
            for_tile(step, lambda q, h, t: load(0, q, h, t).start())

        load(par, 0, 0, 0).wait()
        for_tile(step, lambda q, h, t: push(par, q, h, t).start())

        @pl.when(step + 1 < total)
        def _():
            @pl.when(step >= 1)
            def _():
                push(1 - par, 0, 0, 0).wait_send()
            for_tile(step + 1, lambda q, h, t: load(1 - par, q, h, t).start())

        @pl.when(step == total - 1)
        def _():
            push(par, 0, 0, 0).wait_send()
            push(1 - par, 0, 0, 0).wait_send()
            three = full_ref.at[pl.ds(0, hr), pl.ds(0, 3 * hc)] if kind == "col" else full_ref.at[pl.ds(0, 3 * hr), pl.ds(0, hc)]
            pltpu.make_async_remote_copy(src_ref=three, dst_ref=three, send_sem=ssem.at[0], recv_sem=rsem,
                                         device_id=(mx, my, 1 - mc), device_id_type=MESH).wait_recv()

    return pl.pallas_call(
        body, name=name, grid=(3, nt),
        in_specs=[ANY], out_specs=ANY,
        out_shape=jax.ShapeDtypeStruct((R, C), BF16),
        scratch_shapes=[pltpu.VMEM((2, tr, hc), BF16), pltpu.SemaphoreType.DMA((2,)), pltpu.SemaphoreType.DMA((2,)),
                        pltpu.SemaphoreType.DMA],
        input_output_aliases={0: 0},
        compiler_params=_cp("arbitrary", "arbitrary"),
    )(full)


def _half_shape(kind, R, C):
    return (R // 2, C) if kind == "col" else (R, C // 2)


def _piece_shape(kind, R, C):
    return (R // 2, C // 4) if kind == "col" else (R // 4, C // 2)


def pair_push(g, kind, c_arr, name):
    R, C = g.shape
    hr, hc = _half_shape(kind, R, C)
    tr = _row_tile(hr, hc, itemsize=2, budget=2 * 1024 * 1024)
    nt = hr // tr

    def body(c_ref, g_ref, out_ref, stage, ssem, rsem):
        i = pl.program_id(0)
        slot = i % 2
        mx, my, mc = _mesh_pos()

        def push(s, t):
            return pltpu.make_async_remote_copy(
                src_ref=stage.at[s], dst_ref=out_ref.at[pl.ds(pl.multiple_of(t * tr, 16), tr)],
                send_sem=ssem.at[s], recv_sem=rsem, device_id=(mx, my, 1 - mc), device_id_type=MESH)

        @pl.when(i >= 2)
        def _():
            push(slot, 0).wait_send()

        stage[slot] = g_ref[...]
        push(slot, i).start()

        @pl.when(i == nt - 1)
        def _():
            push(slot, 0).wait_send()
            if nt >= 2:
                push(1 - slot, 0).wait_send()
            pltpu.make_async_remote_copy(src_ref=out_ref, dst_ref=out_ref, send_sem=ssem.at[0], recv_sem=rsem,
                                         device_id=(mx, my, 1 - mc), device_id_type=MESH).wait_recv()

    if kind == "col":
        g_spec = pl.BlockSpec((tr, hc), lambda i, c: ((1 - c[0]) * nt + i, 0))
    else:
        g_spec = pl.BlockSpec((tr, hc), lambda i, c: (i, 1 - c[0]))
    return pl.pallas_call(
        body, name=name,
        grid_spec=pltpu.PrefetchScalarGridSpec(
            num_scalar_prefetch=1, grid=(nt,), in_specs=[g_spec], out_specs=ANY,
            scratch_shapes=[pltpu.VMEM((2, tr, hc), BF16), pltpu.SemaphoreType.DMA((2,)), pltpu.SemaphoreType.DMA]),
        out_shape=jax.ShapeDtypeStruct((hr, hc), BF16),
        compiler_params=_cp("arbitrary"),
    )(c_arr, g)


def _partner_chip(k, p):
    return p ^ jnp.where(k == 0, 2, jnp.where(k == 1, 1, jnp.where(k == 2, 3, 0)))


def pair_add(g, got, kind, cp_arr, name):
    R, C = g.shape
    pr, pc = _piece_shape(kind, R, C)
    tr = _row_tile(pr, pc, itemsize=2, budget=2 * 1024 * 1024)
    nt = pr // tr

    def body(cp_ref, g_ref, got_ref, ps_ref, rb_ref):
        tile = (g_ref[...].astype(F32) + got_ref[...].astype(F32)).astype(BF16)
        ps_ref[...] = tile

        @pl.when(pl.program_id(1) == cp_ref[1])
        def _():
            rb_ref[...] = tile

    if kind == "col":
        g_spec = pl.BlockSpec((tr, pc), lambda i, q, cp: (cp[0] * nt + i, q))
        got_spec = pl.BlockSpec((tr, pc), lambda i, q, cp: (i, q))
    else:
        g_spec = pl.BlockSpec((tr, pc), lambda i, q, cp: (q * nt + i, cp[0]))
        got_spec = pl.BlockSpec((tr, pc), lambda i, q, cp: (q * nt + i, 0))
    return pl.pallas_call(
        body, name=name,
        grid_spec=pltpu.PrefetchScalarGridSpec(
            num_scalar_prefetch=1, grid=(nt, 4), in_specs=[g_spec, got_spec],
            out_specs=[pl.BlockSpec((None, tr, pc), lambda i, q, cp: (q, i, 0)),
                       pl.BlockSpec((None, tr, pc), lambda i, q, cp: (cp[1], i, 0))]),
        out_shape=[jax.ShapeDtypeStruct((4, pr, pc), BF16)] * 2,
        compiler_params=_cp("arbitrary", "arbitrary"),
    )(cp_arr, g, got)


def _rs_copies(ps, rb, ssem, rsem, mx, my, mc):
    p = 2 * mx + my
    out = []
    for w in range(len(ps)):
        for k, chip in enumerate(_other_chips(mx, my)):
            out.append(pltpu.make_async_remote_copy(
                src_ref=ps[w].at[2 * chip[0] + chip[1]], dst_ref=rb[w].at[p], send_sem=ssem.at[3 * w + k],
                recv_sem=rsem.at[3 * w + k], device_id=(*chip, mc), device_id_type=MESH))
    return out


def rs_start(ps, rb, after, name):
    n = len(ps)
    after = list(after)
    m = len(after)

    def body(*refs):
        ssem, rsem = refs[2 * n + m:2 * n + m + 2]
        ps_o = refs[2 * n + m + 2:3 * n + m + 2]
        rb_o = refs[3 * n + m + 2:4 * n + m + 2]
        token = refs[4 * n + m + 2]
        for cp in _rs_copies(ps_o, rb_o, ssem, rsem, *_mesh_pos()):
            cp.start()
        token[...] = jnp.zeros_like(token)

    both = list(ps) + list(rb)
    res = pl.pallas_call(
        body, name=name,
        out_shape=[pltpu.SemaphoreType.DMA((3 * n,)), pltpu.SemaphoreType.DMA((3 * n,))]
        + [pltpu.HBM(a.shape, a.dtype) for a in both] + [jax.ShapeDtypeStruct((8, 128), F32)],
        in_specs=[HBM_SPEC] * (2 * n) + [ANY] * m,
        out_specs=[SEM_SPEC, SEM_SPEC] + [HBM_SPEC] * (2 * n) + [pl.BlockSpec(memory_space=pltpu.VMEM)],
        input_output_aliases={w: 2 + w for w in range(2 * n)},
        compiler_params=pltpu.CompilerParams(has_side_effects=pltpu.SideEffectType.DATAFLOW_SIDE_EFFECTING),
    )(*[pltpu.with_memory_space_constraint(a, pltpu.HBM) for a in both], *after)
    return res[0], res[1], list(res[2:2 + n]), list(res[2 + n:2 + 2 * n]), res[2 + 2 * n]


def rs_wait(ps, rb, ssem, rsem, after, name):
    n = len(ps)
    after = list(after)
    m = len(after)

    def body(*refs):
        ps_i, rb_i = refs[:n], refs[n:2 * n]
        ssem_ref, rsem_ref = refs[2 * n], refs[2 * n + 1]
        for cp in _rs_copies(ps_i, rb_i, ssem_ref, rsem_ref, *_mesh_pos()):
            cp.wait_send()
            cp.wait_recv()

    both = list(ps) + list(rb)
    res = pl.pallas_call(
        body, name=name,
        out_shape=[pltpu.HBM(a.shape, a.dtype) for a in both],
        in_specs=[HBM_SPEC] * (2 * n) + [SEM_SPEC, SEM_SPEC] + [ANY] * m,
        out_specs=[HBM_SPEC] * (2 * n),
        input_output_aliases={w: w for w in range(2 * n)},
        compiler_params=pltpu.CompilerParams(has_side_effects=pltpu.SideEffectType.DATAFLOW_SIDE_EFFECTING),
    )(*both, ssem, rsem, *after)
    return list(res[n:])


def _slot_copies(blk, ssem, rsem):
    mx, my, mc = _mesh_pos()
    mine = blk.at[4 * mx + 2 * my + mc]
    peers = [(mx, my, 1 - mc)] + [(*chip, mc) for chip in _other_chips(mx, my)] \
        + [(*chip, 1 - mc) for chip in _other_chips(mx, my)]
    return [pltpu.make_async_remote_copy(src_ref=mine, dst_ref=mine, send_sem=ssem.at[k], recv_sem=rsem.at[k],
                                         device_id=peer, device_id_type=MESH) for k, peer in enumerate(peers)]


def slot_start(blk, name):
    def body(_, ssem, rsem, out, token):
        for cp in _slot_copies(out, ssem, rsem):
            cp.start()
        token[...] = jnp.zeros_like(token)

    return pl.pallas_call(
        body, name=name,
        out_shape=[pltpu.SemaphoreType.DMA((7,)), pltpu.SemaphoreType.DMA((7,)), pltpu.HBM(blk.shape, blk.dtype),
                   jax.ShapeDtypeStruct((8, 128), F32)],
        in_specs=[HBM_SPEC],
        out_specs=[SEM_SPEC, SEM_SPEC, HBM_SPEC, pl.BlockSpec(memory_space=pltpu.VMEM)],
        input_output_aliases={0: 2},
        compiler_params=pltpu.CompilerParams(has_side_effects=pltpu.SideEffectType.DATAFLOW_SIDE_EFFECTING),
    )(pltpu.with_memory_space_constraint(blk, pltpu.HBM))


def slot_wait(blk, ssem, rsem, after, name):
    after = list(after)

    def body(blk_ref, ssem_ref, rsem_ref, *_):
        for cp in _slot_copies(blk_ref, ssem_ref, rsem_ref):
            cp.wait_send()
            cp.wait_recv()

    return pl.pallas_call(
        body, name=name,
        out_shape=pltpu.HBM(blk.shape, blk.dtype),
        in_specs=[HBM_SPEC, SEM_SPEC, SEM_SPEC] + [ANY] * len(after),
        out_specs=HBM_SPEC,
        input_output_aliases={0: 0},
        compiler_params=pltpu.CompilerParams(has_side_effects=pltpu.SideEffectType.DATAFLOW_SIDE_EFFECTING),
    )(blk, ssem, rsem, *after)


def sum_share(parts, kind, R, C, name):
    _, pr, pc = parts.shape
    sr, sc = _shard_shape(kind, R, C)
    tr = _row_tile(pr, pc * 4, budget=8 * 1024 * 1024)
    nt = pr // tr

    def body(p_ref, fin_ref, stage, lsem, ssem, rsem):
        i = pl.program_id(0)
        slot = i % 2
        mx, my, mc = _mesh_pos()

        def region(h, t):
            r0 = pl.multiple_of(t * tr, 8)
            if kind == "col":
                return fin_ref.at[pl.ds(pl.multiple_of(h * pr + r0, 8), tr)]
            return fin_ref.at[pl.ds(r0, tr), pl.ds(h * pc, pc)]

        def copies(s, h, t):
            return (pltpu.make_async_copy(stage.at[s], region(h, t), lsem.at[s]),
                    pltpu.make_async_remote_copy(src_ref=stage.at[s], dst_ref=region(h, t), send_sem=ssem.at[s],
                                                 recv_sem=rsem, device_id=(mx, my, 1 - mc), device_id_type=MESH))

        def wait_sent(s):
            loc, rem = copies(s, 0, 0)
            loc.wait()
            rem.wait_send()

        @pl.when(i >= 2)
        def _():
            wait_sent(slot)

        acc = p_ref[0].astype(F32)
        for k in range(1, 4):
            acc = acc + p_ref[k].astype(F32)
        stage[slot] = acc
        if kind == "col":
            for cp in copies(slot, mc, i):
                cp.start()
        else:
            for h in range(2):
                @pl.when(mc == h)
                def _(h=h):
                    for cp in copies(slot, h, i):
                        cp.start()

        @pl.when(i == nt - 1)
        def _():
            wait_sent(slot)
            if nt >= 2:
                wait_sent(1 - slot)
            half = fin_ref.at[pl.ds(0, pr), pl.ds(0, pc)]
            pltpu.make_async_remote_copy(src_ref=half, dst_ref=half, send_sem=ssem.at[0], recv_sem=rsem,
                                         device_id=(mx, my, 1 - mc), device_id_type=MESH).wait_recv()

    return pl.pallas_call(
        body, name=name, grid=(nt,),
        in_specs=[pl.BlockSpec((4, tr, pc), lambda i: (0, i, 0))],
        out_specs=ANY,
        out_shape=jax.ShapeDtypeStruct((sr, sc), F32),
        scratch_shapes=[pltpu.VMEM((2, tr, pc), F32), pltpu.SemaphoreType.DMA((2,)), pltpu.SemaphoreType.DMA((2,)),
                        pltpu.SemaphoreType.DMA],
        compiler_params=_cp("arbitrary"),
    )(parts)


def _pack(parts, rows):
    flat = []
    for a in parts:
        a = jnp.ravel(a).astype(F32)
        flat.append(jnp.pad(a, (0, (-a.shape[0]) % 128)))
    v = jnp.concatenate(flat)
    return jnp.pad(v, (0, rows * 128 - v.shape[0])).reshape(rows, 128)


def _unpack(block, shapes):
    lead = block.shape[:-2]
    v = block.reshape(lead + (-1,))
    out, off = [], 0
    for shp in shapes:
        n = int(np.prod(shp))
        out.append(v[..., off:off + n].reshape(lead + tuple(shp)))
        off += n + (-n) % 128
    return out


def _block_diag4(w):
    w4 = w.reshape(4, 4, 64, 64)
    eye = jnp.eye(4, dtype=w.dtype)
    return (w4[:, :, :, None, :] * eye[None, :, None, :, None]).reshape(4, 256, 256)


def _diag_blocks(bd):
    b5 = bd.reshape(4, 4, 64, 4, 64)
    return jnp.stack([b5[:, i, :, i, :] for i in range(4)], axis=1).reshape(16, 64, 64)


def _bias_window(rel_bias):
    m = (np.arange(768) + 127) % 768 - 127
    w = rel_bias[:, np.clip(512 - m, -128, 128) + 128]
    win = jnp.tile(w, (1, 128))[:, :128 * 767].reshape(8, 128, 767)[:, :, :WIN]
    qh = np.arange(128)[:, None] // CHUNK
    kc = np.arange(WIN)[None, :] // CHUNK
    valid = (kc >= qh) & (kc <= qh + 8)
    return jnp.where(jnp.asarray(valid)[None], win, NEG)


SMALL = ("b_ada", "norm_pre", "norm_post", "rel_bias", "conv_w", "conv_b", "lru_wa", "lru_ba", "lru_wx",
         "lru_bx", "lru_lambda")
WEIGHTS = ("w_ada", "b_ada", "norm_pre", "norm_post", "ffn1_w_gu", "ffn1_w_down", "w_in", "rel_bias", "conv_w",
           "conv_b", "lru_wa", "lru_ba", "lru_wx", "lru_bx", "lru_lambda", "w_att_o", "w_rec_o", "w_out",
           "ffn2_w_gu", "ffn2_w_down")


def kernel(x, c, w_ada, b_ada, norm_pre, norm_post, ffn1_w_gu, ffn1_w_down, w_in, rel_bias, conv_w, conv_b, lru_wa, lru_ba, lru_wx, lru_bx, lru_lambda, w_att_o, w_rec_o, w_out, ffn2_w_gu, ffn2_w_down, loss_target, m_w_ada, m_b_ada, m_norm_pre, m_norm_post, m_ffn1_w_gu, m_ffn1_w_down, m_w_in, m_rel_bias, m_conv_w, m_conv_b, m_lru_wa, m_lru_ba, m_lru_wx, m_lru_bx, m_lru_lambda, m_w_att_o, m_w_rec_o, m_w_out, m_ffn2_w_gu, m_ffn2_w_down, v_w_ada, v_b_ada, v_norm_pre, v_norm_post, v_ffn1_w_gu, v_ffn1_w_down, v_w_in, v_rel_bias, v_conv_w, v_conv_b, v_lru_wa, v_lru_ba, v_lru_wx, v_lru_bx, v_lru_lambda, v_w_att_o, v_w_rec_o, v_w_out, v_ffn2_w_gu, v_ffn2_w_down):
    W = dict(w_ada=w_ada, b_ada=b_ada, norm_pre=norm_pre, norm_post=norm_post, ffn1_w_gu=ffn1_w_gu,
             ffn1_w_down=ffn1_w_down, w_in=w_in, rel_bias=rel_bias, conv_w=conv_w, conv_b=conv_b, lru_wa=lru_wa,
             lru_ba=lru_ba, lru_wx=lru_wx, lru_bx=lru_bx, lru_lambda=lru_lambda, w_att_o=w_att_o, w_rec_o=w_rec_o,
             w_out=w_out, ffn2_w_gu=ffn2_w_gu, ffn2_w_down=ffn2_w_down)
    M = dict(w_ada=m_w_ada, b_ada=m_b_ada, norm_pre=m_norm_pre, norm_post=m_norm_post, ffn1_w_gu=m_ffn1_w_gu,
             ffn1_w_down=m_ffn1_w_down, w_in=m_w_in, rel_bias=m_rel_bias, conv_w=m_conv_w, conv_b=m_conv_b,
             lru_wa=m_lru_wa, lru_ba=m_lru_ba, lru_wx=m_lru_wx, lru_bx=m_lru_bx, lru_lambda=m_lru_lambda,
             w_att_o=m_w_att_o, w_rec_o=m_w_rec_o, w_out=m_w_out, ffn2_w_gu=m_ffn2_w_gu, ffn2_w_down=m_ffn2_w_down)
    V = dict(w_ada=v_w_ada, b_ada=v_b_ada, norm_pre=v_norm_pre, norm_post=v_norm_post, ffn1_w_gu=v_ffn1_w_gu,
             ffn1_w_down=v_ffn1_w_down, w_in=v_w_in, rel_bias=v_rel_bias, conv_w=v_conv_w, conv_b=v_conv_b,
             lru_wa=v_lru_wa, lru_ba=v_lru_ba, lru_wx=v_lru_wx, lru_bx=v_lru_bx, lru_lambda=v_lru_lambda,
             w_att_o=v_w_att_o, w_rec_o=v_w_rec_o, w_out=v_w_out, ffn2_w_gu=v_ffn2_w_gu, ffn2_w_down=v_ffn2_w_down)
    mx, my, mc = _mesh_pos()
    p = 2 * mx + my
    e = 4 * mx + 2 * my + mc
    xs = x[0]

    c_arr = jnp.reshape(mc, (1,)).astype(jnp.int32)
    cp_arr = jnp.stack([mc, p]).astype(jnp.int32)
    p_arr = jnp.reshape(p, (1,)).astype(jnp.int32)
    direct = ("w_att_o", "w_rec_o", "w_out", "ffn2_w_gu", "ffn2_w_down")
    geoms = [(kind, R, C, n in direct) for (n, kind, R, C) in BIG]
    names = [b[0] for b in BIG]
    placed = [ag_local(W[n][0], kind, R, C, p_arr, "ag_local_" + n) for (n, kind, R, C) in BIG[:2]]

    def arrived(fly, lo, hi, ssem, rsem, after, tag):
        done = ag_wait(fly, geoms[lo:hi], ssem, rsem, after, "ag_wait_" + tag)
        return [a if both else ag_forward(a, kind, R, C, "ag_forward_" + n)
                for a, (kind, R, C, both), n in zip(done, geoms[lo:hi], names[lo:hi])]

    g1 = ag_small(_pack([c, norm_pre, norm_post, conv_w], 32), "ag_small_params")
    c_all, npre4, npost4, cw4 = _unpack(g1, [(D,), (3, 256), (3, 256), (4, 256)])
    chipwise = lambda a: jnp.moveaxis(a[0::2], 0, 1).reshape(a.shape[1], D)
    npre, npost, conv_full = chipwise(npre4), chipwise(npost4), chipwise(cw4)

    b_cols = lax.dynamic_slice(b_ada, (0, p * 2304), (1, 2304))
    mod_cols = ada_fwd(c_all, w_ada[0], b_cols, "ada_fwd")
    g2 = ag_small(mod_cols.reshape(144, 128), "ag_mod")
    mod_all = jnp.moveaxis(g2[0::2].reshape(4, 8, 2304), 0, 1).reshape(8, 9 * D)
    mod = lax.dynamic_index_in_dim(mod_all, e, 0, keepdims=False).reshape(3, 3, D)
    zeros3 = jnp.zeros((3, D), F32)
    vecs = [jnp.concatenate([npre[k:k + 1], npost[k:k + 1], mod[k], zeros3], axis=0) for k in range(3)]

    gu_s, gu_r, gu_fly, tok_gu = ag_start(placed[:1], geoms[:1], [g2], "ag_start_ffn1_gu")
    dn_s, dn_r, dn_fly, tok0 = ag_start(placed[1:2], geoms[1:2], [tok_gu], "ag_start_ffn1_down")
    placed += [ag_local(W[n][0], kind, R, C, p_arr, "ag_local_" + n, after=[tok0]) for (n, kind, R, C) in BIG[2:]]
    f1_gu, = arrived(gu_fly, 0, 1, gu_s, gu_r, placed[2:], "ffn1_gu")
    f1_dn, = arrived(dn_fly, 1, 2, dn_s, dn_r, f1_gu, "ffn1_down")
    mix_s, mix_r, mix_fly, tok1 = ag_start(placed[2:6], geoms[2:6], [f1_gu, f1_dn], "ag_start_mixer")
    ffn_s, ffn_r, ffn_fly, tok2 = ag_start(placed[6:], geoms[6:], [tok1], "ag_start_ffn2")
    wa_bd = _block_diag4(lru_wa[0]).astype(BF16)
    wx_bd = _block_diag4(lru_wx[0]).astype(BF16)
    pvec = jnp.concatenate([conv_full, conv_b, lru_ba, lru_bx, lru_lambda], axis=0)
    bias = _bias_window(rel_bias[0]).reshape(4, 256, WIN)

    x1, h1, g1_, u1, a1, f1 = ffn_fwd(xs, vecs[0] + tok2[0:1, 0:1], f1_gu, f1_dn, 0.5, "ffn1_fwd")
    win, wao, wro, wout = arrived(mix_fly, 2, 6, mix_s, mix_r, x1, "mixer")
    h2, qkv, rest = proj_fwd(x1, vecs[1], win, "proj_fwd")
    ao = attn_fwd(qkv, bias, "attn_fwd")
    hl, hg = lru_fwd(rest, pvec, wa_bd, wx_bd, "lru_fwd")
    x2, att, rec, mg, f2 = mix_out_fwd(x1, ao, hg, rest, vecs[1], wao, wro, wout, "mix_out_fwd")
    f2_gu, f2_dn = arrived(ffn_fly, 6, 8, ffn_s, ffn_r, x2, "ffn2")
    dy, h3, g3_, u3, a3, f3, lvec = ffn_fwd(x2, vecs[2], f2_gu, f2_dn, 0.5, "ffn2_fwd", tgt=loss_target[0])

    G, grads = {}, {}
    geo = {n: (kind, R, C) for (n, kind, R, C) in BIG}

    def reduce_begin(names, tag):
        ps, rb = [], []
        for n in names:
            got = pair_push(G[n], geo[n][0], c_arr, "rs_push_" + n)
            a, b = pair_add(G[n], got, geo[n][0], cp_arr, "rs_pair_sum_" + n)
            ps.append(a)
            rb.append(b)
        return rs_start(ps, rb, [], "rs_start_" + tag)

    def reduce_end(names, flight, after, tag):
        ssem, rsem, ps, rb, _ = flight
        for a, n in zip(rs_wait(ps, rb, ssem, rsem, after, "rs_wait_" + tag), names):
            grads[n] = sum_share(a, *geo[n], "rs_sum_share_" + n)[None]

    dx2, df3, dgu3, va2 = ffn_bwd(dy, x2, f3, g3_, u3, vecs[2], f2_gu, f2_dn, 0.5, "ffn2_bwd")
    G["ffn2_w_gu"] = mm_tn(h3, dgu3, "dw_ffn2_gu", D, 1408, 2048, a_resident=True)
    G["ffn2_w_down"] = mm_tn(a3, df3, "dw_ffn2_down", 1408, D, 2048)
    fly_ffn2 = reduce_begin(("ffn2_w_gu", "ffn2_w_down"), "ffn2")
    vec1 = vecs[1] + fly_ffn2[4][0:1, 0:1]
    df2, d_att, d_rec, dao, dhl, d3, va_out = mix_out_bwd(dx2, f2, att, rec, rest, hl, vec1, wao, wro, wout,
                                                          "mix_out_bwd")
    G["w_out"] = mm_tn(mg, df2, "dw_out", D, D, 1024)
    G["w_att_o"] = mm_tn(ao, d_att, "dw_att_o", 512, D, 1024)
    G["w_rec_o"] = mm_tn(hg, d_rec, "dw_rec_o", D, D, 1024)
    dq, db, dkv = attn_bwd(qkv, dao, bias, "attn_bwd")
    dxr, v_lru, dwa_bd, dwx_bd = lru_bwd(dhl, hl, rest, pvec, wa_bd, wx_bd, "lru_bwd")
    lru_w = jnp.concatenate([_diag_blocks(dwa_bd), _diag_blocks(dwx_bd)]).reshape(1, 1024, 128)
    lru_slots = lax.dynamic_update_slice(jnp.zeros((N_DEV, 1024, 128), F32), lru_w, (e, 0, 0))
    lw_s, lw_r, lw_fly, lw_tok = slot_start(lru_slots, "lru_w_start")
    dx1, va_in = proj_bwd(dq, dkv, dxr, d3, win, x1, dx2, vecs[1] + lw_tok[0:1, 0:1], "proj_bwd")
    G["w_in"] = dw_in(h2, dq, dkv, dxr, d3, "dw_in")
    fly_mix = reduce_begin(("w_in", "w_att_o", "w_rec_o", "w_out"), "mixer")
    vec0 = vecs[0] + fly_mix[4][0:1, 0:1]
    dx0, df1, dgu1, va0 = ffn_bwd(dx1, xs, f1, g1_, u1, vec0, f1_gu, f1_dn, 0.5, "ffn1_bwd")
    G["ffn1_w_gu"] = mm_tn(h1, dgu1, "dw_ffn1_gu", D, 1408, 2048, a_resident=True)
    G["ffn1_w_down"] = mm_tn(a1, df1, "dw_ffn1_down", 1408, D, 2048)
    fly_ffn1 = reduce_begin(("ffn1_w_gu", "ffn1_w_down"), "ffn1")
    reduce_end(("ffn2_w_gu", "ffn2_w_down"), fly_ffn2, [fly_ffn1[4]], "ffn2")
    reduce_end(("w_in", "w_att_o", "w_rec_o", "w_out"), fly_mix, [fly_ffn1[4], grads["ffn2_w_down"]], "mixer")

    va1 = va_out + va_in
    vas = (va0, va1, va2)
    dmod = jnp.stack([v[2:5] for v in vas])
    part = {"b_ada": dmod, "norm_pre": jnp.stack([v[0] for v in vas]), "norm_post": jnp.stack([v[1] for v in vas]),
            "rel_bias": bias_grad(db.reshape(8, 128, WIN), "bias_grad")[:, :257], "conv_w": v_lru[0:4], "conv_b": v_lru[4],
            "lru_ba": v_lru[5], "lru_bx": v_lru[6], "lru_lambda": v_lru[7]}
    full_shapes = {"b_ada": (9 * D,), "norm_pre": (3, D), "norm_post": (3, D), "rel_bias": (8, 257),
                   "conv_w": (4, D), "conv_b": (D,), "lru_wa": (16, 64, 64), "lru_ba": (D,),
                   "lru_wx": (16, 64, 64), "lru_bx": (D,), "lru_lambda": (D,)}
    gathered = [n for n in SMALL if n in part]
    g3 = ag_small(_pack([part[n] for n in gathered] + [lvec[0:1, 0:1]], 208), "ag_small_grads")
    summed = _unpack(sum_lead(g3, "sum_small_grads"), [full_shapes[n] for n in gathered] + [(1,)])
    red = dict(zip(gathered, summed[:-1]))
    loss = summed[-1][0]
    lru_all = slot_wait(lw_fly, lw_s, lw_r, [dx0], "lru_w_wait")
    red["lru_wa"], red["lru_wx"] = sum_lead(lru_all, "sum_lru_w").reshape(2, 16, 64, 64)
    cols = lambda a: lax.dynamic_slice(a, (0, p * 256), (a.shape[0], 256))
    grads.update({"b_ada": red["b_ada"][None], "norm_pre": cols(red["norm_pre"])[None],
                  "norm_post": cols(red["norm_post"])[None], "rel_bias": red["rel_bias"][None],
                  "conv_w": cols(red["conv_w"])[None], "conv_b": red["conv_b"][None], "lru_wa": red["lru_wa"][None],
                  "lru_ba": red["lru_ba"][None], "lru_wx": red["lru_wx"][None], "lru_bx": red["lru_bx"][None],
                  "lru_lambda": red["lru_lambda"][None]})

    dmod_all = g3[:, :72].reshape(8, 9 * D)
    dmod_cols = jnp.pad(lax.dynamic_slice(dmod_all, (0, p * 2304), (8, 2304)), ((0, 120), (0, 0)))
    c_all_t = jnp.pad(c_all.T, ((0, 0), (0, 120)))
    grads["w_ada"] = ada_bwd(c_all_t, dmod_cols, "ada_bwd")[None]

    delta, new_m, new_v = {}, {}, {}

    def update(n):
        shp = W[n].shape
        res = adamw(W[n][0], grads[n][0], M[n][0], V[n][0], "adamw_" + n, emit_g=n in geo)
        delta[n], new_m[n], new_v[n] = [a.reshape(shp) for a in res[:3]]
        if n in geo:
            grads[n] = res[3].reshape(shp)

    for n in ("w_ada", "ffn2_w_gu", "ffn2_w_down", "w_in", "w_att_o", "w_rec_o", "w_out"):
        update(n)
    packed = [_pack([src[n] for n in SMALL], 1168) for src in (W, grads, M, V)]
    outs = adamw(*packed, "adamw_small")
    for dst, blk in zip((delta, new_m, new_v), outs):
        for n, a in zip(SMALL, _unpack(blk, [W[n].shape for n in SMALL])):
            dst[n] = a
    reduce_end(("ffn1_w_gu", "ffn1_w_down"), fly_ffn1,
               [outs[0], delta["w_ada"], delta["ffn2_w_gu"], delta["ffn2_w_down"], delta["w_in"], delta["w_out"]], "ffn1")
    for n in ("ffn1_w_gu", "ffn1_w_down"):
        update(n)

    return (loss, dx0[None], *[grads[n] for n in WEIGHTS], *[delta[n] for n in WEIGHTS],
            *[new_m[n] for n in WEIGHTS], *[new_v[n] for n in WEIGHTS])
```

```python
import numpy as np
import jax
import jax.numpy as jnp
from jax import lax
from jax.experimental import pallas as pl
from jax.experimental.pallas import tpu as pltpu

F32 = jnp.float32
BF16 = jnp.bfloat16

D = 1024
FF = 2816
PW = 5632
HP = 128
CHUNK = 64
WIN = 640
TQ = 512
EPS = 1e-6
NEG = -1e30
LRU_C = 8.0
N_DEV = 8
VMEM_LIMIT = 56 * 1024 * 1024

ADAM_LR, ADAM_B1, ADAM_B2, ADAM_EPS, ADAM_WD, ADAM_STEP = 0.001, 0.9, 0.999, 1e-08, 0.01, 10

MESH = pl.DeviceIdType.MESH
ANY = pl.BlockSpec(memory_space=pl.ANY)


def _cp(*sem):
    return pltpu.CompilerParams(dimension_semantics=tuple(sem), vmem_limit_bytes=VMEM_LIMIT)


def _dot(a, b):
    return jnp.dot(a, b, preferred_element_type=F32)


def _dot_nt(a, b):
    return lax.dot_general(a, b, (((1,), (1,)), ((), ())), preferred_element_type=F32)


def _dot_tn(a, b):
    return lax.dot_general(a, b, (((0,), (0,)), ((), ())), preferred_element_type=F32)


def _mean(v):
    return jnp.mean(v, axis=-1, keepdims=True)


def _colsum(v):
    return jnp.sum(v, axis=0, keepdims=True)


def _sigmoid(v):
    return 0.5 * jnp.tanh(0.5 * v) + 0.5


_GK = 0.7978845608028654


def _gelu(v):
    t = jnp.tanh(_GK * (v + 0.044715 * v * v * v))
    return 0.5 * v * (1.0 + t)


def _pre_norm(xv, vec_ref):
    r = lax.rsqrt(_mean(xv * xv) + EPS)
    n = xv * r * vec_ref[0:1, :]
    return n * (1.0 + vec_ref[3:4, :]) + vec_ref[2:3, :]


def _pre_norm_bwd(dh, xv, dres, vec_ref, vacc_ref):
    r = lax.rsqrt(_mean(xv * xv) + EPS)
    xh = xv * r
    n = xh * vec_ref[0:1, :]
    vacc_ref[2:3, :] += _colsum(dh)
    vacc_ref[3:4, :] += _colsum(dh * n)
    dn = dh * (1.0 + vec_ref[3:4, :])
    vacc_ref[0:1, :] += _colsum(dn * xh)
    dxh = dn * vec_ref[0:1, :]
    return r * (dxh - xh * _mean(dxh * xh)) + dres


def _post_norm_bwd(dxo, fv, res, vec_ref, vacc_ref):
    rf = lax.rsqrt(_mean(fv * fv) + EPS)
    fh = fv * rf
    gp = vec_ref[1:2, :]
    vacc_ref[4:5, :] += _colsum(res * dxo * (fh * gp))
    dy = (res * vec_ref[4:5, :]) * dxo
    vacc_ref[1:2, :] += _colsum(dy * fh)
    dfn = dy * gp
    return rf * (dfn - fh * _mean(dfn * fh))


def _u_spec(tf):
    return pl.BlockSpec((pl.Element(D), pl.Element(tf)),
                        lambda i, j: (0, pl.multiple_of(jnp.minimum(FF + j * tf, 2 * FF - tf), 128)))


def ffn_fwd(x, vec, w_gu, w_dn, res, name, tgt=None, tm=1024, tf=512):
    S = x.shape[0]
    tm = min(tm, S)
    nt = S // tm
    nf = -(-FF // tf)
    tail = FF - tf * (nf - 1)
    halves = [pl.ds(r * (tm // 2), tm // 2) for r in range(2)]
    head = tgt is not None

    def body(*refs):
        x_ref, vec_ref, wg_ref, wu_ref, wd_ref = refs[:5]
        if head:
            t_hbm, xo_ref, h_ref, g_ref, u_ref, a_ref, f_ref, l_ref, hs, acc, lacc, ts, tsem = refs[5:]
        else:
            xo_ref, h_ref, g_ref, u_ref, a_ref, f_ref, hs, acc = refs[5:]
        i, j = pl.program_id(0), pl.program_id(1)
        if head:
            late = pltpu.make_async_copy(t_hbm.at[pl.ds(pl.multiple_of(i * tm, 8), tm)], ts, tsem)

        @pl.when(j == 0)
        def _():
            if head:
                late.start()
            h = _pre_norm(x_ref[...], vec_ref).astype(BF16)
            hs[...] = h
            h_ref[...] = h
            acc[...] = jnp.zeros_like(acc)

        def chunk(w):
            gu = [(_dot(hs[r, :], wg_ref[:, 0:w]), _dot(hs[r, :], wu_ref[:, tf - w:tf])) for r in halves]
            acts = []
            for r, (g, u) in zip(halves, gu):
                g_ref[r, 0:w] = g.astype(BF16)
                u_ref[r, 0:w] = u.astype(BF16)
                a = (g * _sigmoid(g) * u).astype(BF16)
                a_ref[r, 0:w] = a
                acts.append(a)
            for r, a in zip(halves, acts):
                acc[r, :] += _dot(a, wd_ref[0:w, :])

        @pl.when(j < nf - 1)
        def _():
            chunk(tf)

        @pl.when(j == nf - 1)
        def _():
            chunk(tail)
            f = acc[...]
            f_ref[...] = f.astype(BF16)
            y = f * lax.rsqrt(_mean(f * f) + EPS) * vec_ref[1:2, :]
            xo = x_ref[...] + (res * vec_ref[4:5, :]) * y
            if head:
                @pl.when(i == 0)
                def _():
                    lacc[...] = jnp.zeros_like(lacc)

                late.wait()
                d = xo - ts[...]
                xo_ref[...] = d * (1.0 / D)
                lacc[...] += _colsum(d * d)

                @pl.when(i == nt - 1)
                def _():
                    l_ref[...] = jnp.broadcast_to(0.5 * jnp.sum(lacc[...]) * (1.0 / D), (8, 128))
            else:
                xo_ref[...] = xo

    row = lambda i, j: (i, 0)
    col = lambda i, j: (i, j)
    in_specs = [pl.BlockSpec((tm, D), row), pl.BlockSpec((8, D), lambda i, j: (0, 0)),
                pl.BlockSpec((D, tf), lambda i, j: (0, j)), _u_spec(tf),
                pl.BlockSpec((tf, D), lambda i, j: (j, 0))]
    out_specs = [pl.BlockSpec((tm, D), row), pl.BlockSpec((tm, D), row), pl.BlockSpec((tm, tf), col),
                 pl.BlockSpec((tm, tf), col), pl.BlockSpec((tm, tf), col), pl.BlockSpec((tm, D), row)]
    out_shape = [jax.ShapeDtypeStruct((S, D), F32), jax.ShapeDtypeStruct((S, D), BF16),
                 jax.ShapeDtypeStruct((S, FF), BF16), jax.ShapeDtypeStruct((S, FF), BF16),
                 jax.ShapeDtypeStruct((S, FF), BF16), jax.ShapeDtypeStruct((S, D), BF16)]
    scratch = [pltpu.VMEM((tm, D), BF16), pltpu.VMEM((tm, D), F32)]
    args = [x, vec, w_gu, w_gu, w_dn]
    if head:
        in_specs.append(ANY)
        out_specs.append(pl.BlockSpec((8, 128), lambda i, j: (0, 0)))
        out_shape.append(jax.ShapeDtypeStruct((8, 128), F32))
        scratch += [pltpu.VMEM((1, D), F32), pltpu.VMEM((tm, D), F32), pltpu.SemaphoreType.DMA]
        args.append(tgt)
    return pl.pallas_call(
        body, name=name, grid=(nt, nf), in_specs=in_specs, out_specs=out_specs, out_shape=out_shape,
        scratch_shapes=scratch,
        compiler_params=_cp("arbitrary" if head else "parallel", "arbitrary"),
    )(*args)


def ffn_bwd(dxo, x, f, g, u, vec, w_gu, w_dn, res, name, tm=1024, tf=512):
    S = x.shape[0]
    tm = min(tm, S)
    nf = -(-FF // tf)
    tail = FF - tf * (nf - 1)
    halves = [pl.ds(r * (tm // 2), tm // 2) for r in range(2)]

    def body(dxo_ref, x_hbm, f_ref, g_ref, u_ref, vec_ref, wg_ref, wu_ref, wd_ref,
             dx_ref, df_ref, dgu_ref, vacc_ref, dfs, acc, xs, xsem):
        i, j = pl.program_id(0), pl.program_id(1)
        late = pltpu.make_async_copy(x_hbm.at[pl.ds(pl.multiple_of(i * tm, 8), tm)], xs, xsem)

        @pl.when((i == 0) & (j == 0))
        def _():
            vacc_ref[...] = jnp.zeros_like(vacc_ref)

        @pl.when(j == 0)
        def _():
            late.start()
            df = _post_norm_bwd(dxo_ref[...], f_ref[...].astype(F32), res, vec_ref, vacc_ref).astype(BF16)
            dfs[...] = df
            df_ref[...] = df
            acc[...] = jnp.zeros_like(acc)

        def chunk(w):
            da = [_dot_nt(dfs[h, :], wd_ref[0:w, :]) for h in halves]
            dgu = []
            for h, d in zip(halves, da):
                gv, uv = g_ref[h, 0:w].astype(F32), u_ref[h, 0:w].astype(F32)
                sg = _sigmoid(gv)
                dg = (d * uv * (sg * (1.0 + gv * (1.0 - sg)))).astype(BF16)
                du = (d * (gv * sg)).astype(BF16)
                dgu_ref[0, h, 0:w] = dg
                dgu_ref[1, h, 0:w] = du
                dgu.append((dg, du))
            for h, (dg, du) in zip(halves, dgu):
                acc[h, :] += _dot_nt(dg, wg_ref[:, 0:w]) + _dot_nt(du, wu_ref[:, tf - w:tf])

        @pl.when(j < nf - 1)
        def _():
            chunk(tf)

        @pl.when(j == nf - 1)
        def _():
            chunk(tail)
            late.wait()
            dx_ref[...] = _pre_norm_bwd(acc[...], xs[...], dxo_ref[...], vec_ref, vacc_ref)

    row = lambda i, j: (i, 0)
    col = lambda i, j: (i, j)
    return pl.pallas_call(
        body, name=name, grid=(S // tm, nf),
        in_specs=[pl.BlockSpec((tm, D), row), ANY, pl.BlockSpec((tm, D), row),
                  pl.BlockSpec((tm, tf), col), pl.BlockSpec((tm, tf), col),
                  pl.BlockSpec((8, D), lambda i, j: (0, 0)),
                  pl.BlockSpec((D, tf), lambda i, j: (0, j)), _u_spec(tf),
                  pl.BlockSpec((tf, D), lambda i, j: (j, 0))],
        out_specs=[pl.BlockSpec((tm, D), row), pl.BlockSpec((tm, D), row),
                   pl.BlockSpec((2, tm, tf), lambda i, j: (0, i, j)),
                   pl.BlockSpec((8, D), lambda i, j: (0, 0))],
        out_shape=[jax.ShapeDtypeStruct((S, D), F32), jax.ShapeDtypeStruct((S, D), BF16),
                   jax.ShapeDtypeStruct((2, S, FF), BF16), jax.ShapeDtypeStruct((8, D), F32)],
        scratch_shapes=[pltpu.VMEM((tm, D), BF16), pltpu.VMEM((tm, D), F32), pltpu.VMEM((tm, D), F32),
                        pltpu.SemaphoreType.DMA],
        compiler_params=_cp("arbitrary", "arbitrary"),
    )(dxo, x, f, g, u, vec, w_gu, w_gu, w_dn)


def mm_tn(a, b, name, tm, tn, tk, out_dtype=BF16, a_resident=False):
    S, M = a.shape
    if b.ndim == 3:
        G, _, Nf = b.shape
    else:
        G, Nf = 1, b.shape[1]
    N = G * Nf
    tk = min(tk, S)
    nbf = Nf // tn
    nk = S // tk

    def body(a_ref, b_ref, o_ref, acc):
        k = pl.program_id(2)

        @pl.when(k == 0)
        def _():
            acc[...] = jnp.zeros_like(acc)

        a_blk = a_ref[pl.ds(pl.multiple_of(k * tk, 16), tk), :] if a_resident else a_ref[...]
        acc[...] += _dot_tn(a_blk, b_ref[...])

        @pl.when(k == nk - 1)
        def _():
            o_ref[...] = acc[...].astype(out_dtype)

    if b.ndim == 3:
        b_spec = pl.BlockSpec((None, tk, tn), lambda i, j, k: (j // nbf, k, j % nbf))
    else:
        b_spec = pl.BlockSpec((tk, tn), lambda i, j, k: (k, j))
    if a_resident:
        a_spec = pl.BlockSpec((S, M), lambda i, j, k: (0, 0), pipeline_mode=pl.Buffered(1))
    else:
        a_spec = pl.BlockSpec((tk, tm), lambda i, j, k: (k, i))
    return pl.pallas_call(
        body, name=name, grid=(M // tm, N // tn, nk),
        in_specs=[a_spec, b_spec],
        out_specs=pl.BlockSpec((tm, tn), lambda i, j, k: (i, j)),
        out_shape=jax.ShapeDtypeStruct((M, N), out_dtype),
        scratch_shapes=[pltpu.VMEM((tm, tn), F32)],
        compiler_params=_cp("parallel", "parallel", "arbitrary"),
    )(a, b)


def proj_fwd(x, vec, w_in, name, tm=2048, tn=512):
    S = x.shape[0]
    tm = min(tm, S)
    nq = 1536 // tn

    def body(x_ref, vec_ref, w_ref, h_ref, qkv_ref, rest_ref, hs):
        j = pl.program_id(1)

        @pl.when(j == 0)
        def _():
            h = _pre_norm(x_ref[...], vec_ref).astype(BF16)
            hs[...] = h
            h_ref[...] = h

        r = _dot(hs[...], w_ref[...])

        @pl.when(j < nq)
        def _():
            qkv_ref[...] = r.astype(BF16)

        @pl.when(j >= nq)
        def _():
            rest_ref[...] = r.astype(BF16)

    row = lambda i, j: (i, 0)
    return pl.pallas_call(
        body, name=name, grid=(S // tm, PW // tn),
        in_specs=[pl.BlockSpec((tm, D), row), pl.BlockSpec((8, D), lambda i, j: (0, 0)),
                  pl.BlockSpec((D, tn), lambda i, j: (0, j))],
        out_specs=[pl.BlockSpec((tm, D), row),
                   pl.BlockSpec((tm, tn), lambda i, j: (i, jnp.minimum(j, nq - 1))),
                   pl.BlockSpec((tm, tn), lambda i, j: (i, jnp.maximum(j - nq, 0)))],
        out_shape=[jax.ShapeDtypeStruct((S, D), BF16), jax.ShapeDtypeStruct((S, 1536), BF16),
                   jax.ShapeDtypeStruct((S, 4096), BF16)],
        scratch_shapes=[pltpu.VMEM((tm, D), BF16)],
        compiler_params=_cp("parallel", "arbitrary"),
    )(x, vec, w_in)


def proj_bwd(dq, dkv, dxr, d3, w_in, x, dxo, vec, name, tm=2048, tk=512):
    S = x.shape[0]
    tm = min(tm, S)
    nk = PW // tk

    def body(dq_ref, dkv_ref, dxr_ref, d3_ref, w_ref, x_hbm, dxo_hbm, vec_ref, dx_ref, vacc_ref, acc, xs, dxos, sems):
        i, j = pl.program_id(0), pl.program_id(1)
        tok = pl.ds(pl.multiple_of(i * tm, 8), tm)
        late = (pltpu.make_async_copy(x_hbm.at[tok], xs, sems.at[0]),
                pltpu.make_async_copy(dxo_hbm.at[tok], dxos, sems.at[1]))

        @pl.when((i == 0) & (j == 0))
        def _():
            vacc_ref[...] = jnp.zeros_like(vacc_ref)

        @pl.when(j == 0)
        def _():
            for cp in late:
                cp.start()
            acc[...] = _dot_nt(dq_ref[...], w_ref[...])

        @pl.when((j >= 1) & (j < 3))
        def _():
            acc[...] += _dot_nt(dkv_ref[...], w_ref[...])

        @pl.when((j >= 3) & (j < 5))
        def _():
            acc[...] += _dot_nt(dxr_ref[...], w_ref[...])

        @pl.when(j >= 5)
        def _():
            acc[...] += _dot_nt(d3_ref[...], w_ref[...])

        @pl.when(j == nk - 1)
        def _():
            for cp in late:
                cp.wait()
            dx_ref[...] = _pre_norm_bwd(acc[...], xs[...], dxos[...], vec_ref, vacc_ref)

    row = lambda i, j: (i, 0)
    return pl.pallas_call(
        body, name=name, grid=(S // tm, nk),
        in_specs=[pl.BlockSpec((None, tm, tk), lambda i, j: (0, i, 0)),
                  pl.BlockSpec((None, tm, tk), lambda i, j: (jnp.clip(j - 1, 0, 1), i, 0)),
                  pl.BlockSpec((tm, tk), lambda i, j: (i, jnp.clip(j - 3, 0, 1))),
                  pl.BlockSpec((None, tm, tk), lambda i, j: (jnp.clip(j - 5, 0, 5) // 2, i, jnp.clip(j - 5, 0, 5) % 2)),
                  pl.BlockSpec((D, tk), lambda i, j: (0, j)),
                  ANY, ANY,
                  pl.BlockSpec((8, D), lambda i, j: (0, 0))],
        out_specs=[pl.BlockSpec((tm, D), row, pipeline_mode=pl.Buffered(1)),
                   pl.BlockSpec((8, D), lambda i, j: (0, 0))],
        out_shape=[jax.ShapeDtypeStruct((S, D), F32), jax.ShapeDtypeStruct((8, D), F32)],
        scratch_shapes=[pltpu.VMEM((tm, D), F32), pltpu.VMEM((tm, D), F32), pltpu.VMEM((tm, D), F32),
                        pltpu.SemaphoreType.DMA((2,))],
        compiler_params=_cp("arbitrary", "arbitrary"),
    )(dq, dkv, dxr, d3, w_in, x, dxo, vec)


def _two_heads(v, lane):
    zero = jnp.zeros((), v.dtype)
    return jnp.concatenate([jnp.where(lane < 64, v, zero), jnp.where(lane >= 64, v, zero)], axis=0)


def _attn_scores(qm, ka, bias_h, i, grp):
    s = _dot_nt(qm, ka) + bias_h
    col = lax.broadcasted_iota(jnp.int32, s.shape, 1)
    first_key = jnp.where(i == 0, 512 - 128 * grp, 0)
    return jnp.where(col >= first_key, s, NEG)


def _softmax(s):
    e = jnp.exp(s - jnp.max(s, axis=-1, keepdims=True))
    return e * (1.0 / jnp.sum(e, axis=-1, keepdims=True))


NG = TQ // 128


def attn_fwd(qkv, bias, name):
    S = qkv.shape[0]
    nb = S // TQ

    def body(q_ref, kp_ref, kc_ref, vp_ref, vc_ref, b_ref, o_ref, kw, vw):
        i = pl.program_id(1)
        kw[0:TQ, :] = kp_ref[...]
        kw[TQ:2 * TQ, :] = kc_ref[...]
        vw[0:TQ, :] = vp_ref[...]
        vw[TQ:2 * TQ, :] = vc_ref[...]
        lane = lax.broadcasted_iota(jnp.int32, (1, HP), 1)

        rows = [pl.ds(128 * a, 128) for a in range(NG)]
        keys = [pl.ds(128 * a, WIN) for a in range(NG)]
        q2 = [_two_heads(q_ref[r, :] * jnp.asarray(0.125, BF16), lane) for r in rows]
        s = [_attn_scores(q2[a], kw[keys[a], :], b_ref[...], i, a) for a in range(NG)]
        p = [_softmax(sa).astype(BF16) for sa in s]
        o2 = [_dot(p[a], vw[keys[a], :]) for a in range(NG)]
        for a in range(NG):
            o_ref[rows[a], :] = jnp.where(lane < 64, o2[a][0:128], o2[a][128:256]).astype(BF16)

    prev = lambda h, i: (jnp.maximum(i - 1, 0), 0)
    return pl.pallas_call(
        body, name=name, grid=(4, nb),
        in_specs=[pl.BlockSpec((TQ, HP), lambda h, i: (i, h)),
                  pl.BlockSpec((TQ, HP), lambda h, i: (jnp.maximum(i - 1, 0), 4 + h)),
                  pl.BlockSpec((TQ, HP), lambda h, i: (i, 4 + h)),
                  pl.BlockSpec((TQ, HP), lambda h, i: (jnp.maximum(i - 1, 0), 8 + h)),
                  pl.BlockSpec((TQ, HP), lambda h, i: (i, 8 + h)),
                  pl.BlockSpec((None, 256, WIN), lambda h, i: (h, 0, 0))],
        out_specs=pl.BlockSpec((TQ, HP), lambda h, i: (i, h)),
        out_shape=jax.ShapeDtypeStruct((S, 512), BF16),
        scratch_shapes=[pltpu.VMEM((2 * TQ, HP), BF16), pltpu.VMEM((2 * TQ, HP), BF16)],
        compiler_params=_cp("parallel", "arbitrary"),
    )(qkv, qkv, qkv, qkv, qkv, bias)


def attn_bwd(qkv, do, bias, name):
    S = qkv.shape[0]
    nb = S // TQ

    def body(q_ref, kp_ref, kc_ref, vp_ref, vc_ref, do_ref, b_ref, dqkv_ref, db_ref, dkv_ref, kw, vw, ak, av):
        i = pl.program_id(1)

        @pl.when(i == 0)
        def _():
            db_ref[...] = jnp.zeros_like(db_ref)
            ak[...] = jnp.zeros_like(ak)
            av[...] = jnp.zeros_like(av)

        @pl.when(i > 0)
        def _():
            ak[0:TQ, :] = ak[TQ:2 * TQ, :]
            av[0:TQ, :] = av[TQ:2 * TQ, :]
            ak[TQ:2 * TQ, :] = jnp.zeros((TQ, HP), F32)
            av[TQ:2 * TQ, :] = jnp.zeros((TQ, HP), F32)

        @pl.when(i < nb)
        def _():
            kw[0:TQ, :] = kp_ref[...]
            kw[TQ:2 * TQ, :] = kc_ref[...]
            vw[0:TQ, :] = vp_ref[...]
            vw[TQ:2 * TQ, :] = vc_ref[...]
            lane = lax.broadcasted_iota(jnp.int32, (1, HP), 1)

            rows = [pl.ds(128 * a, 128) for a in range(NG)]
            keys = [pl.ds(128 * a, WIN) for a in range(NG)]
            q2 = [_two_heads(q_ref[r, :] * jnp.asarray(0.125, BF16), lane) for r in rows]
            do2 = [_two_heads(do_ref[r, :], lane) for r in rows]
            s = [_attn_scores(q2[a], kw[keys[a], :], b_ref[...], i, a) for a in range(NG)]
            dp = [_dot_nt(do2[a], vw[keys[a], :]) for a in range(NG)]
            p = [_softmax(sa) for sa in s]
            ds = [p[a] * (dp[a] - jnp.sum(p[a] * dp[a], axis=-1, keepdims=True)) for a in range(NG)]
            db_ref[...] += (ds[0] + ds[1]) + (ds[2] + ds[3])
            dsb = [d.astype(BF16) for d in ds]
            dq2 = [_dot(dsb[a], kw[keys[a], :]) for a in range(NG)]
            dk = [_dot_tn(dsb[a], q2[a]) for a in range(NG)]
            dv = [_dot_tn(p[a].astype(BF16), do2[a]) for a in range(NG)]
            for a in range(NG):
                ak[keys[a], :] += dk[a]
                av[keys[a], :] += dv[a]
                dq = jnp.where(lane < 64, dq2[a][0:128], dq2[a][128:256])
                dqkv_ref[0, rows[a], :] = (dq * 0.125).astype(BF16)

        @pl.when(i > 0)
        def _():
            dkv_ref[0] = ak[0:TQ, :].astype(BF16)
            dkv_ref[1] = av[0:TQ, :].astype(BF16)

    cur = lambda i: jnp.minimum(i, nb - 1)
    prv = lambda i: jnp.clip(i - 1, 0, nb - 1)
    dq, db, dkv = pl.pallas_call(
        body, name=name, grid=(4, nb + 1),
        in_specs=[pl.BlockSpec((TQ, HP), lambda h, i: (cur(i), h)),
                  pl.BlockSpec((TQ, HP), lambda h, i: (prv(i), 4 + h)),
                  pl.BlockSpec((TQ, HP), lambda h, i: (cur(i), 4 + h)),
                  pl.BlockSpec((TQ, HP), lambda h, i: (prv(i), 8 + h)),
                  pl.BlockSpec((TQ, HP), lambda h, i: (cur(i), 8 + h)),
                  pl.BlockSpec((TQ, HP), lambda h, i: (cur(i), h)),
                  pl.BlockSpec((None, 256, WIN), lambda h, i: (h, 0, 0))],
        out_specs=[pl.BlockSpec((1, TQ, HP), lambda h, i: (0, cur(i), h)),
                   pl.BlockSpec((None, 256, WIN), lambda h, i: (h, 0, 0)),
                   pl.BlockSpec((2, TQ, HP), lambda h, i: (0, prv(i), h))],
        out_shape=[jax.ShapeDtypeStruct((1, S, 512), BF16), jax.ShapeDtypeStruct((4, 256, WIN), F32),
                   jax.ShapeDtypeStruct((2, S, 512), BF16)],
        scratch_shapes=[pltpu.VMEM((2 * TQ, HP), BF16), pltpu.VMEM((2 * TQ, HP), BF16),
                        pltpu.VMEM((2 * TQ, HP), F32), pltpu.VMEM((2 * TQ, HP), F32)],
        compiler_params=_cp("parallel", "arbitrary"),
    )(qkv, qkv, qkv, qkv, qkv, do, bias)
    return dq, db, dkv


def bias_grad(db, name):
    def body(db_ref, o_ref):
        r = lax.broadcasted_iota(jnp.int32, (128, 128), 0)
        c = lax.broadcasted_iota(jnp.int32, (128, 128), 1)
        flip = (r + c == 127).astype(BF16)
        lane = lax.broadcasted_iota(jnp.int32, (16, 384), 1)
        src = lax.broadcasted_iota(jnp.int32, (128, 384), 0)
        dst = lax.broadcasted_iota(jnp.int32, (128, 384), 1)

        def split_dot(v, m):
            hi = v.astype(BF16)
            r1 = v - hi.astype(F32)
            mid = r1.astype(BF16)
            lo = (r1 - mid.astype(F32)).astype(BF16)
            return _dot(hi, m) + _dot(mid, m) + _dot(lo, m)

        def diag_sums(w):
            y = pltpu.roll(split_dot(w, flip), 0, 1, stride=1, stride_axis=0)
            return jnp.broadcast_to(_colsum(y), (16, 128))

        w4 = db_ref[0, :, 512:640]
        w3 = db_ref[0, :, 384:512]
        far = jnp.sum(db_ref[0, :, 0:384]) + jnp.sum(jnp.where(r >= c, w3, 0.0))
        lo4 = diag_sums(jnp.where(r >= c, w4, 0.0))
        up4 = diag_sums(jnp.where(r < c, w4, 0.0))
        up3 = diag_sums(jnp.where(r < c, w3, 0.0))
        p_lo4 = (dst == 128 + (src + 1) % 128).astype(BF16)
        p_up4 = ((dst == src + 1) & (src < 127)).astype(BF16)
        p_up3 = ((dst == src + 129) & (src < 127)).astype(BF16)
        out = split_dot(lo4, p_lo4) + split_dot(up4, p_up4) + split_dot(up3, p_up3)
        o_ref[0] = out + jnp.where(lane == 256, far, 0.0)

    return pl.pallas_call(
        body, name=name, grid=(8,),
        in_specs=[pl.BlockSpec((1, 128, WIN), lambda h: (h, 0, 0))],
        out_specs=pl.BlockSpec((1, 16, 384), lambda h: (h, 0, 0)),
        out_shape=jax.ShapeDtypeStruct((8, 16, 384), F32),
        compiler_params=_cp("parallel"),
    )(db)[:, 0, :]


LT = 1024
LC = 512
SG = 4


def _lru_gates(xs, pv_ref, wa_ref, wx_ref, tl):
    xc = (pv_ref[4:5, :] + pv_ref[3:4, :] * xs[pl.ds(8, tl), :] + pv_ref[2:3, :] * xs[pl.ds(7, tl), :]
          + pv_ref[1:2, :] * xs[pl.ds(6, tl), :] + pv_ref[0:1, :] * xs[pl.ds(5, tl), :])
    xcb = xc.astype(BF16)
    pa = jnp.concatenate([_dot(xcb[:, 0:256], wa_ref[0]), _dot(xcb[:, 256:512], wa_ref[1])], axis=1)
    px = jnp.concatenate([_dot(xcb[:, 0:256], wx_ref[0]), _dot(xcb[:, 256:512], wx_ref[1])], axis=1)
    r = _sigmoid(pa + pv_ref[5:6, :])
    ig = _sigmoid(px + pv_ref[6:7, :])
    z = -pv_ref[7:8, :]
    sp = jnp.maximum(z, 0.0) + jnp.log1p(jnp.exp(-jnp.abs(z)))
    log_a = (-LRU_C * r) * sp
    a = jnp.exp(log_a)
    s = jnp.tanh(-log_a) * (1.0 + a * a)
    inv_mult = lax.rsqrt(s)
    mult = jnp.where(s > 0.0, s * inv_mult, 0.0)
    return xc, xcb, r, ig, sp, a, mult, inv_mult


def lru_fwd(rest, pvec, wa, wx, name):
    S = rest.shape[0]
    tl = min(LT, S)
    nt = S // tl

    def body(xr_ref, halo_ref, yr_ref, pv_ref, wa_ref, wx_ref, h_ref, hg_ref, xs, a_s, u_s, h_s, carry):
        ti = pl.program_id(1)

        @pl.when(ti == 0)
        def _():
            carry[...] = jnp.zeros_like(carry)

        xs[0:8, :] = jnp.where(ti > 0, halo_ref[8:16, :].astype(F32), 0.0)
        xs[pl.ds(8, tl), :] = xr_ref[...].astype(F32)
        xc, _, _, ig, _, a, mult, _ = _lru_gates(xs, pv_ref, wa_ref, wx_ref, tl)
        a_s[...] = a
        u_s[...] = mult * (ig * xc)
        row = lax.broadcasted_iota(jnp.int32, (8, LC), 0)

        def local(o):
            av = a_s[pl.ds(o, 8), :]
            bv = u_s[pl.ds(o, 8), :]
            for d in (1, 2, 4):
                a_sh = pltpu.roll(av, d, 0)
                b_sh = pltpu.roll(bv, d, 0)
                m = row >= d
                bv = jnp.where(m, av * b_sh + bv, bv)
                av = jnp.where(m, av * a_sh, av)
            return av, bv

        def blk(gi, c):
            os_ = [pl.multiple_of(gi * (8 * SG) + 8 * q, 8) for q in range(SG)]
            loc = [local(o) for o in os_]
            cs = []
            for av, bv in loc:
                cs.append(c)
                c = bv[7:8, :] + av[7:8, :] * c
            for o, (av, bv), ci in zip(os_, loc, cs):
                h_s[pl.ds(o, 8), :] = bv + av * ci
            return c

        carry[...] = lax.fori_loop(0, tl // (8 * SG), blk, carry[...])
        h = h_s[...]
        h_ref[...] = h
        hg_ref[...] = (h * _gelu(yr_ref[...].astype(F32))).astype(BF16)

    hb = tl // 16
    return pl.pallas_call(
        body, name=name, grid=(2, nt),
        in_specs=[pl.BlockSpec((tl, LC), lambda c, t: (t, c)),
                  pl.BlockSpec((16, LC), lambda c, t: (jnp.maximum(t * hb - 1, 0), c)),
                  pl.BlockSpec((tl, LC), lambda c, t: (t, 2 + c)),
                  pl.BlockSpec((8, LC), lambda c, t: (0, c)),
                  pl.BlockSpec((2, 256, 256), lambda c, t: (c, 0, 0)),
                  pl.BlockSpec((2, 256, 256), lambda c, t: (c, 0, 0))],
        out_specs=[pl.BlockSpec((tl, LC), lambda c, t: (t, c)), pl.BlockSpec((tl, LC), lambda c, t: (t, c))],
        out_shape=[jax.ShapeDtypeStruct((S, D), F32), jax.ShapeDtypeStruct((S, D), BF16)],
        scratch_shapes=[pltpu.VMEM((tl + 8, LC), F32), pltpu.VMEM((tl, LC), F32), pltpu.VMEM((tl, LC), F32),
                        pltpu.VMEM((tl, LC), F32), pltpu.VMEM((1, LC), F32)],
        compiler_params=_cp("parallel", "arbitrary"),
    )(rest, rest, rest, pvec, wa, wx)


def lru_bwd(dh, h, rest, pvec, wa, wx, name):
    S = rest.shape[0]
    tl = min(LT, S)
    nt = S // tl

    def body(dh_ref, h_ref, hhalo_ref, xr_ref, xhalo_ref, pv_ref, wa_ref, wx_ref,
             dxr_ref, vacc_ref, dwa_ref, dwx_ref,
             xs, hs, a_s, ash_s, b_s, lam_s, dxe, anext, lnext, dxnext):
        ti = pl.program_id(1)
        tr = nt - 1 - ti

        @pl.when(ti == 0)
        def _():
            anext[...] = jnp.zeros_like(anext)
            lnext[...] = jnp.zeros_like(lnext)
            dxnext[...] = jnp.zeros_like(dxnext)
            vacc_ref[...] = jnp.zeros_like(vacc_ref)
            dwa_ref[...] = jnp.zeros_like(dwa_ref)
            dwx_ref[...] = jnp.zeros_like(dwx_ref)

        xs[0:8, :] = jnp.where(tr > 0, xhalo_ref[8:16, :].astype(F32), 0.0)
        xs[pl.ds(8, tl), :] = xr_ref[...].astype(F32)
        xc, xcb, r, ig, sp, a, mult, inv_mult = _lru_gates(xs, pv_ref, wa_ref, wx_ref, tl)

        a_s[pl.ds(0, tl), :] = a
        a_s[pl.ds(tl, 8), :] = jnp.broadcast_to(anext[...], (8, LC))
        ash_s[...] = a_s[pl.ds(1, tl), :]
        b_s[...] = dh_ref[...]
        row = lax.broadcasted_iota(jnp.int32, (8, LC), 0)

        def local(o):
            av = ash_s[pl.ds(o, 8), :]
            bv = b_s[pl.ds(o, 8), :]
            for d in (1, 2, 4):
                a_sh = pltpu.roll(av, 8 - d, 0)
                b_sh = pltpu.roll(bv, 8 - d, 0)
                m = row < 8 - d
                bv = jnp.where(m, bv + av * b_sh, bv)
                av = jnp.where(m, av * a_sh, av)
            return av, bv

        def blk(k, c):
            os_ = [pl.multiple_of((tl // 8 - 1 - (k * SG + q)) * 8, 8) for q in range(SG)]
            loc = [local(o) for o in os_]
            cs = []
            for av, bv in loc:
                cs.append(c)
                c = bv[0:1, :] + av[0:1, :] * c
            for o, (av, bv), ci in zip(os_, loc, cs):
                lam_s[pl.ds(o, 8), :] = bv + av * ci
            return c

        lnext[...] = lax.fori_loop(0, tl // (8 * SG), blk, lnext[...])
        anext[...] = a[0:1, :]
        lam = lam_s[...]

        hs[0:8, :] = jnp.where(tr > 0, hhalo_ref[...], 0.0)
        hs[pl.ds(8, tl), :] = h_ref[...]
        d_a = lam * hs[pl.ds(7, tl), :]
        d_mult = lam * (ig * xc)
        d_ig = lam * mult * xc
        dxc = lam * mult * ig
        d_log_a = d_a * a - d_mult * (a * a) * inv_mult
        d_r = d_log_a * (-LRU_C * sp)
        vacc_ref[7:8, :] += _colsum(d_log_a * (-LRU_C * r)) * (-_sigmoid(-pv_ref[7:8, :]))
        d_pa = d_r * r * (1.0 - r)
        d_px = d_ig * ig * (1.0 - ig)
        vacc_ref[5:6, :] += _colsum(d_pa)
        vacc_ref[6:7, :] += _colsum(d_px)
        dpa = d_pa.astype(BF16)
        dpx = d_px.astype(BF16)
        back = []
        for g in range(2):
            sl = slice(256 * g, 256 * g + 256)
            dwa_ref[g] += _dot_tn(xcb[:, sl], dpa[:, sl])
            dwx_ref[g] += _dot_tn(xcb[:, sl], dpx[:, sl])
            back.append(_dot_nt(dpa[:, sl], wa_ref[g]) + _dot_nt(dpx[:, sl], wx_ref[g]))
        dxc = dxc + jnp.concatenate(back, axis=1)
        vacc_ref[4:5, :] += _colsum(dxc)
        for k in range(4):
            vacc_ref[k:k + 1, :] += _colsum(dxc * xs[pl.ds(5 + k, tl), :])
        dxe[pl.ds(0, tl), :] = dxc
        dxe[pl.ds(tl, 8), :] = dxnext[...]
        dxr = (pv_ref[3:4, :] * dxc + pv_ref[2:3, :] * dxe[pl.ds(1, tl), :]
               + pv_ref[1:2, :] * dxe[pl.ds(2, tl), :] + pv_ref[0:1, :] * dxe[pl.ds(3, tl), :])
        dxr_ref[...] = dxr.astype(BF16)
        dxnext[...] = dxc[0:8, :]

    hb = tl // 8
    rev = lambda t: nt - 1 - t
    halo = lambda t: jnp.maximum(rev(t) * hb - 1, 0)
    big = lambda: pltpu.VMEM((tl + 8, LC), F32)
    til = lambda: pltpu.VMEM((tl, LC), F32)
    return pl.pallas_call(
        body, name=name, grid=(2, nt),
        in_specs=[pl.BlockSpec((tl, LC), lambda c, t: (rev(t), c)),
                  pl.BlockSpec((tl, LC), lambda c, t: (rev(t), c)),
                  pl.BlockSpec((8, LC), lambda c, t: (halo(t), c)),
                  pl.BlockSpec((tl, LC), lambda c, t: (rev(t), c)),
                  pl.BlockSpec((16, LC), lambda c, t: (jnp.maximum(rev(t) * (tl // 16) - 1, 0), c)),
                  pl.BlockSpec((8, LC), lambda c, t: (0, c)),
                  pl.BlockSpec((2, 256, 256), lambda c, t: (c, 0, 0)),
                  pl.BlockSpec((2, 256, 256), lambda c, t: (c, 0, 0))],
        out_specs=[pl.BlockSpec((tl, LC), lambda c, t: (rev(t), c)),
                   pl.BlockSpec((8, LC), lambda c, t: (0, c)),
                   pl.BlockSpec((2, 256, 256), lambda c, t: (c, 0, 0)),
                   pl.BlockSpec((2, 256, 256), lambda c, t: (c, 0, 0))],
        out_shape=[jax.ShapeDtypeStruct((S, D), BF16), jax.ShapeDtypeStruct((8, D), F32),
                   jax.ShapeDtypeStruct((4, 256, 256), F32), jax.ShapeDtypeStruct((4, 256, 256), F32)],
        scratch_shapes=[big(), big(), big(), til(), til(), til(), big(),
                        pltpu.VMEM((1, LC), F32), pltpu.VMEM((1, LC), F32), pltpu.VMEM((8, LC), F32)],
        compiler_params=_cp("parallel", "arbitrary"),
    )(dh, h, h, rest, rest, pvec, wa, wx)


def mix_out_fwd(x, ao, hg, rest, vec, w_att_o, w_rec_o, w_out, name, tm=512):
    S = x.shape[0]
    tm = min(tm, S)

    def body(x_ref, ao_ref, hg_ref, ga_ref, gr_ref, vec_ref, wa_ref, wr_ref, wo_ref,
             xo_ref, att_ref, rec_ref, mg_ref, f_ref):
        att = _dot(ao_ref[...], wa_ref[...])
        rec = _dot(hg_ref[...], wr_ref[...])
        att_ref[...] = att.astype(BF16)
        rec_ref[...] = rec.astype(BF16)
        mg = (_sigmoid(ga_ref[...].astype(F32)) * att + _sigmoid(gr_ref[...].astype(F32)) * rec).astype(BF16)
        mg_ref[...] = mg
        f = _dot(mg, wo_ref[...])
        f_ref[...] = f.astype(BF16)
        y = f * lax.rsqrt(_mean(f * f) + EPS) * vec_ref[1:2, :]
        xo_ref[...] = x_ref[...] + (1.0 * vec_ref[4:5, :]) * y

    row = lambda i: (i, 0)
    full = lambda r: pl.BlockSpec((r, D), lambda i: (0, 0))
    return pl.pallas_call(
        body, name=name, grid=(S // tm,),
        in_specs=[pl.BlockSpec((tm, D), row), pl.BlockSpec((tm, 512), row), pl.BlockSpec((tm, D), row),
                  pl.BlockSpec((tm, D), lambda i: (i, 2)), pl.BlockSpec((tm, D), lambda i: (i, 3)),
                  full(8), full(512), full(D), full(D)],
        out_specs=[pl.BlockSpec((tm, D), row)] * 5,
        out_shape=[jax.ShapeDtypeStruct((S, D), F32)] + [jax.ShapeDtypeStruct((S, D), BF16)] * 4,
        compiler_params=_cp("parallel"),
    )(x, ao, hg, rest, rest, vec, w_att_o, w_rec_o, w_out)


def mix_out_bwd(dxo, f, att, rec, rest, h, vec, w_att_o, w_rec_o, w_out, name, tm=512):
    S = dxo.shape[0]
    tm = min(tm, S)

    def body(dxo_ref, f_ref, att_ref, rec_ref, yr_ref, ga_ref, gr_ref, h_ref, vec_ref, wa_ref, wr_ref, wo_ref,
             df_ref, da_ref, dr_ref, dao_ref, dh_ref, d3_ref, vacc_ref):
        @pl.when(pl.program_id(0) == 0)
        def _():
            vacc_ref[...] = jnp.zeros_like(vacc_ref)

        df = _post_norm_bwd(dxo_ref[...], f_ref[...].astype(F32), 1.0, vec_ref, vacc_ref).astype(BF16)
        df_ref[...] = df
        dm = _dot_nt(df, wo_ref[...])
        sa = _sigmoid(ga_ref[...].astype(F32))
        sr = _sigmoid(gr_ref[...].astype(F32))
        d_att = (dm * sa).astype(BF16)
        d_rec = (dm * sr).astype(BF16)
        da_ref[...] = d_att
        dr_ref[...] = d_rec
        d3_ref[1] = (dm * att_ref[...].astype(F32) * (sa * (1.0 - sa))).astype(BF16)
        d3_ref[2] = (dm * rec_ref[...].astype(F32) * (sr * (1.0 - sr))).astype(BF16)
        dao_ref[...] = _dot_nt(d_att, wa_ref[...]).astype(BF16)
        d_hg = _dot_nt(d_rec, wr_ref[...])
        yr = yr_ref[...].astype(F32)
        t = jnp.tanh(_GK * (yr + 0.044715 * yr * yr * yr))
        dh_ref[...] = d_hg * (0.5 * yr * (1.0 + t))
        gelu_grad = 0.5 * (1.0 + t) + 0.5 * yr * (1.0 - t * t) * _GK * (1.0 + 3.0 * 0.044715 * yr * yr)
        d3_ref[0] = (d_hg * h_ref[...] * gelu_grad).astype(BF16)

    row = lambda i: (i, 0)
    full = lambda r: pl.BlockSpec((r, D), lambda i: (0, 0))
    return pl.pallas_call(
        body, name=name, grid=(S // tm,),
        in_specs=[pl.BlockSpec((tm, D), row)] * 4
        + [pl.BlockSpec((tm, D), lambda i: (i, 1)), pl.BlockSpec((tm, D), lambda i: (i, 2)),
           pl.BlockSpec((tm, D), lambda i: (i, 3)), pl.BlockSpec((tm, D), row),
           full(8), full(512), full(D), full(D)],
        out_specs=[pl.BlockSpec((tm, D), row)] * 3
        + [pl.BlockSpec((tm, 512), row), pl.BlockSpec((tm, D), row),
           pl.BlockSpec((3, tm, D), lambda i: (0, i, 0)), pl.BlockSpec((8, D), lambda i: (0, 0))],
        out_shape=[jax.ShapeDtypeStruct((S, D), BF16)] * 3
        + [jax.ShapeDtypeStruct((S, 512), BF16), jax.ShapeDtypeStruct((S, D), F32),
           jax.ShapeDtypeStruct((3, S, D), BF16), jax.ShapeDtypeStruct((8, D), F32)],
        compiler_params=_cp("arbitrary"),
    )(dxo, f, att, rec, rest, rest, rest, h, vec, w_att_o, w_rec_o, w_out)


def dw_in(h, dq, dkv, dxr, d3, name, tk=1024, tn=512):
    S = h.shape[0]
    tk = min(tk, S)
    nk = S // tk

    def body(h_ref, dq_ref, dkv_ref, dxr_ref, d3_ref, o_ref, acc):
        j, k = pl.program_id(0), pl.program_id(1)

        @pl.when(k == 0)
        def _():
            acc[...] = jnp.zeros_like(acc)

        @pl.when(j == 0)
        def _():
            acc[...] += _dot_tn(h_ref[pl.ds(pl.multiple_of(k * tk, 16), tk), :], dq_ref[...])

        @pl.when((j >= 1) & (j < 3))
        def _():
            acc[...] += _dot_tn(h_ref[pl.ds(pl.multiple_of(k * tk, 16), tk), :], dkv_ref[...])

        @pl.when((j >= 3) & (j < 5))
        def _():
            acc[...] += _dot_tn(h_ref[pl.ds(pl.multiple_of(k * tk, 16), tk), :], dxr_ref[...])

        @pl.when(j >= 5)
        def _():
            acc[...] += _dot_tn(h_ref[pl.ds(pl.multiple_of(k * tk, 16), tk), :], d3_ref[...])

        @pl.when(k == nk - 1)
        def _():
            o_ref[...] = acc[...].astype(BF16)

    use = lambda j, k, lo, hi: jnp.where((j >= lo) & (j < hi), k, 0)
    g3 = lambda j: jnp.clip(j - 5, 0, 5)
    return pl.pallas_call(
        body, name=name, grid=(PW // tn, nk),
        in_specs=[pl.BlockSpec((S, D), lambda j, k: (0, 0), pipeline_mode=pl.Buffered(1)),
                  pl.BlockSpec((None, tk, tn), lambda j, k: (0, use(j, k, 0, 1), 0)),
                  pl.BlockSpec((None, tk, tn), lambda j, k: (jnp.clip(j - 1, 0, 1), use(j, k, 1, 3), 0)),
                  pl.BlockSpec((tk, tn), lambda j, k: (use(j, k, 3, 5), jnp.clip(j - 3, 0, 1))),
                  pl.BlockSpec((None, tk, tn), lambda j, k: (g3(j) // 2, use(j, k, 5, 11), g3(j) % 2))],
        out_specs=pl.BlockSpec((D, tn), lambda j, k: (0, j)),
        out_shape=jax.ShapeDtypeStruct((D, PW), BF16),
        scratch_shapes=[pltpu.VMEM((D, tn), F32)],
        compiler_params=_cp("parallel", "arbitrary"),
    )(h, dq, dkv, dxr, d3)


def ada_fwd(c_all, w_ada, b_ada, name, tn=768):
    n = w_ada.shape[1]

    def body(c_ref, w_ref, b_ref, o_ref):
        cv = c_ref[...]
        ca = (cv * _sigmoid(cv)).astype(BF16)
        o_ref[...] = _dot(ca, w_ref[...].astype(BF16)) + b_ref[...]

    return pl.pallas_call(
        body, name=name, grid=(n // tn,),
        in_specs=[pl.BlockSpec((8, D), lambda j: (0, 0)), pl.BlockSpec((D, tn), lambda j: (0, j)),
                  pl.BlockSpec((1, tn), lambda j: (0, j))],
        out_specs=pl.BlockSpec((8, tn), lambda j: (0, j)),
        out_shape=jax.ShapeDtypeStruct((8, n), F32),
        compiler_params=_cp("parallel"),
    )(c_all, w_ada, b_ada)


def ada_bwd(c_all_t, dmod, name, tn=768):
    n = dmod.shape[1]

    def body(c_ref, d_ref, o_ref):
        cv = c_ref[...]
        ca = (cv * _sigmoid(cv)).astype(BF16)
        o_ref[...] = _dot(ca, d_ref[...].astype(BF16))

    return pl.pallas_call(
        body, name=name, grid=(n // tn,),
        in_specs=[pl.BlockSpec((D, 128), lambda j: (0, 0)), pl.BlockSpec((128, tn), lambda j: (0, j))],
        out_specs=pl.BlockSpec((D, tn), lambda j: (0, j)),
        out_shape=jax.ShapeDtypeStruct((D, n), F32),
        compiler_params=_cp("parallel"),
    )(c_all_t, dmod)


def _row_tile(rows, cols, itemsize=4, budget=1536 * 1024):
    best = None
    for t in range(8, rows + 1, 8):
        if rows % t == 0 and t * cols * itemsize <= budget:
            best = t
    return rows if best is None else best


def sum_lead(parts, name, out_dtype=F32):
    n, R, C = parts.shape
    tr = _row_tile(R, C * n)

    def body(p_ref, o_ref):
        acc = p_ref[0].astype(F32)
        for k in range(1, n):
            acc = acc + p_ref[k].astype(F32)
        o_ref[...] = acc.astype(out_dtype)

    return pl.pallas_call(
        body, name=name, grid=(R // tr,),
        in_specs=[pl.BlockSpec((n, tr, C), lambda i: (0, i, 0))],
        out_specs=pl.BlockSpec((tr, C), lambda i: (i, 0)),
        out_shape=jax.ShapeDtypeStruct((R, C), out_dtype),
        compiler_params=_cp("parallel"),
    )(parts)


def adamw(w, g, m, v, name, emit_g=False):
    R, C = w.shape
    tr = _row_tile(R, C * 8, budget=16 * 1024 * 1024)

    def body(w_ref, g_ref, m_ref, v_ref, d_ref, mo_ref, vo_ref, *go_ref):
        gv = g_ref[...]
        if emit_g:
            go_ref[0][...] = gv
        mn = ADAM_B1 * m_ref[...] + (1.0 - ADAM_B1) * gv
        vn = ADAM_B2 * v_ref[...] + (1.0 - ADAM_B2) * (gv * gv)
        m_hat = mn / (1.0 - ADAM_B1 ** ADAM_STEP)
        v_hat = vn / (1.0 - ADAM_B2 ** ADAM_STEP)
        d_ref[...] = -ADAM_LR * (m_hat / (jnp.sqrt(v_hat) + ADAM_EPS) + ADAM_WD * w_ref[...])
        mo_ref[...] = mn
        vo_ref[...] = vn

    spec = pl.BlockSpec((tr, C), lambda i: (i, 0))
    return pl.pallas_call(
        body, name=name, grid=(R // tr,),
        in_specs=[spec] * 4, out_specs=[spec] * (4 if emit_g else 3),
        out_shape=[jax.ShapeDtypeStruct((R, C), F32)] * (4 if emit_g else 3),
        compiler_params=_cp("parallel"),
    )(w, g, m, v)


def _mesh_pos():
    return lax.axis_index("x"), lax.axis_index("y"), lax.axis_index("c")


def _other_chips(mx, my):
    return [(1 - mx, my), (mx, 1 - my), (1 - mx, 1 - my)]


def ag_small(x, name):
    R = x.shape[0]

    def body(x_ref, out_ref, send_sems, recv_sems, local_sem):
        mx, my, mc = _mesh_pos()
        me, sibling = (mx, my, mc), (mx, my, 1 - mc)
        chips = _other_chips(mx, my)

        def slot(px, py, pc):
            return out_ref.at[4 * px + 2 * py + pc]

        def copy(k, block, to, src=None):
            return pltpu.make_async_remote_copy(
                src_ref=slot(*block) if src is None else src, dst_ref=slot(*block),
                send_sem=send_sems.at[k], recv_sem=recv_sems.at[k], device_id=to, device_id_type=MESH)

        mine = pltpu.make_async_copy(x_ref, slot(*me), local_sem)
        mine.start()
        first = [copy(0, me, sibling, src=x_ref)]
        first += [copy(1 + j, me, (*chip, mc), src=x_ref) for j, chip in enumerate(chips)]
        for cp in first:
            cp.start()
        passed = [copy(4 + j, (*chip, mc), sibling) for j, chip in enumerate(chips)]
        for j, chip in enumerate(chips):
            copy(1 + j, (*chip, mc), me).wait_recv()
            passed[j].start()
        copy(0, sibling, me).wait_recv()
        for j, chip in enumerate(chips):
            copy(4 + j, (*chip, 1 - mc), me).wait_recv()
        for cp in first + passed:
            cp.wait_send()
        mine.wait()

    return pl.pallas_call(
        body, name=name,
        out_shape=jax.ShapeDtypeStruct((N_DEV, R, 128), F32),
        in_specs=[pl.BlockSpec(memory_space=pltpu.VMEM)],
        out_specs=pl.BlockSpec(memory_space=pltpu.VMEM),
        scratch_shapes=[pltpu.SemaphoreType.DMA((7,)), pltpu.SemaphoreType.DMA((7,)), pltpu.SemaphoreType.DMA],
        compiler_params=pltpu.CompilerParams(vmem_limit_bytes=VMEM_LIMIT),
    )(x)


BIG = (("ffn1_w_gu", "col", D, PW), ("ffn1_w_down", "row", FF, D), ("w_in", "col", D, PW),
       ("w_att_o", "col", 512, D), ("w_rec_o", "row", D, D), ("w_out", "row", D, D),
       ("ffn2_w_gu", "col", D, PW), ("ffn2_w_down", "row", FF, D))
NBIG = len(BIG)


def _shard_shape(kind, R, C):
    return (R, C // 4) if kind == "col" else (R // 4, C)


def _region(ref, kind, R, C, q, half, t, tr):
    sr, sc = _shard_shape(kind, R, C)
    if kind == "col":
        return ref.at[pl.ds(pl.multiple_of(half * (R // 2) + t * tr, 16), tr), pl.ds(q * sc, sc)]
    return ref.at[pl.ds(pl.multiple_of(q * sr + t * tr, 16), tr), pl.ds(half * (C // 2), C // 2)]


def ag_local(w, kind, R, C, p_arr, name, after=()):
    sr, sc = _shard_shape(kind, R, C)
    tr = _row_tile(sr, sc, budget=2 * 1024 * 1024)
    nt = sr // tr
    after = list(after)

    def body(p_ref, w_ref, *rest):
        rest[-1][...] = w_ref[...].astype(BF16)

    if kind == "col":
        o_spec = pl.BlockSpec((tr, sc), lambda i, p: (i, p[0]))
    else:
        o_spec = pl.BlockSpec((tr, sc), lambda i, p: (p[0] * nt + i, 0))
    return pl.pallas_call(
        body, name=name,
        grid_spec=pltpu.PrefetchScalarGridSpec(
            num_scalar_prefetch=1, grid=(nt,),
            in_specs=[pl.BlockSpec((tr, sc), lambda i, p: (i, 0))] + [ANY] * len(after), out_specs=o_spec),
        out_shape=jax.ShapeDtypeStruct((R, C), BF16),
        compiler_params=_cp("parallel"),
    )(p_arr, w, *after)


HBM_SPEC = pl.BlockSpec(memory_space=pltpu.HBM)
SEM_SPEC = pl.BlockSpec(memory_space=pltpu.SEMAPHORE)


def _ag_sems(geoms):
    return sum(6 if both else 3 for (_, _, _, both) in geoms)


def _ag_copies(fulls, geoms, ssem, rsem, mx, my, mc, q, h):
    chips = _other_chips(mx, my)
    out, base = [], 0
    for w, (kind, R, C, both) in enumerate(geoms):
        sr, sc = _shard_shape(kind, R, C)
        hr = sr // 2 if kind == "col" else sr
        reg = _region(fulls[w], kind, R, C, q, h, 0, hr)
        out.append([pltpu.make_async_remote_copy(
            src_ref=reg, dst_ref=reg, send_sem=ssem.at[base + 3 * t + k], recv_sem=rsem.at[base + 3 * t + k],
            device_id=(*chips[k], mc if t == 0 else 1 - mc), device_id_type=MESH)
            for t in range(2 if both else 1) for k in range(3)])
        base += 6 if both else 3
    return out


def ag_start(fulls, geoms, after, name):
    n = len(fulls)
    after = list(after)
    m = len(after)

    def body(*refs):
        ssem, rsem = refs[n + m:n + m + 2]
        outs, token = refs[n + m + 2:2 * n + m + 2], refs[2 * n + m + 2]
        mx, my, mc = _mesh_pos()
        p = 2 * mx + my
        col = [w for w, g in enumerate(geoms) if g[0] == "col"]
        row = [w for w, g in enumerate(geoms) if g[0] == "row"]
        for q in range(4):
            @pl.when(p == q)
            def _(q=q):
                cps = _ag_copies(outs, geoms, ssem, rsem, mx, my, mc, q, mc)
                for w in col:
                    for cp in cps[w]:
                        cp.start()
        for h in range(2):
            @pl.when(mc == h)
            def _(h=h):
                cps = _ag_copies(outs, geoms, ssem, rsem, mx, my, mc, p, h)
                for w in row:
                    for cp in cps[w]:
                        cp.start()
        token[...] = jnp.zeros_like(token)

    res = pl.pallas_call(
        body, name=name,
        out_shape=[pltpu.SemaphoreType.DMA((_ag_sems(geoms),)), pltpu.SemaphoreType.DMA((_ag_sems(geoms),))]
        + [pltpu.HBM(a.shape, a.dtype) for a in fulls] + [jax.ShapeDtypeStruct((8, 128), F32)],
        in_specs=[HBM_SPEC] * n + [ANY] * m,
        out_specs=[SEM_SPEC, SEM_SPEC] + [HBM_SPEC] * n + [pl.BlockSpec(memory_space=pltpu.VMEM)],
        input_output_aliases={w: 2 + w for w in range(n)},
        compiler_params=pltpu.CompilerParams(has_side_effects=pltpu.SideEffectType.DATAFLOW_SIDE_EFFECTING),
    )(*[pltpu.with_memory_space_constraint(a, pltpu.HBM) for a in fulls], *after)
    return res[0], res[1], list(res[2:2 + n]), res[2 + n]


def ag_wait(fulls, geoms, ssem, rsem, after, name):
    n = len(fulls)
    after = list(after) if isinstance(after, (list, tuple)) else [after]

    def body(*refs):
        ins, ssem_ref, rsem_ref = refs[:n], refs[n], refs[n + 1]
        mx, my, mc = _mesh_pos()
        for cps in _ag_copies(ins, geoms, ssem_ref, rsem_ref, mx, my, mc, 0, 0):
            for cp in cps:
                cp.wait_send()
                cp.wait_recv()

    return list(pl.pallas_call(
        body, name=name,
        out_shape=[pltpu.HBM(a.shape, a.dtype) for a in fulls],
        in_specs=[HBM_SPEC] * n + [SEM_SPEC, SEM_SPEC] + [ANY] * len(after),
        out_specs=[HBM_SPEC] * n,
        input_output_aliases={w: w for w in range(n)},
        compiler_params=pltpu.CompilerParams(has_side_effects=pltpu.SideEffectType.DATAFLOW_SIDE_EFFECTING),
    )(*fulls, ssem, rsem, *after))


def ag_forward(full, kind, R, C, name):
    sr, sc = _shard_shape(kind, R, C)
    hr, hc = (sr // 2, sc) if kind == "col" else (sr, sc // 2)
    tr = _row_tile(hr, hc, itemsize=2, budget=2 * 1024 * 1024)
    nt = hr // tr
    total = 3 * nt

    def body(src_ref, full_ref, stage, lsem, ssem, rsem):
        step = pl.program_id(0) * nt + pl.program_id(1)
        par = step % 2
        mx, my, mc = _mesh_pos()

        def load(s, q, h, t):
            return pltpu.make_async_copy(_region(src_ref, kind, R, C, q, h, t, tr), stage.at[s], lsem.at[s])

        def push(s, q, h, t):
            return pltpu.make_async_remote_copy(src_ref=stage.at[s], dst_ref=_region(full_ref, kind, R, C, q, h, t, tr),
                                                send_sem=ssem.at[s], recv_sem=rsem, device_id=(mx, my, 1 - mc),
                                                device_id_type=MESH)

        def for_tile(stp, fn):
            q_k = _partner_chip(stp // nt, 2 * mx + my)
            if kind == "col":
                for q in range(4):
                    @pl.when(q_k == q)
                    def _(q=q):
                        fn(q, mc, stp % nt)
            else:
                for h in range(2):
                    @pl.when(mc == h)
                    def _(h=h):
                        fn(q_k, h, stp % nt)

        @pl.when(step == 0)
        def _():
            for_tile(step, lambda q, h, t: load(0, q, h, t).start())

        load(par, 0, 0, 0).wait()
        for_tile(step, lambda q, h, t: push(par, q, h, t).start())

        @pl.when(step + 1 < total)
        def _():
            @pl.when(step >= 1)
            def _():
                push(1 - par, 0, 0, 0).wait_send()
            for_tile(step + 1, lambda q, h, t: load(1 - par, q, h, t).start())

        @pl.when(step == total - 1)
        def _():
            push(par, 0, 0, 0).wait_send()
            push(1 - par, 0, 0, 0).wait_send()
            three = full_ref.at[pl.ds(0, hr), pl.ds(0, 3 * hc)] if kind == "col" else full_ref.at[pl.ds(0, 3 * hr), pl.ds(0, hc)]
            pltpu.make_async_remote_copy(src_ref=three, dst_ref=three, send_sem=ssem.at[0], recv_sem=rsem,
                                         device_id=(mx, my, 1 - mc), device_id_type=MESH).wait_recv()

    return pl.pallas_call(
        body, name=name, grid=(3, nt),
        in_specs=[ANY], out_specs=ANY,
        out_shape=jax.ShapeDtypeStruct((R, C), BF16),
        scratch_shapes=[pltpu.VMEM((2, tr, hc), BF16), pltpu.SemaphoreType.DMA((2,)), pltpu.SemaphoreType.DMA((2,)),
                        pltpu.SemaphoreType.DMA],
        input_output_aliases={0: 0},
        compiler_params=_cp("arbitrary", "arbitrary"),
    )(full)


def _half_shape(kind, R, C):
    return (R // 2, C) if kind == "col" else (R, C // 2)


def _piece_shape(kind, R, C):
    return (R // 2, C // 4) if kind == "col" else (R // 4, C // 2)


def pair_push(g, kind, c_arr, name):
    R, C = g.shape
    hr, hc = _half_shape(kind, R, C)
    tr = _row_tile(hr, hc, itemsize=2, budget=2 * 1024 * 1024)
    nt = hr // tr

    def body(c_ref, g_ref, out_ref, stage, ssem, rsem):
        i = pl.program_id(0)
        slot = i % 2
        mx, my, mc = _mesh_pos()

        def push(s, t):
            return pltpu.make_async_remote_copy(
                src_ref=stage.at[s], dst_ref=out_ref.at[pl.ds(pl.multiple_of(t * tr, 16), tr)],
                send_sem=ssem.at[s], recv_sem=rsem, device_id=(mx, my, 1 - mc), device_id_type=MESH)

        @pl.when(i >= 2)
        def _():
            push(slot, 0).wait_send()

        stage[slot] = g_ref[...]
        push(slot, i).start()

        @pl.when(i == nt - 1)
        def _():
            push(slot, 0).wait_send()
            if nt >= 2:
                push(1 - slot, 0).wait_send()
            pltpu.make_async_remote_copy(src_ref=out_ref, dst_ref=out_ref, send_sem=ssem.at[0], recv_sem=rsem,
                                         device_id=(mx, my, 1 - mc), device_id_type=MESH).wait_recv()

    if kind == "col":
        g_spec = pl.BlockSpec((tr, hc), lambda i, c: ((1 - c[0]) * nt + i, 0))
    else:
        g_spec = pl.BlockSpec((tr, hc), lambda i, c: (i, 1 - c[0]))
    return pl.pallas_call(
        body, name=name,
        grid_spec=pltpu.PrefetchScalarGridSpec(
            num_scalar_prefetch=1, grid=(nt,), in_specs=[g_spec], out_specs=ANY,
            scratch_shapes=[pltpu.VMEM((2, tr, hc), BF16), pltpu.SemaphoreType.DMA((2,)), pltpu.SemaphoreType.DMA]),
        out_shape=jax.ShapeDtypeStruct((hr, hc), BF16),
        compiler_params=_cp("arbitrary"),
    )(c_arr, g)


def _partner_chip(k, p):
    return p ^ jnp.where(k == 0, 2, jnp.where(k == 1, 1, jnp.where(k == 2, 3, 0)))


def pair_add(g, got, kind, cp_arr, name):
    R, C = g.shape
    pr, pc = _piece_shape(kind, R, C)
    tr = _row_tile(pr, pc, itemsize=2, budget=2 * 1024 * 1024)
    nt = pr // tr

    def body(cp_ref, g_ref, got_ref, ps_ref, rb_ref):
        tile = (g_ref[...].astype(F32) + got_ref[...].astype(F32)).astype(BF16)
        ps_ref[...] = tile

        @pl.when(pl.program_id(1) == cp_ref[1])
        def _():
            rb_ref[...] = tile

    if kind == "col":
        g_spec = pl.BlockSpec((tr, pc), lambda i, q, cp: (cp[0] * nt + i, q))
        got_spec = pl.BlockSpec((tr, pc), lambda i, q, cp: (i, q))
    else:
        g_spec = pl.BlockSpec((tr, pc), lambda i, q, cp: (q * nt + i, cp[0]))
        got_spec = pl.BlockSpec((tr, pc), lambda i, q, cp: (q * nt + i, 0))
    return pl.pallas_call(
        body, name=name,
        grid_spec=pltpu.PrefetchScalarGridSpec(
            num_scalar_prefetch=1, grid=(nt, 4), in_specs=[g_spec, got_spec],
            out_specs=[pl.BlockSpec((None, tr, pc), lambda i, q, cp: (q, i, 0)),
                       pl.BlockSpec((None, tr, pc), lambda i, q, cp: (cp[1], i, 0))]),
        out_shape=[jax.ShapeDtypeStruct((4, pr, pc), BF16)] * 2,
        compiler_params=_cp("arbitrary", "arbitrary"),
    )(cp_arr, g, got)


def _rs_copies(ps, rb, ssem, rsem, mx, my, mc):
    p = 2 * mx + my
    out = []
    for w in range(len(ps)):
        for k, chip in enumerate(_other_chips(mx, my)):
            out.append(pltpu.make_async_remote_copy(
                src_ref=ps[w].at[2 * chip[0] + chip[1]], dst_ref=rb[w].at[p], send_sem=ssem.at[3 * w + k],
                recv_sem=rsem.at[3 * w + k], device_id=(*chip, mc), device_id_type=MESH))
    return out


def rs_start(ps, rb, after, name):
    n = len(ps)
    after = list(after)
    m = len(after)

    def body(*refs):
        ssem, rsem = refs[2 * n + m:2 * n + m + 2]
        ps_o = refs[2 * n + m + 2:3 * n + m + 2]
        rb_o = refs[3 * n + m + 2:4 * n + m + 2]
        token = refs[4 * n + m + 2]
        for cp in _rs_copies(ps_o, rb_o, ssem, rsem, *_mesh_pos()):
            cp.start()
        token[...] = jnp.zeros_like(token)

    both = list(ps) + list(rb)
    res = pl.pallas_call(
        body, name=name,
        out_shape=[pltpu.SemaphoreType.DMA((3 * n,)), pltpu.SemaphoreType.DMA((3 * n,))]
        + [pltpu.HBM(a.shape, a.dtype) for a in both] + [jax.ShapeDtypeStruct((8, 128), F32)],
        in_specs=[HBM_SPEC] * (2 * n) + [ANY] * m,
        out_specs=[SEM_SPEC, SEM_SPEC] + [HBM_SPEC] * (2 * n) + [pl.BlockSpec(memory_space=pltpu.VMEM)],
        input_output_aliases={w: 2 + w for w in range(2 * n)},
        compiler_params=pltpu.CompilerParams(has_side_effects=pltpu.SideEffectType.DATAFLOW_SIDE_EFFECTING),
    )(*[pltpu.with_memory_space_constraint(a, pltpu.HBM) for a in both], *after)
    return res[0], res[1], list(res[2:2 + n]), list(res[2 + n:2 + 2 * n]), res[2 + 2 * n]


def rs_wait(ps, rb, ssem, rsem, after, name):
    n = len(ps)
    after = list(after)
    m = len(after)

    def body(*refs):
        ps_i, rb_i = refs[:n], refs[n:2 * n]
        ssem_ref, rsem_ref = refs[2 * n], refs[2 * n + 1]
        for cp in _rs_copies(ps_i, rb_i, ssem_ref, rsem_ref, *_mesh_pos()):
            cp.wait_send()
            cp.wait_recv()

    both = list(ps) + list(rb)
    res = pl.pallas_call(
        body, name=name,
        out_shape=[pltpu.HBM(a.shape, a.dtype) for a in both],
        in_specs=[HBM_SPEC] * (2 * n) + [SEM_SPEC, SEM_SPEC] + [ANY] * m,
        out_specs=[HBM_SPEC] * (2 * n),
        input_output_aliases={w: w for w in range(2 * n)},
        compiler_params=pltpu.CompilerParams(has_side_effects=pltpu.SideEffectType.DATAFLOW_SIDE_EFFECTING),
    )(*both, ssem, rsem, *after)
    return list(res[n:])


def _slot_copies(blk, ssem, rsem):
    mx, my, mc = _mesh_pos()
    mine = blk.at[4 * mx + 2 * my + mc]
    peers = [(mx, my, 1 - mc)] + [(*chip, mc) for chip in _other_chips(mx, my)] \
        + [(*chip, 1 - mc) for chip in _other_chips(mx, my)]
    return [pltpu.make_async_remote_copy(src_ref=mine, dst_ref=mine, send_sem=ssem.at[k], recv_sem=rsem.at[k],
                                         device_id=peer, device_id_type=MESH) for k, peer in enumerate(peers)]


def slot_start(blk, name):
    def body(_, ssem, rsem, out, token):
        for cp in _slot_copies(out, ssem, rsem):
            cp.start()
        token[...] = jnp.zeros_like(token)

    return pl.pallas_call(
        body, name=name,
        out_shape=[pltpu.SemaphoreType.DMA((7,)), pltpu.SemaphoreType.DMA((7,)), pltpu.HBM(blk.shape, blk.dtype),
                   jax.ShapeDtypeStruct((8, 128), F32)],
        in_specs=[HBM_SPEC],
        out_specs=[SEM_SPEC, SEM_SPEC, HBM_SPEC, pl.BlockSpec(memory_space=pltpu.VMEM)],
        input_output_aliases={0: 2},
        compiler_params=pltpu.CompilerParams(has_side_effects=pltpu.SideEffectType.DATAFLOW_SIDE_EFFECTING),
    )(pltpu.with_memory_space_constraint(blk, pltpu.HBM))


def slot_wait(blk, ssem, rsem, after, name):
    after = list(after)

    def body(blk_ref, ssem_ref, rsem_ref, *_):
        for cp in _slot_copies(blk_ref, ssem_ref, rsem_ref):
            cp.wait_send()
            cp.wait_recv()

    return pl.pallas_call(
        body, name=name,
        out_shape=pltpu.HBM(blk.shape, blk.dtype),
        in_specs=[HBM_SPEC, SEM_SPEC, SEM_SPEC] + [ANY] * len(after),
        out_specs=HBM_SPEC,
        input_output_aliases={0: 0},
        compiler_params=pltpu.CompilerParams(has_side_effects=pltpu.SideEffectType.DATAFLOW_SIDE_EFFECTING),
    )(blk, ssem, rsem, *after)


def sum_share(parts, kind, R, C, name):
    _, pr, pc = parts.shape
    sr, sc = _shard_shape(kind, R, C)
    tr = _row_tile(pr, pc * 4, budget=8 * 1024 * 1024)
    nt = pr // tr

    def body(p_ref, fin_ref, stage, lsem, ssem, rsem):
        i = pl.program_id(0)
        slot = i % 2
        mx, my, mc = _mesh_pos()

        def region(h, t):
            r0 = pl.multiple_of(t * tr, 8)
            if kind == "col":
                return fin_ref.at[pl.ds(pl.multiple_of(h * pr + r0, 8), tr)]
            return fin_ref.at[pl.ds(r0, tr), pl.ds(h * pc, pc)]

        def copies(s, h, t):
            return (pltpu.make_async_copy(stage.at[s], region(h, t), lsem.at[s]),
                    pltpu.make_async_remote_copy(src_ref=stage.at[s], dst_ref=region(h, t), send_sem=ssem.at[s],
                                                 recv_sem=rsem, device_id=(mx, my, 1 - mc), device_id_type=MESH))

        def wait_sent(s):
            loc, rem = copies(s, 0, 0)
            loc.wait()
            rem.wait_send()

        @pl.when(i >= 2)
        def _():
            wait_sent(slot)

        acc = p_ref[0].astype(F32)
        for k in range(1, 4):
            acc = acc + p_ref[k].astype(F32)
        stage[slot] = acc
        if kind == "col":
            for cp in copies(slot, mc, i):
                cp.start()
        else:
            for h in range(2):
                @pl.when(mc == h)
                def _(h=h):
                    for cp in copies(slot, h, i):
                        cp.start()

        @pl.when(i == nt - 1)
        def _():
            wait_sent(slot)
            if nt >= 2:
                wait_sent(1 - slot)
            half = fin_ref.at[pl.ds(0, pr), pl.ds(0, pc)]
            pltpu.make_async_remote_copy(src_ref=half, dst_ref=half, send_sem=ssem.at[0], recv_sem=rsem,
                                         device_id=(mx, my, 1 - mc), device_id_type=MESH).wait_recv()

    return pl.pallas_call(
        body, name=name, grid=(nt,),
        in_specs=[pl.BlockSpec((4, tr, pc), lambda i: (0, i, 0))],
        out_specs=ANY,
        out_shape=jax.ShapeDtypeStruct((sr, sc), F32),
        scratch_shapes=[pltpu.VMEM((2, tr, pc), F32), pltpu.SemaphoreType.DMA((2,)), pltpu.SemaphoreType.DMA((2,)),
                        pltpu.SemaphoreType.DMA],
        compiler_params=_cp("arbitrary"),
    )(parts)


def _pack(parts, rows):
    flat = []
    for a in parts:
        a = jnp.ravel(a).astype(F32)
        flat.append(jnp.pad(a, (0, (-a.shape[0]) % 128)))
    v = jnp.concatenate(flat)
    return jnp.pad(v, (0, rows * 128 - v.shape[0])).reshape(rows, 128)


def _unpack(block, shapes):
    lead = block.shape[:-2]
    v = block.reshape(lead + (-1,))
    out, off = [], 0
    for shp in shapes:
        n = int(np.prod(shp))
        out.append(v[..., off:off + n].reshape(lead + tuple(shp)))
        off += n + (-n) % 128
    return out


def _block_diag4(w):
    w4 = w.reshape(4, 4, 64, 64)
    eye = jnp.eye(4, dtype=w.dtype)
    return (w4[:, :, :, None, :] * eye[None, :, None, :, None]).reshape(4, 256, 256)


def _diag_blocks(bd):
    b5 = bd.reshape(4, 4, 64, 4, 64)
    return jnp.stack([b5[:, i, :, i, :] for i in range(4)], axis=1).reshape(16, 64, 64)


def _bias_window(rel_bias):
    m = (np.arange(768) + 127) % 768 - 127
    w = rel_bias[:, np.clip(512 - m, -128, 128) + 128]
    win = jnp.tile(w, (1, 128))[:, :128 * 767].reshape(8, 128, 767)[:, :, :WIN]
    qh = np.arange(128)[:, None] // CHUNK
    kc = np.arange(WIN)[None, :] // CHUNK
    valid = (kc >= qh) & (kc <= qh + 8)
    return jnp.where(jnp.asarray(valid)[None], win, NEG)


SMALL = ("b_ada", "norm_pre", "norm_post", "rel_bias", "conv_w", "conv_b", "lru_wa", "lru_ba", "lru_wx",
         "lru_bx", "lru_lambda")
WEIGHTS = ("w_ada", "b_ada", "norm_pre", "norm_post", "ffn1_w_gu", "ffn1_w_down", "w_in", "rel_bias", "conv_w",
           "conv_b", "lru_wa", "lru_ba", "lru_wx", "lru_bx", "lru_lambda", "w_att_o", "w_rec_o", "w_out",
           "ffn2_w_gu", "ffn2_w_down")


def kernel(x, c, w_ada, b_ada, norm_pre, norm_post, ffn1_w_gu, ffn1_w_down, w_in, rel_bias, conv_w, conv_b, lru_wa, lru_ba, lru_wx, lru_bx, lru_lambda, w_att_o, w_rec_o, w_out, ffn2_w_gu, ffn2_w_down, loss_target, m_w_ada, m_b_ada, m_norm_pre, m_norm_post, m_ffn1_w_gu, m_ffn1_w_down, m_w_in, m_rel_bias, m_conv_w, m_conv_b, m_lru_wa, m_lru_ba, m_lru_wx, m_lru_bx, m_lru_lambda, m_w_att_o, m_w_rec_o, m_w_out, m_ffn2_w_gu, m_ffn2_w_down, v_w_ada, v_b_ada, v_norm_pre, v_norm_post, v_ffn1_w_gu, v_ffn1_w_down, v_w_in, v_rel_bias, v_conv_w, v_conv_b, v_lru_wa, v_lru_ba, v_lru_wx, v_lru_bx, v_lru_lambda, v_w_att_o, v_w_rec_o, v_w_out, v_ffn2_w_gu, v_ffn2_w_down):
    W = dict(w_ada=w_ada, b_ada=b_ada, norm_pre=norm_pre, norm_post=norm_post, ffn1_w_gu=ffn1_w_gu,
             ffn1_w_down=ffn1_w_down, w_in=w_in, rel_bias=rel_bias, conv_w=conv_w, conv_b=conv_b, lru_wa=lru_wa,
             lru_ba=lru_ba, lru_wx=lru_wx, lru_bx=lru_bx, lru_lambda=lru_lambda, w_att_o=w_att_o, w_rec_o=w_rec_o,
             w_out=w_out, ffn2_w_gu=ffn2_w_gu, ffn2_w_down=ffn2_w_down)
    M = dict(w_ada=m_w_ada, b_ada=m_b_ada, norm_pre=m_norm_pre, norm_post=m_norm_post, ffn1_w_gu=m_ffn1_w_gu,
             ffn1_w_down=m_ffn1_w_down, w_in=m_w_in, rel_bias=m_rel_bias, conv_w=m_conv_w, conv_b=m_conv_b,
             lru_wa=m_lru_wa, lru_ba=m_lru_ba, lru_wx=m_lru_wx, lru_bx=m_lru_bx, lru_lambda=m_lru_lambda,
             w_att_o=m_w_att_o, w_rec_o=m_w_rec_o, w_out=m_w_out, ffn2_w_gu=m_ffn2_w_gu, ffn2_w_down=m_ffn2_w_down)
    V = dict(w_ada=v_w_ada, b_ada=v_b_ada, norm_pre=v_norm_pre, norm_post=v_norm_post, ffn1_w_gu=v_ffn1_w_gu,
             ffn1_w_down=v_ffn1_w_down, w_in=v_w_in, rel_bias=v_rel_bias, conv_w=v_conv_w, conv_b=v_conv_b,
             lru_wa=v_lru_wa, lru_ba=v_lru_ba, lru_wx=v_lru_wx, lru_bx=v_lru_bx, lru_lambda=v_lru_lambda,
             w_att_o=v_w_att_o, w_rec_o=v_w_rec_o, w_out=v_w_out, ffn2_w_gu=v_ffn2_w_gu, ffn2_w_down=v_ffn2_w_down)
    mx, my, mc = _mesh_pos()
    p = 2 * mx + my
    e = 4 * mx + 2 * my + mc
    xs = x[0]

    c_arr = jnp.reshape(mc, (1,)).astype(jnp.int32)
    cp_arr = jnp.stack([mc, p]).astype(jnp.int32)
    p_arr = jnp.reshape(p, (1,)).astype(jnp.int32)
    direct = ("w_att_o", "w_rec_o", "w_out", "ffn2_w_gu", "ffn2_w_down")
    geoms = [(kind, R, C, n in direct) for (n, kind, R, C) in BIG]
    names = [b[0] for b in BIG]
    placed = [ag_local(W[n][0], kind, R, C, p_arr, "ag_local_" + n) for (n, kind, R, C) in BIG[:2]]

    def arrived(fly, lo, hi, ssem, rsem, after, tag):
        done = ag_wait(fly, geoms[lo:hi], ssem, rsem, after, "ag_wait_" + tag)
        return [a if both else ag_forward(a, kind, R, C, "ag_forward_" + n)
                for a, (kind, R, C, both), n in zip(done, geoms[lo:hi], names[lo:hi])]

    g1 = ag_small(_pack([c, norm_pre, norm_post, conv_w], 32), "ag_small_params")
    c_all, npre4, npost4, cw4 = _unpack(g1, [(D,), (3, 256), (3, 256), (4, 256)])
    chipwise = lambda a: jnp.moveaxis(a[0::2], 0, 1).reshape(a.shape[1], D)
    npre, npost, conv_full = chipwise(npre4), chipwise(npost4), chipwise(cw4)

    b_cols = lax.dynamic_slice(b_ada, (0, p * 2304), (1, 2304))
    mod_cols = ada_fwd(c_all, w_ada[0], b_cols, "ada_fwd")
    g2 = ag_small(mod_cols.reshape(144, 128), "ag_mod")
    mod_all = jnp.moveaxis(g2[0::2].reshape(4, 8, 2304), 0, 1).reshape(8, 9 * D)
    mod = lax.dynamic_index_in_dim(mod_all, e, 0, keepdims=False).reshape(3, 3, D)
    zeros3 = jnp.zeros((3, D), F32)
    vecs = [jnp.concatenate([npre[k:k + 1], npost[k:k + 1], mod[k], zeros3], axis=0) for k in range(3)]

    gu_s, gu_r, gu_fly, tok_gu = ag_start(placed[:1], geoms[:1], [g2], "ag_start_ffn1_gu")
    dn_s, dn_r, dn_fly, tok0 = ag_start(placed[1:2], geoms[1:2], [tok_gu], "ag_start_ffn1_down")
    placed += [ag_local(W[n][0], kind, R, C, p_arr, "ag_local_" + n, after=[tok0]) for (n, kind, R, C) in BIG[2:]]
    f1_gu, = arrived(gu_fly, 0, 1, gu_s, gu_r, placed[2:], "ffn1_gu")
    f1_dn, = arrived(dn_fly, 1, 2, dn_s, dn_r, f1_gu, "ffn1_down")
    mix_s, mix_r, mix_fly, tok1 = ag_start(placed[2:6], geoms[2:6], [f1_gu, f1_dn], "ag_start_mixer")
    ffn_s, ffn_r, ffn_fly, tok2 = ag_start(placed[6:], geoms[6:], [tok1], "ag_start_ffn2")
    wa_bd = _block_diag4(lru_wa[0]).astype(BF16)
    wx_bd = _block_diag4(lru_wx[0]).astype(BF16)
    pvec = jnp.concatenate([conv_full, conv_b, lru_ba, lru_bx, lru_lambda], axis=0)
    bias = _bias_window(rel_bias[0]).reshape(4, 256, WIN)

    x1, h1, g1_, u1, a1, f1 = ffn_fwd(xs, vecs[0] + tok2[0:1, 0:1], f1_gu, f1_dn, 0.5, "ffn1_fwd")
    win, wao, wro, wout = arrived(mix_fly, 2, 6, mix_s, mix_r, x1, "mixer")
    h2, qkv, rest = proj_fwd(x1, vecs[1], win, "proj_fwd")
    ao = attn_fwd(qkv, bias, "attn_fwd")
    hl, hg = lru_fwd(rest, pvec, wa_bd, wx_bd, "lru_fwd")
    x2, att, rec, mg, f2 = mix_out_fwd(x1, ao, hg, rest, vecs[1], wao, wro, wout, "mix_out_fwd")
    f2_gu, f2_dn = arrived(ffn_fly, 6, 8, ffn_s, ffn_r, x2, "ffn2")
    dy, h3, g3_, u3, a3, f3, lvec = ffn_fwd(x2, vecs[2], f2_gu, f2_dn, 0.5, "ffn2_fwd", tgt=loss_target[0])

    G, grads = {}, {}
    geo = {n: (kind, R, C) for (n, kind, R, C) in BIG}

    def reduce_begin(names, tag):
        ps, rb = [], []
        for n in names:
            got = pair_push(G[n], geo[n][0], c_arr, "rs_push_" + n)
            a, b = pair_add(G[n], got, geo[n][0], cp_arr, "rs_pair_sum_" + n)
            ps.append(a)
            rb.append(b)
        return rs_start(ps, rb, [], "rs_start_" + tag)

    def reduce_end(names, flight, after, tag):
        ssem, rsem, ps, rb, _ = flight
        for a, n in zip(rs_wait(ps, rb, ssem, rsem, after, "rs_wait_" + tag), names):
            grads[n] = sum_share(a, *geo[n], "rs_sum_share_" + n)[None]

    dx2, df3, dgu3, va2 = ffn_bwd(dy, x2, f3, g3_, u3, vecs[2], f2_gu, f2_dn, 0.5, "ffn2_bwd")
    G["ffn2_w_gu"] = mm_tn(h3, dgu3, "dw_ffn2_gu", D, 1408, 2048, a_resident=True)
    G["ffn2_w_down"] = mm_tn(a3, df3, "dw_ffn2_down", 1408, D, 2048)
    fly_ffn2 = reduce_begin(("ffn2_w_gu", "ffn2_w_down"), "ffn2")
    vec1 = vecs[1] + fly_ffn2[4][0:1, 0:1]
    df2, d_att, d_rec, dao, dhl, d3, va_out = mix_out_bwd(dx2, f2, att, rec, rest, hl, vec1, wao, wro, wout,
                                                          "mix_out_bwd")
    G["w_out"] = mm_tn(mg, df2, "dw_out", D, D, 1024)
    G["w_att_o"] = mm_tn(ao, d_att, "dw_att_o", 512, D, 1024)
    G["w_rec_o"] = mm_tn(hg, d_rec, "dw_rec_o", D, D, 1024)
    dq, db, dkv = attn_bwd(qkv, dao, bias, "attn_bwd")
    dxr, v_lru, dwa_bd, dwx_bd = lru_bwd(dhl, hl, rest, pvec, wa_bd, wx_bd, "lru_bwd")
    lru_w = jnp.concatenate([_diag_blocks(dwa_bd), _diag_blocks(dwx_bd)]).reshape(1, 1024, 128)
    lru_slots = lax.dynamic_update_slice(jnp.zeros((N_DEV, 1024, 128), F32), lru_w, (e, 0, 0))
    lw_s, lw_r, lw_fly, lw_tok = slot_start(lru_slots, "lru_w_start")
    dx1, va_in = proj_bwd(dq, dkv, dxr, d3, win, x1, dx2, vecs[1] + lw_tok[0:1, 0:1], "proj_bwd")
    G["w_in"] = dw_in(h2, dq, dkv, dxr, d3, "dw_in")
    fly_mix = reduce_begin(("w_in", "w_att_o", "w_rec_o", "w_out"), "mixer")
    vec0 = vecs[0] + fly_mix[4][0:1, 0:1]
    dx0, df1, dgu1, va0 = ffn_bwd(dx1, xs, f1, g1_, u1, vec0, f1_gu, f1_dn, 0.5, "ffn1_bwd")
    G["ffn1_w_gu"] = mm_tn(h1, dgu1, "dw_ffn1_gu", D, 1408, 2048, a_resident=True)
    G["ffn1_w_down"] = mm_tn(a1, df1, "dw_ffn1_down", 1408, D, 2048)
    fly_ffn1 = reduce_begin(("ffn1_w_gu", "ffn1_w_down"), "ffn1")
    reduce_end(("ffn2_w_gu", "ffn2_w_down"), fly_ffn2, [fly_ffn1[4]], "ffn2")
    reduce_end(("w_in", "w_att_o", "w_rec_o", "w_out"), fly_mix, [fly_ffn1[4], grads["ffn2_w_down"]], "mixer")

    va1 = va_out + va_in
    vas = (va0, va1, va2)
    dmod = jnp.stack([v[2:5] for v in vas])
    part = {"b_ada": dmod, "norm_pre": jnp.stack([v[0] for v in vas]), "norm_post": jnp.stack([v[1] for v in vas]),
            "rel_bias": bias_grad(db.reshape(8, 128, WIN), "bias_grad")[:, :257], "conv_w": v_lru[0:4], "conv_b": v_lru[4],
            "lru_ba": v_lru[5], "lru_bx": v_lru[6], "lru_lambda": v_lru[7]}
    full_shapes = {"b_ada": (9 * D,), "norm_pre": (3, D), "norm_post": (3, D), "rel_bias": (8, 257),
                   "conv_w": (4, D), "conv_b": (D,), "lru_wa": (16, 64, 64), "lru_ba": (D,),
                   "lru_wx": (16, 64, 64), "lru_bx": (D,), "lru_lambda": (D,)}
    gathered = [n for n in SMALL if n in part]
    g3 = ag_small(_pack([part[n] for n in gathered] + [lvec[0:1, 0:1]], 208), "ag_small_grads")
    summed = _unpack(sum_lead(g3, "sum_small_grads"), [full_shapes[n] for n in gathered] + [(1,)])
    red = dict(zip(gathered, summed[:-1]))
    loss = summed[-1][0]
    lru_all = slot_wait(lw_fly, lw_s, lw_r, [dx0], "lru_w_wait")
    red["lru_wa"], red["lru_wx"] = sum_lead(lru_all, "sum_lru_w").reshape(2, 16, 64, 64)
    cols = lambda a: lax.dynamic_slice(a, (0, p * 256), (a.shape[0], 256))
    grads.update({"b_ada": red["b_ada"][None], "norm_pre": cols(red["norm_pre"])[None],
                  "norm_post": cols(red["norm_post"])[None], "rel_bias": red["rel_bias"][None],
                  "conv_w": cols(red["conv_w"])[None], "conv_b": red["conv_b"][None], "lru_wa": red["lru_wa"][None],
                  "lru_ba": red["lru_ba"][None], "lru_wx": red["lru_wx"][None], "lru_bx": red["lru_bx"][None],
                  "lru_lambda": red["lru_lambda"][None]})

    dmod_all = g3[:, :72].reshape(8, 9 * D)
    dmod_cols = jnp.pad(lax.dynamic_slice(dmod_all, (0, p * 2304), (8, 2304)), ((0, 120), (0, 0)))
    c_all_t = jnp.pad(c_all.T, ((0, 0), (0, 120)))
    grads["w_ada"] = ada_bwd(c_all_t, dmod_cols, "ada_bwd")[None]

    delta, new_m, new_v = {}, {}, {}

    def update(n):
        shp = W[n].shape
        res = adamw(W[n][0], grads[n][0], M[n][0], V[n][0], "adamw_" + n, emit_g=n in geo)
        delta[n], new_m[n], new_v[n] = [a.reshape(shp) for a in res[:3]]
        if n in geo:
            grads[n] = res[3].reshape(shp)

    for n in ("w_ada", "ffn2_w_gu", "ffn2_w_down", "w_in", "w_att_o", "w_rec_o", "w_out"):
        update(n)
    packed = [_pack([src[n] for n in SMALL], 1168) for src in (W, grads, M, V)]
    outs = adamw(*packed, "adamw_small")
    for dst, blk in zip((delta, new_m, new_v), outs):
        for n, a in zip(SMALL, _unpack(blk, [W[n].shape for n in SMALL])):
            dst[n] = a
    reduce_end(("ffn1_w_gu", "ffn1_w_down"), fly_ffn1,
               [outs[0], delta["w_ada"], delta["ffn2_w_gu"], delta["ffn2_w_down"], delta["w_in"], delta["w_out"]], "ffn1")
    for n in ("ffn1_w_gu", "ffn1_w_down"):
        update(n)

    return (loss, dx0[None], *[grads[n] for n in WEIGHTS], *[delta[n] for n in WEIGHTS],
            *[new_m[n] for n in WEIGHTS], *[new_v[n] for n in WEIGHTS])
```

```python
import numpy as np
import jax
import jax.numpy as jnp
from jax import lax
from jax.experimental import pallas as pl
from jax.experimental.pallas import tpu as pltpu

F32 = jnp.float32
BF16 = jnp.bfloat16

D = 1024
FF = 2816
PW = 5632
HP = 128
CHUNK = 64
WIN = 640
TQ = 512
EPS = 1e-6
NEG = -1e30
LRU_C = 8.0
N_DEV = 8
VMEM_LIMIT = 56 * 1024 * 1024

ADAM_LR, ADAM_B1, ADAM_B2, ADAM_EPS, ADAM_WD, ADAM_STEP = 0.001, 0.9, 0.999, 1e-08, 0.01, 10

MESH = pl.DeviceIdType.MESH
ANY = pl.BlockSpec(memory_space=pl.ANY)


def _cp(*sem):
    return pltpu.CompilerParams(dimension_semantics=tuple(sem), vmem_limit_bytes=VMEM_LIMIT)


def _dot(a, b):
    return jnp.dot(a, b, preferred_element_type=F32)


def _dot_nt(a, b):
    return lax.dot_general(a, b, (((1,), (1,)), ((), ())), preferred_element_type=F32)


def _dot_tn(a, b):
    return lax.dot_general(a, b, (((0,), (0,)), ((), ())), preferred_element_type=F32)


def _mean(v):
    return jnp.mean(v, axis=-1, keepdims=True)


def _colsum(v):
    return jnp.sum(v, axis=0, keepdims=True)


def _sigmoid(v):
    return 0.5 * jnp.tanh(0.5 * v) + 0.5


_GK = 0.7978845608028654


def _gelu(v):
    t = jnp.tanh(_GK * (v + 0.044715 * v * v * v))
    return 0.5 * v * (1.0 + t)


def _pre_norm(xv, vec_ref):
    r = lax.rsqrt(_mean(xv * xv) + EPS)
    n = xv * r * vec_ref[0:1, :]
    return n * (1.0 + vec_ref[3:4, :]) + vec_ref[2:3, :]


def _pre_norm_bwd(dh, xv, dres, vec_ref, vacc_ref):
    r = lax.rsqrt(_mean(xv * xv) + EPS)
    xh = xv * r
    n = xh * vec_ref[0:1, :]
    vacc_ref[2:3, :] += _colsum(dh)
    vacc_ref[3:4, :] += _colsum(dh * n)
    dn = dh * (1.0 + vec_ref[3:4, :])
    vacc_ref[0:1, :] += _colsum(dn * xh)
    dxh = dn * vec_ref[0:1, :]
    return r * (dxh - xh * _mean(dxh * xh)) + dres


def _post_norm_bwd(dxo, fv, res, vec_ref, vacc_ref):
    rf = lax.rsqrt(_mean(fv * fv) + EPS)
    fh = fv * rf
    gp = vec_ref[1:2, :]
    vacc_ref[4:5, :] += _colsum(res * dxo * (fh * gp))
    dy = (res * vec_ref[4:5, :]) * dxo
    vacc_ref[1:2, :] += _colsum(dy * fh)
    dfn = dy * gp
    return rf * (dfn - fh * _mean(dfn * fh))


def _u_spec(tf):
    return pl.BlockSpec((pl.Element(D), pl.Element(tf)),
                        lambda i, j: (0, pl.multiple_of(jnp.minimum(FF + j * tf, 2 * FF - tf), 128)))


def ffn_fwd(x, vec, w_gu, w_dn, res, name, tgt=None, tm=1024, tf=512):
    S = x.shape[0]
    tm = min(tm, S)
    nt = S // tm
    nf = -(-FF // tf)
    tail = FF - tf * (nf - 1)
    halves = [pl.ds(r * (tm // 2), tm // 2) for r in range(2)]
    head = tgt is not None

    def body(*refs):
        x_ref, vec_ref, wg_ref, wu_ref, wd_ref = refs[:5]
        if head:
            t_hbm, xo_ref, h_ref, g_ref, u_ref, a_ref, f_ref, l_ref, hs, acc, lacc, ts, tsem = refs[5:]
        else:
            xo_ref, h_ref, g_ref, u_ref, a_ref, f_ref, hs, acc = refs[5:]
        i, j = pl.program_id(0), pl.program_id(1)
        if head:
            late = pltpu.make_async_copy(t_hbm.at[pl.ds(pl.multiple_of(i * tm, 8), tm)], ts, tsem)

        @pl.when(j == 0)
        def _():
            if head:
                late.start()
            h = _pre_norm(x_ref[...], vec_ref).astype(BF16)
            hs[...] = h
            h_ref[...] = h
            acc[...] = jnp.zeros_like(acc)

        def chunk(w):
            gu = [(_dot(hs[r, :], wg_ref[:, 0:w]), _dot(hs[r, :], wu_ref[:, tf - w:tf])) for r in halves]
            acts = []
            for r, (g, u) in zip(halves, gu):
                g_ref[r, 0:w] = g.astype(BF16)
                u_ref[r, 0:w] = u.astype(BF16)
                a = (g * _sigmoid(g) * u).astype(BF16)
                a_ref[r, 0:w] = a
                acts.append(a)
            for r, a in zip(halves, acts):
                acc[r, :] += _dot(a, wd_ref[0:w, :])

        @pl.when(j < nf - 1)
        def _():
            chunk(tf)

        @pl.when(j == nf - 1)
        def _():
            chunk(tail)
            f = acc[...]
            f_ref[...] = f.astype(BF16)
            y = f * lax.rsqrt(_mean(f * f) + EPS) * vec_ref[1:2, :]
            xo = x_ref[...] + (res * vec_ref[4:5, :]) * y
            if head:
                @pl.when(i == 0)
                def _():
                    lacc[...] = jnp.zeros_like(lacc)

                late.wait()
                d = xo - ts[...]
                xo_ref[...] = d * (1.0 / D)
                lacc[...] += _colsum(d * d)

                @pl.when(i == nt - 1)
                def _():
                    l_ref[...] = jnp.broadcast_to(0.5 * jnp.sum(lacc[...]) * (1.0 / D), (8, 128))
            else:
                xo_ref[...] = xo

    row = lambda i, j: (i, 0)
    col = lambda i, j: (i, j)
    in_specs = [pl.BlockSpec((tm, D), row), pl.BlockSpec((8, D), lambda i, j: (0, 0)),
                pl.BlockSpec((D, tf), lambda i, j: (0, j)), _u_spec(tf),
                pl.BlockSpec((tf, D), lambda i, j: (j, 0))]
    out_specs = [pl.BlockSpec((tm, D), row), pl.BlockSpec((tm, D), row), pl.BlockSpec((tm, tf), col),
                 pl.BlockSpec((tm, tf), col), pl.BlockSpec((tm, tf), col), pl.BlockSpec((tm, D), row)]
    out_shape = [jax.ShapeDtypeStruct((S, D), F32), jax.ShapeDtypeStruct((S, D), BF16),
                 jax.ShapeDtypeStruct((S, FF), BF16), jax.ShapeDtypeStruct((S, FF), BF16),
                 jax.ShapeDtypeStruct((S, FF), BF16), jax.ShapeDtypeStruct((S, D), BF16)]
    scratch = [pltpu.VMEM((tm, D), BF16), pltpu.VMEM((tm, D), F32)]
    args = [x, vec, w_gu, w_gu, w_dn]
    if head:
        in_specs.append(ANY)
        out_specs.append(pl.BlockSpec((8, 128), lambda i, j: (0, 0)))
        out_shape.append(jax.ShapeDtypeStruct((8, 128), F32))
        scratch += [pltpu.VMEM((1, D), F32), pltpu.VMEM((tm, D), F32), pltpu.SemaphoreType.DMA]
        args.append(tgt)
    return pl.pallas_call(
        body, name=name, grid=(nt, nf), in_specs=in_specs, out_specs=out_specs, out_shape=out_shape,
        scratch_shapes=scratch,
        compiler_params=_cp("arbitrary" if head else "parallel", "arbitrary"),
    )(*args)


def ffn_bwd(dxo, x, f, g, u, vec, w_gu, w_dn, res, name, tm=1024, tf=512):
    S = x.shape[0]
    tm = min(tm, S)
    nf = -(-FF // tf)
    tail = FF - tf * (nf - 1)
    halves = [pl.ds(r * (tm // 2), tm // 2) for r in range(2)]

    def body(dxo_ref, x_hbm, f_ref, g_ref, u_ref, vec_ref, wg_ref, wu_ref, wd_ref,
             dx_ref, df_ref, dgu_ref, vacc_ref, dfs, acc, xs, xsem):
        i, j = pl.program_id(0), pl.program_id(1)
        late = pltpu.make_async_copy(x_hbm.at[pl.ds(pl.multiple_of(i * tm, 8), tm)], xs, xsem)

        @pl.when((i == 0) & (j == 0))
        def _():
            vacc_ref[...] = jnp.zeros_like(vacc_ref)

        @pl.when(j == 0)
        def _():
            late.start()
            df = _post_norm_bwd(dxo_ref[...], f_ref[...].astype(F32), res, vec_ref, vacc_ref).astype(BF16)
            dfs[...] = df
            df_ref[...] = df
            acc[...] = jnp.zeros_like(acc)

        def chunk(w):
            da = [_dot_nt(dfs[h, :], wd_ref[0:w, :]) for h in halves]
            dgu = []
            for h, d in zip(halves, da):
                gv, uv = g_ref[h, 0:w].astype(F32), u_ref[h, 0:w].astype(F32)
                sg = _sigmoid(gv)
                dg = (d * uv * (sg * (1.0 + gv * (1.0 - sg)))).astype(BF16)
                du = (d * (gv * sg)).astype(BF16)
                dgu_ref[0, h, 0:w] = dg
                dgu_ref[1, h, 0:w] = du
                dgu.append((dg, du))
            for h, (dg, du) in zip(halves, dgu):
                acc[h, :] += _dot_nt(dg, wg_ref[:, 0:w]) + _dot_nt(du, wu_ref[:, tf - w:tf])

        @pl.when(j < nf - 1)
        def _():
            chunk(tf)

        @pl.when(j == nf - 1)
        def _():
            chunk(tail)
            late.wait()
            dx_ref[...] = _pre_norm_bwd(acc[...], xs[...], dxo_ref[...], vec_ref, vacc_ref)

    row = lambda i, j: (i, 0)
    col = lambda i, j: (i, j)
    return pl.pallas_call(
        body, name=name, grid=(S // tm, nf),
        in_specs=[pl.BlockSpec((tm, D), row), ANY, pl.BlockSpec((tm, D), row),
                  pl.BlockSpec((tm, tf), col), pl.BlockSpec((tm, tf), col),
                  pl.BlockSpec((8, D), lambda i, j: (0, 0)),
                  pl.BlockSpec((D, tf), lambda i, j: (0, j)), _u_spec(tf),
                  pl.BlockSpec((tf, D), lambda i, j: (j, 0))],
        out_specs=[pl.BlockSpec((tm, D), row), pl.BlockSpec((tm, D), row),
                   pl.BlockSpec((2, tm, tf), lambda i, j: (0, i, j)),
                   pl.BlockSpec((8, D), lambda i, j: (0, 0))],
        out_shape=[jax.ShapeDtypeStruct((S, D), F32), jax.ShapeDtypeStruct((S, D), BF16),
                   jax.ShapeDtypeStruct((2, S, FF), BF16), jax.ShapeDtypeStruct((8, D), F32)],
        scratch_shapes=[pltpu.VMEM((tm, D), BF16), pltpu.VMEM((tm, D), F32), pltpu.VMEM((tm, D), F32),
                        pltpu.SemaphoreType.DMA],
        compiler_params=_cp("arbitrary", "arbitrary"),
    )(dxo, x, f, g, u, vec, w_gu, w_gu, w_dn)


def mm_tn(a, b, name, tm, tn, tk, out_dtype=BF16, a_resident=False):
    S, M = a.shape
    if b.ndim == 3:
        G, _, Nf = b.shape
    else:
        G, Nf = 1, b.shape[1]
    N = G * Nf
    tk = min(tk, S)
    nbf = Nf // tn
    nk = S // tk

    def body(a_ref, b_ref, o_ref, acc):
        k = pl.program_id(2)

        @pl.when(k == 0)
        def _():
            acc[...] = jnp.zeros_like(acc)

        a_blk = a_ref[pl.ds(pl.multiple_of(k * tk, 16), tk), :] if a_resident else a_ref[...]
        acc[...] += _dot_tn(a_blk, b_ref[...])

        @pl.when(k == nk - 1)
        def _():
            o_ref[...] = acc[...].astype(out_dtype)

    if b.ndim == 3:
        b_spec = pl.BlockSpec((None, tk, tn), lambda i, j, k: (j // nbf, k, j % nbf))
    else:
        b_spec = pl.BlockSpec((tk, tn), lambda i, j, k: (k, j))
    if a_resident:
        a_spec = pl.BlockSpec((S, M), lambda i, j, k: (0, 0), pipeline_mode=pl.Buffered(1))
    else:
        a_spec = pl.BlockSpec((tk, tm), lambda i, j, k: (k, i))
    return pl.pallas_call(
        body, name=name, grid=(M // tm, N // tn, nk),
        in_specs=[a_spec, b_spec],
        out_specs=pl.BlockSpec((tm, tn), lambda i, j, k: (i, j)),
        out_shape=jax.ShapeDtypeStruct((M, N), out_dtype),
        scratch_shapes=[pltpu.VMEM((tm, tn), F32)],
        compiler_params=_cp("parallel", "parallel", "arbitrary"),
    )(a, b)


def proj_fwd(x, vec, w_in, name, tm=2048, tn=512):
    S = x.shape[0]
    tm = min(tm, S)
    nq = 1536 // tn

    def body(x_ref, vec_ref, w_ref, h_ref, qkv_ref, rest_ref, hs):
        j = pl.program_id(1)

        @pl.when(j == 0)
        def _():
            h = _pre_norm(x_ref[...], vec_ref).astype(BF16)
            hs[...] = h
            h_ref[...] = h

        r = _dot(hs[...], w_ref[...])

        @pl.when(j < nq)
        def _():
            qkv_ref[...] = r.astype(BF16)

        @pl.when(j >= nq)
        def _():
            rest_ref[...] = r.astype(BF16)

    row = lambda i, j: (i, 0)
    return pl.pallas_call(
        body, name=name, grid=(S // tm, PW // tn),
        in_specs=[pl.BlockSpec((tm, D), row), pl.BlockSpec((8, D), lambda i, j: (0, 0)),
                  pl.BlockSpec((D, tn), lambda i, j: (0, j))],
        out_specs=[pl.BlockSpec((tm, D), row),
                   pl.BlockSpec((tm, tn), lambda i, j: (i, jnp.minimum(j, nq - 1))),
                   pl.BlockSpec((tm, tn), lambda i, j: (i, jnp.maximum(j - nq, 0)))],
        out_shape=[jax.ShapeDtypeStruct((S, D), BF16), jax.ShapeDtypeStruct((S, 1536), BF16),
                   jax.ShapeDtypeStruct((S, 4096), BF16)],
        scratch_shapes=[pltpu.VMEM((tm, D), BF16)],
        compiler_params=_cp("parallel", "arbitrary"),
    )(x, vec, w_in)


def proj_bwd(dq, dkv, dxr, d3, w_in, x, dxo, vec, name, tm=2048, tk=512):
    S = x.shape[0]
    tm = min(tm, S)
    nk = PW // tk

    def body(dq_ref, dkv_ref, dxr_ref, d3_ref, w_ref, x_hbm, dxo_hbm, vec_ref, dx_ref, vacc_ref, acc, xs, dxos, sems):
        i, j = pl.program_id(0), pl.program_id(1)
        tok = pl.ds(pl.multiple_of(i * tm, 8), tm)
        late = (pltpu.make_async_copy(x_hbm.at[tok], xs, sems.at[0]),
                pltpu.make_async_copy(dxo_hbm.at[tok], dxos, sems.at[1]))

        @pl.when((i == 0) & (j == 0))
        def _():
            vacc_ref[...] = jnp.zeros_like(vacc_ref)

        @pl.when(j == 0)
        def _():
            for cp in late:
                cp.start()
            acc[...] = _dot_nt(dq_ref[...], w_ref[...])

        @pl.when((j >= 1) & (j < 3))
        def _():
            acc[...] += _dot_nt(dkv_ref[...], w_ref[...])

        @pl.when((j >= 3) & (j < 5))
        def _():
            acc[...] += _dot_nt(dxr_ref[...], w_ref[...])

        @pl.when(j >= 5)
        def _():
            acc[...] += _dot_nt(d3_ref[...], w_ref[...])

        @pl.when(j == nk - 1)
        def _():
            for cp in late:
                cp.wait()
            dx_ref[...] = _pre_norm_bwd(acc[...], xs[...], dxos[...], vec_ref, vacc_ref)

    row = lambda i, j: (i, 0)
    return pl.pallas_call(
        body, name=name, grid=(S // tm, nk),
        in_specs=[pl.BlockSpec((None, tm, tk), lambda i, j: (0, i, 0)),
                  pl.BlockSpec((None, tm, tk), lambda i, j: (jnp.clip(j - 1, 0, 1), i, 0)),
                  pl.BlockSpec((tm, tk), lambda i, j: (i, jnp.clip(j - 3, 0, 1))),
                  pl.BlockSpec((None, tm, tk), lambda i, j: (jnp.clip(j - 5, 0, 5) // 2, i, jnp.clip(j - 5, 0, 5) % 2)),
                  pl.BlockSpec((D, tk), lambda i, j: (0, j)),
                  ANY, ANY,
                  pl.BlockSpec((8, D), lambda i, j: (0, 0))],
        out_specs=[pl.BlockSpec((tm, D), row, pipeline_mode=pl.Buffered(1)),
                   pl.BlockSpec((8, D), lambda i, j: (0, 0))],
        out_shape=[jax.ShapeDtypeStruct((S, D), F32), jax.ShapeDtypeStruct((8, D), F32)],
        scratch_shapes=[pltpu.VMEM((tm, D), F32), pltpu.VMEM((tm, D), F32), pltpu.VMEM((tm, D), F32),
                        pltpu.SemaphoreType.DMA((2,))],
        compiler_params=_cp("arbitrary", "arbitrary"),
    )(dq, dkv, dxr, d3, w_in, x, dxo, vec)


def _two_heads(v, lane):
    zero = jnp.zeros((), v.dtype)
    return jnp.concatenate([jnp.where(lane < 64, v, zero), jnp.where(lane >= 64, v, zero)], axis=0)


def _attn_scores(qm, ka, bias_h, i, grp):
    s = _dot_nt(qm, ka) + bias_h
    col = lax.broadcasted_iota(jnp.int32, s.shape, 1)
    first_key = jnp.where(i == 0, 512 - 128 * grp, 0)
    return jnp.where(col >= first_key, s, NEG)


def _softmax(s):
    e = jnp.exp(s - jnp.max(s, axis=-1, keepdims=True))
    return e * (1.0 / jnp.sum(e, axis=-1, keepdims=True))


NG = TQ // 128


def attn_fwd(qkv, bias, name):
    S = qkv.shape[0]
    nb = S // TQ

    def body(q_ref, kp_ref, kc_ref, vp_ref, vc_ref, b_ref, o_ref, kw, vw):
        i = pl.program_id(1)
        kw[0:TQ, :] = kp_ref[...]
        kw[TQ:2 * TQ, :] = kc_ref[...]
        vw[0:TQ, :] = vp_ref[...]
        vw[TQ:2 * TQ, :] = vc_ref[...]
        lane = lax.broadcasted_iota(jnp.int32, (1, HP), 1)

        rows = [pl.ds(128 * a, 128) for a in range(NG)]
        keys = [pl.ds(128 * a, WIN) for a in range(NG)]
        q2 = [_two_heads(q_ref[r, :] * jnp.asarray(0.125, BF16), lane) for r in rows]
        s = [_attn_scores(q2[a], kw[keys[a], :], b_ref[...], i, a) for a in range(NG)]
        p = [_softmax(sa).astype(BF16) for sa in s]
        o2 = [_dot(p[a], vw[keys[a], :]) for a in range(NG)]
        for a in range(NG):
            o_ref[rows[a], :] = jnp.where(lane < 64, o2[a][0:128], o2[a][128:256]).astype(BF16)

    prev = lambda h, i: (jnp.maximum(i - 1, 0), 0)
    return pl.pallas_call(
        body, name=name, grid=(4, nb),
        in_specs=[pl.BlockSpec((TQ, HP), lambda h, i: (i, h)),
                  pl.BlockSpec((TQ, HP), lambda h, i: (jnp.maximum(i - 1, 0), 4 + h)),
                  pl.BlockSpec((TQ, HP), lambda h, i: (i, 4 + h)),
                  pl.BlockSpec((TQ, HP), lambda h, i: (jnp.maximum(i - 1, 0), 8 + h)),
                  pl.BlockSpec((TQ, HP), lambda h, i: (i, 8 + h)),
                  pl.BlockSpec((None, 256, WIN), lambda h, i: (h, 0, 0))],
        out_specs=pl.BlockSpec((TQ, HP), lambda h, i: (i, h)),
        out_shape=jax.ShapeDtypeStruct((S, 512), BF16),
        scratch_shapes=[pltpu.VMEM((2 * TQ, HP), BF16), pltpu.VMEM((2 * TQ, HP), BF16)],
        compiler_params=_cp("parallel", "arbitrary"),
    )(qkv, qkv, qkv, qkv, qkv, bias)


def attn_bwd(qkv, do, bias, name):
    S = qkv.shape[0]
    nb = S // TQ

    def body(q_ref, kp_ref, kc_ref, vp_ref, vc_ref, do_ref, b_ref, dqkv_ref, db_ref, dkv_ref, kw, vw, ak, av):
        i = pl.program_id(1)

        @pl.when(i == 0)
        def _():
            db_ref[...] = jnp.zeros_like(db_ref)
            ak[...] = jnp.zeros_like(ak)
            av[...] = jnp.zeros_like(av)

        @pl.when(i > 0)
        def _():
            ak[0:TQ, :] = ak[TQ:2 * TQ, :]
            av[0:TQ, :] = av[TQ:2 * TQ, :]
            ak[TQ:2 * TQ, :] = jnp.zeros((TQ, HP), F32)
            av[TQ:2 * TQ, :] = jnp.zeros((TQ, HP), F32)

        @pl.when(i < nb)
        def _():
            kw[0:TQ, :] = kp_ref[...]
            kw[TQ:2 * TQ, :] = kc_ref[...]
            vw[0:TQ, :] = vp_ref[...]
            vw[TQ:2 * TQ, :] = vc_ref[...]
            lane = lax.broadcasted_iota(jnp.int32, (1, HP), 1)

            rows = [pl.ds(128 * a, 128) for a in range(NG)]
            keys = [pl.ds(128 * a, WIN) for a in range(NG)]
            q2 = [_two_heads(q_ref[r, :] * jnp.asarray(0.125, BF16), lane) for r in rows]
            do2 = [_two_heads(do_ref[r, :], lane) for r in rows]
            s = [_attn_scores(q2[a], kw[keys[a], :], b_ref[...], i, a) for a in range(NG)]
            dp = [_dot_nt(do2[a], vw[keys[a], :]) for a in range(NG)]
            p = [_softmax(sa) for sa in s]
            ds = [p[a] * (dp[a] - jnp.sum(p[a] * dp[a], axis=-1, keepdims=True)) for a in range(NG)]
            db_ref[...] += (ds[0] + ds[1]) + (ds[2] + ds[3])
            dsb = [d.astype(BF16) for d in ds]
            dq2 = [_dot(dsb[a], kw[keys[a], :]) for a in range(NG)]
            dk = [_dot_tn(dsb[a], q2[a]) for a in range(NG)]
            dv = [_dot_tn(p[a].astype(BF16), do2[a]) for a in range(NG)]
            for a in range(NG):
                ak[keys[a], :] += dk[a]
                av[keys[a], :] += dv[a]
                dq = jnp.where(lane < 64, dq2[a][0:128], dq2[a][128:256])
                dqkv_ref[0, rows[a], :] = (dq * 0.125).astype(BF16)

        @pl.when(i > 0)
        def _():
            dkv_ref[0] = ak[0:TQ, :].astype(BF16)
            dkv_ref[1] = av[0:TQ, :].astype(BF16)

    cur = lambda i: jnp.minimum(i, nb - 1)
    prv = lambda i: jnp.clip(i - 1, 0, nb - 1)
    dq, db, dkv = pl.pallas_call(
        body, name=name, grid=(4, nb + 1),
        in_specs=[pl.BlockSpec((TQ, HP), lambda h, i: (cur(i), h)),
                  pl.BlockSpec((TQ, HP), lambda h, i: (prv(i), 4 + h)),
                  pl.BlockSpec((TQ, HP), lambda h, i: (cur(i), 4 + h)),
                  pl.BlockSpec((TQ, HP), lambda h, i: (prv(i), 8 + h)),
                  pl.BlockSpec((TQ, HP), lambda h, i: (cur(i), 8 + h)),
                  pl.BlockSpec((TQ, HP), lambda h, i: (cur(i), h)),
                  pl.BlockSpec((None, 256, WIN), lambda h, i: (h, 0, 0))],
        out_specs=[pl.BlockSpec((1, TQ, HP), lambda h, i: (0, cur(i), h)),
                   pl.BlockSpec((None, 256, WIN), lambda h, i: (h, 0, 0)),
                   pl.BlockSpec((2, TQ, HP), lambda h, i: (0, prv(i), h))],
        out_shape=[jax.ShapeDtypeStruct((1, S, 512), BF16), jax.ShapeDtypeStruct((4, 256, WIN), F32),
                   jax.ShapeDtypeStruct((2, S, 512), BF16)],
        scratch_shapes=[pltpu.VMEM((2 * TQ, HP), BF16), pltpu.VMEM((2 * TQ, HP), BF16),
                        pltpu.VMEM((2 * TQ, HP), F32), pltpu.VMEM((2 * TQ, HP), F32)],
        compiler_params=_cp("parallel", "arbitrary"),
    )(qkv, qkv, qkv, qkv, qkv, do, bias)
    return dq, db, dkv


def bias_grad(db, name):
    def body(db_ref, o_ref):
        r = lax.broadcasted_iota(jnp.int32, (128, 128), 0)
        c = lax.broadcasted_iota(jnp.int32, (128, 128), 1)
        flip = (r + c == 127).astype(BF16)
        lane = lax.broadcasted_iota(jnp.int32, (16, 384), 1)
        src = lax.broadcasted_iota(jnp.int32, (128, 384), 0)
        dst = lax.broadcasted_iota(jnp.int32, (128, 384), 1)

        def split_dot(v, m):
            hi = v.astype(BF16)
            r1 = v - hi.astype(F32)
            mid = r1.astype(BF16)
            lo = (r1 - mid.astype(F32)).astype(BF16)
            return _dot(hi, m) + _dot(mid, m) + _dot(lo, m)

        def diag_sums(w):
            y = pltpu.roll(split_dot(w, flip), 0, 1, stride=1, stride_axis=0)
            return jnp.broadcast_to(_colsum(y), (16, 128))

        w4 = db_ref[0, :, 512:640]
        w3 = db_ref[0, :, 384:512]
        far = jnp.sum(db_ref[0, :, 0:384]) + jnp.sum(jnp.where(r >= c, w3, 0.0))
        lo4 = diag_sums(jnp.where(r >= c, w4, 0.0))
        up4 = diag_sums(jnp.where(r < c, w4, 0.0))
        up3 = diag_sums(jnp.where(r < c, w3, 0.0))
        p_lo4 = (dst == 128 + (src + 1) % 128).astype(BF16)
        p_up4 = ((dst == src + 1) & (src < 127)).astype(BF16)
        p_up3 = ((dst == src + 129) & (src < 127)).astype(BF16)
        out = split_dot(lo4, p_lo4) + split_dot(up4, p_up4) + split_dot(up3, p_up3)
        o_ref[0] = out + jnp.where(lane == 256, far, 0.0)

    return pl.pallas_call(
        body, name=name, grid=(8,),
        in_specs=[pl.BlockSpec((1, 128, WIN), lambda h: (h, 0, 0))],
        out_specs=pl.BlockSpec((1, 16, 384), lambda h: (h, 0, 0)),
        out_shape=jax.ShapeDtypeStruct((8, 16, 384), F32),
        compiler_params=_cp("parallel"),
    )(db)[:, 0, :]


LT = 1024
LC = 512
SG = 4


def _lru_gates(xs, pv_ref, wa_ref, wx_ref, tl):
    xc = (pv_ref[4:5, :] + pv_ref[3:4, :] * xs[pl.ds(8, tl), :] + pv_ref[2:3, :] * xs[pl.ds(7, tl), :]
          + pv_ref[1:2, :] * xs[pl.ds(6, tl), :] + pv_ref[0:1, :] * xs[pl.ds(5, tl), :])
    xcb = xc.astype(BF16)
    pa = jnp.concatenate([_dot(xcb[:, 0:256], wa_ref[0]), _dot(xcb[:, 256:512], wa_ref[1])], axis=1)
    px = jnp.concatenate([_dot(xcb[:, 0:256], wx_ref[0]), _dot(xcb[:, 256:512], wx_ref[1])], axis=1)
    r = _sigmoid(pa + pv_ref[5:6, :])
    ig = _sigmoid(px + pv_ref[6:7, :])
    z = -pv_ref[7:8, :]
    sp = jnp.maximum(z, 0.0) + jnp.log1p(jnp.exp(-jnp.abs(z)))
    log_a = (-LRU_C * r) * sp
    a = jnp.exp(log_a)
    s = jnp.tanh(-log_a) * (1.0 + a * a)
    inv_mult = lax.rsqrt(s)
    mult = jnp.where(s > 0.0, s * inv_mult, 0.0)
    return xc, xcb, r, ig, sp, a, mult, inv_mult


def lru_fwd(rest, pvec, wa, wx, name):
    S = rest.shape[0]
    tl = min(LT, S)
    nt = S // tl

    def body(xr_ref, halo_ref, yr_ref, pv_ref, wa_ref, wx_ref, h_ref, hg_ref, xs, a_s, u_s, h_s, carry):
        ti = pl.program_id(1)

        @pl.when(ti == 0)
        def _():
            carry[...] = jnp.zeros_like(carry)

        xs[0:8, :] = jnp.where(ti > 0, halo_ref[8:16, :].astype(F32), 0.0)
        xs[pl.ds(8, tl), :] = xr_ref[...].astype(F32)
        xc, _, _, ig, _, a, mult, _ = _lru_gates(xs, pv_ref, wa_ref, wx_ref, tl)
        a_s[...] = a
        u_s[...] = mult * (ig * xc)
        row = lax.broadcasted_iota(jnp.int32, (8, LC), 0)

        def local(o):
            av = a_s[pl.ds(o, 8), :]
            bv = u_s[pl.ds(o, 8), :]
            for d in (1, 2, 4):
                a_sh = pltpu.roll(av, d, 0)
                b_sh = pltpu.roll(bv, d, 0)
                m = row >= d
                bv = jnp.where(m, av * b_sh + bv, bv)
                av = jnp.where(m, av * a_sh, av)
            return av, bv

        def blk(gi, c):
            os_ = [pl.multiple_of(gi * (8 * SG) + 8 * q, 8) for q in range(SG)]
            loc = [local(o) for o in os_]
            cs = []
            for av, bv in loc:
                cs.append(c)
                c = bv[7:8, :] + av[7:8, :] * c
            for o, (av, bv), ci in zip(os_, loc, cs):
                h_s[pl.ds(o, 8), :] = bv + av * ci
            return c

        carry[...] = lax.fori_loop(0, tl // (8 * SG), blk, carry[...], unroll=2)
        h = h_s[...]
        h_ref[...] = h
        hg_ref[...] = (h * _gelu(yr_ref[...].astype(F32))).astype(BF16)

    hb = tl // 16
    return pl.pallas_call(
        body, name=name, grid=(2, nt),
        in_specs=[pl.BlockSpec((tl, LC), lambda c, t: (t, c)),
                  pl.BlockSpec((16, LC), lambda c, t: (jnp.maximum(t * hb - 1, 0), c)),
                  pl.BlockSpec((tl, LC), lambda c, t: (t, 2 + c)),
                  pl.BlockSpec((8, LC), lambda c, t: (0, c)),
                  pl.BlockSpec((2, 256, 256), lambda c, t: (c, 0, 0)),
                  pl.BlockSpec((2, 256, 256), lambda c, t: (c, 0, 0))],
        out_specs=[pl.BlockSpec((tl, LC), lambda c, t: (t, c)), pl.BlockSpec((tl, LC), lambda c, t: (t, c))],
        out_shape=[jax.ShapeDtypeStruct((S, D), F32), jax.ShapeDtypeStruct((S, D), BF16)],
        scratch_shapes=[pltpu.VMEM((tl + 8, LC), F32), pltpu.VMEM((tl, LC), F32), pltpu.VMEM((tl, LC), F32),
                        pltpu.VMEM((tl, LC), F32), pltpu.VMEM((1, LC), F32)],
        compiler_params=_cp("parallel", "arbitrary"),
    )(rest, rest, rest, pvec, wa, wx)


def lru_bwd(dh, h, rest, pvec, wa, wx, name):
    S = rest.shape[0]
    tl = min(LT, S)
    nt = S // tl

    def body(dh_ref, h_ref, hhalo_ref, xr_ref, xhalo_ref, pv_ref, wa_ref, wx_ref,
             dxr_ref, vacc_ref, dwa_ref, dwx_ref,
             xs, hs, a_s, ash_s, b_s, lam_s, dxe, anext, lnext, dxnext):
        ti = pl.program_id(1)
        tr = nt - 1 - ti

        @pl.when(ti == 0)
        def _():
            anext[...] = jnp.zeros_like(anext)
            lnext[...] = jnp.zeros_like(lnext)
            dxnext[...] = jnp.zeros_like(dxnext)
            vacc_ref[...] = jnp.zeros_like(vacc_ref)
            dwa_ref[...] = jnp.zeros_like(dwa_ref)
            dwx_ref[...] = jnp.zeros_like(dwx_ref)

        xs[0:8, :] = jnp.where(tr > 0, xhalo_ref[8:16, :].astype(F32), 0.0)
        xs[pl.ds(8, tl), :] = xr_ref[...].astype(F32)
        xc, xcb, r, ig, sp, a, mult, inv_mult = _lru_gates(xs, pv_ref, wa_ref, wx_ref, tl)

        a_s[pl.ds(0, tl), :] = a
        a_s[pl.ds(tl, 8), :] = jnp.broadcast_to(anext[...], (8, LC))
        ash_s[...] = a_s[pl.ds(1, tl), :]
        b_s[...] = dh_ref[...]
        row = lax.broadcasted_iota(jnp.int32, (8, LC), 0)

        def local(o):
            av = ash_s[pl.ds(o, 8), :]
            bv = b_s[pl.ds(o, 8), :]
            for d in (1, 2, 4):
                a_sh = pltpu.roll(av, 8 - d, 0)
                b_sh = pltpu.roll(bv, 8 - d, 0)
                m = row < 8 - d
                bv = jnp.where(m, bv + av * b_sh, bv)
                av = jnp.where(m, av * a_sh, av)
            return av, bv

        def blk(k, c):
            os_ = [pl.multiple_of((tl // 8 - 1 - (k * SG + q)) * 8, 8) for q in range(SG)]
            loc = [local(o) for o in os_]
            cs = []
            for av, bv in loc:
                cs.append(c)
                c = bv[0:1, :] + av[0:1, :] * c
            for o, (av, bv), ci in zip(os_, loc, cs):
                lam_s[pl.ds(o, 8), :] = bv + av * ci
            return c

        lnext[...] = lax.fori_loop(0, tl // (8 * SG), blk, lnext[...], unroll=2)
        anext[...] = a[0:1, :]
        lam = lam_s[...]

        hs[0:8, :] = jnp.where(tr > 0, hhalo_ref[...], 0.0)
        hs[pl.ds(8, tl), :] = h_ref[...]
        d_a = lam * hs[pl.ds(7, tl), :]
        d_mult = lam * (ig * xc)
        d_ig = lam * mult * xc
        dxc = lam * mult * ig
        d_log_a = d_a * a - d_mult * (a * a) * inv_mult
        d_r = d_log_a * (-LRU_C * sp)
        vacc_ref[7:8, :] += _colsum(d_log_a * (-LRU_C * r)) * (-_sigmoid(-pv_ref[7:8, :]))
        d_pa = d_r * r * (1.0 - r)
        d_px = d_ig * ig * (1.0 - ig)
        vacc_ref[5:6, :] += _colsum(d_pa)
        vacc_ref[6:7, :] += _colsum(d_px)
        dpa = d_pa.astype(BF16)
        dpx = d_px.astype(BF16)
        back = []
        for g in range(2):
            sl = slice(256 * g, 256 * g + 256)
            dwa_ref[g] += _dot_tn(xcb[:, sl], dpa[:, sl])
            dwx_ref[g] += _dot_tn(xcb[:, sl], dpx[:, sl])
            back.append(_dot_nt(dpa[:, sl], wa_ref[g]) + _dot_nt(dpx[:, sl], wx_ref[g]))
        dxc = dxc + jnp.concatenate(back, axis=1)
        vacc_ref[4:5, :] += _colsum(dxc)
        for k in range(4):
            vacc_ref[k:k + 1, :] += _colsum(dxc * xs[pl.ds(5 + k, tl), :])
        dxe[pl.ds(0, tl), :] = dxc
        dxe[pl.ds(tl, 8), :] = dxnext[...]
        dxr = (pv_ref[3:4, :] * dxc + pv_ref[2:3, :] * dxe[pl.ds(1, tl), :]
               + pv_ref[1:2, :] * dxe[pl.ds(2, tl), :] + pv_ref[0:1, :] * dxe[pl.ds(3, tl), :])
        dxr_ref[...] = dxr.astype(BF16)
        dxnext[...] = dxc[0:8, :]

    hb = tl // 8
    rev = lambda t: nt - 1 - t
    halo = lambda t: jnp.maximum(rev(t) * hb - 1, 0)
    big = lambda: pltpu.VMEM((tl + 8, LC), F32)
    til = lambda: pltpu.VMEM((tl, LC), F32)
    return pl.pallas_call(
        body, name=name, grid=(2, nt),
        in_specs=[pl.BlockSpec((tl, LC), lambda c, t: (rev(t), c)),
                  pl.BlockSpec((tl, LC), lambda c, t: (rev(t), c)),
                  pl.BlockSpec((8, LC), lambda c, t: (halo(t), c)),
                  pl.BlockSpec((tl, LC), lambda c, t: (rev(t), c)),
                  pl.BlockSpec((16, LC), lambda c, t: (jnp.maximum(rev(t) * (tl // 16) - 1, 0), c)),
                  pl.BlockSpec((8, LC), lambda c, t: (0, c)),
                  pl.BlockSpec((2, 256, 256), lambda c, t: (c, 0, 0)),
                  pl.BlockSpec((2, 256, 256), lambda c, t: (c, 0, 0))],
        out_specs=[pl.BlockSpec((tl, LC), lambda c, t: (rev(t), c)),
                   pl.BlockSpec((8, LC), lambda c, t: (0, c)),
                   pl.BlockSpec((2, 256, 256), lambda c, t: (c, 0, 0)),
                   pl.BlockSpec((2, 256, 256), lambda c, t: (c, 0, 0))],
        out_shape=[jax.ShapeDtypeStruct((S, D), BF16), jax.ShapeDtypeStruct((8, D), F32),
                   jax.ShapeDtypeStruct((4, 256, 256), F32), jax.ShapeDtypeStruct((4, 256, 256), F32)],
        scratch_shapes=[big(), big(), big(), til(), til(), til(), big(),
                        pltpu.VMEM((1, LC), F32), pltpu.VMEM((1, LC), F32), pltpu.VMEM((8, LC), F32)],
        compiler_params=_cp("parallel", "arbitrary"),
    )(dh, h, h, rest, rest, pvec, wa, wx)


def mix_out_fwd(x, ao, hg, rest, vec, w_att_o, w_rec_o, w_out, name, tm=512):
    S = x.shape[0]
    tm = min(tm, S)

    def body(x_ref, ao_ref, hg_ref, ga_ref, gr_ref, vec_ref, wa_ref, wr_ref, wo_ref,
             xo_ref, att_ref, rec_ref, mg_ref, f_ref):
        att = _dot(ao_ref[...], wa_ref[...])
        rec = _dot(hg_ref[...], wr_ref[...])
        att_ref[...] = att.astype(BF16)
        rec_ref[...] = rec.astype(BF16)
        mg = (_sigmoid(ga_ref[...].astype(F32)) * att + _sigmoid(gr_ref[...].astype(F32)) * rec).astype(BF16)
        mg_ref[...] = mg
        f = _dot(mg, wo_ref[...])
        f_ref[...] = f.astype(BF16)
        y = f * lax.rsqrt(_mean(f * f) + EPS) * vec_ref[1:2, :]
        xo_ref[...] = x_ref[...] + (1.0 * vec_ref[4:5, :]) * y

    row = lambda i: (i, 0)
    full = lambda r: pl.BlockSpec((r, D), lambda i: (0, 0))
    return pl.pallas_call(
        body, name=name, grid=(S // tm,),
        in_specs=[pl.BlockSpec((tm, D), row), pl.BlockSpec((tm, 512), row), pl.BlockSpec((tm, D), row),
                  pl.BlockSpec((tm, D), lambda i: (i, 2)), pl.BlockSpec((tm, D), lambda i: (i, 3)),
                  full(8), full(512), full(D), full(D)],
        out_specs=[pl.BlockSpec((tm, D), row)] * 5,
        out_shape=[jax.ShapeDtypeStruct((S, D), F32)] + [jax.ShapeDtypeStruct((S, D), BF16)] * 4,
        compiler_params=_cp("parallel"),
    )(x, ao, hg, rest, rest, vec, w_att_o, w_rec_o, w_out)


def mix_out_bwd(dxo, f, att, rec, rest, h, vec, w_att_o, w_rec_o, w_out, name, tm=512):
    S = dxo.shape[0]
    tm = min(tm, S)

    def body(dxo_ref, f_ref, att_ref, rec_ref, yr_ref, ga_ref, gr_ref, h_ref, vec_ref, wa_ref, wr_ref, wo_ref,
             df_ref, da_ref, dr_ref, dao_ref, dh_ref, d3_ref, vacc_ref):
        @pl.when(pl.program_id(0) == 0)
        def _():
            vacc_ref[...] = jnp.zeros_like(vacc_ref)

        df = _post_norm_bwd(dxo_ref[...], f_ref[...].astype(F32), 1.0, vec_ref, vacc_ref).astype(BF16)
        df_ref[...] = df
        dm = _dot_nt(df, wo_ref[...])
        sa = _sigmoid(ga_ref[...].astype(F32))
        sr = _sigmoid(gr_ref[...].astype(F32))
        d_att = (dm * sa).astype(BF16)
        d_rec = (dm * sr).astype(BF16)
        da_ref[...] = d_att
        dr_ref[...] = d_rec
        d3_ref[1] = (dm * att_ref[...].astype(F32) * (sa * (1.0 - sa))).astype(BF16)
        d3_ref[2] = (dm * rec_ref[...].astype(F32) * (sr * (1.0 - sr))).astype(BF16)
        dao_ref[...] = _dot_nt(d_att, wa_ref[...]).astype(BF16)
        d_hg = _dot_nt(d_rec, wr_ref[...])
        yr = yr_ref[...].astype(F32)
        t = jnp.tanh(_GK * (yr + 0.044715 * yr * yr * yr))
        dh_ref[...] = d_hg * (0.5 * yr * (1.0 + t))
        gelu_grad = 0.5 * (1.0 + t) + 0.5 * yr * (1.0 - t * t) * _GK * (1.0 + 3.0 * 0.044715 * yr * yr)
        d3_ref[0] = (d_hg * h_ref[...] * gelu_grad).astype(BF16)

    row = lambda i: (i, 0)
    full = lambda r: pl.BlockSpec((r, D), lambda i: (0, 0))
    return pl.pallas_call(
        body, name=name, grid=(S // tm,),
        in_specs=[pl.BlockSpec((tm, D), row)] * 4
        + [pl.BlockSpec((tm, D), lambda i: (i, 1)), pl.BlockSpec((tm, D), lambda i: (i, 2)),
           pl.BlockSpec((tm, D), lambda i: (i, 3)), pl.BlockSpec((tm, D), row),
           full(8), full(512), full(D), full(D)],
        out_specs=[pl.BlockSpec((tm, D), row)] * 3
        + [pl.BlockSpec((tm, 512), row), pl.BlockSpec((tm, D), row),
           pl.BlockSpec((3, tm, D), lambda i: (0, i, 0)), pl.BlockSpec((8, D), lambda i: (0, 0))],
        out_shape=[jax.ShapeDtypeStruct((S, D), BF16)] * 3
        + [jax.ShapeDtypeStruct((S, 512), BF16), jax.ShapeDtypeStruct((S, D), F32),
           jax.ShapeDtypeStruct((3, S, D), BF16), jax.ShapeDtypeStruct((8, D), F32)],
        compiler_params=_cp("arbitrary"),
    )(dxo, f, att, rec, rest, rest, rest, h, vec, w_att_o, w_rec_o, w_out)


def dw_in(h, dq, dkv, dxr, d3, name, tk=1024, tn=512):
    S = h.shape[0]
    tk = min(tk, S)
    nk = S // tk

    def body(h_ref, dq_ref, dkv_ref, dxr_ref, d3_ref, o_ref, acc):
        j, k = pl.program_id(0), pl.program_id(1)

        @pl.when(k == 0)
        def _():
            acc[...] = jnp.zeros_like(acc)

        @pl.when(j == 0)
        def _():
            acc[...] += _dot_tn(h_ref[pl.ds(pl.multiple_of(k * tk, 16), tk), :], dq_ref[...])

        @pl.when((j >= 1) & (j < 3))
        def _():
            acc[...] += _dot_tn(h_ref[pl.ds(pl.multiple_of(k * tk, 16), tk), :], dkv_ref[...])

        @pl.when((j >= 3) & (j < 5))
        def _():
            acc[...] += _dot_tn(h_ref[pl.ds(pl.multiple_of(k * tk, 16), tk), :], dxr_ref[...])

        @pl.when(j >= 5)
        def _():
            acc[...] += _dot_tn(h_ref[pl.ds(pl.multiple_of(k * tk, 16), tk), :], d3_ref[...])

        @pl.when(k == nk - 1)
        def _():
            o_ref[...] = acc[...].astype(BF16)

    use = lambda j, k, lo, hi: jnp.where((j >= lo) & (j < hi), k, 0)
    g3 = lambda j: jnp.clip(j - 5, 0, 5)
    return pl.pallas_call(
        body, name=name, grid=(PW // tn, nk),
        in_specs=[pl.BlockSpec((S, D), lambda j, k: (0, 0), pipeline_mode=pl.Buffered(1)),
                  pl.BlockSpec((None, tk, tn), lambda j, k: (0, use(j, k, 0, 1), 0)),
                  pl.BlockSpec((None, tk, tn), lambda j, k: (jnp.clip(j - 1, 0, 1), use(j, k, 1, 3), 0)),
                  pl.BlockSpec((tk, tn), lambda j, k: (use(j, k, 3, 5), jnp.clip(j - 3, 0, 1))),
                  pl.BlockSpec((None, tk, tn), lambda j, k: (g3(j) // 2, use(j, k, 5, 11), g3(j) % 2))],
        out_specs=pl.BlockSpec((D, tn), lambda j, k: (0, j)),
        out_shape=jax.ShapeDtypeStruct((D, PW), BF16),
        scratch_shapes=[pltpu.VMEM((D, tn), F32)],
        compiler_params=_cp("parallel", "arbitrary"),
    )(h, dq, dkv, dxr, d3)


def ada_fwd(c_all, w_ada, b_ada, name, tn=768):
    n = w_ada.shape[1]

    def body(c_ref, w_ref, b_ref, o_ref):
        cv = c_ref[...]
        ca = (cv * _sigmoid(cv)).astype(BF16)
        o_ref[...] = _dot(ca, w_ref[...].astype(BF16)) + b_ref[...]

    return pl.pallas_call(
        body, name=name, grid=(n // tn,),
        in_specs=[pl.BlockSpec((8, D), lambda j: (0, 0)), pl.BlockSpec((D, tn), lambda j: (0, j)),
                  pl.BlockSpec((1, tn), lambda j: (0, j))],
        out_specs=pl.BlockSpec((8, tn), lambda j: (0, j)),
        out_shape=jax.ShapeDtypeStruct((8, n), F32),
        compiler_params=_cp("parallel"),
    )(c_all, w_ada, b_ada)


def ada_bwd(c_all_t, dmod, name, tn=768):
    n = dmod.shape[1]

    def body(c_ref, d_ref, o_ref):
        cv = c_ref[...]
        ca = (cv * _sigmoid(cv)).astype(BF16)
        o_ref[...] = _dot(ca, d_ref[...].astype(BF16))

    return pl.pallas_call(
        body, name=name, grid=(n // tn,),
        in_specs=[pl.BlockSpec((D, 128), lambda j: (0, 0)), pl.BlockSpec((128, tn), lambda j: (0, j))],
        out_specs=pl.BlockSpec((D, tn), lambda j: (0, j)),
        out_shape=jax.ShapeDtypeStruct((D, n), F32),
        compiler_params=_cp("parallel"),
    )(c_all_t, dmod)


def _row_tile(rows, cols, itemsize=4, budget=1536 * 1024):
    best = None
    for t in range(8, rows + 1, 8):
        if rows % t == 0 and t * cols * itemsize <= budget:
            best = t
    return rows if best is None else best


def sum_lead(parts, name, out_dtype=F32):
    n, R, C = parts.shape
    tr = _row_tile(R, C * n)

    def body(p_ref, o_ref):
        acc = p_ref[0].astype(F32)
        for k in range(1, n):
            acc = acc + p_ref[k].astype(F32)
        o_ref[...] = acc.astype(out_dtype)

    return pl.pallas_call(
        body, name=name, grid=(R // tr,),
        in_specs=[pl.BlockSpec((n, tr, C), lambda i: (0, i, 0))],
        out_specs=pl.BlockSpec((tr, C), lambda i: (i, 0)),
        out_shape=jax.ShapeDtypeStruct((R, C), out_dtype),
        compiler_params=_cp("parallel"),
    )(parts)


def adamw(w, g, m, v, name, emit_g=False):
    R, C = w.shape
    tr = _row_tile(R, C * 8, budget=16 * 1024 * 1024)

    def body(w_ref, g_ref, m_ref, v_ref, d_ref, mo_ref, vo_ref, *go_ref):
        gv = g_ref[...]
        if emit_g:
            go_ref[0][...] = gv
        mn = ADAM_B1 * m_ref[...] + (1.0 - ADAM_B1) * gv
        vn = ADAM_B2 * v_ref[...] + (1.0 - ADAM_B2) * (gv * gv)
        m_hat = mn / (1.0 - ADAM_B1 ** ADAM_STEP)
        v_hat = vn / (1.0 - ADAM_B2 ** ADAM_STEP)
        d_ref[...] = -ADAM_LR * (m_hat / (jnp.sqrt(v_hat) + ADAM_EPS) + ADAM_WD * w_ref[...])
        mo_ref[...] = mn
        vo_ref[...] = vn

    spec = pl.BlockSpec((tr, C), lambda i: (i, 0))
    return pl.pallas_call(
        body, name=name, grid=(R // tr,),
        in_specs=[spec] * 4, out_specs=[spec] * (4 if emit_g else 3),
        out_shape=[jax.ShapeDtypeStruct((R, C), F32)] * (4 if emit_g else 3),
        compiler_params=_cp("parallel"),
    )(w, g, m, v)


def _mesh_pos():
    return lax.axis_index("x"), lax.axis_index("y"), lax.axis_index("c")


def _other_chips(mx, my):
    return [(1 - mx, my), (mx, 1 - my), (1 - mx, 1 - my)]


def ag_small(x, name):
    R = x.shape[0]

    def body(x_ref, out_ref, send_sems, recv_sems, local_sem):
        mx, my, mc = _mesh_pos()
        me, sibling = (mx, my, mc), (mx, my, 1 - mc)
        chips = _other_chips(mx, my)

        def slot(px, py, pc):
            return out_ref.at[4 * px + 2 * py + pc]

        def copy(k, block, to, src=None):
            return pltpu.make_async_remote_copy(
                src_ref=slot(*block) if src is None else src, dst_ref=slot(*block),
                send_sem=send_sems.at[k], recv_sem=recv_sems.at[k], device_id=to, device_id_type=MESH)

        mine = pltpu.make_async_copy(x_ref, slot(*me), local_sem)
        mine.start()
        first = [copy(0, me, sibling, src=x_ref)]
        first += [copy(1 + j, me, (*chip, mc), src=x_ref) for j, chip in enumerate(chips)]
        for cp in first:
            cp.start()
        passed = [copy(4 + j, (*chip, mc), sibling) for j, chip in enumerate(chips)]
        for j, chip in enumerate(chips):
            copy(1 + j, (*chip, mc), me).wait_recv()
            passed[j].start()
        copy(0, sibling, me).wait_recv()
        for j, chip in enumerate(chips):
            copy(4 + j, (*chip, 1 - mc), me).wait_recv()
        for cp in first + passed:
            cp.wait_send()
        mine.wait()

    return pl.pallas_call(
        body, name=name,
        out_shape=jax.ShapeDtypeStruct((N_DEV, R, 128), F32),
        in_specs=[pl.BlockSpec(memory_space=pltpu.VMEM)],
        out_specs=pl.BlockSpec(memory_space=pltpu.VMEM),
        scratch_shapes=[pltpu.SemaphoreType.DMA((7,)), pltpu.SemaphoreType.DMA((7,)), pltpu.SemaphoreType.DMA],
        compiler_params=pltpu.CompilerParams(vmem_limit_bytes=VMEM_LIMIT),
    )(x)


BIG = (("ffn1_w_gu", "col", D, PW), ("ffn1_w_down", "row", FF, D), ("w_in", "col", D, PW),
       ("w_att_o", "col", 512, D), ("w_rec_o", "row", D, D), ("w_out", "row", D, D),
       ("ffn2_w_gu", "col", D, PW), ("ffn2_w_down", "row", FF, D))
NBIG = len(BIG)


def _shard_shape(kind, R, C):
    return (R, C // 4) if kind == "col" else (R // 4, C)


def _region(ref, kind, R, C, q, half, t, tr):
    sr, sc = _shard_shape(kind, R, C)
    if kind == "col":
        return ref.at[pl.ds(pl.multiple_of(half * (R // 2) + t * tr, 16), tr), pl.ds(q * sc, sc)]
    return ref.at[pl.ds(pl.multiple_of(q * sr + t * tr, 16), tr), pl.ds(half * (C // 2), C // 2)]


def ag_local(w, kind, R, C, p_arr, name, after=()):
    sr, sc = _shard_shape(kind, R, C)
    tr = _row_tile(sr, sc, budget=2 * 1024 * 1024)
    nt = sr // tr
    after = list(after)

    def body(p_ref, w_ref, *rest):
        rest[-1][...] = w_ref[...].astype(BF16)

    if kind == "col":
        o_spec = pl.BlockSpec((tr, sc), lambda i, p: (i, p[0]))
    else:
        o_spec = pl.BlockSpec((tr, sc), lambda i, p: (p[0] * nt + i, 0))
    return pl.pallas_call(
        body, name=name,
        grid_spec=pltpu.PrefetchScalarGridSpec(
            num_scalar_prefetch=1, grid=(nt,),
            in_specs=[pl.BlockSpec((tr, sc), lambda i, p: (i, 0))] + [ANY] * len(after), out_specs=o_spec),
        out_shape=jax.ShapeDtypeStruct((R, C), BF16),
        compiler_params=_cp("parallel"),
    )(p_arr, w, *after)


HBM_SPEC = pl.BlockSpec(memory_space=pltpu.HBM)
SEM_SPEC = pl.BlockSpec(memory_space=pltpu.SEMAPHORE)


def _ag_sems(geoms):
    return sum(6 if both else 3 for (_, _, _, both) in geoms)


def _ag_copies(fulls, geoms, ssem, rsem, mx, my, mc, q, h):
    chips = _other_chips(mx, my)
    out, base = [], 0
    for w, (kind, R, C, both) in enumerate(geoms):
        sr, sc = _shard_shape(kind, R, C)
        hr = sr // 2 if kind == "col" else sr
        reg = _region(fulls[w], kind, R, C, q, h, 0, hr)
        out.append([pltpu.make_async_remote_copy(
            src_ref=reg, dst_ref=reg, send_sem=ssem.at[base + 3 * t + k], recv_sem=rsem.at[base + 3 * t + k],
            device_id=(*chips[k], mc if t == 0 else 1 - mc), device_id_type=MESH)
            for t in range(2 if both else 1) for k in range(3)])
        base += 6 if both else 3
    return out


def ag_start(fulls, geoms, after, name):
    n = len(fulls)
    after = list(after)
    m = len(after)

    def body(*refs):
        ssem, rsem = refs[n + m:n + m + 2]
        outs, token = refs[n + m + 2:2 * n + m + 2], refs[2 * n + m + 2]
        mx, my, mc = _mesh_pos()
        p = 2 * mx + my
        col = [w for w, g in enumerate(geoms) if g[0] == "col"]
        row = [w for w, g in enumerate(geoms) if g[0] == "row"]
        for q in range(4):
            @pl.when(p == q)
            def _(q=q):
                cps = _ag_copies(outs, geoms, ssem, rsem, mx, my, mc, q, mc)
                for w in col:
                    for cp in cps[w]:
                        cp.start()
        for h in range(2):
            @pl.when(mc == h)
            def _(h=h):
                cps = _ag_copies(outs, geoms, ssem, rsem, mx, my, mc, p, h)
                for w in row:
                    for cp in cps[w]:
                        cp.start()
        token[...] = jnp.zeros_like(token)

    res = pl.pallas_call(
        body, name=name,
        out_shape=[pltpu.SemaphoreType.DMA((_ag_sems(geoms),)), pltpu.SemaphoreType.DMA((_ag_sems(geoms),))]
        + [pltpu.HBM(a.shape, a.dtype) for a in fulls] + [jax.ShapeDtypeStruct((8, 128), F32)],
        in_specs=[HBM_SPEC] * n + [ANY] * m,
        out_specs=[SEM_SPEC, SEM_SPEC] + [HBM_SPEC] * n + [pl.BlockSpec(memory_space=pltpu.VMEM)],
        input_output_aliases={w: 2 + w for w in range(n)},
        compiler_params=pltpu.CompilerParams(has_side_effects=pltpu.SideEffectType.DATAFLOW_SIDE_EFFECTING),
    )(*[pltpu.with_memory_space_constraint(a, pltpu.HBM) for a in fulls], *after)
    return res[0], res[1], list(res[2:2 + n]), res[2 + n]


def ag_wait(fulls, geoms, ssem, rsem, after, name):
    n = len(fulls)
    after = list(after) if isinstance(after, (list, tuple)) else [after]

    def body(*refs):
        ins, ssem_ref, rsem_ref = refs[:n], refs[n], refs[n + 1]
        mx, my, mc = _mesh_pos()
        for cps in _ag_copies(ins, geoms, ssem_ref, rsem_ref, mx, my, mc, 0, 0):
            for cp in cps:
                cp.wait_send()
                cp.wait_recv()

    return list(pl.pallas_call(
        body, name=name,
        out_shape=[pltpu.HBM(a.shape, a.dtype) for a in fulls],
        in_specs=[HBM_SPEC] * n + [SEM_SPEC, SEM_SPEC] + [ANY] * len(after),
        out_specs=[HBM_SPEC] * n,
        input_output_aliases={w: w for w in range(n)},
        compiler_params=pltpu.CompilerParams(has_side_effects=pltpu.SideEffectType.DATAFLOW_SIDE_EFFECTING),
    )(*fulls, ssem, rsem, *after))


def ag_forward(full, kind, R, C, name):
    sr, sc = _shard_shape(kind, R, C)
    hr, hc = (sr // 2, sc) if kind == "col" else (sr, sc // 2)
    tr = _row_tile(hr, hc, itemsize=2, budget=2 * 1024 * 1024)
    nt = hr // tr
    total = 3 * nt

    def body(src_ref, full_ref, stage, lsem, ssem, rsem):
        step = pl.program_id(0) * nt + pl.program_id(1)
        par = step % 2
        mx, my, mc = _mesh_pos()

        def load(s, q, h, t):
            return pltpu.make_async_copy(_region(src_ref, kind, R, C, q, h, t, tr), stage.at[s], lsem.at[s])

        def push(s, q, h, t):
            return pltpu.make_async_remote_copy(src_ref=stage.at[s], dst_ref=_region(full_ref, kind, R, C, q, h, t, tr),
                                                send_sem=ssem.at[s], recv_sem=rsem, device_id=(mx, my, 1 - mc),
                                                device_id_type=MESH)

        def for_tile(stp, fn):
            q_k = _partner_chip(stp // nt, 2 * mx + my)
            if kind == "col":
                for q in range(4):
                    @pl.when(q_k == q)
                    def _(q=q):
                        fn(q, mc, stp % nt)
            else:
                for h in range(2):
                    @pl.when(mc == h)
                    def _(h=h):
                        fn(q_k, h, stp % nt)

        @pl.when(step == 0)
        def _():
            for_tile(step, lambda q, h, t: load(0, q, h, t).start())

        load(par, 0, 0, 0).wait()
        for_tile(step, lambda q, h, t: push(par, q, h, t).start())

        @pl.when(step + 1 < total)
        def _():
            @pl.when(step >= 1)
            def _():
                push(1 - par, 0, 0, 0).wait_send()
            for_tile(step + 1, lambda q, h, t: load(1 - par, q, h, t).start())

        @pl.when(step == total - 1)
        def _():
            push(par, 0, 0, 0).wait_send()
            push(1 - par, 0, 0, 0).wait_send()
            three = full_ref.at[pl.ds(0, hr), pl.ds(0, 3 * hc)] if kind == "col" else full_ref.at[pl.ds(0, 3 * hr), pl.ds(0, hc)]
            pltpu.make_async_remote_copy(src_ref=three, dst_ref=three, send_sem=ssem.at[0], recv_sem=rsem,
                                         device_id=(mx, my, 1 - mc), device_id_type=MESH).wait_recv()

    return pl.pallas_call(
        body, name=name, grid=(3, nt),
        in_specs=[ANY], out_specs=ANY,
        out_shape=jax.ShapeDtypeStruct((R, C), BF16),
        scratch_shapes=[pltpu.VMEM((2, tr, hc), BF16), pltpu.SemaphoreType.DMA((2,)), pltpu.SemaphoreType.DMA((2,)),
                        pltpu.SemaphoreType.DMA],
        input_output_aliases={0: 0},
        compiler_params=_cp("arbitrary", "arbitrary"),
    )(full)


def _half_shape(kind, R, C):
    return (R // 2, C) if kind == "col" else (R, C // 2)


def _piece_shape(kind, R, C):
    return (R // 2, C // 4) if kind == "col" else (R // 4, C // 2)


def pair_push(g, kind, c_arr, name):
    R, C = g.shape
    hr, hc = _half_shape(kind, R, C)
    tr = _row_tile(hr, hc, itemsize=2, budget=2 * 1024 * 1024)
    nt = hr // tr

    def body(c_ref, g_ref, out_ref, stage, ssem, rsem):
        i = pl.program_id(0)
        slot = i % 2
        mx, my, mc = _mesh_pos()

        def push(s, t):
            return pltpu.make_async_remote_copy(
                src_ref=stage.at[s], dst_ref=out_ref.at[pl.ds(pl.multiple_of(t * tr, 16), tr)],
                send_sem=ssem.at[s], recv_sem=rsem, device_id=(mx, my, 1 - mc), device_id_type=MESH)

        @pl.when(i >= 2)
        def _():
            push(slot, 0).wait_send()

        stage[slot] = g_ref[...]
        push(slot, i).start()

        @pl.when(i == nt - 1)
        def _():
            push(slot, 0).wait_send()
            if nt >= 2:
                push(1 - slot, 0).wait_send()
            pltpu.make_async_remote_copy(src_ref=out_ref, dst_ref=out_ref, send_sem=ssem.at[0], recv_sem=rsem,
                                         device_id=(mx, my, 1 - mc), device_id_type=MESH).wait_recv()

    if kind == "col":
        g_spec = pl.BlockSpec((tr, hc), lambda i, c: ((1 - c[0]) * nt + i, 0))
    else:
        g_spec = pl.BlockSpec((tr, hc), lambda i, c: (i, 1 - c[0]))
    return pl.pallas_call(
        body, name=name,
        grid_spec=pltpu.PrefetchScalarGridSpec(
            num_scalar_prefetch=1, grid=(nt,), in_specs=[g_spec], out_specs=ANY,
            scratch_shapes=[pltpu.VMEM((2, tr, hc), BF16), pltpu.SemaphoreType.DMA((2,)), pltpu.SemaphoreType.DMA]),
        out_shape=jax.ShapeDtypeStruct((hr, hc), BF16),
        compiler_params=_cp("arbitrary"),
    )(c_arr, g)


def _partner_chip(k, p):
    return p ^ jnp.where(k == 0, 2, jnp.where(k == 1, 1, jnp.where(k == 2, 3, 0)))


def pair_add(g, got, kind, cp_arr, name):
    R, C = g.shape
    pr, pc = _piece_shape(kind, R, C)
    tr = _row_tile(pr, pc, itemsize=2, budget=2 * 1024 * 1024)
    nt = pr // tr

    def body(cp_ref, g_ref, got_ref, ps_ref, rb_ref):
        tile = (g_ref[...].astype(F32) + got_ref[...].astype(F32)).astype(BF16)
        ps_ref[...] = tile

        @pl.when(pl.program_id(1) == cp_ref[1])
        def _():
            rb_ref[...] = tile

    if kind == "col":
        g_spec = pl.BlockSpec((tr, pc), lambda i, q, cp: (cp[0] * nt + i, q))
        got_spec = pl.BlockSpec((tr, pc), lambda i, q, cp: (i, q))
    else:
        g_spec = pl.BlockSpec((tr, pc), lambda i, q, cp: (q * nt + i, cp[0]))
        got_spec = pl.BlockSpec((tr, pc), lambda i, q, cp: (q * nt + i, 0))
    return pl.pallas_call(
        body, name=name,
        grid_spec=pltpu.PrefetchScalarGridSpec(
            num_scalar_prefetch=1, grid=(nt, 4), in_specs=[g_spec, got_spec],
            out_specs=[pl.BlockSpec((None, tr, pc), lambda i, q, cp: (q, i, 0)),
                       pl.BlockSpec((None, tr, pc), lambda i, q, cp: (cp[1], i, 0))]),
        out_shape=[jax.ShapeDtypeStruct((4, pr, pc), BF16)] * 2,
        compiler_params=_cp("arbitrary", "arbitrary"),
    )(cp_arr, g, got)


def _rs_copies(ps, rb, ssem, rsem, mx, my, mc):
    p = 2 * mx + my
    out = []
    for w in range(len(ps)):
        for k, chip in enumerate(_other_chips(mx, my)):
            out.append(pltpu.make_async_remote_copy(
                src_ref=ps[w].at[2 * chip[0] + chip[1]], dst_ref=rb[w].at[p], send_sem=ssem.at[3 * w + k],
                recv_sem=rsem.at[3 * w + k], device_id=(*chip, mc), device_id_type=MESH))
    return out


def rs_start(ps, rb, after, name):
    n = len(ps)
    after = list(after)
    m = len(after)

    def body(*refs):
        ssem, rsem = refs[2 * n + m:2 * n + m + 2]
        ps_o = refs[2 * n + m + 2:3 * n + m + 2]
        rb_o = refs[3 * n + m + 2:4 * n + m + 2]
        token = refs[4 * n + m + 2]
        for cp in _rs_copies(ps_o, rb_o, ssem, rsem, *_mesh_pos()):
            cp.start()
        token[...] = jnp.zeros_like(token)

    both = list(ps) + list(rb)
    res = pl.pallas_call(
        body, name=name,
        out_shape=[pltpu.SemaphoreType.DMA((3 * n,)), pltpu.SemaphoreType.DMA((3 * n,))]
        + [pltpu.HBM(a.shape, a.dtype) for a in both] + [jax.ShapeDtypeStruct((8, 128), F32)],
        in_specs=[HBM_SPEC] * (2 * n) + [ANY] * m,
        out_specs=[SEM_SPEC, SEM_SPEC] + [HBM_SPEC] * (2 * n) + [pl.BlockSpec(memory_space=pltpu.VMEM)],
        input_output_aliases={w: 2 + w for w in range(2 * n)},
        compiler_params=pltpu.CompilerParams(has_side_effects=pltpu.SideEffectType.DATAFLOW_SIDE_EFFECTING),
    )(*[pltpu.with_memory_space_constraint(a, pltpu.HBM) for a in both], *after)
    return res[0], res[1], list(res[2:2 + n]), list(res[2 + n:2 + 2 * n]), res[2 + 2 * n]


def rs_wait(ps, rb, ssem, rsem, after, name):
    n = len(ps)
    after = list(after)
    m = len(after)

    def body(*refs):
        ps_i, rb_i = refs[:n], refs[n:2 * n]
        ssem_ref, rsem_ref = refs[2 * n], refs[2 * n + 1]
        for cp in _rs_copies(ps_i, rb_i, ssem_ref, rsem_ref, *_mesh_pos()):
            cp.wait_send()
            cp.wait_recv()

    both = list(ps) + list(rb)
    res = pl.pallas_call(
        body, name=name,
        out_shape=[pltpu.HBM(a.shape, a.dtype) for a in both],
        in_specs=[HBM_SPEC] * (2 * n) + [SEM_SPEC, SEM_SPEC] + [ANY] * m,
        out_specs=[HBM_SPEC] * (2 * n),
        input_output_aliases={w: w for w in range(2 * n)},
        compiler_params=pltpu.CompilerParams(has_side_effects=pltpu.SideEffectType.DATAFLOW_SIDE_EFFECTING),
    )(*both, ssem, rsem, *after)
    return list(res[n:])


def _slot_copies(blk, ssem, rsem):
    mx, my, mc = _mesh_pos()
    mine = blk.at[4 * mx + 2 * my + mc]
    peers = [(mx, my, 1 - mc)] + [(*chip, mc) for chip in _other_chips(mx, my)] \
        + [(*chip, 1 - mc) for chip in _other_chips(mx, my)]
    return [pltpu.make_async_remote_copy(src_ref=mine, dst_ref=mine, send_sem=ssem.at[k], recv_sem=rsem.at[k],
                                         device_id=peer, device_id_type=MESH) for k, peer in enumerate(peers)]


def slot_start(blk, name):
    def body(_, ssem, rsem, out, token):
        for cp in _slot_copies(out, ssem, rsem):
            cp.start()
        token[...] = jnp.zeros_like(token)

    return pl.pallas_call(
        body, name=name,
        out_shape=[pltpu.SemaphoreType.DMA((7,)), pltpu.SemaphoreType.DMA((7,)), pltpu.HBM(blk.shape, blk.dtype),
                   jax.ShapeDtypeStruct((8, 128), F32)],
        in_specs=[HBM_SPEC],
        out_specs=[SEM_SPEC, SEM_SPEC, HBM_SPEC, pl.BlockSpec(memory_space=pltpu.VMEM)],
        input_output_aliases={0: 2},
        compiler_params=pltpu.CompilerParams(has_side_effects=pltpu.SideEffectType.DATAFLOW_SIDE_EFFECTING),
    )(pltpu.with_memory_space_constraint(blk, pltpu.HBM))


def slot_wait(blk, ssem, rsem, after, name):
    after = list(after)

    def body(blk_ref, ssem_ref, rsem_ref, *_):
        for cp in _slot_copies(blk_ref, ssem_ref, rsem_ref):
            cp.wait_send()
            cp.wait_recv()

    return pl.pallas_call(
        body, name=name,
        out_shape=pltpu.HBM(blk.shape, blk.dtype),
        in_specs=[HBM_SPEC, SEM_SPEC, SEM_SPEC] + [ANY] * len(after),
        out_specs=HBM_SPEC,
        input_output_aliases={0: 0},
        compiler_params=pltpu.CompilerParams(has_side_effects=pltpu.SideEffectType.DATAFLOW_SIDE_EFFECTING),
    )(blk, ssem, rsem, *after)


def sum_share(parts, kind, R, C, name):
    _, pr, pc = parts.shape
    sr, sc = _shard_shape(kind, R, C)
    tr = _row_tile(pr, pc * 4, budget=8 * 1024 * 1024)
    nt = pr // tr

    def body(p_ref, fin_ref, stage, lsem, ssem, rsem):
        i = pl.program_id(0)
        slot = i % 2
        mx, my, mc = _mesh_pos()

        def region(h, t):
            r0 = pl.multiple_of(t * tr, 8)
            if kind == "col":
                return fin_ref.at[pl.ds(pl.multiple_of(h * pr + r0, 8), tr)]
            return fin_ref.at[pl.ds(r0, tr), pl.ds(h * pc, pc)]

        def copies(s, h, t):
            return (pltpu.make_async_copy(stage.at[s], region(h, t), lsem.at[s]),
                    pltpu.make_async_remote_copy(src_ref=stage.at[s], dst_ref=region(h, t), send_sem=ssem.at[s],
                                                 recv_sem=rsem, device_id=(mx, my, 1 - mc), device_id_type=MESH))

        def wait_sent(s):
            loc, rem = copies(s, 0, 0)
            loc.wait()
            rem.wait_send()

        @pl.when(i >= 2)
        def _():
            wait_sent(slot)

        acc = p_ref[0].astype(F32)
        for k in range(1, 4):
            acc = acc + p_ref[k].astype(F32)
        stage[slot] = acc
        if kind == "col":
            for cp in copies(slot, mc, i):
                cp.start()
        else:
            for h in range(2):
                @pl.when(mc == h)
                def _(h=h):
                    for cp in copies(slot, h, i):
                        cp.start()

        @pl.when(i == nt - 1)
        def _():
            wait_sent(slot)
            if nt >= 2:
                wait_sent(1 - slot)
            half = fin_ref.at[pl.ds(0, pr), pl.ds(0, pc)]
            pltpu.make_async_remote_copy(src_ref=half, dst_ref=half, send_sem=ssem.at[0], recv_sem=rsem,
                                         device_id=(mx, my, 1 - mc), device_id_type=MESH).wait_recv()

    return pl.pallas_call(
        body, name=name, grid=(nt,),
        in_specs=[pl.BlockSpec((4, tr, pc), lambda i: (0, i, 0))],
        out_specs=ANY,
        out_shape=jax.ShapeDtypeStruct((sr, sc), F32),
        scratch_shapes=[pltpu.VMEM((2, tr, pc), F32), pltpu.SemaphoreType.DMA((2,)), pltpu.SemaphoreType.DMA((2,)),
                        pltpu.SemaphoreType.DMA],
        compiler_params=_cp("arbitrary"),
    )(parts)


def _pack(parts, rows):
    flat = []
    for a in parts:
        a = jnp.ravel(a).astype(F32)
        flat.append(jnp.pad(a, (0, (-a.shape[0]) % 128)))
    v = jnp.concatenate(flat)
    return jnp.pad(v, (0, rows * 128 - v.shape[0])).reshape(rows, 128)


def _unpack(block, shapes):
    lead = block.shape[:-2]
    v = block.reshape(lead + (-1,))
    out, off = [], 0
    for shp in shapes:
        n = int(np.prod(shp))
        out.append(v[..., off:off + n].reshape(lead + tuple(shp)))
        off += n + (-n) % 128
    return out


def _block_diag4(w):
    w4 = w.reshape(4, 4, 64, 64)
    eye = jnp.eye(4, dtype=w.dtype)
    return (w4[:, :, :, None, :] * eye[None, :, None, :, None]).reshape(4, 256, 256)


def _diag_blocks(bd):
    b5 = bd.reshape(4, 4, 64, 4, 64)
    return jnp.stack([b5[:, i, :, i, :] for i in range(4)], axis=1).reshape(16, 64, 64)


def _bias_window(rel_bias):
    m = (np.arange(768) + 127) % 768 - 127
    w = rel_bias[:, np.clip(512 - m, -128, 128) + 128]
    win = jnp.tile(w, (1, 128))[:, :128 * 767].reshape(8, 128, 767)[:, :, :WIN]
    qh = np.arange(128)[:, None] // CHUNK
    kc = np.arange(WIN)[None, :] // CHUNK
    valid = (kc >= qh) & (kc <= qh + 8)
    return jnp.where(jnp.asarray(valid)[None], win, NEG)


SMALL = ("b_ada", "norm_pre", "norm_post", "rel_bias", "conv_w", "conv_b", "lru_wa", "lru_ba", "lru_wx",
         "lru_bx", "lru_lambda")
WEIGHTS = ("w_ada", "b_ada", "norm_pre", "norm_post", "ffn1_w_gu", "ffn1_w_down", "w_in", "rel_bias", "conv_w",
           "conv_b", "lru_wa", "lru_ba", "lru_wx", "lru_bx", "lru_lambda", "w_att_o", "w_rec_o", "w_out",
           "ffn2_w_gu", "ffn2_w_down")


def kernel(x, c, w_ada, b_ada, norm_pre, norm_post, ffn1_w_gu, ffn1_w_down, w_in, rel_bias, conv_w, conv_b, lru_wa, lru_ba, lru_wx, lru_bx, lru_lambda, w_att_o, w_rec_o, w_out, ffn2_w_gu, ffn2_w_down, loss_target, m_w_ada, m_b_ada, m_norm_pre, m_norm_post, m_ffn1_w_gu, m_ffn1_w_down, m_w_in, m_rel_bias, m_conv_w, m_conv_b, m_lru_wa, m_lru_ba, m_lru_wx, m_lru_bx, m_lru_lambda, m_w_att_o, m_w_rec_o, m_w_out, m_ffn2_w_gu, m_ffn2_w_down, v_w_ada, v_b_ada, v_norm_pre, v_norm_post, v_ffn1_w_gu, v_ffn1_w_down, v_w_in, v_rel_bias, v_conv_w, v_conv_b, v_lru_wa, v_lru_ba, v_lru_wx, v_lru_bx, v_lru_lambda, v_w_att_o, v_w_rec_o, v_w_out, v_ffn2_w_gu, v_ffn2_w_down):
    W = dict(w_ada=w_ada, b_ada=b_ada, norm_pre=norm_pre, norm_post=norm_post, ffn1_w_gu=ffn1_w_gu,
             ffn1_w_down=ffn1_w_down, w_in=w_in, rel_bias=rel_bias, conv_w=conv_w, conv_b=conv_b, lru_wa=lru_wa,
             lru_ba=lru_ba, lru_wx=lru_wx, lru_bx=lru_bx, lru_lambda=lru_lambda, w_att_o=w_att_o, w_rec_o=w_rec_o,
             w_out=w_out, ffn2_w_gu=ffn2_w_gu, ffn2_w_down=ffn2_w_down)
    M = dict(w_ada=m_w_ada, b_ada=m_b_ada, norm_pre=m_norm_pre, norm_post=m_norm_post, ffn1_w_gu=m_ffn1_w_gu,
             ffn1_w_down=m_ffn1_w_down, w_in=m_w_in, rel_bias=m_rel_bias, conv_w=m_conv_w, conv_b=m_conv_b,
             lru_wa=m_lru_wa, lru_ba=m_lru_ba, lru_wx=m_lru_wx, lru_bx=m_lru_bx, lru_lambda=m_lru_lambda,
             w_att_o=m_w_att_o, w_rec_o=m_w_rec_o, w_out=m_w_out, ffn2_w_gu=m_ffn2_w_gu, ffn2_w_down=m_ffn2_w_down)
    V = dict(w_ada=v_w_ada, b_ada=v_b_ada, norm_pre=v_norm_pre, norm_post=v_norm_post, ffn1_w_gu=v_ffn1_w_gu,
             ffn1_w_down=v_ffn1_w_down, w_in=v_w_in, rel_bias=v_rel_bias, conv_w=v_conv_w, conv_b=v_conv_b,
             lru_wa=v_lru_wa, lru_ba=v_lru_ba, lru_wx=v_lru_wx, lru_bx=v_lru_bx, lru_lambda=v_lru_lambda,
             w_att_o=v_w_att_o, w_rec_o=v_w_rec_o, w_out=v_w_out, ffn2_w_gu=v_ffn2_w_gu, ffn2_w_down=v_ffn2_w_down)
    mx, my, mc = _mesh_pos()
    p = 2 * mx + my
    e = 4 * mx + 2 * my + mc
    xs = x[0]

    c_arr = jnp.reshape(mc, (1,)).astype(jnp.int32)
    cp_arr = jnp.stack([mc, p]).astype(jnp.int32)
    p_arr = jnp.reshape(p, (1,)).astype(jnp.int32)
    direct = ("w_att_o", "w_rec_o", "w_out", "ffn2_w_gu", "ffn2_w_down")
    geoms = [(kind, R, C, n in direct) for (n, kind, R, C) in BIG]
    names = [b[0] for b in BIG]
    placed = [ag_local(W[n][0], kind, R, C, p_arr, "ag_local_" + n) for (n, kind, R, C) in BIG[:2]]

    def arrived(fly, lo, hi, ssem, rsem, after, tag):
        done = ag_wait(fly, geoms[lo:hi], ssem, rsem, after, "ag_wait_" + tag)
        return [a if both else ag_forward(a, kind, R, C, "ag_forward_" + n)
                for a, (kind, R, C, both), n in zip(done, geoms[lo:hi], names[lo:hi])]

    g1 = ag_small(_pack([c, norm_pre, norm_post, conv_w], 32), "ag_small_params")
    c_all, npre4, npost4, cw4 = _unpack(g1, [(D,), (3, 256), (3, 256), (4, 256)])
    chipwise = lambda a: jnp.moveaxis(a[0::2], 0, 1).reshape(a.shape[1], D)
    npre, npost, conv_full = chipwise(npre4), chipwise(npost4), chipwise(cw4)

    b_cols = lax.dynamic_slice(b_ada, (0, p * 2304), (1, 2304))
    mod_cols = ada_fwd(c_all, w_ada[0], b_cols, "ada_fwd")
    g2 = ag_small(mod_cols.reshape(144, 128), "ag_mod")
    mod_all = jnp.moveaxis(g2[0::2].reshape(4, 8, 2304), 0, 1).reshape(8, 9 * D)
    mod = lax.dynamic_index_in_dim(mod_all, e, 0, keepdims=False).reshape(3, 3, D)
    zeros3 = jnp.zeros((3, D), F32)
    vecs = [jnp.concatenate([npre[k:k + 1], npost[k:k + 1], mod[k], zeros3], axis=0) for k in range(3)]

    gu_s, gu_r, gu_fly, tok_gu = ag_start(placed[:1], geoms[:1], [g2], "ag_start_ffn1_gu")
    dn_s, dn_r, dn_fly, tok0 = ag_start(placed[1:2], geoms[1:2], [tok_gu], "ag_start_ffn1_down")
    placed += [ag_local(W[n][0], kind, R, C, p_arr, "ag_local_" + n, after=[tok0]) for (n, kind, R, C) in BIG[2:]]
    f1_gu, = arrived(gu_fly, 0, 1, gu_s, gu_r, placed[2:], "ffn1_gu")
    f1_dn, = arrived(dn_fly, 1, 2, dn_s, dn_r, f1_gu, "ffn1_down")
    mix_s, mix_r, mix_fly, tok1 = ag_start(placed[2:6], geoms[2:6], [f1_gu, f1_dn], "ag_start_mixer")
    ffn_s, ffn_r, ffn_fly, tok2 = ag_start(placed[6:], geoms[6:], [tok1], "ag_start_ffn2")
    wa_bd = _block_diag4(lru_wa[0]).astype(BF16)
    wx_bd = _block_diag4(lru_wx[0]).astype(BF16)
    pvec = jnp.concatenate([conv_full, conv_b, lru_ba, lru_bx, lru_lambda], axis=0)
    bias = _bias_window(rel_bias[0]).reshape(4, 256, WIN)

    x1, h1, g1_, u1, a1, f1 = ffn_fwd(xs, vecs[0] + tok2[0:1, 0:1], f1_gu, f1_dn, 0.5, "ffn1_fwd")
    win, wao, wro, wout = arrived(mix_fly, 2, 6, mix_s, mix_r, x1, "mixer")
    h2, qkv, rest = proj_fwd(x1, vecs[1], win, "proj_fwd")
    ao = attn_fwd(qkv, bias, "attn_fwd")
    hl, hg = lru_fwd(rest, pvec, wa_bd, wx_bd, "lru_fwd")
    x2, att, rec, mg, f2 = mix_out_fwd(x1, ao, hg, rest, vecs[1], wao, wro, wout, "mix_out_fwd")
    f2_gu, f2_dn = arrived(ffn_fly, 6, 8, ffn_s, ffn_r, x2, "ffn2")
    dy, h3, g3_, u3, a3, f3, lvec = ffn_fwd(x2, vecs[2], f2_gu, f2_dn, 0.5, "ffn2_fwd", tgt=loss_target[0])

    G, grads = {}, {}
    geo = {n: (kind, R, C) for (n, kind, R, C) in BIG}

    def reduce_begin(names, tag):
        ps, rb = [], []
        for n in names:
            got = pair_push(G[n], geo[n][0], c_arr, "rs_push_" + n)
            a, b = pair_add(G[n], got, geo[n][0], cp_arr, "rs_pair_sum_" + n)
            ps.append(a)
            rb.append(b)
        return rs_start(ps, rb, [], "rs_start_" + tag)

    def reduce_end(names, flight, after, tag):
        ssem, rsem, ps, rb, _ = flight
        for a, n in zip(rs_wait(ps, rb, ssem, rsem, after, "rs_wait_" + tag), names):
            grads[n] = sum_share(a, *geo[n], "rs_sum_share_" + n)[None]

    dx2, df3, dgu3, va2 = ffn_bwd(dy, x2, f3, g3_, u3, vecs[2], f2_gu, f2_dn, 0.5, "ffn2_bwd")
    G["ffn2_w_gu"] = mm_tn(h3, dgu3, "dw_ffn2_gu", D, 1408, 2048, a_resident=True)
    G["ffn2_w_down"] = mm_tn(a3, df3, "dw_ffn2_down", 1408, D, 2048)
    fly_ffn2 = reduce_begin(("ffn2_w_gu", "ffn2_w_down"), "ffn2")
    vec1 = vecs[1] + fly_ffn2[4][0:1, 0:1]
    df2, d_att, d_rec, dao, dhl, d3, va_out = mix_out_bwd(dx2, f2, att, rec, rest, hl, vec1, wao, wro, wout,
                                                          "mix_out_bwd")
    G["w_out"] = mm_tn(mg, df2, "dw_out", D, D, 1024)
    G["w_att_o"] = mm_tn(ao, d_att, "dw_att_o", 512, D, 1024)
    G["w_rec_o"] = mm_tn(hg, d_rec, "dw_rec_o", D, D, 1024)
    dq, db, dkv = attn_bwd(qkv, dao, bias, "attn_bwd")
    dxr, v_lru, dwa_bd, dwx_bd = lru_bwd(dhl, hl, rest, pvec, wa_bd, wx_bd, "lru_bwd")
    lru_w = jnp.concatenate([_diag_blocks(dwa_bd), _diag_blocks(dwx_bd)]).reshape(1, 1024, 128)
    lru_slots = lax.dynamic_update_slice(jnp.zeros((N_DEV, 1024, 128), F32), lru_w, (e, 0, 0))
    lw_s, lw_r, lw_fly, lw_tok = slot_start(lru_slots, "lru_w_start")
    dx1, va_in = proj_bwd(dq, dkv, dxr, d3, win, x1, dx2, vecs[1] + lw_tok[0:1, 0:1], "proj_bwd")
    G["w_in"] = dw_in(h2, dq, dkv, dxr, d3, "dw_in")
    fly_mix = reduce_begin(("w_in", "w_att_o", "w_rec_o", "w_out"), "mixer")
    vec0 = vecs[0] + fly_mix[4][0:1, 0:1]
    dx0, df1, dgu1, va0 = ffn_bwd(dx1, xs, f1, g1_, u1, vec0, f1_gu, f1_dn, 0.5, "ffn1_bwd")
    G["ffn1_w_gu"] = mm_tn(h1, dgu1, "dw_ffn1_gu", D, 1408, 2048, a_resident=True)
    G["ffn1_w_down"] = mm_tn(a1, df1, "dw_ffn1_down", 1408, D, 2048)
    fly_ffn1 = reduce_begin(("ffn1_w_gu", "ffn1_w_down"), "ffn1")
    reduce_end(("ffn2_w_gu", "ffn2_w_down"), fly_ffn2, [fly_ffn1[4]], "ffn2")
    reduce_end(("w_in", "w_att_o", "w_rec_o", "w_out"), fly_mix, [fly_ffn1[4], grads["ffn2_w_down"]], "mixer")

    va1 = va_out + va_in
    vas = (va0, va1, va2)
    dmod = jnp.stack([v[2:5] for v in vas])
    part = {"b_ada": dmod, "norm_pre": jnp.stack([v[0] for v in vas]), "norm_post": jnp.stack([v[1] for v in vas]),
            "rel_bias": bias_grad(db.reshape(8, 128, WIN), "bias_grad")[:, :257], "conv_w": v_lru[0:4], "conv_b": v_lru[4],
            "lru_ba": v_lru[5], "lru_bx": v_lru[6], "lru_lambda": v_lru[7]}
    full_shapes = {"b_ada": (9 * D,), "norm_pre": (3, D), "norm_post": (3, D), "rel_bias": (8, 257),
                   "conv_w": (4, D), "conv_b": (D,), "lru_wa": (16, 64, 64), "lru_ba": (D,),
                   "lru_wx": (16, 64, 64), "lru_bx": (D,), "lru_lambda": (D,)}
    gathered = [n for n in SMALL if n in part]
    g3 = ag_small(_pack([part[n] for n in gathered] + [lvec[0:1, 0:1]], 208), "ag_small_grads")
    summed = _unpack(sum_lead(g3, "sum_small_grads"), [full_shapes[n] for n in gathered] + [(1,)])
    red = dict(zip(gathered, summed[:-1]))
    loss = summed[-1][0]
    lru_all = slot_wait(lw_fly, lw_s, lw_r, [dx0], "lru_w_wait")
    red["lru_wa"], red["lru_wx"] = sum_lead(lru_all, "sum_lru_w").reshape(2, 16, 64, 64)
    cols = lambda a: lax.dynamic_slice(a, (0, p * 256), (a.shape[0], 256))
    grads.update({"b_ada": red["b_ada"][None], "norm_pre": cols(red["norm_pre"])[None],
                  "norm_post": cols(red["norm_post"])[None], "rel_bias": red["rel_bias"][None],
                  "conv_w": cols(red["conv_w"])[None], "conv_b": red["conv_b"][None], "lru_wa": red["lru_wa"][None],
                  "lru_ba": red["lru_ba"][None], "lru_wx": red["lru_wx"][None], "lru_bx": red["lru_bx"][None],
                  "lru_lambda": red["lru_lambda"][None]})

    dmod_all = g3[:, :72].reshape(8, 9 * D)
    dmod_cols = jnp.pad(lax.dynamic_slice(dmod_all, (0, p * 2304), (8, 2304)), ((0, 120), (0, 0)))
    c_all_t = jnp.pad(c_all.T, ((0, 0), (0, 120)))
    grads["w_ada"] = ada_bwd(c_all_t, dmod_cols, "ada_bwd")[None]

    delta, new_m, new_v = {}, {}, {}

    def update(n):
        shp = W[n].shape
        res = adamw(W[n][0], grads[n][0], M[n][0], V[n][0], "adamw_" + n, emit_g=n in geo)
        delta[n], new_m[n], new_v[n] = [a.reshape(shp) for a in res[:3]]
        if n in geo:
            grads[n] = res[3].reshape(shp)

    for n in ("w_ada", "ffn2_w_gu", "ffn2_w_down", "w_in", "w_att_o", "w_rec_o", "w_out"):
        update(n)
    packed = [_pack([src[n] for n in SMALL], 1168) for src in (W, grads, M, V)]
    outs = adamw(*packed, "adamw_small")
    for dst, blk in zip((delta, new_m, new_v), outs):
        for n, a in zip(SMALL, _unpack(blk, [W[n].shape for n in SMALL])):
            dst[n] = a
    reduce_end(("ffn1_w_gu", "ffn1_w_down"), fly_ffn1,
               [outs[0], delta["w_ada"], delta["ffn2_w_gu"], delta["ffn2_w_down"], delta["w_in"], delta["w_out"]], "ffn1")
    for n in ("ffn1_w_gu", "ffn1_w_down"):
        update(n)

    return (loss, dx0[None], *[grads[n] for n in WEIGHTS], *[delta[n] for n in WEIGHTS],
            *[new_m[n] for n in WEIGHTS], *[new_v[n] for n in WEIGHTS])
```
